```python
import math
import jax, jax.numpy as jnp
from jax import lax
import numpy as np

D_MODEL = 1024
BATCH = 8
SEQ = 8192
DEPTH = 1

MIX_WIDTH = D_MODEL
POOL_WIDTH = MIX_WIDTH // 2
POOL_WINDOWS = (2, 4, 8, 16)
N_POOL_GROUPS = len(POOL_WINDOWS)
POOL_GROUP_DIM = POOL_WIDTH // N_POOL_GROUPS
ATTN_WIDTH = MIX_WIDTH - POOL_WIDTH
HEAD_DIM = 64
N_HEADS = ATTN_WIDTH // HEAD_DIM
DILATED_PATTERNS = ((128, 1), (512, 4), (2048, 16))
BLOCK = 128
N_BUCKETS = 32
MAX_DISTANCE = 2048
D_FF = 4 * D_MODEL
NORM_EPS = 1e-6
NEG_INF = -1e30

kernel_name = "hybrid_pool_dilated_attn_layer"


def rms_norm(x, g):
    xf = x.astype(jnp.float32)
    y = xf * lax.rsqrt(jnp.mean(xf * xf, axis=-1, keepdims=True) + NORM_EPS)
    return (y * g.astype(jnp.float32)).astype(x.dtype)


def t5_bucket(dist):
    max_exact = N_BUCKETS // 2
    d_f = jnp.maximum(dist, 1).astype(jnp.float32)
    large = max_exact + (jnp.log(d_f / max_exact) / math.log(MAX_DISTANCE / max_exact)
                         * (N_BUCKETS - max_exact)).astype(jnp.int32)
    large = jnp.minimum(large, N_BUCKETS - 1)
    return jnp.where(dist < max_exact, dist, large)


def causal_mean_minus_self(u, window):
    uf = u.astype(jnp.float32)
    s = uf.shape[1]
    c = jnp.pad(jnp.cumsum(uf, axis=1), ((0, 0), (window, 0), (0, 0)))
    window_sum = c[:, window:] - c[:, :s]
    count = jnp.minimum(jnp.arange(1, s + 1), window).astype(jnp.float32)
    return window_sum / count[None, :, None] - uf


def multiscale_pool_mixer(u, pool_w, pool_scale):
    b, s, _ = u.shape
    ug = u.reshape(b, s, N_POOL_GROUPS, POOL_GROUP_DIM)
    pooled = jnp.stack([causal_mean_minus_self(ug[:, :, g], w)
                        for g, w in enumerate(POOL_WINDOWS)], axis=2)
    mixed = jnp.einsum('bsgc,gcd->bsgd', pooled.astype(u.dtype), pool_w)
    mixed = mixed * pool_scale.reshape(N_POOL_GROUPS, POOL_GROUP_DIM)
    return mixed.reshape(b, s, POOL_WIDTH).astype(u.dtype)


def dilated_window_attention(q, k, v, rel_bias, window, dilation):
    b, h, s, d = q.shape
    w = window // dilation
    assert w <= BLOCK
    L = s // dilation
    nb = -(-L // BLOCK)
    Lp = nb * BLOCK

    def to_blocks(t):
        t = t.reshape(b, h, L, dilation, d).transpose(0, 1, 3, 2, 4)
        t = jnp.pad(t, ((0, 0), (0, 0), (0, 0), (0, Lp - L), (0, 0)))
        return t.reshape(b, h, dilation, nb, BLOCK, d)

    def with_prev(t):
        prev = jnp.pad(t[:, :, :, :-1], ((0, 0), (0, 0), (0, 0), (1, 0), (0, 0), (0, 0)))
        return jnp.concatenate([prev, t], axis=-2)

    qb = to_blocks(q)
    kb = with_prev(to_blocks(k))
    vb = with_prev(to_blocks(v))

    qq = jnp.arange(BLOCK)[:, None]
    kk = jnp.arange(2 * BLOCK)[None, :]
    dist = qq + BLOCK - kk
    rel_ok = (dist >= 0) & (dist <= w)
    mask = rel_ok[None] & ((jnp.arange(nb)[:, None, None] > 0) | (kk >= BLOCK)[None])
    bucket = t5_bucket(jnp.clip(dist, 0, w) * dilation)
    bias = rel_bias[bucket].astype(jnp.float32).transpose(2, 0, 1)

    scores = jnp.einsum('bhrnqd,bhrnkd->bhrnqk', qb, kb,
                        preferred_element_type=jnp.float32)
    scores = jnp.where(mask[None, None, None], scores + bias[None, :, None, None], NEG_INF)
    m = jnp.max(scores, axis=-1, keepdims=True)
    p = jnp.exp(scores - m)
    l = jnp.sum(p, axis=-1, keepdims=True)
    acc = jnp.einsum('bhrnqk,bhrnkd->bhrnqd', p.astype(v.dtype), vb,
                     preferred_element_type=jnp.float32)

    def from_blocks(t):
        e = t.shape[-1]
        t = t.reshape(b, h, dilation, Lp, e)[:, :, :, :L]
        return t.transpose(0, 1, 3, 2, 4).reshape(b, h, s, e)

    return from_blocks(acc), from_blocks(m), from_blocks(l)


def dilated_attention_mixer(q, k, v, q_norm_g, k_norm_g, rel_bias):
    b, s, _ = q.shape
    split = lambda t: t.reshape(b, s, N_HEADS, HEAD_DIM).transpose(0, 2, 1, 3)
    qh = rms_norm(split(q), q_norm_g) * (HEAD_DIM ** -0.5)
    kh = rms_norm(split(k), k_norm_g)
    vh = split(v)
    outs = [dilated_window_attention(qh, kh, vh, rel_bias, w, dl)
            for (w, dl) in DILATED_PATTERNS]
    m_all = jnp.max(jnp.stack([o[1] for o in outs]), axis=0)
    num = sum(jnp.exp(mi - m_all) * acc for acc, mi, _ in outs)
    den = sum(jnp.exp(mi - m_all) * li for _, mi, li in outs)
    o = (num / den).astype(q.dtype)
    return o.transpose(0, 2, 1, 3).reshape(b, s, ATTN_WIDTH)


def _fwd_setup_inputs(seed: int = 0) -> dict:
    key = jax.random.key(seed)
    ks = jax.random.split(key, 12)
    f32 = jnp.float32
    nrm = lambda k, shape, scale: jax.random.normal(k, shape, f32) * scale
    return {
        "x": nrm(ks[0], (BATCH, SEQ, D_MODEL), 1.0),
        "mix_norm_g": 1.0 + nrm(ks[1], (D_MODEL,), 0.02),
        "w_in": nrm(ks[2], (D_MODEL, POOL_WIDTH + 3 * ATTN_WIDTH), D_MODEL ** -0.5),
        "pool_w": nrm(ks[3], (N_POOL_GROUPS, POOL_GROUP_DIM, POOL_GROUP_DIM), POOL_GROUP_DIM ** -0.5),
        "pool_scale": 1.0 + nrm(ks[4], (POOL_WIDTH,), 0.02),
        "q_norm_g": 1.0 + nrm(ks[5], (HEAD_DIM,), 0.02),
        "k_norm_g": 1.0 + nrm(ks[6], (HEAD_DIM,), 0.02),
        "rel_bias": nrm(ks[7], (N_BUCKETS, N_HEADS), 0.1),
        "w_out": nrm(ks[8], (MIX_WIDTH, D_MODEL), MIX_WIDTH ** -0.5),
        "mlp_norm_g": 1.0 + nrm(ks[9], (D_MODEL,), 0.02),
        "w_up": nrm(ks[10], (D_MODEL, D_FF), D_MODEL ** -0.5),
        "w_down": nrm(ks[11], (D_FF, D_MODEL), D_FF ** -0.5),
    }


def _fwd_reference(x, mix_norm_g, w_in, pool_w, pool_scale, q_norm_g, k_norm_g, rel_bias,
              w_out, mlp_norm_g, w_up, w_down):
    h = x
    for _ in range(DEPTH):
        a = rms_norm(h, mix_norm_g)
        proj = jnp.einsum('bsd,de->bse', a, w_in)
        u_pool = proj[..., :POOL_WIDTH]
        q = proj[..., POOL_WIDTH:POOL_WIDTH + ATTN_WIDTH]
        k = proj[..., POOL_WIDTH + ATTN_WIDTH:POOL_WIDTH + 2 * ATTN_WIDTH]
        v = proj[..., POOL_WIDTH + 2 * ATTN_WIDTH:]
        y_pool = multiscale_pool_mixer(u_pool, pool_w, pool_scale)
        y_attn = dilated_attention_mixer(q, k, v, q_norm_g, k_norm_g, rel_bias)
        mixed = jnp.concatenate([y_pool, y_attn], axis=-1)
        h = h + jnp.einsum('bse,ed->bsd', mixed, w_out)
        c = rms_norm(h, mlp_norm_g)
        ff = jnp.square(jax.nn.relu(jnp.einsum('bsd,df->bsf', c, w_up)))
        h = h + jnp.einsum('bsf,fd->bsd', ff, w_down)
    return h


import jax as _jax
import jax.numpy as _jnp

TWIN_FORMAT = 'train_step'
FWD_PARAMS = ['x', 'mix_norm_g', 'w_in', 'pool_w', 'pool_scale', 'q_norm_g', 'k_norm_g', 'rel_bias', 'w_out', 'mlp_norm_g', 'w_up', 'w_down']
TWIN_WEIGHTS = ['mix_norm_g', 'w_in', 'pool_w', 'pool_scale', 'q_norm_g', 'k_norm_g', 'rel_bias', 'w_out', 'mlp_norm_g', 'w_up', 'w_down']
TWIN_DIFF_INPUT = 'x'
TWIN_INPUTS = ['x', 'mix_norm_g', 'w_in', 'pool_w', 'pool_scale', 'q_norm_g', 'k_norm_g', 'rel_bias', 'w_out', 'mlp_norm_g', 'w_up', 'w_down', 'loss_target', 'm_mix_norm_g', 'm_w_in', 'm_pool_w', 'm_pool_scale', 'm_q_norm_g', 'm_k_norm_g', 'm_rel_bias', 'm_w_out', 'm_mlp_norm_g', 'm_w_up', 'm_w_down', 'v_mix_norm_g', 'v_w_in', 'v_pool_w', 'v_pool_scale', 'v_q_norm_g', 'v_k_norm_g', 'v_rel_bias', 'v_w_out', 'v_mlp_norm_g', 'v_w_up', 'v_w_down']
TWIN_OUTPUTS = ['loss', 'grad_x', 'grad_mix_norm_g', 'grad_w_in', 'grad_pool_w', 'grad_pool_scale', 'grad_q_norm_g', 'grad_k_norm_g', 'grad_rel_bias', 'grad_w_out', 'grad_mlp_norm_g', 'grad_w_up', 'grad_w_down', 'delta_mix_norm_g', 'delta_w_in', 'delta_pool_w', 'delta_pool_scale', 'delta_q_norm_g', 'delta_k_norm_g', 'delta_rel_bias', 'delta_w_out', 'delta_mlp_norm_g', 'delta_w_up', 'delta_w_down', 'new_m_mix_norm_g', 'new_m_w_in', 'new_m_pool_w', 'new_m_pool_scale', 'new_m_q_norm_g', 'new_m_k_norm_g', 'new_m_rel_bias', 'new_m_w_out', 'new_m_mlp_norm_g', 'new_m_w_up', 'new_m_w_down', 'new_v_mix_norm_g', 'new_v_w_in', 'new_v_pool_w', 'new_v_pool_scale', 'new_v_q_norm_g', 'new_v_k_norm_g', 'new_v_rel_bias', 'new_v_w_out', 'new_v_mlp_norm_g', 'new_v_w_up', 'new_v_w_down']
TWIN_LEAF_KINDS = {'loss': 'loss', 'grad_x': 'grad_x', 'grad_mix_norm_g': 'grad_w', 'grad_w_in': 'grad_w', 'grad_pool_w': 'grad_w', 'grad_pool_scale': 'grad_w', 'grad_q_norm_g': 'grad_w', 'grad_k_norm_g': 'grad_w', 'grad_rel_bias': 'grad_w', 'grad_w_out': 'grad_w', 'grad_mlp_norm_g': 'grad_w', 'grad_w_up': 'grad_w', 'grad_w_down': 'grad_w', 'delta_mix_norm_g': 'delta_w', 'delta_w_in': 'delta_w', 'delta_pool_w': 'delta_w', 'delta_pool_scale': 'delta_w', 'delta_q_norm_g': 'delta_w', 'delta_k_norm_g': 'delta_w', 'delta_rel_bias': 'delta_w', 'delta_w_out': 'delta_w', 'delta_mlp_norm_g': 'delta_w', 'delta_w_up': 'delta_w', 'delta_w_down': 'delta_w', 'new_m_mix_norm_g': 'new_m', 'new_m_w_in': 'new_m', 'new_m_pool_w': 'new_m', 'new_m_pool_scale': 'new_m', 'new_m_q_norm_g': 'new_m', 'new_m_k_norm_g': 'new_m', 'new_m_rel_bias': 'new_m', 'new_m_w_out': 'new_m', 'new_m_mlp_norm_g': 'new_m', 'new_m_w_up': 'new_m', 'new_m_w_down': 'new_m', 'new_v_mix_norm_g': 'new_v', 'new_v_w_in': 'new_v', 'new_v_pool_w': 'new_v', 'new_v_pool_scale': 'new_v', 'new_v_q_norm_g': 'new_v', 'new_v_k_norm_g': 'new_v', 'new_v_rel_bias': 'new_v', 'new_v_w_out': 'new_v', 'new_v_mlp_norm_g': 'new_v', 'new_v_w_up': 'new_v', 'new_v_w_down': 'new_v'}


def _forward(args):
    return _fwd_reference(*[args[k] for k in FWD_PARAMS])


def _output_shape():
    def fwd():
        inp = _fwd_setup_inputs(0)
        return _fwd_reference(*[inp[k] for k in FWD_PARAMS])
    out = _jax.eval_shape(fwd)
    return out.shape, out.dtype

N_MICROBATCH = 1
ADAM_LR = 0.001
ADAM_B1 = 0.9
ADAM_B2 = 0.999
ADAM_EPS = 1e-08
ADAM_WD = 0.01
ADAM_STEP = 10
PER_EXAMPLE_BATCH_AXIS = {'x': 0, 'loss_target': 0}
SHARED_INPUTS = []
_WEIGHT_DTYPES = {'mix_norm_g': _jnp.float32, 'w_in': _jnp.float32, 'pool_w': _jnp.float32, 'pool_scale': _jnp.float32, 'q_norm_g': _jnp.float32, 'k_norm_g': _jnp.float32, 'rel_bias': _jnp.float32, 'w_out': _jnp.float32, 'mlp_norm_g': _jnp.float32, 'w_up': _jnp.float32, 'w_down': _jnp.float32}
MOMENT_SCALE = {'mix_norm_g': 2.413324e+01, 'w_in': 1.451778e+00, 'pool_w': 5.552982e+00, 'pool_scale': 5.055854e+01, 'q_norm_g': 3.493568e+00, 'k_norm_g': 3.524481e+00, 'rel_bias': 3.798269e-01, 'w_out': 2.965982e+00, 'mlp_norm_g': 1.911005e+02, 'w_up': 1.934276e+00, 'w_down': 1.588558e+01}


def _to_microbatches(a, axis):
    t = _jnp.moveaxis(a, axis, 0)
    t = t.reshape((N_MICROBATCH, t.shape[0] // N_MICROBATCH) + t.shape[1:])
    return _jnp.moveaxis(t, 1, axis + 1)


def setup_inputs(seed: int = 0) -> dict:
    inp = _fwd_setup_inputs(seed)
    key = _jax.random.fold_in(_jax.random.key(seed), 7919)
    shape, _ = _output_shape()
    out = dict(inp)
    out["loss_target"] = _jax.random.normal(_jax.random.fold_in(key, 0), shape, _jnp.float32)
    for i, name in enumerate(TWIN_WEIGHTS):
        w = inp[name].astype(_jnp.float32)
        if MOMENT_SCALE is None:
            s = _jnp.sqrt(_jnp.mean(_jnp.square(w)) + 1e-30)
        else:
            s = MOMENT_SCALE[name]
        km, kv = _jax.random.split(_jax.random.fold_in(key, i + 1))
        out[name] = w
        out["m_" + name] = s * _jax.random.normal(km, w.shape, _jnp.float32)
        out["v_" + name] = (s * s) * _jax.random.uniform(kv, w.shape, _jnp.float32, 0.5, 1.5)
    if N_MICROBATCH > 1:
        for name, axis in PER_EXAMPLE_BATCH_AXIS.items():
            out[name] = _to_microbatches(out[name], axis)
    return {'x': out['x'], 'mix_norm_g': out['mix_norm_g'], 'w_in': out['w_in'], 'pool_w': out['pool_w'], 'pool_scale': out['pool_scale'], 'q_norm_g': out['q_norm_g'], 'k_norm_g': out['k_norm_g'], 'rel_bias': out['rel_bias'], 'w_out': out['w_out'], 'mlp_norm_g': out['mlp_norm_g'], 'w_up': out['w_up'], 'w_down': out['w_down'], 'loss_target': out['loss_target'], 'm_mix_norm_g': out['m_mix_norm_g'], 'm_w_in': out['m_w_in'], 'm_pool_w': out['m_pool_w'], 'm_pool_scale': out['m_pool_scale'], 'm_q_norm_g': out['m_q_norm_g'], 'm_k_norm_g': out['m_k_norm_g'], 'm_rel_bias': out['m_rel_bias'], 'm_w_out': out['m_w_out'], 'm_mlp_norm_g': out['m_mlp_norm_g'], 'm_w_up': out['m_w_up'], 'm_w_down': out['m_w_down'], 'v_mix_norm_g': out['v_mix_norm_g'], 'v_w_in': out['v_w_in'], 'v_pool_w': out['v_pool_w'], 'v_pool_scale': out['v_pool_scale'], 'v_q_norm_g': out['v_q_norm_g'], 'v_k_norm_g': out['v_k_norm_g'], 'v_rel_bias': out['v_rel_bias'], 'v_w_out': out['v_w_out'], 'v_mlp_norm_g': out['v_mlp_norm_g'], 'v_w_up': out['v_w_up'], 'v_w_down': out['v_w_down']}


def _loss(weights, diff, rest, loss_target):
    with _jax.named_scope("forward"):
        args = {**rest, TWIN_DIFF_INPUT: diff, **{k: w.astype(_WEIGHT_DTYPES[k]) for k, w in weights.items()}}
        y = _forward(args)
    with _jax.named_scope("loss_head"):
        err = _jnp.square(y.astype(_jnp.float32) - loss_target)
        return 0.5 * _jnp.sum(_jnp.mean(err, axis=-1)) if err.ndim else 0.5 * err


def _adamw(w, g, m, v):
    m = ADAM_B1 * m + (1.0 - ADAM_B1) * g
    v = ADAM_B2 * v + (1.0 - ADAM_B2) * _jnp.square(g)
    m_hat = m / (1.0 - ADAM_B1 ** ADAM_STEP)
    v_hat = v / (1.0 - ADAM_B2 ** ADAM_STEP)
    delta = -ADAM_LR * (m_hat / (_jnp.sqrt(v_hat) + ADAM_EPS) + ADAM_WD * w)
    return delta, m, v


def reference(x, mix_norm_g, w_in, pool_w, pool_scale, q_norm_g, k_norm_g, rel_bias, w_out, mlp_norm_g, w_up, w_down, loss_target, m_mix_norm_g, m_w_in, m_pool_w, m_pool_scale, m_q_norm_g, m_k_norm_g, m_rel_bias, m_w_out, m_mlp_norm_g, m_w_up, m_w_down, v_mix_norm_g, v_w_in, v_pool_w, v_pool_scale, v_q_norm_g, v_k_norm_g, v_rel_bias, v_w_out, v_mlp_norm_g, v_w_up, v_w_down):
    given = dict(x=x, mix_norm_g=mix_norm_g, w_in=w_in, pool_w=pool_w, pool_scale=pool_scale, q_norm_g=q_norm_g, k_norm_g=k_norm_g, rel_bias=rel_bias, w_out=w_out, mlp_norm_g=mlp_norm_g, w_up=w_up, w_down=w_down, loss_target=loss_target, m_mix_norm_g=m_mix_norm_g, m_w_in=m_w_in, m_pool_w=m_pool_w, m_pool_scale=m_pool_scale, m_q_norm_g=m_q_norm_g, m_k_norm_g=m_k_norm_g, m_rel_bias=m_rel_bias, m_w_out=m_w_out, m_mlp_norm_g=m_mlp_norm_g, m_w_up=m_w_up, m_w_down=m_w_down, v_mix_norm_g=v_mix_norm_g, v_w_in=v_w_in, v_pool_w=v_pool_w, v_pool_scale=v_pool_scale, v_q_norm_g=v_q_norm_g, v_k_norm_g=v_k_norm_g, v_rel_bias=v_rel_bias, v_w_out=v_w_out, v_mlp_norm_g=v_mlp_norm_g, v_w_up=v_w_up, v_w_down=v_w_down)
    weights = {n: given[n] for n in TWIN_WEIGHTS}
    shared = {n: given[n] for n in SHARED_INPUTS}
    per_example = {n: given[n] for n in ['x']}
    grad_fn = _jax.value_and_grad(_loss, argnums=(0, 1))

    def one_microbatch(ex, loss_target):
        ex = dict(ex)
        diff = ex.pop(TWIN_DIFF_INPUT)
        return grad_fn(weights, diff, {**shared, **ex}, loss_target)

    if N_MICROBATCH == 1:
        loss, (grad_w, grad_x) = one_microbatch(per_example, given["loss_target"])
    else:
        def body(carry, xs):
            loss_sum, grad_sum = carry
            l_k, (gw_k, gx_k) = one_microbatch(xs[0], xs[1])
            with _jax.named_scope("update"):
                return (loss_sum + l_k, _jax.tree.map(_jnp.add, grad_sum, gw_k)), gx_k

        init = (_jnp.zeros((), _jnp.float32), _jax.tree.map(_jnp.zeros_like, weights))
        (loss, grad_w), grad_x = _jax.lax.scan(body, init, (per_example, given["loss_target"]))
    with _jax.named_scope("update"):
        delta_w, new_m, new_v = {}, {}, {}
        for n in TWIN_WEIGHTS:
            delta_w[n], new_m[n], new_v[n] = _adamw(weights[n], grad_w[n], given["m_" + n], given["v_" + n])
    return (loss, grad_x, *[grad_w[n] for n in TWIN_WEIGHTS], *[delta_w[n] for n in TWIN_WEIGHTS],
            *[new_m[n] for n in TWIN_WEIGHTS], *[new_v[n] for n in TWIN_WEIGHTS])
```

```python
import functools
import math

import jax
import jax.numpy as jnp
import numpy as np
from jax import lax
from jax.experimental import pallas as pl
from jax.experimental.pallas import tpu as pltpu

F32 = jnp.float32
MXU_DTYPE = jnp.bfloat16
WIRE_DTYPE = jnp.bfloat16

NORM_EPS = 1e-6
NEG_INF = -1e30
LANES = 128
HEAD_DIM = 64
N_HEADS = 8
POOL_WIDTH = 512
ATTN_WIDTH = 512
POOL_WINDOWS = (2, 4, 8, 16)
POOL_HALO = 16
DILATED_PATTERNS = ((128, 1), (512, 4), (2048, 16))
ATT_BLOCK = 128
N_BUCKETS = 32
MAX_DISTANCE = 2048
N_CHIPS = 4
N_DEV = 8
ADAM_LR, ADAM_B1, ADAM_B2, ADAM_EPS, ADAM_WD, ADAM_STEP = 0.001, 0.9, 0.999, 1e-08, 0.01, 10
VMEM_LIMIT = 56 * 1024 * 1024
MESH = pl.DeviceIdType.MESH
ANY = pl.BlockSpec(memory_space=pl.ANY)

SMALL_ROWS = 72


def _mm(a, b):
    return jnp.dot(a, b, preferred_element_type=F32)


def _mm_nt(a, b):
    return lax.dot_general(a, b, (((1,), (1,)), ((), ())), preferred_element_type=F32)


def _mm_tn(a, b):
    return lax.dot_general(a, b, (((0,), (0,)), ((), ())), preferred_element_type=F32)


def _params(sem=None, **kw):
    if sem is not None:
        kw["dimension_semantics"] = sem
    return pltpu.CompilerParams(vmem_limit_bytes=VMEM_LIMIT, **kw)


def _low_half():
    return lax.broadcasted_iota(jnp.int32, (1, LANES), 1) < HEAD_DIM


def _head_sum_bcast(y):
    lo = _low_half()
    outs = []
    for j in range(y.shape[1] // LANES):
        c = y[:, j * LANES:(j + 1) * LANES]
        s_lo = jnp.sum(jnp.where(lo, c, 0.0), axis=-1, keepdims=True)
        s_hi = jnp.sum(jnp.where(lo, 0.0, c), axis=-1, keepdims=True)
        outs.append(jnp.where(lo, s_lo, s_hi))
    return jnp.concatenate(outs, axis=-1)


def _rms_bwd(dn, hn, r):
    return r * (dn - hn * jnp.mean(dn * hn, axis=-1, keepdims=True))


def _t5_bucket_np(dist):
    max_exact = N_BUCKETS // 2
    d_f = np.maximum(dist, 1).astype(np.float32)
    ratio = (np.log(d_f / np.float32(max_exact)) / np.float32(math.log(MAX_DISTANCE / max_exact))).astype(np.float32)
    large = max_exact + (ratio * np.float32(N_BUCKETS - max_exact)).astype(np.int32)
    large = np.minimum(large, N_BUCKETS - 1)
    return np.where(dist < max_exact, dist, large).astype(np.int32)


def _bucket_tables():
    qq = np.arange(ATT_BLOCK)[:, None]
    kk = np.arange(2 * ATT_BLOCK)[None, :]
    dist = np.clip(qq + ATT_BLOCK - kk, 0, ATT_BLOCK)
    return np.stack([_t5_bucket_np(dist * dl) for (_, dl) in DILATED_PATTERNS])


def _f1_call(x, g1, win, poolw, pscale, qg, kg, tm):
    s, d = x.shape
    nblk = s // tm

    def body(x_ref, g1_ref, win_ref, pw_ref, ps_ref, qg_ref, kg_ref,
             a_ref, pooled_ref, ypool_ref, q32_ref, k32_ref, qn_ref, kn_ref, v_ref, ubuf):
        i = pl.program_id(0)
        xv = x_ref[...]
        r = lax.rsqrt(jnp.mean(xv * xv, axis=-1, keepdims=True) + NORM_EPS)
        a = ((xv * r) * g1_ref[...]).astype(MXU_DTYPE)
        a_ref[...] = a
        u = _mm(a, win_ref[0])
        q = _mm(a, win_ref[1])
        k = _mm(a, win_ref[2])
        v_ref[...] = _mm(a, win_ref[3]).astype(MXU_DTYPE)
        q32_ref[...] = q
        k32_ref[...] = k
        rq = lax.rsqrt(_head_sum_bcast(q * q) * (1.0 / HEAD_DIM) + NORM_EPS)
        qn_ref[...] = (((q * rq) * qg_ref[...]) * (HEAD_DIM ** -0.5)).astype(MXU_DTYPE)
        rk = lax.rsqrt(_head_sum_bcast(k * k) * (1.0 / HEAD_DIM) + NORM_EPS)
        kn_ref[...] = ((k * rk) * kg_ref[...]).astype(MXU_DTYPE)

        @pl.when(i == 0)
        def _():
            ubuf[0:POOL_HALO, :] = jnp.zeros((POOL_HALO, POOL_WIDTH), F32)

        @pl.when(i > 0)
        def _():
            ubuf[0:POOL_HALO, :] = ubuf[tm:tm + POOL_HALO, :]

        ubuf[POOL_HALO:POOL_HALO + tm, :] = u
        t = i * tm + lax.broadcasted_iota(jnp.int32, (tm, 1), 0)
        for g, w in enumerate(POOL_WINDOWS):
            ls = slice(g * LANES, (g + 1) * LANES)
            ug = u[:, ls]
            acc = ug
            for sh in range(1, w):
                acc = acc + ubuf[POOL_HALO - sh:POOL_HALO - sh + tm, ls]
            cnt = jnp.minimum(t + 1, w).astype(F32)
            pooled = (acc / cnt - ug).astype(MXU_DTYPE)
            pooled_ref[:, ls] = pooled
            ypool_ref[:, ls] = (_mm(pooled, pw_ref[g]) * ps_ref[:, ls]).astype(MXU_DTYPE)

    tok = lambda w: pl.BlockSpec((tm, w), lambda i: (i, 0))
    full = lambda shp: pl.BlockSpec(shp, lambda i: (0,) * len(shp))
    return pl.pallas_call(
        body, name="fwd_inproj",
        grid=(nblk,),
        in_specs=[tok(d), full((1, d)), full(win.shape), full(poolw.shape), full((1, POOL_WIDTH)),
                  full((1, ATTN_WIDTH)), full((1, ATTN_WIDTH))],
        out_specs=[tok(d), tok(POOL_WIDTH), tok(POOL_WIDTH), tok(ATTN_WIDTH), tok(ATTN_WIDTH),
                   tok(ATTN_WIDTH), tok(ATTN_WIDTH), tok(ATTN_WIDTH)],
        out_shape=[jax.ShapeDtypeStruct((s, d), MXU_DTYPE),
                   jax.ShapeDtypeStruct((s, POOL_WIDTH), MXU_DTYPE),
                   jax.ShapeDtypeStruct((s, POOL_WIDTH), MXU_DTYPE),
                   jax.ShapeDtypeStruct((s, ATTN_WIDTH), F32),
                   jax.ShapeDtypeStruct((s, ATTN_WIDTH), F32),
                   jax.ShapeDtypeStruct((s, ATTN_WIDTH), MXU_DTYPE),
                   jax.ShapeDtypeStruct((s, ATTN_WIDTH), MXU_DTYPE),
                   jax.ShapeDtypeStruct((s, ATTN_WIDTH), MXU_DTYPE)],
        scratch_shapes=[pltpu.VMEM((tm + POOL_HALO, POOL_WIDTH), F32)],
        compiler_params=_params(("arbitrary",)),
    )(x, g1, win, poolw, pscale, qg, kg)


def _band_mask(n):
    qq = lax.broadcasted_iota(jnp.int32, (ATT_BLOCK, 2 * ATT_BLOCK), 0)
    kk = lax.broadcasted_iota(jnp.int32, (ATT_BLOCK, 2 * ATT_BLOCK), 1)
    dist = qq + ATT_BLOCK - kk
    return (dist >= 0) & (dist <= ATT_BLOCK) & ((n > 0) | (kk >= ATT_BLOCK))


def _attn_fwd_call(qn, kn, v, bias, dl, name):
    s, w = qn.shape
    rows = s // dl
    nb = rows // ATT_BLOCK
    view = lambda t: t.reshape(rows, dl * w)

    def body(q_ref, kc_ref, kp_ref, vc_ref, vp_ref, b_ref, o_ref, lse_ref):
        n = pl.program_id(1)
        valid = _band_mask(n)
        lo = _low_half()
        for j in range(w // LANES):
            ls = slice(j * LANES, (j + 1) * LANES)
            qp = q_ref[:, ls]
            kcat = jnp.concatenate([kp_ref[:, ls], kc_ref[:, ls]], axis=0)
            vcat = jnp.concatenate([vp_ref[:, ls], vc_ref[:, ls]], axis=0)
            o_h, lse_h = [], []
            for e in range(2):
                sel = lo if e == 0 else jnp.logical_not(lo)
                qm = jnp.where(sel, qp, jnp.zeros_like(qp))
                sc = _mm_nt(qm, kcat) + b_ref[2 * j + e]
                sc = jnp.where(valid, sc, NEG_INF)
                m = jnp.max(sc, axis=-1, keepdims=True)
                p = jnp.exp(sc - m)
                l = jnp.sum(p, axis=-1, keepdims=True)
                acc = _mm(p.astype(MXU_DTYPE), vcat)
                o_h.append(acc / l)
                lse_h.append(m + jnp.log(l))
            o_ref[:, ls] = jnp.where(lo, o_h[0], o_h[1])
            lse_ref[:, ls] = jnp.where(lo, lse_h[0], lse_h[1])

    cur = pl.BlockSpec((ATT_BLOCK, w), lambda r, n: (n, r))
    prev = pl.BlockSpec((ATT_BLOCK, w), lambda r, n: (jnp.maximum(n - 1, 0), r))
    o, lse = pl.pallas_call(
        body, name=name,
        grid=(dl, nb),
        in_specs=[cur, cur, prev, cur, prev, pl.BlockSpec(bias.shape, lambda r, n: (0, 0, 0))],
        out_specs=[cur, cur],
        out_shape=[jax.ShapeDtypeStruct((rows, dl * w), F32), jax.ShapeDtypeStruct((rows, dl * w), F32)],
        compiler_params=_params(("arbitrary", "arbitrary")),
    )(view(qn), view(kn), view(kn), view(v), view(v), bias)
    return o.reshape(s, w), lse.reshape(s, w)


def _attn_bwd_call(qn, kn, v, do, lse, delta, bias, dl, name):
    s, w = qn.shape
    rows = s // dl
    nb = rows // ATT_BLOCK
    view = lambda t: t.reshape(rows, dl * w)

    def body(q_ref, kc_ref, kp_ref, vc_ref, vp_ref, do_ref, lse_ref, dlt_ref, b_ref,
             dq_ref, dk_ref, dv_ref, db_ref, carry_k, carry_v):
        r = pl.program_id(0)
        step = pl.program_id(1)
        n = nb - 1 - step
        valid = _band_mask(n)
        lo = _low_half()

        @pl.when((r == 0) & (step == 0))
        def _():
            db_ref[...] = jnp.zeros(db_ref.shape, F32)

        @pl.when(step == 0)
        def _():
            carry_k[...] = jnp.zeros(carry_k.shape, F32)
            carry_v[...] = jnp.zeros(carry_v.shape, F32)

        for j in range(w // LANES):
            ls = slice(j * LANES, (j + 1) * LANES)
            qp = q_ref[:, ls]
            dop = do_ref[:, ls]
            kcat = jnp.concatenate([kp_ref[:, ls], kc_ref[:, ls]], axis=0)
            vcat = jnp.concatenate([vp_ref[:, ls], vc_ref[:, ls]], axis=0)
            dq_h = []
            dk = jnp.zeros((2 * ATT_BLOCK, LANES), F32)
            dv = jnp.zeros((2 * ATT_BLOCK, LANES), F32)
            for e in range(2):
                sel = lo if e == 0 else jnp.logical_not(lo)
                col = j * LANES + e * HEAD_DIM
                qm = jnp.where(sel, qp, jnp.zeros_like(qp))
                dom = jnp.where(sel, dop, jnp.zeros_like(dop))
                sc = _mm_nt(qm, kcat) + b_ref[2 * j + e]
                p = jnp.where(valid, jnp.exp(sc - lse_ref[:, col:col + 1]), 0.0)
                dp = _mm_nt(dom, vcat)
                ds = p * (dp - dlt_ref[:, col:col + 1])
                db_ref[2 * j + e] += ds
                ds_c = ds.astype(MXU_DTYPE)
                dq_h.append(_mm(ds_c, kcat))
                dk = dk + _mm_tn(ds_c, qm)
                dv = dv + _mm_tn(p.astype(MXU_DTYPE), dom)
            dq_ref[:, ls] = jnp.where(lo, dq_h[0], dq_h[1])
            dk_ref[:, ls] = dk[ATT_BLOCK:, :] + carry_k[:, ls]
            dv_ref[:, ls] = dv[ATT_BLOCK:, :] + carry_v[:, ls]
            carry_k[:, ls] = dk[:ATT_BLOCK, :]
            carry_v[:, ls] = dv[:ATT_BLOCK, :]

    cur = pl.BlockSpec((ATT_BLOCK, w), lambda r, t: (nb - 1 - t, r))
    prev = pl.BlockSpec((ATT_BLOCK, w), lambda r, t: (jnp.maximum(nb - 2 - t, 0), r))
    bspec = pl.BlockSpec(bias.shape, lambda r, t: (0, 0, 0))
    dq, dk, dv, db = pl.pallas_call(
        body, name=name,
        grid=(dl, nb),
        in_specs=[cur, cur, prev, cur, prev, cur, cur, cur, bspec],
        out_specs=[cur, cur, cur, bspec],
        out_shape=[jax.ShapeDtypeStruct((rows, dl * w), F32)] * 3 + [jax.ShapeDtypeStruct(bias.shape, F32)],
        scratch_shapes=[pltpu.VMEM((ATT_BLOCK, w), F32), pltpu.VMEM((ATT_BLOCK, w), F32)],
        compiler_params=_params(("arbitrary", "arbitrary")),
    )(view(qn), view(kn), view(kn), view(v), view(v), view(do), view(lse), view(delta), bias)
    return dq.reshape(s, w), dk.reshape(s, w), dv.reshape(s, w), db


def _f2_call(x, tgt, ypool, o_ps, lse_ps, wout, wup, wdown, g2, tm):
    s, d = x.shape
    nblk = s // tm
    nch, _, fch = wup.shape
    dff = nch * fch
    mixw = POOL_WIDTH + ATTN_WIDTH

    def body(x_ref, t_ref, yp_ref, o1, o2, o3, l1, l2, l3, g2_ref, wout_hbm, wup_hbm, wdown_hbm,
             mixed_ref, c_ref, ff_ref, dz_ref, dy_ref, dh1_ref, dyp_ref, do_ref, dlt_ref, lse_ref, dg2_ref, loss_ref,
             wout_v, wup_v, wdown_v, rz):
        i = pl.program_id(0)

        @pl.when(i == 0)
        def _():
            pltpu.sync_copy(wout_hbm, wout_v)
            pltpu.sync_copy(wup_hbm, wup_v)
            pltpu.sync_copy(wdown_hbm, wdown_v)
            dg2_ref[...] = jnp.zeros(dg2_ref.shape, F32)
            loss_ref[...] = jnp.zeros(loss_ref.shape, F32)

        la, lb, lc = l1[...], l2[...], l3[...]
        mx = jnp.maximum(jnp.maximum(la, lb), lc)
        wa, wb, wc = jnp.exp(la - mx), jnp.exp(lb - mx), jnp.exp(lc - mx)
        den = wa + wb + wc
        o = (wa * o1[...] + wb * o2[...] + wc * o3[...]) / den
        lse_ref[...] = mx + jnp.log(den)
        mixed = jnp.concatenate([yp_ref[...], o.astype(MXU_DTYPE)], axis=-1)
        mixed_ref[...] = mixed
        h1 = x_ref[...] + _mm(mixed, wout_v[...])
        r2 = lax.rsqrt(jnp.mean(h1 * h1, axis=-1, keepdims=True) + NORM_EPS)
        hn = h1 * r2
        c = (hn * g2_ref[...]).astype(MXU_DTYPE)
        c_ref[...] = c
        y = h1
        for j in range(nch):
            cs = slice(j * fch, (j + 1) * fch)
            z = jnp.maximum(_mm(c, wup_v[j]), 0.0)
            rz[:, cs] = z
            ff = (z * z).astype(MXU_DTYPE)
            ff_ref[:, cs] = ff
            y = y + _mm(ff, wdown_v[j])
        err = y - t_ref[...]
        loss_ref[...] += jnp.sum(err * err) * (0.5 / d)
        dy = err * (1.0 / d)
        dy_c = dy.astype(MXU_DTYPE)
        dy_ref[...] = dy_c
        dc = jnp.zeros((tm, d), F32)
        for j in range(nch):
            cs = slice(j * fch, (j + 1) * fch)
            dz = (_mm_nt(dy_c, wdown_v[j]) * (2.0 * rz[:, cs])).astype(MXU_DTYPE)
            dz_ref[:, cs] = dz
            dc = dc + _mm_nt(dz, wup_v[j])
        dg2_ref[...] += jnp.sum(dc * hn, axis=0, keepdims=True)
        dh1 = dy + _rms_bwd(dc * g2_ref[...], hn, r2)
        dh1_ref[...] = dh1
        dmix = _mm_nt(dh1.astype(MXU_DTYPE), wout_v[...])
        dyp_ref[...] = dmix[:, :POOL_WIDTH]
        do = dmix[:, POOL_WIDTH:]
        do_ref[...] = do.astype(MXU_DTYPE)
        dlt_ref[...] = _head_sum_bcast(do * o)

    tok = lambda w: pl.BlockSpec((tm, w), lambda i: (i, 0))
    const = lambda shp: pl.BlockSpec(shp, lambda i: (0,) * len(shp))
    return pl.pallas_call(
        body, name="fwd_mlp_bwd_mlp",
        grid=(nblk,),
        in_specs=[tok(d), tok(d), tok(POOL_WIDTH)] + [tok(ATTN_WIDTH)] * 6 + [const((1, d)), ANY, ANY, ANY],
        out_specs=[tok(mixw), tok(d), tok(dff), tok(dff), tok(d), tok(d), tok(POOL_WIDTH), tok(ATTN_WIDTH),
                   tok(ATTN_WIDTH), tok(ATTN_WIDTH), const((1, d)), const((1, LANES))],
        out_shape=[jax.ShapeDtypeStruct((s, mixw), MXU_DTYPE),
                   jax.ShapeDtypeStruct((s, d), MXU_DTYPE),
                   jax.ShapeDtypeStruct((s, dff), MXU_DTYPE),
                   jax.ShapeDtypeStruct((s, dff), MXU_DTYPE),
                   jax.ShapeDtypeStruct((s, d), MXU_DTYPE),
                   jax.ShapeDtypeStruct((s, d), F32),
                   jax.ShapeDtypeStruct((s, POOL_WIDTH), F32),
                   jax.ShapeDtypeStruct((s, ATTN_WIDTH), MXU_DTYPE),
                   jax.ShapeDtypeStruct((s, ATTN_WIDTH), F32),
                   jax.ShapeDtypeStruct((s, ATTN_WIDTH), F32),
                   jax.ShapeDtypeStruct((1, d), F32),
                   jax.ShapeDtypeStruct((1, LANES), F32)],
        scratch_shapes=[pltpu.VMEM(wout.shape, MXU_DTYPE), pltpu.VMEM(wup.shape, MXU_DTYPE),
                        pltpu.VMEM(wdown.shape, MXU_DTYPE), pltpu.VMEM((tm, dff), F32)],
        compiler_params=_params(("arbitrary",)),
    )(x, tgt, ypool, *o_ps, *lse_ps, g2, wout, wup, wdown)


def _bproj_call(dq_ps, dk_ps, dv_ps, q32, k32, dypool, pooled, x, dh1, win, poolw, pscale, qg, kg, g1, tm):
    s, d = x.shape
    nblk = s // tm
    ngrp = len(POOL_WINDOWS)

    def body(dq1, dq2, dq3, dk1, dk2, dk3, dv1, dv2, dv3, q_ref, k_ref, dyp_ref, pooled_ref, x_ref, dh1_ref,
             win_hbm, pw_ref, ps_ref, qg_ref, kg_ref, g1_ref,
             dx_ref, dproj_ref, dg1_ref, dqg_ref, dkg_ref, dpw_ref, dps_ref, win_v, ebuf):
        step = pl.program_id(0)
        i = nblk - 1 - step

        @pl.when(step == 0)
        def _():
            pltpu.sync_copy(win_hbm, win_v)
            dg1_ref[...] = jnp.zeros(dg1_ref.shape, F32)
            dqg_ref[...] = jnp.zeros(dqg_ref.shape, F32)
            dkg_ref[...] = jnp.zeros(dkg_ref.shape, F32)
            dpw_ref[...] = jnp.zeros(dpw_ref.shape, F32)
            dps_ref[...] = jnp.zeros(dps_ref.shape, F32)
            ebuf[tm:tm + POOL_HALO, :] = jnp.zeros((POOL_HALO, POOL_WIDTH), F32)

        @pl.when(step > 0)
        def _():
            ebuf[tm:tm + POOL_HALO, :] = ebuf[0:POOL_HALO, :]

        def qk_bwd(dn_sum, raw, gain, scale, dgain_ref):
            rr = lax.rsqrt(_head_sum_bcast(raw * raw) * (1.0 / HEAD_DIM) + NORM_EPS)
            hn = raw * rr
            dgain_ref[...] += jnp.sum(dn_sum * hn, axis=0, keepdims=True) * scale
            dn = dn_sum * (gain * scale)
            return rr * (dn - hn * (_head_sum_bcast(dn * hn) * (1.0 / HEAD_DIM)))

        dq = qk_bwd(dq1[...] + dq2[...] + dq3[...], q_ref[...], qg_ref[...], HEAD_DIM ** -0.5, dqg_ref)
        dk = qk_bwd(dk1[...] + dk2[...] + dk3[...], k_ref[...], kg_ref[...], 1.0, dkg_ref)
        dv = dv1[...] + dv2[...] + dv3[...]

        t = i * tm + lax.broadcasted_iota(jnp.int32, (tm, 1), 0)
        dpooled = []
        for g, w in enumerate(POOL_WINDOWS):
            ls = slice(g * LANES, (g + 1) * LANES)
            dm = dyp_ref[:, ls]
            pg = pooled_ref[:, ls]
            dps_ref[:, ls] += jnp.sum(dm * _mm(pg, pw_ref[g]), axis=0, keepdims=True)
            dms = (dm * ps_ref[:, ls]).astype(MXU_DTYPE)
            dpw_ref[g] += _mm_tn(pg, dms)
            dpg = _mm_nt(dms, pw_ref[g])
            dpooled.append(dpg)
            ebuf[0:tm, ls] = dpg / jnp.minimum(t + 1, w).astype(F32)
        du = []
        for g, w in enumerate(POOL_WINDOWS):
            ls = slice(g * LANES, (g + 1) * LANES)
            acc = ebuf[0:tm, ls]
            for sh in range(1, w):
                acc = acc + ebuf[sh:sh + tm, ls]
            du.append(acc - dpooled[g])
        parts = [jnp.concatenate(du, axis=-1), dq, dk, dv]
        da = jnp.zeros((tm, d), F32)
        for p, part in enumerate(parts):
            pc = part.astype(MXU_DTYPE)
            dproj_ref[:, p * POOL_WIDTH:(p + 1) * POOL_WIDTH] = pc
            da = da + _mm_nt(pc, win_v[p])
        xv = x_ref[...]
        r = lax.rsqrt(jnp.mean(xv * xv, axis=-1, keepdims=True) + NORM_EPS)
        xn = xv * r
        dg1_ref[...] += jnp.sum(da * xn, axis=0, keepdims=True)
        dx_ref[...] = dh1_ref[...] + _rms_bwd(da * g1_ref[...], xn, r)

    tok = lambda w: pl.BlockSpec((tm, w), lambda t: (nblk - 1 - t, 0))
    const = lambda shp: pl.BlockSpec(shp, lambda t: (0,) * len(shp))
    return pl.pallas_call(
        body, name="bwd_inproj",
        grid=(nblk,),
        in_specs=[tok(ATTN_WIDTH)] * 9 + [tok(ATTN_WIDTH), tok(ATTN_WIDTH), tok(POOL_WIDTH), tok(POOL_WIDTH), tok(d), tok(d),
                                           ANY, const(poolw.shape), const((1, POOL_WIDTH)), const((1, ATTN_WIDTH)),
                                           const((1, ATTN_WIDTH)), const((1, d))],
        out_specs=[tok(d), tok(4 * POOL_WIDTH), const((1, d)), const((1, ATTN_WIDTH)), const((1, ATTN_WIDTH)),
                   const((ngrp, LANES, LANES)), const((1, POOL_WIDTH))],
        out_shape=[jax.ShapeDtypeStruct((s, d), F32),
                   jax.ShapeDtypeStruct((s, 4 * POOL_WIDTH), MXU_DTYPE),
                   jax.ShapeDtypeStruct((1, d), F32),
                   jax.ShapeDtypeStruct((1, ATTN_WIDTH), F32),
                   jax.ShapeDtypeStruct((1, ATTN_WIDTH), F32),
                   jax.ShapeDtypeStruct((ngrp, LANES, LANES), F32),
                   jax.ShapeDtypeStruct((1, POOL_WIDTH), F32)],
        scratch_shapes=[pltpu.VMEM(win.shape, MXU_DTYPE), pltpu.VMEM((tm + POOL_HALO, POOL_WIDTH), F32)],
        compiler_params=_params(("arbitrary",)),
    )(*dq_ps, *dk_ps, *dv_ps, q32, k32, dypool, pooled, x, dh1, win, poolw, pscale, qg, kg, g1)


def _wgrad_call(a, b, bm, bn, bk, out_shape, out_block, out_index, name):
    s, m = a.shape
    _, n = b.shape
    nk = s // bk

    def body(a_ref, b_ref, o_ref):
        k = pl.program_id(2)

        @pl.when(k == 0)
        def _():
            o_ref[...] = jnp.zeros(o_ref.shape, F32)

        o_ref[...] += _mm_tn(a_ref[...].astype(MXU_DTYPE), b_ref[...].astype(MXU_DTYPE))

    return pl.pallas_call(
        body, name=name,
        grid=(m // bm, n // bn, nk),
        in_specs=[pl.BlockSpec((bk, bm), lambda i, j, k: (k, i)), pl.BlockSpec((bk, bn), lambda i, j, k: (k, j))],
        out_specs=pl.BlockSpec(out_block, out_index),
        out_shape=jax.ShapeDtypeStruct(out_shape, F32),
        compiler_params=_params(("arbitrary", "arbitrary", "arbitrary")),
    )(a, b)


def _rel_bias_grad_call(dbias, buckets):
    npat, nh = dbias.shape[0], dbias.shape[1]

    def body(db_ref, bk_ref, out_ref):
        lane = lax.broadcasted_iota(jnp.int32, (nh, LANES), 1)
        out = jnp.zeros((nh, LANES), F32)
        for b in range(N_BUCKETS):
            tot = jnp.zeros((nh, 1), F32)
            for p in range(npat):
                hit = jnp.where(bk_ref[p][None] == b, db_ref[p], 0.0)
                tot = tot + jnp.sum(jnp.sum(hit, axis=2), axis=1, keepdims=True)
            out = jnp.where(lane == b, tot, out)
        out_ref[...] = out

    return pl.pallas_call(
        body, name="rel_bias_grad",
        out_shape=jax.ShapeDtypeStruct((nh, LANES), F32),
        compiler_params=_params(),
    )(dbias, buckets)


def _bias_tables(rel_bias):
    return [jnp.transpose(rel_bias[bk], (2, 0, 1)) for bk in _bucket_tables()]


def _local_grads(x, tgt, g1, win, poolw, pscale, qg, kg, rel_bias, wout, g2, wup, wdown):
    s, d = x.shape
    g1r, g2r = g1.reshape(1, d), g2.reshape(1, d)
    psr = pscale.reshape(1, POOL_WIDTH)
    qgr = jnp.tile(qg, N_HEADS).reshape(1, ATTN_WIDTH)
    kgr = jnp.tile(kg, N_HEADS).reshape(1, ATTN_WIDTH)
    pw_c = poolw.astype(MXU_DTYPE)
    biases = _bias_tables(rel_bias)

    a, pooled, ypool, q32, k32, qn, kn, v = _f1_call(x, g1r, win, pw_c, psr, qgr, kgr, tm=512)
    o_ps, lse_ps = [], []
    for p, (_, dl) in enumerate(DILATED_PATTERNS):
        o, lse = _attn_fwd_call(qn, kn, v, biases[p], dl, name=f"attn_fwd_d{dl}")
        o_ps.append(o)
        lse_ps.append(lse)
    (mixed, c, ff, dz, dy, dh1, dypool, do, delta, lse, dg2, loss) = _f2_call(
        x, tgt, ypool, o_ps, lse_ps, wout, wup, wdown, g2r, tm=256)
    dq_ps, dk_ps, dv_ps, db_ps = [], [], [], []
    for p, (_, dl) in enumerate(DILATED_PATTERNS):
        dq, dk, dv, db = _attn_bwd_call(qn, kn, v, do, lse, delta, biases[p], dl, name=f"attn_bwd_d{dl}")
        dq_ps.append(dq)
        dk_ps.append(dk)
        dv_ps.append(dv)
        db_ps.append(db)
    dx, dproj, dg1, dqg, dkg, dpw, dps = _bproj_call(
        dq_ps, dk_ps, dv_ps, q32, k32, dypool, pooled, x, dh1, win, pw_c, psr, qgr, kgr, g1r, tm=256)

    dff = ff.shape[1]
    nin = dproj.shape[1] // N_CHIPS
    g_in = _wgrad_call(a, dproj, d, nin, 512, (N_CHIPS, d, nin), (None, d, nin), lambda i, j, k: (j, 0, 0), "wgrad_in")
    g_out = _wgrad_call(mixed, dh1, d // N_CHIPS, d, 512, (N_CHIPS, d // N_CHIPS, d), (None, d // N_CHIPS, d),
                        lambda i, j, k: (i, 0, 0), "wgrad_out")
    g_up = _wgrad_call(c, dz, d, dff // N_CHIPS, 512, (N_CHIPS, d, dff // N_CHIPS), (None, d, dff // N_CHIPS),
                       lambda i, j, k: (j, 0, 0), "wgrad_up")
    g_down = _wgrad_call(ff, dy, dff // N_CHIPS, d, 512, (N_CHIPS, dff // N_CHIPS, d), (None, dff // N_CHIPS, d),
                         lambda i, j, k: (i, 0, 0), "wgrad_down")
    drb = _rel_bias_grad_call(jnp.stack(db_ps), jnp.asarray(_bucket_tables()))
    small = dict(
        mix_norm_g=dg1.reshape(d), mlp_norm_g=dg2.reshape(d), pool_scale=dps.reshape(POOL_WIDTH),
        q_norm_g=dqg.reshape(ATTN_WIDTH), k_norm_g=dkg.reshape(ATTN_WIDTH),
        rel_bias=drb[:, :N_BUCKETS].T, pool_w=dpw)
    return loss[0, 0], dx, (g_in, g_out, g_up, g_down), small


def _coords():
    return lax.axis_index("x"), lax.axis_index("y"), lax.axis_index("c")


def _other_chips(x, y):
    return [(1 - x, y), (x, 1 - y), (1 - x, 1 - y)]


def _remote(src, dst, send_sem, recv_sem, dev):
    return pltpu.make_async_remote_copy(src_ref=src, dst_ref=dst, send_sem=send_sem, recv_sem=recv_sem,
                                        device_id=dev, device_id_type=MESH)


def _allgather_call(shards):
    nw = len(shards)
    ncp = 3 * nw

    def body(*refs):
        ins, outs = refs[:nw], refs[nw:2 * nw]
        send1, recv1, send2, recv2, loc = refs[2 * nw:]
        x, y, c = _coords()
        chip = 2 * x + y
        others = _other_chips(x, y)
        local = [pltpu.make_async_copy(ins[w], outs[w].at[chip], loc.at[w]) for w in range(nw)]
        for cp in local:
            cp.start()
        first, passed = [], []
        for w in range(nw):
            rh = shards[w].shape[0] // 2
            for k, (ox, oy) in enumerate(others):
                cp = _remote(ins[w].at[pl.ds(c * rh, rh)], outs[w].at[chip, pl.ds(c * rh, rh)],
                             send1.at[3 * w + k], recv1.at[3 * w + k], (ox, oy, c))
                cp.start()
                first.append(cp)
        for w in range(nw):
            rh = shards[w].shape[0] // 2
            for k, (ox, oy) in enumerate(others):
                piece = outs[w].at[2 * ox + oy, pl.ds(c * rh, rh)]
                _remote(piece, piece, send1.at[3 * w + k], recv1.at[3 * w + k], (ox, oy, c)).wait_recv()
                cp = _remote(piece, piece, send2.at[3 * w + k], recv2.at[3 * w + k], (x, y, 1 - c))
                cp.start()
                passed.append(cp)
        for w in range(nw):
            rh = shards[w].shape[0] // 2
            for k, (ox, oy) in enumerate(others):
                piece = outs[w].at[2 * ox + oy, pl.ds((1 - c) * rh, rh)]
                _remote(piece, piece, send2.at[3 * w + k], recv2.at[3 * w + k], (x, y, 1 - c)).wait_recv()
        for cp in first + passed:
            cp.wait_send()
        for cp in local:
            cp.wait()

    return pl.pallas_call(
        body, name="weights_allgather",
        in_specs=[ANY] * nw, out_specs=[ANY] * nw,
        out_shape=[jax.ShapeDtypeStruct((N_CHIPS,) + s.shape, s.dtype) for s in shards],
        scratch_shapes=[pltpu.SemaphoreType.DMA((ncp,))] * 4 + [pltpu.SemaphoreType.DMA((nw,))],
    )(*shards)


def _pair_exchange_call(grads, small):
    nw = len(grads)

    def body(*refs):
        ins, small_ref = refs[:nw], refs[nw]
        outs, gathered = refs[nw + 1:2 * nw + 1], refs[2 * nw + 1]
        send, recv, ssend, srecv, loc = refs[2 * nw + 2:]
        x, y, c = _coords()
        me = 4 * x + 2 * y + c
        own = pltpu.make_async_copy(small_ref, gathered.at[me], loc)
        own.start()
        sent = []
        for w in range(nw):
            rh = grads[w].shape[1] // 2
            cp = _remote(ins[w].at[:, pl.ds((1 - c) * rh, rh), :], outs[w], send.at[w], recv.at[w], (x, y, 1 - c))
            cp.start()
            sent.append(cp)
        flips = [(r >> 2 & 1, r >> 1 & 1, r & 1) for r in range(1, N_DEV)]
        peer = lambda f: ((1 - x) if f[0] else x, (1 - y) if f[1] else y, (1 - c) if f[2] else c)
        for r, f in enumerate(flips):
            cp = _remote(small_ref, gathered.at[me], ssend.at[r], srecv.at[r], peer(f))
            cp.start()
            sent.append(cp)
        for w in range(nw):
            _remote(outs[w], outs[w], send.at[w], recv.at[w], (x, y, 1 - c)).wait_recv()
        for r, f in enumerate(flips):
            px, py, pc = peer(f)
            slot = gathered.at[4 * px + 2 * py + pc]
            _remote(slot, slot, ssend.at[r], srecv.at[r], (px, py, pc)).wait_recv()
        for cp in sent:
            cp.wait_send()
        own.wait()

    return pl.pallas_call(
        body, name="grads_pair_exchange",
        in_specs=[ANY] * (nw + 1), out_specs=[ANY] * (nw + 1),
        out_shape=[jax.ShapeDtypeStruct((g.shape[0], g.shape[1] // 2, g.shape[2]), g.dtype) for g in grads]
        + [jax.ShapeDtypeStruct((N_DEV,) + small.shape, small.dtype)],
        scratch_shapes=[pltpu.SemaphoreType.DMA((nw,)), pltpu.SemaphoreType.DMA((nw,)),
                        pltpu.SemaphoreType.DMA((N_DEV - 1,)), pltpu.SemaphoreType.DMA((N_DEV - 1,)),
                        pltpu.SemaphoreType.DMA],
    )(*grads, small)


def _pair_sum_call(grads, recvd, c_idx, nch):
    nw = len(grads)

    def body(c_ref, *refs):
        for w in range(nw):
            tot = refs[w][...] + refs[nw + w][...]
            refs[2 * nw + w][...] = tot
            refs[3 * nw + w][...] = tot.astype(WIRE_DTYPE)

    in_specs, out_specs, out_shape = [], [], []
    for g in grads:
        rb = g.shape[1] // 2 // nch
        in_specs.append(pl.BlockSpec((None, rb, g.shape[2]), lambda j, i, c_ref: (j, c_ref[0] * nch + i, 0)))
    for g in grads:
        rb = g.shape[1] // 2 // nch
        in_specs.append(pl.BlockSpec((None, rb, g.shape[2]), lambda j, i, c_ref: (j, i, 0)))
    for dt in (F32, WIRE_DTYPE):
        for g in grads:
            rb = g.shape[1] // 2 // nch
            out_specs.append(pl.BlockSpec((None, rb, g.shape[2]), lambda j, i, c_ref: (j, i, 0)))
            out_shape.append(jax.ShapeDtypeStruct((g.shape[0], g.shape[1] // 2, g.shape[2]), dt))
    res = pl.pallas_call(
        body, name="grads_pair_sum",
        grid_spec=pltpu.PrefetchScalarGridSpec(num_scalar_prefetch=1, grid=(N_CHIPS, nch),
                                               in_specs=in_specs, out_specs=out_specs),
        out_shape=out_shape,
        compiler_params=_params(("arbitrary", "arbitrary")),
    )(c_idx, *grads, *recvd)
    return res[:nw], res[nw:]


def _chip_exchange_call(sums):
    nw = len(sums)

    def body(*refs):
        ins, outs = refs[:nw], refs[nw:2 * nw]
        send, recv = refs[2 * nw:]
        x, y, c = _coords()
        others = _other_chips(x, y)
        sent = []
        for w in range(nw):
            for k, (ox, oy) in enumerate(others):
                cp = _remote(ins[w].at[2 * ox + oy], outs[w].at[k], send.at[3 * w + k], recv.at[3 * w + k], (ox, oy, c))
                cp.start()
                sent.append(cp)
        for w in range(nw):
            for k, (ox, oy) in enumerate(others):
                _remote(outs[w].at[k], outs[w].at[k], send.at[3 * w + k], recv.at[3 * w + k], (ox, oy, c)).wait_recv()
        for cp in sent:
            cp.wait_send()

    return pl.pallas_call(
        body, name="grads_chip_exchange",
        in_specs=[ANY] * nw, out_specs=[ANY] * nw,
        out_shape=[jax.ShapeDtypeStruct((3,) + s.shape[1:], s.dtype) for s in sums],
        scratch_shapes=[pltpu.SemaphoreType.DMA((3 * nw,)), pltpu.SemaphoreType.DMA((3 * nw,))],
    )(*sums)


def _chip_sum_call(own, recvd, chip_idx, nch):
    nw = len(own)

    def body(chip_ref, *refs):
        for w in range(nw):
            q = refs[nw + w]
            refs[2 * nw + w][...] = ((refs[w][...] + q[0].astype(F32)) + q[1].astype(F32)) + q[2].astype(F32)

    in_specs, out_specs, out_shape = [], [], []
    for s in own:
        rb = s.shape[1] // nch
        in_specs.append(pl.BlockSpec((None, rb, s.shape[2]), lambda i, chip_ref: (chip_ref[0], i, 0)))
    for s in own:
        rb = s.shape[1] // nch
        in_specs.append(pl.BlockSpec((3, rb, s.shape[2]), lambda i, chip_ref: (0, i, 0)))
    for s in own:
        rb = s.shape[1] // nch
        out_specs.append(pl.BlockSpec((rb, s.shape[2]), lambda i, chip_ref: (i, 0)))
        out_shape.append(jax.ShapeDtypeStruct(s.shape[1:], F32))
    return pl.pallas_call(
        body, name="grads_chip_sum",
        grid_spec=pltpu.PrefetchScalarGridSpec(num_scalar_prefetch=1, grid=(nch,),
                                               in_specs=in_specs, out_specs=out_specs),
        out_shape=out_shape,
        compiler_params=_params(("arbitrary",)),
    )(chip_idx, *own, *recvd)


def _pair_allgather_call(halves):
    nw = len(halves)

    def body(*refs):
        ins, outs = refs[:nw], refs[nw:2 * nw]
        send, recv, loc = refs[2 * nw:]
        x, y, c = _coords()
        cps = []
        for w in range(nw):
            rh = halves[w].shape[0]
            lc = pltpu.make_async_copy(ins[w], outs[w].at[pl.ds(c * rh, rh)], loc.at[w])
            lc.start()
            cp = _remote(ins[w], outs[w].at[pl.ds(c * rh, rh)], send.at[w], recv.at[w], (x, y, 1 - c))
            cp.start()
            cps.append((lc, cp))
        for w in range(nw):
            rh = halves[w].shape[0]
            theirs = outs[w].at[pl.ds((1 - c) * rh, rh)]
            _remote(theirs, theirs, send.at[w], recv.at[w], (x, y, 1 - c)).wait_recv()
        for lc, cp in cps:
            cp.wait_send()
            lc.wait()

    return pl.pallas_call(
        body, name="grads_pair_allgather",
        in_specs=[ANY] * nw, out_specs=[ANY] * nw,
        out_shape=[jax.ShapeDtypeStruct((2 * h.shape[0], h.shape[1]), h.dtype) for h in halves],
        scratch_shapes=[pltpu.SemaphoreType.DMA((nw,))] * 3,
    )(*halves)


def _adamw(w, g, m, v):
    m = ADAM_B1 * m + (1.0 - ADAM_B1) * g
    v = ADAM_B2 * v + (1.0 - ADAM_B2) * (g * g)
    m_hat = m / (1.0 - ADAM_B1 ** ADAM_STEP)
    v_hat = v / (1.0 - ADAM_B2 ** ADAM_STEP)
    delta = -ADAM_LR * (m_hat / (jnp.sqrt(v_hat) + ADAM_EPS) + ADAM_WD * w)
    return delta, m, v


def _adamw_call(ws, gs, ms, vs, nch):
    nw = len(ws)

    def body(*refs):
        for w in range(nw):
            delta, m, v = _adamw(refs[w][...], refs[nw + w][...], refs[2 * nw + w][...], refs[3 * nw + w][...])
            refs[4 * nw + w][...] = delta
            refs[5 * nw + w][...] = m
            refs[6 * nw + w][...] = v

    specs = [pl.BlockSpec((a.shape[0] // nch, a.shape[1]), lambda i: (i, 0)) for a in ws]
    res = pl.pallas_call(
        body, name="adamw_big",
        grid=(nch,),
        in_specs=specs * 4, out_specs=specs * 3,
        out_shape=[jax.ShapeDtypeStruct(a.shape, F32) for a in ws] * 3,
        compiler_params=_params(("arbitrary",)),
    )(*ws, *gs, *ms, *vs)
    return res[:nw], res[nw:2 * nw], res[2 * nw:]


def _small_call(gathered, w, m, v):
    def fold(row):
        tot = row[:, 0:LANES] + row[:, LANES:2 * LANES] + row[:, 2 * LANES:3 * LANES] + row[:, 3 * LANES:4 * LANES]
        return tot + pltpu.roll(tot, HEAD_DIM, axis=1)

    def body(ga_ref, w_ref, m_ref, v_ref, g_out, d_out, m_out, v_out):
        g = ga_ref[0]
        for i in range(1, N_DEV):
            g = g + ga_ref[i]
        unfolded = g[4:5, :]
        folded = jnp.concatenate([fold(unfolded[:, :ATTN_WIDTH]), fold(unfolded[:, ATTN_WIDTH:]),
                                  jnp.zeros((1, 1024 - 2 * LANES), F32)], axis=-1)
        row = lax.broadcasted_iota(jnp.int32, g.shape, 0)
        g = jnp.where(row == 3, folded, g)
        delta, mm, vv = _adamw(w_ref[...], g, m_ref[...], v_ref[...])
        g_out[...] = g
        d_out[...] = delta
        m_out[...] = mm
        v_out[...] = vv

    return pl.pallas_call(
        body, name="adamw_small",
        out_shape=[jax.ShapeDtypeStruct(w.shape, F32)] * 4,
        compiler_params=_params(),
    )(gathered, w, m, v)


_SMALL_NAMES = ("mix_norm_g", "pool_w", "pool_scale", "q_norm_g", "k_norm_g", "rel_bias", "mlp_norm_g")


def _pack_small(p, folded=True):
    z = lambda n: jnp.zeros((n,), F32)
    rows = [p["mix_norm_g"], p["mlp_norm_g"],
            jnp.concatenate([p["pool_scale"], p["rel_bias"].reshape(-1), z(1024 - POOL_WIDTH - N_BUCKETS * N_HEADS)])]
    if folded:
        rows += [jnp.concatenate([p["q_norm_g"], z(LANES - HEAD_DIM), p["k_norm_g"], z(1024 - LANES - HEAD_DIM)]), z(1024)]
    else:
        rows += [z(1024), jnp.concatenate([p["q_norm_g"], p["k_norm_g"]])]
    head = jnp.stack(rows + [z(1024)] * 3)
    return jnp.concatenate([head, p["pool_w"].reshape(-1, 1024)], axis=0)


def _unpack_small(a):
    return dict(
        mix_norm_g=a[0], mlp_norm_g=a[1], pool_scale=a[2, :POOL_WIDTH],
        rel_bias=a[2, POOL_WIDTH:POOL_WIDTH + N_BUCKETS * N_HEADS].reshape(N_BUCKETS, N_HEADS),
        q_norm_g=a[3, :HEAD_DIM], k_norm_g=a[3, LANES:LANES + HEAD_DIM],
        pool_w=a[8:].reshape(len(POOL_WINDOWS), LANES, LANES))


_WEIGHT_ORDER = ("mix_norm_g", "w_in", "pool_w", "pool_scale", "q_norm_g", "k_norm_g", "rel_bias", "w_out",
                 "mlp_norm_g", "w_up", "w_down")
_BIG = ("w_in", "w_out", "w_up", "w_down")


def kernel(x, mix_norm_g, w_in, pool_w, pool_scale, q_norm_g, k_norm_g, rel_bias, w_out, mlp_norm_g, w_up, w_down, loss_target, m_mix_norm_g, m_w_in, m_pool_w, m_pool_scale, m_q_norm_g, m_k_norm_g, m_rel_bias, m_w_out, m_mlp_norm_g, m_w_up, m_w_down, v_mix_norm_g, v_w_in, v_pool_w, v_pool_scale, v_q_norm_g, v_k_norm_g, v_rel_bias, v_w_out, v_mlp_norm_g, v_w_up, v_w_down):
    w = dict(mix_norm_g=mix_norm_g, w_in=w_in, pool_w=pool_w, pool_scale=pool_scale, q_norm_g=q_norm_g,
             k_norm_g=k_norm_g, rel_bias=rel_bias, w_out=w_out, mlp_norm_g=mlp_norm_g, w_up=w_up, w_down=w_down)
    m = dict(mix_norm_g=m_mix_norm_g, w_in=m_w_in, pool_w=m_pool_w, pool_scale=m_pool_scale, q_norm_g=m_q_norm_g,
             k_norm_g=m_k_norm_g, rel_bias=m_rel_bias, w_out=m_w_out, mlp_norm_g=m_mlp_norm_g, w_up=m_w_up, w_down=m_w_down)
    v = dict(mix_norm_g=v_mix_norm_g, w_in=v_w_in, pool_w=v_pool_w, pool_scale=v_pool_scale, q_norm_g=v_q_norm_g,
             k_norm_g=v_k_norm_g, rel_bias=v_rel_bias, w_out=v_w_out, mlp_norm_g=v_mlp_norm_g, w_up=v_w_up, w_down=v_w_down)
    xc, yc, cc = _coords()

    win_f, wout_f, wup_f, wdown_f = _allgather_call([w[n].astype(WIRE_DTYPE) for n in _BIG])
    loss_part, dx, big_grads, small_grads = _local_grads(
        x[0], loss_target[0], mix_norm_g, win_f, pool_w, pool_scale, q_norm_g, k_norm_g, rel_bias,
        wout_f.reshape(wout_f.shape[0] * wout_f.shape[1], wout_f.shape[2]), mlp_norm_g, wup_f, wdown_f)

    *from_sibling, small_all = _pair_exchange_call(list(big_grads), _pack_small(small_grads, folded=False))
    c_idx = jnp.reshape(cc, (1,)).astype(jnp.int32)
    chip_idx = jnp.reshape(2 * xc + yc, (1,)).astype(jnp.int32)
    sums32, sums_wire = _pair_sum_call(list(big_grads), from_sibling, c_idx, nch=2)
    from_chips = _chip_exchange_call(list(sums_wire))
    halves = _chip_sum_call(list(sums32), from_chips, chip_idx, nch=2)
    g_big = _pair_allgather_call(list(halves))
    d_big, m_big, v_big = _adamw_call([w[n] for n in _BIG], list(g_big), [m[n] for n in _BIG], [v[n] for n in _BIG], nch=8)
    g_s, d_s, m_s, v_s = (_unpack_small(a) for a in _small_call(small_all, _pack_small(w), _pack_small(m), _pack_small(v)))

    grads, deltas, new_m, new_v = dict(g_s), dict(d_s), dict(m_s), dict(v_s)
    for i, n in enumerate(_BIG):
        grads[n], deltas[n], new_m[n], new_v[n] = g_big[i], d_big[i], m_big[i], v_big[i]
    loss = lax.psum(loss_part, ("x", "y", "c"))
    return (loss, dx[None], *[grads[n] for n in _WEIGHT_ORDER], *[deltas[n] for n in _WEIGHT_ORDER],
            *[new_m[n] for n in _WEIGHT_ORDER], *[new_v[n] for n in _WEIGHT_ORDER])
```

```python
import math

import jax
import jax.numpy as jnp
import numpy as np
from jax import lax
from jax.experimental import pallas as pl
from jax.experimental.pallas import tpu as pltpu

F32 = jnp.float32
MXU_DTYPE = jnp.bfloat16
WIRE_DTYPE = jnp.bfloat16

NORM_EPS = 1e-6
NEG_INF = -1e30
LANES = 128
HEAD_DIM = 64
N_HEADS = 8
POOL_WIDTH = 512
ATTN_WIDTH = 512
POOL_WINDOWS = (2, 4, 8, 16)
POOL_HALO = 16
DILATED_PATTERNS = ((128, 1), (512, 4), (2048, 16))
ATT_BLOCK = 128
ATT_SUPER = ATT_BLOCK * max(dl for _, dl in DILATED_PATTERNS)
ATT_UNITS = ATT_SUPER // ATT_BLOCK
N_BUCKETS = 32
MAX_DISTANCE = 2048
N_CHIPS = 4
N_DEV = 8
ADAM_LR, ADAM_B1, ADAM_B2, ADAM_EPS, ADAM_WD, ADAM_STEP = 0.001, 0.9, 0.999, 1e-08, 0.01, 10
VMEM_LIMIT = 56 * 1024 * 1024
MESH = pl.DeviceIdType.MESH
ANY = pl.BlockSpec(memory_space=pl.ANY)

SMALL_ROWS = 72
LOSS_ROW = 5


def _mm(a, b):
    return jnp.dot(a, b, preferred_element_type=F32)


def _mm_nt(a, b):
    return lax.dot_general(a, b, (((1,), (1,)), ((), ())), preferred_element_type=F32)


def _mm_tn(a, b):
    return lax.dot_general(a, b, (((0,), (0,)), ((), ())), preferred_element_type=F32)


def _params(sem=None, **kw):
    if sem is not None:
        kw["dimension_semantics"] = sem
    return pltpu.CompilerParams(vmem_limit_bytes=VMEM_LIMIT, **kw)


def _low_half():
    return lax.broadcasted_iota(jnp.int32, (1, LANES), 1) < HEAD_DIM


def _head_sum_bcast(y):
    lo = _low_half()
    outs = []
    for j in range(y.shape[1] // LANES):
        c = y[:, j * LANES:(j + 1) * LANES]
        s_lo = jnp.sum(jnp.where(lo, c, 0.0), axis=-1, keepdims=True)
        s_hi = jnp.sum(jnp.where(lo, 0.0, c), axis=-1, keepdims=True)
        outs.append(jnp.where(lo, s_lo, s_hi))
    return jnp.concatenate(outs, axis=-1)


def _rms_bwd(dn, hn, r):
    return r * (dn - hn * jnp.mean(dn * hn, axis=-1, keepdims=True))


def _t5_bucket_np(dist):
    max_exact = N_BUCKETS // 2
    d_f = np.maximum(dist, 1).astype(np.float32)
    ratio = (np.log(d_f / np.float32(max_exact)) / np.float32(math.log(MAX_DISTANCE / max_exact))).astype(np.float32)
    large = max_exact + (ratio * np.float32(N_BUCKETS - max_exact)).astype(np.int32)
    large = np.minimum(large, N_BUCKETS - 1)
    return np.where(dist < max_exact, dist, large).astype(np.int32)


def _bucket_tables():
    qq = np.arange(ATT_BLOCK)[:, None]
    kk = np.arange(2 * ATT_BLOCK)[None, :]
    dist = np.clip(qq + ATT_BLOCK - kk, 0, ATT_BLOCK)
    return np.stack([_t5_bucket_np(dist * dl) for (_, dl) in DILATED_PATTERNS])


def _f1_call(x, g1, win, poolw, pscale, qg, kg, tm):
    s, d = x.shape
    nblk = s // tm

    def body(x_ref, g1_ref, win_ref, pw_ref, ps_ref, qg_ref, kg_ref,
             a_ref, pooled_ref, ypool_ref, q32_ref, k32_ref, qn_ref, kn_ref, v_ref, ubuf):
        i = pl.program_id(0)
        xv = x_ref[...]
        r = lax.rsqrt(jnp.mean(xv * xv, axis=-1, keepdims=True) + NORM_EPS)
        a = ((xv * r) * g1_ref[...]).astype(MXU_DTYPE)
        a_ref[...] = a
        u = _mm(a, win_ref[0])
        q = _mm(a, win_ref[1])
        k = _mm(a, win_ref[2])
        v_ref[...] = _mm(a, win_ref[3]).astype(MXU_DTYPE)
        q32_ref[...] = q
        k32_ref[...] = k
        rq = lax.rsqrt(_head_sum_bcast(q * q) * (1.0 / HEAD_DIM) + NORM_EPS)
        qn_ref[...] = (((q * rq) * qg_ref[...]) * (HEAD_DIM ** -0.5)).astype(MXU_DTYPE)
        rk = lax.rsqrt(_head_sum_bcast(k * k) * (1.0 / HEAD_DIM) + NORM_EPS)
        kn_ref[...] = ((k * rk) * kg_ref[...]).astype(MXU_DTYPE)

        @pl.when(i == 0)
        def _():
            ubuf[0:POOL_HALO, :] = jnp.zeros((POOL_HALO, POOL_WIDTH), F32)

        @pl.when(i > 0)
        def _():
            ubuf[0:POOL_HALO, :] = ubuf[tm:tm + POOL_HALO, :]

        ubuf[POOL_HALO:POOL_HALO + tm, :] = u
        t = i * tm + lax.broadcasted_iota(jnp.int32, (tm, 1), 0)
        for g, w in enumerate(POOL_WINDOWS):
            ls = slice(g * LANES, (g + 1) * LANES)
            ug = u[:, ls]
            acc = ug
            for sh in range(1, w):
                acc = acc + ubuf[POOL_HALO - sh:POOL_HALO - sh + tm, ls]
            cnt = jnp.minimum(t + 1, w).astype(F32)
            pooled = (acc / cnt - ug).astype(MXU_DTYPE)
            pooled_ref[:, ls] = pooled
            ypool_ref[:, ls] = (_mm(pooled, pw_ref[g]) * ps_ref[:, ls]).astype(MXU_DTYPE)

    tok = lambda w: pl.BlockSpec((tm, w), lambda i: (i, 0))
    full = lambda shp: pl.BlockSpec(shp, lambda i: (0,) * len(shp))
    return pl.pallas_call(
        body, name="fwd_inproj",
        grid=(nblk,),
        in_specs=[tok(d), full((1, d)), full(win.shape), full(poolw.shape), full((1, POOL_WIDTH)),
                  full((1, ATTN_WIDTH)), full((1, ATTN_WIDTH))],
        out_specs=[tok(d), tok(POOL_WIDTH), tok(POOL_WIDTH), tok(ATTN_WIDTH), tok(ATTN_WIDTH),
                   tok(ATTN_WIDTH), tok(ATTN_WIDTH), tok(ATTN_WIDTH)],
        out_shape=[jax.ShapeDtypeStruct((s, d), MXU_DTYPE),
                   jax.ShapeDtypeStruct((s, POOL_WIDTH), MXU_DTYPE),
                   jax.ShapeDtypeStruct((s, POOL_WIDTH), MXU_DTYPE),
                   jax.ShapeDtypeStruct((s, ATTN_WIDTH), F32),
                   jax.ShapeDtypeStruct((s, ATTN_WIDTH), F32),
                   jax.ShapeDtypeStruct((s, ATTN_WIDTH), MXU_DTYPE),
                   jax.ShapeDtypeStruct((s, ATTN_WIDTH), MXU_DTYPE),
                   jax.ShapeDtypeStruct((s, ATTN_WIDTH), MXU_DTYPE)],
        scratch_shapes=[pltpu.VMEM((tm + POOL_HALO, POOL_WIDTH), F32)],
        compiler_params=_params(("arbitrary",)),
    )(x, g1, win, poolw, pscale, qg, kg)


def _band_mask(n):
    qq = lax.broadcasted_iota(jnp.int32, (ATT_BLOCK, 2 * ATT_BLOCK), 0)
    kk = lax.broadcasted_iota(jnp.int32, (ATT_BLOCK, 2 * ATT_BLOCK), 1)
    dist = qq + ATT_BLOCK - kk
    return (dist >= 0) & (dist <= ATT_BLOCK) & ((n > 0) | (kk >= ATT_BLOCK))


def _unit_rows(u, dl):
    r = u % dl
    b = u // dl
    q0 = r + dl * ATT_BLOCK * b
    return b, pl.ds(q0, ATT_BLOCK, stride=dl), pl.ds(ATT_SUPER + q0 - dl * ATT_BLOCK, 2 * ATT_BLOCK, stride=dl)


def _attn_fwd_call(qn, kn, v, bias):
    s, w = qn.shape
    nsb = s // ATT_SUPER
    npair = w // LANES

    def body(q_ref, kc_ref, kp_ref, vc_ref, vp_ref, b_ref, o_ref, lse_ref, qf, kf, vf, acc_s, m_s, l_s):
        sb = pl.program_id(1)
        qf[...] = q_ref[...].astype(F32)
        kf[0:ATT_SUPER, :] = kp_ref[...].astype(F32)
        kf[ATT_SUPER:, :] = kc_ref[...].astype(F32)
        vf[0:ATT_SUPER, :] = vp_ref[...].astype(F32)
        vf[ATT_SUPER:, :] = vc_ref[...].astype(F32)
        lo = _low_half()
        for p, (_, dl) in enumerate(DILATED_PATTERNS):
            def unit(u, carry, p=p, dl=dl):
                b, rows_q, rows_k = _unit_rows(u, dl)
                valid = _band_mask(sb * (ATT_UNITS // dl) + b)
                qp = qf[rows_q, :].astype(MXU_DTYPE)
                kcat = kf[rows_k, :].astype(MXU_DTYPE)
                vcat = vf[rows_k, :].astype(MXU_DTYPE)
                acc_h, m_h, l_h = [], [], []
                for e in range(2):
                    sel = lo if e == 0 else jnp.logical_not(lo)
                    qm = jnp.where(sel, qp, jnp.zeros_like(qp))
                    sc = _mm_nt(qm, kcat) + b_ref[p, e]
                    sc = jnp.where(valid, sc, NEG_INF)
                    m = jnp.max(sc, axis=-1, keepdims=True)
                    pr = jnp.exp(sc - m)
                    l_h.append(jnp.sum(pr, axis=-1, keepdims=True))
                    m_h.append(m)
                    acc_h.append(_mm(pr.astype(MXU_DTYPE), vcat))
                acc = jnp.where(lo, acc_h[0], acc_h[1])
                m = jnp.where(lo, m_h[0], m_h[1])
                l = jnp.where(lo, l_h[0], l_h[1])
                if p == 0:
                    acc_s[rows_q, :] = acc
                    m_s[rows_q, :] = m
                    l_s[rows_q, :] = l
                else:
                    m_old = m_s[rows_q, :]
                    m_new = jnp.maximum(m_old, m)
                    a_old = jnp.exp(m_old - m_new)
                    a_new = jnp.exp(m - m_new)
                    acc_s[rows_q, :] = a_old * acc_s[rows_q, :] + a_new * acc
                    l_s[rows_q, :] = a_old * l_s[rows_q, :] + a_new * l
                    m_s[rows_q, :] = m_new
                return carry

            lax.fori_loop(0, ATT_UNITS, unit, 0)
        l = l_s[...]
        o_ref[...] = acc_s[...] / l
        lse_ref[...] = m_s[...] + jnp.log(l)

    cur = pl.BlockSpec((ATT_SUPER, LANES), lambda j, t: (t, j))
    prev = pl.BlockSpec((ATT_SUPER, LANES), lambda j, t: (jnp.maximum(t - 1, 0), j))
    bspec = pl.BlockSpec((len(DILATED_PATTERNS), 2, ATT_BLOCK, 2 * ATT_BLOCK), lambda j, t: (0, j, 0, 0))
    return pl.pallas_call(
        body, name="attn_fwd",
        grid=(npair, nsb),
        in_specs=[cur, cur, prev, cur, prev, bspec],
        out_specs=[cur, cur],
        out_shape=[jax.ShapeDtypeStruct((s, w), F32), jax.ShapeDtypeStruct((s, w), F32)],
        scratch_shapes=[pltpu.VMEM((ATT_SUPER, LANES), F32), pltpu.VMEM((2 * ATT_SUPER, LANES), F32),
                        pltpu.VMEM((2 * ATT_SUPER, LANES), F32), pltpu.VMEM((ATT_SUPER, LANES), F32),
                        pltpu.VMEM((ATT_SUPER, LANES), F32), pltpu.VMEM((ATT_SUPER, LANES), F32)],
        compiler_params=_params(("arbitrary", "arbitrary")),
    )(qn, kn, kn, v, v, bias)


def _attn_bwd_call(qn, kn, v, do, lse, delta, bias):
    s, w = qn.shape
    nsb = s // ATT_SUPER
    npair = w // LANES

    def body(q_ref, kc_ref, kp_ref, vc_ref, vp_ref, do_ref, lse_ref, dlt_ref, b_ref,
             dq_ref, dk_ref, dv_ref, db_ref, qf, kf, vf, dof, dkf, dvf):
        step = pl.program_id(1)
        sb = nsb - 1 - step
        qf[...] = q_ref[...].astype(F32)
        dof[...] = do_ref[...].astype(F32)
        kf[0:ATT_SUPER, :] = kp_ref[...].astype(F32)
        kf[ATT_SUPER:, :] = kc_ref[...].astype(F32)
        vf[0:ATT_SUPER, :] = vp_ref[...].astype(F32)
        vf[ATT_SUPER:, :] = vc_ref[...].astype(F32)

        @pl.when(step == 0)
        def _():
            db_ref[...] = jnp.zeros(db_ref.shape, F32)
            dkf[ATT_SUPER:, :] = jnp.zeros((ATT_SUPER, LANES), F32)
            dvf[ATT_SUPER:, :] = jnp.zeros((ATT_SUPER, LANES), F32)

        @pl.when(step > 0)
        def _():
            dkf[ATT_SUPER:, :] = dkf[0:ATT_SUPER, :]
            dvf[ATT_SUPER:, :] = dvf[0:ATT_SUPER, :]

        dkf[0:ATT_SUPER, :] = jnp.zeros((ATT_SUPER, LANES), F32)
        dvf[0:ATT_SUPER, :] = jnp.zeros((ATT_SUPER, LANES), F32)
        lo = _low_half()
        for p, (_, dl) in enumerate(DILATED_PATTERNS):
            def unit(u, carry, p=p, dl=dl):
                b, rows_q, rows_k = _unit_rows(u, dl)
                valid = _band_mask(sb * (ATT_UNITS // dl) + b)
                qp = qf[rows_q, :].astype(MXU_DTYPE)
                dop = dof[rows_q, :].astype(MXU_DTYPE)
                kcat = kf[rows_k, :].astype(MXU_DTYPE)
                vcat = vf[rows_k, :].astype(MXU_DTYPE)
                lse2 = lse_ref[rows_q, :]
                dlt2 = dlt_ref[rows_q, :]
                dq_h = []
                dk = jnp.zeros((2 * ATT_BLOCK, LANES), F32)
                dv = jnp.zeros((2 * ATT_BLOCK, LANES), F32)
                for e in range(2):
                    sel = lo if e == 0 else jnp.logical_not(lo)
                    col = e * HEAD_DIM
                    qm = jnp.where(sel, qp, jnp.zeros_like(qp))
                    dom = jnp.where(sel, dop, jnp.zeros_like(dop))
                    sc = _mm_nt(qm, kcat) + b_ref[p, e]
                    pr = jnp.where(valid, jnp.exp(sc - lse2[:, col:col + 1]), 0.0)
                    dp = _mm_nt(dom, vcat)
                    ds = pr * (dp - dlt2[:, col:col + 1])
                    db_ref[p, e] += ds
                    ds_c = ds.astype(MXU_DTYPE)
                    dq_h.append(_mm(ds_c, kcat))
                    dk = dk + _mm_tn(ds_c, qm)
                    dv = dv + _mm_tn(pr.astype(MXU_DTYPE), dom)
                dq = jnp.where(lo, dq_h[0], dq_h[1])
                if p == 0:
                    dq_ref[rows_q, :] = dq
                else:
                    dq_ref[rows_q, :] += dq
                dkf[rows_k, :] += dk
                dvf[rows_k, :] += dv
                return carry

            lax.fori_loop(0, ATT_UNITS, unit, 0)
        dk_ref[...] = dkf[ATT_SUPER:, :]
        dv_ref[...] = dvf[ATT_SUPER:, :]

    cur = pl.BlockSpec((ATT_SUPER, LANES), lambda j, t: (nsb - 1 - t, j))
    prev = pl.BlockSpec((ATT_SUPER, LANES), lambda j, t: (jnp.maximum(nsb - 2 - t, 0), j))
    bshape = (len(DILATED_PATTERNS), 2, ATT_BLOCK, 2 * ATT_BLOCK)
    bspec = pl.BlockSpec(bshape, lambda j, t: (0, j, 0, 0))
    sup = lambda: pltpu.VMEM((ATT_SUPER, LANES), F32)
    sup2 = lambda: pltpu.VMEM((2 * ATT_SUPER, LANES), F32)
    return pl.pallas_call(
        body, name="attn_bwd",
        grid=(npair, nsb),
        in_specs=[cur, cur, prev, cur, prev, cur, cur, cur, bspec],
        out_specs=[cur, cur, cur, bspec],
        out_shape=[jax.ShapeDtypeStruct((s, w), F32)] * 3 + [jax.ShapeDtypeStruct(bias.shape, F32)],
        scratch_shapes=[sup(), sup2(), sup2(), sup(), sup2(), sup2()],
        compiler_params=_params(("arbitrary", "arbitrary")),
    )(qn, kn, kn, v, v, do, lse, delta, bias)


def _bias_table_call(rel_bias, buckets):
    npat = buckets.shape[0]

    def body(rb_ref, bk_ref, out_ref):
        for p in range(npat):
            for half in range(2):
                ks = slice(half * ATT_BLOCK, (half + 1) * ATT_BLOCK)
                bk = bk_ref[p, :, ks]
                for h in range(N_HEADS):
                    def pick(b, acc, h=h, bk=bk):
                        return jnp.where(bk == b, rb_ref[b, h], acc)

                    out_ref[p, h, :, ks] = lax.fori_loop(0, N_BUCKETS, pick, jnp.zeros((ATT_BLOCK, ATT_BLOCK), F32))

    return pl.pallas_call(
        body, name="bias_table",
        in_specs=[pl.BlockSpec(memory_space=pltpu.SMEM), pl.BlockSpec(memory_space=pltpu.VMEM)],
        out_shape=jax.ShapeDtypeStruct((npat, N_HEADS, ATT_BLOCK, 2 * ATT_BLOCK), F32),
        compiler_params=_params(),
    )(rel_bias, buckets)


def _rel_bias_grad_call(dbias, buckets):
    npat, nh = dbias.shape[0], dbias.shape[1]

    def body(db_ref, bk_ref, out_ref):
        lane = lax.broadcasted_iota(jnp.int32, (nh, LANES), 1)
        out = jnp.zeros((nh, LANES), F32)
        for b in range(N_BUCKETS):
            tot = jnp.zeros((nh, 1), F32)
            for p in range(npat):
                hit = jnp.where(bk_ref[p][None] == b, db_ref[p], 0.0)
                tot = tot + jnp.sum(jnp.sum(hit, axis=2), axis=1, keepdims=True)
            out = jnp.where(lane == b, tot, out)
        out_ref[...] = out

    return pl.pallas_call(
        body, name="rel_bias_grad",
        out_shape=jax.ShapeDtypeStruct((nh, LANES), F32),
        compiler_params=_params(),
    )(dbias, buckets)


def _f2_call(x, tgt, ypool, o, wout, wup, wdown, g2, tm):
    s, d = x.shape
    nblk = s // tm
    nch, _, fch = wup.shape
    dff = nch * fch
    mixw = POOL_WIDTH + ATTN_WIDTH

    def body(x_ref, t_ref, yp_ref, o_ref, g2_ref, wout_hbm, wup_hbm, wdown_hbm,
             mixed_ref, c_ref, ff_ref, dz_ref, dy_ref, dh1_ref, dyp_ref, do_ref, dlt_ref, dg2_ref, loss_ref,
             wout_v, wup_v, wdown_v, rz):
        i = pl.program_id(0)

        @pl.when(i == 0)
        def _():
            pltpu.sync_copy(wout_hbm, wout_v)
            pltpu.sync_copy(wup_hbm, wup_v)
            pltpu.sync_copy(wdown_hbm, wdown_v)
            dg2_ref[...] = jnp.zeros(dg2_ref.shape, F32)
            loss_ref[...] = jnp.zeros(loss_ref.shape, F32)

        o = o_ref[...]
        mixed = jnp.concatenate([yp_ref[...], o.astype(MXU_DTYPE)], axis=-1)
        mixed_ref[...] = mixed
        h1 = x_ref[...] + _mm(mixed, wout_v[...])
        r2 = lax.rsqrt(jnp.mean(h1 * h1, axis=-1, keepdims=True) + NORM_EPS)
        hn = h1 * r2
        c = (hn * g2_ref[...]).astype(MXU_DTYPE)
        c_ref[...] = c
        y = h1
        for j in range(nch):
            cs = slice(j * fch, (j + 1) * fch)
            z = jnp.maximum(_mm(c, wup_v[j]), 0.0)
            rz[:, cs] = z
            ff = (z * z).astype(MXU_DTYPE)
            ff_ref[:, cs] = ff
            y = y + _mm(ff, wdown_v[j])
        err = y - t_ref[...]
        loss_ref[...] += jnp.sum(err * err) * (0.5 / d)
        dy = err * (1.0 / d)
        dy_c = dy.astype(MXU_DTYPE)
        dy_ref[...] = dy_c
        dc = jnp.zeros((tm, d), F32)
        for j in range(nch):
            cs = slice(j * fch, (j + 1) * fch)
            dz = (_mm_nt(dy_c, wdown_v[j]) * (2.0 * rz[:, cs])).astype(MXU_DTYPE)
            dz_ref[:, cs] = dz
            dc = dc + _mm_nt(dz, wup_v[j])
        dg2_ref[...] += jnp.sum(dc * hn, axis=0, keepdims=True)
        dh1 = dy + _rms_bwd(dc * g2_ref[...], hn, r2)
        dh1_ref[...] = dh1
        dmix = _mm_nt(dh1.astype(MXU_DTYPE), wout_v[...])
        dyp_ref[...] = dmix[:, :POOL_WIDTH]
        do = dmix[:, POOL_WIDTH:]
        do_ref[...] = do.astype(MXU_DTYPE)
        dlt_ref[...] = _head_sum_bcast(do * o)

    tok = lambda w: pl.BlockSpec((tm, w), lambda i: (i, 0))
    const = lambda shp: pl.BlockSpec(shp, lambda i: (0,) * len(shp))
    return pl.pallas_call(
        body, name="fwd_mlp_bwd_mlp",
        grid=(nblk,),
        in_specs=[tok(d), tok(d), tok(POOL_WIDTH), tok(ATTN_WIDTH), const((1, d)), ANY, ANY, ANY],
        out_specs=[tok(mixw), tok(d), tok(dff), tok(dff), tok(d), tok(d), tok(POOL_WIDTH), tok(ATTN_WIDTH),
                   tok(ATTN_WIDTH), const((1, d)), const((1, LANES))],
        out_shape=[jax.ShapeDtypeStruct((s, mixw), MXU_DTYPE),
                   jax.ShapeDtypeStruct((s, d), MXU_DTYPE),
                   jax.ShapeDtypeStruct((s, dff), MXU_DTYPE),
                   jax.ShapeDtypeStruct((s, dff), MXU_DTYPE),
                   jax.ShapeDtypeStruct((s, d), MXU_DTYPE),
                   jax.ShapeDtypeStruct((s, d), F32),
                   jax.ShapeDtypeStruct((s, POOL_WIDTH), F32),
                   jax.ShapeDtypeStruct((s, ATTN_WIDTH), MXU_DTYPE),
                   jax.ShapeDtypeStruct((s, ATTN_WIDTH), F32),
                   jax.ShapeDtypeStruct((1, d), F32),
                   jax.ShapeDtypeStruct((1, LANES), F32)],
        scratch_shapes=[pltpu.VMEM(wout.shape, MXU_DTYPE), pltpu.VMEM(wup.shape, MXU_DTYPE),
                        pltpu.VMEM(wdown.shape, MXU_DTYPE), pltpu.VMEM((tm, dff), F32)],
        compiler_params=_params(("arbitrary",)),
    )(x, tgt, ypool, o, g2, wout, wup, wdown)


def _bproj_call(dqn, dkn, dv, q32, k32, dypool, pooled, x, dh1, win, poolw, pscale, qg, kg, g1, tm):
    s, d = x.shape
    nblk = s // tm
    ngrp = len(POOL_WINDOWS)

    def body(dqn_ref, dkn_ref, dv_ref, q_ref, k_ref, dyp_ref, pooled_ref, x_ref, dh1_ref,
             win_hbm, pw_ref, ps_ref, qg_ref, kg_ref, g1_ref,
             dx_ref, dproj_ref, dg1_ref, dqg_ref, dkg_ref, dpw_ref, dps_ref, win_v, ebuf):
        step = pl.program_id(0)
        i = nblk - 1 - step

        @pl.when(step == 0)
        def _():
            pltpu.sync_copy(win_hbm, win_v)
            dg1_ref[...] = jnp.zeros(dg1_ref.shape, F32)
            dqg_ref[...] = jnp.zeros(dqg_ref.shape, F32)
            dkg_ref[...] = jnp.zeros(dkg_ref.shape, F32)
            dpw_ref[...] = jnp.zeros(dpw_ref.shape, F32)
            dps_ref[...] = jnp.zeros(dps_ref.shape, F32)
            ebuf[tm:tm + POOL_HALO, :] = jnp.zeros((POOL_HALO, POOL_WIDTH), F32)

        @pl.when(step > 0)
        def _():
            ebuf[tm:tm + POOL_HALO, :] = ebuf[0:POOL_HALO, :]

        def qk_bwd(dn_sum, raw, gain, scale, dgain_ref):
            rr = lax.rsqrt(_head_sum_bcast(raw * raw) * (1.0 / HEAD_DIM) + NORM_EPS)
            hn = raw * rr
            dgain_ref[...] += jnp.sum(dn_sum * hn, axis=0, keepdims=True) * scale
            dn = dn_sum * (gain * scale)
            return rr * (dn - hn * (_head_sum_bcast(dn * hn) * (1.0 / HEAD_DIM)))

        dq = qk_bwd(dqn_ref[...], q_ref[...], qg_ref[...], HEAD_DIM ** -0.5, dqg_ref)
        dk = qk_bwd(dkn_ref[...], k_ref[...], kg_ref[...], 1.0, dkg_ref)

        t = i * tm + lax.broadcasted_iota(jnp.int32, (tm, 1), 0)
        dpooled = []
        for g, w in enumerate(POOL_WINDOWS):
            ls = slice(g * LANES, (g + 1) * LANES)
            dm = dyp_ref[:, ls]
            pg = pooled_ref[:, ls]
            dps_ref[:, ls] += jnp.sum(dm * _mm(pg, pw_ref[g]), axis=0, keepdims=True)
            dms = (dm * ps_ref[:, ls]).astype(MXU_DTYPE)
            dpw_ref[g] += _mm_tn(pg, dms)
            dpg = _mm_nt(dms, pw_ref[g])
            dpooled.append(dpg)
            ebuf[0:tm, ls] = dpg / jnp.minimum(t + 1, w).astype(F32)
        du = []
        for g, w in enumerate(POOL_WINDOWS):
            ls = slice(g * LANES, (g + 1) * LANES)
            acc = ebuf[0:tm, ls]
            for sh in range(1, w):
                acc = acc + ebuf[sh:sh + tm, ls]
            du.append(acc - dpooled[g])
        parts = [jnp.concatenate(du, axis=-1), dq, dk, dv_ref[...]]
        da = jnp.zeros((tm, d), F32)
        for p, part in enumerate(parts):
            pc = part.astype(MXU_DTYPE)
            dproj_ref[:, p * POOL_WIDTH:(p + 1) * POOL_WIDTH] = pc
            da = da + _mm_nt(pc, win_v[p])
        xv = x_ref[...]
        r = lax.rsqrt(jnp.mean(xv * xv, axis=-1, keepdims=True) + NORM_EPS)
        xn = xv * r
        dg1_ref[...] += jnp.sum(da * xn, axis=0, keepdims=True)
        dx_ref[...] = dh1_ref[...] + _rms_bwd(da * g1_ref[...], xn, r)

    tok = lambda w: pl.BlockSpec((tm, w), lambda t: (nblk - 1 - t, 0))
    const = lambda shp: pl.BlockSpec(shp, lambda t: (0,) * len(shp))
    return pl.pallas_call(
        body, name="bwd_inproj",
        grid=(nblk,),
        in_specs=[tok(ATTN_WIDTH)] * 5 + [tok(POOL_WIDTH), tok(POOL_WIDTH), tok(d), tok(d),
                                          ANY, const(poolw.shape), const((1, POOL_WIDTH)), const((1, ATTN_WIDTH)),
                                          const((1, ATTN_WIDTH)), const((1, d))],
        out_specs=[tok(d), tok(4 * POOL_WIDTH), const((1, d)), const((1, ATTN_WIDTH)), const((1, ATTN_WIDTH)),
                   const((ngrp, LANES, LANES)), const((1, POOL_WIDTH))],
        out_shape=[jax.ShapeDtypeStruct((s, d), F32),
                   jax.ShapeDtypeStruct((s, 4 * POOL_WIDTH), MXU_DTYPE),
                   jax.ShapeDtypeStruct((1, d), F32),
                   jax.ShapeDtypeStruct((1, ATTN_WIDTH), F32),
                   jax.ShapeDtypeStruct((1, ATTN_WIDTH), F32),
                   jax.ShapeDtypeStruct((ngrp, LANES, LANES), F32),
                   jax.ShapeDtypeStruct((1, POOL_WIDTH), F32)],
        scratch_shapes=[pltpu.VMEM(win.shape, MXU_DTYPE), pltpu.VMEM((tm + POOL_HALO, POOL_WIDTH), F32)],
        compiler_params=_params(("arbitrary",)),
    )(dqn, dkn, dv, q32, k32, dypool, pooled, x, dh1, win, poolw, pscale, qg, kg, g1)


def _wgrad_call(a, b, bm, bn, bk, out_shape, out_block, out_index, name):
    s, m = a.shape
    _, n = b.shape
    nk = s // bk

    def body(a_ref, b_ref, o_ref):
        k = pl.program_id(2)

        @pl.when(k == 0)
        def _():
            o_ref[...] = jnp.zeros(o_ref.shape, F32)

        o_ref[...] += _mm_tn(a_ref[...].astype(MXU_DTYPE), b_ref[...].astype(MXU_DTYPE))

    return pl.pallas_call(
        body, name=name,
        grid=(m // bm, n // bn, nk),
        in_specs=[pl.BlockSpec((bk, bm), lambda i, j, k: (k, i)), pl.BlockSpec((bk, bn), lambda i, j, k: (k, j))],
        out_specs=pl.BlockSpec(out_block, out_index),
        out_shape=jax.ShapeDtypeStruct(out_shape, F32),
        compiler_params=_params(("arbitrary", "arbitrary", "arbitrary")),
    )(a, b)


def _local_grads(x, tgt, g1, win, poolw, pscale, qg, kg, rel_bias, wout, g2, wup, wdown):
    s, d = x.shape
    g1r, g2r = g1.reshape(1, d), g2.reshape(1, d)
    psr = pscale.reshape(1, POOL_WIDTH)
    qgr = jnp.tile(qg, N_HEADS).reshape(1, ATTN_WIDTH)
    kgr = jnp.tile(kg, N_HEADS).reshape(1, ATTN_WIDTH)
    pw_c = poolw.astype(MXU_DTYPE)
    buckets = jnp.asarray(_bucket_tables())
    bias = _bias_table_call(rel_bias, buckets)

    a, pooled, ypool, q32, k32, qn, kn, v = _f1_call(x, g1r, win, pw_c, psr, qgr, kgr, tm=512)
    o, lse = _attn_fwd_call(qn, kn, v, bias)
    mixed, c, ff, dz, dy, dh1, dypool, do, delta, dg2, loss = _f2_call(x, tgt, ypool, o, wout, wup, wdown, g2r, tm=256)
    dqn, dkn, dv, dbias = _attn_bwd_call(qn, kn, v, do, lse, delta, bias)
    dx, dproj, dg1, dqg, dkg, dpw, dps = _bproj_call(
        dqn, dkn, dv, q32, k32, dypool, pooled, x, dh1, win, pw_c, psr, qgr, kgr, g1r, tm=256)

    dff = ff.shape[1]
    nin = dproj.shape[1] // N_CHIPS
    g_in = _wgrad_call(a, dproj, d, nin, 512, (N_CHIPS, d, nin), (None, d, nin), lambda i, j, k: (j, 0, 0), "wgrad_in")
    g_out = _wgrad_call(mixed, dh1, d // N_CHIPS, d, 512, (N_CHIPS, d // N_CHIPS, d), (None, d // N_CHIPS, d),
                        lambda i, j, k: (i, 0, 0), "wgrad_out")
    g_up = _wgrad_call(c, dz, d, dff // N_CHIPS, 512, (N_CHIPS, d, dff // N_CHIPS), (None, d, dff // N_CHIPS),
                       lambda i, j, k: (j, 0, 0), "wgrad_up")
    g_down = _wgrad_call(ff, dy, dff // N_CHIPS, d, 512, (N_CHIPS, dff // N_CHIPS, d), (None, dff // N_CHIPS, d),
                         lambda i, j, k: (i, 0, 0), "wgrad_down")
    drb = _rel_bias_grad_call(dbias, buckets)
    small = dict(
        mix_norm_g=dg1.reshape(d), mlp_norm_g=dg2.reshape(d), pool_scale=dps.reshape(POOL_WIDTH),
        q_norm_g=dqg.reshape(ATTN_WIDTH), k_norm_g=dkg.reshape(ATTN_WIDTH),
        rel_bias=drb[:, :N_BUCKETS].T, pool_w=dpw)
    return loss[0, 0], dx, (g_in, g_out, g_up, g_down), small


def _coords():
    return lax.axis_index("x"), lax.axis_index("y"), lax.axis_index("c")


def _other_chips(x, y):
    return [(1 - x, y), (x, 1 - y), (1 - x, 1 - y)]


def _remote(src, dst, send_sem, recv_sem, dev):
    return pltpu.make_async_remote_copy(src_ref=src, dst_ref=dst, send_sem=send_sem, recv_sem=recv_sem,
                                        device_id=dev, device_id_type=MESH)


def _halves(a):
    return a.reshape(a.shape[:-2] + (2, a.shape[-2] // 2, a.shape[-1]))


def _allgather_call(shards):
    nw = len(shards)
    ncp = 3 * nw

    def body(*refs):
        ins, outs = refs[:nw], refs[nw:2 * nw]
        send1, recv1, send2, recv2, loc = refs[2 * nw:]
        x, y, c = _coords()
        chip = 2 * x + y
        others = _other_chips(x, y)
        local = [pltpu.make_async_copy(ins[w], outs[w].at[chip], loc.at[w]) for w in range(nw)]
        for cp in local:
            cp.start()
        first, passed = [], []
        for w in range(nw):
            for k, (ox, oy) in enumerate(others):
                cp = _remote(ins[w].at[c], outs[w].at[chip, c], send1.at[3 * w + k], recv1.at[3 * w + k], (ox, oy, c))
                cp.start()
                first.append(cp)
        for w in range(nw):
            for k, (ox, oy) in enumerate(others):
                piece = outs[w].at[2 * ox + oy, c]
                _remote(piece, piece, send1.at[3 * w + k], recv1.at[3 * w + k], (ox, oy, c)).wait_recv()
                cp = _remote(piece, piece, send2.at[3 * w + k], recv2.at[3 * w + k], (x, y, 1 - c))
                cp.start()
                passed.append(cp)
        for w in range(nw):
            for k, (ox, oy) in enumerate(others):
                piece = outs[w].at[2 * ox + oy, 1 - c]
                _remote(piece, piece, send2.at[3 * w + k], recv2.at[3 * w + k], (x, y, 1 - c)).wait_recv()
        for cp in first + passed:
            cp.wait_send()
        for cp in local:
            cp.wait()

    split = [_halves(s) for s in shards]
    outs = pl.pallas_call(
        body, name="weights_allgather",
        in_specs=[ANY] * nw, out_specs=[ANY] * nw,
        out_shape=[jax.ShapeDtypeStruct((N_CHIPS,) + s.shape, s.dtype) for s in split],
        scratch_shapes=[pltpu.SemaphoreType.DMA((ncp,))] * 4 + [pltpu.SemaphoreType.DMA((nw,))],
    )(*split)
    return [o.reshape((N_CHIPS,) + s.shape) for o, s in zip(outs, shards)]


def _pair_exchange_call(grads, small):
    nw = len(grads)

    def body(*refs):
        ins, small_ref = refs[:nw], refs[nw]
        outs, gathered = refs[nw + 1:2 * nw + 1], refs[2 * nw + 1]
        send, recv, ssend, srecv, loc = refs[2 * nw + 2:]
        x, y, c = _coords()
        me = 4 * x + 2 * y + c
        own = pltpu.make_async_copy(small_ref, gathered.at[me], loc)
        own.start()
        sent = []
        for w in range(nw):
            cp = _remote(ins[w].at[:, 1 - c], outs[w], send.at[w], recv.at[w], (x, y, 1 - c))
            cp.start()
            sent.append(cp)
        flips = [(r >> 2 & 1, r >> 1 & 1, r & 1) for r in range(1, N_DEV)]
        peer = lambda f: ((1 - x) if f[0] else x, (1 - y) if f[1] else y, (1 - c) if f[2] else c)
        for r, f in enumerate(flips):
            cp = _remote(small_ref, gathered.at[me], ssend.at[r], srecv.at[r], peer(f))
            cp.start()
            sent.append(cp)
        for w in range(nw):
            _remote(outs[w], outs[w], send.at[w], recv.at[w], (x, y, 1 - c)).wait_recv()
        for r, f in enumerate(flips):
            px, py, pc = peer(f)
            slot = gathered.at[4 * px + 2 * py + pc]
            _remote(slot, slot, ssend.at[r], srecv.at[r], (px, py, pc)).wait_recv()
        for cp in sent:
            cp.wait_send()
        own.wait()

    return pl.pallas_call(
        body, name="grads_pair_exchange",
        in_specs=[ANY] * (nw + 1), out_specs=[ANY] * (nw + 1),
        out_shape=[jax.ShapeDtypeStruct((g.shape[0],) + g.shape[2:], g.dtype) for g in grads]
        + [jax.ShapeDtypeStruct((N_DEV,) + small.shape, small.dtype)],
        scratch_shapes=[pltpu.SemaphoreType.DMA((nw,)), pltpu.SemaphoreType.DMA((nw,)),
                        pltpu.SemaphoreType.DMA((N_DEV - 1,)), pltpu.SemaphoreType.DMA((N_DEV - 1,)),
                        pltpu.SemaphoreType.DMA],
    )(*grads, small)


def _pair_sum_call(grads, recvd, c_idx, nch):
    nw = len(grads)

    def body(c_ref, *refs):
        for w in range(nw):
            tot = refs[w][...] + refs[nw + w][...]
            refs[2 * nw + w][...] = tot
            refs[3 * nw + w][...] = tot.astype(WIRE_DTYPE)

    in_specs, out_specs, out_shape = [], [], []
    for g in grads:
        in_specs.append(pl.BlockSpec((None, None, g.shape[2] // nch, g.shape[3]), lambda j, i, c_ref: (j, c_ref[0], i, 0)))
    for g in grads:
        in_specs.append(pl.BlockSpec((None, g.shape[2] // nch, g.shape[3]), lambda j, i, c_ref: (j, i, 0)))
    for dt in (F32, WIRE_DTYPE):
        for g in grads:
            out_specs.append(pl.BlockSpec((None, g.shape[2] // nch, g.shape[3]), lambda j, i, c_ref: (j, i, 0)))
            out_shape.append(jax.ShapeDtypeStruct((g.shape[0],) + g.shape[2:], dt))
    res = pl.pallas_call(
        body, name="grads_pair_sum",
        grid_spec=pltpu.PrefetchScalarGridSpec(num_scalar_prefetch=1, grid=(N_CHIPS, nch),
                                               in_specs=in_specs, out_specs=out_specs),
        out_shape=out_shape,
        compiler_params=_params(("arbitrary", "arbitrary")),
    )(c_idx, *grads, *recvd)
    return res[:nw], res[nw:]


def _chip_exchange_call(sums):
    nw = len(sums)

    def body(*refs):
        ins, outs = refs[:nw], refs[nw:2 * nw]
        send, recv = refs[2 * nw:]
        x, y, c = _coords()
        others = _other_chips(x, y)
        sent = []
        for w in range(nw):
            for k, (ox, oy) in enumerate(others):
                cp = _remote(ins[w].at[2 * ox + oy], outs[w].at[k], send.at[3 * w + k], recv.at[3 * w + k], (ox, oy, c))
                cp.start()
                sent.append(cp)
        for w in range(nw):
            for k, (ox, oy) in enumerate(others):
                _remote(outs[w].at[k], outs[w].at[k], send.at[3 * w + k], recv.at[3 * w + k], (ox, oy, c)).wait_recv()
        for cp in sent:
            cp.wait_send()

    return pl.pallas_call(
        body, name="grads_chip_exchange",
        in_specs=[ANY] * nw, out_specs=[ANY] * nw,
        out_shape=[jax.ShapeDtypeStruct((3,) + s.shape[1:], s.dtype) for s in sums],
        scratch_shapes=[pltpu.SemaphoreType.DMA((3 * nw,)), pltpu.SemaphoreType.DMA((3 * nw,))],
    )(*sums)


def _chip_sum_call(own, recvd, chip_idx, nch):
    nw = len(own)

    def body(chip_ref, *refs):
        for w in range(nw):
            q = refs[nw + w]
            refs[2 * nw + w][...] = ((refs[w][...] + q[0].astype(F32)) + q[1].astype(F32)) + q[2].astype(F32)

    in_specs, out_specs, out_shape = [], [], []
    for s in own:
        in_specs.append(pl.BlockSpec((None, s.shape[1] // nch, s.shape[2]), lambda i, chip_ref: (chip_ref[0], i, 0)))
    for s in own:
        in_specs.append(pl.BlockSpec((3, s.shape[1] // nch, s.shape[2]), lambda i, chip_ref: (0, i, 0)))
    for s in own:
        out_specs.append(pl.BlockSpec((s.shape[1] // nch, s.shape[2]), lambda i, chip_ref: (i, 0)))
        out_shape.append(jax.ShapeDtypeStruct(s.shape[1:], F32))
    return pl.pallas_call(
        body, name="grads_chip_sum",
        grid_spec=pltpu.PrefetchScalarGridSpec(num_scalar_prefetch=1, grid=(nch,),
                                               in_specs=in_specs, out_specs=out_specs),
        out_shape=out_shape,
        compiler_params=_params(("arbitrary",)),
    )(chip_idx, *own, *recvd)


def _pair_allgather_call(halves):
    nw = len(halves)

    def body(*refs):
        ins, outs = refs[:nw], refs[nw:2 * nw]
        send, recv, loc = refs[2 * nw:]
        x, y, c = _coords()
        cps = []
        for w in range(nw):
            lc = pltpu.make_async_copy(ins[w], outs[w].at[c], loc.at[w])
            lc.start()
            cp = _remote(ins[w], outs[w].at[c], send.at[w], recv.at[w], (x, y, 1 - c))
            cp.start()
            cps.append((lc, cp))
        for w in range(nw):
            theirs = outs[w].at[1 - c]
            _remote(theirs, theirs, send.at[w], recv.at[w], (x, y, 1 - c)).wait_recv()
        for lc, cp in cps:
            cp.wait_send()
            lc.wait()

    outs = pl.pallas_call(
        body, name="grads_pair_allgather",
        in_specs=[ANY] * nw, out_specs=[ANY] * nw,
        out_shape=[jax.ShapeDtypeStruct((2,) + h.shape, h.dtype) for h in halves],
        scratch_shapes=[pltpu.SemaphoreType.DMA((nw,))] * 3,
    )(*halves)
    return [o.reshape(2 * h.shape[0], h.shape[1]) for o, h in zip(outs, halves)]


def _adamw(w, g, m, v):
    m = ADAM_B1 * m + (1.0 - ADAM_B1) * g
    v = ADAM_B2 * v + (1.0 - ADAM_B2) * (g * g)
    m_hat = m / (1.0 - ADAM_B1 ** ADAM_STEP)
    v_hat = v / (1.0 - ADAM_B2 ** ADAM_STEP)
    delta = -ADAM_LR * (m_hat / (jnp.sqrt(v_hat) + ADAM_EPS) + ADAM_WD * w)
    return delta, m, v


def _adamw_call(ws, gs, ms, vs, nch):
    nw = len(ws)

    def body(*refs):
        for w in range(nw):
            g = refs[nw + w][...]
            delta, m, v = _adamw(refs[w][...], g, refs[2 * nw + w][...], refs[3 * nw + w][...])
            refs[4 * nw + w][...] = g
            refs[5 * nw + w][...] = delta
            refs[6 * nw + w][...] = m
            refs[7 * nw + w][...] = v

    specs = [pl.BlockSpec((a.shape[0] // nch, a.shape[1]), lambda i: (i, 0)) for a in ws]
    res = pl.pallas_call(
        body, name="adamw_big",
        grid=(nch,),
        in_specs=specs * 4, out_specs=specs * 4,
        out_shape=[jax.ShapeDtypeStruct(a.shape, F32) for a in ws] * 4,
        compiler_params=_params(("arbitrary",)),
    )(*ws, *gs, *ms, *vs)
    return res[:nw], res[nw:2 * nw], res[2 * nw:3 * nw], res[3 * nw:]


def _small_call(gathered, w, m, v):
    def fold(row):
        tot = row[:, 0:LANES] + row[:, LANES:2 * LANES] + row[:, 2 * LANES:3 * LANES] + row[:, 3 * LANES:4 * LANES]
        return tot + pltpu.roll(tot, HEAD_DIM, axis=1)

    def body(ga_ref, w_ref, m_ref, v_ref, g_out, d_out, m_out, v_out):
        g = ga_ref[0]
        for i in range(1, N_DEV):
            g = g + ga_ref[i]
        unfolded = g[4:5, :]
        folded = jnp.concatenate([fold(unfolded[:, :ATTN_WIDTH]), fold(unfolded[:, ATTN_WIDTH:]),
                                  jnp.zeros((1, 1024 - 2 * LANES), F32)], axis=-1)
        row = lax.broadcasted_iota(jnp.int32, g.shape, 0)
        g = jnp.where(row == 3, folded, g)
        delta, mm, vv = _adamw(w_ref[...], g, m_ref[...], v_ref[...])
        g_out[...] = g
        d_out[...] = delta
        m_out[...] = mm
        v_out[...] = vv

    return pl.pallas_call(
        body, name="adamw_small",
        out_shape=[jax.ShapeDtypeStruct(w.shape, F32)] * 4,
        compiler_params=_params(),
    )(gathered, w, m, v)


def _pack_small(p, folded=True, loss=None):
    z = lambda n: jnp.zeros((n,), F32)
    rows = [p["mix_norm_g"], p["mlp_norm_g"],
            jnp.concatenate([p["pool_scale"], p["rel_bias"].reshape(-1), z(1024 - POOL_WIDTH - N_BUCKETS * N_HEADS)])]
    if folded:
        rows += [jnp.concatenate([p["q_norm_g"], z(LANES - HEAD_DIM), p["k_norm_g"], z(1024 - LANES - HEAD_DIM)]), z(1024)]
    else:
        rows += [z(1024), jnp.concatenate([p["q_norm_g"], p["k_norm_g"]])]
    rows += [z(1024) if loss is None else jnp.concatenate([loss.reshape(1), z(1023)])]
    head = jnp.stack(rows + [z(1024)] * 2)
    return jnp.concatenate([head, p["pool_w"].reshape(-1, 1024)], axis=0)


def _unpack_small(a):
    return dict(
        mix_norm_g=a[0], mlp_norm_g=a[1], pool_scale=a[2, :POOL_WIDTH],
        rel_bias=a[2, POOL_WIDTH:POOL_WIDTH + N_BUCKETS * N_HEADS].reshape(N_BUCKETS, N_HEADS),
        q_norm_g=a[3, :HEAD_DIM], k_norm_g=a[3, LANES:LANES + HEAD_DIM],
        pool_w=a[8:].reshape(len(POOL_WINDOWS), LANES, LANES))


_WEIGHT_ORDER = ("mix_norm_g", "w_in", "pool_w", "pool_scale", "q_norm_g", "k_norm_g", "rel_bias", "w_out",
                 "mlp_norm_g", "w_up", "w_down")
_BIG = ("w_in", "w_out", "w_up", "w_down")


def kernel(x, mix_norm_g, w_in, pool_w, pool_scale, q_norm_g, k_norm_g, rel_bias, w_out, mlp_norm_g, w_up, w_down, loss_target, m_mix_norm_g, m_w_in, m_pool_w, m_pool_scale, m_q_norm_g, m_k_norm_g, m_rel_bias, m_w_out, m_mlp_norm_g, m_w_up, m_w_down, v_mix_norm_g, v_w_in, v_pool_w, v_pool_scale, v_q_norm_g, v_k_norm_g, v_rel_bias, v_w_out, v_mlp_norm_g, v_w_up, v_w_down):
    w = dict(mix_norm_g=mix_norm_g, w_in=w_in, pool_w=pool_w, pool_scale=pool_scale, q_norm_g=q_norm_g,
             k_norm_g=k_norm_g, rel_bias=rel_bias, w_out=w_out, mlp_norm_g=mlp_norm_g, w_up=w_up, w_down=w_down)
    m = dict(mix_norm_g=m_mix_norm_g, w_in=m_w_in, pool_w=m_pool_w, pool_scale=m_pool_scale, q_norm_g=m_q_norm_g,
             k_norm_g=m_k_norm_g, rel_bias=m_rel_bias, w_out=m_w_out, mlp_norm_g=m_mlp_norm_g, w_up=m_w_up, w_down=m_w_down)
    v = dict(mix_norm_g=v_mix_norm_g, w_in=v_w_in, pool_w=v_pool_w, pool_scale=v_pool_scale, q_norm_g=v_q_norm_g,
             k_norm_g=v_k_norm_g, rel_bias=v_rel_bias, w_out=v_w_out, mlp_norm_g=v_mlp_norm_g, w_up=v_w_up, w_down=v_w_down)
    xc, yc, cc = _coords()

    win_f, wout_f, wup_f, wdown_f = _allgather_call([w[n].astype(WIRE_DTYPE) for n in _BIG])
    loss_part, dx, big_grads, small_grads = _local_grads(
        x[0], loss_target[0], mix_norm_g, win_f, pool_w, pool_scale, q_norm_g, k_norm_g, rel_bias,
        wout_f.reshape(wout_f.shape[0] * wout_f.shape[1], wout_f.shape[2]), mlp_norm_g, wup_f, wdown_f)

    big_halves = [_halves(g) for g in big_grads]
    *from_sibling, small_all = _pair_exchange_call(big_halves, _pack_small(small_grads, folded=False, loss=loss_part))
    c_idx = jnp.reshape(cc, (1,)).astype(jnp.int32)
    chip_idx = jnp.reshape(2 * xc + yc, (1,)).astype(jnp.int32)
    sums32, sums_wire = _pair_sum_call(big_halves, from_sibling, c_idx, nch=2)
    from_chips = _chip_exchange_call(list(sums_wire))
    halves = _chip_sum_call(list(sums32), from_chips, chip_idx, nch=2)
    g_reduced = _pair_allgather_call(list(halves))
    g_big, d_big, m_big, v_big = _adamw_call(
        [w[n] for n in _BIG], g_reduced, [m[n] for n in _BIG], [v[n] for n in _BIG], nch=8)
    g_pack, d_pack, m_pack, v_pack = _small_call(small_all, _pack_small(w), _pack_small(m), _pack_small(v))

    grads, deltas, new_m, new_v = (_unpack_small(a) for a in (g_pack, d_pack, m_pack, v_pack))
    for i, n in enumerate(_BIG):
        grads[n], deltas[n], new_m[n], new_v[n] = g_big[i], d_big[i], m_big[i], v_big[i]
    loss = g_pack[LOSS_ROW, 0]
    return (loss, dx[None], *[grads[n] for n in _WEIGHT_ORDER], *[deltas[n] for n in _WEIGHT_ORDER],
            *[new_m[n] for n in _WEIGHT_ORDER], *[new_v[n] for n in _WEIGHT_ORDER])
```

```python
import math

import jax
import jax.numpy as jnp
import numpy as np
from jax import lax
from jax.experimental import pallas as pl
from jax.experimental.pallas import tpu as pltpu

F32 = jnp.float32
MXU_DTYPE = jnp.bfloat16
WIRE_DTYPE = jnp.bfloat16

NORM_EPS = 1e-6
NEG_INF = -1e30
LANES = 128
HEAD_DIM = 64
N_HEADS = 8
POOL_WIDTH = 512
ATTN_WIDTH = 512
POOL_WINDOWS = (2, 4, 8, 16)
POOL_HALO = 16
DILATED_PATTERNS = ((128, 1), (512, 4), (2048, 16))
ATT_BLOCK = 128
ATT_SUPER = ATT_BLOCK * max(dl for _, dl in DILATED_PATTERNS)
ATT_UNITS = ATT_SUPER // ATT_BLOCK
N_BUCKETS = 32
MAX_DISTANCE = 2048
N_CHIPS = 4
N_DEV = 8
ADAM_LR, ADAM_B1, ADAM_B2, ADAM_EPS, ADAM_WD, ADAM_STEP = 0.001, 0.9, 0.999, 1e-08, 0.01, 10
VMEM_LIMIT = 56 * 1024 * 1024
MESH = pl.DeviceIdType.MESH
ANY = pl.BlockSpec(memory_space=pl.ANY)

SMALL_ROWS = 72
LOSS_ROW = 5


def _mm(a, b):
    return jnp.dot(a, b, preferred_element_type=F32)


def _mm_nt(a, b):
    return lax.dot_general(a, b, (((1,), (1,)), ((), ())), preferred_element_type=F32)


def _mm_tn(a, b):
    return lax.dot_general(a, b, (((0,), (0,)), ((), ())), preferred_element_type=F32)


def _params(sem=None, **kw):
    if sem is not None:
        kw["dimension_semantics"] = sem
    return pltpu.CompilerParams(vmem_limit_bytes=VMEM_LIMIT, **kw)


def _low_half():
    return lax.broadcasted_iota(jnp.int32, (1, LANES), 1) < HEAD_DIM


def _head_sum_bcast(y):
    lo = _low_half()
    outs = []
    for j in range(y.shape[1] // LANES):
        c = y[:, j * LANES:(j + 1) * LANES]
        s_lo = jnp.sum(jnp.where(lo, c, 0.0), axis=-1, keepdims=True)
        s_hi = jnp.sum(jnp.where(lo, 0.0, c), axis=-1, keepdims=True)
        outs.append(jnp.where(lo, s_lo, s_hi))
    return jnp.concatenate(outs, axis=-1)


def _rms_bwd(dn, hn, r):
    return r * (dn - hn * jnp.mean(dn * hn, axis=-1, keepdims=True))


def _t5_bucket_np(dist):
    max_exact = N_BUCKETS // 2
    d_f = np.maximum(dist, 1).astype(np.float32)
    ratio = (np.log(d_f / np.float32(max_exact)) / np.float32(math.log(MAX_DISTANCE / max_exact))).astype(np.float32)
    large = max_exact + (ratio * np.float32(N_BUCKETS - max_exact)).astype(np.int32)
    large = np.minimum(large, N_BUCKETS - 1)
    return np.where(dist < max_exact, dist, large).astype(np.int32)


def _bucket_tables():
    qq = np.arange(ATT_BLOCK)[:, None]
    kk = np.arange(2 * ATT_BLOCK)[None, :]
    dist = np.clip(qq + ATT_BLOCK - kk, 0, ATT_BLOCK)
    return np.stack([_t5_bucket_np(dist * dl) for (_, dl) in DILATED_PATTERNS])


def _f1_call(x, g1, win, poolw, pscale, qg, kg, tm):
    s, d = x.shape
    nblk = s // tm

    def body(x_ref, g1_ref, win_ref, pw_ref, ps_ref, qg_ref, kg_ref,
             a_ref, pooled_ref, ypool_ref, q32_ref, k32_ref, qn_ref, kn_ref, v_ref, ubuf):
        i = pl.program_id(0)
        xv = x_ref[...]
        r = lax.rsqrt(jnp.mean(xv * xv, axis=-1, keepdims=True) + NORM_EPS)
        a = ((xv * r) * g1_ref[...]).astype(MXU_DTYPE)
        a_ref[...] = a
        u = _mm(a, win_ref[0])
        q = _mm(a, win_ref[1])
        k = _mm(a, win_ref[2])
        v_ref[...] = _mm(a, win_ref[3]).astype(MXU_DTYPE)
        q32_ref[...] = q
        k32_ref[...] = k
        rq = lax.rsqrt(_head_sum_bcast(q * q) * (1.0 / HEAD_DIM) + NORM_EPS)
        qn_ref[...] = (((q * rq) * qg_ref[...]) * (HEAD_DIM ** -0.5)).astype(MXU_DTYPE)
        rk = lax.rsqrt(_head_sum_bcast(k * k) * (1.0 / HEAD_DIM) + NORM_EPS)
        kn_ref[...] = ((k * rk) * kg_ref[...]).astype(MXU_DTYPE)

        @pl.when(i == 0)
        def _():
            ubuf[0:POOL_HALO, :] = jnp.zeros((POOL_HALO, POOL_WIDTH), F32)

        @pl.when(i > 0)
        def _():
            ubuf[0:POOL_HALO, :] = ubuf[tm:tm + POOL_HALO, :]

        ubuf[POOL_HALO:POOL_HALO + tm, :] = u
        t = i * tm + lax.broadcasted_iota(jnp.int32, (tm, 1), 0)
        for g, w in enumerate(POOL_WINDOWS):
            ls = slice(g * LANES, (g + 1) * LANES)
            ug = u[:, ls]
            acc = ug
            for sh in range(1, w):
                acc = acc + ubuf[POOL_HALO - sh:POOL_HALO - sh + tm, ls]
            cnt = jnp.minimum(t + 1, w).astype(F32)
            pooled = (acc / cnt - ug).astype(MXU_DTYPE)
            pooled_ref[:, ls] = pooled
            ypool_ref[:, ls] = (_mm(pooled, pw_ref[g]) * ps_ref[:, ls]).astype(MXU_DTYPE)

    tok = lambda w: pl.BlockSpec((tm, w), lambda i: (i, 0))
    full = lambda shp: pl.BlockSpec(shp, lambda i: (0,) * len(shp))
    return pl.pallas_call(
        body, name="fwd_inproj",
        grid=(nblk,),
        in_specs=[tok(d), full((1, d)), full(win.shape), full(poolw.shape), full((1, POOL_WIDTH)),
                  full((1, ATTN_WIDTH)), full((1, ATTN_WIDTH))],
        out_specs=[tok(d), tok(POOL_WIDTH), tok(POOL_WIDTH), tok(ATTN_WIDTH), tok(ATTN_WIDTH),
                   tok(ATTN_WIDTH), tok(ATTN_WIDTH), tok(ATTN_WIDTH)],
        out_shape=[jax.ShapeDtypeStruct((s, d), MXU_DTYPE),
                   jax.ShapeDtypeStruct((s, POOL_WIDTH), MXU_DTYPE),
                   jax.ShapeDtypeStruct((s, POOL_WIDTH), MXU_DTYPE),
                   jax.ShapeDtypeStruct((s, ATTN_WIDTH), F32),
                   jax.ShapeDtypeStruct((s, ATTN_WIDTH), F32),
                   jax.ShapeDtypeStruct((s, ATTN_WIDTH), MXU_DTYPE),
                   jax.ShapeDtypeStruct((s, ATTN_WIDTH), MXU_DTYPE),
                   jax.ShapeDtypeStruct((s, ATTN_WIDTH), MXU_DTYPE)],
        scratch_shapes=[pltpu.VMEM((tm + POOL_HALO, POOL_WIDTH), F32)],
        compiler_params=_params(("arbitrary",)),
    )(x, g1, win, poolw, pscale, qg, kg)


def _band_mask(n):
    qq = lax.broadcasted_iota(jnp.int32, (ATT_BLOCK, 2 * ATT_BLOCK), 0)
    kk = lax.broadcasted_iota(jnp.int32, (ATT_BLOCK, 2 * ATT_BLOCK), 1)
    dist = qq + ATT_BLOCK - kk
    return (dist >= 0) & (dist <= ATT_BLOCK) & ((n > 0) | (kk >= ATT_BLOCK))


def _unit_rows(u, dl):
    r = u % dl
    b = u // dl
    q0 = r + dl * ATT_BLOCK * b
    return b, pl.ds(q0, ATT_BLOCK, stride=dl), pl.ds(ATT_SUPER + q0 - dl * ATT_BLOCK, 2 * ATT_BLOCK, stride=dl)


def _attn_fwd_call(qn, kn, v, bias):
    s, w = qn.shape
    nsb = s // ATT_SUPER
    npair = w // LANES

    def body(q_ref, kc_ref, kp_ref, vc_ref, vp_ref, b_ref, o_ref, lse_ref, qf, kf, vf, acc_s, m_s, l_s):
        sb = pl.program_id(1)
        qf[...] = q_ref[...].astype(F32)
        kf[0:ATT_SUPER, :] = kp_ref[...].astype(F32)
        kf[ATT_SUPER:, :] = kc_ref[...].astype(F32)
        vf[0:ATT_SUPER, :] = vp_ref[...].astype(F32)
        vf[ATT_SUPER:, :] = vc_ref[...].astype(F32)
        lo = _low_half()
        for p, (_, dl) in enumerate(DILATED_PATTERNS):
            def unit(u, carry, p=p, dl=dl):
                b, rows_q, rows_k = _unit_rows(u, dl)
                valid = _band_mask(sb * (ATT_UNITS // dl) + b)
                qp = qf[rows_q, :].astype(MXU_DTYPE)
                kcat = kf[rows_k, :].astype(MXU_DTYPE)
                vcat = vf[rows_k, :].astype(MXU_DTYPE)
                acc_h, m_h, l_h = [], [], []
                for e in range(2):
                    sel = lo if e == 0 else jnp.logical_not(lo)
                    qm = jnp.where(sel, qp, jnp.zeros_like(qp))
                    sc = _mm_nt(qm, kcat) + b_ref[p, e]
                    sc = jnp.where(valid, sc, NEG_INF)
                    m = jnp.max(sc, axis=-1, keepdims=True)
                    pr = jnp.exp(sc - m)
                    l_h.append(jnp.sum(pr, axis=-1, keepdims=True))
                    m_h.append(m)
                    acc_h.append(_mm(pr.astype(MXU_DTYPE), vcat))
                acc = jnp.where(lo, acc_h[0], acc_h[1])
                m = jnp.where(lo, m_h[0], m_h[1])
                l = jnp.where(lo, l_h[0], l_h[1])
                if p == 0:
                    acc_s[rows_q, :] = acc
                    m_s[rows_q, :] = m
                    l_s[rows_q, :] = l
                else:
                    m_old = m_s[rows_q, :]
                    m_new = jnp.maximum(m_old, m)
                    a_old = jnp.exp(m_old - m_new)
                    a_new = jnp.exp(m - m_new)
                    acc_s[rows_q, :] = a_old * acc_s[rows_q, :] + a_new * acc
                    l_s[rows_q, :] = a_old * l_s[rows_q, :] + a_new * l
                    m_s[rows_q, :] = m_new
                return carry

            lax.fori_loop(0, ATT_UNITS, unit, 0, unroll=4)
        l = l_s[...]
        o_ref[...] = acc_s[...] / l
        lse_ref[...] = m_s[...] + jnp.log(l)

    cur = pl.BlockSpec((ATT_SUPER, LANES), lambda j, t: (t, j))
    prev = pl.BlockSpec((ATT_SUPER, LANES), lambda j, t: (jnp.maximum(t - 1, 0), j))
    bspec = pl.BlockSpec((len(DILATED_PATTERNS), 2, ATT_BLOCK, 2 * ATT_BLOCK), lambda j, t: (0, j, 0, 0))
    return pl.pallas_call(
        body, name="attn_fwd",
        grid=(npair, nsb),
        in_specs=[cur, cur, prev, cur, prev, bspec],
        out_specs=[cur, cur],
        out_shape=[jax.ShapeDtypeStruct((s, w), F32), jax.ShapeDtypeStruct((s, w), F32)],
        scratch_shapes=[pltpu.VMEM((ATT_SUPER, LANES), F32), pltpu.VMEM((2 * ATT_SUPER, LANES), F32),
                        pltpu.VMEM((2 * ATT_SUPER, LANES), F32), pltpu.VMEM((ATT_SUPER, LANES), F32),
                        pltpu.VMEM((ATT_SUPER, LANES), F32), pltpu.VMEM((ATT_SUPER, LANES), F32)],
        compiler_params=_params(("arbitrary", "arbitrary")),
    )(qn, kn, kn, v, v, bias)


def _attn_bwd_call(qn, kn, v, do, lse, delta, bias):
    s, w = qn.shape
    nsb = s // ATT_SUPER
    npair = w // LANES

    def body(q_ref, kc_ref, kp_ref, vc_ref, vp_ref, do_ref, lse_ref, dlt_ref, b_ref,
             dq_ref, dk_ref, dv_ref, db_ref, qf, kf, vf, dof, dkf, dvf):
        step = pl.program_id(1)
        sb = nsb - 1 - step
        qf[...] = q_ref[...].astype(F32)
        dof[...] = do_ref[...].astype(F32)
        kf[0:ATT_SUPER, :] = kp_ref[...].astype(F32)
        kf[ATT_SUPER:, :] = kc_ref[...].astype(F32)
        vf[0:ATT_SUPER, :] = vp_ref[...].astype(F32)
        vf[ATT_SUPER:, :] = vc_ref[...].astype(F32)

        @pl.when(step == 0)
        def _():
            db_ref[...] = jnp.zeros(db_ref.shape, F32)
            dkf[ATT_SUPER:, :] = jnp.zeros((ATT_SUPER, LANES), F32)
            dvf[ATT_SUPER:, :] = jnp.zeros((ATT_SUPER, LANES), F32)

        @pl.when(step > 0)
        def _():
            dkf[ATT_SUPER:, :] = dkf[0:ATT_SUPER, :]
            dvf[ATT_SUPER:, :] = dvf[0:ATT_SUPER, :]

        dkf[0:ATT_SUPER, :] = jnp.zeros((ATT_SUPER, LANES), F32)
        dvf[0:ATT_SUPER, :] = jnp.zeros((ATT_SUPER, LANES), F32)
        lo = _low_half()
        for p, (_, dl) in enumerate(DILATED_PATTERNS):
            def unit(u, carry, p=p, dl=dl):
                b, rows_q, rows_k = _unit_rows(u, dl)
                valid = _band_mask(sb * (ATT_UNITS // dl) + b)
                qp = qf[rows_q, :].astype(MXU_DTYPE)
                dop = dof[rows_q, :].astype(MXU_DTYPE)
                kcat = kf[rows_k, :].astype(MXU_DTYPE)
                vcat = vf[rows_k, :].astype(MXU_DTYPE)
                lse2 = lse_ref[rows_q, :]
                dlt2 = dlt_ref[rows_q, :]
                dq_h = []
                dk = jnp.zeros((2 * ATT_BLOCK, LANES), F32)
                dv = jnp.zeros((2 * ATT_BLOCK, LANES), F32)
                for e in range(2):
                    sel = lo if e == 0 else jnp.logical_not(lo)
                    col = e * HEAD_DIM
                    qm = jnp.where(sel, qp, jnp.zeros_like(qp))
                    dom = jnp.where(sel, dop, jnp.zeros_like(dop))
                    sc = _mm_nt(qm, kcat) + b_ref[p, e]
                    pr = jnp.where(valid, jnp.exp(sc - lse2[:, col:col + 1]), 0.0)
                    dp = _mm_nt(dom, vcat)
                    ds = pr * (dp - dlt2[:, col:col + 1])
                    db_ref[p, e] += ds
                    ds_c = ds.astype(MXU_DTYPE)
                    dq_h.append(_mm(ds_c, kcat))
                    dk = dk + _mm_tn(ds_c, qm)
                    dv = dv + _mm_tn(pr.astype(MXU_DTYPE), dom)
                dq = jnp.where(lo, dq_h[0], dq_h[1])
                if p == 0:
                    dq_ref[rows_q, :] = dq
                else:
                    dq_ref[rows_q, :] += dq
                dkf[rows_k, :] += dk
                dvf[rows_k, :] += dv
                return carry

            lax.fori_loop(0, ATT_UNITS, unit, 0, unroll=2)
        dk_ref[...] = dkf[ATT_SUPER:, :]
        dv_ref[...] = dvf[ATT_SUPER:, :]

    cur = pl.BlockSpec((ATT_SUPER, LANES), lambda j, t: (nsb - 1 - t, j))
    prev = pl.BlockSpec((ATT_SUPER, LANES), lambda j, t: (jnp.maximum(nsb - 2 - t, 0), j))
    bshape = (len(DILATED_PATTERNS), 2, ATT_BLOCK, 2 * ATT_BLOCK)
    bspec = pl.BlockSpec(bshape, lambda j, t: (0, j, 0, 0))
    sup = lambda: pltpu.VMEM((ATT_SUPER, LANES), F32)
    sup2 = lambda: pltpu.VMEM((2 * ATT_SUPER, LANES), F32)
    return pl.pallas_call(
        body, name="attn_bwd",
        grid=(npair, nsb),
        in_specs=[cur, cur, prev, cur, prev, cur, cur, cur, bspec],
        out_specs=[cur, cur, cur, bspec],
        out_shape=[jax.ShapeDtypeStruct((s, w), F32)] * 3 + [jax.ShapeDtypeStruct(bias.shape, F32)],
        scratch_shapes=[sup(), sup2(), sup2(), sup(), sup2(), sup2()],
        compiler_params=_params(("arbitrary", "arbitrary")),
    )(qn, kn, kn, v, v, do, lse, delta, bias)


def _bias_table_call(rel_bias, buckets):
    npat = buckets.shape[0]

    def body(rb_ref, bk_ref, out_ref):
        for p in range(npat):
            for half in range(2):
                ks = slice(half * ATT_BLOCK, (half + 1) * ATT_BLOCK)
                bk = bk_ref[p, :, ks]
                for h in range(N_HEADS):
                    def pick(b, acc, h=h, bk=bk):
                        return jnp.where(bk == b, rb_ref[b, h], acc)

                    out_ref[p, h, :, ks] = lax.fori_loop(0, N_BUCKETS, pick, jnp.zeros((ATT_BLOCK, ATT_BLOCK), F32))

    return pl.pallas_call(
        body, name="bias_table",
        in_specs=[pl.BlockSpec(memory_space=pltpu.SMEM), pl.BlockSpec(memory_space=pltpu.VMEM)],
        out_shape=jax.ShapeDtypeStruct((npat, N_HEADS, ATT_BLOCK, 2 * ATT_BLOCK), F32),
        compiler_params=_params(),
    )(rel_bias, buckets)


def _rel_bias_grad_call(dbias, buckets):
    npat, nh = dbias.shape[0], dbias.shape[1]

    def body(db_ref, bk_ref, out_ref):
        lane = lax.broadcasted_iota(jnp.int32, (nh, LANES), 1)
        out = jnp.zeros((nh, LANES), F32)
        for b in range(N_BUCKETS):
            tot = jnp.zeros((nh, 1), F32)
            for p in range(npat):
                hit = jnp.where(bk_ref[p][None] == b, db_ref[p], 0.0)
                tot = tot + jnp.sum(jnp.sum(hit, axis=2), axis=1, keepdims=True)
            out = jnp.where(lane == b, tot, out)
        out_ref[...] = out

    return pl.pallas_call(
        body, name="rel_bias_grad",
        out_shape=jax.ShapeDtypeStruct((nh, LANES), F32),
        compiler_params=_params(),
    )(dbias, buckets)


def _f2_call(x, tgt, ypool, o, wout, wup, wdown, g2, tm):
    s, d = x.shape
    nblk = s // tm
    nch, _, fch = wup.shape
    dff = nch * fch
    mixw = POOL_WIDTH + ATTN_WIDTH

    def body(x_ref, t_ref, yp_ref, o_ref, g2_ref, wout_hbm, wup_hbm, wdown_hbm,
             mixed_ref, c_ref, ff_ref, dz_ref, dy_ref, dh1_ref, dyp_ref, do_ref, dlt_ref, dg2_ref, loss_ref,
             wout_v, wup_v, wdown_v, rz):
        i = pl.program_id(0)

        @pl.when(i == 0)
        def _():
            pltpu.sync_copy(wout_hbm, wout_v)
            pltpu.sync_copy(wup_hbm, wup_v)
            pltpu.sync_copy(wdown_hbm, wdown_v)
            dg2_ref[...] = jnp.zeros(dg2_ref.shape, F32)
            loss_ref[...] = jnp.zeros(loss_ref.shape, F32)

        o = o_ref[...]
        mixed = jnp.concatenate([yp_ref[...], o.astype(MXU_DTYPE)], axis=-1)
        mixed_ref[...] = mixed
        h1 = x_ref[...] + _mm(mixed, wout_v[...])
        r2 = lax.rsqrt(jnp.mean(h1 * h1, axis=-1, keepdims=True) + NORM_EPS)
        hn = h1 * r2
        c = (hn * g2_ref[...]).astype(MXU_DTYPE)
        c_ref[...] = c
        y = h1
        for j in range(nch):
            cs = slice(j * fch, (j + 1) * fch)
            z = jnp.maximum(_mm(c, wup_v[j]), 0.0)
            rz[:, cs] = z
            ff = (z * z).astype(MXU_DTYPE)
            ff_ref[:, cs] = ff
            y = y + _mm(ff, wdown_v[j])
        err = y - t_ref[...]
        loss_ref[...] += jnp.sum(err * err) * (0.5 / d)
        dy = err * (1.0 / d)
        dy_c = dy.astype(MXU_DTYPE)
        dy_ref[...] = dy_c
        dc = jnp.zeros((tm, d), F32)
        for j in range(nch):
            cs = slice(j * fch, (j + 1) * fch)
            dz = (_mm_nt(dy_c, wdown_v[j]) * (2.0 * rz[:, cs])).astype(MXU_DTYPE)
            dz_ref[:, cs] = dz
            dc = dc + _mm_nt(dz, wup_v[j])
        dg2_ref[...] += jnp.sum(dc * hn, axis=0, keepdims=True)
        dh1 = dy + _rms_bwd(dc * g2_ref[...], hn, r2)
        dh1_ref[...] = dh1
        dmix = _mm_nt(dh1.astype(MXU_DTYPE), wout_v[...])
        dyp_ref[...] = dmix[:, :POOL_WIDTH]
        do = dmix[:, POOL_WIDTH:]
        do_ref[...] = do.astype(MXU_DTYPE)
        dlt_ref[...] = _head_sum_bcast(do * o)

    tok = lambda w: pl.BlockSpec((tm, w), lambda i: (i, 0))
    const = lambda shp: pl.BlockSpec(shp, lambda i: (0,) * len(shp))
    return pl.pallas_call(
        body, name="fwd_mlp_bwd_mlp",
        grid=(nblk,),
        in_specs=[tok(d), tok(d), tok(POOL_WIDTH), tok(ATTN_WIDTH), const((1, d)), ANY, ANY, ANY],
        out_specs=[tok(mixw), tok(d), tok(dff), tok(dff), tok(d), tok(d), tok(POOL_WIDTH), tok(ATTN_WIDTH),
                   tok(ATTN_WIDTH), const((1, d)), const((1, LANES))],
        out_shape=[jax.ShapeDtypeStruct((s, mixw), MXU_DTYPE),
                   jax.ShapeDtypeStruct((s, d), MXU_DTYPE),
                   jax.ShapeDtypeStruct((s, dff), MXU_DTYPE),
                   jax.ShapeDtypeStruct((s, dff), MXU_DTYPE),
                   jax.ShapeDtypeStruct((s, d), MXU_DTYPE),
                   jax.ShapeDtypeStruct((s, d), F32),
                   jax.ShapeDtypeStruct((s, POOL_WIDTH), F32),
                   jax.ShapeDtypeStruct((s, ATTN_WIDTH), MXU_DTYPE),
                   jax.ShapeDtypeStruct((s, ATTN_WIDTH), F32),
                   jax.ShapeDtypeStruct((1, d), F32),
                   jax.ShapeDtypeStruct((1, LANES), F32)],
        scratch_shapes=[pltpu.VMEM(wout.shape, MXU_DTYPE), pltpu.VMEM(wup.shape, MXU_DTYPE),
                        pltpu.VMEM(wdown.shape, MXU_DTYPE), pltpu.VMEM((tm, dff), F32)],
        compiler_params=_params(("arbitrary",)),
    )(x, tgt, ypool, o, g2, wout, wup, wdown)


def _bproj_call(dqn, dkn, dv, q32, k32, dypool, pooled, x, dh1, win, poolw, pscale, qg, kg, g1, tm):
    s, d = x.shape
    nblk = s // tm
    ngrp = len(POOL_WINDOWS)

    def body(dqn_ref, dkn_ref, dv_ref, q_ref, k_ref, dyp_ref, pooled_ref, x_ref, dh1_ref,
             win_hbm, pw_ref, ps_ref, qg_ref, kg_ref, g1_ref,
             dx_ref, dproj_ref, dg1_ref, dqg_ref, dkg_ref, dpw_ref, dps_ref, win_v, ebuf):
        step = pl.program_id(0)
        i = nblk - 1 - step

        @pl.when(step == 0)
        def _():
            pltpu.sync_copy(win_hbm, win_v)
            dg1_ref[...] = jnp.zeros(dg1_ref.shape, F32)
            dqg_ref[...] = jnp.zeros(dqg_ref.shape, F32)
            dkg_ref[...] = jnp.zeros(dkg_ref.shape, F32)
            dpw_ref[...] = jnp.zeros(dpw_ref.shape, F32)
            dps_ref[...] = jnp.zeros(dps_ref.shape, F32)
            ebuf[tm:tm + POOL_HALO, :] = jnp.zeros((POOL_HALO, POOL_WIDTH), F32)

        @pl.when(step > 0)
        def _():
            ebuf[tm:tm + POOL_HALO, :] = ebuf[0:POOL_HALO, :]

        def qk_bwd(dn_sum, raw, gain, scale, dgain_ref):
            rr = lax.rsqrt(_head_sum_bcast(raw * raw) * (1.0 / HEAD_DIM) + NORM_EPS)
            hn = raw * rr
            dgain_ref[...] += jnp.sum(dn_sum * hn, axis=0, keepdims=True) * scale
            dn = dn_sum * (gain * scale)
            return rr * (dn - hn * (_head_sum_bcast(dn * hn) * (1.0 / HEAD_DIM)))

        dq = qk_bwd(dqn_ref[...], q_ref[...], qg_ref[...], HEAD_DIM ** -0.5, dqg_ref)
        dk = qk_bwd(dkn_ref[...], k_ref[...], kg_ref[...], 1.0, dkg_ref)

        t = i * tm + lax.broadcasted_iota(jnp.int32, (tm, 1), 0)
        dpooled = []
        for g, w in enumerate(POOL_WINDOWS):
            ls = slice(g * LANES, (g + 1) * LANES)
            dm = dyp_ref[:, ls]
            pg = pooled_ref[:, ls]
            dps_ref[:, ls] += jnp.sum(dm * _mm(pg, pw_ref[g]), axis=0, keepdims=True)
            dms = (dm * ps_ref[:, ls]).astype(MXU_DTYPE)
            dpw_ref[g] += _mm_tn(pg, dms)
            dpg = _mm_nt(dms, pw_ref[g])
            dpooled.append(dpg)
            ebuf[0:tm, ls] = dpg / jnp.minimum(t + 1, w).astype(F32)
        du = []
        for g, w in enumerate(POOL_WINDOWS):
            ls = slice(g * LANES, (g + 1) * LANES)
            acc = ebuf[0:tm, ls]
            for sh in range(1, w):
                acc = acc + ebuf[sh:sh + tm, ls]
            du.append(acc - dpooled[g])
        parts = [jnp.concatenate(du, axis=-1), dq, dk, dv_ref[...]]
        da = jnp.zeros((tm, d), F32)
        for p, part in enumerate(parts):
            pc = part.astype(MXU_DTYPE)
            dproj_ref[:, p * POOL_WIDTH:(p + 1) * POOL_WIDTH] = pc
            da = da + _mm_nt(pc, win_v[p])
        xv = x_ref[...]
        r = lax.rsqrt(jnp.mean(xv * xv, axis=-1, keepdims=True) + NORM_EPS)
        xn = xv * r
        dg1_ref[...] += jnp.sum(da * xn, axis=0, keepdims=True)
        dx_ref[...] = dh1_ref[...] + _rms_bwd(da * g1_ref[...], xn, r)

    tok = lambda w: pl.BlockSpec((tm, w), lambda t: (nblk - 1 - t, 0))
    const = lambda shp: pl.BlockSpec(shp, lambda t: (0,) * len(shp))
    return pl.pallas_call(
        body, name="bwd_inproj",
        grid=(nblk,),
        in_specs=[tok(ATTN_WIDTH)] * 5 + [tok(POOL_WIDTH), tok(POOL_WIDTH), tok(d), tok(d),
                                          ANY, const(poolw.shape), const((1, POOL_WIDTH)), const((1, ATTN_WIDTH)),
                                          const((1, ATTN_WIDTH)), const((1, d))],
        out_specs=[tok(d), tok(4 * POOL_WIDTH), const((1, d)), const((1, ATTN_WIDTH)), const((1, ATTN_WIDTH)),
                   const((ngrp, LANES, LANES)), const((1, POOL_WIDTH))],
        out_shape=[jax.ShapeDtypeStruct((s, d), F32),
                   jax.ShapeDtypeStruct((s, 4 * POOL_WIDTH), MXU_DTYPE),
                   jax.ShapeDtypeStruct((1, d), F32),
                   jax.ShapeDtypeStruct((1, ATTN_WIDTH), F32),
                   jax.ShapeDtypeStruct((1, ATTN_WIDTH), F32),
                   jax.ShapeDtypeStruct((ngrp, LANES, LANES), F32),
                   jax.ShapeDtypeStruct((1, POOL_WIDTH), F32)],
        scratch_shapes=[pltpu.VMEM(win.shape, MXU_DTYPE), pltpu.VMEM((tm + POOL_HALO, POOL_WIDTH), F32)],
        compiler_params=_params(("arbitrary",)),
    )(dqn, dkn, dv, q32, k32, dypool, pooled, x, dh1, win, poolw, pscale, qg, kg, g1)


def _wgrad_call(a, b, bm, bn, bk, out_shape, out_block, out_index, name):
    s, m = a.shape
    _, n = b.shape
    nk = s // bk

    def body(a_ref, b_ref, o_ref):
        k = pl.program_id(2)

        @pl.when(k == 0)
        def _():
            o_ref[...] = jnp.zeros(o_ref.shape, F32)

        o_ref[...] += _mm_tn(a_ref[...].astype(MXU_DTYPE), b_ref[...].astype(MXU_DTYPE))

    return pl.pallas_call(
        body, name=name,
        grid=(m // bm, n // bn, nk),
        in_specs=[pl.BlockSpec((bk, bm), lambda i, j, k: (k, i)), pl.BlockSpec((bk, bn), lambda i, j, k: (k, j))],
        out_specs=pl.BlockSpec(out_block, out_index),
        out_shape=jax.ShapeDtypeStruct(out_shape, F32),
        compiler_params=_params(("arbitrary", "arbitrary", "arbitrary")),
    )(a, b)


def _local_grads(x, tgt, g1, win, poolw, pscale, qg, kg, rel_bias, wout, g2, wup, wdown):
    s, d = x.shape
    g1r, g2r = g1.reshape(1, d), g2.reshape(1, d)
    psr = pscale.reshape(1, POOL_WIDTH)
    qgr = jnp.tile(qg, N_HEADS).reshape(1, ATTN_WIDTH)
    kgr = jnp.tile(kg, N_HEADS).reshape(1, ATTN_WIDTH)
    pw_c = poolw.astype(MXU_DTYPE)
    buckets = jnp.asarray(_bucket_tables())
    bias = _bias_table_call(rel_bias, buckets)

    a, pooled, ypool, q32, k32, qn, kn, v = _f1_call(x, g1r, win, pw_c, psr, qgr, kgr, tm=512)
    o, lse = _attn_fwd_call(qn, kn, v, bias)
    mixed, c, ff, dz, dy, dh1, dypool, do, delta, dg2, loss = _f2_call(x, tgt, ypool, o, wout, wup, wdown, g2r, tm=256)
    dqn, dkn, dv, dbias = _attn_bwd_call(qn, kn, v, do, lse, delta, bias)
    dx, dproj, dg1, dqg, dkg, dpw, dps = _bproj_call(
        dqn, dkn, dv, q32, k32, dypool, pooled, x, dh1, win, pw_c, psr, qgr, kgr, g1r, tm=256)

    dff = ff.shape[1]
    nin = dproj.shape[1] // N_CHIPS
    g_in = _wgrad_call(a, dproj, d, nin, 512, (N_CHIPS, d, nin), (None, d, nin), lambda i, j, k: (j, 0, 0), "wgrad_in")
    g_out = _wgrad_call(mixed, dh1, d // N_CHIPS, d, 512, (N_CHIPS, d // N_CHIPS, d), (None, d // N_CHIPS, d),
                        lambda i, j, k: (i, 0, 0), "wgrad_out")
    g_up = _wgrad_call(c, dz, d, dff // N_CHIPS, 512, (N_CHIPS, d, dff // N_CHIPS), (None, d, dff // N_CHIPS),
                       lambda i, j, k: (j, 0, 0), "wgrad_up")
    g_down = _wgrad_call(ff, dy, dff // N_CHIPS, d, 512, (N_CHIPS, dff // N_CHIPS, d), (None, dff // N_CHIPS, d),
                         lambda i, j, k: (i, 0, 0), "wgrad_down")
    drb = _rel_bias_grad_call(dbias, buckets)
    small = dict(
        mix_norm_g=dg1.reshape(d), mlp_norm_g=dg2.reshape(d), pool_scale=dps.reshape(POOL_WIDTH),
        q_norm_g=dqg.reshape(ATTN_WIDTH), k_norm_g=dkg.reshape(ATTN_WIDTH),
        rel_bias=drb[:, :N_BUCKETS].T, pool_w=dpw)
    return loss[0, 0], dx, (g_in, g_out, g_up, g_down), small


def _coords():
    return lax.axis_index("x"), lax.axis_index("y"), lax.axis_index("c")


def _other_chips(x, y):
    return [(1 - x, y), (x, 1 - y), (1 - x, 1 - y)]


def _remote(src, dst, send_sem, recv_sem, dev):
    return pltpu.make_async_remote_copy(src_ref=src, dst_ref=dst, send_sem=send_sem, recv_sem=recv_sem,
                                        device_id=dev, device_id_type=MESH)


def _halves(a):
    return a.reshape(a.shape[:-2] + (2, a.shape[-2] // 2, a.shape[-1]))


def _place_shards_call(shards, chip_idx, nch):
    nw = len(shards)

    def body(chip_ref, *refs):
        for w in range(nw):
            refs[nw + w][...] = refs[w][...].astype(WIRE_DTYPE)

    in_specs = [pl.BlockSpec((s.shape[0] // nch, s.shape[1]), lambda i, chip_ref: (i, 0)) for s in shards]
    out_specs = [pl.BlockSpec((None, s.shape[0] // nch, s.shape[1]), lambda i, chip_ref: (chip_ref[0], i, 0))
                 for s in shards]
    return pl.pallas_call(
        body, name="weights_place",
        grid_spec=pltpu.PrefetchScalarGridSpec(num_scalar_prefetch=1, grid=(nch,),
                                               in_specs=in_specs, out_specs=out_specs),
        out_shape=[jax.ShapeDtypeStruct((N_CHIPS,) + s.shape, WIRE_DTYPE) for s in shards],
        compiler_params=_params(("arbitrary",)),
    )(chip_idx, *shards)


def _allgather_call(placed):
    nw = len(placed)
    ncp = 3 * nw

    def body(*refs):
        outs = refs[nw:2 * nw]
        send1, recv1, send2, recv2 = refs[2 * nw:]
        x, y, c = _coords()
        chip = 2 * x + y
        others = _other_chips(x, y)
        first, passed = [], []
        for w in range(nw):
            for k, (ox, oy) in enumerate(others):
                mine = outs[w].at[chip, c]
                cp = _remote(mine, mine, send1.at[3 * w + k], recv1.at[3 * w + k], (ox, oy, c))
                cp.start()
                first.append(cp)
        for w in range(nw):
            for k, (ox, oy) in enumerate(others):
                piece = outs[w].at[2 * ox + oy, c]
                _remote(piece, piece, send1.at[3 * w + k], recv1.at[3 * w + k], (ox, oy, c)).wait_recv()
                cp = _remote(piece, piece, send2.at[3 * w + k], recv2.at[3 * w + k], (x, y, 1 - c))
                cp.start()
                passed.append(cp)
        for w in range(nw):
            for k, (ox, oy) in enumerate(others):
                piece = outs[w].at[2 * ox + oy, 1 - c]
                _remote(piece, piece, send2.at[3 * w + k], recv2.at[3 * w + k], (x, y, 1 - c)).wait_recv()
        for cp in first + passed:
            cp.wait_send()

    split = [_halves(s) for s in placed]
    outs = pl.pallas_call(
        body, name="weights_allgather",
        in_specs=[ANY] * nw, out_specs=[ANY] * nw,
        out_shape=[jax.ShapeDtypeStruct(s.shape, s.dtype) for s in split],
        input_output_aliases={w: w for w in range(nw)},
        scratch_shapes=[pltpu.SemaphoreType.DMA((ncp,))] * 4,
    )(*split)
    return [o.reshape(s.shape) for o, s in zip(outs, placed)]


def _pair_exchange_call(grads, small):
    nw = len(grads)

    def body(*refs):
        ins, small_ref = refs[:nw], refs[nw]
        outs, gathered = refs[nw + 1:2 * nw + 1], refs[2 * nw + 1]
        send, recv, ssend, srecv, loc = refs[2 * nw + 2:]
        x, y, c = _coords()
        me = 4 * x + 2 * y + c
        own = pltpu.make_async_copy(small_ref, gathered.at[me], loc)
        own.start()
        sent = []
        for w in range(nw):
            cp = _remote(ins[w].at[:, 1 - c], outs[w], send.at[w], recv.at[w], (x, y, 1 - c))
            cp.start()
            sent.append(cp)
        flips = [(r >> 2 & 1, r >> 1 & 1, r & 1) for r in range(1, N_DEV)]
        peer = lambda f: ((1 - x) if f[0] else x, (1 - y) if f[1] else y, (1 - c) if f[2] else c)
        for r, f in enumerate(flips):
            cp = _remote(small_ref, gathered.at[me], ssend.at[r], srecv.at[r], peer(f))
            cp.start()
            sent.append(cp)
        for w in range(nw):
            _remote(outs[w], outs[w], send.at[w], recv.at[w], (x, y, 1 - c)).wait_recv()
        for r, f in enumerate(flips):
            px, py, pc = peer(f)
            slot = gathered.at[4 * px + 2 * py + pc]
            _remote(slot, slot, ssend.at[r], srecv.at[r], (px, py, pc)).wait_recv()
        for cp in sent:
            cp.wait_send()
        own.wait()

    return pl.pallas_call(
        body, name="grads_pair_exchange",
        in_specs=[ANY] * (nw + 1), out_specs=[ANY] * (nw + 1),
        out_shape=[jax.ShapeDtypeStruct((g.shape[0],) + g.shape[2:], g.dtype) for g in grads]
        + [jax.ShapeDtypeStruct((N_DEV,) + small.shape, small.dtype)],
        scratch_shapes=[pltpu.SemaphoreType.DMA((nw,)), pltpu.SemaphoreType.DMA((nw,)),
                        pltpu.SemaphoreType.DMA((N_DEV - 1,)), pltpu.SemaphoreType.DMA((N_DEV - 1,)),
                        pltpu.SemaphoreType.DMA],
    )(*grads, small)


def _pair_sum_call(grads, recvd, c_idx, nch):
    nw = len(grads)

    def body(c_ref, *refs):
        for w in range(nw):
            tot = refs[w][...] + refs[nw + w][...]
            refs[2 * nw + w][...] = tot
            refs[3 * nw + w][...] = tot.astype(WIRE_DTYPE)

    in_specs, out_specs, out_shape = [], [], []
    for g in grads:
        in_specs.append(pl.BlockSpec((None, None, g.shape[2] // nch, g.shape[3]), lambda j, i, c_ref: (j, c_ref[0], i, 0)))
    for g in grads:
        in_specs.append(pl.BlockSpec((None, g.shape[2] // nch, g.shape[3]), lambda j, i, c_ref: (j, i, 0)))
    for dt in (F32, WIRE_DTYPE):
        for g in grads:
            out_specs.append(pl.BlockSpec((None, g.shape[2] // nch, g.shape[3]), lambda j, i, c_ref: (j, i, 0)))
            out_shape.append(jax.ShapeDtypeStruct((g.shape[0],) + g.shape[2:], dt))
    res = pl.pallas_call(
        body, name="grads_pair_sum",
        grid_spec=pltpu.PrefetchScalarGridSpec(num_scalar_prefetch=1, grid=(N_CHIPS, nch),
                                               in_specs=in_specs, out_specs=out_specs),
        out_shape=out_shape,
        compiler_params=_params(("arbitrary", "arbitrary")),
    )(c_idx, *grads, *recvd)
    return res[:nw], res[nw:]


def _chip_exchange_call(sums):
    nw = len(sums)

    def body(*refs):
        ins, outs = refs[:nw], refs[nw:2 * nw]
        send, recv = refs[2 * nw:]
        x, y, c = _coords()
        others = _other_chips(x, y)
        sent = []
        for w in range(nw):
            for k, (ox, oy) in enumerate(others):
                cp = _remote(ins[w].at[2 * ox + oy], outs[w].at[k], send.at[3 * w + k], recv.at[3 * w + k], (ox, oy, c))
                cp.start()
                sent.append(cp)
        for w in range(nw):
            for k, (ox, oy) in enumerate(others):
                _remote(outs[w].at[k], outs[w].at[k], send.at[3 * w + k], recv.at[3 * w + k], (ox, oy, c)).wait_recv()
        for cp in sent:
            cp.wait_send()

    return pl.pallas_call(
        body, name="grads_chip_exchange",
        in_specs=[ANY] * nw, out_specs=[ANY] * nw,
        out_shape=[jax.ShapeDtypeStruct((3,) + s.shape[1:], s.dtype) for s in sums],
        scratch_shapes=[pltpu.SemaphoreType.DMA((3 * nw,)), pltpu.SemaphoreType.DMA((3 * nw,))],
    )(*sums)


def _chip_sum_call(own, recvd, chip_c_idx, nch):
    nw = len(own)

    def body(idx_ref, *refs):
        for w in range(nw):
            q = refs[nw + w]
            refs[2 * nw + w][...] = ((refs[w][...] + q[0].astype(F32)) + q[1].astype(F32)) + q[2].astype(F32)

    in_specs, out_specs, out_shape = [], [], []
    for s in own:
        in_specs.append(pl.BlockSpec((None, s.shape[1] // nch, s.shape[2]), lambda i, idx_ref: (idx_ref[0], i, 0)))
    for s in own:
        in_specs.append(pl.BlockSpec((3, s.shape[1] // nch, s.shape[2]), lambda i, idx_ref: (0, i, 0)))
    for s in own:
        out_specs.append(pl.BlockSpec((None, s.shape[1] // nch, s.shape[2]), lambda i, idx_ref: (idx_ref[1], i, 0)))
        out_shape.append(jax.ShapeDtypeStruct((2,) + s.shape[1:], F32))
    return pl.pallas_call(
        body, name="grads_chip_sum",
        grid_spec=pltpu.PrefetchScalarGridSpec(num_scalar_prefetch=1, grid=(nch,),
                                               in_specs=in_specs, out_specs=out_specs),
        out_shape=out_shape,
        compiler_params=_params(("arbitrary",)),
    )(chip_c_idx, *own, *recvd)


def _pair_allgather_call(halves):
    nw = len(halves)

    def body(*refs):
        outs = refs[nw:2 * nw]
        send, recv = refs[2 * nw:]
        x, y, c = _coords()
        cps = []
        for w in range(nw):
            cp = _remote(outs[w].at[c], outs[w].at[c], send.at[w], recv.at[w], (x, y, 1 - c))
            cp.start()
            cps.append(cp)
        for w in range(nw):
            theirs = outs[w].at[1 - c]
            _remote(theirs, theirs, send.at[w], recv.at[w], (x, y, 1 - c)).wait_recv()
        for cp in cps:
            cp.wait_send()

    outs = pl.pallas_call(
        body, name="grads_pair_allgather",
        in_specs=[ANY] * nw, out_specs=[ANY] * nw,
        out_shape=[jax.ShapeDtypeStruct(h.shape, h.dtype) for h in halves],
        input_output_aliases={w: w for w in range(nw)},
        scratch_shapes=[pltpu.SemaphoreType.DMA((nw,))] * 2,
    )(*halves)
    return [o.reshape(2 * h.shape[1], h.shape[2]) for o, h in zip(outs, halves)]


def _adamw(w, g, m, v):
    m = ADAM_B1 * m + (1.0 - ADAM_B1) * g
    v = ADAM_B2 * v + (1.0 - ADAM_B2) * (g * g)
    m_hat = m / (1.0 - ADAM_B1 ** ADAM_STEP)
    v_hat = v / (1.0 - ADAM_B2 ** ADAM_STEP)
    delta = -ADAM_LR * (m_hat / (jnp.sqrt(v_hat) + ADAM_EPS) + ADAM_WD * w)
    return delta, m, v


def _adamw_call(ws, gs, ms, vs, nch):
    nw = len(ws)

    def body(*refs):
        for w in range(nw):
            g = refs[nw + w][...]
            delta, m, v = _adamw(refs[w][...], g, refs[2 * nw + w][...], refs[3 * nw + w][...])
            refs[4 * nw + w][...] = g
            refs[5 * nw + w][...] = delta
            refs[6 * nw + w][...] = m
            refs[7 * nw + w][...] = v

    specs = [pl.BlockSpec((a.shape[0] // nch, a.shape[1]), lambda i: (i, 0)) for a in ws]
    res = pl.pallas_call(
        body, name="adamw_big",
        grid=(nch,),
        in_specs=specs * 4, out_specs=specs * 4,
        out_shape=[jax.ShapeDtypeStruct(a.shape, F32) for a in ws] * 4,
        compiler_params=_params(("arbitrary",)),
    )(*ws, *gs, *ms, *vs)
    return res[:nw], res[nw:2 * nw], res[2 * nw:3 * nw], res[3 * nw:]


def _small_call(gathered, w, m, v):
    def fold(row):
        tot = row[:, 0:LANES] + row[:, LANES:2 * LANES] + row[:, 2 * LANES:3 * LANES] + row[:, 3 * LANES:4 * LANES]
        return tot + pltpu.roll(tot, HEAD_DIM, axis=1)

    def body(ga_ref, w_ref, m_ref, v_ref, g_out, d_out, m_out, v_out):
        g = ga_ref[0]
        for i in range(1, N_DEV):
            g = g + ga_ref[i]
        unfolded = g[4:5, :]
        folded = jnp.concatenate([fold(unfolded[:, :ATTN_WIDTH]), fold(unfolded[:, ATTN_WIDTH:]),
                                  jnp.zeros((1, 1024 - 2 * LANES), F32)], axis=-1)
        row = lax.broadcasted_iota(jnp.int32, g.shape, 0)
        g = jnp.where(row == 3, folded, g)
        delta, mm, vv = _adamw(w_ref[...], g, m_ref[...], v_ref[...])
        g_out[...] = g
        d_out[...] = delta
        m_out[...] = mm
        v_out[...] = vv

    return pl.pallas_call(
        body, name="adamw_small",
        out_shape=[jax.ShapeDtypeStruct(w.shape, F32)] * 4,
        compiler_params=_params(),
    )(gathered, w, m, v)


def _pack_small(p, folded=True, loss=None):
    z = lambda n: jnp.zeros((n,), F32)
    rows = [p["mix_norm_g"], p["mlp_norm_g"],
            jnp.concatenate([p["pool_scale"], p["rel_bias"].reshape(-1), z(1024 - POOL_WIDTH - N_BUCKETS * N_HEADS)])]
    if folded:
        rows += [jnp.concatenate([p["q_norm_g"], z(LANES - HEAD_DIM), p["k_norm_g"], z(1024 - LANES - HEAD_DIM)]), z(1024)]
    else:
        rows += [z(1024), jnp.concatenate([p["q_norm_g"], p["k_norm_g"]])]
    rows += [z(1024) if loss is None else jnp.concatenate([loss.reshape(1), z(1023)])]
    head = jnp.stack(rows + [z(1024)] * 2)
    return jnp.concatenate([head, p["pool_w"].reshape(-1, 1024)], axis=0)


def _unpack_small(a):
    return dict(
        mix_norm_g=a[0], mlp_norm_g=a[1], pool_scale=a[2, :POOL_WIDTH],
        rel_bias=a[2, POOL_WIDTH:POOL_WIDTH + N_BUCKETS * N_HEADS].reshape(N_BUCKETS, N_HEADS),
        q_norm_g=a[3, :HEAD_DIM], k_norm_g=a[3, LANES:LANES + HEAD_DIM],
        pool_w=a[8:].reshape(len(POOL_WINDOWS), LANES, LANES))


_WEIGHT_ORDER = ("mix_norm_g", "w_in", "pool_w", "pool_scale", "q_norm_g", "k_norm_g", "rel_bias", "w_out",
                 "mlp_norm_g", "w_up", "w_down")
_BIG = ("w_in", "w_out", "w_up", "w_down")


def kernel(x, mix_norm_g, w_in, pool_w, pool_scale, q_norm_g, k_norm_g, rel_bias, w_out, mlp_norm_g, w_up, w_down, loss_target, m_mix_norm_g, m_w_in, m_pool_w, m_pool_scale, m_q_norm_g, m_k_norm_g, m_rel_bias, m_w_out, m_mlp_norm_g, m_w_up, m_w_down, v_mix_norm_g, v_w_in, v_pool_w, v_pool_scale, v_q_norm_g, v_k_norm_g, v_rel_bias, v_w_out, v_mlp_norm_g, v_w_up, v_w_down):
    w = dict(mix_norm_g=mix_norm_g, w_in=w_in, pool_w=pool_w, pool_scale=pool_scale, q_norm_g=q_norm_g,
             k_norm_g=k_norm_g, rel_bias=rel_bias, w_out=w_out, mlp_norm_g=mlp_norm_g, w_up=w_up, w_down=w_down)
    m = dict(mix_norm_g=m_mix_norm_g, w_in=m_w_in, pool_w=m_pool_w, pool_scale=m_pool_scale, q_norm_g=m_q_norm_g,
             k_norm_g=m_k_norm_g, rel_bias=m_rel_bias, w_out=m_w_out, mlp_norm_g=m_mlp_norm_g, w_up=m_w_up, w_down=m_w_down)
    v = dict(mix_norm_g=v_mix_norm_g, w_in=v_w_in, pool_w=v_pool_w, pool_scale=v_pool_scale, q_norm_g=v_q_norm_g,
             k_norm_g=v_k_norm_g, rel_bias=v_rel_bias, w_out=v_w_out, mlp_norm_g=v_mlp_norm_g, w_up=v_w_up, w_down=v_w_down)
    xc, yc, cc = _coords()

    c_idx = jnp.reshape(cc, (1,)).astype(jnp.int32)
    chip_idx = jnp.reshape(2 * xc + yc, (1,)).astype(jnp.int32)
    win_f, wout_f, wup_f, wdown_f = _allgather_call(_place_shards_call([w[n] for n in _BIG], chip_idx, nch=4))
    loss_part, dx, big_grads, small_grads = _local_grads(
        x[0], loss_target[0], mix_norm_g, win_f, pool_w, pool_scale, q_norm_g, k_norm_g, rel_bias,
        wout_f.reshape(wout_f.shape[0] * wout_f.shape[1], wout_f.shape[2]), mlp_norm_g, wup_f, wdown_f)

    big_halves = [_halves(g) for g in big_grads]
    *from_sibling, small_all = _pair_exchange_call(big_halves, _pack_small(small_grads, folded=False, loss=loss_part))
    sums32, sums_wire = _pair_sum_call(big_halves, from_sibling, c_idx, nch=2)
    from_chips = _chip_exchange_call(list(sums_wire))
    halves = _chip_sum_call(list(sums32), from_chips, jnp.concatenate([chip_idx, c_idx]), nch=2)
    g_reduced = _pair_allgather_call(list(halves))
    g_big, d_big, m_big, v_big = _adamw_call(
        [w[n] for n in _BIG], g_reduced, [m[n] for n in _BIG], [v[n] for n in _BIG], nch=8)
    g_pack, d_pack, m_pack, v_pack = _small_call(small_all, _pack_small(w), _pack_small(m), _pack_small(v))

    grads, deltas, new_m, new_v = (_unpack_small(a) for a in (g_pack, d_pack, m_pack, v_pack))
    for i, n in enumerate(_BIG):
        grads[n], deltas[n], new_m[n], new_v[n] = g_big[i], d_big[i], m_big[i], v_big[i]
    loss = g_pack[LOSS_ROW, 0]
    return (loss, dx[None], *[grads[n] for n in _WEIGHT_ORDER], *[deltas[n] for n in _WEIGHT_ORDER],
            *[new_m[n] for n in _WEIGHT_ORDER], *[new_v[n] for n in _WEIGHT_ORDER])
```

```python
import math

import jax
import jax.numpy as jnp
import numpy as np
from jax import lax
from jax.experimental import pallas as pl
from jax.experimental.pallas import tpu as pltpu

F32 = jnp.float32
MXU_DTYPE = jnp.bfloat16
WIRE_DTYPE = jnp.bfloat16

NORM_EPS = 1e-6
NEG_INF = -1e30
LANES = 128
HEAD_DIM = 64
N_HEADS = 8
POOL_WIDTH = 512
ATTN_WIDTH = 512
POOL_WINDOWS = (2, 4, 8, 16)
POOL_HALO = 16
DILATED_PATTERNS = ((128, 1), (512, 4), (2048, 16))
ATT_BLOCK = 128
ATT_SUPER = ATT_BLOCK * max(dl for _, dl in DILATED_PATTERNS)
ATT_UNITS = ATT_SUPER // ATT_BLOCK
N_BUCKETS = 32
MAX_DISTANCE = 2048
N_CHIPS = 4
N_DEV = 8
ADAM_LR, ADAM_B1, ADAM_B2, ADAM_EPS, ADAM_WD, ADAM_STEP = 0.001, 0.9, 0.999, 1e-08, 0.01, 10
VMEM_LIMIT = 56 * 1024 * 1024
MESH = pl.DeviceIdType.MESH
ANY = pl.BlockSpec(memory_space=pl.ANY)

SMALL_ROWS = 72
LOSS_ROW = 5


def _mm(a, b):
    return jnp.dot(a, b, preferred_element_type=F32)


def _mm_nt(a, b):
    return lax.dot_general(a, b, (((1,), (1,)), ((), ())), preferred_element_type=F32)


def _mm_tn(a, b):
    return lax.dot_general(a, b, (((0,), (0,)), ((), ())), preferred_element_type=F32)


def _params(sem=None, **kw):
    if sem is not None:
        kw["dimension_semantics"] = sem
    return pltpu.CompilerParams(vmem_limit_bytes=VMEM_LIMIT, **kw)


def _low_half():
    return lax.broadcasted_iota(jnp.int32, (1, LANES), 1) < HEAD_DIM


def _head_sum_bcast(y):
    lo = _low_half()
    outs = []
    for j in range(y.shape[1] // LANES):
        c = y[:, j * LANES:(j + 1) * LANES]
        s_lo = jnp.sum(jnp.where(lo, c, 0.0), axis=-1, keepdims=True)
        s_hi = jnp.sum(jnp.where(lo, 0.0, c), axis=-1, keepdims=True)
        outs.append(jnp.where(lo, s_lo, s_hi))
    return jnp.concatenate(outs, axis=-1)


def _rms_bwd(dn, hn, r):
    return r * (dn - hn * jnp.mean(dn * hn, axis=-1, keepdims=True))


def _t5_bucket_np(dist):
    max_exact = N_BUCKETS // 2
    d_f = np.maximum(dist, 1).astype(np.float32)
    ratio = (np.log(d_f / np.float32(max_exact)) / np.float32(math.log(MAX_DISTANCE / max_exact))).astype(np.float32)
    large = max_exact + (ratio * np.float32(N_BUCKETS - max_exact)).astype(np.int32)
    large = np.minimum(large, N_BUCKETS - 1)
    return np.where(dist < max_exact, dist, large).astype(np.int32)


def _bucket_tables():
    qq = np.arange(ATT_BLOCK)[:, None]
    kk = np.arange(2 * ATT_BLOCK)[None, :]
    dist = np.clip(qq + ATT_BLOCK - kk, 0, ATT_BLOCK)
    return np.stack([_t5_bucket_np(dist * dl) for (_, dl) in DILATED_PATTERNS])


def _f1_call(x, g1, win, poolw, pscale, qg, kg, tm):
    s, d = x.shape
    nblk = s // tm

    def body(x_ref, g1_ref, win_ref, pw_ref, ps_ref, qg_ref, kg_ref,
             a_ref, pooled_ref, ypool_ref, q32_ref, k32_ref, qn_ref, kn_ref, v_ref, ubuf):
        i = pl.program_id(0)
        xv = x_ref[...]
        r = lax.rsqrt(jnp.mean(xv * xv, axis=-1, keepdims=True) + NORM_EPS)
        a = ((xv * r) * g1_ref[...]).astype(MXU_DTYPE)
        a_ref[...] = a
        u = _mm(a, win_ref[0])
        q = _mm(a, win_ref[1])
        k = _mm(a, win_ref[2])
        v_ref[...] = _mm(a, win_ref[3]).astype(MXU_DTYPE)
        q32_ref[...] = q
        k32_ref[...] = k
        rq = lax.rsqrt(_head_sum_bcast(q * q) * (1.0 / HEAD_DIM) + NORM_EPS)
        qn_ref[...] = (((q * rq) * qg_ref[...]) * (HEAD_DIM ** -0.5)).astype(MXU_DTYPE)
        rk = lax.rsqrt(_head_sum_bcast(k * k) * (1.0 / HEAD_DIM) + NORM_EPS)
        kn_ref[...] = ((k * rk) * kg_ref[...]).astype(MXU_DTYPE)

        @pl.when(i == 0)
        def _():
            ubuf[0:POOL_HALO, :] = jnp.zeros((POOL_HALO, POOL_WIDTH), F32)

        @pl.when(i > 0)
        def _():
            ubuf[0:POOL_HALO, :] = ubuf[tm:tm + POOL_HALO, :]

        ubuf[POOL_HALO:POOL_HALO + tm, :] = u
        t = i * tm + lax.broadcasted_iota(jnp.int32, (tm, 1), 0)
        for g, w in enumerate(POOL_WINDOWS):
            ls = slice(g * LANES, (g + 1) * LANES)
            ug = u[:, ls]
            acc = ug
            for sh in range(1, w):
                acc = acc + ubuf[POOL_HALO - sh:POOL_HALO - sh + tm, ls]
            cnt = jnp.minimum(t + 1, w).astype(F32)
            pooled = (acc / cnt - ug).astype(MXU_DTYPE)
            pooled_ref[:, ls] = pooled
            ypool_ref[:, ls] = (_mm(pooled, pw_ref[g]) * ps_ref[:, ls]).astype(MXU_DTYPE)

    tok = lambda w: pl.BlockSpec((tm, w), lambda i: (i, 0))
    full = lambda shp: pl.BlockSpec(shp, lambda i: (0,) * len(shp))
    return pl.pallas_call(
        body, name="fwd_inproj",
        grid=(nblk,),
        in_specs=[tok(d), full((1, d)), full(win.shape), full(poolw.shape), full((1, POOL_WIDTH)),
                  full((1, ATTN_WIDTH)), full((1, ATTN_WIDTH))],
        out_specs=[tok(d), tok(POOL_WIDTH), tok(POOL_WIDTH), tok(ATTN_WIDTH), tok(ATTN_WIDTH),
                   tok(ATTN_WIDTH), tok(ATTN_WIDTH), tok(ATTN_WIDTH)],
        out_shape=[jax.ShapeDtypeStruct((s, d), MXU_DTYPE),
                   jax.ShapeDtypeStruct((s, POOL_WIDTH), MXU_DTYPE),
                   jax.ShapeDtypeStruct((s, POOL_WIDTH), MXU_DTYPE),
                   jax.ShapeDtypeStruct((s, ATTN_WIDTH), F32),
                   jax.ShapeDtypeStruct((s, ATTN_WIDTH), F32),
                   jax.ShapeDtypeStruct((s, ATTN_WIDTH), MXU_DTYPE),
                   jax.ShapeDtypeStruct((s, ATTN_WIDTH), MXU_DTYPE),
                   jax.ShapeDtypeStruct((s, ATTN_WIDTH), MXU_DTYPE)],
        scratch_shapes=[pltpu.VMEM((tm + POOL_HALO, POOL_WIDTH), F32)],
        compiler_params=_params(("arbitrary",)),
    )(x, g1, win, poolw, pscale, qg, kg)


def _band_mask(n):
    qq = lax.broadcasted_iota(jnp.int32, (ATT_BLOCK, 2 * ATT_BLOCK), 0)
    kk = lax.broadcasted_iota(jnp.int32, (ATT_BLOCK, 2 * ATT_BLOCK), 1)
    dist = qq + ATT_BLOCK - kk
    return (dist >= 0) & (dist <= ATT_BLOCK) & ((n > 0) | (kk >= ATT_BLOCK))


def _unit_rows(u, dl):
    r = u % dl
    b = u // dl
    q0 = r + dl * ATT_BLOCK * b
    return b, pl.ds(q0, ATT_BLOCK, stride=dl), pl.ds(ATT_SUPER + q0 - dl * ATT_BLOCK, 2 * ATT_BLOCK, stride=dl)


def _attn_fwd_call(qn, kn, v, bias):
    s, w = qn.shape
    nsb = s // ATT_SUPER
    npair = w // LANES

    def body(q_ref, kc_ref, kp_ref, vc_ref, vp_ref, b_ref, o_ref, lse_ref, qf, kf, vf, acc_s, m_s, l_s):
        sb = pl.program_id(1)
        qf[...] = q_ref[...].astype(F32)
        kf[0:ATT_SUPER, :] = kp_ref[...].astype(F32)
        kf[ATT_SUPER:, :] = kc_ref[...].astype(F32)
        vf[0:ATT_SUPER, :] = vp_ref[...].astype(F32)
        vf[ATT_SUPER:, :] = vc_ref[...].astype(F32)
        lo = _low_half()
        for p, (_, dl) in enumerate(DILATED_PATTERNS):
            def unit(u, carry, p=p, dl=dl):
                b, rows_q, rows_k = _unit_rows(u, dl)
                valid = _band_mask(sb * (ATT_UNITS // dl) + b)
                qp = qf[rows_q, :].astype(MXU_DTYPE)
                kcat = kf[rows_k, :].astype(MXU_DTYPE)
                vcat = vf[rows_k, :].astype(MXU_DTYPE)
                zero = jnp.zeros_like(qp)
                q2 = jnp.concatenate([jnp.where(lo, qp, zero), jnp.where(lo, zero, qp)], axis=0)
                sc = _mm_nt(q2, kcat) + b_ref[p].reshape(2 * ATT_BLOCK, 2 * ATT_BLOCK)
                sc = jnp.where(jnp.concatenate([valid, valid], axis=0), sc, NEG_INF)
                m2 = jnp.max(sc, axis=-1, keepdims=True)
                pr = jnp.exp(sc - m2)
                l2 = jnp.sum(pr, axis=-1, keepdims=True)
                acc2 = _mm(pr.astype(MXU_DTYPE), vcat)
                acc = jnp.where(lo, acc2[:ATT_BLOCK], acc2[ATT_BLOCK:])
                m = jnp.where(lo, m2[:ATT_BLOCK], m2[ATT_BLOCK:])
                l = jnp.where(lo, l2[:ATT_BLOCK], l2[ATT_BLOCK:])
                if p == 0:
                    acc_s[rows_q, :] = acc
                    m_s[rows_q, :] = m
                    l_s[rows_q, :] = l
                else:
                    m_old = m_s[rows_q, :]
                    m_new = jnp.maximum(m_old, m)
                    a_old = jnp.exp(m_old - m_new)
                    a_new = jnp.exp(m - m_new)
                    acc_s[rows_q, :] = a_old * acc_s[rows_q, :] + a_new * acc
                    l_s[rows_q, :] = a_old * l_s[rows_q, :] + a_new * l
                    m_s[rows_q, :] = m_new
                return carry

            lax.fori_loop(0, ATT_UNITS, unit, 0, unroll=4)
        l = l_s[...]
        o_ref[...] = acc_s[...] / l
        lse_ref[...] = m_s[...] + jnp.log(l)

    cur = pl.BlockSpec((ATT_SUPER, LANES), lambda j, t: (t, j))
    prev = pl.BlockSpec((ATT_SUPER, LANES), lambda j, t: (jnp.maximum(t - 1, 0), j))
    bspec = pl.BlockSpec((len(DILATED_PATTERNS), 2, ATT_BLOCK, 2 * ATT_BLOCK), lambda j, t: (0, j, 0, 0))
    return pl.pallas_call(
        body, name="attn_fwd",
        grid=(npair, nsb),
        in_specs=[cur, cur, prev, cur, prev, bspec],
        out_specs=[cur, cur],
        out_shape=[jax.ShapeDtypeStruct((s, w), F32), jax.ShapeDtypeStruct((s, w), F32)],
        scratch_shapes=[pltpu.VMEM((ATT_SUPER, LANES), F32), pltpu.VMEM((2 * ATT_SUPER, LANES), F32),
                        pltpu.VMEM((2 * ATT_SUPER, LANES), F32), pltpu.VMEM((ATT_SUPER, LANES), F32),
                        pltpu.VMEM((ATT_SUPER, LANES), F32), pltpu.VMEM((ATT_SUPER, LANES), F32)],
        compiler_params=_params(("arbitrary", "arbitrary")),
    )(qn, kn, kn, v, v, bias)


def _attn_bwd_call(qn, kn, v, do, lse, delta, bias):
    s, w = qn.shape
    nsb = s // ATT_SUPER
    npair = w // LANES

    def body(q_ref, kc_ref, kp_ref, vc_ref, vp_ref, do_ref, lse_ref, dlt_ref, b_ref,
             dq_ref, dk_ref, dv_ref, db_ref, qf, kf, vf, dof, dkf, dvf):
        step = pl.program_id(1)
        sb = nsb - 1 - step
        qf[...] = q_ref[...].astype(F32)
        dof[...] = do_ref[...].astype(F32)
        kf[0:ATT_SUPER, :] = kp_ref[...].astype(F32)
        kf[ATT_SUPER:, :] = kc_ref[...].astype(F32)
        vf[0:ATT_SUPER, :] = vp_ref[...].astype(F32)
        vf[ATT_SUPER:, :] = vc_ref[...].astype(F32)

        @pl.when(step == 0)
        def _():
            db_ref[...] = jnp.zeros(db_ref.shape, F32)
            dkf[ATT_SUPER:, :] = jnp.zeros((ATT_SUPER, LANES), F32)
            dvf[ATT_SUPER:, :] = jnp.zeros((ATT_SUPER, LANES), F32)

        @pl.when(step > 0)
        def _():
            dkf[ATT_SUPER:, :] = dkf[0:ATT_SUPER, :]
            dvf[ATT_SUPER:, :] = dvf[0:ATT_SUPER, :]

        dkf[0:ATT_SUPER, :] = jnp.zeros((ATT_SUPER, LANES), F32)
        dvf[0:ATT_SUPER, :] = jnp.zeros((ATT_SUPER, LANES), F32)
        lo = _low_half()
        for p, (_, dl) in enumerate(DILATED_PATTERNS):
            def unit(u, carry, p=p, dl=dl):
                b, rows_q, rows_k = _unit_rows(u, dl)
                valid = _band_mask(sb * (ATT_UNITS // dl) + b)
                qp = qf[rows_q, :].astype(MXU_DTYPE)
                dop = dof[rows_q, :].astype(MXU_DTYPE)
                kcat = kf[rows_k, :].astype(MXU_DTYPE)
                vcat = vf[rows_k, :].astype(MXU_DTYPE)
                lse2 = lse_ref[rows_q, :]
                dlt2 = dlt_ref[rows_q, :]
                zero = jnp.zeros_like(qp)
                q2 = jnp.concatenate([jnp.where(lo, qp, zero), jnp.where(lo, zero, qp)], axis=0)
                do2 = jnp.concatenate([jnp.where(lo, dop, zero), jnp.where(lo, zero, dop)], axis=0)
                lse_c = jnp.concatenate([lse2[:, 0:1], lse2[:, HEAD_DIM:HEAD_DIM + 1]], axis=0)
                dlt_c = jnp.concatenate([dlt2[:, 0:1], dlt2[:, HEAD_DIM:HEAD_DIM + 1]], axis=0)
                sc = _mm_nt(q2, kcat) + b_ref[p].reshape(2 * ATT_BLOCK, 2 * ATT_BLOCK)
                pr = jnp.where(jnp.concatenate([valid, valid], axis=0), jnp.exp(sc - lse_c), 0.0)
                ds = pr * (_mm_nt(do2, vcat) - dlt_c)
                db_ref[p] += ds.reshape(2, ATT_BLOCK, 2 * ATT_BLOCK)
                ds_c = ds.astype(MXU_DTYPE)
                dq2 = _mm(ds_c, kcat)
                dk = _mm_tn(ds_c, q2)
                dv = _mm_tn(pr.astype(MXU_DTYPE), do2)
                dq = jnp.where(lo, dq2[:ATT_BLOCK], dq2[ATT_BLOCK:])
                if p == 0:
                    dq_ref[rows_q, :] = dq
                else:
                    dq_ref[rows_q, :] += dq
                dkf[rows_k, :] += dk
                dvf[rows_k, :] += dv
                return carry

            lax.fori_loop(0, ATT_UNITS, unit, 0, unroll=4)
        dk_ref[...] = dkf[ATT_SUPER:, :]
        dv_ref[...] = dvf[ATT_SUPER:, :]

    cur = pl.BlockSpec((ATT_SUPER, LANES), lambda j, t: (nsb - 1 - t, j))
    prev = pl.BlockSpec((ATT_SUPER, LANES), lambda j, t: (jnp.maximum(nsb - 2 - t, 0), j))
    bshape = (len(DILATED_PATTERNS), 2, ATT_BLOCK, 2 * ATT_BLOCK)
    bspec = pl.BlockSpec(bshape, lambda j, t: (0, j, 0, 0))
    sup = lambda: pltpu.VMEM((ATT_SUPER, LANES), F32)
    sup2 = lambda: pltpu.VMEM((2 * ATT_SUPER, LANES), F32)
    return pl.pallas_call(
        body, name="attn_bwd",
        grid=(npair, nsb),
        in_specs=[cur, cur, prev, cur, prev, cur, cur, cur, bspec],
        out_specs=[cur, cur, cur, bspec],
        out_shape=[jax.ShapeDtypeStruct((s, w), F32)] * 3 + [jax.ShapeDtypeStruct(bias.shape, F32)],
        scratch_shapes=[sup(), sup2(), sup2(), sup(), sup2(), sup2()],
        compiler_params=_params(("arbitrary", "arbitrary")),
    )(qn, kn, kn, v, v, do, lse, delta, bias)


def _bias_table_call(rel_bias, buckets):
    npat = buckets.shape[0]

    def body(rb_ref, bk_ref, out_ref):
        for p in range(npat):
            for half in range(2):
                ks = slice(half * ATT_BLOCK, (half + 1) * ATT_BLOCK)
                bk = bk_ref[p, :, ks]
                for h in range(N_HEADS):
                    def pick(b, acc, h=h, bk=bk):
                        return jnp.where(bk == b, rb_ref[b, h], acc)

                    out_ref[p, h, :, ks] = lax.fori_loop(0, N_BUCKETS, pick, jnp.zeros((ATT_BLOCK, ATT_BLOCK), F32))

    return pl.pallas_call(
        body, name="bias_table",
        in_specs=[pl.BlockSpec(memory_space=pltpu.SMEM), pl.BlockSpec(memory_space=pltpu.VMEM)],
        out_shape=jax.ShapeDtypeStruct((npat, N_HEADS, ATT_BLOCK, 2 * ATT_BLOCK), F32),
        compiler_params=_params(),
    )(rel_bias, buckets)


def _rel_bias_grad_call(dbias, buckets):
    npat, nh = dbias.shape[0], dbias.shape[1]

    def body(db_ref, bk_ref, out_ref):
        lane = lax.broadcasted_iota(jnp.int32, (nh, LANES), 1)
        out = jnp.zeros((nh, LANES), F32)
        for b in range(N_BUCKETS):
            tot = jnp.zeros((nh, 1), F32)
            for p in range(npat):
                hit = jnp.where(bk_ref[p][None] == b, db_ref[p], 0.0)
                tot = tot + jnp.sum(jnp.sum(hit, axis=2), axis=1, keepdims=True)
            out = jnp.where(lane == b, tot, out)
        out_ref[...] = out

    return pl.pallas_call(
        body, name="rel_bias_grad",
        out_shape=jax.ShapeDtypeStruct((nh, LANES), F32),
        compiler_params=_params(),
    )(dbias, buckets)


def _f2_call(x, tgt, ypool, o, wout, wup, wdown, g2, tm):
    s, d = x.shape
    nblk = s // tm
    nch, _, fch = wup.shape
    dff = nch * fch
    mixw = POOL_WIDTH + ATTN_WIDTH

    def body(x_ref, t_ref, yp_ref, o_ref, g2_ref, wout_hbm, wup_hbm, wdown_hbm,
             mixed_ref, c_ref, ff_ref, dz_ref, dy_ref, dh1_ref, dyp_ref, do_ref, dlt_ref, dg2_ref, loss_ref,
             wout_v, wup_v, wdown_v, rz):
        i = pl.program_id(0)

        @pl.when(i == 0)
        def _():
            pltpu.sync_copy(wout_hbm, wout_v)
            pltpu.sync_copy(wup_hbm, wup_v)
            pltpu.sync_copy(wdown_hbm, wdown_v)
            dg2_ref[...] = jnp.zeros(dg2_ref.shape, F32)
            loss_ref[...] = jnp.zeros(loss_ref.shape, F32)

        o = o_ref[...]
        mixed = jnp.concatenate([yp_ref[...], o.astype(MXU_DTYPE)], axis=-1)
        mixed_ref[...] = mixed
        h1 = x_ref[...] + _mm(mixed, wout_v[...])
        r2 = lax.rsqrt(jnp.mean(h1 * h1, axis=-1, keepdims=True) + NORM_EPS)
        hn = h1 * r2
        c = (hn * g2_ref[...]).astype(MXU_DTYPE)
        c_ref[...] = c
        y = h1
        for j in range(nch):
            cs = slice(j * fch, (j + 1) * fch)
            z = jnp.maximum(_mm(c, wup_v[j]), 0.0)
            rz[:, cs] = z
            ff = (z * z).astype(MXU_DTYPE)
            ff_ref[:, cs] = ff
            y = y + _mm(ff, wdown_v[j])
        err = y - t_ref[...]
        loss_ref[...] += jnp.sum(err * err) * (0.5 / d)
        dy = err * (1.0 / d)
        dy_c = dy.astype(MXU_DTYPE)
        dy_ref[...] = dy_c
        dc = jnp.zeros((tm, d), F32)
        for j in range(nch):
            cs = slice(j * fch, (j + 1) * fch)
            dz = (_mm_nt(dy_c, wdown_v[j]) * (2.0 * rz[:, cs])).astype(MXU_DTYPE)
            dz_ref[:, cs] = dz
            dc = dc + _mm_nt(dz, wup_v[j])
        dg2_ref[...] += jnp.sum(dc * hn, axis=0, keepdims=True)
        dh1 = dy + _rms_bwd(dc * g2_ref[...], hn, r2)
        dh1_ref[...] = dh1
        dmix = _mm_nt(dh1.astype(MXU_DTYPE), wout_v[...])
        dyp_ref[...] = dmix[:, :POOL_WIDTH]
        do = dmix[:, POOL_WIDTH:]
        do_ref[...] = do.astype(MXU_DTYPE)
        dlt_ref[...] = _head_sum_bcast(do * o)

    tok = lambda w: pl.BlockSpec((tm, w), lambda i: (i, 0))
    const = lambda shp: pl.BlockSpec(shp, lambda i: (0,) * len(shp))
    return pl.pallas_call(
        body, name="fwd_mlp_bwd_mlp",
        grid=(nblk,),
        in_specs=[tok(d), tok(d), tok(POOL_WIDTH), tok(ATTN_WIDTH), const((1, d)), ANY, ANY, ANY],
        out_specs=[tok(mixw), tok(d), tok(dff), tok(dff), tok(d), tok(d), tok(POOL_WIDTH), tok(ATTN_WIDTH),
                   tok(ATTN_WIDTH), const((1, d)), const((1, LANES))],
        out_shape=[jax.ShapeDtypeStruct((s, mixw), MXU_DTYPE),
                   jax.ShapeDtypeStruct((s, d), MXU_DTYPE),
                   jax.ShapeDtypeStruct((s, dff), MXU_DTYPE),
                   jax.ShapeDtypeStruct((s, dff), MXU_DTYPE),
                   jax.ShapeDtypeStruct((s, d), MXU_DTYPE),
                   jax.ShapeDtypeStruct((s, d), F32),
                   jax.ShapeDtypeStruct((s, POOL_WIDTH), F32),
                   jax.ShapeDtypeStruct((s, ATTN_WIDTH), MXU_DTYPE),
                   jax.ShapeDtypeStruct((s, ATTN_WIDTH), F32),
                   jax.ShapeDtypeStruct((1, d), F32),
                   jax.ShapeDtypeStruct((1, LANES), F32)],
        scratch_shapes=[pltpu.VMEM(wout.shape, MXU_DTYPE), pltpu.VMEM(wup.shape, MXU_DTYPE),
                        pltpu.VMEM(wdown.shape, MXU_DTYPE), pltpu.VMEM((tm, dff), F32)],
        compiler_params=_params(("arbitrary",)),
    )(x, tgt, ypool, o, g2, wout, wup, wdown)


def _bproj_call(dqn, dkn, dv, q32, k32, dypool, pooled, x, dh1, win, poolw, pscale, qg, kg, g1, tm):
    s, d = x.shape
    nblk = s // tm
    ngrp = len(POOL_WINDOWS)

    def body(dqn_ref, dkn_ref, dv_ref, q_ref, k_ref, dyp_ref, pooled_ref, x_ref, dh1_ref,
             win_hbm, pw_ref, ps_ref, qg_ref, kg_ref, g1_ref,
             dx_ref, dproj_ref, dg1_ref, dqg_ref, dkg_ref, dpw_ref, dps_ref, win_v, ebuf):
        step = pl.program_id(0)
        i = nblk - 1 - step

        @pl.when(step == 0)
        def _():
            pltpu.sync_copy(win_hbm, win_v)
            dg1_ref[...] = jnp.zeros(dg1_ref.shape, F32)
            dqg_ref[...] = jnp.zeros(dqg_ref.shape, F32)
            dkg_ref[...] = jnp.zeros(dkg_ref.shape, F32)
            dpw_ref[...] = jnp.zeros(dpw_ref.shape, F32)
            dps_ref[...] = jnp.zeros(dps_ref.shape, F32)
            ebuf[tm:tm + POOL_HALO, :] = jnp.zeros((POOL_HALO, POOL_WIDTH), F32)

        @pl.when(step > 0)
        def _():
            ebuf[tm:tm + POOL_HALO, :] = ebuf[0:POOL_HALO, :]

        def qk_bwd(dn_sum, raw, gain, scale, dgain_ref):
            rr = lax.rsqrt(_head_sum_bcast(raw * raw) * (1.0 / HEAD_DIM) + NORM_EPS)
            hn = raw * rr
            dgain_ref[...] += jnp.sum(dn_sum * hn, axis=0, keepdims=True) * scale
            dn = dn_sum * (gain * scale)
            return rr * (dn - hn * (_head_sum_bcast(dn * hn) * (1.0 / HEAD_DIM)))

        dq = qk_bwd(dqn_ref[...], q_ref[...], qg_ref[...], HEAD_DIM ** -0.5, dqg_ref)
        dk = qk_bwd(dkn_ref[...], k_ref[...], kg_ref[...], 1.0, dkg_ref)

        t = i * tm + lax.broadcasted_iota(jnp.int32, (tm, 1), 0)
        dpooled = []
        for g, w in enumerate(POOL_WINDOWS):
            ls = slice(g * LANES, (g + 1) * LANES)
            dm = dyp_ref[:, ls]
            pg = pooled_ref[:, ls]
            dps_ref[:, ls] += jnp.sum(dm * _mm(pg, pw_ref[g]), axis=0, keepdims=True)
            dms = (dm * ps_ref[:, ls]).astype(MXU_DTYPE)
            dpw_ref[g] += _mm_tn(pg, dms)
            dpg = _mm_nt(dms, pw_ref[g])
            dpooled.append(dpg)
            ebuf[0:tm, ls] = dpg / jnp.minimum(t + 1, w).astype(F32)
        du = []
        for g, w in enumerate(POOL_WINDOWS):
            ls = slice(g * LANES, (g + 1) * LANES)
            acc = ebuf[0:tm, ls]
            for sh in range(1, w):
                acc = acc + ebuf[sh:sh + tm, ls]
            du.append(acc - dpooled[g])
        parts = [jnp.concatenate(du, axis=-1), dq, dk, dv_ref[...]]
        da = jnp.zeros((tm, d), F32)
        for p, part in enumerate(parts):
            pc = part.astype(MXU_DTYPE)
            dproj_ref[:, p * POOL_WIDTH:(p + 1) * POOL_WIDTH] = pc
            da = da + _mm_nt(pc, win_v[p])
        xv = x_ref[...]
        r = lax.rsqrt(jnp.mean(xv * xv, axis=-1, keepdims=True) + NORM_EPS)
        xn = xv * r
        dg1_ref[...] += jnp.sum(da * xn, axis=0, keepdims=True)
        dx_ref[...] = dh1_ref[...] + _rms_bwd(da * g1_ref[...], xn, r)

    tok = lambda w: pl.BlockSpec((tm, w), lambda t: (nblk - 1 - t, 0))
    const = lambda shp: pl.BlockSpec(shp, lambda t: (0,) * len(shp))
    return pl.pallas_call(
        body, name="bwd_inproj",
        grid=(nblk,),
        in_specs=[tok(ATTN_WIDTH)] * 5 + [tok(POOL_WIDTH), tok(POOL_WIDTH), tok(d), tok(d),
                                          ANY, const(poolw.shape), const((1, POOL_WIDTH)), const((1, ATTN_WIDTH)),
                                          const((1, ATTN_WIDTH)), const((1, d))],
        out_specs=[tok(d), tok(4 * POOL_WIDTH), const((1, d)), const((1, ATTN_WIDTH)), const((1, ATTN_WIDTH)),
                   const((ngrp, LANES, LANES)), const((1, POOL_WIDTH))],
        out_shape=[jax.ShapeDtypeStruct((s, d), F32),
                   jax.ShapeDtypeStruct((s, 4 * POOL_WIDTH), MXU_DTYPE),
                   jax.ShapeDtypeStruct((1, d), F32),
                   jax.ShapeDtypeStruct((1, ATTN_WIDTH), F32),
                   jax.ShapeDtypeStruct((1, ATTN_WIDTH), F32),
                   jax.ShapeDtypeStruct((ngrp, LANES, LANES), F32),
                   jax.ShapeDtypeStruct((1, POOL_WIDTH), F32)],
        scratch_shapes=[pltpu.VMEM(win.shape, MXU_DTYPE), pltpu.VMEM((tm + POOL_HALO, POOL_WIDTH), F32)],
        compiler_params=_params(("arbitrary",)),
    )(dqn, dkn, dv, q32, k32, dypool, pooled, x, dh1, win, poolw, pscale, qg, kg, g1)


def _wgrad_call(a, b, bm, bn, bk, out_shape, out_block, out_index, name):
    s, m = a.shape
    _, n = b.shape
    nk = s // bk

    def body(a_ref, b_ref, o_ref):
        k = pl.program_id(2)

        @pl.when(k == 0)
        def _():
            o_ref[...] = jnp.zeros(o_ref.shape, F32)

        o_ref[...] += _mm_tn(a_ref[...].astype(MXU_DTYPE), b_ref[...].astype(MXU_DTYPE))

    return pl.pallas_call(
        body, name=name,
        grid=(m // bm, n // bn, nk),
        in_specs=[pl.BlockSpec((bk, bm), lambda i, j, k: (k, i)), pl.BlockSpec((bk, bn), lambda i, j, k: (k, j))],
        out_specs=pl.BlockSpec(out_block, out_index),
        out_shape=jax.ShapeDtypeStruct(out_shape, F32),
        compiler_params=_params(("arbitrary", "arbitrary", "arbitrary")),
    )(a, b)


def _local_grads(x, tgt, g1, win, poolw, pscale, qg, kg, rel_bias, wout, g2, wup, wdown):
    s, d = x.shape
    g1r, g2r = g1.reshape(1, d), g2.reshape(1, d)
    psr = pscale.reshape(1, POOL_WIDTH)
    qgr = jnp.tile(qg, N_HEADS).reshape(1, ATTN_WIDTH)
    kgr = jnp.tile(kg, N_HEADS).reshape(1, ATTN_WIDTH)
    pw_c = poolw.astype(MXU_DTYPE)
    buckets = jnp.asarray(_bucket_tables())
    bias = _bias_table_call(rel_bias, buckets)

    a, pooled, ypool, q32, k32, qn, kn, v = _f1_call(x, g1r, win, pw_c, psr, qgr, kgr, tm=512)
    o, lse = _attn_fwd_call(qn, kn, v, bias)
    mixed, c, ff, dz, dy, dh1, dypool, do, delta, dg2, loss = _f2_call(x, tgt, ypool, o, wout, wup, wdown, g2r, tm=256)
    dqn, dkn, dv, dbias = _attn_bwd_call(qn, kn, v, do, lse, delta, bias)
    dx, dproj, dg1, dqg, dkg, dpw, dps = _bproj_call(
        dqn, dkn, dv, q32, k32, dypool, pooled, x, dh1, win, pw_c, psr, qgr, kgr, g1r, tm=256)

    dff = ff.shape[1]
    nin = dproj.shape[1] // N_CHIPS
    bk = min(s, 2048)
    g_in = _wgrad_call(a, dproj, d, nin, bk, (N_CHIPS, d, nin), (None, d, nin), lambda i, j, k: (j, 0, 0), "wgrad_in")
    g_out = _wgrad_call(mixed, dh1, d, d, bk // 2, (d, d), (d, d), lambda i, j, k: (0, 0), "wgrad_out")
    g_out = g_out.reshape(N_CHIPS, d // N_CHIPS, d)
    g_up = _wgrad_call(c, dz, d, dff // N_CHIPS, bk, (N_CHIPS, d, dff // N_CHIPS), (None, d, dff // N_CHIPS),
                       lambda i, j, k: (j, 0, 0), "wgrad_up")
    g_down = _wgrad_call(ff, dy, dff // N_CHIPS, d, bk, (N_CHIPS, dff // N_CHIPS, d), (None, dff // N_CHIPS, d),
                         lambda i, j, k: (i, 0, 0), "wgrad_down")
    drb = _rel_bias_grad_call(dbias, buckets)
    small = dict(
        mix_norm_g=dg1.reshape(d), mlp_norm_g=dg2.reshape(d), pool_scale=dps.reshape(POOL_WIDTH),
        q_norm_g=dqg.reshape(ATTN_WIDTH), k_norm_g=dkg.reshape(ATTN_WIDTH),
        rel_bias=drb[:, :N_BUCKETS].T, pool_w=dpw)
    return loss[0, 0], dx, (g_in, g_out, g_up, g_down), small


def _coords():
    return lax.axis_index("x"), lax.axis_index("y"), lax.axis_index("c")


def _other_chips(x, y):
    return [(1 - x, y), (x, 1 - y), (1 - x, 1 - y)]


def _remote(src, dst, send_sem, recv_sem, dev):
    return pltpu.make_async_remote_copy(src_ref=src, dst_ref=dst, send_sem=send_sem, recv_sem=recv_sem,
                                        device_id=dev, device_id_type=MESH)


def _halves(a):
    return a.reshape(a.shape[:-2] + (2, a.shape[-2] // 2, a.shape[-1]))


def _place_shards_call(shards, chip_idx, nch):
    nw = len(shards)

    def body(chip_ref, *refs):
        for w in range(nw):
            refs[nw + w][...] = refs[w][...].astype(WIRE_DTYPE)

    in_specs = [pl.BlockSpec((s.shape[0] // nch, s.shape[1]), lambda i, chip_ref: (i, 0)) for s in shards]
    out_specs = [pl.BlockSpec((None, s.shape[0] // nch, s.shape[1]), lambda i, chip_ref: (chip_ref[0], i, 0))
                 for s in shards]
    return pl.pallas_call(
        body, name="weights_place",
        grid_spec=pltpu.PrefetchScalarGridSpec(num_scalar_prefetch=1, grid=(nch,),
                                               in_specs=in_specs, out_specs=out_specs),
        out_shape=[jax.ShapeDtypeStruct((N_CHIPS,) + s.shape, WIRE_DTYPE) for s in shards],
        compiler_params=_params(("arbitrary",)),
    )(chip_idx, *shards)


def _allgather_call(placed):
    nw = len(placed)
    ncp = 3 * nw

    def body(*refs):
        outs = refs[nw:2 * nw]
        send1, recv1, send2, recv2 = refs[2 * nw:]
        x, y, c = _coords()
        chip = 2 * x + y
        others = _other_chips(x, y)
        first, passed = [], []
        for w in range(nw):
            for k, (ox, oy) in enumerate(others):
                mine = outs[w].at[chip, c]
                cp = _remote(mine, mine, send1.at[3 * w + k], recv1.at[3 * w + k], (ox, oy, c))
                cp.start()
                first.append(cp)
        for w in range(nw):
            for k, (ox, oy) in enumerate(others):
                piece = outs[w].at[2 * ox + oy, c]
                _remote(piece, piece, send1.at[3 * w + k], recv1.at[3 * w + k], (ox, oy, c)).wait_recv()
                cp = _remote(piece, piece, send2.at[3 * w + k], recv2.at[3 * w + k], (x, y, 1 - c))
                cp.start()
                passed.append(cp)
        for w in range(nw):
            for k, (ox, oy) in enumerate(others):
                piece = outs[w].at[2 * ox + oy, 1 - c]
                _remote(piece, piece, send2.at[3 * w + k], recv2.at[3 * w + k], (x, y, 1 - c)).wait_recv()
        for cp in first + passed:
            cp.wait_send()

    split = [_halves(s) for s in placed]
    outs = pl.pallas_call(
        body, name="weights_allgather",
        in_specs=[ANY] * nw, out_specs=[ANY] * nw,
        out_shape=[jax.ShapeDtypeStruct(s.shape, s.dtype) for s in split],
        input_output_aliases={w: w for w in range(nw)},
        scratch_shapes=[pltpu.SemaphoreType.DMA((ncp,))] * 4,
    )(*split)
    return [o.reshape(s.shape) for o, s in zip(outs, placed)]


def _pair_exchange_call(grads, small):
    nw = len(grads)

    def body(*refs):
        ins, small_ref = refs[:nw], refs[nw]
        outs, gathered = refs[nw + 1:2 * nw + 1], refs[2 * nw + 1]
        send, recv, ssend, srecv, loc = refs[2 * nw + 2:]
        x, y, c = _coords()
        me = 4 * x + 2 * y + c
        own = pltpu.make_async_copy(small_ref, gathered.at[me], loc)
        own.start()
        sent = []
        for w in range(nw):
            cp = _remote(ins[w].at[:, 1 - c], outs[w], send.at[w], recv.at[w], (x, y, 1 - c))
            cp.start()
            sent.append(cp)
        flips = [(r >> 2 & 1, r >> 1 & 1, r & 1) for r in range(1, N_DEV)]
        peer = lambda f: ((1 - x) if f[0] else x, (1 - y) if f[1] else y, (1 - c) if f[2] else c)
        for r, f in enumerate(flips):
            cp = _remote(small_ref, gathered.at[me], ssend.at[r], srecv.at[r], peer(f))
            cp.start()
            sent.append(cp)
        for w in range(nw):
            _remote(outs[w], outs[w], send.at[w], recv.at[w], (x, y, 1 - c)).wait_recv()
        for r, f in enumerate(flips):
            px, py, pc = peer(f)
            slot = gathered.at[4 * px + 2 * py + pc]
            _remote(slot, slot, ssend.at[r], srecv.at[r], (px, py, pc)).wait_recv()
        for cp in sent:
            cp.wait_send()
        own.wait()

    return pl.pallas_call(
        body, name="grads_pair_exchange",
        in_specs=[ANY] * (nw + 1), out_specs=[ANY] * (nw + 1),
        out_shape=[jax.ShapeDtypeStruct((g.shape[0],) + g.shape[2:], g.dtype) for g in grads]
        + [jax.ShapeDtypeStruct((N_DEV,) + small.shape, small.dtype)],
        scratch_shapes=[pltpu.SemaphoreType.DMA((nw,)), pltpu.SemaphoreType.DMA((nw,)),
                        pltpu.SemaphoreType.DMA((N_DEV - 1,)), pltpu.SemaphoreType.DMA((N_DEV - 1,)),
                        pltpu.SemaphoreType.DMA],
    )(*grads, small)


def _pair_sum_call(grads, recvd, c_idx, nch):
    nw = len(grads)

    def body(c_ref, *refs):
        for w in range(nw):
            tot = refs[w][...] + refs[nw + w][...]
            refs[2 * nw + w][...] = tot
            refs[3 * nw + w][...] = tot.astype(WIRE_DTYPE)

    in_specs, out_specs, out_shape = [], [], []
    for g in grads:
        in_specs.append(pl.BlockSpec((None, None, g.shape[2] // nch, g.shape[3]), lambda j, i, c_ref: (j, c_ref[0], i, 0)))
    for g in grads:
        in_specs.append(pl.BlockSpec((None, g.shape[2] // nch, g.shape[3]), lambda j, i, c_ref: (j, i, 0)))
    for dt in (F32, WIRE_DTYPE):
        for g in grads:
            out_specs.append(pl.BlockSpec((None, g.shape[2] // nch, g.shape[3]), lambda j, i, c_ref: (j, i, 0)))
            out_shape.append(jax.ShapeDtypeStruct((g.shape[0],) + g.shape[2:], dt))
    res = pl.pallas_call(
        body, name="grads_pair_sum",
        grid_spec=pltpu.PrefetchScalarGridSpec(num_scalar_prefetch=1, grid=(N_CHIPS, nch),
                                               in_specs=in_specs, out_specs=out_specs),
        out_shape=out_shape,
        compiler_params=_params(("arbitrary", "arbitrary")),
    )(c_idx, *grads, *recvd)
    return res[:nw], res[nw:]


def _chip_exchange_call(sums):
    nw = len(sums)

    def body(*refs):
        ins, outs = refs[:nw], refs[nw:2 * nw]
        send, recv = refs[2 * nw:]
        x, y, c = _coords()
        others = _other_chips(x, y)
        sent = []
        for w in range(nw):
            for k, (ox, oy) in enumerate(others):
                cp = _remote(ins[w].at[2 * ox + oy], outs[w].at[k], send.at[3 * w + k], recv.at[3 * w + k], (ox, oy, c))
                cp.start()
                sent.append(cp)
        for w in range(nw):
            for k, (ox, oy) in enumerate(others):
                _remote(outs[w].at[k], outs[w].at[k], send.at[3 * w + k], recv.at[3 * w + k], (ox, oy, c)).wait_recv()
        for cp in sent:
            cp.wait_send()

    return pl.pallas_call(
        body, name="grads_chip_exchange",
        in_specs=[ANY] * nw, out_specs=[ANY] * nw,
        out_shape=[jax.ShapeDtypeStruct((3,) + s.shape[1:], s.dtype) for s in sums],
        scratch_shapes=[pltpu.SemaphoreType.DMA((3 * nw,)), pltpu.SemaphoreType.DMA((3 * nw,))],
    )(*sums)


def _chip_sum_call(own, recvd, chip_c_idx, nch):
    nw = len(own)

    def body(idx_ref, *refs):
        for w in range(nw):
            q = refs[nw + w]
            refs[2 * nw + w][...] = ((refs[w][...] + q[0].astype(F32)) + q[1].astype(F32)) + q[2].astype(F32)

    in_specs, out_specs, out_shape = [], [], []
    for s in own:
        in_specs.append(pl.BlockSpec((None, s.shape[1] // nch, s.shape[2]), lambda i, idx_ref: (idx_ref[0], i, 0)))
    for s in own:
        in_specs.append(pl.BlockSpec((3, s.shape[1] // nch, s.shape[2]), lambda i, idx_ref: (0, i, 0)))
    for s in own:
        out_specs.append(pl.BlockSpec((None, s.shape[1] // nch, s.shape[2]), lambda i, idx_ref: (idx_ref[1], i, 0)))
        out_shape.append(jax.ShapeDtypeStruct((2,) + s.shape[1:], F32))
    return pl.pallas_call(
        body, name="grads_chip_sum",
        grid_spec=pltpu.PrefetchScalarGridSpec(num_scalar_prefetch=1, grid=(nch,),
                                               in_specs=in_specs, out_specs=out_specs),
        out_shape=out_shape,
        compiler_params=_params(("arbitrary",)),
    )(chip_c_idx, *own, *recvd)


def _pair_allgather_call(halves):
    nw = len(halves)

    def body(*refs):
        outs = refs[nw:2 * nw]
        send, recv = refs[2 * nw:]
        x, y, c = _coords()
        cps = []
        for w in range(nw):
            cp = _remote(outs[w].at[c], outs[w].at[c], send.at[w], recv.at[w], (x, y, 1 - c))
            cp.start()
            cps.append(cp)
        for w in range(nw):
            theirs = outs[w].at[1 - c]
            _remote(theirs, theirs, send.at[w], recv.at[w], (x, y, 1 - c)).wait_recv()
        for cp in cps:
            cp.wait_send()

    outs = pl.pallas_call(
        body, name="grads_pair_allgather",
        in_specs=[ANY] * nw, out_specs=[ANY] * nw,
        out_shape=[jax.ShapeDtypeStruct(h.shape, h.dtype) for h in halves],
        input_output_aliases={w: w for w in range(nw)},
        scratch_shapes=[pltpu.SemaphoreType.DMA((nw,))] * 2,
    )(*halves)
    return [o.reshape(2 * h.shape[1], h.shape[2]) for o, h in zip(outs, halves)]


def _adamw(w, g, m, v):
    m = ADAM_B1 * m + (1.0 - ADAM_B1) * g
    v = ADAM_B2 * v + (1.0 - ADAM_B2) * (g * g)
    m_hat = m / (1.0 - ADAM_B1 ** ADAM_STEP)
    v_hat = v / (1.0 - ADAM_B2 ** ADAM_STEP)
    delta = -ADAM_LR * (m_hat / (jnp.sqrt(v_hat) + ADAM_EPS) + ADAM_WD * w)
    return delta, m, v


def _adamw_call(ws, gs, ms, vs, nch):
    nw = len(ws)

    def body(*refs):
        for w in range(nw):
            g = refs[nw + w][...]
            delta, m, v = _adamw(refs[w][...], g, refs[2 * nw + w][...], refs[3 * nw + w][...])
            refs[4 * nw + w][...] = g
            refs[5 * nw + w][...] = delta
            refs[6 * nw + w][...] = m
            refs[7 * nw + w][...] = v

    specs = [pl.BlockSpec((a.shape[0] // nch, a.shape[1]), lambda i: (i, 0)) for a in ws]
    res = pl.pallas_call(
        body, name="adamw_big",
        grid=(nch,),
        in_specs=specs * 4, out_specs=specs * 4,
        out_shape=[jax.ShapeDtypeStruct(a.shape, F32) for a in ws] * 4,
        compiler_params=_params(("arbitrary",)),
    )(*ws, *gs, *ms, *vs)
    return res[:nw], res[nw:2 * nw], res[2 * nw:3 * nw], res[3 * nw:]


def _small_call(gathered, w, m, v):
    def fold(row):
        tot = row[:, 0:LANES] + row[:, LANES:2 * LANES] + row[:, 2 * LANES:3 * LANES] + row[:, 3 * LANES:4 * LANES]
        return tot + pltpu.roll(tot, HEAD_DIM, axis=1)

    def body(ga_ref, w_ref, m_ref, v_ref, g_out, d_out, m_out, v_out):
        g = ga_ref[0]
        for i in range(1, N_DEV):
            g = g + ga_ref[i]
        unfolded = g[4:5, :]
        folded = jnp.concatenate([fold(unfolded[:, :ATTN_WIDTH]), fold(unfolded[:, ATTN_WIDTH:]),
                                  jnp.zeros((1, 1024 - 2 * LANES), F32)], axis=-1)
        row = lax.broadcasted_iota(jnp.int32, g.shape, 0)
        g = jnp.where(row == 3, folded, g)
        delta, mm, vv = _adamw(w_ref[...], g, m_ref[...], v_ref[...])
        g_out[...] = g
        d_out[...] = delta
        m_out[...] = mm
        v_out[...] = vv

    return pl.pallas_call(
        body, name="adamw_small",
        out_shape=[jax.ShapeDtypeStruct(w.shape, F32)] * 4,
        compiler_params=_params(),
    )(gathered, w, m, v)


def _pack_small(p, folded=True, loss=None):
    z = lambda n: jnp.zeros((n,), F32)
    rows = [p["mix_norm_g"], p["mlp_norm_g"],
            jnp.concatenate([p["pool_scale"], p["rel_bias"].reshape(-1), z(1024 - POOL_WIDTH - N_BUCKETS * N_HEADS)])]
    if folded:
        rows += [jnp.concatenate([p["q_norm_g"], z(LANES - HEAD_DIM), p["k_norm_g"], z(1024 - LANES - HEAD_DIM)]), z(1024)]
    else:
        rows += [z(1024), jnp.concatenate([p["q_norm_g"], p["k_norm_g"]])]
    rows += [z(1024) if loss is None else jnp.concatenate([loss.reshape(1), z(1023)])]
    head = jnp.stack(rows + [z(1024)] * 2)
    return jnp.concatenate([head, p["pool_w"].reshape(-1, 1024)], axis=0)


def _unpack_small(a):
    return dict(
        mix_norm_g=a[0], mlp_norm_g=a[1], pool_scale=a[2, :POOL_WIDTH],
        rel_bias=a[2, POOL_WIDTH:POOL_WIDTH + N_BUCKETS * N_HEADS].reshape(N_BUCKETS, N_HEADS),
        q_norm_g=a[3, :HEAD_DIM], k_norm_g=a[3, LANES:LANES + HEAD_DIM],
        pool_w=a[8:].reshape(len(POOL_WINDOWS), LANES, LANES))


_WEIGHT_ORDER = ("mix_norm_g", "w_in", "pool_w", "pool_scale", "q_norm_g", "k_norm_g", "rel_bias", "w_out",
                 "mlp_norm_g", "w_up", "w_down")
_BIG = ("w_in", "w_out", "w_up", "w_down")


def kernel(x, mix_norm_g, w_in, pool_w, pool_scale, q_norm_g, k_norm_g, rel_bias, w_out, mlp_norm_g, w_up, w_down, loss_target, m_mix_norm_g, m_w_in, m_pool_w, m_pool_scale, m_q_norm_g, m_k_norm_g, m_rel_bias, m_w_out, m_mlp_norm_g, m_w_up, m_w_down, v_mix_norm_g, v_w_in, v_pool_w, v_pool_scale, v_q_norm_g, v_k_norm_g, v_rel_bias, v_w_out, v_mlp_norm_g, v_w_up, v_w_down):
    w = dict(mix_norm_g=mix_norm_g, w_in=w_in, pool_w=pool_w, pool_scale=pool_scale, q_norm_g=q_norm_g,
             k_norm_g=k_norm_g, rel_bias=rel_bias, w_out=w_out, mlp_norm_g=mlp_norm_g, w_up=w_up, w_down=w_down)
    m = dict(mix_norm_g=m_mix_norm_g, w_in=m_w_in, pool_w=m_pool_w, pool_scale=m_pool_scale, q_norm_g=m_q_norm_g,
             k_norm_g=m_k_norm_g, rel_bias=m_rel_bias, w_out=m_w_out, mlp_norm_g=m_mlp_norm_g, w_up=m_w_up, w_down=m_w_down)
    v = dict(mix_norm_g=v_mix_norm_g, w_in=v_w_in, pool_w=v_pool_w, pool_scale=v_pool_scale, q_norm_g=v_q_norm_g,
             k_norm_g=v_k_norm_g, rel_bias=v_rel_bias, w_out=v_w_out, mlp_norm_g=v_mlp_norm_g, w_up=v_w_up, w_down=v_w_down)
    xc, yc, cc = _coords()

    c_idx = jnp.reshape(cc, (1,)).astype(jnp.int32)
    chip_idx = jnp.reshape(2 * xc + yc, (1,)).astype(jnp.int32)
    win_f, wout_f, wup_f, wdown_f = _allgather_call(_place_shards_call([w[n] for n in _BIG], chip_idx, nch=4))
    loss_part, dx, big_grads, small_grads = _local_grads(
        x[0], loss_target[0], mix_norm_g, win_f, pool_w, pool_scale, q_norm_g, k_norm_g, rel_bias,
        wout_f.reshape(wout_f.shape[0] * wout_f.shape[1], wout_f.shape[2]), mlp_norm_g, wup_f, wdown_f)

    big_halves = [_halves(g) for g in big_grads]
    *from_sibling, small_all = _pair_exchange_call(big_halves, _pack_small(small_grads, folded=False, loss=loss_part))
    sums32, sums_wire = _pair_sum_call(big_halves, from_sibling, c_idx, nch=2)
    from_chips = _chip_exchange_call(list(sums_wire))
    halves = _chip_sum_call(list(sums32), from_chips, jnp.concatenate([chip_idx, c_idx]), nch=2)
    g_reduced = _pair_allgather_call(list(halves))
    g_big, d_big, m_big, v_big = _adamw_call(
        [w[n] for n in _BIG], g_reduced, [m[n] for n in _BIG], [v[n] for n in _BIG], nch=8)
    g_pack, d_pack, m_pack, v_pack = _small_call(small_all, _pack_small(w), _pack_small(m), _pack_small(v))

    grads, deltas, new_m, new_v = (_unpack_small(a) for a in (g_pack, d_pack, m_pack, v_pack))
    for i, n in enumerate(_BIG):
        grads[n], deltas[n], new_m[n], new_v[n] = g_big[i], d_big[i], m_big[i], v_big[i]
    loss = g_pack[LOSS_ROW, 0]
    return (loss, dx[None], *[grads[n] for n in _WEIGHT_ORDER], *[deltas[n] for n in _WEIGHT_ORDER],
            *[new_m[n] for n in _WEIGHT_ORDER], *[new_v[n] for n in _WEIGHT_ORDER])
```

```python
import math

import jax
import jax.numpy as jnp
import numpy as np
from jax import lax
from jax.experimental import pallas as pl
from jax.experimental.pallas import tpu as pltpu

F32 = jnp.float32
MXU_DTYPE = jnp.bfloat16
WIRE_DTYPE = jnp.bfloat16

NORM_EPS = 1e-6
NEG_INF = -1e30
LANES = 128
HEAD_DIM = 64
N_HEADS = 8
POOL_WIDTH = 512
ATTN_WIDTH = 512
POOL_WINDOWS = (2, 4, 8, 16)
POOL_HALO = 16
DILATED_PATTERNS = ((128, 1), (512, 4), (2048, 16))
ATT_BLOCK = 128
ATT_SUPER = ATT_BLOCK * max(dl for _, dl in DILATED_PATTERNS)
ATT_UNITS = ATT_SUPER // ATT_BLOCK
N_BUCKETS = 32
MAX_DISTANCE = 2048
N_CHIPS = 4
N_DEV = 8
ADAM_LR, ADAM_B1, ADAM_B2, ADAM_EPS, ADAM_WD, ADAM_STEP = 0.001, 0.9, 0.999, 1e-08, 0.01, 10
VMEM_LIMIT = 56 * 1024 * 1024
MESH = pl.DeviceIdType.MESH
ANY = pl.BlockSpec(memory_space=pl.ANY)

SMALL_ROWS = 72
LOSS_ROW = 5


def _mm(a, b):
    return jnp.dot(a, b, preferred_element_type=F32)


def _mm_nt(a, b):
    return lax.dot_general(a, b, (((1,), (1,)), ((), ())), preferred_element_type=F32)


def _mm_tn(a, b):
    return lax.dot_general(a, b, (((0,), (0,)), ((), ())), preferred_element_type=F32)


def _params(sem=None, **kw):
    if sem is not None:
        kw["dimension_semantics"] = sem
    return pltpu.CompilerParams(vmem_limit_bytes=VMEM_LIMIT, **kw)


def _low_half():
    return lax.broadcasted_iota(jnp.int32, (1, LANES), 1) < HEAD_DIM


def _head_sum_bcast(y):
    lo = _low_half()
    outs = []
    for j in range(y.shape[1] // LANES):
        c = y[:, j * LANES:(j + 1) * LANES]
        s_lo = jnp.sum(jnp.where(lo, c, 0.0), axis=-1, keepdims=True)
        s_hi = jnp.sum(jnp.where(lo, 0.0, c), axis=-1, keepdims=True)
        outs.append(jnp.where(lo, s_lo, s_hi))
    return jnp.concatenate(outs, axis=-1)


def _rms_bwd(dn, hn, r):
    return r * (dn - hn * jnp.mean(dn * hn, axis=-1, keepdims=True))


def _t5_bucket_np(dist):
    max_exact = N_BUCKETS // 2
    d_f = np.maximum(dist, 1).astype(np.float32)
    ratio = (np.log(d_f / np.float32(max_exact)) / np.float32(math.log(MAX_DISTANCE / max_exact))).astype(np.float32)
    large = max_exact + (ratio * np.float32(N_BUCKETS - max_exact)).astype(np.int32)
    large = np.minimum(large, N_BUCKETS - 1)
    return np.where(dist < max_exact, dist, large).astype(np.int32)


def _bucket_tables():
    qq = np.arange(ATT_BLOCK)[:, None]
    kk = np.arange(2 * ATT_BLOCK)[None, :]
    dist = np.clip(qq + ATT_BLOCK - kk, 0, ATT_BLOCK)
    return np.stack([_t5_bucket_np(dist * dl) for (_, dl) in DILATED_PATTERNS])


def _f1_call(x, g1, win, poolw, pscale, qg, kg, tm):
    s, d = x.shape
    nblk = s // tm

    def body(x_ref, g1_ref, win_ref, pw_ref, ps_ref, qg_ref, kg_ref,
             a_ref, pooled_ref, ypool_ref, q32_ref, k32_ref, qn_ref, kn_ref, v_ref, ubuf):
        i = pl.program_id(0)
        xv = x_ref[...]
        r = lax.rsqrt(jnp.mean(xv * xv, axis=-1, keepdims=True) + NORM_EPS)
        a = ((xv * r) * g1_ref[...]).astype(MXU_DTYPE)
        a_ref[...] = a
        u = _mm(a, win_ref[0])
        q = _mm(a, win_ref[1])
        k = _mm(a, win_ref[2])
        v_ref[...] = _mm(a, win_ref[3]).astype(MXU_DTYPE)
        q32_ref[...] = q
        k32_ref[...] = k
        rq = lax.rsqrt(_head_sum_bcast(q * q) * (1.0 / HEAD_DIM) + NORM_EPS)
        qn_ref[...] = (((q * rq) * qg_ref[...]) * (HEAD_DIM ** -0.5)).astype(MXU_DTYPE)
        rk = lax.rsqrt(_head_sum_bcast(k * k) * (1.0 / HEAD_DIM) + NORM_EPS)
        kn_ref[...] = ((k * rk) * kg_ref[...]).astype(MXU_DTYPE)

        @pl.when(i == 0)
        def _():
            ubuf[0:POOL_HALO, :] = jnp.zeros((POOL_HALO, POOL_WIDTH), F32)

        @pl.when(i > 0)
        def _():
            ubuf[0:POOL_HALO, :] = ubuf[tm:tm + POOL_HALO, :]

        ubuf[POOL_HALO:POOL_HALO + tm, :] = u
        t = i * tm + lax.broadcasted_iota(jnp.int32, (tm, 1), 0)
        for g, w in enumerate(POOL_WINDOWS):
            ls = slice(g * LANES, (g + 1) * LANES)
            ug = u[:, ls]
            acc = ug
            for sh in range(1, w):
                acc = acc + ubuf[POOL_HALO - sh:POOL_HALO - sh + tm, ls]
            cnt = jnp.minimum(t + 1, w).astype(F32)
            pooled = (acc / cnt - ug).astype(MXU_DTYPE)
            pooled_ref[:, ls] = pooled
            ypool_ref[:, ls] = (_mm(pooled, pw_ref[g]) * ps_ref[:, ls]).astype(MXU_DTYPE)

    tok = lambda w: pl.BlockSpec((tm, w), lambda i: (i, 0))
    full = lambda shp: pl.BlockSpec(shp, lambda i: (0,) * len(shp))
    return pl.pallas_call(
        body, name="fwd_inproj",
        grid=(nblk,),
        in_specs=[tok(d), full((1, d)), full(win.shape), full(poolw.shape), full((1, POOL_WIDTH)),
                  full((1, ATTN_WIDTH)), full((1, ATTN_WIDTH))],
        out_specs=[tok(d), tok(POOL_WIDTH), tok(POOL_WIDTH), tok(ATTN_WIDTH), tok(ATTN_WIDTH),
                   tok(ATTN_WIDTH), tok(ATTN_WIDTH), tok(ATTN_WIDTH)],
        out_shape=[jax.ShapeDtypeStruct((s, d), MXU_DTYPE),
                   jax.ShapeDtypeStruct((s, POOL_WIDTH), MXU_DTYPE),
                   jax.ShapeDtypeStruct((s, POOL_WIDTH), MXU_DTYPE),
                   jax.ShapeDtypeStruct((s, ATTN_WIDTH), F32),
                   jax.ShapeDtypeStruct((s, ATTN_WIDTH), F32),
                   jax.ShapeDtypeStruct((s, ATTN_WIDTH), MXU_DTYPE),
                   jax.ShapeDtypeStruct((s, ATTN_WIDTH), MXU_DTYPE),
                   jax.ShapeDtypeStruct((s, ATTN_WIDTH), MXU_DTYPE)],
        scratch_shapes=[pltpu.VMEM((tm + POOL_HALO, POOL_WIDTH), F32)],
        compiler_params=_params(("arbitrary",)),
    )(x, g1, win, poolw, pscale, qg, kg)


def _band_mask(n):
    qq = lax.broadcasted_iota(jnp.int32, (ATT_BLOCK, 2 * ATT_BLOCK), 0)
    kk = lax.broadcasted_iota(jnp.int32, (ATT_BLOCK, 2 * ATT_BLOCK), 1)
    dist = qq + ATT_BLOCK - kk
    return (dist >= 0) & (dist <= ATT_BLOCK) & ((n > 0) | (kk >= ATT_BLOCK))


def _unit_rows(u, dl):
    r = u % dl
    b = u // dl
    q0 = r + dl * ATT_BLOCK * b
    return b, pl.ds(q0, ATT_BLOCK, stride=dl), pl.ds(ATT_SUPER + q0 - dl * ATT_BLOCK, 2 * ATT_BLOCK, stride=dl)


def _attn_fwd_call(qn, kn, v, bias):
    s, w = qn.shape
    nsb = s // ATT_SUPER
    npair = w // LANES

    def body(q_ref, kc_ref, kp_ref, vc_ref, vp_ref, b_ref, o_ref, lse_ref, qf, kf, vf, acc_s, m_s, l_s):
        sb = pl.program_id(1)
        qf[...] = q_ref[...].astype(F32)
        kf[0:ATT_SUPER, :] = kp_ref[...].astype(F32)
        kf[ATT_SUPER:, :] = kc_ref[...].astype(F32)
        vf[0:ATT_SUPER, :] = vp_ref[...].astype(F32)
        vf[ATT_SUPER:, :] = vc_ref[...].astype(F32)
        lo = _low_half()
        for p, (_, dl) in enumerate(DILATED_PATTERNS):
            def unit(u, carry, p=p, dl=dl):
                b, rows_q, rows_k = _unit_rows(u, dl)
                valid = _band_mask(sb * (ATT_UNITS // dl) + b)
                qp = qf[rows_q, :].astype(MXU_DTYPE)
                kcat = kf[rows_k, :].astype(MXU_DTYPE)
                vcat = vf[rows_k, :].astype(MXU_DTYPE)
                zero = jnp.zeros_like(qp)
                q2 = jnp.concatenate([jnp.where(lo, qp, zero), jnp.where(lo, zero, qp)], axis=0)
                sc = _mm_nt(q2, kcat) + b_ref[p].reshape(2 * ATT_BLOCK, 2 * ATT_BLOCK)
                sc = jnp.where(jnp.concatenate([valid, valid], axis=0), sc, NEG_INF)
                m2 = jnp.max(sc, axis=-1, keepdims=True)
                pr = jnp.exp(sc - m2)
                l2 = jnp.sum(pr, axis=-1, keepdims=True)
                acc2 = _mm(pr.astype(MXU_DTYPE), vcat)
                acc = jnp.where(lo, acc2[:ATT_BLOCK], acc2[ATT_BLOCK:])
                m = jnp.where(lo, m2[:ATT_BLOCK], m2[ATT_BLOCK:])
                l = jnp.where(lo, l2[:ATT_BLOCK], l2[ATT_BLOCK:])
                if p == 0:
                    acc_s[rows_q, :] = acc
                    m_s[rows_q, :] = m
                    l_s[rows_q, :] = l
                else:
                    m_old = m_s[rows_q, :]
                    m_new = jnp.maximum(m_old, m)
                    a_old = jnp.exp(m_old - m_new)
                    a_new = jnp.exp(m - m_new)
                    acc_s[rows_q, :] = a_old * acc_s[rows_q, :] + a_new * acc
                    l_s[rows_q, :] = a_old * l_s[rows_q, :] + a_new * l
                    m_s[rows_q, :] = m_new
                return carry

            lax.fori_loop(0, ATT_UNITS, unit, 0, unroll=4)
        l = l_s[...]
        o_ref[...] = acc_s[...] / l
        lse_ref[...] = m_s[...] + jnp.log(l)

    cur = pl.BlockSpec((ATT_SUPER, LANES), lambda j, t: (t, j))
    prev = pl.BlockSpec((ATT_SUPER, LANES), lambda j, t: (jnp.maximum(t - 1, 0), j))
    bspec = pl.BlockSpec((len(DILATED_PATTERNS), 2, ATT_BLOCK, 2 * ATT_BLOCK), lambda j, t: (0, j, 0, 0))
    return pl.pallas_call(
        body, name="attn_fwd",
        grid=(npair, nsb),
        in_specs=[cur, cur, prev, cur, prev, bspec],
        out_specs=[cur, cur],
        out_shape=[jax.ShapeDtypeStruct((s, w), F32), jax.ShapeDtypeStruct((s, w), F32)],
        scratch_shapes=[pltpu.VMEM((ATT_SUPER, LANES), F32), pltpu.VMEM((2 * ATT_SUPER, LANES), F32),
                        pltpu.VMEM((2 * ATT_SUPER, LANES), F32), pltpu.VMEM((ATT_SUPER, LANES), F32),
                        pltpu.VMEM((ATT_SUPER, LANES), F32), pltpu.VMEM((ATT_SUPER, LANES), F32)],
        compiler_params=_params(("arbitrary", "arbitrary")),
    )(qn, kn, kn, v, v, bias)


def _attn_bwd_call(qn, kn, v, do, lse, delta, bias, dep=None):
    s, w = qn.shape
    nsb = s // ATT_SUPER
    npair = w // LANES
    deps = [] if dep is None else [dep]

    def body(q_ref, kc_ref, kp_ref, vc_ref, vp_ref, do_ref, lse_ref, dlt_ref, b_ref, *rest):
        dq_ref, dk_ref, dv_ref, db_ref, qf, kf, vf, dof, dkf, dvf = rest[len(deps):]
        step = pl.program_id(1)
        sb = nsb - 1 - step
        qf[...] = q_ref[...].astype(F32)
        dof[...] = do_ref[...].astype(F32)
        kf[0:ATT_SUPER, :] = kp_ref[...].astype(F32)
        kf[ATT_SUPER:, :] = kc_ref[...].astype(F32)
        vf[0:ATT_SUPER, :] = vp_ref[...].astype(F32)
        vf[ATT_SUPER:, :] = vc_ref[...].astype(F32)

        @pl.when(step == 0)
        def _():
            db_ref[...] = jnp.zeros(db_ref.shape, F32)
            dkf[ATT_SUPER:, :] = jnp.zeros((ATT_SUPER, LANES), F32)
            dvf[ATT_SUPER:, :] = jnp.zeros((ATT_SUPER, LANES), F32)

        @pl.when(step > 0)
        def _():
            dkf[ATT_SUPER:, :] = dkf[0:ATT_SUPER, :]
            dvf[ATT_SUPER:, :] = dvf[0:ATT_SUPER, :]

        dkf[0:ATT_SUPER, :] = jnp.zeros((ATT_SUPER, LANES), F32)
        dvf[0:ATT_SUPER, :] = jnp.zeros((ATT_SUPER, LANES), F32)
        lo = _low_half()
        for p, (_, dl) in enumerate(DILATED_PATTERNS):
            def unit(u, carry, p=p, dl=dl):
                b, rows_q, rows_k = _unit_rows(u, dl)
                valid = _band_mask(sb * (ATT_UNITS // dl) + b)
                qp = qf[rows_q, :].astype(MXU_DTYPE)
                dop = dof[rows_q, :].astype(MXU_DTYPE)
                kcat = kf[rows_k, :].astype(MXU_DTYPE)
                vcat = vf[rows_k, :].astype(MXU_DTYPE)
                lse2 = lse_ref[rows_q, :]
                dlt2 = dlt_ref[rows_q, :]
                zero = jnp.zeros_like(qp)
                q2 = jnp.concatenate([jnp.where(lo, qp, zero), jnp.where(lo, zero, qp)], axis=0)
                do2 = jnp.concatenate([jnp.where(lo, dop, zero), jnp.where(lo, zero, dop)], axis=0)
                lse_c = jnp.concatenate([lse2[:, 0:1], lse2[:, HEAD_DIM:HEAD_DIM + 1]], axis=0)
                dlt_c = jnp.concatenate([dlt2[:, 0:1], dlt2[:, HEAD_DIM:HEAD_DIM + 1]], axis=0)
                sc = _mm_nt(q2, kcat) + b_ref[p].reshape(2 * ATT_BLOCK, 2 * ATT_BLOCK)
                pr = jnp.where(jnp.concatenate([valid, valid], axis=0), jnp.exp(sc - lse_c), 0.0)
                ds = pr * (_mm_nt(do2, vcat) - dlt_c)
                db_ref[p] += ds.reshape(2, ATT_BLOCK, 2 * ATT_BLOCK)
                ds_c = ds.astype(MXU_DTYPE)
                dq2 = _mm(ds_c, kcat)
                dk = _mm_tn(ds_c, q2)
                dv = _mm_tn(pr.astype(MXU_DTYPE), do2)
                dq = jnp.where(lo, dq2[:ATT_BLOCK], dq2[ATT_BLOCK:])
                if p == 0:
                    dq_ref[rows_q, :] = dq
                else:
                    dq_ref[rows_q, :] += dq
                dkf[rows_k, :] += dk
                dvf[rows_k, :] += dv
                return carry

            lax.fori_loop(0, ATT_UNITS, unit, 0, unroll=4)
        dk_ref[...] = dkf[ATT_SUPER:, :]
        dv_ref[...] = dvf[ATT_SUPER:, :]

    cur = pl.BlockSpec((ATT_SUPER, LANES), lambda j, t: (nsb - 1 - t, j))
    prev = pl.BlockSpec((ATT_SUPER, LANES), lambda j, t: (jnp.maximum(nsb - 2 - t, 0), j))
    bshape = (len(DILATED_PATTERNS), 2, ATT_BLOCK, 2 * ATT_BLOCK)
    bspec = pl.BlockSpec(bshape, lambda j, t: (0, j, 0, 0))
    sup = lambda: pltpu.VMEM((ATT_SUPER, LANES), F32)
    sup2 = lambda: pltpu.VMEM((2 * ATT_SUPER, LANES), F32)
    return pl.pallas_call(
        body, name="attn_bwd",
        grid=(npair, nsb),
        in_specs=[cur, cur, prev, cur, prev, cur, cur, cur, bspec] + [ANY] * len(deps),
        out_specs=[cur, cur, cur, bspec],
        out_shape=[jax.ShapeDtypeStruct((s, w), F32)] * 3 + [jax.ShapeDtypeStruct(bias.shape, F32)],
        scratch_shapes=[sup(), sup2(), sup2(), sup(), sup2(), sup2()],
        compiler_params=_params(("arbitrary", "arbitrary")),
    )(qn, kn, kn, v, v, do, lse, delta, bias, *deps)


def _bias_table_call(rel_bias, buckets):
    npat = buckets.shape[0]

    def body(rb_ref, bk_ref, out_ref):
        for p in range(npat):
            for half in range(2):
                ks = slice(half * ATT_BLOCK, (half + 1) * ATT_BLOCK)
                bk = bk_ref[p, :, ks]
                for h in range(N_HEADS):
                    def pick(b, acc, h=h, bk=bk):
                        return jnp.where(bk == b, rb_ref[b, h], acc)

                    out_ref[p, h, :, ks] = lax.fori_loop(0, N_BUCKETS, pick, jnp.zeros((ATT_BLOCK, ATT_BLOCK), F32))

    return pl.pallas_call(
        body, name="bias_table",
        in_specs=[pl.BlockSpec(memory_space=pltpu.SMEM), pl.BlockSpec(memory_space=pltpu.VMEM)],
        out_shape=jax.ShapeDtypeStruct((npat, N_HEADS, ATT_BLOCK, 2 * ATT_BLOCK), F32),
        compiler_params=_params(),
    )(rel_bias, buckets)


def _rel_bias_grad_call(dbias, buckets):
    npat, nh = dbias.shape[0], dbias.shape[1]

    def body(db_ref, bk_ref, out_ref):
        lane = lax.broadcasted_iota(jnp.int32, (nh, LANES), 1)
        out = jnp.zeros((nh, LANES), F32)
        for b in range(N_BUCKETS):
            tot = jnp.zeros((nh, 1), F32)
            for p in range(npat):
                hit = jnp.where(bk_ref[p][None] == b, db_ref[p], 0.0)
                tot = tot + jnp.sum(jnp.sum(hit, axis=2), axis=1, keepdims=True)
            out = jnp.where(lane == b, tot, out)
        out_ref[...] = out

    return pl.pallas_call(
        body, name="rel_bias_grad",
        out_shape=jax.ShapeDtypeStruct((nh, LANES), F32),
        compiler_params=_params(),
    )(dbias, buckets)


def _f2_call(x, tgt, ypool, o, wout, wup, wdown, g2, tm):
    s, d = x.shape
    nblk = s // tm
    nch, _, fch = wup.shape
    dff = nch * fch
    mixw = POOL_WIDTH + ATTN_WIDTH

    def body(x_ref, t_ref, yp_ref, o_ref, g2_ref, wout_hbm, wup_hbm, wdown_hbm,
             mixed_ref, c_ref, ff_ref, dz_ref, dy_ref, dh1_ref, dyp_ref, do_ref, dlt_ref, dg2_ref, loss_ref,
             wout_v, wup_v, wdown_v, rz):
        i = pl.program_id(0)

        @pl.when(i == 0)
        def _():
            pltpu.sync_copy(wout_hbm, wout_v)
            pltpu.sync_copy(wup_hbm, wup_v)
            pltpu.sync_copy(wdown_hbm, wdown_v)
            dg2_ref[...] = jnp.zeros(dg2_ref.shape, F32)
            loss_ref[...] = jnp.zeros(loss_ref.shape, F32)

        o = o_ref[...]
        mixed = jnp.concatenate([yp_ref[...], o.astype(MXU_DTYPE)], axis=-1)
        mixed_ref[...] = mixed
        h1 = x_ref[...] + _mm(mixed, wout_v[...])
        r2 = lax.rsqrt(jnp.mean(h1 * h1, axis=-1, keepdims=True) + NORM_EPS)
        hn = h1 * r2
        c = (hn * g2_ref[...]).astype(MXU_DTYPE)
        c_ref[...] = c
        y = h1
        for j in range(nch):
            cs = slice(j * fch, (j + 1) * fch)
            z = jnp.maximum(_mm(c, wup_v[j]), 0.0)
            rz[:, cs] = z
            ff = (z * z).astype(MXU_DTYPE)
            ff_ref[:, cs] = ff
            y = y + _mm(ff, wdown_v[j])
        err = y - t_ref[...]
        loss_ref[...] += jnp.sum(err * err) * (0.5 / d)
        dy = err * (1.0 / d)
        dy_c = dy.astype(MXU_DTYPE)
        dy_ref[...] = dy_c
        dc = jnp.zeros((tm, d), F32)
        for j in range(nch):
            cs = slice(j * fch, (j + 1) * fch)
            dz = (_mm_nt(dy_c, wdown_v[j]) * (2.0 * rz[:, cs])).astype(MXU_DTYPE)
            dz_ref[:, cs] = dz
            dc = dc + _mm_nt(dz, wup_v[j])
        dg2_ref[...] += jnp.sum(dc * hn, axis=0, keepdims=True)
        dh1 = dy + _rms_bwd(dc * g2_ref[...], hn, r2)
        dh1_ref[...] = dh1
        dmix = _mm_nt(dh1.astype(MXU_DTYPE), wout_v[...])
        dyp_ref[...] = dmix[:, :POOL_WIDTH]
        do = dmix[:, POOL_WIDTH:]
        do_ref[...] = do.astype(MXU_DTYPE)
        dlt_ref[...] = _head_sum_bcast(do * o)

    tok = lambda w: pl.BlockSpec((tm, w), lambda i: (i, 0))
    const = lambda shp: pl.BlockSpec(shp, lambda i: (0,) * len(shp))
    return pl.pallas_call(
        body, name="fwd_mlp_bwd_mlp",
        grid=(nblk,),
        in_specs=[tok(d), tok(d), tok(POOL_WIDTH), tok(ATTN_WIDTH), const((1, d)), ANY, ANY, ANY],
        out_specs=[tok(mixw), tok(d), tok(dff), tok(dff), tok(d), tok(d), tok(POOL_WIDTH), tok(ATTN_WIDTH),
                   tok(ATTN_WIDTH), const((1, d)), const((1, LANES))],
        out_shape=[jax.ShapeDtypeStruct((s, mixw), MXU_DTYPE),
                   jax.ShapeDtypeStruct((s, d), MXU_DTYPE),
                   jax.ShapeDtypeStruct((s, dff), MXU_DTYPE),
                   jax.ShapeDtypeStruct((s, dff), MXU_DTYPE),
                   jax.ShapeDtypeStruct((s, d), MXU_DTYPE),
                   jax.ShapeDtypeStruct((s, d), F32),
                   jax.ShapeDtypeStruct((s, POOL_WIDTH), F32),
                   jax.ShapeDtypeStruct((s, ATTN_WIDTH), MXU_DTYPE),
                   jax.ShapeDtypeStruct((s, ATTN_WIDTH), F32),
                   jax.ShapeDtypeStruct((1, d), F32),
                   jax.ShapeDtypeStruct((1, LANES), F32)],
        scratch_shapes=[pltpu.VMEM(wout.shape, MXU_DTYPE), pltpu.VMEM(wup.shape, MXU_DTYPE),
                        pltpu.VMEM(wdown.shape, MXU_DTYPE), pltpu.VMEM((tm, dff), F32)],
        compiler_params=_params(("arbitrary",)),
    )(x, tgt, ypool, o, g2, wout, wup, wdown)


def _bproj_call(dqn, dkn, dv, q32, k32, dypool, pooled, x, dh1, win, poolw, pscale, qg, kg, g1, tm):
    s, d = x.shape
    nblk = s // tm
    ngrp = len(POOL_WINDOWS)

    def body(dqn_ref, dkn_ref, dv_ref, q_ref, k_ref, dyp_ref, pooled_ref, x_ref, dh1_ref,
             win_hbm, pw_ref, ps_ref, qg_ref, kg_ref, g1_ref,
             dx_ref, dproj_ref, dg1_ref, dqg_ref, dkg_ref, dpw_ref, dps_ref, win_v, ebuf):
        step = pl.program_id(0)
        i = nblk - 1 - step

        @pl.when(step == 0)
        def _():
            pltpu.sync_copy(win_hbm, win_v)
            dg1_ref[...] = jnp.zeros(dg1_ref.shape, F32)
            dqg_ref[...] = jnp.zeros(dqg_ref.shape, F32)
            dkg_ref[...] = jnp.zeros(dkg_ref.shape, F32)
            dpw_ref[...] = jnp.zeros(dpw_ref.shape, F32)
            dps_ref[...] = jnp.zeros(dps_ref.shape, F32)
            ebuf[tm:tm + POOL_HALO, :] = jnp.zeros((POOL_HALO, POOL_WIDTH), F32)

        @pl.when(step > 0)
        def _():
            ebuf[tm:tm + POOL_HALO, :] = ebuf[0:POOL_HALO, :]

        def qk_bwd(dn_sum, raw, gain, scale, dgain_ref):
            rr = lax.rsqrt(_head_sum_bcast(raw * raw) * (1.0 / HEAD_DIM) + NORM_EPS)
            hn = raw * rr
            dgain_ref[...] += jnp.sum(dn_sum * hn, axis=0, keepdims=True) * scale
            dn = dn_sum * (gain * scale)
            return rr * (dn - hn * (_head_sum_bcast(dn * hn) * (1.0 / HEAD_DIM)))

        dq = qk_bwd(dqn_ref[...], q_ref[...], qg_ref[...], HEAD_DIM ** -0.5, dqg_ref)
        dk = qk_bwd(dkn_ref[...], k_ref[...], kg_ref[...], 1.0, dkg_ref)

        t = i * tm + lax.broadcasted_iota(jnp.int32, (tm, 1), 0)
        dpooled = []
        for g, w in enumerate(POOL_WINDOWS):
            ls = slice(g * LANES, (g + 1) * LANES)
            dm = dyp_ref[:, ls]
            pg = pooled_ref[:, ls]
            dps_ref[:, ls] += jnp.sum(dm * _mm(pg, pw_ref[g]), axis=0, keepdims=True)
            dms = (dm * ps_ref[:, ls]).astype(MXU_DTYPE)
            dpw_ref[g] += _mm_tn(pg, dms)
            dpg = _mm_nt(dms, pw_ref[g])
            dpooled.append(dpg)
            ebuf[0:tm, ls] = dpg / jnp.minimum(t + 1, w).astype(F32)
        du = []
        for g, w in enumerate(POOL_WINDOWS):
            ls = slice(g * LANES, (g + 1) * LANES)
            acc = ebuf[0:tm, ls]
            for sh in range(1, w):
                acc = acc + ebuf[sh:sh + tm, ls]
            du.append(acc - dpooled[g])
        parts = [jnp.concatenate(du, axis=-1), dq, dk, dv_ref[...]]
        da = jnp.zeros((tm, d), F32)
        for p, part in enumerate(parts):
            pc = part.astype(MXU_DTYPE)
            dproj_ref[:, p * POOL_WIDTH:(p + 1) * POOL_WIDTH] = pc
            da = da + _mm_nt(pc, win_v[p])
        xv = x_ref[...]
        r = lax.rsqrt(jnp.mean(xv * xv, axis=-1, keepdims=True) + NORM_EPS)
        xn = xv * r
        dg1_ref[...] += jnp.sum(da * xn, axis=0, keepdims=True)
        dx_ref[...] = dh1_ref[...] + _rms_bwd(da * g1_ref[...], xn, r)

    tok = lambda w: pl.BlockSpec((tm, w), lambda t: (nblk - 1 - t, 0))
    const = lambda shp: pl.BlockSpec(shp, lambda t: (0,) * len(shp))
    return pl.pallas_call(
        body, name="bwd_inproj",
        grid=(nblk,),
        in_specs=[tok(ATTN_WIDTH)] * 5 + [tok(POOL_WIDTH), tok(POOL_WIDTH), tok(d), tok(d),
                                          ANY, const(poolw.shape), const((1, POOL_WIDTH)), const((1, ATTN_WIDTH)),
                                          const((1, ATTN_WIDTH)), const((1, d))],
        out_specs=[tok(d), tok(4 * POOL_WIDTH), const((1, d)), const((1, ATTN_WIDTH)), const((1, ATTN_WIDTH)),
                   const((ngrp, LANES, LANES)), const((1, POOL_WIDTH))],
        out_shape=[jax.ShapeDtypeStruct((s, d), F32),
                   jax.ShapeDtypeStruct((s, 4 * POOL_WIDTH), MXU_DTYPE),
                   jax.ShapeDtypeStruct((1, d), F32),
                   jax.ShapeDtypeStruct((1, ATTN_WIDTH), F32),
                   jax.ShapeDtypeStruct((1, ATTN_WIDTH), F32),
                   jax.ShapeDtypeStruct((ngrp, LANES, LANES), F32),
                   jax.ShapeDtypeStruct((1, POOL_WIDTH), F32)],
        scratch_shapes=[pltpu.VMEM(win.shape, MXU_DTYPE), pltpu.VMEM((tm + POOL_HALO, POOL_WIDTH), F32)],
        compiler_params=_params(("arbitrary",)),
    )(dqn, dkn, dv, q32, k32, dypool, pooled, x, dh1, win, poolw, pscale, qg, kg, g1)


def _wgrad_call(a, b, bm, bn, bk, out_shape, out_block, out_index, name):
    s, m = a.shape
    _, n = b.shape
    nk = s // bk

    def body(a_ref, b_ref, o_ref, wire_ref):
        k = pl.program_id(2)

        @pl.when(k == 0)
        def _():
            o_ref[...] = jnp.zeros(o_ref.shape, F32)

        o_ref[...] += _mm_tn(a_ref[...].astype(MXU_DTYPE), b_ref[...].astype(MXU_DTYPE))

        @pl.when(k == nk - 1)
        def _():
            wire_ref[...] = o_ref[...].astype(WIRE_DTYPE)

    return pl.pallas_call(
        body, name=name,
        grid=(m // bm, n // bn, nk),
        in_specs=[pl.BlockSpec((bk, bm), lambda i, j, k: (k, i)), pl.BlockSpec((bk, bn), lambda i, j, k: (k, j))],
        out_specs=[pl.BlockSpec(out_block, out_index)] * 2,
        out_shape=[jax.ShapeDtypeStruct(out_shape, F32), jax.ShapeDtypeStruct(out_shape, WIRE_DTYPE)],
        compiler_params=_params(("arbitrary", "arbitrary", "arbitrary")),
    )(a, b)


def _local_grads(x, tgt, g1, win, poolw, pscale, qg, kg, rel_bias, g2, mlp_weights, on_mlp_grads=None):
    s, d = x.shape
    g1r, g2r = g1.reshape(1, d), g2.reshape(1, d)
    psr = pscale.reshape(1, POOL_WIDTH)
    qgr = jnp.tile(qg, N_HEADS).reshape(1, ATTN_WIDTH)
    kgr = jnp.tile(kg, N_HEADS).reshape(1, ATTN_WIDTH)
    pw_c = poolw.astype(MXU_DTYPE)
    buckets = jnp.asarray(_bucket_tables())
    bias = _bias_table_call(rel_bias, buckets)
    bk = min(s, 2048)

    a, pooled, ypool, q32, k32, qn, kn, v = _f1_call(x, g1r, win, pw_c, psr, qgr, kgr, tm=512)
    o, lse = _attn_fwd_call(qn, kn, v, bias)
    wout, wup, wdown = mlp_weights(o)
    mixed, c, ff, dz, dy, dh1, dypool, do, delta, dg2, loss = _f2_call(x, tgt, ypool, o, wout, wup, wdown, g2r, tm=256)
    dff = ff.shape[1]
    g_out = [g.reshape(N_CHIPS, d // N_CHIPS, d)
             for g in _wgrad_call(mixed, dh1, d, d, bk // 2, (d, d), (d, d), lambda i, j, k: (0, 0), "wgrad_out")]
    g_up = _wgrad_call(c, dz, d, dff // N_CHIPS, bk, (N_CHIPS, d, dff // N_CHIPS), (None, d, dff // N_CHIPS),
                       lambda i, j, k: (j, 0, 0), "wgrad_up")
    g_down = _wgrad_call(ff, dy, dff // N_CHIPS, d, bk, (N_CHIPS, dff // N_CHIPS, d), (None, dff // N_CHIPS, d),
                         lambda i, j, k: (i, 0, 0), "wgrad_down")
    dep = None if on_mlp_grads is None else on_mlp_grads(g_out[1], g_up[1], g_down[1])
    dqn, dkn, dv, dbias = _attn_bwd_call(qn, kn, v, do, lse, delta, bias, dep)
    dx, dproj, dg1, dqg, dkg, dpw, dps = _bproj_call(
        dqn, dkn, dv, q32, k32, dypool, pooled, x, dh1, win, pw_c, psr, qgr, kgr, g1r, tm=256)
    nin = dproj.shape[1] // N_CHIPS
    g_in = _wgrad_call(a, dproj, d, nin, bk, (N_CHIPS, d, nin), (None, d, nin), lambda i, j, k: (j, 0, 0), "wgrad_in")
    drb = _rel_bias_grad_call(dbias, buckets)
    small = dict(
        mix_norm_g=dg1.reshape(d), mlp_norm_g=dg2.reshape(d), pool_scale=dps.reshape(POOL_WIDTH),
        q_norm_g=dqg.reshape(ATTN_WIDTH), k_norm_g=dkg.reshape(ATTN_WIDTH),
        rel_bias=drb[:, :N_BUCKETS].T, pool_w=dpw)
    return loss[0, 0], dx, (g_in, g_out, g_up, g_down), small


def _coords():
    return lax.axis_index("x"), lax.axis_index("y"), lax.axis_index("c")


def _other_chips(x, y):
    return [(1 - x, y), (x, 1 - y), (1 - x, 1 - y)]


def _remote(src, dst, send_sem, recv_sem, dev):
    return pltpu.make_async_remote_copy(src_ref=src, dst_ref=dst, send_sem=send_sem, recv_sem=recv_sem,
                                        device_id=dev, device_id_type=MESH)


def _halves(a):
    return a.reshape(a.shape[:-2] + (2, a.shape[-2] // 2, a.shape[-1]))


def _place_shards_call(shards, chip_idx, nch):
    nw = len(shards)

    def body(chip_ref, *refs):
        for w in range(nw):
            refs[nw + w][...] = refs[w][...].astype(WIRE_DTYPE)

    in_specs = [pl.BlockSpec((s.shape[0] // nch, s.shape[1]), lambda i, chip_ref: (i, 0)) for s in shards]
    out_specs = [pl.BlockSpec((None, s.shape[0] // nch, s.shape[1]), lambda i, chip_ref: (chip_ref[0], i, 0))
                 for s in shards]
    return pl.pallas_call(
        body, name="weights_place",
        grid_spec=pltpu.PrefetchScalarGridSpec(num_scalar_prefetch=1, grid=(nch,),
                                               in_specs=in_specs, out_specs=out_specs),
        out_shape=[jax.ShapeDtypeStruct((N_CHIPS,) + s.shape, WIRE_DTYPE) for s in shards],
        compiler_params=_params(("arbitrary",)),
    )(chip_idx, *shards)


def _allgather_call(placed, from_chips, name):
    nw = len(placed)
    ncp = 3 * nw

    def body(*refs):
        outs = refs[nw:2 * nw]
        send1, recv1, send2, recv2 = refs[2 * nw:]
        x, y, c = _coords()
        chip = 2 * x + y
        others = _other_chips(x, y)
        first, passed = [], []
        if from_chips:
            for w in range(nw):
                for k, (ox, oy) in enumerate(others):
                    mine = outs[w].at[chip, c]
                    cp = _remote(mine, mine, send1.at[3 * w + k], recv1.at[3 * w + k], (ox, oy, c))
                    cp.start()
                    first.append(cp)
        for w in range(nw):
            for k, (ox, oy) in enumerate(others):
                piece = outs[w].at[2 * ox + oy, c]
                if from_chips:
                    _remote(piece, piece, send1.at[3 * w + k], recv1.at[3 * w + k], (ox, oy, c)).wait_recv()
                cp = _remote(piece, piece, send2.at[3 * w + k], recv2.at[3 * w + k], (x, y, 1 - c))
                cp.start()
                passed.append(cp)
        for w in range(nw):
            for k, (ox, oy) in enumerate(others):
                piece = outs[w].at[2 * ox + oy, 1 - c]
                _remote(piece, piece, send2.at[3 * w + k], recv2.at[3 * w + k], (x, y, 1 - c)).wait_recv()
        for cp in first + passed:
            cp.wait_send()

    return pl.pallas_call(
        body, name=name,
        in_specs=[ANY] * nw, out_specs=[ANY] * nw,
        out_shape=[jax.ShapeDtypeStruct(s.shape, s.dtype) for s in placed],
        input_output_aliases={w: w for w in range(nw)},
        scratch_shapes=[pltpu.SemaphoreType.DMA((ncp,))] * 4,
    )(*placed)


HBM_SPEC = pl.BlockSpec(memory_space=pltpu.HBM)
SEM_SPEC = pl.BlockSpec(memory_space=pltpu.SEMAPHORE)
SPLIT_EFFECT = pltpu.SideEffectType.DATAFLOW_SIDE_EFFECTING


def _in_hbm(a):
    return pltpu.with_memory_space_constraint(a, pltpu.HBM)


def _gather_copies(bufs, send, recv):
    x, y, c = _coords()
    chip = 2 * x + y
    cps = []
    for w, buf in enumerate(bufs):
        for k, (ox, oy) in enumerate(_other_chips(x, y)):
            mine, theirs = buf.at[chip, c], buf.at[2 * ox + oy, c]
            sems = (send.at[3 * w + k], recv.at[3 * w + k], (ox, oy, c))
            cps.append((_remote(mine, mine, *sems), _remote(theirs, theirs, *sems)))
    return cps


def _gather_start_call(bufs):
    nw = len(bufs)

    def body(*refs):
        ins, send, recv, token = refs[:nw], refs[nw], refs[nw + 1], refs[2 * nw + 2]
        for out, _ in _gather_copies(ins, send, recv):
            out.start()
        token[...] = jnp.zeros(token.shape, F32)

    res = pl.pallas_call(
        body, name="weights_gather_start",
        in_specs=[HBM_SPEC] * nw,
        out_specs=[SEM_SPEC, SEM_SPEC] + [HBM_SPEC] * nw + [pl.BlockSpec(memory_space=pltpu.VMEM)],
        out_shape=[pltpu.SemaphoreType.DMA((3 * nw,)), pltpu.SemaphoreType.DMA((3 * nw,))]
        + [pltpu.HBM(b.shape, b.dtype) for b in bufs] + [jax.ShapeDtypeStruct((8, LANES), F32)],
        input_output_aliases={w: 2 + w for w in range(nw)},
        compiler_params=pltpu.CompilerParams(has_side_effects=SPLIT_EFFECT),
    )(*[_in_hbm(b) for b in bufs])
    return res[0], res[1], list(res[2:2 + nw]), res[2 + nw]


def _gather_wait_call(bufs, send, recv, after):
    nw = len(bufs)

    def body(*refs):
        ins, send, recv = refs[:nw], refs[nw], refs[nw + 1]
        for out, back in _gather_copies(ins, send, recv):
            out.wait_send()
            back.wait_recv()

    return pl.pallas_call(
        body, name="weights_gather_wait",
        in_specs=[HBM_SPEC] * nw + [SEM_SPEC, SEM_SPEC, ANY],
        out_specs=[HBM_SPEC] * nw,
        out_shape=[pltpu.HBM(b.shape, b.dtype) for b in bufs],
        input_output_aliases={w: w for w in range(nw)},
        compiler_params=pltpu.CompilerParams(has_side_effects=SPLIT_EFFECT),
    )(*bufs, send, recv, after)


def _scatter_copies(srcs, lands, send, recv, whole=False):
    x, y, c = _coords()
    me = 4 * x + 2 * y + c
    cps = []
    for w, (src, land) in enumerate(zip(srcs, lands)):
        for r in range(1, N_DEV):
            px, py, pc = ((1 - x) if r & 4 else x, (1 - y) if r & 2 else y, (1 - c) if r & 1 else c)
            sems = (send.at[(N_DEV - 1) * w + r - 1], recv.at[(N_DEV - 1) * w + r - 1], (px, py, pc))
            piece = src if whole else src.at[2 * px + py, pc]
            cps.append((_remote(piece, land.at[me], *sems), _remote(piece, land.at[4 * px + 2 * py + pc], *sems)))
    return cps


def _scatter_start_call(srcs, lands):
    nw = len(srcs)
    ncp = (N_DEV - 1) * nw

    def body(*refs):
        ins, lnd, send, recv, token = refs[:nw], refs[nw:2 * nw], refs[2 * nw], refs[2 * nw + 1], refs[4 * nw + 2]
        for out, _ in _scatter_copies(ins, lnd, send, recv):
            out.start()
        token[...] = jnp.zeros(token.shape, F32)

    res = pl.pallas_call(
        body, name="grads_scatter_start",
        in_specs=[HBM_SPEC] * (2 * nw),
        out_specs=[SEM_SPEC, SEM_SPEC] + [HBM_SPEC] * (2 * nw) + [pl.BlockSpec(memory_space=pltpu.VMEM)],
        out_shape=[pltpu.SemaphoreType.DMA((ncp,)), pltpu.SemaphoreType.DMA((ncp,))]
        + [pltpu.HBM(b.shape, b.dtype) for b in list(srcs) + list(lands)] + [jax.ShapeDtypeStruct((8, LANES), F32)],
        input_output_aliases={i: 2 + i for i in range(2 * nw)},
        compiler_params=pltpu.CompilerParams(has_side_effects=SPLIT_EFFECT),
    )(*[_in_hbm(b) for b in list(srcs) + list(lands)])
    return res[0], res[1], list(res[2:2 + nw]), list(res[2 + nw:2 + 2 * nw]), res[2 + 2 * nw]


def _scatter_wait_call(srcs, lands, send, recv, after):
    nw = len(srcs)

    def body(*refs):
        ins, lnd, send, recv = refs[:nw], refs[nw:2 * nw], refs[2 * nw], refs[2 * nw + 1]
        for out, back in _scatter_copies(ins, lnd, send, recv):
            out.wait_send()
            back.wait_recv()

    res = pl.pallas_call(
        body, name="grads_scatter_wait",
        in_specs=[HBM_SPEC] * (2 * nw) + [SEM_SPEC, SEM_SPEC, ANY],
        out_specs=[HBM_SPEC] * (2 * nw),
        out_shape=[pltpu.HBM(b.shape, b.dtype) for b in list(srcs) + list(lands)],
        input_output_aliases={i: i for i in range(2 * nw)},
        compiler_params=pltpu.CompilerParams(has_side_effects=SPLIT_EFFECT),
    )(*srcs, *lands, send, recv, after)
    return list(res[nw:])


def _scatter_sync_call(srcs, small):
    nw = len(srcs)
    ncp = (N_DEV - 1) * nw

    def body(*refs):
        ins, small_ref = refs[:nw], refs[nw]
        lnd, gathered = refs[nw + 1:2 * nw + 1], refs[2 * nw + 1]
        send, recv, ssend, srecv, loc = refs[2 * nw + 2:]
        x, y, c = _coords()
        me = 4 * x + 2 * y + c
        own = pltpu.make_async_copy(small_ref, gathered.at[me], loc)
        own.start()
        cps = _scatter_copies(ins, lnd, send, recv)
        cps += _scatter_copies([small_ref], [gathered], ssend, srecv, whole=True)
        for out, _ in cps:
            out.start()
        for out, back in cps:
            back.wait_recv()
            out.wait_send()
        own.wait()

    res = pl.pallas_call(
        body, name="grads_scatter_last",
        in_specs=[ANY] * (nw + 1), out_specs=[ANY] * (nw + 1),
        out_shape=[jax.ShapeDtypeStruct((N_DEV,) + s.shape[2:], s.dtype) for s in srcs]
        + [jax.ShapeDtypeStruct((N_DEV,) + small.shape, small.dtype)],
        scratch_shapes=[pltpu.SemaphoreType.DMA((ncp,)), pltpu.SemaphoreType.DMA((ncp,)),
                        pltpu.SemaphoreType.DMA((N_DEV - 1,)), pltpu.SemaphoreType.DMA((N_DEV - 1,)),
                        pltpu.SemaphoreType.DMA],
    )(*srcs, small)
    return list(res[:nw]), res[nw]


def _reduce_call(own, lands, idx, nch):
    nw = len(own)

    def body(idx_ref, *refs):
        for w in range(nw):
            tot = refs[w][...]
            for r in range(1, N_DEV):
                tot = tot + refs[nw + w][idx_ref[1 + r]].astype(F32)
            refs[2 * nw + w][...] = tot

    in_specs, out_specs, out_shape = [], [], []
    for s in own:
        in_specs.append(pl.BlockSpec((None, None, s.shape[2] // nch, s.shape[3]),
                                     lambda i, idx_ref: (idx_ref[0], idx_ref[1], i, 0)))
    for s in own:
        in_specs.append(pl.BlockSpec((N_DEV, s.shape[2] // nch, s.shape[3]), lambda i, idx_ref: (0, i, 0)))
    for s in own:
        out_specs.append(pl.BlockSpec((None, s.shape[2] // nch, s.shape[3]), lambda i, idx_ref: (idx_ref[1], i, 0)))
        out_shape.append(jax.ShapeDtypeStruct((2,) + s.shape[2:], F32))
    return pl.pallas_call(
        body, name="grads_reduce",
        grid_spec=pltpu.PrefetchScalarGridSpec(num_scalar_prefetch=1, grid=(nch,),
                                               in_specs=in_specs, out_specs=out_specs),
        out_shape=out_shape,
        compiler_params=_params(("arbitrary",)),
    )(idx, *own, *lands)


def _pair_allgather_call(halves):
    nw = len(halves)

    def body(*refs):
        outs = refs[nw:2 * nw]
        send, recv = refs[2 * nw:]
        x, y, c = _coords()
        cps = []
        for w in range(nw):
            cp = _remote(outs[w].at[c], outs[w].at[c], send.at[w], recv.at[w], (x, y, 1 - c))
            cp.start()
            cps.append(cp)
        for w in range(nw):
            theirs = outs[w].at[1 - c]
            _remote(theirs, theirs, send.at[w], recv.at[w], (x, y, 1 - c)).wait_recv()
        for cp in cps:
            cp.wait_send()

    outs = pl.pallas_call(
        body, name="grads_pair_allgather",
        in_specs=[ANY] * nw, out_specs=[ANY] * nw,
        out_shape=[jax.ShapeDtypeStruct(h.shape, h.dtype) for h in halves],
        input_output_aliases={w: w for w in range(nw)},
        scratch_shapes=[pltpu.SemaphoreType.DMA((nw,))] * 2,
    )(*halves)
    return [o.reshape(2 * h.shape[1], h.shape[2]) for o, h in zip(outs, halves)]


def _adamw(w, g, m, v):
    m = ADAM_B1 * m + (1.0 - ADAM_B1) * g
    v = ADAM_B2 * v + (1.0 - ADAM_B2) * (g * g)
    m_hat = m / (1.0 - ADAM_B1 ** ADAM_STEP)
    v_hat = v / (1.0 - ADAM_B2 ** ADAM_STEP)
    delta = -ADAM_LR * (m_hat / (jnp.sqrt(v_hat) + ADAM_EPS) + ADAM_WD * w)
    return delta, m, v


def _adamw_call(ws, gs, ms, vs, nch):
    nw = len(ws)

    def body(*refs):
        for w in range(nw):
            g = refs[nw + w][...]
            delta, m, v = _adamw(refs[w][...], g, refs[2 * nw + w][...], refs[3 * nw + w][...])
            refs[4 * nw + w][...] = g
            refs[5 * nw + w][...] = delta
            refs[6 * nw + w][...] = m
            refs[7 * nw + w][...] = v

    specs = [pl.BlockSpec((a.shape[0] // nch, a.shape[1]), lambda i: (i, 0)) for a in ws]
    res = pl.pallas_call(
        body, name="adamw_big",
        grid=(nch,),
        in_specs=specs * 4, out_specs=specs * 4,
        out_shape=[jax.ShapeDtypeStruct(a.shape, F32) for a in ws] * 4,
        compiler_params=_params(("arbitrary",)),
    )(*ws, *gs, *ms, *vs)
    return res[:nw], res[nw:2 * nw], res[2 * nw:3 * nw], res[3 * nw:]


def _small_call(gathered, w, m, v):
    def fold(row):
        tot = row[:, 0:LANES] + row[:, LANES:2 * LANES] + row[:, 2 * LANES:3 * LANES] + row[:, 3 * LANES:4 * LANES]
        return tot + pltpu.roll(tot, HEAD_DIM, axis=1)

    def body(ga_ref, w_ref, m_ref, v_ref, g_out, d_out, m_out, v_out):
        g = ga_ref[0]
        for i in range(1, N_DEV):
            g = g + ga_ref[i]
        unfolded = g[4:5, :]
        folded = jnp.concatenate([fold(unfolded[:, :ATTN_WIDTH]), fold(unfolded[:, ATTN_WIDTH:]),
                                  jnp.zeros((1, 1024 - 2 * LANES), F32)], axis=-1)
        row = lax.broadcasted_iota(jnp.int32, g.shape, 0)
        g = jnp.where(row == 3, folded, g)
        delta, mm, vv = _adamw(w_ref[...], g, m_ref[...], v_ref[...])
        g_out[...] = g
        d_out[...] = delta
        m_out[...] = mm
        v_out[...] = vv

    return pl.pallas_call(
        body, name="adamw_small",
        out_shape=[jax.ShapeDtypeStruct(w.shape, F32)] * 4,
        compiler_params=_params(),
    )(gathered, w, m, v)


def _pack_small(p, folded=True, loss=None):
    z = lambda n: jnp.zeros((n,), F32)
    rows = [p["mix_norm_g"], p["mlp_norm_g"],
            jnp.concatenate([p["pool_scale"], p["rel_bias"].reshape(-1), z(1024 - POOL_WIDTH - N_BUCKETS * N_HEADS)])]
    if folded:
        rows += [jnp.concatenate([p["q_norm_g"], z(LANES - HEAD_DIM), p["k_norm_g"], z(1024 - LANES - HEAD_DIM)]), z(1024)]
    else:
        rows += [z(1024), jnp.concatenate([p["q_norm_g"], p["k_norm_g"]])]
    rows += [z(1024) if loss is None else jnp.concatenate([loss.reshape(1), z(1023)])]
    head = jnp.stack(rows + [z(1024)] * 2)
    return jnp.concatenate([head, p["pool_w"].reshape(-1, 1024)], axis=0)


def _unpack_small(a):
    return dict(
        mix_norm_g=a[0], mlp_norm_g=a[1], pool_scale=a[2, :POOL_WIDTH],
        rel_bias=a[2, POOL_WIDTH:POOL_WIDTH + N_BUCKETS * N_HEADS].reshape(N_BUCKETS, N_HEADS),
        q_norm_g=a[3, :HEAD_DIM], k_norm_g=a[3, LANES:LANES + HEAD_DIM],
        pool_w=a[8:].reshape(len(POOL_WINDOWS), LANES, LANES))


_WEIGHT_ORDER = ("mix_norm_g", "w_in", "pool_w", "pool_scale", "q_norm_g", "k_norm_g", "rel_bias", "w_out",
                 "mlp_norm_g", "w_up", "w_down")
_BIG = ("w_in", "w_out", "w_up", "w_down")


def kernel(x, mix_norm_g, w_in, pool_w, pool_scale, q_norm_g, k_norm_g, rel_bias, w_out, mlp_norm_g, w_up, w_down, loss_target, m_mix_norm_g, m_w_in, m_pool_w, m_pool_scale, m_q_norm_g, m_k_norm_g, m_rel_bias, m_w_out, m_mlp_norm_g, m_w_up, m_w_down, v_mix_norm_g, v_w_in, v_pool_w, v_pool_scale, v_q_norm_g, v_k_norm_g, v_rel_bias, v_w_out, v_mlp_norm_g, v_w_up, v_w_down):
    w = dict(mix_norm_g=mix_norm_g, w_in=w_in, pool_w=pool_w, pool_scale=pool_scale, q_norm_g=q_norm_g,
             k_norm_g=k_norm_g, rel_bias=rel_bias, w_out=w_out, mlp_norm_g=mlp_norm_g, w_up=w_up, w_down=w_down)
    m = dict(mix_norm_g=m_mix_norm_g, w_in=m_w_in, pool_w=m_pool_w, pool_scale=m_pool_scale, q_norm_g=m_q_norm_g,
             k_norm_g=m_k_norm_g, rel_bias=m_rel_bias, w_out=m_w_out, mlp_norm_g=m_mlp_norm_g, w_up=m_w_up, w_down=m_w_down)
    v = dict(mix_norm_g=v_mix_norm_g, w_in=v_w_in, pool_w=v_pool_w, pool_scale=v_pool_scale, q_norm_g=v_q_norm_g,
             k_norm_g=v_k_norm_g, rel_bias=v_rel_bias, w_out=v_w_out, mlp_norm_g=v_mlp_norm_g, w_up=v_w_up, w_down=v_w_down)
    xc, yc, cc = _coords()

    c_idx = jnp.reshape(cc, (1,)).astype(jnp.int32)
    chip_idx = jnp.reshape(2 * xc + yc, (1,)).astype(jnp.int32)
    me = 4 * xc + 2 * yc + cc
    whole = lambda t: t.reshape(t.shape[0], t.shape[1] * t.shape[2], t.shape[3])

    placed = [_halves(p) for p in _place_shards_call([w[n] for n in _BIG], chip_idx, nch=4)]
    (win_f,) = _allgather_call(placed[:1], from_chips=True, name="weights_allgather_in")
    wsend, wrecv, in_flight, started = _gather_start_call(placed[1:])

    def mlp_weights(after):
        landed = _gather_wait_call(in_flight, wsend, wrecv, after)
        wout_f, wup_f, wdown_f = _allgather_call(landed, from_chips=False, name="weights_pair_forward")
        return whole(wout_f).reshape(-1, wout_f.shape[-1]), whole(wup_f), whole(wdown_f)

    split = []

    def on_mlp_grads(*wire_grads):
        srcs = [_halves(g) for g in wire_grads]
        lands = [lax.empty((N_DEV,) + s.shape[2:], s.dtype) for s in srcs]
        split.extend(_scatter_start_call(srcs, lands))
        return split[4]

    loss_part, dx, big_grads, small_grads = _local_grads(
        x[0], loss_target[0], mix_norm_g + started[0, 0], whole(win_f), pool_w, pool_scale, q_norm_g, k_norm_g, rel_bias,
        mlp_norm_g, mlp_weights, on_mlp_grads)
    g_in, g_out, g_up, g_down = big_grads
    gsend, grecv, srcs_thru, lands_thru, _ = split
    lands_mlp = _scatter_wait_call(srcs_thru, lands_thru, gsend, grecv, g_in[1])
    (land_in,), small_all = _scatter_sync_call(
        [_halves(g_in[1])], _pack_small(small_grads, folded=False, loss=loss_part))
    idx = jnp.concatenate([chip_idx, c_idx] + [jnp.reshape(jnp.bitwise_xor(me, r), (1,)) for r in range(1, N_DEV)])
    halves = _reduce_call([_halves(g[0]) for g in (g_in, g_out, g_up, g_down)], [land_in] + lands_mlp,
                          idx.astype(jnp.int32), nch=4)
    g_reduced = _pair_allgather_call(list(halves))
    g_big, d_big, m_big, v_big = _adamw_call(
        [w[n] for n in _BIG], g_reduced, [m[n] for n in _BIG], [v[n] for n in _BIG], nch=8)
    g_pack, d_pack, m_pack, v_pack = _small_call(small_all, _pack_small(w), _pack_small(m), _pack_small(v))

    grads, deltas, new_m, new_v = (_unpack_small(a) for a in (g_pack, d_pack, m_pack, v_pack))
    for i, n in enumerate(_BIG):
        grads[n], deltas[n], new_m[n], new_v[n] = g_big[i], d_big[i], m_big[i], v_big[i]
    loss = g_pack[LOSS_ROW, 0]
    return (loss, dx[None], *[grads[n] for n in _WEIGHT_ORDER], *[deltas[n] for n in _WEIGHT_ORDER],
            *[new_m[n] for n in _WEIGHT_ORDER], *[new_v[n] for n in _WEIGHT_ORDER])
```

```python
import math

import jax
import jax.numpy as jnp
import numpy as np
from jax import lax
from jax.experimental import pallas as pl
from jax.experimental.pallas import tpu as pltpu

F32 = jnp.float32
MXU_DTYPE = jnp.bfloat16
WIRE_DTYPE = jnp.bfloat16

NORM_EPS = 1e-6
NEG_INF = -1e30
LANES = 128
HEAD_DIM = 64
N_HEADS = 8
POOL_WIDTH = 512
ATTN_WIDTH = 512
POOL_WINDOWS = (2, 4, 8, 16)
POOL_HALO = 16
DILATED_PATTERNS = ((128, 1), (512, 4), (2048, 16))
ATT_BLOCK = 128
ATT_SUPER = ATT_BLOCK * max(dl for _, dl in DILATED_PATTERNS)
ATT_UNITS = ATT_SUPER // ATT_BLOCK
N_BUCKETS = 32
MAX_DISTANCE = 2048
N_CHIPS = 4
N_DEV = 8
ADAM_LR, ADAM_B1, ADAM_B2, ADAM_EPS, ADAM_WD, ADAM_STEP = 0.001, 0.9, 0.999, 1e-08, 0.01, 10
VMEM_LIMIT = 56 * 1024 * 1024
MESH = pl.DeviceIdType.MESH
ANY = pl.BlockSpec(memory_space=pl.ANY)

SMALL_ROWS = 72
LOSS_ROW = 5


def _mm(a, b):
    return jnp.dot(a, b, preferred_element_type=F32)


def _mm_nt(a, b):
    return lax.dot_general(a, b, (((1,), (1,)), ((), ())), preferred_element_type=F32)


def _mm_tn(a, b):
    return lax.dot_general(a, b, (((0,), (0,)), ((), ())), preferred_element_type=F32)


def _params(sem=None, **kw):
    if sem is not None:
        kw["dimension_semantics"] = sem
    return pltpu.CompilerParams(vmem_limit_bytes=VMEM_LIMIT, **kw)


def _low_half():
    return lax.broadcasted_iota(jnp.int32, (1, LANES), 1) < HEAD_DIM


def _head_sum_bcast(y):
    lo = _low_half()
    outs = []
    for j in range(y.shape[1] // LANES):
        c = y[:, j * LANES:(j + 1) * LANES]
        s_lo = jnp.sum(jnp.where(lo, c, 0.0), axis=-1, keepdims=True)
        s_hi = jnp.sum(jnp.where(lo, 0.0, c), axis=-1, keepdims=True)
        outs.append(jnp.where(lo, s_lo, s_hi))
    return jnp.concatenate(outs, axis=-1)


def _rms_bwd(dn, hn, r):
    return r * (dn - hn * jnp.mean(dn * hn, axis=-1, keepdims=True))


def _t5_bucket_np(dist):
    max_exact = N_BUCKETS // 2
    d_f = np.maximum(dist, 1).astype(np.float32)
    ratio = (np.log(d_f / np.float32(max_exact)) / np.float32(math.log(MAX_DISTANCE / max_exact))).astype(np.float32)
    large = max_exact + (ratio * np.float32(N_BUCKETS - max_exact)).astype(np.int32)
    large = np.minimum(large, N_BUCKETS - 1)
    return np.where(dist < max_exact, dist, large).astype(np.int32)


def _bucket_tables():
    qq = np.arange(ATT_BLOCK)[:, None]
    kk = np.arange(2 * ATT_BLOCK)[None, :]
    dist = np.clip(qq + ATT_BLOCK - kk, 0, ATT_BLOCK)
    return np.stack([_t5_bucket_np(dist * dl) for (_, dl) in DILATED_PATTERNS])


def _f1_call(x, g1, win, poolw, pscale, qg, kg, tm):
    s, d = x.shape
    nblk = s // tm

    def body(x_ref, g1_ref, win_ref, pw_ref, ps_ref, qg_ref, kg_ref,
             a_ref, pooled_ref, ypool_ref, q32_ref, k32_ref, qn_ref, kn_ref, v_ref, ubuf):
        i = pl.program_id(0)
        xv = x_ref[...]
        r = lax.rsqrt(jnp.mean(xv * xv, axis=-1, keepdims=True) + NORM_EPS)
        a = ((xv * r) * g1_ref[...]).astype(MXU_DTYPE)
        a_ref[...] = a
        u = _mm(a, win_ref[0])
        q = _mm(a, win_ref[1])
        k = _mm(a, win_ref[2])
        v_ref[...] = _mm(a, win_ref[3]).astype(MXU_DTYPE)
        q32_ref[...] = q
        k32_ref[...] = k
        rq = lax.rsqrt(_head_sum_bcast(q * q) * (1.0 / HEAD_DIM) + NORM_EPS)
        qn_ref[...] = (((q * rq) * qg_ref[...]) * (HEAD_DIM ** -0.5)).astype(MXU_DTYPE)
        rk = lax.rsqrt(_head_sum_bcast(k * k) * (1.0 / HEAD_DIM) + NORM_EPS)
        kn_ref[...] = ((k * rk) * kg_ref[...]).astype(MXU_DTYPE)

        @pl.when(i == 0)
        def _():
            ubuf[0:POOL_HALO, :] = jnp.zeros((POOL_HALO, POOL_WIDTH), F32)

        @pl.when(i > 0)
        def _():
            ubuf[0:POOL_HALO, :] = ubuf[tm:tm + POOL_HALO, :]

        ubuf[POOL_HALO:POOL_HALO + tm, :] = u
        t = i * tm + lax.broadcasted_iota(jnp.int32, (tm, 1), 0)
        for g, w in enumerate(POOL_WINDOWS):
            ls = slice(g * LANES, (g + 1) * LANES)
            ug = u[:, ls]
            acc = ug
            for sh in range(1, w):
                acc = acc + ubuf[POOL_HALO - sh:POOL_HALO - sh + tm, ls]
            cnt = jnp.minimum(t + 1, w).astype(F32)
            pooled = (acc / cnt - ug).astype(MXU_DTYPE)
            pooled_ref[:, ls] = pooled
            ypool_ref[:, ls] = (_mm(pooled, pw_ref[g]) * ps_ref[:, ls]).astype(MXU_DTYPE)

    tok = lambda w: pl.BlockSpec((tm, w), lambda i: (i, 0))
    full = lambda shp: pl.BlockSpec(shp, lambda i: (0,) * len(shp))
    return pl.pallas_call(
        body, name="fwd_inproj",
        grid=(nblk,),
        in_specs=[tok(d), full((1, d)), full(win.shape), full(poolw.shape), full((1, POOL_WIDTH)),
                  full((1, ATTN_WIDTH)), full((1, ATTN_WIDTH))],
        out_specs=[tok(d), tok(POOL_WIDTH), tok(POOL_WIDTH), tok(ATTN_WIDTH), tok(ATTN_WIDTH),
                   tok(ATTN_WIDTH), tok(ATTN_WIDTH), tok(ATTN_WIDTH)],
        out_shape=[jax.ShapeDtypeStruct((s, d), MXU_DTYPE),
                   jax.ShapeDtypeStruct((s, POOL_WIDTH), MXU_DTYPE),
                   jax.ShapeDtypeStruct((s, POOL_WIDTH), MXU_DTYPE),
                   jax.ShapeDtypeStruct((s, ATTN_WIDTH), F32),
                   jax.ShapeDtypeStruct((s, ATTN_WIDTH), F32),
                   jax.ShapeDtypeStruct((s, ATTN_WIDTH), MXU_DTYPE),
                   jax.ShapeDtypeStruct((s, ATTN_WIDTH), MXU_DTYPE),
                   jax.ShapeDtypeStruct((s, ATTN_WIDTH), MXU_DTYPE)],
        scratch_shapes=[pltpu.VMEM((tm + POOL_HALO, POOL_WIDTH), F32)],
        compiler_params=_params(("arbitrary",)),
    )(x, g1, win, poolw, pscale, qg, kg)


def _band_mask(n):
    qq = lax.broadcasted_iota(jnp.int32, (ATT_BLOCK, 2 * ATT_BLOCK), 0)
    kk = lax.broadcasted_iota(jnp.int32, (ATT_BLOCK, 2 * ATT_BLOCK), 1)
    dist = qq + ATT_BLOCK - kk
    return (dist >= 0) & (dist <= ATT_BLOCK) & ((n > 0) | (kk >= ATT_BLOCK))


def _unit_rows(u, dl):
    r = u % dl
    b = u // dl
    q0 = r + dl * ATT_BLOCK * b
    return b, pl.ds(q0, ATT_BLOCK, stride=dl), pl.ds(ATT_SUPER + q0 - dl * ATT_BLOCK, 2 * ATT_BLOCK, stride=dl)


def _attn_fwd_call(qn, kn, v, bias):
    s, w = qn.shape
    nsb = s // ATT_SUPER
    npair = w // LANES

    def body(q_ref, kc_ref, kp_ref, vc_ref, vp_ref, b_ref, o_ref, lse_ref, qf, kf, vf, acc_s, m_s, l_s):
        sb = pl.program_id(1)
        qf[...] = q_ref[...].astype(F32)
        kf[0:ATT_SUPER, :] = kp_ref[...].astype(F32)
        kf[ATT_SUPER:, :] = kc_ref[...].astype(F32)
        vf[0:ATT_SUPER, :] = vp_ref[...].astype(F32)
        vf[ATT_SUPER:, :] = vc_ref[...].astype(F32)
        lo = _low_half()
        for p, (_, dl) in enumerate(DILATED_PATTERNS):
            def unit(u, carry, p=p, dl=dl):
                b, rows_q, rows_k = _unit_rows(u, dl)
                valid = _band_mask(sb * (ATT_UNITS // dl) + b)
                qp = qf[rows_q, :].astype(MXU_DTYPE)
                kcat = kf[rows_k, :].astype(MXU_DTYPE)
                vcat = vf[rows_k, :].astype(MXU_DTYPE)
                zero = jnp.zeros_like(qp)
                q2 = jnp.concatenate([jnp.where(lo, qp, zero), jnp.where(lo, zero, qp)], axis=0)
                sc = _mm_nt(q2, kcat) + b_ref[p].reshape(2 * ATT_BLOCK, 2 * ATT_BLOCK)
                sc = jnp.where(jnp.concatenate([valid, valid], axis=0), sc, NEG_INF)
                m2 = jnp.max(sc, axis=-1, keepdims=True)
                pr = jnp.exp(sc - m2)
                l2 = jnp.sum(pr, axis=-1, keepdims=True)
                acc2 = _mm(pr.astype(MXU_DTYPE), vcat)
                acc = jnp.where(lo, acc2[:ATT_BLOCK], acc2[ATT_BLOCK:])
                m = jnp.where(lo, m2[:ATT_BLOCK], m2[ATT_BLOCK:])
                l = jnp.where(lo, l2[:ATT_BLOCK], l2[ATT_BLOCK:])
                if p == 0:
                    acc_s[rows_q, :] = acc
                    m_s[rows_q, :] = m
                    l_s[rows_q, :] = l
                else:
                    m_old = m_s[rows_q, :]
                    m_new = jnp.maximum(m_old, m)
                    a_old = jnp.exp(m_old - m_new)
                    a_new = jnp.exp(m - m_new)
                    acc_s[rows_q, :] = a_old * acc_s[rows_q, :] + a_new * acc
                    l_s[rows_q, :] = a_old * l_s[rows_q, :] + a_new * l
                    m_s[rows_q, :] = m_new
                return carry

            lax.fori_loop(0, ATT_UNITS, unit, 0, unroll=4)
        l = l_s[...]
        o_ref[...] = acc_s[...] / l
        lse_ref[...] = m_s[...] + jnp.log(l)

    cur = pl.BlockSpec((ATT_SUPER, LANES), lambda j, t: (t, j))
    prev = pl.BlockSpec((ATT_SUPER, LANES), lambda j, t: (jnp.maximum(t - 1, 0), j))
    bspec = pl.BlockSpec((len(DILATED_PATTERNS), 2, ATT_BLOCK, 2 * ATT_BLOCK), lambda j, t: (0, j, 0, 0))
    return pl.pallas_call(
        body, name="attn_fwd",
        grid=(npair, nsb),
        in_specs=[cur, cur, prev, cur, prev, bspec],
        out_specs=[cur, cur],
        out_shape=[jax.ShapeDtypeStruct((s, w), F32), jax.ShapeDtypeStruct((s, w), F32)],
        scratch_shapes=[pltpu.VMEM((ATT_SUPER, LANES), F32), pltpu.VMEM((2 * ATT_SUPER, LANES), F32),
                        pltpu.VMEM((2 * ATT_SUPER, LANES), F32), pltpu.VMEM((ATT_SUPER, LANES), F32),
                        pltpu.VMEM((ATT_SUPER, LANES), F32), pltpu.VMEM((ATT_SUPER, LANES), F32)],
        compiler_params=_params(("arbitrary", "arbitrary")),
    )(qn, kn, kn, v, v, bias)


def _attn_bwd_call(qn, kn, v, do, lse, delta, bias, dep=None):
    s, w = qn.shape
    nsb = s // ATT_SUPER
    npair = w // LANES
    deps = [] if dep is None else [dep]

    def body(q_ref, kc_ref, kp_ref, vc_ref, vp_ref, do_ref, lse_ref, dlt_ref, b_ref, *rest):
        dq_ref, dk_ref, dv_ref, db_ref, qf, kf, vf, dof, dkf, dvf = rest[len(deps):]
        step = pl.program_id(1)
        sb = nsb - 1 - step
        qf[...] = q_ref[...].astype(F32)
        dof[...] = do_ref[...].astype(F32)
        kf[0:ATT_SUPER, :] = kp_ref[...].astype(F32)
        kf[ATT_SUPER:, :] = kc_ref[...].astype(F32)
        vf[0:ATT_SUPER, :] = vp_ref[...].astype(F32)
        vf[ATT_SUPER:, :] = vc_ref[...].astype(F32)

        @pl.when(step == 0)
        def _():
            db_ref[...] = jnp.zeros(db_ref.shape, F32)
            dkf[ATT_SUPER:, :] = jnp.zeros((ATT_SUPER, LANES), F32)
            dvf[ATT_SUPER:, :] = jnp.zeros((ATT_SUPER, LANES), F32)

        @pl.when(step > 0)
        def _():
            dkf[ATT_SUPER:, :] = dkf[0:ATT_SUPER, :]
            dvf[ATT_SUPER:, :] = dvf[0:ATT_SUPER, :]

        dkf[0:ATT_SUPER, :] = jnp.zeros((ATT_SUPER, LANES), F32)
        dvf[0:ATT_SUPER, :] = jnp.zeros((ATT_SUPER, LANES), F32)
        lo = _low_half()
        for p, (_, dl) in enumerate(DILATED_PATTERNS):
            def unit(u, carry, p=p, dl=dl):
                b, rows_q, rows_k = _unit_rows(u, dl)
                valid = _band_mask(sb * (ATT_UNITS // dl) + b)
                qp = qf[rows_q, :].astype(MXU_DTYPE)
                dop = dof[rows_q, :].astype(MXU_DTYPE)
                kcat = kf[rows_k, :].astype(MXU_DTYPE)
                vcat = vf[rows_k, :].astype(MXU_DTYPE)
                lse2 = lse_ref[rows_q, :]
                dlt2 = dlt_ref[rows_q, :]
                zero = jnp.zeros_like(qp)
                q2 = jnp.concatenate([jnp.where(lo, qp, zero), jnp.where(lo, zero, qp)], axis=0)
                do2 = jnp.concatenate([jnp.where(lo, dop, zero), jnp.where(lo, zero, dop)], axis=0)
                lse_c = jnp.concatenate([lse2[:, 0:1], lse2[:, HEAD_DIM:HEAD_DIM + 1]], axis=0)
                dlt_c = jnp.concatenate([dlt2[:, 0:1], dlt2[:, HEAD_DIM:HEAD_DIM + 1]], axis=0)
                sc = _mm_nt(q2, kcat) + b_ref[p].reshape(2 * ATT_BLOCK, 2 * ATT_BLOCK)
                pr = jnp.where(jnp.concatenate([valid, valid], axis=0), jnp.exp(sc - lse_c), 0.0)
                ds = pr * (_mm_nt(do2, vcat) - dlt_c)
                db_ref[p] += ds.reshape(2, ATT_BLOCK, 2 * ATT_BLOCK)
                ds_c = ds.astype(MXU_DTYPE)
                dq2 = _mm(ds_c, kcat)
                dk = _mm_tn(ds_c, q2)
                dv = _mm_tn(pr.astype(MXU_DTYPE), do2)
                dq = jnp.where(lo, dq2[:ATT_BLOCK], dq2[ATT_BLOCK:])
                if p == 0:
                    dq_ref[rows_q, :] = dq
                else:
                    dq_ref[rows_q, :] += dq
                dkf[rows_k, :] += dk
                dvf[rows_k, :] += dv
                return carry

            lax.fori_loop(0, ATT_UNITS, unit, 0, unroll=4)
        dk_ref[...] = dkf[ATT_SUPER:, :]
        dv_ref[...] = dvf[ATT_SUPER:, :]

    cur = pl.BlockSpec((ATT_SUPER, LANES), lambda j, t: (nsb - 1 - t, j))
    prev = pl.BlockSpec((ATT_SUPER, LANES), lambda j, t: (jnp.maximum(nsb - 2 - t, 0), j))
    bshape = (len(DILATED_PATTERNS), 2, ATT_BLOCK, 2 * ATT_BLOCK)
    bspec = pl.BlockSpec(bshape, lambda j, t: (0, j, 0, 0))
    sup = lambda: pltpu.VMEM((ATT_SUPER, LANES), F32)
    sup2 = lambda: pltpu.VMEM((2 * ATT_SUPER, LANES), F32)
    return pl.pallas_call(
        body, name="attn_bwd",
        grid=(npair, nsb),
        in_specs=[cur, cur, prev, cur, prev, cur, cur, cur, bspec] + [ANY] * len(deps),
        out_specs=[cur, cur, cur, bspec],
        out_shape=[jax.ShapeDtypeStruct((s, w), F32)] * 3 + [jax.ShapeDtypeStruct(bias.shape, F32)],
        scratch_shapes=[sup(), sup2(), sup2(), sup(), sup2(), sup2()],
        compiler_params=_params(("arbitrary", "arbitrary")),
    )(qn, kn, kn, v, v, do, lse, delta, bias, *deps)


def _bias_table_call(rel_bias, buckets):
    npat = buckets.shape[0]

    def body(rb_ref, bk_ref, out_ref):
        for p in range(npat):
            for half in range(2):
                ks = slice(half * ATT_BLOCK, (half + 1) * ATT_BLOCK)
                bk = bk_ref[p, :, ks]
                for h in range(N_HEADS):
                    def pick(b, acc, h=h, bk=bk):
                        return jnp.where(bk == b, rb_ref[b, h], acc)

                    out_ref[p, h, :, ks] = lax.fori_loop(0, N_BUCKETS, pick, jnp.zeros((ATT_BLOCK, ATT_BLOCK), F32))

    return pl.pallas_call(
        body, name="bias_table",
        in_specs=[pl.BlockSpec(memory_space=pltpu.SMEM), pl.BlockSpec(memory_space=pltpu.VMEM)],
        out_shape=jax.ShapeDtypeStruct((npat, N_HEADS, ATT_BLOCK, 2 * ATT_BLOCK), F32),
        compiler_params=_params(),
    )(rel_bias, buckets)


def _rel_bias_grad_call(dbias, buckets):
    npat, nh = dbias.shape[0], dbias.shape[1]

    def body(db_ref, bk_ref, out_ref):
        lane = lax.broadcasted_iota(jnp.int32, (nh, LANES), 1)
        out = jnp.zeros((nh, LANES), F32)
        for b in range(N_BUCKETS):
            tot = jnp.zeros((nh, 1), F32)
            for p in range(npat):
                hit = jnp.where(bk_ref[p][None] == b, db_ref[p], 0.0)
                tot = tot + jnp.sum(jnp.sum(hit, axis=2), axis=1, keepdims=True)
            out = jnp.where(lane == b, tot, out)
        out_ref[...] = out

    return pl.pallas_call(
        body, name="rel_bias_grad",
        out_shape=jax.ShapeDtypeStruct((nh, LANES), F32),
        compiler_params=_params(),
    )(dbias, buckets)


def _f2_call(x, tgt, ypool, o, wout, wup, wdown, g2, tm):
    s, d = x.shape
    nblk = s // tm
    nch, _, fch = wup.shape
    dff = nch * fch
    mixw = POOL_WIDTH + ATTN_WIDTH

    def body(x_ref, t_ref, yp_ref, o_ref, g2_ref, wout_hbm, wup_hbm, wdown_hbm,
             mixed_ref, c_ref, ff_ref, dz_ref, dy_ref, dh1_ref, dyp_ref, do_ref, dlt_ref, dg2_ref, loss_ref,
             wout_v, wup_v, wdown_v, rz):
        i = pl.program_id(0)

        @pl.when(i == 0)
        def _():
            pltpu.sync_copy(wout_hbm, wout_v)
            pltpu.sync_copy(wup_hbm, wup_v)
            pltpu.sync_copy(wdown_hbm, wdown_v)
            dg2_ref[...] = jnp.zeros(dg2_ref.shape, F32)
            loss_ref[...] = jnp.zeros(loss_ref.shape, F32)

        o = o_ref[...]
        mixed = jnp.concatenate([yp_ref[...], o.astype(MXU_DTYPE)], axis=-1)
        mixed_ref[...] = mixed
        h1 = x_ref[...] + _mm(mixed, wout_v[...])
        r2 = lax.rsqrt(jnp.mean(h1 * h1, axis=-1, keepdims=True) + NORM_EPS)
        hn = h1 * r2
        c = (hn * g2_ref[...]).astype(MXU_DTYPE)
        c_ref[...] = c
        y = h1
        for j in range(nch):
            cs = slice(j * fch, (j + 1) * fch)
            z = jnp.maximum(_mm(c, wup_v[j]), 0.0)
            rz[:, cs] = z
            ff = (z * z).astype(MXU_DTYPE)
            ff_ref[:, cs] = ff
            y = y + _mm(ff, wdown_v[j])
        err = y - t_ref[...]
        loss_ref[...] += jnp.sum(err * err) * (0.5 / d)
        dy = err * (1.0 / d)
        dy_c = dy.astype(MXU_DTYPE)
        dy_ref[...] = dy_c
        dc = jnp.zeros((tm, d), F32)
        for j in range(nch):
            cs = slice(j * fch, (j + 1) * fch)
            dz = (_mm_nt(dy_c, wdown_v[j]) * (2.0 * rz[:, cs])).astype(MXU_DTYPE)
            dz_ref[:, cs] = dz
            dc = dc + _mm_nt(dz, wup_v[j])
        dg2_ref[...] += jnp.sum(dc * hn, axis=0, keepdims=True)
        dh1 = dy + _rms_bwd(dc * g2_ref[...], hn, r2)
        dh1_ref[...] = dh1
        dmix = _mm_nt(dh1.astype(MXU_DTYPE), wout_v[...])
        dyp_ref[...] = dmix[:, :POOL_WIDTH]
        do = dmix[:, POOL_WIDTH:]
        do_ref[...] = do.astype(MXU_DTYPE)
        dlt_ref[...] = _head_sum_bcast(do * o)

    tok = lambda w: pl.BlockSpec((tm, w), lambda i: (i, 0))
    const = lambda shp: pl.BlockSpec(shp, lambda i: (0,) * len(shp))
    return pl.pallas_call(
        body, name="fwd_mlp_bwd_mlp",
        grid=(nblk,),
        in_specs=[tok(d), tok(d), tok(POOL_WIDTH), tok(ATTN_WIDTH), const((1, d)), ANY, ANY, ANY],
        out_specs=[tok(mixw), tok(d), tok(dff), tok(dff), tok(d), tok(d), tok(POOL_WIDTH), tok(ATTN_WIDTH),
                   tok(ATTN_WIDTH), const((1, d)), const((1, LANES))],
        out_shape=[jax.ShapeDtypeStruct((s, mixw), MXU_DTYPE),
                   jax.ShapeDtypeStruct((s, d), MXU_DTYPE),
                   jax.ShapeDtypeStruct((s, dff), MXU_DTYPE),
                   jax.ShapeDtypeStruct((s, dff), MXU_DTYPE),
                   jax.ShapeDtypeStruct((s, d), MXU_DTYPE),
                   jax.ShapeDtypeStruct((s, d), F32),
                   jax.ShapeDtypeStruct((s, POOL_WIDTH), F32),
                   jax.ShapeDtypeStruct((s, ATTN_WIDTH), MXU_DTYPE),
                   jax.ShapeDtypeStruct((s, ATTN_WIDTH), F32),
                   jax.ShapeDtypeStruct((1, d), F32),
                   jax.ShapeDtypeStruct((1, LANES), F32)],
        scratch_shapes=[pltpu.VMEM(wout.shape, MXU_DTYPE), pltpu.VMEM(wup.shape, MXU_DTYPE),
                        pltpu.VMEM(wdown.shape, MXU_DTYPE), pltpu.VMEM((tm, dff), F32)],
        compiler_params=_params(("arbitrary",)),
    )(x, tgt, ypool, o, g2, wout, wup, wdown)


def _bproj_call(dqn, dkn, dv, q32, k32, dypool, pooled, x, dh1, win, poolw, pscale, qg, kg, g1, tm):
    s, d = x.shape
    nblk = s // tm
    ngrp = len(POOL_WINDOWS)

    def body(dqn_ref, dkn_ref, dv_ref, q_ref, k_ref, dyp_ref, pooled_ref, x_ref, dh1_ref,
             win_hbm, pw_ref, ps_ref, qg_ref, kg_ref, g1_ref,
             dx_ref, dproj_ref, dg1_ref, dqg_ref, dkg_ref, dpw_ref, dps_ref, win_v, ebuf):
        step = pl.program_id(0)
        i = nblk - 1 - step

        @pl.when(step == 0)
        def _():
            pltpu.sync_copy(win_hbm, win_v)
            dg1_ref[...] = jnp.zeros(dg1_ref.shape, F32)
            dqg_ref[...] = jnp.zeros(dqg_ref.shape, F32)
            dkg_ref[...] = jnp.zeros(dkg_ref.shape, F32)
            dpw_ref[...] = jnp.zeros(dpw_ref.shape, F32)
            dps_ref[...] = jnp.zeros(dps_ref.shape, F32)
            ebuf[tm:tm + POOL_HALO, :] = jnp.zeros((POOL_HALO, POOL_WIDTH), F32)

        @pl.when(step > 0)
        def _():
            ebuf[tm:tm + POOL_HALO, :] = ebuf[0:POOL_HALO, :]

        def qk_bwd(dn_sum, raw, gain, scale, dgain_ref):
            rr = lax.rsqrt(_head_sum_bcast(raw * raw) * (1.0 / HEAD_DIM) + NORM_EPS)
            hn = raw * rr
            dgain_ref[...] += jnp.sum(dn_sum * hn, axis=0, keepdims=True) * scale
            dn = dn_sum * (gain * scale)
            return rr * (dn - hn * (_head_sum_bcast(dn * hn) * (1.0 / HEAD_DIM)))

        dq = qk_bwd(dqn_ref[...], q_ref[...], qg_ref[...], HEAD_DIM ** -0.5, dqg_ref)
        dk = qk_bwd(dkn_ref[...], k_ref[...], kg_ref[...], 1.0, dkg_ref)

        t = i * tm + lax.broadcasted_iota(jnp.int32, (tm, 1), 0)
        dpooled = []
        for g, w in enumerate(POOL_WINDOWS):
            ls = slice(g * LANES, (g + 1) * LANES)
            dm = dyp_ref[:, ls]
            pg = pooled_ref[:, ls]
            dps_ref[:, ls] += jnp.sum(dm * _mm(pg, pw_ref[g]), axis=0, keepdims=True)
            dms = (dm * ps_ref[:, ls]).astype(MXU_DTYPE)
            dpw_ref[g] += _mm_tn(pg, dms)
            dpg = _mm_nt(dms, pw_ref[g])
            dpooled.append(dpg)
            ebuf[0:tm, ls] = dpg / jnp.minimum(t + 1, w).astype(F32)
        du = []
        for g, w in enumerate(POOL_WINDOWS):
            ls = slice(g * LANES, (g + 1) * LANES)
            acc = ebuf[0:tm, ls]
            for sh in range(1, w):
                acc = acc + ebuf[sh:sh + tm, ls]
            du.append(acc - dpooled[g])
        parts = [jnp.concatenate(du, axis=-1), dq, dk, dv_ref[...]]
        da = jnp.zeros((tm, d), F32)
        for p, part in enumerate(parts):
            pc = part.astype(MXU_DTYPE)
            dproj_ref[:, p * POOL_WIDTH:(p + 1) * POOL_WIDTH] = pc
            da = da + _mm_nt(pc, win_v[p])
        xv = x_ref[...]
        r = lax.rsqrt(jnp.mean(xv * xv, axis=-1, keepdims=True) + NORM_EPS)
        xn = xv * r
        dg1_ref[...] += jnp.sum(da * xn, axis=0, keepdims=True)
        dx_ref[...] = dh1_ref[...] + _rms_bwd(da * g1_ref[...], xn, r)

    tok = lambda w: pl.BlockSpec((tm, w), lambda t: (nblk - 1 - t, 0))
    const = lambda shp: pl.BlockSpec(shp, lambda t: (0,) * len(shp))
    return pl.pallas_call(
        body, name="bwd_inproj",
        grid=(nblk,),
        in_specs=[tok(ATTN_WIDTH)] * 5 + [tok(POOL_WIDTH), tok(POOL_WIDTH), tok(d), tok(d),
                                          ANY, const(poolw.shape), const((1, POOL_WIDTH)), const((1, ATTN_WIDTH)),
                                          const((1, ATTN_WIDTH)), const((1, d))],
        out_specs=[tok(d), tok(4 * POOL_WIDTH), const((1, d)), const((1, ATTN_WIDTH)), const((1, ATTN_WIDTH)),
                   const((ngrp, LANES, LANES)), const((1, POOL_WIDTH))],
        out_shape=[jax.ShapeDtypeStruct((s, d), F32),
                   jax.ShapeDtypeStruct((s, 4 * POOL_WIDTH), MXU_DTYPE),
                   jax.ShapeDtypeStruct((1, d), F32),
                   jax.ShapeDtypeStruct((1, ATTN_WIDTH), F32),
                   jax.ShapeDtypeStruct((1, ATTN_WIDTH), F32),
                   jax.ShapeDtypeStruct((ngrp, LANES, LANES), F32),
                   jax.ShapeDtypeStruct((1, POOL_WIDTH), F32)],
        scratch_shapes=[pltpu.VMEM(win.shape, MXU_DTYPE), pltpu.VMEM((tm + POOL_HALO, POOL_WIDTH), F32)],
        compiler_params=_params(("arbitrary",)),
    )(dqn, dkn, dv, q32, k32, dypool, pooled, x, dh1, win, poolw, pscale, qg, kg, g1)


def _wgrad_call(a, b, bm, bn, bk, out_shape, out_block, out_index, name):
    s, m = a.shape
    _, n = b.shape
    nk = s // bk

    def body(a_ref, b_ref, o_ref, wire_ref):
        k = pl.program_id(2)

        @pl.when(k == 0)
        def _():
            o_ref[...] = jnp.zeros(o_ref.shape, F32)

        o_ref[...] += _mm_tn(a_ref[...].astype(MXU_DTYPE), b_ref[...].astype(MXU_DTYPE))

        @pl.when(k == nk - 1)
        def _():
            wire_ref[...] = o_ref[...].astype(WIRE_DTYPE)

    return pl.pallas_call(
        body, name=name,
        grid=(m // bm, n // bn, nk),
        in_specs=[pl.BlockSpec((bk, bm), lambda i, j, k: (k, i)), pl.BlockSpec((bk, bn), lambda i, j, k: (k, j))],
        out_specs=[pl.BlockSpec(out_block, out_index)] * 2,
        out_shape=[jax.ShapeDtypeStruct(out_shape, F32), jax.ShapeDtypeStruct(out_shape, WIRE_DTYPE)],
        compiler_params=_params(("arbitrary", "arbitrary", "arbitrary")),
    )(a, b)


def _local_grads(x, tgt, g1, win, poolw, pscale, qg, kg, rel_bias, g2, mlp_weights, on_mlp_grads=None):
    s, d = x.shape
    g1r, g2r = g1.reshape(1, d), g2.reshape(1, d)
    psr = pscale.reshape(1, POOL_WIDTH)
    qgr = jnp.tile(qg, N_HEADS).reshape(1, ATTN_WIDTH)
    kgr = jnp.tile(kg, N_HEADS).reshape(1, ATTN_WIDTH)
    pw_c = poolw.astype(MXU_DTYPE)
    buckets = jnp.asarray(_bucket_tables())
    bias = _bias_table_call(rel_bias, buckets)
    bk = min(s, 2048)

    a, pooled, ypool, q32, k32, qn, kn, v = _f1_call(x, g1r, win, pw_c, psr, qgr, kgr, tm=512)
    o, lse = _attn_fwd_call(qn, kn, v, bias)
    wout, wup, wdown = mlp_weights(o)
    mixed, c, ff, dz, dy, dh1, dypool, do, delta, dg2, loss = _f2_call(x, tgt, ypool, o, wout, wup, wdown, g2r, tm=256)
    dff = ff.shape[1]
    g_out = [g.reshape(N_CHIPS, d // N_CHIPS, d)
             for g in _wgrad_call(mixed, dh1, d, d, bk // 2, (d, d), (d, d), lambda i, j, k: (0, 0), "wgrad_out")]
    g_up = _wgrad_call(c, dz, d, dff // N_CHIPS, bk, (N_CHIPS, d, dff // N_CHIPS), (None, d, dff // N_CHIPS),
                       lambda i, j, k: (j, 0, 0), "wgrad_up")
    g_down = _wgrad_call(ff, dy, dff // N_CHIPS, d, bk, (N_CHIPS, dff // N_CHIPS, d), (None, dff // N_CHIPS, d),
                         lambda i, j, k: (i, 0, 0), "wgrad_down")
    dep = None if on_mlp_grads is None else on_mlp_grads(g_out[1], g_up[1], g_down[1])
    dqn, dkn, dv, dbias = _attn_bwd_call(qn, kn, v, do, lse, delta, bias, dep)
    dx, dproj, dg1, dqg, dkg, dpw, dps = _bproj_call(
        dqn, dkn, dv, q32, k32, dypool, pooled, x, dh1, win, pw_c, psr, qgr, kgr, g1r, tm=256)
    nin = dproj.shape[1] // N_CHIPS
    g_in = _wgrad_call(a, dproj, d, nin, bk, (N_CHIPS, d, nin), (None, d, nin), lambda i, j, k: (j, 0, 0), "wgrad_in")
    drb = _rel_bias_grad_call(dbias, buckets)
    small = dict(
        mix_norm_g=dg1.reshape(d), mlp_norm_g=dg2.reshape(d), pool_scale=dps.reshape(POOL_WIDTH),
        q_norm_g=dqg.reshape(ATTN_WIDTH), k_norm_g=dkg.reshape(ATTN_WIDTH),
        rel_bias=drb[:, :N_BUCKETS].T, pool_w=dpw)
    return loss[0, 0], dx, (g_in, g_out, g_up, g_down), small


def _coords():
    return lax.axis_index("x"), lax.axis_index("y"), lax.axis_index("c")


def _other_chips(x, y):
    return [(1 - x, y), (x, 1 - y), (1 - x, 1 - y)]


def _remote(src, dst, send_sem, recv_sem, dev):
    return pltpu.make_async_remote_copy(src_ref=src, dst_ref=dst, send_sem=send_sem, recv_sem=recv_sem,
                                        device_id=dev, device_id_type=MESH)


def _halves(a):
    return a.reshape(a.shape[:-2] + (2, a.shape[-2] // 2, a.shape[-1]))


def _place_shards_call(shards, chip_idx, nch):
    nw = len(shards)

    def body(chip_ref, *refs):
        for w in range(nw):
            refs[nw + w][...] = refs[w][...].astype(WIRE_DTYPE)

    in_specs = [pl.BlockSpec((s.shape[0] // nch, s.shape[1]), lambda i, chip_ref: (i, 0)) for s in shards]
    out_specs = [pl.BlockSpec((None, s.shape[0] // nch, s.shape[1]), lambda i, chip_ref: (chip_ref[0], i, 0))
                 for s in shards]
    return pl.pallas_call(
        body, name="weights_place",
        grid_spec=pltpu.PrefetchScalarGridSpec(num_scalar_prefetch=1, grid=(nch,),
                                               in_specs=in_specs, out_specs=out_specs),
        out_shape=[jax.ShapeDtypeStruct((N_CHIPS,) + s.shape, WIRE_DTYPE) for s in shards],
        compiler_params=_params(("arbitrary",)),
    )(chip_idx, *shards)


def _allgather_call(placed, from_chips, name):
    nw = len(placed)
    ncp = 3 * nw

    def body(*refs):
        outs = refs[nw:2 * nw]
        send1, recv1, send2, recv2 = refs[2 * nw:]
        x, y, c = _coords()
        chip = 2 * x + y
        others = _other_chips(x, y)
        first, passed = [], []
        if from_chips:
            for w in range(nw):
                for k, (ox, oy) in enumerate(others):
                    mine = outs[w].at[chip, c]
                    cp = _remote(mine, mine, send1.at[3 * w + k], recv1.at[3 * w + k], (ox, oy, c))
                    cp.start()
                    first.append(cp)
        for w in range(nw):
            for k, (ox, oy) in enumerate(others):
                piece = outs[w].at[2 * ox + oy, c]
                if from_chips:
                    _remote(piece, piece, send1.at[3 * w + k], recv1.at[3 * w + k], (ox, oy, c)).wait_recv()
                cp = _remote(piece, piece, send2.at[3 * w + k], recv2.at[3 * w + k], (x, y, 1 - c))
                cp.start()
                passed.append(cp)
        for w in range(nw):
            for k, (ox, oy) in enumerate(others):
                piece = outs[w].at[2 * ox + oy, 1 - c]
                _remote(piece, piece, send2.at[3 * w + k], recv2.at[3 * w + k], (x, y, 1 - c)).wait_recv()
        for cp in first + passed:
            cp.wait_send()

    return pl.pallas_call(
        body, name=name,
        in_specs=[ANY] * nw, out_specs=[ANY] * nw,
        out_shape=[jax.ShapeDtypeStruct(s.shape, s.dtype) for s in placed],
        input_output_aliases={w: w for w in range(nw)},
        scratch_shapes=[pltpu.SemaphoreType.DMA((ncp,))] * 4,
    )(*placed)


HBM_SPEC = pl.BlockSpec(memory_space=pltpu.HBM)
SEM_SPEC = pl.BlockSpec(memory_space=pltpu.SEMAPHORE)
SPLIT_EFFECT = pltpu.SideEffectType.DATAFLOW_SIDE_EFFECTING


def _in_hbm(a):
    return pltpu.with_memory_space_constraint(a, pltpu.HBM)


def _gather_copies(bufs, send, recv):
    x, y, c = _coords()
    chip = 2 * x + y
    cps = []
    for w, buf in enumerate(bufs):
        for k, (ox, oy) in enumerate(_other_chips(x, y)):
            mine, theirs = buf.at[chip, c], buf.at[2 * ox + oy, c]
            sems = (send.at[3 * w + k], recv.at[3 * w + k], (ox, oy, c))
            cps.append((_remote(mine, mine, *sems), _remote(theirs, theirs, *sems)))
    return cps


def _gather_start_call(bufs, after):
    nw = len(bufs)

    def body(*refs):
        ins, send, recv, token = refs[:nw], refs[nw + 1], refs[nw + 2], refs[2 * nw + 3]
        for out, _ in _gather_copies(ins, send, recv):
            out.start()
        token[...] = jnp.zeros(token.shape, F32)

    res = pl.pallas_call(
        body, name="weights_gather_start",
        in_specs=[HBM_SPEC] * nw + [ANY],
        out_specs=[SEM_SPEC, SEM_SPEC] + [HBM_SPEC] * nw + [pl.BlockSpec(memory_space=pltpu.VMEM)],
        out_shape=[pltpu.SemaphoreType.DMA((3 * nw,)), pltpu.SemaphoreType.DMA((3 * nw,))]
        + [pltpu.HBM(b.shape, b.dtype) for b in bufs] + [jax.ShapeDtypeStruct((8, LANES), F32)],
        input_output_aliases={w: 2 + w for w in range(nw)},
        compiler_params=pltpu.CompilerParams(has_side_effects=SPLIT_EFFECT),
    )(*[_in_hbm(b) for b in bufs], after)
    return res[0], res[1], list(res[2:2 + nw]), res[2 + nw]


def _gather_wait_call(bufs, send, recv, after):
    nw = len(bufs)

    def body(*refs):
        ins, send, recv = refs[:nw], refs[nw], refs[nw + 1]
        for out, back in _gather_copies(ins, send, recv):
            out.wait_send()
            back.wait_recv()

    return pl.pallas_call(
        body, name="weights_gather_wait",
        in_specs=[HBM_SPEC] * nw + [SEM_SPEC, SEM_SPEC, ANY],
        out_specs=[HBM_SPEC] * nw,
        out_shape=[pltpu.HBM(b.shape, b.dtype) for b in bufs],
        input_output_aliases={w: w for w in range(nw)},
        compiler_params=pltpu.CompilerParams(has_side_effects=SPLIT_EFFECT),
    )(*bufs, send, recv, after)


def _scatter_copies(srcs, lands, send, recv, whole=False):
    x, y, c = _coords()
    me = 4 * x + 2 * y + c
    cps = []
    for w, (src, land) in enumerate(zip(srcs, lands)):
        for r in range(1, N_DEV):
            px, py, pc = ((1 - x) if r & 4 else x, (1 - y) if r & 2 else y, (1 - c) if r & 1 else c)
            sems = (send.at[(N_DEV - 1) * w + r - 1], recv.at[(N_DEV - 1) * w + r - 1], (px, py, pc))
            piece = src if whole else src.at[2 * px + py, pc]
            cps.append((_remote(piece, land.at[me], *sems), _remote(piece, land.at[4 * px + 2 * py + pc], *sems)))
    return cps


def _scatter_start_call(srcs, lands):
    nw = len(srcs)
    ncp = (N_DEV - 1) * nw

    def body(*refs):
        ins, lnd, send, recv, token = refs[:nw], refs[nw:2 * nw], refs[2 * nw], refs[2 * nw + 1], refs[4 * nw + 2]
        for out, _ in _scatter_copies(ins, lnd, send, recv):
            out.start()
        token[...] = jnp.zeros(token.shape, F32)

    res = pl.pallas_call(
        body, name="grads_scatter_start",
        in_specs=[HBM_SPEC] * (2 * nw),
        out_specs=[SEM_SPEC, SEM_SPEC] + [HBM_SPEC] * (2 * nw) + [pl.BlockSpec(memory_space=pltpu.VMEM)],
        out_shape=[pltpu.SemaphoreType.DMA((ncp,)), pltpu.SemaphoreType.DMA((ncp,))]
        + [pltpu.HBM(b.shape, b.dtype) for b in list(srcs) + list(lands)] + [jax.ShapeDtypeStruct((8, LANES), F32)],
        input_output_aliases={i: 2 + i for i in range(2 * nw)},
        compiler_params=pltpu.CompilerParams(has_side_effects=SPLIT_EFFECT),
    )(*[_in_hbm(b) for b in list(srcs) + list(lands)])
    return res[0], res[1], list(res[2:2 + nw]), list(res[2 + nw:2 + 2 * nw]), res[2 + 2 * nw]


def _scatter_wait_call(srcs, lands, send, recv, after):
    nw = len(srcs)

    def body(*refs):
        ins, lnd, send, recv = refs[:nw], refs[nw:2 * nw], refs[2 * nw], refs[2 * nw + 1]
        for out, back in _scatter_copies(ins, lnd, send, recv):
            out.wait_send()
            back.wait_recv()

    res = pl.pallas_call(
        body, name="grads_scatter_wait",
        in_specs=[HBM_SPEC] * (2 * nw) + [SEM_SPEC, SEM_SPEC, ANY],
        out_specs=[HBM_SPEC] * (2 * nw),
        out_shape=[pltpu.HBM(b.shape, b.dtype) for b in list(srcs) + list(lands)],
        input_output_aliases={i: i for i in range(2 * nw)},
        compiler_params=pltpu.CompilerParams(has_side_effects=SPLIT_EFFECT),
    )(*srcs, *lands, send, recv, after)
    return list(res[nw:])


def _scatter_sync_call(srcs, small):
    nw = len(srcs)
    ncp = (N_DEV - 1) * nw

    def body(*refs):
        ins, small_ref = refs[:nw], refs[nw]
        lnd, gathered = refs[nw + 1:2 * nw + 1], refs[2 * nw + 1]
        send, recv, ssend, srecv, loc = refs[2 * nw + 2:]
        x, y, c = _coords()
        me = 4 * x + 2 * y + c
        own = pltpu.make_async_copy(small_ref, gathered.at[me], loc)
        own.start()
        cps = _scatter_copies(ins, lnd, send, recv)
        cps += _scatter_copies([small_ref], [gathered], ssend, srecv, whole=True)
        for out, _ in cps:
            out.start()
        for out, back in cps:
            back.wait_recv()
            out.wait_send()
        own.wait()

    res = pl.pallas_call(
        body, name="grads_scatter_last",
        in_specs=[ANY] * (nw + 1), out_specs=[ANY] * (nw + 1),
        out_shape=[jax.ShapeDtypeStruct((N_DEV,) + s.shape[2:], s.dtype) for s in srcs]
        + [jax.ShapeDtypeStruct((N_DEV,) + small.shape, small.dtype)],
        scratch_shapes=[pltpu.SemaphoreType.DMA((ncp,)), pltpu.SemaphoreType.DMA((ncp,)),
                        pltpu.SemaphoreType.DMA((N_DEV - 1,)), pltpu.SemaphoreType.DMA((N_DEV - 1,)),
                        pltpu.SemaphoreType.DMA],
    )(*srcs, small)
    return list(res[:nw]), res[nw]


def _reduce_call(own, lands, idx, nch):
    nw = len(own)

    def body(idx_ref, *refs):
        for w in range(nw):
            tot = refs[w][...]
            for r in range(1, N_DEV):
                tot = tot + refs[nw + w][idx_ref[1 + r]].astype(F32)
            refs[2 * nw + w][...] = tot

    in_specs, out_specs, out_shape = [], [], []
    for s in own:
        in_specs.append(pl.BlockSpec((None, None, s.shape[2] // nch, s.shape[3]),
                                     lambda i, idx_ref: (idx_ref[0], idx_ref[1], i, 0)))
    for s in own:
        in_specs.append(pl.BlockSpec((N_DEV, s.shape[2] // nch, s.shape[3]), lambda i, idx_ref: (0, i, 0)))
    for s in own:
        out_specs.append(pl.BlockSpec((None, s.shape[2] // nch, s.shape[3]), lambda i, idx_ref: (idx_ref[1], i, 0)))
        out_shape.append(jax.ShapeDtypeStruct((2,) + s.shape[2:], F32))
    return pl.pallas_call(
        body, name="grads_reduce",
        grid_spec=pltpu.PrefetchScalarGridSpec(num_scalar_prefetch=1, grid=(nch,),
                                               in_specs=in_specs, out_specs=out_specs),
        out_shape=out_shape,
        compiler_params=_params(("arbitrary",)),
    )(idx, *own, *lands)


def _pair_allgather_call(halves):
    nw = len(halves)

    def body(*refs):
        outs = refs[nw:2 * nw]
        send, recv = refs[2 * nw:]
        x, y, c = _coords()
        cps = []
        for w in range(nw):
            cp = _remote(outs[w].at[c], outs[w].at[c], send.at[w], recv.at[w], (x, y, 1 - c))
            cp.start()
            cps.append(cp)
        for w in range(nw):
            theirs = outs[w].at[1 - c]
            _remote(theirs, theirs, send.at[w], recv.at[w], (x, y, 1 - c)).wait_recv()
        for cp in cps:
            cp.wait_send()

    outs = pl.pallas_call(
        body, name="grads_pair_allgather",
        in_specs=[ANY] * nw, out_specs=[ANY] * nw,
        out_shape=[jax.ShapeDtypeStruct(h.shape, h.dtype) for h in halves],
        input_output_aliases={w: w for w in range(nw)},
        scratch_shapes=[pltpu.SemaphoreType.DMA((nw,))] * 2,
    )(*halves)
    return [o.reshape(2 * h.shape[1], h.shape[2]) for o, h in zip(outs, halves)]


def _adamw(w, g, m, v):
    m = ADAM_B1 * m + (1.0 - ADAM_B1) * g
    v = ADAM_B2 * v + (1.0 - ADAM_B2) * (g * g)
    m_hat = m / (1.0 - ADAM_B1 ** ADAM_STEP)
    v_hat = v / (1.0 - ADAM_B2 ** ADAM_STEP)
    delta = -ADAM_LR * (m_hat / (jnp.sqrt(v_hat) + ADAM_EPS) + ADAM_WD * w)
    return delta, m, v


def _adamw_call(ws, gs, ms, vs, nch):
    nw = len(ws)

    def body(*refs):
        for w in range(nw):
            g = refs[nw + w][...]
            delta, m, v = _adamw(refs[w][...], g, refs[2 * nw + w][...], refs[3 * nw + w][...])
            refs[4 * nw + w][...] = g
            refs[5 * nw + w][...] = delta
            refs[6 * nw + w][...] = m
            refs[7 * nw + w][...] = v

    specs = [pl.BlockSpec((a.shape[0] // nch, a.shape[1]), lambda i: (i, 0)) for a in ws]
    res = pl.pallas_call(
        body, name="adamw_big",
        grid=(nch,),
        in_specs=specs * 4, out_specs=specs * 4,
        out_shape=[jax.ShapeDtypeStruct(a.shape, F32) for a in ws] * 4,
        compiler_params=_params(("arbitrary",)),
    )(*ws, *gs, *ms, *vs)
    return res[:nw], res[nw:2 * nw], res[2 * nw:3 * nw], res[3 * nw:]


def _small_call(gathered, w, m, v):
    def fold(row):
        tot = row[:, 0:LANES] + row[:, LANES:2 * LANES] + row[:, 2 * LANES:3 * LANES] + row[:, 3 * LANES:4 * LANES]
        return tot + pltpu.roll(tot, HEAD_DIM, axis=1)

    def body(ga_ref, w_ref, m_ref, v_ref, g_out, d_out, m_out, v_out):
        g = ga_ref[0]
        for i in range(1, N_DEV):
            g = g + ga_ref[i]
        unfolded = g[4:5, :]
        folded = jnp.concatenate([fold(unfolded[:, :ATTN_WIDTH]), fold(unfolded[:, ATTN_WIDTH:]),
                                  jnp.zeros((1, 1024 - 2 * LANES), F32)], axis=-1)
        row = lax.broadcasted_iota(jnp.int32, g.shape, 0)
        g = jnp.where(row == 3, folded, g)
        delta, mm, vv = _adamw(w_ref[...], g, m_ref[...], v_ref[...])
        g_out[...] = g
        d_out[...] = delta
        m_out[...] = mm
        v_out[...] = vv

    return pl.pallas_call(
        body, name="adamw_small",
        out_shape=[jax.ShapeDtypeStruct(w.shape, F32)] * 4,
        compiler_params=_params(),
    )(gathered, w, m, v)


def _pack_small(p, folded=True, loss=None):
    z = lambda n: jnp.zeros((n,), F32)
    rows = [p["mix_norm_g"], p["mlp_norm_g"],
            jnp.concatenate([p["pool_scale"], p["rel_bias"].reshape(-1), z(1024 - POOL_WIDTH - N_BUCKETS * N_HEADS)])]
    if folded:
        rows += [jnp.concatenate([p["q_norm_g"], z(LANES - HEAD_DIM), p["k_norm_g"], z(1024 - LANES - HEAD_DIM)]), z(1024)]
    else:
        rows += [z(1024), jnp.concatenate([p["q_norm_g"], p["k_norm_g"]])]
    rows += [z(1024) if loss is None else jnp.concatenate([loss.reshape(1), z(1023)])]
    head = jnp.stack(rows + [z(1024)] * 2)
    return jnp.concatenate([head, p["pool_w"].reshape(-1, 1024)], axis=0)


def _unpack_small(a):
    return dict(
        mix_norm_g=a[0], mlp_norm_g=a[1], pool_scale=a[2, :POOL_WIDTH],
        rel_bias=a[2, POOL_WIDTH:POOL_WIDTH + N_BUCKETS * N_HEADS].reshape(N_BUCKETS, N_HEADS),
        q_norm_g=a[3, :HEAD_DIM], k_norm_g=a[3, LANES:LANES + HEAD_DIM],
        pool_w=a[8:].reshape(len(POOL_WINDOWS), LANES, LANES))


_WEIGHT_ORDER = ("mix_norm_g", "w_in", "pool_w", "pool_scale", "q_norm_g", "k_norm_g", "rel_bias", "w_out",
                 "mlp_norm_g", "w_up", "w_down")
_BIG = ("w_in", "w_out", "w_up", "w_down")


def kernel(x, mix_norm_g, w_in, pool_w, pool_scale, q_norm_g, k_norm_g, rel_bias, w_out, mlp_norm_g, w_up, w_down, loss_target, m_mix_norm_g, m_w_in, m_pool_w, m_pool_scale, m_q_norm_g, m_k_norm_g, m_rel_bias, m_w_out, m_mlp_norm_g, m_w_up, m_w_down, v_mix_norm_g, v_w_in, v_pool_w, v_pool_scale, v_q_norm_g, v_k_norm_g, v_rel_bias, v_w_out, v_mlp_norm_g, v_w_up, v_w_down):
    w = dict(mix_norm_g=mix_norm_g, w_in=w_in, pool_w=pool_w, pool_scale=pool_scale, q_norm_g=q_norm_g,
             k_norm_g=k_norm_g, rel_bias=rel_bias, w_out=w_out, mlp_norm_g=mlp_norm_g, w_up=w_up, w_down=w_down)
    m = dict(mix_norm_g=m_mix_norm_g, w_in=m_w_in, pool_w=m_pool_w, pool_scale=m_pool_scale, q_norm_g=m_q_norm_g,
             k_norm_g=m_k_norm_g, rel_bias=m_rel_bias, w_out=m_w_out, mlp_norm_g=m_mlp_norm_g, w_up=m_w_up, w_down=m_w_down)
    v = dict(mix_norm_g=v_mix_norm_g, w_in=v_w_in, pool_w=v_pool_w, pool_scale=v_pool_scale, q_norm_g=v_q_norm_g,
             k_norm_g=v_k_norm_g, rel_bias=v_rel_bias, w_out=v_w_out, mlp_norm_g=v_mlp_norm_g, w_up=v_w_up, w_down=v_w_down)
    xc, yc, cc = _coords()

    c_idx = jnp.reshape(cc, (1,)).astype(jnp.int32)
    chip_idx = jnp.reshape(2 * xc + yc, (1,)).astype(jnp.int32)
    me = 4 * xc + 2 * yc + cc
    whole = lambda t: t.reshape(t.shape[0], t.shape[1] * t.shape[2], t.shape[3])

    placed = [_halves(p) for p in _place_shards_call([w[n] for n in _BIG], chip_idx, nch=4)]
    (win_f,) = _allgather_call(placed[:1], from_chips=True, name="weights_allgather_in")
    wsend, wrecv, in_flight, started = _gather_start_call(placed[1:], win_f)

    def mlp_weights(after):
        landed = _gather_wait_call(in_flight, wsend, wrecv, after)
        wout_f, wup_f, wdown_f = _allgather_call(landed, from_chips=False, name="weights_pair_forward")
        return whole(wout_f).reshape(-1, wout_f.shape[-1]), whole(wup_f), whole(wdown_f)

    split = []

    def on_mlp_grads(*wire_grads):
        srcs = [_halves(g) for g in wire_grads]
        lands = [lax.empty((N_DEV,) + s.shape[2:], s.dtype) for s in srcs]
        split.extend(_scatter_start_call(srcs, lands))
        return split[4]

    loss_part, dx, big_grads, small_grads = _local_grads(
        x[0], loss_target[0], mix_norm_g + started[0, 0], whole(win_f), pool_w, pool_scale, q_norm_g, k_norm_g, rel_bias,
        mlp_norm_g, mlp_weights, on_mlp_grads)
    g_in, g_out, g_up, g_down = big_grads
    gsend, grecv, srcs_thru, lands_thru, _ = split
    lands_mlp = _scatter_wait_call(srcs_thru, lands_thru, gsend, grecv, g_in[1])
    (land_in,), small_all = _scatter_sync_call(
        [_halves(g_in[1])], _pack_small(small_grads, folded=False, loss=loss_part))
    idx = jnp.concatenate([chip_idx, c_idx] + [jnp.reshape(jnp.bitwise_xor(me, r), (1,)) for r in range(1, N_DEV)])
    halves = _reduce_call([_halves(g[0]) for g in (g_in, g_out, g_up, g_down)], [land_in] + lands_mlp,
                          idx.astype(jnp.int32), nch=4)
    g_reduced = _pair_allgather_call(list(halves))
    g_big, d_big, m_big, v_big = _adamw_call(
        [w[n] for n in _BIG], g_reduced, [m[n] for n in _BIG], [v[n] for n in _BIG], nch=8)
    g_pack, d_pack, m_pack, v_pack = _small_call(small_all, _pack_small(w), _pack_small(m), _pack_small(v))

    grads, deltas, new_m, new_v = (_unpack_small(a) for a in (g_pack, d_pack, m_pack, v_pack))
    for i, n in enumerate(_BIG):
        grads[n], deltas[n], new_m[n], new_v[n] = g_big[i], d_big[i], m_big[i], v_big[i]
    loss = g_pack[LOSS_ROW, 0]
    return (loss, dx[None], *[grads[n] for n in _WEIGHT_ORDER], *[deltas[n] for n in _WEIGHT_ORDER],
            *[new_m[n] for n in _WEIGHT_ORDER], *[new_v[n] for n in _WEIGHT_ORDER])
```

```python
import math

import jax
import jax.numpy as jnp
import numpy as np
from jax import lax
from jax.experimental import pallas as pl
from jax.experimental.pallas import tpu as pltpu

F32 = jnp.float32
MXU_DTYPE = jnp.bfloat16
WIRE_DTYPE = jnp.bfloat16

NORM_EPS = 1e-6
NEG_INF = -1e30
LANES = 128
HEAD_DIM = 64
N_HEADS = 8
POOL_WIDTH = 512
ATTN_WIDTH = 512
POOL_WINDOWS = (2, 4, 8, 16)
POOL_HALO = 16
DILATED_PATTERNS = ((128, 1), (512, 4), (2048, 16))
ATT_BLOCK = 128
ATT_SUPER = ATT_BLOCK * max(dl for _, dl in DILATED_PATTERNS)
ATT_UNITS = ATT_SUPER // ATT_BLOCK
N_BUCKETS = 32
MAX_DISTANCE = 2048
N_CHIPS = 4
N_DEV = 8
ADAM_LR, ADAM_B1, ADAM_B2, ADAM_EPS, ADAM_WD, ADAM_STEP = 0.001, 0.9, 0.999, 1e-08, 0.01, 10
VMEM_LIMIT = 56 * 1024 * 1024
MESH = pl.DeviceIdType.MESH
ANY = pl.BlockSpec(memory_space=pl.ANY)

SMALL_ROWS = 72
LOSS_ROW = 5


def _mm(a, b):
    return jnp.dot(a, b, preferred_element_type=F32)


def _mm_nt(a, b):
    return lax.dot_general(a, b, (((1,), (1,)), ((), ())), preferred_element_type=F32)


def _mm_tn(a, b):
    return lax.dot_general(a, b, (((0,), (0,)), ((), ())), preferred_element_type=F32)


def _params(sem=None, **kw):
    if sem is not None:
        kw["dimension_semantics"] = sem
    return pltpu.CompilerParams(vmem_limit_bytes=VMEM_LIMIT, **kw)


def _low_half():
    return lax.broadcasted_iota(jnp.int32, (1, LANES), 1) < HEAD_DIM


def _head_sum_bcast(y):
    lo = _low_half()
    outs = []
    for j in range(y.shape[1] // LANES):
        c = y[:, j * LANES:(j + 1) * LANES]
        s_lo = jnp.sum(jnp.where(lo, c, 0.0), axis=-1, keepdims=True)
        s_hi = jnp.sum(jnp.where(lo, 0.0, c), axis=-1, keepdims=True)
        outs.append(jnp.where(lo, s_lo, s_hi))
    return jnp.concatenate(outs, axis=-1)


def _rms_bwd(dn, hn, r):
    return r * (dn - hn * jnp.mean(dn * hn, axis=-1, keepdims=True))


def _t5_bucket_np(dist):
    max_exact = N_BUCKETS // 2
    d_f = np.maximum(dist, 1).astype(np.float32)
    ratio = (np.log(d_f / np.float32(max_exact)) / np.float32(math.log(MAX_DISTANCE / max_exact))).astype(np.float32)
    large = max_exact + (ratio * np.float32(N_BUCKETS - max_exact)).astype(np.int32)
    large = np.minimum(large, N_BUCKETS - 1)
    return np.where(dist < max_exact, dist, large).astype(np.int32)


def _bucket_tables():
    tables = []
    for _, dl in DILATED_PATTERNS:
        qq, kk = np.arange(ATT_BLOCK), np.arange(2 * ATT_BLOCK)
        if dl == 1:
            qq, kk = _by4_positions(ATT_BLOCK), _by4_positions(2 * ATT_BLOCK)
        dist = np.clip(qq[:, None] + ATT_BLOCK - kk[None, :], 0, ATT_BLOCK)
        tables.append(_t5_bucket_np(dist * dl))
    return np.stack(tables)


def _f1_call(x, g1, win, poolw, pscale, qg, kg, tm):
    s, d = x.shape
    nblk = s // tm

    def body(x_ref, g1_ref, win_ref, pw_ref, ps_ref, qg_ref, kg_ref,
             a_ref, pooled_ref, ypool_ref, q32_ref, k32_ref, qn_ref, kn_ref, v_ref, ubuf):
        i = pl.program_id(0)
        xv = x_ref[...]
        r = lax.rsqrt(jnp.mean(xv * xv, axis=-1, keepdims=True) + NORM_EPS)
        a = ((xv * r) * g1_ref[...]).astype(MXU_DTYPE)
        a_ref[...] = a
        u = _mm(a, win_ref[0])
        q = _mm(a, win_ref[1])
        k = _mm(a, win_ref[2])
        v_ref[...] = _mm(a, win_ref[3]).astype(MXU_DTYPE)
        q32_ref[...] = q
        k32_ref[...] = k
        rq = lax.rsqrt(_head_sum_bcast(q * q) * (1.0 / HEAD_DIM) + NORM_EPS)
        qn_ref[...] = (((q * rq) * qg_ref[...]) * (HEAD_DIM ** -0.5)).astype(MXU_DTYPE)
        rk = lax.rsqrt(_head_sum_bcast(k * k) * (1.0 / HEAD_DIM) + NORM_EPS)
        kn_ref[...] = ((k * rk) * kg_ref[...]).astype(MXU_DTYPE)

        @pl.when(i == 0)
        def _():
            ubuf[0:POOL_HALO, :] = jnp.zeros((POOL_HALO, POOL_WIDTH), F32)

        @pl.when(i > 0)
        def _():
            ubuf[0:POOL_HALO, :] = ubuf[tm:tm + POOL_HALO, :]

        ubuf[POOL_HALO:POOL_HALO + tm, :] = u
        t = i * tm + lax.broadcasted_iota(jnp.int32, (tm, 1), 0)
        for g, w in enumerate(POOL_WINDOWS):
            ls = slice(g * LANES, (g + 1) * LANES)
            ug = u[:, ls]
            acc = ug
            for sh in range(1, w):
                acc = acc + ubuf[POOL_HALO - sh:POOL_HALO - sh + tm, ls]
            cnt = jnp.minimum(t + 1, w).astype(F32)
            pooled = (acc / cnt - ug).astype(MXU_DTYPE)
            pooled_ref[:, ls] = pooled
            ypool_ref[:, ls] = (_mm(pooled, pw_ref[g]) * ps_ref[:, ls]).astype(MXU_DTYPE)

    tok = lambda w: pl.BlockSpec((tm, w), lambda i: (i, 0))
    full = lambda shp: pl.BlockSpec(shp, lambda i: (0,) * len(shp))
    return pl.pallas_call(
        body, name="fwd_inproj",
        grid=(nblk,),
        in_specs=[tok(d), full((1, d)), full(win.shape), full(poolw.shape), full((1, POOL_WIDTH)),
                  full((1, ATTN_WIDTH)), full((1, ATTN_WIDTH))],
        out_specs=[tok(d), tok(POOL_WIDTH), tok(POOL_WIDTH), tok(ATTN_WIDTH), tok(ATTN_WIDTH),
                   tok(ATTN_WIDTH), tok(ATTN_WIDTH), tok(ATTN_WIDTH)],
        out_shape=[jax.ShapeDtypeStruct((s, d), MXU_DTYPE),
                   jax.ShapeDtypeStruct((s, POOL_WIDTH), MXU_DTYPE),
                   jax.ShapeDtypeStruct((s, POOL_WIDTH), MXU_DTYPE),
                   jax.ShapeDtypeStruct((s, ATTN_WIDTH), F32),
                   jax.ShapeDtypeStruct((s, ATTN_WIDTH), F32),
                   jax.ShapeDtypeStruct((s, ATTN_WIDTH), MXU_DTYPE),
                   jax.ShapeDtypeStruct((s, ATTN_WIDTH), MXU_DTYPE),
                   jax.ShapeDtypeStruct((s, ATTN_WIDTH), MXU_DTYPE)],
        scratch_shapes=[pltpu.VMEM((tm + POOL_HALO, POOL_WIDTH), F32)],
        compiler_params=_params(("arbitrary",)),
    )(x, g1, win, poolw, pscale, qg, kg)


DEINT = 4
assert [dl for _, dl in DILATED_PATTERNS] == [1, DEINT, DEINT * DEINT]


def _by4_positions(n):
    pos = np.arange(n)
    return DEINT * (pos % (n // DEINT)) + pos // (n // DEINT)


def _band_mask(n, by4):
    qq = lax.broadcasted_iota(jnp.int32, (ATT_BLOCK, 2 * ATT_BLOCK), 0)
    kk = lax.broadcasted_iota(jnp.int32, (ATT_BLOCK, 2 * ATT_BLOCK), 1)
    if by4:
        nq, nk = ATT_BLOCK // DEINT, 2 * ATT_BLOCK // DEINT
        qq = DEINT * (qq % nq) + qq // nq
        kk = DEINT * (kk % nk) + kk // nk
    dist = qq + ATT_BLOCK - kk
    return (dist >= 0) & (dist <= ATT_BLOCK) & ((n > 0) | (kk >= ATT_BLOCK))


def _unit_rows(u, dl):
    sq, sk = ATT_SUPER // DEINT, 2 * ATT_SUPER // DEINT
    if dl == 1:
        n = ATT_BLOCK // DEINT
        return (u, [pl.ds(pl.multiple_of(r * sq + n * u, 8), n) for r in range(DEINT)],
                [pl.ds(pl.multiple_of(r * sk + sk // 2 + n * (u - 1), 8), 2 * n) for r in range(DEINT)])
    if dl == DEINT:
        r, b = u % DEINT, u // DEINT
        return (b, [pl.ds(pl.multiple_of(r * sq + ATT_BLOCK * b, 8), ATT_BLOCK)],
                [pl.ds(pl.multiple_of(r * sk + sk // 2 + ATT_BLOCK * (b - 1), 8), 2 * ATT_BLOCK)])
    r, a = u % DEINT, u // DEINT
    return 0, [pl.ds(r * sq + a, ATT_BLOCK, stride=DEINT)], [pl.ds(r * sk + a, 2 * ATT_BLOCK, stride=DEINT)]


def _take(ref, runs):
    parts = [ref[run, :] for run in runs]
    return parts[0] if len(parts) == 1 else jnp.concatenate(parts, axis=0)


def _put(ref, runs, value, add=False):
    n = value.shape[0] // len(runs)
    for i, run in enumerate(runs):
        part = value[i * n:(i + 1) * n]
        ref[run, :] = ref[run, :] + part if add else part


def _deinterleave(dst, src, n):
    seg = n // DEINT
    for r in range(DEINT):
        dst[r * seg:(r + 1) * seg, :] = src[pl.ds(r, seg, stride=DEINT), :]


def _interleave(dst, src, n, offset=0):
    seg = n // DEINT
    stride = src.shape[0] // DEINT
    for r in range(DEINT):
        dst[pl.ds(r, seg, stride=DEINT), :] = src[r * stride + offset:r * stride + offset + seg, :]


def _attn_fwd_call(qn, kn, v, bias):
    s, w = qn.shape
    nsb = s // ATT_SUPER
    npair = w // LANES

    def body(q_ref, kc_ref, kp_ref, vc_ref, vp_ref, b_ref, o_ref, lse_ref, tmp, qf, kf, vf, acc_s, m_s, l_s):
        sb = pl.program_id(1)
        tmp[0:ATT_SUPER, :] = q_ref[...].astype(F32)
        _deinterleave(qf, tmp, ATT_SUPER)
        tmp[0:ATT_SUPER, :] = kp_ref[...].astype(F32)
        tmp[ATT_SUPER:, :] = kc_ref[...].astype(F32)
        _deinterleave(kf, tmp, 2 * ATT_SUPER)
        tmp[0:ATT_SUPER, :] = vp_ref[...].astype(F32)
        tmp[ATT_SUPER:, :] = vc_ref[...].astype(F32)
        _deinterleave(vf, tmp, 2 * ATT_SUPER)
        lo = _low_half()
        for p, (_, dl) in enumerate(DILATED_PATTERNS):
            def unit(u, carry, p=p, dl=dl):
                b, rows_q, rows_k = _unit_rows(u, dl)
                valid = _band_mask(sb * (ATT_UNITS // dl) + b, by4=dl == 1)
                qp = _take(qf, rows_q).astype(MXU_DTYPE)
                kcat = _take(kf, rows_k).astype(MXU_DTYPE)
                vcat = _take(vf, rows_k).astype(MXU_DTYPE)
                zero = jnp.zeros_like(qp)
                q2 = jnp.concatenate([jnp.where(lo, qp, zero), jnp.where(lo, zero, qp)], axis=0)
                sc = _mm_nt(q2, kcat) + b_ref[p].reshape(2 * ATT_BLOCK, 2 * ATT_BLOCK)
                sc = jnp.where(jnp.concatenate([valid, valid], axis=0), sc, NEG_INF)
                m2 = jnp.max(sc, axis=-1, keepdims=True)
                pr = jnp.exp(sc - m2)
                l2 = jnp.sum(pr, axis=-1, keepdims=True)
                acc2 = _mm(pr.astype(MXU_DTYPE), vcat)
                acc = jnp.where(lo, acc2[:ATT_BLOCK], acc2[ATT_BLOCK:])
                m = jnp.where(lo, m2[:ATT_BLOCK], m2[ATT_BLOCK:])
                l = jnp.where(lo, l2[:ATT_BLOCK], l2[ATT_BLOCK:])
                if p == 0:
                    _put(acc_s, rows_q, acc)
                    _put(m_s, rows_q, m)
                    _put(l_s, rows_q, l)
                else:
                    m_old = _take(m_s, rows_q)
                    m_new = jnp.maximum(m_old, m)
                    a_old = jnp.exp(m_old - m_new)
                    a_new = jnp.exp(m - m_new)
                    _put(acc_s, rows_q, a_old * _take(acc_s, rows_q) + a_new * acc)
                    _put(l_s, rows_q, a_old * _take(l_s, rows_q) + a_new * l)
                    _put(m_s, rows_q, m_new)
                return carry

            lax.fori_loop(0, ATT_UNITS, unit, 0, unroll=4)
        l = l_s[...]
        acc_s[...] = acc_s[...] / l
        m_s[...] = m_s[...] + jnp.log(l)
        _interleave(o_ref, acc_s, ATT_SUPER)
        _interleave(lse_ref, m_s, ATT_SUPER)

    cur = pl.BlockSpec((ATT_SUPER, LANES), lambda j, t: (t, j))
    prev = pl.BlockSpec((ATT_SUPER, LANES), lambda j, t: (jnp.maximum(t - 1, 0), j))
    bspec = pl.BlockSpec((len(DILATED_PATTERNS), 2, ATT_BLOCK, 2 * ATT_BLOCK), lambda j, t: (0, j, 0, 0))
    return pl.pallas_call(
        body, name="attn_fwd",
        grid=(npair, nsb),
        in_specs=[cur, cur, prev, cur, prev, bspec],
        out_specs=[cur, cur],
        out_shape=[jax.ShapeDtypeStruct((s, w), F32), jax.ShapeDtypeStruct((s, w), F32)],
        scratch_shapes=[pltpu.VMEM((2 * ATT_SUPER, LANES), F32),
                        pltpu.VMEM((ATT_SUPER, LANES), F32), pltpu.VMEM((2 * ATT_SUPER, LANES), F32),
                        pltpu.VMEM((2 * ATT_SUPER, LANES), F32), pltpu.VMEM((ATT_SUPER, LANES), F32),
                        pltpu.VMEM((ATT_SUPER, LANES), F32), pltpu.VMEM((ATT_SUPER, LANES), F32)],
        compiler_params=_params(("arbitrary", "arbitrary")),
    )(qn, kn, kn, v, v, bias)


def _attn_bwd_call(qn, kn, v, do, lse, delta, bias, dep=None):
    s, w = qn.shape
    nsb = s // ATT_SUPER
    npair = w // LANES
    deps = [] if dep is None else [dep]

    def body(q_ref, kc_ref, kp_ref, vc_ref, vp_ref, do_ref, lse_ref, dlt_ref, b_ref, *rest):
        dq_ref, dk_ref, dv_ref, db_ref, tmp, qf, kf, vf, dof, lsef, dltf, dqf, dkf, dvf = rest[len(deps):]
        step = pl.program_id(1)
        sb = nsb - 1 - step
        seg = ATT_SUPER // DEINT
        tmp[0:ATT_SUPER, :] = q_ref[...].astype(F32)
        _deinterleave(qf, tmp, ATT_SUPER)
        tmp[0:ATT_SUPER, :] = do_ref[...].astype(F32)
        _deinterleave(dof, tmp, ATT_SUPER)
        tmp[0:ATT_SUPER, :] = kp_ref[...].astype(F32)
        tmp[ATT_SUPER:, :] = kc_ref[...].astype(F32)
        _deinterleave(kf, tmp, 2 * ATT_SUPER)
        tmp[0:ATT_SUPER, :] = vp_ref[...].astype(F32)
        tmp[ATT_SUPER:, :] = vc_ref[...].astype(F32)
        _deinterleave(vf, tmp, 2 * ATT_SUPER)
        _deinterleave(lsef, lse_ref, ATT_SUPER)
        _deinterleave(dltf, dlt_ref, ATT_SUPER)

        @pl.when(step == 0)
        def _():
            db_ref[...] = jnp.zeros(db_ref.shape, F32)

        for acc in (dkf, dvf):
            for r in range(DEINT):
                this, before = pl.ds((2 * r + 1) * seg, seg), pl.ds(2 * r * seg, seg)

                @pl.when(step == 0)
                def _(acc=acc, this=this):
                    acc[this, :] = jnp.zeros((seg, LANES), F32)

                @pl.when(step > 0)
                def _(acc=acc, this=this, before=before):
                    acc[this, :] = acc[before, :]

                acc[before, :] = jnp.zeros((seg, LANES), F32)
        lo = _low_half()
        for p, (_, dl) in enumerate(DILATED_PATTERNS):
            def unit(u, carry, p=p, dl=dl):
                b, rows_q, rows_k = _unit_rows(u, dl)
                valid = _band_mask(sb * (ATT_UNITS // dl) + b, by4=dl == 1)
                qp = _take(qf, rows_q).astype(MXU_DTYPE)
                dop = _take(dof, rows_q).astype(MXU_DTYPE)
                kcat = _take(kf, rows_k).astype(MXU_DTYPE)
                vcat = _take(vf, rows_k).astype(MXU_DTYPE)
                lse2 = _take(lsef, rows_q)
                dlt2 = _take(dltf, rows_q)
                zero = jnp.zeros_like(qp)
                q2 = jnp.concatenate([jnp.where(lo, qp, zero), jnp.where(lo, zero, qp)], axis=0)
                do2 = jnp.concatenate([jnp.where(lo, dop, zero), jnp.where(lo, zero, dop)], axis=0)
                lse_c = jnp.concatenate([lse2[:, 0:1], lse2[:, HEAD_DIM:HEAD_DIM + 1]], axis=0)
                dlt_c = jnp.concatenate([dlt2[:, 0:1], dlt2[:, HEAD_DIM:HEAD_DIM + 1]], axis=0)
                sc = _mm_nt(q2, kcat) + b_ref[p].reshape(2 * ATT_BLOCK, 2 * ATT_BLOCK)
                pr = jnp.where(jnp.concatenate([valid, valid], axis=0), jnp.exp(sc - lse_c), 0.0)
                ds = pr * (_mm_nt(do2, vcat) - dlt_c)
                db_ref[p] += ds.reshape(2, ATT_BLOCK, 2 * ATT_BLOCK)
                ds_c = ds.astype(MXU_DTYPE)
                dq2 = _mm(ds_c, kcat)
                dk = _mm_tn(ds_c, q2)
                dv = _mm_tn(pr.astype(MXU_DTYPE), do2)
                dq = jnp.where(lo, dq2[:ATT_BLOCK], dq2[ATT_BLOCK:])
                _put(dqf, rows_q, dq, add=p > 0)
                _put(dkf, rows_k, dk, add=True)
                _put(dvf, rows_k, dv, add=True)
                return carry

            lax.fori_loop(0, ATT_UNITS, unit, 0, unroll=4)
        _interleave(dq_ref, dqf, ATT_SUPER)
        _interleave(dk_ref, dkf, ATT_SUPER, offset=seg)
        _interleave(dv_ref, dvf, ATT_SUPER, offset=seg)

    cur = pl.BlockSpec((ATT_SUPER, LANES), lambda j, t: (nsb - 1 - t, j))
    prev = pl.BlockSpec((ATT_SUPER, LANES), lambda j, t: (jnp.maximum(nsb - 2 - t, 0), j))
    bshape = (len(DILATED_PATTERNS), 2, ATT_BLOCK, 2 * ATT_BLOCK)
    bspec = pl.BlockSpec(bshape, lambda j, t: (0, j, 0, 0))
    sup = lambda: pltpu.VMEM((ATT_SUPER, LANES), F32)
    sup2 = lambda: pltpu.VMEM((2 * ATT_SUPER, LANES), F32)
    return pl.pallas_call(
        body, name="attn_bwd",
        grid=(npair, nsb),
        in_specs=[cur, cur, prev, cur, prev, cur, cur, cur, bspec] + [ANY] * len(deps),
        out_specs=[cur, cur, cur, bspec],
        out_shape=[jax.ShapeDtypeStruct((s, w), F32)] * 3 + [jax.ShapeDtypeStruct(bias.shape, F32)],
        scratch_shapes=[sup2(), sup(), sup2(), sup2(), sup(), sup(), sup(), sup(), sup2(), sup2()],
        compiler_params=_params(("arbitrary", "arbitrary")),
    )(qn, kn, kn, v, v, do, lse, delta, bias, *deps)


def _bias_table_call(rel_bias, buckets):
    npat = buckets.shape[0]

    def body(rb_ref, bk_ref, out_ref):
        for p in range(npat):
            for half in range(2):
                ks = slice(half * ATT_BLOCK, (half + 1) * ATT_BLOCK)
                bk = bk_ref[p, :, ks]
                for h in range(N_HEADS):
                    def pick(b, acc, h=h, bk=bk):
                        return jnp.where(bk == b, rb_ref[b, h], acc)

                    out_ref[p, h, :, ks] = lax.fori_loop(0, N_BUCKETS, pick, jnp.zeros((ATT_BLOCK, ATT_BLOCK), F32))

    return pl.pallas_call(
        body, name="bias_table",
        in_specs=[pl.BlockSpec(memory_space=pltpu.SMEM), pl.BlockSpec(memory_space=pltpu.VMEM)],
        out_shape=jax.ShapeDtypeStruct((npat, N_HEADS, ATT_BLOCK, 2 * ATT_BLOCK), F32),
        compiler_params=_params(),
    )(rel_bias, buckets)


def _rel_bias_grad_call(dbias, buckets):
    npat, nh = dbias.shape[0], dbias.shape[1]

    def body(db_ref, bk_ref, out_ref):
        lane = lax.broadcasted_iota(jnp.int32, (nh, LANES), 1)
        out = jnp.zeros((nh, LANES), F32)
        for b in range(N_BUCKETS):
            tot = jnp.zeros((nh, 1), F32)
            for p in range(npat):
                hit = jnp.where(bk_ref[p][None] == b, db_ref[p], 0.0)
                tot = tot + jnp.sum(jnp.sum(hit, axis=2), axis=1, keepdims=True)
            out = jnp.where(lane == b, tot, out)
        out_ref[...] = out

    return pl.pallas_call(
        body, name="rel_bias_grad",
        out_shape=jax.ShapeDtypeStruct((nh, LANES), F32),
        compiler_params=_params(),
    )(dbias, buckets)


def _f2_call(x, tgt, ypool, o, wout, wup, wdown, g2, tm):
    s, d = x.shape
    nblk = s // tm
    nch, _, fch = wup.shape
    dff = nch * fch
    mixw = POOL_WIDTH + ATTN_WIDTH

    def body(x_ref, t_ref, yp_ref, o_ref, g2_ref, wout_hbm, wup_hbm, wdown_hbm,
             mixed_ref, c_ref, ff_ref, dz_ref, dy_ref, dh1_ref, dyp_ref, do_ref, dlt_ref, dg2_ref, loss_ref,
             wout_v, wup_v, wdown_v, rz):
        i = pl.program_id(0)

        @pl.when(i == 0)
        def _():
            pltpu.sync_copy(wout_hbm, wout_v)
            pltpu.sync_copy(wup_hbm, wup_v)
            pltpu.sync_copy(wdown_hbm, wdown_v)
            dg2_ref[...] = jnp.zeros(dg2_ref.shape, F32)
            loss_ref[...] = jnp.zeros(loss_ref.shape, F32)

        o = o_ref[...]
        mixed = jnp.concatenate([yp_ref[...], o.astype(MXU_DTYPE)], axis=-1)
        mixed_ref[...] = mixed
        h1 = x_ref[...] + _mm(mixed, wout_v[...])
        r2 = lax.rsqrt(jnp.mean(h1 * h1, axis=-1, keepdims=True) + NORM_EPS)
        hn = h1 * r2
        c = (hn * g2_ref[...]).astype(MXU_DTYPE)
        c_ref[...] = c
        y = h1
        for j in range(nch):
            cs = slice(j * fch, (j + 1) * fch)
            z = jnp.maximum(_mm(c, wup_v[j]), 0.0)
            rz[:, cs] = z
            ff = (z * z).astype(MXU_DTYPE)
            ff_ref[:, cs] = ff
            y = y + _mm(ff, wdown_v[j])
        err = y - t_ref[...]
        loss_ref[...] += jnp.sum(err * err) * (0.5 / d)
        dy = err * (1.0 / d)
        dy_c = dy.astype(MXU_DTYPE)
        dy_ref[...] = dy_c
        dc = jnp.zeros((tm, d), F32)
        for j in range(nch):
            cs = slice(j * fch, (j + 1) * fch)
            dz = (_mm_nt(dy_c, wdown_v[j]) * (2.0 * rz[:, cs])).astype(MXU_DTYPE)
            dz_ref[:, cs] = dz
            dc = dc + _mm_nt(dz, wup_v[j])
        dg2_ref[...] += jnp.sum(dc * hn, axis=0, keepdims=True)
        dh1 = dy + _rms_bwd(dc * g2_ref[...], hn, r2)
        dh1_ref[...] = dh1
        dmix = _mm_nt(dh1.astype(MXU_DTYPE), wout_v[...])
        dyp_ref[...] = dmix[:, :POOL_WIDTH]
        do = dmix[:, POOL_WIDTH:]
        do_ref[...] = do.astype(MXU_DTYPE)
        dlt_ref[...] = _head_sum_bcast(do * o)

    tok = lambda w: pl.BlockSpec((tm, w), lambda i: (i, 0))
    const = lambda shp: pl.BlockSpec(shp, lambda i: (0,) * len(shp))
    return pl.pallas_call(
        body, name="fwd_mlp_bwd_mlp",
        grid=(nblk,),
        in_specs=[tok(d), tok(d), tok(POOL_WIDTH), tok(ATTN_WIDTH), const((1, d)), ANY, ANY, ANY],
        out_specs=[tok(mixw), tok(d), tok(dff), tok(dff), tok(d), tok(d), tok(POOL_WIDTH), tok(ATTN_WIDTH),
                   tok(ATTN_WIDTH), const((1, d)), const((1, LANES))],
        out_shape=[jax.ShapeDtypeStruct((s, mixw), MXU_DTYPE),
                   jax.ShapeDtypeStruct((s, d), MXU_DTYPE),
                   jax.ShapeDtypeStruct((s, dff), MXU_DTYPE),
                   jax.ShapeDtypeStruct((s, dff), MXU_DTYPE),
                   jax.ShapeDtypeStruct((s, d), MXU_DTYPE),
                   jax.ShapeDtypeStruct((s, d), F32),
                   jax.ShapeDtypeStruct((s, POOL_WIDTH), F32),
                   jax.ShapeDtypeStruct((s, ATTN_WIDTH), MXU_DTYPE),
                   jax.ShapeDtypeStruct((s, ATTN_WIDTH), F32),
                   jax.ShapeDtypeStruct((1, d), F32),
                   jax.ShapeDtypeStruct((1, LANES), F32)],
        scratch_shapes=[pltpu.VMEM(wout.shape, MXU_DTYPE), pltpu.VMEM(wup.shape, MXU_DTYPE),
                        pltpu.VMEM(wdown.shape, MXU_DTYPE), pltpu.VMEM((tm, dff), F32)],
        compiler_params=_params(("arbitrary",)),
    )(x, tgt, ypool, o, g2, wout, wup, wdown)


def _bproj_call(dqn, dkn, dv, q32, k32, dypool, pooled, x, dh1, win, poolw, pscale, qg, kg, g1, tm):
    s, d = x.shape
    nblk = s // tm
    ngrp = len(POOL_WINDOWS)

    def body(dqn_ref, dkn_ref, dv_ref, q_ref, k_ref, dyp_ref, pooled_ref, x_ref, dh1_ref,
             win_hbm, pw_ref, ps_ref, qg_ref, kg_ref, g1_ref,
             dx_ref, dproj_ref, dg1_ref, dqg_ref, dkg_ref, dpw_ref, dps_ref, win_v, ebuf):
        step = pl.program_id(0)
        i = nblk - 1 - step

        @pl.when(step == 0)
        def _():
            pltpu.sync_copy(win_hbm, win_v)
            dg1_ref[...] = jnp.zeros(dg1_ref.shape, F32)
            dqg_ref[...] = jnp.zeros(dqg_ref.shape, F32)
            dkg_ref[...] = jnp.zeros(dkg_ref.shape, F32)
            dpw_ref[...] = jnp.zeros(dpw_ref.shape, F32)
            dps_ref[...] = jnp.zeros(dps_ref.shape, F32)
            ebuf[tm:tm + POOL_HALO, :] = jnp.zeros((POOL_HALO, POOL_WIDTH), F32)

        @pl.when(step > 0)
        def _():
            ebuf[tm:tm + POOL_HALO, :] = ebuf[0:POOL_HALO, :]

        def qk_bwd(dn_sum, raw, gain, scale, dgain_ref):
            rr = lax.rsqrt(_head_sum_bcast(raw * raw) * (1.0 / HEAD_DIM) + NORM_EPS)
            hn = raw * rr
            dgain_ref[...] += jnp.sum(dn_sum * hn, axis=0, keepdims=True) * scale
            dn = dn_sum * (gain * scale)
            return rr * (dn - hn * (_head_sum_bcast(dn * hn) * (1.0 / HEAD_DIM)))

        dq = qk_bwd(dqn_ref[...], q_ref[...], qg_ref[...], HEAD_DIM ** -0.5, dqg_ref)
        dk = qk_bwd(dkn_ref[...], k_ref[...], kg_ref[...], 1.0, dkg_ref)

        t = i * tm + lax.broadcasted_iota(jnp.int32, (tm, 1), 0)
        dpooled = []
        for g, w in enumerate(POOL_WINDOWS):
            ls = slice(g * LANES, (g + 1) * LANES)
            dm = dyp_ref[:, ls]
            pg = pooled_ref[:, ls]
            dps_ref[:, ls] += jnp.sum(dm * _mm(pg, pw_ref[g]), axis=0, keepdims=True)
            dms = (dm * ps_ref[:, ls]).astype(MXU_DTYPE)
            dpw_ref[g] += _mm_tn(pg, dms)
            dpg = _mm_nt(dms, pw_ref[g])
            dpooled.append(dpg)
            ebuf[0:tm, ls] = dpg / jnp.minimum(t + 1, w).astype(F32)
        du = []
        for g, w in enumerate(POOL_WINDOWS):
            ls = slice(g * LANES, (g + 1) * LANES)
            acc = ebuf[0:tm, ls]
            for sh in range(1, w):
                acc = acc + ebuf[sh:sh + tm, ls]
            du.append(acc - dpooled[g])
        parts = [jnp.concatenate(du, axis=-1), dq, dk, dv_ref[...]]
        da = jnp.zeros((tm, d), F32)
        for p, part in enumerate(parts):
            pc = part.astype(MXU_DTYPE)
            dproj_ref[:, p * POOL_WIDTH:(p + 1) * POOL_WIDTH] = pc
            da = da + _mm_nt(pc, win_v[p])
        xv = x_ref[...]
        r = lax.rsqrt(jnp.mean(xv * xv, axis=-1, keepdims=True) + NORM_EPS)
        xn = xv * r
        dg1_ref[...] += jnp.sum(da * xn, axis=0, keepdims=True)
        dx_ref[...] = dh1_ref[...] + _rms_bwd(da * g1_ref[...], xn, r)

    tok = lambda w: pl.BlockSpec((tm, w), lambda t: (nblk - 1 - t, 0))
    const = lambda shp: pl.BlockSpec(shp, lambda t: (0,) * len(shp))
    return pl.pallas_call(
        body, name="bwd_inproj",
        grid=(nblk,),
        in_specs=[tok(ATTN_WIDTH)] * 5 + [tok(POOL_WIDTH), tok(POOL_WIDTH), tok(d), tok(d),
                                          ANY, const(poolw.shape), const((1, POOL_WIDTH)), const((1, ATTN_WIDTH)),
                                          const((1, ATTN_WIDTH)), const((1, d))],
        out_specs=[tok(d), tok(4 * POOL_WIDTH), const((1, d)), const((1, ATTN_WIDTH)), const((1, ATTN_WIDTH)),
                   const((ngrp, LANES, LANES)), const((1, POOL_WIDTH))],
        out_shape=[jax.ShapeDtypeStruct((s, d), F32),
                   jax.ShapeDtypeStruct((s, 4 * POOL_WIDTH), MXU_DTYPE),
                   jax.ShapeDtypeStruct((1, d), F32),
                   jax.ShapeDtypeStruct((1, ATTN_WIDTH), F32),
                   jax.ShapeDtypeStruct((1, ATTN_WIDTH), F32),
                   jax.ShapeDtypeStruct((ngrp, LANES, LANES), F32),
                   jax.ShapeDtypeStruct((1, POOL_WIDTH), F32)],
        scratch_shapes=[pltpu.VMEM(win.shape, MXU_DTYPE), pltpu.VMEM((tm + POOL_HALO, POOL_WIDTH), F32)],
        compiler_params=_params(("arbitrary",)),
    )(dqn, dkn, dv, q32, k32, dypool, pooled, x, dh1, win, poolw, pscale, qg, kg, g1)


def _wgrad_call(a, b, bm, bn, bk, out_shape, out_block, out_index, name):
    s, m = a.shape
    _, n = b.shape
    nk = s // bk

    def body(a_ref, b_ref, o_ref, wire_ref):
        k = pl.program_id(2)

        @pl.when(k == 0)
        def _():
            o_ref[...] = jnp.zeros(o_ref.shape, F32)

        o_ref[...] += _mm_tn(a_ref[...].astype(MXU_DTYPE), b_ref[...].astype(MXU_DTYPE))

        @pl.when(k == nk - 1)
        def _():
            wire_ref[...] = o_ref[...].astype(WIRE_DTYPE)

    return pl.pallas_call(
        body, name=name,
        grid=(m // bm, n // bn, nk),
        in_specs=[pl.BlockSpec((bk, bm), lambda i, j, k: (k, i)), pl.BlockSpec((bk, bn), lambda i, j, k: (k, j))],
        out_specs=[pl.BlockSpec(out_block, out_index)] * 2,
        out_shape=[jax.ShapeDtypeStruct(out_shape, F32), jax.ShapeDtypeStruct(out_shape, WIRE_DTYPE)],
        compiler_params=_params(("arbitrary", "arbitrary", "arbitrary")),
    )(a, b)


def _local_grads(x, tgt, g1, win, poolw, pscale, qg, kg, rel_bias, g2, mlp_weights, on_mlp_grads=None):
    s, d = x.shape
    g1r, g2r = g1.reshape(1, d), g2.reshape(1, d)
    psr = pscale.reshape(1, POOL_WIDTH)
    qgr = jnp.tile(qg, N_HEADS).reshape(1, ATTN_WIDTH)
    kgr = jnp.tile(kg, N_HEADS).reshape(1, ATTN_WIDTH)
    pw_c = poolw.astype(MXU_DTYPE)
    buckets = jnp.asarray(_bucket_tables())
    bias = _bias_table_call(rel_bias, buckets)
    bk = min(s, 2048)

    a, pooled, ypool, q32, k32, qn, kn, v = _f1_call(x, g1r, win, pw_c, psr, qgr, kgr, tm=512)
    o, lse = _attn_fwd_call(qn, kn, v, bias)
    wout, wup, wdown = mlp_weights(o)
    mixed, c, ff, dz, dy, dh1, dypool, do, delta, dg2, loss = _f2_call(x, tgt, ypool, o, wout, wup, wdown, g2r, tm=256)
    dff = ff.shape[1]
    g_out = [g.reshape(N_CHIPS, d // N_CHIPS, d)
             for g in _wgrad_call(mixed, dh1, d, d, bk // 2, (d, d), (d, d), lambda i, j, k: (0, 0), "wgrad_out")]
    g_up = _wgrad_call(c, dz, d, dff // N_CHIPS, bk, (N_CHIPS, d, dff // N_CHIPS), (None, d, dff // N_CHIPS),
                       lambda i, j, k: (j, 0, 0), "wgrad_up")
    g_down = _wgrad_call(ff, dy, dff // N_CHIPS, d, bk, (N_CHIPS, dff // N_CHIPS, d), (None, dff // N_CHIPS, d),
                         lambda i, j, k: (i, 0, 0), "wgrad_down")
    dep = None if on_mlp_grads is None else on_mlp_grads(g_out[1], g_up[1], g_down[1])
    dqn, dkn, dv, dbias = _attn_bwd_call(qn, kn, v, do, lse, delta, bias, dep)
    dx, dproj, dg1, dqg, dkg, dpw, dps = _bproj_call(
        dqn, dkn, dv, q32, k32, dypool, pooled, x, dh1, win, pw_c, psr, qgr, kgr, g1r, tm=256)
    nin = dproj.shape[1] // N_CHIPS
    g_in = _wgrad_call(a, dproj, d, nin, bk, (N_CHIPS, d, nin), (None, d, nin), lambda i, j, k: (j, 0, 0), "wgrad_in")
    drb = _rel_bias_grad_call(dbias, buckets)
    small = dict(
        mix_norm_g=dg1.reshape(d), mlp_norm_g=dg2.reshape(d), pool_scale=dps.reshape(POOL_WIDTH),
        q_norm_g=dqg.reshape(ATTN_WIDTH), k_norm_g=dkg.reshape(ATTN_WIDTH),
        rel_bias=drb[:, :N_BUCKETS].T, pool_w=dpw)
    return loss[0, 0], dx, (g_in, g_out, g_up, g_down), small


def _coords():
    return lax.axis_index("x"), lax.axis_index("y"), lax.axis_index("c")


def _other_chips(x, y):
    return [(1 - x, y), (x, 1 - y), (1 - x, 1 - y)]


def _remote(src, dst, send_sem, recv_sem, dev):
    return pltpu.make_async_remote_copy(src_ref=src, dst_ref=dst, send_sem=send_sem, recv_sem=recv_sem,
                                        device_id=dev, device_id_type=MESH)


def _halves(a):
    return a.reshape(a.shape[:-2] + (2, a.shape[-2] // 2, a.shape[-1]))


def _place_shards_call(shards, chip_idx, nch):
    nw = len(shards)

    def body(chip_ref, *refs):
        for w in range(nw):
            refs[nw + w][...] = refs[w][...].astype(WIRE_DTYPE)

    in_specs = [pl.BlockSpec((s.shape[0] // nch, s.shape[1]), lambda i, chip_ref: (i, 0)) for s in shards]
    out_specs = [pl.BlockSpec((None, s.shape[0] // nch, s.shape[1]), lambda i, chip_ref: (chip_ref[0], i, 0))
                 for s in shards]
    return pl.pallas_call(
        body, name="weights_place",
        grid_spec=pltpu.PrefetchScalarGridSpec(num_scalar_prefetch=1, grid=(nch,),
                                               in_specs=in_specs, out_specs=out_specs),
        out_shape=[jax.ShapeDtypeStruct((N_CHIPS,) + s.shape, WIRE_DTYPE) for s in shards],
        compiler_params=_params(("arbitrary",)),
    )(chip_idx, *shards)


def _allgather_call(placed, from_chips, name):
    nw = len(placed)
    ncp = 3 * nw

    def body(*refs):
        outs = refs[nw:2 * nw]
        send1, recv1, send2, recv2 = refs[2 * nw:]
        x, y, c = _coords()
        chip = 2 * x + y
        others = _other_chips(x, y)
        first, passed = [], []
        if from_chips:
            for w in range(nw):
                for k, (ox, oy) in enumerate(others):
                    mine = outs[w].at[chip, c]
                    cp = _remote(mine, mine, send1.at[3 * w + k], recv1.at[3 * w + k], (ox, oy, c))
                    cp.start()
                    first.append(cp)
        for w in range(nw):
            for k, (ox, oy) in enumerate(others):
                piece = outs[w].at[2 * ox + oy, c]
                if from_chips:
                    _remote(piece, piece, send1.at[3 * w + k], recv1.at[3 * w + k], (ox, oy, c)).wait_recv()
                cp = _remote(piece, piece, send2.at[3 * w + k], recv2.at[3 * w + k], (x, y, 1 - c))
                cp.start()
                passed.append(cp)
        for w in range(nw):
            for k, (ox, oy) in enumerate(others):
                piece = outs[w].at[2 * ox + oy, 1 - c]
                _remote(piece, piece, send2.at[3 * w + k], recv2.at[3 * w + k], (x, y, 1 - c)).wait_recv()
        for cp in first + passed:
            cp.wait_send()

    return pl.pallas_call(
        body, name=name,
        in_specs=[ANY] * nw, out_specs=[ANY] * nw,
        out_shape=[jax.ShapeDtypeStruct(s.shape, s.dtype) for s in placed],
        input_output_aliases={w: w for w in range(nw)},
        scratch_shapes=[pltpu.SemaphoreType.DMA((ncp,))] * 4,
    )(*placed)


HBM_SPEC = pl.BlockSpec(memory_space=pltpu.HBM)
SEM_SPEC = pl.BlockSpec(memory_space=pltpu.SEMAPHORE)
SPLIT_EFFECT = pltpu.SideEffectType.DATAFLOW_SIDE_EFFECTING


def _in_hbm(a):
    return pltpu.with_memory_space_constraint(a, pltpu.HBM)


def _gather_copies(bufs, send, recv):
    x, y, c = _coords()
    chip = 2 * x + y
    cps = []
    for w, buf in enumerate(bufs):
        for k, (ox, oy) in enumerate(_other_chips(x, y)):
            mine, theirs = buf.at[chip, c], buf.at[2 * ox + oy, c]
            sems = (send.at[3 * w + k], recv.at[3 * w + k], (ox, oy, c))
            cps.append((_remote(mine, mine, *sems), _remote(theirs, theirs, *sems)))
    return cps


def _gather_start_call(bufs, after):
    nw = len(bufs)

    def body(*refs):
        ins, send, recv, token = refs[:nw], refs[nw + 1], refs[nw + 2], refs[2 * nw + 3]
        for out, _ in _gather_copies(ins, send, recv):
            out.start()
        token[...] = jnp.zeros(token.shape, F32)

    res = pl.pallas_call(
        body, name="weights_gather_start",
        in_specs=[HBM_SPEC] * nw + [ANY],
        out_specs=[SEM_SPEC, SEM_SPEC] + [HBM_SPEC] * nw + [pl.BlockSpec(memory_space=pltpu.VMEM)],
        out_shape=[pltpu.SemaphoreType.DMA((3 * nw,)), pltpu.SemaphoreType.DMA((3 * nw,))]
        + [pltpu.HBM(b.shape, b.dtype) for b in bufs] + [jax.ShapeDtypeStruct((8, LANES), F32)],
        input_output_aliases={w: 2 + w for w in range(nw)},
        compiler_params=pltpu.CompilerParams(has_side_effects=SPLIT_EFFECT),
    )(*[_in_hbm(b) for b in bufs], after)
    return res[0], res[1], list(res[2:2 + nw]), res[2 + nw]


def _gather_wait_call(bufs, send, recv, after):
    nw = len(bufs)

    def body(*refs):
        ins, send, recv = refs[:nw], refs[nw], refs[nw + 1]
        for out, back in _gather_copies(ins, send, recv):
            out.wait_send()
            back.wait_recv()

    return pl.pallas_call(
        body, name="weights_gather_wait",
        in_specs=[HBM_SPEC] * nw + [SEM_SPEC, SEM_SPEC, ANY],
        out_specs=[HBM_SPEC] * nw,
        out_shape=[pltpu.HBM(b.shape, b.dtype) for b in bufs],
        input_output_aliases={w: w for w in range(nw)},
        compiler_params=pltpu.CompilerParams(has_side_effects=SPLIT_EFFECT),
    )(*bufs, send, recv, after)


def _scatter_copies(srcs, lands, send, recv, whole=False):
    x, y, c = _coords()
    me = 4 * x + 2 * y + c
    cps = []
    for w, (src, land) in enumerate(zip(srcs, lands)):
        for r in range(1, N_DEV):
            px, py, pc = ((1 - x) if r & 4 else x, (1 - y) if r & 2 else y, (1 - c) if r & 1 else c)
            sems = (send.at[(N_DEV - 1) * w + r - 1], recv.at[(N_DEV - 1) * w + r - 1], (px, py, pc))
            piece = src if whole else src.at[2 * px + py, pc]
            cps.append((_remote(piece, land.at[me], *sems), _remote(piece, land.at[4 * px + 2 * py + pc], *sems)))
    return cps


def _scatter_start_call(srcs, lands):
    nw = len(srcs)
    ncp = (N_DEV - 1) * nw

    def body(*refs):
        ins, lnd, send, recv, token = refs[:nw], refs[nw:2 * nw], refs[2 * nw], refs[2 * nw + 1], refs[4 * nw + 2]
        for out, _ in _scatter_copies(ins, lnd, send, recv):
            out.start()
        token[...] = jnp.zeros(token.shape, F32)

    res = pl.pallas_call(
        body, name="grads_scatter_start",
        in_specs=[HBM_SPEC] * (2 * nw),
        out_specs=[SEM_SPEC, SEM_SPEC] + [HBM_SPEC] * (2 * nw) + [pl.BlockSpec(memory_space=pltpu.VMEM)],
        out_shape=[pltpu.SemaphoreType.DMA((ncp,)), pltpu.SemaphoreType.DMA((ncp,))]
        + [pltpu.HBM(b.shape, b.dtype) for b in list(srcs) + list(lands)] + [jax.ShapeDtypeStruct((8, LANES), F32)],
        input_output_aliases={i: 2 + i for i in range(2 * nw)},
        compiler_params=pltpu.CompilerParams(has_side_effects=SPLIT_EFFECT),
    )(*[_in_hbm(b) for b in list(srcs) + list(lands)])
    return res[0], res[1], list(res[2:2 + nw]), list(res[2 + nw:2 + 2 * nw]), res[2 + 2 * nw]


def _scatter_wait_call(srcs, lands, send, recv, after):
    nw = len(srcs)

    def body(*refs):
        ins, lnd, send, recv = refs[:nw], refs[nw:2 * nw], refs[2 * nw], refs[2 * nw + 1]
        for out, back in _scatter_copies(ins, lnd, send, recv):
            out.wait_send()
            back.wait_recv()

    res = pl.pallas_call(
        body, name="grads_scatter_wait",
        in_specs=[HBM_SPEC] * (2 * nw) + [SEM_SPEC, SEM_SPEC, ANY],
        out_specs=[HBM_SPEC] * (2 * nw),
        out_shape=[pltpu.HBM(b.shape, b.dtype) for b in list(srcs) + list(lands)],
        input_output_aliases={i: i for i in range(2 * nw)},
        compiler_params=pltpu.CompilerParams(has_side_effects=SPLIT_EFFECT),
    )(*srcs, *lands, send, recv, after)
    return list(res[nw:])


def _scatter_sync_call(srcs, small):
    nw = len(srcs)
    ncp = (N_DEV - 1) * nw

    def body(*refs):
        ins, small_ref = refs[:nw], refs[nw]
        lnd, gathered = refs[nw + 1:2 * nw + 1], refs[2 * nw + 1]
        send, recv, ssend, srecv, loc = refs[2 * nw + 2:]
        x, y, c = _coords()
        me = 4 * x + 2 * y + c
        own = pltpu.make_async_copy(small_ref, gathered.at[me], loc)
        own.start()
        cps = _scatter_copies(ins, lnd, send, recv)
        cps += _scatter_copies([small_ref], [gathered], ssend, srecv, whole=True)
        for out, _ in cps:
            out.start()
        for out, back in cps:
            back.wait_recv()
            out.wait_send()
        own.wait()

    res = pl.pallas_call(
        body, name="grads_scatter_last",
        in_specs=[ANY] * (nw + 1), out_specs=[ANY] * (nw + 1),
        out_shape=[jax.ShapeDtypeStruct((N_DEV,) + s.shape[2:], s.dtype) for s in srcs]
        + [jax.ShapeDtypeStruct((N_DEV,) + small.shape, small.dtype)],
        scratch_shapes=[pltpu.SemaphoreType.DMA((ncp,)), pltpu.SemaphoreType.DMA((ncp,)),
                        pltpu.SemaphoreType.DMA((N_DEV - 1,)), pltpu.SemaphoreType.DMA((N_DEV - 1,)),
                        pltpu.SemaphoreType.DMA],
    )(*srcs, small)
    return list(res[:nw]), res[nw]


def _reduce_call(own, lands, idx, nch):
    nw = len(own)

    def body(idx_ref, *refs):
        for w in range(nw):
            tot = refs[w][...]
            for r in range(1, N_DEV):
                tot = tot + refs[nw + w][idx_ref[1 + r]].astype(F32)
            refs[2 * nw + w][...] = tot

    in_specs, out_specs, out_shape = [], [], []
    for s in own:
        in_specs.append(pl.BlockSpec((None, None, s.shape[2] // nch, s.shape[3]),
                                     lambda i, idx_ref: (idx_ref[0], idx_ref[1], i, 0)))
    for s in own:
        in_specs.append(pl.BlockSpec((N_DEV, s.shape[2] // nch, s.shape[3]), lambda i, idx_ref: (0, i, 0)))
    for s in own:
        out_specs.append(pl.BlockSpec((None, s.shape[2] // nch, s.shape[3]), lambda i, idx_ref: (idx_ref[1], i, 0)))
        out_shape.append(jax.ShapeDtypeStruct((2,) + s.shape[2:], F32))
    return pl.pallas_call(
        body, name="grads_reduce",
        grid_spec=pltpu.PrefetchScalarGridSpec(num_scalar_prefetch=1, grid=(nch,),
                                               in_specs=in_specs, out_specs=out_specs),
        out_shape=out_shape,
        compiler_params=_params(("arbitrary",)),
    )(idx, *own, *lands)


def _pair_allgather_call(halves):
    nw = len(halves)

    def body(*refs):
        outs = refs[nw:2 * nw]
        send, recv = refs[2 * nw:]
        x, y, c = _coords()
        cps = []
        for w in range(nw):
            cp = _remote(outs[w].at[c], outs[w].at[c], send.at[w], recv.at[w], (x, y, 1 - c))
            cp.start()
            cps.append(cp)
        for w in range(nw):
            theirs = outs[w].at[1 - c]
            _remote(theirs, theirs, send.at[w], recv.at[w], (x, y, 1 - c)).wait_recv()
        for cp in cps:
            cp.wait_send()

    outs = pl.pallas_call(
        body, name="grads_pair_allgather",
        in_specs=[ANY] * nw, out_specs=[ANY] * nw,
        out_shape=[jax.ShapeDtypeStruct(h.shape, h.dtype) for h in halves],
        input_output_aliases={w: w for w in range(nw)},
        scratch_shapes=[pltpu.SemaphoreType.DMA((nw,))] * 2,
    )(*halves)
    return [o.reshape(2 * h.shape[1], h.shape[2]) for o, h in zip(outs, halves)]


def _adamw(w, g, m, v):
    m = ADAM_B1 * m + (1.0 - ADAM_B1) * g
    v = ADAM_B2 * v + (1.0 - ADAM_B2) * (g * g)
    m_hat = m / (1.0 - ADAM_B1 ** ADAM_STEP)
    v_hat = v / (1.0 - ADAM_B2 ** ADAM_STEP)
    delta = -ADAM_LR * (m_hat / (jnp.sqrt(v_hat) + ADAM_EPS) + ADAM_WD * w)
    return delta, m, v


def _adamw_call(ws, gs, ms, vs, nch):
    nw = len(ws)

    def body(*refs):
        for w in range(nw):
            g = refs[nw + w][...]
            delta, m, v = _adamw(refs[w][...], g, refs[2 * nw + w][...], refs[3 * nw + w][...])
            refs[4 * nw + w][...] = g
            refs[5 * nw + w][...] = delta
            refs[6 * nw + w][...] = m
            refs[7 * nw + w][...] = v

    specs = [pl.BlockSpec((a.shape[0] // nch, a.shape[1]), lambda i: (i, 0)) for a in ws]
    res = pl.pallas_call(
        body, name="adamw_big",
        grid=(nch,),
        in_specs=specs * 4, out_specs=specs * 4,
        out_shape=[jax.ShapeDtypeStruct(a.shape, F32) for a in ws] * 4,
        compiler_params=_params(("arbitrary",)),
    )(*ws, *gs, *ms, *vs)
    return res[:nw], res[nw:2 * nw], res[2 * nw:3 * nw], res[3 * nw:]


def _small_call(gathered, w, m, v):
    def fold(row):
        tot = row[:, 0:LANES] + row[:, LANES:2 * LANES] + row[:, 2 * LANES:3 * LANES] + row[:, 3 * LANES:4 * LANES]
        return tot + pltpu.roll(tot, HEAD_DIM, axis=1)

    def body(ga_ref, w_ref, m_ref, v_ref, g_out, d_out, m_out, v_out):
        g = ga_ref[0]
        for i in range(1, N_DEV):
            g = g + ga_ref[i]
        unfolded = g[4:5, :]
        folded = jnp.concatenate([fold(unfolded[:, :ATTN_WIDTH]), fold(unfolded[:, ATTN_WIDTH:]),
                                  jnp.zeros((1, 1024 - 2 * LANES), F32)], axis=-1)
        row = lax.broadcasted_iota(jnp.int32, g.shape, 0)
        g = jnp.where(row == 3, folded, g)
        delta, mm, vv = _adamw(w_ref[...], g, m_ref[...], v_ref[...])
        g_out[...] = g
        d_out[...] = delta
        m_out[...] = mm
        v_out[...] = vv

    return pl.pallas_call(
        body, name="adamw_small",
        out_shape=[jax.ShapeDtypeStruct(w.shape, F32)] * 4,
        compiler_params=_params(),
    )(gathered, w, m, v)


def _pack_small(p, folded=True, loss=None):
    z = lambda n: jnp.zeros((n,), F32)
    rows = [p["mix_norm_g"], p["mlp_norm_g"],
            jnp.concatenate([p["pool_scale"], p["rel_bias"].reshape(-1), z(1024 - POOL_WIDTH - N_BUCKETS * N_HEADS)])]
    if folded:
        rows += [jnp.concatenate([p["q_norm_g"], z(LANES - HEAD_DIM), p["k_norm_g"], z(1024 - LANES - HEAD_DIM)]), z(1024)]
    else:
        rows += [z(1024), jnp.concatenate([p["q_norm_g"], p["k_norm_g"]])]
    rows += [z(1024) if loss is None else jnp.concatenate([loss.reshape(1), z(1023)])]
    head = jnp.stack(rows + [z(1024)] * 2)
    return jnp.concatenate([head, p["pool_w"].reshape(-1, 1024)], axis=0)


def _unpack_small(a):
    return dict(
        mix_norm_g=a[0], mlp_norm_g=a[1], pool_scale=a[2, :POOL_WIDTH],
        rel_bias=a[2, POOL_WIDTH:POOL_WIDTH + N_BUCKETS * N_HEADS].reshape(N_BUCKETS, N_HEADS),
        q_norm_g=a[3, :HEAD_DIM], k_norm_g=a[3, LANES:LANES + HEAD_DIM],
        pool_w=a[8:].reshape(len(POOL_WINDOWS), LANES, LANES))


_WEIGHT_ORDER = ("mix_norm_g", "w_in", "pool_w", "pool_scale", "q_norm_g", "k_norm_g", "rel_bias", "w_out",
                 "mlp_norm_g", "w_up", "w_down")
_BIG = ("w_in", "w_out", "w_up", "w_down")


def kernel(x, mix_norm_g, w_in, pool_w, pool_scale, q_norm_g, k_norm_g, rel_bias, w_out, mlp_norm_g, w_up, w_down, loss_target, m_mix_norm_g, m_w_in, m_pool_w, m_pool_scale, m_q_norm_g, m_k_norm_g, m_rel_bias, m_w_out, m_mlp_norm_g, m_w_up, m_w_down, v_mix_norm_g, v_w_in, v_pool_w, v_pool_scale, v_q_norm_g, v_k_norm_g, v_rel_bias, v_w_out, v_mlp_norm_g, v_w_up, v_w_down):
    w = dict(mix_norm_g=mix_norm_g, w_in=w_in, pool_w=pool_w, pool_scale=pool_scale, q_norm_g=q_norm_g,
             k_norm_g=k_norm_g, rel_bias=rel_bias, w_out=w_out, mlp_norm_g=mlp_norm_g, w_up=w_up, w_down=w_down)
    m = dict(mix_norm_g=m_mix_norm_g, w_in=m_w_in, pool_w=m_pool_w, pool_scale=m_pool_scale, q_norm_g=m_q_norm_g,
             k_norm_g=m_k_norm_g, rel_bias=m_rel_bias, w_out=m_w_out, mlp_norm_g=m_mlp_norm_g, w_up=m_w_up, w_down=m_w_down)
    v = dict(mix_norm_g=v_mix_norm_g, w_in=v_w_in, pool_w=v_pool_w, pool_scale=v_pool_scale, q_norm_g=v_q_norm_g,
             k_norm_g=v_k_norm_g, rel_bias=v_rel_bias, w_out=v_w_out, mlp_norm_g=v_mlp_norm_g, w_up=v_w_up, w_down=v_w_down)
    xc, yc, cc = _coords()

    c_idx = jnp.reshape(cc, (1,)).astype(jnp.int32)
    chip_idx = jnp.reshape(2 * xc + yc, (1,)).astype(jnp.int32)
    me = 4 * xc + 2 * yc + cc
    whole = lambda t: t.reshape(t.shape[0], t.shape[1] * t.shape[2], t.shape[3])

    placed = [_halves(p) for p in _place_shards_call([w[n] for n in _BIG], chip_idx, nch=4)]
    (win_f,) = _allgather_call(placed[:1], from_chips=True, name="weights_allgather_in")
    wsend, wrecv, in_flight, started = _gather_start_call(placed[1:], win_f)

    def mlp_weights(after):
        landed = _gather_wait_call(in_flight, wsend, wrecv, after)
        wout_f, wup_f, wdown_f = _allgather_call(landed, from_chips=False, name="weights_pair_forward")
        return whole(wout_f).reshape(-1, wout_f.shape[-1]), whole(wup_f), whole(wdown_f)

    split = []

    def on_mlp_grads(*wire_grads):
        srcs = [_halves(g) for g in wire_grads]
        lands = [lax.empty((N_DEV,) + s.shape[2:], s.dtype) for s in srcs]
        split.extend(_scatter_start_call(srcs, lands))
        return split[4]

    loss_part, dx, big_grads, small_grads = _local_grads(
        x[0], loss_target[0], mix_norm_g + started[0, 0], whole(win_f), pool_w, pool_scale, q_norm_g, k_norm_g, rel_bias,
        mlp_norm_g, mlp_weights, on_mlp_grads)
    g_in, g_out, g_up, g_down = big_grads
    gsend, grecv, srcs_thru, lands_thru, _ = split
    lands_mlp = _scatter_wait_call(srcs_thru, lands_thru, gsend, grecv, g_in[1])
    (land_in,), small_all = _scatter_sync_call(
        [_halves(g_in[1])], _pack_small(small_grads, folded=False, loss=loss_part))
    idx = jnp.concatenate([chip_idx, c_idx] + [jnp.reshape(jnp.bitwise_xor(me, r), (1,)) for r in range(1, N_DEV)])
    halves = _reduce_call([_halves(g[0]) for g in (g_in, g_out, g_up, g_down)], [land_in] + lands_mlp,
                          idx.astype(jnp.int32), nch=4)
    g_reduced = _pair_allgather_call(list(halves))
    g_big, d_big, m_big, v_big = _adamw_call(
        [w[n] for n in _BIG], g_reduced, [m[n] for n in _BIG], [v[n] for n in _BIG], nch=8)
    g_pack, d_pack, m_pack, v_pack = _small_call(small_all, _pack_small(w), _pack_small(m), _pack_small(v))

    grads, deltas, new_m, new_v = (_unpack_small(a) for a in (g_pack, d_pack, m_pack, v_pack))
    for i, n in enumerate(_BIG):
        grads[n], deltas[n], new_m[n], new_v[n] = g_big[i], d_big[i], m_big[i], v_big[i]
    loss = g_pack[LOSS_ROW, 0]
    return (loss, dx[None], *[grads[n] for n in _WEIGHT_ORDER], *[deltas[n] for n in _WEIGHT_ORDER],
            *[new_m[n] for n in _WEIGHT_ORDER], *[new_v[n] for n in _WEIGHT_ORDER])
```

```python
import math

import jax
import jax.numpy as jnp
import numpy as np
from jax import lax
from jax.experimental import pallas as pl
from jax.experimental.pallas import tpu as pltpu

F32 = jnp.float32
MXU_DTYPE = jnp.bfloat16
WIRE_DTYPE = jnp.bfloat16

NORM_EPS = 1e-6
NEG_INF = -1e30
LANES = 128
HEAD_DIM = 64
N_HEADS = 8
POOL_WIDTH = 512
ATTN_WIDTH = 512
POOL_WINDOWS = (2, 4, 8, 16)
POOL_HALO = 16
DILATED_PATTERNS = ((128, 1), (512, 4), (2048, 16))
ATT_BLOCK = 128
ATT_SUPER = ATT_BLOCK * max(dl for _, dl in DILATED_PATTERNS)
ATT_UNITS = ATT_SUPER // ATT_BLOCK
N_BUCKETS = 32
MAX_DISTANCE = 2048
N_CHIPS = 4
N_DEV = 8
ADAM_LR, ADAM_B1, ADAM_B2, ADAM_EPS, ADAM_WD, ADAM_STEP = 0.001, 0.9, 0.999, 1e-08, 0.01, 10
VMEM_LIMIT = 56 * 1024 * 1024
MESH = pl.DeviceIdType.MESH
ANY = pl.BlockSpec(memory_space=pl.ANY)

SMALL_ROWS = 72
LOSS_ROW = 5


def _mm(a, b):
    return jnp.dot(a, b, preferred_element_type=F32)


def _mm_nt(a, b):
    return lax.dot_general(a, b, (((1,), (1,)), ((), ())), preferred_element_type=F32)


def _mm_tn(a, b):
    return lax.dot_general(a, b, (((0,), (0,)), ((), ())), preferred_element_type=F32)


def _params(sem=None, **kw):
    if sem is not None:
        kw["dimension_semantics"] = sem
    return pltpu.CompilerParams(vmem_limit_bytes=VMEM_LIMIT, **kw)


def _low_half():
    return lax.broadcasted_iota(jnp.int32, (1, LANES), 1) < HEAD_DIM


def _head_sum_bcast(y):
    lo = _low_half()
    outs = []
    for j in range(y.shape[1] // LANES):
        c = y[:, j * LANES:(j + 1) * LANES]
        s_lo = jnp.sum(jnp.where(lo, c, 0.0), axis=-1, keepdims=True)
        s_hi = jnp.sum(jnp.where(lo, 0.0, c), axis=-1, keepdims=True)
        outs.append(jnp.where(lo, s_lo, s_hi))
    return jnp.concatenate(outs, axis=-1)


def _rms_bwd(dn, hn, r):
    return r * (dn - hn * jnp.mean(dn * hn, axis=-1, keepdims=True))


def _t5_bucket_np(dist):
    max_exact = N_BUCKETS // 2
    d_f = np.maximum(dist, 1).astype(np.float32)
    ratio = (np.log(d_f / np.float32(max_exact)) / np.float32(math.log(MAX_DISTANCE / max_exact))).astype(np.float32)
    large = max_exact + (ratio * np.float32(N_BUCKETS - max_exact)).astype(np.int32)
    large = np.minimum(large, N_BUCKETS - 1)
    return np.where(dist < max_exact, dist, large).astype(np.int32)


def _bucket_tables():
    tables = []
    for _, dl in DILATED_PATTERNS:
        qq, kk = np.arange(ATT_BLOCK), np.arange(2 * ATT_BLOCK)
        if dl == 1:
            qq, kk = _by4_positions(ATT_BLOCK), _by4_positions(2 * ATT_BLOCK)
        dist = np.clip(qq[:, None] + ATT_BLOCK - kk[None, :], 0, ATT_BLOCK)
        tables.append(_t5_bucket_np(dist * dl))
    return np.stack(tables)


def _f1_call(x, g1, win, poolw, pscale, qg, kg, tm):
    s, d = x.shape
    nblk = s // tm

    def body(x_ref, g1_ref, win_ref, pw_ref, ps_ref, qg_ref, kg_ref,
             a_ref, pooled_ref, ypool_ref, q32_ref, k32_ref, qn_ref, kn_ref, v_ref, ubuf):
        i = pl.program_id(0)
        xv = x_ref[...]
        r = lax.rsqrt(jnp.mean(xv * xv, axis=-1, keepdims=True) + NORM_EPS)
        a = ((xv * r) * g1_ref[...]).astype(MXU_DTYPE)
        a_ref[...] = a
        u = _mm(a, win_ref[0])
        q = _mm(a, win_ref[1])
        k = _mm(a, win_ref[2])
        v_ref[...] = _mm(a, win_ref[3]).astype(MXU_DTYPE)
        q32_ref[...] = q
        k32_ref[...] = k
        rq = lax.rsqrt(_head_sum_bcast(q * q) * (1.0 / HEAD_DIM) + NORM_EPS)
        qn_ref[...] = (((q * rq) * qg_ref[...]) * (HEAD_DIM ** -0.5)).astype(MXU_DTYPE)
        rk = lax.rsqrt(_head_sum_bcast(k * k) * (1.0 / HEAD_DIM) + NORM_EPS)
        kn_ref[...] = ((k * rk) * kg_ref[...]).astype(MXU_DTYPE)

        @pl.when(i == 0)
        def _():
            ubuf[0:POOL_HALO, :] = jnp.zeros((POOL_HALO, POOL_WIDTH), F32)

        @pl.when(i > 0)
        def _():
            ubuf[0:POOL_HALO, :] = ubuf[tm:tm + POOL_HALO, :]

        ubuf[POOL_HALO:POOL_HALO + tm, :] = u
        t = i * tm + lax.broadcasted_iota(jnp.int32, (tm, 1), 0)
        for g, w in enumerate(POOL_WINDOWS):
            ls = slice(g * LANES, (g + 1) * LANES)
            ug = u[:, ls]
            acc = ug
            for sh in range(1, w):
                acc = acc + ubuf[POOL_HALO - sh:POOL_HALO - sh + tm, ls]
            cnt = jnp.minimum(t + 1, w).astype(F32)
            pooled = (acc / cnt - ug).astype(MXU_DTYPE)
            pooled_ref[:, ls] = pooled
            ypool_ref[:, ls] = (_mm(pooled, pw_ref[g]) * ps_ref[:, ls]).astype(MXU_DTYPE)

    tok = lambda w: pl.BlockSpec((tm, w), lambda i: (i, 0))
    full = lambda shp: pl.BlockSpec(shp, lambda i: (0,) * len(shp))
    return pl.pallas_call(
        body, name="fwd_inproj",
        grid=(nblk,),
        in_specs=[tok(d), full((1, d)), full(win.shape), full(poolw.shape), full((1, POOL_WIDTH)),
                  full((1, ATTN_WIDTH)), full((1, ATTN_WIDTH))],
        out_specs=[tok(d), tok(POOL_WIDTH), tok(POOL_WIDTH), tok(ATTN_WIDTH), tok(ATTN_WIDTH),
                   tok(ATTN_WIDTH), tok(ATTN_WIDTH), tok(ATTN_WIDTH)],
        out_shape=[jax.ShapeDtypeStruct((s, d), MXU_DTYPE),
                   jax.ShapeDtypeStruct((s, POOL_WIDTH), MXU_DTYPE),
                   jax.ShapeDtypeStruct((s, POOL_WIDTH), MXU_DTYPE),
                   jax.ShapeDtypeStruct((s, ATTN_WIDTH), F32),
                   jax.ShapeDtypeStruct((s, ATTN_WIDTH), F32),
                   jax.ShapeDtypeStruct((s, ATTN_WIDTH), MXU_DTYPE),
                   jax.ShapeDtypeStruct((s, ATTN_WIDTH), MXU_DTYPE),
                   jax.ShapeDtypeStruct((s, ATTN_WIDTH), MXU_DTYPE)],
        scratch_shapes=[pltpu.VMEM((tm + POOL_HALO, POOL_WIDTH), F32)],
        compiler_params=_params(("arbitrary",)),
    )(x, g1, win, poolw, pscale, qg, kg)


DEINT = 4
assert [dl for _, dl in DILATED_PATTERNS] == [1, DEINT, DEINT * DEINT]


def _by4_positions(n):
    pos = np.arange(n)
    return DEINT * (pos % (n // DEINT)) + pos // (n // DEINT)


def _band_mask(n, by4):
    qq = lax.broadcasted_iota(jnp.int32, (ATT_BLOCK, 2 * ATT_BLOCK), 0)
    kk = lax.broadcasted_iota(jnp.int32, (ATT_BLOCK, 2 * ATT_BLOCK), 1)
    if by4:
        nq, nk = ATT_BLOCK // DEINT, 2 * ATT_BLOCK // DEINT
        qq = DEINT * (qq % nq) + qq // nq
        kk = DEINT * (kk % nk) + kk // nk
    dist = qq + ATT_BLOCK - kk
    return (dist >= 0) & (dist <= ATT_BLOCK) & ((n > 0) | (kk >= ATT_BLOCK))


def _unit_rows(u, dl):
    sq, sk = ATT_SUPER // DEINT, 2 * ATT_SUPER // DEINT
    if dl == 1:
        n = ATT_BLOCK // DEINT
        return (u, [pl.ds(pl.multiple_of(r * sq + n * u, 8), n) for r in range(DEINT)],
                [pl.ds(pl.multiple_of(r * sk + sk // 2 + n * (u - 1), 8), 2 * n) for r in range(DEINT)])
    if dl == DEINT:
        r, b = u % DEINT, u // DEINT
        return (b, [pl.ds(pl.multiple_of(r * sq + ATT_BLOCK * b, 8), ATT_BLOCK)],
                [pl.ds(pl.multiple_of(r * sk + sk // 2 + ATT_BLOCK * (b - 1), 8), 2 * ATT_BLOCK)])
    r, a = u % DEINT, u // DEINT
    return 0, [pl.ds(r * sq + a, ATT_BLOCK, stride=DEINT)], [pl.ds(r * sk + a, 2 * ATT_BLOCK, stride=DEINT)]


def _take(ref, runs):
    parts = [ref[run, :] for run in runs]
    return parts[0] if len(parts) == 1 else jnp.concatenate(parts, axis=0)


def _put(ref, runs, value, add=False):
    n = value.shape[0] // len(runs)
    for i, run in enumerate(runs):
        part = value[i * n:(i + 1) * n]
        ref[run, :] = ref[run, :] + part if add else part


def _deinterleave(dst, src, n):
    seg = n // DEINT
    for r in range(DEINT):
        dst[r * seg:(r + 1) * seg, :] = src[pl.ds(r, seg, stride=DEINT), :]


def _interleave(dst, src, n, offset=0):
    seg = n // DEINT
    stride = src.shape[0] // DEINT
    for r in range(DEINT):
        dst[pl.ds(r, seg, stride=DEINT), :] = src[r * stride + offset:r * stride + offset + seg, :]


def _attn_fwd_call(qn, kn, v, bias):
    s, w = qn.shape
    nsb = s // ATT_SUPER
    npair = w // LANES

    def body(q_ref, kc_ref, kp_ref, vc_ref, vp_ref, b_ref, o_ref, lse_ref, tmp, qf, kf, vf, acc_s, m_s, l_s):
        sb = pl.program_id(1)
        tmp[0:ATT_SUPER, :] = q_ref[...].astype(F32)
        _deinterleave(qf, tmp, ATT_SUPER)
        tmp[0:ATT_SUPER, :] = kp_ref[...].astype(F32)
        tmp[ATT_SUPER:, :] = kc_ref[...].astype(F32)
        _deinterleave(kf, tmp, 2 * ATT_SUPER)
        tmp[0:ATT_SUPER, :] = vp_ref[...].astype(F32)
        tmp[ATT_SUPER:, :] = vc_ref[...].astype(F32)
        _deinterleave(vf, tmp, 2 * ATT_SUPER)
        lo = _low_half()
        for p, (_, dl) in enumerate(DILATED_PATTERNS):
            def unit(u, carry, p=p, dl=dl):
                b, rows_q, rows_k = _unit_rows(u, dl)
                valid = _band_mask(sb * (ATT_UNITS // dl) + b, by4=dl == 1)
                qp = _take(qf, rows_q).astype(MXU_DTYPE)
                kcat = _take(kf, rows_k).astype(MXU_DTYPE)
                vcat = _take(vf, rows_k).astype(MXU_DTYPE)
                zero = jnp.zeros_like(qp)
                q2 = jnp.concatenate([jnp.where(lo, qp, zero), jnp.where(lo, zero, qp)], axis=0)
                sc = _mm_nt(q2, kcat) + b_ref[p].reshape(2 * ATT_BLOCK, 2 * ATT_BLOCK)
                sc = jnp.where(jnp.concatenate([valid, valid], axis=0), sc, NEG_INF)
                m2 = jnp.max(sc, axis=-1, keepdims=True)
                pr = jnp.exp(sc - m2)
                l2 = jnp.sum(pr, axis=-1, keepdims=True)
                acc2 = _mm(pr.astype(MXU_DTYPE), vcat)
                acc = jnp.where(lo, acc2[:ATT_BLOCK], acc2[ATT_BLOCK:])
                m = jnp.where(lo, m2[:ATT_BLOCK], m2[ATT_BLOCK:])
                l = jnp.where(lo, l2[:ATT_BLOCK], l2[ATT_BLOCK:])
                if p == 0:
                    _put(acc_s, rows_q, acc)
                    _put(m_s, rows_q, m)
                    _put(l_s, rows_q, l)
                else:
                    m_old = _take(m_s, rows_q)
                    m_new = jnp.maximum(m_old, m)
                    a_old = jnp.exp(m_old - m_new)
                    a_new = jnp.exp(m - m_new)
                    _put(acc_s, rows_q, a_old * _take(acc_s, rows_q) + a_new * acc)
                    _put(l_s, rows_q, a_old * _take(l_s, rows_q) + a_new * l)
                    _put(m_s, rows_q, m_new)
                return carry

            lax.fori_loop(0, ATT_UNITS, unit, 0, unroll=4)
        l = l_s[...]
        acc_s[...] = acc_s[...] / l
        m_s[...] = m_s[...] + jnp.log(l)
        _interleave(o_ref, acc_s, ATT_SUPER)
        _interleave(lse_ref, m_s, ATT_SUPER)

    cur = pl.BlockSpec((ATT_SUPER, LANES), lambda j, t: (t, j))
    prev = pl.BlockSpec((ATT_SUPER, LANES), lambda j, t: (jnp.maximum(t - 1, 0), j))
    bspec = pl.BlockSpec((len(DILATED_PATTERNS), 2, ATT_BLOCK, 2 * ATT_BLOCK), lambda j, t: (0, j, 0, 0))
    return pl.pallas_call(
        body, name="attn_fwd",
        grid=(npair, nsb),
        in_specs=[cur, cur, prev, cur, prev, bspec],
        out_specs=[cur, cur],
        out_shape=[jax.ShapeDtypeStruct((s, w), F32), jax.ShapeDtypeStruct((s, w), F32)],
        scratch_shapes=[pltpu.VMEM((2 * ATT_SUPER, LANES), F32),
                        pltpu.VMEM((ATT_SUPER, LANES), F32), pltpu.VMEM((2 * ATT_SUPER, LANES), F32),
                        pltpu.VMEM((2 * ATT_SUPER, LANES), F32), pltpu.VMEM((ATT_SUPER, LANES), F32),
                        pltpu.VMEM((ATT_SUPER, LANES), F32), pltpu.VMEM((ATT_SUPER, LANES), F32)],
        compiler_params=_params(("arbitrary", "arbitrary")),
    )(qn, kn, kn, v, v, bias)


def _attn_bwd_call(qn, kn, v, do, lse, delta, bias, dep=None):
    s, w = qn.shape
    nsb = s // ATT_SUPER
    npair = w // LANES
    deps = [] if dep is None else [dep]

    def body(q_ref, kc_ref, kp_ref, vc_ref, vp_ref, do_ref, lse_ref, dlt_ref, b_ref, *rest):
        dq_ref, dk_ref, dv_ref, db_ref, tmp, qf, kf, vf, dof, lsef, dltf, dqf, dkf, dvf = rest[len(deps):]
        step = pl.program_id(1)
        sb = nsb - 1 - step
        seg = ATT_SUPER // DEINT
        tmp[0:ATT_SUPER, :] = q_ref[...].astype(F32)
        _deinterleave(qf, tmp, ATT_SUPER)
        tmp[0:ATT_SUPER, :] = do_ref[...].astype(F32)
        _deinterleave(dof, tmp, ATT_SUPER)
        tmp[0:ATT_SUPER, :] = kp_ref[...].astype(F32)
        tmp[ATT_SUPER:, :] = kc_ref[...].astype(F32)
        _deinterleave(kf, tmp, 2 * ATT_SUPER)
        tmp[0:ATT_SUPER, :] = vp_ref[...].astype(F32)
        tmp[ATT_SUPER:, :] = vc_ref[...].astype(F32)
        _deinterleave(vf, tmp, 2 * ATT_SUPER)
        _deinterleave(lsef, lse_ref, ATT_SUPER)
        _deinterleave(dltf, dlt_ref, ATT_SUPER)

        @pl.when(step == 0)
        def _():
            db_ref[...] = jnp.zeros(db_ref.shape, F32)

        for acc in (dkf, dvf):
            for r in range(DEINT):
                this, before = pl.ds((2 * r + 1) * seg, seg), pl.ds(2 * r * seg, seg)

                @pl.when(step == 0)
                def _(acc=acc, this=this):
                    acc[this, :] = jnp.zeros((seg, LANES), F32)

                @pl.when(step > 0)
                def _(acc=acc, this=this, before=before):
                    acc[this, :] = acc[before, :]

                acc[before, :] = jnp.zeros((seg, LANES), F32)
        lo = _low_half()
        for p, (_, dl) in enumerate(DILATED_PATTERNS):
            def unit(u, carry, p=p, dl=dl):
                b, rows_q, rows_k = _unit_rows(u, dl)
                valid = _band_mask(sb * (ATT_UNITS // dl) + b, by4=dl == 1)
                qp = _take(qf, rows_q).astype(MXU_DTYPE)
                dop = _take(dof, rows_q).astype(MXU_DTYPE)
                kcat = _take(kf, rows_k).astype(MXU_DTYPE)
                vcat = _take(vf, rows_k).astype(MXU_DTYPE)
                lse2 = _take(lsef, rows_q)
                dlt2 = _take(dltf, rows_q)
                zero = jnp.zeros_like(qp)
                q2 = jnp.concatenate([jnp.where(lo, qp, zero), jnp.where(lo, zero, qp)], axis=0)
                do2 = jnp.concatenate([jnp.where(lo, dop, zero), jnp.where(lo, zero, dop)], axis=0)
                lse_c = jnp.concatenate([lse2[:, 0:1], lse2[:, HEAD_DIM:HEAD_DIM + 1]], axis=0)
                dlt_c = jnp.concatenate([dlt2[:, 0:1], dlt2[:, HEAD_DIM:HEAD_DIM + 1]], axis=0)
                sc = _mm_nt(q2, kcat) + b_ref[p].reshape(2 * ATT_BLOCK, 2 * ATT_BLOCK)
                pr = jnp.where(jnp.concatenate([valid, valid], axis=0), jnp.exp(sc - lse_c), 0.0)
                ds = pr * (_mm_nt(do2, vcat) - dlt_c)
                db_ref[p] += ds.reshape(2, ATT_BLOCK, 2 * ATT_BLOCK)
                ds_c = ds.astype(MXU_DTYPE)
                dq2 = _mm(ds_c, kcat)
                dk = _mm_tn(ds_c, q2)
                dv = _mm_tn(pr.astype(MXU_DTYPE), do2)
                dq = jnp.where(lo, dq2[:ATT_BLOCK], dq2[ATT_BLOCK:])
                _put(dqf, rows_q, dq, add=p > 0)
                _put(dkf, rows_k, dk, add=True)
                _put(dvf, rows_k, dv, add=True)
                return carry

            lax.fori_loop(0, ATT_UNITS, unit, 0, unroll=4)
        _interleave(dq_ref, dqf, ATT_SUPER)
        _interleave(dk_ref, dkf, ATT_SUPER, offset=seg)
        _interleave(dv_ref, dvf, ATT_SUPER, offset=seg)

    cur = pl.BlockSpec((ATT_SUPER, LANES), lambda j, t: (nsb - 1 - t, j))
    prev = pl.BlockSpec((ATT_SUPER, LANES), lambda j, t: (jnp.maximum(nsb - 2 - t, 0), j))
    bshape = (len(DILATED_PATTERNS), 2, ATT_BLOCK, 2 * ATT_BLOCK)
    bspec = pl.BlockSpec(bshape, lambda j, t: (0, j, 0, 0))
    sup = lambda: pltpu.VMEM((ATT_SUPER, LANES), F32)
    sup2 = lambda: pltpu.VMEM((2 * ATT_SUPER, LANES), F32)
    return pl.pallas_call(
        body, name="attn_bwd",
        grid=(npair, nsb),
        in_specs=[cur, cur, prev, cur, prev, cur, cur, cur, bspec] + [ANY] * len(deps),
        out_specs=[cur, cur, cur, bspec],
        out_shape=[jax.ShapeDtypeStruct((s, w), F32)] * 3 + [jax.ShapeDtypeStruct(bias.shape, F32)],
        scratch_shapes=[sup2(), sup(), sup2(), sup2(), sup(), sup(), sup(), sup(), sup2(), sup2()],
        compiler_params=_params(("arbitrary", "arbitrary")),
    )(qn, kn, kn, v, v, do, lse, delta, bias, *deps)


def _bias_table_call(rel_bias, buckets):
    npat = buckets.shape[0]

    def body(rb_ref, bk_ref, out_ref):
        for p in range(npat):
            for half in range(2):
                ks = slice(half * ATT_BLOCK, (half + 1) * ATT_BLOCK)
                bk = bk_ref[p, :, ks]
                for h in range(N_HEADS):
                    def pick(b, acc, h=h, bk=bk):
                        return jnp.where(bk == b, rb_ref[b, h], acc)

                    out_ref[p, h, :, ks] = lax.fori_loop(0, N_BUCKETS, pick, jnp.zeros((ATT_BLOCK, ATT_BLOCK), F32))

    return pl.pallas_call(
        body, name="bias_table",
        in_specs=[pl.BlockSpec(memory_space=pltpu.SMEM), pl.BlockSpec(memory_space=pltpu.VMEM)],
        out_shape=jax.ShapeDtypeStruct((npat, N_HEADS, ATT_BLOCK, 2 * ATT_BLOCK), F32),
        compiler_params=_params(),
    )(rel_bias, buckets)


def _rel_bias_grad_call(dbias, buckets):
    npat, nh = dbias.shape[0], dbias.shape[1]

    def body(db_ref, bk_ref, out_ref):
        lane = lax.broadcasted_iota(jnp.int32, (nh, LANES), 1)
        out = jnp.zeros((nh, LANES), F32)
        for b in range(N_BUCKETS):
            tot = jnp.zeros((nh, 1), F32)
            for p in range(npat):
                hit = jnp.where(bk_ref[p][None] == b, db_ref[p], 0.0)
                tot = tot + jnp.sum(jnp.sum(hit, axis=2), axis=1, keepdims=True)
            out = jnp.where(lane == b, tot, out)
        out_ref[...] = out

    return pl.pallas_call(
        body, name="rel_bias_grad",
        out_shape=jax.ShapeDtypeStruct((nh, LANES), F32),
        compiler_params=_params(),
    )(dbias, buckets)


def _f2_call(x, tgt, ypool, o, wout, wup, wdown, g2, tm):
    s, d = x.shape
    nblk = s // tm
    nch, _, fch = wup.shape
    dff = nch * fch
    mixw = POOL_WIDTH + ATTN_WIDTH

    def body(x_ref, t_ref, yp_ref, o_ref, g2_ref, wout_hbm, wup_hbm, wdown_hbm,
             mixed_ref, c_ref, ff_ref, dz_ref, dy_ref, dh1_ref, dyp_ref, do_ref, dlt_ref, dg2_ref, loss_ref,
             wout_v, wup_v, wdown_v, rz):
        i = pl.program_id(0)

        @pl.when(i == 0)
        def _():
            pltpu.sync_copy(wout_hbm, wout_v)
            pltpu.sync_copy(wup_hbm, wup_v)
            pltpu.sync_copy(wdown_hbm, wdown_v)
            dg2_ref[...] = jnp.zeros(dg2_ref.shape, F32)
            loss_ref[...] = jnp.zeros(loss_ref.shape, F32)

        o = o_ref[...]
        mixed = jnp.concatenate([yp_ref[...], o.astype(MXU_DTYPE)], axis=-1)
        mixed_ref[...] = mixed
        h1 = x_ref[...] + _mm(mixed, wout_v[...])
        r2 = lax.rsqrt(jnp.mean(h1 * h1, axis=-1, keepdims=True) + NORM_EPS)
        hn = h1 * r2
        c = (hn * g2_ref[...]).astype(MXU_DTYPE)
        c_ref[...] = c
        y = h1
        for j in range(nch):
            cs = slice(j * fch, (j + 1) * fch)
            z = jnp.maximum(_mm(c, wup_v[j]), 0.0)
            rz[:, cs] = z
            ff = (z * z).astype(MXU_DTYPE)
            ff_ref[:, cs] = ff
            y = y + _mm(ff, wdown_v[j])
        err = y - t_ref[...]
        loss_ref[...] += jnp.sum(err * err) * (0.5 / d)
        dy = err * (1.0 / d)
        dy_c = dy.astype(MXU_DTYPE)
        dy_ref[...] = dy_c
        dc = jnp.zeros((tm, d), F32)
        for j in range(nch):
            cs = slice(j * fch, (j + 1) * fch)
            dz = (_mm_nt(dy_c, wdown_v[j]) * (2.0 * rz[:, cs])).astype(MXU_DTYPE)
            dz_ref[:, cs] = dz
            dc = dc + _mm_nt(dz, wup_v[j])
        dg2_ref[...] += jnp.sum(dc * hn, axis=0, keepdims=True)
        dh1 = dy + _rms_bwd(dc * g2_ref[...], hn, r2)
        dh1_ref[...] = dh1
        dmix = _mm_nt(dh1.astype(MXU_DTYPE), wout_v[...])
        dyp_ref[...] = dmix[:, :POOL_WIDTH]
        do = dmix[:, POOL_WIDTH:]
        do_ref[...] = do.astype(MXU_DTYPE)
        dlt_ref[...] = _head_sum_bcast(do * o)

    tok = lambda w: pl.BlockSpec((tm, w), lambda i: (i, 0))
    const = lambda shp: pl.BlockSpec(shp, lambda i: (0,) * len(shp))
    return pl.pallas_call(
        body, name="fwd_mlp_bwd_mlp",
        grid=(nblk,),
        in_specs=[tok(d), tok(d), tok(POOL_WIDTH), tok(ATTN_WIDTH), const((1, d)), ANY, ANY, ANY],
        out_specs=[tok(mixw), tok(d), tok(dff), tok(dff), tok(d), tok(d), tok(POOL_WIDTH), tok(ATTN_WIDTH),
                   tok(ATTN_WIDTH), const((1, d)), const((1, LANES))],
        out_shape=[jax.ShapeDtypeStruct((s, mixw), MXU_DTYPE),
                   jax.ShapeDtypeStruct((s, d), MXU_DTYPE),
                   jax.ShapeDtypeStruct((s, dff), MXU_DTYPE),
                   jax.ShapeDtypeStruct((s, dff), MXU_DTYPE),
                   jax.ShapeDtypeStruct((s, d), MXU_DTYPE),
                   jax.ShapeDtypeStruct((s, d), F32),
                   jax.ShapeDtypeStruct((s, POOL_WIDTH), F32),
                   jax.ShapeDtypeStruct((s, ATTN_WIDTH), MXU_DTYPE),
                   jax.ShapeDtypeStruct((s, ATTN_WIDTH), F32),
                   jax.ShapeDtypeStruct((1, d), F32),
                   jax.ShapeDtypeStruct((1, LANES), F32)],
        scratch_shapes=[pltpu.VMEM(wout.shape, MXU_DTYPE), pltpu.VMEM(wup.shape, MXU_DTYPE),
                        pltpu.VMEM(wdown.shape, MXU_DTYPE), pltpu.VMEM((tm, dff), F32)],
        compiler_params=_params(("arbitrary",)),
    )(x, tgt, ypool, o, g2, wout, wup, wdown)


def _bproj_call(dqn, dkn, dv, q32, k32, dypool, pooled, x, dh1, win, poolw, pscale, qg, kg, g1, tm):
    s, d = x.shape
    nblk = s // tm
    ngrp = len(POOL_WINDOWS)

    def body(dqn_ref, dkn_ref, dv_ref, q_ref, k_ref, dyp_ref, pooled_ref, x_ref, dh1_ref,
             win_hbm, pw_ref, ps_ref, qg_ref, kg_ref, g1_ref,
             dx_ref, dproj_ref, dg1_ref, dqg_ref, dkg_ref, dpw_ref, dps_ref, win_v, ebuf):
        step = pl.program_id(0)
        i = nblk - 1 - step

        @pl.when(step == 0)
        def _():
            pltpu.sync_copy(win_hbm, win_v)
            dg1_ref[...] = jnp.zeros(dg1_ref.shape, F32)
            dqg_ref[...] = jnp.zeros(dqg_ref.shape, F32)
            dkg_ref[...] = jnp.zeros(dkg_ref.shape, F32)
            dpw_ref[...] = jnp.zeros(dpw_ref.shape, F32)
            dps_ref[...] = jnp.zeros(dps_ref.shape, F32)
            ebuf[tm:tm + POOL_HALO, :] = jnp.zeros((POOL_HALO, POOL_WIDTH), F32)

        @pl.when(step > 0)
        def _():
            ebuf[tm:tm + POOL_HALO, :] = ebuf[0:POOL_HALO, :]

        def qk_bwd(dn_sum, raw, gain, scale, dgain_ref):
            rr = lax.rsqrt(_head_sum_bcast(raw * raw) * (1.0 / HEAD_DIM) + NORM_EPS)
            hn = raw * rr
            dgain_ref[...] += jnp.sum(dn_sum * hn, axis=0, keepdims=True) * scale
            dn = dn_sum * (gain * scale)
            return rr * (dn - hn * (_head_sum_bcast(dn * hn) * (1.0 / HEAD_DIM)))

        dq = qk_bwd(dqn_ref[...], q_ref[...], qg_ref[...], HEAD_DIM ** -0.5, dqg_ref)
        dk = qk_bwd(dkn_ref[...], k_ref[...], kg_ref[...], 1.0, dkg_ref)

        t = i * tm + lax.broadcasted_iota(jnp.int32, (tm, 1), 0)
        dpooled = []
        for g, w in enumerate(POOL_WINDOWS):
            ls = slice(g * LANES, (g + 1) * LANES)
            dm = dyp_ref[:, ls]
            pg = pooled_ref[:, ls]
            dps_ref[:, ls] += jnp.sum(dm * _mm(pg, pw_ref[g]), axis=0, keepdims=True)
            dms = (dm * ps_ref[:, ls]).astype(MXU_DTYPE)
            dpw_ref[g] += _mm_tn(pg, dms)
            dpg = _mm_nt(dms, pw_ref[g])
            dpooled.append(dpg)
            ebuf[0:tm, ls] = dpg / jnp.minimum(t + 1, w).astype(F32)
        du = []
        for g, w in enumerate(POOL_WINDOWS):
            ls = slice(g * LANES, (g + 1) * LANES)
            acc = ebuf[0:tm, ls]
            for sh in range(1, w):
                acc = acc + ebuf[sh:sh + tm, ls]
            du.append(acc - dpooled[g])
        parts = [jnp.concatenate(du, axis=-1), dq, dk, dv_ref[...]]
        da = jnp.zeros((tm, d), F32)
        for p, part in enumerate(parts):
            pc = part.astype(MXU_DTYPE)
            dproj_ref[:, p * POOL_WIDTH:(p + 1) * POOL_WIDTH] = pc
            da = da + _mm_nt(pc, win_v[p])
        xv = x_ref[...]
        r = lax.rsqrt(jnp.mean(xv * xv, axis=-1, keepdims=True) + NORM_EPS)
        xn = xv * r
        dg1_ref[...] += jnp.sum(da * xn, axis=0, keepdims=True)
        dx_ref[...] = dh1_ref[...] + _rms_bwd(da * g1_ref[...], xn, r)

    tok = lambda w: pl.BlockSpec((tm, w), lambda t: (nblk - 1 - t, 0))
    const = lambda shp: pl.BlockSpec(shp, lambda t: (0,) * len(shp))
    return pl.pallas_call(
        body, name="bwd_inproj",
        grid=(nblk,),
        in_specs=[tok(ATTN_WIDTH)] * 5 + [tok(POOL_WIDTH), tok(POOL_WIDTH), tok(d), tok(d),
                                          ANY, const(poolw.shape), const((1, POOL_WIDTH)), const((1, ATTN_WIDTH)),
                                          const((1, ATTN_WIDTH)), const((1, d))],
        out_specs=[tok(d), tok(4 * POOL_WIDTH), const((1, d)), const((1, ATTN_WIDTH)), const((1, ATTN_WIDTH)),
                   const((ngrp, LANES, LANES)), const((1, POOL_WIDTH))],
        out_shape=[jax.ShapeDtypeStruct((s, d), F32),
                   jax.ShapeDtypeStruct((s, 4 * POOL_WIDTH), MXU_DTYPE),
                   jax.ShapeDtypeStruct((1, d), F32),
                   jax.ShapeDtypeStruct((1, ATTN_WIDTH), F32),
                   jax.ShapeDtypeStruct((1, ATTN_WIDTH), F32),
                   jax.ShapeDtypeStruct((ngrp, LANES, LANES), F32),
                   jax.ShapeDtypeStruct((1, POOL_WIDTH), F32)],
        scratch_shapes=[pltpu.VMEM(win.shape, MXU_DTYPE), pltpu.VMEM((tm + POOL_HALO, POOL_WIDTH), F32)],
        compiler_params=_params(("arbitrary",)),
    )(dqn, dkn, dv, q32, k32, dypool, pooled, x, dh1, win, poolw, pscale, qg, kg, g1)


def _wgrad_call(a, b, bm, bn, bk, out_shape, out_block, out_index, name):
    s, m = a.shape
    _, n = b.shape
    nk = s // bk

    def body(a_ref, b_ref, o_ref, wire_ref):
        k = pl.program_id(2)

        @pl.when(k == 0)
        def _():
            o_ref[...] = jnp.zeros(o_ref.shape, F32)

        o_ref[...] += _mm_tn(a_ref[...].astype(MXU_DTYPE), b_ref[...].astype(MXU_DTYPE))

        @pl.when(k == nk - 1)
        def _():
            wire_ref[...] = o_ref[...].astype(WIRE_DTYPE)

    return pl.pallas_call(
        body, name=name,
        grid=(m // bm, n // bn, nk),
        in_specs=[pl.BlockSpec((bk, bm), lambda i, j, k: (k, i)), pl.BlockSpec((bk, bn), lambda i, j, k: (k, j))],
        out_specs=[pl.BlockSpec(out_block, out_index)] * 2,
        out_shape=[jax.ShapeDtypeStruct(out_shape, F32), jax.ShapeDtypeStruct(out_shape, WIRE_DTYPE)],
        compiler_params=_params(("arbitrary", "arbitrary", "arbitrary")),
    )(a, b)


def _local_grads(x, tgt, g1, win, poolw, pscale, qg, kg, rel_bias, g2, mlp_weights, on_mlp_grads=None):
    s, d = x.shape
    g1r, g2r = g1.reshape(1, d), g2.reshape(1, d)
    psr = pscale.reshape(1, POOL_WIDTH)
    qgr = jnp.tile(qg, N_HEADS).reshape(1, ATTN_WIDTH)
    kgr = jnp.tile(kg, N_HEADS).reshape(1, ATTN_WIDTH)
    pw_c = poolw.astype(MXU_DTYPE)
    buckets = jnp.asarray(_bucket_tables())
    bias = _bias_table_call(rel_bias, buckets)
    bk = min(s, 2048)

    a, pooled, ypool, q32, k32, qn, kn, v = _f1_call(x, g1r, win, pw_c, psr, qgr, kgr, tm=512)
    o, lse = _attn_fwd_call(qn, kn, v, bias)
    wout, wup, wdown = mlp_weights(o)
    mixed, c, ff, dz, dy, dh1, dypool, do, delta, dg2, loss = _f2_call(x, tgt, ypool, o, wout, wup, wdown, g2r, tm=256)
    dff = ff.shape[1]
    g_out = [g.reshape(N_CHIPS, d // N_CHIPS, d)
             for g in _wgrad_call(mixed, dh1, d, d, bk // 2, (d, d), (d, d), lambda i, j, k: (0, 0), "wgrad_out")]
    g_up = _wgrad_call(c, dz, d, dff // N_CHIPS, bk, (N_CHIPS, d, dff // N_CHIPS), (None, d, dff // N_CHIPS),
                       lambda i, j, k: (j, 0, 0), "wgrad_up")
    g_down = _wgrad_call(ff, dy, dff // N_CHIPS, d, bk, (N_CHIPS, dff // N_CHIPS, d), (None, dff // N_CHIPS, d),
                         lambda i, j, k: (i, 0, 0), "wgrad_down")
    dep = None if on_mlp_grads is None else on_mlp_grads(g_out[1], g_up[1], g_down[1])
    dqn, dkn, dv, dbias = _attn_bwd_call(qn, kn, v, do, lse, delta, bias, dep)
    dx, dproj, dg1, dqg, dkg, dpw, dps = _bproj_call(
        dqn, dkn, dv, q32, k32, dypool, pooled, x, dh1, win, pw_c, psr, qgr, kgr, g1r, tm=256)
    nin = dproj.shape[1] // N_CHIPS
    g_in = _wgrad_call(a, dproj, d, nin, bk, (N_CHIPS, d, nin), (None, d, nin), lambda i, j, k: (j, 0, 0), "wgrad_in")
    drb = _rel_bias_grad_call(dbias, buckets)
    small = dict(
        mix_norm_g=dg1.reshape(d), mlp_norm_g=dg2.reshape(d), pool_scale=dps.reshape(POOL_WIDTH),
        q_norm_g=dqg.reshape(ATTN_WIDTH), k_norm_g=dkg.reshape(ATTN_WIDTH),
        rel_bias=drb[:, :N_BUCKETS].T, pool_w=dpw)
    return loss[0, 0], dx, (g_in, g_out, g_up, g_down), small


def _coords():
    return lax.axis_index("x"), lax.axis_index("y"), lax.axis_index("c")


def _other_chips(x, y):
    return [(1 - x, y), (x, 1 - y), (1 - x, 1 - y)]


def _remote(src, dst, send_sem, recv_sem, dev):
    return pltpu.make_async_remote_copy(src_ref=src, dst_ref=dst, send_sem=send_sem, recv_sem=recv_sem,
                                        device_id=dev, device_id_type=MESH)


def _halves(a):
    return a.reshape(a.shape[:-2] + (2, a.shape[-2] // 2, a.shape[-1]))


def _place_shards_call(shards, chip_idx, nch):
    nw = len(shards)

    def body(chip_ref, *refs):
        for w in range(nw):
            refs[nw + w][...] = refs[w][...].astype(WIRE_DTYPE)

    in_specs = [pl.BlockSpec((s.shape[0] // nch, s.shape[1]), lambda i, chip_ref: (i, 0)) for s in shards]
    out_specs = [pl.BlockSpec((None, s.shape[0] // nch, s.shape[1]), lambda i, chip_ref: (chip_ref[0], i, 0))
                 for s in shards]
    return pl.pallas_call(
        body, name="weights_place",
        grid_spec=pltpu.PrefetchScalarGridSpec(num_scalar_prefetch=1, grid=(nch,),
                                               in_specs=in_specs, out_specs=out_specs),
        out_shape=[jax.ShapeDtypeStruct((N_CHIPS,) + s.shape, WIRE_DTYPE) for s in shards],
        compiler_params=_params(("arbitrary",)),
    )(chip_idx, *shards)


def _allgather_call(placed, from_chips, name):
    nw = len(placed)
    ncp = 3 * nw

    def body(*refs):
        outs = refs[nw:2 * nw]
        send1, recv1, send2, recv2 = refs[2 * nw:]
        x, y, c = _coords()
        chip = 2 * x + y
        others = _other_chips(x, y)
        first, passed = [], []
        if from_chips:
            for w in range(nw):
                for k, (ox, oy) in enumerate(others):
                    mine = outs[w].at[chip, c]
                    cp = _remote(mine, mine, send1.at[3 * w + k], recv1.at[3 * w + k], (ox, oy, c))
                    cp.start()
                    first.append(cp)
        for w in range(nw):
            for k, (ox, oy) in enumerate(others):
                piece = outs[w].at[2 * ox + oy, c]
                if from_chips:
                    _remote(piece, piece, send1.at[3 * w + k], recv1.at[3 * w + k], (ox, oy, c)).wait_recv()
                cp = _remote(piece, piece, send2.at[3 * w + k], recv2.at[3 * w + k], (x, y, 1 - c))
                cp.start()
                passed.append(cp)
        for w in range(nw):
            for k, (ox, oy) in enumerate(others):
                piece = outs[w].at[2 * ox + oy, 1 - c]
                _remote(piece, piece, send2.at[3 * w + k], recv2.at[3 * w + k], (x, y, 1 - c)).wait_recv()
        for cp in first + passed:
            cp.wait_send()

    return pl.pallas_call(
        body, name=name,
        in_specs=[ANY] * nw, out_specs=[ANY] * nw,
        out_shape=[jax.ShapeDtypeStruct(s.shape, s.dtype) for s in placed],
        input_output_aliases={w: w for w in range(nw)},
        scratch_shapes=[pltpu.SemaphoreType.DMA((ncp,))] * 4,
    )(*placed)


HBM_SPEC = pl.BlockSpec(memory_space=pltpu.HBM)
SEM_SPEC = pl.BlockSpec(memory_space=pltpu.SEMAPHORE)
SPLIT_EFFECT = pltpu.SideEffectType.DATAFLOW_SIDE_EFFECTING


def _in_hbm(a):
    return pltpu.with_memory_space_constraint(a, pltpu.HBM)


def _gather_copies(bufs, send, recv):
    x, y, c = _coords()
    chip = 2 * x + y
    cps = []
    for w, buf in enumerate(bufs):
        for k, (ox, oy) in enumerate(_other_chips(x, y)):
            mine, theirs = buf.at[chip, c], buf.at[2 * ox + oy, c]
            sems = (send.at[3 * w + k], recv.at[3 * w + k], (ox, oy, c))
            cps.append((_remote(mine, mine, *sems), _remote(theirs, theirs, *sems)))
    return cps


def _gather_start_call(bufs, after):
    nw = len(bufs)

    def body(*refs):
        ins, send, recv, token = refs[:nw], refs[nw + 1], refs[nw + 2], refs[2 * nw + 3]
        for out, _ in _gather_copies(ins, send, recv):
            out.start()
        token[...] = jnp.zeros(token.shape, F32)

    res = pl.pallas_call(
        body, name="weights_gather_start",
        in_specs=[HBM_SPEC] * nw + [ANY],
        out_specs=[SEM_SPEC, SEM_SPEC] + [HBM_SPEC] * nw + [pl.BlockSpec(memory_space=pltpu.VMEM)],
        out_shape=[pltpu.SemaphoreType.DMA((3 * nw,)), pltpu.SemaphoreType.DMA((3 * nw,))]
        + [pltpu.HBM(b.shape, b.dtype) for b in bufs] + [jax.ShapeDtypeStruct((8, LANES), F32)],
        input_output_aliases={w: 2 + w for w in range(nw)},
        compiler_params=pltpu.CompilerParams(has_side_effects=SPLIT_EFFECT),
    )(*[_in_hbm(b) for b in bufs], after)
    return res[0], res[1], list(res[2:2 + nw]), res[2 + nw]


def _gather_wait_call(bufs, send, recv, after):
    nw = len(bufs)

    def body(*refs):
        ins, send, recv = refs[:nw], refs[nw], refs[nw + 1]
        for out, back in _gather_copies(ins, send, recv):
            out.wait_send()
            back.wait_recv()

    return pl.pallas_call(
        body, name="weights_gather_wait",
        in_specs=[HBM_SPEC] * nw + [SEM_SPEC, SEM_SPEC, ANY],
        out_specs=[HBM_SPEC] * nw,
        out_shape=[pltpu.HBM(b.shape, b.dtype) for b in bufs],
        input_output_aliases={w: w for w in range(nw)},
        compiler_params=pltpu.CompilerParams(has_side_effects=SPLIT_EFFECT),
    )(*bufs, send, recv, after)


def _scatter_copies(srcs, lands, send, recv, wholes):
    x, y, c = _coords()
    me = 4 * x + 2 * y + c
    cps = []
    for w, (src, land) in enumerate(zip(srcs, lands)):
        for r in range(1, N_DEV):
            px, py, pc = ((1 - x) if r & 4 else x, (1 - y) if r & 2 else y, (1 - c) if r & 1 else c)
            sems = (send.at[(N_DEV - 1) * w + r - 1], recv.at[(N_DEV - 1) * w + r - 1], (px, py, pc))
            piece = src if wholes[w] else src.at[2 * px + py, pc]
            cps.append((_remote(piece, land.at[me], *sems), _remote(piece, land.at[4 * px + 2 * py + pc], *sems)))
    return cps


def _scatter_start_call(srcs, lands, wholes, name):
    nw = len(srcs)
    ncp = (N_DEV - 1) * nw

    def body(*refs):
        ins, lnd, send, recv, token = refs[:nw], refs[nw:2 * nw], refs[2 * nw], refs[2 * nw + 1], refs[4 * nw + 2]
        for out, _ in _scatter_copies(ins, lnd, send, recv, wholes):
            out.start()
        token[...] = jnp.zeros(token.shape, F32)

    res = pl.pallas_call(
        body, name=name,
        in_specs=[HBM_SPEC] * (2 * nw),
        out_specs=[SEM_SPEC, SEM_SPEC] + [HBM_SPEC] * (2 * nw) + [pl.BlockSpec(memory_space=pltpu.VMEM)],
        out_shape=[pltpu.SemaphoreType.DMA((ncp,)), pltpu.SemaphoreType.DMA((ncp,))]
        + [pltpu.HBM(b.shape, b.dtype) for b in list(srcs) + list(lands)] + [jax.ShapeDtypeStruct((8, LANES), F32)],
        input_output_aliases={i: 2 + i for i in range(2 * nw)},
        compiler_params=pltpu.CompilerParams(has_side_effects=SPLIT_EFFECT),
    )(*[_in_hbm(b) for b in list(srcs) + list(lands)])
    return res[0], res[1], list(res[2:2 + nw]), list(res[2 + nw:2 + 2 * nw]), res[2 + 2 * nw]


def _scatter_wait_call(srcs, lands, send, recv, after, wholes, name):
    nw = len(srcs)

    def body(*refs):
        ins, lnd, send, recv = refs[:nw], refs[nw:2 * nw], refs[2 * nw], refs[2 * nw + 1]
        for out, back in _scatter_copies(ins, lnd, send, recv, wholes):
            out.wait_send()
            back.wait_recv()

    res = pl.pallas_call(
        body, name=name,
        in_specs=[HBM_SPEC] * (2 * nw) + [SEM_SPEC, SEM_SPEC, ANY],
        out_specs=[HBM_SPEC] * (2 * nw),
        out_shape=[pltpu.HBM(b.shape, b.dtype) for b in list(srcs) + list(lands)],
        input_output_aliases={i: i for i in range(2 * nw)},
        compiler_params=pltpu.CompilerParams(has_side_effects=SPLIT_EFFECT),
    )(*srcs, *lands, send, recv, after)
    return list(res[nw:])


def _reduce_call(own, lands, idx, nch, name, dep=None):
    nw = len(own)
    deps = [] if dep is None else [dep]

    def body(idx_ref, *refs):
        refs = refs[:2 * nw] + refs[2 * nw + len(deps):]
        for w in range(nw):
            tot = refs[w][...]
            for r in range(1, N_DEV):
                tot = tot + refs[nw + w][idx_ref[1 + r]].astype(F32)
            refs[2 * nw + w][...] = tot

    in_specs, out_specs, out_shape = [], [], []
    for s in own:
        in_specs.append(pl.BlockSpec((None, None, s.shape[2] // nch, s.shape[3]),
                                     lambda i, idx_ref: (idx_ref[0], idx_ref[1], i, 0)))
    for s in own:
        in_specs.append(pl.BlockSpec((N_DEV, s.shape[2] // nch, s.shape[3]), lambda i, idx_ref: (0, i, 0)))
    for s in own:
        out_specs.append(pl.BlockSpec((None, s.shape[2] // nch, s.shape[3]), lambda i, idx_ref: (idx_ref[1], i, 0)))
        out_shape.append(jax.ShapeDtypeStruct((2,) + s.shape[2:], F32))
    return pl.pallas_call(
        body, name=name,
        grid_spec=pltpu.PrefetchScalarGridSpec(num_scalar_prefetch=1, grid=(nch,),
                                               in_specs=in_specs + [ANY] * len(deps), out_specs=out_specs),
        out_shape=out_shape,
        compiler_params=_params(("arbitrary",)),
    )(idx, *own, *lands, *deps)


def _pair_allgather_call(halves, name):
    nw = len(halves)

    def body(*refs):
        outs = refs[nw:2 * nw]
        send, recv = refs[2 * nw:]
        x, y, c = _coords()
        cps = []
        for w in range(nw):
            cp = _remote(outs[w].at[c], outs[w].at[c], send.at[w], recv.at[w], (x, y, 1 - c))
            cp.start()
            cps.append(cp)
        for w in range(nw):
            theirs = outs[w].at[1 - c]
            _remote(theirs, theirs, send.at[w], recv.at[w], (x, y, 1 - c)).wait_recv()
        for cp in cps:
            cp.wait_send()

    outs = pl.pallas_call(
        body, name=name,
        in_specs=[ANY] * nw, out_specs=[ANY] * nw,
        out_shape=[jax.ShapeDtypeStruct(h.shape, h.dtype) for h in halves],
        input_output_aliases={w: w for w in range(nw)},
        scratch_shapes=[pltpu.SemaphoreType.DMA((nw,))] * 2,
    )(*halves)
    return [o.reshape(2 * h.shape[1], h.shape[2]) for o, h in zip(outs, halves)]


def _adamw(w, g, m, v):
    m = ADAM_B1 * m + (1.0 - ADAM_B1) * g
    v = ADAM_B2 * v + (1.0 - ADAM_B2) * (g * g)
    m_hat = m / (1.0 - ADAM_B1 ** ADAM_STEP)
    v_hat = v / (1.0 - ADAM_B2 ** ADAM_STEP)
    delta = -ADAM_LR * (m_hat / (jnp.sqrt(v_hat) + ADAM_EPS) + ADAM_WD * w)
    return delta, m, v


def _adamw_call(ws, gs, ms, vs, nch, name):
    nw = len(ws)

    def body(*refs):
        for w in range(nw):
            g = refs[nw + w][...]
            delta, m, v = _adamw(refs[w][...], g, refs[2 * nw + w][...], refs[3 * nw + w][...])
            refs[4 * nw + w][...] = g
            refs[5 * nw + w][...] = delta
            refs[6 * nw + w][...] = m
            refs[7 * nw + w][...] = v

    specs = [pl.BlockSpec((a.shape[0] // nch, a.shape[1]), lambda i: (i, 0)) for a in ws]
    res = pl.pallas_call(
        body, name=name,
        grid=(nch,),
        in_specs=specs * 4, out_specs=specs * 4,
        out_shape=[jax.ShapeDtypeStruct(a.shape, F32) for a in ws] * 4,
        compiler_params=_params(("arbitrary",)),
    )(*ws, *gs, *ms, *vs)
    return res[:nw], res[nw:2 * nw], res[2 * nw:3 * nw], res[3 * nw:]


def _small_call(gathered, own, me_idx, w, m, v):
    def fold(row):
        tot = row[:, 0:LANES] + row[:, LANES:2 * LANES] + row[:, 2 * LANES:3 * LANES] + row[:, 3 * LANES:4 * LANES]
        return tot + pltpu.roll(tot, HEAD_DIM, axis=1)

    def body(me_ref, ga_ref, own_ref, w_ref, m_ref, v_ref, g_out, d_out, m_out, v_out):
        me = me_ref[0]
        term = lambda i: jnp.where(me == i, own_ref[...], ga_ref[i])
        g = term(0)
        for i in range(1, N_DEV):
            g = g + term(i)
        unfolded = g[4:5, :]
        folded = jnp.concatenate([fold(unfolded[:, :ATTN_WIDTH]), fold(unfolded[:, ATTN_WIDTH:]),
                                  jnp.zeros((1, 1024 - 2 * LANES), F32)], axis=-1)
        row = lax.broadcasted_iota(jnp.int32, g.shape, 0)
        g = jnp.where(row == 3, folded, g)
        delta, mm, vv = _adamw(w_ref[...], g, m_ref[...], v_ref[...])
        g_out[...] = g
        d_out[...] = delta
        m_out[...] = mm
        v_out[...] = vv

    vmem = pl.BlockSpec(memory_space=pltpu.VMEM)
    return pl.pallas_call(
        body, name="adamw_small",
        in_specs=[pl.BlockSpec(memory_space=pltpu.SMEM)] + [vmem] * 5,
        out_shape=[jax.ShapeDtypeStruct(w.shape, F32)] * 4,
        compiler_params=_params(),
    )(me_idx, gathered, own, w, m, v)


def _pack_small(p, folded=True, loss=None):
    z = lambda n: jnp.zeros((n,), F32)
    rows = [p["mix_norm_g"], p["mlp_norm_g"],
            jnp.concatenate([p["pool_scale"], p["rel_bias"].reshape(-1), z(1024 - POOL_WIDTH - N_BUCKETS * N_HEADS)])]
    if folded:
        rows += [jnp.concatenate([p["q_norm_g"], z(LANES - HEAD_DIM), p["k_norm_g"], z(1024 - LANES - HEAD_DIM)]), z(1024)]
    else:
        rows += [z(1024), jnp.concatenate([p["q_norm_g"], p["k_norm_g"]])]
    rows += [z(1024) if loss is None else jnp.concatenate([loss.reshape(1), z(1023)])]
    head = jnp.stack(rows + [z(1024)] * 2)
    return jnp.concatenate([head, p["pool_w"].reshape(-1, 1024)], axis=0)


def _unpack_small(a):
    return dict(
        mix_norm_g=a[0], mlp_norm_g=a[1], pool_scale=a[2, :POOL_WIDTH],
        rel_bias=a[2, POOL_WIDTH:POOL_WIDTH + N_BUCKETS * N_HEADS].reshape(N_BUCKETS, N_HEADS),
        q_norm_g=a[3, :HEAD_DIM], k_norm_g=a[3, LANES:LANES + HEAD_DIM],
        pool_w=a[8:].reshape(len(POOL_WINDOWS), LANES, LANES))


_WEIGHT_ORDER = ("mix_norm_g", "w_in", "pool_w", "pool_scale", "q_norm_g", "k_norm_g", "rel_bias", "w_out",
                 "mlp_norm_g", "w_up", "w_down")
_BIG = ("w_in", "w_out", "w_up", "w_down")


def kernel(x, mix_norm_g, w_in, pool_w, pool_scale, q_norm_g, k_norm_g, rel_bias, w_out, mlp_norm_g, w_up, w_down, loss_target, m_mix_norm_g, m_w_in, m_pool_w, m_pool_scale, m_q_norm_g, m_k_norm_g, m_rel_bias, m_w_out, m_mlp_norm_g, m_w_up, m_w_down, v_mix_norm_g, v_w_in, v_pool_w, v_pool_scale, v_q_norm_g, v_k_norm_g, v_rel_bias, v_w_out, v_mlp_norm_g, v_w_up, v_w_down):
    w = dict(mix_norm_g=mix_norm_g, w_in=w_in, pool_w=pool_w, pool_scale=pool_scale, q_norm_g=q_norm_g,
             k_norm_g=k_norm_g, rel_bias=rel_bias, w_out=w_out, mlp_norm_g=mlp_norm_g, w_up=w_up, w_down=w_down)
    m = dict(mix_norm_g=m_mix_norm_g, w_in=m_w_in, pool_w=m_pool_w, pool_scale=m_pool_scale, q_norm_g=m_q_norm_g,
             k_norm_g=m_k_norm_g, rel_bias=m_rel_bias, w_out=m_w_out, mlp_norm_g=m_mlp_norm_g, w_up=m_w_up, w_down=m_w_down)
    v = dict(mix_norm_g=v_mix_norm_g, w_in=v_w_in, pool_w=v_pool_w, pool_scale=v_pool_scale, q_norm_g=v_q_norm_g,
             k_norm_g=v_k_norm_g, rel_bias=v_rel_bias, w_out=v_w_out, mlp_norm_g=v_mlp_norm_g, w_up=v_w_up, w_down=v_w_down)
    xc, yc, cc = _coords()

    c_idx = jnp.reshape(cc, (1,)).astype(jnp.int32)
    chip_idx = jnp.reshape(2 * xc + yc, (1,)).astype(jnp.int32)
    me = 4 * xc + 2 * yc + cc
    whole = lambda t: t.reshape(t.shape[0], t.shape[1] * t.shape[2], t.shape[3])

    placed = [_halves(p) for p in _place_shards_call([w[n] for n in _BIG], chip_idx, nch=4)]
    (win_f,) = _allgather_call(placed[:1], from_chips=True, name="weights_allgather_in")
    wsend, wrecv, in_flight, started = _gather_start_call(placed[1:], win_f)

    def mlp_weights(after):
        landed = _gather_wait_call(in_flight, wsend, wrecv, after)
        wout_f, wup_f, wdown_f = _allgather_call(landed, from_chips=False, name="weights_pair_forward")
        return whole(wout_f).reshape(-1, wout_f.shape[-1]), whole(wup_f), whole(wdown_f)

    split = []

    def on_mlp_grads(*wire_grads):
        srcs = [_halves(g) for g in wire_grads]
        lands = [lax.empty((N_DEV,) + s.shape[2:], s.dtype) for s in srcs]
        split.extend(_scatter_start_call(srcs, lands, [False] * len(srcs), "grads_scatter_start"))
        return split[4]

    loss_part, dx, big_grads, small_grads = _local_grads(
        x[0], loss_target[0], mix_norm_g + started[0, 0], whole(win_f), pool_w, pool_scale, q_norm_g, k_norm_g, rel_bias,
        mlp_norm_g, mlp_weights, on_mlp_grads)
    g_in, g_out, g_up, g_down = big_grads
    gsend, grecv, srcs_thru, lands_thru, _ = split
    lands_mlp = _scatter_wait_call(srcs_thru, lands_thru, gsend, grecv, g_in[1], [False] * 3, "grads_scatter_wait")

    small_own = _pack_small(small_grads, folded=False, loss=loss_part)
    last_srcs = [_halves(g_in[1]), small_own]
    last_lands = [lax.empty((N_DEV,) + last_srcs[0].shape[2:], WIRE_DTYPE), lax.empty((N_DEV,) + small_own.shape, F32)]
    lsend, lrecv, last_srcs, last_lands, last_started = _scatter_start_call(
        last_srcs, last_lands, [False, True], "grads_scatter_start_last")
    idx = jnp.concatenate([chip_idx, c_idx] + [jnp.reshape(jnp.bitwise_xor(me, r), (1,)) for r in range(1, N_DEV)])
    idx = idx.astype(jnp.int32)
    mlp = _BIG[1:]

    def update(names, own32, lands, tag, dep=None):
        halves = _reduce_call([_halves(g) for g in own32], lands, idx, 4, "grads_reduce_" + tag, dep)
        reduced = _pair_allgather_call(list(halves), "grads_pair_allgather_" + tag)
        return _adamw_call([w[n] for n in names], reduced, [m[n] for n in names], [v[n] for n in names], 8, "adamw_" + tag)

    out_mlp = update(mlp, [g_out[0], g_up[0], g_down[0]], lands_mlp, "mlp", last_started)
    land_in, small_all = _scatter_wait_call(last_srcs, last_lands, lsend, lrecv, out_mlp[3][-1], [False, True],
                                            "grads_scatter_wait_last")
    out_in = update(_BIG[:1], [g_in[0]], [land_in], "in")
    g_pack, d_pack, m_pack, v_pack = _small_call(
        small_all, small_own, jnp.reshape(me, (1,)).astype(jnp.int32), _pack_small(w), _pack_small(m), _pack_small(v))

    grads, deltas, new_m, new_v = (_unpack_small(a) for a in (g_pack, d_pack, m_pack, v_pack))
    for k, res in enumerate((grads, deltas, new_m, new_v)):
        res[_BIG[0]] = out_in[k][0]
        for i, n in enumerate(mlp):
            res[n] = out_mlp[k][i]
    loss = g_pack[LOSS_ROW, 0]
    return (loss, dx[None], *[grads[n] for n in _WEIGHT_ORDER], *[deltas[n] for n in _WEIGHT_ORDER],
            *[new_m[n] for n in _WEIGHT_ORDER], *[new_v[n] for n in _WEIGHT_ORDER])
```

```python
import math

import jax
import jax.numpy as jnp
import numpy as np
from jax import lax
from jax.experimental import pallas as pl
from jax.experimental.pallas import tpu as pltpu

F32 = jnp.float32
MXU_DTYPE = jnp.bfloat16
WIRE_DTYPE = jnp.bfloat16

NORM_EPS = 1e-6
NEG_INF = -1e30
LANES = 128
HEAD_DIM = 64
N_HEADS = 8
POOL_WIDTH = 512
ATTN_WIDTH = 512
POOL_WINDOWS = (2, 4, 8, 16)
POOL_HALO = 16
DILATED_PATTERNS = ((128, 1), (512, 4), (2048, 16))
ATT_BLOCK = 128
ATT_SUPER = ATT_BLOCK * max(dl for _, dl in DILATED_PATTERNS)
ATT_UNITS = ATT_SUPER // ATT_BLOCK
N_BUCKETS = 32
NO_BUCKET = -1
MAX_DISTANCE = 2048
N_CHIPS = 4
N_DEV = 8
ADAM_LR, ADAM_B1, ADAM_B2, ADAM_EPS, ADAM_WD, ADAM_STEP = 0.001, 0.9, 0.999, 1e-08, 0.01, 10
VMEM_LIMIT = 56 * 1024 * 1024
MESH = pl.DeviceIdType.MESH
ANY = pl.BlockSpec(memory_space=pl.ANY)

SMALL_ROWS = 72
LOSS_ROW = 5


def _mm(a, b):
    return jnp.dot(a, b, preferred_element_type=F32)


def _mm_nt(a, b):
    return lax.dot_general(a, b, (((1,), (1,)), ((), ())), preferred_element_type=F32)


def _mm_tn(a, b):
    return lax.dot_general(a, b, (((0,), (0,)), ((), ())), preferred_element_type=F32)


def _params(sem=None, **kw):
    if sem is not None:
        kw["dimension_semantics"] = sem
    return pltpu.CompilerParams(vmem_limit_bytes=VMEM_LIMIT, **kw)


def _low_half():
    return lax.broadcasted_iota(jnp.int32, (1, LANES), 1) < HEAD_DIM


def _head_sum_bcast(y):
    lo = _low_half()
    outs = []
    for j in range(y.shape[1] // LANES):
        c = y[:, j * LANES:(j + 1) * LANES]
        s_lo = jnp.sum(jnp.where(lo, c, 0.0), axis=-1, keepdims=True)
        s_hi = jnp.sum(jnp.where(lo, 0.0, c), axis=-1, keepdims=True)
        outs.append(jnp.where(lo, s_lo, s_hi))
    return jnp.concatenate(outs, axis=-1)


def _rms_bwd(dn, hn, r):
    return r * (dn - hn * jnp.mean(dn * hn, axis=-1, keepdims=True))


def _t5_bucket_np(dist):
    max_exact = N_BUCKETS // 2
    d_f = np.maximum(dist, 1).astype(np.float32)
    ratio = (np.log(d_f / np.float32(max_exact)) / np.float32(math.log(MAX_DISTANCE / max_exact))).astype(np.float32)
    large = max_exact + (ratio * np.float32(N_BUCKETS - max_exact)).astype(np.int32)
    large = np.minimum(large, N_BUCKETS - 1)
    return np.where(dist < max_exact, dist, large).astype(np.int32)


def _window_offsets(dl):
    if dl == 1:
        return _by4_positions(ATT_BLOCK), _by4_positions(2 * ATT_BLOCK)
    return np.arange(ATT_BLOCK), np.arange(2 * ATT_BLOCK)


def _bucket_tables():
    tables = []
    for _, dl in DILATED_PATTERNS:
        qq, kk = _window_offsets(dl)
        dist = qq[:, None] + ATT_BLOCK - kk[None, :]
        bucket = _t5_bucket_np(np.clip(dist, 0, ATT_BLOCK) * dl)
        tables.append(np.where((dist >= 0) & (dist <= ATT_BLOCK), bucket, NO_BUCKET))
    return np.stack(tables).astype(np.int32)


def _previous_block_keys():
    return np.stack([np.broadcast_to(_window_offsets(dl)[1][None, :] < ATT_BLOCK, (ATT_BLOCK, 2 * ATT_BLOCK))
                     for _, dl in DILATED_PATTERNS])


def _f1_call(x, g1, win, poolw, pscale, qg, kg, tm):
    s, d = x.shape
    nblk = s // tm

    def body(x_ref, g1_ref, win_ref, pw_ref, ps_ref, qg_ref, kg_ref,
             a_ref, pooled_ref, ypool_ref, q32_ref, k32_ref, qn_ref, kn_ref, v_ref, ubuf):
        i = pl.program_id(0)
        xv = x_ref[...]
        r = lax.rsqrt(jnp.mean(xv * xv, axis=-1, keepdims=True) + NORM_EPS)
        a = ((xv * r) * g1_ref[...]).astype(MXU_DTYPE)
        a_ref[...] = a
        u = _mm(a, win_ref[0])
        q = _mm(a, win_ref[1])
        k = _mm(a, win_ref[2])
        v_ref[...] = _mm(a, win_ref[3]).astype(MXU_DTYPE)
        q32_ref[...] = q
        k32_ref[...] = k
        rq = lax.rsqrt(_head_sum_bcast(q * q) * (1.0 / HEAD_DIM) + NORM_EPS)
        qn_ref[...] = (((q * rq) * qg_ref[...]) * (HEAD_DIM ** -0.5)).astype(MXU_DTYPE)
        rk = lax.rsqrt(_head_sum_bcast(k * k) * (1.0 / HEAD_DIM) + NORM_EPS)
        kn_ref[...] = ((k * rk) * kg_ref[...]).astype(MXU_DTYPE)

        @pl.when(i == 0)
        def _():
            ubuf[0:POOL_HALO, :] = jnp.zeros((POOL_HALO, POOL_WIDTH), F32)

        @pl.when(i > 0)
        def _():
            ubuf[0:POOL_HALO, :] = ubuf[tm:tm + POOL_HALO, :]

        ubuf[POOL_HALO:POOL_HALO + tm, :] = u
        t = i * tm + lax.broadcasted_iota(jnp.int32, (tm, 1), 0)
        for g, w in enumerate(POOL_WINDOWS):
            ls = slice(g * LANES, (g + 1) * LANES)
            ug = u[:, ls]
            acc = ug
            for sh in range(1, w):
                acc = acc + ubuf[POOL_HALO - sh:POOL_HALO - sh + tm, ls]
            cnt = jnp.minimum(t + 1, w).astype(F32)
            pooled = (acc / cnt - ug).astype(MXU_DTYPE)
            pooled_ref[:, ls] = pooled
            ypool_ref[:, ls] = (_mm(pooled, pw_ref[g]) * ps_ref[:, ls]).astype(MXU_DTYPE)

    tok = lambda w: pl.BlockSpec((tm, w), lambda i: (i, 0))
    full = lambda shp: pl.BlockSpec(shp, lambda i: (0,) * len(shp))
    return pl.pallas_call(
        body, name="fwd_inproj",
        grid=(nblk,),
        in_specs=[tok(d), full((1, d)), full(win.shape), full(poolw.shape), full((1, POOL_WIDTH)),
                  full((1, ATTN_WIDTH)), full((1, ATTN_WIDTH))],
        out_specs=[tok(d), tok(POOL_WIDTH), tok(POOL_WIDTH), tok(ATTN_WIDTH), tok(ATTN_WIDTH),
                   tok(ATTN_WIDTH), tok(ATTN_WIDTH), tok(ATTN_WIDTH)],
        out_shape=[jax.ShapeDtypeStruct((s, d), MXU_DTYPE),
                   jax.ShapeDtypeStruct((s, POOL_WIDTH), MXU_DTYPE),
                   jax.ShapeDtypeStruct((s, POOL_WIDTH), MXU_DTYPE),
                   jax.ShapeDtypeStruct((s, ATTN_WIDTH), F32),
                   jax.ShapeDtypeStruct((s, ATTN_WIDTH), F32),
                   jax.ShapeDtypeStruct((s, ATTN_WIDTH), MXU_DTYPE),
                   jax.ShapeDtypeStruct((s, ATTN_WIDTH), MXU_DTYPE),
                   jax.ShapeDtypeStruct((s, ATTN_WIDTH), MXU_DTYPE)],
        scratch_shapes=[pltpu.VMEM((tm + POOL_HALO, POOL_WIDTH), F32)],
        compiler_params=_params(("arbitrary",)),
    )(x, g1, win, poolw, pscale, qg, kg)


DEINT = 4
assert [dl for _, dl in DILATED_PATTERNS] == [1, DEINT, DEINT * DEINT]


def _by4_positions(n):
    pos = np.arange(n)
    return DEINT * (pos % (n // DEINT)) + pos // (n // DEINT)


def _masked_bias(b_ref, p, n):
    return b_ref[p, jnp.minimum(n, 1)].reshape(2 * ATT_BLOCK, 2 * ATT_BLOCK)


def _unit_rows(u, dl):
    sq, sk = ATT_SUPER // DEINT, 2 * ATT_SUPER // DEINT
    if dl == 1:
        n = ATT_BLOCK // DEINT
        return (u, [pl.ds(pl.multiple_of(r * sq + n * u, 8), n) for r in range(DEINT)],
                [pl.ds(pl.multiple_of(r * sk + sk // 2 + n * (u - 1), 8), 2 * n) for r in range(DEINT)])
    if dl == DEINT:
        r, b = u % DEINT, u // DEINT
        return (b, [pl.ds(pl.multiple_of(r * sq + ATT_BLOCK * b, 8), ATT_BLOCK)],
                [pl.ds(pl.multiple_of(r * sk + sk // 2 + ATT_BLOCK * (b - 1), 8), 2 * ATT_BLOCK)])
    r, a = u % DEINT, u // DEINT
    return 0, [pl.ds(r * sq + a, ATT_BLOCK, stride=DEINT)], [pl.ds(r * sk + a, 2 * ATT_BLOCK, stride=DEINT)]


def _take(ref, runs):
    parts = [ref[run, :] for run in runs]
    return parts[0] if len(parts) == 1 else jnp.concatenate(parts, axis=0)


def _put(ref, runs, value, add=False):
    n = value.shape[0] // len(runs)
    for i, run in enumerate(runs):
        part = value[i * n:(i + 1) * n]
        ref[run, :] = ref[run, :] + part if add else part


def _deinterleave(dst, src, n):
    seg = n // DEINT
    for r in range(DEINT):
        dst[r * seg:(r + 1) * seg, :] = src[pl.ds(r, seg, stride=DEINT), :]


def _interleave(dst, src, n, offset=0):
    seg = n // DEINT
    stride = src.shape[0] // DEINT
    for r in range(DEINT):
        dst[pl.ds(r, seg, stride=DEINT), :] = src[r * stride + offset:r * stride + offset + seg, :]


def _attn_fwd_call(qn, kn, v, bias):
    s, w = qn.shape
    nsb = s // ATT_SUPER
    npair = w // LANES

    def body(q_ref, kc_ref, kp_ref, vc_ref, vp_ref, b_ref, o_ref, lse_ref, tmp, qf, kf, vf, acc_s, m_s, l_s):
        sb = pl.program_id(1)
        tmp[0:ATT_SUPER, :] = q_ref[...].astype(F32)
        _deinterleave(qf, tmp, ATT_SUPER)
        tmp[0:ATT_SUPER, :] = kp_ref[...].astype(F32)
        tmp[ATT_SUPER:, :] = kc_ref[...].astype(F32)
        _deinterleave(kf, tmp, 2 * ATT_SUPER)
        tmp[0:ATT_SUPER, :] = vp_ref[...].astype(F32)
        tmp[ATT_SUPER:, :] = vc_ref[...].astype(F32)
        _deinterleave(vf, tmp, 2 * ATT_SUPER)
        lo = _low_half()
        for p, (_, dl) in enumerate(DILATED_PATTERNS):
            def two_units(g, carry, p=p, dl=dl):
                rows, q2, kcat, vcat, bias2 = [], [], [], [], []
                for u in (2 * g, 2 * g + 1):
                    b, rows_q, rows_k = _unit_rows(u, dl)
                    qp = _take(qf, rows_q).astype(MXU_DTYPE)
                    zero = jnp.zeros_like(qp)
                    q2.append(jnp.concatenate([jnp.where(lo, qp, zero), jnp.where(lo, zero, qp)], axis=0))
                    kcat.append(_take(kf, rows_k).astype(MXU_DTYPE))
                    vcat.append(_take(vf, rows_k).astype(MXU_DTYPE))
                    bias2.append(_masked_bias(b_ref, p, sb * (ATT_UNITS // dl) + b))
                    rows.append(rows_q)
                nokey = jnp.zeros_like(kcat[0])
                keys = jnp.concatenate([jnp.concatenate([kcat[0], nokey], axis=1),
                                        jnp.concatenate([nokey, kcat[1]], axis=1)], axis=0)
                vals = jnp.concatenate([jnp.concatenate([vcat[0], nokey], axis=1),
                                        jnp.concatenate([nokey, vcat[1]], axis=1)], axis=0)
                sc_ab = _mm_nt(jnp.concatenate(q2, axis=1), keys)
                m2, l2, pr = [], [], []
                for i in range(2):
                    sc = sc_ab[:, i * 2 * ATT_BLOCK:(i + 1) * 2 * ATT_BLOCK] + bias2[i]
                    m2.append(jnp.max(sc, axis=-1, keepdims=True))
                    e = jnp.exp(sc - m2[i])
                    l2.append(jnp.sum(e, axis=-1, keepdims=True))
                    pr.append(e.astype(MXU_DTYPE))
                acc_ab = _mm(jnp.concatenate(pr, axis=1), vals)
                for i in range(2):
                    acc2 = acc_ab[:, i * LANES:(i + 1) * LANES]
                    acc = jnp.where(lo, acc2[:ATT_BLOCK], acc2[ATT_BLOCK:])
                    m = jnp.where(lo, m2[i][:ATT_BLOCK], m2[i][ATT_BLOCK:])
                    l = jnp.where(lo, l2[i][:ATT_BLOCK], l2[i][ATT_BLOCK:])
                    if p == 0:
                        _put(acc_s, rows[i], acc)
                        _put(m_s, rows[i], m)
                        _put(l_s, rows[i], l)
                    else:
                        m_old = _take(m_s, rows[i])
                        m_new = jnp.maximum(m_old, m)
                        a_old = jnp.exp(m_old - m_new)
                        a_new = jnp.exp(m - m_new)
                        _put(acc_s, rows[i], a_old * _take(acc_s, rows[i]) + a_new * acc)
                        _put(l_s, rows[i], a_old * _take(l_s, rows[i]) + a_new * l)
                        _put(m_s, rows[i], m_new)
                return carry

            lax.fori_loop(0, ATT_UNITS // 2, two_units, 0, unroll=2)
        l = l_s[...]
        acc_s[...] = acc_s[...] / l
        m_s[...] = m_s[...] + jnp.log(l)
        _interleave(o_ref, acc_s, ATT_SUPER)
        _interleave(lse_ref, m_s, ATT_SUPER)

    cur = pl.BlockSpec((ATT_SUPER, LANES), lambda j, t: (t, j))
    prev = pl.BlockSpec((ATT_SUPER, LANES), lambda j, t: (jnp.maximum(t - 1, 0), j))
    bspec = pl.BlockSpec((len(DILATED_PATTERNS), 2, 2, ATT_BLOCK, 2 * ATT_BLOCK), lambda j, t: (0, 0, j, 0, 0))
    return pl.pallas_call(
        body, name="attn_fwd",
        grid=(npair, nsb),
        in_specs=[cur, cur, prev, cur, prev, bspec],
        out_specs=[cur, cur],
        out_shape=[jax.ShapeDtypeStruct((s, w), F32), jax.ShapeDtypeStruct((s, w), F32)],
        scratch_shapes=[pltpu.VMEM((2 * ATT_SUPER, LANES), F32),
                        pltpu.VMEM((ATT_SUPER, LANES), F32), pltpu.VMEM((2 * ATT_SUPER, LANES), F32),
                        pltpu.VMEM((2 * ATT_SUPER, LANES), F32), pltpu.VMEM((ATT_SUPER, LANES), F32),
                        pltpu.VMEM((ATT_SUPER, LANES), F32), pltpu.VMEM((ATT_SUPER, LANES), F32)],
        compiler_params=_params(("arbitrary", "arbitrary")),
    )(qn, kn, kn, v, v, bias)


def _attn_bwd_call(qn, kn, v, do, lse, delta, bias, dep=None):
    s, w = qn.shape
    nsb = s // ATT_SUPER
    npair = w // LANES
    deps = [] if dep is None else [dep]

    def body(q_ref, kc_ref, kp_ref, vc_ref, vp_ref, do_ref, lse_ref, dlt_ref, b_ref, *rest):
        dq_ref, dk_ref, dv_ref, db_ref, tmp, qf, kf, vf, dof, lsef, dltf, dqf, dkf, dvf = rest[len(deps):]
        step = pl.program_id(1)
        sb = nsb - 1 - step
        seg = ATT_SUPER // DEINT
        tmp[0:ATT_SUPER, :] = q_ref[...].astype(F32)
        _deinterleave(qf, tmp, ATT_SUPER)
        tmp[0:ATT_SUPER, :] = do_ref[...].astype(F32)
        _deinterleave(dof, tmp, ATT_SUPER)
        tmp[0:ATT_SUPER, :] = kp_ref[...].astype(F32)
        tmp[ATT_SUPER:, :] = kc_ref[...].astype(F32)
        _deinterleave(kf, tmp, 2 * ATT_SUPER)
        tmp[0:ATT_SUPER, :] = vp_ref[...].astype(F32)
        tmp[ATT_SUPER:, :] = vc_ref[...].astype(F32)
        _deinterleave(vf, tmp, 2 * ATT_SUPER)
        _deinterleave(lsef, lse_ref, ATT_SUPER)
        _deinterleave(dltf, dlt_ref, ATT_SUPER)

        @pl.when(step == 0)
        def _():
            db_ref[...] = jnp.zeros(db_ref.shape, F32)

        for acc in (dkf, dvf):
            for r in range(DEINT):
                this, before = pl.ds((2 * r + 1) * seg, seg), pl.ds(2 * r * seg, seg)

                @pl.when(step == 0)
                def _(acc=acc, this=this):
                    acc[this, :] = jnp.zeros((seg, LANES), F32)

                @pl.when(step > 0)
                def _(acc=acc, this=this, before=before):
                    acc[this, :] = acc[before, :]

                acc[before, :] = jnp.zeros((seg, LANES), F32)
        lo = _low_half()
        for p, (_, dl) in enumerate(DILATED_PATTERNS):
            def unit(u, carry, p=p, dl=dl):
                b, rows_q, rows_k = _unit_rows(u, dl)
                qp = _take(qf, rows_q).astype(MXU_DTYPE)
                dop = _take(dof, rows_q).astype(MXU_DTYPE)
                kcat = _take(kf, rows_k).astype(MXU_DTYPE)
                vcat = _take(vf, rows_k).astype(MXU_DTYPE)
                lse2 = _take(lsef, rows_q)
                dlt2 = _take(dltf, rows_q)
                zero = jnp.zeros_like(qp)
                q2 = jnp.concatenate([jnp.where(lo, qp, zero), jnp.where(lo, zero, qp)], axis=0)
                do2 = jnp.concatenate([jnp.where(lo, dop, zero), jnp.where(lo, zero, dop)], axis=0)
                lse_c = jnp.concatenate([lse2[:, 0:1], lse2[:, HEAD_DIM:HEAD_DIM + 1]], axis=0)
                dlt_c = jnp.concatenate([dlt2[:, 0:1], dlt2[:, HEAD_DIM:HEAD_DIM + 1]], axis=0)
                sc = _mm_nt(q2, kcat) + _masked_bias(b_ref, p, sb * (ATT_UNITS // dl) + b)
                pr = jnp.exp(sc - lse_c)
                ds = pr * (_mm_nt(do2, vcat) - dlt_c)
                db_ref[p] += ds.reshape(2, ATT_BLOCK, 2 * ATT_BLOCK)
                ds_c = ds.astype(MXU_DTYPE)
                dq2 = _mm(ds_c, kcat)
                dk = _mm_tn(ds_c, q2)
                dv = _mm_tn(pr.astype(MXU_DTYPE), do2)
                dq = jnp.where(lo, dq2[:ATT_BLOCK], dq2[ATT_BLOCK:])
                _put(dqf, rows_q, dq, add=p > 0)
                _put(dkf, rows_k, dk, add=True)
                _put(dvf, rows_k, dv, add=True)
                return carry

            lax.fori_loop(0, ATT_UNITS, unit, 0, unroll=4)
        _interleave(dq_ref, dqf, ATT_SUPER)
        _interleave(dk_ref, dkf, ATT_SUPER, offset=seg)
        _interleave(dv_ref, dvf, ATT_SUPER, offset=seg)

    cur = pl.BlockSpec((ATT_SUPER, LANES), lambda j, t: (nsb - 1 - t, j))
    prev = pl.BlockSpec((ATT_SUPER, LANES), lambda j, t: (jnp.maximum(nsb - 2 - t, 0), j))
    npat = len(DILATED_PATTERNS)
    bspec = pl.BlockSpec((npat, 2, 2, ATT_BLOCK, 2 * ATT_BLOCK), lambda j, t: (0, 0, j, 0, 0))
    dbspec = pl.BlockSpec((npat, 2, ATT_BLOCK, 2 * ATT_BLOCK), lambda j, t: (0, j, 0, 0))
    sup = lambda: pltpu.VMEM((ATT_SUPER, LANES), F32)
    sup2 = lambda: pltpu.VMEM((2 * ATT_SUPER, LANES), F32)
    return pl.pallas_call(
        body, name="attn_bwd",
        grid=(npair, nsb),
        in_specs=[cur, cur, prev, cur, prev, cur, cur, cur, bspec] + [ANY] * len(deps),
        out_specs=[cur, cur, cur, dbspec],
        out_shape=[jax.ShapeDtypeStruct((s, w), F32)] * 3
        + [jax.ShapeDtypeStruct((npat, N_HEADS, ATT_BLOCK, 2 * ATT_BLOCK), F32)],
        scratch_shapes=[sup2(), sup(), sup2(), sup2(), sup(), sup(), sup(), sup(), sup2(), sup2()],
        compiler_params=_params(("arbitrary", "arbitrary")),
    )(qn, kn, kn, v, v, do, lse, delta, bias, *deps)


def _bias_table_call(rel_bias, buckets, prev_keys):
    npat = buckets.shape[0]

    def body(rb_ref, bk_ref, pk_ref, out_ref):
        for p in range(npat):
            for half in range(2):
                ks = slice(half * ATT_BLOCK, (half + 1) * ATT_BLOCK)
                bk = bk_ref[p, :, ks]
                absent = pk_ref[p, :, ks] != 0
                for h in range(N_HEADS):
                    def pick(b, acc, h=h, bk=bk):
                        return jnp.where(bk == b, rb_ref[b, h], acc)

                    tab = lax.fori_loop(0, N_BUCKETS, pick, jnp.full((ATT_BLOCK, ATT_BLOCK), NEG_INF, F32))
                    out_ref[p, 1, h, :, ks] = tab
                    out_ref[p, 0, h, :, ks] = jnp.where(absent, NEG_INF, tab)

    vmem = pl.BlockSpec(memory_space=pltpu.VMEM)
    return pl.pallas_call(
        body, name="bias_table",
        in_specs=[pl.BlockSpec(memory_space=pltpu.SMEM), vmem, vmem],
        out_shape=jax.ShapeDtypeStruct((npat, 2, N_HEADS, ATT_BLOCK, 2 * ATT_BLOCK), F32),
        compiler_params=_params(),
    )(rel_bias, buckets, prev_keys)


def _rel_bias_grad_call(dbias, buckets):
    npat, nh = dbias.shape[0], dbias.shape[1]

    def body(db_ref, bk_ref, out_ref):
        lane = lax.broadcasted_iota(jnp.int32, (nh, LANES), 1)
        out = jnp.zeros((nh, LANES), F32)
        for b in range(N_BUCKETS):
            tot = jnp.zeros((nh, 1), F32)
            for p in range(npat):
                hit = jnp.where(bk_ref[p][None] == b, db_ref[p], 0.0)
                tot = tot + jnp.sum(jnp.sum(hit, axis=2), axis=1, keepdims=True)
            out = jnp.where(lane == b, tot, out)
        out_ref[...] = out

    return pl.pallas_call(
        body, name="rel_bias_grad",
        out_shape=jax.ShapeDtypeStruct((nh, LANES), F32),
        compiler_params=_params(),
    )(dbias, buckets)


def _f2_call(x, tgt, ypool, o, wout, wup, wdown, g2, tm):
    s, d = x.shape
    nblk = s // tm
    nch, _, fch = wup.shape
    dff = nch * fch
    mixw = POOL_WIDTH + ATTN_WIDTH

    def body(x_ref, t_ref, yp_ref, o_ref, g2_ref, wout_hbm, wup_hbm, wdown_hbm,
             mixed_ref, c_ref, ff_ref, dz_ref, dy_ref, dh1_ref, dyp_ref, do_ref, dlt_ref, dg2_ref, loss_ref,
             wout_v, wup_v, wdown_v, rz):
        i = pl.program_id(0)

        @pl.when(i == 0)
        def _():
            pltpu.sync_copy(wout_hbm, wout_v)
            pltpu.sync_copy(wup_hbm, wup_v)
            pltpu.sync_copy(wdown_hbm, wdown_v)
            dg2_ref[...] = jnp.zeros(dg2_ref.shape, F32)
            loss_ref[...] = jnp.zeros(loss_ref.shape, F32)

        o = o_ref[...]
        mixed = jnp.concatenate([yp_ref[...], o.astype(MXU_DTYPE)], axis=-1)
        mixed_ref[...] = mixed
        h1 = x_ref[...] + _mm(mixed, wout_v[...])
        r2 = lax.rsqrt(jnp.mean(h1 * h1, axis=-1, keepdims=True) + NORM_EPS)
        hn = h1 * r2
        c = (hn * g2_ref[...]).astype(MXU_DTYPE)
        c_ref[...] = c
        y = h1
        for j in range(nch):
            cs = slice(j * fch, (j + 1) * fch)
            z = jnp.maximum(_mm(c, wup_v[j]), 0.0)
            rz[:, cs] = z
            ff = (z * z).astype(MXU_DTYPE)
            ff_ref[:, cs] = ff
            y = y + _mm(ff, wdown_v[j])
        err = y - t_ref[...]
        loss_ref[...] += jnp.sum(err * err) * (0.5 / d)
        dy = err * (1.0 / d)
        dy_c = dy.astype(MXU_DTYPE)
        dy_ref[...] = dy_c
        dc = jnp.zeros((tm, d), F32)
        for j in range(nch):
            cs = slice(j * fch, (j + 1) * fch)
            dz = (_mm_nt(dy_c, wdown_v[j]) * (2.0 * rz[:, cs])).astype(MXU_DTYPE)
            dz_ref[:, cs] = dz
            dc = dc + _mm_nt(dz, wup_v[j])
        dg2_ref[...] += jnp.sum(dc * hn, axis=0, keepdims=True)
        dh1 = dy + _rms_bwd(dc * g2_ref[...], hn, r2)
        dh1_ref[...] = dh1
        dmix = _mm_nt(dh1.astype(MXU_DTYPE), wout_v[...])
        dyp_ref[...] = dmix[:, :POOL_WIDTH]
        do = dmix[:, POOL_WIDTH:]
        do_ref[...] = do.astype(MXU_DTYPE)
        dlt_ref[...] = _head_sum_bcast(do * o)

    tok = lambda w: pl.BlockSpec((tm, w), lambda i: (i, 0))
    const = lambda shp: pl.BlockSpec(shp, lambda i: (0,) * len(shp))
    return pl.pallas_call(
        body, name="fwd_mlp_bwd_mlp",
        grid=(nblk,),
        in_specs=[tok(d), tok(d), tok(POOL_WIDTH), tok(ATTN_WIDTH), const((1, d)), ANY, ANY, ANY],
        out_specs=[tok(mixw), tok(d), tok(dff), tok(dff), tok(d), tok(d), tok(POOL_WIDTH), tok(ATTN_WIDTH),
                   tok(ATTN_WIDTH), const((1, d)), const((1, LANES))],
        out_shape=[jax.ShapeDtypeStruct((s, mixw), MXU_DTYPE),
                   jax.ShapeDtypeStruct((s, d), MXU_DTYPE),
                   jax.ShapeDtypeStruct((s, dff), MXU_DTYPE),
                   jax.ShapeDtypeStruct((s, dff), MXU_DTYPE),
                   jax.ShapeDtypeStruct((s, d), MXU_DTYPE),
                   jax.ShapeDtypeStruct((s, d), F32),
                   jax.ShapeDtypeStruct((s, POOL_WIDTH), F32),
                   jax.ShapeDtypeStruct((s, ATTN_WIDTH), MXU_DTYPE),
                   jax.ShapeDtypeStruct((s, ATTN_WIDTH), F32),
                   jax.ShapeDtypeStruct((1, d), F32),
                   jax.ShapeDtypeStruct((1, LANES), F32)],
        scratch_shapes=[pltpu.VMEM(wout.shape, MXU_DTYPE), pltpu.VMEM(wup.shape, MXU_DTYPE),
                        pltpu.VMEM(wdown.shape, MXU_DTYPE), pltpu.VMEM((tm, dff), F32)],
        compiler_params=_params(("arbitrary",)),
    )(x, tgt, ypool, o, g2, wout, wup, wdown)


def _bproj_call(dqn, dkn, dv, q32, k32, dypool, pooled, x, dh1, win, poolw, pscale, qg, kg, g1, tm):
    s, d = x.shape
    nblk = s // tm
    ngrp = len(POOL_WINDOWS)

    def body(dqn_ref, dkn_ref, dv_ref, q_ref, k_ref, dyp_ref, pooled_ref, x_ref, dh1_ref,
             win_hbm, pw_ref, ps_ref, qg_ref, kg_ref, g1_ref,
             dx_ref, dproj_ref, dg1_ref, dqg_ref, dkg_ref, dpw_ref, dps_ref, win_v, ebuf):
        step = pl.program_id(0)
        i = nblk - 1 - step

        @pl.when(step == 0)
        def _():
            pltpu.sync_copy(win_hbm, win_v)
            dg1_ref[...] = jnp.zeros(dg1_ref.shape, F32)
            dqg_ref[...] = jnp.zeros(dqg_ref.shape, F32)
            dkg_ref[...] = jnp.zeros(dkg_ref.shape, F32)
            dpw_ref[...] = jnp.zeros(dpw_ref.shape, F32)
            dps_ref[...] = jnp.zeros(dps_ref.shape, F32)
            ebuf[tm:tm + POOL_HALO, :] = jnp.zeros((POOL_HALO, POOL_WIDTH), F32)

        @pl.when(step > 0)
        def _():
            ebuf[tm:tm + POOL_HALO, :] = ebuf[0:POOL_HALO, :]

        def qk_bwd(dn_sum, raw, gain, scale, dgain_ref):
            rr = lax.rsqrt(_head_sum_bcast(raw * raw) * (1.0 / HEAD_DIM) + NORM_EPS)
            hn = raw * rr
            dgain_ref[...] += jnp.sum(dn_sum * hn, axis=0, keepdims=True) * scale
            dn = dn_sum * (gain * scale)
            return rr * (dn - hn * (_head_sum_bcast(dn * hn) * (1.0 / HEAD_DIM)))

        dq = qk_bwd(dqn_ref[...], q_ref[...], qg_ref[...], HEAD_DIM ** -0.5, dqg_ref)
        dk = qk_bwd(dkn_ref[...], k_ref[...], kg_ref[...], 1.0, dkg_ref)

        t = i * tm + lax.broadcasted_iota(jnp.int32, (tm, 1), 0)
        dpooled = []
        for g, w in enumerate(POOL_WINDOWS):
            ls = slice(g * LANES, (g + 1) * LANES)
            dm = dyp_ref[:, ls]
            pg = pooled_ref[:, ls]
            dps_ref[:, ls] += jnp.sum(dm * _mm(pg, pw_ref[g]), axis=0, keepdims=True)
            dms = (dm * ps_ref[:, ls]).astype(MXU_DTYPE)
            dpw_ref[g] += _mm_tn(pg, dms)
            dpg = _mm_nt(dms, pw_ref[g])
            dpooled.append(dpg)
            ebuf[0:tm, ls] = dpg / jnp.minimum(t + 1, w).astype(F32)
        du = []
        for g, w in enumerate(POOL_WINDOWS):
            ls = slice(g * LANES, (g + 1) * LANES)
            acc = ebuf[0:tm, ls]
            for sh in range(1, w):
                acc = acc + ebuf[sh:sh + tm, ls]
            du.append(acc - dpooled[g])
        parts = [jnp.concatenate(du, axis=-1), dq, dk, dv_ref[...]]
        da = jnp.zeros((tm, d), F32)
        for p, part in enumerate(parts):
            pc = part.astype(MXU_DTYPE)
            dproj_ref[:, p * POOL_WIDTH:(p + 1) * POOL_WIDTH] = pc
            da = da + _mm_nt(pc, win_v[p])
        xv = x_ref[...]
        r = lax.rsqrt(jnp.mean(xv * xv, axis=-1, keepdims=True) + NORM_EPS)
        xn = xv * r
        dg1_ref[...] += jnp.sum(da * xn, axis=0, keepdims=True)
        dx_ref[...] = dh1_ref[...] + _rms_bwd(da * g1_ref[...], xn, r)

    tok = lambda w: pl.BlockSpec((tm, w), lambda t: (nblk - 1 - t, 0))
    const = lambda shp: pl.BlockSpec(shp, lambda t: (0,) * len(shp))
    return pl.pallas_call(
        body, name="bwd_inproj",
        grid=(nblk,),
        in_specs=[tok(ATTN_WIDTH)] * 5 + [tok(POOL_WIDTH), tok(POOL_WIDTH), tok(d), tok(d),
                                          ANY, const(poolw.shape), const((1, POOL_WIDTH)), const((1, ATTN_WIDTH)),
                                          const((1, ATTN_WIDTH)), const((1, d))],
        out_specs=[tok(d), tok(4 * POOL_WIDTH), const((1, d)), const((1, ATTN_WIDTH)), const((1, ATTN_WIDTH)),
                   const((ngrp, LANES, LANES)), const((1, POOL_WIDTH))],
        out_shape=[jax.ShapeDtypeStruct((s, d), F32),
                   jax.ShapeDtypeStruct((s, 4 * POOL_WIDTH), MXU_DTYPE),
                   jax.ShapeDtypeStruct((1, d), F32),
                   jax.ShapeDtypeStruct((1, ATTN_WIDTH), F32),
                   jax.ShapeDtypeStruct((1, ATTN_WIDTH), F32),
                   jax.ShapeDtypeStruct((ngrp, LANES, LANES), F32),
                   jax.ShapeDtypeStruct((1, POOL_WIDTH), F32)],
        scratch_shapes=[pltpu.VMEM(win.shape, MXU_DTYPE), pltpu.VMEM((tm + POOL_HALO, POOL_WIDTH), F32)],
        compiler_params=_params(("arbitrary",)),
    )(dqn, dkn, dv, q32, k32, dypool, pooled, x, dh1, win, poolw, pscale, qg, kg, g1)


def _wgrad_call(a, b, bm, bn, bk, out_shape, out_block, out_index, name):
    s, m = a.shape
    _, n = b.shape
    nk = s // bk

    def body(a_ref, b_ref, o_ref, wire_ref):
        k = pl.program_id(2)

        @pl.when(k == 0)
        def _():
            o_ref[...] = jnp.zeros(o_ref.shape, F32)

        o_ref[...] += _mm_tn(a_ref[...].astype(MXU_DTYPE), b_ref[...].astype(MXU_DTYPE))

        @pl.when(k == nk - 1)
        def _():
            wire_ref[...] = o_ref[...].astype(WIRE_DTYPE)

    return pl.pallas_call(
        body, name=name,
        grid=(m // bm, n // bn, nk),
        in_specs=[pl.BlockSpec((bk, bm), lambda i, j, k: (k, i)), pl.BlockSpec((bk, bn), lambda i, j, k: (k, j))],
        out_specs=[pl.BlockSpec(out_block, out_index)] * 2,
        out_shape=[jax.ShapeDtypeStruct(out_shape, F32), jax.ShapeDtypeStruct(out_shape, WIRE_DTYPE)],
        compiler_params=_params(("arbitrary", "arbitrary", "arbitrary")),
    )(a, b)


def _local_grads(x, tgt, g1, win, poolw, pscale, qg, kg, rel_bias, g2, mlp_weights, on_mlp_grads=None):
    s, d = x.shape
    g1r, g2r = g1.reshape(1, d), g2.reshape(1, d)
    psr = pscale.reshape(1, POOL_WIDTH)
    qgr = jnp.tile(qg, N_HEADS).reshape(1, ATTN_WIDTH)
    kgr = jnp.tile(kg, N_HEADS).reshape(1, ATTN_WIDTH)
    pw_c = poolw.astype(MXU_DTYPE)
    buckets = jnp.asarray(_bucket_tables())
    bias = _bias_table_call(rel_bias, buckets, jnp.asarray(_previous_block_keys().astype(np.int32)))
    bk = min(s, 2048)

    a, pooled, ypool, q32, k32, qn, kn, v = _f1_call(x, g1r, win, pw_c, psr, qgr, kgr, tm=512)
    o, lse = _attn_fwd_call(qn, kn, v, bias)
    wout, wup, wdown = mlp_weights(o)
    mixed, c, ff, dz, dy, dh1, dypool, do, delta, dg2, loss = _f2_call(x, tgt, ypool, o, wout, wup, wdown, g2r, tm=256)
    dff = ff.shape[1]
    g_out = [g.reshape(N_CHIPS, d // N_CHIPS, d)
             for g in _wgrad_call(mixed, dh1, d, d, bk // 2, (d, d), (d, d), lambda i, j, k: (0, 0), "wgrad_out")]
    g_up = _wgrad_call(c, dz, d, dff // N_CHIPS, bk, (N_CHIPS, d, dff // N_CHIPS), (None, d, dff // N_CHIPS),
                       lambda i, j, k: (j, 0, 0), "wgrad_up")
    g_down = _wgrad_call(ff, dy, dff // N_CHIPS, d, bk, (N_CHIPS, dff // N_CHIPS, d), (None, dff // N_CHIPS, d),
                         lambda i, j, k: (i, 0, 0), "wgrad_down")
    dep = None if on_mlp_grads is None else on_mlp_grads(g_out[1], g_up[1], g_down[1])
    dqn, dkn, dv, dbias = _attn_bwd_call(qn, kn, v, do, lse, delta, bias, dep)
    dx, dproj, dg1, dqg, dkg, dpw, dps = _bproj_call(
        dqn, dkn, dv, q32, k32, dypool, pooled, x, dh1, win, pw_c, psr, qgr, kgr, g1r, tm=256)
    nin = dproj.shape[1] // N_CHIPS
    g_in = _wgrad_call(a, dproj, d, nin, bk, (N_CHIPS, d, nin), (None, d, nin), lambda i, j, k: (j, 0, 0), "wgrad_in")
    drb = _rel_bias_grad_call(dbias, buckets)
    small = dict(
        mix_norm_g=dg1.reshape(d), mlp_norm_g=dg2.reshape(d), pool_scale=dps.reshape(POOL_WIDTH),
        q_norm_g=dqg.reshape(ATTN_WIDTH), k_norm_g=dkg.reshape(ATTN_WIDTH),
        rel_bias=drb[:, :N_BUCKETS].T, pool_w=dpw)
    return loss[0, 0], dx, (g_in, g_out, g_up, g_down), small


def _coords():
    return lax.axis_index("x"), lax.axis_index("y"), lax.axis_index("c")


def _other_chips(x, y):
    return [(1 - x, y), (x, 1 - y), (1 - x, 1 - y)]


def _remote(src, dst, send_sem, recv_sem, dev):
    return pltpu.make_async_remote_copy(src_ref=src, dst_ref=dst, send_sem=send_sem, recv_sem=recv_sem,
                                        device_id=dev, device_id_type=MESH)


def _halves(a):
    return a.reshape(a.shape[:-2] + (2, a.shape[-2] // 2, a.shape[-1]))


def _place_shards_call(shards, chip_idx, nch):
    nw = len(shards)

    def body(chip_ref, *refs):
        for w in range(nw):
            refs[nw + w][...] = refs[w][...].astype(WIRE_DTYPE)

    in_specs = [pl.BlockSpec((s.shape[0] // nch, s.shape[1]), lambda i, chip_ref: (i, 0)) for s in shards]
    out_specs = [pl.BlockSpec((None, s.shape[0] // nch, s.shape[1]), lambda i, chip_ref: (chip_ref[0], i, 0))
                 for s in shards]
    return pl.pallas_call(
        body, name="weights_place",
        grid_spec=pltpu.PrefetchScalarGridSpec(num_scalar_prefetch=1, grid=(nch,),
                                               in_specs=in_specs, out_specs=out_specs),
        out_shape=[jax.ShapeDtypeStruct((N_CHIPS,) + s.shape, WIRE_DTYPE) for s in shards],
        compiler_params=_params(("arbitrary",)),
    )(chip_idx, *shards)


def _allgather_call(placed, from_chips, name):
    nw = len(placed)
    ncp = 3 * nw

    def body(*refs):
        outs = refs[nw:2 * nw]
        send1, recv1, send2, recv2 = refs[2 * nw:]
        x, y, c = _coords()
        chip = 2 * x + y
        others = _other_chips(x, y)
        first, passed = [], []
        if from_chips:
            for w in range(nw):
                for k, (ox, oy) in enumerate(others):
                    mine = outs[w].at[chip, c]
                    cp = _remote(mine, mine, send1.at[3 * w + k], recv1.at[3 * w + k], (ox, oy, c))
                    cp.start()
                    first.append(cp)
        for w in range(nw):
            for k, (ox, oy) in enumerate(others):
                piece = outs[w].at[2 * ox + oy, c]
                if from_chips:
                    _remote(piece, piece, send1.at[3 * w + k], recv1.at[3 * w + k], (ox, oy, c)).wait_recv()
                cp = _remote(piece, piece, send2.at[3 * w + k], recv2.at[3 * w + k], (x, y, 1 - c))
                cp.start()
                passed.append(cp)
        for w in range(nw):
            for k, (ox, oy) in enumerate(others):
                piece = outs[w].at[2 * ox + oy, 1 - c]
                _remote(piece, piece, send2.at[3 * w + k], recv2.at[3 * w + k], (x, y, 1 - c)).wait_recv()
        for cp in first + passed:
            cp.wait_send()

    return pl.pallas_call(
        body, name=name,
        in_specs=[ANY] * nw, out_specs=[ANY] * nw,
        out_shape=[jax.ShapeDtypeStruct(s.shape, s.dtype) for s in placed],
        input_output_aliases={w: w for w in range(nw)},
        scratch_shapes=[pltpu.SemaphoreType.DMA((ncp,))] * 4,
    )(*placed)


HBM_SPEC = pl.BlockSpec(memory_space=pltpu.HBM)
SEM_SPEC = pl.BlockSpec(memory_space=pltpu.SEMAPHORE)
SPLIT_EFFECT = pltpu.SideEffectType.DATAFLOW_SIDE_EFFECTING


def _in_hbm(a):
    return pltpu.with_memory_space_constraint(a, pltpu.HBM)


def _gather_copies(bufs, send, recv):
    x, y, c = _coords()
    chip = 2 * x + y
    cps = []
    for w, buf in enumerate(bufs):
        for k, (ox, oy) in enumerate(_other_chips(x, y)):
            mine, theirs = buf.at[chip, c], buf.at[2 * ox + oy, c]
            sems = (send.at[3 * w + k], recv.at[3 * w + k], (ox, oy, c))
            cps.append((_remote(mine, mine, *sems), _remote(theirs, theirs, *sems)))
    return cps


def _gather_start_call(bufs, after):
    nw = len(bufs)

    def body(*refs):
        ins, send, recv, token = refs[:nw], refs[nw + 1], refs[nw + 2], refs[2 * nw + 3]
        for out, _ in _gather_copies(ins, send, recv):
            out.start()
        token[...] = jnp.zeros(token.shape, F32)

    res = pl.pallas_call(
        body, name="weights_gather_start",
        in_specs=[HBM_SPEC] * nw + [ANY],
        out_specs=[SEM_SPEC, SEM_SPEC] + [HBM_SPEC] * nw + [pl.BlockSpec(memory_space=pltpu.VMEM)],
        out_shape=[pltpu.SemaphoreType.DMA((3 * nw,)), pltpu.SemaphoreType.DMA((3 * nw,))]
        + [pltpu.HBM(b.shape, b.dtype) for b in bufs] + [jax.ShapeDtypeStruct((8, LANES), F32)],
        input_output_aliases={w: 2 + w for w in range(nw)},
        compiler_params=pltpu.CompilerParams(has_side_effects=SPLIT_EFFECT),
    )(*[_in_hbm(b) for b in bufs], after)
    return res[0], res[1], list(res[2:2 + nw]), res[2 + nw]


def _gather_wait_call(bufs, send, recv, after):
    nw = len(bufs)

    def body(*refs):
        ins, send, recv = refs[:nw], refs[nw], refs[nw + 1]
        for out, back in _gather_copies(ins, send, recv):
            out.wait_send()
            back.wait_recv()

    return pl.pallas_call(
        body, name="weights_gather_wait",
        in_specs=[HBM_SPEC] * nw + [SEM_SPEC, SEM_SPEC, ANY],
        out_specs=[HBM_SPEC] * nw,
        out_shape=[pltpu.HBM(b.shape, b.dtype) for b in bufs],
        input_output_aliases={w: w for w in range(nw)},
        compiler_params=pltpu.CompilerParams(has_side_effects=SPLIT_EFFECT),
    )(*bufs, send, recv, after)


def _scatter_copies(srcs, lands, send, recv, wholes):
    x, y, c = _coords()
    me = 4 * x + 2 * y + c
    cps = []
    for w, (src, land) in enumerate(zip(srcs, lands)):
        for r in range(1, N_DEV):
            px, py, pc = ((1 - x) if r & 4 else x, (1 - y) if r & 2 else y, (1 - c) if r & 1 else c)
            sems = (send.at[(N_DEV - 1) * w + r - 1], recv.at[(N_DEV - 1) * w + r - 1], (px, py, pc))
            piece = src if wholes[w] else src.at[2 * px + py, pc]
            cps.append((_remote(piece, land.at[me], *sems), _remote(piece, land.at[4 * px + 2 * py + pc], *sems)))
    return cps


def _scatter_start_call(srcs, lands, wholes, name):
    nw = len(srcs)
    ncp = (N_DEV - 1) * nw

    def body(*refs):
        ins, lnd, send, recv, token = refs[:nw], refs[nw:2 * nw], refs[2 * nw], refs[2 * nw + 1], refs[4 * nw + 2]
        for out, _ in _scatter_copies(ins, lnd, send, recv, wholes):
            out.start()
        token[...] = jnp.zeros(token.shape, F32)

    res = pl.pallas_call(
        body, name=name,
        in_specs=[HBM_SPEC] * (2 * nw),
        out_specs=[SEM_SPEC, SEM_SPEC] + [HBM_SPEC] * (2 * nw) + [pl.BlockSpec(memory_space=pltpu.VMEM)],
        out_shape=[pltpu.SemaphoreType.DMA((ncp,)), pltpu.SemaphoreType.DMA((ncp,))]
        + [pltpu.HBM(b.shape, b.dtype) for b in list(srcs) + list(lands)] + [jax.ShapeDtypeStruct((8, LANES), F32)],
        input_output_aliases={i: 2 + i for i in range(2 * nw)},
        compiler_params=pltpu.CompilerParams(has_side_effects=SPLIT_EFFECT),
    )(*[_in_hbm(b) for b in list(srcs) + list(lands)])
    return res[0], res[1], list(res[2:2 + nw]), list(res[2 + nw:2 + 2 * nw]), res[2 + 2 * nw]


def _scatter_wait_call(srcs, lands, send, recv, after, wholes, name):
    nw = len(srcs)

    def body(*refs):
        ins, lnd, send, recv = refs[:nw], refs[nw:2 * nw], refs[2 * nw], refs[2 * nw + 1]
        for out, back in _scatter_copies(ins, lnd, send, recv, wholes):
            out.wait_send()
            back.wait_recv()

    res = pl.pallas_call(
        body, name=name,
        in_specs=[HBM_SPEC] * (2 * nw) + [SEM_SPEC, SEM_SPEC, ANY],
        out_specs=[HBM_SPEC] * (2 * nw),
        out_shape=[pltpu.HBM(b.shape, b.dtype) for b in list(srcs) + list(lands)],
        input_output_aliases={i: i for i in range(2 * nw)},
        compiler_params=pltpu.CompilerParams(has_side_effects=SPLIT_EFFECT),
    )(*srcs, *lands, send, recv, after)
    return list(res[nw:])


def _reduce_call(own, lands, idx, nch, name, dep=None):
    nw = len(own)
    deps = [] if dep is None else [dep]

    def body(idx_ref, *refs):
        refs = refs[:2 * nw] + refs[2 * nw + len(deps):]
        for w in range(nw):
            tot = refs[w][...]
            for r in range(1, N_DEV):
                tot = tot + refs[nw + w][idx_ref[1 + r]].astype(F32)
            refs[2 * nw + w][...] = tot

    in_specs, out_specs, out_shape = [], [], []
    for s in own:
        in_specs.append(pl.BlockSpec((None, None, s.shape[2] // nch, s.shape[3]),
                                     lambda i, idx_ref: (idx_ref[0], idx_ref[1], i, 0)))
    for s in own:
        in_specs.append(pl.BlockSpec((N_DEV, s.shape[2] // nch, s.shape[3]), lambda i, idx_ref: (0, i, 0)))
    for s in own:
        out_specs.append(pl.BlockSpec((None, s.shape[2] // nch, s.shape[3]), lambda i, idx_ref: (idx_ref[1], i, 0)))
        out_shape.append(jax.ShapeDtypeStruct((2,) + s.shape[2:], F32))
    return pl.pallas_call(
        body, name=name,
        grid_spec=pltpu.PrefetchScalarGridSpec(num_scalar_prefetch=1, grid=(nch,),
                                               in_specs=in_specs + [ANY] * len(deps), out_specs=out_specs),
        out_shape=out_shape,
        compiler_params=_params(("arbitrary",)),
    )(idx, *own, *lands, *deps)


def _pair_allgather_call(halves, name):
    nw = len(halves)

    def body(*refs):
        outs = refs[nw:2 * nw]
        send, recv = refs[2 * nw:]
        x, y, c = _coords()
        cps = []
        for w in range(nw):
            cp = _remote(outs[w].at[c], outs[w].at[c], send.at[w], recv.at[w], (x, y, 1 - c))
            cp.start()
            cps.append(cp)
        for w in range(nw):
            theirs = outs[w].at[1 - c]
            _remote(theirs, theirs, send.at[w], recv.at[w], (x, y, 1 - c)).wait_recv()
        for cp in cps:
            cp.wait_send()

    outs = pl.pallas_call(
        body, name=name,
        in_specs=[ANY] * nw, out_specs=[ANY] * nw,
        out_shape=[jax.ShapeDtypeStruct(h.shape, h.dtype) for h in halves],
        input_output_aliases={w: w for w in range(nw)},
        scratch_shapes=[pltpu.SemaphoreType.DMA((nw,))] * 2,
    )(*halves)
    return [o.reshape(2 * h.shape[1], h.shape[2]) for o, h in zip(outs, halves)]


def _adamw(w, g, m, v):
    m = ADAM_B1 * m + (1.0 - ADAM_B1) * g
    v = ADAM_B2 * v + (1.0 - ADAM_B2) * (g * g)
    m_hat = m / (1.0 - ADAM_B1 ** ADAM_STEP)
    v_hat = v / (1.0 - ADAM_B2 ** ADAM_STEP)
    delta = -ADAM_LR * (m_hat / (jnp.sqrt(v_hat) + ADAM_EPS) + ADAM_WD * w)
    return delta, m, v


def _adamw_call(ws, gs, ms, vs, nch, name):
    nw = len(ws)

    def body(*refs):
        for w in range(nw):
            g = refs[nw + w][...]
            delta, m, v = _adamw(refs[w][...], g, refs[2 * nw + w][...], refs[3 * nw + w][...])
            refs[4 * nw + w][...] = g
            refs[5 * nw + w][...] = delta
            refs[6 * nw + w][...] = m
            refs[7 * nw + w][...] = v

    specs = [pl.BlockSpec((a.shape[0] // nch, a.shape[1]), lambda i: (i, 0)) for a in ws]
    res = pl.pallas_call(
        body, name=name,
        grid=(nch,),
        in_specs=specs * 4, out_specs=specs * 4,
        out_shape=[jax.ShapeDtypeStruct(a.shape, F32) for a in ws] * 4,
        compiler_params=_params(("arbitrary",)),
    )(*ws, *gs, *ms, *vs)
    return res[:nw], res[nw:2 * nw], res[2 * nw:3 * nw], res[3 * nw:]


def _small_call(gathered, own, me_idx, w, m, v):
    def fold(row):
        tot = row[:, 0:LANES] + row[:, LANES:2 * LANES] + row[:, 2 * LANES:3 * LANES] + row[:, 3 * LANES:4 * LANES]
        return tot + pltpu.roll(tot, HEAD_DIM, axis=1)

    def body(me_ref, ga_ref, own_ref, w_ref, m_ref, v_ref, g_out, d_out, m_out, v_out):
        me = me_ref[0]
        term = lambda i: jnp.where(me == i, own_ref[...], ga_ref[i])
        g = term(0)
        for i in range(1, N_DEV):
            g = g + term(i)
        unfolded = g[4:5, :]
        folded = jnp.concatenate([fold(unfolded[:, :ATTN_WIDTH]), fold(unfolded[:, ATTN_WIDTH:]),
                                  jnp.zeros((1, 1024 - 2 * LANES), F32)], axis=-1)
        row = lax.broadcasted_iota(jnp.int32, g.shape, 0)
        g = jnp.where(row == 3, folded, g)
        delta, mm, vv = _adamw(w_ref[...], g, m_ref[...], v_ref[...])
        g_out[...] = g
        d_out[...] = delta
        m_out[...] = mm
        v_out[...] = vv

    vmem = pl.BlockSpec(memory_space=pltpu.VMEM)
    return pl.pallas_call(
        body, name="adamw_small",
        in_specs=[pl.BlockSpec(memory_space=pltpu.SMEM)] + [vmem] * 5,
        out_shape=[jax.ShapeDtypeStruct(w.shape, F32)] * 4,
        compiler_params=_params(),
    )(me_idx, gathered, own, w, m, v)


def _pack_small(p, folded=True, loss=None):
    z = lambda n: jnp.zeros((n,), F32)
    rows = [p["mix_norm_g"], p["mlp_norm_g"],
            jnp.concatenate([p["pool_scale"], p["rel_bias"].reshape(-1), z(1024 - POOL_WIDTH - N_BUCKETS * N_HEADS)])]
    if folded:
        rows += [jnp.concatenate([p["q_norm_g"], z(LANES - HEAD_DIM), p["k_norm_g"], z(1024 - LANES - HEAD_DIM)]), z(1024)]
    else:
        rows += [z(1024), jnp.concatenate([p["q_norm_g"], p["k_norm_g"]])]
    rows += [z(1024) if loss is None else jnp.concatenate([loss.reshape(1), z(1023)])]
    head = jnp.stack(rows + [z(1024)] * 2)
    return jnp.concatenate([head, p["pool_w"].reshape(-1, 1024)], axis=0)


def _unpack_small(a):
    return dict(
        mix_norm_g=a[0], mlp_norm_g=a[1], pool_scale=a[2, :POOL_WIDTH],
        rel_bias=a[2, POOL_WIDTH:POOL_WIDTH + N_BUCKETS * N_HEADS].reshape(N_BUCKETS, N_HEADS),
        q_norm_g=a[3, :HEAD_DIM], k_norm_g=a[3, LANES:LANES + HEAD_DIM],
        pool_w=a[8:].reshape(len(POOL_WINDOWS), LANES, LANES))


_WEIGHT_ORDER = ("mix_norm_g", "w_in", "pool_w", "pool_scale", "q_norm_g", "k_norm_g", "rel_bias", "w_out",
                 "mlp_norm_g", "w_up", "w_down")
_BIG = ("w_in", "w_out", "w_up", "w_down")


def kernel(x, mix_norm_g, w_in, pool_w, pool_scale, q_norm_g, k_norm_g, rel_bias, w_out, mlp_norm_g, w_up, w_down, loss_target, m_mix_norm_g, m_w_in, m_pool_w, m_pool_scale, m_q_norm_g, m_k_norm_g, m_rel_bias, m_w_out, m_mlp_norm_g, m_w_up, m_w_down, v_mix_norm_g, v_w_in, v_pool_w, v_pool_scale, v_q_norm_g, v_k_norm_g, v_rel_bias, v_w_out, v_mlp_norm_g, v_w_up, v_w_down):
    w = dict(mix_norm_g=mix_norm_g, w_in=w_in, pool_w=pool_w, pool_scale=pool_scale, q_norm_g=q_norm_g,
             k_norm_g=k_norm_g, rel_bias=rel_bias, w_out=w_out, mlp_norm_g=mlp_norm_g, w_up=w_up, w_down=w_down)
    m = dict(mix_norm_g=m_mix_norm_g, w_in=m_w_in, pool_w=m_pool_w, pool_scale=m_pool_scale, q_norm_g=m_q_norm_g,
             k_norm_g=m_k_norm_g, rel_bias=m_rel_bias, w_out=m_w_out, mlp_norm_g=m_mlp_norm_g, w_up=m_w_up, w_down=m_w_down)
    v = dict(mix_norm_g=v_mix_norm_g, w_in=v_w_in, pool_w=v_pool_w, pool_scale=v_pool_scale, q_norm_g=v_q_norm_g,
             k_norm_g=v_k_norm_g, rel_bias=v_rel_bias, w_out=v_w_out, mlp_norm_g=v_mlp_norm_g, w_up=v_w_up, w_down=v_w_down)
    xc, yc, cc = _coords()

    c_idx = jnp.reshape(cc, (1,)).astype(jnp.int32)
    chip_idx = jnp.reshape(2 * xc + yc, (1,)).astype(jnp.int32)
    me = 4 * xc + 2 * yc + cc
    whole = lambda t: t.reshape(t.shape[0], t.shape[1] * t.shape[2], t.shape[3])

    placed = [_halves(p) for p in _place_shards_call([w[n] for n in _BIG], chip_idx, nch=4)]
    (win_f,) = _allgather_call(placed[:1], from_chips=True, name="weights_allgather_in")
    wsend, wrecv, in_flight, started = _gather_start_call(placed[1:], win_f)

    def mlp_weights(after):
        landed = _gather_wait_call(in_flight, wsend, wrecv, after)
        wout_f, wup_f, wdown_f = _allgather_call(landed, from_chips=False, name="weights_pair_forward")
        return whole(wout_f).reshape(-1, wout_f.shape[-1]), whole(wup_f), whole(wdown_f)

    split = []

    def on_mlp_grads(*wire_grads):
        srcs = [_halves(g) for g in wire_grads]
        lands = [lax.empty((N_DEV,) + s.shape[2:], s.dtype) for s in srcs]
        split.extend(_scatter_start_call(srcs, lands, [False] * len(srcs), "grads_scatter_start"))
        return split[4]

    loss_part, dx, big_grads, small_grads = _local_grads(
        x[0], loss_target[0], mix_norm_g + started[0, 0], whole(win_f), pool_w, pool_scale, q_norm_g, k_norm_g, rel_bias,
        mlp_norm_g, mlp_weights, on_mlp_grads)
    g_in, g_out, g_up, g_down = big_grads
    gsend, grecv, srcs_thru, lands_thru, _ = split
    lands_mlp = _scatter_wait_call(srcs_thru, lands_thru, gsend, grecv, g_in[1], [False] * 3, "grads_scatter_wait")

    small_own = _pack_small(small_grads, folded=False, loss=loss_part)
    last_srcs = [_halves(g_in[1]), small_own]
    last_lands = [lax.empty((N_DEV,) + last_srcs[0].shape[2:], WIRE_DTYPE), lax.empty((N_DEV,) + small_own.shape, F32)]
    lsend, lrecv, last_srcs, last_lands, last_started = _scatter_start_call(
        last_srcs, last_lands, [False, True], "grads_scatter_start_last")
    idx = jnp.concatenate([chip_idx, c_idx] + [jnp.reshape(jnp.bitwise_xor(me, r), (1,)) for r in range(1, N_DEV)])
    idx = idx.astype(jnp.int32)
    mlp = _BIG[1:]

    def update(names, own32, lands, tag, dep=None):
        halves = _reduce_call([_halves(g) for g in own32], lands, idx, 4, "grads_reduce_" + tag, dep)
        reduced = _pair_allgather_call(list(halves), "grads_pair_allgather_" + tag)
        return _adamw_call([w[n] for n in names], reduced, [m[n] for n in names], [v[n] for n in names], 8, "adamw_" + tag)

    out_mlp = update(mlp, [g_out[0], g_up[0], g_down[0]], lands_mlp, "mlp", last_started)
    land_in, small_all = _scatter_wait_call(last_srcs, last_lands, lsend, lrecv, out_mlp[3][-1], [False, True],
                                            "grads_scatter_wait_last")
    out_in = update(_BIG[:1], [g_in[0]], [land_in], "in")
    g_pack, d_pack, m_pack, v_pack = _small_call(
        small_all, small_own, jnp.reshape(me, (1,)).astype(jnp.int32), _pack_small(w), _pack_small(m), _pack_small(v))

    grads, deltas, new_m, new_v = (_unpack_small(a) for a in (g_pack, d_pack, m_pack, v_pack))
    for k, res in enumerate((grads, deltas, new_m, new_v)):
        res[_BIG[0]] = out_in[k][0]
        for i, n in enumerate(mlp):
            res[n] = out_mlp[k][i]
    loss = g_pack[LOSS_ROW, 0]
    return (loss, dx[None], *[grads[n] for n in _WEIGHT_ORDER], *[deltas[n] for n in _WEIGHT_ORDER],
            *[new_m[n] for n in _WEIGHT_ORDER], *[new_v[n] for n in _WEIGHT_ORDER])
```

```python
import math

import jax
import jax.numpy as jnp
import numpy as np
from jax import lax
from jax.experimental import pallas as pl
from jax.experimental.pallas import tpu as pltpu

F32 = jnp.float32
MXU_DTYPE = jnp.bfloat16
WIRE_DTYPE = jnp.bfloat16

NORM_EPS = 1e-6
NEG_INF = -1e30
LANES = 128
HEAD_DIM = 64
N_HEADS = 8
POOL_WIDTH = 512
ATTN_WIDTH = 512
POOL_WINDOWS = (2, 4, 8, 16)
POOL_HALO = 16
DILATED_PATTERNS = ((128, 1), (512, 4), (2048, 16))
ATT_BLOCK = 128
ATT_SUPER = ATT_BLOCK * max(dl for _, dl in DILATED_PATTERNS)
ATT_UNITS = ATT_SUPER // ATT_BLOCK
N_BUCKETS = 32
NO_BUCKET = -1
MAX_DISTANCE = 2048
N_CHIPS = 4
N_DEV = 8
ADAM_LR, ADAM_B1, ADAM_B2, ADAM_EPS, ADAM_WD, ADAM_STEP = 0.001, 0.9, 0.999, 1e-08, 0.01, 10
VMEM_LIMIT = 56 * 1024 * 1024
MESH = pl.DeviceIdType.MESH
ANY = pl.BlockSpec(memory_space=pl.ANY)

SMALL_ROWS = 72
LOSS_ROW = 5


def _mm(a, b):
    return jnp.dot(a, b, preferred_element_type=F32)


def _mm_nt(a, b):
    return lax.dot_general(a, b, (((1,), (1,)), ((), ())), preferred_element_type=F32)


def _mm_tn(a, b):
    return lax.dot_general(a, b, (((0,), (0,)), ((), ())), preferred_element_type=F32)


def _params(sem=None, **kw):
    if sem is not None:
        kw["dimension_semantics"] = sem
    return pltpu.CompilerParams(vmem_limit_bytes=VMEM_LIMIT, **kw)


def _low_half():
    return lax.broadcasted_iota(jnp.int32, (1, LANES), 1) < HEAD_DIM


def _head_sum_bcast(y):
    lo = _low_half()
    outs = []
    for j in range(y.shape[1] // LANES):
        c = y[:, j * LANES:(j + 1) * LANES]
        s_lo = jnp.sum(jnp.where(lo, c, 0.0), axis=-1, keepdims=True)
        s_hi = jnp.sum(jnp.where(lo, 0.0, c), axis=-1, keepdims=True)
        outs.append(jnp.where(lo, s_lo, s_hi))
    return jnp.concatenate(outs, axis=-1)


def _rms_bwd(dn, hn, r):
    return r * (dn - hn * jnp.mean(dn * hn, axis=-1, keepdims=True))


def _t5_bucket_np(dist):
    max_exact = N_BUCKETS // 2
    d_f = np.maximum(dist, 1).astype(np.float32)
    ratio = (np.log(d_f / np.float32(max_exact)) / np.float32(math.log(MAX_DISTANCE / max_exact))).astype(np.float32)
    large = max_exact + (ratio * np.float32(N_BUCKETS - max_exact)).astype(np.int32)
    large = np.minimum(large, N_BUCKETS - 1)
    return np.where(dist < max_exact, dist, large).astype(np.int32)


def _window_offsets(dl):
    if dl == 1:
        return _by4_positions(ATT_BLOCK), _by4_positions(2 * ATT_BLOCK)
    return np.arange(ATT_BLOCK), np.arange(2 * ATT_BLOCK)


def _bucket_tables():
    tables = []
    for _, dl in DILATED_PATTERNS:
        qq, kk = _window_offsets(dl)
        dist = qq[:, None] + ATT_BLOCK - kk[None, :]
        bucket = _t5_bucket_np(np.clip(dist, 0, ATT_BLOCK) * dl)
        tables.append(np.where((dist >= 0) & (dist <= ATT_BLOCK), bucket, NO_BUCKET))
    return np.stack(tables).astype(np.int32)


def _previous_block_keys():
    return np.stack([np.broadcast_to(_window_offsets(dl)[1][None, :] < ATT_BLOCK, (ATT_BLOCK, 2 * ATT_BLOCK))
                     for _, dl in DILATED_PATTERNS])


def _f1_call(x, g1, win, poolw, pscale, qg, kg, tm):
    s, d = x.shape
    nblk = s // tm

    def body(x_ref, g1_ref, win_ref, pw_ref, ps_ref, qg_ref, kg_ref,
             a_ref, pooled_ref, ypool_ref, q32_ref, k32_ref, qn_ref, kn_ref, v_ref, ubuf):
        i = pl.program_id(0)
        xv = x_ref[...]
        r = lax.rsqrt(jnp.mean(xv * xv, axis=-1, keepdims=True) + NORM_EPS)
        a = ((xv * r) * g1_ref[...]).astype(MXU_DTYPE)
        a_ref[...] = a
        u = _mm(a, win_ref[0])
        q = _mm(a, win_ref[1])
        k = _mm(a, win_ref[2])
        v_ref[...] = _mm(a, win_ref[3]).astype(MXU_DTYPE)
        q32_ref[...] = q
        k32_ref[...] = k
        rq = lax.rsqrt(_head_sum_bcast(q * q) * (1.0 / HEAD_DIM) + NORM_EPS)
        qn_ref[...] = (((q * rq) * qg_ref[...]) * (HEAD_DIM ** -0.5)).astype(MXU_DTYPE)
        rk = lax.rsqrt(_head_sum_bcast(k * k) * (1.0 / HEAD_DIM) + NORM_EPS)
        kn_ref[...] = ((k * rk) * kg_ref[...]).astype(MXU_DTYPE)

        @pl.when(i == 0)
        def _():
            ubuf[0:POOL_HALO, :] = jnp.zeros((POOL_HALO, POOL_WIDTH), F32)

        @pl.when(i > 0)
        def _():
            ubuf[0:POOL_HALO, :] = ubuf[tm:tm + POOL_HALO, :]

        ubuf[POOL_HALO:POOL_HALO + tm, :] = u
        t = i * tm + lax.broadcasted_iota(jnp.int32, (tm, 1), 0)
        for g, w in enumerate(POOL_WINDOWS):
            ls = slice(g * LANES, (g + 1) * LANES)
            ug = u[:, ls]
            acc = ug
            for sh in range(1, w):
                acc = acc + ubuf[POOL_HALO - sh:POOL_HALO - sh + tm, ls]
            cnt = jnp.minimum(t + 1, w).astype(F32)
            pooled = (acc / cnt - ug).astype(MXU_DTYPE)
            pooled_ref[:, ls] = pooled
            ypool_ref[:, ls] = (_mm(pooled, pw_ref[g]) * ps_ref[:, ls]).astype(MXU_DTYPE)

    tok = lambda w: pl.BlockSpec((tm, w), lambda i: (i, 0))
    full = lambda shp: pl.BlockSpec(shp, lambda i: (0,) * len(shp))
    return pl.pallas_call(
        body, name="fwd_inproj",
        grid=(nblk,),
        in_specs=[tok(d), full((1, d)), full(win.shape), full(poolw.shape), full((1, POOL_WIDTH)),
                  full((1, ATTN_WIDTH)), full((1, ATTN_WIDTH))],
        out_specs=[tok(d), tok(POOL_WIDTH), tok(POOL_WIDTH), tok(ATTN_WIDTH), tok(ATTN_WIDTH),
                   tok(ATTN_WIDTH), tok(ATTN_WIDTH), tok(ATTN_WIDTH)],
        out_shape=[jax.ShapeDtypeStruct((s, d), MXU_DTYPE),
                   jax.ShapeDtypeStruct((s, POOL_WIDTH), MXU_DTYPE),
                   jax.ShapeDtypeStruct((s, POOL_WIDTH), MXU_DTYPE),
                   jax.ShapeDtypeStruct((s, ATTN_WIDTH), F32),
                   jax.ShapeDtypeStruct((s, ATTN_WIDTH), F32),
                   jax.ShapeDtypeStruct((s, ATTN_WIDTH), MXU_DTYPE),
                   jax.ShapeDtypeStruct((s, ATTN_WIDTH), MXU_DTYPE),
                   jax.ShapeDtypeStruct((s, ATTN_WIDTH), MXU_DTYPE)],
        scratch_shapes=[pltpu.VMEM((tm + POOL_HALO, POOL_WIDTH), F32)],
        compiler_params=_params(("arbitrary",)),
    )(x, g1, win, poolw, pscale, qg, kg)


DEINT = 4
assert [dl for _, dl in DILATED_PATTERNS] == [1, DEINT, DEINT * DEINT]


def _by4_positions(n):
    pos = np.arange(n)
    return DEINT * (pos % (n // DEINT)) + pos // (n // DEINT)


def _masked_bias(b_ref, p, n):
    return b_ref[p, jnp.minimum(n, 1)].reshape(2 * ATT_BLOCK, 2 * ATT_BLOCK)


def _unit_rows(u, dl):
    sq, sk = ATT_SUPER // DEINT, 2 * ATT_SUPER // DEINT
    if dl == 1:
        n = ATT_BLOCK // DEINT
        return (u, [pl.ds(pl.multiple_of(r * sq + n * u, 8), n) for r in range(DEINT)],
                [pl.ds(pl.multiple_of(r * sk + sk // 2 + n * (u - 1), 8), 2 * n) for r in range(DEINT)])
    if dl == DEINT:
        r, b = u % DEINT, u // DEINT
        return (b, [pl.ds(pl.multiple_of(r * sq + ATT_BLOCK * b, 8), ATT_BLOCK)],
                [pl.ds(pl.multiple_of(r * sk + sk // 2 + ATT_BLOCK * (b - 1), 8), 2 * ATT_BLOCK)])
    r, a = u % DEINT, u // DEINT
    return 0, [pl.ds(r * sq + a, ATT_BLOCK, stride=DEINT)], [pl.ds(r * sk + a, 2 * ATT_BLOCK, stride=DEINT)]


def _take(ref, runs):
    parts = [ref[run, :] for run in runs]
    return parts[0] if len(parts) == 1 else jnp.concatenate(parts, axis=0)


def _put(ref, runs, value, add=False):
    n = value.shape[0] // len(runs)
    for i, run in enumerate(runs):
        part = value[i * n:(i + 1) * n]
        ref[run, :] = ref[run, :] + part if add else part


def _deinterleave(dst, src, n):
    seg = n // DEINT
    for r in range(DEINT):
        dst[r * seg:(r + 1) * seg, :] = src[pl.ds(r, seg, stride=DEINT), :]


def _interleave(dst, src, n, offset=0):
    seg = n // DEINT
    stride = src.shape[0] // DEINT
    for r in range(DEINT):
        dst[pl.ds(r, seg, stride=DEINT), :] = src[r * stride + offset:r * stride + offset + seg, :]


def _attn_fwd_call(qn, kn, v, bias):
    s, w = qn.shape
    nsb = s // ATT_SUPER
    npair = w // LANES

    def body(q_ref, kc_ref, kp_ref, vc_ref, vp_ref, b_ref, o_ref, lse_ref, tmp, qf, kf, vf, acc_s, m_s, l_s):
        sb = pl.program_id(1)
        tmp[0:ATT_SUPER, :] = q_ref[...].astype(F32)
        _deinterleave(qf, tmp, ATT_SUPER)
        tmp[0:ATT_SUPER, :] = kp_ref[...].astype(F32)
        tmp[ATT_SUPER:, :] = kc_ref[...].astype(F32)
        _deinterleave(kf, tmp, 2 * ATT_SUPER)
        tmp[0:ATT_SUPER, :] = vp_ref[...].astype(F32)
        tmp[ATT_SUPER:, :] = vc_ref[...].astype(F32)
        _deinterleave(vf, tmp, 2 * ATT_SUPER)
        lo = _low_half()
        for p, (_, dl) in enumerate(DILATED_PATTERNS):
            def unit(u, carry, p=p, dl=dl):
                b, rows_q, rows_k = _unit_rows(u, dl)
                qp = _take(qf, rows_q).astype(MXU_DTYPE)
                kcat = _take(kf, rows_k).astype(MXU_DTYPE)
                vcat = _take(vf, rows_k).astype(MXU_DTYPE)
                zero = jnp.zeros_like(qp)
                q2 = jnp.concatenate([jnp.where(lo, qp, zero), jnp.where(lo, zero, qp)], axis=0)
                sc = _mm_nt(q2, kcat) + _masked_bias(b_ref, p, sb * (ATT_UNITS // dl) + b)
                m2 = jnp.max(sc, axis=-1, keepdims=True)
                pr = jnp.exp(sc - m2)
                l2 = jnp.sum(pr, axis=-1, keepdims=True)
                acc2 = _mm(pr.astype(MXU_DTYPE), vcat)
                acc = jnp.where(lo, acc2[:ATT_BLOCK], acc2[ATT_BLOCK:])
                m = jnp.where(lo, m2[:ATT_BLOCK], m2[ATT_BLOCK:])
                l = jnp.where(lo, l2[:ATT_BLOCK], l2[ATT_BLOCK:])
                if p == 0:
                    _put(acc_s, rows_q, acc)
                    _put(m_s, rows_q, m)
                    _put(l_s, rows_q, l)
                else:
                    m_old = _take(m_s, rows_q)
                    m_new = jnp.maximum(m_old, m)
                    a_old = jnp.exp(m_old - m_new)
                    a_new = jnp.exp(m - m_new)
                    _put(acc_s, rows_q, a_old * _take(acc_s, rows_q) + a_new * acc)
                    _put(l_s, rows_q, a_old * _take(l_s, rows_q) + a_new * l)
                    _put(m_s, rows_q, m_new)
                return carry

            lax.fori_loop(0, ATT_UNITS, unit, 0, unroll=16)
        l = l_s[...]
        acc_s[...] = acc_s[...] / l
        m_s[...] = m_s[...] + jnp.log(l)
        _interleave(o_ref, acc_s, ATT_SUPER)
        _interleave(lse_ref, m_s, ATT_SUPER)

    cur = pl.BlockSpec((ATT_SUPER, LANES), lambda j, t: (t, j))
    prev = pl.BlockSpec((ATT_SUPER, LANES), lambda j, t: (jnp.maximum(t - 1, 0), j))
    bspec = pl.BlockSpec((len(DILATED_PATTERNS), 2, 2, ATT_BLOCK, 2 * ATT_BLOCK), lambda j, t: (0, 0, j, 0, 0))
    return pl.pallas_call(
        body, name="attn_fwd",
        grid=(npair, nsb),
        in_specs=[cur, cur, prev, cur, prev, bspec],
        out_specs=[cur, cur],
        out_shape=[jax.ShapeDtypeStruct((s, w), F32), jax.ShapeDtypeStruct((s, w), F32)],
        scratch_shapes=[pltpu.VMEM((2 * ATT_SUPER, LANES), F32),
                        pltpu.VMEM((ATT_SUPER, LANES), F32), pltpu.VMEM((2 * ATT_SUPER, LANES), F32),
                        pltpu.VMEM((2 * ATT_SUPER, LANES), F32), pltpu.VMEM((ATT_SUPER, LANES), F32),
                        pltpu.VMEM((ATT_SUPER, LANES), F32), pltpu.VMEM((ATT_SUPER, LANES), F32)],
        compiler_params=_params(("arbitrary", "arbitrary")),
    )(qn, kn, kn, v, v, bias)


def _attn_bwd_call(qn, kn, v, do, lse, delta, bias, dep=None):
    s, w = qn.shape
    nsb = s // ATT_SUPER
    npair = w // LANES
    deps = [] if dep is None else [dep]

    def body(q_ref, kc_ref, kp_ref, vc_ref, vp_ref, do_ref, lse_ref, dlt_ref, b_ref, *rest):
        dq_ref, dk_ref, dv_ref, db_ref, tmp, qf, kf, vf, dof, lsef, dltf, dqf, dkf, dvf = rest[len(deps):]
        step = pl.program_id(1)
        sb = nsb - 1 - step
        seg = ATT_SUPER // DEINT
        tmp[0:ATT_SUPER, :] = q_ref[...].astype(F32)
        _deinterleave(qf, tmp, ATT_SUPER)
        tmp[0:ATT_SUPER, :] = do_ref[...].astype(F32)
        _deinterleave(dof, tmp, ATT_SUPER)
        tmp[0:ATT_SUPER, :] = kp_ref[...].astype(F32)
        tmp[ATT_SUPER:, :] = kc_ref[...].astype(F32)
        _deinterleave(kf, tmp, 2 * ATT_SUPER)
        tmp[0:ATT_SUPER, :] = vp_ref[...].astype(F32)
        tmp[ATT_SUPER:, :] = vc_ref[...].astype(F32)
        _deinterleave(vf, tmp, 2 * ATT_SUPER)
        _deinterleave(lsef, lse_ref, ATT_SUPER)
        _deinterleave(dltf, dlt_ref, ATT_SUPER)

        @pl.when(step == 0)
        def _():
            db_ref[...] = jnp.zeros(db_ref.shape, F32)

        for acc in (dkf, dvf):
            for r in range(DEINT):
                this, before = pl.ds((2 * r + 1) * seg, seg), pl.ds(2 * r * seg, seg)

                @pl.when(step == 0)
                def _(acc=acc, this=this):
                    acc[this, :] = jnp.zeros((seg, LANES), F32)

                @pl.when(step > 0)
                def _(acc=acc, this=this, before=before):
                    acc[this, :] = acc[before, :]

                acc[before, :] = jnp.zeros((seg, LANES), F32)
        lo = _low_half()
        for p, (_, dl) in enumerate(DILATED_PATTERNS):
            def unit(u, carry, p=p, dl=dl):
                b, rows_q, rows_k = _unit_rows(u, dl)
                qp = _take(qf, rows_q).astype(MXU_DTYPE)
                dop = _take(dof, rows_q).astype(MXU_DTYPE)
                kcat = _take(kf, rows_k).astype(MXU_DTYPE)
                vcat = _take(vf, rows_k).astype(MXU_DTYPE)
                lse2 = _take(lsef, rows_q)
                dlt2 = _take(dltf, rows_q)
                zero = jnp.zeros_like(qp)
                q2 = jnp.concatenate([jnp.where(lo, qp, zero), jnp.where(lo, zero, qp)], axis=0)
                do2 = jnp.concatenate([jnp.where(lo, dop, zero), jnp.where(lo, zero, dop)], axis=0)
                lse_c = jnp.concatenate([lse2[:, 0:1], lse2[:, HEAD_DIM:HEAD_DIM + 1]], axis=0)
                dlt_c = jnp.concatenate([dlt2[:, 0:1], dlt2[:, HEAD_DIM:HEAD_DIM + 1]], axis=0)
                sc = _mm_nt(q2, kcat) + _masked_bias(b_ref, p, sb * (ATT_UNITS // dl) + b)
                pr = jnp.exp(sc - lse_c)
                ds = pr * (_mm_nt(do2, vcat) - dlt_c)
                db_ref[p] += ds.reshape(2, ATT_BLOCK, 2 * ATT_BLOCK)
                ds_c = ds.astype(MXU_DTYPE)
                dq2 = _mm(ds_c, kcat)
                dk = _mm_tn(ds_c, q2)
                dv = _mm_tn(pr.astype(MXU_DTYPE), do2)
                dq = jnp.where(lo, dq2[:ATT_BLOCK], dq2[ATT_BLOCK:])
                _put(dqf, rows_q, dq, add=p > 0)
                _put(dkf, rows_k, dk, add=True)
                _put(dvf, rows_k, dv, add=True)
                return carry

            lax.fori_loop(0, ATT_UNITS, unit, 0, unroll=16)
        _interleave(dq_ref, dqf, ATT_SUPER)
        _interleave(dk_ref, dkf, ATT_SUPER, offset=seg)
        _interleave(dv_ref, dvf, ATT_SUPER, offset=seg)

    cur = pl.BlockSpec((ATT_SUPER, LANES), lambda j, t: (nsb - 1 - t, j))
    prev = pl.BlockSpec((ATT_SUPER, LANES), lambda j, t: (jnp.maximum(nsb - 2 - t, 0), j))
    npat = len(DILATED_PATTERNS)
    bspec = pl.BlockSpec((npat, 2, 2, ATT_BLOCK, 2 * ATT_BLOCK), lambda j, t: (0, 0, j, 0, 0))
    dbspec = pl.BlockSpec((npat, 2, ATT_BLOCK, 2 * ATT_BLOCK), lambda j, t: (0, j, 0, 0))
    sup = lambda: pltpu.VMEM((ATT_SUPER, LANES), F32)
    sup2 = lambda: pltpu.VMEM((2 * ATT_SUPER, LANES), F32)
    return pl.pallas_call(
        body, name="attn_bwd",
        grid=(npair, nsb),
        in_specs=[cur, cur, prev, cur, prev, cur, cur, cur, bspec] + [ANY] * len(deps),
        out_specs=[cur, cur, cur, dbspec],
        out_shape=[jax.ShapeDtypeStruct((s, w), F32)] * 3
        + [jax.ShapeDtypeStruct((npat, N_HEADS, ATT_BLOCK, 2 * ATT_BLOCK), F32)],
        scratch_shapes=[sup2(), sup(), sup2(), sup2(), sup(), sup(), sup(), sup(), sup2(), sup2()],
        compiler_params=_params(("arbitrary", "arbitrary")),
    )(qn, kn, kn, v, v, do, lse, delta, bias, *deps)


def _bias_table_call(rel_bias, buckets, prev_keys):
    npat = buckets.shape[0]

    def body(rb_ref, bk_ref, pk_ref, out_ref):
        for p in range(npat):
            for half in range(2):
                ks = slice(half * ATT_BLOCK, (half + 1) * ATT_BLOCK)
                bk = bk_ref[p, :, ks]
                absent = pk_ref[p, :, ks] != 0
                for h in range(N_HEADS):
                    def pick(b, acc, h=h, bk=bk):
                        return jnp.where(bk == b, rb_ref[b, h], acc)

                    tab = lax.fori_loop(0, N_BUCKETS, pick, jnp.full((ATT_BLOCK, ATT_BLOCK), NEG_INF, F32))
                    out_ref[p, 1, h, :, ks] = tab
                    out_ref[p, 0, h, :, ks] = jnp.where(absent, NEG_INF, tab)

    vmem = pl.BlockSpec(memory_space=pltpu.VMEM)
    return pl.pallas_call(
        body, name="bias_table",
        in_specs=[pl.BlockSpec(memory_space=pltpu.SMEM), vmem, vmem],
        out_shape=jax.ShapeDtypeStruct((npat, 2, N_HEADS, ATT_BLOCK, 2 * ATT_BLOCK), F32),
        compiler_params=_params(),
    )(rel_bias, buckets, prev_keys)


def _rel_bias_grad_call(dbias, buckets):
    npat, nh = dbias.shape[0], dbias.shape[1]

    def body(db_ref, bk_ref, out_ref):
        lane = lax.broadcasted_iota(jnp.int32, (nh, LANES), 1)
        out = jnp.zeros((nh, LANES), F32)
        for b in range(N_BUCKETS):
            tot = jnp.zeros((nh, 1), F32)
            for p in range(npat):
                hit = jnp.where(bk_ref[p][None] == b, db_ref[p], 0.0)
                tot = tot + jnp.sum(jnp.sum(hit, axis=2), axis=1, keepdims=True)
            out = jnp.where(lane == b, tot, out)
        out_ref[...] = out

    return pl.pallas_call(
        body, name="rel_bias_grad",
        out_shape=jax.ShapeDtypeStruct((nh, LANES), F32),
        compiler_params=_params(),
    )(dbias, buckets)


def _f2_call(x, tgt, ypool, o, wout, wup, wdown, g2, tm):
    s, d = x.shape
    nblk = s // tm
    nch, _, fch = wup.shape
    dff = nch * fch
    mixw = POOL_WIDTH + ATTN_WIDTH

    def body(x_ref, t_ref, yp_ref, o_ref, g2_ref, wout_hbm, wup_hbm, wdown_hbm,
             mixed_ref, c_ref, ff_ref, dz_ref, dy_ref, dh1_ref, dyp_ref, do_ref, dlt_ref, dg2_ref, loss_ref,
             wout_v, wup_v, wdown_v, rz):
        i = pl.program_id(0)

        @pl.when(i == 0)
        def _():
            pltpu.sync_copy(wout_hbm, wout_v)
            pltpu.sync_copy(wup_hbm, wup_v)
            pltpu.sync_copy(wdown_hbm, wdown_v)
            dg2_ref[...] = jnp.zeros(dg2_ref.shape, F32)
            loss_ref[...] = jnp.zeros(loss_ref.shape, F32)

        o = o_ref[...]
        mixed = jnp.concatenate([yp_ref[...], o.astype(MXU_DTYPE)], axis=-1)
        mixed_ref[...] = mixed
        h1 = x_ref[...] + _mm(mixed, wout_v[...])
        r2 = lax.rsqrt(jnp.mean(h1 * h1, axis=-1, keepdims=True) + NORM_EPS)
        hn = h1 * r2
        c = (hn * g2_ref[...]).astype(MXU_DTYPE)
        c_ref[...] = c
        y = h1
        for j in range(nch):
            cs = slice(j * fch, (j + 1) * fch)
            z = jnp.maximum(_mm(c, wup_v[j]), 0.0)
            rz[:, cs] = z
            ff = (z * z).astype(MXU_DTYPE)
            ff_ref[:, cs] = ff
            y = y + _mm(ff, wdown_v[j])
        err = y - t_ref[...]
        loss_ref[...] += jnp.sum(err * err) * (0.5 / d)
        dy = err * (1.0 / d)
        dy_c = dy.astype(MXU_DTYPE)
        dy_ref[...] = dy_c
        dc = jnp.zeros((tm, d), F32)
        for j in range(nch):
            cs = slice(j * fch, (j + 1) * fch)
            dz = (_mm_nt(dy_c, wdown_v[j]) * (2.0 * rz[:, cs])).astype(MXU_DTYPE)
            dz_ref[:, cs] = dz
            dc = dc + _mm_nt(dz, wup_v[j])
        dg2_ref[...] += jnp.sum(dc * hn, axis=0, keepdims=True)
        dh1 = dy + _rms_bwd(dc * g2_ref[...], hn, r2)
        dh1_ref[...] = dh1
        dmix = _mm_nt(dh1.astype(MXU_DTYPE), wout_v[...])
        dyp_ref[...] = dmix[:, :POOL_WIDTH]
        do = dmix[:, POOL_WIDTH:]
        do_ref[...] = do.astype(MXU_DTYPE)
        dlt_ref[...] = _head_sum_bcast(do * o)

    tok = lambda w: pl.BlockSpec((tm, w), lambda i: (i, 0))
    const = lambda shp: pl.BlockSpec(shp, lambda i: (0,) * len(shp))
    return pl.pallas_call(
        body, name="fwd_mlp_bwd_mlp",
        grid=(nblk,),
        in_specs=[tok(d), tok(d), tok(POOL_WIDTH), tok(ATTN_WIDTH), const((1, d)), ANY, ANY, ANY],
        out_specs=[tok(mixw), tok(d), tok(dff), tok(dff), tok(d), tok(d), tok(POOL_WIDTH), tok(ATTN_WIDTH),
                   tok(ATTN_WIDTH), const((1, d)), const((1, LANES))],
        out_shape=[jax.ShapeDtypeStruct((s, mixw), MXU_DTYPE),
                   jax.ShapeDtypeStruct((s, d), MXU_DTYPE),
                   jax.ShapeDtypeStruct((s, dff), MXU_DTYPE),
                   jax.ShapeDtypeStruct((s, dff), MXU_DTYPE),
                   jax.ShapeDtypeStruct((s, d), MXU_DTYPE),
                   jax.ShapeDtypeStruct((s, d), F32),
                   jax.ShapeDtypeStruct((s, POOL_WIDTH), F32),
                   jax.ShapeDtypeStruct((s, ATTN_WIDTH), MXU_DTYPE),
                   jax.ShapeDtypeStruct((s, ATTN_WIDTH), F32),
                   jax.ShapeDtypeStruct((1, d), F32),
                   jax.ShapeDtypeStruct((1, LANES), F32)],
        scratch_shapes=[pltpu.VMEM(wout.shape, MXU_DTYPE), pltpu.VMEM(wup.shape, MXU_DTYPE),
                        pltpu.VMEM(wdown.shape, MXU_DTYPE), pltpu.VMEM((tm, dff), F32)],
        compiler_params=_params(("arbitrary",)),
    )(x, tgt, ypool, o, g2, wout, wup, wdown)


def _bproj_call(dqn, dkn, dv, q32, k32, dypool, pooled, x, dh1, win, poolw, pscale, qg, kg, g1, tm):
    s, d = x.shape
    nblk = s // tm
    ngrp = len(POOL_WINDOWS)

    def body(dqn_ref, dkn_ref, dv_ref, q_ref, k_ref, dyp_ref, pooled_ref, x_ref, dh1_ref,
             win_hbm, pw_ref, ps_ref, qg_ref, kg_ref, g1_ref,
             dx_ref, dproj_ref, dg1_ref, dqg_ref, dkg_ref, dpw_ref, dps_ref, win_v, ebuf):
        step = pl.program_id(0)
        i = nblk - 1 - step

        @pl.when(step == 0)
        def _():
            pltpu.sync_copy(win_hbm, win_v)
            dg1_ref[...] = jnp.zeros(dg1_ref.shape, F32)
            dqg_ref[...] = jnp.zeros(dqg_ref.shape, F32)
            dkg_ref[...] = jnp.zeros(dkg_ref.shape, F32)
            dpw_ref[...] = jnp.zeros(dpw_ref.shape, F32)
            dps_ref[...] = jnp.zeros(dps_ref.shape, F32)
            ebuf[tm:tm + POOL_HALO, :] = jnp.zeros((POOL_HALO, POOL_WIDTH), F32)

        @pl.when(step > 0)
        def _():
            ebuf[tm:tm + POOL_HALO, :] = ebuf[0:POOL_HALO, :]

        def qk_bwd(dn_sum, raw, gain, scale, dgain_ref):
            rr = lax.rsqrt(_head_sum_bcast(raw * raw) * (1.0 / HEAD_DIM) + NORM_EPS)
            hn = raw * rr
            dgain_ref[...] += jnp.sum(dn_sum * hn, axis=0, keepdims=True) * scale
            dn = dn_sum * (gain * scale)
            return rr * (dn - hn * (_head_sum_bcast(dn * hn) * (1.0 / HEAD_DIM)))

        dq = qk_bwd(dqn_ref[...], q_ref[...], qg_ref[...], HEAD_DIM ** -0.5, dqg_ref)
        dk = qk_bwd(dkn_ref[...], k_ref[...], kg_ref[...], 1.0, dkg_ref)

        t = i * tm + lax.broadcasted_iota(jnp.int32, (tm, 1), 0)
        dpooled = []
        for g, w in enumerate(POOL_WINDOWS):
            ls = slice(g * LANES, (g + 1) * LANES)
            dm = dyp_ref[:, ls]
            pg = pooled_ref[:, ls]
            dps_ref[:, ls] += jnp.sum(dm * _mm(pg, pw_ref[g]), axis=0, keepdims=True)
            dms = (dm * ps_ref[:, ls]).astype(MXU_DTYPE)
            dpw_ref[g] += _mm_tn(pg, dms)
            dpg = _mm_nt(dms, pw_ref[g])
            dpooled.append(dpg)
            ebuf[0:tm, ls] = dpg / jnp.minimum(t + 1, w).astype(F32)
        du = []
        for g, w in enumerate(POOL_WINDOWS):
            ls = slice(g * LANES, (g + 1) * LANES)
            acc = ebuf[0:tm, ls]
            for sh in range(1, w):
                acc = acc + ebuf[sh:sh + tm, ls]
            du.append(acc - dpooled[g])
        parts = [jnp.concatenate(du, axis=-1), dq, dk, dv_ref[...]]
        da = jnp.zeros((tm, d), F32)
        for p, part in enumerate(parts):
            pc = part.astype(MXU_DTYPE)
            dproj_ref[:, p * POOL_WIDTH:(p + 1) * POOL_WIDTH] = pc
            da = da + _mm_nt(pc, win_v[p])
        xv = x_ref[...]
        r = lax.rsqrt(jnp.mean(xv * xv, axis=-1, keepdims=True) + NORM_EPS)
        xn = xv * r
        dg1_ref[...] += jnp.sum(da * xn, axis=0, keepdims=True)
        dx_ref[...] = dh1_ref[...] + _rms_bwd(da * g1_ref[...], xn, r)

    tok = lambda w: pl.BlockSpec((tm, w), lambda t: (nblk - 1 - t, 0))
    const = lambda shp: pl.BlockSpec(shp, lambda t: (0,) * len(shp))
    return pl.pallas_call(
        body, name="bwd_inproj",
        grid=(nblk,),
        in_specs=[tok(ATTN_WIDTH)] * 5 + [tok(POOL_WIDTH), tok(POOL_WIDTH), tok(d), tok(d),
                                          ANY, const(poolw.shape), const((1, POOL_WIDTH)), const((1, ATTN_WIDTH)),
                                          const((1, ATTN_WIDTH)), const((1, d))],
        out_specs=[tok(d), tok(4 * POOL_WIDTH), const((1, d)), const((1, ATTN_WIDTH)), const((1, ATTN_WIDTH)),
                   const((ngrp, LANES, LANES)), const((1, POOL_WIDTH))],
        out_shape=[jax.ShapeDtypeStruct((s, d), F32),
                   jax.ShapeDtypeStruct((s, 4 * POOL_WIDTH), MXU_DTYPE),
                   jax.ShapeDtypeStruct((1, d), F32),
                   jax.ShapeDtypeStruct((1, ATTN_WIDTH), F32),
                   jax.ShapeDtypeStruct((1, ATTN_WIDTH), F32),
                   jax.ShapeDtypeStruct((ngrp, LANES, LANES), F32),
                   jax.ShapeDtypeStruct((1, POOL_WIDTH), F32)],
        scratch_shapes=[pltpu.VMEM(win.shape, MXU_DTYPE), pltpu.VMEM((tm + POOL_HALO, POOL_WIDTH), F32)],
        compiler_params=_params(("arbitrary",)),
    )(dqn, dkn, dv, q32, k32, dypool, pooled, x, dh1, win, poolw, pscale, qg, kg, g1)


def _wgrad_call(a, b, bm, bn, bk, out_shape, out_block, out_index, name):
    s, m = a.shape
    _, n = b.shape
    nk = s // bk

    def body(a_ref, b_ref, o_ref, wire_ref):
        k = pl.program_id(2)

        @pl.when(k == 0)
        def _():
            o_ref[...] = jnp.zeros(o_ref.shape, F32)

        o_ref[...] += _mm_tn(a_ref[...].astype(MXU_DTYPE), b_ref[...].astype(MXU_DTYPE))

        @pl.when(k == nk - 1)
        def _():
            wire_ref[...] = o_ref[...].astype(WIRE_DTYPE)

    return pl.pallas_call(
        body, name=name,
        grid=(m // bm, n // bn, nk),
        in_specs=[pl.BlockSpec((bk, bm), lambda i, j, k: (k, i)), pl.BlockSpec((bk, bn), lambda i, j, k: (k, j))],
        out_specs=[pl.BlockSpec(out_block, out_index)] * 2,
        out_shape=[jax.ShapeDtypeStruct(out_shape, F32), jax.ShapeDtypeStruct(out_shape, WIRE_DTYPE)],
        compiler_params=_params(("arbitrary", "arbitrary", "arbitrary")),
    )(a, b)


def _local_grads(x, tgt, g1, win, poolw, pscale, qg, kg, rel_bias, g2, mlp_weights, on_mlp_grads=None):
    s, d = x.shape
    g1r, g2r = g1.reshape(1, d), g2.reshape(1, d)
    psr = pscale.reshape(1, POOL_WIDTH)
    qgr = jnp.tile(qg, N_HEADS).reshape(1, ATTN_WIDTH)
    kgr = jnp.tile(kg, N_HEADS).reshape(1, ATTN_WIDTH)
    pw_c = poolw.astype(MXU_DTYPE)
    buckets = jnp.asarray(_bucket_tables())
    bias = _bias_table_call(rel_bias, buckets, jnp.asarray(_previous_block_keys().astype(np.int32)))
    bk = min(s, 2048)

    a, pooled, ypool, q32, k32, qn, kn, v = _f1_call(x, g1r, win, pw_c, psr, qgr, kgr, tm=512)
    o, lse = _attn_fwd_call(qn, kn, v, bias)
    wout, wup, wdown = mlp_weights(o)
    mixed, c, ff, dz, dy, dh1, dypool, do, delta, dg2, loss = _f2_call(x, tgt, ypool, o, wout, wup, wdown, g2r, tm=256)
    dff = ff.shape[1]
    g_out = [g.reshape(N_CHIPS, d // N_CHIPS, d)
             for g in _wgrad_call(mixed, dh1, d, d, bk // 2, (d, d), (d, d), lambda i, j, k: (0, 0), "wgrad_out")]
    g_up = _wgrad_call(c, dz, d, dff // N_CHIPS, bk, (N_CHIPS, d, dff // N_CHIPS), (None, d, dff // N_CHIPS),
                       lambda i, j, k: (j, 0, 0), "wgrad_up")
    g_down = _wgrad_call(ff, dy, dff // N_CHIPS, d, bk, (N_CHIPS, dff // N_CHIPS, d), (None, dff // N_CHIPS, d),
                         lambda i, j, k: (i, 0, 0), "wgrad_down")
    dep = None if on_mlp_grads is None else on_mlp_grads(g_out[1], g_up[1], g_down[1])
    dqn, dkn, dv, dbias = _attn_bwd_call(qn, kn, v, do, lse, delta, bias, dep)
    dx, dproj, dg1, dqg, dkg, dpw, dps = _bproj_call(
        dqn, dkn, dv, q32, k32, dypool, pooled, x, dh1, win, pw_c, psr, qgr, kgr, g1r, tm=256)
    nin = dproj.shape[1] // N_CHIPS
    g_in = _wgrad_call(a, dproj, d, nin, bk, (N_CHIPS, d, nin), (None, d, nin), lambda i, j, k: (j, 0, 0), "wgrad_in")
    drb = _rel_bias_grad_call(dbias, buckets)
    small = dict(
        mix_norm_g=dg1.reshape(d), mlp_norm_g=dg2.reshape(d), pool_scale=dps.reshape(POOL_WIDTH),
        q_norm_g=dqg.reshape(ATTN_WIDTH), k_norm_g=dkg.reshape(ATTN_WIDTH),
        rel_bias=drb[:, :N_BUCKETS].T, pool_w=dpw)
    return loss[0, 0], dx, (g_in, g_out, g_up, g_down), small


def _coords():
    return lax.axis_index("x"), lax.axis_index("y"), lax.axis_index("c")


def _other_chips(x, y):
    return [(1 - x, y), (x, 1 - y), (1 - x, 1 - y)]


def _remote(src, dst, send_sem, recv_sem, dev):
    return pltpu.make_async_remote_copy(src_ref=src, dst_ref=dst, send_sem=send_sem, recv_sem=recv_sem,
                                        device_id=dev, device_id_type=MESH)


def _halves(a):
    return a.reshape(a.shape[:-2] + (2, a.shape[-2] // 2, a.shape[-1]))


def _place_shards_call(shards, chip_idx, nch):
    nw = len(shards)

    def body(chip_ref, *refs):
        for w in range(nw):
            refs[nw + w][...] = refs[w][...].astype(WIRE_DTYPE)

    in_specs = [pl.BlockSpec((s.shape[0] // nch, s.shape[1]), lambda i, chip_ref: (i, 0)) for s in shards]
    out_specs = [pl.BlockSpec((None, s.shape[0] // nch, s.shape[1]), lambda i, chip_ref: (chip_ref[0], i, 0))
                 for s in shards]
    return pl.pallas_call(
        body, name="weights_place",
        grid_spec=pltpu.PrefetchScalarGridSpec(num_scalar_prefetch=1, grid=(nch,),
                                               in_specs=in_specs, out_specs=out_specs),
        out_shape=[jax.ShapeDtypeStruct((N_CHIPS,) + s.shape, WIRE_DTYPE) for s in shards],
        compiler_params=_params(("arbitrary",)),
    )(chip_idx, *shards)


def _allgather_call(placed, from_chips, name):
    nw = len(placed)
    ncp = 3 * nw

    def body(*refs):
        outs = refs[nw:2 * nw]
        send1, recv1, send2, recv2 = refs[2 * nw:]
        x, y, c = _coords()
        chip = 2 * x + y
        others = _other_chips(x, y)
        first, passed = [], []
        if from_chips:
            for w in range(nw):
                for k, (ox, oy) in enumerate(others):
                    mine = outs[w].at[chip, c]
                    cp = _remote(mine, mine, send1.at[3 * w + k], recv1.at[3 * w + k], (ox, oy, c))
                    cp.start()
                    first.append(cp)
        for w in range(nw):
            for k, (ox, oy) in enumerate(others):
                piece = outs[w].at[2 * ox + oy, c]
                if from_chips:
                    _remote(piece, piece, send1.at[3 * w + k], recv1.at[3 * w + k], (ox, oy, c)).wait_recv()
                cp = _remote(piece, piece, send2.at[3 * w + k], recv2.at[3 * w + k], (x, y, 1 - c))
                cp.start()
                passed.append(cp)
        for w in range(nw):
            for k, (ox, oy) in enumerate(others):
                piece = outs[w].at[2 * ox + oy, 1 - c]
                _remote(piece, piece, send2.at[3 * w + k], recv2.at[3 * w + k], (x, y, 1 - c)).wait_recv()
        for cp in first + passed:
            cp.wait_send()

    return pl.pallas_call(
        body, name=name,
        in_specs=[ANY] * nw, out_specs=[ANY] * nw,
        out_shape=[jax.ShapeDtypeStruct(s.shape, s.dtype) for s in placed],
        input_output_aliases={w: w for w in range(nw)},
        scratch_shapes=[pltpu.SemaphoreType.DMA((ncp,))] * 4,
    )(*placed)


HBM_SPEC = pl.BlockSpec(memory_space=pltpu.HBM)
SEM_SPEC = pl.BlockSpec(memory_space=pltpu.SEMAPHORE)
SPLIT_EFFECT = pltpu.SideEffectType.DATAFLOW_SIDE_EFFECTING


def _in_hbm(a):
    return pltpu.with_memory_space_constraint(a, pltpu.HBM)


def _gather_copies(bufs, send, recv):
    x, y, c = _coords()
    chip = 2 * x + y
    cps = []
    for w, buf in enumerate(bufs):
        for k, (ox, oy) in enumerate(_other_chips(x, y)):
            mine, theirs = buf.at[chip, c], buf.at[2 * ox + oy, c]
            sems = (send.at[3 * w + k], recv.at[3 * w + k], (ox, oy, c))
            cps.append((_remote(mine, mine, *sems), _remote(theirs, theirs, *sems)))
    return cps


def _gather_start_call(bufs, after):
    nw = len(bufs)

    def body(*refs):
        ins, send, recv, token = refs[:nw], refs[nw + 1], refs[nw + 2], refs[2 * nw + 3]
        for out, _ in _gather_copies(ins, send, recv):
            out.start()
        token[...] = jnp.zeros(token.shape, F32)

    res = pl.pallas_call(
        body, name="weights_gather_start",
        in_specs=[HBM_SPEC] * nw + [ANY],
        out_specs=[SEM_SPEC, SEM_SPEC] + [HBM_SPEC] * nw + [pl.BlockSpec(memory_space=pltpu.VMEM)],
        out_shape=[pltpu.SemaphoreType.DMA((3 * nw,)), pltpu.SemaphoreType.DMA((3 * nw,))]
        + [pltpu.HBM(b.shape, b.dtype) for b in bufs] + [jax.ShapeDtypeStruct((8, LANES), F32)],
        input_output_aliases={w: 2 + w for w in range(nw)},
        compiler_params=pltpu.CompilerParams(has_side_effects=SPLIT_EFFECT),
    )(*[_in_hbm(b) for b in bufs], after)
    return res[0], res[1], list(res[2:2 + nw]), res[2 + nw]


def _gather_wait_call(bufs, send, recv, after):
    nw = len(bufs)

    def body(*refs):
        ins, send, recv = refs[:nw], refs[nw], refs[nw + 1]
        for out, back in _gather_copies(ins, send, recv):
            out.wait_send()
            back.wait_recv()

    return pl.pallas_call(
        body, name="weights_gather_wait",
        in_specs=[HBM_SPEC] * nw + [SEM_SPEC, SEM_SPEC, ANY],
        out_specs=[HBM_SPEC] * nw,
        out_shape=[pltpu.HBM(b.shape, b.dtype) for b in bufs],
        input_output_aliases={w: w for w in range(nw)},
        compiler_params=pltpu.CompilerParams(has_side_effects=SPLIT_EFFECT),
    )(*bufs, send, recv, after)


def _scatter_copies(srcs, lands, send, recv, wholes):
    x, y, c = _coords()
    me = 4 * x + 2 * y + c
    cps = []
    for w, (src, land) in enumerate(zip(srcs, lands)):
        for r in range(1, N_DEV):
            px, py, pc = ((1 - x) if r & 4 else x, (1 - y) if r & 2 else y, (1 - c) if r & 1 else c)
            sems = (send.at[(N_DEV - 1) * w + r - 1], recv.at[(N_DEV - 1) * w + r - 1], (px, py, pc))
            piece = src if wholes[w] else src.at[2 * px + py, pc]
            cps.append((_remote(piece, land.at[me], *sems), _remote(piece, land.at[4 * px + 2 * py + pc], *sems)))
    return cps


def _scatter_start_call(srcs, lands, wholes, name):
    nw = len(srcs)
    ncp = (N_DEV - 1) * nw

    def body(*refs):
        ins, lnd, send, recv, token = refs[:nw], refs[nw:2 * nw], refs[2 * nw], refs[2 * nw + 1], refs[4 * nw + 2]
        for out, _ in _scatter_copies(ins, lnd, send, recv, wholes):
            out.start()
        token[...] = jnp.zeros(token.shape, F32)

    res = pl.pallas_call(
        body, name=name,
        in_specs=[HBM_SPEC] * (2 * nw),
        out_specs=[SEM_SPEC, SEM_SPEC] + [HBM_SPEC] * (2 * nw) + [pl.BlockSpec(memory_space=pltpu.VMEM)],
        out_shape=[pltpu.SemaphoreType.DMA((ncp,)), pltpu.SemaphoreType.DMA((ncp,))]
        + [pltpu.HBM(b.shape, b.dtype) for b in list(srcs) + list(lands)] + [jax.ShapeDtypeStruct((8, LANES), F32)],
        input_output_aliases={i: 2 + i for i in range(2 * nw)},
        compiler_params=pltpu.CompilerParams(has_side_effects=SPLIT_EFFECT),
    )(*[_in_hbm(b) for b in list(srcs) + list(lands)])
    return res[0], res[1], list(res[2:2 + nw]), list(res[2 + nw:2 + 2 * nw]), res[2 + 2 * nw]


def _scatter_wait_call(srcs, lands, send, recv, after, wholes, name):
    nw = len(srcs)

    def body(*refs):
        ins, lnd, send, recv = refs[:nw], refs[nw:2 * nw], refs[2 * nw], refs[2 * nw + 1]
        for out, back in _scatter_copies(ins, lnd, send, recv, wholes):
            out.wait_send()
            back.wait_recv()

    res = pl.pallas_call(
        body, name=name,
        in_specs=[HBM_SPEC] * (2 * nw) + [SEM_SPEC, SEM_SPEC, ANY],
        out_specs=[HBM_SPEC] * (2 * nw),
        out_shape=[pltpu.HBM(b.shape, b.dtype) for b in list(srcs) + list(lands)],
        input_output_aliases={i: i for i in range(2 * nw)},
        compiler_params=pltpu.CompilerParams(has_side_effects=SPLIT_EFFECT),
    )(*srcs, *lands, send, recv, after)
    return list(res[nw:])


def _reduce_call(own, lands, idx, nch, name, dep=None):
    nw = len(own)
    deps = [] if dep is None else [dep]

    def body(idx_ref, *refs):
        refs = refs[:2 * nw] + refs[2 * nw + len(deps):]
        for w in range(nw):
            tot = refs[w][...]
            for r in range(1, N_DEV):
                tot = tot + refs[nw + w][idx_ref[1 + r]].astype(F32)
            refs[2 * nw + w][...] = tot

    in_specs, out_specs, out_shape = [], [], []
    for s in own:
        in_specs.append(pl.BlockSpec((None, None, s.shape[2] // nch, s.shape[3]),
                                     lambda i, idx_ref: (idx_ref[0], idx_ref[1], i, 0)))
    for s in own:
        in_specs.append(pl.BlockSpec((N_DEV, s.shape[2] // nch, s.shape[3]), lambda i, idx_ref: (0, i, 0)))
    for s in own:
        out_specs.append(pl.BlockSpec((None, s.shape[2] // nch, s.shape[3]), lambda i, idx_ref: (idx_ref[1], i, 0)))
        out_shape.append(jax.ShapeDtypeStruct((2,) + s.shape[2:], F32))
    return pl.pallas_call(
        body, name=name,
        grid_spec=pltpu.PrefetchScalarGridSpec(num_scalar_prefetch=1, grid=(nch,),
                                               in_specs=in_specs + [ANY] * len(deps), out_specs=out_specs),
        out_shape=out_shape,
        compiler_params=_params(("arbitrary",)),
    )(idx, *own, *lands, *deps)


def _pair_allgather_call(halves, name):
    nw = len(halves)

    def body(*refs):
        outs = refs[nw:2 * nw]
        send, recv = refs[2 * nw:]
        x, y, c = _coords()
        cps = []
        for w in range(nw):
            cp = _remote(outs[w].at[c], outs[w].at[c], send.at[w], recv.at[w], (x, y, 1 - c))
            cp.start()
            cps.append(cp)
        for w in range(nw):
            theirs = outs[w].at[1 - c]
            _remote(theirs, theirs, send.at[w], recv.at[w], (x, y, 1 - c)).wait_recv()
        for cp in cps:
            cp.wait_send()

    outs = pl.pallas_call(
        body, name=name,
        in_specs=[ANY] * nw, out_specs=[ANY] * nw,
        out_shape=[jax.ShapeDtypeStruct(h.shape, h.dtype) for h in halves],
        input_output_aliases={w: w for w in range(nw)},
        scratch_shapes=[pltpu.SemaphoreType.DMA((nw,))] * 2,
    )(*halves)
    return [o.reshape(2 * h.shape[1], h.shape[2]) for o, h in zip(outs, halves)]


def _adamw(w, g, m, v):
    m = ADAM_B1 * m + (1.0 - ADAM_B1) * g
    v = ADAM_B2 * v + (1.0 - ADAM_B2) * (g * g)
    m_hat = m / (1.0 - ADAM_B1 ** ADAM_STEP)
    v_hat = v / (1.0 - ADAM_B2 ** ADAM_STEP)
    delta = -ADAM_LR * (m_hat / (jnp.sqrt(v_hat) + ADAM_EPS) + ADAM_WD * w)
    return delta, m, v


def _adamw_call(ws, gs, ms, vs, nch, name):
    nw = len(ws)

    def body(*refs):
        for w in range(nw):
            g = refs[nw + w][...]
            delta, m, v = _adamw(refs[w][...], g, refs[2 * nw + w][...], refs[3 * nw + w][...])
            refs[4 * nw + w][...] = g
            refs[5 * nw + w][...] = delta
            refs[6 * nw + w][...] = m
            refs[7 * nw + w][...] = v

    specs = [pl.BlockSpec((a.shape[0] // nch, a.shape[1]), lambda i: (i, 0)) for a in ws]
    res = pl.pallas_call(
        body, name=name,
        grid=(nch,),
        in_specs=specs * 4, out_specs=specs * 4,
        out_shape=[jax.ShapeDtypeStruct(a.shape, F32) for a in ws] * 4,
        compiler_params=_params(("arbitrary",)),
    )(*ws, *gs, *ms, *vs)
    return res[:nw], res[nw:2 * nw], res[2 * nw:3 * nw], res[3 * nw:]


def _small_call(gathered, own, me_idx, w, m, v):
    def fold(row):
        tot = row[:, 0:LANES] + row[:, LANES:2 * LANES] + row[:, 2 * LANES:3 * LANES] + row[:, 3 * LANES:4 * LANES]
        return tot + pltpu.roll(tot, HEAD_DIM, axis=1)

    def body(me_ref, ga_ref, own_ref, w_ref, m_ref, v_ref, g_out, d_out, m_out, v_out):
        me = me_ref[0]
        term = lambda i: jnp.where(me == i, own_ref[...], ga_ref[i])
        g = term(0)
        for i in range(1, N_DEV):
            g = g + term(i)
        unfolded = g[4:5, :]
        folded = jnp.concatenate([fold(unfolded[:, :ATTN_WIDTH]), fold(unfolded[:, ATTN_WIDTH:]),
                                  jnp.zeros((1, 1024 - 2 * LANES), F32)], axis=-1)
        row = lax.broadcasted_iota(jnp.int32, g.shape, 0)
        g = jnp.where(row == 3, folded, g)
        delta, mm, vv = _adamw(w_ref[...], g, m_ref[...], v_ref[...])
        g_out[...] = g
        d_out[...] = delta
        m_out[...] = mm
        v_out[...] = vv

    vmem = pl.BlockSpec(memory_space=pltpu.VMEM)
    return pl.pallas_call(
        body, name="adamw_small",
        in_specs=[pl.BlockSpec(memory_space=pltpu.SMEM)] + [vmem] * 5,
        out_shape=[jax.ShapeDtypeStruct(w.shape, F32)] * 4,
        compiler_params=_params(),
    )(me_idx, gathered, own, w, m, v)


def _pack_small(p, folded=True, loss=None):
    z = lambda n: jnp.zeros((n,), F32)
    rows = [p["mix_norm_g"], p["mlp_norm_g"],
            jnp.concatenate([p["pool_scale"], p["rel_bias"].reshape(-1), z(1024 - POOL_WIDTH - N_BUCKETS * N_HEADS)])]
    if folded:
        rows += [jnp.concatenate([p["q_norm_g"], z(LANES - HEAD_DIM), p["k_norm_g"], z(1024 - LANES - HEAD_DIM)]), z(1024)]
    else:
        rows += [z(1024), jnp.concatenate([p["q_norm_g"], p["k_norm_g"]])]
    rows += [z(1024) if loss is None else jnp.concatenate([loss.reshape(1), z(1023)])]
    head = jnp.stack(rows + [z(1024)] * 2)
    return jnp.concatenate([head, p["pool_w"].reshape(-1, 1024)], axis=0)


def _unpack_small(a):
    return dict(
        mix_norm_g=a[0], mlp_norm_g=a[1], pool_scale=a[2, :POOL_WIDTH],
        rel_bias=a[2, POOL_WIDTH:POOL_WIDTH + N_BUCKETS * N_HEADS].reshape(N_BUCKETS, N_HEADS),
        q_norm_g=a[3, :HEAD_DIM], k_norm_g=a[3, LANES:LANES + HEAD_DIM],
        pool_w=a[8:].reshape(len(POOL_WINDOWS), LANES, LANES))


_WEIGHT_ORDER = ("mix_norm_g", "w_in", "pool_w", "pool_scale", "q_norm_g", "k_norm_g", "rel_bias", "w_out",
                 "mlp_norm_g", "w_up", "w_down")
_BIG = ("w_in", "w_out", "w_up", "w_down")


def kernel(x, mix_norm_g, w_in, pool_w, pool_scale, q_norm_g, k_norm_g, rel_bias, w_out, mlp_norm_g, w_up, w_down, loss_target, m_mix_norm_g, m_w_in, m_pool_w, m_pool_scale, m_q_norm_g, m_k_norm_g, m_rel_bias, m_w_out, m_mlp_norm_g, m_w_up, m_w_down, v_mix_norm_g, v_w_in, v_pool_w, v_pool_scale, v_q_norm_g, v_k_norm_g, v_rel_bias, v_w_out, v_mlp_norm_g, v_w_up, v_w_down):
    w = dict(mix_norm_g=mix_norm_g, w_in=w_in, pool_w=pool_w, pool_scale=pool_scale, q_norm_g=q_norm_g,
             k_norm_g=k_norm_g, rel_bias=rel_bias, w_out=w_out, mlp_norm_g=mlp_norm_g, w_up=w_up, w_down=w_down)
    m = dict(mix_norm_g=m_mix_norm_g, w_in=m_w_in, pool_w=m_pool_w, pool_scale=m_pool_scale, q_norm_g=m_q_norm_g,
             k_norm_g=m_k_norm_g, rel_bias=m_rel_bias, w_out=m_w_out, mlp_norm_g=m_mlp_norm_g, w_up=m_w_up, w_down=m_w_down)
    v = dict(mix_norm_g=v_mix_norm_g, w_in=v_w_in, pool_w=v_pool_w, pool_scale=v_pool_scale, q_norm_g=v_q_norm_g,
             k_norm_g=v_k_norm_g, rel_bias=v_rel_bias, w_out=v_w_out, mlp_norm_g=v_mlp_norm_g, w_up=v_w_up, w_down=v_w_down)
    xc, yc, cc = _coords()

    c_idx = jnp.reshape(cc, (1,)).astype(jnp.int32)
    chip_idx = jnp.reshape(2 * xc + yc, (1,)).astype(jnp.int32)
    me = 4 * xc + 2 * yc + cc
    whole = lambda t: t.reshape(t.shape[0], t.shape[1] * t.shape[2], t.shape[3])

    placed = [_halves(p) for p in _place_shards_call([w[n] for n in _BIG], chip_idx, nch=4)]
    (win_f,) = _allgather_call(placed[:1], from_chips=True, name="weights_allgather_in")
    wsend, wrecv, in_flight, started = _gather_start_call(placed[1:], win_f)

    def mlp_weights(after):
        landed = _gather_wait_call(in_flight, wsend, wrecv, after)
        wout_f, wup_f, wdown_f = _allgather_call(landed, from_chips=False, name="weights_pair_forward")
        return whole(wout_f).reshape(-1, wout_f.shape[-1]), whole(wup_f), whole(wdown_f)

    split = []

    def on_mlp_grads(*wire_grads):
        srcs = [_halves(g) for g in wire_grads]
        lands = [lax.empty((N_DEV,) + s.shape[2:], s.dtype) for s in srcs]
        split.extend(_scatter_start_call(srcs, lands, [False] * len(srcs), "grads_scatter_start"))
        return split[4]

    loss_part, dx, big_grads, small_grads = _local_grads(
        x[0], loss_target[0], mix_norm_g + started[0, 0], whole(win_f), pool_w, pool_scale, q_norm_g, k_norm_g, rel_bias,
        mlp_norm_g, mlp_weights, on_mlp_grads)
    g_in, g_out, g_up, g_down = big_grads
    gsend, grecv, srcs_thru, lands_thru, _ = split
    lands_mlp = _scatter_wait_call(srcs_thru, lands_thru, gsend, grecv, g_in[1], [False] * 3, "grads_scatter_wait")

    small_own = _pack_small(small_grads, folded=False, loss=loss_part)
    last_srcs = [_halves(g_in[1]), small_own]
    last_lands = [lax.empty((N_DEV,) + last_srcs[0].shape[2:], WIRE_DTYPE), lax.empty((N_DEV,) + small_own.shape, F32)]
    lsend, lrecv, last_srcs, last_lands, last_started = _scatter_start_call(
        last_srcs, last_lands, [False, True], "grads_scatter_start_last")
    idx = jnp.concatenate([chip_idx, c_idx] + [jnp.reshape(jnp.bitwise_xor(me, r), (1,)) for r in range(1, N_DEV)])
    idx = idx.astype(jnp.int32)
    mlp = _BIG[1:]

    def update(names, own32, lands, tag, dep=None):
        halves = _reduce_call([_halves(g) for g in own32], lands, idx, 4, "grads_reduce_" + tag, dep)
        reduced = _pair_allgather_call(list(halves), "grads_pair_allgather_" + tag)
        return _adamw_call([w[n] for n in names], reduced, [m[n] for n in names], [v[n] for n in names], 8, "adamw_" + tag)

    out_mlp = update(mlp, [g_out[0], g_up[0], g_down[0]], lands_mlp, "mlp", last_started)
    land_in, small_all = _scatter_wait_call(last_srcs, last_lands, lsend, lrecv, out_mlp[3][-1], [False, True],
                                            "grads_scatter_wait_last")
    out_in = update(_BIG[:1], [g_in[0]], [land_in], "in")
    g_pack, d_pack, m_pack, v_pack = _small_call(
        small_all, small_own, jnp.reshape(me, (1,)).astype(jnp.int32), _pack_small(w), _pack_small(m), _pack_small(v))

    grads, deltas, new_m, new_v = (_unpack_small(a) for a in (g_pack, d_pack, m_pack, v_pack))
    for k, res in enumerate((grads, deltas, new_m, new_v)):
        res[_BIG[0]] = out_in[k][0]
        for i, n in enumerate(mlp):
            res[n] = out_mlp[k][i]
    loss = g_pack[LOSS_ROW, 0]
    return (loss, dx[None], *[grads[n] for n in _WEIGHT_ORDER], *[deltas[n] for n in _WEIGHT_ORDER],
            *[new_m[n] for n in _WEIGHT_ORDER], *[new_v[n] for n in _WEIGHT_ORDER])
```

```python
import math

import jax
import jax.numpy as jnp
import numpy as np
from jax import lax
from jax.experimental import pallas as pl
from jax.experimental.pallas import tpu as pltpu

F32 = jnp.float32
MXU_DTYPE = jnp.bfloat16
WIRE_DTYPE = jnp.bfloat16

NORM_EPS = 1e-6
NEG_INF = -1e30
LANES = 128
HEAD_DIM = 64
N_HEADS = 8
POOL_WIDTH = 512
ATTN_WIDTH = 512
POOL_WINDOWS = (2, 4, 8, 16)
POOL_HALO = 16
DILATED_PATTERNS = ((128, 1), (512, 4), (2048, 16))
ATT_BLOCK = 128
ATT_SUPER = ATT_BLOCK * max(dl for _, dl in DILATED_PATTERNS)
ATT_UNITS = ATT_SUPER // ATT_BLOCK
N_BUCKETS = 32
NO_BUCKET = -1
MAX_DISTANCE = 2048
N_CHIPS = 4
N_DEV = 8
ADAM_LR, ADAM_B1, ADAM_B2, ADAM_EPS, ADAM_WD, ADAM_STEP = 0.001, 0.9, 0.999, 1e-08, 0.01, 10
VMEM_LIMIT = 56 * 1024 * 1024
MESH = pl.DeviceIdType.MESH
ANY = pl.BlockSpec(memory_space=pl.ANY)

SMALL_ROWS = 72
LOSS_ROW = 5


def _mm(a, b):
    return jnp.dot(a, b, preferred_element_type=F32)


def _mm_nt(a, b):
    return lax.dot_general(a, b, (((1,), (1,)), ((), ())), preferred_element_type=F32)


def _mm_tn(a, b):
    return lax.dot_general(a, b, (((0,), (0,)), ((), ())), preferred_element_type=F32)


def _params(sem=None, **kw):
    if sem is not None:
        kw["dimension_semantics"] = sem
    return pltpu.CompilerParams(vmem_limit_bytes=VMEM_LIMIT, **kw)


def _low_half():
    return lax.broadcasted_iota(jnp.int32, (1, LANES), 1) < HEAD_DIM


def _head_sum_bcast(y):
    lo = _low_half()
    outs = []
    for j in range(y.shape[1] // LANES):
        c = y[:, j * LANES:(j + 1) * LANES]
        s_lo = jnp.sum(jnp.where(lo, c, 0.0), axis=-1, keepdims=True)
        s_hi = jnp.sum(jnp.where(lo, 0.0, c), axis=-1, keepdims=True)
        outs.append(jnp.where(lo, s_lo, s_hi))
    return jnp.concatenate(outs, axis=-1)


def _rms_bwd(dn, hn, r):
    return r * (dn - hn * jnp.mean(dn * hn, axis=-1, keepdims=True))


def _t5_bucket_np(dist):
    max_exact = N_BUCKETS // 2
    d_f = np.maximum(dist, 1).astype(np.float32)
    ratio = (np.log(d_f / np.float32(max_exact)) / np.float32(math.log(MAX_DISTANCE / max_exact))).astype(np.float32)
    large = max_exact + (ratio * np.float32(N_BUCKETS - max_exact)).astype(np.int32)
    large = np.minimum(large, N_BUCKETS - 1)
    return np.where(dist < max_exact, dist, large).astype(np.int32)


def _window_offsets(dl):
    if dl == 1:
        return _by4_positions(ATT_BLOCK), _by4_positions(2 * ATT_BLOCK)
    return np.arange(ATT_BLOCK), np.arange(2 * ATT_BLOCK)


def _bucket_tables():
    tables = []
    for _, dl in DILATED_PATTERNS:
        qq, kk = _window_offsets(dl)
        dist = qq[:, None] + ATT_BLOCK - kk[None, :]
        bucket = _t5_bucket_np(np.clip(dist, 0, ATT_BLOCK) * dl)
        tables.append(np.where((dist >= 0) & (dist <= ATT_BLOCK), bucket, NO_BUCKET))
    return np.stack(tables).astype(np.int32)


def _previous_block_keys():
    return np.stack([np.broadcast_to(_window_offsets(dl)[1][None, :] < ATT_BLOCK, (ATT_BLOCK, 2 * ATT_BLOCK))
                     for _, dl in DILATED_PATTERNS])


def _f1_call(x, g1, win, poolw, pscale, qg, kg, tm):
    s, d = x.shape
    nblk = s // tm

    def body(x_ref, g1_ref, win_ref, pw_ref, ps_ref, qg_ref, kg_ref,
             a_ref, pooled_ref, ypool_ref, q32_ref, k32_ref, qn_ref, kn_ref, v_ref, ubuf):
        i = pl.program_id(0)
        xv = x_ref[...]
        r = lax.rsqrt(jnp.mean(xv * xv, axis=-1, keepdims=True) + NORM_EPS)
        a = ((xv * r) * g1_ref[...]).astype(MXU_DTYPE)
        a_ref[...] = a
        u = _mm(a, win_ref[0])
        q = _mm(a, win_ref[1])
        k = _mm(a, win_ref[2])
        v_ref[...] = _mm(a, win_ref[3])
        q32_ref[...] = q
        k32_ref[...] = k
        rq = lax.rsqrt(_head_sum_bcast(q * q) * (1.0 / HEAD_DIM) + NORM_EPS)
        qn_ref[...] = ((q * rq) * qg_ref[...]) * (HEAD_DIM ** -0.5)
        rk = lax.rsqrt(_head_sum_bcast(k * k) * (1.0 / HEAD_DIM) + NORM_EPS)
        kn_ref[...] = (k * rk) * kg_ref[...]

        @pl.when(i == 0)
        def _():
            ubuf[0:POOL_HALO, :] = jnp.zeros((POOL_HALO, POOL_WIDTH), F32)

        @pl.when(i > 0)
        def _():
            ubuf[0:POOL_HALO, :] = ubuf[tm:tm + POOL_HALO, :]

        ubuf[POOL_HALO:POOL_HALO + tm, :] = u
        t = i * tm + lax.broadcasted_iota(jnp.int32, (tm, 1), 0)
        for g, w in enumerate(POOL_WINDOWS):
            ls = slice(g * LANES, (g + 1) * LANES)
            ug = u[:, ls]
            acc = ug
            for sh in range(1, w):
                acc = acc + ubuf[POOL_HALO - sh:POOL_HALO - sh + tm, ls]
            cnt = jnp.minimum(t + 1, w).astype(F32)
            pooled = (acc / cnt - ug).astype(MXU_DTYPE)
            pooled_ref[:, ls] = pooled
            ypool_ref[:, ls] = (_mm(pooled, pw_ref[g]) * ps_ref[:, ls]).astype(MXU_DTYPE)

    tok = lambda w: pl.BlockSpec((tm, w), lambda i: (i, 0))
    full = lambda shp: pl.BlockSpec(shp, lambda i: (0,) * len(shp))
    return pl.pallas_call(
        body, name="fwd_inproj",
        grid=(nblk,),
        in_specs=[tok(d), full((1, d)), full(win.shape), full(poolw.shape), full((1, POOL_WIDTH)),
                  full((1, ATTN_WIDTH)), full((1, ATTN_WIDTH))],
        out_specs=[tok(d), tok(POOL_WIDTH), tok(POOL_WIDTH), tok(ATTN_WIDTH), tok(ATTN_WIDTH),
                   tok(ATTN_WIDTH), tok(ATTN_WIDTH), tok(ATTN_WIDTH)],
        out_shape=[jax.ShapeDtypeStruct((s, d), MXU_DTYPE),
                   jax.ShapeDtypeStruct((s, POOL_WIDTH), MXU_DTYPE),
                   jax.ShapeDtypeStruct((s, POOL_WIDTH), MXU_DTYPE),
                   jax.ShapeDtypeStruct((s, ATTN_WIDTH), F32),
                   jax.ShapeDtypeStruct((s, ATTN_WIDTH), F32),
                   jax.ShapeDtypeStruct((s, ATTN_WIDTH), F32),
                   jax.ShapeDtypeStruct((s, ATTN_WIDTH), F32),
                   jax.ShapeDtypeStruct((s, ATTN_WIDTH), F32)],
        scratch_shapes=[pltpu.VMEM((tm + POOL_HALO, POOL_WIDTH), F32)],
        compiler_params=_params(("arbitrary",)),
    )(x, g1, win, poolw, pscale, qg, kg)


DEINT = 4
assert [dl for _, dl in DILATED_PATTERNS] == [1, DEINT, DEINT * DEINT]


def _by4_positions(n):
    pos = np.arange(n)
    return DEINT * (pos % (n // DEINT)) + pos // (n // DEINT)


def _masked_bias(b_ref, p, n):
    return b_ref[p, jnp.minimum(n, 1)].reshape(2 * ATT_BLOCK, 2 * ATT_BLOCK)


def _unit_rows(u, dl):
    sq, sk = ATT_SUPER // DEINT, 2 * ATT_SUPER // DEINT
    if dl == 1:
        n = ATT_BLOCK // DEINT
        return (u, [pl.ds(pl.multiple_of(r * sq + n * u, 8), n) for r in range(DEINT)],
                [pl.ds(pl.multiple_of(r * sk + sk // 2 + n * (u - 1), 8), 2 * n) for r in range(DEINT)])
    if dl == DEINT:
        r, b = u % DEINT, u // DEINT
        return (b, [pl.ds(pl.multiple_of(r * sq + ATT_BLOCK * b, 8), ATT_BLOCK)],
                [pl.ds(pl.multiple_of(r * sk + sk // 2 + ATT_BLOCK * (b - 1), 8), 2 * ATT_BLOCK)])
    r, a = u % DEINT, u // DEINT
    return 0, [pl.ds(r * sq + a, ATT_BLOCK, stride=DEINT)], [pl.ds(r * sk + a, 2 * ATT_BLOCK, stride=DEINT)]


def _take(ref, runs):
    parts = [ref[run, :] for run in runs]
    return parts[0] if len(parts) == 1 else jnp.concatenate(parts, axis=0)


def _put(ref, runs, value, add=False):
    n = value.shape[0] // len(runs)
    for i, run in enumerate(runs):
        part = value[i * n:(i + 1) * n]
        ref[run, :] = ref[run, :] + part if add else part


def _deinterleave(dst, src, n):
    seg = n // DEINT
    for r in range(DEINT):
        dst[r * seg:(r + 1) * seg, :] = src[pl.ds(r, seg, stride=DEINT), :]


def _deinterleave_pair(dst, prev, cur):
    seg = prev.shape[0] // DEINT
    for r in range(DEINT):
        dst[2 * r * seg:(2 * r + 1) * seg, :] = prev[pl.ds(r, seg, stride=DEINT), :]
        dst[(2 * r + 1) * seg:(2 * r + 2) * seg, :] = cur[pl.ds(r, seg, stride=DEINT), :]


def _interleave(dst, src, n, offset=0):
    seg = n // DEINT
    stride = src.shape[0] // DEINT
    for r in range(DEINT):
        dst[pl.ds(r, seg, stride=DEINT), :] = src[r * stride + offset:r * stride + offset + seg, :]


def _attn_fwd_call(qn, kn, v, bias):
    s, w = qn.shape
    nsb = s // ATT_SUPER
    npair = w // LANES

    def body(q_ref, kc_ref, kp_ref, vc_ref, vp_ref, b_ref, o_ref, lse_ref, qf, kf, vf, acc_s, m_s, l_s):
        sb = pl.program_id(1)
        _deinterleave(qf, q_ref, ATT_SUPER)
        _deinterleave_pair(kf, kp_ref, kc_ref)
        _deinterleave_pair(vf, vp_ref, vc_ref)
        lo = _low_half()
        for p, (_, dl) in enumerate(DILATED_PATTERNS):
            def unit(u, carry, p=p, dl=dl):
                b, rows_q, rows_k = _unit_rows(u, dl)
                qp = _take(qf, rows_q).astype(MXU_DTYPE)
                kcat = _take(kf, rows_k).astype(MXU_DTYPE)
                vcat = _take(vf, rows_k).astype(MXU_DTYPE)
                zero = jnp.zeros_like(qp)
                q2 = jnp.concatenate([jnp.where(lo, qp, zero), jnp.where(lo, zero, qp)], axis=0)
                sc = _mm_nt(q2, kcat) + _masked_bias(b_ref, p, sb * (ATT_UNITS // dl) + b)
                m2 = jnp.max(sc, axis=-1, keepdims=True)
                pr = jnp.exp(sc - m2)
                l2 = jnp.sum(pr, axis=-1, keepdims=True)
                acc2 = _mm(pr.astype(MXU_DTYPE), vcat)
                acc = jnp.where(lo, acc2[:ATT_BLOCK], acc2[ATT_BLOCK:])
                m = jnp.where(lo, m2[:ATT_BLOCK], m2[ATT_BLOCK:])
                l = jnp.where(lo, l2[:ATT_BLOCK], l2[ATT_BLOCK:])
                if p == 0:
                    _put(acc_s, rows_q, acc)
                    _put(m_s, rows_q, m)
                    _put(l_s, rows_q, l)
                else:
                    m_old = _take(m_s, rows_q)
                    m_new = jnp.maximum(m_old, m)
                    a_old = jnp.exp(m_old - m_new)
                    a_new = jnp.exp(m - m_new)
                    _put(acc_s, rows_q, a_old * _take(acc_s, rows_q) + a_new * acc)
                    _put(l_s, rows_q, a_old * _take(l_s, rows_q) + a_new * l)
                    _put(m_s, rows_q, m_new)
                return carry

            lax.fori_loop(0, ATT_UNITS, unit, 0, unroll=16)
        l = l_s[...]
        acc_s[...] = acc_s[...] / l
        m_s[...] = m_s[...] + jnp.log(l)
        _interleave(o_ref, acc_s, ATT_SUPER)
        _interleave(lse_ref, m_s, ATT_SUPER)

    cur = pl.BlockSpec((ATT_SUPER, LANES), lambda j, t: (t, j))
    prev = pl.BlockSpec((ATT_SUPER, LANES), lambda j, t: (jnp.maximum(t - 1, 0), j))
    bspec = pl.BlockSpec((len(DILATED_PATTERNS), 2, 2, ATT_BLOCK, 2 * ATT_BLOCK), lambda j, t: (0, 0, j, 0, 0))
    return pl.pallas_call(
        body, name="attn_fwd",
        grid=(npair, nsb),
        in_specs=[cur, cur, prev, cur, prev, bspec],
        out_specs=[cur, cur],
        out_shape=[jax.ShapeDtypeStruct((s, w), F32), jax.ShapeDtypeStruct((s, w), F32)],
        scratch_shapes=[pltpu.VMEM((ATT_SUPER, LANES), F32), pltpu.VMEM((2 * ATT_SUPER, LANES), F32),
                        pltpu.VMEM((2 * ATT_SUPER, LANES), F32), pltpu.VMEM((ATT_SUPER, LANES), F32),
                        pltpu.VMEM((ATT_SUPER, LANES), F32), pltpu.VMEM((ATT_SUPER, LANES), F32)],
        compiler_params=_params(("arbitrary", "arbitrary")),
    )(qn, kn, kn, v, v, bias)


def _attn_bwd_call(qn, kn, v, do, lse, delta, bias, dep=None):
    s, w = qn.shape
    nsb = s // ATT_SUPER
    npair = w // LANES
    deps = [] if dep is None else [dep]

    def body(q_ref, kc_ref, kp_ref, vc_ref, vp_ref, do_ref, lse_ref, dlt_ref, b_ref, *rest):
        dq_ref, dk_ref, dv_ref, db_ref, qf, kf, vf, dof, lsef, dltf, dqf, dkf, dvf = rest[len(deps):]
        step = pl.program_id(1)
        sb = nsb - 1 - step
        seg = ATT_SUPER // DEINT
        _deinterleave(qf, q_ref, ATT_SUPER)
        _deinterleave(dof, do_ref, ATT_SUPER)
        _deinterleave_pair(kf, kp_ref, kc_ref)
        _deinterleave_pair(vf, vp_ref, vc_ref)
        _deinterleave(lsef, lse_ref, ATT_SUPER)
        _deinterleave(dltf, dlt_ref, ATT_SUPER)

        @pl.when(step == 0)
        def _():
            db_ref[...] = jnp.zeros(db_ref.shape, F32)

        for acc in (dkf, dvf):
            for r in range(DEINT):
                this, before = pl.ds((2 * r + 1) * seg, seg), pl.ds(2 * r * seg, seg)

                @pl.when(step == 0)
                def _(acc=acc, this=this):
                    acc[this, :] = jnp.zeros((seg, LANES), F32)

                @pl.when(step > 0)
                def _(acc=acc, this=this, before=before):
                    acc[this, :] = acc[before, :]

                acc[before, :] = jnp.zeros((seg, LANES), F32)
        lo = _low_half()
        for p, (_, dl) in enumerate(DILATED_PATTERNS):
            def unit(u, carry, p=p, dl=dl):
                b, rows_q, rows_k = _unit_rows(u, dl)
                qp = _take(qf, rows_q).astype(MXU_DTYPE)
                dop = _take(dof, rows_q).astype(MXU_DTYPE)
                kcat = _take(kf, rows_k).astype(MXU_DTYPE)
                vcat = _take(vf, rows_k).astype(MXU_DTYPE)
                lse2 = _take(lsef, rows_q)
                dlt2 = _take(dltf, rows_q)
                zero = jnp.zeros_like(qp)
                q2 = jnp.concatenate([jnp.where(lo, qp, zero), jnp.where(lo, zero, qp)], axis=0)
                do2 = jnp.concatenate([jnp.where(lo, dop, zero), jnp.where(lo, zero, dop)], axis=0)
                lse_c = jnp.concatenate([lse2[:, 0:1], lse2[:, HEAD_DIM:HEAD_DIM + 1]], axis=0)
                dlt_c = jnp.concatenate([dlt2[:, 0:1], dlt2[:, HEAD_DIM:HEAD_DIM + 1]], axis=0)
                sc = _mm_nt(q2, kcat) + _masked_bias(b_ref, p, sb * (ATT_UNITS // dl) + b)
                pr = jnp.exp(sc - lse_c)
                ds = pr * (_mm_nt(do2, vcat) - dlt_c)
                db_ref[p] += ds.reshape(2, ATT_BLOCK, 2 * ATT_BLOCK)
                ds_c = ds.astype(MXU_DTYPE)
                dq2 = _mm(ds_c, kcat)
                dk = _mm_tn(ds_c, q2)
                dv = _mm_tn(pr.astype(MXU_DTYPE), do2)
                dq = jnp.where(lo, dq2[:ATT_BLOCK], dq2[ATT_BLOCK:])
                _put(dqf, rows_q, dq, add=p > 0)
                _put(dkf, rows_k, dk, add=True)
                _put(dvf, rows_k, dv, add=True)
                return carry

            lax.fori_loop(0, ATT_UNITS, unit, 0, unroll=16)
        _interleave(dq_ref, dqf, ATT_SUPER)
        _interleave(dk_ref, dkf, ATT_SUPER, offset=seg)
        _interleave(dv_ref, dvf, ATT_SUPER, offset=seg)

    cur = pl.BlockSpec((ATT_SUPER, LANES), lambda j, t: (nsb - 1 - t, j))
    prev = pl.BlockSpec((ATT_SUPER, LANES), lambda j, t: (jnp.maximum(nsb - 2 - t, 0), j))
    npat = len(DILATED_PATTERNS)
    bspec = pl.BlockSpec((npat, 2, 2, ATT_BLOCK, 2 * ATT_BLOCK), lambda j, t: (0, 0, j, 0, 0))
    dbspec = pl.BlockSpec((npat, 2, ATT_BLOCK, 2 * ATT_BLOCK), lambda j, t: (0, j, 0, 0))
    sup = lambda: pltpu.VMEM((ATT_SUPER, LANES), F32)
    sup2 = lambda: pltpu.VMEM((2 * ATT_SUPER, LANES), F32)
    return pl.pallas_call(
        body, name="attn_bwd",
        grid=(npair, nsb),
        in_specs=[cur, cur, prev, cur, prev, cur, cur, cur, bspec] + [ANY] * len(deps),
        out_specs=[cur, cur, cur, dbspec],
        out_shape=[jax.ShapeDtypeStruct((s, w), F32)] * 3
        + [jax.ShapeDtypeStruct((npat, N_HEADS, ATT_BLOCK, 2 * ATT_BLOCK), F32)],
        scratch_shapes=[sup(), sup2(), sup2(), sup(), sup(), sup(), sup(), sup2(), sup2()],
        compiler_params=_params(("arbitrary", "arbitrary")),
    )(qn, kn, kn, v, v, do, lse, delta, bias, *deps)


def _bias_table_work(rel_bias):
    buckets = jnp.asarray(_bucket_tables())
    prev_keys = jnp.asarray(_previous_block_keys().astype(np.int32))
    npat = buckets.shape[0]

    def body(rb_ref, bk_ref, pk_ref, out_ref):
        for p in range(npat):
            for half in range(2):
                ks = slice(half * ATT_BLOCK, (half + 1) * ATT_BLOCK)
                bk = bk_ref[p, :, ks]
                absent = pk_ref[p, :, ks] != 0
                for h in range(N_HEADS):
                    def pick(b, acc, h=h, bk=bk):
                        return jnp.where(bk == b, rb_ref[b, h], acc)

                    tab = lax.fori_loop(0, N_BUCKETS, pick, jnp.full((ATT_BLOCK, ATT_BLOCK), NEG_INF, F32))
                    out_ref[p, 1, h, :, ks] = tab
                    out_ref[p, 0, h, :, ks] = jnp.where(absent, NEG_INF, tab)

    vmem = pl.BlockSpec(memory_space=pltpu.VMEM)
    return ([rel_bias, buckets, prev_keys], [pl.BlockSpec(memory_space=pltpu.SMEM), vmem, vmem],
            jax.ShapeDtypeStruct((npat, 2, N_HEADS, ATT_BLOCK, 2 * ATT_BLOCK), F32), body)


def _bias_table_call(rel_bias):
    operands, specs, shape, body = _bias_table_work(rel_bias)
    return pl.pallas_call(body, name="bias_table", in_specs=specs, out_shape=shape, compiler_params=_params())(*operands)


def _rel_bias_grad_call(dbias, buckets):
    npat, nh = dbias.shape[0], dbias.shape[1]

    def body(db_ref, bk_ref, out_ref):
        lane = lax.broadcasted_iota(jnp.int32, (nh, LANES), 1)
        out = jnp.zeros((nh, LANES), F32)
        for b in range(N_BUCKETS):
            tot = jnp.zeros((nh, 1), F32)
            for p in range(npat):
                hit = jnp.where(bk_ref[p][None] == b, db_ref[p], 0.0)
                tot = tot + jnp.sum(jnp.sum(hit, axis=2), axis=1, keepdims=True)
            out = jnp.where(lane == b, tot, out)
        out_ref[...] = out

    return pl.pallas_call(
        body, name="rel_bias_grad",
        out_shape=jax.ShapeDtypeStruct((nh, LANES), F32),
        compiler_params=_params(),
    )(dbias, buckets)


def _f2_call(x, tgt, ypool, o, wout, wup, wdown, g2, tm):
    s, d = x.shape
    nblk = s // tm
    nch, _, fch = wup.shape
    dff = nch * fch
    mixw = POOL_WIDTH + ATTN_WIDTH

    def body(x_ref, t_ref, yp_ref, o_ref, g2_ref, wout_hbm, wup_hbm, wdown_hbm,
             mixed_ref, c_ref, ff_ref, dz_ref, dy_ref, dh1_ref, dyp_ref, do_ref, dlt_ref, dg2_ref, loss_ref,
             wout_v, wup_v, wdown_v, rz):
        i = pl.program_id(0)

        @pl.when(i == 0)
        def _():
            pltpu.sync_copy(wout_hbm, wout_v)
            pltpu.sync_copy(wup_hbm, wup_v)
            pltpu.sync_copy(wdown_hbm, wdown_v)
            dg2_ref[...] = jnp.zeros(dg2_ref.shape, F32)
            loss_ref[...] = jnp.zeros(loss_ref.shape, F32)

        o = o_ref[...]
        mixed = jnp.concatenate([yp_ref[...], o.astype(MXU_DTYPE)], axis=-1)
        mixed_ref[...] = mixed
        h1 = x_ref[...] + _mm(mixed, wout_v[...])
        r2 = lax.rsqrt(jnp.mean(h1 * h1, axis=-1, keepdims=True) + NORM_EPS)
        hn = h1 * r2
        c = (hn * g2_ref[...]).astype(MXU_DTYPE)
        c_ref[...] = c
        y = h1
        for j in range(nch):
            cs = slice(j * fch, (j + 1) * fch)
            z = jnp.maximum(_mm(c, wup_v[j]), 0.0)
            rz[:, cs] = z
            ff = (z * z).astype(MXU_DTYPE)
            ff_ref[:, cs] = ff
            y = y + _mm(ff, wdown_v[j])
        err = y - t_ref[...]
        loss_ref[...] += jnp.sum(err * err) * (0.5 / d)
        dy = err * (1.0 / d)
        dy_c = dy.astype(MXU_DTYPE)
        dy_ref[...] = dy_c
        dc = jnp.zeros((tm, d), F32)
        for j in range(nch):
            cs = slice(j * fch, (j + 1) * fch)
            dz = (_mm_nt(dy_c, wdown_v[j]) * (2.0 * rz[:, cs])).astype(MXU_DTYPE)
            dz_ref[:, cs] = dz
            dc = dc + _mm_nt(dz, wup_v[j])
        dg2_ref[...] += jnp.sum(dc * hn, axis=0, keepdims=True)
        dh1 = dy + _rms_bwd(dc * g2_ref[...], hn, r2)
        dh1_ref[...] = dh1
        dmix = _mm_nt(dh1.astype(MXU_DTYPE), wout_v[...])
        dyp_ref[...] = dmix[:, :POOL_WIDTH]
        do = dmix[:, POOL_WIDTH:]
        do_ref[...] = do
        dlt_ref[...] = _head_sum_bcast(do * o)

    tok = lambda w: pl.BlockSpec((tm, w), lambda i: (i, 0))
    const = lambda shp: pl.BlockSpec(shp, lambda i: (0,) * len(shp))
    return pl.pallas_call(
        body, name="fwd_mlp_bwd_mlp",
        grid=(nblk,),
        in_specs=[tok(d), tok(d), tok(POOL_WIDTH), tok(ATTN_WIDTH), const((1, d)), ANY, ANY, ANY],
        out_specs=[tok(mixw), tok(d), tok(dff), tok(dff), tok(d), tok(d), tok(POOL_WIDTH), tok(ATTN_WIDTH),
                   tok(ATTN_WIDTH), const((1, d)), const((1, LANES))],
        out_shape=[jax.ShapeDtypeStruct((s, mixw), MXU_DTYPE),
                   jax.ShapeDtypeStruct((s, d), MXU_DTYPE),
                   jax.ShapeDtypeStruct((s, dff), MXU_DTYPE),
                   jax.ShapeDtypeStruct((s, dff), MXU_DTYPE),
                   jax.ShapeDtypeStruct((s, d), MXU_DTYPE),
                   jax.ShapeDtypeStruct((s, d), F32),
                   jax.ShapeDtypeStruct((s, POOL_WIDTH), F32),
                   jax.ShapeDtypeStruct((s, ATTN_WIDTH), F32),
                   jax.ShapeDtypeStruct((s, ATTN_WIDTH), F32),
                   jax.ShapeDtypeStruct((1, d), F32),
                   jax.ShapeDtypeStruct((1, LANES), F32)],
        scratch_shapes=[pltpu.VMEM(wout.shape, MXU_DTYPE), pltpu.VMEM(wup.shape, MXU_DTYPE),
                        pltpu.VMEM(wdown.shape, MXU_DTYPE), pltpu.VMEM((tm, dff), F32)],
        compiler_params=_params(("arbitrary",)),
    )(x, tgt, ypool, o, g2, wout, wup, wdown)


def _bproj_call(dqn, dkn, dv, q32, k32, dypool, pooled, x, dh1, win, poolw, pscale, qg, kg, g1, tm):
    s, d = x.shape
    nblk = s // tm
    ngrp = len(POOL_WINDOWS)

    def body(dqn_ref, dkn_ref, dv_ref, q_ref, k_ref, dyp_ref, pooled_ref, x_ref, dh1_ref,
             win_hbm, pw_ref, ps_ref, qg_ref, kg_ref, g1_ref,
             dx_ref, dproj_ref, dg1_ref, dqg_ref, dkg_ref, dpw_ref, dps_ref, win_v, ebuf):
        step = pl.program_id(0)
        i = nblk - 1 - step

        @pl.when(step == 0)
        def _():
            pltpu.sync_copy(win_hbm, win_v)
            dg1_ref[...] = jnp.zeros(dg1_ref.shape, F32)
            dqg_ref[...] = jnp.zeros(dqg_ref.shape, F32)
            dkg_ref[...] = jnp.zeros(dkg_ref.shape, F32)
            dpw_ref[...] = jnp.zeros(dpw_ref.shape, F32)
            dps_ref[...] = jnp.zeros(dps_ref.shape, F32)
            ebuf[tm:tm + POOL_HALO, :] = jnp.zeros((POOL_HALO, POOL_WIDTH), F32)

        @pl.when(step > 0)
        def _():
            ebuf[tm:tm + POOL_HALO, :] = ebuf[0:POOL_HALO, :]

        def qk_bwd(dn_sum, raw, gain, scale, dgain_ref):
            rr = lax.rsqrt(_head_sum_bcast(raw * raw) * (1.0 / HEAD_DIM) + NORM_EPS)
            hn = raw * rr
            dgain_ref[...] += jnp.sum(dn_sum * hn, axis=0, keepdims=True) * scale
            dn = dn_sum * (gain * scale)
            return rr * (dn - hn * (_head_sum_bcast(dn * hn) * (1.0 / HEAD_DIM)))

        dq = qk_bwd(dqn_ref[...], q_ref[...], qg_ref[...], HEAD_DIM ** -0.5, dqg_ref)
        dk = qk_bwd(dkn_ref[...], k_ref[...], kg_ref[...], 1.0, dkg_ref)

        t = i * tm + lax.broadcasted_iota(jnp.int32, (tm, 1), 0)
        dpooled = []
        for g, w in enumerate(POOL_WINDOWS):
            ls = slice(g * LANES, (g + 1) * LANES)
            dm = dyp_ref[:, ls]
            pg = pooled_ref[:, ls]
            dps_ref[:, ls] += jnp.sum(dm * _mm(pg, pw_ref[g]), axis=0, keepdims=True)
            dms = (dm * ps_ref[:, ls]).astype(MXU_DTYPE)
            dpw_ref[g] += _mm_tn(pg, dms)
            dpg = _mm_nt(dms, pw_ref[g])
            dpooled.append(dpg)
            ebuf[0:tm, ls] = dpg / jnp.minimum(t + 1, w).astype(F32)
        du = []
        for g, w in enumerate(POOL_WINDOWS):
            ls = slice(g * LANES, (g + 1) * LANES)
            acc = ebuf[0:tm, ls]
            for sh in range(1, w):
                acc = acc + ebuf[sh:sh + tm, ls]
            du.append(acc - dpooled[g])
        parts = [jnp.concatenate(du, axis=-1), dq, dk, dv_ref[...]]
        da = jnp.zeros((tm, d), F32)
        for p, part in enumerate(parts):
            pc = part.astype(MXU_DTYPE)
            dproj_ref[:, p * POOL_WIDTH:(p + 1) * POOL_WIDTH] = pc
            da = da + _mm_nt(pc, win_v[p])
        xv = x_ref[...]
        r = lax.rsqrt(jnp.mean(xv * xv, axis=-1, keepdims=True) + NORM_EPS)
        xn = xv * r
        dg1_ref[...] += jnp.sum(da * xn, axis=0, keepdims=True)
        dx_ref[...] = dh1_ref[...] + _rms_bwd(da * g1_ref[...], xn, r)

    tok = lambda w: pl.BlockSpec((tm, w), lambda t: (nblk - 1 - t, 0))
    const = lambda shp: pl.BlockSpec(shp, lambda t: (0,) * len(shp))
    return pl.pallas_call(
        body, name="bwd_inproj",
        grid=(nblk,),
        in_specs=[tok(ATTN_WIDTH)] * 5 + [tok(POOL_WIDTH), tok(POOL_WIDTH), tok(d), tok(d),
                                          ANY, const(poolw.shape), const((1, POOL_WIDTH)), const((1, ATTN_WIDTH)),
                                          const((1, ATTN_WIDTH)), const((1, d))],
        out_specs=[tok(d), tok(4 * POOL_WIDTH), const((1, d)), const((1, ATTN_WIDTH)), const((1, ATTN_WIDTH)),
                   const((ngrp, LANES, LANES)), const((1, POOL_WIDTH))],
        out_shape=[jax.ShapeDtypeStruct((s, d), F32),
                   jax.ShapeDtypeStruct((s, 4 * POOL_WIDTH), MXU_DTYPE),
                   jax.ShapeDtypeStruct((1, d), F32),
                   jax.ShapeDtypeStruct((1, ATTN_WIDTH), F32),
                   jax.ShapeDtypeStruct((1, ATTN_WIDTH), F32),
                   jax.ShapeDtypeStruct((ngrp, LANES, LANES), F32),
                   jax.ShapeDtypeStruct((1, POOL_WIDTH), F32)],
        scratch_shapes=[pltpu.VMEM(win.shape, MXU_DTYPE), pltpu.VMEM((tm + POOL_HALO, POOL_WIDTH), F32)],
        compiler_params=_params(("arbitrary",)),
    )(dqn, dkn, dv, q32, k32, dypool, pooled, x, dh1, win, poolw, pscale, qg, kg, g1)


def _wgrad_call(a, b, bm, bn, bk, out_shape, out_block, out_index, name):
    s, m = a.shape
    _, n = b.shape
    nk = s // bk

    def body(a_ref, b_ref, o_ref, wire_ref):
        k = pl.program_id(2)

        @pl.when(k == 0)
        def _():
            o_ref[...] = jnp.zeros(o_ref.shape, F32)

        o_ref[...] += _mm_tn(a_ref[...].astype(MXU_DTYPE), b_ref[...].astype(MXU_DTYPE))

        @pl.when(k == nk - 1)
        def _():
            wire_ref[...] = o_ref[...].astype(WIRE_DTYPE)

    return pl.pallas_call(
        body, name=name,
        grid=(m // bm, n // bn, nk),
        in_specs=[pl.BlockSpec((bk, bm), lambda i, j, k: (k, i)), pl.BlockSpec((bk, bn), lambda i, j, k: (k, j))],
        out_specs=[pl.BlockSpec(out_block, out_index)] * 2,
        out_shape=[jax.ShapeDtypeStruct(out_shape, F32), jax.ShapeDtypeStruct(out_shape, WIRE_DTYPE)],
        compiler_params=_params(("arbitrary", "arbitrary", "arbitrary")),
    )(a, b)


def _local_grads(x, tgt, g1, win, poolw, pscale, qg, kg, rel_bias, g2, mlp_weights, on_mlp_grads=None, bias=None):
    s, d = x.shape
    g1r, g2r = g1.reshape(1, d), g2.reshape(1, d)
    psr = pscale.reshape(1, POOL_WIDTH)
    qgr = jnp.tile(qg, N_HEADS).reshape(1, ATTN_WIDTH)
    kgr = jnp.tile(kg, N_HEADS).reshape(1, ATTN_WIDTH)
    pw_c = poolw.astype(MXU_DTYPE)
    buckets = jnp.asarray(_bucket_tables())
    bias = _bias_table_call(rel_bias) if bias is None else bias
    bk = min(s, 2048)

    a, pooled, ypool, q32, k32, qn, kn, v = _f1_call(x, g1r, win, pw_c, psr, qgr, kgr, tm=512)
    o, lse = _attn_fwd_call(qn, kn, v, bias)
    wout, wup, wdown = mlp_weights(o)
    mixed, c, ff, dz, dy, dh1, dypool, do, delta, dg2, loss = _f2_call(x, tgt, ypool, o, wout, wup, wdown, g2r, tm=256)
    dff = ff.shape[1]
    g_out = [g.reshape(N_CHIPS, d // N_CHIPS, d)
             for g in _wgrad_call(mixed, dh1, d, d, bk // 2, (d, d), (d, d), lambda i, j, k: (0, 0), "wgrad_out")]
    g_up = _wgrad_call(c, dz, d, dff // N_CHIPS, bk, (N_CHIPS, d, dff // N_CHIPS), (None, d, dff // N_CHIPS),
                       lambda i, j, k: (j, 0, 0), "wgrad_up")
    g_down = _wgrad_call(ff, dy, dff // N_CHIPS, d, bk, (N_CHIPS, dff // N_CHIPS, d), (None, dff // N_CHIPS, d),
                         lambda i, j, k: (i, 0, 0), "wgrad_down")
    dep = None if on_mlp_grads is None else on_mlp_grads(g_out[1], g_up[1], g_down[1])
    dqn, dkn, dv, dbias = _attn_bwd_call(qn, kn, v, do, lse, delta, bias, dep)
    dx, dproj, dg1, dqg, dkg, dpw, dps = _bproj_call(
        dqn, dkn, dv, q32, k32, dypool, pooled, x, dh1, win, pw_c, psr, qgr, kgr, g1r, tm=256)
    nin = dproj.shape[1] // N_CHIPS
    g_in = _wgrad_call(a, dproj, d, nin, bk, (N_CHIPS, d, nin), (None, d, nin), lambda i, j, k: (j, 0, 0), "wgrad_in")
    drb = _rel_bias_grad_call(dbias, buckets)
    small = dict(
        mix_norm_g=dg1.reshape(d), mlp_norm_g=dg2.reshape(d), pool_scale=dps.reshape(POOL_WIDTH),
        q_norm_g=dqg.reshape(ATTN_WIDTH), k_norm_g=dkg.reshape(ATTN_WIDTH),
        rel_bias=drb[:, :N_BUCKETS].T, pool_w=dpw)
    return loss[0, 0], dx, (g_in, g_out, g_up, g_down), small


def _coords():
    return lax.axis_index("x"), lax.axis_index("y"), lax.axis_index("c")


def _other_chips(x, y):
    return [(1 - x, y), (x, 1 - y), (1 - x, 1 - y)]


def _remote(src, dst, send_sem, recv_sem, dev):
    return pltpu.make_async_remote_copy(src_ref=src, dst_ref=dst, send_sem=send_sem, recv_sem=recv_sem,
                                        device_id=dev, device_id_type=MESH)


def _halves(a):
    return a.reshape(a.shape[:-2] + (2, a.shape[-2] // 2, a.shape[-1]))


def _place_shards_call(shards, chip_idx, nch):
    nw = len(shards)

    def body(chip_ref, *refs):
        for w in range(nw):
            refs[nw + w][...] = refs[w][...].astype(WIRE_DTYPE)

    in_specs = [pl.BlockSpec((s.shape[0] // nch, s.shape[1]), lambda i, chip_ref: (i, 0)) for s in shards]
    out_specs = [pl.BlockSpec((None, s.shape[0] // nch, s.shape[1]), lambda i, chip_ref: (chip_ref[0], i, 0))
                 for s in shards]
    return pl.pallas_call(
        body, name="weights_place",
        grid_spec=pltpu.PrefetchScalarGridSpec(num_scalar_prefetch=1, grid=(nch,),
                                               in_specs=in_specs, out_specs=out_specs),
        out_shape=[jax.ShapeDtypeStruct((N_CHIPS,) + s.shape, WIRE_DTYPE) for s in shards],
        compiler_params=_params(("arbitrary",)),
    )(chip_idx, *shards)


def _allgather_call(placed, from_chips, name, meanwhile=None):
    nw = len(placed)
    ncp = 3 * nw
    extra, extra_specs, extra_shape, extra_body = meanwhile if meanwhile else ([], [], None, None)
    ne = len(extra)

    def body(*refs):
        outs = refs[nw + ne:2 * nw + ne]
        send1, recv1, send2, recv2 = refs[-4:]
        x, y, c = _coords()
        chip = 2 * x + y
        others = _other_chips(x, y)
        first, passed = [], []
        if from_chips:
            for w in range(nw):
                for k, (ox, oy) in enumerate(others):
                    mine = outs[w].at[chip, c]
                    cp = _remote(mine, mine, send1.at[3 * w + k], recv1.at[3 * w + k], (ox, oy, c))
                    cp.start()
                    first.append(cp)
        if meanwhile:
            extra_body(*refs[nw:nw + ne], refs[2 * nw + ne])
        for w in range(nw):
            for k, (ox, oy) in enumerate(others):
                piece = outs[w].at[2 * ox + oy, c]
                if from_chips:
                    _remote(piece, piece, send1.at[3 * w + k], recv1.at[3 * w + k], (ox, oy, c)).wait_recv()
                cp = _remote(piece, piece, send2.at[3 * w + k], recv2.at[3 * w + k], (x, y, 1 - c))
                cp.start()
                passed.append(cp)
        for w in range(nw):
            for k, (ox, oy) in enumerate(others):
                piece = outs[w].at[2 * ox + oy, 1 - c]
                _remote(piece, piece, send2.at[3 * w + k], recv2.at[3 * w + k], (x, y, 1 - c)).wait_recv()
        for cp in first + passed:
            cp.wait_send()

    return pl.pallas_call(
        body, name=name,
        in_specs=[ANY] * nw + list(extra_specs),
        out_specs=[ANY] * nw + ([pl.BlockSpec(memory_space=pltpu.VMEM)] if meanwhile else []),
        out_shape=[jax.ShapeDtypeStruct(s.shape, s.dtype) for s in placed] + ([extra_shape] if meanwhile else []),
        input_output_aliases={w: w for w in range(nw)},
        scratch_shapes=[pltpu.SemaphoreType.DMA((ncp,))] * 4,
        compiler_params=_params(),
    )(*placed, *extra)


HBM_SPEC = pl.BlockSpec(memory_space=pltpu.HBM)
SEM_SPEC = pl.BlockSpec(memory_space=pltpu.SEMAPHORE)
SPLIT_EFFECT = pltpu.SideEffectType.DATAFLOW_SIDE_EFFECTING


def _in_hbm(a):
    return pltpu.with_memory_space_constraint(a, pltpu.HBM)


def _gather_copies(bufs, send, recv):
    x, y, c = _coords()
    chip = 2 * x + y
    cps = []
    for w, buf in enumerate(bufs):
        for k, (ox, oy) in enumerate(_other_chips(x, y)):
            mine, theirs = buf.at[chip, c], buf.at[2 * ox + oy, c]
            sems = (send.at[3 * w + k], recv.at[3 * w + k], (ox, oy, c))
            cps.append((_remote(mine, mine, *sems), _remote(theirs, theirs, *sems)))
    return cps


def _gather_start_call(bufs, after):
    nw = len(bufs)

    def body(*refs):
        ins, send, recv, token = refs[:nw], refs[nw + 1], refs[nw + 2], refs[2 * nw + 3]
        for out, _ in _gather_copies(ins, send, recv):
            out.start()
        token[...] = jnp.zeros(token.shape, F32)

    res = pl.pallas_call(
        body, name="weights_gather_start",
        in_specs=[HBM_SPEC] * nw + [ANY],
        out_specs=[SEM_SPEC, SEM_SPEC] + [HBM_SPEC] * nw + [pl.BlockSpec(memory_space=pltpu.VMEM)],
        out_shape=[pltpu.SemaphoreType.DMA((3 * nw,)), pltpu.SemaphoreType.DMA((3 * nw,))]
        + [pltpu.HBM(b.shape, b.dtype) for b in bufs] + [jax.ShapeDtypeStruct((8, LANES), F32)],
        input_output_aliases={w: 2 + w for w in range(nw)},
        compiler_params=pltpu.CompilerParams(has_side_effects=SPLIT_EFFECT),
    )(*[_in_hbm(b) for b in bufs], after)
    return res[0], res[1], list(res[2:2 + nw]), res[2 + nw]


def _gather_wait_call(bufs, send, recv, after):
    nw = len(bufs)

    def body(*refs):
        ins, send, recv = refs[:nw], refs[nw], refs[nw + 1]
        for out, back in _gather_copies(ins, send, recv):
            out.wait_send()
            back.wait_recv()

    return pl.pallas_call(
        body, name="weights_gather_wait",
        in_specs=[HBM_SPEC] * nw + [SEM_SPEC, SEM_SPEC, ANY],
        out_specs=[HBM_SPEC] * nw,
        out_shape=[pltpu.HBM(b.shape, b.dtype) for b in bufs],
        input_output_aliases={w: w for w in range(nw)},
        compiler_params=pltpu.CompilerParams(has_side_effects=SPLIT_EFFECT),
    )(*bufs, send, recv, after)


def _scatter_copies(srcs, lands, send, recv, wholes):
    x, y, c = _coords()
    me = 4 * x + 2 * y + c
    cps = []
    for w, (src, land) in enumerate(zip(srcs, lands)):
        for r in range(1, N_DEV):
            px, py, pc = ((1 - x) if r & 4 else x, (1 - y) if r & 2 else y, (1 - c) if r & 1 else c)
            sems = (send.at[(N_DEV - 1) * w + r - 1], recv.at[(N_DEV - 1) * w + r - 1], (px, py, pc))
            piece = src if wholes[w] else src.at[2 * px + py, pc]
            cps.append((_remote(piece, land.at[me], *sems), _remote(piece, land.at[4 * px + 2 * py + pc], *sems)))
    return cps


def _scatter_start_call(srcs, lands, wholes, name):
    nw = len(srcs)
    ncp = (N_DEV - 1) * nw

    def body(*refs):
        ins, lnd, send, recv, token = refs[:nw], refs[nw:2 * nw], refs[2 * nw], refs[2 * nw + 1], refs[4 * nw + 2]
        for out, _ in _scatter_copies(ins, lnd, send, recv, wholes):
            out.start()
        token[...] = jnp.zeros(token.shape, F32)

    res = pl.pallas_call(
        body, name=name,
        in_specs=[HBM_SPEC] * (2 * nw),
        out_specs=[SEM_SPEC, SEM_SPEC] + [HBM_SPEC] * (2 * nw) + [pl.BlockSpec(memory_space=pltpu.VMEM)],
        out_shape=[pltpu.SemaphoreType.DMA((ncp,)), pltpu.SemaphoreType.DMA((ncp,))]
        + [pltpu.HBM(b.shape, b.dtype) for b in list(srcs) + list(lands)] + [jax.ShapeDtypeStruct((8, LANES), F32)],
        input_output_aliases={i: 2 + i for i in range(2 * nw)},
        compiler_params=pltpu.CompilerParams(has_side_effects=SPLIT_EFFECT),
    )(*[_in_hbm(b) for b in list(srcs) + list(lands)])
    return res[0], res[1], list(res[2:2 + nw]), list(res[2 + nw:2 + 2 * nw]), res[2 + 2 * nw]


def _scatter_wait_call(srcs, lands, send, recv, after, wholes, name):
    nw = len(srcs)

    def body(*refs):
        ins, lnd, send, recv = refs[:nw], refs[nw:2 * nw], refs[2 * nw], refs[2 * nw + 1]
        for out, back in _scatter_copies(ins, lnd, send, recv, wholes):
            out.wait_send()
            back.wait_recv()

    res = pl.pallas_call(
        body, name=name,
        in_specs=[HBM_SPEC] * (2 * nw) + [SEM_SPEC, SEM_SPEC, ANY],
        out_specs=[HBM_SPEC] * (2 * nw),
        out_shape=[pltpu.HBM(b.shape, b.dtype) for b in list(srcs) + list(lands)],
        input_output_aliases={i: i for i in range(2 * nw)},
        compiler_params=pltpu.CompilerParams(has_side_effects=SPLIT_EFFECT),
    )(*srcs, *lands, send, recv, after)
    return list(res[nw:])


def _reduce_call(own, lands, idx, nch, name, dep=None):
    nw = len(own)
    deps = [] if dep is None else [dep]

    def body(idx_ref, *refs):
        refs = refs[:2 * nw] + refs[2 * nw + len(deps):]
        for w in range(nw):
            tot = refs[w][...]
            for r in range(1, N_DEV):
                tot = tot + refs[nw + w][idx_ref[1 + r]].astype(F32)
            refs[2 * nw + w][...] = tot

    in_specs, out_specs, out_shape = [], [], []
    for s in own:
        in_specs.append(pl.BlockSpec((None, None, s.shape[2] // nch, s.shape[3]),
                                     lambda i, idx_ref: (idx_ref[0], idx_ref[1], i, 0)))
    for s in own:
        in_specs.append(pl.BlockSpec((N_DEV, s.shape[2] // nch, s.shape[3]), lambda i, idx_ref: (0, i, 0)))
    for s in own:
        out_specs.append(pl.BlockSpec((None, s.shape[2] // nch, s.shape[3]), lambda i, idx_ref: (idx_ref[1], i, 0)))
        out_shape.append(jax.ShapeDtypeStruct((2,) + s.shape[2:], F32))
    return pl.pallas_call(
        body, name=name,
        grid_spec=pltpu.PrefetchScalarGridSpec(num_scalar_prefetch=1, grid=(nch,),
                                               in_specs=in_specs + [ANY] * len(deps), out_specs=out_specs),
        out_shape=out_shape,
        compiler_params=_params(("arbitrary",)),
    )(idx, *own, *lands, *deps)


def _pair_allgather_call(halves, name):
    nw = len(halves)

    def body(*refs):
        outs = refs[nw:2 * nw]
        send, recv = refs[2 * nw:]
        x, y, c = _coords()
        cps = []
        for w in range(nw):
            cp = _remote(outs[w].at[c], outs[w].at[c], send.at[w], recv.at[w], (x, y, 1 - c))
            cp.start()
            cps.append(cp)
        for w in range(nw):
            theirs = outs[w].at[1 - c]
            _remote(theirs, theirs, send.at[w], recv.at[w], (x, y, 1 - c)).wait_recv()
        for cp in cps:
            cp.wait_send()

    outs = pl.pallas_call(
        body, name=name,
        in_specs=[ANY] * nw, out_specs=[ANY] * nw,
        out_shape=[jax.ShapeDtypeStruct(h.shape, h.dtype) for h in halves],
        input_output_aliases={w: w for w in range(nw)},
        scratch_shapes=[pltpu.SemaphoreType.DMA((nw,))] * 2,
    )(*halves)
    return [o.reshape(2 * h.shape[1], h.shape[2]) for o, h in zip(outs, halves)]


def _adamw(w, g, m, v):
    m = ADAM_B1 * m + (1.0 - ADAM_B1) * g
    v = ADAM_B2 * v + (1.0 - ADAM_B2) * (g * g)
    m_hat = m / (1.0 - ADAM_B1 ** ADAM_STEP)
    v_hat = v / (1.0 - ADAM_B2 ** ADAM_STEP)
    delta = -ADAM_LR * (m_hat / (jnp.sqrt(v_hat) + ADAM_EPS) + ADAM_WD * w)
    return delta, m, v


def _adamw_call(ws, gs, ms, vs, nch, name):
    nw = len(ws)

    def body(*refs):
        for w in range(nw):
            g = refs[nw + w][...]
            delta, m, v = _adamw(refs[w][...], g, refs[2 * nw + w][...], refs[3 * nw + w][...])
            refs[4 * nw + w][...] = g
            refs[5 * nw + w][...] = delta
            refs[6 * nw + w][...] = m
            refs[7 * nw + w][...] = v

    specs = [pl.BlockSpec((a.shape[0] // nch, a.shape[1]), lambda i: (i, 0)) for a in ws]
    res = pl.pallas_call(
        body, name=name,
        grid=(nch,),
        in_specs=specs * 4, out_specs=specs * 4,
        out_shape=[jax.ShapeDtypeStruct(a.shape, F32) for a in ws] * 4,
        compiler_params=_params(("arbitrary",)),
    )(*ws, *gs, *ms, *vs)
    return res[:nw], res[nw:2 * nw], res[2 * nw:3 * nw], res[3 * nw:]


def _small_call(gathered, own, me_idx, w, m, v):
    def fold(row):
        tot = row[:, 0:LANES] + row[:, LANES:2 * LANES] + row[:, 2 * LANES:3 * LANES] + row[:, 3 * LANES:4 * LANES]
        return tot + pltpu.roll(tot, HEAD_DIM, axis=1)

    def body(me_ref, ga_ref, own_ref, w_ref, m_ref, v_ref, g_out, d_out, m_out, v_out):
        me = me_ref[0]
        term = lambda i: jnp.where(me == i, own_ref[...], ga_ref[i])
        g = term(0)
        for i in range(1, N_DEV):
            g = g + term(i)
        unfolded = g[4:5, :]
        folded = jnp.concatenate([fold(unfolded[:, :ATTN_WIDTH]), fold(unfolded[:, ATTN_WIDTH:]),
                                  jnp.zeros((1, 1024 - 2 * LANES), F32)], axis=-1)
        row = lax.broadcasted_iota(jnp.int32, g.shape, 0)
        g = jnp.where(row == 3, folded, g)
        delta, mm, vv = _adamw(w_ref[...], g, m_ref[...], v_ref[...])
        g_out[...] = g
        d_out[...] = delta
        m_out[...] = mm
        v_out[...] = vv

    vmem = pl.BlockSpec(memory_space=pltpu.VMEM)
    return pl.pallas_call(
        body, name="adamw_small",
        in_specs=[pl.BlockSpec(memory_space=pltpu.SMEM)] + [vmem] * 5,
        out_shape=[jax.ShapeDtypeStruct(w.shape, F32)] * 4,
        compiler_params=_params(),
    )(me_idx, gathered, own, w, m, v)


def _pack_small(p, folded=True, loss=None):
    z = lambda n: jnp.zeros((n,), F32)
    rows = [p["mix_norm_g"], p["mlp_norm_g"],
            jnp.concatenate([p["pool_scale"], p["rel_bias"].reshape(-1), z(1024 - POOL_WIDTH - N_BUCKETS * N_HEADS)])]
    if folded:
        rows += [jnp.concatenate([p["q_norm_g"], z(LANES - HEAD_DIM), p["k_norm_g"], z(1024 - LANES - HEAD_DIM)]), z(1024)]
    else:
        rows += [z(1024), jnp.concatenate([p["q_norm_g"], p["k_norm_g"]])]
    rows += [z(1024) if loss is None else jnp.concatenate([loss.reshape(1), z(1023)])]
    head = jnp.stack(rows + [z(1024)] * 2)
    return jnp.concatenate([head, p["pool_w"].reshape(-1, 1024)], axis=0)


def _unpack_small(a):
    return dict(
        mix_norm_g=a[0], mlp_norm_g=a[1], pool_scale=a[2, :POOL_WIDTH],
        rel_bias=a[2, POOL_WIDTH:POOL_WIDTH + N_BUCKETS * N_HEADS].reshape(N_BUCKETS, N_HEADS),
        q_norm_g=a[3, :HEAD_DIM], k_norm_g=a[3, LANES:LANES + HEAD_DIM],
        pool_w=a[8:].reshape(len(POOL_WINDOWS), LANES, LANES))


_WEIGHT_ORDER = ("mix_norm_g", "w_in", "pool_w", "pool_scale", "q_norm_g", "k_norm_g", "rel_bias", "w_out",
                 "mlp_norm_g", "w_up", "w_down")
_BIG = ("w_in", "w_out", "w_up", "w_down")


def kernel(x, mix_norm_g, w_in, pool_w, pool_scale, q_norm_g, k_norm_g, rel_bias, w_out, mlp_norm_g, w_up, w_down, loss_target, m_mix_norm_g, m_w_in, m_pool_w, m_pool_scale, m_q_norm_g, m_k_norm_g, m_rel_bias, m_w_out, m_mlp_norm_g, m_w_up, m_w_down, v_mix_norm_g, v_w_in, v_pool_w, v_pool_scale, v_q_norm_g, v_k_norm_g, v_rel_bias, v_w_out, v_mlp_norm_g, v_w_up, v_w_down):
    w = dict(mix_norm_g=mix_norm_g, w_in=w_in, pool_w=pool_w, pool_scale=pool_scale, q_norm_g=q_norm_g,
             k_norm_g=k_norm_g, rel_bias=rel_bias, w_out=w_out, mlp_norm_g=mlp_norm_g, w_up=w_up, w_down=w_down)
    m = dict(mix_norm_g=m_mix_norm_g, w_in=m_w_in, pool_w=m_pool_w, pool_scale=m_pool_scale, q_norm_g=m_q_norm_g,
             k_norm_g=m_k_norm_g, rel_bias=m_rel_bias, w_out=m_w_out, mlp_norm_g=m_mlp_norm_g, w_up=m_w_up, w_down=m_w_down)
    v = dict(mix_norm_g=v_mix_norm_g, w_in=v_w_in, pool_w=v_pool_w, pool_scale=v_pool_scale, q_norm_g=v_q_norm_g,
             k_norm_g=v_k_norm_g, rel_bias=v_rel_bias, w_out=v_w_out, mlp_norm_g=v_mlp_norm_g, w_up=v_w_up, w_down=v_w_down)
    xc, yc, cc = _coords()

    c_idx = jnp.reshape(cc, (1,)).astype(jnp.int32)
    chip_idx = jnp.reshape(2 * xc + yc, (1,)).astype(jnp.int32)
    me = 4 * xc + 2 * yc + cc
    whole = lambda t: t.reshape(t.shape[0], t.shape[1] * t.shape[2], t.shape[3])

    placed = [_halves(p) for p in _place_shards_call([w[n] for n in _BIG], chip_idx, nch=4)]
    win_f, bias = _allgather_call(placed[:1], from_chips=True, name="weights_allgather_in",
                                  meanwhile=_bias_table_work(rel_bias))
    wsend, wrecv, in_flight, started = _gather_start_call(placed[1:], win_f)

    def mlp_weights(after):
        landed = _gather_wait_call(in_flight, wsend, wrecv, after)
        wout_f, wup_f, wdown_f = _allgather_call(landed, from_chips=False, name="weights_pair_forward")
        return whole(wout_f).reshape(-1, wout_f.shape[-1]), whole(wup_f), whole(wdown_f)

    split = []

    def on_mlp_grads(*wire_grads):
        srcs = [_halves(g) for g in wire_grads]
        lands = [lax.empty((N_DEV,) + s.shape[2:], s.dtype) for s in srcs]
        split.extend(_scatter_start_call(srcs, lands, [False] * len(srcs), "grads_scatter_start"))
        return split[4]

    loss_part, dx, big_grads, small_grads = _local_grads(
        x[0], loss_target[0], mix_norm_g + started[0, 0], whole(win_f), pool_w, pool_scale, q_norm_g, k_norm_g, rel_bias,
        mlp_norm_g, mlp_weights, on_mlp_grads, bias)
    g_in, g_out, g_up, g_down = big_grads
    gsend, grecv, srcs_thru, lands_thru, _ = split
    lands_mlp = _scatter_wait_call(srcs_thru, lands_thru, gsend, grecv, g_in[1], [False] * 3, "grads_scatter_wait")

    small_own = _pack_small(small_grads, folded=False, loss=loss_part)
    last_srcs = [_halves(g_in[1]), small_own]
    last_lands = [lax.empty((N_DEV,) + last_srcs[0].shape[2:], WIRE_DTYPE), lax.empty((N_DEV,) + small_own.shape, F32)]
    lsend, lrecv, last_srcs, last_lands, last_started = _scatter_start_call(
        last_srcs, last_lands, [False, True], "grads_scatter_start_last")
    idx = jnp.concatenate([chip_idx, c_idx] + [jnp.reshape(jnp.bitwise_xor(me, r), (1,)) for r in range(1, N_DEV)])
    idx = idx.astype(jnp.int32)
    mlp = _BIG[1:]

    def update(names, own32, lands, tag, dep=None):
        halves = _reduce_call([_halves(g) for g in own32], lands, idx, 4, "grads_reduce_" + tag, dep)
        reduced = _pair_allgather_call(list(halves), "grads_pair_allgather_" + tag)
        return _adamw_call([w[n] for n in names], reduced, [m[n] for n in names], [v[n] for n in names], 8, "adamw_" + tag)

    out_mlp = update(mlp, [g_out[0], g_up[0], g_down[0]], lands_mlp, "mlp", last_started)
    land_in, small_all = _scatter_wait_call(last_srcs, last_lands, lsend, lrecv, out_mlp[3][-1], [False, True],
                                            "grads_scatter_wait_last")
    out_in = update(_BIG[:1], [g_in[0]], [land_in], "in")
    g_pack, d_pack, m_pack, v_pack = _small_call(
        small_all, small_own, jnp.reshape(me, (1,)).astype(jnp.int32), _pack_small(w), _pack_small(m), _pack_small(v))

    grads, deltas, new_m, new_v = (_unpack_small(a) for a in (g_pack, d_pack, m_pack, v_pack))
    for k, res in enumerate((grads, deltas, new_m, new_v)):
        res[_BIG[0]] = out_in[k][0]
        for i, n in enumerate(mlp):
            res[n] = out_mlp[k][i]
    loss = g_pack[LOSS_ROW, 0]
    return (loss, dx[None], *[grads[n] for n in _WEIGHT_ORDER], *[deltas[n] for n in _WEIGHT_ORDER],
            *[new_m[n] for n in _WEIGHT_ORDER], *[new_v[n] for n in _WEIGHT_ORDER])
```

```python
import math

import jax
import jax.numpy as jnp
import numpy as np
from jax import lax
from jax.experimental import pallas as pl
from jax.experimental.pallas import tpu as pltpu

F32 = jnp.float32
MXU_DTYPE = jnp.bfloat16
WIRE_DTYPE = jnp.bfloat16

NORM_EPS = 1e-6
NEG_INF = -1e30
LANES = 128
HEAD_DIM = 64
N_HEADS = 8
POOL_WIDTH = 512
ATTN_WIDTH = 512
POOL_WINDOWS = (2, 4, 8, 16)
POOL_HALO = 16
DILATED_PATTERNS = ((128, 1), (512, 4), (2048, 16))
ATT_BLOCK = 128
ATT_SUPER = ATT_BLOCK * max(dl for _, dl in DILATED_PATTERNS)
ATT_UNITS = ATT_SUPER // ATT_BLOCK
N_BUCKETS = 32
NO_BUCKET = -1
MAX_DISTANCE = 2048
N_CHIPS = 4
N_DEV = 8
ADAM_LR, ADAM_B1, ADAM_B2, ADAM_EPS, ADAM_WD, ADAM_STEP = 0.001, 0.9, 0.999, 1e-08, 0.01, 10
VMEM_LIMIT = 56 * 1024 * 1024
MESH = pl.DeviceIdType.MESH
ANY = pl.BlockSpec(memory_space=pl.ANY)

SMALL_ROWS = 72
LOSS_ROW = 5


def _mm(a, b):
    return jnp.dot(a, b, preferred_element_type=F32)


def _mm_nt(a, b):
    return lax.dot_general(a, b, (((1,), (1,)), ((), ())), preferred_element_type=F32)


def _mm_tn(a, b):
    return lax.dot_general(a, b, (((0,), (0,)), ((), ())), preferred_element_type=F32)


def _params(sem=None, **kw):
    if sem is not None:
        kw["dimension_semantics"] = sem
    return pltpu.CompilerParams(vmem_limit_bytes=VMEM_LIMIT, **kw)


def _low_half():
    return lax.broadcasted_iota(jnp.int32, (1, LANES), 1) < HEAD_DIM


def _head_sum_bcast(y):
    lo = _low_half()
    outs = []
    for j in range(y.shape[1] // LANES):
        c = y[:, j * LANES:(j + 1) * LANES]
        s_lo = jnp.sum(jnp.where(lo, c, 0.0), axis=-1, keepdims=True)
        s_hi = jnp.sum(jnp.where(lo, 0.0, c), axis=-1, keepdims=True)
        outs.append(jnp.where(lo, s_lo, s_hi))
    return jnp.concatenate(outs, axis=-1)


def _rms_bwd(dn, hn, r):
    return r * (dn - hn * jnp.mean(dn * hn, axis=-1, keepdims=True))


def _t5_bucket_np(dist):
    max_exact = N_BUCKETS // 2
    d_f = np.maximum(dist, 1).astype(np.float32)
    ratio = (np.log(d_f / np.float32(max_exact)) / np.float32(math.log(MAX_DISTANCE / max_exact))).astype(np.float32)
    large = max_exact + (ratio * np.float32(N_BUCKETS - max_exact)).astype(np.int32)
    large = np.minimum(large, N_BUCKETS - 1)
    return np.where(dist < max_exact, dist, large).astype(np.int32)


def _window_offsets(dl):
    if dl == 1:
        return _by4_positions(ATT_BLOCK), _by4_positions(2 * ATT_BLOCK)
    return np.arange(ATT_BLOCK), np.arange(2 * ATT_BLOCK)


def _bucket_tables():
    tables = []
    for _, dl in DILATED_PATTERNS:
        qq, kk = _window_offsets(dl)
        dist = qq[:, None] + ATT_BLOCK - kk[None, :]
        bucket = _t5_bucket_np(np.clip(dist, 0, ATT_BLOCK) * dl)
        tables.append(np.where((dist >= 0) & (dist <= ATT_BLOCK), bucket, NO_BUCKET))
    return np.stack(tables).astype(np.int32)


def _previous_block_keys():
    return np.stack([np.broadcast_to(_window_offsets(dl)[1][None, :] < ATT_BLOCK, (ATT_BLOCK, 2 * ATT_BLOCK))
                     for _, dl in DILATED_PATTERNS])


def _f1_call(x, g1, win, poolw, pscale, qg, kg, tm):
    s, d = x.shape
    nblk = s // tm

    def body(x_ref, g1_ref, win_ref, pw_ref, ps_ref, qg_ref, kg_ref,
             a_ref, pooled_ref, ypool_ref, q32_ref, k32_ref, qn_ref, kn_ref, v_ref, ubuf):
        i = pl.program_id(0)
        xv = x_ref[...]
        r = lax.rsqrt(jnp.mean(xv * xv, axis=-1, keepdims=True) + NORM_EPS)
        a = ((xv * r) * g1_ref[...]).astype(MXU_DTYPE)
        a_ref[...] = a
        u = _mm(a, win_ref[0])
        q = _mm(a, win_ref[1])
        k = _mm(a, win_ref[2])
        v_ref[...] = _mm(a, win_ref[3])
        q32_ref[...] = q
        k32_ref[...] = k
        rq = lax.rsqrt(_head_sum_bcast(q * q) * (1.0 / HEAD_DIM) + NORM_EPS)
        qn_ref[...] = ((q * rq) * qg_ref[...]) * (HEAD_DIM ** -0.5)
        rk = lax.rsqrt(_head_sum_bcast(k * k) * (1.0 / HEAD_DIM) + NORM_EPS)
        kn_ref[...] = (k * rk) * kg_ref[...]

        @pl.when(i == 0)
        def _():
            ubuf[0:POOL_HALO, :] = jnp.zeros((POOL_HALO, POOL_WIDTH), F32)

        @pl.when(i > 0)
        def _():
            ubuf[0:POOL_HALO, :] = ubuf[tm:tm + POOL_HALO, :]

        ubuf[POOL_HALO:POOL_HALO + tm, :] = u
        t = i * tm + lax.broadcasted_iota(jnp.int32, (tm, 1), 0)
        for g, w in enumerate(POOL_WINDOWS):
            ls = slice(g * LANES, (g + 1) * LANES)
            ug = u[:, ls]
            acc = ug
            for sh in range(1, w):
                acc = acc + ubuf[POOL_HALO - sh:POOL_HALO - sh + tm, ls]
            cnt = jnp.minimum(t + 1, w).astype(F32)
            pooled = (acc / cnt - ug).astype(MXU_DTYPE)
            pooled_ref[:, ls] = pooled
            ypool_ref[:, ls] = (_mm(pooled, pw_ref[g]) * ps_ref[:, ls]).astype(MXU_DTYPE)

    tok = lambda w: pl.BlockSpec((tm, w), lambda i: (i, 0))
    full = lambda shp: pl.BlockSpec(shp, lambda i: (0,) * len(shp))
    return pl.pallas_call(
        body, name="fwd_inproj",
        grid=(nblk,),
        in_specs=[tok(d), full((1, d)), full(win.shape), full(poolw.shape), full((1, POOL_WIDTH)),
                  full((1, ATTN_WIDTH)), full((1, ATTN_WIDTH))],
        out_specs=[tok(d), tok(POOL_WIDTH), tok(POOL_WIDTH), tok(ATTN_WIDTH), tok(ATTN_WIDTH),
                   tok(ATTN_WIDTH), tok(ATTN_WIDTH), tok(ATTN_WIDTH)],
        out_shape=[jax.ShapeDtypeStruct((s, d), MXU_DTYPE),
                   jax.ShapeDtypeStruct((s, POOL_WIDTH), MXU_DTYPE),
                   jax.ShapeDtypeStruct((s, POOL_WIDTH), MXU_DTYPE),
                   jax.ShapeDtypeStruct((s, ATTN_WIDTH), F32),
                   jax.ShapeDtypeStruct((s, ATTN_WIDTH), F32),
                   jax.ShapeDtypeStruct((s, ATTN_WIDTH), F32),
                   jax.ShapeDtypeStruct((s, ATTN_WIDTH), F32),
                   jax.ShapeDtypeStruct((s, ATTN_WIDTH), F32)],
        scratch_shapes=[pltpu.VMEM((tm + POOL_HALO, POOL_WIDTH), F32)],
        compiler_params=_params(("arbitrary",)),
    )(x, g1, win, poolw, pscale, qg, kg)


DEINT = 4
assert [dl for _, dl in DILATED_PATTERNS] == [1, DEINT, DEINT * DEINT]


def _by4_positions(n):
    pos = np.arange(n)
    return DEINT * (pos % (n // DEINT)) + pos // (n // DEINT)


def _masked_bias(b_ref, p, n):
    return b_ref[p, jnp.minimum(n, 1)].reshape(2 * ATT_BLOCK, 2 * ATT_BLOCK)


def _unit_rows(u, dl):
    sq, sk = ATT_SUPER // DEINT, 2 * ATT_SUPER // DEINT
    if dl == 1:
        n = ATT_BLOCK // DEINT
        return (u, [pl.ds(pl.multiple_of(r * sq + n * u, 8), n) for r in range(DEINT)],
                [pl.ds(pl.multiple_of(r * sk + sk // 2 + n * (u - 1), 8), 2 * n) for r in range(DEINT)])
    if dl == DEINT:
        r, b = u % DEINT, u // DEINT
        return (b, [pl.ds(pl.multiple_of(r * sq + ATT_BLOCK * b, 8), ATT_BLOCK)],
                [pl.ds(pl.multiple_of(r * sk + sk // 2 + ATT_BLOCK * (b - 1), 8), 2 * ATT_BLOCK)])
    r, a = u % DEINT, u // DEINT
    return 0, [pl.ds(r * sq + a, ATT_BLOCK, stride=DEINT)], [pl.ds(r * sk + a, 2 * ATT_BLOCK, stride=DEINT)]


def _take(ref, runs):
    parts = [ref[run, :] for run in runs]
    return parts[0] if len(parts) == 1 else jnp.concatenate(parts, axis=0)


def _put(ref, runs, value, add=False):
    n = value.shape[0] // len(runs)
    for i, run in enumerate(runs):
        part = value[i * n:(i + 1) * n]
        ref[run, :] = ref[run, :] + part if add else part


def _deinterleave(dst, src, n):
    seg = n // DEINT
    for r in range(DEINT):
        dst[r * seg:(r + 1) * seg, :] = src[pl.ds(r, seg, stride=DEINT), :]


def _deinterleave_pair(dst, prev, cur):
    seg = prev.shape[0] // DEINT
    for r in range(DEINT):
        dst[2 * r * seg:(2 * r + 1) * seg, :] = prev[pl.ds(r, seg, stride=DEINT), :]
        dst[(2 * r + 1) * seg:(2 * r + 2) * seg, :] = cur[pl.ds(r, seg, stride=DEINT), :]


def _interleave(dst, src, n, offset=0):
    seg = n // DEINT
    stride = src.shape[0] // DEINT
    for r in range(DEINT):
        dst[pl.ds(r, seg, stride=DEINT), :] = src[r * stride + offset:r * stride + offset + seg, :]


def _attn_fwd_call(qn, kn, v, bias):
    s, w = qn.shape
    nsb = s // ATT_SUPER
    npair = w // LANES

    def body(q_ref, kc_ref, kp_ref, vc_ref, vp_ref, b_ref, o_ref, lse_ref, qf, kf, vf, acc_s, m_s, l_s):
        sb = pl.program_id(1)
        _deinterleave(qf, q_ref, ATT_SUPER)
        _deinterleave_pair(kf, kp_ref, kc_ref)
        _deinterleave_pair(vf, vp_ref, vc_ref)
        lo = _low_half()
        for p, (_, dl) in enumerate(DILATED_PATTERNS):
            def unit(u, carry, p=p, dl=dl):
                b, rows_q, rows_k = _unit_rows(u, dl)
                qp = _take(qf, rows_q).astype(MXU_DTYPE)
                kcat = _take(kf, rows_k).astype(MXU_DTYPE)
                vcat = _take(vf, rows_k).astype(MXU_DTYPE)
                zero = jnp.zeros_like(qp)
                q2 = jnp.concatenate([jnp.where(lo, qp, zero), jnp.where(lo, zero, qp)], axis=0)
                sc = _mm_nt(q2, kcat) + _masked_bias(b_ref, p, sb * (ATT_UNITS // dl) + b)
                m2 = jnp.max(sc, axis=-1, keepdims=True)
                pr = jnp.exp(sc - m2)
                l2 = jnp.sum(pr, axis=-1, keepdims=True)
                acc2 = _mm(pr.astype(MXU_DTYPE), vcat)
                acc = jnp.where(lo, acc2[:ATT_BLOCK], acc2[ATT_BLOCK:])
                m = jnp.where(lo, m2[:ATT_BLOCK], m2[ATT_BLOCK:])
                l = jnp.where(lo, l2[:ATT_BLOCK], l2[ATT_BLOCK:])
                if p == 0:
                    _put(acc_s, rows_q, acc)
                    _put(m_s, rows_q, m)
                    _put(l_s, rows_q, l)
                else:
                    m_old = _take(m_s, rows_q)
                    m_new = jnp.maximum(m_old, m)
                    a_old = jnp.exp(m_old - m_new)
                    a_new = jnp.exp(m - m_new)
                    _put(acc_s, rows_q, a_old * _take(acc_s, rows_q) + a_new * acc)
                    _put(l_s, rows_q, a_old * _take(l_s, rows_q) + a_new * l)
                    _put(m_s, rows_q, m_new)
                return carry

            lax.fori_loop(0, ATT_UNITS, unit, 0, unroll=16)
        l = l_s[...]
        acc_s[...] = acc_s[...] / l
        m_s[...] = m_s[...] + jnp.log(l)
        _interleave(o_ref, acc_s, ATT_SUPER)
        _interleave(lse_ref, m_s, ATT_SUPER)

    cur = pl.BlockSpec((ATT_SUPER, LANES), lambda j, t: (t, j))
    prev = pl.BlockSpec((ATT_SUPER, LANES), lambda j, t: (jnp.maximum(t - 1, 0), j))
    bspec = pl.BlockSpec((len(DILATED_PATTERNS), 2, 2, ATT_BLOCK, 2 * ATT_BLOCK), lambda j, t: (0, 0, j, 0, 0))
    return pl.pallas_call(
        body, name="attn_fwd",
        grid=(npair, nsb),
        in_specs=[cur, cur, prev, cur, prev, bspec],
        out_specs=[cur, cur],
        out_shape=[jax.ShapeDtypeStruct((s, w), F32), jax.ShapeDtypeStruct((s, w), F32)],
        scratch_shapes=[pltpu.VMEM((ATT_SUPER, LANES), F32), pltpu.VMEM((2 * ATT_SUPER, LANES), F32),
                        pltpu.VMEM((2 * ATT_SUPER, LANES), F32), pltpu.VMEM((ATT_SUPER, LANES), F32),
                        pltpu.VMEM((ATT_SUPER, LANES), F32), pltpu.VMEM((ATT_SUPER, LANES), F32)],
        compiler_params=_params(("arbitrary", "arbitrary")),
    )(qn, kn, kn, v, v, bias)


def _attn_bwd_call(qn, kn, v, do, lse, delta, bias, dep=None):
    s, w = qn.shape
    nsb = s // ATT_SUPER
    npair = w // LANES
    deps = [] if dep is None else [dep]

    def body(q_ref, kc_ref, kp_ref, vc_ref, vp_ref, do_ref, lse_ref, dlt_ref, b_ref, *rest):
        dq_ref, dk_ref, dv_ref, db_ref, qf, kf, vf, dof, lsef, dltf, dqf, dkf, dvf = rest[len(deps):]
        step = pl.program_id(1)
        sb = nsb - 1 - step
        seg = ATT_SUPER // DEINT
        _deinterleave(qf, q_ref, ATT_SUPER)
        _deinterleave(dof, do_ref, ATT_SUPER)
        _deinterleave_pair(kf, kp_ref, kc_ref)
        _deinterleave_pair(vf, vp_ref, vc_ref)
        _deinterleave(lsef, lse_ref, ATT_SUPER)
        _deinterleave(dltf, dlt_ref, ATT_SUPER)

        @pl.when(step == 0)
        def _():
            db_ref[...] = jnp.zeros(db_ref.shape, F32)

        for acc in (dkf, dvf):
            for r in range(DEINT):
                this, before = pl.ds((2 * r + 1) * seg, seg), pl.ds(2 * r * seg, seg)

                @pl.when(step == 0)
                def _(acc=acc, this=this):
                    acc[this, :] = jnp.zeros((seg, LANES), F32)

                @pl.when(step > 0)
                def _(acc=acc, this=this, before=before):
                    acc[this, :] = acc[before, :]

                acc[before, :] = jnp.zeros((seg, LANES), F32)
        lo = _low_half()
        for p, (_, dl) in enumerate(DILATED_PATTERNS):
            def unit(u, carry, p=p, dl=dl):
                b, rows_q, rows_k = _unit_rows(u, dl)
                qp = _take(qf, rows_q).astype(MXU_DTYPE)
                dop = _take(dof, rows_q).astype(MXU_DTYPE)
                kcat = _take(kf, rows_k).astype(MXU_DTYPE)
                vcat = _take(vf, rows_k).astype(MXU_DTYPE)
                lse2 = _take(lsef, rows_q)
                dlt2 = _take(dltf, rows_q)
                zero = jnp.zeros_like(qp)
                q2 = jnp.concatenate([jnp.where(lo, qp, zero), jnp.where(lo, zero, qp)], axis=0)
                do2 = jnp.concatenate([jnp.where(lo, dop, zero), jnp.where(lo, zero, dop)], axis=0)
                lse_c = jnp.concatenate([lse2[:, 0:1], lse2[:, HEAD_DIM:HEAD_DIM + 1]], axis=0)
                dlt_c = jnp.concatenate([dlt2[:, 0:1], dlt2[:, HEAD_DIM:HEAD_DIM + 1]], axis=0)
                sc = _mm_nt(q2, kcat) + _masked_bias(b_ref, p, sb * (ATT_UNITS // dl) + b)
                pr = jnp.exp(sc - lse_c)
                ds = pr * (_mm_nt(do2, vcat) - dlt_c)
                db_ref[p] += ds.reshape(2, ATT_BLOCK, 2 * ATT_BLOCK)
                ds_c = ds.astype(MXU_DTYPE)
                dq2 = _mm(ds_c, kcat)
                dk = _mm_tn(ds_c, q2)
                dv = _mm_tn(pr.astype(MXU_DTYPE), do2)
                dq = jnp.where(lo, dq2[:ATT_BLOCK], dq2[ATT_BLOCK:])
                _put(dqf, rows_q, dq, add=p > 0)
                _put(dkf, rows_k, dk, add=True)
                _put(dvf, rows_k, dv, add=True)
                return carry

            lax.fori_loop(0, ATT_UNITS, unit, 0, unroll=16)
        _interleave(dq_ref, dqf, ATT_SUPER)
        _interleave(dk_ref, dkf, ATT_SUPER, offset=seg)
        _interleave(dv_ref, dvf, ATT_SUPER, offset=seg)

    cur = pl.BlockSpec((ATT_SUPER, LANES), lambda j, t: (nsb - 1 - t, j))
    prev = pl.BlockSpec((ATT_SUPER, LANES), lambda j, t: (jnp.maximum(nsb - 2 - t, 0), j))
    npat = len(DILATED_PATTERNS)
    bspec = pl.BlockSpec((npat, 2, 2, ATT_BLOCK, 2 * ATT_BLOCK), lambda j, t: (0, 0, j, 0, 0))
    dbspec = pl.BlockSpec((npat, 2, ATT_BLOCK, 2 * ATT_BLOCK), lambda j, t: (0, j, 0, 0))
    sup = lambda: pltpu.VMEM((ATT_SUPER, LANES), F32)
    sup2 = lambda: pltpu.VMEM((2 * ATT_SUPER, LANES), F32)
    return pl.pallas_call(
        body, name="attn_bwd",
        grid=(npair, nsb),
        in_specs=[cur, cur, prev, cur, prev, cur, cur, cur, bspec] + [ANY] * len(deps),
        out_specs=[cur, cur, cur, dbspec],
        out_shape=[jax.ShapeDtypeStruct((s, w), F32)] * 3
        + [jax.ShapeDtypeStruct((npat, N_HEADS, ATT_BLOCK, 2 * ATT_BLOCK), F32)],
        scratch_shapes=[sup(), sup2(), sup2(), sup(), sup(), sup(), sup(), sup2(), sup2()],
        compiler_params=_params(("arbitrary", "arbitrary")),
    )(qn, kn, kn, v, v, do, lse, delta, bias, *deps)


def _bias_table_work(rel_bias):
    buckets = jnp.asarray(_bucket_tables())
    prev_keys = jnp.asarray(_previous_block_keys().astype(np.int32))
    npat = buckets.shape[0]

    def body(rb_ref, bk_ref, pk_ref, out_ref):
        for p in range(npat):
            for half in range(2):
                ks = slice(half * ATT_BLOCK, (half + 1) * ATT_BLOCK)
                bk = bk_ref[p, :, ks]
                absent = pk_ref[p, :, ks] != 0
                for h in range(N_HEADS):
                    def pick(b, acc, h=h, bk=bk):
                        return jnp.where(bk == b, rb_ref[b, h], acc)

                    tab = lax.fori_loop(0, N_BUCKETS, pick, jnp.full((ATT_BLOCK, ATT_BLOCK), NEG_INF, F32))
                    out_ref[p, 1, h, :, ks] = tab
                    out_ref[p, 0, h, :, ks] = jnp.where(absent, NEG_INF, tab)

    vmem = pl.BlockSpec(memory_space=pltpu.VMEM)
    return ([rel_bias, buckets, prev_keys], [pl.BlockSpec(memory_space=pltpu.SMEM), vmem, vmem],
            jax.ShapeDtypeStruct((npat, 2, N_HEADS, ATT_BLOCK, 2 * ATT_BLOCK), F32), body)


def _bias_table_call(rel_bias):
    operands, specs, shape, body = _bias_table_work(rel_bias)
    return pl.pallas_call(body, name="bias_table", in_specs=specs, out_shape=shape, compiler_params=_params())(*operands)


def _rel_bias_grad_call(dbias, buckets):
    npat, nh = dbias.shape[0], dbias.shape[1]

    def body(db_ref, bk_ref, out_ref):
        lane = lax.broadcasted_iota(jnp.int32, (nh, LANES), 1)
        out = jnp.zeros((nh, LANES), F32)
        for b in range(N_BUCKETS):
            tot = jnp.zeros((nh, 1), F32)
            for p in range(npat):
                hit = jnp.where(bk_ref[p][None] == b, db_ref[p], 0.0)
                tot = tot + jnp.sum(jnp.sum(hit, axis=2), axis=1, keepdims=True)
            out = jnp.where(lane == b, tot, out)
        out_ref[...] = out

    return pl.pallas_call(
        body, name="rel_bias_grad",
        out_shape=jax.ShapeDtypeStruct((nh, LANES), F32),
        compiler_params=_params(),
    )(dbias, buckets)


def _f2_call(x, tgt, ypool, o, wout, wup, wdown, g2, tm):
    s, d = x.shape
    nblk = s // tm
    nch, _, fch = wup.shape
    dff = nch * fch
    mixw = POOL_WIDTH + ATTN_WIDTH

    def body(x_ref, t_ref, yp_ref, o_ref, g2_ref, wout_hbm, wup_hbm, wdown_hbm,
             mixed_ref, c_ref, ff_ref, dz_ref, dy_ref, dh1_ref, dyp_ref, do_ref, dlt_ref, dg2_ref, loss_ref,
             wout_v, wup_v, wdown_v, rz):
        i = pl.program_id(0)

        @pl.when(i == 0)
        def _():
            pltpu.sync_copy(wout_hbm, wout_v)
            pltpu.sync_copy(wup_hbm, wup_v)
            pltpu.sync_copy(wdown_hbm, wdown_v)
            dg2_ref[...] = jnp.zeros(dg2_ref.shape, F32)
            loss_ref[...] = jnp.zeros(loss_ref.shape, F32)

        o = o_ref[...]
        mixed = jnp.concatenate([yp_ref[...], o.astype(MXU_DTYPE)], axis=-1)
        mixed_ref[...] = mixed
        h1 = x_ref[...] + _mm(mixed, wout_v[...])
        r2 = lax.rsqrt(jnp.mean(h1 * h1, axis=-1, keepdims=True) + NORM_EPS)
        hn = h1 * r2
        c = (hn * g2_ref[...]).astype(MXU_DTYPE)
        c_ref[...] = c
        y = h1
        for j in range(nch):
            cs = slice(j * fch, (j + 1) * fch)
            z = jnp.maximum(_mm(c, wup_v[j]), 0.0)
            rz[:, cs] = z
            ff = (z * z).astype(MXU_DTYPE)
            ff_ref[:, cs] = ff
            y = y + _mm(ff, wdown_v[j])
        err = y - t_ref[...]
        loss_ref[...] += jnp.sum(err * err) * (0.5 / d)
        dy = err * (1.0 / d)
        dy_c = dy.astype(MXU_DTYPE)
        dy_ref[...] = dy_c
        dc = jnp.zeros((tm, d), F32)
        for j in range(nch):
            cs = slice(j * fch, (j + 1) * fch)
            dz = (_mm_nt(dy_c, wdown_v[j]) * (2.0 * rz[:, cs])).astype(MXU_DTYPE)
            dz_ref[:, cs] = dz
            dc = dc + _mm_nt(dz, wup_v[j])
        dg2_ref[...] += jnp.sum(dc * hn, axis=0, keepdims=True)
        dh1 = dy + _rms_bwd(dc * g2_ref[...], hn, r2)
        dh1_ref[...] = dh1
        dmix = _mm_nt(dh1.astype(MXU_DTYPE), wout_v[...])
        dyp_ref[...] = dmix[:, :POOL_WIDTH]
        do = dmix[:, POOL_WIDTH:]
        do_ref[...] = do
        dlt_ref[...] = _head_sum_bcast(do * o)

    tok = lambda w: pl.BlockSpec((tm, w), lambda i: (i, 0))
    const = lambda shp: pl.BlockSpec(shp, lambda i: (0,) * len(shp))
    return pl.pallas_call(
        body, name="fwd_mlp_bwd_mlp",
        grid=(nblk,),
        in_specs=[tok(d), tok(d), tok(POOL_WIDTH), tok(ATTN_WIDTH), const((1, d)), ANY, ANY, ANY],
        out_specs=[tok(mixw), tok(d), tok(dff), tok(dff), tok(d), tok(d), tok(POOL_WIDTH), tok(ATTN_WIDTH),
                   tok(ATTN_WIDTH), const((1, d)), const((1, LANES))],
        out_shape=[jax.ShapeDtypeStruct((s, mixw), MXU_DTYPE),
                   jax.ShapeDtypeStruct((s, d), MXU_DTYPE),
                   jax.ShapeDtypeStruct((s, dff), MXU_DTYPE),
                   jax.ShapeDtypeStruct((s, dff), MXU_DTYPE),
                   jax.ShapeDtypeStruct((s, d), MXU_DTYPE),
                   jax.ShapeDtypeStruct((s, d), F32),
                   jax.ShapeDtypeStruct((s, POOL_WIDTH), F32),
                   jax.ShapeDtypeStruct((s, ATTN_WIDTH), F32),
                   jax.ShapeDtypeStruct((s, ATTN_WIDTH), F32),
                   jax.ShapeDtypeStruct((1, d), F32),
                   jax.ShapeDtypeStruct((1, LANES), F32)],
        scratch_shapes=[pltpu.VMEM(wout.shape, MXU_DTYPE), pltpu.VMEM(wup.shape, MXU_DTYPE),
                        pltpu.VMEM(wdown.shape, MXU_DTYPE), pltpu.VMEM((tm, dff), F32)],
        compiler_params=_params(("arbitrary",)),
    )(x, tgt, ypool, o, g2, wout, wup, wdown)


def _bproj_call(dqn, dkn, dv, q32, k32, dypool, pooled, x, dh1, win, poolw, pscale, qg, kg, g1, tm):
    s, d = x.shape
    nblk = s // tm
    ngrp = len(POOL_WINDOWS)

    def body(dqn_ref, dkn_ref, dv_ref, q_ref, k_ref, dyp_ref, pooled_ref, x_ref, dh1_ref,
             win_hbm, pw_ref, ps_ref, qg_ref, kg_ref, g1_ref,
             dx_ref, dproj_ref, dg1_ref, dqg_ref, dkg_ref, dpw_ref, dps_ref, win_v, ebuf):
        step = pl.program_id(0)
        i = nblk - 1 - step

        @pl.when(step == 0)
        def _():
            pltpu.sync_copy(win_hbm, win_v)
            dg1_ref[...] = jnp.zeros(dg1_ref.shape, F32)
            dqg_ref[...] = jnp.zeros(dqg_ref.shape, F32)
            dkg_ref[...] = jnp.zeros(dkg_ref.shape, F32)
            dpw_ref[...] = jnp.zeros(dpw_ref.shape, F32)
            dps_ref[...] = jnp.zeros(dps_ref.shape, F32)
            ebuf[tm:tm + POOL_HALO, :] = jnp.zeros((POOL_HALO, POOL_WIDTH), F32)

        @pl.when(step > 0)
        def _():
            ebuf[tm:tm + POOL_HALO, :] = ebuf[0:POOL_HALO, :]

        def qk_bwd(dn_sum, raw, gain, scale, dgain_ref):
            rr = lax.rsqrt(_head_sum_bcast(raw * raw) * (1.0 / HEAD_DIM) + NORM_EPS)
            hn = raw * rr
            dgain_ref[...] += jnp.sum(dn_sum * hn, axis=0, keepdims=True) * scale
            dn = dn_sum * (gain * scale)
            return rr * (dn - hn * (_head_sum_bcast(dn * hn) * (1.0 / HEAD_DIM)))

        dq = qk_bwd(dqn_ref[...], q_ref[...], qg_ref[...], HEAD_DIM ** -0.5, dqg_ref)
        dk = qk_bwd(dkn_ref[...], k_ref[...], kg_ref[...], 1.0, dkg_ref)

        t = i * tm + lax.broadcasted_iota(jnp.int32, (tm, 1), 0)
        dpooled = []
        for g, w in enumerate(POOL_WINDOWS):
            ls = slice(g * LANES, (g + 1) * LANES)
            dm = dyp_ref[:, ls]
            pg = pooled_ref[:, ls]
            dps_ref[:, ls] += jnp.sum(dm * _mm(pg, pw_ref[g]), axis=0, keepdims=True)
            dms = (dm * ps_ref[:, ls]).astype(MXU_DTYPE)
            dpw_ref[g] += _mm_tn(pg, dms)
            dpg = _mm_nt(dms, pw_ref[g])
            dpooled.append(dpg)
            ebuf[0:tm, ls] = dpg / jnp.minimum(t + 1, w).astype(F32)
        du = []
        for g, w in enumerate(POOL_WINDOWS):
            ls = slice(g * LANES, (g + 1) * LANES)
            acc = ebuf[0:tm, ls]
            for sh in range(1, w):
                acc = acc + ebuf[sh:sh + tm, ls]
            du.append(acc - dpooled[g])
        parts = [jnp.concatenate(du, axis=-1), dq, dk, dv_ref[...]]
        da = jnp.zeros((tm, d), F32)
        for p, part in enumerate(parts):
            pc = part.astype(MXU_DTYPE)
            dproj_ref[:, p * POOL_WIDTH:(p + 1) * POOL_WIDTH] = pc
            da = da + _mm_nt(pc, win_v[p])
        xv = x_ref[...]
        r = lax.rsqrt(jnp.mean(xv * xv, axis=-1, keepdims=True) + NORM_EPS)
        xn = xv * r
        dg1_ref[...] += jnp.sum(da * xn, axis=0, keepdims=True)
        dx_ref[...] = dh1_ref[...] + _rms_bwd(da * g1_ref[...], xn, r)

    tok = lambda w: pl.BlockSpec((tm, w), lambda t: (nblk - 1 - t, 0))
    const = lambda shp: pl.BlockSpec(shp, lambda t: (0,) * len(shp))
    return pl.pallas_call(
        body, name="bwd_inproj",
        grid=(nblk,),
        in_specs=[tok(ATTN_WIDTH)] * 5 + [tok(POOL_WIDTH), tok(POOL_WIDTH), tok(d), tok(d),
                                          ANY, const(poolw.shape), const((1, POOL_WIDTH)), const((1, ATTN_WIDTH)),
                                          const((1, ATTN_WIDTH)), const((1, d))],
        out_specs=[tok(d), tok(4 * POOL_WIDTH), const((1, d)), const((1, ATTN_WIDTH)), const((1, ATTN_WIDTH)),
                   const((ngrp, LANES, LANES)), const((1, POOL_WIDTH))],
        out_shape=[jax.ShapeDtypeStruct((s, d), F32),
                   jax.ShapeDtypeStruct((s, 4 * POOL_WIDTH), MXU_DTYPE),
                   jax.ShapeDtypeStruct((1, d), F32),
                   jax.ShapeDtypeStruct((1, ATTN_WIDTH), F32),
                   jax.ShapeDtypeStruct((1, ATTN_WIDTH), F32),
                   jax.ShapeDtypeStruct((ngrp, LANES, LANES), F32),
                   jax.ShapeDtypeStruct((1, POOL_WIDTH), F32)],
        scratch_shapes=[pltpu.VMEM(win.shape, MXU_DTYPE), pltpu.VMEM((tm + POOL_HALO, POOL_WIDTH), F32)],
        compiler_params=_params(("arbitrary",)),
    )(dqn, dkn, dv, q32, k32, dypool, pooled, x, dh1, win, poolw, pscale, qg, kg, g1)


def _wgrad_call(a, b, bm, bn, bk, out_shape, out_block, out_index, name):
    s, m = a.shape
    _, n = b.shape
    nk = s // bk

    def body(a_ref, b_ref, o_ref, wire_ref):
        k = pl.program_id(2)

        @pl.when(k == 0)
        def _():
            o_ref[...] = jnp.zeros(o_ref.shape, F32)

        o_ref[...] += _mm_tn(a_ref[...].astype(MXU_DTYPE), b_ref[...].astype(MXU_DTYPE))

        @pl.when(k == nk - 1)
        def _():
            wire_ref[...] = o_ref[...].astype(WIRE_DTYPE)

    return pl.pallas_call(
        body, name=name,
        grid=(m // bm, n // bn, nk),
        in_specs=[pl.BlockSpec((bk, bm), lambda i, j, k: (k, i)), pl.BlockSpec((bk, bn), lambda i, j, k: (k, j))],
        out_specs=[pl.BlockSpec(out_block, out_index)] * 2,
        out_shape=[jax.ShapeDtypeStruct(out_shape, F32), jax.ShapeDtypeStruct(out_shape, WIRE_DTYPE)],
        compiler_params=_params(("arbitrary", "arbitrary", "arbitrary")),
    )(a, b)


def _local_grads(x, tgt, g1, win, poolw, pscale, qg, kg, rel_bias, g2, mlp_weights, on_mlp_grads=None, bias=None):
    s, d = x.shape
    g1r, g2r = g1.reshape(1, d), g2.reshape(1, d)
    psr = pscale.reshape(1, POOL_WIDTH)
    qgr = jnp.tile(qg, N_HEADS).reshape(1, ATTN_WIDTH)
    kgr = jnp.tile(kg, N_HEADS).reshape(1, ATTN_WIDTH)
    pw_c = poolw.astype(MXU_DTYPE)
    buckets = jnp.asarray(_bucket_tables())
    bias = _bias_table_call(rel_bias) if bias is None else bias
    bk = min(s, 4096)

    a, pooled, ypool, q32, k32, qn, kn, v = _f1_call(x, g1r, win, pw_c, psr, qgr, kgr, tm=512)
    o, lse = _attn_fwd_call(qn, kn, v, bias)
    wout, wup, wdown = mlp_weights(o)
    mixed, c, ff, dz, dy, dh1, dypool, do, delta, dg2, loss = _f2_call(x, tgt, ypool, o, wout, wup, wdown, g2r, tm=256)
    dff = ff.shape[1]
    g_out = [g.reshape(N_CHIPS, d // N_CHIPS, d)
             for g in _wgrad_call(mixed, dh1, d, d, bk // 2, (d, d), (d, d), lambda i, j, k: (0, 0), "wgrad_out")]
    g_up = _wgrad_call(c, dz, d, dff // N_CHIPS, bk, (N_CHIPS, d, dff // N_CHIPS), (None, d, dff // N_CHIPS),
                       lambda i, j, k: (j, 0, 0), "wgrad_up")
    g_down = _wgrad_call(ff, dy, dff // N_CHIPS, d, bk, (N_CHIPS, dff // N_CHIPS, d), (None, dff // N_CHIPS, d),
                         lambda i, j, k: (i, 0, 0), "wgrad_down")
    dep = None if on_mlp_grads is None else on_mlp_grads(g_out[1], g_up[1], g_down[1])
    dqn, dkn, dv, dbias = _attn_bwd_call(qn, kn, v, do, lse, delta, bias, dep)
    dx, dproj, dg1, dqg, dkg, dpw, dps = _bproj_call(
        dqn, dkn, dv, q32, k32, dypool, pooled, x, dh1, win, pw_c, psr, qgr, kgr, g1r, tm=256)
    nin = dproj.shape[1] // N_CHIPS
    g_in = _wgrad_call(a, dproj, d, nin, bk, (N_CHIPS, d, nin), (None, d, nin), lambda i, j, k: (j, 0, 0), "wgrad_in")
    drb = _rel_bias_grad_call(dbias, buckets)
    small = dict(
        mix_norm_g=dg1.reshape(d), mlp_norm_g=dg2.reshape(d), pool_scale=dps.reshape(POOL_WIDTH),
        q_norm_g=dqg.reshape(ATTN_WIDTH), k_norm_g=dkg.reshape(ATTN_WIDTH),
        rel_bias=drb[:, :N_BUCKETS].T, pool_w=dpw)
    return loss[0, 0], dx, (g_in, g_out, g_up, g_down), small


def _coords():
    return lax.axis_index("x"), lax.axis_index("y"), lax.axis_index("c")


def _other_chips(x, y):
    return [(1 - x, y), (x, 1 - y), (1 - x, 1 - y)]


def _remote(src, dst, send_sem, recv_sem, dev):
    return pltpu.make_async_remote_copy(src_ref=src, dst_ref=dst, send_sem=send_sem, recv_sem=recv_sem,
                                        device_id=dev, device_id_type=MESH)


def _halves(a):
    return a.reshape(a.shape[:-2] + (2, a.shape[-2] // 2, a.shape[-1]))


def _place_shards_call(shards, chip_idx, nch):
    nw = len(shards)

    def body(chip_ref, *refs):
        for w in range(nw):
            refs[nw + w][...] = refs[w][...].astype(WIRE_DTYPE)

    in_specs = [pl.BlockSpec((s.shape[0] // nch, s.shape[1]), lambda i, chip_ref: (i, 0)) for s in shards]
    out_specs = [pl.BlockSpec((None, s.shape[0] // nch, s.shape[1]), lambda i, chip_ref: (chip_ref[0], i, 0))
                 for s in shards]
    return pl.pallas_call(
        body, name="weights_place",
        grid_spec=pltpu.PrefetchScalarGridSpec(num_scalar_prefetch=1, grid=(nch,),
                                               in_specs=in_specs, out_specs=out_specs),
        out_shape=[jax.ShapeDtypeStruct((N_CHIPS,) + s.shape, WIRE_DTYPE) for s in shards],
        compiler_params=_params(("arbitrary",)),
    )(chip_idx, *shards)


def _allgather_call(placed, from_chips, name, meanwhile=None):
    nw = len(placed)
    ncp = 3 * nw
    extra, extra_specs, extra_shape, extra_body = meanwhile if meanwhile else ([], [], None, None)
    ne = len(extra)

    def body(*refs):
        outs = refs[nw + ne:2 * nw + ne]
        send1, recv1, send2, recv2 = refs[-4:]
        x, y, c = _coords()
        chip = 2 * x + y
        others = _other_chips(x, y)
        first, passed = [], []
        if from_chips:
            for w in range(nw):
                for k, (ox, oy) in enumerate(others):
                    mine = outs[w].at[chip, c]
                    cp = _remote(mine, mine, send1.at[3 * w + k], recv1.at[3 * w + k], (ox, oy, c))
                    cp.start()
                    first.append(cp)
        if meanwhile:
            extra_body(*refs[nw:nw + ne], refs[2 * nw + ne])
        for w in range(nw):
            for k, (ox, oy) in enumerate(others):
                piece = outs[w].at[2 * ox + oy, c]
                if from_chips:
                    _remote(piece, piece, send1.at[3 * w + k], recv1.at[3 * w + k], (ox, oy, c)).wait_recv()
                cp = _remote(piece, piece, send2.at[3 * w + k], recv2.at[3 * w + k], (x, y, 1 - c))
                cp.start()
                passed.append(cp)
        for w in range(nw):
            for k, (ox, oy) in enumerate(others):
                piece = outs[w].at[2 * ox + oy, 1 - c]
                _remote(piece, piece, send2.at[3 * w + k], recv2.at[3 * w + k], (x, y, 1 - c)).wait_recv()
        for cp in first + passed:
            cp.wait_send()

    return pl.pallas_call(
        body, name=name,
        in_specs=[ANY] * nw + list(extra_specs),
        out_specs=[ANY] * nw + ([pl.BlockSpec(memory_space=pltpu.VMEM)] if meanwhile else []),
        out_shape=[jax.ShapeDtypeStruct(s.shape, s.dtype) for s in placed] + ([extra_shape] if meanwhile else []),
        input_output_aliases={w: w for w in range(nw)},
        scratch_shapes=[pltpu.SemaphoreType.DMA((ncp,))] * 4,
        compiler_params=_params(),
    )(*placed, *extra)


HBM_SPEC = pl.BlockSpec(memory_space=pltpu.HBM)
SEM_SPEC = pl.BlockSpec(memory_space=pltpu.SEMAPHORE)
SPLIT_EFFECT = pltpu.SideEffectType.DATAFLOW_SIDE_EFFECTING


def _in_hbm(a):
    return pltpu.with_memory_space_constraint(a, pltpu.HBM)


def _gather_copies(bufs, send, recv):
    x, y, c = _coords()
    chip = 2 * x + y
    cps = []
    for w, buf in enumerate(bufs):
        for k, (ox, oy) in enumerate(_other_chips(x, y)):
            mine, theirs = buf.at[chip, c], buf.at[2 * ox + oy, c]
            sems = (send.at[3 * w + k], recv.at[3 * w + k], (ox, oy, c))
            cps.append((_remote(mine, mine, *sems), _remote(theirs, theirs, *sems)))
    return cps


def _gather_start_call(bufs, after):
    nw = len(bufs)

    def body(*refs):
        ins, send, recv, token = refs[:nw], refs[nw + 1], refs[nw + 2], refs[2 * nw + 3]
        for out, _ in _gather_copies(ins, send, recv):
            out.start()
        token[...] = jnp.zeros(token.shape, F32)

    res = pl.pallas_call(
        body, name="weights_gather_start",
        in_specs=[HBM_SPEC] * nw + [ANY],
        out_specs=[SEM_SPEC, SEM_SPEC] + [HBM_SPEC] * nw + [pl.BlockSpec(memory_space=pltpu.VMEM)],
        out_shape=[pltpu.SemaphoreType.DMA((3 * nw,)), pltpu.SemaphoreType.DMA((3 * nw,))]
        + [pltpu.HBM(b.shape, b.dtype) for b in bufs] + [jax.ShapeDtypeStruct((8, LANES), F32)],
        input_output_aliases={w: 2 + w for w in range(nw)},
        compiler_params=pltpu.CompilerParams(has_side_effects=SPLIT_EFFECT),
    )(*[_in_hbm(b) for b in bufs], after)
    return res[0], res[1], list(res[2:2 + nw]), res[2 + nw]


def _gather_wait_call(bufs, send, recv, after):
    nw = len(bufs)

    def body(*refs):
        ins, send, recv = refs[:nw], refs[nw], refs[nw + 1]
        for out, back in _gather_copies(ins, send, recv):
            out.wait_send()
            back.wait_recv()

    return pl.pallas_call(
        body, name="weights_gather_wait",
        in_specs=[HBM_SPEC] * nw + [SEM_SPEC, SEM_SPEC, ANY],
        out_specs=[HBM_SPEC] * nw,
        out_shape=[pltpu.HBM(b.shape, b.dtype) for b in bufs],
        input_output_aliases={w: w for w in range(nw)},
        compiler_params=pltpu.CompilerParams(has_side_effects=SPLIT_EFFECT),
    )(*bufs, send, recv, after)


def _scatter_copies(srcs, lands, send, recv, wholes):
    x, y, c = _coords()
    me = 4 * x + 2 * y + c
    cps = []
    for w, (src, land) in enumerate(zip(srcs, lands)):
        for r in range(1, N_DEV):
            px, py, pc = ((1 - x) if r & 4 else x, (1 - y) if r & 2 else y, (1 - c) if r & 1 else c)
            sems = (send.at[(N_DEV - 1) * w + r - 1], recv.at[(N_DEV - 1) * w + r - 1], (px, py, pc))
            piece = src if wholes[w] else src.at[2 * px + py, pc]
            cps.append((_remote(piece, land.at[me], *sems), _remote(piece, land.at[4 * px + 2 * py + pc], *sems)))
    return cps


def _scatter_start_call(srcs, lands, wholes, name):
    nw = len(srcs)
    ncp = (N_DEV - 1) * nw

    def body(*refs):
        ins, lnd, send, recv, token = refs[:nw], refs[nw:2 * nw], refs[2 * nw], refs[2 * nw + 1], refs[4 * nw + 2]
        for out, _ in _scatter_copies(ins, lnd, send, recv, wholes):
            out.start()
        token[...] = jnp.zeros(token.shape, F32)

    res = pl.pallas_call(
        body, name=name,
        in_specs=[HBM_SPEC] * (2 * nw),
        out_specs=[SEM_SPEC, SEM_SPEC] + [HBM_SPEC] * (2 * nw) + [pl.BlockSpec(memory_space=pltpu.VMEM)],
        out_shape=[pltpu.SemaphoreType.DMA((ncp,)), pltpu.SemaphoreType.DMA((ncp,))]
        + [pltpu.HBM(b.shape, b.dtype) for b in list(srcs) + list(lands)] + [jax.ShapeDtypeStruct((8, LANES), F32)],
        input_output_aliases={i: 2 + i for i in range(2 * nw)},
        compiler_params=pltpu.CompilerParams(has_side_effects=SPLIT_EFFECT),
    )(*[_in_hbm(b) for b in list(srcs) + list(lands)])
    return res[0], res[1], list(res[2:2 + nw]), list(res[2 + nw:2 + 2 * nw]), res[2 + 2 * nw]


def _scatter_wait_call(srcs, lands, send, recv, after, wholes, name):
    nw = len(srcs)

    def body(*refs):
        ins, lnd, send, recv = refs[:nw], refs[nw:2 * nw], refs[2 * nw], refs[2 * nw + 1]
        for out, back in _scatter_copies(ins, lnd, send, recv, wholes):
            out.wait_send()
            back.wait_recv()

    res = pl.pallas_call(
        body, name=name,
        in_specs=[HBM_SPEC] * (2 * nw) + [SEM_SPEC, SEM_SPEC, ANY],
        out_specs=[HBM_SPEC] * (2 * nw),
        out_shape=[pltpu.HBM(b.shape, b.dtype) for b in list(srcs) + list(lands)],
        input_output_aliases={i: i for i in range(2 * nw)},
        compiler_params=pltpu.CompilerParams(has_side_effects=SPLIT_EFFECT),
    )(*srcs, *lands, send, recv, after)
    return list(res[nw:])


def _reduce_call(own, lands, idx, nch, name, dep=None):
    nw = len(own)
    deps = [] if dep is None else [dep]

    def body(idx_ref, *refs):
        refs = refs[:2 * nw] + refs[2 * nw + len(deps):]
        for w in range(nw):
            tot = refs[w][...]
            for r in range(1, N_DEV):
                tot = tot + refs[nw + w][idx_ref[1 + r]].astype(F32)
            refs[2 * nw + w][...] = tot

    in_specs, out_specs, out_shape = [], [], []
    for s in own:
        in_specs.append(pl.BlockSpec((None, None, s.shape[2] // nch, s.shape[3]),
                                     lambda i, idx_ref: (idx_ref[0], idx_ref[1], i, 0)))
    for s in own:
        in_specs.append(pl.BlockSpec((N_DEV, s.shape[2] // nch, s.shape[3]), lambda i, idx_ref: (0, i, 0)))
    for s in own:
        out_specs.append(pl.BlockSpec((None, s.shape[2] // nch, s.shape[3]), lambda i, idx_ref: (idx_ref[1], i, 0)))
        out_shape.append(jax.ShapeDtypeStruct((2,) + s.shape[2:], F32))
    return pl.pallas_call(
        body, name=name,
        grid_spec=pltpu.PrefetchScalarGridSpec(num_scalar_prefetch=1, grid=(nch,),
                                               in_specs=in_specs + [ANY] * len(deps), out_specs=out_specs),
        out_shape=out_shape,
        compiler_params=_params(("arbitrary",)),
    )(idx, *own, *lands, *deps)


def _pair_allgather_call(halves, name):
    nw = len(halves)

    def body(*refs):
        outs = refs[nw:2 * nw]
        send, recv = refs[2 * nw:]
        x, y, c = _coords()
        cps = []
        for w in range(nw):
            cp = _remote(outs[w].at[c], outs[w].at[c], send.at[w], recv.at[w], (x, y, 1 - c))
            cp.start()
            cps.append(cp)
        for w in range(nw):
            theirs = outs[w].at[1 - c]
            _remote(theirs, theirs, send.at[w], recv.at[w], (x, y, 1 - c)).wait_recv()
        for cp in cps:
            cp.wait_send()

    outs = pl.pallas_call(
        body, name=name,
        in_specs=[ANY] * nw, out_specs=[ANY] * nw,
        out_shape=[jax.ShapeDtypeStruct(h.shape, h.dtype) for h in halves],
        input_output_aliases={w: w for w in range(nw)},
        scratch_shapes=[pltpu.SemaphoreType.DMA((nw,))] * 2,
    )(*halves)
    return [o.reshape(2 * h.shape[1], h.shape[2]) for o, h in zip(outs, halves)]


def _adamw(w, g, m, v):
    m = ADAM_B1 * m + (1.0 - ADAM_B1) * g
    v = ADAM_B2 * v + (1.0 - ADAM_B2) * (g * g)
    m_hat = m / (1.0 - ADAM_B1 ** ADAM_STEP)
    v_hat = v / (1.0 - ADAM_B2 ** ADAM_STEP)
    delta = -ADAM_LR * (m_hat / (jnp.sqrt(v_hat) + ADAM_EPS) + ADAM_WD * w)
    return delta, m, v


def _adamw_call(ws, gs, ms, vs, nch, name):
    nw = len(ws)

    def body(*refs):
        for w in range(nw):
            g = refs[nw + w][...]
            delta, m, v = _adamw(refs[w][...], g, refs[2 * nw + w][...], refs[3 * nw + w][...])
            refs[4 * nw + w][...] = g
            refs[5 * nw + w][...] = delta
            refs[6 * nw + w][...] = m
            refs[7 * nw + w][...] = v

    specs = [pl.BlockSpec((a.shape[0] // nch, a.shape[1]), lambda i: (i, 0)) for a in ws]
    res = pl.pallas_call(
        body, name=name,
        grid=(nch,),
        in_specs=specs * 4, out_specs=specs * 4,
        out_shape=[jax.ShapeDtypeStruct(a.shape, F32) for a in ws] * 4,
        compiler_params=_params(("arbitrary",)),
    )(*ws, *gs, *ms, *vs)
    return res[:nw], res[nw:2 * nw], res[2 * nw:3 * nw], res[3 * nw:]


def _small_call(gathered, own, me_idx, w, m, v):
    def fold(row):
        tot = row[:, 0:LANES] + row[:, LANES:2 * LANES] + row[:, 2 * LANES:3 * LANES] + row[:, 3 * LANES:4 * LANES]
        return tot + pltpu.roll(tot, HEAD_DIM, axis=1)

    def body(me_ref, ga_ref, own_ref, w_ref, m_ref, v_ref, g_out, d_out, m_out, v_out):
        me = me_ref[0]
        term = lambda i: jnp.where(me == i, own_ref[...], ga_ref[i])
        g = term(0)
        for i in range(1, N_DEV):
            g = g + term(i)
        unfolded = g[4:5, :]
        folded = jnp.concatenate([fold(unfolded[:, :ATTN_WIDTH]), fold(unfolded[:, ATTN_WIDTH:]),
                                  jnp.zeros((1, 1024 - 2 * LANES), F32)], axis=-1)
        row = lax.broadcasted_iota(jnp.int32, g.shape, 0)
        g = jnp.where(row == 3, folded, g)
        delta, mm, vv = _adamw(w_ref[...], g, m_ref[...], v_ref[...])
        g_out[...] = g
        d_out[...] = delta
        m_out[...] = mm
        v_out[...] = vv

    vmem = pl.BlockSpec(memory_space=pltpu.VMEM)
    return pl.pallas_call(
        body, name="adamw_small",
        in_specs=[pl.BlockSpec(memory_space=pltpu.SMEM)] + [vmem] * 5,
        out_shape=[jax.ShapeDtypeStruct(w.shape, F32)] * 4,
        compiler_params=_params(),
    )(me_idx, gathered, own, w, m, v)


def _pack_small(p, folded=True, loss=None):
    z = lambda n: jnp.zeros((n,), F32)
    rows = [p["mix_norm_g"], p["mlp_norm_g"],
            jnp.concatenate([p["pool_scale"], p["rel_bias"].reshape(-1), z(1024 - POOL_WIDTH - N_BUCKETS * N_HEADS)])]
    if folded:
        rows += [jnp.concatenate([p["q_norm_g"], z(LANES - HEAD_DIM), p["k_norm_g"], z(1024 - LANES - HEAD_DIM)]), z(1024)]
    else:
        rows += [z(1024), jnp.concatenate([p["q_norm_g"], p["k_norm_g"]])]
    rows += [z(1024) if loss is None else jnp.concatenate([loss.reshape(1), z(1023)])]
    head = jnp.stack(rows + [z(1024)] * 2)
    return jnp.concatenate([head, p["pool_w"].reshape(-1, 1024)], axis=0)


def _unpack_small(a):
    return dict(
        mix_norm_g=a[0], mlp_norm_g=a[1], pool_scale=a[2, :POOL_WIDTH],
        rel_bias=a[2, POOL_WIDTH:POOL_WIDTH + N_BUCKETS * N_HEADS].reshape(N_BUCKETS, N_HEADS),
        q_norm_g=a[3, :HEAD_DIM], k_norm_g=a[3, LANES:LANES + HEAD_DIM],
        pool_w=a[8:].reshape(len(POOL_WINDOWS), LANES, LANES))


_WEIGHT_ORDER = ("mix_norm_g", "w_in", "pool_w", "pool_scale", "q_norm_g", "k_norm_g", "rel_bias", "w_out",
                 "mlp_norm_g", "w_up", "w_down")
_BIG = ("w_in", "w_out", "w_up", "w_down")


def kernel(x, mix_norm_g, w_in, pool_w, pool_scale, q_norm_g, k_norm_g, rel_bias, w_out, mlp_norm_g, w_up, w_down, loss_target, m_mix_norm_g, m_w_in, m_pool_w, m_pool_scale, m_q_norm_g, m_k_norm_g, m_rel_bias, m_w_out, m_mlp_norm_g, m_w_up, m_w_down, v_mix_norm_g, v_w_in, v_pool_w, v_pool_scale, v_q_norm_g, v_k_norm_g, v_rel_bias, v_w_out, v_mlp_norm_g, v_w_up, v_w_down):
    w = dict(mix_norm_g=mix_norm_g, w_in=w_in, pool_w=pool_w, pool_scale=pool_scale, q_norm_g=q_norm_g,
             k_norm_g=k_norm_g, rel_bias=rel_bias, w_out=w_out, mlp_norm_g=mlp_norm_g, w_up=w_up, w_down=w_down)
    m = dict(mix_norm_g=m_mix_norm_g, w_in=m_w_in, pool_w=m_pool_w, pool_scale=m_pool_scale, q_norm_g=m_q_norm_g,
             k_norm_g=m_k_norm_g, rel_bias=m_rel_bias, w_out=m_w_out, mlp_norm_g=m_mlp_norm_g, w_up=m_w_up, w_down=m_w_down)
    v = dict(mix_norm_g=v_mix_norm_g, w_in=v_w_in, pool_w=v_pool_w, pool_scale=v_pool_scale, q_norm_g=v_q_norm_g,
             k_norm_g=v_k_norm_g, rel_bias=v_rel_bias, w_out=v_w_out, mlp_norm_g=v_mlp_norm_g, w_up=v_w_up, w_down=v_w_down)
    xc, yc, cc = _coords()

    c_idx = jnp.reshape(cc, (1,)).astype(jnp.int32)
    chip_idx = jnp.reshape(2 * xc + yc, (1,)).astype(jnp.int32)
    me = 4 * xc + 2 * yc + cc
    whole = lambda t: t.reshape(t.shape[0], t.shape[1] * t.shape[2], t.shape[3])

    placed = [_halves(p) for p in _place_shards_call([w[n] for n in _BIG], chip_idx, nch=4)]
    win_f, bias = _allgather_call(placed[:1], from_chips=True, name="weights_allgather_in",
                                  meanwhile=_bias_table_work(rel_bias))
    wsend, wrecv, in_flight, started = _gather_start_call(placed[1:], win_f)

    def mlp_weights(after):
        landed = _gather_wait_call(in_flight, wsend, wrecv, after)
        wout_f, wup_f, wdown_f = _allgather_call(landed, from_chips=False, name="weights_pair_forward")
        return whole(wout_f).reshape(-1, wout_f.shape[-1]), whole(wup_f), whole(wdown_f)

    split = []

    def on_mlp_grads(*wire_grads):
        srcs = [_halves(g) for g in wire_grads]
        lands = [lax.empty((N_DEV,) + s.shape[2:], s.dtype) for s in srcs]
        split.extend(_scatter_start_call(srcs, lands, [False] * len(srcs), "grads_scatter_start"))
        return split[4]

    loss_part, dx, big_grads, small_grads = _local_grads(
        x[0], loss_target[0], mix_norm_g + started[0, 0], whole(win_f), pool_w, pool_scale, q_norm_g, k_norm_g, rel_bias,
        mlp_norm_g, mlp_weights, on_mlp_grads, bias)
    g_in, g_out, g_up, g_down = big_grads
    gsend, grecv, srcs_thru, lands_thru, _ = split
    lands_mlp = _scatter_wait_call(srcs_thru, lands_thru, gsend, grecv, g_in[1], [False] * 3, "grads_scatter_wait")

    small_own = _pack_small(small_grads, folded=False, loss=loss_part)
    last_srcs = [_halves(g_in[1]), small_own]
    last_lands = [lax.empty((N_DEV,) + last_srcs[0].shape[2:], WIRE_DTYPE), lax.empty((N_DEV,) + small_own.shape, F32)]
    lsend, lrecv, last_srcs, last_lands, last_started = _scatter_start_call(
        last_srcs, last_lands, [False, True], "grads_scatter_start_last")
    idx = jnp.concatenate([chip_idx, c_idx] + [jnp.reshape(jnp.bitwise_xor(me, r), (1,)) for r in range(1, N_DEV)])
    idx = idx.astype(jnp.int32)
    mlp = _BIG[1:]

    def update(names, own32, lands, tag, dep=None):
        halves = _reduce_call([_halves(g) for g in own32], lands, idx, 4, "grads_reduce_" + tag, dep)
        reduced = _pair_allgather_call(list(halves), "grads_pair_allgather_" + tag)
        return _adamw_call([w[n] for n in names], reduced, [m[n] for n in names], [v[n] for n in names], 8, "adamw_" + tag)

    out_mlp = update(mlp, [g_out[0], g_up[0], g_down[0]], lands_mlp, "mlp", last_started)
    land_in, small_all = _scatter_wait_call(last_srcs, last_lands, lsend, lrecv, out_mlp[3][-1], [False, True],
                                            "grads_scatter_wait_last")
    out_in = update(_BIG[:1], [g_in[0]], [land_in], "in")
    g_pack, d_pack, m_pack, v_pack = _small_call(
        small_all, small_own, jnp.reshape(me, (1,)).astype(jnp.int32), _pack_small(w), _pack_small(m), _pack_small(v))

    grads, deltas, new_m, new_v = (_unpack_small(a) for a in (g_pack, d_pack, m_pack, v_pack))
    for k, res in enumerate((grads, deltas, new_m, new_v)):
        res[_BIG[0]] = out_in[k][0]
        for i, n in enumerate(mlp):
            res[n] = out_mlp[k][i]
    loss = g_pack[LOSS_ROW, 0]
    return (loss, dx[None], *[grads[n] for n in _WEIGHT_ORDER], *[deltas[n] for n in _WEIGHT_ORDER],
            *[new_m[n] for n in _WEIGHT_ORDER], *[new_v[n] for n in _WEIGHT_ORDER])
```

```python
import math

import jax
import jax.numpy as jnp
import numpy as np
from jax import lax
from jax.experimental import pallas as pl
from jax.experimental.pallas import tpu as pltpu

F32 = jnp.float32
MXU_DTYPE = jnp.bfloat16
WIRE_DTYPE = jnp.bfloat16

NORM_EPS = 1e-6
NEG_INF = -1e30
LANES = 128
HEAD_DIM = 64
N_HEADS = 8
POOL_WIDTH = 512
ATTN_WIDTH = 512
POOL_WINDOWS = (2, 4, 8, 16)
POOL_HALO = 16
DILATED_PATTERNS = ((128, 1), (512, 4), (2048, 16))
ATT_BLOCK = 128
ATT_SUPER = ATT_BLOCK * max(dl for _, dl in DILATED_PATTERNS)
ATT_UNITS = ATT_SUPER // ATT_BLOCK
N_BUCKETS = 32
NO_BUCKET = -1
MAX_DISTANCE = 2048
N_CHIPS = 4
N_DEV = 8
ADAM_LR, ADAM_B1, ADAM_B2, ADAM_EPS, ADAM_WD, ADAM_STEP = 0.001, 0.9, 0.999, 1e-08, 0.01, 10
VMEM_LIMIT = 56 * 1024 * 1024
MESH = pl.DeviceIdType.MESH
ANY = pl.BlockSpec(memory_space=pl.ANY)

SMALL_ROWS = 72
LOSS_ROW = 5


def _mm(a, b):
    return jnp.dot(a, b, preferred_element_type=F32)


def _mm_nt(a, b):
    return lax.dot_general(a, b, (((1,), (1,)), ((), ())), preferred_element_type=F32)


def _mm_tn(a, b):
    return lax.dot_general(a, b, (((0,), (0,)), ((), ())), preferred_element_type=F32)


def _params(sem=None, **kw):
    if sem is not None:
        kw["dimension_semantics"] = sem
    return pltpu.CompilerParams(vmem_limit_bytes=VMEM_LIMIT, **kw)


def _low_half():
    return lax.broadcasted_iota(jnp.int32, (1, LANES), 1) < HEAD_DIM


def _head_sum_bcast(y):
    lo = _low_half()
    outs = []
    for j in range(y.shape[1] // LANES):
        c = y[:, j * LANES:(j + 1) * LANES]
        s_lo = jnp.sum(jnp.where(lo, c, 0.0), axis=-1, keepdims=True)
        s_hi = jnp.sum(jnp.where(lo, 0.0, c), axis=-1, keepdims=True)
        outs.append(jnp.where(lo, s_lo, s_hi))
    return jnp.concatenate(outs, axis=-1)


def _rms_bwd(dn, hn, r):
    return r * (dn - hn * jnp.mean(dn * hn, axis=-1, keepdims=True))


def _t5_bucket_np(dist):
    max_exact = N_BUCKETS // 2
    d_f = np.maximum(dist, 1).astype(np.float32)
    ratio = (np.log(d_f / np.float32(max_exact)) / np.float32(math.log(MAX_DISTANCE / max_exact))).astype(np.float32)
    large = max_exact + (ratio * np.float32(N_BUCKETS - max_exact)).astype(np.int32)
    large = np.minimum(large, N_BUCKETS - 1)
    return np.where(dist < max_exact, dist, large).astype(np.int32)


def _window_offsets(dl):
    if dl == 1:
        return _by4_positions(ATT_BLOCK), _by4_positions(2 * ATT_BLOCK)
    return np.arange(ATT_BLOCK), np.arange(2 * ATT_BLOCK)


def _bucket_tables():
    tables = []
    for _, dl in DILATED_PATTERNS:
        qq, kk = _window_offsets(dl)
        dist = qq[:, None] + ATT_BLOCK - kk[None, :]
        bucket = _t5_bucket_np(np.clip(dist, 0, ATT_BLOCK) * dl)
        tables.append(np.where((dist >= 0) & (dist <= ATT_BLOCK), bucket, NO_BUCKET))
    return np.stack(tables).astype(np.int32)


def _previous_block_keys():
    return np.stack([np.broadcast_to(_window_offsets(dl)[1][None, :] < ATT_BLOCK, (ATT_BLOCK, 2 * ATT_BLOCK))
                     for _, dl in DILATED_PATTERNS])


def _f1_call(x, g1, win, poolw, pscale, qg, kg, tm):
    s, d = x.shape
    nblk = s // tm

    def body(x_ref, g1_ref, win_ref, pw_ref, ps_ref, qg_ref, kg_ref,
             a_ref, pooled_ref, ypool_ref, q32_ref, k32_ref, qn_ref, kn_ref, v_ref, ubuf):
        i = pl.program_id(0)
        xv = x_ref[...]
        r = lax.rsqrt(jnp.mean(xv * xv, axis=-1, keepdims=True) + NORM_EPS)
        a = ((xv * r) * g1_ref[...]).astype(MXU_DTYPE)
        a_ref[...] = a
        u = _mm(a, win_ref[0])
        q = _mm(a, win_ref[1])
        k = _mm(a, win_ref[2])
        v_ref[...] = _mm(a, win_ref[3])
        q32_ref[...] = q
        k32_ref[...] = k
        rq = lax.rsqrt(_head_sum_bcast(q * q) * (1.0 / HEAD_DIM) + NORM_EPS)
        qn_ref[...] = ((q * rq) * qg_ref[...]) * (HEAD_DIM ** -0.5)
        rk = lax.rsqrt(_head_sum_bcast(k * k) * (1.0 / HEAD_DIM) + NORM_EPS)
        kn_ref[...] = (k * rk) * kg_ref[...]

        @pl.when(i == 0)
        def _():
            ubuf[0:POOL_HALO, :] = jnp.zeros((POOL_HALO, POOL_WIDTH), F32)

        @pl.when(i > 0)
        def _():
            ubuf[0:POOL_HALO, :] = ubuf[tm:tm + POOL_HALO, :]

        ubuf[POOL_HALO:POOL_HALO + tm, :] = u
        t = i * tm + lax.broadcasted_iota(jnp.int32, (tm, 1), 0)
        for g, w in enumerate(POOL_WINDOWS):
            ls = slice(g * LANES, (g + 1) * LANES)
            ug = u[:, ls]
            acc = ug
            for sh in range(1, w):
                acc = acc + ubuf[POOL_HALO - sh:POOL_HALO - sh + tm, ls]
            cnt = jnp.minimum(t + 1, w).astype(F32)
            pooled = (acc / cnt - ug).astype(MXU_DTYPE)
            pooled_ref[:, ls] = pooled
            ypool_ref[:, ls] = (_mm(pooled, pw_ref[g]) * ps_ref[:, ls]).astype(MXU_DTYPE)

    tok = lambda w: pl.BlockSpec((tm, w), lambda i: (i, 0))
    full = lambda shp: pl.BlockSpec(shp, lambda i: (0,) * len(shp))
    return pl.pallas_call(
        body, name="fwd_inproj",
        grid=(nblk,),
        in_specs=[tok(d), full((1, d)), full(win.shape), full(poolw.shape), full((1, POOL_WIDTH)),
                  full((1, ATTN_WIDTH)), full((1, ATTN_WIDTH))],
        out_specs=[tok(d), tok(POOL_WIDTH), tok(POOL_WIDTH), tok(ATTN_WIDTH), tok(ATTN_WIDTH),
                   tok(ATTN_WIDTH), tok(ATTN_WIDTH), tok(ATTN_WIDTH)],
        out_shape=[jax.ShapeDtypeStruct((s, d), MXU_DTYPE),
                   jax.ShapeDtypeStruct((s, POOL_WIDTH), MXU_DTYPE),
                   jax.ShapeDtypeStruct((s, POOL_WIDTH), MXU_DTYPE),
                   jax.ShapeDtypeStruct((s, ATTN_WIDTH), F32),
                   jax.ShapeDtypeStruct((s, ATTN_WIDTH), F32),
                   jax.ShapeDtypeStruct((s, ATTN_WIDTH), F32),
                   jax.ShapeDtypeStruct((s, ATTN_WIDTH), F32),
                   jax.ShapeDtypeStruct((s, ATTN_WIDTH), F32)],
        scratch_shapes=[pltpu.VMEM((tm + POOL_HALO, POOL_WIDTH), F32)],
        compiler_params=_params(("arbitrary",)),
    )(x, g1, win, poolw, pscale, qg, kg)


DEINT = 4
assert [dl for _, dl in DILATED_PATTERNS] == [1, DEINT, DEINT * DEINT]


def _by4_positions(n):
    pos = np.arange(n)
    return DEINT * (pos % (n // DEINT)) + pos // (n // DEINT)


def _masked_bias(b_ref, p, n):
    return b_ref[p, jnp.minimum(n, 1)].reshape(2 * ATT_BLOCK, 2 * ATT_BLOCK)


def _unit_rows(u, dl):
    sq, sk = ATT_SUPER // DEINT, 2 * ATT_SUPER // DEINT
    if dl == 1:
        n = ATT_BLOCK // DEINT
        return (u, [pl.ds(pl.multiple_of(r * sq + n * u, 8), n) for r in range(DEINT)],
                [pl.ds(pl.multiple_of(r * sk + sk // 2 + n * (u - 1), 8), 2 * n) for r in range(DEINT)])
    if dl == DEINT:
        r, b = u % DEINT, u // DEINT
        return (b, [pl.ds(pl.multiple_of(r * sq + ATT_BLOCK * b, 8), ATT_BLOCK)],
                [pl.ds(pl.multiple_of(r * sk + sk // 2 + ATT_BLOCK * (b - 1), 8), 2 * ATT_BLOCK)])
    r, a = u % DEINT, u // DEINT
    return 0, [pl.ds(r * sq + a, ATT_BLOCK, stride=DEINT)], [pl.ds(r * sk + a, 2 * ATT_BLOCK, stride=DEINT)]


def _take(ref, runs):
    parts = [ref[run, :] for run in runs]
    return parts[0] if len(parts) == 1 else jnp.concatenate(parts, axis=0)


def _put(ref, runs, value, add=False):
    n = value.shape[0] // len(runs)
    for i, run in enumerate(runs):
        part = value[i * n:(i + 1) * n]
        ref[run, :] = ref[run, :] + part if add else part


def _deinterleave(dst, src, n):
    seg = n // DEINT
    for r in range(DEINT):
        dst[r * seg:(r + 1) * seg, :] = src[pl.ds(r, seg, stride=DEINT), :]


def _deinterleave_pair(dst, prev, cur):
    seg = prev.shape[0] // DEINT
    for r in range(DEINT):
        dst[2 * r * seg:(2 * r + 1) * seg, :] = prev[pl.ds(r, seg, stride=DEINT), :]
        dst[(2 * r + 1) * seg:(2 * r + 2) * seg, :] = cur[pl.ds(r, seg, stride=DEINT), :]


def _interleave(dst, src, n, offset=0):
    seg = n // DEINT
    stride = src.shape[0] // DEINT
    for r in range(DEINT):
        dst[pl.ds(r, seg, stride=DEINT), :] = src[r * stride + offset:r * stride + offset + seg, :]


def _attn_fwd_call(qn, kn, v, bias):
    s, w = qn.shape
    nsb = s // ATT_SUPER
    npair = w // LANES

    def body(q_ref, kc_ref, kp_ref, vc_ref, vp_ref, b_ref, o_ref, lse_ref, qf, kf, vf, acc_s, m_s, l_s):
        sb = pl.program_id(1)
        _deinterleave(qf, q_ref, ATT_SUPER)
        _deinterleave_pair(kf, kp_ref, kc_ref)
        _deinterleave_pair(vf, vp_ref, vc_ref)
        lo = _low_half()
        for p, (_, dl) in enumerate(DILATED_PATTERNS):
            def unit(u, carry, p=p, dl=dl):
                b, rows_q, rows_k = _unit_rows(u, dl)
                qp = _take(qf, rows_q).astype(MXU_DTYPE)
                kcat = _take(kf, rows_k).astype(MXU_DTYPE)
                vcat = _take(vf, rows_k).astype(MXU_DTYPE)
                zero = jnp.zeros_like(qp)
                q2 = jnp.concatenate([jnp.where(lo, qp, zero), jnp.where(lo, zero, qp)], axis=0)
                sc = _mm_nt(q2, kcat) + _masked_bias(b_ref, p, sb * (ATT_UNITS // dl) + b)
                m2 = jnp.max(sc, axis=-1, keepdims=True)
                pr = jnp.exp(sc - m2)
                l2 = jnp.sum(pr, axis=-1, keepdims=True)
                acc2 = _mm(pr.astype(MXU_DTYPE), vcat)
                acc = jnp.where(lo, acc2[:ATT_BLOCK], acc2[ATT_BLOCK:])
                m = jnp.where(lo, m2[:ATT_BLOCK], m2[ATT_BLOCK:])
                l = jnp.where(lo, l2[:ATT_BLOCK], l2[ATT_BLOCK:])
                if p == 0:
                    _put(acc_s, rows_q, acc)
                    _put(m_s, rows_q, m)
                    _put(l_s, rows_q, l)
                else:
                    m_old = _take(m_s, rows_q)
                    m_new = jnp.maximum(m_old, m)
                    a_old = jnp.exp(m_old - m_new)
                    a_new = jnp.exp(m - m_new)
                    _put(acc_s, rows_q, a_old * _take(acc_s, rows_q) + a_new * acc)
                    _put(l_s, rows_q, a_old * _take(l_s, rows_q) + a_new * l)
                    _put(m_s, rows_q, m_new)
                return carry

            lax.fori_loop(0, ATT_UNITS, unit, 0, unroll=16)
        l = l_s[...]
        acc_s[...] = acc_s[...] / l
        m_s[...] = m_s[...] + jnp.log(l)
        _interleave(o_ref, acc_s, ATT_SUPER)
        _interleave(lse_ref, m_s, ATT_SUPER)

    cur = pl.BlockSpec((ATT_SUPER, LANES), lambda j, t: (t, j))
    prev = pl.BlockSpec((ATT_SUPER, LANES), lambda j, t: (jnp.maximum(t - 1, 0), j))
    bspec = pl.BlockSpec((len(DILATED_PATTERNS), 2, 2, ATT_BLOCK, 2 * ATT_BLOCK), lambda j, t: (0, 0, j, 0, 0))
    return pl.pallas_call(
        body, name="attn_fwd",
        grid=(npair, nsb),
        in_specs=[cur, cur, prev, cur, prev, bspec],
        out_specs=[cur, cur],
        out_shape=[jax.ShapeDtypeStruct((s, w), F32), jax.ShapeDtypeStruct((s, w), F32)],
        scratch_shapes=[pltpu.VMEM((ATT_SUPER, LANES), F32), pltpu.VMEM((2 * ATT_SUPER, LANES), F32),
                        pltpu.VMEM((2 * ATT_SUPER, LANES), F32), pltpu.VMEM((ATT_SUPER, LANES), F32),
                        pltpu.VMEM((ATT_SUPER, LANES), F32), pltpu.VMEM((ATT_SUPER, LANES), F32)],
        compiler_params=_params(("arbitrary", "arbitrary")),
    )(qn, kn, kn, v, v, bias)


def _attn_bwd_call(qn, kn, v, do, lse, delta, bias, dep=None):
    s, w = qn.shape
    nsb = s // ATT_SUPER
    npair = w // LANES
    deps = [] if dep is None else [dep]

    def body(q_ref, kc_ref, kp_ref, vc_ref, vp_ref, do_ref, lse_ref, dlt_ref, b_ref, *rest):
        dq_ref, dk_ref, dv_ref, db_ref, qf, kf, vf, dof, lsef, dltf, dqf, dkf, dvf = rest[len(deps):]
        step = pl.program_id(1)
        sb = nsb - 1 - step
        seg = ATT_SUPER // DEINT
        _deinterleave(qf, q_ref, ATT_SUPER)
        _deinterleave(dof, do_ref, ATT_SUPER)
        _deinterleave_pair(kf, kp_ref, kc_ref)
        _deinterleave_pair(vf, vp_ref, vc_ref)
        _deinterleave(lsef, lse_ref, ATT_SUPER)
        _deinterleave(dltf, dlt_ref, ATT_SUPER)

        @pl.when(step == 0)
        def _():
            db_ref[...] = jnp.zeros(db_ref.shape, F32)

        for acc in (dkf, dvf):
            for r in range(DEINT):
                this, before = pl.ds((2 * r + 1) * seg, seg), pl.ds(2 * r * seg, seg)

                @pl.when(step == 0)
                def _(acc=acc, this=this):
                    acc[this, :] = jnp.zeros((seg, LANES), F32)

                @pl.when(step > 0)
                def _(acc=acc, this=this, before=before):
                    acc[this, :] = acc[before, :]

                acc[before, :] = jnp.zeros((seg, LANES), F32)
        lo = _low_half()
        for p, (_, dl) in enumerate(DILATED_PATTERNS):
            def unit(u, carry, p=p, dl=dl):
                b, rows_q, rows_k = _unit_rows(u, dl)
                qp = _take(qf, rows_q).astype(MXU_DTYPE)
                dop = _take(dof, rows_q).astype(MXU_DTYPE)
                kcat = _take(kf, rows_k).astype(MXU_DTYPE)
                vcat = _take(vf, rows_k).astype(MXU_DTYPE)
                lse2 = _take(lsef, rows_q)
                dlt2 = _take(dltf, rows_q)
                zero = jnp.zeros_like(qp)
                q2 = jnp.concatenate([jnp.where(lo, qp, zero), jnp.where(lo, zero, qp)], axis=0)
                do2 = jnp.concatenate([jnp.where(lo, dop, zero), jnp.where(lo, zero, dop)], axis=0)
                lse_c = jnp.concatenate([lse2[:, 0:1], lse2[:, HEAD_DIM:HEAD_DIM + 1]], axis=0)
                dlt_c = jnp.concatenate([dlt2[:, 0:1], dlt2[:, HEAD_DIM:HEAD_DIM + 1]], axis=0)
                sc = _mm_nt(q2, kcat) + _masked_bias(b_ref, p, sb * (ATT_UNITS // dl) + b)
                pr = jnp.exp(sc - lse_c)
                ds = pr * (_mm_nt(do2, vcat) - dlt_c)
                db_ref[p] += ds.reshape(2, ATT_BLOCK, 2 * ATT_BLOCK)
                ds_c = ds.astype(MXU_DTYPE)
                dq2 = _mm(ds_c, kcat)
                dk = _mm_tn(ds_c, q2)
                dv = _mm_tn(pr.astype(MXU_DTYPE), do2)
                dq = jnp.where(lo, dq2[:ATT_BLOCK], dq2[ATT_BLOCK:])
                _put(dqf, rows_q, dq, add=p > 0)
                _put(dkf, rows_k, dk, add=True)
                _put(dvf, rows_k, dv, add=True)
                return carry

            lax.fori_loop(0, ATT_UNITS, unit, 0, unroll=16)
        _interleave(dq_ref, dqf, ATT_SUPER)
        _interleave(dk_ref, dkf, ATT_SUPER, offset=seg)
        _interleave(dv_ref, dvf, ATT_SUPER, offset=seg)

    cur = pl.BlockSpec((ATT_SUPER, LANES), lambda j, t: (nsb - 1 - t, j))
    prev = pl.BlockSpec((ATT_SUPER, LANES), lambda j, t: (jnp.maximum(nsb - 2 - t, 0), j))
    npat = len(DILATED_PATTERNS)
    bspec = pl.BlockSpec((npat, 2, 2, ATT_BLOCK, 2 * ATT_BLOCK), lambda j, t: (0, 0, j, 0, 0))
    dbspec = pl.BlockSpec((npat, 2, ATT_BLOCK, 2 * ATT_BLOCK), lambda j, t: (0, j, 0, 0))
    sup = lambda: pltpu.VMEM((ATT_SUPER, LANES), F32)
    sup2 = lambda: pltpu.VMEM((2 * ATT_SUPER, LANES), F32)
    return pl.pallas_call(
        body, name="attn_bwd",
        grid=(npair, nsb),
        in_specs=[cur, cur, prev, cur, prev, cur, cur, cur, bspec] + [ANY] * len(deps),
        out_specs=[cur, cur, cur, dbspec],
        out_shape=[jax.ShapeDtypeStruct((s, w), F32)] * 3
        + [jax.ShapeDtypeStruct((npat, N_HEADS, ATT_BLOCK, 2 * ATT_BLOCK), F32)],
        scratch_shapes=[sup(), sup2(), sup2(), sup(), sup(), sup(), sup(), sup2(), sup2()],
        compiler_params=_params(("arbitrary", "arbitrary")),
    )(qn, kn, kn, v, v, do, lse, delta, bias, *deps)


def _bias_table_work(rel_bias):
    buckets = jnp.asarray(_bucket_tables())
    prev_keys = jnp.asarray(_previous_block_keys().astype(np.int32))
    npat = buckets.shape[0]

    def body(rb_ref, bk_ref, pk_ref, out_ref):
        for p in range(npat):
            for half in range(2):
                ks = slice(half * ATT_BLOCK, (half + 1) * ATT_BLOCK)
                bk = bk_ref[p, :, ks]
                absent = pk_ref[p, :, ks] != 0
                for h in range(N_HEADS):
                    def pick(b, acc, h=h, bk=bk):
                        return jnp.where(bk == b, rb_ref[b, h], acc)

                    tab = lax.fori_loop(0, N_BUCKETS, pick, jnp.full((ATT_BLOCK, ATT_BLOCK), NEG_INF, F32))
                    out_ref[p, 1, h, :, ks] = tab
                    out_ref[p, 0, h, :, ks] = jnp.where(absent, NEG_INF, tab)

    vmem = pl.BlockSpec(memory_space=pltpu.VMEM)
    return ([rel_bias, buckets, prev_keys], [pl.BlockSpec(memory_space=pltpu.SMEM), vmem, vmem],
            jax.ShapeDtypeStruct((npat, 2, N_HEADS, ATT_BLOCK, 2 * ATT_BLOCK), F32), body)


def _bias_table_call(rel_bias):
    operands, specs, shape, body = _bias_table_work(rel_bias)
    return pl.pallas_call(body, name="bias_table", in_specs=specs, out_shape=shape, compiler_params=_params())(*operands)


def _rel_bias_grad_call(dbias, buckets):
    npat, nh = dbias.shape[0], dbias.shape[1]

    def body(db_ref, bk_ref, out_ref):
        lane = lax.broadcasted_iota(jnp.int32, (nh, LANES), 1)
        out = jnp.zeros((nh, LANES), F32)
        for b in range(N_BUCKETS):
            tot = jnp.zeros((nh, 1), F32)
            for p in range(npat):
                hit = jnp.where(bk_ref[p][None] == b, db_ref[p], 0.0)
                tot = tot + jnp.sum(jnp.sum(hit, axis=1), axis=-1, keepdims=True)
            out = jnp.where(lane == b, tot, out)
        out_ref[...] = out

    return pl.pallas_call(
        body, name="rel_bias_grad",
        out_shape=jax.ShapeDtypeStruct((nh, LANES), F32),
        compiler_params=_params(),
    )(dbias, buckets)


def _f2_call(x, tgt, ypool, o, wout, wup, wdown, g2, tm):
    s, d = x.shape
    nblk = s // tm
    nch, _, fch = wup.shape
    dff = nch * fch
    mixw = POOL_WIDTH + ATTN_WIDTH

    def body(x_ref, t_ref, yp_ref, o_ref, g2_ref, wout_hbm, wup_hbm, wdown_hbm,
             mixed_ref, c_ref, ff_ref, dz_ref, dy_ref, dh1_ref, dyp_ref, do_ref, dlt_ref, dg2_ref, loss_ref,
             wout_v, wup_v, wdown_v, rz):
        i = pl.program_id(0)

        @pl.when(i == 0)
        def _():
            pltpu.sync_copy(wout_hbm, wout_v)
            pltpu.sync_copy(wup_hbm, wup_v)
            pltpu.sync_copy(wdown_hbm, wdown_v)
            dg2_ref[...] = jnp.zeros(dg2_ref.shape, F32)
            loss_ref[...] = jnp.zeros(loss_ref.shape, F32)

        o = o_ref[...]
        mixed = jnp.concatenate([yp_ref[...], o.astype(MXU_DTYPE)], axis=-1)
        mixed_ref[...] = mixed
        h1 = x_ref[...] + _mm(mixed, wout_v[...])
        r2 = lax.rsqrt(jnp.mean(h1 * h1, axis=-1, keepdims=True) + NORM_EPS)
        hn = h1 * r2
        c = (hn * g2_ref[...]).astype(MXU_DTYPE)
        c_ref[...] = c
        y = h1
        for j in range(nch):
            cs = slice(j * fch, (j + 1) * fch)
            z = jnp.maximum(_mm(c, wup_v[j]), 0.0)
            rz[:, cs] = z
            ff = (z * z).astype(MXU_DTYPE)
            ff_ref[:, cs] = ff
            y = y + _mm(ff, wdown_v[j])
        err = y - t_ref[...]
        loss_ref[...] += jnp.sum(err * err) * (0.5 / d)
        dy = err * (1.0 / d)
        dy_c = dy.astype(MXU_DTYPE)
        dy_ref[...] = dy_c
        dc = jnp.zeros((tm, d), F32)
        for j in range(nch):
            cs = slice(j * fch, (j + 1) * fch)
            dz = (_mm_nt(dy_c, wdown_v[j]) * (2.0 * rz[:, cs])).astype(MXU_DTYPE)
            dz_ref[:, cs] = dz
            dc = dc + _mm_nt(dz, wup_v[j])
        dg2_ref[...] += jnp.sum(dc * hn, axis=0, keepdims=True)
        dh1 = dy + _rms_bwd(dc * g2_ref[...], hn, r2)
        dh1_ref[...] = dh1
        dmix = _mm_nt(dh1.astype(MXU_DTYPE), wout_v[...])
        dyp_ref[...] = dmix[:, :POOL_WIDTH]
        do = dmix[:, POOL_WIDTH:]
        do_ref[...] = do
        dlt_ref[...] = _head_sum_bcast(do * o)

    tok = lambda w: pl.BlockSpec((tm, w), lambda i: (i, 0))
    const = lambda shp: pl.BlockSpec(shp, lambda i: (0,) * len(shp))
    return pl.pallas_call(
        body, name="fwd_mlp_bwd_mlp",
        grid=(nblk,),
        in_specs=[tok(d), tok(d), tok(POOL_WIDTH), tok(ATTN_WIDTH), const((1, d)), ANY, ANY, ANY],
        out_specs=[tok(mixw), tok(d), tok(dff), tok(dff), tok(d), tok(d), tok(POOL_WIDTH), tok(ATTN_WIDTH),
                   tok(ATTN_WIDTH), const((1, d)), const((1, LANES))],
        out_shape=[jax.ShapeDtypeStruct((s, mixw), MXU_DTYPE),
                   jax.ShapeDtypeStruct((s, d), MXU_DTYPE),
                   jax.ShapeDtypeStruct((s, dff), MXU_DTYPE),
                   jax.ShapeDtypeStruct((s, dff), MXU_DTYPE),
                   jax.ShapeDtypeStruct((s, d), MXU_DTYPE),
                   jax.ShapeDtypeStruct((s, d), F32),
                   jax.ShapeDtypeStruct((s, POOL_WIDTH), F32),
                   jax.ShapeDtypeStruct((s, ATTN_WIDTH), F32),
                   jax.ShapeDtypeStruct((s, ATTN_WIDTH), F32),
                   jax.ShapeDtypeStruct((1, d), F32),
                   jax.ShapeDtypeStruct((1, LANES), F32)],
        scratch_shapes=[pltpu.VMEM(wout.shape, MXU_DTYPE), pltpu.VMEM(wup.shape, MXU_DTYPE),
                        pltpu.VMEM(wdown.shape, MXU_DTYPE), pltpu.VMEM((tm, dff), F32)],
        compiler_params=_params(("arbitrary",)),
    )(x, tgt, ypool, o, g2, wout, wup, wdown)


def _bproj_call(dqn, dkn, dv, q32, k32, dypool, pooled, x, dh1, win, poolw, pscale, qg, kg, g1, tm):
    s, d = x.shape
    nblk = s // tm
    ngrp = len(POOL_WINDOWS)

    def body(dqn_ref, dkn_ref, dv_ref, q_ref, k_ref, dyp_ref, pooled_ref, x_ref, dh1_ref,
             win_hbm, pw_ref, ps_ref, qg_ref, kg_ref, g1_ref,
             dx_ref, dproj_ref, dg1_ref, dqg_ref, dkg_ref, dpw_ref, dps_ref, win_v, ebuf):
        step = pl.program_id(0)
        i = nblk - 1 - step

        @pl.when(step == 0)
        def _():
            pltpu.sync_copy(win_hbm, win_v)
            dg1_ref[...] = jnp.zeros(dg1_ref.shape, F32)
            dqg_ref[...] = jnp.zeros(dqg_ref.shape, F32)
            dkg_ref[...] = jnp.zeros(dkg_ref.shape, F32)
            dpw_ref[...] = jnp.zeros(dpw_ref.shape, F32)
            dps_ref[...] = jnp.zeros(dps_ref.shape, F32)
            ebuf[tm:tm + POOL_HALO, :] = jnp.zeros((POOL_HALO, POOL_WIDTH), F32)

        @pl.when(step > 0)
        def _():
            ebuf[tm:tm + POOL_HALO, :] = ebuf[0:POOL_HALO, :]

        def qk_bwd(dn_sum, raw, gain, scale, dgain_ref):
            rr = lax.rsqrt(_head_sum_bcast(raw * raw) * (1.0 / HEAD_DIM) + NORM_EPS)
            hn = raw * rr
            dgain_ref[...] += jnp.sum(dn_sum * hn, axis=0, keepdims=True) * scale
            dn = dn_sum * (gain * scale)
            return rr * (dn - hn * (_head_sum_bcast(dn * hn) * (1.0 / HEAD_DIM)))

        dq = qk_bwd(dqn_ref[...], q_ref[...], qg_ref[...], HEAD_DIM ** -0.5, dqg_ref)
        dk = qk_bwd(dkn_ref[...], k_ref[...], kg_ref[...], 1.0, dkg_ref)

        t = i * tm + lax.broadcasted_iota(jnp.int32, (tm, 1), 0)
        dpooled = []
        for g, w in enumerate(POOL_WINDOWS):
            ls = slice(g * LANES, (g + 1) * LANES)
            dm = dyp_ref[:, ls]
            pg = pooled_ref[:, ls]
            dps_ref[:, ls] += jnp.sum(dm * _mm(pg, pw_ref[g]), axis=0, keepdims=True)
            dms = (dm * ps_ref[:, ls]).astype(MXU_DTYPE)
            dpw_ref[g] += _mm_tn(pg, dms)
            dpg = _mm_nt(dms, pw_ref[g])
            dpooled.append(dpg)
            ebuf[0:tm, ls] = dpg / jnp.minimum(t + 1, w).astype(F32)
        du = []
        for g, w in enumerate(POOL_WINDOWS):
            ls = slice(g * LANES, (g + 1) * LANES)
            acc = ebuf[0:tm, ls]
            for sh in range(1, w):
                acc = acc + ebuf[sh:sh + tm, ls]
            du.append(acc - dpooled[g])
        parts = [jnp.concatenate(du, axis=-1), dq, dk, dv_ref[...]]
        da = jnp.zeros((tm, d), F32)
        for p, part in enumerate(parts):
            pc = part.astype(MXU_DTYPE)
            dproj_ref[:, p * POOL_WIDTH:(p + 1) * POOL_WIDTH] = pc
            da = da + _mm_nt(pc, win_v[p])
        xv = x_ref[...]
        r = lax.rsqrt(jnp.mean(xv * xv, axis=-1, keepdims=True) + NORM_EPS)
        xn = xv * r
        dg1_ref[...] += jnp.sum(da * xn, axis=0, keepdims=True)
        dx_ref[...] = dh1_ref[...] + _rms_bwd(da * g1_ref[...], xn, r)

    tok = lambda w: pl.BlockSpec((tm, w), lambda t: (nblk - 1 - t, 0))
    const = lambda shp: pl.BlockSpec(shp, lambda t: (0,) * len(shp))
    return pl.pallas_call(
        body, name="bwd_inproj",
        grid=(nblk,),
        in_specs=[tok(ATTN_WIDTH)] * 5 + [tok(POOL_WIDTH), tok(POOL_WIDTH), tok(d), tok(d),
                                          ANY, const(poolw.shape), const((1, POOL_WIDTH)), const((1, ATTN_WIDTH)),
                                          const((1, ATTN_WIDTH)), const((1, d))],
        out_specs=[tok(d), tok(4 * POOL_WIDTH), const((1, d)), const((1, ATTN_WIDTH)), const((1, ATTN_WIDTH)),
                   const((ngrp, LANES, LANES)), const((1, POOL_WIDTH))],
        out_shape=[jax.ShapeDtypeStruct((s, d), F32),
                   jax.ShapeDtypeStruct((s, 4 * POOL_WIDTH), MXU_DTYPE),
                   jax.ShapeDtypeStruct((1, d), F32),
                   jax.ShapeDtypeStruct((1, ATTN_WIDTH), F32),
                   jax.ShapeDtypeStruct((1, ATTN_WIDTH), F32),
                   jax.ShapeDtypeStruct((ngrp, LANES, LANES), F32),
                   jax.ShapeDtypeStruct((1, POOL_WIDTH), F32)],
        scratch_shapes=[pltpu.VMEM(win.shape, MXU_DTYPE), pltpu.VMEM((tm + POOL_HALO, POOL_WIDTH), F32)],
        compiler_params=_params(("arbitrary",)),
    )(dqn, dkn, dv, q32, k32, dypool, pooled, x, dh1, win, poolw, pscale, qg, kg, g1)


def _wgrad_call(a, b, bm, bn, bk, out_shape, out_block, out_index, name):
    s, m = a.shape
    _, n = b.shape
    nk = s // bk

    def body(a_ref, b_ref, o_ref, wire_ref):
        k = pl.program_id(2)

        @pl.when(k == 0)
        def _():
            o_ref[...] = jnp.zeros(o_ref.shape, F32)

        o_ref[...] += _mm_tn(a_ref[...].astype(MXU_DTYPE), b_ref[...].astype(MXU_DTYPE))

        @pl.when(k == nk - 1)
        def _():
            wire_ref[...] = o_ref[...].astype(WIRE_DTYPE)

    return pl.pallas_call(
        body, name=name,
        grid=(m // bm, n // bn, nk),
        in_specs=[pl.BlockSpec((bk, bm), lambda i, j, k: (k, i)), pl.BlockSpec((bk, bn), lambda i, j, k: (k, j))],
        out_specs=[pl.BlockSpec(out_block, out_index)] * 2,
        out_shape=[jax.ShapeDtypeStruct(out_shape, F32), jax.ShapeDtypeStruct(out_shape, WIRE_DTYPE)],
        compiler_params=_params(("arbitrary", "arbitrary", "arbitrary")),
    )(a, b)


def _local_grads(x, tgt, g1, win, poolw, pscale, qg, kg, rel_bias, g2, mlp_weights, on_mlp_grads=None, bias=None):
    s, d = x.shape
    g1r, g2r = g1.reshape(1, d), g2.reshape(1, d)
    psr = pscale.reshape(1, POOL_WIDTH)
    qgr = jnp.tile(qg, N_HEADS).reshape(1, ATTN_WIDTH)
    kgr = jnp.tile(kg, N_HEADS).reshape(1, ATTN_WIDTH)
    pw_c = poolw.astype(MXU_DTYPE)
    buckets = jnp.asarray(_bucket_tables())
    bias = _bias_table_call(rel_bias) if bias is None else bias
    bk = min(s, 4096)

    a, pooled, ypool, q32, k32, qn, kn, v = _f1_call(x, g1r, win, pw_c, psr, qgr, kgr, tm=512)
    o, lse = _attn_fwd_call(qn, kn, v, bias)
    wout, wup, wdown = mlp_weights(o)
    mixed, c, ff, dz, dy, dh1, dypool, do, delta, dg2, loss = _f2_call(x, tgt, ypool, o, wout, wup, wdown, g2r, tm=256)
    dff = ff.shape[1]
    g_out = [g.reshape(N_CHIPS, d // N_CHIPS, d)
             for g in _wgrad_call(mixed, dh1, d, d, bk // 4, (d, d), (d, d), lambda i, j, k: (0, 0), "wgrad_out")]
    g_up = _wgrad_call(c, dz, d, dff // N_CHIPS, bk, (N_CHIPS, d, dff // N_CHIPS), (None, d, dff // N_CHIPS),
                       lambda i, j, k: (j, 0, 0), "wgrad_up")
    g_down = _wgrad_call(ff, dy, dff // N_CHIPS, d, bk, (N_CHIPS, dff // N_CHIPS, d), (None, dff // N_CHIPS, d),
                         lambda i, j, k: (i, 0, 0), "wgrad_down")
    dep = None if on_mlp_grads is None else on_mlp_grads(g_out[1], g_up[1], g_down[1])
    dqn, dkn, dv, dbias = _attn_bwd_call(qn, kn, v, do, lse, delta, bias, dep)
    dx, dproj, dg1, dqg, dkg, dpw, dps = _bproj_call(
        dqn, dkn, dv, q32, k32, dypool, pooled, x, dh1, win, pw_c, psr, qgr, kgr, g1r, tm=256)
    nin = dproj.shape[1] // N_CHIPS
    g_in = _wgrad_call(a, dproj, d, nin, bk, (N_CHIPS, d, nin), (None, d, nin), lambda i, j, k: (j, 0, 0), "wgrad_in")
    drb = _rel_bias_grad_call(dbias, buckets)
    small = dict(
        mix_norm_g=dg1.reshape(d), mlp_norm_g=dg2.reshape(d), pool_scale=dps.reshape(POOL_WIDTH),
        q_norm_g=dqg.reshape(ATTN_WIDTH), k_norm_g=dkg.reshape(ATTN_WIDTH),
        rel_bias=drb[:, :N_BUCKETS].T, pool_w=dpw)
    return loss[0, 0], dx, (g_in, g_out, g_up, g_down), small


def _coords():
    return lax.axis_index("x"), lax.axis_index("y"), lax.axis_index("c")


def _other_chips(x, y):
    return [(1 - x, y), (x, 1 - y), (1 - x, 1 - y)]


def _remote(src, dst, send_sem, recv_sem, dev):
    return pltpu.make_async_remote_copy(src_ref=src, dst_ref=dst, send_sem=send_sem, recv_sem=recv_sem,
                                        device_id=dev, device_id_type=MESH)


def _halves(a):
    return a.reshape(a.shape[:-2] + (2, a.shape[-2] // 2, a.shape[-1]))


def _place_shards_call(shards, chip_idx, nch):
    nw = len(shards)

    def body(chip_ref, *refs):
        for w in range(nw):
            refs[nw + w][...] = refs[w][...].astype(WIRE_DTYPE)

    in_specs = [pl.BlockSpec((s.shape[0] // nch, s.shape[1]), lambda i, chip_ref: (i, 0)) for s in shards]
    out_specs = [pl.BlockSpec((None, s.shape[0] // nch, s.shape[1]), lambda i, chip_ref: (chip_ref[0], i, 0))
                 for s in shards]
    return pl.pallas_call(
        body, name="weights_place",
        grid_spec=pltpu.PrefetchScalarGridSpec(num_scalar_prefetch=1, grid=(nch,),
                                               in_specs=in_specs, out_specs=out_specs),
        out_shape=[jax.ShapeDtypeStruct((N_CHIPS,) + s.shape, WIRE_DTYPE) for s in shards],
        compiler_params=_params(("arbitrary",)),
    )(chip_idx, *shards)


def _allgather_call(placed, from_chips, name, meanwhile=None):
    nw = len(placed)
    ncp = 3 * nw
    extra, extra_specs, extra_shape, extra_body = meanwhile if meanwhile else ([], [], None, None)
    ne = len(extra)

    def body(*refs):
        outs = refs[nw + ne:2 * nw + ne]
        send1, recv1, send2, recv2 = refs[-4:]
        x, y, c = _coords()
        chip = 2 * x + y
        others = _other_chips(x, y)
        first, passed = [], []
        if from_chips:
            for w in range(nw):
                for k, (ox, oy) in enumerate(others):
                    mine = outs[w].at[chip, c]
                    cp = _remote(mine, mine, send1.at[3 * w + k], recv1.at[3 * w + k], (ox, oy, c))
                    cp.start()
                    first.append(cp)
        if meanwhile:
            extra_body(*refs[nw:nw + ne], refs[2 * nw + ne])
        for w in range(nw):
            for k, (ox, oy) in enumerate(others):
                piece = outs[w].at[2 * ox + oy, c]
                if from_chips:
                    _remote(piece, piece, send1.at[3 * w + k], recv1.at[3 * w + k], (ox, oy, c)).wait_recv()
                cp = _remote(piece, piece, send2.at[3 * w + k], recv2.at[3 * w + k], (x, y, 1 - c))
                cp.start()
                passed.append(cp)
        for w in range(nw):
            for k, (ox, oy) in enumerate(others):
                piece = outs[w].at[2 * ox + oy, 1 - c]
                _remote(piece, piece, send2.at[3 * w + k], recv2.at[3 * w + k], (x, y, 1 - c)).wait_recv()
        for cp in first + passed:
            cp.wait_send()

    return pl.pallas_call(
        body, name=name,
        in_specs=[ANY] * nw + list(extra_specs),
        out_specs=[ANY] * nw + ([pl.BlockSpec(memory_space=pltpu.VMEM)] if meanwhile else []),
        out_shape=[jax.ShapeDtypeStruct(s.shape, s.dtype) for s in placed] + ([extra_shape] if meanwhile else []),
        input_output_aliases={w: w for w in range(nw)},
        scratch_shapes=[pltpu.SemaphoreType.DMA((ncp,))] * 4,
        compiler_params=_params(),
    )(*placed, *extra)


HBM_SPEC = pl.BlockSpec(memory_space=pltpu.HBM)
SEM_SPEC = pl.BlockSpec(memory_space=pltpu.SEMAPHORE)
SPLIT_EFFECT = pltpu.SideEffectType.DATAFLOW_SIDE_EFFECTING


def _in_hbm(a):
    return pltpu.with_memory_space_constraint(a, pltpu.HBM)


def _gather_copies(bufs, send, recv):
    x, y, c = _coords()
    chip = 2 * x + y
    cps = []
    for w, buf in enumerate(bufs):
        for k, (ox, oy) in enumerate(_other_chips(x, y)):
            mine, theirs = buf.at[chip, c], buf.at[2 * ox + oy, c]
            sems = (send.at[3 * w + k], recv.at[3 * w + k], (ox, oy, c))
            cps.append((_remote(mine, mine, *sems), _remote(theirs, theirs, *sems)))
    return cps


def _gather_start_call(bufs, after):
    nw = len(bufs)

    def body(*refs):
        ins, send, recv, token = refs[:nw], refs[nw + 1], refs[nw + 2], refs[2 * nw + 3]
        for out, _ in _gather_copies(ins, send, recv):
            out.start()
        token[...] = jnp.zeros(token.shape, F32)

    res = pl.pallas_call(
        body, name="weights_gather_start",
        in_specs=[HBM_SPEC] * nw + [ANY],
        out_specs=[SEM_SPEC, SEM_SPEC] + [HBM_SPEC] * nw + [pl.BlockSpec(memory_space=pltpu.VMEM)],
        out_shape=[pltpu.SemaphoreType.DMA((3 * nw,)), pltpu.SemaphoreType.DMA((3 * nw,))]
        + [pltpu.HBM(b.shape, b.dtype) for b in bufs] + [jax.ShapeDtypeStruct((8, LANES), F32)],
        input_output_aliases={w: 2 + w for w in range(nw)},
        compiler_params=pltpu.CompilerParams(has_side_effects=SPLIT_EFFECT),
    )(*[_in_hbm(b) for b in bufs], after)
    return res[0], res[1], list(res[2:2 + nw]), res[2 + nw]


def _gather_wait_call(bufs, send, recv, after):
    nw = len(bufs)

    def body(*refs):
        ins, send, recv = refs[:nw], refs[nw], refs[nw + 1]
        for out, back in _gather_copies(ins, send, recv):
            out.wait_send()
            back.wait_recv()

    return pl.pallas_call(
        body, name="weights_gather_wait",
        in_specs=[HBM_SPEC] * nw + [SEM_SPEC, SEM_SPEC, ANY],
        out_specs=[HBM_SPEC] * nw,
        out_shape=[pltpu.HBM(b.shape, b.dtype) for b in bufs],
        input_output_aliases={w: w for w in range(nw)},
        compiler_params=pltpu.CompilerParams(has_side_effects=SPLIT_EFFECT),
    )(*bufs, send, recv, after)


def _scatter_copies(srcs, lands, send, recv, wholes):
    x, y, c = _coords()
    me = 4 * x + 2 * y + c
    cps = []
    for w, (src, land) in enumerate(zip(srcs, lands)):
        for r in range(1, N_DEV):
            px, py, pc = ((1 - x) if r & 4 else x, (1 - y) if r & 2 else y, (1 - c) if r & 1 else c)
            sems = (send.at[(N_DEV - 1) * w + r - 1], recv.at[(N_DEV - 1) * w + r - 1], (px, py, pc))
            piece = src if wholes[w] else src.at[2 * px + py, pc]
            cps.append((_remote(piece, land.at[me], *sems), _remote(piece, land.at[4 * px + 2 * py + pc], *sems)))
    return cps


def _scatter_start_call(srcs, lands, wholes, name):
    nw = len(srcs)
    ncp = (N_DEV - 1) * nw

    def body(*refs):
        ins, lnd, send, recv, token = refs[:nw], refs[nw:2 * nw], refs[2 * nw], refs[2 * nw + 1], refs[4 * nw + 2]
        for out, _ in _scatter_copies(ins, lnd, send, recv, wholes):
            out.start()
        token[...] = jnp.zeros(token.shape, F32)

    res = pl.pallas_call(
        body, name=name,
        in_specs=[HBM_SPEC] * (2 * nw),
        out_specs=[SEM_SPEC, SEM_SPEC] + [HBM_SPEC] * (2 * nw) + [pl.BlockSpec(memory_space=pltpu.VMEM)],
        out_shape=[pltpu.SemaphoreType.DMA((ncp,)), pltpu.SemaphoreType.DMA((ncp,))]
        + [pltpu.HBM(b.shape, b.dtype) for b in list(srcs) + list(lands)] + [jax.ShapeDtypeStruct((8, LANES), F32)],
        input_output_aliases={i: 2 + i for i in range(2 * nw)},
        compiler_params=pltpu.CompilerParams(has_side_effects=SPLIT_EFFECT),
    )(*[_in_hbm(b) for b in list(srcs) + list(lands)])
    return res[0], res[1], list(res[2:2 + nw]), list(res[2 + nw:2 + 2 * nw]), res[2 + 2 * nw]


def _scatter_wait_call(srcs, lands, send, recv, after, wholes, name):
    nw = len(srcs)

    def body(*refs):
        ins, lnd, send, recv = refs[:nw], refs[nw:2 * nw], refs[2 * nw], refs[2 * nw + 1]
        for out, back in _scatter_copies(ins, lnd, send, recv, wholes):
            out.wait_send()
            back.wait_recv()

    res = pl.pallas_call(
        body, name=name,
        in_specs=[HBM_SPEC] * (2 * nw) + [SEM_SPEC, SEM_SPEC, ANY],
        out_specs=[HBM_SPEC] * (2 * nw),
        out_shape=[pltpu.HBM(b.shape, b.dtype) for b in list(srcs) + list(lands)],
        input_output_aliases={i: i for i in range(2 * nw)},
        compiler_params=pltpu.CompilerParams(has_side_effects=SPLIT_EFFECT),
    )(*srcs, *lands, send, recv, after)
    return list(res[nw:])


def _reduce_call(own, lands, idx, nch, name, dep=None):
    nw = len(own)
    deps = [] if dep is None else [dep]

    def body(idx_ref, *refs):
        refs = refs[:2 * nw] + refs[2 * nw + len(deps):]
        for w in range(nw):
            tot = refs[w][...]
            for r in range(1, N_DEV):
                tot = tot + refs[nw + w][idx_ref[1 + r]].astype(F32)
            refs[2 * nw + w][...] = tot

    in_specs, out_specs, out_shape = [], [], []
    for s in own:
        in_specs.append(pl.BlockSpec((None, None, s.shape[2] // nch, s.shape[3]),
                                     lambda i, idx_ref: (idx_ref[0], idx_ref[1], i, 0)))
    for s in own:
        in_specs.append(pl.BlockSpec((N_DEV, s.shape[2] // nch, s.shape[3]), lambda i, idx_ref: (0, i, 0)))
    for s in own:
        out_specs.append(pl.BlockSpec((None, s.shape[2] // nch, s.shape[3]), lambda i, idx_ref: (idx_ref[1], i, 0)))
        out_shape.append(jax.ShapeDtypeStruct((2,) + s.shape[2:], F32))
    return pl.pallas_call(
        body, name=name,
        grid_spec=pltpu.PrefetchScalarGridSpec(num_scalar_prefetch=1, grid=(nch,),
                                               in_specs=in_specs + [ANY] * len(deps), out_specs=out_specs),
        out_shape=out_shape,
        compiler_params=_params(("arbitrary",)),
    )(idx, *own, *lands, *deps)


def _pair_allgather_call(halves, name):
    nw = len(halves)

    def body(*refs):
        outs = refs[nw:2 * nw]
        send, recv = refs[2 * nw:]
        x, y, c = _coords()
        cps = []
        for w in range(nw):
            cp = _remote(outs[w].at[c], outs[w].at[c], send.at[w], recv.at[w], (x, y, 1 - c))
            cp.start()
            cps.append(cp)
        for w in range(nw):
            theirs = outs[w].at[1 - c]
            _remote(theirs, theirs, send.at[w], recv.at[w], (x, y, 1 - c)).wait_recv()
        for cp in cps:
            cp.wait_send()

    outs = pl.pallas_call(
        body, name=name,
        in_specs=[ANY] * nw, out_specs=[ANY] * nw,
        out_shape=[jax.ShapeDtypeStruct(h.shape, h.dtype) for h in halves],
        input_output_aliases={w: w for w in range(nw)},
        scratch_shapes=[pltpu.SemaphoreType.DMA((nw,))] * 2,
    )(*halves)
    return [o.reshape(2 * h.shape[1], h.shape[2]) for o, h in zip(outs, halves)]


def _adamw(w, g, m, v):
    m = ADAM_B1 * m + (1.0 - ADAM_B1) * g
    v = ADAM_B2 * v + (1.0 - ADAM_B2) * (g * g)
    m_hat = m / (1.0 - ADAM_B1 ** ADAM_STEP)
    v_hat = v / (1.0 - ADAM_B2 ** ADAM_STEP)
    delta = -ADAM_LR * (m_hat / (jnp.sqrt(v_hat) + ADAM_EPS) + ADAM_WD * w)
    return delta, m, v


def _adamw_call(ws, gs, ms, vs, nch, name):
    nw = len(ws)

    def body(*refs):
        for w in range(nw):
            g = refs[nw + w][...]
            delta, m, v = _adamw(refs[w][...], g, refs[2 * nw + w][...], refs[3 * nw + w][...])
            refs[4 * nw + w][...] = g
            refs[5 * nw + w][...] = delta
            refs[6 * nw + w][...] = m
            refs[7 * nw + w][...] = v

    specs = [pl.BlockSpec((a.shape[0] // nch, a.shape[1]), lambda i: (i, 0)) for a in ws]
    res = pl.pallas_call(
        body, name=name,
        grid=(nch,),
        in_specs=specs * 4, out_specs=specs * 4,
        out_shape=[jax.ShapeDtypeStruct(a.shape, F32) for a in ws] * 4,
        compiler_params=_params(("arbitrary",)),
    )(*ws, *gs, *ms, *vs)
    return res[:nw], res[nw:2 * nw], res[2 * nw:3 * nw], res[3 * nw:]


def _small_call(gathered, own, me_idx, w, m, v):
    def fold(row):
        tot = row[:, 0:LANES] + row[:, LANES:2 * LANES] + row[:, 2 * LANES:3 * LANES] + row[:, 3 * LANES:4 * LANES]
        return tot + pltpu.roll(tot, HEAD_DIM, axis=1)

    def body(me_ref, ga_ref, own_ref, w_ref, m_ref, v_ref, g_out, d_out, m_out, v_out):
        me = me_ref[0]
        term = lambda i: jnp.where(me == i, own_ref[...], ga_ref[i])
        g = term(0)
        for i in range(1, N_DEV):
            g = g + term(i)
        unfolded = g[4:5, :]
        folded = jnp.concatenate([fold(unfolded[:, :ATTN_WIDTH]), fold(unfolded[:, ATTN_WIDTH:]),
                                  jnp.zeros((1, 1024 - 2 * LANES), F32)], axis=-1)
        row = lax.broadcasted_iota(jnp.int32, g.shape, 0)
        g = jnp.where(row == 3, folded, g)
        delta, mm, vv = _adamw(w_ref[...], g, m_ref[...], v_ref[...])
        g_out[...] = g
        d_out[...] = delta
        m_out[...] = mm
        v_out[...] = vv

    vmem = pl.BlockSpec(memory_space=pltpu.VMEM)
    return pl.pallas_call(
        body, name="adamw_small",
        in_specs=[pl.BlockSpec(memory_space=pltpu.SMEM)] + [vmem] * 5,
        out_shape=[jax.ShapeDtypeStruct(w.shape, F32)] * 4,
        compiler_params=_params(),
    )(me_idx, gathered, own, w, m, v)


def _pack_small(p, folded=True, loss=None):
    z = lambda n: jnp.zeros((n,), F32)
    rows = [p["mix_norm_g"], p["mlp_norm_g"],
            jnp.concatenate([p["pool_scale"], p["rel_bias"].reshape(-1), z(1024 - POOL_WIDTH - N_BUCKETS * N_HEADS)])]
    if folded:
        rows += [jnp.concatenate([p["q_norm_g"], z(LANES - HEAD_DIM), p["k_norm_g"], z(1024 - LANES - HEAD_DIM)]), z(1024)]
    else:
        rows += [z(1024), jnp.concatenate([p["q_norm_g"], p["k_norm_g"]])]
    rows += [z(1024) if loss is None else jnp.concatenate([loss.reshape(1), z(1023)])]
    head = jnp.stack(rows + [z(1024)] * 2)
    return jnp.concatenate([head, p["pool_w"].reshape(-1, 1024)], axis=0)


def _unpack_small(a):
    return dict(
        mix_norm_g=a[0], mlp_norm_g=a[1], pool_scale=a[2, :POOL_WIDTH],
        rel_bias=a[2, POOL_WIDTH:POOL_WIDTH + N_BUCKETS * N_HEADS].reshape(N_BUCKETS, N_HEADS),
        q_norm_g=a[3, :HEAD_DIM], k_norm_g=a[3, LANES:LANES + HEAD_DIM],
        pool_w=a[8:].reshape(len(POOL_WINDOWS), LANES, LANES))


_WEIGHT_ORDER = ("mix_norm_g", "w_in", "pool_w", "pool_scale", "q_norm_g", "k_norm_g", "rel_bias", "w_out",
                 "mlp_norm_g", "w_up", "w_down")
_BIG = ("w_in", "w_out", "w_up", "w_down")


def kernel(x, mix_norm_g, w_in, pool_w, pool_scale, q_norm_g, k_norm_g, rel_bias, w_out, mlp_norm_g, w_up, w_down, loss_target, m_mix_norm_g, m_w_in, m_pool_w, m_pool_scale, m_q_norm_g, m_k_norm_g, m_rel_bias, m_w_out, m_mlp_norm_g, m_w_up, m_w_down, v_mix_norm_g, v_w_in, v_pool_w, v_pool_scale, v_q_norm_g, v_k_norm_g, v_rel_bias, v_w_out, v_mlp_norm_g, v_w_up, v_w_down):
    w = dict(mix_norm_g=mix_norm_g, w_in=w_in, pool_w=pool_w, pool_scale=pool_scale, q_norm_g=q_norm_g,
             k_norm_g=k_norm_g, rel_bias=rel_bias, w_out=w_out, mlp_norm_g=mlp_norm_g, w_up=w_up, w_down=w_down)
    m = dict(mix_norm_g=m_mix_norm_g, w_in=m_w_in, pool_w=m_pool_w, pool_scale=m_pool_scale, q_norm_g=m_q_norm_g,
             k_norm_g=m_k_norm_g, rel_bias=m_rel_bias, w_out=m_w_out, mlp_norm_g=m_mlp_norm_g, w_up=m_w_up, w_down=m_w_down)
    v = dict(mix_norm_g=v_mix_norm_g, w_in=v_w_in, pool_w=v_pool_w, pool_scale=v_pool_scale, q_norm_g=v_q_norm_g,
             k_norm_g=v_k_norm_g, rel_bias=v_rel_bias, w_out=v_w_out, mlp_norm_g=v_mlp_norm_g, w_up=v_w_up, w_down=v_w_down)
    xc, yc, cc = _coords()

    c_idx = jnp.reshape(cc, (1,)).astype(jnp.int32)
    chip_idx = jnp.reshape(2 * xc + yc, (1,)).astype(jnp.int32)
    me = 4 * xc + 2 * yc + cc
    whole = lambda t: t.reshape(t.shape[0], t.shape[1] * t.shape[2], t.shape[3])

    placed = [_halves(p) for p in _place_shards_call([w[n] for n in _BIG], chip_idx, nch=4)]
    win_f, bias = _allgather_call(placed[:1], from_chips=True, name="weights_allgather_in",
                                  meanwhile=_bias_table_work(rel_bias))
    wsend, wrecv, in_flight, started = _gather_start_call(placed[1:], win_f)

    def mlp_weights(after):
        landed = _gather_wait_call(in_flight, wsend, wrecv, after)
        wout_f, wup_f, wdown_f = _allgather_call(landed, from_chips=False, name="weights_pair_forward")
        return whole(wout_f).reshape(-1, wout_f.shape[-1]), whole(wup_f), whole(wdown_f)

    split = []

    def on_mlp_grads(*wire_grads):
        srcs = [_halves(g) for g in wire_grads]
        lands = [lax.empty((N_DEV,) + s.shape[2:], s.dtype) for s in srcs]
        split.extend(_scatter_start_call(srcs, lands, [False] * len(srcs), "grads_scatter_start"))
        return split[4]

    loss_part, dx, big_grads, small_grads = _local_grads(
        x[0], loss_target[0], mix_norm_g + started[0, 0], whole(win_f), pool_w, pool_scale, q_norm_g, k_norm_g, rel_bias,
        mlp_norm_g, mlp_weights, on_mlp_grads, bias)
    g_in, g_out, g_up, g_down = big_grads
    gsend, grecv, srcs_thru, lands_thru, _ = split
    lands_mlp = _scatter_wait_call(srcs_thru, lands_thru, gsend, grecv, g_in[1], [False] * 3, "grads_scatter_wait")

    small_own = _pack_small(small_grads, folded=False, loss=loss_part)
    last_srcs = [_halves(g_in[1]), small_own]
    last_lands = [lax.empty((N_DEV,) + last_srcs[0].shape[2:], WIRE_DTYPE), lax.empty((N_DEV,) + small_own.shape, F32)]
    lsend, lrecv, last_srcs, last_lands, last_started = _scatter_start_call(
        last_srcs, last_lands, [False, True], "grads_scatter_start_last")
    idx = jnp.concatenate([chip_idx, c_idx] + [jnp.reshape(jnp.bitwise_xor(me, r), (1,)) for r in range(1, N_DEV)])
    idx = idx.astype(jnp.int32)
    mlp = _BIG[1:]

    def update(names, own32, lands, tag, dep=None):
        halves = _reduce_call([_halves(g) for g in own32], lands, idx, 4, "grads_reduce_" + tag, dep)
        reduced = _pair_allgather_call(list(halves), "grads_pair_allgather_" + tag)
        return _adamw_call([w[n] for n in names], reduced, [m[n] for n in names], [v[n] for n in names], 8, "adamw_" + tag)

    out_mlp = update(mlp, [g_out[0], g_up[0], g_down[0]], lands_mlp, "mlp", last_started)
    land_in, small_all = _scatter_wait_call(last_srcs, last_lands, lsend, lrecv, out_mlp[3][-1], [False, True],
                                            "grads_scatter_wait_last")
    out_in = update(_BIG[:1], [g_in[0]], [land_in], "in")
    g_pack, d_pack, m_pack, v_pack = _small_call(
        small_all, small_own, jnp.reshape(me, (1,)).astype(jnp.int32), _pack_small(w), _pack_small(m), _pack_small(v))

    grads, deltas, new_m, new_v = (_unpack_small(a) for a in (g_pack, d_pack, m_pack, v_pack))
    for k, res in enumerate((grads, deltas, new_m, new_v)):
        res[_BIG[0]] = out_in[k][0]
        for i, n in enumerate(mlp):
            res[n] = out_mlp[k][i]
    loss = g_pack[LOSS_ROW, 0]
    return (loss, dx[None], *[grads[n] for n in _WEIGHT_ORDER], *[deltas[n] for n in _WEIGHT_ORDER],
            *[new_m[n] for n in _WEIGHT_ORDER], *[new_v[n] for n in _WEIGHT_ORDER])
```

```python
import math

import jax
import jax.numpy as jnp
import numpy as np
from jax import lax
from jax.experimental import pallas as pl
from jax.experimental.pallas import tpu as pltpu

F32 = jnp.float32
MXU_DTYPE = jnp.bfloat16
WIRE_DTYPE = jnp.bfloat16

NORM_EPS = 1e-6
NEG_INF = -1e30
LANES = 128
HEAD_DIM = 64
N_HEADS = 8
POOL_WIDTH = 512
ATTN_WIDTH = 512
POOL_WINDOWS = (2, 4, 8, 16)
POOL_HALO = 16
DILATED_PATTERNS = ((128, 1), (512, 4), (2048, 16))
ATT_BLOCK = 128
ATT_SUPER = ATT_BLOCK * max(dl for _, dl in DILATED_PATTERNS)
ATT_UNITS = ATT_SUPER // ATT_BLOCK
N_BUCKETS = 32
NO_BUCKET = -1
MAX_DISTANCE = 2048
N_CHIPS = 4
N_DEV = 8
ADAM_LR, ADAM_B1, ADAM_B2, ADAM_EPS, ADAM_WD, ADAM_STEP = 0.001, 0.9, 0.999, 1e-08, 0.01, 10
VMEM_LIMIT = 56 * 1024 * 1024
MESH = pl.DeviceIdType.MESH
ANY = pl.BlockSpec(memory_space=pl.ANY)

SMALL_ROWS = 72
LOSS_ROW = 5


def _mm(a, b):
    return jnp.dot(a, b, preferred_element_type=F32)


def _mm_nt(a, b):
    return lax.dot_general(a, b, (((1,), (1,)), ((), ())), preferred_element_type=F32)


def _mm_tn(a, b):
    return lax.dot_general(a, b, (((0,), (0,)), ((), ())), preferred_element_type=F32)


def _params(sem=None, **kw):
    if sem is not None:
        kw["dimension_semantics"] = sem
    return pltpu.CompilerParams(vmem_limit_bytes=VMEM_LIMIT, **kw)


def _low_half():
    return lax.broadcasted_iota(jnp.int32, (1, LANES), 1) < HEAD_DIM


def _head_sum_bcast(y):
    lo = _low_half()
    outs = []
    for j in range(y.shape[1] // LANES):
        c = y[:, j * LANES:(j + 1) * LANES]
        s_lo = jnp.sum(jnp.where(lo, c, 0.0), axis=-1, keepdims=True)
        s_hi = jnp.sum(jnp.where(lo, 0.0, c), axis=-1, keepdims=True)
        outs.append(jnp.where(lo, s_lo, s_hi))
    return jnp.concatenate(outs, axis=-1)


def _rms_bwd(dn, hn, r):
    return r * (dn - hn * jnp.mean(dn * hn, axis=-1, keepdims=True))


def _t5_bucket_np(dist):
    max_exact = N_BUCKETS // 2
    d_f = np.maximum(dist, 1).astype(np.float32)
    ratio = (np.log(d_f / np.float32(max_exact)) / np.float32(math.log(MAX_DISTANCE / max_exact))).astype(np.float32)
    large = max_exact + (ratio * np.float32(N_BUCKETS - max_exact)).astype(np.int32)
    large = np.minimum(large, N_BUCKETS - 1)
    return np.where(dist < max_exact, dist, large).astype(np.int32)


def _window_offsets(dl):
    if dl == 1:
        return _by4_positions(ATT_BLOCK), _by4_positions(2 * ATT_BLOCK)
    return np.arange(ATT_BLOCK), np.arange(2 * ATT_BLOCK)


def _bucket_tables():
    tables = []
    for _, dl in DILATED_PATTERNS:
        qq, kk = _window_offsets(dl)
        dist = qq[:, None] + ATT_BLOCK - kk[None, :]
        bucket = _t5_bucket_np(np.clip(dist, 0, ATT_BLOCK) * dl)
        tables.append(np.where((dist >= 0) & (dist <= ATT_BLOCK), bucket, NO_BUCKET))
    return np.stack(tables).astype(np.int32)


def _previous_block_keys():
    return np.stack([np.broadcast_to(_window_offsets(dl)[1][None, :] < ATT_BLOCK, (ATT_BLOCK, 2 * ATT_BLOCK))
                     for _, dl in DILATED_PATTERNS])


def _f1_call(x, g1, win, poolw, pscale, qg, kg, tm):
    s, d = x.shape
    nblk = s // tm

    def body(x_ref, g1_ref, win_ref, pw_ref, ps_ref, qg_ref, kg_ref,
             a_ref, pooled_ref, ypool_ref, q32_ref, k32_ref, qn_ref, kn_ref, v_ref, ubuf):
        i = pl.program_id(0)
        xv = x_ref[...]
        r = lax.rsqrt(jnp.mean(xv * xv, axis=-1, keepdims=True) + NORM_EPS)
        a = ((xv * r) * g1_ref[...]).astype(MXU_DTYPE)
        a_ref[...] = a
        u = _mm(a, win_ref[0])
        q = _mm(a, win_ref[1])
        k = _mm(a, win_ref[2])
        v_ref[...] = _mm(a, win_ref[3])
        q32_ref[...] = q
        k32_ref[...] = k
        rq = lax.rsqrt(_head_sum_bcast(q * q) * (1.0 / HEAD_DIM) + NORM_EPS)
        qn_ref[...] = ((q * rq) * qg_ref[...]) * (HEAD_DIM ** -0.5)
        rk = lax.rsqrt(_head_sum_bcast(k * k) * (1.0 / HEAD_DIM) + NORM_EPS)
        kn_ref[...] = (k * rk) * kg_ref[...]

        @pl.when(i == 0)
        def _():
            ubuf[0:POOL_HALO, :] = jnp.zeros((POOL_HALO, POOL_WIDTH), F32)

        @pl.when(i > 0)
        def _():
            ubuf[0:POOL_HALO, :] = ubuf[tm:tm + POOL_HALO, :]

        ubuf[POOL_HALO:POOL_HALO + tm, :] = u
        t = i * tm + lax.broadcasted_iota(jnp.int32, (tm, 1), 0)
        for g, w in enumerate(POOL_WINDOWS):
            ls = slice(g * LANES, (g + 1) * LANES)
            ug = u[:, ls]
            acc = ug
            for sh in range(1, w):
                acc = acc + ubuf[POOL_HALO - sh:POOL_HALO - sh + tm, ls]
            cnt = jnp.minimum(t + 1, w).astype(F32)
            pooled = (acc / cnt - ug).astype(MXU_DTYPE)
            pooled_ref[:, ls] = pooled
            ypool_ref[:, ls] = (_mm(pooled, pw_ref[g]) * ps_ref[:, ls]).astype(MXU_DTYPE)

    tok = lambda w: pl.BlockSpec((tm, w), lambda i: (i, 0))
    full = lambda shp: pl.BlockSpec(shp, lambda i: (0,) * len(shp))
    return pl.pallas_call(
        body, name="fwd_inproj",
        grid=(nblk,),
        in_specs=[tok(d), full((1, d)), full(win.shape), full(poolw.shape), full((1, POOL_WIDTH)),
                  full((1, ATTN_WIDTH)), full((1, ATTN_WIDTH))],
        out_specs=[tok(d), tok(POOL_WIDTH), tok(POOL_WIDTH), tok(ATTN_WIDTH), tok(ATTN_WIDTH),
                   tok(ATTN_WIDTH), tok(ATTN_WIDTH), tok(ATTN_WIDTH)],
        out_shape=[jax.ShapeDtypeStruct((s, d), MXU_DTYPE),
                   jax.ShapeDtypeStruct((s, POOL_WIDTH), MXU_DTYPE),
                   jax.ShapeDtypeStruct((s, POOL_WIDTH), MXU_DTYPE),
                   jax.ShapeDtypeStruct((s, ATTN_WIDTH), F32),
                   jax.ShapeDtypeStruct((s, ATTN_WIDTH), F32),
                   jax.ShapeDtypeStruct((s, ATTN_WIDTH), F32),
                   jax.ShapeDtypeStruct((s, ATTN_WIDTH), F32),
                   jax.ShapeDtypeStruct((s, ATTN_WIDTH), F32)],
        scratch_shapes=[pltpu.VMEM((tm + POOL_HALO, POOL_WIDTH), F32)],
        compiler_params=_params(("arbitrary",)),
    )(x, g1, win, poolw, pscale, qg, kg)


DEINT = 4
assert [dl for _, dl in DILATED_PATTERNS] == [1, DEINT, DEINT * DEINT]


def _by4_positions(n):
    pos = np.arange(n)
    return DEINT * (pos % (n // DEINT)) + pos // (n // DEINT)


def _masked_bias(b_ref, p, n):
    return b_ref[p, jnp.minimum(n, 1)].reshape(2 * ATT_BLOCK, 2 * ATT_BLOCK)


def _unit_rows(u, dl):
    sq, sk = ATT_SUPER // DEINT, 2 * ATT_SUPER // DEINT
    if dl == 1:
        n = ATT_BLOCK // DEINT
        return (u, [pl.ds(pl.multiple_of(r * sq + n * u, 8), n) for r in range(DEINT)],
                [pl.ds(pl.multiple_of(r * sk + sk // 2 + n * (u - 1), 8), 2 * n) for r in range(DEINT)])
    if dl == DEINT:
        r, b = u % DEINT, u // DEINT
        return (b, [pl.ds(pl.multiple_of(r * sq + ATT_BLOCK * b, 8), ATT_BLOCK)],
                [pl.ds(pl.multiple_of(r * sk + sk // 2 + ATT_BLOCK * (b - 1), 8), 2 * ATT_BLOCK)])
    r, a = u % DEINT, u // DEINT
    return 0, [pl.ds(r * sq + a, ATT_BLOCK, stride=DEINT)], [pl.ds(r * sk + a, 2 * ATT_BLOCK, stride=DEINT)]


def _take(ref, runs):
    parts = [ref[run, :] for run in runs]
    return parts[0] if len(parts) == 1 else jnp.concatenate(parts, axis=0)


def _put(ref, runs, value, add=False):
    n = value.shape[0] // len(runs)
    for i, run in enumerate(runs):
        part = value[i * n:(i + 1) * n]
        ref[run, :] = ref[run, :] + part if add else part


def _deinterleave(dst, src, n):
    seg = n // DEINT
    for r in range(DEINT):
        dst[r * seg:(r + 1) * seg, :] = src[pl.ds(r, seg, stride=DEINT), :]


def _deinterleave_pair(dst, prev, cur):
    seg = prev.shape[0] // DEINT
    for r in range(DEINT):
        dst[2 * r * seg:(2 * r + 1) * seg, :] = prev[pl.ds(r, seg, stride=DEINT), :]
        dst[(2 * r + 1) * seg:(2 * r + 2) * seg, :] = cur[pl.ds(r, seg, stride=DEINT), :]


def _interleave(dst, src, n, offset=0):
    seg = n // DEINT
    stride = src.shape[0] // DEINT
    for r in range(DEINT):
        dst[pl.ds(r, seg, stride=DEINT), :] = src[r * stride + offset:r * stride + offset + seg, :]


def _attn_fwd_call(qn, kn, v, bias):
    s, w = qn.shape
    nsb = s // ATT_SUPER
    npair = w // LANES

    def body(q_ref, kc_ref, kp_ref, vc_ref, vp_ref, b_ref, o_ref, lse_ref, qf, kf, vf, acc_s, m_s, l_s):
        sb = pl.program_id(1)
        _deinterleave(qf, q_ref, ATT_SUPER)
        _deinterleave_pair(kf, kp_ref, kc_ref)
        _deinterleave_pair(vf, vp_ref, vc_ref)
        lo = _low_half()
        for p, (_, dl) in enumerate(DILATED_PATTERNS):
            def unit(u, carry, p=p, dl=dl):
                b, rows_q, rows_k = _unit_rows(u, dl)
                qp = _take(qf, rows_q).astype(MXU_DTYPE)
                kcat = _take(kf, rows_k).astype(MXU_DTYPE)
                vcat = _take(vf, rows_k).astype(MXU_DTYPE)
                zero = jnp.zeros_like(qp)
                q2 = jnp.concatenate([jnp.where(lo, qp, zero), jnp.where(lo, zero, qp)], axis=0)
                sc = _mm_nt(q2, kcat) + _masked_bias(b_ref, p, sb * (ATT_UNITS // dl) + b)
                m2 = jnp.max(sc, axis=-1, keepdims=True)
                pr = jnp.exp(sc - m2)
                l2 = jnp.sum(pr, axis=-1, keepdims=True)
                acc2 = _mm(pr.astype(MXU_DTYPE), vcat)
                acc = jnp.where(lo, acc2[:ATT_BLOCK], acc2[ATT_BLOCK:])
                m = jnp.where(lo, m2[:ATT_BLOCK], m2[ATT_BLOCK:])
                l = jnp.where(lo, l2[:ATT_BLOCK], l2[ATT_BLOCK:])
                if p == 0:
                    _put(acc_s, rows_q, acc)
                    _put(m_s, rows_q, m)
                    _put(l_s, rows_q, l)
                else:
                    m_old = _take(m_s, rows_q)
                    m_new = jnp.maximum(m_old, m)
                    a_old = jnp.exp(m_old - m_new)
                    a_new = jnp.exp(m - m_new)
                    _put(acc_s, rows_q, a_old * _take(acc_s, rows_q) + a_new * acc)
                    _put(l_s, rows_q, a_old * _take(l_s, rows_q) + a_new * l)
                    _put(m_s, rows_q, m_new)
                return carry

            lax.fori_loop(0, ATT_UNITS, unit, 0, unroll=16)
        l = l_s[...]
        acc_s[...] = acc_s[...] / l
        m_s[...] = m_s[...] + jnp.log(l)
        _interleave(o_ref, acc_s, ATT_SUPER)
        _interleave(lse_ref, m_s, ATT_SUPER)

    cur = pl.BlockSpec((ATT_SUPER, LANES), lambda j, t: (t, j))
    prev = pl.BlockSpec((ATT_SUPER, LANES), lambda j, t: (jnp.maximum(t - 1, 0), j))
    bspec = pl.BlockSpec((len(DILATED_PATTERNS), 2, 2, ATT_BLOCK, 2 * ATT_BLOCK), lambda j, t: (0, 0, j, 0, 0))
    return pl.pallas_call(
        body, name="attn_fwd",
        grid=(npair, nsb),
        in_specs=[cur, cur, prev, cur, prev, bspec],
        out_specs=[cur, cur],
        out_shape=[jax.ShapeDtypeStruct((s, w), F32), jax.ShapeDtypeStruct((s, w), F32)],
        scratch_shapes=[pltpu.VMEM((ATT_SUPER, LANES), F32), pltpu.VMEM((2 * ATT_SUPER, LANES), F32),
                        pltpu.VMEM((2 * ATT_SUPER, LANES), F32), pltpu.VMEM((ATT_SUPER, LANES), F32),
                        pltpu.VMEM((ATT_SUPER, LANES), F32), pltpu.VMEM((ATT_SUPER, LANES), F32)],
        compiler_params=_params(("arbitrary", "arbitrary")),
    )(qn, kn, kn, v, v, bias)


def _attn_bwd_call(qn, kn, v, do, lse, delta, bias, dep=None):
    s, w = qn.shape
    nsb = s // ATT_SUPER
    npair = w // LANES
    deps = [] if dep is None else [dep]

    def body(q_ref, kc_ref, kp_ref, vc_ref, vp_ref, do_ref, lse_ref, dlt_ref, b_ref, *rest):
        dq_ref, dk_ref, dv_ref, db_ref, qf, kf, vf, dof, lsef, dltf, dqf, dkf, dvf = rest[len(deps):]
        step = pl.program_id(1)
        sb = nsb - 1 - step
        seg = ATT_SUPER // DEINT
        _deinterleave(qf, q_ref, ATT_SUPER)
        _deinterleave(dof, do_ref, ATT_SUPER)
        _deinterleave_pair(kf, kp_ref, kc_ref)
        _deinterleave_pair(vf, vp_ref, vc_ref)
        _deinterleave(lsef, lse_ref, ATT_SUPER)
        _deinterleave(dltf, dlt_ref, ATT_SUPER)

        @pl.when(step == 0)
        def _():
            db_ref[...] = jnp.zeros(db_ref.shape, F32)

        for acc in (dkf, dvf):
            for r in range(DEINT):
                this, before = pl.ds((2 * r + 1) * seg, seg), pl.ds(2 * r * seg, seg)

                @pl.when(step == 0)
                def _(acc=acc, this=this):
                    acc[this, :] = jnp.zeros((seg, LANES), F32)

                @pl.when(step > 0)
                def _(acc=acc, this=this, before=before):
                    acc[this, :] = acc[before, :]

                acc[before, :] = jnp.zeros((seg, LANES), F32)
        lo = _low_half()
        for p, (_, dl) in enumerate(DILATED_PATTERNS):
            def unit(u, carry, p=p, dl=dl):
                b, rows_q, rows_k = _unit_rows(u, dl)
                qp = _take(qf, rows_q).astype(MXU_DTYPE)
                dop = _take(dof, rows_q).astype(MXU_DTYPE)
                kcat = _take(kf, rows_k).astype(MXU_DTYPE)
                vcat = _take(vf, rows_k).astype(MXU_DTYPE)
                lse2 = _take(lsef, rows_q)
                dlt2 = _take(dltf, rows_q)
                zero = jnp.zeros_like(qp)
                q2 = jnp.concatenate([jnp.where(lo, qp, zero), jnp.where(lo, zero, qp)], axis=0)
                do2 = jnp.concatenate([jnp.where(lo, dop, zero), jnp.where(lo, zero, dop)], axis=0)
                lse_c = jnp.concatenate([lse2[:, 0:1], lse2[:, HEAD_DIM:HEAD_DIM + 1]], axis=0)
                dlt_c = jnp.concatenate([dlt2[:, 0:1], dlt2[:, HEAD_DIM:HEAD_DIM + 1]], axis=0)
                sc = _mm_nt(q2, kcat) + _masked_bias(b_ref, p, sb * (ATT_UNITS // dl) + b)
                pr = jnp.exp(sc - lse_c)
                ds = pr * (_mm_nt(do2, vcat) - dlt_c)
                db_ref[p] += ds.reshape(2, ATT_BLOCK, 2 * ATT_BLOCK)
                ds_c = ds.astype(MXU_DTYPE)
                dq2 = _mm(ds_c, kcat)
                dk = _mm_tn(ds_c, q2)
                dv = _mm_tn(pr.astype(MXU_DTYPE), do2)
                dq = jnp.where(lo, dq2[:ATT_BLOCK], dq2[ATT_BLOCK:])
                _put(dqf, rows_q, dq, add=p > 0)
                _put(dkf, rows_k, dk, add=True)
                _put(dvf, rows_k, dv, add=True)
                return carry

            lax.fori_loop(0, ATT_UNITS, unit, 0, unroll=16)
        _interleave(dq_ref, dqf, ATT_SUPER)
        _interleave(dk_ref, dkf, ATT_SUPER, offset=seg)
        _interleave(dv_ref, dvf, ATT_SUPER, offset=seg)

    cur = pl.BlockSpec((ATT_SUPER, LANES), lambda j, t: (nsb - 1 - t, j))
    prev = pl.BlockSpec((ATT_SUPER, LANES), lambda j, t: (jnp.maximum(nsb - 2 - t, 0), j))
    npat = len(DILATED_PATTERNS)
    bspec = pl.BlockSpec((npat, 2, 2, ATT_BLOCK, 2 * ATT_BLOCK), lambda j, t: (0, 0, j, 0, 0))
    dbspec = pl.BlockSpec((npat, 2, ATT_BLOCK, 2 * ATT_BLOCK), lambda j, t: (0, j, 0, 0))
    sup = lambda: pltpu.VMEM((ATT_SUPER, LANES), F32)
    sup2 = lambda: pltpu.VMEM((2 * ATT_SUPER, LANES), F32)
    return pl.pallas_call(
        body, name="attn_bwd",
        grid=(npair, nsb),
        in_specs=[cur, cur, prev, cur, prev, cur, cur, cur, bspec] + [ANY] * len(deps),
        out_specs=[cur, cur, cur, dbspec],
        out_shape=[jax.ShapeDtypeStruct((s, w), F32)] * 3
        + [jax.ShapeDtypeStruct((npat, N_HEADS, ATT_BLOCK, 2 * ATT_BLOCK), F32)],
        scratch_shapes=[sup(), sup2(), sup2(), sup(), sup(), sup(), sup(), sup2(), sup2()],
        compiler_params=_params(("arbitrary", "arbitrary")),
    )(qn, kn, kn, v, v, do, lse, delta, bias, *deps)


def _bias_table_work(rel_bias):
    buckets = jnp.asarray(_bucket_tables())
    prev_keys = jnp.asarray(_previous_block_keys().astype(np.int32))
    npat = buckets.shape[0]

    def body(rb_ref, bk_ref, pk_ref, out_ref):
        for p in range(npat):
            for half in range(2):
                ks = slice(half * ATT_BLOCK, (half + 1) * ATT_BLOCK)
                bk = bk_ref[p, :, ks]
                absent = pk_ref[p, :, ks] != 0
                for h in range(N_HEADS):
                    def pick(b, acc, h=h, bk=bk):
                        return jnp.where(bk == b, rb_ref[b, h], acc)

                    tab = lax.fori_loop(0, N_BUCKETS, pick, jnp.full((ATT_BLOCK, ATT_BLOCK), NEG_INF, F32))
                    out_ref[p, 1, h, :, ks] = tab
                    out_ref[p, 0, h, :, ks] = jnp.where(absent, NEG_INF, tab)

    vmem = pl.BlockSpec(memory_space=pltpu.VMEM)
    return ([rel_bias, buckets, prev_keys], [pl.BlockSpec(memory_space=pltpu.SMEM), vmem, vmem],
            jax.ShapeDtypeStruct((npat, 2, N_HEADS, ATT_BLOCK, 2 * ATT_BLOCK), F32), body)


def _bias_table_call(rel_bias):
    operands, specs, shape, body = _bias_table_work(rel_bias)
    return pl.pallas_call(body, name="bias_table", in_specs=specs, out_shape=shape, compiler_params=_params())(*operands)


def _rel_bias_grad_call(dbias, buckets):
    npat, nh = dbias.shape[0], dbias.shape[1]

    def body(db_ref, bk_ref, out_ref):
        lane = lax.broadcasted_iota(jnp.int32, (nh, LANES), 1)
        out = jnp.zeros((nh, LANES), F32)
        for b in range(N_BUCKETS):
            tot = jnp.zeros((nh, 1), F32)
            for p in range(npat):
                hit = jnp.where(bk_ref[p][None] == b, db_ref[p], 0.0)
                tot = tot + jnp.sum(jnp.sum(hit, axis=1), axis=-1, keepdims=True)
            out = jnp.where(lane == b, tot, out)
        out_ref[...] = out

    return pl.pallas_call(
        body, name="rel_bias_grad",
        out_shape=jax.ShapeDtypeStruct((nh, LANES), F32),
        compiler_params=_params(),
    )(dbias, buckets)


def _f2_call(x, tgt, ypool, o, wout, wup, wdown, g2, tm):
    s, d = x.shape
    nblk = s // tm
    nch, _, fch = wup.shape
    dff = nch * fch
    mixw = POOL_WIDTH + ATTN_WIDTH

    def body(x_ref, t_ref, yp_ref, o_ref, g2_ref, wout_hbm, wup_hbm, wdown_hbm,
             mixed_ref, c_ref, ff_ref, dz_ref, dy_ref, dh1_ref, dyp_ref, do_ref, dlt_ref, dg2_ref, loss_ref,
             wout_v, wup_v, wdown_v, rz):
        i = pl.program_id(0)

        @pl.when(i == 0)
        def _():
            pltpu.sync_copy(wout_hbm, wout_v)
            pltpu.sync_copy(wup_hbm, wup_v)
            pltpu.sync_copy(wdown_hbm, wdown_v)
            dg2_ref[...] = jnp.zeros(dg2_ref.shape, F32)
            loss_ref[...] = jnp.zeros(loss_ref.shape, F32)

        o = o_ref[...]
        mixed = jnp.concatenate([yp_ref[...], o.astype(MXU_DTYPE)], axis=-1)
        mixed_ref[...] = mixed
        h1 = x_ref[...] + _mm(mixed, wout_v[...])
        r2 = lax.rsqrt(jnp.mean(h1 * h1, axis=-1, keepdims=True) + NORM_EPS)
        hn = h1 * r2
        c = (hn * g2_ref[...]).astype(MXU_DTYPE)
        c_ref[...] = c
        y = h1
        for j in range(nch):
            cs = slice(j * fch, (j + 1) * fch)
            z = jnp.maximum(_mm(c, wup_v[j]), 0.0)
            rz[:, cs] = z
            ff = (z * z).astype(MXU_DTYPE)
            ff_ref[:, cs] = ff
            y = y + _mm(ff, wdown_v[j])
        err = y - t_ref[...]
        loss_ref[...] += jnp.sum(err * err) * (0.5 / d)
        dy = err * (1.0 / d)
        dy_c = dy.astype(MXU_DTYPE)
        dy_ref[...] = dy_c
        dc = jnp.zeros((tm, d), F32)
        for j in range(nch):
            cs = slice(j * fch, (j + 1) * fch)
            dz = (_mm_nt(dy_c, wdown_v[j]) * (2.0 * rz[:, cs])).astype(MXU_DTYPE)
            dz_ref[:, cs] = dz
            dc = dc + _mm_nt(dz, wup_v[j])
        dg2_ref[...] += jnp.sum(dc * hn, axis=0, keepdims=True)
        dh1 = dy + _rms_bwd(dc * g2_ref[...], hn, r2)
        dh1_ref[...] = dh1
        dmix = _mm_nt(dh1.astype(MXU_DTYPE), wout_v[...])
        dyp_ref[...] = dmix[:, :POOL_WIDTH]
        do = dmix[:, POOL_WIDTH:]
        do_ref[...] = do
        dlt_ref[...] = _head_sum_bcast(do * o)

    tok = lambda w: pl.BlockSpec((tm, w), lambda i: (i, 0))
    const = lambda shp: pl.BlockSpec(shp, lambda i: (0,) * len(shp))
    return pl.pallas_call(
        body, name="fwd_mlp_bwd_mlp",
        grid=(nblk,),
        in_specs=[tok(d), tok(d), tok(POOL_WIDTH), tok(ATTN_WIDTH), const((1, d)), ANY, ANY, ANY],
        out_specs=[tok(mixw), tok(d), tok(dff), tok(dff), tok(d), tok(d), tok(POOL_WIDTH), tok(ATTN_WIDTH),
                   tok(ATTN_WIDTH), const((1, d)), const((1, LANES))],
        out_shape=[jax.ShapeDtypeStruct((s, mixw), MXU_DTYPE),
                   jax.ShapeDtypeStruct((s, d), MXU_DTYPE),
                   jax.ShapeDtypeStruct((s, dff), MXU_DTYPE),
                   jax.ShapeDtypeStruct((s, dff), MXU_DTYPE),
                   jax.ShapeDtypeStruct((s, d), MXU_DTYPE),
                   jax.ShapeDtypeStruct((s, d), F32),
                   jax.ShapeDtypeStruct((s, POOL_WIDTH), F32),
                   jax.ShapeDtypeStruct((s, ATTN_WIDTH), F32),
                   jax.ShapeDtypeStruct((s, ATTN_WIDTH), F32),
                   jax.ShapeDtypeStruct((1, d), F32),
                   jax.ShapeDtypeStruct((1, LANES), F32)],
        scratch_shapes=[pltpu.VMEM(wout.shape, MXU_DTYPE), pltpu.VMEM(wup.shape, MXU_DTYPE),
                        pltpu.VMEM(wdown.shape, MXU_DTYPE), pltpu.VMEM((tm, dff), F32)],
        compiler_params=_params(("arbitrary",)),
    )(x, tgt, ypool, o, g2, wout, wup, wdown)


def _bproj_call(dqn, dkn, dv, q32, k32, dypool, pooled, x, dh1, win, poolw, pscale, qg, kg, g1, tm):
    s, d = x.shape
    nblk = s // tm
    ngrp = len(POOL_WINDOWS)

    def body(dqn_ref, dkn_ref, dv_ref, q_ref, k_ref, dyp_ref, pooled_ref, x_ref, dh1_ref,
             win_hbm, pw_ref, ps_ref, qg_ref, kg_ref, g1_ref,
             dx_ref, dproj_ref, dg1_ref, dqg_ref, dkg_ref, dpw_ref, dps_ref, win_v, ebuf):
        step = pl.program_id(0)
        i = nblk - 1 - step

        @pl.when(step == 0)
        def _():
            pltpu.sync_copy(win_hbm, win_v)
            dg1_ref[...] = jnp.zeros(dg1_ref.shape, F32)
            dqg_ref[...] = jnp.zeros(dqg_ref.shape, F32)
            dkg_ref[...] = jnp.zeros(dkg_ref.shape, F32)
            dpw_ref[...] = jnp.zeros(dpw_ref.shape, F32)
            dps_ref[...] = jnp.zeros(dps_ref.shape, F32)
            ebuf[tm:tm + POOL_HALO, :] = jnp.zeros((POOL_HALO, POOL_WIDTH), F32)

        @pl.when(step > 0)
        def _():
            ebuf[tm:tm + POOL_HALO, :] = ebuf[0:POOL_HALO, :]

        def qk_bwd(dn_sum, raw, gain, scale, dgain_ref):
            rr = lax.rsqrt(_head_sum_bcast(raw * raw) * (1.0 / HEAD_DIM) + NORM_EPS)
            hn = raw * rr
            dgain_ref[...] += jnp.sum(dn_sum * hn, axis=0, keepdims=True) * scale
            dn = dn_sum * (gain * scale)
            return rr * (dn - hn * (_head_sum_bcast(dn * hn) * (1.0 / HEAD_DIM)))

        dq = qk_bwd(dqn_ref[...], q_ref[...], qg_ref[...], HEAD_DIM ** -0.5, dqg_ref)
        dk = qk_bwd(dkn_ref[...], k_ref[...], kg_ref[...], 1.0, dkg_ref)

        t = i * tm + lax.broadcasted_iota(jnp.int32, (tm, 1), 0)
        dpooled = []
        for g, w in enumerate(POOL_WINDOWS):
            ls = slice(g * LANES, (g + 1) * LANES)
            dm = dyp_ref[:, ls]
            pg = pooled_ref[:, ls]
            dps_ref[:, ls] += jnp.sum(dm * _mm(pg, pw_ref[g]), axis=0, keepdims=True)
            dms = (dm * ps_ref[:, ls]).astype(MXU_DTYPE)
            dpw_ref[g] += _mm_tn(pg, dms)
            dpg = _mm_nt(dms, pw_ref[g])
            dpooled.append(dpg)
            ebuf[0:tm, ls] = dpg / jnp.minimum(t + 1, w).astype(F32)
        du = []
        for g, w in enumerate(POOL_WINDOWS):
            ls = slice(g * LANES, (g + 1) * LANES)
            acc = ebuf[0:tm, ls]
            for sh in range(1, w):
                acc = acc + ebuf[sh:sh + tm, ls]
            du.append(acc - dpooled[g])
        parts = [jnp.concatenate(du, axis=-1), dq, dk, dv_ref[...]]
        da = jnp.zeros((tm, d), F32)
        for p, part in enumerate(parts):
            pc = part.astype(MXU_DTYPE)
            dproj_ref[:, p * POOL_WIDTH:(p + 1) * POOL_WIDTH] = pc
            da = da + _mm_nt(pc, win_v[p])
        xv = x_ref[...]
        r = lax.rsqrt(jnp.mean(xv * xv, axis=-1, keepdims=True) + NORM_EPS)
        xn = xv * r
        dg1_ref[...] += jnp.sum(da * xn, axis=0, keepdims=True)
        dx_ref[...] = dh1_ref[...] + _rms_bwd(da * g1_ref[...], xn, r)

    tok = lambda w: pl.BlockSpec((tm, w), lambda t: (nblk - 1 - t, 0))
    const = lambda shp: pl.BlockSpec(shp, lambda t: (0,) * len(shp))
    return pl.pallas_call(
        body, name="bwd_inproj",
        grid=(nblk,),
        in_specs=[tok(ATTN_WIDTH)] * 5 + [tok(POOL_WIDTH), tok(POOL_WIDTH), tok(d), tok(d),
                                          ANY, const(poolw.shape), const((1, POOL_WIDTH)), const((1, ATTN_WIDTH)),
                                          const((1, ATTN_WIDTH)), const((1, d))],
        out_specs=[tok(d), tok(4 * POOL_WIDTH), const((1, d)), const((1, ATTN_WIDTH)), const((1, ATTN_WIDTH)),
                   const((ngrp, LANES, LANES)), const((1, POOL_WIDTH))],
        out_shape=[jax.ShapeDtypeStruct((s, d), F32),
                   jax.ShapeDtypeStruct((s, 4 * POOL_WIDTH), MXU_DTYPE),
                   jax.ShapeDtypeStruct((1, d), F32),
                   jax.ShapeDtypeStruct((1, ATTN_WIDTH), F32),
                   jax.ShapeDtypeStruct((1, ATTN_WIDTH), F32),
                   jax.ShapeDtypeStruct((ngrp, LANES, LANES), F32),
                   jax.ShapeDtypeStruct((1, POOL_WIDTH), F32)],
        scratch_shapes=[pltpu.VMEM(win.shape, MXU_DTYPE), pltpu.VMEM((tm + POOL_HALO, POOL_WIDTH), F32)],
        compiler_params=_params(("arbitrary",)),
    )(dqn, dkn, dv, q32, k32, dypool, pooled, x, dh1, win, poolw, pscale, qg, kg, g1)


def _wgrad_call(a, b, bm, bn, bk, out_shape, out_block, out_index, name):
    s, m = a.shape
    _, n = b.shape
    nk = s // bk

    def body(a_ref, b_ref, o_ref, wire_ref):
        k = pl.program_id(2)

        @pl.when(k == 0)
        def _():
            o_ref[...] = jnp.zeros(o_ref.shape, F32)

        o_ref[...] += _mm_tn(a_ref[...].astype(MXU_DTYPE), b_ref[...].astype(MXU_DTYPE))

        @pl.when(k == nk - 1)
        def _():
            wire_ref[...] = o_ref[...].astype(WIRE_DTYPE)

    return pl.pallas_call(
        body, name=name,
        grid=(m // bm, n // bn, nk),
        in_specs=[pl.BlockSpec((bk, bm), lambda i, j, k: (k, i)), pl.BlockSpec((bk, bn), lambda i, j, k: (k, j))],
        out_specs=[pl.BlockSpec(out_block, out_index)] * 2,
        out_shape=[jax.ShapeDtypeStruct(out_shape, F32), jax.ShapeDtypeStruct(out_shape, WIRE_DTYPE)],
        compiler_params=_params(("arbitrary", "arbitrary", "arbitrary")),
    )(a, b)


def _local_grads(x, tgt, g1, win, poolw, pscale, qg, kg, rel_bias, g2, mlp_weights, on_mlp_grads=None, bias=None):
    s, d = x.shape
    g1r, g2r = g1.reshape(1, d), g2.reshape(1, d)
    psr = pscale.reshape(1, POOL_WIDTH)
    qgr = jnp.tile(qg, N_HEADS).reshape(1, ATTN_WIDTH)
    kgr = jnp.tile(kg, N_HEADS).reshape(1, ATTN_WIDTH)
    pw_c = poolw.astype(MXU_DTYPE)
    buckets = jnp.asarray(_bucket_tables())
    bias = _bias_table_call(rel_bias) if bias is None else bias
    bk = min(s, 4096)

    a, pooled, ypool, q32, k32, qn, kn, v = _f1_call(x, g1r, win, pw_c, psr, qgr, kgr, tm=512)
    o, lse = _attn_fwd_call(qn, kn, v, bias)
    wout, wup, wdown = mlp_weights(o)
    mixed, c, ff, dz, dy, dh1, dypool, do, delta, dg2, loss = _f2_call(x, tgt, ypool, o, wout, wup, wdown, g2r, tm=256)
    dff = ff.shape[1]
    g_out = [g.reshape(N_CHIPS, d // N_CHIPS, d)
             for g in _wgrad_call(mixed, dh1, d, d, bk // 4, (d, d), (d, d), lambda i, j, k: (0, 0), "wgrad_out")]
    g_up = _wgrad_call(c, dz, d, dff // N_CHIPS, bk, (N_CHIPS, d, dff // N_CHIPS), (None, d, dff // N_CHIPS),
                       lambda i, j, k: (j, 0, 0), "wgrad_up")
    g_down = _wgrad_call(ff, dy, dff // N_CHIPS, d, bk, (N_CHIPS, dff // N_CHIPS, d), (None, dff // N_CHIPS, d),
                         lambda i, j, k: (i, 0, 0), "wgrad_down")
    dep = None if on_mlp_grads is None else on_mlp_grads(g_out[1], g_up[1], g_down[1])
    dqn, dkn, dv, dbias = _attn_bwd_call(qn, kn, v, do, lse, delta, bias, dep)
    dx, dproj, dg1, dqg, dkg, dpw, dps = _bproj_call(
        dqn, dkn, dv, q32, k32, dypool, pooled, x, dh1, win, pw_c, psr, qgr, kgr, g1r, tm=512)
    nin = dproj.shape[1] // N_CHIPS
    g_in = _wgrad_call(a, dproj, d, nin, bk, (N_CHIPS, d, nin), (None, d, nin), lambda i, j, k: (j, 0, 0), "wgrad_in")
    drb = _rel_bias_grad_call(dbias, buckets)
    small = dict(
        mix_norm_g=dg1.reshape(d), mlp_norm_g=dg2.reshape(d), pool_scale=dps.reshape(POOL_WIDTH),
        q_norm_g=dqg.reshape(ATTN_WIDTH), k_norm_g=dkg.reshape(ATTN_WIDTH),
        rel_bias=drb[:, :N_BUCKETS].T, pool_w=dpw)
    return loss[0, 0], dx, (g_in, g_out, g_up, g_down), small


def _coords():
    return lax.axis_index("x"), lax.axis_index("y"), lax.axis_index("c")


def _other_chips(x, y):
    return [(1 - x, y), (x, 1 - y), (1 - x, 1 - y)]


def _remote(src, dst, send_sem, recv_sem, dev):
    return pltpu.make_async_remote_copy(src_ref=src, dst_ref=dst, send_sem=send_sem, recv_sem=recv_sem,
                                        device_id=dev, device_id_type=MESH)


def _halves(a):
    return a.reshape(a.shape[:-2] + (2, a.shape[-2] // 2, a.shape[-1]))


def _place_shards_call(shards, chip_idx, nch):
    nw = len(shards)

    def body(chip_ref, *refs):
        for w in range(nw):
            refs[nw + w][...] = refs[w][...].astype(WIRE_DTYPE)

    in_specs = [pl.BlockSpec((s.shape[0] // nch, s.shape[1]), lambda i, chip_ref: (i, 0)) for s in shards]
    out_specs = [pl.BlockSpec((None, s.shape[0] // nch, s.shape[1]), lambda i, chip_ref: (chip_ref[0], i, 0))
                 for s in shards]
    return pl.pallas_call(
        body, name="weights_place",
        grid_spec=pltpu.PrefetchScalarGridSpec(num_scalar_prefetch=1, grid=(nch,),
                                               in_specs=in_specs, out_specs=out_specs),
        out_shape=[jax.ShapeDtypeStruct((N_CHIPS,) + s.shape, WIRE_DTYPE) for s in shards],
        compiler_params=_params(("arbitrary",)),
    )(chip_idx, *shards)


def _allgather_call(placed, from_chips, name, meanwhile=None):
    nw = len(placed)
    ncp = 3 * nw
    extra, extra_specs, extra_shape, extra_body = meanwhile if meanwhile else ([], [], None, None)
    ne = len(extra)

    def body(*refs):
        outs = refs[nw + ne:2 * nw + ne]
        send1, recv1, send2, recv2 = refs[-4:]
        x, y, c = _coords()
        chip = 2 * x + y
        others = _other_chips(x, y)
        first, passed = [], []
        if from_chips:
            for w in range(nw):
                for k, (ox, oy) in enumerate(others):
                    mine = outs[w].at[chip, c]
                    cp = _remote(mine, mine, send1.at[3 * w + k], recv1.at[3 * w + k], (ox, oy, c))
                    cp.start()
                    first.append(cp)
        if meanwhile:
            extra_body(*refs[nw:nw + ne], refs[2 * nw + ne])
        for w in range(nw):
            for k, (ox, oy) in enumerate(others):
                piece = outs[w].at[2 * ox + oy, c]
                if from_chips:
                    _remote(piece, piece, send1.at[3 * w + k], recv1.at[3 * w + k], (ox, oy, c)).wait_recv()
                cp = _remote(piece, piece, send2.at[3 * w + k], recv2.at[3 * w + k], (x, y, 1 - c))
                cp.start()
                passed.append(cp)
        for w in range(nw):
            for k, (ox, oy) in enumerate(others):
                piece = outs[w].at[2 * ox + oy, 1 - c]
                _remote(piece, piece, send2.at[3 * w + k], recv2.at[3 * w + k], (x, y, 1 - c)).wait_recv()
        for cp in first + passed:
            cp.wait_send()

    return pl.pallas_call(
        body, name=name,
        in_specs=[ANY] * nw + list(extra_specs),
        out_specs=[ANY] * nw + ([pl.BlockSpec(memory_space=pltpu.VMEM)] if meanwhile else []),
        out_shape=[jax.ShapeDtypeStruct(s.shape, s.dtype) for s in placed] + ([extra_shape] if meanwhile else []),
        input_output_aliases={w: w for w in range(nw)},
        scratch_shapes=[pltpu.SemaphoreType.DMA((ncp,))] * 4,
        compiler_params=_params(),
    )(*placed, *extra)


HBM_SPEC = pl.BlockSpec(memory_space=pltpu.HBM)
SEM_SPEC = pl.BlockSpec(memory_space=pltpu.SEMAPHORE)
SPLIT_EFFECT = pltpu.SideEffectType.DATAFLOW_SIDE_EFFECTING


def _in_hbm(a):
    return pltpu.with_memory_space_constraint(a, pltpu.HBM)


def _gather_copies(bufs, send, recv):
    x, y, c = _coords()
    chip = 2 * x + y
    cps = []
    for w, buf in enumerate(bufs):
        for k, (ox, oy) in enumerate(_other_chips(x, y)):
            mine, theirs = buf.at[chip, c], buf.at[2 * ox + oy, c]
            sems = (send.at[3 * w + k], recv.at[3 * w + k], (ox, oy, c))
            cps.append((_remote(mine, mine, *sems), _remote(theirs, theirs, *sems)))
    return cps


def _gather_start_call(bufs, after):
    nw = len(bufs)

    def body(*refs):
        ins, send, recv, token = refs[:nw], refs[nw + 1], refs[nw + 2], refs[2 * nw + 3]
        for out, _ in _gather_copies(ins, send, recv):
            out.start()
        token[...] = jnp.zeros(token.shape, F32)

    res = pl.pallas_call(
        body, name="weights_gather_start",
        in_specs=[HBM_SPEC] * nw + [ANY],
        out_specs=[SEM_SPEC, SEM_SPEC] + [HBM_SPEC] * nw + [pl.BlockSpec(memory_space=pltpu.VMEM)],
        out_shape=[pltpu.SemaphoreType.DMA((3 * nw,)), pltpu.SemaphoreType.DMA((3 * nw,))]
        + [pltpu.HBM(b.shape, b.dtype) for b in bufs] + [jax.ShapeDtypeStruct((8, LANES), F32)],
        input_output_aliases={w: 2 + w for w in range(nw)},
        compiler_params=pltpu.CompilerParams(has_side_effects=SPLIT_EFFECT),
    )(*[_in_hbm(b) for b in bufs], after)
    return res[0], res[1], list(res[2:2 + nw]), res[2 + nw]


def _gather_wait_call(bufs, send, recv, after):
    nw = len(bufs)

    def body(*refs):
        ins, send, recv = refs[:nw], refs[nw], refs[nw + 1]
        for out, back in _gather_copies(ins, send, recv):
            out.wait_send()
            back.wait_recv()

    return pl.pallas_call(
        body, name="weights_gather_wait",
        in_specs=[HBM_SPEC] * nw + [SEM_SPEC, SEM_SPEC, ANY],
        out_specs=[HBM_SPEC] * nw,
        out_shape=[pltpu.HBM(b.shape, b.dtype) for b in bufs],
        input_output_aliases={w: w for w in range(nw)},
        compiler_params=pltpu.CompilerParams(has_side_effects=SPLIT_EFFECT),
    )(*bufs, send, recv, after)


def _scatter_copies(srcs, lands, send, recv, wholes):
    x, y, c = _coords()
    me = 4 * x + 2 * y + c
    cps = []
    for w, (src, land) in enumerate(zip(srcs, lands)):
        for r in range(1, N_DEV):
            px, py, pc = ((1 - x) if r & 4 else x, (1 - y) if r & 2 else y, (1 - c) if r & 1 else c)
            sems = (send.at[(N_DEV - 1) * w + r - 1], recv.at[(N_DEV - 1) * w + r - 1], (px, py, pc))
            piece = src if wholes[w] else src.at[2 * px + py, pc]
            cps.append((_remote(piece, land.at[me], *sems), _remote(piece, land.at[4 * px + 2 * py + pc], *sems)))
    return cps


def _scatter_start_call(srcs, lands, wholes, name):
    nw = len(srcs)
    ncp = (N_DEV - 1) * nw

    def body(*refs):
        ins, lnd, send, recv, token = refs[:nw], refs[nw:2 * nw], refs[2 * nw], refs[2 * nw + 1], refs[4 * nw + 2]
        for out, _ in _scatter_copies(ins, lnd, send, recv, wholes):
            out.start()
        token[...] = jnp.zeros(token.shape, F32)

    res = pl.pallas_call(
        body, name=name,
        in_specs=[HBM_SPEC] * (2 * nw),
        out_specs=[SEM_SPEC, SEM_SPEC] + [HBM_SPEC] * (2 * nw) + [pl.BlockSpec(memory_space=pltpu.VMEM)],
        out_shape=[pltpu.SemaphoreType.DMA((ncp,)), pltpu.SemaphoreType.DMA((ncp,))]
        + [pltpu.HBM(b.shape, b.dtype) for b in list(srcs) + list(lands)] + [jax.ShapeDtypeStruct((8, LANES), F32)],
        input_output_aliases={i: 2 + i for i in range(2 * nw)},
        compiler_params=pltpu.CompilerParams(has_side_effects=SPLIT_EFFECT),
    )(*[_in_hbm(b) for b in list(srcs) + list(lands)])
    return res[0], res[1], list(res[2:2 + nw]), list(res[2 + nw:2 + 2 * nw]), res[2 + 2 * nw]


def _scatter_wait_call(srcs, lands, send, recv, after, wholes, name):
    nw = len(srcs)

    def body(*refs):
        ins, lnd, send, recv = refs[:nw], refs[nw:2 * nw], refs[2 * nw], refs[2 * nw + 1]
        for out, back in _scatter_copies(ins, lnd, send, recv, wholes):
            out.wait_send()
            back.wait_recv()

    res = pl.pallas_call(
        body, name=name,
        in_specs=[HBM_SPEC] * (2 * nw) + [SEM_SPEC, SEM_SPEC, ANY],
        out_specs=[HBM_SPEC] * (2 * nw),
        out_shape=[pltpu.HBM(b.shape, b.dtype) for b in list(srcs) + list(lands)],
        input_output_aliases={i: i for i in range(2 * nw)},
        compiler_params=pltpu.CompilerParams(has_side_effects=SPLIT_EFFECT),
    )(*srcs, *lands, send, recv, after)
    return list(res[nw:])


def _reduce_call(own, lands, idx, nch, name, dep=None):
    nw = len(own)
    deps = [] if dep is None else [dep]

    def body(idx_ref, *refs):
        refs = refs[:2 * nw] + refs[2 * nw + len(deps):]
        for w in range(nw):
            tot = refs[w][...]
            for r in range(1, N_DEV):
                tot = tot + refs[nw + w][idx_ref[1 + r]].astype(F32)
            refs[2 * nw + w][...] = tot

    in_specs, out_specs, out_shape = [], [], []
    for s in own:
        in_specs.append(pl.BlockSpec((None, None, s.shape[2] // nch, s.shape[3]),
                                     lambda i, idx_ref: (idx_ref[0], idx_ref[1], i, 0)))
    for s in own:
        in_specs.append(pl.BlockSpec((N_DEV, s.shape[2] // nch, s.shape[3]), lambda i, idx_ref: (0, i, 0)))
    for s in own:
        out_specs.append(pl.BlockSpec((None, s.shape[2] // nch, s.shape[3]), lambda i, idx_ref: (idx_ref[1], i, 0)))
        out_shape.append(jax.ShapeDtypeStruct((2,) + s.shape[2:], F32))
    return pl.pallas_call(
        body, name=name,
        grid_spec=pltpu.PrefetchScalarGridSpec(num_scalar_prefetch=1, grid=(nch,),
                                               in_specs=in_specs + [ANY] * len(deps), out_specs=out_specs),
        out_shape=out_shape,
        compiler_params=_params(("arbitrary",)),
    )(idx, *own, *lands, *deps)


def _pair_allgather_call(halves, name):
    nw = len(halves)

    def body(*refs):
        outs = refs[nw:2 * nw]
        send, recv = refs[2 * nw:]
        x, y, c = _coords()
        cps = []
        for w in range(nw):
            cp = _remote(outs[w].at[c], outs[w].at[c], send.at[w], recv.at[w], (x, y, 1 - c))
            cp.start()
            cps.append(cp)
        for w in range(nw):
            theirs = outs[w].at[1 - c]
            _remote(theirs, theirs, send.at[w], recv.at[w], (x, y, 1 - c)).wait_recv()
        for cp in cps:
            cp.wait_send()

    outs = pl.pallas_call(
        body, name=name,
        in_specs=[ANY] * nw, out_specs=[ANY] * nw,
        out_shape=[jax.ShapeDtypeStruct(h.shape, h.dtype) for h in halves],
        input_output_aliases={w: w for w in range(nw)},
        scratch_shapes=[pltpu.SemaphoreType.DMA((nw,))] * 2,
    )(*halves)
    return [o.reshape(2 * h.shape[1], h.shape[2]) for o, h in zip(outs, halves)]


def _adamw(w, g, m, v):
    m = ADAM_B1 * m + (1.0 - ADAM_B1) * g
    v = ADAM_B2 * v + (1.0 - ADAM_B2) * (g * g)
    m_hat = m / (1.0 - ADAM_B1 ** ADAM_STEP)
    v_hat = v / (1.0 - ADAM_B2 ** ADAM_STEP)
    delta = -ADAM_LR * (m_hat / (jnp.sqrt(v_hat) + ADAM_EPS) + ADAM_WD * w)
    return delta, m, v


def _adamw_call(ws, gs, ms, vs, nch, name):
    nw = len(ws)

    def body(*refs):
        for w in range(nw):
            g = refs[nw + w][...]
            delta, m, v = _adamw(refs[w][...], g, refs[2 * nw + w][...], refs[3 * nw + w][...])
            refs[4 * nw + w][...] = g
            refs[5 * nw + w][...] = delta
            refs[6 * nw + w][...] = m
            refs[7 * nw + w][...] = v

    specs = [pl.BlockSpec((a.shape[0] // nch, a.shape[1]), lambda i: (i, 0)) for a in ws]
    res = pl.pallas_call(
        body, name=name,
        grid=(nch,),
        in_specs=specs * 4, out_specs=specs * 4,
        out_shape=[jax.ShapeDtypeStruct(a.shape, F32) for a in ws] * 4,
        compiler_params=_params(("arbitrary",)),
    )(*ws, *gs, *ms, *vs)
    return res[:nw], res[nw:2 * nw], res[2 * nw:3 * nw], res[3 * nw:]


def _small_call(gathered, own, me_idx, w, m, v):
    def fold(row):
        tot = row[:, 0:LANES] + row[:, LANES:2 * LANES] + row[:, 2 * LANES:3 * LANES] + row[:, 3 * LANES:4 * LANES]
        return tot + pltpu.roll(tot, HEAD_DIM, axis=1)

    def body(me_ref, ga_ref, own_ref, w_ref, m_ref, v_ref, g_out, d_out, m_out, v_out):
        me = me_ref[0]
        term = lambda i: jnp.where(me == i, own_ref[...], ga_ref[i])
        g = term(0)
        for i in range(1, N_DEV):
            g = g + term(i)
        unfolded = g[4:5, :]
        folded = jnp.concatenate([fold(unfolded[:, :ATTN_WIDTH]), fold(unfolded[:, ATTN_WIDTH:]),
                                  jnp.zeros((1, 1024 - 2 * LANES), F32)], axis=-1)
        row = lax.broadcasted_iota(jnp.int32, g.shape, 0)
        g = jnp.where(row == 3, folded, g)
        delta, mm, vv = _adamw(w_ref[...], g, m_ref[...], v_ref[...])
        g_out[...] = g
        d_out[...] = delta
        m_out[...] = mm
        v_out[...] = vv

    vmem = pl.BlockSpec(memory_space=pltpu.VMEM)
    return pl.pallas_call(
        body, name="adamw_small",
        in_specs=[pl.BlockSpec(memory_space=pltpu.SMEM)] + [vmem] * 5,
        out_shape=[jax.ShapeDtypeStruct(w.shape, F32)] * 4,
        compiler_params=_params(),
    )(me_idx, gathered, own, w, m, v)


def _pack_small(p, folded=True, loss=None):
    z = lambda n: jnp.zeros((n,), F32)
    rows = [p["mix_norm_g"], p["mlp_norm_g"],
            jnp.concatenate([p["pool_scale"], p["rel_bias"].reshape(-1), z(1024 - POOL_WIDTH - N_BUCKETS * N_HEADS)])]
    if folded:
        rows += [jnp.concatenate([p["q_norm_g"], z(LANES - HEAD_DIM), p["k_norm_g"], z(1024 - LANES - HEAD_DIM)]), z(1024)]
    else:
        rows += [z(1024), jnp.concatenate([p["q_norm_g"], p["k_norm_g"]])]
    rows += [z(1024) if loss is None else jnp.concatenate([loss.reshape(1), z(1023)])]
    head = jnp.stack(rows + [z(1024)] * 2)
    return jnp.concatenate([head, p["pool_w"].reshape(-1, 1024)], axis=0)


def _unpack_small(a):
    return dict(
        mix_norm_g=a[0], mlp_norm_g=a[1], pool_scale=a[2, :POOL_WIDTH],
        rel_bias=a[2, POOL_WIDTH:POOL_WIDTH + N_BUCKETS * N_HEADS].reshape(N_BUCKETS, N_HEADS),
        q_norm_g=a[3, :HEAD_DIM], k_norm_g=a[3, LANES:LANES + HEAD_DIM],
        pool_w=a[8:].reshape(len(POOL_WINDOWS), LANES, LANES))


_WEIGHT_ORDER = ("mix_norm_g", "w_in", "pool_w", "pool_scale", "q_norm_g", "k_norm_g", "rel_bias", "w_out",
                 "mlp_norm_g", "w_up", "w_down")
_BIG = ("w_in", "w_out", "w_up", "w_down")


def kernel(x, mix_norm_g, w_in, pool_w, pool_scale, q_norm_g, k_norm_g, rel_bias, w_out, mlp_norm_g, w_up, w_down, loss_target, m_mix_norm_g, m_w_in, m_pool_w, m_pool_scale, m_q_norm_g, m_k_norm_g, m_rel_bias, m_w_out, m_mlp_norm_g, m_w_up, m_w_down, v_mix_norm_g, v_w_in, v_pool_w, v_pool_scale, v_q_norm_g, v_k_norm_g, v_rel_bias, v_w_out, v_mlp_norm_g, v_w_up, v_w_down):
    w = dict(mix_norm_g=mix_norm_g, w_in=w_in, pool_w=pool_w, pool_scale=pool_scale, q_norm_g=q_norm_g,
             k_norm_g=k_norm_g, rel_bias=rel_bias, w_out=w_out, mlp_norm_g=mlp_norm_g, w_up=w_up, w_down=w_down)
    m = dict(mix_norm_g=m_mix_norm_g, w_in=m_w_in, pool_w=m_pool_w, pool_scale=m_pool_scale, q_norm_g=m_q_norm_g,
             k_norm_g=m_k_norm_g, rel_bias=m_rel_bias, w_out=m_w_out, mlp_norm_g=m_mlp_norm_g, w_up=m_w_up, w_down=m_w_down)
    v = dict(mix_norm_g=v_mix_norm_g, w_in=v_w_in, pool_w=v_pool_w, pool_scale=v_pool_scale, q_norm_g=v_q_norm_g,
             k_norm_g=v_k_norm_g, rel_bias=v_rel_bias, w_out=v_w_out, mlp_norm_g=v_mlp_norm_g, w_up=v_w_up, w_down=v_w_down)
    xc, yc, cc = _coords()

    c_idx = jnp.reshape(cc, (1,)).astype(jnp.int32)
    chip_idx = jnp.reshape(2 * xc + yc, (1,)).astype(jnp.int32)
    me = 4 * xc + 2 * yc + cc
    whole = lambda t: t.reshape(t.shape[0], t.shape[1] * t.shape[2], t.shape[3])

    placed = [_halves(p) for p in _place_shards_call([w[n] for n in _BIG], chip_idx, nch=4)]
    win_f, bias = _allgather_call(placed[:1], from_chips=True, name="weights_allgather_in",
                                  meanwhile=_bias_table_work(rel_bias))
    wsend, wrecv, in_flight, started = _gather_start_call(placed[1:], win_f)

    def mlp_weights(after):
        landed = _gather_wait_call(in_flight, wsend, wrecv, after)
        wout_f, wup_f, wdown_f = _allgather_call(landed, from_chips=False, name="weights_pair_forward")
        return whole(wout_f).reshape(-1, wout_f.shape[-1]), whole(wup_f), whole(wdown_f)

    split = []

    def on_mlp_grads(*wire_grads):
        srcs = [_halves(g) for g in wire_grads]
        lands = [lax.empty((N_DEV,) + s.shape[2:], s.dtype) for s in srcs]
        split.extend(_scatter_start_call(srcs, lands, [False] * len(srcs), "grads_scatter_start"))
        return split[4]

    loss_part, dx, big_grads, small_grads = _local_grads(
        x[0], loss_target[0], mix_norm_g + started[0, 0], whole(win_f), pool_w, pool_scale, q_norm_g, k_norm_g, rel_bias,
        mlp_norm_g, mlp_weights, on_mlp_grads, bias)
    g_in, g_out, g_up, g_down = big_grads
    gsend, grecv, srcs_thru, lands_thru, _ = split
    lands_mlp = _scatter_wait_call(srcs_thru, lands_thru, gsend, grecv, g_in[1], [False] * 3, "grads_scatter_wait")

    small_own = _pack_small(small_grads, folded=False, loss=loss_part)
    last_srcs = [_halves(g_in[1]), small_own]
    last_lands = [lax.empty((N_DEV,) + last_srcs[0].shape[2:], WIRE_DTYPE), lax.empty((N_DEV,) + small_own.shape, F32)]
    lsend, lrecv, last_srcs, last_lands, last_started = _scatter_start_call(
        last_srcs, last_lands, [False, True], "grads_scatter_start_last")
    idx = jnp.concatenate([chip_idx, c_idx] + [jnp.reshape(jnp.bitwise_xor(me, r), (1,)) for r in range(1, N_DEV)])
    idx = idx.astype(jnp.int32)
    mlp = _BIG[1:]

    def update(names, own32, lands, tag, dep=None):
        halves = _reduce_call([_halves(g) for g in own32], lands, idx, 4, "grads_reduce_" + tag, dep)
        reduced = _pair_allgather_call(list(halves), "grads_pair_allgather_" + tag)
        return _adamw_call([w[n] for n in names], reduced, [m[n] for n in names], [v[n] for n in names], 8, "adamw_" + tag)

    out_mlp = update(mlp, [g_out[0], g_up[0], g_down[0]], lands_mlp, "mlp", last_started)
    land_in, small_all = _scatter_wait_call(last_srcs, last_lands, lsend, lrecv, out_mlp[3][-1], [False, True],
                                            "grads_scatter_wait_last")
    out_in = update(_BIG[:1], [g_in[0]], [land_in], "in")
    g_pack, d_pack, m_pack, v_pack = _small_call(
        small_all, small_own, jnp.reshape(me, (1,)).astype(jnp.int32), _pack_small(w), _pack_small(m), _pack_small(v))

    grads, deltas, new_m, new_v = (_unpack_small(a) for a in (g_pack, d_pack, m_pack, v_pack))
    for k, res in enumerate((grads, deltas, new_m, new_v)):
        res[_BIG[0]] = out_in[k][0]
        for i, n in enumerate(mlp):
            res[n] = out_mlp[k][i]
    loss = g_pack[LOSS_ROW, 0]
    return (loss, dx[None], *[grads[n] for n in _WEIGHT_ORDER], *[deltas[n] for n in _WEIGHT_ORDER],
            *[new_m[n] for n in _WEIGHT_ORDER], *[new_v[n] for n in _WEIGHT_ORDER])
```

```python
import math

import jax
import jax.numpy as jnp
import numpy as np
from jax import lax
from jax.experimental import pallas as pl
from jax.experimental.pallas import tpu as pltpu

F32 = jnp.float32
MXU_DTYPE = jnp.bfloat16
WIRE_DTYPE = jnp.bfloat16

NORM_EPS = 1e-6
NEG_INF = -1e30
LANES = 128
HEAD_DIM = 64
N_HEADS = 8
POOL_WIDTH = 512
ATTN_WIDTH = 512
POOL_WINDOWS = (2, 4, 8, 16)
POOL_HALO = 16
DILATED_PATTERNS = ((128, 1), (512, 4), (2048, 16))
ATT_BLOCK = 128
ATT_SUPER = ATT_BLOCK * max(dl for _, dl in DILATED_PATTERNS)
ATT_UNITS = ATT_SUPER // ATT_BLOCK
N_BUCKETS = 32
NO_BUCKET = -1
MAX_DISTANCE = 2048
N_CHIPS = 4
N_DEV = 8
ADAM_LR, ADAM_B1, ADAM_B2, ADAM_EPS, ADAM_WD, ADAM_STEP = 0.001, 0.9, 0.999, 1e-08, 0.01, 10
VMEM_LIMIT = 56 * 1024 * 1024
MESH = pl.DeviceIdType.MESH
ANY = pl.BlockSpec(memory_space=pl.ANY)

SMALL_ROWS = 72
LOSS_ROW = 5


def _mm(a, b):
    return jnp.dot(a, b, preferred_element_type=F32)


def _mm_nt(a, b):
    return lax.dot_general(a, b, (((1,), (1,)), ((), ())), preferred_element_type=F32)


def _mm_tn(a, b):
    return lax.dot_general(a, b, (((0,), (0,)), ((), ())), preferred_element_type=F32)


def _params(sem=None, **kw):
    if sem is not None:
        kw["dimension_semantics"] = sem
    return pltpu.CompilerParams(vmem_limit_bytes=VMEM_LIMIT, **kw)


def _low_half():
    return lax.broadcasted_iota(jnp.int32, (1, LANES), 1) < HEAD_DIM


def _head_sum_bcast(y):
    lo = _low_half()
    outs = []
    for j in range(y.shape[1] // LANES):
        c = y[:, j * LANES:(j + 1) * LANES]
        s_lo = jnp.sum(jnp.where(lo, c, 0.0), axis=-1, keepdims=True)
        s_hi = jnp.sum(jnp.where(lo, 0.0, c), axis=-1, keepdims=True)
        outs.append(jnp.where(lo, s_lo, s_hi))
    return jnp.concatenate(outs, axis=-1)


def _rms_bwd(dn, hn, r):
    return r * (dn - hn * jnp.mean(dn * hn, axis=-1, keepdims=True))


def _t5_bucket_np(dist):
    max_exact = N_BUCKETS // 2
    d_f = np.maximum(dist, 1).astype(np.float32)
    ratio = (np.log(d_f / np.float32(max_exact)) / np.float32(math.log(MAX_DISTANCE / max_exact))).astype(np.float32)
    large = max_exact + (ratio * np.float32(N_BUCKETS - max_exact)).astype(np.int32)
    large = np.minimum(large, N_BUCKETS - 1)
    return np.where(dist < max_exact, dist, large).astype(np.int32)


def _window_offsets(dl):
    if dl == 1:
        return _by4_positions(ATT_BLOCK), _by4_positions(2 * ATT_BLOCK)
    return np.arange(ATT_BLOCK), np.arange(2 * ATT_BLOCK)


def _bucket_tables():
    tables = []
    for _, dl in DILATED_PATTERNS:
        qq, kk = _window_offsets(dl)
        dist = qq[:, None] + ATT_BLOCK - kk[None, :]
        bucket = _t5_bucket_np(np.clip(dist, 0, ATT_BLOCK) * dl)
        tables.append(np.where((dist >= 0) & (dist <= ATT_BLOCK), bucket, NO_BUCKET))
    return np.stack(tables).astype(np.int32)


def _previous_block_keys():
    return np.stack([np.broadcast_to(_window_offsets(dl)[1][None, :] < ATT_BLOCK, (ATT_BLOCK, 2 * ATT_BLOCK))
                     for _, dl in DILATED_PATTERNS])


def _f1_call(x, g1, win, poolw, pscale, qg, kg, tm):
    s, d = x.shape
    nblk = s // tm

    def body(x_ref, g1_ref, win_ref, pw_ref, ps_ref, qg_ref, kg_ref,
             a_ref, pooled_ref, ypool_ref, q32_ref, k32_ref, qn_ref, kn_ref, v_ref, ubuf):
        i = pl.program_id(0)
        xv = x_ref[...]
        r = lax.rsqrt(jnp.mean(xv * xv, axis=-1, keepdims=True) + NORM_EPS)
        a = ((xv * r) * g1_ref[...]).astype(MXU_DTYPE)
        a_ref[...] = a
        u = _mm(a, win_ref[0])
        q = _mm(a, win_ref[1])
        k = _mm(a, win_ref[2])
        v_ref[...] = _mm(a, win_ref[3])
        q32_ref[...] = q
        k32_ref[...] = k
        rq = lax.rsqrt(_head_sum_bcast(q * q) * (1.0 / HEAD_DIM) + NORM_EPS)
        qn_ref[...] = ((q * rq) * qg_ref[...]) * (HEAD_DIM ** -0.5)
        rk = lax.rsqrt(_head_sum_bcast(k * k) * (1.0 / HEAD_DIM) + NORM_EPS)
        kn_ref[...] = (k * rk) * kg_ref[...]

        @pl.when(i == 0)
        def _():
            ubuf[0:POOL_HALO, :] = jnp.zeros((POOL_HALO, POOL_WIDTH), F32)

        @pl.when(i > 0)
        def _():
            ubuf[0:POOL_HALO, :] = ubuf[tm:tm + POOL_HALO, :]

        ubuf[POOL_HALO:POOL_HALO + tm, :] = u
        t = i * tm + lax.broadcasted_iota(jnp.int32, (tm, 1), 0)
        for g, w in enumerate(POOL_WINDOWS):
            ls = slice(g * LANES, (g + 1) * LANES)
            ug = u[:, ls]
            acc = ug
            for sh in range(1, w):
                acc = acc + ubuf[POOL_HALO - sh:POOL_HALO - sh + tm, ls]
            cnt = jnp.minimum(t + 1, w).astype(F32)
            pooled = (acc / cnt - ug).astype(MXU_DTYPE)
            pooled_ref[:, ls] = pooled
            ypool_ref[:, ls] = (_mm(pooled, pw_ref[g]) * ps_ref[:, ls]).astype(MXU_DTYPE)

    tok = lambda w: pl.BlockSpec((tm, w), lambda i: (i, 0))
    full = lambda shp: pl.BlockSpec(shp, lambda i: (0,) * len(shp))
    return pl.pallas_call(
        body, name="fwd_inproj",
        grid=(nblk,),
        in_specs=[tok(d), full((1, d)), full(win.shape), full(poolw.shape), full((1, POOL_WIDTH)),
                  full((1, ATTN_WIDTH)), full((1, ATTN_WIDTH))],
        out_specs=[tok(d), tok(POOL_WIDTH), tok(POOL_WIDTH), tok(ATTN_WIDTH), tok(ATTN_WIDTH),
                   tok(ATTN_WIDTH), tok(ATTN_WIDTH), tok(ATTN_WIDTH)],
        out_shape=[jax.ShapeDtypeStruct((s, d), MXU_DTYPE),
                   jax.ShapeDtypeStruct((s, POOL_WIDTH), MXU_DTYPE),
                   jax.ShapeDtypeStruct((s, POOL_WIDTH), MXU_DTYPE),
                   jax.ShapeDtypeStruct((s, ATTN_WIDTH), F32),
                   jax.ShapeDtypeStruct((s, ATTN_WIDTH), F32),
                   jax.ShapeDtypeStruct((s, ATTN_WIDTH), F32),
                   jax.ShapeDtypeStruct((s, ATTN_WIDTH), F32),
                   jax.ShapeDtypeStruct((s, ATTN_WIDTH), F32)],
        scratch_shapes=[pltpu.VMEM((tm + POOL_HALO, POOL_WIDTH), F32)],
        compiler_params=_params(("arbitrary",)),
    )(x, g1, win, poolw, pscale, qg, kg)


DEINT = 4
assert [dl for _, dl in DILATED_PATTERNS] == [1, DEINT, DEINT * DEINT]


def _by4_positions(n):
    pos = np.arange(n)
    return DEINT * (pos % (n // DEINT)) + pos // (n // DEINT)


def _masked_bias(b_ref, p, n):
    return b_ref[p, jnp.minimum(n, 1)].reshape(2 * ATT_BLOCK, 2 * ATT_BLOCK)


def _unit_rows(u, dl):
    sq, sk = ATT_SUPER // DEINT, 2 * ATT_SUPER // DEINT
    if dl == 1:
        n = ATT_BLOCK // DEINT
        return (u, [pl.ds(pl.multiple_of(r * sq + n * u, 8), n) for r in range(DEINT)],
                [pl.ds(pl.multiple_of(r * sk + sk // 2 + n * (u - 1), 8), 2 * n) for r in range(DEINT)])
    if dl == DEINT:
        r, b = u % DEINT, u // DEINT
        return (b, [pl.ds(pl.multiple_of(r * sq + ATT_BLOCK * b, 8), ATT_BLOCK)],
                [pl.ds(pl.multiple_of(r * sk + sk // 2 + ATT_BLOCK * (b - 1), 8), 2 * ATT_BLOCK)])
    r, a = u % DEINT, u // DEINT
    return 0, [pl.ds(r * sq + a, ATT_BLOCK, stride=DEINT)], [pl.ds(r * sk + a, 2 * ATT_BLOCK, stride=DEINT)]


def _take(ref, runs):
    parts = [ref[run, :] for run in runs]
    return parts[0] if len(parts) == 1 else jnp.concatenate(parts, axis=0)


def _put(ref, runs, value, add=False):
    n = value.shape[0] // len(runs)
    for i, run in enumerate(runs):
        part = value[i * n:(i + 1) * n]
        ref[run, :] = ref[run, :] + part if add else part


def _deinterleave(dst, src, n):
    seg = n // DEINT
    for r in range(DEINT):
        dst[r * seg:(r + 1) * seg, :] = src[pl.ds(r, seg, stride=DEINT), :]


def _deinterleave_pair(dst, prev, cur):
    seg = prev.shape[0] // DEINT
    for r in range(DEINT):
        dst[2 * r * seg:(2 * r + 1) * seg, :] = prev[pl.ds(r, seg, stride=DEINT), :]
        dst[(2 * r + 1) * seg:(2 * r + 2) * seg, :] = cur[pl.ds(r, seg, stride=DEINT), :]


def _interleave(dst, src, n, offset=0):
    seg = n // DEINT
    stride = src.shape[0] // DEINT
    for r in range(DEINT):
        dst[pl.ds(r, seg, stride=DEINT), :] = src[r * stride + offset:r * stride + offset + seg, :]


def _attn_fwd_call(qn, kn, v, bias):
    s, w = qn.shape
    nsb = s // ATT_SUPER
    npair = w // LANES

    def body(q_ref, kc_ref, kp_ref, vc_ref, vp_ref, b_ref, o_ref, lse_ref, qf, kf, vf, acc_s, m_s, l_s):
        sb = pl.program_id(1)
        _deinterleave(qf, q_ref, ATT_SUPER)
        _deinterleave_pair(kf, kp_ref, kc_ref)
        _deinterleave_pair(vf, vp_ref, vc_ref)
        lo = _low_half()
        for p, (_, dl) in enumerate(DILATED_PATTERNS):
            def unit(u, carry, p=p, dl=dl):
                b, rows_q, rows_k = _unit_rows(u, dl)
                qp = _take(qf, rows_q).astype(MXU_DTYPE)
                kcat = _take(kf, rows_k).astype(MXU_DTYPE)
                vcat = _take(vf, rows_k).astype(MXU_DTYPE)
                zero = jnp.zeros_like(qp)
                q2 = jnp.concatenate([jnp.where(lo, qp, zero), jnp.where(lo, zero, qp)], axis=0)
                sc = _mm_nt(q2, kcat) + _masked_bias(b_ref, p, sb * (ATT_UNITS // dl) + b)
                m2 = jnp.max(sc, axis=-1, keepdims=True)
                pr = jnp.exp(sc - m2)
                l2 = jnp.sum(pr, axis=-1, keepdims=True)
                acc2 = _mm(pr.astype(MXU_DTYPE), vcat)
                acc = jnp.where(lo, acc2[:ATT_BLOCK], acc2[ATT_BLOCK:])
                m = jnp.where(lo, m2[:ATT_BLOCK], m2[ATT_BLOCK:])
                l = jnp.where(lo, l2[:ATT_BLOCK], l2[ATT_BLOCK:])
                if p == 0:
                    _put(acc_s, rows_q, acc)
                    _put(m_s, rows_q, m)
                    _put(l_s, rows_q, l)
                else:
                    m_old = _take(m_s, rows_q)
                    m_new = jnp.maximum(m_old, m)
                    a_old = jnp.exp(m_old - m_new)
                    a_new = jnp.exp(m - m_new)
                    _put(acc_s, rows_q, a_old * _take(acc_s, rows_q) + a_new * acc)
                    _put(l_s, rows_q, a_old * _take(l_s, rows_q) + a_new * l)
                    _put(m_s, rows_q, m_new)
                return carry

            lax.fori_loop(0, ATT_UNITS, unit, 0, unroll=16)
        l = l_s[...]
        acc_s[...] = acc_s[...] / l
        m_s[...] = m_s[...] + jnp.log(l)
        _interleave(o_ref, acc_s, ATT_SUPER)
        _interleave(lse_ref, m_s, ATT_SUPER)

    cur = pl.BlockSpec((ATT_SUPER, LANES), lambda j, t: (t, j))
    prev = pl.BlockSpec((ATT_SUPER, LANES), lambda j, t: (jnp.maximum(t - 1, 0), j))
    bspec = pl.BlockSpec((len(DILATED_PATTERNS), 2, 2, ATT_BLOCK, 2 * ATT_BLOCK), lambda j, t: (0, 0, j, 0, 0))
    return pl.pallas_call(
        body, name="attn_fwd",
        grid=(npair, nsb),
        in_specs=[cur, cur, prev, cur, prev, bspec],
        out_specs=[cur, cur],
        out_shape=[jax.ShapeDtypeStruct((s, w), F32), jax.ShapeDtypeStruct((s, w), F32)],
        scratch_shapes=[pltpu.VMEM((ATT_SUPER, LANES), F32), pltpu.VMEM((2 * ATT_SUPER, LANES), F32),
                        pltpu.VMEM((2 * ATT_SUPER, LANES), F32), pltpu.VMEM((ATT_SUPER, LANES), F32),
                        pltpu.VMEM((ATT_SUPER, LANES), F32), pltpu.VMEM((ATT_SUPER, LANES), F32)],
        compiler_params=_params(("arbitrary", "arbitrary")),
    )(qn, kn, kn, v, v, bias)


def _attn_bwd_call(qn, kn, v, do, lse, delta, bias, dep=None):
    s, w = qn.shape
    nsb = s // ATT_SUPER
    npair = w // LANES
    deps = [] if dep is None else [dep]

    def body(q_ref, kc_ref, kp_ref, vc_ref, vp_ref, do_ref, lse_ref, dlt_ref, b_ref, *rest):
        dq_ref, dk_ref, dv_ref, db_ref, qf, kf, vf, dof, lsef, dltf, dqf, dkf, dvf = rest[len(deps):]
        step = pl.program_id(1)
        sb = nsb - 1 - step
        seg = ATT_SUPER // DEINT
        _deinterleave(qf, q_ref, ATT_SUPER)
        _deinterleave(dof, do_ref, ATT_SUPER)
        _deinterleave_pair(kf, kp_ref, kc_ref)
        _deinterleave_pair(vf, vp_ref, vc_ref)
        _deinterleave(lsef, lse_ref, ATT_SUPER)
        _deinterleave(dltf, dlt_ref, ATT_SUPER)

        @pl.when(step == 0)
        def _():
            db_ref[...] = jnp.zeros(db_ref.shape, F32)

        for acc in (dkf, dvf):
            for r in range(DEINT):
                this, before = pl.ds((2 * r + 1) * seg, seg), pl.ds(2 * r * seg, seg)

                @pl.when(step == 0)
                def _(acc=acc, this=this):
                    acc[this, :] = jnp.zeros((seg, LANES), F32)

                @pl.when(step > 0)
                def _(acc=acc, this=this, before=before):
                    acc[this, :] = acc[before, :]

                acc[before, :] = jnp.zeros((seg, LANES), F32)
        lo = _low_half()
        for p, (_, dl) in enumerate(DILATED_PATTERNS):
            def unit(u, carry, p=p, dl=dl):
                b, rows_q, rows_k = _unit_rows(u, dl)
                qp = _take(qf, rows_q).astype(MXU_DTYPE)
                dop = _take(dof, rows_q).astype(MXU_DTYPE)
                kcat = _take(kf, rows_k).astype(MXU_DTYPE)
                vcat = _take(vf, rows_k).astype(MXU_DTYPE)
                lse2 = _take(lsef, rows_q)
                dlt2 = _take(dltf, rows_q)
                zero = jnp.zeros_like(qp)
                q2 = jnp.concatenate([jnp.where(lo, qp, zero), jnp.where(lo, zero, qp)], axis=0)
                do2 = jnp.concatenate([jnp.where(lo, dop, zero), jnp.where(lo, zero, dop)], axis=0)
                lse_c = jnp.concatenate([lse2[:, 0:1], lse2[:, HEAD_DIM:HEAD_DIM + 1]], axis=0)
                dlt_c = jnp.concatenate([dlt2[:, 0:1], dlt2[:, HEAD_DIM:HEAD_DIM + 1]], axis=0)
                sc = _mm_nt(q2, kcat) + _masked_bias(b_ref, p, sb * (ATT_UNITS // dl) + b)
                pr = jnp.exp(sc - lse_c)
                ds = pr * (_mm_nt(do2, vcat) - dlt_c)
                db_ref[p] += ds.reshape(2, ATT_BLOCK, 2 * ATT_BLOCK)
                ds_c = ds.astype(MXU_DTYPE)
                dq2 = _mm(ds_c, kcat)
                dk = _mm_tn(ds_c, q2)
                dv = _mm_tn(pr.astype(MXU_DTYPE), do2)
                dq = jnp.where(lo, dq2[:ATT_BLOCK], dq2[ATT_BLOCK:])
                _put(dqf, rows_q, dq, add=p > 0)
                _put(dkf, rows_k, dk, add=True)
                _put(dvf, rows_k, dv, add=True)
                return carry

            lax.fori_loop(0, ATT_UNITS, unit, 0, unroll=16)
        _interleave(dq_ref, dqf, ATT_SUPER)
        _interleave(dk_ref, dkf, ATT_SUPER, offset=seg)
        _interleave(dv_ref, dvf, ATT_SUPER, offset=seg)

    cur = pl.BlockSpec((ATT_SUPER, LANES), lambda j, t: (nsb - 1 - t, j))
    prev = pl.BlockSpec((ATT_SUPER, LANES), lambda j, t: (jnp.maximum(nsb - 2 - t, 0), j))
    npat = len(DILATED_PATTERNS)
    bspec = pl.BlockSpec((npat, 2, 2, ATT_BLOCK, 2 * ATT_BLOCK), lambda j, t: (0, 0, j, 0, 0))
    dbspec = pl.BlockSpec((npat, 2, ATT_BLOCK, 2 * ATT_BLOCK), lambda j, t: (0, j, 0, 0))
    sup = lambda: pltpu.VMEM((ATT_SUPER, LANES), F32)
    sup2 = lambda: pltpu.VMEM((2 * ATT_SUPER, LANES), F32)
    return pl.pallas_call(
        body, name="attn_bwd",
        grid=(npair, nsb),
        in_specs=[cur, cur, prev, cur, prev, cur, cur, cur, bspec] + [ANY] * len(deps),
        out_specs=[cur, cur, cur, dbspec],
        out_shape=[jax.ShapeDtypeStruct((s, w), F32)] * 3
        + [jax.ShapeDtypeStruct((npat, N_HEADS, ATT_BLOCK, 2 * ATT_BLOCK), F32)],
        scratch_shapes=[sup(), sup2(), sup2(), sup(), sup(), sup(), sup(), sup2(), sup2()],
        compiler_params=_params(("arbitrary", "arbitrary")),
    )(qn, kn, kn, v, v, do, lse, delta, bias, *deps)


def _bias_table_work(rel_bias):
    buckets = jnp.asarray(_bucket_tables())
    prev_keys = jnp.asarray(_previous_block_keys().astype(np.int32))
    npat = buckets.shape[0]

    def body(rb_ref, bk_ref, pk_ref, out_ref):
        for p in range(npat):
            for half in range(2):
                ks = slice(half * ATT_BLOCK, (half + 1) * ATT_BLOCK)
                bk = bk_ref[p, :, ks]
                absent = pk_ref[p, :, ks] != 0
                for h in range(N_HEADS):
                    def pick(b, acc, h=h, bk=bk):
                        return jnp.where(bk == b, rb_ref[b, h], acc)

                    tab = lax.fori_loop(0, N_BUCKETS, pick, jnp.full((ATT_BLOCK, ATT_BLOCK), NEG_INF, F32))
                    out_ref[p, 1, h, :, ks] = tab
                    out_ref[p, 0, h, :, ks] = jnp.where(absent, NEG_INF, tab)

    vmem = pl.BlockSpec(memory_space=pltpu.VMEM)
    return ([rel_bias, buckets, prev_keys], [pl.BlockSpec(memory_space=pltpu.SMEM), vmem, vmem],
            jax.ShapeDtypeStruct((npat, 2, N_HEADS, ATT_BLOCK, 2 * ATT_BLOCK), F32), body)


def _bias_table_call(rel_bias):
    operands, specs, shape, body = _bias_table_work(rel_bias)
    return pl.pallas_call(body, name="bias_table", in_specs=specs, out_shape=shape, compiler_params=_params())(*operands)


def _rel_bias_grad_call(dbias, buckets):
    npat, nh = dbias.shape[0], dbias.shape[1]

    def body(db_ref, bk_ref, out_ref):
        lane = lax.broadcasted_iota(jnp.int32, (nh, LANES), 1)
        out = jnp.zeros((nh, LANES), F32)
        for b in range(N_BUCKETS):
            tot = jnp.zeros((nh, 1), F32)
            for p in range(npat):
                hit = jnp.where(bk_ref[p][None] == b, db_ref[p], 0.0)
                tot = tot + jnp.sum(jnp.sum(hit, axis=1), axis=-1, keepdims=True)
            out = jnp.where(lane == b, tot, out)
        out_ref[...] = out

    return pl.pallas_call(
        body, name="rel_bias_grad",
        out_shape=jax.ShapeDtypeStruct((nh, LANES), F32),
        compiler_params=_params(),
    )(dbias, buckets)


def _f2_call(x, tgt, ypool, o, wout, wup, wdown, g2, tm):
    s, d = x.shape
    nblk = s // tm
    nch, _, fch = wup.shape
    dff = nch * fch
    mixw = POOL_WIDTH + ATTN_WIDTH

    def body(x_ref, t_ref, yp_ref, o_ref, g2_ref, wout_hbm, wup_hbm, wdown_hbm,
             mixed_ref, c_ref, ff_ref, dz_ref, dy_ref, dh1_ref, dyp_ref, do_ref, dlt_ref, dg2_ref, loss_ref,
             wout_v, wup_v, wdown_v, rz, wsem):
        i = pl.program_id(0)

        def weight_copies():
            cps = [pltpu.make_async_copy(wout_hbm, wout_v, wsem.at[0])]
            for j in range(nch):
                cps.append(pltpu.make_async_copy(wup_hbm.at[j], wup_v.at[j], wsem.at[1 + 2 * j]))
                cps.append(pltpu.make_async_copy(wdown_hbm.at[j], wdown_v.at[j], wsem.at[2 + 2 * j]))
            return cps

        def arrived(k):
            @pl.when(i == 0)
            def _():
                weight_copies()[k].wait()

        @pl.when(i == 0)
        def _():
            for cp in weight_copies():
                cp.start()
            dg2_ref[...] = jnp.zeros(dg2_ref.shape, F32)
            loss_ref[...] = jnp.zeros(loss_ref.shape, F32)

        o = o_ref[...]
        mixed = jnp.concatenate([yp_ref[...], o.astype(MXU_DTYPE)], axis=-1)
        mixed_ref[...] = mixed
        arrived(0)
        h1 = x_ref[...] + _mm(mixed, wout_v[...])
        r2 = lax.rsqrt(jnp.mean(h1 * h1, axis=-1, keepdims=True) + NORM_EPS)
        hn = h1 * r2
        c = (hn * g2_ref[...]).astype(MXU_DTYPE)
        c_ref[...] = c
        y = h1
        for j in range(nch):
            cs = slice(j * fch, (j + 1) * fch)
            arrived(1 + 2 * j)
            z = jnp.maximum(_mm(c, wup_v[j]), 0.0)
            rz[:, cs] = z
            ff = (z * z).astype(MXU_DTYPE)
            ff_ref[:, cs] = ff
            arrived(2 + 2 * j)
            y = y + _mm(ff, wdown_v[j])
        err = y - t_ref[...]
        loss_ref[...] += jnp.sum(err * err) * (0.5 / d)
        dy = err * (1.0 / d)
        dy_c = dy.astype(MXU_DTYPE)
        dy_ref[...] = dy_c
        dc = jnp.zeros((tm, d), F32)
        for j in range(nch):
            cs = slice(j * fch, (j + 1) * fch)
            dz = (_mm_nt(dy_c, wdown_v[j]) * (2.0 * rz[:, cs])).astype(MXU_DTYPE)
            dz_ref[:, cs] = dz
            dc = dc + _mm_nt(dz, wup_v[j])
        dg2_ref[...] += jnp.sum(dc * hn, axis=0, keepdims=True)
        dh1 = dy + _rms_bwd(dc * g2_ref[...], hn, r2)
        dh1_ref[...] = dh1
        dmix = _mm_nt(dh1.astype(MXU_DTYPE), wout_v[...])
        dyp_ref[...] = dmix[:, :POOL_WIDTH]
        do = dmix[:, POOL_WIDTH:]
        do_ref[...] = do
        dlt_ref[...] = _head_sum_bcast(do * o)

    tok = lambda w: pl.BlockSpec((tm, w), lambda i: (i, 0))
    const = lambda shp: pl.BlockSpec(shp, lambda i: (0,) * len(shp))
    return pl.pallas_call(
        body, name="fwd_mlp_bwd_mlp",
        grid=(nblk,),
        in_specs=[tok(d), tok(d), tok(POOL_WIDTH), tok(ATTN_WIDTH), const((1, d)), ANY, ANY, ANY],
        out_specs=[tok(mixw), tok(d), tok(dff), tok(dff), tok(d), tok(d), tok(POOL_WIDTH), tok(ATTN_WIDTH),
                   tok(ATTN_WIDTH), const((1, d)), const((1, LANES))],
        out_shape=[jax.ShapeDtypeStruct((s, mixw), MXU_DTYPE),
                   jax.ShapeDtypeStruct((s, d), MXU_DTYPE),
                   jax.ShapeDtypeStruct((s, dff), MXU_DTYPE),
                   jax.ShapeDtypeStruct((s, dff), MXU_DTYPE),
                   jax.ShapeDtypeStruct((s, d), MXU_DTYPE),
                   jax.ShapeDtypeStruct((s, d), F32),
                   jax.ShapeDtypeStruct((s, POOL_WIDTH), F32),
                   jax.ShapeDtypeStruct((s, ATTN_WIDTH), F32),
                   jax.ShapeDtypeStruct((s, ATTN_WIDTH), F32),
                   jax.ShapeDtypeStruct((1, d), F32),
                   jax.ShapeDtypeStruct((1, LANES), F32)],
        scratch_shapes=[pltpu.VMEM(wout.shape, MXU_DTYPE), pltpu.VMEM(wup.shape, MXU_DTYPE),
                        pltpu.VMEM(wdown.shape, MXU_DTYPE), pltpu.VMEM((tm, dff), F32),
                        pltpu.SemaphoreType.DMA((1 + 2 * nch,))],
        compiler_params=_params(("arbitrary",)),
    )(x, tgt, ypool, o, g2, wout, wup, wdown)


def _bproj_call(dqn, dkn, dv, q32, k32, dypool, pooled, x, dh1, win, poolw, pscale, qg, kg, g1, tm):
    s, d = x.shape
    nblk = s // tm
    ngrp = len(POOL_WINDOWS)

    def body(dqn_ref, dkn_ref, dv_ref, q_ref, k_ref, dyp_ref, pooled_ref, x_ref, dh1_ref,
             win_hbm, pw_ref, ps_ref, qg_ref, kg_ref, g1_ref,
             dx_ref, dproj_ref, dg1_ref, dqg_ref, dkg_ref, dpw_ref, dps_ref, win_v, ebuf, wsem):
        step = pl.program_id(0)
        i = nblk - 1 - step

        @pl.when(step == 0)
        def _():
            pltpu.make_async_copy(win_hbm, win_v, wsem).start()
            dg1_ref[...] = jnp.zeros(dg1_ref.shape, F32)
            dqg_ref[...] = jnp.zeros(dqg_ref.shape, F32)
            dkg_ref[...] = jnp.zeros(dkg_ref.shape, F32)
            dpw_ref[...] = jnp.zeros(dpw_ref.shape, F32)
            dps_ref[...] = jnp.zeros(dps_ref.shape, F32)
            ebuf[tm:tm + POOL_HALO, :] = jnp.zeros((POOL_HALO, POOL_WIDTH), F32)

        @pl.when(step > 0)
        def _():
            ebuf[tm:tm + POOL_HALO, :] = ebuf[0:POOL_HALO, :]

        def qk_bwd(dn_sum, raw, gain, scale, dgain_ref):
            rr = lax.rsqrt(_head_sum_bcast(raw * raw) * (1.0 / HEAD_DIM) + NORM_EPS)
            hn = raw * rr
            dgain_ref[...] += jnp.sum(dn_sum * hn, axis=0, keepdims=True) * scale
            dn = dn_sum * (gain * scale)
            return rr * (dn - hn * (_head_sum_bcast(dn * hn) * (1.0 / HEAD_DIM)))

        dq = qk_bwd(dqn_ref[...], q_ref[...], qg_ref[...], HEAD_DIM ** -0.5, dqg_ref)
        dk = qk_bwd(dkn_ref[...], k_ref[...], kg_ref[...], 1.0, dkg_ref)

        t = i * tm + lax.broadcasted_iota(jnp.int32, (tm, 1), 0)
        dpooled = []
        for g, w in enumerate(POOL_WINDOWS):
            ls = slice(g * LANES, (g + 1) * LANES)
            dm = dyp_ref[:, ls]
            pg = pooled_ref[:, ls]
            dps_ref[:, ls] += jnp.sum(dm * _mm(pg, pw_ref[g]), axis=0, keepdims=True)
            dms = (dm * ps_ref[:, ls]).astype(MXU_DTYPE)
            dpw_ref[g] += _mm_tn(pg, dms)
            dpg = _mm_nt(dms, pw_ref[g])
            dpooled.append(dpg)
            ebuf[0:tm, ls] = dpg / jnp.minimum(t + 1, w).astype(F32)
        du = []
        for g, w in enumerate(POOL_WINDOWS):
            ls = slice(g * LANES, (g + 1) * LANES)
            acc = ebuf[0:tm, ls]
            for sh in range(1, w):
                acc = acc + ebuf[sh:sh + tm, ls]
            du.append(acc - dpooled[g])
        parts = [jnp.concatenate(du, axis=-1), dq, dk, dv_ref[...]]
        @pl.when(step == 0)
        def _():
            pltpu.make_async_copy(win_hbm, win_v, wsem).wait()

        da = jnp.zeros((tm, d), F32)
        for p, part in enumerate(parts):
            pc = part.astype(MXU_DTYPE)
            dproj_ref[:, p * POOL_WIDTH:(p + 1) * POOL_WIDTH] = pc
            da = da + _mm_nt(pc, win_v[p])
        xv = x_ref[...]
        r = lax.rsqrt(jnp.mean(xv * xv, axis=-1, keepdims=True) + NORM_EPS)
        xn = xv * r
        dg1_ref[...] += jnp.sum(da * xn, axis=0, keepdims=True)
        dx_ref[...] = dh1_ref[...] + _rms_bwd(da * g1_ref[...], xn, r)

    tok = lambda w: pl.BlockSpec((tm, w), lambda t: (nblk - 1 - t, 0))
    const = lambda shp: pl.BlockSpec(shp, lambda t: (0,) * len(shp))
    return pl.pallas_call(
        body, name="bwd_inproj",
        grid=(nblk,),
        in_specs=[tok(ATTN_WIDTH)] * 5 + [tok(POOL_WIDTH), tok(POOL_WIDTH), tok(d), tok(d),
                                          ANY, const(poolw.shape), const((1, POOL_WIDTH)), const((1, ATTN_WIDTH)),
                                          const((1, ATTN_WIDTH)), const((1, d))],
        out_specs=[tok(d), tok(4 * POOL_WIDTH), const((1, d)), const((1, ATTN_WIDTH)), const((1, ATTN_WIDTH)),
                   const((ngrp, LANES, LANES)), const((1, POOL_WIDTH))],
        out_shape=[jax.ShapeDtypeStruct((s, d), F32),
                   jax.ShapeDtypeStruct((s, 4 * POOL_WIDTH), MXU_DTYPE),
                   jax.ShapeDtypeStruct((1, d), F32),
                   jax.ShapeDtypeStruct((1, ATTN_WIDTH), F32),
                   jax.ShapeDtypeStruct((1, ATTN_WIDTH), F32),
                   jax.ShapeDtypeStruct((ngrp, LANES, LANES), F32),
                   jax.ShapeDtypeStruct((1, POOL_WIDTH), F32)],
        scratch_shapes=[pltpu.VMEM(win.shape, MXU_DTYPE), pltpu.VMEM((tm + POOL_HALO, POOL_WIDTH), F32),
                        pltpu.SemaphoreType.DMA],
        compiler_params=_params(("arbitrary",)),
    )(dqn, dkn, dv, q32, k32, dypool, pooled, x, dh1, win, poolw, pscale, qg, kg, g1)


def _wgrad_call(a, b, bm, bn, bk, out_shape, out_block, out_index, name):
    s, m = a.shape
    _, n = b.shape
    nk = s // bk

    def body(a_ref, b_ref, o_ref, wire_ref):
        k = pl.program_id(2)

        @pl.when(k == 0)
        def _():
            o_ref[...] = jnp.zeros(o_ref.shape, F32)

        o_ref[...] += _mm_tn(a_ref[...].astype(MXU_DTYPE), b_ref[...].astype(MXU_DTYPE))

        @pl.when(k == nk - 1)
        def _():
            wire_ref[...] = o_ref[...].astype(WIRE_DTYPE)

    return pl.pallas_call(
        body, name=name,
        grid=(m // bm, n // bn, nk),
        in_specs=[pl.BlockSpec((bk, bm), lambda i, j, k: (k, i)), pl.BlockSpec((bk, bn), lambda i, j, k: (k, j))],
        out_specs=[pl.BlockSpec(out_block, out_index)] * 2,
        out_shape=[jax.ShapeDtypeStruct(out_shape, F32), jax.ShapeDtypeStruct(out_shape, WIRE_DTYPE)],
        compiler_params=_params(("arbitrary", "arbitrary", "arbitrary")),
    )(a, b)


def _local_grads(x, tgt, g1, win, poolw, pscale, qg, kg, rel_bias, g2, mlp_weights, on_mlp_grads=None, bias=None):
    s, d = x.shape
    g1r, g2r = g1.reshape(1, d), g2.reshape(1, d)
    psr = pscale.reshape(1, POOL_WIDTH)
    qgr = jnp.tile(qg, N_HEADS).reshape(1, ATTN_WIDTH)
    kgr = jnp.tile(kg, N_HEADS).reshape(1, ATTN_WIDTH)
    pw_c = poolw.astype(MXU_DTYPE)
    buckets = jnp.asarray(_bucket_tables())
    bias = _bias_table_call(rel_bias) if bias is None else bias
    bk = min(s, 4096)

    a, pooled, ypool, q32, k32, qn, kn, v = _f1_call(x, g1r, win, pw_c, psr, qgr, kgr, tm=512)
    o, lse = _attn_fwd_call(qn, kn, v, bias)
    wout, wup, wdown = mlp_weights(o)
    mixed, c, ff, dz, dy, dh1, dypool, do, delta, dg2, loss = _f2_call(x, tgt, ypool, o, wout, wup, wdown, g2r, tm=256)
    dff = ff.shape[1]
    g_out = [g.reshape(N_CHIPS, d // N_CHIPS, d)
             for g in _wgrad_call(mixed, dh1, d, d, bk // 4, (d, d), (d, d), lambda i, j, k: (0, 0), "wgrad_out")]
    g_up = _wgrad_call(c, dz, d, dff // N_CHIPS, bk, (N_CHIPS, d, dff // N_CHIPS), (None, d, dff // N_CHIPS),
                       lambda i, j, k: (j, 0, 0), "wgrad_up")
    g_down = _wgrad_call(ff, dy, dff // N_CHIPS, d, bk, (N_CHIPS, dff // N_CHIPS, d), (None, dff // N_CHIPS, d),
                         lambda i, j, k: (i, 0, 0), "wgrad_down")
    dep = None if on_mlp_grads is None else on_mlp_grads(g_out[1], g_up[1], g_down[1])
    dqn, dkn, dv, dbias = _attn_bwd_call(qn, kn, v, do, lse, delta, bias, dep)
    dx, dproj, dg1, dqg, dkg, dpw, dps = _bproj_call(
        dqn, dkn, dv, q32, k32, dypool, pooled, x, dh1, win, pw_c, psr, qgr, kgr, g1r, tm=512)
    nin = dproj.shape[1] // N_CHIPS
    g_in = _wgrad_call(a, dproj, d, nin, bk, (N_CHIPS, d, nin), (None, d, nin), lambda i, j, k: (j, 0, 0), "wgrad_in")
    drb = _rel_bias_grad_call(dbias, buckets)
    small = dict(
        mix_norm_g=dg1.reshape(d), mlp_norm_g=dg2.reshape(d), pool_scale=dps.reshape(POOL_WIDTH),
        q_norm_g=dqg.reshape(ATTN_WIDTH), k_norm_g=dkg.reshape(ATTN_WIDTH),
        rel_bias=drb[:, :N_BUCKETS].T, pool_w=dpw)
    return loss[0, 0], dx, (g_in, g_out, g_up, g_down), small


def _coords():
    return lax.axis_index("x"), lax.axis_index("y"), lax.axis_index("c")


def _other_chips(x, y):
    return [(1 - x, y), (x, 1 - y), (1 - x, 1 - y)]


def _remote(src, dst, send_sem, recv_sem, dev):
    return pltpu.make_async_remote_copy(src_ref=src, dst_ref=dst, send_sem=send_sem, recv_sem=recv_sem,
                                        device_id=dev, device_id_type=MESH)


def _halves(a):
    return a.reshape(a.shape[:-2] + (2, a.shape[-2] // 2, a.shape[-1]))


def _place_shards_call(shards, chip_idx, nch):
    nw = len(shards)

    def body(chip_ref, *refs):
        for w in range(nw):
            refs[nw + w][...] = refs[w][...].astype(WIRE_DTYPE)

    in_specs = [pl.BlockSpec((s.shape[0] // nch, s.shape[1]), lambda i, chip_ref: (i, 0)) for s in shards]
    out_specs = [pl.BlockSpec((None, s.shape[0] // nch, s.shape[1]), lambda i, chip_ref: (chip_ref[0], i, 0))
                 for s in shards]
    return pl.pallas_call(
        body, name="weights_place",
        grid_spec=pltpu.PrefetchScalarGridSpec(num_scalar_prefetch=1, grid=(nch,),
                                               in_specs=in_specs, out_specs=out_specs),
        out_shape=[jax.ShapeDtypeStruct((N_CHIPS,) + s.shape, WIRE_DTYPE) for s in shards],
        compiler_params=_params(("arbitrary",)),
    )(chip_idx, *shards)


def _allgather_call(placed, from_chips, name, meanwhile=None):
    nw = len(placed)
    ncp = 3 * nw
    extra, extra_specs, extra_shape, extra_body = meanwhile if meanwhile else ([], [], None, None)
    ne = len(extra)

    def body(*refs):
        outs = refs[nw + ne:2 * nw + ne]
        send1, recv1, send2, recv2 = refs[-4:]
        x, y, c = _coords()
        chip = 2 * x + y
        others = _other_chips(x, y)
        first, passed = [], []
        if from_chips:
            for w in range(nw):
                for k, (ox, oy) in enumerate(others):
                    mine = outs[w].at[chip, c]
                    cp = _remote(mine, mine, send1.at[3 * w + k], recv1.at[3 * w + k], (ox, oy, c))
                    cp.start()
                    first.append(cp)
        if meanwhile:
            extra_body(*refs[nw:nw + ne], refs[2 * nw + ne])
        for w in range(nw):
            for k, (ox, oy) in enumerate(others):
                piece = outs[w].at[2 * ox + oy, c]
                if from_chips:
                    _remote(piece, piece, send1.at[3 * w + k], recv1.at[3 * w + k], (ox, oy, c)).wait_recv()
                cp = _remote(piece, piece, send2.at[3 * w + k], recv2.at[3 * w + k], (x, y, 1 - c))
                cp.start()
                passed.append(cp)
        for w in range(nw):
            for k, (ox, oy) in enumerate(others):
                piece = outs[w].at[2 * ox + oy, 1 - c]
                _remote(piece, piece, send2.at[3 * w + k], recv2.at[3 * w + k], (x, y, 1 - c)).wait_recv()
        for cp in first + passed:
            cp.wait_send()

    return pl.pallas_call(
        body, name=name,
        in_specs=[ANY] * nw + list(extra_specs),
        out_specs=[ANY] * nw + ([pl.BlockSpec(memory_space=pltpu.VMEM)] if meanwhile else []),
        out_shape=[jax.ShapeDtypeStruct(s.shape, s.dtype) for s in placed] + ([extra_shape] if meanwhile else []),
        input_output_aliases={w: w for w in range(nw)},
        scratch_shapes=[pltpu.SemaphoreType.DMA((ncp,))] * 4,
        compiler_params=_params(),
    )(*placed, *extra)


HBM_SPEC = pl.BlockSpec(memory_space=pltpu.HBM)
SEM_SPEC = pl.BlockSpec(memory_space=pltpu.SEMAPHORE)
SPLIT_EFFECT = pltpu.SideEffectType.DATAFLOW_SIDE_EFFECTING


def _in_hbm(a):
    return pltpu.with_memory_space_constraint(a, pltpu.HBM)


def _gather_copies(bufs, send, recv):
    x, y, c = _coords()
    chip = 2 * x + y
    cps = []
    for w, buf in enumerate(bufs):
        for k, (ox, oy) in enumerate(_other_chips(x, y)):
            mine, theirs = buf.at[chip, c], buf.at[2 * ox + oy, c]
            sems = (send.at[3 * w + k], recv.at[3 * w + k], (ox, oy, c))
            cps.append((_remote(mine, mine, *sems), _remote(theirs, theirs, *sems)))
    return cps


def _gather_start_call(bufs, after):
    nw = len(bufs)

    def body(*refs):
        ins, send, recv, token = refs[:nw], refs[nw + 1], refs[nw + 2], refs[2 * nw + 3]
        for out, _ in _gather_copies(ins, send, recv):
            out.start()
        token[...] = jnp.zeros(token.shape, F32)

    res = pl.pallas_call(
        body, name="weights_gather_start",
        in_specs=[HBM_SPEC] * nw + [ANY],
        out_specs=[SEM_SPEC, SEM_SPEC] + [HBM_SPEC] * nw + [pl.BlockSpec(memory_space=pltpu.VMEM)],
        out_shape=[pltpu.SemaphoreType.DMA((3 * nw,)), pltpu.SemaphoreType.DMA((3 * nw,))]
        + [pltpu.HBM(b.shape, b.dtype) for b in bufs] + [jax.ShapeDtypeStruct((8, LANES), F32)],
        input_output_aliases={w: 2 + w for w in range(nw)},
        compiler_params=pltpu.CompilerParams(has_side_effects=SPLIT_EFFECT),
    )(*[_in_hbm(b) for b in bufs], after)
    return res[0], res[1], list(res[2:2 + nw]), res[2 + nw]


def _gather_wait_call(bufs, send, recv, after):
    nw = len(bufs)

    def body(*refs):
        ins, send, recv = refs[:nw], refs[nw], refs[nw + 1]
        for out, back in _gather_copies(ins, send, recv):
            out.wait_send()
            back.wait_recv()

    return pl.pallas_call(
        body, name="weights_gather_wait",
        in_specs=[HBM_SPEC] * nw + [SEM_SPEC, SEM_SPEC, ANY],
        out_specs=[HBM_SPEC] * nw,
        out_shape=[pltpu.HBM(b.shape, b.dtype) for b in bufs],
        input_output_aliases={w: w for w in range(nw)},
        compiler_params=pltpu.CompilerParams(has_side_effects=SPLIT_EFFECT),
    )(*bufs, send, recv, after)


def _scatter_copies(srcs, lands, send, recv, wholes):
    x, y, c = _coords()
    me = 4 * x + 2 * y + c
    cps = []
    for w, (src, land) in enumerate(zip(srcs, lands)):
        for r in range(1, N_DEV):
            px, py, pc = ((1 - x) if r & 4 else x, (1 - y) if r & 2 else y, (1 - c) if r & 1 else c)
            sems = (send.at[(N_DEV - 1) * w + r - 1], recv.at[(N_DEV - 1) * w + r - 1], (px, py, pc))
            piece = src if wholes[w] else src.at[2 * px + py, pc]
            cps.append((_remote(piece, land.at[me], *sems), _remote(piece, land.at[4 * px + 2 * py + pc], *sems)))
    return cps


def _scatter_start_call(srcs, lands, wholes, name):
    nw = len(srcs)
    ncp = (N_DEV - 1) * nw

    def body(*refs):
        ins, lnd, send, recv, token = refs[:nw], refs[nw:2 * nw], refs[2 * nw], refs[2 * nw + 1], refs[4 * nw + 2]
        for out, _ in _scatter_copies(ins, lnd, send, recv, wholes):
            out.start()
        token[...] = jnp.zeros(token.shape, F32)

    res = pl.pallas_call(
        body, name=name,
        in_specs=[HBM_SPEC] * (2 * nw),
        out_specs=[SEM_SPEC, SEM_SPEC] + [HBM_SPEC] * (2 * nw) + [pl.BlockSpec(memory_space=pltpu.VMEM)],
        out_shape=[pltpu.SemaphoreType.DMA((ncp,)), pltpu.SemaphoreType.DMA((ncp,))]
        + [pltpu.HBM(b.shape, b.dtype) for b in list(srcs) + list(lands)] + [jax.ShapeDtypeStruct((8, LANES), F32)],
        input_output_aliases={i: 2 + i for i in range(2 * nw)},
        compiler_params=pltpu.CompilerParams(has_side_effects=SPLIT_EFFECT),
    )(*[_in_hbm(b) for b in list(srcs) + list(lands)])
    return res[0], res[1], list(res[2:2 + nw]), list(res[2 + nw:2 + 2 * nw]), res[2 + 2 * nw]


def _scatter_wait_call(srcs, lands, send, recv, after, wholes, name):
    nw = len(srcs)

    def body(*refs):
        ins, lnd, send, recv = refs[:nw], refs[nw:2 * nw], refs[2 * nw], refs[2 * nw + 1]
        for out, back in _scatter_copies(ins, lnd, send, recv, wholes):
            out.wait_send()
            back.wait_recv()

    res = pl.pallas_call(
        body, name=name,
        in_specs=[HBM_SPEC] * (2 * nw) + [SEM_SPEC, SEM_SPEC, ANY],
        out_specs=[HBM_SPEC] * (2 * nw),
        out_shape=[pltpu.HBM(b.shape, b.dtype) for b in list(srcs) + list(lands)],
        input_output_aliases={i: i for i in range(2 * nw)},
        compiler_params=pltpu.CompilerParams(has_side_effects=SPLIT_EFFECT),
    )(*srcs, *lands, send, recv, after)
    return list(res[nw:])


def _reduce_call(own, lands, idx, nch, name, dep=None):
    nw = len(own)
    deps = [] if dep is None else [dep]

    def body(idx_ref, *refs):
        refs = refs[:2 * nw] + refs[2 * nw + len(deps):]
        for w in range(nw):
            tot = refs[w][...]
            for r in range(1, N_DEV):
                tot = tot + refs[nw + w][idx_ref[1 + r]].astype(F32)
            refs[2 * nw + w][...] = tot

    in_specs, out_specs, out_shape = [], [], []
    for s in own:
        in_specs.append(pl.BlockSpec((None, None, s.shape[2] // nch, s.shape[3]),
                                     lambda i, idx_ref: (idx_ref[0], idx_ref[1], i, 0)))
    for s in own:
        in_specs.append(pl.BlockSpec((N_DEV, s.shape[2] // nch, s.shape[3]), lambda i, idx_ref: (0, i, 0)))
    for s in own:
        out_specs.append(pl.BlockSpec((None, s.shape[2] // nch, s.shape[3]), lambda i, idx_ref: (idx_ref[1], i, 0)))
        out_shape.append(jax.ShapeDtypeStruct((2,) + s.shape[2:], F32))
    return pl.pallas_call(
        body, name=name,
        grid_spec=pltpu.PrefetchScalarGridSpec(num_scalar_prefetch=1, grid=(nch,),
                                               in_specs=in_specs + [ANY] * len(deps), out_specs=out_specs),
        out_shape=out_shape,
        compiler_params=_params(("arbitrary",)),
    )(idx, *own, *lands, *deps)


def _pair_allgather_call(halves, name):
    nw = len(halves)

    def body(*refs):
        outs = refs[nw:2 * nw]
        send, recv = refs[2 * nw:]
        x, y, c = _coords()
        cps = []
        for w in range(nw):
            cp = _remote(outs[w].at[c], outs[w].at[c], send.at[w], recv.at[w], (x, y, 1 - c))
            cp.start()
            cps.append(cp)
        for w in range(nw):
            theirs = outs[w].at[1 - c]
            _remote(theirs, theirs, send.at[w], recv.at[w], (x, y, 1 - c)).wait_recv()
        for cp in cps:
            cp.wait_send()

    outs = pl.pallas_call(
        body, name=name,
        in_specs=[ANY] * nw, out_specs=[ANY] * nw,
        out_shape=[jax.ShapeDtypeStruct(h.shape, h.dtype) for h in halves],
        input_output_aliases={w: w for w in range(nw)},
        scratch_shapes=[pltpu.SemaphoreType.DMA((nw,))] * 2,
    )(*halves)
    return [o.reshape(2 * h.shape[1], h.shape[2]) for o, h in zip(outs, halves)]


def _adamw(w, g, m, v):
    m = ADAM_B1 * m + (1.0 - ADAM_B1) * g
    v = ADAM_B2 * v + (1.0 - ADAM_B2) * (g * g)
    m_hat = m / (1.0 - ADAM_B1 ** ADAM_STEP)
    v_hat = v / (1.0 - ADAM_B2 ** ADAM_STEP)
    delta = -ADAM_LR * (m_hat / (jnp.sqrt(v_hat) + ADAM_EPS) + ADAM_WD * w)
    return delta, m, v


def _adamw_call(ws, gs, ms, vs, nch, name):
    nw = len(ws)

    def body(*refs):
        for w in range(nw):
            g = refs[nw + w][...]
            delta, m, v = _adamw(refs[w][...], g, refs[2 * nw + w][...], refs[3 * nw + w][...])
            refs[4 * nw + w][...] = g
            refs[5 * nw + w][...] = delta
            refs[6 * nw + w][...] = m
            refs[7 * nw + w][...] = v

    specs = [pl.BlockSpec((a.shape[0] // nch, a.shape[1]), lambda i: (i, 0)) for a in ws]
    res = pl.pallas_call(
        body, name=name,
        grid=(nch,),
        in_specs=specs * 4, out_specs=specs * 4,
        out_shape=[jax.ShapeDtypeStruct(a.shape, F32) for a in ws] * 4,
        compiler_params=_params(("arbitrary",)),
    )(*ws, *gs, *ms, *vs)
    return res[:nw], res[nw:2 * nw], res[2 * nw:3 * nw], res[3 * nw:]


def _small_call(gathered, own, me_idx, w, m, v):
    def fold(row):
        tot = row[:, 0:LANES] + row[:, LANES:2 * LANES] + row[:, 2 * LANES:3 * LANES] + row[:, 3 * LANES:4 * LANES]
        return tot + pltpu.roll(tot, HEAD_DIM, axis=1)

    def body(me_ref, ga_ref, own_ref, w_ref, m_ref, v_ref, g_out, d_out, m_out, v_out):
        me = me_ref[0]
        term = lambda i: jnp.where(me == i, own_ref[...], ga_ref[i])
        g = term(0)
        for i in range(1, N_DEV):
            g = g + term(i)
        unfolded = g[4:5, :]
        folded = jnp.concatenate([fold(unfolded[:, :ATTN_WIDTH]), fold(unfolded[:, ATTN_WIDTH:]),
                                  jnp.zeros((1, 1024 - 2 * LANES), F32)], axis=-1)
        row = lax.broadcasted_iota(jnp.int32, g.shape, 0)
        g = jnp.where(row == 3, folded, g)
        delta, mm, vv = _adamw(w_ref[...], g, m_ref[...], v_ref[...])
        g_out[...] = g
        d_out[...] = delta
        m_out[...] = mm
        v_out[...] = vv

    vmem = pl.BlockSpec(memory_space=pltpu.VMEM)
    return pl.pallas_call(
        body, name="adamw_small",
        in_specs=[pl.BlockSpec(memory_space=pltpu.SMEM)] + [vmem] * 5,
        out_shape=[jax.ShapeDtypeStruct(w.shape, F32)] * 4,
        compiler_params=_params(),
    )(me_idx, gathered, own, w, m, v)


def _pack_small(p, folded=True, loss=None):
    z = lambda n: jnp.zeros((n,), F32)
    rows = [p["mix_norm_g"], p["mlp_norm_g"],
            jnp.concatenate([p["pool_scale"], p["rel_bias"].reshape(-1), z(1024 - POOL_WIDTH - N_BUCKETS * N_HEADS)])]
    if folded:
        rows += [jnp.concatenate([p["q_norm_g"], z(LANES - HEAD_DIM), p["k_norm_g"], z(1024 - LANES - HEAD_DIM)]), z(1024)]
    else:
        rows += [z(1024), jnp.concatenate([p["q_norm_g"], p["k_norm_g"]])]
    rows += [z(1024) if loss is None else jnp.concatenate([loss.reshape(1), z(1023)])]
    head = jnp.stack(rows + [z(1024)] * 2)
    return jnp.concatenate([head, p["pool_w"].reshape(-1, 1024)], axis=0)


def _unpack_small(a):
    return dict(
        mix_norm_g=a[0], mlp_norm_g=a[1], pool_scale=a[2, :POOL_WIDTH],
        rel_bias=a[2, POOL_WIDTH:POOL_WIDTH + N_BUCKETS * N_HEADS].reshape(N_BUCKETS, N_HEADS),
        q_norm_g=a[3, :HEAD_DIM], k_norm_g=a[3, LANES:LANES + HEAD_DIM],
        pool_w=a[8:].reshape(len(POOL_WINDOWS), LANES, LANES))


_WEIGHT_ORDER = ("mix_norm_g", "w_in", "pool_w", "pool_scale", "q_norm_g", "k_norm_g", "rel_bias", "w_out",
                 "mlp_norm_g", "w_up", "w_down")
_BIG = ("w_in", "w_out", "w_up", "w_down")


def kernel(x, mix_norm_g, w_in, pool_w, pool_scale, q_norm_g, k_norm_g, rel_bias, w_out, mlp_norm_g, w_up, w_down, loss_target, m_mix_norm_g, m_w_in, m_pool_w, m_pool_scale, m_q_norm_g, m_k_norm_g, m_rel_bias, m_w_out, m_mlp_norm_g, m_w_up, m_w_down, v_mix_norm_g, v_w_in, v_pool_w, v_pool_scale, v_q_norm_g, v_k_norm_g, v_rel_bias, v_w_out, v_mlp_norm_g, v_w_up, v_w_down):
    w = dict(mix_norm_g=mix_norm_g, w_in=w_in, pool_w=pool_w, pool_scale=pool_scale, q_norm_g=q_norm_g,
             k_norm_g=k_norm_g, rel_bias=rel_bias, w_out=w_out, mlp_norm_g=mlp_norm_g, w_up=w_up, w_down=w_down)
    m = dict(mix_norm_g=m_mix_norm_g, w_in=m_w_in, pool_w=m_pool_w, pool_scale=m_pool_scale, q_norm_g=m_q_norm_g,
             k_norm_g=m_k_norm_g, rel_bias=m_rel_bias, w_out=m_w_out, mlp_norm_g=m_mlp_norm_g, w_up=m_w_up, w_down=m_w_down)
    v = dict(mix_norm_g=v_mix_norm_g, w_in=v_w_in, pool_w=v_pool_w, pool_scale=v_pool_scale, q_norm_g=v_q_norm_g,
             k_norm_g=v_k_norm_g, rel_bias=v_rel_bias, w_out=v_w_out, mlp_norm_g=v_mlp_norm_g, w_up=v_w_up, w_down=v_w_down)
    xc, yc, cc = _coords()

    c_idx = jnp.reshape(cc, (1,)).astype(jnp.int32)
    chip_idx = jnp.reshape(2 * xc + yc, (1,)).astype(jnp.int32)
    me = 4 * xc + 2 * yc + cc
    whole = lambda t: t.reshape(t.shape[0], t.shape[1] * t.shape[2], t.shape[3])

    placed = [_halves(p) for p in _place_shards_call([w[n] for n in _BIG], chip_idx, nch=4)]
    win_f, bias = _allgather_call(placed[:1], from_chips=True, name="weights_allgather_in",
                                  meanwhile=_bias_table_work(rel_bias))
    wsend, wrecv, in_flight, started = _gather_start_call(placed[1:], win_f)

    def mlp_weights(after):
        landed = _gather_wait_call(in_flight, wsend, wrecv, after)
        wout_f, wup_f, wdown_f = _allgather_call(landed, from_chips=False, name="weights_pair_forward")
        return whole(wout_f).reshape(-1, wout_f.shape[-1]), whole(wup_f), whole(wdown_f)

    split = []

    def on_mlp_grads(*wire_grads):
        srcs = [_halves(g) for g in wire_grads]
        lands = [lax.empty((N_DEV,) + s.shape[2:], s.dtype) for s in srcs]
        split.extend(_scatter_start_call(srcs, lands, [False] * len(srcs), "grads_scatter_start"))
        return split[4]

    loss_part, dx, big_grads, small_grads = _local_grads(
        x[0], loss_target[0], mix_norm_g + started[0, 0], whole(win_f), pool_w, pool_scale, q_norm_g, k_norm_g, rel_bias,
        mlp_norm_g, mlp_weights, on_mlp_grads, bias)
    g_in, g_out, g_up, g_down = big_grads
    gsend, grecv, srcs_thru, lands_thru, _ = split
    lands_mlp = _scatter_wait_call(srcs_thru, lands_thru, gsend, grecv, g_in[1], [False] * 3, "grads_scatter_wait")

    small_own = _pack_small(small_grads, folded=False, loss=loss_part)
    last_srcs = [_halves(g_in[1]), small_own]
    last_lands = [lax.empty((N_DEV,) + last_srcs[0].shape[2:], WIRE_DTYPE), lax.empty((N_DEV,) + small_own.shape, F32)]
    lsend, lrecv, last_srcs, last_lands, last_started = _scatter_start_call(
        last_srcs, last_lands, [False, True], "grads_scatter_start_last")
    idx = jnp.concatenate([chip_idx, c_idx] + [jnp.reshape(jnp.bitwise_xor(me, r), (1,)) for r in range(1, N_DEV)])
    idx = idx.astype(jnp.int32)
    mlp = _BIG[1:]

    def update(names, own32, lands, tag, dep=None):
        halves = _reduce_call([_halves(g) for g in own32], lands, idx, 4, "grads_reduce_" + tag, dep)
        reduced = _pair_allgather_call(list(halves), "grads_pair_allgather_" + tag)
        return _adamw_call([w[n] for n in names], reduced, [m[n] for n in names], [v[n] for n in names], 8, "adamw_" + tag)

    out_mlp = update(mlp, [g_out[0], g_up[0], g_down[0]], lands_mlp, "mlp", last_started)
    land_in, small_all = _scatter_wait_call(last_srcs, last_lands, lsend, lrecv, out_mlp[3][-1], [False, True],
                                            "grads_scatter_wait_last")
    out_in = update(_BIG[:1], [g_in[0]], [land_in], "in")
    g_pack, d_pack, m_pack, v_pack = _small_call(
        small_all, small_own, jnp.reshape(me, (1,)).astype(jnp.int32), _pack_small(w), _pack_small(m), _pack_small(v))

    grads, deltas, new_m, new_v = (_unpack_small(a) for a in (g_pack, d_pack, m_pack, v_pack))
    for k, res in enumerate((grads, deltas, new_m, new_v)):
        res[_BIG[0]] = out_in[k][0]
        for i, n in enumerate(mlp):
            res[n] = out_mlp[k][i]
    loss = g_pack[LOSS_ROW, 0]
    return (loss, dx[None], *[grads[n] for n in _WEIGHT_ORDER], *[deltas[n] for n in _WEIGHT_ORDER],
            *[new_m[n] for n in _WEIGHT_ORDER], *[new_v[n] for n in _WEIGHT_ORDER])
```

```python
import math

import jax
import jax.numpy as jnp
import numpy as np
from jax import lax
from jax.experimental import pallas as pl
from jax.experimental.pallas import tpu as pltpu

F32 = jnp.float32
MXU_DTYPE = jnp.bfloat16
WIRE_DTYPE = jnp.bfloat16

NORM_EPS = 1e-6
NEG_INF = -1e30
LANES = 128
HEAD_DIM = 64
N_HEADS = 8
POOL_WIDTH = 512
ATTN_WIDTH = 512
POOL_WINDOWS = (2, 4, 8, 16)
POOL_HALO = 16
DILATED_PATTERNS = ((128, 1), (512, 4), (2048, 16))
ATT_BLOCK = 128
ATT_SUPER = ATT_BLOCK * max(dl for _, dl in DILATED_PATTERNS)
ATT_UNITS = ATT_SUPER // ATT_BLOCK
N_BUCKETS = 32
NO_BUCKET = -1
MAX_DISTANCE = 2048
N_CHIPS = 4
N_DEV = 8
ADAM_LR, ADAM_B1, ADAM_B2, ADAM_EPS, ADAM_WD, ADAM_STEP = 0.001, 0.9, 0.999, 1e-08, 0.01, 10
VMEM_LIMIT = 56 * 1024 * 1024
MESH = pl.DeviceIdType.MESH
ANY = pl.BlockSpec(memory_space=pl.ANY)

SMALL_ROWS = 72
LOSS_ROW = 5


def _mm(a, b):
    return jnp.dot(a, b, preferred_element_type=F32)


def _mm_nt(a, b):
    return lax.dot_general(a, b, (((1,), (1,)), ((), ())), preferred_element_type=F32)


def _mm_tn(a, b):
    return lax.dot_general(a, b, (((0,), (0,)), ((), ())), preferred_element_type=F32)


def _params(sem=None, **kw):
    if sem is not None:
        kw["dimension_semantics"] = sem
    return pltpu.CompilerParams(vmem_limit_bytes=VMEM_LIMIT, **kw)


def _low_half():
    return lax.broadcasted_iota(jnp.int32, (1, LANES), 1) < HEAD_DIM


def _head_sum_bcast(y):
    lo = _low_half()
    outs = []
    for j in range(y.shape[1] // LANES):
        c = y[:, j * LANES:(j + 1) * LANES]
        s_lo = jnp.sum(jnp.where(lo, c, 0.0), axis=-1, keepdims=True)
        s_hi = jnp.sum(jnp.where(lo, 0.0, c), axis=-1, keepdims=True)
        outs.append(jnp.where(lo, s_lo, s_hi))
    return jnp.concatenate(outs, axis=-1)


def _rms_bwd(dn, hn, r):
    return r * (dn - hn * jnp.mean(dn * hn, axis=-1, keepdims=True))


def _t5_bucket_np(dist):
    max_exact = N_BUCKETS // 2
    d_f = np.maximum(dist, 1).astype(np.float32)
    ratio = (np.log(d_f / np.float32(max_exact)) / np.float32(math.log(MAX_DISTANCE / max_exact))).astype(np.float32)
    large = max_exact + (ratio * np.float32(N_BUCKETS - max_exact)).astype(np.int32)
    large = np.minimum(large, N_BUCKETS - 1)
    return np.where(dist < max_exact, dist, large).astype(np.int32)


def _window_offsets(dl):
    if dl == 1:
        return _by4_positions(ATT_BLOCK), _by4_positions(2 * ATT_BLOCK)
    return np.arange(ATT_BLOCK), np.arange(2 * ATT_BLOCK)


def _bucket_tables():
    tables = []
    for _, dl in DILATED_PATTERNS:
        qq, kk = _window_offsets(dl)
        dist = qq[:, None] + ATT_BLOCK - kk[None, :]
        bucket = _t5_bucket_np(np.clip(dist, 0, ATT_BLOCK) * dl)
        tables.append(np.where((dist >= 0) & (dist <= ATT_BLOCK), bucket, NO_BUCKET))
    return np.stack(tables).astype(np.int32)


def _previous_block_keys():
    return np.stack([np.broadcast_to(_window_offsets(dl)[1][None, :] < ATT_BLOCK, (ATT_BLOCK, 2 * ATT_BLOCK))
                     for _, dl in DILATED_PATTERNS])


def _f1_call(x, g1, win, poolw, pscale, qg, kg, tm):
    s, d = x.shape
    nblk = s // tm

    def body(x_ref, g1_ref, win_ref, pw_ref, ps_ref, qg_ref, kg_ref,
             a_ref, pooled_ref, ypool_ref, q32_ref, k32_ref, qn_ref, kn_ref, v_ref, ubuf):
        i = pl.program_id(0)
        xv = x_ref[...]
        r = lax.rsqrt(jnp.mean(xv * xv, axis=-1, keepdims=True) + NORM_EPS)
        a = ((xv * r) * g1_ref[...]).astype(MXU_DTYPE)
        a_ref[...] = a
        u = _mm(a, win_ref[0])
        q = _mm(a, win_ref[1])
        k = _mm(a, win_ref[2])
        v_ref[...] = _mm(a, win_ref[3])
        q32_ref[...] = q
        k32_ref[...] = k
        rq = lax.rsqrt(_head_sum_bcast(q * q) * (1.0 / HEAD_DIM) + NORM_EPS)
        qn_ref[...] = ((q * rq) * qg_ref[...]) * (HEAD_DIM ** -0.5)
        rk = lax.rsqrt(_head_sum_bcast(k * k) * (1.0 / HEAD_DIM) + NORM_EPS)
        kn_ref[...] = (k * rk) * kg_ref[...]

        ubuf[0:POOL_HALO, :] = jnp.where(i > 0, ubuf[tm:tm + POOL_HALO, :], 0.0)
        ubuf[POOL_HALO:POOL_HALO + tm, :] = u
        t = i * tm + lax.broadcasted_iota(jnp.int32, (tm, 1), 0)
        for g, w in enumerate(POOL_WINDOWS):
            ls = slice(g * LANES, (g + 1) * LANES)
            ug = u[:, ls]
            acc = ug
            for sh in range(1, w):
                acc = acc + ubuf[POOL_HALO - sh:POOL_HALO - sh + tm, ls]
            cnt = jnp.minimum(t + 1, w).astype(F32)
            pooled = (acc / cnt - ug).astype(MXU_DTYPE)
            pooled_ref[:, ls] = pooled
            ypool_ref[:, ls] = (_mm(pooled, pw_ref[g]) * ps_ref[:, ls]).astype(MXU_DTYPE)

    tok = lambda w: pl.BlockSpec((tm, w), lambda i: (i, 0))
    full = lambda shp: pl.BlockSpec(shp, lambda i: (0,) * len(shp))
    return pl.pallas_call(
        body, name="fwd_inproj",
        grid=(nblk,),
        in_specs=[tok(d), full((1, d)), full(win.shape), full(poolw.shape), full((1, POOL_WIDTH)),
                  full((1, ATTN_WIDTH)), full((1, ATTN_WIDTH))],
        out_specs=[tok(d), tok(POOL_WIDTH), tok(POOL_WIDTH), tok(ATTN_WIDTH), tok(ATTN_WIDTH),
                   tok(ATTN_WIDTH), tok(ATTN_WIDTH), tok(ATTN_WIDTH)],
        out_shape=[jax.ShapeDtypeStruct((s, d), MXU_DTYPE),
                   jax.ShapeDtypeStruct((s, POOL_WIDTH), MXU_DTYPE),
                   jax.ShapeDtypeStruct((s, POOL_WIDTH), MXU_DTYPE),
                   jax.ShapeDtypeStruct((s, ATTN_WIDTH), F32),
                   jax.ShapeDtypeStruct((s, ATTN_WIDTH), F32),
                   jax.ShapeDtypeStruct((s, ATTN_WIDTH), F32),
                   jax.ShapeDtypeStruct((s, ATTN_WIDTH), F32),
                   jax.ShapeDtypeStruct((s, ATTN_WIDTH), F32)],
        scratch_shapes=[pltpu.VMEM((tm + POOL_HALO, POOL_WIDTH), F32)],
        compiler_params=_params(("arbitrary",)),
    )(x, g1, win, poolw, pscale, qg, kg)


DEINT = 4
assert [dl for _, dl in DILATED_PATTERNS] == [1, DEINT, DEINT * DEINT]


def _by4_positions(n):
    pos = np.arange(n)
    return DEINT * (pos % (n // DEINT)) + pos // (n // DEINT)


def _masked_bias(b_ref, p, n):
    return b_ref[p, jnp.minimum(n, 1)].reshape(2 * ATT_BLOCK, 2 * ATT_BLOCK)


def _unit_rows(u, dl):
    sq, sk = ATT_SUPER // DEINT, 2 * ATT_SUPER // DEINT
    if dl == 1:
        n = ATT_BLOCK // DEINT
        return (u, [pl.ds(pl.multiple_of(r * sq + n * u, 8), n) for r in range(DEINT)],
                [pl.ds(pl.multiple_of(r * sk + sk // 2 + n * (u - 1), 8), 2 * n) for r in range(DEINT)])
    if dl == DEINT:
        r, b = u % DEINT, u // DEINT
        return (b, [pl.ds(pl.multiple_of(r * sq + ATT_BLOCK * b, 8), ATT_BLOCK)],
                [pl.ds(pl.multiple_of(r * sk + sk // 2 + ATT_BLOCK * (b - 1), 8), 2 * ATT_BLOCK)])
    r, a = u % DEINT, u // DEINT
    return 0, [pl.ds(r * sq + a, ATT_BLOCK, stride=DEINT)], [pl.ds(r * sk + a, 2 * ATT_BLOCK, stride=DEINT)]


def _take(ref, runs):
    parts = [ref[run, :] for run in runs]
    return parts[0] if len(parts) == 1 else jnp.concatenate(parts, axis=0)


def _put(ref, runs, value, add=False):
    n = value.shape[0] // len(runs)
    for i, run in enumerate(runs):
        part = value[i * n:(i + 1) * n]
        ref[run, :] = ref[run, :] + part if add else part


def _deinterleave(dst, src, n):
    seg = n // DEINT
    for r in range(DEINT):
        dst[r * seg:(r + 1) * seg, :] = src[pl.ds(r, seg, stride=DEINT), :]


def _deinterleave_pair(dst, prev, cur):
    seg = prev.shape[0] // DEINT
    for r in range(DEINT):
        dst[2 * r * seg:(2 * r + 1) * seg, :] = prev[pl.ds(r, seg, stride=DEINT), :]
        dst[(2 * r + 1) * seg:(2 * r + 2) * seg, :] = cur[pl.ds(r, seg, stride=DEINT), :]


def _interleave(dst, src, n, offset=0):
    seg = n // DEINT
    stride = src.shape[0] // DEINT
    for r in range(DEINT):
        dst[pl.ds(r, seg, stride=DEINT), :] = src[r * stride + offset:r * stride + offset + seg, :]


def _attn_fwd_call(qn, kn, v, bias):
    s, w = qn.shape
    nsb = s // ATT_SUPER
    npair = w // LANES

    def body(q_ref, kc_ref, kp_ref, vc_ref, vp_ref, b_ref, o_ref, lse_ref, qf, kf, vf, acc_s, m_s, l_s):
        sb = pl.program_id(1)
        _deinterleave(qf, q_ref, ATT_SUPER)
        _deinterleave_pair(kf, kp_ref, kc_ref)
        _deinterleave_pair(vf, vp_ref, vc_ref)
        lo = _low_half()
        for p, (_, dl) in enumerate(DILATED_PATTERNS):
            def unit(u, carry, p=p, dl=dl):
                b, rows_q, rows_k = _unit_rows(u, dl)
                qp = _take(qf, rows_q).astype(MXU_DTYPE)
                kcat = _take(kf, rows_k).astype(MXU_DTYPE)
                vcat = _take(vf, rows_k).astype(MXU_DTYPE)
                zero = jnp.zeros_like(qp)
                q2 = jnp.concatenate([jnp.where(lo, qp, zero), jnp.where(lo, zero, qp)], axis=0)
                sc = _mm_nt(q2, kcat) + _masked_bias(b_ref, p, sb * (ATT_UNITS // dl) + b)
                m2 = jnp.max(sc, axis=-1, keepdims=True)
                pr = jnp.exp(sc - m2)
                l2 = jnp.sum(pr, axis=-1, keepdims=True)
                acc2 = _mm(pr.astype(MXU_DTYPE), vcat)
                acc = jnp.where(lo, acc2[:ATT_BLOCK], acc2[ATT_BLOCK:])
                m = jnp.where(lo, m2[:ATT_BLOCK], m2[ATT_BLOCK:])
                l = jnp.where(lo, l2[:ATT_BLOCK], l2[ATT_BLOCK:])
                if p == 0:
                    _put(acc_s, rows_q, acc)
                    _put(m_s, rows_q, m)
                    _put(l_s, rows_q, l)
                else:
                    m_old = _take(m_s, rows_q)
                    m_new = jnp.maximum(m_old, m)
                    a_old = jnp.exp(m_old - m_new)
                    a_new = jnp.exp(m - m_new)
                    _put(acc_s, rows_q, a_old * _take(acc_s, rows_q) + a_new * acc)
                    _put(l_s, rows_q, a_old * _take(l_s, rows_q) + a_new * l)
                    _put(m_s, rows_q, m_new)
                return carry

            lax.fori_loop(0, ATT_UNITS, unit, 0, unroll=16)
        l = l_s[...]
        acc_s[...] = acc_s[...] / l
        m_s[...] = m_s[...] + jnp.log(l)
        _interleave(o_ref, acc_s, ATT_SUPER)
        _interleave(lse_ref, m_s, ATT_SUPER)

    cur = pl.BlockSpec((ATT_SUPER, LANES), lambda j, t: (t, j))
    prev = pl.BlockSpec((ATT_SUPER, LANES), lambda j, t: (jnp.maximum(t - 1, 0), j))
    bspec = pl.BlockSpec((len(DILATED_PATTERNS), 2, 2, ATT_BLOCK, 2 * ATT_BLOCK), lambda j, t: (0, 0, j, 0, 0))
    return pl.pallas_call(
        body, name="attn_fwd",
        grid=(npair, nsb),
        in_specs=[cur, cur, prev, cur, prev, bspec],
        out_specs=[cur, cur],
        out_shape=[jax.ShapeDtypeStruct((s, w), F32), jax.ShapeDtypeStruct((s, w), F32)],
        scratch_shapes=[pltpu.VMEM((ATT_SUPER, LANES), F32), pltpu.VMEM((2 * ATT_SUPER, LANES), F32),
                        pltpu.VMEM((2 * ATT_SUPER, LANES), F32), pltpu.VMEM((ATT_SUPER, LANES), F32),
                        pltpu.VMEM((ATT_SUPER, LANES), F32), pltpu.VMEM((ATT_SUPER, LANES), F32)],
        compiler_params=_params(("arbitrary", "arbitrary")),
    )(qn, kn, kn, v, v, bias)


def _attn_bwd_call(qn, kn, v, do, lse, delta, bias, dep=None):
    s, w = qn.shape
    nsb = s // ATT_SUPER
    npair = w // LANES
    deps = [] if dep is None else [dep]

    def body(q_ref, kc_ref, kp_ref, vc_ref, vp_ref, do_ref, lse_ref, dlt_ref, b_ref, *rest):
        dq_ref, dk_ref, dv_ref, db_ref, qf, kf, vf, dof, lsef, dltf, dqf, dkf, dvf = rest[len(deps):]
        step = pl.program_id(1)
        sb = nsb - 1 - step
        seg = ATT_SUPER // DEINT
        _deinterleave(qf, q_ref, ATT_SUPER)
        _deinterleave(dof, do_ref, ATT_SUPER)
        _deinterleave_pair(kf, kp_ref, kc_ref)
        _deinterleave_pair(vf, vp_ref, vc_ref)
        _deinterleave(lsef, lse_ref, ATT_SUPER)
        _deinterleave(dltf, dlt_ref, ATT_SUPER)

        db_ref[...] = jnp.where(step > 0, db_ref[...], 0.0)
        for acc in (dkf, dvf):
            for r in range(DEINT):
                this, before = pl.ds((2 * r + 1) * seg, seg), pl.ds(2 * r * seg, seg)
                acc[this, :] = jnp.where(step > 0, acc[before, :], 0.0)
                acc[before, :] = jnp.zeros((seg, LANES), F32)
        lo = _low_half()
        for p, (_, dl) in enumerate(DILATED_PATTERNS):
            def unit(u, carry, p=p, dl=dl):
                b, rows_q, rows_k = _unit_rows(u, dl)
                qp = _take(qf, rows_q).astype(MXU_DTYPE)
                dop = _take(dof, rows_q).astype(MXU_DTYPE)
                kcat = _take(kf, rows_k).astype(MXU_DTYPE)
                vcat = _take(vf, rows_k).astype(MXU_DTYPE)
                lse2 = _take(lsef, rows_q)
                dlt2 = _take(dltf, rows_q)
                zero = jnp.zeros_like(qp)
                q2 = jnp.concatenate([jnp.where(lo, qp, zero), jnp.where(lo, zero, qp)], axis=0)
                do2 = jnp.concatenate([jnp.where(lo, dop, zero), jnp.where(lo, zero, dop)], axis=0)
                lse_c = jnp.concatenate([lse2[:, 0:1], lse2[:, HEAD_DIM:HEAD_DIM + 1]], axis=0)
                dlt_c = jnp.concatenate([dlt2[:, 0:1], dlt2[:, HEAD_DIM:HEAD_DIM + 1]], axis=0)
                sc = _mm_nt(q2, kcat) + _masked_bias(b_ref, p, sb * (ATT_UNITS // dl) + b)
                pr = jnp.exp(sc - lse_c)
                ds = pr * (_mm_nt(do2, vcat) - dlt_c)
                db_ref[p] += ds.reshape(2, ATT_BLOCK, 2 * ATT_BLOCK)
                ds_c = ds.astype(MXU_DTYPE)
                dq2 = _mm(ds_c, kcat)
                dk = _mm_tn(ds_c, q2)
                dv = _mm_tn(pr.astype(MXU_DTYPE), do2)
                dq = jnp.where(lo, dq2[:ATT_BLOCK], dq2[ATT_BLOCK:])
                _put(dqf, rows_q, dq, add=p > 0)
                _put(dkf, rows_k, dk, add=True)
                _put(dvf, rows_k, dv, add=True)
                return carry

            lax.fori_loop(0, ATT_UNITS, unit, 0, unroll=16)
        _interleave(dq_ref, dqf, ATT_SUPER)
        _interleave(dk_ref, dkf, ATT_SUPER, offset=seg)
        _interleave(dv_ref, dvf, ATT_SUPER, offset=seg)

    cur = pl.BlockSpec((ATT_SUPER, LANES), lambda j, t: (nsb - 1 - t, j))
    prev = pl.BlockSpec((ATT_SUPER, LANES), lambda j, t: (jnp.maximum(nsb - 2 - t, 0), j))
    npat = len(DILATED_PATTERNS)
    bspec = pl.BlockSpec((npat, 2, 2, ATT_BLOCK, 2 * ATT_BLOCK), lambda j, t: (0, 0, j, 0, 0))
    dbspec = pl.BlockSpec((npat, 2, ATT_BLOCK, 2 * ATT_BLOCK), lambda j, t: (0, j, 0, 0))
    sup = lambda: pltpu.VMEM((ATT_SUPER, LANES), F32)
    sup2 = lambda: pltpu.VMEM((2 * ATT_SUPER, LANES), F32)
    return pl.pallas_call(
        body, name="attn_bwd",
        grid=(npair, nsb),
        in_specs=[cur, cur, prev, cur, prev, cur, cur, cur, bspec] + [ANY] * len(deps),
        out_specs=[cur, cur, cur, dbspec],
        out_shape=[jax.ShapeDtypeStruct((s, w), F32)] * 3
        + [jax.ShapeDtypeStruct((npat, N_HEADS, ATT_BLOCK, 2 * ATT_BLOCK), F32)],
        scratch_shapes=[sup(), sup2(), sup2(), sup(), sup(), sup(), sup(), sup2(), sup2()],
        compiler_params=_params(("arbitrary", "arbitrary")),
    )(qn, kn, kn, v, v, do, lse, delta, bias, *deps)


def _bias_table_work(rel_bias):
    buckets = jnp.asarray(_bucket_tables())
    prev_keys = jnp.asarray(_previous_block_keys().astype(np.int32))
    npat = buckets.shape[0]

    def body(rb_ref, bk_ref, pk_ref, out_ref):
        for p in range(npat):
            for half in range(2):
                ks = slice(half * ATT_BLOCK, (half + 1) * ATT_BLOCK)
                bk = bk_ref[p, :, ks]
                absent = pk_ref[p, :, ks] != 0
                for h in range(N_HEADS):
                    def pick(b, acc, h=h, bk=bk):
                        return jnp.where(bk == b, rb_ref[b, h], acc)

                    tab = lax.fori_loop(0, N_BUCKETS, pick, jnp.full((ATT_BLOCK, ATT_BLOCK), NEG_INF, F32))
                    out_ref[p, 1, h, :, ks] = tab
                    out_ref[p, 0, h, :, ks] = jnp.where(absent, NEG_INF, tab)

    vmem = pl.BlockSpec(memory_space=pltpu.VMEM)
    return ([rel_bias, buckets, prev_keys], [pl.BlockSpec(memory_space=pltpu.SMEM), vmem, vmem],
            jax.ShapeDtypeStruct((npat, 2, N_HEADS, ATT_BLOCK, 2 * ATT_BLOCK), F32), body)


def _bias_table_call(rel_bias):
    operands, specs, shape, body = _bias_table_work(rel_bias)
    return pl.pallas_call(body, name="bias_table", in_specs=specs, out_shape=shape, compiler_params=_params())(*operands)


def _rel_bias_grad_call(dbias, buckets):
    npat, nh = dbias.shape[0], dbias.shape[1]

    def body(db_ref, bk_ref, out_ref):
        lane = lax.broadcasted_iota(jnp.int32, (nh, LANES), 1)
        out = jnp.zeros((nh, LANES), F32)
        for b in range(N_BUCKETS):
            tot = jnp.zeros((nh, 1), F32)
            for p in range(npat):
                hit = jnp.where(bk_ref[p][None] == b, db_ref[p], 0.0)
                tot = tot + jnp.sum(jnp.sum(hit, axis=1), axis=-1, keepdims=True)
            out = jnp.where(lane == b, tot, out)
        out_ref[...] = out

    return pl.pallas_call(
        body, name="rel_bias_grad",
        out_shape=jax.ShapeDtypeStruct((nh, LANES), F32),
        compiler_params=_params(),
    )(dbias, buckets)


def _f2_call(x, tgt, ypool, o, wout, wup, wdown, g2, tm):
    s, d = x.shape
    nblk = s // tm
    nch, _, fch = wup.shape
    dff = nch * fch
    mixw = POOL_WIDTH + ATTN_WIDTH

    def body(x_ref, t_ref, yp_ref, o_ref, g2_ref, wout_hbm, wup_hbm, wdown_hbm,
             mixed_ref, c_ref, ff_ref, dz_ref, dy_ref, dh1_ref, dyp_ref, do_ref, dlt_ref, dg2_ref, loss_ref,
             wout_v, wup_v, wdown_v, rz):
        i = pl.program_id(0)

        @pl.when(i == 0)
        def _():
            pltpu.sync_copy(wout_hbm, wout_v)
            pltpu.sync_copy(wup_hbm, wup_v)
            pltpu.sync_copy(wdown_hbm, wdown_v)
            dg2_ref[...] = jnp.zeros(dg2_ref.shape, F32)
            loss_ref[...] = jnp.zeros(loss_ref.shape, F32)

        o = o_ref[...]
        mixed = jnp.concatenate([yp_ref[...], o.astype(MXU_DTYPE)], axis=-1)
        mixed_ref[...] = mixed
        h1 = x_ref[...] + _mm(mixed, wout_v[...])
        r2 = lax.rsqrt(jnp.mean(h1 * h1, axis=-1, keepdims=True) + NORM_EPS)
        hn = h1 * r2
        c = (hn * g2_ref[...]).astype(MXU_DTYPE)
        c_ref[...] = c
        y = h1
        for j in range(nch):
            cs = slice(j * fch, (j + 1) * fch)
            z = jnp.maximum(_mm(c, wup_v[j]), 0.0)
            rz[:, cs] = z
            ff = (z * z).astype(MXU_DTYPE)
            ff_ref[:, cs] = ff
            y = y + _mm(ff, wdown_v[j])
        err = y - t_ref[...]
        loss_ref[...] += jnp.sum(err * err) * (0.5 / d)
        dy = err * (1.0 / d)
        dy_c = dy.astype(MXU_DTYPE)
        dy_ref[...] = dy_c
        dc = jnp.zeros((tm, d), F32)
        for j in range(nch):
            cs = slice(j * fch, (j + 1) * fch)
            dz = (_mm_nt(dy_c, wdown_v[j]) * (2.0 * rz[:, cs])).astype(MXU_DTYPE)
            dz_ref[:, cs] = dz
            dc = dc + _mm_nt(dz, wup_v[j])
        dg2_ref[...] += jnp.sum(dc * hn, axis=0, keepdims=True)
        dh1 = dy + _rms_bwd(dc * g2_ref[...], hn, r2)
        dh1_ref[...] = dh1
        dmix = _mm_nt(dh1.astype(MXU_DTYPE), wout_v[...])
        dyp_ref[...] = dmix[:, :POOL_WIDTH]
        do = dmix[:, POOL_WIDTH:]
        do_ref[...] = do
        dlt_ref[...] = _head_sum_bcast(do * o)

    tok = lambda w: pl.BlockSpec((tm, w), lambda i: (i, 0))
    const = lambda shp: pl.BlockSpec(shp, lambda i: (0,) * len(shp))
    return pl.pallas_call(
        body, name="fwd_mlp_bwd_mlp",
        grid=(nblk,),
        in_specs=[tok(d), tok(d), tok(POOL_WIDTH), tok(ATTN_WIDTH), const((1, d)), ANY, ANY, ANY],
        out_specs=[tok(mixw), tok(d), tok(dff), tok(dff), tok(d), tok(d), tok(POOL_WIDTH), tok(ATTN_WIDTH),
                   tok(ATTN_WIDTH), const((1, d)), const((1, LANES))],
        out_shape=[jax.ShapeDtypeStruct((s, mixw), MXU_DTYPE),
                   jax.ShapeDtypeStruct((s, d), MXU_DTYPE),
                   jax.ShapeDtypeStruct((s, dff), MXU_DTYPE),
                   jax.ShapeDtypeStruct((s, dff), MXU_DTYPE),
                   jax.ShapeDtypeStruct((s, d), MXU_DTYPE),
                   jax.ShapeDtypeStruct((s, d), F32),
                   jax.ShapeDtypeStruct((s, POOL_WIDTH), F32),
                   jax.ShapeDtypeStruct((s, ATTN_WIDTH), F32),
                   jax.ShapeDtypeStruct((s, ATTN_WIDTH), F32),
                   jax.ShapeDtypeStruct((1, d), F32),
                   jax.ShapeDtypeStruct((1, LANES), F32)],
        scratch_shapes=[pltpu.VMEM(wout.shape, MXU_DTYPE), pltpu.VMEM(wup.shape, MXU_DTYPE),
                        pltpu.VMEM(wdown.shape, MXU_DTYPE), pltpu.VMEM((tm, dff), F32)],
        compiler_params=_params(("arbitrary",)),
    )(x, tgt, ypool, o, g2, wout, wup, wdown)


def _bproj_call(dqn, dkn, dv, q32, k32, dypool, pooled, x, dh1, win, poolw, pscale, qg, kg, g1, tm):
    s, d = x.shape
    nblk = s // tm
    ngrp = len(POOL_WINDOWS)

    def body(dqn_ref, dkn_ref, dv_ref, q_ref, k_ref, dyp_ref, pooled_ref, x_ref, dh1_ref,
             win_hbm, pw_ref, ps_ref, qg_ref, kg_ref, g1_ref,
             dx_ref, dproj_ref, dg1_ref, dqg_ref, dkg_ref, dpw_ref, dps_ref, win_v, ebuf):
        step = pl.program_id(0)
        i = nblk - 1 - step

        @pl.when(step == 0)
        def _():
            pltpu.sync_copy(win_hbm, win_v)
            dg1_ref[...] = jnp.zeros(dg1_ref.shape, F32)
            dqg_ref[...] = jnp.zeros(dqg_ref.shape, F32)
            dkg_ref[...] = jnp.zeros(dkg_ref.shape, F32)
            dpw_ref[...] = jnp.zeros(dpw_ref.shape, F32)
            dps_ref[...] = jnp.zeros(dps_ref.shape, F32)
            ebuf[tm:tm + POOL_HALO, :] = jnp.zeros((POOL_HALO, POOL_WIDTH), F32)

        @pl.when(step > 0)
        def _():
            ebuf[tm:tm + POOL_HALO, :] = ebuf[0:POOL_HALO, :]

        def qk_bwd(dn_sum, raw, gain, scale, dgain_ref):
            rr = lax.rsqrt(_head_sum_bcast(raw * raw) * (1.0 / HEAD_DIM) + NORM_EPS)
            hn = raw * rr
            dgain_ref[...] += jnp.sum(dn_sum * hn, axis=0, keepdims=True) * scale
            dn = dn_sum * (gain * scale)
            return rr * (dn - hn * (_head_sum_bcast(dn * hn) * (1.0 / HEAD_DIM)))

        dq = qk_bwd(dqn_ref[...], q_ref[...], qg_ref[...], HEAD_DIM ** -0.5, dqg_ref)
        dk = qk_bwd(dkn_ref[...], k_ref[...], kg_ref[...], 1.0, dkg_ref)

        t = i * tm + lax.broadcasted_iota(jnp.int32, (tm, 1), 0)
        dpooled = []
        for g, w in enumerate(POOL_WINDOWS):
            ls = slice(g * LANES, (g + 1) * LANES)
            dm = dyp_ref[:, ls]
            pg = pooled_ref[:, ls]
            dps_ref[:, ls] += jnp.sum(dm * _mm(pg, pw_ref[g]), axis=0, keepdims=True)
            dms = (dm * ps_ref[:, ls]).astype(MXU_DTYPE)
            dpw_ref[g] += _mm_tn(pg, dms)
            dpg = _mm_nt(dms, pw_ref[g])
            dpooled.append(dpg)
            ebuf[0:tm, ls] = dpg / jnp.minimum(t + 1, w).astype(F32)
        du = []
        for g, w in enumerate(POOL_WINDOWS):
            ls = slice(g * LANES, (g + 1) * LANES)
            acc = ebuf[0:tm, ls]
            for sh in range(1, w):
                acc = acc + ebuf[sh:sh + tm, ls]
            du.append(acc - dpooled[g])
        parts = [jnp.concatenate(du, axis=-1), dq, dk, dv_ref[...]]
        da = jnp.zeros((tm, d), F32)
        for p, part in enumerate(parts):
            pc = part.astype(MXU_DTYPE)
            dproj_ref[:, p * POOL_WIDTH:(p + 1) * POOL_WIDTH] = pc
            da = da + _mm_nt(pc, win_v[p])
        xv = x_ref[...]
        r = lax.rsqrt(jnp.mean(xv * xv, axis=-1, keepdims=True) + NORM_EPS)
        xn = xv * r
        dg1_ref[...] += jnp.sum(da * xn, axis=0, keepdims=True)
        dx_ref[...] = dh1_ref[...] + _rms_bwd(da * g1_ref[...], xn, r)

    tok = lambda w: pl.BlockSpec((tm, w), lambda t: (nblk - 1 - t, 0))
    const = lambda shp: pl.BlockSpec(shp, lambda t: (0,) * len(shp))
    return pl.pallas_call(
        body, name="bwd_inproj",
        grid=(nblk,),
        in_specs=[tok(ATTN_WIDTH)] * 5 + [tok(POOL_WIDTH), tok(POOL_WIDTH), tok(d), tok(d),
                                          ANY, const(poolw.shape), const((1, POOL_WIDTH)), const((1, ATTN_WIDTH)),
                                          const((1, ATTN_WIDTH)), const((1, d))],
        out_specs=[tok(d), tok(4 * POOL_WIDTH), const((1, d)), const((1, ATTN_WIDTH)), const((1, ATTN_WIDTH)),
                   const((ngrp, LANES, LANES)), const((1, POOL_WIDTH))],
        out_shape=[jax.ShapeDtypeStruct((s, d), F32),
                   jax.ShapeDtypeStruct((s, 4 * POOL_WIDTH), MXU_DTYPE),
                   jax.ShapeDtypeStruct((1, d), F32),
                   jax.ShapeDtypeStruct((1, ATTN_WIDTH), F32),
                   jax.ShapeDtypeStruct((1, ATTN_WIDTH), F32),
                   jax.ShapeDtypeStruct((ngrp, LANES, LANES), F32),
                   jax.ShapeDtypeStruct((1, POOL_WIDTH), F32)],
        scratch_shapes=[pltpu.VMEM(win.shape, MXU_DTYPE), pltpu.VMEM((tm + POOL_HALO, POOL_WIDTH), F32)],
        compiler_params=_params(("arbitrary",)),
    )(dqn, dkn, dv, q32, k32, dypool, pooled, x, dh1, win, poolw, pscale, qg, kg, g1)


def _wgrad_call(a, b, bm, bn, bk, out_shape, out_block, out_index, name):
    s, m = a.shape
    _, n = b.shape
    nk = s // bk

    def body(a_ref, b_ref, o_ref, wire_ref):
        k = pl.program_id(2)

        @pl.when(k == 0)
        def _():
            o_ref[...] = jnp.zeros(o_ref.shape, F32)

        o_ref[...] += _mm_tn(a_ref[...].astype(MXU_DTYPE), b_ref[...].astype(MXU_DTYPE))

        @pl.when(k == nk - 1)
        def _():
            wire_ref[...] = o_ref[...].astype(WIRE_DTYPE)

    return pl.pallas_call(
        body, name=name,
        grid=(m // bm, n // bn, nk),
        in_specs=[pl.BlockSpec((bk, bm), lambda i, j, k: (k, i)), pl.BlockSpec((bk, bn), lambda i, j, k: (k, j))],
        out_specs=[pl.BlockSpec(out_block, out_index)] * 2,
        out_shape=[jax.ShapeDtypeStruct(out_shape, F32), jax.ShapeDtypeStruct(out_shape, WIRE_DTYPE)],
        compiler_params=_params(("arbitrary", "arbitrary", "arbitrary")),
    )(a, b)


def _local_grads(x, tgt, g1, win, poolw, pscale, qg, kg, rel_bias, g2, mlp_weights, on_mlp_grads=None, bias=None):
    s, d = x.shape
    g1r, g2r = g1.reshape(1, d), g2.reshape(1, d)
    psr = pscale.reshape(1, POOL_WIDTH)
    qgr = jnp.tile(qg, N_HEADS).reshape(1, ATTN_WIDTH)
    kgr = jnp.tile(kg, N_HEADS).reshape(1, ATTN_WIDTH)
    pw_c = poolw.astype(MXU_DTYPE)
    buckets = jnp.asarray(_bucket_tables())
    bias = _bias_table_call(rel_bias) if bias is None else bias
    bk = min(s, 4096)

    a, pooled, ypool, q32, k32, qn, kn, v = _f1_call(x, g1r, win, pw_c, psr, qgr, kgr, tm=512)
    o, lse = _attn_fwd_call(qn, kn, v, bias)
    wout, wup, wdown = mlp_weights(o)
    mixed, c, ff, dz, dy, dh1, dypool, do, delta, dg2, loss = _f2_call(x, tgt, ypool, o, wout, wup, wdown, g2r, tm=256)
    dff = ff.shape[1]
    g_out = [g.reshape(N_CHIPS, d // N_CHIPS, d)
             for g in _wgrad_call(mixed, dh1, d, d, bk // 4, (d, d), (d, d), lambda i, j, k: (0, 0), "wgrad_out")]
    g_up = _wgrad_call(c, dz, d, dff // N_CHIPS, bk, (N_CHIPS, d, dff // N_CHIPS), (None, d, dff // N_CHIPS),
                       lambda i, j, k: (j, 0, 0), "wgrad_up")
    g_down = _wgrad_call(ff, dy, dff // N_CHIPS, d, bk, (N_CHIPS, dff // N_CHIPS, d), (None, dff // N_CHIPS, d),
                         lambda i, j, k: (i, 0, 0), "wgrad_down")
    dep = None if on_mlp_grads is None else on_mlp_grads(g_out[1], g_up[1], g_down[1])
    dqn, dkn, dv, dbias = _attn_bwd_call(qn, kn, v, do, lse, delta, bias, dep)
    dx, dproj, dg1, dqg, dkg, dpw, dps = _bproj_call(
        dqn, dkn, dv, q32, k32, dypool, pooled, x, dh1, win, pw_c, psr, qgr, kgr, g1r, tm=512)
    nin = dproj.shape[1] // N_CHIPS
    g_in = _wgrad_call(a, dproj, d, nin, bk, (N_CHIPS, d, nin), (None, d, nin), lambda i, j, k: (j, 0, 0), "wgrad_in")
    drb = _rel_bias_grad_call(dbias, buckets)
    small = dict(
        mix_norm_g=dg1.reshape(d), mlp_norm_g=dg2.reshape(d), pool_scale=dps.reshape(POOL_WIDTH),
        q_norm_g=dqg.reshape(ATTN_WIDTH), k_norm_g=dkg.reshape(ATTN_WIDTH),
        rel_bias=drb[:, :N_BUCKETS].T, pool_w=dpw)
    return loss[0, 0], dx, (g_in, g_out, g_up, g_down), small


def _coords():
    return lax.axis_index("x"), lax.axis_index("y"), lax.axis_index("c")


def _other_chips(x, y):
    return [(1 - x, y), (x, 1 - y), (1 - x, 1 - y)]


def _remote(src, dst, send_sem, recv_sem, dev):
    return pltpu.make_async_remote_copy(src_ref=src, dst_ref=dst, send_sem=send_sem, recv_sem=recv_sem,
                                        device_id=dev, device_id_type=MESH)


def _halves(a):
    return a.reshape(a.shape[:-2] + (2, a.shape[-2] // 2, a.shape[-1]))


def _place_shards_call(shards, chip_idx, nch):
    nw = len(shards)

    def body(chip_ref, *refs):
        for w in range(nw):
            refs[nw + w][...] = refs[w][...].astype(WIRE_DTYPE)

    in_specs = [pl.BlockSpec((s.shape[0] // nch, s.shape[1]), lambda i, chip_ref: (i, 0)) for s in shards]
    out_specs = [pl.BlockSpec((None, s.shape[0] // nch, s.shape[1]), lambda i, chip_ref: (chip_ref[0], i, 0))
                 for s in shards]
    return pl.pallas_call(
        body, name="weights_place",
        grid_spec=pltpu.PrefetchScalarGridSpec(num_scalar_prefetch=1, grid=(nch,),
                                               in_specs=in_specs, out_specs=out_specs),
        out_shape=[jax.ShapeDtypeStruct((N_CHIPS,) + s.shape, WIRE_DTYPE) for s in shards],
        compiler_params=_params(("arbitrary",)),
    )(chip_idx, *shards)


def _allgather_call(placed, from_chips, name, meanwhile=None):
    nw = len(placed)
    ncp = 3 * nw
    extra, extra_specs, extra_shape, extra_body = meanwhile if meanwhile else ([], [], None, None)
    ne = len(extra)

    def body(*refs):
        outs = refs[nw + ne:2 * nw + ne]
        send1, recv1, send2, recv2 = refs[-4:]
        x, y, c = _coords()
        chip = 2 * x + y
        others = _other_chips(x, y)
        first, passed = [], []
        if from_chips:
            for w in range(nw):
                for k, (ox, oy) in enumerate(others):
                    mine = outs[w].at[chip, c]
                    cp = _remote(mine, mine, send1.at[3 * w + k], recv1.at[3 * w + k], (ox, oy, c))
                    cp.start()
                    first.append(cp)
        if meanwhile:
            extra_body(*refs[nw:nw + ne], refs[2 * nw + ne])
        for w in range(nw):
            for k, (ox, oy) in enumerate(others):
                piece = outs[w].at[2 * ox + oy, c]
                if from_chips:
                    _remote(piece, piece, send1.at[3 * w + k], recv1.at[3 * w + k], (ox, oy, c)).wait_recv()
                cp = _remote(piece, piece, send2.at[3 * w + k], recv2.at[3 * w + k], (x, y, 1 - c))
                cp.start()
                passed.append(cp)
        for w in range(nw):
            for k, (ox, oy) in enumerate(others):
                piece = outs[w].at[2 * ox + oy, 1 - c]
                _remote(piece, piece, send2.at[3 * w + k], recv2.at[3 * w + k], (x, y, 1 - c)).wait_recv()
        for cp in first + passed:
            cp.wait_send()

    return pl.pallas_call(
        body, name=name,
        in_specs=[ANY] * nw + list(extra_specs),
        out_specs=[ANY] * nw + ([pl.BlockSpec(memory_space=pltpu.VMEM)] if meanwhile else []),
        out_shape=[jax.ShapeDtypeStruct(s.shape, s.dtype) for s in placed] + ([extra_shape] if meanwhile else []),
        input_output_aliases={w: w for w in range(nw)},
        scratch_shapes=[pltpu.SemaphoreType.DMA((ncp,))] * 4,
        compiler_params=_params(),
    )(*placed, *extra)


HBM_SPEC = pl.BlockSpec(memory_space=pltpu.HBM)
SEM_SPEC = pl.BlockSpec(memory_space=pltpu.SEMAPHORE)
SPLIT_EFFECT = pltpu.SideEffectType.DATAFLOW_SIDE_EFFECTING


def _in_hbm(a):
    return pltpu.with_memory_space_constraint(a, pltpu.HBM)


def _gather_copies(bufs, send, recv):
    x, y, c = _coords()
    chip = 2 * x + y
    cps = []
    for w, buf in enumerate(bufs):
        for k, (ox, oy) in enumerate(_other_chips(x, y)):
            mine, theirs = buf.at[chip, c], buf.at[2 * ox + oy, c]
            sems = (send.at[3 * w + k], recv.at[3 * w + k], (ox, oy, c))
            cps.append((_remote(mine, mine, *sems), _remote(theirs, theirs, *sems)))
    return cps


def _gather_start_call(bufs, after):
    nw = len(bufs)

    def body(*refs):
        ins, send, recv, token = refs[:nw], refs[nw + 1], refs[nw + 2], refs[2 * nw + 3]
        for out, _ in _gather_copies(ins, send, recv):
            out.start()
        token[...] = jnp.zeros(token.shape, F32)

    res = pl.pallas_call(
        body, name="weights_gather_start",
        in_specs=[HBM_SPEC] * nw + [ANY],
        out_specs=[SEM_SPEC, SEM_SPEC] + [HBM_SPEC] * nw + [pl.BlockSpec(memory_space=pltpu.VMEM)],
        out_shape=[pltpu.SemaphoreType.DMA((3 * nw,)), pltpu.SemaphoreType.DMA((3 * nw,))]
        + [pltpu.HBM(b.shape, b.dtype) for b in bufs] + [jax.ShapeDtypeStruct((8, LANES), F32)],
        input_output_aliases={w: 2 + w for w in range(nw)},
        compiler_params=pltpu.CompilerParams(has_side_effects=SPLIT_EFFECT),
    )(*[_in_hbm(b) for b in bufs], after)
    return res[0], res[1], list(res[2:2 + nw]), res[2 + nw]


def _gather_wait_call(bufs, send, recv, after):
    nw = len(bufs)

    def body(*refs):
        ins, send, recv = refs[:nw], refs[nw], refs[nw + 1]
        for out, back in _gather_copies(ins, send, recv):
            out.wait_send()
            back.wait_recv()

    return pl.pallas_call(
        body, name="weights_gather_wait",
        in_specs=[HBM_SPEC] * nw + [SEM_SPEC, SEM_SPEC, ANY],
        out_specs=[HBM_SPEC] * nw,
        out_shape=[pltpu.HBM(b.shape, b.dtype) for b in bufs],
        input_output_aliases={w: w for w in range(nw)},
        compiler_params=pltpu.CompilerParams(has_side_effects=SPLIT_EFFECT),
    )(*bufs, send, recv, after)


def _scatter_copies(srcs, lands, send, recv, wholes):
    x, y, c = _coords()
    me = 4 * x + 2 * y + c
    cps = []
    for w, (src, land) in enumerate(zip(srcs, lands)):
        for r in range(1, N_DEV):
            px, py, pc = ((1 - x) if r & 4 else x, (1 - y) if r & 2 else y, (1 - c) if r & 1 else c)
            sems = (send.at[(N_DEV - 1) * w + r - 1], recv.at[(N_DEV - 1) * w + r - 1], (px, py, pc))
            piece = src if wholes[w] else src.at[2 * px + py, pc]
            cps.append((_remote(piece, land.at[me], *sems), _remote(piece, land.at[4 * px + 2 * py + pc], *sems)))
    return cps


def _scatter_start_call(srcs, lands, wholes, name):
    nw = len(srcs)
    ncp = (N_DEV - 1) * nw

    def body(*refs):
        ins, lnd, send, recv, token = refs[:nw], refs[nw:2 * nw], refs[2 * nw], refs[2 * nw + 1], refs[4 * nw + 2]
        for out, _ in _scatter_copies(ins, lnd, send, recv, wholes):
            out.start()
        token[...] = jnp.zeros(token.shape, F32)

    res = pl.pallas_call(
        body, name=name,
        in_specs=[HBM_SPEC] * (2 * nw),
        out_specs=[SEM_SPEC, SEM_SPEC] + [HBM_SPEC] * (2 * nw) + [pl.BlockSpec(memory_space=pltpu.VMEM)],
        out_shape=[pltpu.SemaphoreType.DMA((ncp,)), pltpu.SemaphoreType.DMA((ncp,))]
        + [pltpu.HBM(b.shape, b.dtype) for b in list(srcs) + list(lands)] + [jax.ShapeDtypeStruct((8, LANES), F32)],
        input_output_aliases={i: 2 + i for i in range(2 * nw)},
        compiler_params=pltpu.CompilerParams(has_side_effects=SPLIT_EFFECT),
    )(*[_in_hbm(b) for b in list(srcs) + list(lands)])
    return res[0], res[1], list(res[2:2 + nw]), list(res[2 + nw:2 + 2 * nw]), res[2 + 2 * nw]


def _scatter_wait_call(srcs, lands, send, recv, after, wholes, name):
    nw = len(srcs)

    def body(*refs):
        ins, lnd, send, recv = refs[:nw], refs[nw:2 * nw], refs[2 * nw], refs[2 * nw + 1]
        for out, back in _scatter_copies(ins, lnd, send, recv, wholes):
            out.wait_send()
            back.wait_recv()

    res = pl.pallas_call(
        body, name=name,
        in_specs=[HBM_SPEC] * (2 * nw) + [SEM_SPEC, SEM_SPEC, ANY],
        out_specs=[HBM_SPEC] * (2 * nw),
        out_shape=[pltpu.HBM(b.shape, b.dtype) for b in list(srcs) + list(lands)],
        input_output_aliases={i: i for i in range(2 * nw)},
        compiler_params=pltpu.CompilerParams(has_side_effects=SPLIT_EFFECT),
    )(*srcs, *lands, send, recv, after)
    return list(res[nw:])


def _reduce_call(own, lands, idx, nch, name, dep=None):
    nw = len(own)
    deps = [] if dep is None else [dep]

    def body(idx_ref, *refs):
        refs = refs[:2 * nw] + refs[2 * nw + len(deps):]
        for w in range(nw):
            tot = refs[w][...]
            for r in range(1, N_DEV):
                tot = tot + refs[nw + w][idx_ref[1 + r]].astype(F32)
            refs[2 * nw + w][...] = tot

    in_specs, out_specs, out_shape = [], [], []
    for s in own:
        in_specs.append(pl.BlockSpec((None, None, s.shape[2] // nch, s.shape[3]),
                                     lambda i, idx_ref: (idx_ref[0], idx_ref[1], i, 0)))
    for s in own:
        in_specs.append(pl.BlockSpec((N_DEV, s.shape[2] // nch, s.shape[3]), lambda i, idx_ref: (0, i, 0)))
    for s in own:
        out_specs.append(pl.BlockSpec((None, s.shape[2] // nch, s.shape[3]), lambda i, idx_ref: (idx_ref[1], i, 0)))
        out_shape.append(jax.ShapeDtypeStruct((2,) + s.shape[2:], F32))
    return pl.pallas_call(
        body, name=name,
        grid_spec=pltpu.PrefetchScalarGridSpec(num_scalar_prefetch=1, grid=(nch,),
                                               in_specs=in_specs + [ANY] * len(deps), out_specs=out_specs),
        out_shape=out_shape,
        compiler_params=_params(("arbitrary",)),
    )(idx, *own, *lands, *deps)


def _pair_allgather_call(halves, name):
    nw = len(halves)

    def body(*refs):
        outs = refs[nw:2 * nw]
        send, recv = refs[2 * nw:]
        x, y, c = _coords()
        cps = []
        for w in range(nw):
            cp = _remote(outs[w].at[c], outs[w].at[c], send.at[w], recv.at[w], (x, y, 1 - c))
            cp.start()
            cps.append(cp)
        for w in range(nw):
            theirs = outs[w].at[1 - c]
            _remote(theirs, theirs, send.at[w], recv.at[w], (x, y, 1 - c)).wait_recv()
        for cp in cps:
            cp.wait_send()

    outs = pl.pallas_call(
        body, name=name,
        in_specs=[ANY] * nw, out_specs=[ANY] * nw,
        out_shape=[jax.ShapeDtypeStruct(h.shape, h.dtype) for h in halves],
        input_output_aliases={w: w for w in range(nw)},
        scratch_shapes=[pltpu.SemaphoreType.DMA((nw,))] * 2,
    )(*halves)
    return [o.reshape(2 * h.shape[1], h.shape[2]) for o, h in zip(outs, halves)]


def _adamw(w, g, m, v):
    m = ADAM_B1 * m + (1.0 - ADAM_B1) * g
    v = ADAM_B2 * v + (1.0 - ADAM_B2) * (g * g)
    m_hat = m / (1.0 - ADAM_B1 ** ADAM_STEP)
    v_hat = v / (1.0 - ADAM_B2 ** ADAM_STEP)
    delta = -ADAM_LR * (m_hat / (jnp.sqrt(v_hat) + ADAM_EPS) + ADAM_WD * w)
    return delta, m, v


def _adamw_call(ws, gs, ms, vs, nch, name):
    nw = len(ws)

    def body(*refs):
        for w in range(nw):
            g = refs[nw + w][...]
            delta, m, v = _adamw(refs[w][...], g, refs[2 * nw + w][...], refs[3 * nw + w][...])
            refs[4 * nw + w][...] = g
            refs[5 * nw + w][...] = delta
            refs[6 * nw + w][...] = m
            refs[7 * nw + w][...] = v

    specs = [pl.BlockSpec((a.shape[0] // nch, a.shape[1]), lambda i: (i, 0)) for a in ws]
    res = pl.pallas_call(
        body, name=name,
        grid=(nch,),
        in_specs=specs * 4, out_specs=specs * 4,
        out_shape=[jax.ShapeDtypeStruct(a.shape, F32) for a in ws] * 4,
        compiler_params=_params(("arbitrary",)),
    )(*ws, *gs, *ms, *vs)
    return res[:nw], res[nw:2 * nw], res[2 * nw:3 * nw], res[3 * nw:]


def _small_call(gathered, own, me_idx, w, m, v):
    def fold(row):
        tot = row[:, 0:LANES] + row[:, LANES:2 * LANES] + row[:, 2 * LANES:3 * LANES] + row[:, 3 * LANES:4 * LANES]
        return tot + pltpu.roll(tot, HEAD_DIM, axis=1)

    def body(me_ref, ga_ref, own_ref, w_ref, m_ref, v_ref, g_out, d_out, m_out, v_out):
        me = me_ref[0]
        term = lambda i: jnp.where(me == i, own_ref[...], ga_ref[i])
        g = term(0)
        for i in range(1, N_DEV):
            g = g + term(i)
        unfolded = g[4:5, :]
        folded = jnp.concatenate([fold(unfolded[:, :ATTN_WIDTH]), fold(unfolded[:, ATTN_WIDTH:]),
                                  jnp.zeros((1, 1024 - 2 * LANES), F32)], axis=-1)
        row = lax.broadcasted_iota(jnp.int32, g.shape, 0)
        g = jnp.where(row == 3, folded, g)
        delta, mm, vv = _adamw(w_ref[...], g, m_ref[...], v_ref[...])
        g_out[...] = g
        d_out[...] = delta
        m_out[...] = mm
        v_out[...] = vv

    vmem = pl.BlockSpec(memory_space=pltpu.VMEM)
    return pl.pallas_call(
        body, name="adamw_small",
        in_specs=[pl.BlockSpec(memory_space=pltpu.SMEM)] + [vmem] * 5,
        out_shape=[jax.ShapeDtypeStruct(w.shape, F32)] * 4,
        compiler_params=_params(),
    )(me_idx, gathered, own, w, m, v)


def _pack_small(p, folded=True, loss=None):
    z = lambda n: jnp.zeros((n,), F32)
    rows = [p["mix_norm_g"], p["mlp_norm_g"],
            jnp.concatenate([p["pool_scale"], p["rel_bias"].reshape(-1), z(1024 - POOL_WIDTH - N_BUCKETS * N_HEADS)])]
    if folded:
        rows += [jnp.concatenate([p["q_norm_g"], z(LANES - HEAD_DIM), p["k_norm_g"], z(1024 - LANES - HEAD_DIM)]), z(1024)]
    else:
        rows += [z(1024), jnp.concatenate([p["q_norm_g"], p["k_norm_g"]])]
    rows += [z(1024) if loss is None else jnp.concatenate([loss.reshape(1), z(1023)])]
    head = jnp.stack(rows + [z(1024)] * 2)
    return jnp.concatenate([head, p["pool_w"].reshape(-1, 1024)], axis=0)


def _unpack_small(a):
    return dict(
        mix_norm_g=a[0], mlp_norm_g=a[1], pool_scale=a[2, :POOL_WIDTH],
        rel_bias=a[2, POOL_WIDTH:POOL_WIDTH + N_BUCKETS * N_HEADS].reshape(N_BUCKETS, N_HEADS),
        q_norm_g=a[3, :HEAD_DIM], k_norm_g=a[3, LANES:LANES + HEAD_DIM],
        pool_w=a[8:].reshape(len(POOL_WINDOWS), LANES, LANES))


_WEIGHT_ORDER = ("mix_norm_g", "w_in", "pool_w", "pool_scale", "q_norm_g", "k_norm_g", "rel_bias", "w_out",
                 "mlp_norm_g", "w_up", "w_down")
_BIG = ("w_in", "w_out", "w_up", "w_down")


def kernel(x, mix_norm_g, w_in, pool_w, pool_scale, q_norm_g, k_norm_g, rel_bias, w_out, mlp_norm_g, w_up, w_down, loss_target, m_mix_norm_g, m_w_in, m_pool_w, m_pool_scale, m_q_norm_g, m_k_norm_g, m_rel_bias, m_w_out, m_mlp_norm_g, m_w_up, m_w_down, v_mix_norm_g, v_w_in, v_pool_w, v_pool_scale, v_q_norm_g, v_k_norm_g, v_rel_bias, v_w_out, v_mlp_norm_g, v_w_up, v_w_down):
    w = dict(mix_norm_g=mix_norm_g, w_in=w_in, pool_w=pool_w, pool_scale=pool_scale, q_norm_g=q_norm_g,
             k_norm_g=k_norm_g, rel_bias=rel_bias, w_out=w_out, mlp_norm_g=mlp_norm_g, w_up=w_up, w_down=w_down)
    m = dict(mix_norm_g=m_mix_norm_g, w_in=m_w_in, pool_w=m_pool_w, pool_scale=m_pool_scale, q_norm_g=m_q_norm_g,
             k_norm_g=m_k_norm_g, rel_bias=m_rel_bias, w_out=m_w_out, mlp_norm_g=m_mlp_norm_g, w_up=m_w_up, w_down=m_w_down)
    v = dict(mix_norm_g=v_mix_norm_g, w_in=v_w_in, pool_w=v_pool_w, pool_scale=v_pool_scale, q_norm_g=v_q_norm_g,
             k_norm_g=v_k_norm_g, rel_bias=v_rel_bias, w_out=v_w_out, mlp_norm_g=v_mlp_norm_g, w_up=v_w_up, w_down=v_w_down)
    xc, yc, cc = _coords()

    c_idx = jnp.reshape(cc, (1,)).astype(jnp.int32)
    chip_idx = jnp.reshape(2 * xc + yc, (1,)).astype(jnp.int32)
    me = 4 * xc + 2 * yc + cc
    whole = lambda t: t.reshape(t.shape[0], t.shape[1] * t.shape[2], t.shape[3])

    placed = [_halves(p) for p in _place_shards_call([w[n] for n in _BIG], chip_idx, nch=4)]
    win_f, bias = _allgather_call(placed[:1], from_chips=True, name="weights_allgather_in",
                                  meanwhile=_bias_table_work(rel_bias))
    wsend, wrecv, in_flight, started = _gather_start_call(placed[1:], win_f)

    def mlp_weights(after):
        landed = _gather_wait_call(in_flight, wsend, wrecv, after)
        wout_f, wup_f, wdown_f = _allgather_call(landed, from_chips=False, name="weights_pair_forward")
        return whole(wout_f).reshape(-1, wout_f.shape[-1]), whole(wup_f), whole(wdown_f)

    split = []

    def on_mlp_grads(*wire_grads):
        srcs = [_halves(g) for g in wire_grads]
        lands = [lax.empty((N_DEV,) + s.shape[2:], s.dtype) for s in srcs]
        split.extend(_scatter_start_call(srcs, lands, [False] * len(srcs), "grads_scatter_start"))
        return split[4]

    loss_part, dx, big_grads, small_grads = _local_grads(
        x[0], loss_target[0], mix_norm_g + started[0, 0], whole(win_f), pool_w, pool_scale, q_norm_g, k_norm_g, rel_bias,
        mlp_norm_g, mlp_weights, on_mlp_grads, bias)
    g_in, g_out, g_up, g_down = big_grads
    gsend, grecv, srcs_thru, lands_thru, _ = split
    lands_mlp = _scatter_wait_call(srcs_thru, lands_thru, gsend, grecv, g_in[1], [False] * 3, "grads_scatter_wait")

    small_own = _pack_small(small_grads, folded=False, loss=loss_part)
    last_srcs = [_halves(g_in[1]), small_own]
    last_lands = [lax.empty((N_DEV,) + last_srcs[0].shape[2:], WIRE_DTYPE), lax.empty((N_DEV,) + small_own.shape, F32)]
    lsend, lrecv, last_srcs, last_lands, last_started = _scatter_start_call(
        last_srcs, last_lands, [False, True], "grads_scatter_start_last")
    idx = jnp.concatenate([chip_idx, c_idx] + [jnp.reshape(jnp.bitwise_xor(me, r), (1,)) for r in range(1, N_DEV)])
    idx = idx.astype(jnp.int32)
    mlp = _BIG[1:]

    def update(names, own32, lands, tag, dep=None):
        halves = _reduce_call([_halves(g) for g in own32], lands, idx, 4, "grads_reduce_" + tag, dep)
        reduced = _pair_allgather_call(list(halves), "grads_pair_allgather_" + tag)
        return _adamw_call([w[n] for n in names], reduced, [m[n] for n in names], [v[n] for n in names], 8, "adamw_" + tag)

    out_mlp = update(mlp, [g_out[0], g_up[0], g_down[0]], lands_mlp, "mlp", last_started)
    land_in, small_all = _scatter_wait_call(last_srcs, last_lands, lsend, lrecv, out_mlp[3][-1], [False, True],
                                            "grads_scatter_wait_last")
    out_in = update(_BIG[:1], [g_in[0]], [land_in], "in")
    g_pack, d_pack, m_pack, v_pack = _small_call(
        small_all, small_own, jnp.reshape(me, (1,)).astype(jnp.int32), _pack_small(w), _pack_small(m), _pack_small(v))

    grads, deltas, new_m, new_v = (_unpack_small(a) for a in (g_pack, d_pack, m_pack, v_pack))
    for k, res in enumerate((grads, deltas, new_m, new_v)):
        res[_BIG[0]] = out_in[k][0]
        for i, n in enumerate(mlp):
            res[n] = out_mlp[k][i]
    loss = g_pack[LOSS_ROW, 0]
    return (loss, dx[None], *[grads[n] for n in _WEIGHT_ORDER], *[deltas[n] for n in _WEIGHT_ORDER],
            *[new_m[n] for n in _WEIGHT_ORDER], *[new_v[n] for n in _WEIGHT_ORDER])
```

```python
import math

import jax
import jax.numpy as jnp
import numpy as np
from jax import lax
from jax.experimental import pallas as pl
from jax.experimental.pallas import tpu as pltpu

F32 = jnp.float32
MXU_DTYPE = jnp.bfloat16
WIRE_DTYPE = jnp.bfloat16

NORM_EPS = 1e-6
NEG_INF = -1e30
LANES = 128
HEAD_DIM = 64
N_HEADS = 8
POOL_WIDTH = 512
ATTN_WIDTH = 512
POOL_WINDOWS = (2, 4, 8, 16)
POOL_HALO = 16
DILATED_PATTERNS = ((128, 1), (512, 4), (2048, 16))
ATT_BLOCK = 128
ATT_SUPER = ATT_BLOCK * max(dl for _, dl in DILATED_PATTERNS)
ATT_UNITS = ATT_SUPER // ATT_BLOCK
N_BUCKETS = 32
NO_BUCKET = -1
MAX_DISTANCE = 2048
N_CHIPS = 4
N_DEV = 8
ADAM_LR, ADAM_B1, ADAM_B2, ADAM_EPS, ADAM_WD, ADAM_STEP = 0.001, 0.9, 0.999, 1e-08, 0.01, 10
VMEM_LIMIT = 56 * 1024 * 1024
MESH = pl.DeviceIdType.MESH
ANY = pl.BlockSpec(memory_space=pl.ANY)

SMALL_ROWS = 72
LOSS_ROW = 5


def _mm(a, b):
    return jnp.dot(a, b, preferred_element_type=F32)


def _mm_nt(a, b):
    return lax.dot_general(a, b, (((1,), (1,)), ((), ())), preferred_element_type=F32)


def _mm_tn(a, b):
    return lax.dot_general(a, b, (((0,), (0,)), ((), ())), preferred_element_type=F32)


def _params(sem=None, **kw):
    if sem is not None:
        kw["dimension_semantics"] = sem
    return pltpu.CompilerParams(vmem_limit_bytes=VMEM_LIMIT, **kw)


def _low_half():
    return lax.broadcasted_iota(jnp.int32, (1, LANES), 1) < HEAD_DIM


def _head_sum_bcast(y):
    lo = _low_half()
    outs = []
    for j in range(y.shape[1] // LANES):
        c = y[:, j * LANES:(j + 1) * LANES]
        s_lo = jnp.sum(jnp.where(lo, c, 0.0), axis=-1, keepdims=True)
        s_hi = jnp.sum(jnp.where(lo, 0.0, c), axis=-1, keepdims=True)
        outs.append(jnp.where(lo, s_lo, s_hi))
    return jnp.concatenate(outs, axis=-1)


def _rms_bwd(dn, hn, r):
    return r * (dn - hn * jnp.mean(dn * hn, axis=-1, keepdims=True))


def _t5_bucket_np(dist):
    max_exact = N_BUCKETS // 2
    d_f = np.maximum(dist, 1).astype(np.float32)
    ratio = (np.log(d_f / np.float32(max_exact)) / np.float32(math.log(MAX_DISTANCE / max_exact))).astype(np.float32)
    large = max_exact + (ratio * np.float32(N_BUCKETS - max_exact)).astype(np.int32)
    large = np.minimum(large, N_BUCKETS - 1)
    return np.where(dist < max_exact, dist, large).astype(np.int32)


def _window_offsets(dl):
    if dl == 1:
        return _by4_positions(ATT_BLOCK), _by4_positions(2 * ATT_BLOCK)
    return np.arange(ATT_BLOCK), np.arange(2 * ATT_BLOCK)


def _bucket_tables():
    tables = []
    for _, dl in DILATED_PATTERNS:
        qq, kk = _window_offsets(dl)
        dist = qq[:, None] + ATT_BLOCK - kk[None, :]
        bucket = _t5_bucket_np(np.clip(dist, 0, ATT_BLOCK) * dl)
        tables.append(np.where((dist >= 0) & (dist <= ATT_BLOCK), bucket, NO_BUCKET))
    return np.stack(tables).astype(np.int32)


def _previous_block_keys():
    return np.stack([np.broadcast_to(_window_offsets(dl)[1][None, :] < ATT_BLOCK, (ATT_BLOCK, 2 * ATT_BLOCK))
                     for _, dl in DILATED_PATTERNS])


def _f1_call(x, g1, win, poolw, pscale, qg, kg, tm):
    s, d = x.shape
    nblk = s // tm

    def body(x_ref, g1_ref, win_ref, pw_ref, ps_ref, qg_ref, kg_ref,
             a_ref, pooled_ref, ypool_ref, q32_ref, k32_ref, qn_ref, kn_ref, v_ref, ubuf):
        i = pl.program_id(0)
        xv = x_ref[...]
        r = lax.rsqrt(jnp.mean(xv * xv, axis=-1, keepdims=True) + NORM_EPS)
        a = ((xv * r) * g1_ref[...]).astype(MXU_DTYPE)
        a_ref[...] = a
        u = _mm(a, win_ref[0])
        q = _mm(a, win_ref[1])
        k = _mm(a, win_ref[2])
        v_ref[...] = _mm(a, win_ref[3])
        q32_ref[...] = q
        k32_ref[...] = k
        rq = lax.rsqrt(_head_sum_bcast(q * q) * (1.0 / HEAD_DIM) + NORM_EPS)
        qn_ref[...] = ((q * rq) * qg_ref[...]) * (HEAD_DIM ** -0.5)
        rk = lax.rsqrt(_head_sum_bcast(k * k) * (1.0 / HEAD_DIM) + NORM_EPS)
        kn_ref[...] = (k * rk) * kg_ref[...]

        ubuf[0:POOL_HALO, :] = jnp.where(i > 0, ubuf[tm:tm + POOL_HALO, :], 0.0)
        ubuf[POOL_HALO:POOL_HALO + tm, :] = u
        t = i * tm + lax.broadcasted_iota(jnp.int32, (tm, 1), 0)
        for g, w in enumerate(POOL_WINDOWS):
            ls = slice(g * LANES, (g + 1) * LANES)
            ug = u[:, ls]
            acc = ug
            for sh in range(1, w):
                acc = acc + ubuf[POOL_HALO - sh:POOL_HALO - sh + tm, ls]
            cnt = jnp.minimum(t + 1, w).astype(F32)
            pooled = (acc / cnt - ug).astype(MXU_DTYPE)
            pooled_ref[:, ls] = pooled
            ypool_ref[:, ls] = (_mm(pooled, pw_ref[g]) * ps_ref[:, ls]).astype(MXU_DTYPE)

    tok = lambda w: pl.BlockSpec((tm, w), lambda i: (i, 0))
    full = lambda shp: pl.BlockSpec(shp, lambda i: (0,) * len(shp))
    return pl.pallas_call(
        body, name="fwd_inproj",
        grid=(nblk,),
        in_specs=[tok(d), full((1, d)), full(win.shape), full(poolw.shape), full((1, POOL_WIDTH)),
                  full((1, ATTN_WIDTH)), full((1, ATTN_WIDTH))],
        out_specs=[tok(d), tok(POOL_WIDTH), tok(POOL_WIDTH), tok(ATTN_WIDTH), tok(ATTN_WIDTH),
                   tok(ATTN_WIDTH), tok(ATTN_WIDTH), tok(ATTN_WIDTH)],
        out_shape=[jax.ShapeDtypeStruct((s, d), MXU_DTYPE),
                   jax.ShapeDtypeStruct((s, POOL_WIDTH), MXU_DTYPE),
                   jax.ShapeDtypeStruct((s, POOL_WIDTH), MXU_DTYPE),
                   jax.ShapeDtypeStruct((s, ATTN_WIDTH), F32),
                   jax.ShapeDtypeStruct((s, ATTN_WIDTH), F32),
                   jax.ShapeDtypeStruct((s, ATTN_WIDTH), F32),
                   jax.ShapeDtypeStruct((s, ATTN_WIDTH), F32),
                   jax.ShapeDtypeStruct((s, ATTN_WIDTH), F32)],
        scratch_shapes=[pltpu.VMEM((tm + POOL_HALO, POOL_WIDTH), F32)],
        compiler_params=_params(("arbitrary",)),
    )(x, g1, win, poolw, pscale, qg, kg)


DEINT = 4
assert [dl for _, dl in DILATED_PATTERNS] == [1, DEINT, DEINT * DEINT]


def _by4_positions(n):
    pos = np.arange(n)
    return DEINT * (pos % (n // DEINT)) + pos // (n // DEINT)


def _masked_bias(b_ref, p, n):
    return b_ref[p, jnp.minimum(n, 1)].reshape(2 * ATT_BLOCK, 2 * ATT_BLOCK)


def _unit_rows(u, dl):
    sq, sk = ATT_SUPER // DEINT, 2 * ATT_SUPER // DEINT
    if dl == 1:
        n = ATT_BLOCK // DEINT
        return (u, [pl.ds(pl.multiple_of(r * sq + n * u, 8), n) for r in range(DEINT)],
                [pl.ds(pl.multiple_of(r * sk + sk // 2 + n * (u - 1), 8), 2 * n) for r in range(DEINT)])
    if dl == DEINT:
        r, b = u % DEINT, u // DEINT
        return (b, [pl.ds(pl.multiple_of(r * sq + ATT_BLOCK * b, 8), ATT_BLOCK)],
                [pl.ds(pl.multiple_of(r * sk + sk // 2 + ATT_BLOCK * (b - 1), 8), 2 * ATT_BLOCK)])
    r, a = u % DEINT, u // DEINT
    return 0, [pl.ds(r * sq + a, ATT_BLOCK, stride=DEINT)], [pl.ds(r * sk + a, 2 * ATT_BLOCK, stride=DEINT)]


def _take(ref, runs):
    parts = [ref[run, :] for run in runs]
    return parts[0] if len(parts) == 1 else jnp.concatenate(parts, axis=0)


def _put(ref, runs, value, add=False):
    n = value.shape[0] // len(runs)
    for i, run in enumerate(runs):
        part = value[i * n:(i + 1) * n]
        ref[run, :] = ref[run, :] + part if add else part


def _deinterleave(dst, src, n):
    seg = n // DEINT
    for r in range(DEINT):
        dst[r * seg:(r + 1) * seg, :] = src[pl.ds(r, seg, stride=DEINT), :]


def _deinterleave_pair(dst, prev, cur):
    seg = prev.shape[0] // DEINT
    for r in range(DEINT):
        dst[2 * r * seg:(2 * r + 1) * seg, :] = prev[pl.ds(r, seg, stride=DEINT), :]
        dst[(2 * r + 1) * seg:(2 * r + 2) * seg, :] = cur[pl.ds(r, seg, stride=DEINT), :]


def _interleave(dst, src, n, offset=0):
    seg = n // DEINT
    stride = src.shape[0] // DEINT
    for r in range(DEINT):
        dst[pl.ds(r, seg, stride=DEINT), :] = src[r * stride + offset:r * stride + offset + seg, :]


def _attn_fwd_call(qn, kn, v, bias):
    s, w = qn.shape
    nsb = s // ATT_SUPER
    npair = w // LANES

    def body(q_ref, kc_ref, kp_ref, vc_ref, vp_ref, b_ref, o_ref, lse_ref, qf, kf, vf, acc_s, m_s, l_s):
        sb = pl.program_id(1)
        _deinterleave(qf, q_ref, ATT_SUPER)
        _deinterleave_pair(kf, kp_ref, kc_ref)
        _deinterleave_pair(vf, vp_ref, vc_ref)
        lo = _low_half()
        for p, (_, dl) in enumerate(DILATED_PATTERNS):
            def unit(u, carry, p=p, dl=dl):
                b, rows_q, rows_k = _unit_rows(u, dl)
                qp = _take(qf, rows_q).astype(MXU_DTYPE)
                kcat = _take(kf, rows_k).astype(MXU_DTYPE)
                vcat = _take(vf, rows_k).astype(MXU_DTYPE)
                zero = jnp.zeros_like(qp)
                q2 = jnp.concatenate([jnp.where(lo, qp, zero), jnp.where(lo, zero, qp)], axis=0)
                sc = _mm_nt(q2, kcat) + _masked_bias(b_ref, p, sb * (ATT_UNITS // dl) + b)
                m2 = jnp.max(sc, axis=-1, keepdims=True)
                pr = jnp.exp(sc - m2)
                l2 = jnp.sum(pr, axis=-1, keepdims=True)
                acc2 = _mm(pr.astype(MXU_DTYPE), vcat)
                acc = jnp.where(lo, acc2[:ATT_BLOCK], acc2[ATT_BLOCK:])
                m = jnp.where(lo, m2[:ATT_BLOCK], m2[ATT_BLOCK:])
                l = jnp.where(lo, l2[:ATT_BLOCK], l2[ATT_BLOCK:])
                if p == 0:
                    _put(acc_s, rows_q, acc)
                    _put(m_s, rows_q, m)
                    _put(l_s, rows_q, l)
                else:
                    m_old = _take(m_s, rows_q)
                    m_new = jnp.maximum(m_old, m)
                    a_old = jnp.exp(m_old - m_new)
                    a_new = jnp.exp(m - m_new)
                    _put(acc_s, rows_q, a_old * _take(acc_s, rows_q) + a_new * acc)
                    _put(l_s, rows_q, a_old * _take(l_s, rows_q) + a_new * l)
                    _put(m_s, rows_q, m_new)
                return carry

            lax.fori_loop(0, ATT_UNITS, unit, 0, unroll=16)
        l = l_s[...]
        acc_s[...] = acc_s[...] / l
        m_s[...] = m_s[...] + jnp.log(l)
        _interleave(o_ref, acc_s, ATT_SUPER)
        _interleave(lse_ref, m_s, ATT_SUPER)

    cur = pl.BlockSpec((ATT_SUPER, LANES), lambda j, t: (t, j))
    prev = pl.BlockSpec((ATT_SUPER, LANES), lambda j, t: (jnp.maximum(t - 1, 0), j))
    bspec = pl.BlockSpec((len(DILATED_PATTERNS), 2, 2, ATT_BLOCK, 2 * ATT_BLOCK), lambda j, t: (0, 0, j, 0, 0))
    return pl.pallas_call(
        body, name="attn_fwd",
        grid=(npair, nsb),
        in_specs=[cur, cur, prev, cur, prev, bspec],
        out_specs=[cur, cur],
        out_shape=[jax.ShapeDtypeStruct((s, w), F32), jax.ShapeDtypeStruct((s, w), F32)],
        scratch_shapes=[pltpu.VMEM((ATT_SUPER, LANES), F32), pltpu.VMEM((2 * ATT_SUPER, LANES), F32),
                        pltpu.VMEM((2 * ATT_SUPER, LANES), F32), pltpu.VMEM((ATT_SUPER, LANES), F32),
                        pltpu.VMEM((ATT_SUPER, LANES), F32), pltpu.VMEM((ATT_SUPER, LANES), F32)],
        compiler_params=_params(("arbitrary", "arbitrary")),
    )(qn, kn, kn, v, v, bias)


def _attn_bwd_call(qn, kn, v, do, lse, delta, bias, dep=None):
    s, w = qn.shape
    nsb = s // ATT_SUPER
    npair = w // LANES
    deps = [] if dep is None else [dep]

    def body(q_ref, kc_ref, kp_ref, vc_ref, vp_ref, do_ref, lse_ref, dlt_ref, b_ref, *rest):
        dq_ref, dk_ref, dv_ref, db_ref, qf, kf, vf, dof, lsef, dltf, dqf, dkf, dvf = rest[len(deps):]
        step = pl.program_id(1)
        sb = nsb - 1 - step
        seg = ATT_SUPER // DEINT
        _deinterleave(qf, q_ref, ATT_SUPER)
        _deinterleave(dof, do_ref, ATT_SUPER)
        _deinterleave_pair(kf, kp_ref, kc_ref)
        _deinterleave_pair(vf, vp_ref, vc_ref)
        _deinterleave(lsef, lse_ref, ATT_SUPER)
        _deinterleave(dltf, dlt_ref, ATT_SUPER)

        db_ref[...] = jnp.where(step > 0, db_ref[...], 0.0)
        for acc in (dkf, dvf):
            for r in range(DEINT):
                this, before = pl.ds((2 * r + 1) * seg, seg), pl.ds(2 * r * seg, seg)
                acc[this, :] = jnp.where(step > 0, acc[before, :], 0.0)
                acc[before, :] = jnp.zeros((seg, LANES), F32)
        lo = _low_half()
        for p, (_, dl) in enumerate(DILATED_PATTERNS):
            def unit(u, carry, p=p, dl=dl):
                b, rows_q, rows_k = _unit_rows(u, dl)
                qp = _take(qf, rows_q).astype(MXU_DTYPE)
                dop = _take(dof, rows_q).astype(MXU_DTYPE)
                kcat = _take(kf, rows_k).astype(MXU_DTYPE)
                vcat = _take(vf, rows_k).astype(MXU_DTYPE)
                lse2 = _take(lsef, rows_q)
                dlt2 = _take(dltf, rows_q)
                zero = jnp.zeros_like(qp)
                q2 = jnp.concatenate([jnp.where(lo, qp, zero), jnp.where(lo, zero, qp)], axis=0)
                do2 = jnp.concatenate([jnp.where(lo, dop, zero), jnp.where(lo, zero, dop)], axis=0)
                lse_c = jnp.concatenate([lse2[:, 0:1], lse2[:, HEAD_DIM:HEAD_DIM + 1]], axis=0)
                dlt_c = jnp.concatenate([dlt2[:, 0:1], dlt2[:, HEAD_DIM:HEAD_DIM + 1]], axis=0)
                sc = _mm_nt(q2, kcat) + _masked_bias(b_ref, p, sb * (ATT_UNITS // dl) + b)
                pr = jnp.exp(sc - lse_c)
                ds = pr * (_mm_nt(do2, vcat) - dlt_c)
                db_ref[p] += ds.reshape(2, ATT_BLOCK, 2 * ATT_BLOCK)
                ds_c = ds.astype(MXU_DTYPE)
                dq2 = _mm(ds_c, kcat)
                dk = _mm_tn(ds_c, q2)
                dv = _mm_tn(pr.astype(MXU_DTYPE), do2)
                dq = jnp.where(lo, dq2[:ATT_BLOCK], dq2[ATT_BLOCK:])
                _put(dqf, rows_q, dq, add=p > 0)
                _put(dkf, rows_k, dk, add=True)
                _put(dvf, rows_k, dv, add=True)
                return carry

            lax.fori_loop(0, ATT_UNITS, unit, 0, unroll=16)
        _interleave(dq_ref, dqf, ATT_SUPER)
        _interleave(dk_ref, dkf, ATT_SUPER, offset=seg)
        _interleave(dv_ref, dvf, ATT_SUPER, offset=seg)

    cur = pl.BlockSpec((ATT_SUPER, LANES), lambda j, t: (nsb - 1 - t, j))
    prev = pl.BlockSpec((ATT_SUPER, LANES), lambda j, t: (jnp.maximum(nsb - 2 - t, 0), j))
    npat = len(DILATED_PATTERNS)
    bspec = pl.BlockSpec((npat, 2, 2, ATT_BLOCK, 2 * ATT_BLOCK), lambda j, t: (0, 0, j, 0, 0))
    dbspec = pl.BlockSpec((npat, 2, ATT_BLOCK, 2 * ATT_BLOCK), lambda j, t: (0, j, 0, 0))
    sup = lambda: pltpu.VMEM((ATT_SUPER, LANES), F32)
    sup2 = lambda: pltpu.VMEM((2 * ATT_SUPER, LANES), F32)
    return pl.pallas_call(
        body, name="attn_bwd",
        grid=(npair, nsb),
        in_specs=[cur, cur, prev, cur, prev, cur, cur, cur, bspec] + [ANY] * len(deps),
        out_specs=[cur, cur, cur, dbspec],
        out_shape=[jax.ShapeDtypeStruct((s, w), F32)] * 3
        + [jax.ShapeDtypeStruct((npat, N_HEADS, ATT_BLOCK, 2 * ATT_BLOCK), F32)],
        scratch_shapes=[sup(), sup2(), sup2(), sup(), sup(), sup(), sup(), sup2(), sup2()],
        compiler_params=_params(("arbitrary", "arbitrary")),
    )(qn, kn, kn, v, v, do, lse, delta, bias, *deps)


def _bias_table_work(rel_bias):
    buckets = jnp.asarray(_bucket_tables())
    prev_keys = jnp.asarray(_previous_block_keys().astype(np.int32))
    npat = buckets.shape[0]

    def body(rb_ref, bk_ref, pk_ref, out_ref):
        for p in range(npat):
            for half in range(2):
                ks = slice(half * ATT_BLOCK, (half + 1) * ATT_BLOCK)
                bk = bk_ref[p, :, ks]
                absent = pk_ref[p, :, ks] != 0
                for h in range(N_HEADS):
                    def pick(b, acc, h=h, bk=bk):
                        return jnp.where(bk == b, rb_ref[b, h], acc)

                    tab = lax.fori_loop(0, N_BUCKETS, pick, jnp.full((ATT_BLOCK, ATT_BLOCK), NEG_INF, F32))
                    out_ref[p, 1, h, :, ks] = tab
                    out_ref[p, 0, h, :, ks] = jnp.where(absent, NEG_INF, tab)

    vmem = pl.BlockSpec(memory_space=pltpu.VMEM)
    return ([rel_bias, buckets, prev_keys], [pl.BlockSpec(memory_space=pltpu.SMEM), vmem, vmem],
            jax.ShapeDtypeStruct((npat, 2, N_HEADS, ATT_BLOCK, 2 * ATT_BLOCK), F32), body)


def _rel_bias_grad_call(dbias, buckets):
    npat, nh = dbias.shape[0], dbias.shape[1]

    def body(db_ref, bk_ref, out_ref):
        lane = lax.broadcasted_iota(jnp.int32, (nh, LANES), 1)
        out = jnp.zeros((nh, LANES), F32)
        for b in range(N_BUCKETS):
            tot = jnp.zeros((nh, 1), F32)
            for p in range(npat):
                hit = jnp.where(bk_ref[p][None] == b, db_ref[p], 0.0)
                tot = tot + jnp.sum(jnp.sum(hit, axis=1), axis=-1, keepdims=True)
            out = jnp.where(lane == b, tot, out)
        out_ref[...] = out

    return pl.pallas_call(
        body, name="rel_bias_grad",
        out_shape=jax.ShapeDtypeStruct((nh, LANES), F32),
        compiler_params=_params(),
    )(dbias, buckets)


def _f2_call(x, tgt, ypool, o, wout, wup, wdown, g2, tm):
    s, d = x.shape
    nblk = s // tm
    nch, _, fch = wup.shape
    dff = nch * fch
    mixw = POOL_WIDTH + ATTN_WIDTH

    def body(x_ref, t_ref, yp_ref, o_ref, g2_ref, wout_hbm, wup_hbm, wdown_hbm,
             mixed_ref, c_ref, ff_ref, dz_ref, dy_ref, dh1_ref, dyp_ref, do_ref, dlt_ref, dg2_ref, loss_ref,
             wout_v, wup_v, wdown_v, rz):
        i = pl.program_id(0)

        @pl.when(i == 0)
        def _():
            pltpu.sync_copy(wout_hbm, wout_v)
            pltpu.sync_copy(wup_hbm, wup_v)
            pltpu.sync_copy(wdown_hbm, wdown_v)
            dg2_ref[...] = jnp.zeros(dg2_ref.shape, F32)
            loss_ref[...] = jnp.zeros(loss_ref.shape, F32)

        o = o_ref[...]
        mixed = jnp.concatenate([yp_ref[...], o.astype(MXU_DTYPE)], axis=-1)
        mixed_ref[...] = mixed
        h1 = x_ref[...] + _mm(mixed, wout_v[...])
        r2 = lax.rsqrt(jnp.mean(h1 * h1, axis=-1, keepdims=True) + NORM_EPS)
        hn = h1 * r2
        c = (hn * g2_ref[...]).astype(MXU_DTYPE)
        c_ref[...] = c
        y = h1
        for j in range(nch):
            cs = slice(j * fch, (j + 1) * fch)
            z = jnp.maximum(_mm(c, wup_v[j]), 0.0)
            rz[:, cs] = z
            ff = (z * z).astype(MXU_DTYPE)
            ff_ref[:, cs] = ff
            y = y + _mm(ff, wdown_v[j])
        err = y - t_ref[...]
        loss_ref[...] += jnp.sum(err * err) * (0.5 / d)
        dy = err * (1.0 / d)
        dy_c = dy.astype(MXU_DTYPE)
        dy_ref[...] = dy_c
        dc = jnp.zeros((tm, d), F32)
        for j in range(nch):
            cs = slice(j * fch, (j + 1) * fch)
            dz = (_mm_nt(dy_c, wdown_v[j]) * (2.0 * rz[:, cs])).astype(MXU_DTYPE)
            dz_ref[:, cs] = dz
            dc = dc + _mm_nt(dz, wup_v[j])
        dg2_ref[...] += jnp.sum(dc * hn, axis=0, keepdims=True)
        dh1 = dy + _rms_bwd(dc * g2_ref[...], hn, r2)
        dh1_ref[...] = dh1
        dmix = _mm_nt(dh1.astype(MXU_DTYPE), wout_v[...])
        dyp_ref[...] = dmix[:, :POOL_WIDTH]
        do = dmix[:, POOL_WIDTH:]
        do_ref[...] = do
        dlt_ref[...] = _head_sum_bcast(do * o)

    tok = lambda w: pl.BlockSpec((tm, w), lambda i: (i, 0))
    const = lambda shp: pl.BlockSpec(shp, lambda i: (0,) * len(shp))
    return pl.pallas_call(
        body, name="fwd_mlp_bwd_mlp",
        grid=(nblk,),
        in_specs=[tok(d), tok(d), tok(POOL_WIDTH), tok(ATTN_WIDTH), const((1, d)), ANY, ANY, ANY],
        out_specs=[tok(mixw), tok(d), tok(dff), tok(dff), tok(d), tok(d), tok(POOL_WIDTH), tok(ATTN_WIDTH),
                   tok(ATTN_WIDTH), const((1, d)), const((1, LANES))],
        out_shape=[jax.ShapeDtypeStruct((s, mixw), MXU_DTYPE),
                   jax.ShapeDtypeStruct((s, d), MXU_DTYPE),
                   jax.ShapeDtypeStruct((s, dff), MXU_DTYPE),
                   jax.ShapeDtypeStruct((s, dff), MXU_DTYPE),
                   jax.ShapeDtypeStruct((s, d), MXU_DTYPE),
                   jax.ShapeDtypeStruct((s, d), F32),
                   jax.ShapeDtypeStruct((s, POOL_WIDTH), F32),
                   jax.ShapeDtypeStruct((s, ATTN_WIDTH), F32),
                   jax.ShapeDtypeStruct((s, ATTN_WIDTH), F32),
                   jax.ShapeDtypeStruct((1, d), F32),
                   jax.ShapeDtypeStruct((1, LANES), F32)],
        scratch_shapes=[pltpu.VMEM(wout.shape, MXU_DTYPE), pltpu.VMEM(wup.shape, MXU_DTYPE),
                        pltpu.VMEM(wdown.shape, MXU_DTYPE), pltpu.VMEM((tm, dff), F32)],
        compiler_params=_params(("arbitrary",)),
    )(x, tgt, ypool, o, g2, wout, wup, wdown)


def _bproj_call(dqn, dkn, dv, q32, k32, dypool, pooled, x, dh1, win, poolw, pscale, qg, kg, g1, tm):
    s, d = x.shape
    nblk = s // tm
    ngrp = len(POOL_WINDOWS)

    def body(dqn_ref, dkn_ref, dv_ref, q_ref, k_ref, dyp_ref, pooled_ref, x_ref, dh1_ref,
             win_hbm, pw_ref, ps_ref, qg_ref, kg_ref, g1_ref,
             dx_ref, dproj_ref, dg1_ref, dqg_ref, dkg_ref, dpw_ref, dps_ref, win_v, ebuf):
        step = pl.program_id(0)
        i = nblk - 1 - step

        @pl.when(step == 0)
        def _():
            pltpu.sync_copy(win_hbm, win_v)
            dg1_ref[...] = jnp.zeros(dg1_ref.shape, F32)
            dqg_ref[...] = jnp.zeros(dqg_ref.shape, F32)
            dkg_ref[...] = jnp.zeros(dkg_ref.shape, F32)
            dpw_ref[...] = jnp.zeros(dpw_ref.shape, F32)
            dps_ref[...] = jnp.zeros(dps_ref.shape, F32)
            ebuf[tm:tm + POOL_HALO, :] = jnp.zeros((POOL_HALO, POOL_WIDTH), F32)

        @pl.when(step > 0)
        def _():
            ebuf[tm:tm + POOL_HALO, :] = ebuf[0:POOL_HALO, :]

        def qk_bwd(dn_sum, raw, gain, scale, dgain_ref):
            rr = lax.rsqrt(_head_sum_bcast(raw * raw) * (1.0 / HEAD_DIM) + NORM_EPS)
            hn = raw * rr
            dgain_ref[...] += jnp.sum(dn_sum * hn, axis=0, keepdims=True) * scale
            dn = dn_sum * (gain * scale)
            return rr * (dn - hn * (_head_sum_bcast(dn * hn) * (1.0 / HEAD_DIM)))

        dq = qk_bwd(dqn_ref[...], q_ref[...], qg_ref[...], HEAD_DIM ** -0.5, dqg_ref)
        dk = qk_bwd(dkn_ref[...], k_ref[...], kg_ref[...], 1.0, dkg_ref)

        t = i * tm + lax.broadcasted_iota(jnp.int32, (tm, 1), 0)
        dpooled = []
        for g, w in enumerate(POOL_WINDOWS):
            ls = slice(g * LANES, (g + 1) * LANES)
            dm = dyp_ref[:, ls]
            pg = pooled_ref[:, ls]
            dps_ref[:, ls] += jnp.sum(dm * _mm(pg, pw_ref[g]), axis=0, keepdims=True)
            dms = (dm * ps_ref[:, ls]).astype(MXU_DTYPE)
            dpw_ref[g] += _mm_tn(pg, dms)
            dpg = _mm_nt(dms, pw_ref[g])
            dpooled.append(dpg)
            ebuf[0:tm, ls] = dpg / jnp.minimum(t + 1, w).astype(F32)
        du = []
        for g, w in enumerate(POOL_WINDOWS):
            ls = slice(g * LANES, (g + 1) * LANES)
            acc = ebuf[0:tm, ls]
            for sh in range(1, w):
                acc = acc + ebuf[sh:sh + tm, ls]
            du.append(acc - dpooled[g])
        parts = [jnp.concatenate(du, axis=-1), dq, dk, dv_ref[...]]
        da = jnp.zeros((tm, d), F32)
        for p, part in enumerate(parts):
            pc = part.astype(MXU_DTYPE)
            dproj_ref[:, p * POOL_WIDTH:(p + 1) * POOL_WIDTH] = pc
            da = da + _mm_nt(pc, win_v[p])
        xv = x_ref[...]
        r = lax.rsqrt(jnp.mean(xv * xv, axis=-1, keepdims=True) + NORM_EPS)
        xn = xv * r
        dg1_ref[...] += jnp.sum(da * xn, axis=0, keepdims=True)
        dx_ref[...] = dh1_ref[...] + _rms_bwd(da * g1_ref[...], xn, r)

    tok = lambda w: pl.BlockSpec((tm, w), lambda t: (nblk - 1 - t, 0))
    const = lambda shp: pl.BlockSpec(shp, lambda t: (0,) * len(shp))
    return pl.pallas_call(
        body, name="bwd_inproj",
        grid=(nblk,),
        in_specs=[tok(ATTN_WIDTH)] * 5 + [tok(POOL_WIDTH), tok(POOL_WIDTH), tok(d), tok(d),
                                          ANY, const(poolw.shape), const((1, POOL_WIDTH)), const((1, ATTN_WIDTH)),
                                          const((1, ATTN_WIDTH)), const((1, d))],
        out_specs=[tok(d), tok(4 * POOL_WIDTH), const((1, d)), const((1, ATTN_WIDTH)), const((1, ATTN_WIDTH)),
                   const((ngrp, LANES, LANES)), const((1, POOL_WIDTH))],
        out_shape=[jax.ShapeDtypeStruct((s, d), F32),
                   jax.ShapeDtypeStruct((s, 4 * POOL_WIDTH), MXU_DTYPE),
                   jax.ShapeDtypeStruct((1, d), F32),
                   jax.ShapeDtypeStruct((1, ATTN_WIDTH), F32),
                   jax.ShapeDtypeStruct((1, ATTN_WIDTH), F32),
                   jax.ShapeDtypeStruct((ngrp, LANES, LANES), F32),
                   jax.ShapeDtypeStruct((1, POOL_WIDTH), F32)],
        scratch_shapes=[pltpu.VMEM(win.shape, MXU_DTYPE), pltpu.VMEM((tm + POOL_HALO, POOL_WIDTH), F32)],
        compiler_params=_params(("arbitrary",)),
    )(dqn, dkn, dv, q32, k32, dypool, pooled, x, dh1, win, poolw, pscale, qg, kg, g1)


def _wgrad_call(a, b, bm, bn, bk, out_shape, out_block, out_index, name):
    s, m = a.shape
    _, n = b.shape
    nk = s // bk

    def body(a_ref, b_ref, o_ref, wire_ref):
        k = pl.program_id(2)
        acc = jnp.where(k > 0, o_ref[...], 0.0) + _mm_tn(a_ref[...].astype(MXU_DTYPE), b_ref[...].astype(MXU_DTYPE))
        o_ref[...] = acc
        wire_ref[...] = acc.astype(WIRE_DTYPE)

    return pl.pallas_call(
        body, name=name,
        grid=(m // bm, n // bn, nk),
        in_specs=[pl.BlockSpec((bk, bm), lambda i, j, k: (k, i)), pl.BlockSpec((bk, bn), lambda i, j, k: (k, j))],
        out_specs=[pl.BlockSpec(out_block, out_index)] * 2,
        out_shape=[jax.ShapeDtypeStruct(out_shape, F32), jax.ShapeDtypeStruct(out_shape, WIRE_DTYPE)],
        compiler_params=_params(("arbitrary", "arbitrary", "arbitrary")),
    )(a, b)


def _local_grads(x, tgt, g1, win, poolw, pscale, qg, kg, bias, g2, mlp_weights, on_mlp_grads=None):
    s, d = x.shape
    g1r, g2r = g1.reshape(1, d), g2.reshape(1, d)
    psr = pscale.reshape(1, POOL_WIDTH)
    qgr = jnp.tile(qg, N_HEADS).reshape(1, ATTN_WIDTH)
    kgr = jnp.tile(kg, N_HEADS).reshape(1, ATTN_WIDTH)
    pw_c = poolw.astype(MXU_DTYPE)
    buckets = jnp.asarray(_bucket_tables())
    bk = min(s, 4096)

    a, pooled, ypool, q32, k32, qn, kn, v = _f1_call(x, g1r, win, pw_c, psr, qgr, kgr, tm=512)
    o, lse = _attn_fwd_call(qn, kn, v, bias)
    wout, wup, wdown = mlp_weights(o)
    mixed, c, ff, dz, dy, dh1, dypool, do, delta, dg2, loss = _f2_call(x, tgt, ypool, o, wout, wup, wdown, g2r, tm=256)
    dff = ff.shape[1]
    g_out = [g.reshape(N_CHIPS, d // N_CHIPS, d)
             for g in _wgrad_call(mixed, dh1, d, d, bk // 4, (d, d), (d, d), lambda i, j, k: (0, 0), "wgrad_out")]
    g_up = _wgrad_call(c, dz, d, dff // N_CHIPS, bk, (N_CHIPS, d, dff // N_CHIPS), (None, d, dff // N_CHIPS),
                       lambda i, j, k: (j, 0, 0), "wgrad_up")
    g_down = _wgrad_call(ff, dy, dff // N_CHIPS, d, bk, (N_CHIPS, dff // N_CHIPS, d), (None, dff // N_CHIPS, d),
                         lambda i, j, k: (i, 0, 0), "wgrad_down")
    dep = None if on_mlp_grads is None else on_mlp_grads(g_out[1], g_up[1], g_down[1])
    dqn, dkn, dv, dbias = _attn_bwd_call(qn, kn, v, do, lse, delta, bias, dep)
    dx, dproj, dg1, dqg, dkg, dpw, dps = _bproj_call(
        dqn, dkn, dv, q32, k32, dypool, pooled, x, dh1, win, pw_c, psr, qgr, kgr, g1r, tm=512)
    nin = dproj.shape[1] // N_CHIPS
    g_in = _wgrad_call(a, dproj, d, nin, bk, (N_CHIPS, d, nin), (None, d, nin), lambda i, j, k: (j, 0, 0), "wgrad_in")
    drb = _rel_bias_grad_call(dbias, buckets)
    small = dict(
        mix_norm_g=dg1.reshape(d), mlp_norm_g=dg2.reshape(d), pool_scale=dps.reshape(POOL_WIDTH),
        q_norm_g=dqg.reshape(ATTN_WIDTH), k_norm_g=dkg.reshape(ATTN_WIDTH),
        rel_bias=drb[:, :N_BUCKETS].T, pool_w=dpw)
    return loss[0, 0], dx, (g_in, g_out, g_up, g_down), small


def _coords():
    return lax.axis_index("x"), lax.axis_index("y"), lax.axis_index("c")


def _other_chips(x, y):
    return [(1 - x, y), (x, 1 - y), (1 - x, 1 - y)]


def _remote(src, dst, send_sem, recv_sem, dev):
    return pltpu.make_async_remote_copy(src_ref=src, dst_ref=dst, send_sem=send_sem, recv_sem=recv_sem,
                                        device_id=dev, device_id_type=MESH)


def _halves(a):
    return a.reshape(a.shape[:-2] + (2, a.shape[-2] // 2, a.shape[-1]))


def _place_shards_call(shards, chip_idx, nch):
    nw = len(shards)

    def body(chip_ref, *refs):
        for w in range(nw):
            refs[nw + w][...] = refs[w][...].astype(WIRE_DTYPE)

    in_specs = [pl.BlockSpec((s.shape[0] // nch, s.shape[1]), lambda i, chip_ref: (i, 0)) for s in shards]
    out_specs = [pl.BlockSpec((None, s.shape[0] // nch, s.shape[1]), lambda i, chip_ref: (chip_ref[0], i, 0))
                 for s in shards]
    return pl.pallas_call(
        body, name="weights_place",
        grid_spec=pltpu.PrefetchScalarGridSpec(num_scalar_prefetch=1, grid=(nch,),
                                               in_specs=in_specs, out_specs=out_specs),
        out_shape=[jax.ShapeDtypeStruct((N_CHIPS,) + s.shape, WIRE_DTYPE) for s in shards],
        compiler_params=_params(("arbitrary",)),
    )(chip_idx, *shards)


def _allgather_call(placed, from_chips, name, meanwhile=None):
    nw = len(placed)
    ncp = 3 * nw
    extra, extra_specs, extra_shape, extra_body = meanwhile if meanwhile else ([], [], None, None)
    ne = len(extra)

    def body(*refs):
        outs = refs[nw + ne:2 * nw + ne]
        send1, recv1, send2, recv2 = refs[-4:]
        x, y, c = _coords()
        chip = 2 * x + y
        others = _other_chips(x, y)
        first, passed = [], []
        if from_chips:
            for w in range(nw):
                for k, (ox, oy) in enumerate(others):
                    mine = outs[w].at[chip, c]
                    cp = _remote(mine, mine, send1.at[3 * w + k], recv1.at[3 * w + k], (ox, oy, c))
                    cp.start()
                    first.append(cp)
        if meanwhile:
            extra_body(*refs[nw:nw + ne], refs[2 * nw + ne])
        for w in range(nw):
            for k, (ox, oy) in enumerate(others):
                piece = outs[w].at[2 * ox + oy, c]
                if from_chips:
                    _remote(piece, piece, send1.at[3 * w + k], recv1.at[3 * w + k], (ox, oy, c)).wait_recv()
                cp = _remote(piece, piece, send2.at[3 * w + k], recv2.at[3 * w + k], (x, y, 1 - c))
                cp.start()
                passed.append(cp)
        for w in range(nw):
            for k, (ox, oy) in enumerate(others):
                piece = outs[w].at[2 * ox + oy, 1 - c]
                _remote(piece, piece, send2.at[3 * w + k], recv2.at[3 * w + k], (x, y, 1 - c)).wait_recv()
        for cp in first + passed:
            cp.wait_send()

    return pl.pallas_call(
        body, name=name,
        in_specs=[ANY] * nw + list(extra_specs),
        out_specs=[ANY] * nw + ([pl.BlockSpec(memory_space=pltpu.VMEM)] if meanwhile else []),
        out_shape=[jax.ShapeDtypeStruct(s.shape, s.dtype) for s in placed] + ([extra_shape] if meanwhile else []),
        input_output_aliases={w: w for w in range(nw)},
        scratch_shapes=[pltpu.SemaphoreType.DMA((ncp,))] * 4,
        compiler_params=_params(),
    )(*placed, *extra)


HBM_SPEC = pl.BlockSpec(memory_space=pltpu.HBM)
SEM_SPEC = pl.BlockSpec(memory_space=pltpu.SEMAPHORE)
SPLIT_EFFECT = pltpu.SideEffectType.DATAFLOW_SIDE_EFFECTING


def _in_hbm(a):
    return pltpu.with_memory_space_constraint(a, pltpu.HBM)


def _gather_copies(bufs, send, recv):
    x, y, c = _coords()
    chip = 2 * x + y
    cps = []
    for w, buf in enumerate(bufs):
        for k, (ox, oy) in enumerate(_other_chips(x, y)):
            mine, theirs = buf.at[chip, c], buf.at[2 * ox + oy, c]
            sems = (send.at[3 * w + k], recv.at[3 * w + k], (ox, oy, c))
            cps.append((_remote(mine, mine, *sems), _remote(theirs, theirs, *sems)))
    return cps


def _gather_start_call(bufs, after):
    nw = len(bufs)

    def body(*refs):
        ins, send, recv, token = refs[:nw], refs[nw + 1], refs[nw + 2], refs[2 * nw + 3]
        for out, _ in _gather_copies(ins, send, recv):
            out.start()
        token[...] = jnp.zeros(token.shape, F32)

    res = pl.pallas_call(
        body, name="weights_gather_start",
        in_specs=[HBM_SPEC] * nw + [ANY],
        out_specs=[SEM_SPEC, SEM_SPEC] + [HBM_SPEC] * nw + [pl.BlockSpec(memory_space=pltpu.VMEM)],
        out_shape=[pltpu.SemaphoreType.DMA((3 * nw,)), pltpu.SemaphoreType.DMA((3 * nw,))]
        + [pltpu.HBM(b.shape, b.dtype) for b in bufs] + [jax.ShapeDtypeStruct((8, LANES), F32)],
        input_output_aliases={w: 2 + w for w in range(nw)},
        compiler_params=pltpu.CompilerParams(has_side_effects=SPLIT_EFFECT),
    )(*[_in_hbm(b) for b in bufs], after)
    return res[0], res[1], list(res[2:2 + nw]), res[2 + nw]


def _gather_wait_call(bufs, send, recv, after):
    nw = len(bufs)

    def body(*refs):
        ins, send, recv = refs[:nw], refs[nw], refs[nw + 1]
        for out, back in _gather_copies(ins, send, recv):
            out.wait_send()
            back.wait_recv()

    return pl.pallas_call(
        body, name="weights_gather_wait",
        in_specs=[HBM_SPEC] * nw + [SEM_SPEC, SEM_SPEC, ANY],
        out_specs=[HBM_SPEC] * nw,
        out_shape=[pltpu.HBM(b.shape, b.dtype) for b in bufs],
        input_output_aliases={w: w for w in range(nw)},
        compiler_params=pltpu.CompilerParams(has_side_effects=SPLIT_EFFECT),
    )(*bufs, send, recv, after)


def _scatter_copies(srcs, lands, send, recv, wholes):
    x, y, c = _coords()
    me = 4 * x + 2 * y + c
    cps = []
    for w, (src, land) in enumerate(zip(srcs, lands)):
        for r in range(1, N_DEV):
            px, py, pc = ((1 - x) if r & 4 else x, (1 - y) if r & 2 else y, (1 - c) if r & 1 else c)
            sems = (send.at[(N_DEV - 1) * w + r - 1], recv.at[(N_DEV - 1) * w + r - 1], (px, py, pc))
            piece = src if wholes[w] else src.at[2 * px + py, pc]
            cps.append((_remote(piece, land.at[me], *sems), _remote(piece, land.at[4 * px + 2 * py + pc], *sems)))
    return cps


def _scatter_start_call(srcs, lands, wholes, name):
    nw = len(srcs)
    ncp = (N_DEV - 1) * nw

    def body(*refs):
        ins, lnd, send, recv, token = refs[:nw], refs[nw:2 * nw], refs[2 * nw], refs[2 * nw + 1], refs[4 * nw + 2]
        for out, _ in _scatter_copies(ins, lnd, send, recv, wholes):
            out.start()
        token[...] = jnp.zeros(token.shape, F32)

    res = pl.pallas_call(
        body, name=name,
        in_specs=[HBM_SPEC] * (2 * nw),
        out_specs=[SEM_SPEC, SEM_SPEC] + [HBM_SPEC] * (2 * nw) + [pl.BlockSpec(memory_space=pltpu.VMEM)],
        out_shape=[pltpu.SemaphoreType.DMA((ncp,)), pltpu.SemaphoreType.DMA((ncp,))]
        + [pltpu.HBM(b.shape, b.dtype) for b in list(srcs) + list(lands)] + [jax.ShapeDtypeStruct((8, LANES), F32)],
        input_output_aliases={i: 2 + i for i in range(2 * nw)},
        compiler_params=pltpu.CompilerParams(has_side_effects=SPLIT_EFFECT),
    )(*[_in_hbm(b) for b in list(srcs) + list(lands)])
    return res[0], res[1], list(res[2:2 + nw]), list(res[2 + nw:2 + 2 * nw]), res[2 + 2 * nw]


def _scatter_wait_call(srcs, lands, send, recv, after, wholes, name):
    nw = len(srcs)

    def body(*refs):
        ins, lnd, send, recv = refs[:nw], refs[nw:2 * nw], refs[2 * nw], refs[2 * nw + 1]
        for out, back in _scatter_copies(ins, lnd, send, recv, wholes):
            out.wait_send()
            back.wait_recv()

    res = pl.pallas_call(
        body, name=name,
        in_specs=[HBM_SPEC] * (2 * nw) + [SEM_SPEC, SEM_SPEC, ANY],
        out_specs=[HBM_SPEC] * (2 * nw),
        out_shape=[pltpu.HBM(b.shape, b.dtype) for b in list(srcs) + list(lands)],
        input_output_aliases={i: i for i in range(2 * nw)},
        compiler_params=pltpu.CompilerParams(has_side_effects=SPLIT_EFFECT),
    )(*srcs, *lands, send, recv, after)
    return list(res[nw:])


def _reduce_call(own, lands, idx, nch, name, dep=None):
    nw = len(own)
    deps = [] if dep is None else [dep]

    def body(idx_ref, *refs):
        refs = refs[:2 * nw] + refs[2 * nw + len(deps):]
        for w in range(nw):
            tot = refs[w][...]
            for r in range(1, N_DEV):
                tot = tot + refs[nw + w][idx_ref[1 + r]].astype(F32)
            refs[2 * nw + w][...] = tot

    in_specs, out_specs, out_shape = [], [], []
    for s in own:
        in_specs.append(pl.BlockSpec((None, None, s.shape[2] // nch, s.shape[3]),
                                     lambda i, idx_ref: (idx_ref[0], idx_ref[1], i, 0)))
    for s in own:
        in_specs.append(pl.BlockSpec((N_DEV, s.shape[2] // nch, s.shape[3]), lambda i, idx_ref: (0, i, 0)))
    for s in own:
        out_specs.append(pl.BlockSpec((None, s.shape[2] // nch, s.shape[3]), lambda i, idx_ref: (idx_ref[1], i, 0)))
        out_shape.append(jax.ShapeDtypeStruct((2,) + s.shape[2:], F32))
    return pl.pallas_call(
        body, name=name,
        grid_spec=pltpu.PrefetchScalarGridSpec(num_scalar_prefetch=1, grid=(nch,),
                                               in_specs=in_specs + [ANY] * len(deps), out_specs=out_specs),
        out_shape=out_shape,
        compiler_params=_params(("arbitrary",)),
    )(idx, *own, *lands, *deps)


def _pair_allgather_call(halves, name):
    nw = len(halves)

    def body(*refs):
        outs = refs[nw:2 * nw]
        send, recv = refs[2 * nw:]
        x, y, c = _coords()
        cps = []
        for w in range(nw):
            cp = _remote(outs[w].at[c], outs[w].at[c], send.at[w], recv.at[w], (x, y, 1 - c))
            cp.start()
            cps.append(cp)
        for w in range(nw):
            theirs = outs[w].at[1 - c]
            _remote(theirs, theirs, send.at[w], recv.at[w], (x, y, 1 - c)).wait_recv()
        for cp in cps:
            cp.wait_send()

    outs = pl.pallas_call(
        body, name=name,
        in_specs=[ANY] * nw, out_specs=[ANY] * nw,
        out_shape=[jax.ShapeDtypeStruct(h.shape, h.dtype) for h in halves],
        input_output_aliases={w: w for w in range(nw)},
        scratch_shapes=[pltpu.SemaphoreType.DMA((nw,))] * 2,
    )(*halves)
    return [o.reshape(2 * h.shape[1], h.shape[2]) for o, h in zip(outs, halves)]


def _adamw(w, g, m, v):
    m = ADAM_B1 * m + (1.0 - ADAM_B1) * g
    v = ADAM_B2 * v + (1.0 - ADAM_B2) * (g * g)
    m_hat = m / (1.0 - ADAM_B1 ** ADAM_STEP)
    v_hat = v / (1.0 - ADAM_B2 ** ADAM_STEP)
    delta = -ADAM_LR * (m_hat / (jnp.sqrt(v_hat) + ADAM_EPS) + ADAM_WD * w)
    return delta, m, v


def _adamw_call(ws, gs, ms, vs, nch, name):
    nw = len(ws)

    def body(*refs):
        for w in range(nw):
            g = refs[nw + w][...]
            delta, m, v = _adamw(refs[w][...], g, refs[2 * nw + w][...], refs[3 * nw + w][...])
            refs[4 * nw + w][...] = g
            refs[5 * nw + w][...] = delta
            refs[6 * nw + w][...] = m
            refs[7 * nw + w][...] = v

    specs = [pl.BlockSpec((a.shape[0] // nch, a.shape[1]), lambda i: (i, 0)) for a in ws]
    res = pl.pallas_call(
        body, name=name,
        grid=(nch,),
        in_specs=specs * 4, out_specs=specs * 4,
        out_shape=[jax.ShapeDtypeStruct(a.shape, F32) for a in ws] * 4,
        compiler_params=_params(("arbitrary",)),
    )(*ws, *gs, *ms, *vs)
    return res[:nw], res[nw:2 * nw], res[2 * nw:3 * nw], res[3 * nw:]


def _small_call(gathered, own, me_idx, w, m, v):
    def fold(row):
        tot = row[:, 0:LANES] + row[:, LANES:2 * LANES] + row[:, 2 * LANES:3 * LANES] + row[:, 3 * LANES:4 * LANES]
        return tot + pltpu.roll(tot, HEAD_DIM, axis=1)

    def body(me_ref, ga_ref, own_ref, w_ref, m_ref, v_ref, g_out, d_out, m_out, v_out):
        me = me_ref[0]
        term = lambda i: jnp.where(me == i, own_ref[...], ga_ref[i])
        g = term(0)
        for i in range(1, N_DEV):
            g = g + term(i)
        unfolded = g[4:5, :]
        folded = jnp.concatenate([fold(unfolded[:, :ATTN_WIDTH]), fold(unfolded[:, ATTN_WIDTH:]),
                                  jnp.zeros((1, 1024 - 2 * LANES), F32)], axis=-1)
        row = lax.broadcasted_iota(jnp.int32, g.shape, 0)
        g = jnp.where(row == 3, folded, g)
        delta, mm, vv = _adamw(w_ref[...], g, m_ref[...], v_ref[...])
        g_out[...] = g
        d_out[...] = delta
        m_out[...] = mm
        v_out[...] = vv

    vmem = pl.BlockSpec(memory_space=pltpu.VMEM)
    return pl.pallas_call(
        body, name="adamw_small",
        in_specs=[pl.BlockSpec(memory_space=pltpu.SMEM)] + [vmem] * 5,
        out_shape=[jax.ShapeDtypeStruct(w.shape, F32)] * 4,
        compiler_params=_params(),
    )(me_idx, gathered, own, w, m, v)


def _pack_small(p, folded=True, loss=None):
    z = lambda n: jnp.zeros((n,), F32)
    rows = [p["mix_norm_g"], p["mlp_norm_g"],
            jnp.concatenate([p["pool_scale"], p["rel_bias"].reshape(-1), z(1024 - POOL_WIDTH - N_BUCKETS * N_HEADS)])]
    if folded:
        rows += [jnp.concatenate([p["q_norm_g"], z(LANES - HEAD_DIM), p["k_norm_g"], z(1024 - LANES - HEAD_DIM)]), z(1024)]
    else:
        rows += [z(1024), jnp.concatenate([p["q_norm_g"], p["k_norm_g"]])]
    rows += [z(1024) if loss is None else jnp.concatenate([loss.reshape(1), z(1023)])]
    head = jnp.stack(rows + [z(1024)] * 2)
    return jnp.concatenate([head, p["pool_w"].reshape(-1, 1024)], axis=0)


def _unpack_small(a):
    return dict(
        mix_norm_g=a[0], mlp_norm_g=a[1], pool_scale=a[2, :POOL_WIDTH],
        rel_bias=a[2, POOL_WIDTH:POOL_WIDTH + N_BUCKETS * N_HEADS].reshape(N_BUCKETS, N_HEADS),
        q_norm_g=a[3, :HEAD_DIM], k_norm_g=a[3, LANES:LANES + HEAD_DIM],
        pool_w=a[8:].reshape(len(POOL_WINDOWS), LANES, LANES))


_WEIGHT_ORDER = ("mix_norm_g", "w_in", "pool_w", "pool_scale", "q_norm_g", "k_norm_g", "rel_bias", "w_out",
                 "mlp_norm_g", "w_up", "w_down")
_BIG = ("w_in", "w_out", "w_up", "w_down")


def kernel(x, mix_norm_g, w_in, pool_w, pool_scale, q_norm_g, k_norm_g, rel_bias, w_out, mlp_norm_g, w_up, w_down, loss_target, m_mix_norm_g, m_w_in, m_pool_w, m_pool_scale, m_q_norm_g, m_k_norm_g, m_rel_bias, m_w_out, m_mlp_norm_g, m_w_up, m_w_down, v_mix_norm_g, v_w_in, v_pool_w, v_pool_scale, v_q_norm_g, v_k_norm_g, v_rel_bias, v_w_out, v_mlp_norm_g, v_w_up, v_w_down):
    w = dict(mix_norm_g=mix_norm_g, w_in=w_in, pool_w=pool_w, pool_scale=pool_scale, q_norm_g=q_norm_g,
             k_norm_g=k_norm_g, rel_bias=rel_bias, w_out=w_out, mlp_norm_g=mlp_norm_g, w_up=w_up, w_down=w_down)
    m = dict(mix_norm_g=m_mix_norm_g, w_in=m_w_in, pool_w=m_pool_w, pool_scale=m_pool_scale, q_norm_g=m_q_norm_g,
             k_norm_g=m_k_norm_g, rel_bias=m_rel_bias, w_out=m_w_out, mlp_norm_g=m_mlp_norm_g, w_up=m_w_up, w_down=m_w_down)
    v = dict(mix_norm_g=v_mix_norm_g, w_in=v_w_in, pool_w=v_pool_w, pool_scale=v_pool_scale, q_norm_g=v_q_norm_g,
             k_norm_g=v_k_norm_g, rel_bias=v_rel_bias, w_out=v_w_out, mlp_norm_g=v_mlp_norm_g, w_up=v_w_up, w_down=v_w_down)
    xc, yc, cc = _coords()

    c_idx = jnp.reshape(cc, (1,)).astype(jnp.int32)
    chip_idx = jnp.reshape(2 * xc + yc, (1,)).astype(jnp.int32)
    me = 4 * xc + 2 * yc + cc
    whole = lambda t: t.reshape(t.shape[0], t.shape[1] * t.shape[2], t.shape[3])

    placed = [_halves(p) for p in _place_shards_call([w[n] for n in _BIG], chip_idx, nch=4)]
    win_f, bias = _allgather_call(placed[:1], from_chips=True, name="weights_allgather_in",
                                  meanwhile=_bias_table_work(rel_bias))
    wsend, wrecv, in_flight, started = _gather_start_call(placed[1:], win_f)

    def mlp_weights(after):
        landed = _gather_wait_call(in_flight, wsend, wrecv, after)
        wout_f, wup_f, wdown_f = _allgather_call(landed, from_chips=False, name="weights_pair_forward")
        return whole(wout_f).reshape(-1, wout_f.shape[-1]), whole(wup_f), whole(wdown_f)

    split = []

    def on_mlp_grads(*wire_grads):
        srcs = [_halves(g) for g in wire_grads]
        lands = [lax.empty((N_DEV,) + s.shape[2:], s.dtype) for s in srcs]
        split.extend(_scatter_start_call(srcs, lands, [False] * len(srcs), "grads_scatter_start"))
        return split[4]

    loss_part, dx, big_grads, small_grads = _local_grads(
        x[0], loss_target[0], mix_norm_g + started[0, 0], whole(win_f), pool_w, pool_scale, q_norm_g, k_norm_g, bias,
        mlp_norm_g, mlp_weights, on_mlp_grads)
    g_in, g_out, g_up, g_down = big_grads
    gsend, grecv, srcs_thru, lands_thru, _ = split
    lands_mlp = _scatter_wait_call(srcs_thru, lands_thru, gsend, grecv, g_in[1], [False] * 3, "grads_scatter_wait")

    small_own = _pack_small(small_grads, folded=False, loss=loss_part)
    last_srcs = [_halves(g_in[1]), small_own]
    last_lands = [lax.empty((N_DEV,) + last_srcs[0].shape[2:], WIRE_DTYPE), lax.empty((N_DEV,) + small_own.shape, F32)]
    lsend, lrecv, last_srcs, last_lands, last_started = _scatter_start_call(
        last_srcs, last_lands, [False, True], "grads_scatter_start_last")
    idx = jnp.concatenate([chip_idx, c_idx] + [jnp.reshape(jnp.bitwise_xor(me, r), (1,)) for r in range(1, N_DEV)])
    idx = idx.astype(jnp.int32)
    mlp = _BIG[1:]

    def update(names, own32, lands, tag, dep=None):
        halves = _reduce_call([_halves(g) for g in own32], lands, idx, 4, "grads_reduce_" + tag, dep)
        reduced = _pair_allgather_call(list(halves), "grads_pair_allgather_" + tag)
        return _adamw_call([w[n] for n in names], reduced, [m[n] for n in names], [v[n] for n in names], 8, "adamw_" + tag)

    out_mlp = update(mlp, [g_out[0], g_up[0], g_down[0]], lands_mlp, "mlp", last_started)
    land_in, small_all = _scatter_wait_call(last_srcs, last_lands, lsend, lrecv, out_mlp[3][-1], [False, True],
                                            "grads_scatter_wait_last")
    out_in = update(_BIG[:1], [g_in[0]], [land_in], "in")
    g_pack, d_pack, m_pack, v_pack = _small_call(
        small_all, small_own, jnp.reshape(me, (1,)).astype(jnp.int32), _pack_small(w), _pack_small(m), _pack_small(v))

    grads, deltas, new_m, new_v = (_unpack_small(a) for a in (g_pack, d_pack, m_pack, v_pack))
    for k, res in enumerate((grads, deltas, new_m, new_v)):
        res[_BIG[0]] = out_in[k][0]
        for i, n in enumerate(mlp):
            res[n] = out_mlp[k][i]
    loss = g_pack[LOSS_ROW, 0]
    return (loss, dx[None], *[grads[n] for n in _WEIGHT_ORDER], *[deltas[n] for n in _WEIGHT_ORDER],
            *[new_m[n] for n in _WEIGHT_ORDER], *[new_v[n] for n in _WEIGHT_ORDER])
```

```python
import math

import jax
import jax.numpy as jnp
import numpy as np
from jax import lax
from jax.experimental import pallas as pl
from jax.experimental.pallas import tpu as pltpu

F32 = jnp.float32
MXU_DTYPE = jnp.bfloat16
WIRE_DTYPE = jnp.bfloat16

NORM_EPS = 1e-6
NEG_INF = -1e30
LANES = 128
HEAD_DIM = 64
N_HEADS = 8
POOL_WIDTH = 512
ATTN_WIDTH = 512
POOL_WINDOWS = (2, 4, 8, 16)
POOL_HALO = 16
DILATED_PATTERNS = ((128, 1), (512, 4), (2048, 16))
ATT_BLOCK = 128
ATT_SUPER = ATT_BLOCK * max(dl for _, dl in DILATED_PATTERNS)
ATT_UNITS = ATT_SUPER // ATT_BLOCK
N_BUCKETS = 32
NO_BUCKET = -1
MAX_DISTANCE = 2048
N_CHIPS = 4
N_DEV = 8
ADAM_LR, ADAM_B1, ADAM_B2, ADAM_EPS, ADAM_WD, ADAM_STEP = 0.001, 0.9, 0.999, 1e-08, 0.01, 10
VMEM_LIMIT = 56 * 1024 * 1024
MESH = pl.DeviceIdType.MESH
ANY = pl.BlockSpec(memory_space=pl.ANY)

SMALL_ROWS = 72
LOSS_ROW = 5


def _mm(a, b):
    return jnp.dot(a, b, preferred_element_type=F32)


def _mm_nt(a, b):
    return lax.dot_general(a, b, (((1,), (1,)), ((), ())), preferred_element_type=F32)


def _mm_tn(a, b):
    return lax.dot_general(a, b, (((0,), (0,)), ((), ())), preferred_element_type=F32)


def _params(sem=None, **kw):
    if sem is not None:
        kw["dimension_semantics"] = sem
    return pltpu.CompilerParams(vmem_limit_bytes=VMEM_LIMIT, **kw)


def _low_half():
    return lax.broadcasted_iota(jnp.int32, (1, LANES), 1) < HEAD_DIM


def _head_sum_bcast(y):
    lo = _low_half()
    outs = []
    for j in range(y.shape[1] // LANES):
        c = y[:, j * LANES:(j + 1) * LANES]
        s_lo = jnp.sum(jnp.where(lo, c, 0.0), axis=-1, keepdims=True)
        s_hi = jnp.sum(jnp.where(lo, 0.0, c), axis=-1, keepdims=True)
        outs.append(jnp.where(lo, s_lo, s_hi))
    return jnp.concatenate(outs, axis=-1)


def _rms_bwd(dn, hn, r):
    return r * (dn - hn * jnp.mean(dn * hn, axis=-1, keepdims=True))


def _t5_bucket_np(dist):
    max_exact = N_BUCKETS // 2
    d_f = np.maximum(dist, 1).astype(np.float32)
    ratio = (np.log(d_f / np.float32(max_exact)) / np.float32(math.log(MAX_DISTANCE / max_exact))).astype(np.float32)
    large = max_exact + (ratio * np.float32(N_BUCKETS - max_exact)).astype(np.int32)
    large = np.minimum(large, N_BUCKETS - 1)
    return np.where(dist < max_exact, dist, large).astype(np.int32)


def _window_offsets(dl):
    if dl == 1:
        return _by4_positions(ATT_BLOCK), _by4_positions(2 * ATT_BLOCK)
    return np.arange(ATT_BLOCK), np.arange(2 * ATT_BLOCK)


def _bucket_tables():
    tables = []
    for _, dl in DILATED_PATTERNS:
        qq, kk = _window_offsets(dl)
        dist = qq[:, None] + ATT_BLOCK - kk[None, :]
        bucket = _t5_bucket_np(np.clip(dist, 0, ATT_BLOCK) * dl)
        tables.append(np.where((dist >= 0) & (dist <= ATT_BLOCK), bucket, NO_BUCKET))
    return np.stack(tables).astype(np.int32)


def _previous_block_keys():
    return np.stack([np.broadcast_to(_window_offsets(dl)[1][None, :] < ATT_BLOCK, (ATT_BLOCK, 2 * ATT_BLOCK))
                     for _, dl in DILATED_PATTERNS])


def _f1_call(x, g1, win, poolw, pscale, qg, kg, tm):
    s, d = x.shape
    nblk = s // tm

    def body(x_ref, g1_ref, win_ref, pw_ref, ps_ref, qg_ref, kg_ref,
             a_ref, pooled_ref, ypool_ref, q32_ref, k32_ref, qn_ref, kn_ref, v_ref, ubuf):
        i = pl.program_id(0)
        xv = x_ref[...]
        r = lax.rsqrt(jnp.mean(xv * xv, axis=-1, keepdims=True) + NORM_EPS)
        a = ((xv * r) * g1_ref[...]).astype(MXU_DTYPE)
        a_ref[...] = a
        u = _mm(a, win_ref[0])
        q = _mm(a, win_ref[1])
        k = _mm(a, win_ref[2])
        v_ref[...] = _mm(a, win_ref[3])
        q32_ref[...] = q
        k32_ref[...] = k
        rq = lax.rsqrt(_head_sum_bcast(q * q) * (1.0 / HEAD_DIM) + NORM_EPS)
        qn_ref[...] = ((q * rq) * qg_ref[...]) * (HEAD_DIM ** -0.5)
        rk = lax.rsqrt(_head_sum_bcast(k * k) * (1.0 / HEAD_DIM) + NORM_EPS)
        kn_ref[...] = (k * rk) * kg_ref[...]

        ubuf[0:POOL_HALO, :] = jnp.where(i > 0, ubuf[tm:tm + POOL_HALO, :], 0.0)
        ubuf[POOL_HALO:POOL_HALO + tm, :] = u
        t = i * tm + lax.broadcasted_iota(jnp.int32, (tm, 1), 0)
        for g, w in enumerate(POOL_WINDOWS):
            ls = slice(g * LANES, (g + 1) * LANES)
            ug = u[:, ls]
            acc = ug
            for sh in range(1, w):
                acc = acc + ubuf[POOL_HALO - sh:POOL_HALO - sh + tm, ls]
            cnt = jnp.minimum(t + 1, w).astype(F32)
            pooled = (acc / cnt - ug).astype(MXU_DTYPE)
            pooled_ref[:, ls] = pooled
            ypool_ref[:, ls] = (_mm(pooled, pw_ref[g]) * ps_ref[:, ls]).astype(MXU_DTYPE)

    tok = lambda w: pl.BlockSpec((tm, w), lambda i: (i, 0))
    full = lambda shp: pl.BlockSpec(shp, lambda i: (0,) * len(shp))
    return pl.pallas_call(
        body, name="fwd_inproj",
        grid=(nblk,),
        in_specs=[tok(d), full((1, d)), full(win.shape), full(poolw.shape), full((1, POOL_WIDTH)),
                  full((1, ATTN_WIDTH)), full((1, ATTN_WIDTH))],
        out_specs=[tok(d), tok(POOL_WIDTH), tok(POOL_WIDTH), tok(ATTN_WIDTH), tok(ATTN_WIDTH),
                   tok(ATTN_WIDTH), tok(ATTN_WIDTH), tok(ATTN_WIDTH)],
        out_shape=[jax.ShapeDtypeStruct((s, d), MXU_DTYPE),
                   jax.ShapeDtypeStruct((s, POOL_WIDTH), MXU_DTYPE),
                   jax.ShapeDtypeStruct((s, POOL_WIDTH), MXU_DTYPE),
                   jax.ShapeDtypeStruct((s, ATTN_WIDTH), F32),
                   jax.ShapeDtypeStruct((s, ATTN_WIDTH), F32),
                   jax.ShapeDtypeStruct((s, ATTN_WIDTH), F32),
                   jax.ShapeDtypeStruct((s, ATTN_WIDTH), F32),
                   jax.ShapeDtypeStruct((s, ATTN_WIDTH), F32)],
        scratch_shapes=[pltpu.VMEM((tm + POOL_HALO, POOL_WIDTH), F32)],
        compiler_params=_params(("arbitrary",)),
    )(x, g1, win, poolw, pscale, qg, kg)


DEINT = 4
assert [dl for _, dl in DILATED_PATTERNS] == [1, DEINT, DEINT * DEINT]


def _by4_positions(n):
    pos = np.arange(n)
    return DEINT * (pos % (n // DEINT)) + pos // (n // DEINT)


def _masked_bias(b_ref, p, n):
    return b_ref[p, jnp.minimum(n, 1)].reshape(2 * ATT_BLOCK, 2 * ATT_BLOCK)


def _unit_rows(u, dl):
    assert isinstance(u, int)
    sq, sk = ATT_SUPER // DEINT, 2 * ATT_SUPER // DEINT
    if dl == 1:
        n = ATT_BLOCK // DEINT
        return (u, [pl.ds(r * sq + n * u, n) for r in range(DEINT)],
                [pl.ds(r * sk + sk // 2 + n * (u - 1), 2 * n) for r in range(DEINT)])
    if dl == DEINT:
        r, b = u % DEINT, u // DEINT
        return (b, [pl.ds(r * sq + ATT_BLOCK * b, ATT_BLOCK)],
                [pl.ds(r * sk + sk // 2 + ATT_BLOCK * (b - 1), 2 * ATT_BLOCK)])
    r, a = u % DEINT, u // DEINT
    return 0, [pl.ds(r * sq + a, ATT_BLOCK, stride=DEINT)], [pl.ds(r * sk + a, 2 * ATT_BLOCK, stride=DEINT)]


def _take(ref, runs):
    parts = [ref[run, :] for run in runs]
    return parts[0] if len(parts) == 1 else jnp.concatenate(parts, axis=0)


def _put(ref, runs, value, add=False):
    n = value.shape[0] // len(runs)
    for i, run in enumerate(runs):
        part = value[i * n:(i + 1) * n]
        ref[run, :] = ref[run, :] + part if add else part


def _deinterleave(dst, src, n):
    seg = n // DEINT
    for r in range(DEINT):
        dst[r * seg:(r + 1) * seg, :] = src[pl.ds(r, seg, stride=DEINT), :]


def _deinterleave_pair(dst, prev, cur):
    seg = prev.shape[0] // DEINT
    for r in range(DEINT):
        dst[2 * r * seg:(2 * r + 1) * seg, :] = prev[pl.ds(r, seg, stride=DEINT), :]
        dst[(2 * r + 1) * seg:(2 * r + 2) * seg, :] = cur[pl.ds(r, seg, stride=DEINT), :]


def _interleave(dst, src, n, offset=0):
    seg = n // DEINT
    stride = src.shape[0] // DEINT
    for r in range(DEINT):
        dst[pl.ds(r, seg, stride=DEINT), :] = src[r * stride + offset:r * stride + offset + seg, :]


def _attn_fwd_call(qn, kn, v, bias):
    s, w = qn.shape
    nsb = s // ATT_SUPER
    npair = w // LANES

    def body(q_ref, kc_ref, kp_ref, vc_ref, vp_ref, b_ref, o_ref, lse_ref, qf, kf, vf, acc_s, m_s, l_s):
        sb = pl.program_id(1)
        _deinterleave(qf, q_ref, ATT_SUPER)
        _deinterleave_pair(kf, kp_ref, kc_ref)
        _deinterleave_pair(vf, vp_ref, vc_ref)
        lo = _low_half()
        for p, (_, dl) in enumerate(DILATED_PATTERNS):
            def unit(u, carry, p=p, dl=dl):
                b, rows_q, rows_k = _unit_rows(u, dl)
                qp = _take(qf, rows_q).astype(MXU_DTYPE)
                kcat = _take(kf, rows_k).astype(MXU_DTYPE)
                vcat = _take(vf, rows_k).astype(MXU_DTYPE)
                zero = jnp.zeros_like(qp)
                q2 = jnp.concatenate([jnp.where(lo, qp, zero), jnp.where(lo, zero, qp)], axis=0)
                sc = _mm_nt(q2, kcat) + _masked_bias(b_ref, p, sb * (ATT_UNITS // dl) + b)
                m2 = jnp.max(sc, axis=-1, keepdims=True)
                pr = jnp.exp(sc - m2)
                l2 = jnp.sum(pr, axis=-1, keepdims=True)
                acc2 = _mm(pr.astype(MXU_DTYPE), vcat)
                acc = jnp.where(lo, acc2[:ATT_BLOCK], acc2[ATT_BLOCK:])
                m = jnp.where(lo, m2[:ATT_BLOCK], m2[ATT_BLOCK:])
                l = jnp.where(lo, l2[:ATT_BLOCK], l2[ATT_BLOCK:])
                if p == 0:
                    _put(acc_s, rows_q, acc)
                    _put(m_s, rows_q, m)
                    _put(l_s, rows_q, l)
                else:
                    m_old = _take(m_s, rows_q)
                    m_new = jnp.maximum(m_old, m)
                    a_old = jnp.exp(m_old - m_new)
                    a_new = jnp.exp(m - m_new)
                    _put(acc_s, rows_q, a_old * _take(acc_s, rows_q) + a_new * acc)
                    _put(l_s, rows_q, a_old * _take(l_s, rows_q) + a_new * l)
                    _put(m_s, rows_q, m_new)
                return carry

            for u in range(ATT_UNITS):
                unit(u, None)
        l = l_s[...]
        acc_s[...] = acc_s[...] / l
        m_s[...] = m_s[...] + jnp.log(l)
        _interleave(o_ref, acc_s, ATT_SUPER)
        _interleave(lse_ref, m_s, ATT_SUPER)

    cur = pl.BlockSpec((ATT_SUPER, LANES), lambda j, t: (t, j))
    prev = pl.BlockSpec((ATT_SUPER, LANES), lambda j, t: (jnp.maximum(t - 1, 0), j))
    bspec = pl.BlockSpec((len(DILATED_PATTERNS), 2, 2, ATT_BLOCK, 2 * ATT_BLOCK), lambda j, t: (0, 0, j, 0, 0))
    return pl.pallas_call(
        body, name="attn_fwd",
        grid=(npair, nsb),
        in_specs=[cur, cur, prev, cur, prev, bspec],
        out_specs=[cur, cur],
        out_shape=[jax.ShapeDtypeStruct((s, w), F32), jax.ShapeDtypeStruct((s, w), F32)],
        scratch_shapes=[pltpu.VMEM((ATT_SUPER, LANES), F32), pltpu.VMEM((2 * ATT_SUPER, LANES), F32),
                        pltpu.VMEM((2 * ATT_SUPER, LANES), F32), pltpu.VMEM((ATT_SUPER, LANES), F32),
                        pltpu.VMEM((ATT_SUPER, LANES), F32), pltpu.VMEM((ATT_SUPER, LANES), F32)],
        compiler_params=_params(("arbitrary", "arbitrary")),
    )(qn, kn, kn, v, v, bias)


def _attn_bwd_call(qn, kn, v, do, lse, delta, bias, dep=None):
    s, w = qn.shape
    nsb = s // ATT_SUPER
    npair = w // LANES
    deps = [] if dep is None else [dep]

    def body(q_ref, kc_ref, kp_ref, vc_ref, vp_ref, do_ref, lse_ref, dlt_ref, b_ref, *rest):
        dq_ref, dk_ref, dv_ref, db_ref, qf, kf, vf, dof, lsef, dltf, dqf, dkf, dvf = rest[len(deps):]
        step = pl.program_id(1)
        sb = nsb - 1 - step
        seg = ATT_SUPER // DEINT
        _deinterleave(qf, q_ref, ATT_SUPER)
        _deinterleave(dof, do_ref, ATT_SUPER)
        _deinterleave_pair(kf, kp_ref, kc_ref)
        _deinterleave_pair(vf, vp_ref, vc_ref)
        _deinterleave(lsef, lse_ref, ATT_SUPER)
        _deinterleave(dltf, dlt_ref, ATT_SUPER)

        db_ref[...] = jnp.where(step > 0, db_ref[...], 0.0)
        for acc in (dkf, dvf):
            for r in range(DEINT):
                this, before = pl.ds((2 * r + 1) * seg, seg), pl.ds(2 * r * seg, seg)
                acc[this, :] = jnp.where(step > 0, acc[before, :], 0.0)
                acc[before, :] = jnp.zeros((seg, LANES), F32)
        lo = _low_half()
        for p, (_, dl) in enumerate(DILATED_PATTERNS):
            def unit(u, carry, p=p, dl=dl):
                b, rows_q, rows_k = _unit_rows(u, dl)
                qp = _take(qf, rows_q).astype(MXU_DTYPE)
                dop = _take(dof, rows_q).astype(MXU_DTYPE)
                kcat = _take(kf, rows_k).astype(MXU_DTYPE)
                vcat = _take(vf, rows_k).astype(MXU_DTYPE)
                lse2 = _take(lsef, rows_q)
                dlt2 = _take(dltf, rows_q)
                zero = jnp.zeros_like(qp)
                q2 = jnp.concatenate([jnp.where(lo, qp, zero), jnp.where(lo, zero, qp)], axis=0)
                do2 = jnp.concatenate([jnp.where(lo, dop, zero), jnp.where(lo, zero, dop)], axis=0)
                lse_c = jnp.concatenate([lse2[:, 0:1], lse2[:, HEAD_DIM:HEAD_DIM + 1]], axis=0)
                dlt_c = jnp.concatenate([dlt2[:, 0:1], dlt2[:, HEAD_DIM:HEAD_DIM + 1]], axis=0)
                sc = _mm_nt(q2, kcat) + _masked_bias(b_ref, p, sb * (ATT_UNITS // dl) + b)
                pr = jnp.exp(sc - lse_c)
                ds = pr * (_mm_nt(do2, vcat) - dlt_c)
                db_ref[p] += ds.reshape(2, ATT_BLOCK, 2 * ATT_BLOCK)
                ds_c = ds.astype(MXU_DTYPE)
                dq2 = _mm(ds_c, kcat)
                dk = _mm_tn(ds_c, q2)
                dv = _mm_tn(pr.astype(MXU_DTYPE), do2)
                dq = jnp.where(lo, dq2[:ATT_BLOCK], dq2[ATT_BLOCK:])
                _put(dqf, rows_q, dq, add=p > 0)
                _put(dkf, rows_k, dk, add=True)
                _put(dvf, rows_k, dv, add=True)
                return carry

            for u in range(ATT_UNITS):
                unit(u, None)
        _interleave(dq_ref, dqf, ATT_SUPER)
        _interleave(dk_ref, dkf, ATT_SUPER, offset=seg)
        _interleave(dv_ref, dvf, ATT_SUPER, offset=seg)

    cur = pl.BlockSpec((ATT_SUPER, LANES), lambda j, t: (nsb - 1 - t, j))
    prev = pl.BlockSpec((ATT_SUPER, LANES), lambda j, t: (jnp.maximum(nsb - 2 - t, 0), j))
    npat = len(DILATED_PATTERNS)
    bspec = pl.BlockSpec((npat, 2, 2, ATT_BLOCK, 2 * ATT_BLOCK), lambda j, t: (0, 0, j, 0, 0))
    dbspec = pl.BlockSpec((npat, 2, ATT_BLOCK, 2 * ATT_BLOCK), lambda j, t: (0, j, 0, 0))
    sup = lambda: pltpu.VMEM((ATT_SUPER, LANES), F32)
    sup2 = lambda: pltpu.VMEM((2 * ATT_SUPER, LANES), F32)
    return pl.pallas_call(
        body, name="attn_bwd",
        grid=(npair, nsb),
        in_specs=[cur, cur, prev, cur, prev, cur, cur, cur, bspec] + [ANY] * len(deps),
        out_specs=[cur, cur, cur, dbspec],
        out_shape=[jax.ShapeDtypeStruct((s, w), F32)] * 3
        + [jax.ShapeDtypeStruct((npat, N_HEADS, ATT_BLOCK, 2 * ATT_BLOCK), F32)],
        scratch_shapes=[sup(), sup2(), sup2(), sup(), sup(), sup(), sup(), sup2(), sup2()],
        compiler_params=_params(("arbitrary", "arbitrary")),
    )(qn, kn, kn, v, v, do, lse, delta, bias, *deps)


def _bias_table_work(rel_bias):
    buckets = jnp.asarray(_bucket_tables())
    prev_keys = jnp.asarray(_previous_block_keys().astype(np.int32))
    npat = buckets.shape[0]

    def body(rb_ref, bk_ref, pk_ref, out_ref):
        for p in range(npat):
            for half in range(2):
                ks = slice(half * ATT_BLOCK, (half + 1) * ATT_BLOCK)
                bk = bk_ref[p, :, ks]
                absent = pk_ref[p, :, ks] != 0
                for h in range(N_HEADS):
                    def pick(b, acc, h=h, bk=bk):
                        return jnp.where(bk == b, rb_ref[b, h], acc)

                    tab = lax.fori_loop(0, N_BUCKETS, pick, jnp.full((ATT_BLOCK, ATT_BLOCK), NEG_INF, F32))
                    out_ref[p, 1, h, :, ks] = tab
                    out_ref[p, 0, h, :, ks] = jnp.where(absent, NEG_INF, tab)

    vmem = pl.BlockSpec(memory_space=pltpu.VMEM)
    return ([rel_bias, buckets, prev_keys], [pl.BlockSpec(memory_space=pltpu.SMEM), vmem, vmem],
            jax.ShapeDtypeStruct((npat, 2, N_HEADS, ATT_BLOCK, 2 * ATT_BLOCK), F32), body)


def _rel_bias_grad_call(dbias, buckets):
    npat, nh = dbias.shape[0], dbias.shape[1]

    def body(db_ref, bk_ref, out_ref):
        lane = lax.broadcasted_iota(jnp.int32, (nh, LANES), 1)
        out = jnp.zeros((nh, LANES), F32)
        for b in range(N_BUCKETS):
            tot = jnp.zeros((nh, 1), F32)
            for p in range(npat):
                hit = jnp.where(bk_ref[p][None] == b, db_ref[p], 0.0)
                tot = tot + jnp.sum(jnp.sum(hit, axis=1), axis=-1, keepdims=True)
            out = jnp.where(lane == b, tot, out)
        out_ref[...] = out

    return pl.pallas_call(
        body, name="rel_bias_grad",
        out_shape=jax.ShapeDtypeStruct((nh, LANES), F32),
        compiler_params=_params(),
    )(dbias, buckets)


def _f2_call(x, tgt, ypool, o, wout, wup, wdown, g2, tm):
    s, d = x.shape
    nblk = s // tm
    nch, _, fch = wup.shape
    dff = nch * fch
    mixw = POOL_WIDTH + ATTN_WIDTH

    def body(x_ref, t_ref, yp_ref, o_ref, g2_ref, wout_hbm, wup_hbm, wdown_hbm,
             mixed_ref, c_ref, ff_ref, dz_ref, dy_ref, dh1_ref, dyp_ref, do_ref, dlt_ref, dg2_ref, loss_ref,
             wout_v, wup_v, wdown_v, rz, wsem):
        i = pl.program_id(0)

        @pl.when(i == 0)
        def _():
            copies = [pltpu.make_async_copy(wout_hbm, wout_v, wsem.at[0])]
            for j in range(nch):
                copies.append(pltpu.make_async_copy(wup_hbm.at[j], wup_v.at[j], wsem.at[1 + 2 * j]))
                copies.append(pltpu.make_async_copy(wdown_hbm.at[j], wdown_v.at[j], wsem.at[2 + 2 * j]))
            for cp in copies:
                cp.start()
            dg2_ref[...] = jnp.zeros(dg2_ref.shape, F32)
            loss_ref[...] = jnp.zeros(loss_ref.shape, F32)
            for cp in copies:
                cp.wait()

        o = o_ref[...]
        mixed = jnp.concatenate([yp_ref[...], o.astype(MXU_DTYPE)], axis=-1)
        mixed_ref[...] = mixed
        h1 = x_ref[...] + _mm(mixed, wout_v[...])
        r2 = lax.rsqrt(jnp.mean(h1 * h1, axis=-1, keepdims=True) + NORM_EPS)
        hn = h1 * r2
        c = (hn * g2_ref[...]).astype(MXU_DTYPE)
        c_ref[...] = c
        y = h1
        for j in range(nch):
            cs = slice(j * fch, (j + 1) * fch)
            z = jnp.maximum(_mm(c, wup_v[j]), 0.0)
            rz[:, cs] = z
            ff = (z * z).astype(MXU_DTYPE)
            ff_ref[:, cs] = ff
            y = y + _mm(ff, wdown_v[j])
        err = y - t_ref[...]
        loss_ref[...] += jnp.sum(err * err) * (0.5 / d)
        dy = err * (1.0 / d)
        dy_c = dy.astype(MXU_DTYPE)
        dy_ref[...] = dy_c
        dc = jnp.zeros((tm, d), F32)
        for j in range(nch):
            cs = slice(j * fch, (j + 1) * fch)
            dz = (_mm_nt(dy_c, wdown_v[j]) * (2.0 * rz[:, cs])).astype(MXU_DTYPE)
            dz_ref[:, cs] = dz
            dc = dc + _mm_nt(dz, wup_v[j])
        dg2_ref[...] += jnp.sum(dc * hn, axis=0, keepdims=True)
        dh1 = dy + _rms_bwd(dc * g2_ref[...], hn, r2)
        dh1_ref[...] = dh1
        dmix = _mm_nt(dh1.astype(MXU_DTYPE), wout_v[...])
        dyp_ref[...] = dmix[:, :POOL_WIDTH]
        do = dmix[:, POOL_WIDTH:]
        do_ref[...] = do
        dlt_ref[...] = _head_sum_bcast(do * o)

    tok = lambda w: pl.BlockSpec((tm, w), lambda i: (i, 0))
    const = lambda shp: pl.BlockSpec(shp, lambda i: (0,) * len(shp))
    return pl.pallas_call(
        body, name="fwd_mlp_bwd_mlp",
        grid=(nblk,),
        in_specs=[tok(d), tok(d), tok(POOL_WIDTH), tok(ATTN_WIDTH), const((1, d)), ANY, ANY, ANY],
        out_specs=[tok(mixw), tok(d), tok(dff), tok(dff), tok(d), tok(d), tok(POOL_WIDTH), tok(ATTN_WIDTH),
                   tok(ATTN_WIDTH), const((1, d)), const((1, LANES))],
        out_shape=[jax.ShapeDtypeStruct((s, mixw), MXU_DTYPE),
                   jax.ShapeDtypeStruct((s, d), MXU_DTYPE),
                   jax.ShapeDtypeStruct((s, dff), MXU_DTYPE),
                   jax.ShapeDtypeStruct((s, dff), MXU_DTYPE),
                   jax.ShapeDtypeStruct((s, d), MXU_DTYPE),
                   jax.ShapeDtypeStruct((s, d), F32),
                   jax.ShapeDtypeStruct((s, POOL_WIDTH), F32),
                   jax.ShapeDtypeStruct((s, ATTN_WIDTH), F32),
                   jax.ShapeDtypeStruct((s, ATTN_WIDTH), F32),
                   jax.ShapeDtypeStruct((1, d), F32),
                   jax.ShapeDtypeStruct((1, LANES), F32)],
        scratch_shapes=[pltpu.VMEM(wout.shape, MXU_DTYPE), pltpu.VMEM(wup.shape, MXU_DTYPE),
                        pltpu.VMEM(wdown.shape, MXU_DTYPE), pltpu.VMEM((tm, dff), F32),
                        pltpu.SemaphoreType.DMA((1 + 2 * nch,))],
        compiler_params=_params(("arbitrary",)),
    )(x, tgt, ypool, o, g2, wout, wup, wdown)


def _bproj_call(dqn, dkn, dv, q32, k32, dypool, pooled, x, dh1, win, poolw, pscale, qg, kg, g1, tm):
    s, d = x.shape
    nblk = s // tm
    ngrp = len(POOL_WINDOWS)

    def body(dqn_ref, dkn_ref, dv_ref, q_ref, k_ref, dyp_ref, pooled_ref, x_ref, dh1_ref,
             win_hbm, pw_ref, ps_ref, qg_ref, kg_ref, g1_ref,
             dx_ref, dproj_ref, dg1_ref, dqg_ref, dkg_ref, dpw_ref, dps_ref, win_v, ebuf):
        step = pl.program_id(0)
        i = nblk - 1 - step

        @pl.when(step == 0)
        def _():
            pltpu.sync_copy(win_hbm, win_v)
            dg1_ref[...] = jnp.zeros(dg1_ref.shape, F32)
            dqg_ref[...] = jnp.zeros(dqg_ref.shape, F32)
            dkg_ref[...] = jnp.zeros(dkg_ref.shape, F32)
            dpw_ref[...] = jnp.zeros(dpw_ref.shape, F32)
            dps_ref[...] = jnp.zeros(dps_ref.shape, F32)
            ebuf[tm:tm + POOL_HALO, :] = jnp.zeros((POOL_HALO, POOL_WIDTH), F32)

        @pl.when(step > 0)
        def _():
            ebuf[tm:tm + POOL_HALO, :] = ebuf[0:POOL_HALO, :]

        def qk_bwd(dn_sum, raw, gain, scale, dgain_ref):
            rr = lax.rsqrt(_head_sum_bcast(raw * raw) * (1.0 / HEAD_DIM) + NORM_EPS)
            hn = raw * rr
            dgain_ref[...] += jnp.sum(dn_sum * hn, axis=0, keepdims=True) * scale
            dn = dn_sum * (gain * scale)
            return rr * (dn - hn * (_head_sum_bcast(dn * hn) * (1.0 / HEAD_DIM)))

        dq = qk_bwd(dqn_ref[...], q_ref[...], qg_ref[...], HEAD_DIM ** -0.5, dqg_ref)
        dk = qk_bwd(dkn_ref[...], k_ref[...], kg_ref[...], 1.0, dkg_ref)

        t = i * tm + lax.broadcasted_iota(jnp.int32, (tm, 1), 0)
        dpooled = []
        for g, w in enumerate(POOL_WINDOWS):
            ls = slice(g * LANES, (g + 1) * LANES)
            dm = dyp_ref[:, ls]
            pg = pooled_ref[:, ls]
            dps_ref[:, ls] += jnp.sum(dm * _mm(pg, pw_ref[g]), axis=0, keepdims=True)
            dms = (dm * ps_ref[:, ls]).astype(MXU_DTYPE)
            dpw_ref[g] += _mm_tn(pg, dms)
            dpg = _mm_nt(dms, pw_ref[g])
            dpooled.append(dpg)
            ebuf[0:tm, ls] = dpg / jnp.minimum(t + 1, w).astype(F32)
        du = []
        for g, w in enumerate(POOL_WINDOWS):
            ls = slice(g * LANES, (g + 1) * LANES)
            acc = ebuf[0:tm, ls]
            for sh in range(1, w):
                acc = acc + ebuf[sh:sh + tm, ls]
            du.append(acc - dpooled[g])
        parts = [jnp.concatenate(du, axis=-1), dq, dk, dv_ref[...]]
        da = jnp.zeros((tm, d), F32)
        for p, part in enumerate(parts):
            pc = part.astype(MXU_DTYPE)
            dproj_ref[:, p * POOL_WIDTH:(p + 1) * POOL_WIDTH] = pc
            da = da + _mm_nt(pc, win_v[p])
        xv = x_ref[...]
        r = lax.rsqrt(jnp.mean(xv * xv, axis=-1, keepdims=True) + NORM_EPS)
        xn = xv * r
        dg1_ref[...] += jnp.sum(da * xn, axis=0, keepdims=True)
        dx_ref[...] = dh1_ref[...] + _rms_bwd(da * g1_ref[...], xn, r)

    tok = lambda w: pl.BlockSpec((tm, w), lambda t: (nblk - 1 - t, 0))
    const = lambda shp: pl.BlockSpec(shp, lambda t: (0,) * len(shp))
    return pl.pallas_call(
        body, name="bwd_inproj",
        grid=(nblk,),
        in_specs=[tok(ATTN_WIDTH)] * 5 + [tok(POOL_WIDTH), tok(POOL_WIDTH), tok(d), tok(d),
                                          ANY, const(poolw.shape), const((1, POOL_WIDTH)), const((1, ATTN_WIDTH)),
                                          const((1, ATTN_WIDTH)), const((1, d))],
        out_specs=[tok(d), tok(4 * POOL_WIDTH), const((1, d)), const((1, ATTN_WIDTH)), const((1, ATTN_WIDTH)),
                   const((ngrp, LANES, LANES)), const((1, POOL_WIDTH))],
        out_shape=[jax.ShapeDtypeStruct((s, d), F32),
                   jax.ShapeDtypeStruct((s, 4 * POOL_WIDTH), MXU_DTYPE),
                   jax.ShapeDtypeStruct((1, d), F32),
                   jax.ShapeDtypeStruct((1, ATTN_WIDTH), F32),
                   jax.ShapeDtypeStruct((1, ATTN_WIDTH), F32),
                   jax.ShapeDtypeStruct((ngrp, LANES, LANES), F32),
                   jax.ShapeDtypeStruct((1, POOL_WIDTH), F32)],
        scratch_shapes=[pltpu.VMEM(win.shape, MXU_DTYPE), pltpu.VMEM((tm + POOL_HALO, POOL_WIDTH), F32)],
        compiler_params=_params(("arbitrary",)),
    )(dqn, dkn, dv, q32, k32, dypool, pooled, x, dh1, win, poolw, pscale, qg, kg, g1)


def _wgrad_call(a, b, bm, bn, bk, out_shape, out_block, out_index, name):
    s, m = a.shape
    _, n = b.shape
    nk = s // bk

    def body(a_ref, b_ref, o_ref, wire_ref):
        k = pl.program_id(2)
        acc = jnp.where(k > 0, o_ref[...], 0.0) + _mm_tn(a_ref[...].astype(MXU_DTYPE), b_ref[...].astype(MXU_DTYPE))
        o_ref[...] = acc
        wire_ref[...] = acc.astype(WIRE_DTYPE)

    return pl.pallas_call(
        body, name=name,
        grid=(m // bm, n // bn, nk),
        in_specs=[pl.BlockSpec((bk, bm), lambda i, j, k: (k, i)), pl.BlockSpec((bk, bn), lambda i, j, k: (k, j))],
        out_specs=[pl.BlockSpec(out_block, out_index)] * 2,
        out_shape=[jax.ShapeDtypeStruct(out_shape, F32), jax.ShapeDtypeStruct(out_shape, WIRE_DTYPE)],
        compiler_params=_params(("arbitrary", "arbitrary", "arbitrary")),
    )(a, b)


def _local_grads(x, tgt, g1, win, poolw, pscale, qg, kg, bias, g2, mlp_weights, on_mlp_grads=None):
    s, d = x.shape
    g1r, g2r = g1.reshape(1, d), g2.reshape(1, d)
    psr = pscale.reshape(1, POOL_WIDTH)
    qgr = jnp.tile(qg, N_HEADS).reshape(1, ATTN_WIDTH)
    kgr = jnp.tile(kg, N_HEADS).reshape(1, ATTN_WIDTH)
    pw_c = poolw.astype(MXU_DTYPE)
    buckets = jnp.asarray(_bucket_tables())
    bk = min(s, 4096)

    a, pooled, ypool, q32, k32, qn, kn, v = _f1_call(x, g1r, win, pw_c, psr, qgr, kgr, tm=1024)
    o, lse = _attn_fwd_call(qn, kn, v, bias)
    wout, wup, wdown = mlp_weights(o)
    mixed, c, ff, dz, dy, dh1, dypool, do, delta, dg2, loss = _f2_call(x, tgt, ypool, o, wout, wup, wdown, g2r, tm=256)
    dff = ff.shape[1]
    g_out = [g.reshape(N_CHIPS, d // N_CHIPS, d)
             for g in _wgrad_call(mixed, dh1, d, d, bk // 4, (d, d), (d, d), lambda i, j, k: (0, 0), "wgrad_out")]
    g_up = _wgrad_call(c, dz, d, dff // N_CHIPS, bk, (N_CHIPS, d, dff // N_CHIPS), (None, d, dff // N_CHIPS),
                       lambda i, j, k: (j, 0, 0), "wgrad_up")
    g_down = _wgrad_call(ff, dy, dff // N_CHIPS, d, bk, (N_CHIPS, dff // N_CHIPS, d), (None, dff // N_CHIPS, d),
                         lambda i, j, k: (i, 0, 0), "wgrad_down")
    dep = None if on_mlp_grads is None else on_mlp_grads(g_out[1], g_up[1], g_down[1])
    dqn, dkn, dv, dbias = _attn_bwd_call(qn, kn, v, do, lse, delta, bias, dep)
    dx, dproj, dg1, dqg, dkg, dpw, dps = _bproj_call(
        dqn, dkn, dv, q32, k32, dypool, pooled, x, dh1, win, pw_c, psr, qgr, kgr, g1r, tm=512)
    nin = dproj.shape[1] // N_CHIPS
    g_in = _wgrad_call(a, dproj, d, nin, bk, (N_CHIPS, d, nin), (None, d, nin), lambda i, j, k: (j, 0, 0), "wgrad_in")
    drb = _rel_bias_grad_call(dbias, buckets)
    small = dict(
        mix_norm_g=dg1.reshape(d), mlp_norm_g=dg2.reshape(d), pool_scale=dps.reshape(POOL_WIDTH),
        q_norm_g=dqg.reshape(ATTN_WIDTH), k_norm_g=dkg.reshape(ATTN_WIDTH),
        rel_bias=drb[:, :N_BUCKETS].T, pool_w=dpw)
    return loss[0, 0], dx, (g_in, g_out, g_up, g_down), small


def _coords():
    return lax.axis_index("x"), lax.axis_index("y"), lax.axis_index("c")


def _other_chips(x, y):
    return [(1 - x, y), (x, 1 - y), (1 - x, 1 - y)]


def _remote(src, dst, send_sem, recv_sem, dev):
    return pltpu.make_async_remote_copy(src_ref=src, dst_ref=dst, send_sem=send_sem, recv_sem=recv_sem,
                                        device_id=dev, device_id_type=MESH)


def _halves(a):
    return a.reshape(a.shape[:-2] + (2, a.shape[-2] // 2, a.shape[-1]))


def _place_shards_call(shards, chip_idx, nch):
    nw = len(shards)

    def body(chip_ref, *refs):
        for w in range(nw):
            refs[nw + w][...] = refs[w][...].astype(WIRE_DTYPE)

    in_specs = [pl.BlockSpec((s.shape[0] // nch, s.shape[1]), lambda i, chip_ref: (i, 0)) for s in shards]
    out_specs = [pl.BlockSpec((None, s.shape[0] // nch, s.shape[1]), lambda i, chip_ref: (chip_ref[0], i, 0))
                 for s in shards]
    return pl.pallas_call(
        body, name="weights_place",
        grid_spec=pltpu.PrefetchScalarGridSpec(num_scalar_prefetch=1, grid=(nch,),
                                               in_specs=in_specs, out_specs=out_specs),
        out_shape=[jax.ShapeDtypeStruct((N_CHIPS,) + s.shape, WIRE_DTYPE) for s in shards],
        compiler_params=_params(("arbitrary",)),
    )(chip_idx, *shards)


def _allgather_call(placed, from_chips, name, meanwhile=None):
    nw = len(placed)
    ncp = 3 * nw
    extra, extra_specs, extra_shape, extra_body = meanwhile if meanwhile else ([], [], None, None)
    ne = len(extra)

    def body(*refs):
        outs = refs[nw + ne:2 * nw + ne]
        send1, recv1, send2, recv2 = refs[-4:]
        x, y, c = _coords()
        chip = 2 * x + y
        others = _other_chips(x, y)
        first, passed = [], []
        if from_chips:
            for w in range(nw):
                for k, (ox, oy) in enumerate(others):
                    mine = outs[w].at[chip, c]
                    cp = _remote(mine, mine, send1.at[3 * w + k], recv1.at[3 * w + k], (ox, oy, c))
                    cp.start()
                    first.append(cp)
        if meanwhile:
            extra_body(*refs[nw:nw + ne], refs[2 * nw + ne])
        for w in range(nw):
            for k, (ox, oy) in enumerate(others):
                piece = outs[w].at[2 * ox + oy, c]
                if from_chips:
                    _remote(piece, piece, send1.at[3 * w + k], recv1.at[3 * w + k], (ox, oy, c)).wait_recv()
                cp = _remote(piece, piece, send2.at[3 * w + k], recv2.at[3 * w + k], (x, y, 1 - c))
                cp.start()
                passed.append(cp)
        for w in range(nw):
            for k, (ox, oy) in enumerate(others):
                piece = outs[w].at[2 * ox + oy, 1 - c]
                _remote(piece, piece, send2.at[3 * w + k], recv2.at[3 * w + k], (x, y, 1 - c)).wait_recv()
        for cp in first + passed:
            cp.wait_send()

    return pl.pallas_call(
        body, name=name,
        in_specs=[ANY] * nw + list(extra_specs),
        out_specs=[ANY] * nw + ([pl.BlockSpec(memory_space=pltpu.VMEM)] if meanwhile else []),
        out_shape=[jax.ShapeDtypeStruct(s.shape, s.dtype) for s in placed] + ([extra_shape] if meanwhile else []),
        input_output_aliases={w: w for w in range(nw)},
        scratch_shapes=[pltpu.SemaphoreType.DMA((ncp,))] * 4,
        compiler_params=_params(),
    )(*placed, *extra)


HBM_SPEC = pl.BlockSpec(memory_space=pltpu.HBM)
SEM_SPEC = pl.BlockSpec(memory_space=pltpu.SEMAPHORE)
SPLIT_EFFECT = pltpu.SideEffectType.DATAFLOW_SIDE_EFFECTING


def _in_hbm(a):
    return pltpu.with_memory_space_constraint(a, pltpu.HBM)


def _gather_copies(bufs, send, recv):
    x, y, c = _coords()
    chip = 2 * x + y
    cps = []
    for w, buf in enumerate(bufs):
        for k, (ox, oy) in enumerate(_other_chips(x, y)):
            mine, theirs = buf.at[chip, c], buf.at[2 * ox + oy, c]
            sems = (send.at[3 * w + k], recv.at[3 * w + k], (ox, oy, c))
            cps.append((_remote(mine, mine, *sems), _remote(theirs, theirs, *sems)))
    return cps


def _gather_start_call(bufs, after):
    nw = len(bufs)

    def body(*refs):
        ins, send, recv, token = refs[:nw], refs[nw + 1], refs[nw + 2], refs[2 * nw + 3]
        for out, _ in _gather_copies(ins, send, recv):
            out.start()
        token[...] = jnp.zeros(token.shape, F32)

    res = pl.pallas_call(
        body, name="weights_gather_start",
        in_specs=[HBM_SPEC] * nw + [ANY],
        out_specs=[SEM_SPEC, SEM_SPEC] + [HBM_SPEC] * nw + [pl.BlockSpec(memory_space=pltpu.VMEM)],
        out_shape=[pltpu.SemaphoreType.DMA((3 * nw,)), pltpu.SemaphoreType.DMA((3 * nw,))]
        + [pltpu.HBM(b.shape, b.dtype) for b in bufs] + [jax.ShapeDtypeStruct((8, LANES), F32)],
        input_output_aliases={w: 2 + w for w in range(nw)},
        compiler_params=pltpu.CompilerParams(has_side_effects=SPLIT_EFFECT),
    )(*[_in_hbm(b) for b in bufs], after)
    return res[0], res[1], list(res[2:2 + nw]), res[2 + nw]


def _gather_wait_call(bufs, send, recv, after):
    nw = len(bufs)

    def body(*refs):
        ins, send, recv = refs[:nw], refs[nw], refs[nw + 1]
        for out, back in _gather_copies(ins, send, recv):
            out.wait_send()
            back.wait_recv()

    return pl.pallas_call(
        body, name="weights_gather_wait",
        in_specs=[HBM_SPEC] * nw + [SEM_SPEC, SEM_SPEC, ANY],
        out_specs=[HBM_SPEC] * nw,
        out_shape=[pltpu.HBM(b.shape, b.dtype) for b in bufs],
        input_output_aliases={w: w for w in range(nw)},
        compiler_params=pltpu.CompilerParams(has_side_effects=SPLIT_EFFECT),
    )(*bufs, send, recv, after)


def _scatter_copies(srcs, lands, send, recv, wholes):
    x, y, c = _coords()
    me = 4 * x + 2 * y + c
    cps = []
    for w, (src, land) in enumerate(zip(srcs, lands)):
        for r in range(1, N_DEV):
            px, py, pc = ((1 - x) if r & 4 else x, (1 - y) if r & 2 else y, (1 - c) if r & 1 else c)
            sems = (send.at[(N_DEV - 1) * w + r - 1], recv.at[(N_DEV - 1) * w + r - 1], (px, py, pc))
            piece = src if wholes[w] else src.at[2 * px + py, pc]
            cps.append((_remote(piece, land.at[me], *sems), _remote(piece, land.at[4 * px + 2 * py + pc], *sems)))
    return cps


def _scatter_start_call(srcs, lands, wholes, name):
    nw = len(srcs)
    ncp = (N_DEV - 1) * nw

    def body(*refs):
        ins, lnd, send, recv, token = refs[:nw], refs[nw:2 * nw], refs[2 * nw], refs[2 * nw + 1], refs[4 * nw + 2]
        for out, _ in _scatter_copies(ins, lnd, send, recv, wholes):
            out.start()
        token[...] = jnp.zeros(token.shape, F32)

    res = pl.pallas_call(
        body, name=name,
        in_specs=[HBM_SPEC] * (2 * nw),
        out_specs=[SEM_SPEC, SEM_SPEC] + [HBM_SPEC] * (2 * nw) + [pl.BlockSpec(memory_space=pltpu.VMEM)],
        out_shape=[pltpu.SemaphoreType.DMA((ncp,)), pltpu.SemaphoreType.DMA((ncp,))]
        + [pltpu.HBM(b.shape, b.dtype) for b in list(srcs) + list(lands)] + [jax.ShapeDtypeStruct((8, LANES), F32)],
        input_output_aliases={i: 2 + i for i in range(2 * nw)},
        compiler_params=pltpu.CompilerParams(has_side_effects=SPLIT_EFFECT),
    )(*[_in_hbm(b) for b in list(srcs) + list(lands)])
    return res[0], res[1], list(res[2:2 + nw]), list(res[2 + nw:2 + 2 * nw]), res[2 + 2 * nw]


def _scatter_wait_call(srcs, lands, send, recv, after, wholes, name):
    nw = len(srcs)

    def body(*refs):
        ins, lnd, send, recv = refs[:nw], refs[nw:2 * nw], refs[2 * nw], refs[2 * nw + 1]
        for out, back in _scatter_copies(ins, lnd, send, recv, wholes):
            out.wait_send()
            back.wait_recv()

    res = pl.pallas_call(
        body, name=name,
        in_specs=[HBM_SPEC] * (2 * nw) + [SEM_SPEC, SEM_SPEC, ANY],
        out_specs=[HBM_SPEC] * (2 * nw),
        out_shape=[pltpu.HBM(b.shape, b.dtype) for b in list(srcs) + list(lands)],
        input_output_aliases={i: i for i in range(2 * nw)},
        compiler_params=pltpu.CompilerParams(has_side_effects=SPLIT_EFFECT),
    )(*srcs, *lands, send, recv, after)
    return list(res[nw:])


def _reduce_call(own, lands, idx, nch, name, dep=None):
    nw = len(own)
    deps = [] if dep is None else [dep]

    def body(idx_ref, *refs):
        refs = refs[:2 * nw] + refs[2 * nw + len(deps):]
        for w in range(nw):
            tot = refs[w][...]
            for r in range(1, N_DEV):
                tot = tot + refs[nw + w][idx_ref[1 + r]].astype(F32)
            refs[2 * nw + w][...] = tot

    in_specs, out_specs, out_shape = [], [], []
    for s in own:
        in_specs.append(pl.BlockSpec((None, None, s.shape[2] // nch, s.shape[3]),
                                     lambda i, idx_ref: (idx_ref[0], idx_ref[1], i, 0)))
    for s in own:
        in_specs.append(pl.BlockSpec((N_DEV, s.shape[2] // nch, s.shape[3]), lambda i, idx_ref: (0, i, 0)))
    for s in own:
        out_specs.append(pl.BlockSpec((None, s.shape[2] // nch, s.shape[3]), lambda i, idx_ref: (idx_ref[1], i, 0)))
        out_shape.append(jax.ShapeDtypeStruct((2,) + s.shape[2:], F32))
    return pl.pallas_call(
        body, name=name,
        grid_spec=pltpu.PrefetchScalarGridSpec(num_scalar_prefetch=1, grid=(nch,),
                                               in_specs=in_specs + [ANY] * len(deps), out_specs=out_specs),
        out_shape=out_shape,
        compiler_params=_params(("arbitrary",)),
    )(idx, *own, *lands, *deps)


def _pair_allgather_call(halves, name):
    nw = len(halves)

    def body(*refs):
        outs = refs[nw:2 * nw]
        send, recv = refs[2 * nw:]
        x, y, c = _coords()
        cps = []
        for w in range(nw):
            cp = _remote(outs[w].at[c], outs[w].at[c], send.at[w], recv.at[w], (x, y, 1 - c))
            cp.start()
            cps.append(cp)
        for w in range(nw):
            theirs = outs[w].at[1 - c]
            _remote(theirs, theirs, send.at[w], recv.at[w], (x, y, 1 - c)).wait_recv()
        for cp in cps:
            cp.wait_send()

    outs = pl.pallas_call(
        body, name=name,
        in_specs=[ANY] * nw, out_specs=[ANY] * nw,
        out_shape=[jax.ShapeDtypeStruct(h.shape, h.dtype) for h in halves],
        input_output_aliases={w: w for w in range(nw)},
        scratch_shapes=[pltpu.SemaphoreType.DMA((nw,))] * 2,
    )(*halves)
    return [o.reshape(2 * h.shape[1], h.shape[2]) for o, h in zip(outs, halves)]


def _adamw(w, g, m, v):
    m = ADAM_B1 * m + (1.0 - ADAM_B1) * g
    v = ADAM_B2 * v + (1.0 - ADAM_B2) * (g * g)
    m_hat = m / (1.0 - ADAM_B1 ** ADAM_STEP)
    v_hat = v / (1.0 - ADAM_B2 ** ADAM_STEP)
    delta = -ADAM_LR * (m_hat / (jnp.sqrt(v_hat) + ADAM_EPS) + ADAM_WD * w)
    return delta, m, v


def _adamw_call(ws, gs, ms, vs, nch, name):
    nw = len(ws)

    def body(*refs):
        for w in range(nw):
            g = refs[nw + w][...]
            delta, m, v = _adamw(refs[w][...], g, refs[2 * nw + w][...], refs[3 * nw + w][...])
            refs[4 * nw + w][...] = g
            refs[5 * nw + w][...] = delta
            refs[6 * nw + w][...] = m
            refs[7 * nw + w][...] = v

    specs = [pl.BlockSpec((a.shape[0] // nch, a.shape[1]), lambda i: (i, 0)) for a in ws]
    res = pl.pallas_call(
        body, name=name,
        grid=(nch,),
        in_specs=specs * 4, out_specs=specs * 4,
        out_shape=[jax.ShapeDtypeStruct(a.shape, F32) for a in ws] * 4,
        compiler_params=_params(("arbitrary",)),
    )(*ws, *gs, *ms, *vs)
    return res[:nw], res[nw:2 * nw], res[2 * nw:3 * nw], res[3 * nw:]


def _small_call(gathered, own, me_idx, w, m, v):
    def fold(row):
        tot = row[:, 0:LANES] + row[:, LANES:2 * LANES] + row[:, 2 * LANES:3 * LANES] + row[:, 3 * LANES:4 * LANES]
        return tot + pltpu.roll(tot, HEAD_DIM, axis=1)

    def body(me_ref, ga_ref, own_ref, w_ref, m_ref, v_ref, g_out, d_out, m_out, v_out):
        me = me_ref[0]
        term = lambda i: jnp.where(me == i, own_ref[...], ga_ref[i])
        g = term(0)
        for i in range(1, N_DEV):
            g = g + term(i)
        unfolded = g[4:5, :]
        folded = jnp.concatenate([fold(unfolded[:, :ATTN_WIDTH]), fold(unfolded[:, ATTN_WIDTH:]),
                                  jnp.zeros((1, 1024 - 2 * LANES), F32)], axis=-1)
        row = lax.broadcasted_iota(jnp.int32, g.shape, 0)
        g = jnp.where(row == 3, folded, g)
        delta, mm, vv = _adamw(w_ref[...], g, m_ref[...], v_ref[...])
        g_out[...] = g
        d_out[...] = delta
        m_out[...] = mm
        v_out[...] = vv

    vmem = pl.BlockSpec(memory_space=pltpu.VMEM)
    return pl.pallas_call(
        body, name="adamw_small",
        in_specs=[pl.BlockSpec(memory_space=pltpu.SMEM)] + [vmem] * 5,
        out_shape=[jax.ShapeDtypeStruct(w.shape, F32)] * 4,
        compiler_params=_params(),
    )(me_idx, gathered, own, w, m, v)


def _pack_small(p, folded=True, loss=None):
    z = lambda n: jnp.zeros((n,), F32)
    rows = [p["mix_norm_g"], p["mlp_norm_g"],
            jnp.concatenate([p["pool_scale"], p["rel_bias"].reshape(-1), z(1024 - POOL_WIDTH - N_BUCKETS * N_HEADS)])]
    if folded:
        rows += [jnp.concatenate([p["q_norm_g"], z(LANES - HEAD_DIM), p["k_norm_g"], z(1024 - LANES - HEAD_DIM)]), z(1024)]
    else:
        rows += [z(1024), jnp.concatenate([p["q_norm_g"], p["k_norm_g"]])]
    rows += [z(1024) if loss is None else jnp.concatenate([loss.reshape(1), z(1023)])]
    head = jnp.stack(rows + [z(1024)] * 2)
    return jnp.concatenate([head, p["pool_w"].reshape(-1, 1024)], axis=0)


def _unpack_small(a):
    return dict(
        mix_norm_g=a[0], mlp_norm_g=a[1], pool_scale=a[2, :POOL_WIDTH],
        rel_bias=a[2, POOL_WIDTH:POOL_WIDTH + N_BUCKETS * N_HEADS].reshape(N_BUCKETS, N_HEADS),
        q_norm_g=a[3, :HEAD_DIM], k_norm_g=a[3, LANES:LANES + HEAD_DIM],
        pool_w=a[8:].reshape(len(POOL_WINDOWS), LANES, LANES))


_WEIGHT_ORDER = ("mix_norm_g", "w_in", "pool_w", "pool_scale", "q_norm_g", "k_norm_g", "rel_bias", "w_out",
                 "mlp_norm_g", "w_up", "w_down")
_BIG = ("w_in", "w_out", "w_up", "w_down")


def kernel(x, mix_norm_g, w_in, pool_w, pool_scale, q_norm_g, k_norm_g, rel_bias, w_out, mlp_norm_g, w_up, w_down, loss_target, m_mix_norm_g, m_w_in, m_pool_w, m_pool_scale, m_q_norm_g, m_k_norm_g, m_rel_bias, m_w_out, m_mlp_norm_g, m_w_up, m_w_down, v_mix_norm_g, v_w_in, v_pool_w, v_pool_scale, v_q_norm_g, v_k_norm_g, v_rel_bias, v_w_out, v_mlp_norm_g, v_w_up, v_w_down):
    w = dict(mix_norm_g=mix_norm_g, w_in=w_in, pool_w=pool_w, pool_scale=pool_scale, q_norm_g=q_norm_g,
             k_norm_g=k_norm_g, rel_bias=rel_bias, w_out=w_out, mlp_norm_g=mlp_norm_g, w_up=w_up, w_down=w_down)
    m = dict(mix_norm_g=m_mix_norm_g, w_in=m_w_in, pool_w=m_pool_w, pool_scale=m_pool_scale, q_norm_g=m_q_norm_g,
             k_norm_g=m_k_norm_g, rel_bias=m_rel_bias, w_out=m_w_out, mlp_norm_g=m_mlp_norm_g, w_up=m_w_up, w_down=m_w_down)
    v = dict(mix_norm_g=v_mix_norm_g, w_in=v_w_in, pool_w=v_pool_w, pool_scale=v_pool_scale, q_norm_g=v_q_norm_g,
             k_norm_g=v_k_norm_g, rel_bias=v_rel_bias, w_out=v_w_out, mlp_norm_g=v_mlp_norm_g, w_up=v_w_up, w_down=v_w_down)
    xc, yc, cc = _coords()

    c_idx = jnp.reshape(cc, (1,)).astype(jnp.int32)
    chip_idx = jnp.reshape(2 * xc + yc, (1,)).astype(jnp.int32)
    me = 4 * xc + 2 * yc + cc
    whole = lambda t: t.reshape(t.shape[0], t.shape[1] * t.shape[2], t.shape[3])

    placed = [_halves(p) for p in _place_shards_call([w[n] for n in _BIG], chip_idx, nch=4)]
    win_f, bias = _allgather_call(placed[:1], from_chips=True, name="weights_allgather_in",
                                  meanwhile=_bias_table_work(rel_bias))
    wsend, wrecv, in_flight, started = _gather_start_call(placed[1:], win_f)

    def mlp_weights(after):
        landed = _gather_wait_call(in_flight, wsend, wrecv, after)
        wout_f, wup_f, wdown_f = _allgather_call(landed, from_chips=False, name="weights_pair_forward")
        return whole(wout_f).reshape(-1, wout_f.shape[-1]), whole(wup_f), whole(wdown_f)

    split = []

    def on_mlp_grads(*wire_grads):
        srcs = [_halves(g) for g in wire_grads]
        lands = [lax.empty((N_DEV,) + s.shape[2:], s.dtype) for s in srcs]
        split.extend(_scatter_start_call(srcs, lands, [False] * len(srcs), "grads_scatter_start"))
        return split[4]

    loss_part, dx, big_grads, small_grads = _local_grads(
        x[0], loss_target[0], mix_norm_g + started[0, 0], whole(win_f), pool_w, pool_scale, q_norm_g, k_norm_g, bias,
        mlp_norm_g, mlp_weights, on_mlp_grads)
    g_in, g_out, g_up, g_down = big_grads
    gsend, grecv, srcs_thru, lands_thru, _ = split
    lands_mlp = _scatter_wait_call(srcs_thru, lands_thru, gsend, grecv, g_in[1], [False] * 3, "grads_scatter_wait")

    small_own = _pack_small(small_grads, folded=False, loss=loss_part)
    last_srcs = [_halves(g_in[1]), small_own]
    last_lands = [lax.empty((N_DEV,) + last_srcs[0].shape[2:], WIRE_DTYPE), lax.empty((N_DEV,) + small_own.shape, F32)]
    lsend, lrecv, last_srcs, last_lands, last_started = _scatter_start_call(
        last_srcs, last_lands, [False, True], "grads_scatter_start_last")
    idx = jnp.concatenate([chip_idx, c_idx] + [jnp.reshape(jnp.bitwise_xor(me, r), (1,)) for r in range(1, N_DEV)])
    idx = idx.astype(jnp.int32)
    mlp = _BIG[1:]

    def update(names, own32, lands, tag, dep=None):
        halves = _reduce_call([_halves(g) for g in own32], lands, idx, 4, "grads_reduce_" + tag, dep)
        reduced = _pair_allgather_call(list(halves), "grads_pair_allgather_" + tag)
        return _adamw_call([w[n] for n in names], reduced, [m[n] for n in names], [v[n] for n in names], 8, "adamw_" + tag)

    out_mlp = update(mlp, [g_out[0], g_up[0], g_down[0]], lands_mlp, "mlp", last_started)
    land_in, small_all = _scatter_wait_call(last_srcs, last_lands, lsend, lrecv, out_mlp[3][-1], [False, True],
                                            "grads_scatter_wait_last")
    out_in = update(_BIG[:1], [g_in[0]], [land_in], "in")
    g_pack, d_pack, m_pack, v_pack = _small_call(
        small_all, small_own, jnp.reshape(me, (1,)).astype(jnp.int32), _pack_small(w), _pack_small(m), _pack_small(v))

    grads, deltas, new_m, new_v = (_unpack_small(a) for a in (g_pack, d_pack, m_pack, v_pack))
    for k, res in enumerate((grads, deltas, new_m, new_v)):
        res[_BIG[0]] = out_in[k][0]
        for i, n in enumerate(mlp):
            res[n] = out_mlp[k][i]
    loss = g_pack[LOSS_ROW, 0]
    return (loss, dx[None], *[grads[n] for n in _WEIGHT_ORDER], *[deltas[n] for n in _WEIGHT_ORDER],
            *[new_m[n] for n in _WEIGHT_ORDER], *[new_v[n] for n in _WEIGHT_ORDER])
```

```python
import math

import jax
import jax.numpy as jnp
import numpy as np
from jax import lax
from jax.experimental import pallas as pl
from jax.experimental.pallas import tpu as pltpu

F32 = jnp.float32
MXU_DTYPE = jnp.bfloat16
WIRE_DTYPE = jnp.bfloat16

NORM_EPS = 1e-6
NEG_INF = -1e30
LANES = 128
HEAD_DIM = 64
N_HEADS = 8
POOL_WIDTH = 512
ATTN_WIDTH = 512
POOL_WINDOWS = (2, 4, 8, 16)
POOL_HALO = 16
DILATED_PATTERNS = ((128, 1), (512, 4), (2048, 16))
ATT_BLOCK = 128
ATT_SUPER = ATT_BLOCK * max(dl for _, dl in DILATED_PATTERNS)
ATT_UNITS = ATT_SUPER // ATT_BLOCK
N_BUCKETS = 32
NO_BUCKET = -1
MAX_DISTANCE = 2048
N_CHIPS = 4
N_DEV = 8
ADAM_LR, ADAM_B1, ADAM_B2, ADAM_EPS, ADAM_WD, ADAM_STEP = 0.001, 0.9, 0.999, 1e-08, 0.01, 10
VMEM_LIMIT = 56 * 1024 * 1024
MESH = pl.DeviceIdType.MESH
ANY = pl.BlockSpec(memory_space=pl.ANY)

SMALL_ROWS = 72
LOSS_ROW = 5


def _mm(a, b):
    return jnp.dot(a, b, preferred_element_type=F32)


def _mm_nt(a, b):
    return lax.dot_general(a, b, (((1,), (1,)), ((), ())), preferred_element_type=F32)


def _mm_tn(a, b):
    return lax.dot_general(a, b, (((0,), (0,)), ((), ())), preferred_element_type=F32)


def _params(sem=None, **kw):
    if sem is not None:
        kw["dimension_semantics"] = sem
    return pltpu.CompilerParams(vmem_limit_bytes=VMEM_LIMIT, **kw)


def _low_half():
    return lax.broadcasted_iota(jnp.int32, (1, LANES), 1) < HEAD_DIM


def _head_sum_bcast(y):
    lo = _low_half()
    outs = []
    for j in range(y.shape[1] // LANES):
        c = y[:, j * LANES:(j + 1) * LANES]
        s_lo = jnp.sum(jnp.where(lo, c, 0.0), axis=-1, keepdims=True)
        s_hi = jnp.sum(jnp.where(lo, 0.0, c), axis=-1, keepdims=True)
        outs.append(jnp.where(lo, s_lo, s_hi))
    return jnp.concatenate(outs, axis=-1)


def _rms_bwd(dn, hn, r):
    return r * (dn - hn * jnp.mean(dn * hn, axis=-1, keepdims=True))


def _t5_bucket_np(dist):
    max_exact = N_BUCKETS // 2
    d_f = np.maximum(dist, 1).astype(np.float32)
    ratio = (np.log(d_f / np.float32(max_exact)) / np.float32(math.log(MAX_DISTANCE / max_exact))).astype(np.float32)
    large = max_exact + (ratio * np.float32(N_BUCKETS - max_exact)).astype(np.int32)
    large = np.minimum(large, N_BUCKETS - 1)
    return np.where(dist < max_exact, dist, large).astype(np.int32)


def _window_offsets(dl):
    if dl == 1:
        return _by4_positions(ATT_BLOCK), _by4_positions(2 * ATT_BLOCK)
    return np.arange(ATT_BLOCK), np.arange(2 * ATT_BLOCK)


def _bucket_tables():
    tables = []
    for _, dl in DILATED_PATTERNS:
        qq, kk = _window_offsets(dl)
        dist = qq[:, None] + ATT_BLOCK - kk[None, :]
        bucket = _t5_bucket_np(np.clip(dist, 0, ATT_BLOCK) * dl)
        tables.append(np.where((dist >= 0) & (dist <= ATT_BLOCK), bucket, NO_BUCKET))
    return np.stack(tables).astype(np.int32)


def _previous_block_keys():
    return np.stack([np.broadcast_to(_window_offsets(dl)[1][None, :] < ATT_BLOCK, (ATT_BLOCK, 2 * ATT_BLOCK))
                     for _, dl in DILATED_PATTERNS])


def _f1_call(x, g1, win, poolw, pscale, qg, kg, tm):
    s, d = x.shape
    nblk = s // tm

    def body(x_ref, g1_ref, win_ref, pw_ref, ps_ref, qg_ref, kg_ref,
             a_ref, pooled_ref, ypool_ref, q32_ref, k32_ref, qn_ref, kn_ref, v_ref, ubuf):
        i = pl.program_id(0)
        xv = x_ref[...]
        r = lax.rsqrt(jnp.mean(xv * xv, axis=-1, keepdims=True) + NORM_EPS)
        a = ((xv * r) * g1_ref[...]).astype(MXU_DTYPE)
        a_ref[...] = a
        u = _mm(a, win_ref[0])
        q = _mm(a, win_ref[1])
        k = _mm(a, win_ref[2])
        v_ref[...] = _mm(a, win_ref[3])
        q32_ref[...] = q
        k32_ref[...] = k
        rq = lax.rsqrt(_head_sum_bcast(q * q) * (1.0 / HEAD_DIM) + NORM_EPS)
        qn_ref[...] = ((q * rq) * qg_ref[...]) * (HEAD_DIM ** -0.5)
        rk = lax.rsqrt(_head_sum_bcast(k * k) * (1.0 / HEAD_DIM) + NORM_EPS)
        kn_ref[...] = (k * rk) * kg_ref[...]

        ubuf[0:POOL_HALO, :] = jnp.where(i > 0, ubuf[tm:tm + POOL_HALO, :], 0.0)
        ubuf[POOL_HALO:POOL_HALO + tm, :] = u
        t = i * tm + lax.broadcasted_iota(jnp.int32, (tm, 1), 0)
        for g, w in enumerate(POOL_WINDOWS):
            ls = slice(g * LANES, (g + 1) * LANES)
            ug = u[:, ls]
            acc = ug
            for sh in range(1, w):
                acc = acc + ubuf[POOL_HALO - sh:POOL_HALO - sh + tm, ls]
            cnt = jnp.minimum(t + 1, w).astype(F32)
            pooled = (acc / cnt - ug).astype(MXU_DTYPE)
            pooled_ref[:, ls] = pooled
            ypool_ref[:, ls] = (_mm(pooled, pw_ref[g]) * ps_ref[:, ls]).astype(MXU_DTYPE)

    tok = lambda w: pl.BlockSpec((tm, w), lambda i: (i, 0))
    full = lambda shp: pl.BlockSpec(shp, lambda i: (0,) * len(shp))
    return pl.pallas_call(
        body, name="fwd_inproj",
        grid=(nblk,),
        in_specs=[tok(d), full((1, d)), full(win.shape), full(poolw.shape), full((1, POOL_WIDTH)),
                  full((1, ATTN_WIDTH)), full((1, ATTN_WIDTH))],
        out_specs=[tok(d), tok(POOL_WIDTH), tok(POOL_WIDTH), tok(ATTN_WIDTH), tok(ATTN_WIDTH),
                   tok(ATTN_WIDTH), tok(ATTN_WIDTH), tok(ATTN_WIDTH)],
        out_shape=[jax.ShapeDtypeStruct((s, d), MXU_DTYPE),
                   jax.ShapeDtypeStruct((s, POOL_WIDTH), MXU_DTYPE),
                   jax.ShapeDtypeStruct((s, POOL_WIDTH), MXU_DTYPE),
                   jax.ShapeDtypeStruct((s, ATTN_WIDTH), F32),
                   jax.ShapeDtypeStruct((s, ATTN_WIDTH), F32),
                   jax.ShapeDtypeStruct((s, ATTN_WIDTH), F32),
                   jax.ShapeDtypeStruct((s, ATTN_WIDTH), F32),
                   jax.ShapeDtypeStruct((s, ATTN_WIDTH), F32)],
        scratch_shapes=[pltpu.VMEM((tm + POOL_HALO, POOL_WIDTH), F32)],
        compiler_params=_params(("arbitrary",)),
    )(x, g1, win, poolw, pscale, qg, kg)


DEINT = 4
assert [dl for _, dl in DILATED_PATTERNS] == [1, DEINT, DEINT * DEINT]


def _by4_positions(n):
    pos = np.arange(n)
    return DEINT * (pos % (n // DEINT)) + pos // (n // DEINT)


def _masked_bias(b_ref, p, n):
    return b_ref[p, jnp.minimum(n, 1)].reshape(2 * ATT_BLOCK, 2 * ATT_BLOCK)


def _unit_rows(u, dl):
    assert isinstance(u, int)
    sq, sk = ATT_SUPER // DEINT, 2 * ATT_SUPER // DEINT
    if dl == 1:
        n = ATT_BLOCK // DEINT
        return (u, [pl.ds(r * sq + n * u, n) for r in range(DEINT)],
                [pl.ds(r * sk + sk // 2 + n * (u - 1), 2 * n) for r in range(DEINT)])
    if dl == DEINT:
        r, b = u % DEINT, u // DEINT
        return (b, [pl.ds(r * sq + ATT_BLOCK * b, ATT_BLOCK)],
                [pl.ds(r * sk + sk // 2 + ATT_BLOCK * (b - 1), 2 * ATT_BLOCK)])
    r, a = u % DEINT, u // DEINT
    return 0, [pl.ds(r * sq + a, ATT_BLOCK, stride=DEINT)], [pl.ds(r * sk + a, 2 * ATT_BLOCK, stride=DEINT)]


def _take(ref, runs):
    parts = [ref[run, :] for run in runs]
    return parts[0] if len(parts) == 1 else jnp.concatenate(parts, axis=0)


def _put(ref, runs, value, add=False):
    n = value.shape[0] // len(runs)
    for i, run in enumerate(runs):
        part = value[i * n:(i + 1) * n]
        ref[run, :] = ref[run, :] + part if add else part


def _deinterleave(dst, src, n):
    seg = n // DEINT
    for r in range(DEINT):
        dst[r * seg:(r + 1) * seg, :] = src[pl.ds(r, seg, stride=DEINT), :]


def _deinterleave_pair(dst, prev, cur):
    seg = prev.shape[0] // DEINT
    for r in range(DEINT):
        dst[2 * r * seg:(2 * r + 1) * seg, :] = prev[pl.ds(r, seg, stride=DEINT), :]
        dst[(2 * r + 1) * seg:(2 * r + 2) * seg, :] = cur[pl.ds(r, seg, stride=DEINT), :]


def _interleave(dst, src, n, offset=0):
    seg = n // DEINT
    stride = src.shape[0] // DEINT
    for r in range(DEINT):
        dst[pl.ds(r, seg, stride=DEINT), :] = src[r * stride + offset:r * stride + offset + seg, :]


def _attn_fwd_call(qn, kn, v, bias):
    s, w = qn.shape
    nsb = s // ATT_SUPER
    npair = w // LANES

    def body(q_ref, kc_ref, kp_ref, vc_ref, vp_ref, b_ref, o_ref, lse_ref, qf, kf, vf, acc_s, m_s, l_s):
        sb = pl.program_id(1)
        _deinterleave(qf, q_ref, ATT_SUPER)
        _deinterleave_pair(kf, kp_ref, kc_ref)
        _deinterleave_pair(vf, vp_ref, vc_ref)
        lo = _low_half()
        for p, (_, dl) in enumerate(DILATED_PATTERNS):
            def unit(u, carry, p=p, dl=dl):
                b, rows_q, rows_k = _unit_rows(u, dl)
                qp = _take(qf, rows_q).astype(MXU_DTYPE)
                kcat = _take(kf, rows_k).astype(MXU_DTYPE)
                vcat = _take(vf, rows_k).astype(MXU_DTYPE)
                zero = jnp.zeros_like(qp)
                q2 = jnp.concatenate([jnp.where(lo, qp, zero), jnp.where(lo, zero, qp)], axis=0)
                sc = _mm_nt(q2, kcat) + _masked_bias(b_ref, p, sb * (ATT_UNITS // dl) + b)
                m2 = jnp.max(sc, axis=-1, keepdims=True)
                pr = jnp.exp(sc - m2)
                l2 = jnp.sum(pr, axis=-1, keepdims=True)
                acc2 = _mm(pr.astype(MXU_DTYPE), vcat)
                acc = jnp.where(lo, acc2[:ATT_BLOCK], acc2[ATT_BLOCK:])
                m = jnp.where(lo, m2[:ATT_BLOCK], m2[ATT_BLOCK:])
                l = jnp.where(lo, l2[:ATT_BLOCK], l2[ATT_BLOCK:])
                if p == 0:
                    _put(acc_s, rows_q, acc)
                    _put(m_s, rows_q, m)
                    _put(l_s, rows_q, l)
                else:
                    m_old = _take(m_s, rows_q)
                    m_new = jnp.maximum(m_old, m)
                    a_old = jnp.exp(m_old - m_new)
                    a_new = jnp.exp(m - m_new)
                    _put(acc_s, rows_q, a_old * _take(acc_s, rows_q) + a_new * acc)
                    _put(l_s, rows_q, a_old * _take(l_s, rows_q) + a_new * l)
                    _put(m_s, rows_q, m_new)
                return carry

            for u in range(ATT_UNITS):
                unit(u, None)
        l = l_s[...]
        acc_s[...] = acc_s[...] / l
        m_s[...] = m_s[...] + jnp.log(l)
        _interleave(o_ref, acc_s, ATT_SUPER)
        _interleave(lse_ref, m_s, ATT_SUPER)

    cur = pl.BlockSpec((ATT_SUPER, LANES), lambda j, t: (t, j))
    prev = pl.BlockSpec((ATT_SUPER, LANES), lambda j, t: (jnp.maximum(t - 1, 0), j))
    bspec = pl.BlockSpec((len(DILATED_PATTERNS), 2, 2, ATT_BLOCK, 2 * ATT_BLOCK), lambda j, t: (0, 0, j, 0, 0))
    return pl.pallas_call(
        body, name="attn_fwd",
        grid=(npair, nsb),
        in_specs=[cur, cur, prev, cur, prev, bspec],
        out_specs=[cur, cur],
        out_shape=[jax.ShapeDtypeStruct((s, w), F32), jax.ShapeDtypeStruct((s, w), F32)],
        scratch_shapes=[pltpu.VMEM((ATT_SUPER, LANES), F32), pltpu.VMEM((2 * ATT_SUPER, LANES), F32),
                        pltpu.VMEM((2 * ATT_SUPER, LANES), F32), pltpu.VMEM((ATT_SUPER, LANES), F32),
                        pltpu.VMEM((ATT_SUPER, LANES), F32), pltpu.VMEM((ATT_SUPER, LANES), F32)],
        compiler_params=_params(("arbitrary", "arbitrary")),
    )(qn, kn, kn, v, v, bias)


def _attn_bwd_call(qn, kn, v, do, lse, delta, bias, dep=None):
    s, w = qn.shape
    nsb = s // ATT_SUPER
    npair = w // LANES
    deps = [] if dep is None else [dep]

    def body(q_ref, kc_ref, kp_ref, vc_ref, vp_ref, do_ref, lse_ref, dlt_ref, b_ref, *rest):
        dq_ref, dk_ref, dv_ref, db_ref, qf, kf, vf, dof, lsef, dltf, dqf, dkf, dvf = rest[len(deps):]
        step = pl.program_id(1)
        sb = nsb - 1 - step
        seg = ATT_SUPER // DEINT
        _deinterleave(qf, q_ref, ATT_SUPER)
        _deinterleave(dof, do_ref, ATT_SUPER)
        _deinterleave_pair(kf, kp_ref, kc_ref)
        _deinterleave_pair(vf, vp_ref, vc_ref)
        _deinterleave(lsef, lse_ref, ATT_SUPER)
        _deinterleave(dltf, dlt_ref, ATT_SUPER)

        db_ref[...] = jnp.where(step > 0, db_ref[...], 0.0)
        for acc in (dkf, dvf):
            for r in range(DEINT):
                this, before = pl.ds((2 * r + 1) * seg, seg), pl.ds(2 * r * seg, seg)
                acc[this, :] = jnp.where(step > 0, acc[before, :], 0.0)
                acc[before, :] = jnp.zeros((seg, LANES), F32)
        lo = _low_half()
        for p, (_, dl) in enumerate(DILATED_PATTERNS):
            def unit(u, carry, p=p, dl=dl):
                b, rows_q, rows_k = _unit_rows(u, dl)
                qp = _take(qf, rows_q).astype(MXU_DTYPE)
                dop = _take(dof, rows_q).astype(MXU_DTYPE)
                kcat = _take(kf, rows_k).astype(MXU_DTYPE)
                vcat = _take(vf, rows_k).astype(MXU_DTYPE)
                lse2 = _take(lsef, rows_q)
                dlt2 = _take(dltf, rows_q)
                zero = jnp.zeros_like(qp)
                q2 = jnp.concatenate([jnp.where(lo, qp, zero), jnp.where(lo, zero, qp)], axis=0)
                do2 = jnp.concatenate([jnp.where(lo, dop, zero), jnp.where(lo, zero, dop)], axis=0)
                lse_c = jnp.concatenate([lse2[:, 0:1], lse2[:, HEAD_DIM:HEAD_DIM + 1]], axis=0)
                dlt_c = jnp.concatenate([dlt2[:, 0:1], dlt2[:, HEAD_DIM:HEAD_DIM + 1]], axis=0)
                sc = _mm_nt(q2, kcat) + _masked_bias(b_ref, p, sb * (ATT_UNITS // dl) + b)
                pr = jnp.exp(sc - lse_c)
                ds = pr * (_mm_nt(do2, vcat) - dlt_c)
                db_ref[p] += ds.reshape(2, ATT_BLOCK, 2 * ATT_BLOCK)
                ds_c = ds.astype(MXU_DTYPE)
                dq2 = _mm(ds_c, kcat)
                dk = _mm_tn(ds_c, q2)
                dv = _mm_tn(pr.astype(MXU_DTYPE), do2)
                dq = jnp.where(lo, dq2[:ATT_BLOCK], dq2[ATT_BLOCK:])
                _put(dqf, rows_q, dq, add=p > 0)
                _put(dkf, rows_k, dk, add=True)
                _put(dvf, rows_k, dv, add=True)
                return carry

            for u in range(ATT_UNITS):
                unit(u, None)
        _interleave(dq_ref, dqf, ATT_SUPER)
        _interleave(dk_ref, dkf, ATT_SUPER, offset=seg)
        _interleave(dv_ref, dvf, ATT_SUPER, offset=seg)

    cur = pl.BlockSpec((ATT_SUPER, LANES), lambda j, t: (nsb - 1 - t, j))
    prev = pl.BlockSpec((ATT_SUPER, LANES), lambda j, t: (jnp.maximum(nsb - 2 - t, 0), j))
    npat = len(DILATED_PATTERNS)
    bspec = pl.BlockSpec((npat, 2, 2, ATT_BLOCK, 2 * ATT_BLOCK), lambda j, t: (0, 0, j, 0, 0))
    dbspec = pl.BlockSpec((npat, 2, ATT_BLOCK, 2 * ATT_BLOCK), lambda j, t: (0, j, 0, 0))
    sup = lambda: pltpu.VMEM((ATT_SUPER, LANES), F32)
    sup2 = lambda: pltpu.VMEM((2 * ATT_SUPER, LANES), F32)
    return pl.pallas_call(
        body, name="attn_bwd",
        grid=(npair, nsb),
        in_specs=[cur, cur, prev, cur, prev, cur, cur, cur, bspec] + [ANY] * len(deps),
        out_specs=[cur, cur, cur, dbspec],
        out_shape=[jax.ShapeDtypeStruct((s, w), F32)] * 3
        + [jax.ShapeDtypeStruct((npat, N_HEADS, ATT_BLOCK, 2 * ATT_BLOCK), F32)],
        scratch_shapes=[sup(), sup2(), sup2(), sup(), sup(), sup(), sup(), sup2(), sup2()],
        compiler_params=_params(("arbitrary", "arbitrary")),
    )(qn, kn, kn, v, v, do, lse, delta, bias, *deps)


def _bias_table_work(rel_bias):
    buckets = jnp.asarray(_bucket_tables())
    prev_keys = jnp.asarray(_previous_block_keys().astype(np.int32))
    npat = buckets.shape[0]

    def body(rb_ref, bk_ref, pk_ref, out_ref):
        for p in range(npat):
            for half in range(2):
                ks = slice(half * ATT_BLOCK, (half + 1) * ATT_BLOCK)
                bk = bk_ref[p, :, ks]
                absent = pk_ref[p, :, ks] != 0
                for h in range(N_HEADS):
                    def pick(b, acc, h=h, bk=bk):
                        return jnp.where(bk == b, rb_ref[b, h], acc)

                    tab = lax.fori_loop(0, N_BUCKETS, pick, jnp.full((ATT_BLOCK, ATT_BLOCK), NEG_INF, F32))
                    out_ref[p, 1, h, :, ks] = tab
                    out_ref[p, 0, h, :, ks] = jnp.where(absent, NEG_INF, tab)

    vmem = pl.BlockSpec(memory_space=pltpu.VMEM)
    return ([rel_bias, buckets, prev_keys], [pl.BlockSpec(memory_space=pltpu.SMEM), vmem, vmem],
            jax.ShapeDtypeStruct((npat, 2, N_HEADS, ATT_BLOCK, 2 * ATT_BLOCK), F32), body)


def _rel_bias_grad_call(dbias, buckets):
    npat, nh = dbias.shape[0], dbias.shape[1]

    def body(db_ref, bk_ref, out_ref):
        lane = lax.broadcasted_iota(jnp.int32, (nh, LANES), 1)
        out = jnp.zeros((nh, LANES), F32)
        for b in range(N_BUCKETS):
            tot = jnp.zeros((nh, 1), F32)
            for p in range(npat):
                hit = jnp.where(bk_ref[p][None] == b, db_ref[p], 0.0)
                tot = tot + jnp.sum(jnp.sum(hit, axis=1), axis=-1, keepdims=True)
            out = jnp.where(lane == b, tot, out)
        out_ref[...] = out

    return pl.pallas_call(
        body, name="rel_bias_grad",
        out_shape=jax.ShapeDtypeStruct((nh, LANES), F32),
        compiler_params=_params(),
    )(dbias, buckets)


def _f2_call(x, tgt, ypool, o, wout, wup, wdown, g2, tm):
    s, d = x.shape
    nblk = s // tm
    nch, _, fch = wup.shape
    dff = nch * fch
    mixw = POOL_WIDTH + ATTN_WIDTH

    def body(x_ref, t_ref, yp_ref, o_ref, g2_ref, wout_hbm, wup_hbm, wdown_hbm,
             mixed_ref, c_ref, ff_ref, dz_ref, dy_ref, dh1_ref, dyp_ref, do_ref, dlt_ref, dg2_ref, loss_ref,
             wout_v, wup_v, wdown_v, rz, wsem):
        i = pl.program_id(0)

        @pl.when(i == 0)
        def _():
            copies = [pltpu.make_async_copy(wout_hbm, wout_v, wsem.at[0])]
            for j in range(nch):
                copies.append(pltpu.make_async_copy(wup_hbm.at[j], wup_v.at[j], wsem.at[1 + 2 * j]))
                copies.append(pltpu.make_async_copy(wdown_hbm.at[j], wdown_v.at[j], wsem.at[2 + 2 * j]))
            for cp in copies:
                cp.start()
            dg2_ref[...] = jnp.zeros(dg2_ref.shape, F32)
            loss_ref[...] = jnp.zeros(loss_ref.shape, F32)
            for cp in copies:
                cp.wait()

        o = o_ref[...]
        mixed = jnp.concatenate([yp_ref[...], o.astype(MXU_DTYPE)], axis=-1)
        mixed_ref[...] = mixed
        h1 = x_ref[...] + _mm(mixed, wout_v[...])
        r2 = lax.rsqrt(jnp.mean(h1 * h1, axis=-1, keepdims=True) + NORM_EPS)
        hn = h1 * r2
        c = (hn * g2_ref[...]).astype(MXU_DTYPE)
        c_ref[...] = c
        y = h1
        for j in range(nch):
            cs = slice(j * fch, (j + 1) * fch)
            z = jnp.maximum(_mm(c, wup_v[j]), 0.0)
            rz[:, cs] = z
            ff = (z * z).astype(MXU_DTYPE)
            ff_ref[:, cs] = ff
            y = y + _mm(ff, wdown_v[j])
        err = y - t_ref[...]
        loss_ref[...] += jnp.sum(err * err) * (0.5 / d)
        dy = err * (1.0 / d)
        dy_c = dy.astype(MXU_DTYPE)
        dy_ref[...] = dy_c
        dc = jnp.zeros((tm, d), F32)
        for j in range(nch):
            cs = slice(j * fch, (j + 1) * fch)
            dz = (_mm_nt(dy_c, wdown_v[j]) * (2.0 * rz[:, cs])).astype(MXU_DTYPE)
            dz_ref[:, cs] = dz
            dc = dc + _mm_nt(dz, wup_v[j])
        dg2_ref[...] += jnp.sum(dc * hn, axis=0, keepdims=True)
        dh1 = dy + _rms_bwd(dc * g2_ref[...], hn, r2)
        dh1_ref[...] = dh1
        dmix = _mm_nt(dh1.astype(MXU_DTYPE), wout_v[...])
        dyp_ref[...] = dmix[:, :POOL_WIDTH]
        do = dmix[:, POOL_WIDTH:]
        do_ref[...] = do
        dlt_ref[...] = _head_sum_bcast(do * o)

    tok = lambda w: pl.BlockSpec((tm, w), lambda i: (i, 0))
    const = lambda shp: pl.BlockSpec(shp, lambda i: (0,) * len(shp))
    return pl.pallas_call(
        body, name="fwd_mlp_bwd_mlp",
        grid=(nblk,),
        in_specs=[tok(d), tok(d), tok(POOL_WIDTH), tok(ATTN_WIDTH), const((1, d)), ANY, ANY, ANY],
        out_specs=[tok(mixw), tok(d), tok(dff), tok(dff), tok(d), tok(d), tok(POOL_WIDTH), tok(ATTN_WIDTH),
                   tok(ATTN_WIDTH), const((1, d)), const((1, LANES))],
        out_shape=[jax.ShapeDtypeStruct((s, mixw), MXU_DTYPE),
                   jax.ShapeDtypeStruct((s, d), MXU_DTYPE),
                   jax.ShapeDtypeStruct((s, dff), MXU_DTYPE),
                   jax.ShapeDtypeStruct((s, dff), MXU_DTYPE),
                   jax.ShapeDtypeStruct((s, d), MXU_DTYPE),
                   jax.ShapeDtypeStruct((s, d), F32),
                   jax.ShapeDtypeStruct((s, POOL_WIDTH), F32),
                   jax.ShapeDtypeStruct((s, ATTN_WIDTH), F32),
                   jax.ShapeDtypeStruct((s, ATTN_WIDTH), F32),
                   jax.ShapeDtypeStruct((1, d), F32),
                   jax.ShapeDtypeStruct((1, LANES), F32)],
        scratch_shapes=[pltpu.VMEM(wout.shape, MXU_DTYPE), pltpu.VMEM(wup.shape, MXU_DTYPE),
                        pltpu.VMEM(wdown.shape, MXU_DTYPE), pltpu.VMEM((tm, dff), F32),
                        pltpu.SemaphoreType.DMA((1 + 2 * nch,))],
        compiler_params=_params(("arbitrary",)),
    )(x, tgt, ypool, o, g2, wout, wup, wdown)


def _bproj_call(dqn, dkn, dv, q32, k32, dypool, pooled, x, dh1, win, poolw, pscale, qg, kg, g1, tm):
    s, d = x.shape
    nblk = s // tm
    ngrp = len(POOL_WINDOWS)

    def body(dqn_ref, dkn_ref, dv_ref, q_ref, k_ref, dyp_ref, pooled_ref, x_ref, dh1_ref,
             win_hbm, pw_ref, ps_ref, qg_ref, kg_ref, g1_ref,
             dx_ref, dproj_ref, dg1_ref, dqg_ref, dkg_ref, dpw_ref, dps_ref, win_v, ebuf):
        step = pl.program_id(0)
        i = nblk - 1 - step

        @pl.when(step == 0)
        def _():
            pltpu.sync_copy(win_hbm, win_v)
            dg1_ref[...] = jnp.zeros(dg1_ref.shape, F32)
            dqg_ref[...] = jnp.zeros(dqg_ref.shape, F32)
            dkg_ref[...] = jnp.zeros(dkg_ref.shape, F32)
            dpw_ref[...] = jnp.zeros(dpw_ref.shape, F32)
            dps_ref[...] = jnp.zeros(dps_ref.shape, F32)
            ebuf[tm:tm + POOL_HALO, :] = jnp.zeros((POOL_HALO, POOL_WIDTH), F32)

        @pl.when(step > 0)
        def _():
            ebuf[tm:tm + POOL_HALO, :] = ebuf[0:POOL_HALO, :]

        def qk_bwd(dn_sum, raw, gain, scale, dgain_ref):
            rr = lax.rsqrt(_head_sum_bcast(raw * raw) * (1.0 / HEAD_DIM) + NORM_EPS)
            hn = raw * rr
            dgain_ref[...] += jnp.sum(dn_sum * hn, axis=0, keepdims=True) * scale
            dn = dn_sum * (gain * scale)
            return rr * (dn - hn * (_head_sum_bcast(dn * hn) * (1.0 / HEAD_DIM)))

        dq = qk_bwd(dqn_ref[...], q_ref[...], qg_ref[...], HEAD_DIM ** -0.5, dqg_ref)
        dk = qk_bwd(dkn_ref[...], k_ref[...], kg_ref[...], 1.0, dkg_ref)

        t = i * tm + lax.broadcasted_iota(jnp.int32, (tm, 1), 0)
        dpooled = []
        for g, w in enumerate(POOL_WINDOWS):
            ls = slice(g * LANES, (g + 1) * LANES)
            dm = dyp_ref[:, ls]
            pg = pooled_ref[:, ls]
            dps_ref[:, ls] += jnp.sum(dm * _mm(pg, pw_ref[g]), axis=0, keepdims=True)
            dms = (dm * ps_ref[:, ls]).astype(MXU_DTYPE)
            dpw_ref[g] += _mm_tn(pg, dms)
            dpg = _mm_nt(dms, pw_ref[g])
            dpooled.append(dpg)
            ebuf[0:tm, ls] = dpg / jnp.minimum(t + 1, w).astype(F32)
        du = []
        for g, w in enumerate(POOL_WINDOWS):
            ls = slice(g * LANES, (g + 1) * LANES)
            acc = ebuf[0:tm, ls]
            for sh in range(1, w):
                acc = acc + ebuf[sh:sh + tm, ls]
            du.append(acc - dpooled[g])
        parts = [jnp.concatenate(du, axis=-1), dq, dk, dv_ref[...]]
        da = jnp.zeros((tm, d), F32)
        for p, part in enumerate(parts):
            pc = part.astype(MXU_DTYPE)
            dproj_ref[:, p * POOL_WIDTH:(p + 1) * POOL_WIDTH] = pc
            da = da + _mm_nt(pc, win_v[p])
        xv = x_ref[...]
        r = lax.rsqrt(jnp.mean(xv * xv, axis=-1, keepdims=True) + NORM_EPS)
        xn = xv * r
        dg1_ref[...] += jnp.sum(da * xn, axis=0, keepdims=True)
        dx_ref[...] = dh1_ref[...] + _rms_bwd(da * g1_ref[...], xn, r)

    tok = lambda w: pl.BlockSpec((tm, w), lambda t: (nblk - 1 - t, 0))
    const = lambda shp: pl.BlockSpec(shp, lambda t: (0,) * len(shp))
    return pl.pallas_call(
        body, name="bwd_inproj",
        grid=(nblk,),
        in_specs=[tok(ATTN_WIDTH)] * 5 + [tok(POOL_WIDTH), tok(POOL_WIDTH), tok(d), tok(d),
                                          ANY, const(poolw.shape), const((1, POOL_WIDTH)), const((1, ATTN_WIDTH)),
                                          const((1, ATTN_WIDTH)), const((1, d))],
        out_specs=[tok(d), tok(4 * POOL_WIDTH), const((1, d)), const((1, ATTN_WIDTH)), const((1, ATTN_WIDTH)),
                   const((ngrp, LANES, LANES)), const((1, POOL_WIDTH))],
        out_shape=[jax.ShapeDtypeStruct((s, d), F32),
                   jax.ShapeDtypeStruct((s, 4 * POOL_WIDTH), MXU_DTYPE),
                   jax.ShapeDtypeStruct((1, d), F32),
                   jax.ShapeDtypeStruct((1, ATTN_WIDTH), F32),
                   jax.ShapeDtypeStruct((1, ATTN_WIDTH), F32),
                   jax.ShapeDtypeStruct((ngrp, LANES, LANES), F32),
                   jax.ShapeDtypeStruct((1, POOL_WIDTH), F32)],
        scratch_shapes=[pltpu.VMEM(win.shape, MXU_DTYPE), pltpu.VMEM((tm + POOL_HALO, POOL_WIDTH), F32)],
        compiler_params=_params(("arbitrary",)),
    )(dqn, dkn, dv, q32, k32, dypool, pooled, x, dh1, win, poolw, pscale, qg, kg, g1)


def _wgrad_call(a, b, bm, bn, bk, out_shape, out_block, out_index, name):
    s, m = a.shape
    _, n = b.shape
    nk = s // bk

    def body(a_ref, b_ref, o_ref, wire_ref):
        k = pl.program_id(2)
        acc = jnp.where(k > 0, o_ref[...], 0.0) + _mm_tn(a_ref[...].astype(MXU_DTYPE), b_ref[...].astype(MXU_DTYPE))
        o_ref[...] = acc
        wire_ref[...] = acc.astype(WIRE_DTYPE)

    return pl.pallas_call(
        body, name=name,
        grid=(m // bm, n // bn, nk),
        in_specs=[pl.BlockSpec((bk, bm), lambda i, j, k: (k, i)), pl.BlockSpec((bk, bn), lambda i, j, k: (k, j))],
        out_specs=[pl.BlockSpec(out_block, out_index)] * 2,
        out_shape=[jax.ShapeDtypeStruct(out_shape, F32), jax.ShapeDtypeStruct(out_shape, WIRE_DTYPE)],
        compiler_params=_params(("arbitrary", "arbitrary", "arbitrary")),
    )(a, b)


def _local_grads(x, tgt, g1, win, poolw, pscale, qg, kg, bias, g2, mlp_weights, on_mlp_grads=None):
    s, d = x.shape
    g1r, g2r = g1.reshape(1, d), g2.reshape(1, d)
    psr = pscale.reshape(1, POOL_WIDTH)
    qgr = jnp.tile(qg, N_HEADS).reshape(1, ATTN_WIDTH)
    kgr = jnp.tile(kg, N_HEADS).reshape(1, ATTN_WIDTH)
    pw_c = poolw.astype(MXU_DTYPE)
    buckets = jnp.asarray(_bucket_tables())
    bk = min(s, 4096)

    a, pooled, ypool, q32, k32, qn, kn, v = _f1_call(x, g1r, win, pw_c, psr, qgr, kgr, tm=1024)
    o, lse = _attn_fwd_call(qn, kn, v, bias)
    wout, wup, wdown = mlp_weights(o)
    mixed, c, ff, dz, dy, dh1, dypool, do, delta, dg2, loss = _f2_call(x, tgt, ypool, o, wout, wup, wdown, g2r, tm=256)
    dff = ff.shape[1]
    g_out = [g.reshape(N_CHIPS, d // N_CHIPS, d)
             for g in _wgrad_call(mixed, dh1, d, d, bk // 4, (d, d), (d, d), lambda i, j, k: (0, 0), "wgrad_out")]
    g_up = _wgrad_call(c, dz, d, dff // N_CHIPS, bk, (N_CHIPS, d, dff // N_CHIPS), (None, d, dff // N_CHIPS),
                       lambda i, j, k: (j, 0, 0), "wgrad_up")
    g_down = _wgrad_call(ff, dy, dff // N_CHIPS, d, bk, (N_CHIPS, dff // N_CHIPS, d), (None, dff // N_CHIPS, d),
                         lambda i, j, k: (i, 0, 0), "wgrad_down")
    dep = None if on_mlp_grads is None else on_mlp_grads(g_out[1], g_up[1], g_down[1])
    dqn, dkn, dv, dbias = _attn_bwd_call(qn, kn, v, do, lse, delta, bias, dep)
    dx, dproj, dg1, dqg, dkg, dpw, dps = _bproj_call(
        dqn, dkn, dv, q32, k32, dypool, pooled, x, dh1, win, pw_c, psr, qgr, kgr, g1r, tm=512)
    nin = dproj.shape[1] // N_CHIPS
    g_in = _wgrad_call(a, dproj, d, nin, bk, (N_CHIPS, d, nin), (None, d, nin), lambda i, j, k: (j, 0, 0), "wgrad_in")
    drb = _rel_bias_grad_call(dbias, buckets)
    small = dict(
        mix_norm_g=dg1.reshape(d), mlp_norm_g=dg2.reshape(d), pool_scale=dps.reshape(POOL_WIDTH),
        q_norm_g=dqg.reshape(ATTN_WIDTH), k_norm_g=dkg.reshape(ATTN_WIDTH),
        rel_bias=drb[:, :N_BUCKETS].T, pool_w=dpw)
    return loss[0, 0], dx, (g_in, g_out, g_up, g_down), small


def _coords():
    return lax.axis_index("x"), lax.axis_index("y"), lax.axis_index("c")


def _other_chips(x, y):
    return [(1 - x, y), (x, 1 - y), (1 - x, 1 - y)]


def _remote(src, dst, send_sem, recv_sem, dev):
    return pltpu.make_async_remote_copy(src_ref=src, dst_ref=dst, send_sem=send_sem, recv_sem=recv_sem,
                                        device_id=dev, device_id_type=MESH)


PAIR_FORWARD_ID = 1
PAIR_ALLGATHER_ID = 2


def _sibling_handshake():
    x, y, c = _coords()
    barrier = pltpu.get_barrier_semaphore()
    pl.semaphore_signal(barrier, inc=1, device_id=(x, y, 1 - c), device_id_type=MESH)
    pl.semaphore_wait(barrier, 1)


def _halves(a):
    return a.reshape(a.shape[:-2] + (2, a.shape[-2] // 2, a.shape[-1]))


def _place_shards_call(shards, chip_idx, nch):
    nw = len(shards)

    def body(chip_ref, *refs):
        for w in range(nw):
            refs[nw + w][...] = refs[w][...].astype(WIRE_DTYPE)

    in_specs = [pl.BlockSpec((s.shape[0] // nch, s.shape[1]), lambda i, chip_ref: (i, 0)) for s in shards]
    out_specs = [pl.BlockSpec((None, s.shape[0] // nch, s.shape[1]), lambda i, chip_ref: (chip_ref[0], i, 0))
                 for s in shards]
    return pl.pallas_call(
        body, name="weights_place",
        grid_spec=pltpu.PrefetchScalarGridSpec(num_scalar_prefetch=1, grid=(nch,),
                                               in_specs=in_specs, out_specs=out_specs),
        out_shape=[jax.ShapeDtypeStruct((N_CHIPS,) + s.shape, WIRE_DTYPE) for s in shards],
        compiler_params=_params(("arbitrary",)),
    )(chip_idx, *shards)


def _allgather_call(placed, from_chips, name, meanwhile=None):
    nw = len(placed)
    ncp = 3 * nw
    extra, extra_specs, extra_shape, extra_body = meanwhile if meanwhile else ([], [], None, None)
    ne = len(extra)

    def body(*refs):
        outs = refs[nw + ne:2 * nw + ne]
        send1, recv1, send2, recv2 = refs[-4:]
        x, y, c = _coords()
        chip = 2 * x + y
        others = _other_chips(x, y)
        first, passed = [], []
        if not from_chips:
            _sibling_handshake()
        if from_chips:
            for w in range(nw):
                for k, (ox, oy) in enumerate(others):
                    mine = outs[w].at[chip, c]
                    cp = _remote(mine, mine, send1.at[3 * w + k], recv1.at[3 * w + k], (ox, oy, c))
                    cp.start()
                    first.append(cp)
        if meanwhile:
            extra_body(*refs[nw:nw + ne], refs[2 * nw + ne])
        for w in range(nw):
            for k, (ox, oy) in enumerate(others):
                piece = outs[w].at[2 * ox + oy, c]
                if from_chips:
                    _remote(piece, piece, send1.at[3 * w + k], recv1.at[3 * w + k], (ox, oy, c)).wait_recv()
                cp = _remote(piece, piece, send2.at[3 * w + k], recv2.at[3 * w + k], (x, y, 1 - c))
                cp.start()
                passed.append(cp)
        for w in range(nw):
            for k, (ox, oy) in enumerate(others):
                piece = outs[w].at[2 * ox + oy, 1 - c]
                _remote(piece, piece, send2.at[3 * w + k], recv2.at[3 * w + k], (x, y, 1 - c)).wait_recv()
        for cp in first + passed:
            cp.wait_send()

    return pl.pallas_call(
        body, name=name,
        in_specs=[ANY] * nw + list(extra_specs),
        out_specs=[ANY] * nw + ([pl.BlockSpec(memory_space=pltpu.VMEM)] if meanwhile else []),
        out_shape=[jax.ShapeDtypeStruct(s.shape, s.dtype) for s in placed] + ([extra_shape] if meanwhile else []),
        input_output_aliases={w: w for w in range(nw)},
        scratch_shapes=[pltpu.SemaphoreType.DMA((ncp,))] * 4,
        compiler_params=_params() if from_chips else _params(collective_id=PAIR_FORWARD_ID),
    )(*placed, *extra)


HBM_SPEC = pl.BlockSpec(memory_space=pltpu.HBM)
SEM_SPEC = pl.BlockSpec(memory_space=pltpu.SEMAPHORE)
SPLIT_EFFECT = pltpu.SideEffectType.DATAFLOW_SIDE_EFFECTING


def _in_hbm(a):
    return pltpu.with_memory_space_constraint(a, pltpu.HBM)


def _gather_copies(bufs, send, recv):
    x, y, c = _coords()
    chip = 2 * x + y
    cps = []
    for w, buf in enumerate(bufs):
        for k, (ox, oy) in enumerate(_other_chips(x, y)):
            mine, theirs = buf.at[chip, c], buf.at[2 * ox + oy, c]
            sems = (send.at[3 * w + k], recv.at[3 * w + k], (ox, oy, c))
            cps.append((_remote(mine, mine, *sems), _remote(theirs, theirs, *sems)))
    return cps


def _gather_start_call(bufs, after):
    nw = len(bufs)

    def body(*refs):
        ins, send, recv, token = refs[:nw], refs[nw + 1], refs[nw + 2], refs[2 * nw + 3]
        for out, _ in _gather_copies(ins, send, recv):
            out.start()
        token[...] = jnp.zeros(token.shape, F32)

    res = pl.pallas_call(
        body, name="weights_gather_start",
        in_specs=[HBM_SPEC] * nw + [ANY],
        out_specs=[SEM_SPEC, SEM_SPEC] + [HBM_SPEC] * nw + [pl.BlockSpec(memory_space=pltpu.VMEM)],
        out_shape=[pltpu.SemaphoreType.DMA((3 * nw,)), pltpu.SemaphoreType.DMA((3 * nw,))]
        + [pltpu.HBM(b.shape, b.dtype) for b in bufs] + [jax.ShapeDtypeStruct((8, LANES), F32)],
        input_output_aliases={w: 2 + w for w in range(nw)},
        compiler_params=pltpu.CompilerParams(has_side_effects=SPLIT_EFFECT),
    )(*[_in_hbm(b) for b in bufs], after)
    return res[0], res[1], list(res[2:2 + nw]), res[2 + nw]


def _gather_wait_call(bufs, send, recv, after):
    nw = len(bufs)

    def body(*refs):
        ins, send, recv = refs[:nw], refs[nw], refs[nw + 1]
        for out, back in _gather_copies(ins, send, recv):
            out.wait_send()
            back.wait_recv()

    return pl.pallas_call(
        body, name="weights_gather_wait",
        in_specs=[HBM_SPEC] * nw + [SEM_SPEC, SEM_SPEC, ANY],
        out_specs=[HBM_SPEC] * nw,
        out_shape=[pltpu.HBM(b.shape, b.dtype) for b in bufs],
        input_output_aliases={w: w for w in range(nw)},
        compiler_params=pltpu.CompilerParams(has_side_effects=SPLIT_EFFECT),
    )(*bufs, send, recv, after)


def _scatter_copies(srcs, lands, send, recv, wholes):
    x, y, c = _coords()
    me = 4 * x + 2 * y + c
    cps = []
    for w, (src, land) in enumerate(zip(srcs, lands)):
        for r in range(1, N_DEV):
            px, py, pc = ((1 - x) if r & 4 else x, (1 - y) if r & 2 else y, (1 - c) if r & 1 else c)
            sems = (send.at[(N_DEV - 1) * w + r - 1], recv.at[(N_DEV - 1) * w + r - 1], (px, py, pc))
            piece = src if wholes[w] else src.at[2 * px + py, pc]
            cps.append((_remote(piece, land.at[me], *sems), _remote(piece, land.at[4 * px + 2 * py + pc], *sems)))
    return cps


def _scatter_start_call(srcs, lands, wholes, name):
    nw = len(srcs)
    ncp = (N_DEV - 1) * nw

    def body(*refs):
        ins, lnd, send, recv, token = refs[:nw], refs[nw:2 * nw], refs[2 * nw], refs[2 * nw + 1], refs[4 * nw + 2]
        for out, _ in _scatter_copies(ins, lnd, send, recv, wholes):
            out.start()
        token[...] = jnp.zeros(token.shape, F32)

    res = pl.pallas_call(
        body, name=name,
        in_specs=[HBM_SPEC] * (2 * nw),
        out_specs=[SEM_SPEC, SEM_SPEC] + [HBM_SPEC] * (2 * nw) + [pl.BlockSpec(memory_space=pltpu.VMEM)],
        out_shape=[pltpu.SemaphoreType.DMA((ncp,)), pltpu.SemaphoreType.DMA((ncp,))]
        + [pltpu.HBM(b.shape, b.dtype) for b in list(srcs) + list(lands)] + [jax.ShapeDtypeStruct((8, LANES), F32)],
        input_output_aliases={i: 2 + i for i in range(2 * nw)},
        compiler_params=pltpu.CompilerParams(has_side_effects=SPLIT_EFFECT),
    )(*[_in_hbm(b) for b in list(srcs) + list(lands)])
    return res[0], res[1], list(res[2:2 + nw]), list(res[2 + nw:2 + 2 * nw]), res[2 + 2 * nw]


def _scatter_wait_call(srcs, lands, send, recv, after, wholes, name):
    nw = len(srcs)

    def body(*refs):
        ins, lnd, send, recv = refs[:nw], refs[nw:2 * nw], refs[2 * nw], refs[2 * nw + 1]
        for out, back in _scatter_copies(ins, lnd, send, recv, wholes):
            out.wait_send()
            back.wait_recv()

    res = pl.pallas_call(
        body, name=name,
        in_specs=[HBM_SPEC] * (2 * nw) + [SEM_SPEC, SEM_SPEC, ANY],
        out_specs=[HBM_SPEC] * (2 * nw),
        out_shape=[pltpu.HBM(b.shape, b.dtype) for b in list(srcs) + list(lands)],
        input_output_aliases={i: i for i in range(2 * nw)},
        compiler_params=pltpu.CompilerParams(has_side_effects=SPLIT_EFFECT),
    )(*srcs, *lands, send, recv, after)
    return list(res[nw:])


def _reduce_call(own, lands, idx, nch, name, dep=None):
    nw = len(own)
    deps = [] if dep is None else [dep]

    def body(idx_ref, *refs):
        refs = refs[:2 * nw] + refs[2 * nw + len(deps):]
        for w in range(nw):
            tot = refs[w][...]
            for r in range(1, N_DEV):
                tot = tot + refs[nw + w][idx_ref[1 + r]].astype(F32)
            refs[2 * nw + w][...] = tot

    in_specs, out_specs, out_shape = [], [], []
    for s in own:
        in_specs.append(pl.BlockSpec((None, None, s.shape[2] // nch, s.shape[3]),
                                     lambda i, idx_ref: (idx_ref[0], idx_ref[1], i, 0)))
    for s in own:
        in_specs.append(pl.BlockSpec((N_DEV, s.shape[2] // nch, s.shape[3]), lambda i, idx_ref: (0, i, 0)))
    for s in own:
        out_specs.append(pl.BlockSpec((None, s.shape[2] // nch, s.shape[3]), lambda i, idx_ref: (idx_ref[1], i, 0)))
        out_shape.append(jax.ShapeDtypeStruct((2,) + s.shape[2:], F32))
    return pl.pallas_call(
        body, name=name,
        grid_spec=pltpu.PrefetchScalarGridSpec(num_scalar_prefetch=1, grid=(nch,),
                                               in_specs=in_specs + [ANY] * len(deps), out_specs=out_specs),
        out_shape=out_shape,
        compiler_params=_params(("arbitrary",)),
    )(idx, *own, *lands, *deps)


def _pair_allgather_call(halves, name):
    nw = len(halves)

    def body(*refs):
        outs = refs[nw:2 * nw]
        send, recv = refs[2 * nw:]
        x, y, c = _coords()
        _sibling_handshake()
        cps = []
        for w in range(nw):
            cp = _remote(outs[w].at[c], outs[w].at[c], send.at[w], recv.at[w], (x, y, 1 - c))
            cp.start()
            cps.append(cp)
        for w in range(nw):
            theirs = outs[w].at[1 - c]
            _remote(theirs, theirs, send.at[w], recv.at[w], (x, y, 1 - c)).wait_recv()
        for cp in cps:
            cp.wait_send()

    outs = pl.pallas_call(
        body, name=name,
        in_specs=[ANY] * nw, out_specs=[ANY] * nw,
        out_shape=[jax.ShapeDtypeStruct(h.shape, h.dtype) for h in halves],
        input_output_aliases={w: w for w in range(nw)},
        scratch_shapes=[pltpu.SemaphoreType.DMA((nw,))] * 2,
        compiler_params=pltpu.CompilerParams(collective_id=PAIR_ALLGATHER_ID),
    )(*halves)
    return [o.reshape(2 * h.shape[1], h.shape[2]) for o, h in zip(outs, halves)]


def _adamw(w, g, m, v):
    m = ADAM_B1 * m + (1.0 - ADAM_B1) * g
    v = ADAM_B2 * v + (1.0 - ADAM_B2) * (g * g)
    m_hat = m / (1.0 - ADAM_B1 ** ADAM_STEP)
    v_hat = v / (1.0 - ADAM_B2 ** ADAM_STEP)
    delta = -ADAM_LR * (m_hat / (jnp.sqrt(v_hat) + ADAM_EPS) + ADAM_WD * w)
    return delta, m, v


def _adamw_call(ws, gs, ms, vs, nch, name):
    nw = len(ws)

    def body(*refs):
        for w in range(nw):
            g = refs[nw + w][...]
            delta, m, v = _adamw(refs[w][...], g, refs[2 * nw + w][...], refs[3 * nw + w][...])
            refs[4 * nw + w][...] = g
            refs[5 * nw + w][...] = delta
            refs[6 * nw + w][...] = m
            refs[7 * nw + w][...] = v

    specs = [pl.BlockSpec((a.shape[0] // nch, a.shape[1]), lambda i: (i, 0)) for a in ws]
    res = pl.pallas_call(
        body, name=name,
        grid=(nch,),
        in_specs=specs * 4, out_specs=specs * 4,
        out_shape=[jax.ShapeDtypeStruct(a.shape, F32) for a in ws] * 4,
        compiler_params=_params(("arbitrary",)),
    )(*ws, *gs, *ms, *vs)
    return res[:nw], res[nw:2 * nw], res[2 * nw:3 * nw], res[3 * nw:]


def _small_call(gathered, own, me_idx, w, m, v):
    def fold(row):
        tot = row[:, 0:LANES] + row[:, LANES:2 * LANES] + row[:, 2 * LANES:3 * LANES] + row[:, 3 * LANES:4 * LANES]
        return tot + pltpu.roll(tot, HEAD_DIM, axis=1)

    def body(me_ref, ga_ref, own_ref, w_ref, m_ref, v_ref, g_out, d_out, m_out, v_out):
        me = me_ref[0]
        term = lambda i: jnp.where(me == i, own_ref[...], ga_ref[i])
        g = term(0)
        for i in range(1, N_DEV):
            g = g + term(i)
        unfolded = g[4:5, :]
        folded = jnp.concatenate([fold(unfolded[:, :ATTN_WIDTH]), fold(unfolded[:, ATTN_WIDTH:]),
                                  jnp.zeros((1, 1024 - 2 * LANES), F32)], axis=-1)
        row = lax.broadcasted_iota(jnp.int32, g.shape, 0)
        g = jnp.where(row == 3, folded, g)
        delta, mm, vv = _adamw(w_ref[...], g, m_ref[...], v_ref[...])
        g_out[...] = g
        d_out[...] = delta
        m_out[...] = mm
        v_out[...] = vv

    vmem = pl.BlockSpec(memory_space=pltpu.VMEM)
    return pl.pallas_call(
        body, name="adamw_small",
        in_specs=[pl.BlockSpec(memory_space=pltpu.SMEM)] + [vmem] * 5,
        out_shape=[jax.ShapeDtypeStruct(w.shape, F32)] * 4,
        compiler_params=_params(),
    )(me_idx, gathered, own, w, m, v)


def _pack_small(p, folded=True, loss=None):
    z = lambda n: jnp.zeros((n,), F32)
    rows = [p["mix_norm_g"], p["mlp_norm_g"],
            jnp.concatenate([p["pool_scale"], p["rel_bias"].reshape(-1), z(1024 - POOL_WIDTH - N_BUCKETS * N_HEADS)])]
    if folded:
        rows += [jnp.concatenate([p["q_norm_g"], z(LANES - HEAD_DIM), p["k_norm_g"], z(1024 - LANES - HEAD_DIM)]), z(1024)]
    else:
        rows += [z(1024), jnp.concatenate([p["q_norm_g"], p["k_norm_g"]])]
    rows += [z(1024) if loss is None else jnp.concatenate([loss.reshape(1), z(1023)])]
    head = jnp.stack(rows + [z(1024)] * 2)
    return jnp.concatenate([head, p["pool_w"].reshape(-1, 1024)], axis=0)


def _unpack_small(a):
    return dict(
        mix_norm_g=a[0], mlp_norm_g=a[1], pool_scale=a[2, :POOL_WIDTH],
        rel_bias=a[2, POOL_WIDTH:POOL_WIDTH + N_BUCKETS * N_HEADS].reshape(N_BUCKETS, N_HEADS),
        q_norm_g=a[3, :HEAD_DIM], k_norm_g=a[3, LANES:LANES + HEAD_DIM],
        pool_w=a[8:].reshape(len(POOL_WINDOWS), LANES, LANES))


_WEIGHT_ORDER = ("mix_norm_g", "w_in", "pool_w", "pool_scale", "q_norm_g", "k_norm_g", "rel_bias", "w_out",
                 "mlp_norm_g", "w_up", "w_down")
_BIG = ("w_in", "w_out", "w_up", "w_down")


def kernel(x, mix_norm_g, w_in, pool_w, pool_scale, q_norm_g, k_norm_g, rel_bias, w_out, mlp_norm_g, w_up, w_down, loss_target, m_mix_norm_g, m_w_in, m_pool_w, m_pool_scale, m_q_norm_g, m_k_norm_g, m_rel_bias, m_w_out, m_mlp_norm_g, m_w_up, m_w_down, v_mix_norm_g, v_w_in, v_pool_w, v_pool_scale, v_q_norm_g, v_k_norm_g, v_rel_bias, v_w_out, v_mlp_norm_g, v_w_up, v_w_down):
    w = dict(mix_norm_g=mix_norm_g, w_in=w_in, pool_w=pool_w, pool_scale=pool_scale, q_norm_g=q_norm_g,
             k_norm_g=k_norm_g, rel_bias=rel_bias, w_out=w_out, mlp_norm_g=mlp_norm_g, w_up=w_up, w_down=w_down)
    m = dict(mix_norm_g=m_mix_norm_g, w_in=m_w_in, pool_w=m_pool_w, pool_scale=m_pool_scale, q_norm_g=m_q_norm_g,
             k_norm_g=m_k_norm_g, rel_bias=m_rel_bias, w_out=m_w_out, mlp_norm_g=m_mlp_norm_g, w_up=m_w_up, w_down=m_w_down)
    v = dict(mix_norm_g=v_mix_norm_g, w_in=v_w_in, pool_w=v_pool_w, pool_scale=v_pool_scale, q_norm_g=v_q_norm_g,
             k_norm_g=v_k_norm_g, rel_bias=v_rel_bias, w_out=v_w_out, mlp_norm_g=v_mlp_norm_g, w_up=v_w_up, w_down=v_w_down)
    xc, yc, cc = _coords()

    c_idx = jnp.reshape(cc, (1,)).astype(jnp.int32)
    chip_idx = jnp.reshape(2 * xc + yc, (1,)).astype(jnp.int32)
    me = 4 * xc + 2 * yc + cc
    whole = lambda t: t.reshape(t.shape[0], t.shape[1] * t.shape[2], t.shape[3])

    placed = [_halves(p) for p in _place_shards_call([w[n] for n in _BIG], chip_idx, nch=4)]
    win_f, bias = _allgather_call(placed[:1], from_chips=True, name="weights_allgather_in",
                                  meanwhile=_bias_table_work(rel_bias))
    wsend, wrecv, in_flight, started = _gather_start_call(placed[1:], win_f)

    def mlp_weights(after):
        landed = _gather_wait_call(in_flight, wsend, wrecv, after)
        wout_f, wup_f, wdown_f = _allgather_call(landed, from_chips=False, name="weights_pair_forward")
        return whole(wout_f).reshape(-1, wout_f.shape[-1]), whole(wup_f), whole(wdown_f)

    split = []

    def on_mlp_grads(*wire_grads):
        srcs = [_halves(g) for g in wire_grads]
        lands = [lax.empty((N_DEV,) + s.shape[2:], s.dtype) for s in srcs]
        split.extend(_scatter_start_call(srcs, lands, [False] * len(srcs), "grads_scatter_start"))
        return split[4]

    loss_part, dx, big_grads, small_grads = _local_grads(
        x[0], loss_target[0], mix_norm_g + started[0, 0], whole(win_f), pool_w, pool_scale, q_norm_g, k_norm_g, bias,
        mlp_norm_g, mlp_weights, on_mlp_grads)
    g_in, g_out, g_up, g_down = big_grads
    gsend, grecv, srcs_thru, lands_thru, _ = split
    lands_mlp = _scatter_wait_call(srcs_thru, lands_thru, gsend, grecv, g_in[1], [False] * 3, "grads_scatter_wait")

    small_own = _pack_small(small_grads, folded=False, loss=loss_part)
    last_srcs = [_halves(g_in[1]), small_own]
    last_lands = [lax.empty((N_DEV,) + last_srcs[0].shape[2:], WIRE_DTYPE), lax.empty((N_DEV,) + small_own.shape, F32)]
    lsend, lrecv, last_srcs, last_lands, last_started = _scatter_start_call(
        last_srcs, last_lands, [False, True], "grads_scatter_start_last")
    idx = jnp.concatenate([chip_idx, c_idx] + [jnp.reshape(jnp.bitwise_xor(me, r), (1,)) for r in range(1, N_DEV)])
    idx = idx.astype(jnp.int32)
    mlp = _BIG[1:]

    def update(names, own32, lands, tag, dep=None):
        halves = _reduce_call([_halves(g) for g in own32], lands, idx, 4, "grads_reduce_" + tag, dep)
        reduced = _pair_allgather_call(list(halves), "grads_pair_allgather_" + tag)
        return _adamw_call([w[n] for n in names], reduced, [m[n] for n in names], [v[n] for n in names], 8, "adamw_" + tag)

    out_mlp = update(mlp, [g_out[0], g_up[0], g_down[0]], lands_mlp, "mlp", last_started)
    land_in, small_all = _scatter_wait_call(last_srcs, last_lands, lsend, lrecv, out_mlp[3][-1], [False, True],
                                            "grads_scatter_wait_last")
    out_in = update(_BIG[:1], [g_in[0]], [land_in], "in")
    g_pack, d_pack, m_pack, v_pack = _small_call(
        small_all, small_own, jnp.reshape(me, (1,)).astype(jnp.int32), _pack_small(w), _pack_small(m), _pack_small(v))

    grads, deltas, new_m, new_v = (_unpack_small(a) for a in (g_pack, d_pack, m_pack, v_pack))
    for k, res in enumerate((grads, deltas, new_m, new_v)):
        res[_BIG[0]] = out_in[k][0]
        for i, n in enumerate(mlp):
            res[n] = out_mlp[k][i]
    loss = g_pack[LOSS_ROW, 0]
    return (loss, dx[None], *[grads[n] for n in _WEIGHT_ORDER], *[deltas[n] for n in _WEIGHT_ORDER],
            *[new_m[n] for n in _WEIGHT_ORDER], *[new_v[n] for n in _WEIGHT_ORDER])
```

```python
import math

import jax
import jax.numpy as jnp
import numpy as np
from jax import lax
from jax.experimental import pallas as pl
from jax.experimental.pallas import tpu as pltpu

F32 = jnp.float32
MXU_DTYPE = jnp.bfloat16
WIRE_DTYPE = jnp.bfloat16

NORM_EPS = 1e-6
NEG_INF = -1e30
LANES = 128
HEAD_DIM = 64
N_HEADS = 8
POOL_WIDTH = 512
ATTN_WIDTH = 512
POOL_WINDOWS = (2, 4, 8, 16)
POOL_HALO = 16
DILATED_PATTERNS = ((128, 1), (512, 4), (2048, 16))
ATT_BLOCK = 128
ATT_SUPER = ATT_BLOCK * max(dl for _, dl in DILATED_PATTERNS)
ATT_UNITS = ATT_SUPER // ATT_BLOCK
N_BUCKETS = 32
NO_BUCKET = -1
MAX_DISTANCE = 2048
N_CHIPS = 4
N_DEV = 8
ADAM_LR, ADAM_B1, ADAM_B2, ADAM_EPS, ADAM_WD, ADAM_STEP = 0.001, 0.9, 0.999, 1e-08, 0.01, 10
VMEM_LIMIT = 56 * 1024 * 1024
MESH = pl.DeviceIdType.MESH
ANY = pl.BlockSpec(memory_space=pl.ANY)

LOSS_ROW = 5


def _mm(a, b):
    return jnp.dot(a, b, preferred_element_type=F32)


def _mm_nt(a, b):
    return lax.dot_general(a, b, (((1,), (1,)), ((), ())), preferred_element_type=F32)


def _mm_tn(a, b):
    return lax.dot_general(a, b, (((0,), (0,)), ((), ())), preferred_element_type=F32)


def _params(sem=None, **kw):
    if sem is not None:
        kw["dimension_semantics"] = sem
    return pltpu.CompilerParams(vmem_limit_bytes=VMEM_LIMIT, **kw)


def _low_half():
    return lax.broadcasted_iota(jnp.int32, (1, LANES), 1) < HEAD_DIM


def _head_sum_bcast(y):
    lo = _low_half()
    outs = []
    for j in range(y.shape[1] // LANES):
        c = y[:, j * LANES:(j + 1) * LANES]
        s_lo = jnp.sum(jnp.where(lo, c, 0.0), axis=-1, keepdims=True)
        s_hi = jnp.sum(jnp.where(lo, 0.0, c), axis=-1, keepdims=True)
        outs.append(jnp.where(lo, s_lo, s_hi))
    return jnp.concatenate(outs, axis=-1)


def _rms_bwd(dn, hn, r):
    return r * (dn - hn * jnp.mean(dn * hn, axis=-1, keepdims=True))


def _t5_bucket_np(dist):
    max_exact = N_BUCKETS // 2
    d_f = np.maximum(dist, 1).astype(np.float32)
    ratio = (np.log(d_f / np.float32(max_exact)) / np.float32(math.log(MAX_DISTANCE / max_exact))).astype(np.float32)
    large = max_exact + (ratio * np.float32(N_BUCKETS - max_exact)).astype(np.int32)
    large = np.minimum(large, N_BUCKETS - 1)
    return np.where(dist < max_exact, dist, large).astype(np.int32)


def _window_offsets(dl):
    if dl == 1:
        return _by4_positions(ATT_BLOCK), _by4_positions(2 * ATT_BLOCK)
    return np.arange(ATT_BLOCK), np.arange(2 * ATT_BLOCK)


def _bucket_tables():
    tables = []
    for _, dl in DILATED_PATTERNS:
        qq, kk = _window_offsets(dl)
        dist = qq[:, None] + ATT_BLOCK - kk[None, :]
        bucket = _t5_bucket_np(np.clip(dist, 0, ATT_BLOCK) * dl)
        tables.append(np.where((dist >= 0) & (dist <= ATT_BLOCK), bucket, NO_BUCKET))
    return np.stack(tables).astype(np.int32)


def _previous_block_keys():
    return np.stack([np.broadcast_to(_window_offsets(dl)[1][None, :] < ATT_BLOCK, (ATT_BLOCK, 2 * ATT_BLOCK))
                     for _, dl in DILATED_PATTERNS])


def _f1_call(x, g1, win, poolw, pscale, qg, kg, tm):
    s, d = x.shape
    nblk = s // tm

    def body(x_ref, g1_ref, win_ref, pw_ref, ps_ref, qg_ref, kg_ref,
             a_ref, pooled_ref, ypool_ref, q32_ref, k32_ref, qn_ref, kn_ref, v_ref, ubuf):
        i = pl.program_id(0)
        xv = x_ref[...]
        r = lax.rsqrt(jnp.mean(xv * xv, axis=-1, keepdims=True) + NORM_EPS)
        a = ((xv * r) * g1_ref[...]).astype(MXU_DTYPE)
        a_ref[...] = a
        u = _mm(a, win_ref[0])
        q = _mm(a, win_ref[1])
        k = _mm(a, win_ref[2])
        v_ref[...] = _mm(a, win_ref[3])
        q32_ref[...] = q
        k32_ref[...] = k
        rq = lax.rsqrt(_head_sum_bcast(q * q) * (1.0 / HEAD_DIM) + NORM_EPS)
        qn_ref[...] = ((q * rq) * qg_ref[...]) * (HEAD_DIM ** -0.5)
        rk = lax.rsqrt(_head_sum_bcast(k * k) * (1.0 / HEAD_DIM) + NORM_EPS)
        kn_ref[...] = (k * rk) * kg_ref[...]

        ubuf[0:POOL_HALO, :] = jnp.where(i > 0, ubuf[tm:tm + POOL_HALO, :], 0.0)
        ubuf[POOL_HALO:POOL_HALO + tm, :] = u
        t = i * tm + lax.broadcasted_iota(jnp.int32, (tm, 1), 0)
        for g, w in enumerate(POOL_WINDOWS):
            ls = slice(g * LANES, (g + 1) * LANES)
            ug = u[:, ls]
            acc = ug
            for sh in range(1, w):
                acc = acc + ubuf[POOL_HALO - sh:POOL_HALO - sh + tm, ls]
            cnt = jnp.minimum(t + 1, w).astype(F32)
            pooled = (acc / cnt - ug).astype(MXU_DTYPE)
            pooled_ref[:, ls] = pooled
            ypool_ref[:, ls] = (_mm(pooled, pw_ref[g]) * ps_ref[:, ls]).astype(MXU_DTYPE)

    tok = lambda w: pl.BlockSpec((tm, w), lambda i: (i, 0))
    full = lambda shp: pl.BlockSpec(shp, lambda i: (0,) * len(shp))
    return pl.pallas_call(
        body, name="fwd_inproj",
        grid=(nblk,),
        in_specs=[tok(d), full((1, d)), full(win.shape), full(poolw.shape), full((1, POOL_WIDTH)),
                  full((1, ATTN_WIDTH)), full((1, ATTN_WIDTH))],
        out_specs=[tok(d), tok(POOL_WIDTH), tok(POOL_WIDTH), tok(ATTN_WIDTH), tok(ATTN_WIDTH),
                   tok(ATTN_WIDTH), tok(ATTN_WIDTH), tok(ATTN_WIDTH)],
        out_shape=[jax.ShapeDtypeStruct((s, d), MXU_DTYPE),
                   jax.ShapeDtypeStruct((s, POOL_WIDTH), MXU_DTYPE),
                   jax.ShapeDtypeStruct((s, POOL_WIDTH), MXU_DTYPE),
                   jax.ShapeDtypeStruct((s, ATTN_WIDTH), F32),
                   jax.ShapeDtypeStruct((s, ATTN_WIDTH), F32),
                   jax.ShapeDtypeStruct((s, ATTN_WIDTH), F32),
                   jax.ShapeDtypeStruct((s, ATTN_WIDTH), F32),
                   jax.ShapeDtypeStruct((s, ATTN_WIDTH), F32)],
        scratch_shapes=[pltpu.VMEM((tm + POOL_HALO, POOL_WIDTH), F32)],
        compiler_params=_params(("arbitrary",)),
    )(x, g1, win, poolw, pscale, qg, kg)


DEINT = 4
assert [dl for _, dl in DILATED_PATTERNS] == [1, DEINT, DEINT * DEINT]


def _by4_positions(n):
    pos = np.arange(n)
    return DEINT * (pos % (n // DEINT)) + pos // (n // DEINT)


def _masked_bias(b_ref, p, n):
    return b_ref[p, jnp.minimum(n, 1)].reshape(2 * ATT_BLOCK, 2 * ATT_BLOCK)


def _unit_rows(u, dl):
    assert isinstance(u, int)
    sq, sk = ATT_SUPER // DEINT, 2 * ATT_SUPER // DEINT
    if dl == 1:
        n = ATT_BLOCK // DEINT
        return (u, [pl.ds(r * sq + n * u, n) for r in range(DEINT)],
                [pl.ds(r * sk + sk // 2 + n * (u - 1), 2 * n) for r in range(DEINT)])
    if dl == DEINT:
        r, b = u % DEINT, u // DEINT
        return (b, [pl.ds(r * sq + ATT_BLOCK * b, ATT_BLOCK)],
                [pl.ds(r * sk + sk // 2 + ATT_BLOCK * (b - 1), 2 * ATT_BLOCK)])
    r, a = u % DEINT, u // DEINT
    return 0, [pl.ds(r * sq + a, ATT_BLOCK, stride=DEINT)], [pl.ds(r * sk + a, 2 * ATT_BLOCK, stride=DEINT)]


def _take(ref, runs):
    parts = [ref[run, :] for run in runs]
    return parts[0] if len(parts) == 1 else jnp.concatenate(parts, axis=0)


def _put(ref, runs, value, add=False):
    n = value.shape[0] // len(runs)
    for i, run in enumerate(runs):
        part = value[i * n:(i + 1) * n]
        ref[run, :] = ref[run, :] + part if add else part


def _deinterleave(dst, src, n):
    seg = n // DEINT
    for r in range(DEINT):
        dst[r * seg:(r + 1) * seg, :] = src[pl.ds(r, seg, stride=DEINT), :]


def _deinterleave_pair(dst, prev, cur):
    seg = prev.shape[0] // DEINT
    for r in range(DEINT):
        dst[2 * r * seg:(2 * r + 1) * seg, :] = prev[pl.ds(r, seg, stride=DEINT), :]
        dst[(2 * r + 1) * seg:(2 * r + 2) * seg, :] = cur[pl.ds(r, seg, stride=DEINT), :]


def _interleave(dst, src, n, offset=0):
    seg = n // DEINT
    stride = src.shape[0] // DEINT
    for r in range(DEINT):
        dst[pl.ds(r, seg, stride=DEINT), :] = src[r * stride + offset:r * stride + offset + seg, :]


def _attn_fwd_call(qn, kn, v, bias):
    s, w = qn.shape
    nsb = s // ATT_SUPER
    npair = w // LANES

    def body(q_ref, kc_ref, kp_ref, vc_ref, vp_ref, b_ref, o_ref, lse_ref, qf, kf, vf, acc_s, m_s, l_s):
        sb = pl.program_id(1)
        _deinterleave(qf, q_ref, ATT_SUPER)
        _deinterleave_pair(kf, kp_ref, kc_ref)
        _deinterleave_pair(vf, vp_ref, vc_ref)
        lo = _low_half()
        for p, (_, dl) in enumerate(DILATED_PATTERNS):
            def unit(u, carry, p=p, dl=dl):
                b, rows_q, rows_k = _unit_rows(u, dl)
                qp = _take(qf, rows_q).astype(MXU_DTYPE)
                kcat = _take(kf, rows_k).astype(MXU_DTYPE)
                vcat = _take(vf, rows_k).astype(MXU_DTYPE)
                zero = jnp.zeros_like(qp)
                q2 = jnp.concatenate([jnp.where(lo, qp, zero), jnp.where(lo, zero, qp)], axis=0)
                sc = _mm_nt(q2, kcat) + _masked_bias(b_ref, p, sb * (ATT_UNITS // dl) + b)
                m2 = jnp.max(sc, axis=-1, keepdims=True)
                pr = jnp.exp(sc - m2)
                l2 = jnp.sum(pr, axis=-1, keepdims=True)
                acc2 = _mm(pr.astype(MXU_DTYPE), vcat)
                acc = jnp.where(lo, acc2[:ATT_BLOCK], acc2[ATT_BLOCK:])
                m = jnp.where(lo, m2[:ATT_BLOCK], m2[ATT_BLOCK:])
                l = jnp.where(lo, l2[:ATT_BLOCK], l2[ATT_BLOCK:])
                if p == 0:
                    _put(acc_s, rows_q, acc)
                    _put(m_s, rows_q, m)
                    _put(l_s, rows_q, l)
                else:
                    m_old = _take(m_s, rows_q)
                    m_new = jnp.maximum(m_old, m)
                    a_old = jnp.exp(m_old - m_new)
                    a_new = jnp.exp(m - m_new)
                    _put(acc_s, rows_q, a_old * _take(acc_s, rows_q) + a_new * acc)
                    _put(l_s, rows_q, a_old * _take(l_s, rows_q) + a_new * l)
                    _put(m_s, rows_q, m_new)
                return carry

            for u in range(ATT_UNITS):
                unit(u, None)
        l = l_s[...]
        acc_s[...] = acc_s[...] / l
        m_s[...] = m_s[...] + jnp.log(l)
        _interleave(o_ref, acc_s, ATT_SUPER)
        _interleave(lse_ref, m_s, ATT_SUPER)

    cur = pl.BlockSpec((ATT_SUPER, LANES), lambda j, t: (t, j))
    prev = pl.BlockSpec((ATT_SUPER, LANES), lambda j, t: (jnp.maximum(t - 1, 0), j))
    bspec = pl.BlockSpec((len(DILATED_PATTERNS), 2, 2, ATT_BLOCK, 2 * ATT_BLOCK), lambda j, t: (0, 0, j, 0, 0))
    return pl.pallas_call(
        body, name="attn_fwd",
        grid=(npair, nsb),
        in_specs=[cur, cur, prev, cur, prev, bspec],
        out_specs=[cur, cur],
        out_shape=[jax.ShapeDtypeStruct((s, w), F32), jax.ShapeDtypeStruct((s, w), F32)],
        scratch_shapes=[pltpu.VMEM((ATT_SUPER, LANES), F32), pltpu.VMEM((2 * ATT_SUPER, LANES), F32),
                        pltpu.VMEM((2 * ATT_SUPER, LANES), F32), pltpu.VMEM((ATT_SUPER, LANES), F32),
                        pltpu.VMEM((ATT_SUPER, LANES), F32), pltpu.VMEM((ATT_SUPER, LANES), F32)],
        compiler_params=_params(("arbitrary", "arbitrary")),
    )(qn, kn, kn, v, v, bias)


def _attn_bwd_call(qn, kn, v, do, lse, delta, bias, dep=None):
    s, w = qn.shape
    nsb = s // ATT_SUPER
    npair = w // LANES
    deps = [] if dep is None else [dep]

    def body(q_ref, kc_ref, kp_ref, vc_ref, vp_ref, do_ref, lse_ref, dlt_ref, b_ref, *rest):
        dq_ref, dk_ref, dv_ref, db_ref, qf, kf, vf, dof, lsef, dltf, dqf, dkf, dvf = rest[len(deps):]
        step = pl.program_id(1)
        sb = nsb - 1 - step
        seg = ATT_SUPER // DEINT
        _deinterleave(qf, q_ref, ATT_SUPER)
        _deinterleave(dof, do_ref, ATT_SUPER)
        _deinterleave_pair(kf, kp_ref, kc_ref)
        _deinterleave_pair(vf, vp_ref, vc_ref)
        _deinterleave(lsef, lse_ref, ATT_SUPER)
        _deinterleave(dltf, dlt_ref, ATT_SUPER)

        db_ref[...] = jnp.where(step > 0, db_ref[...], 0.0)
        for acc in (dkf, dvf):
            for r in range(DEINT):
                this, before = pl.ds((2 * r + 1) * seg, seg), pl.ds(2 * r * seg, seg)
                acc[this, :] = jnp.where(step > 0, acc[before, :], 0.0)
                acc[before, :] = jnp.zeros((seg, LANES), F32)
        lo = _low_half()
        for p, (_, dl) in enumerate(DILATED_PATTERNS):
            def unit(u, carry, p=p, dl=dl):
                b, rows_q, rows_k = _unit_rows(u, dl)
                qp = _take(qf, rows_q).astype(MXU_DTYPE)
                dop = _take(dof, rows_q).astype(MXU_DTYPE)
                kcat = _take(kf, rows_k).astype(MXU_DTYPE)
                vcat = _take(vf, rows_k).astype(MXU_DTYPE)
                lse2 = _take(lsef, rows_q)
                dlt2 = _take(dltf, rows_q)
                zero = jnp.zeros_like(qp)
                q2 = jnp.concatenate([jnp.where(lo, qp, zero), jnp.where(lo, zero, qp)], axis=0)
                do2 = jnp.concatenate([jnp.where(lo, dop, zero), jnp.where(lo, zero, dop)], axis=0)
                lse_c = jnp.concatenate([lse2[:, 0:1], lse2[:, HEAD_DIM:HEAD_DIM + 1]], axis=0)
                dlt_c = jnp.concatenate([dlt2[:, 0:1], dlt2[:, HEAD_DIM:HEAD_DIM + 1]], axis=0)
                sc = _mm_nt(q2, kcat) + _masked_bias(b_ref, p, sb * (ATT_UNITS // dl) + b)
                pr = jnp.exp(sc - lse_c)
                ds = pr * (_mm_nt(do2, vcat) - dlt_c)
                db_ref[p] += ds.reshape(2, ATT_BLOCK, 2 * ATT_BLOCK)
                ds_c = ds.astype(MXU_DTYPE)
                dq2 = _mm(ds_c, kcat)
                dk = _mm_tn(ds_c, q2)
                dv = _mm_tn(pr.astype(MXU_DTYPE), do2)
                dq = jnp.where(lo, dq2[:ATT_BLOCK], dq2[ATT_BLOCK:])
                _put(dqf, rows_q, dq, add=p > 0)
                _put(dkf, rows_k, dk, add=True)
                _put(dvf, rows_k, dv, add=True)
                return carry

            for u in range(ATT_UNITS):
                unit(u, None)
        _interleave(dq_ref, dqf, ATT_SUPER)
        _interleave(dk_ref, dkf, ATT_SUPER, offset=seg)
        _interleave(dv_ref, dvf, ATT_SUPER, offset=seg)

    cur = pl.BlockSpec((ATT_SUPER, LANES), lambda j, t: (nsb - 1 - t, j))
    prev = pl.BlockSpec((ATT_SUPER, LANES), lambda j, t: (jnp.maximum(nsb - 2 - t, 0), j))
    npat = len(DILATED_PATTERNS)
    bspec = pl.BlockSpec((npat, 2, 2, ATT_BLOCK, 2 * ATT_BLOCK), lambda j, t: (0, 0, j, 0, 0))
    dbspec = pl.BlockSpec((npat, 2, ATT_BLOCK, 2 * ATT_BLOCK), lambda j, t: (0, j, 0, 0))
    sup = lambda: pltpu.VMEM((ATT_SUPER, LANES), F32)
    sup2 = lambda: pltpu.VMEM((2 * ATT_SUPER, LANES), F32)
    return pl.pallas_call(
        body, name="attn_bwd",
        grid=(npair, nsb),
        in_specs=[cur, cur, prev, cur, prev, cur, cur, cur, bspec] + [ANY] * len(deps),
        out_specs=[cur, cur, cur, dbspec],
        out_shape=[jax.ShapeDtypeStruct((s, w), F32)] * 3
        + [jax.ShapeDtypeStruct((npat, N_HEADS, ATT_BLOCK, 2 * ATT_BLOCK), F32)],
        scratch_shapes=[sup(), sup2(), sup2(), sup(), sup(), sup(), sup(), sup2(), sup2()],
        compiler_params=_params(("arbitrary", "arbitrary")),
    )(qn, kn, kn, v, v, do, lse, delta, bias, *deps)


def _bias_table_work(rel_bias):
    buckets = jnp.asarray(_bucket_tables())
    prev_keys = jnp.asarray(_previous_block_keys().astype(np.int32))
    npat = buckets.shape[0]

    def body(rb_ref, bk_ref, pk_ref, out_ref):
        for p in range(npat):
            for half in range(2):
                ks = slice(half * ATT_BLOCK, (half + 1) * ATT_BLOCK)
                bk = bk_ref[p, :, ks]
                absent = pk_ref[p, :, ks] != 0
                for h in range(N_HEADS):
                    def pick(b, acc, h=h, bk=bk):
                        return jnp.where(bk == b, rb_ref[b, h], acc)

                    tab = lax.fori_loop(0, N_BUCKETS, pick, jnp.full((ATT_BLOCK, ATT_BLOCK), NEG_INF, F32))
                    out_ref[p, 1, h, :, ks] = tab
                    out_ref[p, 0, h, :, ks] = jnp.where(absent, NEG_INF, tab)

    vmem = pl.BlockSpec(memory_space=pltpu.VMEM)
    return ([rel_bias, buckets, prev_keys], [pl.BlockSpec(memory_space=pltpu.SMEM), vmem, vmem],
            jax.ShapeDtypeStruct((npat, 2, N_HEADS, ATT_BLOCK, 2 * ATT_BLOCK), F32), body)


def _rel_bias_grad_call(dbias, buckets):
    npat, nh = dbias.shape[0], dbias.shape[1]

    def body(db_ref, bk_ref, out_ref):
        lane = lax.broadcasted_iota(jnp.int32, (nh, LANES), 1)
        out = jnp.zeros((nh, LANES), F32)
        for b in range(N_BUCKETS):
            tot = jnp.zeros((nh, 1), F32)
            for p in range(npat):
                hit = jnp.where(bk_ref[p][None] == b, db_ref[p], 0.0)
                tot = tot + jnp.sum(jnp.sum(hit, axis=1), axis=-1, keepdims=True)
            out = jnp.where(lane == b, tot, out)
        out_ref[...] = out

    return pl.pallas_call(
        body, name="rel_bias_grad",
        out_shape=jax.ShapeDtypeStruct((nh, LANES), F32),
        compiler_params=_params(),
    )(dbias, buckets)


def _f2_call(x, tgt, ypool, o, wout, wup, wdown, g2, tm):
    s, d = x.shape
    nblk = s // tm
    nch, _, fch = wup.shape
    dff = nch * fch
    mixw = POOL_WIDTH + ATTN_WIDTH

    def body(x_ref, t_ref, yp_ref, o_ref, g2_ref, wout_hbm, wup_hbm, wdown_hbm,
             mixed_ref, c_ref, ff_ref, dz_ref, dy_ref, dh1_ref, dyp_ref, do_ref, dlt_ref, dg2_ref, loss_ref,
             wout_v, wup_v, wdown_v, rz, wsem):
        i = pl.program_id(0)

        @pl.when(i == 0)
        def _():
            copies = [pltpu.make_async_copy(wout_hbm, wout_v, wsem.at[0])]
            for j in range(nch):
                copies.append(pltpu.make_async_copy(wup_hbm.at[j], wup_v.at[j], wsem.at[1 + 2 * j]))
                copies.append(pltpu.make_async_copy(wdown_hbm.at[j], wdown_v.at[j], wsem.at[2 + 2 * j]))
            for cp in copies:
                cp.start()
            dg2_ref[...] = jnp.zeros(dg2_ref.shape, F32)
            loss_ref[...] = jnp.zeros(loss_ref.shape, F32)
            for cp in copies:
                cp.wait()

        o = o_ref[...]
        mixed = jnp.concatenate([yp_ref[...], o.astype(MXU_DTYPE)], axis=-1)
        mixed_ref[...] = mixed
        h1 = x_ref[...] + _mm(mixed, wout_v[...])
        r2 = lax.rsqrt(jnp.mean(h1 * h1, axis=-1, keepdims=True) + NORM_EPS)
        hn = h1 * r2
        c = (hn * g2_ref[...]).astype(MXU_DTYPE)
        c_ref[...] = c
        y = h1
        for j in range(nch):
            cs = slice(j * fch, (j + 1) * fch)
            z = jnp.maximum(_mm(c, wup_v[j]), 0.0)
            rz[:, cs] = z
            ff = (z * z).astype(MXU_DTYPE)
            ff_ref[:, cs] = ff
            y = y + _mm(ff, wdown_v[j])
        err = y - t_ref[...]
        loss_ref[...] += jnp.sum(err * err) * (0.5 / d)
        dy = err * (1.0 / d)
        dy_c = dy.astype(MXU_DTYPE)
        dy_ref[...] = dy_c
        dc = jnp.zeros((tm, d), F32)
        for j in range(nch):
            cs = slice(j * fch, (j + 1) * fch)
            dz = (_mm_nt(dy_c, wdown_v[j]) * (2.0 * rz[:, cs])).astype(MXU_DTYPE)
            dz_ref[:, cs] = dz
            dc = dc + _mm_nt(dz, wup_v[j])
        dg2_ref[...] += jnp.sum(dc * hn, axis=0, keepdims=True)
        dh1 = dy + _rms_bwd(dc * g2_ref[...], hn, r2)
        dh1_ref[...] = dh1
        dmix = _mm_nt(dh1.astype(MXU_DTYPE), wout_v[...])
        dyp_ref[...] = dmix[:, :POOL_WIDTH]
        do = dmix[:, POOL_WIDTH:]
        do_ref[...] = do
        dlt_ref[...] = _head_sum_bcast(do * o)

    tok = lambda w: pl.BlockSpec((tm, w), lambda i: (i, 0))
    const = lambda shp: pl.BlockSpec(shp, lambda i: (0,) * len(shp))
    return pl.pallas_call(
        body, name="fwd_mlp_bwd_mlp",
        grid=(nblk,),
        in_specs=[tok(d), tok(d), tok(POOL_WIDTH), tok(ATTN_WIDTH), const((1, d)), ANY, ANY, ANY],
        out_specs=[tok(mixw), tok(d), tok(dff), tok(dff), tok(d), tok(d), tok(POOL_WIDTH), tok(ATTN_WIDTH),
                   tok(ATTN_WIDTH), const((1, d)), const((1, LANES))],
        out_shape=[jax.ShapeDtypeStruct((s, mixw), MXU_DTYPE),
                   jax.ShapeDtypeStruct((s, d), MXU_DTYPE),
                   jax.ShapeDtypeStruct((s, dff), MXU_DTYPE),
                   jax.ShapeDtypeStruct((s, dff), MXU_DTYPE),
                   jax.ShapeDtypeStruct((s, d), MXU_DTYPE),
                   jax.ShapeDtypeStruct((s, d), F32),
                   jax.ShapeDtypeStruct((s, POOL_WIDTH), F32),
                   jax.ShapeDtypeStruct((s, ATTN_WIDTH), F32),
                   jax.ShapeDtypeStruct((s, ATTN_WIDTH), F32),
                   jax.ShapeDtypeStruct((1, d), F32),
                   jax.ShapeDtypeStruct((1, LANES), F32)],
        scratch_shapes=[pltpu.VMEM(wout.shape, MXU_DTYPE), pltpu.VMEM(wup.shape, MXU_DTYPE),
                        pltpu.VMEM(wdown.shape, MXU_DTYPE), pltpu.VMEM((tm, dff), F32),
                        pltpu.SemaphoreType.DMA((1 + 2 * nch,))],
        compiler_params=_params(("arbitrary",)),
    )(x, tgt, ypool, o, g2, wout, wup, wdown)


def _bproj_call(dqn, dkn, dv, q32, k32, dypool, pooled, x, dh1, win, poolw, pscale, qg, kg, g1, tm):
    s, d = x.shape
    nblk = s // tm
    ngrp = len(POOL_WINDOWS)

    def body(dqn_ref, dkn_ref, dv_ref, q_ref, k_ref, dyp_ref, pooled_ref, x_ref, dh1_ref,
             win_hbm, pw_ref, ps_ref, qg_ref, kg_ref, g1_ref,
             dx_ref, dproj_ref, dg1_ref, dqg_ref, dkg_ref, dpw_ref, dps_ref, win_v, ebuf):
        step = pl.program_id(0)
        i = nblk - 1 - step

        @pl.when(step == 0)
        def _():
            pltpu.sync_copy(win_hbm, win_v)
            dg1_ref[...] = jnp.zeros(dg1_ref.shape, F32)
            dqg_ref[...] = jnp.zeros(dqg_ref.shape, F32)
            dkg_ref[...] = jnp.zeros(dkg_ref.shape, F32)
            dpw_ref[...] = jnp.zeros(dpw_ref.shape, F32)
            dps_ref[...] = jnp.zeros(dps_ref.shape, F32)
            ebuf[tm:tm + POOL_HALO, :] = jnp.zeros((POOL_HALO, POOL_WIDTH), F32)

        @pl.when(step > 0)
        def _():
            ebuf[tm:tm + POOL_HALO, :] = ebuf[0:POOL_HALO, :]

        def qk_bwd(dn_sum, raw, gain, scale, dgain_ref):
            rr = lax.rsqrt(_head_sum_bcast(raw * raw) * (1.0 / HEAD_DIM) + NORM_EPS)
            hn = raw * rr
            dgain_ref[...] += jnp.sum(dn_sum * hn, axis=0, keepdims=True) * scale
            dn = dn_sum * (gain * scale)
            return rr * (dn - hn * (_head_sum_bcast(dn * hn) * (1.0 / HEAD_DIM)))

        dq = qk_bwd(dqn_ref[...], q_ref[...], qg_ref[...], HEAD_DIM ** -0.5, dqg_ref)
        dk = qk_bwd(dkn_ref[...], k_ref[...], kg_ref[...], 1.0, dkg_ref)

        t = i * tm + lax.broadcasted_iota(jnp.int32, (tm, 1), 0)
        dpooled = []
        for g, w in enumerate(POOL_WINDOWS):
            ls = slice(g * LANES, (g + 1) * LANES)
            dm = dyp_ref[:, ls]
            pg = pooled_ref[:, ls]
            dps_ref[:, ls] += jnp.sum(dm * _mm(pg, pw_ref[g]), axis=0, keepdims=True)
            dms = (dm * ps_ref[:, ls]).astype(MXU_DTYPE)
            dpw_ref[g] += _mm_tn(pg, dms)
            dpg = _mm_nt(dms, pw_ref[g])
            dpooled.append(dpg)
            ebuf[0:tm, ls] = dpg / jnp.minimum(t + 1, w).astype(F32)
        du = []
        for g, w in enumerate(POOL_WINDOWS):
            ls = slice(g * LANES, (g + 1) * LANES)
            acc = ebuf[0:tm, ls]
            for sh in range(1, w):
                acc = acc + ebuf[sh:sh + tm, ls]
            du.append(acc - dpooled[g])
        parts = [jnp.concatenate(du, axis=-1), dq, dk, dv_ref[...]]
        da = jnp.zeros((tm, d), F32)
        for p, part in enumerate(parts):
            pc = part.astype(MXU_DTYPE)
            dproj_ref[:, p * POOL_WIDTH:(p + 1) * POOL_WIDTH] = pc
            da = da + _mm_nt(pc, win_v[p])
        xv = x_ref[...]
        r = lax.rsqrt(jnp.mean(xv * xv, axis=-1, keepdims=True) + NORM_EPS)
        xn = xv * r
        dg1_ref[...] += jnp.sum(da * xn, axis=0, keepdims=True)
        dx_ref[...] = dh1_ref[...] + _rms_bwd(da * g1_ref[...], xn, r)

    tok = lambda w: pl.BlockSpec((tm, w), lambda t: (nblk - 1 - t, 0))
    const = lambda shp: pl.BlockSpec(shp, lambda t: (0,) * len(shp))
    return pl.pallas_call(
        body, name="bwd_inproj",
        grid=(nblk,),
        in_specs=[tok(ATTN_WIDTH)] * 5 + [tok(POOL_WIDTH), tok(POOL_WIDTH), tok(d), tok(d),
                                          ANY, const(poolw.shape), const((1, POOL_WIDTH)), const((1, ATTN_WIDTH)),
                                          const((1, ATTN_WIDTH)), const((1, d))],
        out_specs=[tok(d), tok(4 * POOL_WIDTH), const((1, d)), const((1, ATTN_WIDTH)), const((1, ATTN_WIDTH)),
                   const((ngrp, LANES, LANES)), const((1, POOL_WIDTH))],
        out_shape=[jax.ShapeDtypeStruct((s, d), F32),
                   jax.ShapeDtypeStruct((s, 4 * POOL_WIDTH), MXU_DTYPE),
                   jax.ShapeDtypeStruct((1, d), F32),
                   jax.ShapeDtypeStruct((1, ATTN_WIDTH), F32),
                   jax.ShapeDtypeStruct((1, ATTN_WIDTH), F32),
                   jax.ShapeDtypeStruct((ngrp, LANES, LANES), F32),
                   jax.ShapeDtypeStruct((1, POOL_WIDTH), F32)],
        scratch_shapes=[pltpu.VMEM(win.shape, MXU_DTYPE), pltpu.VMEM((tm + POOL_HALO, POOL_WIDTH), F32)],
        compiler_params=_params(("arbitrary",)),
    )(dqn, dkn, dv, q32, k32, dypool, pooled, x, dh1, win, poolw, pscale, qg, kg, g1)


def _wgrad_call(a, b, bm, bn, bk, out_shape, out_block, out_index, name):
    s, m = a.shape
    _, n = b.shape
    nk = s // bk

    def body(a_ref, b_ref, o_ref, wire_ref):
        k = pl.program_id(2)
        acc = jnp.where(k > 0, o_ref[...], 0.0) + _mm_tn(a_ref[...].astype(MXU_DTYPE), b_ref[...].astype(MXU_DTYPE))
        o_ref[...] = acc
        wire_ref[...] = acc.astype(WIRE_DTYPE)

    return pl.pallas_call(
        body, name=name,
        grid=(m // bm, n // bn, nk),
        in_specs=[pl.BlockSpec((bk, bm), lambda i, j, k: (k, i)), pl.BlockSpec((bk, bn), lambda i, j, k: (k, j))],
        out_specs=[pl.BlockSpec(out_block, out_index)] * 2,
        out_shape=[jax.ShapeDtypeStruct(out_shape, F32), jax.ShapeDtypeStruct(out_shape, WIRE_DTYPE)],
        compiler_params=_params(("arbitrary", "arbitrary", "arbitrary")),
    )(a, b)


def _local_grads(x, tgt, g1, win, poolw, pscale, qg, kg, bias, g2, mlp_weights, on_mlp_grads=None):
    s, d = x.shape
    g1r, g2r = g1.reshape(1, d), g2.reshape(1, d)
    psr = pscale.reshape(1, POOL_WIDTH)
    qgr = jnp.tile(qg, N_HEADS).reshape(1, ATTN_WIDTH)
    kgr = jnp.tile(kg, N_HEADS).reshape(1, ATTN_WIDTH)
    pw_c = poolw.astype(MXU_DTYPE)
    buckets = jnp.asarray(_bucket_tables())
    bk = min(s, 4096)

    a, pooled, ypool, q32, k32, qn, kn, v = _f1_call(x, g1r, win, pw_c, psr, qgr, kgr, tm=1024)
    o, lse = _attn_fwd_call(qn, kn, v, bias)
    wout, wup, wdown = mlp_weights(o)
    mixed, c, ff, dz, dy, dh1, dypool, do, delta, dg2, loss = _f2_call(x, tgt, ypool, o, wout, wup, wdown, g2r, tm=256)
    dff = ff.shape[1]
    g_out = [g.reshape(N_CHIPS, d // N_CHIPS, d)
             for g in _wgrad_call(mixed, dh1, d, d, bk // 4, (d, d), (d, d), lambda i, j, k: (0, 0), "wgrad_out")]
    g_up = _wgrad_call(c, dz, d, dff // N_CHIPS, bk, (N_CHIPS, d, dff // N_CHIPS), (None, d, dff // N_CHIPS),
                       lambda i, j, k: (j, 0, 0), "wgrad_up")
    g_down = _wgrad_call(ff, dy, dff // N_CHIPS, d, bk, (N_CHIPS, dff // N_CHIPS, d), (None, dff // N_CHIPS, d),
                         lambda i, j, k: (i, 0, 0), "wgrad_down")
    dep = None if on_mlp_grads is None else on_mlp_grads(g_out[1], g_up[1], g_down[1])
    dqn, dkn, dv, dbias = _attn_bwd_call(qn, kn, v, do, lse, delta, bias, dep)
    dx, dproj, dg1, dqg, dkg, dpw, dps = _bproj_call(
        dqn, dkn, dv, q32, k32, dypool, pooled, x, dh1, win, pw_c, psr, qgr, kgr, g1r, tm=512)
    nin = dproj.shape[1] // N_CHIPS
    g_in = _wgrad_call(a, dproj, d, nin, bk, (N_CHIPS, d, nin), (None, d, nin), lambda i, j, k: (j, 0, 0), "wgrad_in")
    drb = _rel_bias_grad_call(dbias, buckets)
    small = dict(
        mix_norm_g=dg1.reshape(d), mlp_norm_g=dg2.reshape(d), pool_scale=dps.reshape(POOL_WIDTH),
        q_norm_g=dqg.reshape(ATTN_WIDTH), k_norm_g=dkg.reshape(ATTN_WIDTH),
        rel_bias=drb[:, :N_BUCKETS].T, pool_w=dpw)
    return loss[0, 0], dx, (g_in, g_out, g_up, g_down), small


def _coords():
    return lax.axis_index("x"), lax.axis_index("y"), lax.axis_index("c")


def _other_chips(x, y):
    return [(1 - x, y), (x, 1 - y), (1 - x, 1 - y)]


def _remote(src, dst, send_sem, recv_sem, dev):
    return pltpu.make_async_remote_copy(src_ref=src, dst_ref=dst, send_sem=send_sem, recv_sem=recv_sem,
                                        device_id=dev, device_id_type=MESH)


PAIR_FORWARD_ID = 1
PAIR_ALLGATHER_ID = 2


def _sibling_handshake():
    x, y, c = _coords()
    barrier = pltpu.get_barrier_semaphore()
    pl.semaphore_signal(barrier, inc=1, device_id=(x, y, 1 - c), device_id_type=MESH)
    pl.semaphore_wait(barrier, 1)


def _halves(a):
    return a.reshape(a.shape[:-2] + (2, a.shape[-2] // 2, a.shape[-1]))


def _place_shards_call(shards, chip_idx, nch):
    nw = len(shards)

    def body(chip_ref, *refs):
        for w in range(nw):
            refs[nw + w][...] = refs[w][...].astype(WIRE_DTYPE)

    in_specs = [pl.BlockSpec((s.shape[0] // nch, s.shape[1]), lambda i, chip_ref: (i, 0)) for s in shards]
    out_specs = [pl.BlockSpec((None, s.shape[0] // nch, s.shape[1]), lambda i, chip_ref: (chip_ref[0], i, 0))
                 for s in shards]
    return pl.pallas_call(
        body, name="weights_place",
        grid_spec=pltpu.PrefetchScalarGridSpec(num_scalar_prefetch=1, grid=(nch,),
                                               in_specs=in_specs, out_specs=out_specs),
        out_shape=[jax.ShapeDtypeStruct((N_CHIPS,) + s.shape, WIRE_DTYPE) for s in shards],
        compiler_params=_params(("arbitrary",)),
    )(chip_idx, *shards)


def _allgather_call(placed, from_chips, name, meanwhile=None):
    nw = len(placed)
    ncp = 3 * nw
    extra, extra_specs, extra_shape, extra_body = meanwhile if meanwhile else ([], [], None, None)
    ne = len(extra)

    def body(*refs):
        outs = refs[nw + ne:2 * nw + ne]
        send1, recv1, send2, recv2 = refs[-4:]
        x, y, c = _coords()
        chip = 2 * x + y
        others = _other_chips(x, y)
        first, passed = [], []
        if not from_chips:
            _sibling_handshake()
        if from_chips:
            for w in range(nw):
                for k, (ox, oy) in enumerate(others):
                    mine = outs[w].at[chip, c]
                    cp = _remote(mine, mine, send1.at[3 * w + k], recv1.at[3 * w + k], (ox, oy, c))
                    cp.start()
                    first.append(cp)
        if meanwhile:
            extra_body(*refs[nw:nw + ne], refs[2 * nw + ne])
        for w in range(nw):
            for k, (ox, oy) in enumerate(others):
                piece = outs[w].at[2 * ox + oy, c]
                if from_chips:
                    _remote(piece, piece, send1.at[3 * w + k], recv1.at[3 * w + k], (ox, oy, c)).wait_recv()
                cp = _remote(piece, piece, send2.at[3 * w + k], recv2.at[3 * w + k], (x, y, 1 - c))
                cp.start()
                passed.append(cp)
        for w in range(nw):
            for k, (ox, oy) in enumerate(others):
                piece = outs[w].at[2 * ox + oy, 1 - c]
                _remote(piece, piece, send2.at[3 * w + k], recv2.at[3 * w + k], (x, y, 1 - c)).wait_recv()
        for cp in first + passed:
            cp.wait_send()

    return pl.pallas_call(
        body, name=name,
        in_specs=[ANY] * nw + list(extra_specs),
        out_specs=[ANY] * nw + ([pl.BlockSpec(memory_space=pltpu.VMEM)] if meanwhile else []),
        out_shape=[jax.ShapeDtypeStruct(s.shape, s.dtype) for s in placed] + ([extra_shape] if meanwhile else []),
        input_output_aliases={w: w for w in range(nw)},
        scratch_shapes=[pltpu.SemaphoreType.DMA((ncp,))] * 4,
        compiler_params=_params() if from_chips else _params(collective_id=PAIR_FORWARD_ID),
    )(*placed, *extra)


HBM_SPEC = pl.BlockSpec(memory_space=pltpu.HBM)
SEM_SPEC = pl.BlockSpec(memory_space=pltpu.SEMAPHORE)
SPLIT_EFFECT = pltpu.SideEffectType.DATAFLOW_SIDE_EFFECTING


def _in_hbm(a):
    return pltpu.with_memory_space_constraint(a, pltpu.HBM)


def _gather_copies(bufs, send, recv):
    x, y, c = _coords()
    chip = 2 * x + y
    cps = []
    for w, buf in enumerate(bufs):
        for k, (ox, oy) in enumerate(_other_chips(x, y)):
            mine, theirs = buf.at[chip, c], buf.at[2 * ox + oy, c]
            sems = (send.at[3 * w + k], recv.at[3 * w + k], (ox, oy, c))
            cps.append((_remote(mine, mine, *sems), _remote(theirs, theirs, *sems)))
    return cps


def _gather_start_call(bufs, after):
    nw = len(bufs)

    def body(*refs):
        ins, send, recv, token = refs[:nw], refs[nw + 1], refs[nw + 2], refs[2 * nw + 3]
        for out, _ in _gather_copies(ins, send, recv):
            out.start()
        token[...] = jnp.zeros(token.shape, F32)

    res = pl.pallas_call(
        body, name="weights_gather_start",
        in_specs=[HBM_SPEC] * nw + [ANY],
        out_specs=[SEM_SPEC, SEM_SPEC] + [HBM_SPEC] * nw + [pl.BlockSpec(memory_space=pltpu.VMEM)],
        out_shape=[pltpu.SemaphoreType.DMA((3 * nw,)), pltpu.SemaphoreType.DMA((3 * nw,))]
        + [pltpu.HBM(b.shape, b.dtype) for b in bufs] + [jax.ShapeDtypeStruct((8, LANES), F32)],
        input_output_aliases={w: 2 + w for w in range(nw)},
        compiler_params=pltpu.CompilerParams(has_side_effects=SPLIT_EFFECT),
    )(*[_in_hbm(b) for b in bufs], after)
    return res[0], res[1], list(res[2:2 + nw]), res[2 + nw]


def _gather_wait_call(bufs, send, recv, after):
    nw = len(bufs)

    def body(*refs):
        ins, send, recv = refs[:nw], refs[nw], refs[nw + 1]
        for out, back in _gather_copies(ins, send, recv):
            out.wait_send()
            back.wait_recv()

    return pl.pallas_call(
        body, name="weights_gather_wait",
        in_specs=[HBM_SPEC] * nw + [SEM_SPEC, SEM_SPEC, ANY],
        out_specs=[HBM_SPEC] * nw,
        out_shape=[pltpu.HBM(b.shape, b.dtype) for b in bufs],
        input_output_aliases={w: w for w in range(nw)},
        compiler_params=pltpu.CompilerParams(has_side_effects=SPLIT_EFFECT),
    )(*bufs, send, recv, after)


def _scatter_copies(srcs, lands, send, recv, wholes):
    x, y, c = _coords()
    me = 4 * x + 2 * y + c
    cps = []
    for w, (src, land) in enumerate(zip(srcs, lands)):
        for r in range(1, N_DEV):
            px, py, pc = ((1 - x) if r & 4 else x, (1 - y) if r & 2 else y, (1 - c) if r & 1 else c)
            sems = (send.at[(N_DEV - 1) * w + r - 1], recv.at[(N_DEV - 1) * w + r - 1], (px, py, pc))
            piece = src if wholes[w] else src.at[2 * px + py, pc]
            cps.append((_remote(piece, land.at[me], *sems), _remote(piece, land.at[4 * px + 2 * py + pc], *sems)))
    return cps


def _scatter_start_call(srcs, lands, wholes, name):
    nw = len(srcs)
    ncp = (N_DEV - 1) * nw

    def body(*refs):
        ins, lnd, send, recv, token = refs[:nw], refs[nw:2 * nw], refs[2 * nw], refs[2 * nw + 1], refs[4 * nw + 2]
        for out, _ in _scatter_copies(ins, lnd, send, recv, wholes):
            out.start()
        token[...] = jnp.zeros(token.shape, F32)

    res = pl.pallas_call(
        body, name=name,
        in_specs=[HBM_SPEC] * (2 * nw),
        out_specs=[SEM_SPEC, SEM_SPEC] + [HBM_SPEC] * (2 * nw) + [pl.BlockSpec(memory_space=pltpu.VMEM)],
        out_shape=[pltpu.SemaphoreType.DMA((ncp,)), pltpu.SemaphoreType.DMA((ncp,))]
        + [pltpu.HBM(b.shape, b.dtype) for b in list(srcs) + list(lands)] + [jax.ShapeDtypeStruct((8, LANES), F32)],
        input_output_aliases={i: 2 + i for i in range(2 * nw)},
        compiler_params=pltpu.CompilerParams(has_side_effects=SPLIT_EFFECT),
    )(*[_in_hbm(b) for b in list(srcs) + list(lands)])
    return res[0], res[1], list(res[2:2 + nw]), list(res[2 + nw:2 + 2 * nw]), res[2 + 2 * nw]


def _scatter_wait_call(srcs, lands, send, recv, after, wholes, name):
    nw = len(srcs)

    def body(*refs):
        ins, lnd, send, recv = refs[:nw], refs[nw:2 * nw], refs[2 * nw], refs[2 * nw + 1]
        for out, back in _scatter_copies(ins, lnd, send, recv, wholes):
            out.wait_send()
            back.wait_recv()

    res = pl.pallas_call(
        body, name=name,
        in_specs=[HBM_SPEC] * (2 * nw) + [SEM_SPEC, SEM_SPEC, ANY],
        out_specs=[HBM_SPEC] * (2 * nw),
        out_shape=[pltpu.HBM(b.shape, b.dtype) for b in list(srcs) + list(lands)],
        input_output_aliases={i: i for i in range(2 * nw)},
        compiler_params=pltpu.CompilerParams(has_side_effects=SPLIT_EFFECT),
    )(*srcs, *lands, send, recv, after)
    return list(res[nw:])


def _reduce_call(own, lands, idx, nch, name, dep=None):
    nw = len(own)
    deps = [] if dep is None else [dep]

    def body(idx_ref, *refs):
        refs = refs[:2 * nw] + refs[2 * nw + len(deps):]
        for w in range(nw):
            tot = refs[w][...]
            for r in range(1, N_DEV):
                tot = tot + refs[nw + w][idx_ref[1 + r]].astype(F32)
            refs[2 * nw + w][...] = tot

    in_specs, out_specs, out_shape = [], [], []
    for s in own:
        in_specs.append(pl.BlockSpec((None, None, s.shape[2] // nch, s.shape[3]),
                                     lambda i, idx_ref: (idx_ref[0], idx_ref[1], i, 0)))
    for s in own:
        in_specs.append(pl.BlockSpec((N_DEV, s.shape[2] // nch, s.shape[3]), lambda i, idx_ref: (0, i, 0)))
    for s in own:
        out_specs.append(pl.BlockSpec((None, s.shape[2] // nch, s.shape[3]), lambda i, idx_ref: (idx_ref[1], i, 0)))
        out_shape.append(jax.ShapeDtypeStruct((2,) + s.shape[2:], F32))
    return pl.pallas_call(
        body, name=name,
        grid_spec=pltpu.PrefetchScalarGridSpec(num_scalar_prefetch=1, grid=(nch,),
                                               in_specs=in_specs + [ANY] * len(deps), out_specs=out_specs),
        out_shape=out_shape,
        compiler_params=_params(("arbitrary",)),
    )(idx, *own, *lands, *deps)


def _pair_allgather_call(halves, name):
    nw = len(halves)

    def body(*refs):
        outs = refs[nw:2 * nw]
        send, recv = refs[2 * nw:]
        x, y, c = _coords()
        _sibling_handshake()
        cps = []
        for w in range(nw):
            cp = _remote(outs[w].at[c], outs[w].at[c], send.at[w], recv.at[w], (x, y, 1 - c))
            cp.start()
            cps.append(cp)
        for w in range(nw):
            theirs = outs[w].at[1 - c]
            _remote(theirs, theirs, send.at[w], recv.at[w], (x, y, 1 - c)).wait_recv()
        for cp in cps:
            cp.wait_send()

    outs = pl.pallas_call(
        body, name=name,
        in_specs=[ANY] * nw, out_specs=[ANY] * nw,
        out_shape=[jax.ShapeDtypeStruct(h.shape, h.dtype) for h in halves],
        input_output_aliases={w: w for w in range(nw)},
        scratch_shapes=[pltpu.SemaphoreType.DMA((nw,))] * 2,
        compiler_params=pltpu.CompilerParams(collective_id=PAIR_ALLGATHER_ID),
    )(*halves)
    return [o.reshape(2 * h.shape[1], h.shape[2]) for o, h in zip(outs, halves)]


def _adamw(w, g, m, v):
    m = ADAM_B1 * m + (1.0 - ADAM_B1) * g
    v = ADAM_B2 * v + (1.0 - ADAM_B2) * (g * g)
    m_hat = m / (1.0 - ADAM_B1 ** ADAM_STEP)
    v_hat = v / (1.0 - ADAM_B2 ** ADAM_STEP)
    delta = -ADAM_LR * (m_hat / (jnp.sqrt(v_hat) + ADAM_EPS) + ADAM_WD * w)
    return delta, m, v


def _adamw_call(ws, gs, ms, vs, nch, name):
    nw = len(ws)

    def body(*refs):
        for w in range(nw):
            g = refs[nw + w][...]
            delta, m, v = _adamw(refs[w][...], g, refs[2 * nw + w][...], refs[3 * nw + w][...])
            refs[4 * nw + w][...] = g
            refs[5 * nw + w][...] = delta
            refs[6 * nw + w][...] = m
            refs[7 * nw + w][...] = v

    specs = [pl.BlockSpec((a.shape[0] // nch, a.shape[1]), lambda i: (i, 0)) for a in ws]
    res = pl.pallas_call(
        body, name=name,
        grid=(nch,),
        in_specs=specs * 4, out_specs=specs * 4,
        out_shape=[jax.ShapeDtypeStruct(a.shape, F32) for a in ws] * 4,
        compiler_params=_params(("arbitrary",)),
    )(*ws, *gs, *ms, *vs)
    return res[:nw], res[nw:2 * nw], res[2 * nw:3 * nw], res[3 * nw:]


def _small_call(gathered, own, me_idx, w, m, v):
    def fold(row):
        tot = row[:, 0:LANES] + row[:, LANES:2 * LANES] + row[:, 2 * LANES:3 * LANES] + row[:, 3 * LANES:4 * LANES]
        return tot + pltpu.roll(tot, HEAD_DIM, axis=1)

    def body(me_ref, gh_ref, gp_ref, oh_ref, op_ref, wh, wp, mh, mp, vh, vp, *outs):
        me = me_ref[0]

        def total(ga_ref, own_ref):
            term = lambda i: jnp.where(me == i, own_ref[...], ga_ref[i]).astype(F32)
            tot = term(0)
            for i in range(1, N_DEV):
                tot = tot + term(i)
            return tot

        g_head, g_pool = total(gh_ref, oh_ref), total(gp_ref, op_ref)
        unfolded = g_head[4:5, :]
        folded = jnp.concatenate([fold(unfolded[:, :ATTN_WIDTH]), fold(unfolded[:, ATTN_WIDTH:]),
                                  jnp.zeros((1, 1024 - 2 * LANES), F32)], axis=-1)
        row = lax.broadcasted_iota(jnp.int32, g_head.shape, 0)
        g_head = jnp.where(row == 3, folded, g_head)
        for k, (g, w_ref, m_ref, v_ref) in enumerate(((g_head, wh, mh, vh), (g_pool, wp, mp, vp))):
            delta, mm, vv = _adamw(w_ref[...], g, m_ref[...], v_ref[...])
            for out, val in zip(outs[k::2], (g, delta, mm, vv)):
                out[...] = val

    vmem = pl.BlockSpec(memory_space=pltpu.VMEM)
    res = pl.pallas_call(
        body, name="adamw_small",
        in_specs=[pl.BlockSpec(memory_space=pltpu.SMEM)] + [vmem] * 10,
        out_shape=[jax.ShapeDtypeStruct(a.shape, F32) for a in w] * 4,
        compiler_params=_params(),
    )(me_idx, *gathered, *own, *w, *m, *v)
    return [(res[2 * k], res[2 * k + 1]) for k in range(4)]


def _pack_small(p, folded=True, loss=None):
    z = lambda n: jnp.zeros((n,), F32)
    rows = [p["mix_norm_g"], p["mlp_norm_g"],
            jnp.concatenate([p["pool_scale"], p["rel_bias"].reshape(-1), z(1024 - POOL_WIDTH - N_BUCKETS * N_HEADS)])]
    if folded:
        rows += [jnp.concatenate([p["q_norm_g"], z(LANES - HEAD_DIM), p["k_norm_g"], z(1024 - LANES - HEAD_DIM)]), z(1024)]
    else:
        rows += [z(1024), jnp.concatenate([p["q_norm_g"], p["k_norm_g"]])]
    rows += [z(1024) if loss is None else jnp.concatenate([loss.reshape(1), z(1023)])]
    return jnp.stack(rows + [z(1024)] * 2), p["pool_w"].reshape(-1, 1024)


def _unpack_small(head, pool):
    return dict(
        mix_norm_g=head[0], mlp_norm_g=head[1], pool_scale=head[2, :POOL_WIDTH],
        rel_bias=head[2, POOL_WIDTH:POOL_WIDTH + N_BUCKETS * N_HEADS].reshape(N_BUCKETS, N_HEADS),
        q_norm_g=head[3, :HEAD_DIM], k_norm_g=head[3, LANES:LANES + HEAD_DIM],
        pool_w=pool.reshape(len(POOL_WINDOWS), LANES, LANES))


_WEIGHT_ORDER = ("mix_norm_g", "w_in", "pool_w", "pool_scale", "q_norm_g", "k_norm_g", "rel_bias", "w_out",
                 "mlp_norm_g", "w_up", "w_down")
_BIG = ("w_in", "w_out", "w_up", "w_down")


def kernel(x, mix_norm_g, w_in, pool_w, pool_scale, q_norm_g, k_norm_g, rel_bias, w_out, mlp_norm_g, w_up, w_down, loss_target, m_mix_norm_g, m_w_in, m_pool_w, m_pool_scale, m_q_norm_g, m_k_norm_g, m_rel_bias, m_w_out, m_mlp_norm_g, m_w_up, m_w_down, v_mix_norm_g, v_w_in, v_pool_w, v_pool_scale, v_q_norm_g, v_k_norm_g, v_rel_bias, v_w_out, v_mlp_norm_g, v_w_up, v_w_down):
    w = dict(mix_norm_g=mix_norm_g, w_in=w_in, pool_w=pool_w, pool_scale=pool_scale, q_norm_g=q_norm_g,
             k_norm_g=k_norm_g, rel_bias=rel_bias, w_out=w_out, mlp_norm_g=mlp_norm_g, w_up=w_up, w_down=w_down)
    m = dict(mix_norm_g=m_mix_norm_g, w_in=m_w_in, pool_w=m_pool_w, pool_scale=m_pool_scale, q_norm_g=m_q_norm_g,
             k_norm_g=m_k_norm_g, rel_bias=m_rel_bias, w_out=m_w_out, mlp_norm_g=m_mlp_norm_g, w_up=m_w_up, w_down=m_w_down)
    v = dict(mix_norm_g=v_mix_norm_g, w_in=v_w_in, pool_w=v_pool_w, pool_scale=v_pool_scale, q_norm_g=v_q_norm_g,
             k_norm_g=v_k_norm_g, rel_bias=v_rel_bias, w_out=v_w_out, mlp_norm_g=v_mlp_norm_g, w_up=v_w_up, w_down=v_w_down)
    xc, yc, cc = _coords()

    c_idx = jnp.reshape(cc, (1,)).astype(jnp.int32)
    chip_idx = jnp.reshape(2 * xc + yc, (1,)).astype(jnp.int32)
    me = 4 * xc + 2 * yc + cc
    whole = lambda t: t.reshape(t.shape[0], t.shape[1] * t.shape[2], t.shape[3])

    placed = [_halves(p) for p in _place_shards_call([w[n] for n in _BIG], chip_idx, nch=4)]
    win_f, bias = _allgather_call(placed[:1], from_chips=True, name="weights_allgather_in",
                                  meanwhile=_bias_table_work(rel_bias))
    wsend, wrecv, in_flight, started = _gather_start_call(placed[1:], win_f)

    def mlp_weights(after):
        landed = _gather_wait_call(in_flight, wsend, wrecv, after)
        wout_f, wup_f, wdown_f = _allgather_call(landed, from_chips=False, name="weights_pair_forward")
        return whole(wout_f).reshape(-1, wout_f.shape[-1]), whole(wup_f), whole(wdown_f)

    split = []

    def on_mlp_grads(*wire_grads):
        srcs = [_halves(g) for g in wire_grads]
        lands = [lax.empty((N_DEV,) + s.shape[2:], s.dtype) for s in srcs]
        split.extend(_scatter_start_call(srcs, lands, [False] * len(srcs), "grads_scatter_start"))
        return split[4]

    loss_part, dx, big_grads, small_grads = _local_grads(
        x[0], loss_target[0], mix_norm_g + started[0, 0], whole(win_f), pool_w, pool_scale, q_norm_g, k_norm_g, bias,
        mlp_norm_g, mlp_weights, on_mlp_grads)
    g_in, g_out, g_up, g_down = big_grads
    gsend, grecv, srcs_thru, lands_thru, _ = split
    lands_mlp = _scatter_wait_call(srcs_thru, lands_thru, gsend, grecv, g_in[1], [False] * 3, "grads_scatter_wait")

    head_own, pool_own = _pack_small(small_grads, folded=False, loss=loss_part)
    small_own = (head_own, pool_own.astype(WIRE_DTYPE))
    last_srcs = [_halves(g_in[1]), *small_own]
    last_lands = [lax.empty((N_DEV,) + last_srcs[0].shape[2:], WIRE_DTYPE)]
    last_lands += [lax.empty((N_DEV,) + a.shape, a.dtype) for a in small_own]
    lsend, lrecv, last_srcs, last_lands, last_started = _scatter_start_call(
        last_srcs, last_lands, [False, True, True], "grads_scatter_start_last")
    idx = jnp.concatenate([chip_idx, c_idx] + [jnp.reshape(jnp.bitwise_xor(me, r), (1,)) for r in range(1, N_DEV)])
    idx = idx.astype(jnp.int32)
    mlp = _BIG[1:]

    def update(names, own32, lands, tag, dep=None):
        halves = _reduce_call([_halves(g) for g in own32], lands, idx, 4, "grads_reduce_" + tag, dep)
        reduced = _pair_allgather_call(list(halves), "grads_pair_allgather_" + tag)
        return _adamw_call([w[n] for n in names], reduced, [m[n] for n in names], [v[n] for n in names], 8, "adamw_" + tag)

    out_mlp = update(mlp, [g_out[0], g_up[0], g_down[0]], lands_mlp, "mlp", last_started)
    land_in, *small_all = _scatter_wait_call(last_srcs, last_lands, lsend, lrecv, out_mlp[3][-1], [False, True, True],
                                             "grads_scatter_wait_last")
    out_in = update(_BIG[:1], [g_in[0]], [land_in], "in")
    g_pack, d_pack, m_pack, v_pack = _small_call(
        small_all, small_own, jnp.reshape(me, (1,)).astype(jnp.int32), _pack_small(w), _pack_small(m), _pack_small(v))

    grads, deltas, new_m, new_v = (_unpack_small(*a) for a in (g_pack, d_pack, m_pack, v_pack))
    for k, res in enumerate((grads, deltas, new_m, new_v)):
        res[_BIG[0]] = out_in[k][0]
        for i, n in enumerate(mlp):
            res[n] = out_mlp[k][i]
    loss = g_pack[0][LOSS_ROW, 0]
    return (loss, dx[None], *[grads[n] for n in _WEIGHT_ORDER], *[deltas[n] for n in _WEIGHT_ORDER],
            *[new_m[n] for n in _WEIGHT_ORDER], *[new_v[n] for n in _WEIGHT_ORDER])
```

```python
import math

import jax
import jax.numpy as jnp
import numpy as np
from jax import lax
from jax.experimental import pallas as pl
from jax.experimental.pallas import tpu as pltpu

F32 = jnp.float32
MXU_DTYPE = jnp.bfloat16
WIRE_DTYPE = jnp.bfloat16

NORM_EPS = 1e-6
NEG_INF = -1e30
LANES = 128
HEAD_DIM = 64
N_HEADS = 8
POOL_WIDTH = 512
ATTN_WIDTH = 512
POOL_WINDOWS = (2, 4, 8, 16)
POOL_HALO = 16
DILATED_PATTERNS = ((128, 1), (512, 4), (2048, 16))
ATT_BLOCK = 128
ATT_SUPER = ATT_BLOCK * max(dl for _, dl in DILATED_PATTERNS)
ATT_UNITS = ATT_SUPER // ATT_BLOCK
N_BUCKETS = 32
NO_BUCKET = -1
MAX_DISTANCE = 2048
N_CHIPS = 4
N_DEV = 8
ADAM_LR, ADAM_B1, ADAM_B2, ADAM_EPS, ADAM_WD, ADAM_STEP = 0.001, 0.9, 0.999, 1e-08, 0.01, 10
VMEM_LIMIT = 56 * 1024 * 1024
MESH = pl.DeviceIdType.MESH
ANY = pl.BlockSpec(memory_space=pl.ANY)

LOSS_ROW = 5


def _mm(a, b):
    return jnp.dot(a, b, preferred_element_type=F32)


def _mm_nt(a, b):
    return lax.dot_general(a, b, (((1,), (1,)), ((), ())), preferred_element_type=F32)


def _mm_tn(a, b):
    return lax.dot_general(a, b, (((0,), (0,)), ((), ())), preferred_element_type=F32)


def _params(sem=None, **kw):
    if sem is not None:
        kw["dimension_semantics"] = sem
    return pltpu.CompilerParams(vmem_limit_bytes=VMEM_LIMIT, **kw)


def _low_half():
    return lax.broadcasted_iota(jnp.int32, (1, LANES), 1) < HEAD_DIM


def _head_sum_bcast(y):
    lo = _low_half()
    outs = []
    for j in range(y.shape[1] // LANES):
        c = y[:, j * LANES:(j + 1) * LANES]
        s_lo = jnp.sum(jnp.where(lo, c, 0.0), axis=-1, keepdims=True)
        s_hi = jnp.sum(jnp.where(lo, 0.0, c), axis=-1, keepdims=True)
        outs.append(jnp.where(lo, s_lo, s_hi))
    return jnp.concatenate(outs, axis=-1)


def _rms_bwd(dn, hn, r):
    return r * (dn - hn * jnp.mean(dn * hn, axis=-1, keepdims=True))


def _t5_bucket_np(dist):
    max_exact = N_BUCKETS // 2
    d_f = np.maximum(dist, 1).astype(np.float32)
    ratio = (np.log(d_f / np.float32(max_exact)) / np.float32(math.log(MAX_DISTANCE / max_exact))).astype(np.float32)
    large = max_exact + (ratio * np.float32(N_BUCKETS - max_exact)).astype(np.int32)
    large = np.minimum(large, N_BUCKETS - 1)
    return np.where(dist < max_exact, dist, large).astype(np.int32)


def _window_offsets(dl):
    if dl == 1:
        return _by4_positions(ATT_BLOCK), _by4_positions(2 * ATT_BLOCK)
    return np.arange(ATT_BLOCK), np.arange(2 * ATT_BLOCK)


def _bucket_tables():
    tables = []
    for _, dl in DILATED_PATTERNS:
        qq, kk = _window_offsets(dl)
        dist = qq[:, None] + ATT_BLOCK - kk[None, :]
        bucket = _t5_bucket_np(np.clip(dist, 0, ATT_BLOCK) * dl)
        tables.append(np.where((dist >= 0) & (dist <= ATT_BLOCK), bucket, NO_BUCKET))
    return np.stack(tables).astype(np.int32)


def _previous_block_keys():
    return np.stack([np.broadcast_to(_window_offsets(dl)[1][None, :] < ATT_BLOCK, (ATT_BLOCK, 2 * ATT_BLOCK))
                     for _, dl in DILATED_PATTERNS])


def _f1_call(x, g1, win, poolw, pscale, qg, kg, tm):
    s, d = x.shape
    nblk = s // tm

    def body(x_ref, g1_ref, win_ref, pw_ref, ps_ref, qg_ref, kg_ref,
             a_ref, pooled_ref, ypool_ref, q32_ref, k32_ref, qn_ref, kn_ref, v_ref, ubuf):
        i = pl.program_id(0)
        xv = x_ref[...]
        r = lax.rsqrt(jnp.mean(xv * xv, axis=-1, keepdims=True) + NORM_EPS)
        a = ((xv * r) * g1_ref[...]).astype(MXU_DTYPE)
        a_ref[...] = a
        u = _mm(a, win_ref[0])
        q = _mm(a, win_ref[1])
        k = _mm(a, win_ref[2])
        v_ref[...] = _mm(a, win_ref[3])
        q32_ref[...] = q
        k32_ref[...] = k
        rq = lax.rsqrt(_head_sum_bcast(q * q) * (1.0 / HEAD_DIM) + NORM_EPS)
        qn_ref[...] = ((q * rq) * qg_ref[...]) * (HEAD_DIM ** -0.5)
        rk = lax.rsqrt(_head_sum_bcast(k * k) * (1.0 / HEAD_DIM) + NORM_EPS)
        kn_ref[...] = (k * rk) * kg_ref[...]

        ubuf[0:POOL_HALO, :] = jnp.where(i > 0, ubuf[tm:tm + POOL_HALO, :], 0.0)
        ubuf[POOL_HALO:POOL_HALO + tm, :] = u
        t = i * tm + lax.broadcasted_iota(jnp.int32, (tm, 1), 0)
        for g, w in enumerate(POOL_WINDOWS):
            ls = slice(g * LANES, (g + 1) * LANES)
            ug = u[:, ls]
            acc = ug
            for sh in range(1, w):
                acc = acc + ubuf[POOL_HALO - sh:POOL_HALO - sh + tm, ls]
            cnt = jnp.minimum(t + 1, w).astype(F32)
            pooled = (acc / cnt - ug).astype(MXU_DTYPE)
            pooled_ref[:, ls] = pooled
            ypool_ref[:, ls] = (_mm(pooled, pw_ref[g]) * ps_ref[:, ls]).astype(MXU_DTYPE)

    tok = lambda w: pl.BlockSpec((tm, w), lambda i: (i, 0))
    full = lambda shp: pl.BlockSpec(shp, lambda i: (0,) * len(shp))
    return pl.pallas_call(
        body, name="fwd_inproj",
        grid=(nblk,),
        in_specs=[tok(d), full((1, d)), full(win.shape), full(poolw.shape), full((1, POOL_WIDTH)),
                  full((1, ATTN_WIDTH)), full((1, ATTN_WIDTH))],
        out_specs=[tok(d), tok(POOL_WIDTH), tok(POOL_WIDTH), tok(ATTN_WIDTH), tok(ATTN_WIDTH),
                   tok(ATTN_WIDTH), tok(ATTN_WIDTH), tok(ATTN_WIDTH)],
        out_shape=[jax.ShapeDtypeStruct((s, d), MXU_DTYPE),
                   jax.ShapeDtypeStruct((s, POOL_WIDTH), MXU_DTYPE),
                   jax.ShapeDtypeStruct((s, POOL_WIDTH), MXU_DTYPE),
                   jax.ShapeDtypeStruct((s, ATTN_WIDTH), F32),
                   jax.ShapeDtypeStruct((s, ATTN_WIDTH), F32),
                   jax.ShapeDtypeStruct((s, ATTN_WIDTH), F32),
                   jax.ShapeDtypeStruct((s, ATTN_WIDTH), F32),
                   jax.ShapeDtypeStruct((s, ATTN_WIDTH), F32)],
        scratch_shapes=[pltpu.VMEM((tm + POOL_HALO, POOL_WIDTH), F32)],
        compiler_params=_params(("arbitrary",)),
    )(x, g1, win, poolw, pscale, qg, kg)


DEINT = 4
assert [dl for _, dl in DILATED_PATTERNS] == [1, DEINT, DEINT * DEINT]


def _by4_positions(n):
    pos = np.arange(n)
    return DEINT * (pos % (n // DEINT)) + pos // (n // DEINT)


def _masked_bias(b_ref, p, n):
    return b_ref[p, jnp.minimum(n, 1)].reshape(2 * ATT_BLOCK, 2 * ATT_BLOCK)


def _unit_rows(u, dl):
    assert isinstance(u, int)
    sq, sk = ATT_SUPER // DEINT, 2 * ATT_SUPER // DEINT
    if dl == 1:
        n = ATT_BLOCK // DEINT
        return (u, [pl.ds(r * sq + n * u, n) for r in range(DEINT)],
                [pl.ds(r * sk + sk // 2 + n * (u - 1), 2 * n) for r in range(DEINT)])
    if dl == DEINT:
        r, b = u % DEINT, u // DEINT
        return (b, [pl.ds(r * sq + ATT_BLOCK * b, ATT_BLOCK)],
                [pl.ds(r * sk + sk // 2 + ATT_BLOCK * (b - 1), 2 * ATT_BLOCK)])
    r, a = u % DEINT, u // DEINT
    return 0, [pl.ds(r * sq + a, ATT_BLOCK, stride=DEINT)], [pl.ds(r * sk + a, 2 * ATT_BLOCK, stride=DEINT)]


def _take(ref, runs):
    parts = [ref[run, :] for run in runs]
    return parts[0] if len(parts) == 1 else jnp.concatenate(parts, axis=0)


def _put(ref, runs, value, add=False):
    n = value.shape[0] // len(runs)
    for i, run in enumerate(runs):
        part = value[i * n:(i + 1) * n]
        ref[run, :] = ref[run, :] + part if add else part


def _deinterleave(dst, src, n):
    seg = n // DEINT
    for r in range(DEINT):
        dst[r * seg:(r + 1) * seg, :] = src[pl.ds(r, seg, stride=DEINT), :]


def _deinterleave_pair(dst, prev, cur):
    seg = prev.shape[0] // DEINT
    for r in range(DEINT):
        dst[2 * r * seg:(2 * r + 1) * seg, :] = prev[pl.ds(r, seg, stride=DEINT), :]
        dst[(2 * r + 1) * seg:(2 * r + 2) * seg, :] = cur[pl.ds(r, seg, stride=DEINT), :]


def _interleave(dst, src, n, offset=0):
    seg = n // DEINT
    stride = src.shape[0] // DEINT
    for r in range(DEINT):
        dst[pl.ds(r, seg, stride=DEINT), :] = src[r * stride + offset:r * stride + offset + seg, :]


def _attn_fwd_call(qn, kn, v, bias):
    s, w = qn.shape
    nsb = s // ATT_SUPER
    npair = w // LANES

    def body(q_ref, kc_ref, kp_ref, vc_ref, vp_ref, b_ref, o_ref, lse_ref, qf, kf, vf, acc_s, m_s, l_s):
        sb = pl.program_id(1)
        _deinterleave(qf, q_ref, ATT_SUPER)
        _deinterleave_pair(kf, kp_ref, kc_ref)
        _deinterleave_pair(vf, vp_ref, vc_ref)
        lo = _low_half()
        for p, (_, dl) in enumerate(DILATED_PATTERNS):
            def unit(u, carry, p=p, dl=dl):
                b, rows_q, rows_k = _unit_rows(u, dl)
                qp = _take(qf, rows_q).astype(MXU_DTYPE)
                kcat = _take(kf, rows_k).astype(MXU_DTYPE)
                vcat = _take(vf, rows_k).astype(MXU_DTYPE)
                zero = jnp.zeros_like(qp)
                q2 = jnp.concatenate([jnp.where(lo, qp, zero), jnp.where(lo, zero, qp)], axis=0)
                sc = _mm_nt(q2, kcat) + _masked_bias(b_ref, p, sb * (ATT_UNITS // dl) + b)
                m2 = jnp.max(sc, axis=-1, keepdims=True)
                pr = jnp.exp(sc - m2)
                l2 = jnp.sum(pr, axis=-1, keepdims=True)
                acc2 = _mm(pr.astype(MXU_DTYPE), vcat)
                acc = jnp.where(lo, acc2[:ATT_BLOCK], acc2[ATT_BLOCK:])
                m = jnp.where(lo, m2[:ATT_BLOCK], m2[ATT_BLOCK:])
                l = jnp.where(lo, l2[:ATT_BLOCK], l2[ATT_BLOCK:])
                if p == 0:
                    _put(acc_s, rows_q, acc)
                    _put(m_s, rows_q, m)
                    _put(l_s, rows_q, l)
                else:
                    m_old = _take(m_s, rows_q)
                    m_new = jnp.maximum(m_old, m)
                    a_old = jnp.exp(m_old - m_new)
                    a_new = jnp.exp(m - m_new)
                    _put(acc_s, rows_q, a_old * _take(acc_s, rows_q) + a_new * acc)
                    _put(l_s, rows_q, a_old * _take(l_s, rows_q) + a_new * l)
                    _put(m_s, rows_q, m_new)
                return carry

            for u in range(ATT_UNITS):
                unit(u, None)
        l = l_s[...]
        acc_s[...] = acc_s[...] / l
        m_s[...] = m_s[...] + jnp.log(l)
        _interleave(o_ref, acc_s, ATT_SUPER)
        _interleave(lse_ref, m_s, ATT_SUPER)

    cur = pl.BlockSpec((ATT_SUPER, LANES), lambda j, t: (t, j))
    prev = pl.BlockSpec((ATT_SUPER, LANES), lambda j, t: (jnp.maximum(t - 1, 0), j))
    bspec = pl.BlockSpec((len(DILATED_PATTERNS), 2, 2, ATT_BLOCK, 2 * ATT_BLOCK), lambda j, t: (0, 0, j, 0, 0))
    return pl.pallas_call(
        body, name="attn_fwd",
        grid=(npair, nsb),
        in_specs=[cur, cur, prev, cur, prev, bspec],
        out_specs=[cur, cur],
        out_shape=[jax.ShapeDtypeStruct((s, w), F32), jax.ShapeDtypeStruct((s, w), F32)],
        scratch_shapes=[pltpu.VMEM((ATT_SUPER, LANES), F32), pltpu.VMEM((2 * ATT_SUPER, LANES), F32),
                        pltpu.VMEM((2 * ATT_SUPER, LANES), F32), pltpu.VMEM((ATT_SUPER, LANES), F32),
                        pltpu.VMEM((ATT_SUPER, LANES), F32), pltpu.VMEM((ATT_SUPER, LANES), F32)],
        compiler_params=_params(("arbitrary", "arbitrary")),
    )(qn, kn, kn, v, v, bias)


def _attn_bwd_call(qn, kn, v, do, lse, delta, bias, dep=None):
    s, w = qn.shape
    nsb = s // ATT_SUPER
    npair = w // LANES
    deps = [] if dep is None else [dep]

    def body(q_ref, kc_ref, kp_ref, vc_ref, vp_ref, do_ref, lse_ref, dlt_ref, b_ref, *rest):
        dq_ref, dk_ref, dv_ref, db_ref, qf, kf, vf, dof, lsef, dltf, dqf, dkf, dvf = rest[len(deps):]
        step = pl.program_id(1)
        sb = nsb - 1 - step
        seg = ATT_SUPER // DEINT
        _deinterleave(qf, q_ref, ATT_SUPER)
        _deinterleave(dof, do_ref, ATT_SUPER)
        _deinterleave_pair(kf, kp_ref, kc_ref)
        _deinterleave_pair(vf, vp_ref, vc_ref)
        _deinterleave(lsef, lse_ref, ATT_SUPER)
        _deinterleave(dltf, dlt_ref, ATT_SUPER)

        db_ref[...] = jnp.where(step > 0, db_ref[...], 0.0)
        for acc in (dkf, dvf):
            for r in range(DEINT):
                this, before = pl.ds((2 * r + 1) * seg, seg), pl.ds(2 * r * seg, seg)
                acc[this, :] = jnp.where(step > 0, acc[before, :], 0.0)
                acc[before, :] = jnp.zeros((seg, LANES), F32)
        lo = _low_half()
        for p, (_, dl) in enumerate(DILATED_PATTERNS):
            def unit(u, carry, p=p, dl=dl):
                b, rows_q, rows_k = _unit_rows(u, dl)
                qp = _take(qf, rows_q).astype(MXU_DTYPE)
                dop = _take(dof, rows_q).astype(MXU_DTYPE)
                kcat = _take(kf, rows_k).astype(MXU_DTYPE)
                vcat = _take(vf, rows_k).astype(MXU_DTYPE)
                lse2 = _take(lsef, rows_q)
                dlt2 = _take(dltf, rows_q)
                zero = jnp.zeros_like(qp)
                q2 = jnp.concatenate([jnp.where(lo, qp, zero), jnp.where(lo, zero, qp)], axis=0)
                do2 = jnp.concatenate([jnp.where(lo, dop, zero), jnp.where(lo, zero, dop)], axis=0)
                lse_c = jnp.concatenate([lse2[:, 0:1], lse2[:, HEAD_DIM:HEAD_DIM + 1]], axis=0)
                dlt_c = jnp.concatenate([dlt2[:, 0:1], dlt2[:, HEAD_DIM:HEAD_DIM + 1]], axis=0)
                sc = _mm_nt(q2, kcat) + _masked_bias(b_ref, p, sb * (ATT_UNITS // dl) + b)
                pr = jnp.exp(sc - lse_c)
                ds = pr * (_mm_nt(do2, vcat) - dlt_c)
                db_ref[p] += ds.reshape(2, ATT_BLOCK, 2 * ATT_BLOCK)
                ds_c = ds.astype(MXU_DTYPE)
                dq2 = _mm(ds_c, kcat)
                dk = _mm_tn(ds_c, q2)
                dv = _mm_tn(pr.astype(MXU_DTYPE), do2)
                dq = jnp.where(lo, dq2[:ATT_BLOCK], dq2[ATT_BLOCK:])
                _put(dqf, rows_q, dq, add=p > 0)
                _put(dkf, rows_k, dk, add=True)
                _put(dvf, rows_k, dv, add=True)
                return carry

            for u in range(ATT_UNITS):
                unit(u, None)
        _interleave(dq_ref, dqf, ATT_SUPER)
        _interleave(dk_ref, dkf, ATT_SUPER, offset=seg)
        _interleave(dv_ref, dvf, ATT_SUPER, offset=seg)

    cur = pl.BlockSpec((ATT_SUPER, LANES), lambda j, t: (nsb - 1 - t, j))
    prev = pl.BlockSpec((ATT_SUPER, LANES), lambda j, t: (jnp.maximum(nsb - 2 - t, 0), j))
    npat = len(DILATED_PATTERNS)
    bspec = pl.BlockSpec((npat, 2, 2, ATT_BLOCK, 2 * ATT_BLOCK), lambda j, t: (0, 0, j, 0, 0))
    dbspec = pl.BlockSpec((npat, 2, ATT_BLOCK, 2 * ATT_BLOCK), lambda j, t: (0, j, 0, 0))
    sup = lambda: pltpu.VMEM((ATT_SUPER, LANES), F32)
    sup2 = lambda: pltpu.VMEM((2 * ATT_SUPER, LANES), F32)
    return pl.pallas_call(
        body, name="attn_bwd",
        grid=(npair, nsb),
        in_specs=[cur, cur, prev, cur, prev, cur, cur, cur, bspec] + [ANY] * len(deps),
        out_specs=[cur, cur, cur, dbspec],
        out_shape=[jax.ShapeDtypeStruct((s, w), F32)] * 3
        + [jax.ShapeDtypeStruct((npat, N_HEADS, ATT_BLOCK, 2 * ATT_BLOCK), F32)],
        scratch_shapes=[sup(), sup2(), sup2(), sup(), sup(), sup(), sup(), sup2(), sup2()],
        compiler_params=_params(("arbitrary", "arbitrary")),
    )(qn, kn, kn, v, v, do, lse, delta, bias, *deps)


def _bias_table_work(rel_bias):
    buckets = jnp.asarray(_bucket_tables())
    prev_keys = jnp.asarray(_previous_block_keys().astype(np.int32))
    npat = buckets.shape[0]

    def body(rb_ref, bk_ref, pk_ref, out_ref):
        for p in range(npat):
            for half in range(2):
                ks = slice(half * ATT_BLOCK, (half + 1) * ATT_BLOCK)
                bk = bk_ref[p, :, ks]
                absent = pk_ref[p, :, ks] != 0
                for h in range(N_HEADS):
                    def pick(b, acc, h=h, bk=bk):
                        return jnp.where(bk == b, rb_ref[b, h], acc)

                    tab = lax.fori_loop(0, N_BUCKETS, pick, jnp.full((ATT_BLOCK, ATT_BLOCK), NEG_INF, F32))
                    out_ref[p, 1, h, :, ks] = tab
                    out_ref[p, 0, h, :, ks] = jnp.where(absent, NEG_INF, tab)

    vmem = pl.BlockSpec(memory_space=pltpu.VMEM)
    return ([rel_bias, buckets, prev_keys], [pl.BlockSpec(memory_space=pltpu.SMEM), vmem, vmem],
            jax.ShapeDtypeStruct((npat, 2, N_HEADS, ATT_BLOCK, 2 * ATT_BLOCK), F32), body)


def _startup_work(rel_bias, shards):
    b_operands, b_specs, b_shape, b_body = _bias_table_work(rel_bias)
    vmem = pl.BlockSpec(memory_space=pltpu.VMEM)
    halves = [(2, s.shape[0] // 2, s.shape[1]) for s in shards]

    def body(ins, outs, staging):
        b_body(*ins[:len(b_operands)], outs[0])
        x, y, _ = _coords()
        for w_ref, buf, stage in zip(ins[len(b_operands):], outs[1:], staging):
            rh = stage.shape[1]
            stage[0] = w_ref[0:rh, :].astype(WIRE_DTYPE)
            stage[1] = w_ref[rh:, :].astype(WIRE_DTYPE)
            pltpu.sync_copy(stage, buf.at[2 * x + y])

    return (b_operands + list(shards), b_specs + [vmem] * len(shards),
            [b_shape] + [jax.ShapeDtypeStruct((N_CHIPS,) + h, WIRE_DTYPE) for h in halves],
            [vmem] + [ANY] * len(shards), [pltpu.VMEM(h, WIRE_DTYPE) for h in halves], body)


def _rel_bias_grad_call(dbias, buckets):
    npat, nh = dbias.shape[0], dbias.shape[1]

    def body(db_ref, bk_ref, out_ref):
        lane = lax.broadcasted_iota(jnp.int32, (nh, LANES), 1)
        out = jnp.zeros((nh, LANES), F32)
        for b in range(N_BUCKETS):
            tot = jnp.zeros((nh, 1), F32)
            for p in range(npat):
                hit = jnp.where(bk_ref[p][None] == b, db_ref[p], 0.0)
                tot = tot + jnp.sum(jnp.sum(hit, axis=1), axis=-1, keepdims=True)
            out = jnp.where(lane == b, tot, out)
        out_ref[...] = out

    return pl.pallas_call(
        body, name="rel_bias_grad",
        out_shape=jax.ShapeDtypeStruct((nh, LANES), F32),
        compiler_params=_params(),
    )(dbias, buckets)


def _f2_call(x, tgt, ypool, o, wout, wup, wdown, g2, tm):
    s, d = x.shape
    nblk = s // tm
    nch, _, fch = wup.shape
    dff = nch * fch
    mixw = POOL_WIDTH + ATTN_WIDTH

    def body(x_ref, t_ref, yp_ref, o_ref, g2_ref, wout_hbm, wup_hbm, wdown_hbm,
             mixed_ref, c_ref, ff_ref, dz_ref, dy_ref, dh1_ref, dyp_ref, do_ref, dlt_ref, dg2_ref, loss_ref,
             wout_v, wup_v, wdown_v, rz, wsem):
        i = pl.program_id(0)

        @pl.when(i == 0)
        def _():
            copies = [pltpu.make_async_copy(wout_hbm, wout_v, wsem.at[0])]
            for j in range(nch):
                copies.append(pltpu.make_async_copy(wup_hbm.at[j], wup_v.at[j], wsem.at[1 + 2 * j]))
                copies.append(pltpu.make_async_copy(wdown_hbm.at[j], wdown_v.at[j], wsem.at[2 + 2 * j]))
            for cp in copies:
                cp.start()
            dg2_ref[...] = jnp.zeros(dg2_ref.shape, F32)
            loss_ref[...] = jnp.zeros(loss_ref.shape, F32)
            for cp in copies:
                cp.wait()

        o = o_ref[...]
        mixed = jnp.concatenate([yp_ref[...], o.astype(MXU_DTYPE)], axis=-1)
        mixed_ref[...] = mixed
        h1 = x_ref[...] + _mm(mixed, wout_v[...])
        r2 = lax.rsqrt(jnp.mean(h1 * h1, axis=-1, keepdims=True) + NORM_EPS)
        hn = h1 * r2
        c = (hn * g2_ref[...]).astype(MXU_DTYPE)
        c_ref[...] = c
        y = h1
        for j in range(nch):
            cs = slice(j * fch, (j + 1) * fch)
            z = jnp.maximum(_mm(c, wup_v[j]), 0.0)
            rz[:, cs] = z
            ff = (z * z).astype(MXU_DTYPE)
            ff_ref[:, cs] = ff
            y = y + _mm(ff, wdown_v[j])
        err = y - t_ref[...]
        loss_ref[...] += jnp.sum(err * err) * (0.5 / d)
        dy = err * (1.0 / d)
        dy_c = dy.astype(MXU_DTYPE)
        dy_ref[...] = dy_c
        dc = jnp.zeros((tm, d), F32)
        for j in range(nch):
            cs = slice(j * fch, (j + 1) * fch)
            dz = (_mm_nt(dy_c, wdown_v[j]) * (2.0 * rz[:, cs])).astype(MXU_DTYPE)
            dz_ref[:, cs] = dz
            dc = dc + _mm_nt(dz, wup_v[j])
        dg2_ref[...] += jnp.sum(dc * hn, axis=0, keepdims=True)
        dh1 = dy + _rms_bwd(dc * g2_ref[...], hn, r2)
        dh1_ref[...] = dh1
        dmix = _mm_nt(dh1.astype(MXU_DTYPE), wout_v[...])
        dyp_ref[...] = dmix[:, :POOL_WIDTH]
        do = dmix[:, POOL_WIDTH:]
        do_ref[...] = do
        dlt_ref[...] = _head_sum_bcast(do * o)

    tok = lambda w: pl.BlockSpec((tm, w), lambda i: (i, 0))
    const = lambda shp: pl.BlockSpec(shp, lambda i: (0,) * len(shp))
    return pl.pallas_call(
        body, name="fwd_mlp_bwd_mlp",
        grid=(nblk,),
        in_specs=[tok(d), tok(d), tok(POOL_WIDTH), tok(ATTN_WIDTH), const((1, d)), ANY, ANY, ANY],
        out_specs=[tok(mixw), tok(d), tok(dff), tok(dff), tok(d), tok(d), tok(POOL_WIDTH), tok(ATTN_WIDTH),
                   tok(ATTN_WIDTH), const((1, d)), const((1, LANES))],
        out_shape=[jax.ShapeDtypeStruct((s, mixw), MXU_DTYPE),
                   jax.ShapeDtypeStruct((s, d), MXU_DTYPE),
                   jax.ShapeDtypeStruct((s, dff), MXU_DTYPE),
                   jax.ShapeDtypeStruct((s, dff), MXU_DTYPE),
                   jax.ShapeDtypeStruct((s, d), MXU_DTYPE),
                   jax.ShapeDtypeStruct((s, d), F32),
                   jax.ShapeDtypeStruct((s, POOL_WIDTH), F32),
                   jax.ShapeDtypeStruct((s, ATTN_WIDTH), F32),
                   jax.ShapeDtypeStruct((s, ATTN_WIDTH), F32),
                   jax.ShapeDtypeStruct((1, d), F32),
                   jax.ShapeDtypeStruct((1, LANES), F32)],
        scratch_shapes=[pltpu.VMEM(wout.shape, MXU_DTYPE), pltpu.VMEM(wup.shape, MXU_DTYPE),
                        pltpu.VMEM(wdown.shape, MXU_DTYPE), pltpu.VMEM((tm, dff), F32),
                        pltpu.SemaphoreType.DMA((1 + 2 * nch,))],
        compiler_params=_params(("arbitrary",)),
    )(x, tgt, ypool, o, g2, wout, wup, wdown)


def _bproj_call(dqn, dkn, dv, q32, k32, dypool, pooled, x, dh1, win, poolw, pscale, qg, kg, g1, tm):
    s, d = x.shape
    nblk = s // tm
    ngrp = len(POOL_WINDOWS)

    def body(dqn_ref, dkn_ref, dv_ref, q_ref, k_ref, dyp_ref, pooled_ref, x_ref, dh1_ref,
             win_hbm, pw_ref, ps_ref, qg_ref, kg_ref, g1_ref,
             dx_ref, dproj_ref, dg1_ref, dqg_ref, dkg_ref, dpw_ref, dps_ref, win_v, ebuf):
        step = pl.program_id(0)
        i = nblk - 1 - step

        @pl.when(step == 0)
        def _():
            pltpu.sync_copy(win_hbm, win_v)
            dg1_ref[...] = jnp.zeros(dg1_ref.shape, F32)
            dqg_ref[...] = jnp.zeros(dqg_ref.shape, F32)
            dkg_ref[...] = jnp.zeros(dkg_ref.shape, F32)
            dpw_ref[...] = jnp.zeros(dpw_ref.shape, F32)
            dps_ref[...] = jnp.zeros(dps_ref.shape, F32)
            ebuf[tm:tm + POOL_HALO, :] = jnp.zeros((POOL_HALO, POOL_WIDTH), F32)

        @pl.when(step > 0)
        def _():
            ebuf[tm:tm + POOL_HALO, :] = ebuf[0:POOL_HALO, :]

        def qk_bwd(dn_sum, raw, gain, scale, dgain_ref):
            rr = lax.rsqrt(_head_sum_bcast(raw * raw) * (1.0 / HEAD_DIM) + NORM_EPS)
            hn = raw * rr
            dgain_ref[...] += jnp.sum(dn_sum * hn, axis=0, keepdims=True) * scale
            dn = dn_sum * (gain * scale)
            return rr * (dn - hn * (_head_sum_bcast(dn * hn) * (1.0 / HEAD_DIM)))

        dq = qk_bwd(dqn_ref[...], q_ref[...], qg_ref[...], HEAD_DIM ** -0.5, dqg_ref)
        dk = qk_bwd(dkn_ref[...], k_ref[...], kg_ref[...], 1.0, dkg_ref)

        t = i * tm + lax.broadcasted_iota(jnp.int32, (tm, 1), 0)
        dpooled = []
        for g, w in enumerate(POOL_WINDOWS):
            ls = slice(g * LANES, (g + 1) * LANES)
            dm = dyp_ref[:, ls]
            pg = pooled_ref[:, ls]
            dps_ref[:, ls] += jnp.sum(dm * _mm(pg, pw_ref[g]), axis=0, keepdims=True)
            dms = (dm * ps_ref[:, ls]).astype(MXU_DTYPE)
            dpw_ref[g] += _mm_tn(pg, dms)
            dpg = _mm_nt(dms, pw_ref[g])
            dpooled.append(dpg)
            ebuf[0:tm, ls] = dpg / jnp.minimum(t + 1, w).astype(F32)
        du = []
        for g, w in enumerate(POOL_WINDOWS):
            ls = slice(g * LANES, (g + 1) * LANES)
            acc = ebuf[0:tm, ls]
            for sh in range(1, w):
                acc = acc + ebuf[sh:sh + tm, ls]
            du.append(acc - dpooled[g])
        parts = [jnp.concatenate(du, axis=-1), dq, dk, dv_ref[...]]
        da = jnp.zeros((tm, d), F32)
        for p, part in enumerate(parts):
            pc = part.astype(MXU_DTYPE)
            dproj_ref[:, p * POOL_WIDTH:(p + 1) * POOL_WIDTH] = pc
            da = da + _mm_nt(pc, win_v[p])
        xv = x_ref[...]
        r = lax.rsqrt(jnp.mean(xv * xv, axis=-1, keepdims=True) + NORM_EPS)
        xn = xv * r
        dg1_ref[...] += jnp.sum(da * xn, axis=0, keepdims=True)
        dx_ref[...] = dh1_ref[...] + _rms_bwd(da * g1_ref[...], xn, r)

    tok = lambda w: pl.BlockSpec((tm, w), lambda t: (nblk - 1 - t, 0))
    const = lambda shp: pl.BlockSpec(shp, lambda t: (0,) * len(shp))
    return pl.pallas_call(
        body, name="bwd_inproj",
        grid=(nblk,),
        in_specs=[tok(ATTN_WIDTH)] * 5 + [tok(POOL_WIDTH), tok(POOL_WIDTH), tok(d), tok(d),
                                          ANY, const(poolw.shape), const((1, POOL_WIDTH)), const((1, ATTN_WIDTH)),
                                          const((1, ATTN_WIDTH)), const((1, d))],
        out_specs=[tok(d), tok(4 * POOL_WIDTH), const((1, d)), const((1, ATTN_WIDTH)), const((1, ATTN_WIDTH)),
                   const((ngrp, LANES, LANES)), const((1, POOL_WIDTH))],
        out_shape=[jax.ShapeDtypeStruct((s, d), F32),
                   jax.ShapeDtypeStruct((s, 4 * POOL_WIDTH), MXU_DTYPE),
                   jax.ShapeDtypeStruct((1, d), F32),
                   jax.ShapeDtypeStruct((1, ATTN_WIDTH), F32),
                   jax.ShapeDtypeStruct((1, ATTN_WIDTH), F32),
                   jax.ShapeDtypeStruct((ngrp, LANES, LANES), F32),
                   jax.ShapeDtypeStruct((1, POOL_WIDTH), F32)],
        scratch_shapes=[pltpu.VMEM(win.shape, MXU_DTYPE), pltpu.VMEM((tm + POOL_HALO, POOL_WIDTH), F32)],
        compiler_params=_params(("arbitrary",)),
    )(dqn, dkn, dv, q32, k32, dypool, pooled, x, dh1, win, poolw, pscale, qg, kg, g1)


def _wgrad_call(a, b, bm, bn, bk, out_shape, out_block, out_index, name):
    s, m = a.shape
    _, n = b.shape
    nk = s // bk

    def body(a_ref, b_ref, o_ref, wire_ref):
        k = pl.program_id(2)
        acc = jnp.where(k > 0, o_ref[...], 0.0) + _mm_tn(a_ref[...].astype(MXU_DTYPE), b_ref[...].astype(MXU_DTYPE))
        o_ref[...] = acc
        wire_ref[...] = acc.astype(WIRE_DTYPE)

    return pl.pallas_call(
        body, name=name,
        grid=(m // bm, n // bn, nk),
        in_specs=[pl.BlockSpec((bk, bm), lambda i, j, k: (k, i)), pl.BlockSpec((bk, bn), lambda i, j, k: (k, j))],
        out_specs=[pl.BlockSpec(out_block, out_index)] * 2,
        out_shape=[jax.ShapeDtypeStruct(out_shape, F32), jax.ShapeDtypeStruct(out_shape, WIRE_DTYPE)],
        compiler_params=_params(("arbitrary", "arbitrary", "arbitrary")),
    )(a, b)


def _local_grads(x, tgt, g1, win, poolw, pscale, qg, kg, bias, g2, mlp_weights, on_mlp_grads=None):
    s, d = x.shape
    g1r, g2r = g1.reshape(1, d), g2.reshape(1, d)
    psr = pscale.reshape(1, POOL_WIDTH)
    qgr = jnp.tile(qg, N_HEADS).reshape(1, ATTN_WIDTH)
    kgr = jnp.tile(kg, N_HEADS).reshape(1, ATTN_WIDTH)
    pw_c = poolw.astype(MXU_DTYPE)
    buckets = jnp.asarray(_bucket_tables())
    bk = min(s, 4096)

    a, pooled, ypool, q32, k32, qn, kn, v = _f1_call(x, g1r, win, pw_c, psr, qgr, kgr, tm=1024)
    o, lse = _attn_fwd_call(qn, kn, v, bias)
    wout, wup, wdown = mlp_weights(o)
    mixed, c, ff, dz, dy, dh1, dypool, do, delta, dg2, loss = _f2_call(x, tgt, ypool, o, wout, wup, wdown, g2r, tm=256)
    dff = ff.shape[1]
    g_out = [g.reshape(N_CHIPS, d // N_CHIPS, d)
             for g in _wgrad_call(mixed, dh1, d, d, bk // 4, (d, d), (d, d), lambda i, j, k: (0, 0), "wgrad_out")]
    g_up = _wgrad_call(c, dz, d, dff // N_CHIPS, bk, (N_CHIPS, d, dff // N_CHIPS), (None, d, dff // N_CHIPS),
                       lambda i, j, k: (j, 0, 0), "wgrad_up")
    g_down = _wgrad_call(ff, dy, dff // N_CHIPS, d, bk, (N_CHIPS, dff // N_CHIPS, d), (None, dff // N_CHIPS, d),
                         lambda i, j, k: (i, 0, 0), "wgrad_down")
    dep = None if on_mlp_grads is None else on_mlp_grads(g_out[1], g_up[1], g_down[1])
    dqn, dkn, dv, dbias = _attn_bwd_call(qn, kn, v, do, lse, delta, bias, dep)
    dx, dproj, dg1, dqg, dkg, dpw, dps = _bproj_call(
        dqn, dkn, dv, q32, k32, dypool, pooled, x, dh1, win, pw_c, psr, qgr, kgr, g1r, tm=512)
    nin = dproj.shape[1] // N_CHIPS
    g_in = _wgrad_call(a, dproj, d, nin, bk, (N_CHIPS, d, nin), (None, d, nin), lambda i, j, k: (j, 0, 0), "wgrad_in")
    drb = _rel_bias_grad_call(dbias, buckets)
    small = dict(
        mix_norm_g=dg1.reshape(d), mlp_norm_g=dg2.reshape(d), pool_scale=dps.reshape(POOL_WIDTH),
        q_norm_g=dqg.reshape(ATTN_WIDTH), k_norm_g=dkg.reshape(ATTN_WIDTH),
        rel_bias=drb[:, :N_BUCKETS].T, pool_w=dpw)
    return loss[0, 0], dx, (g_in, g_out, g_up, g_down), small


def _coords():
    return lax.axis_index("x"), lax.axis_index("y"), lax.axis_index("c")


def _other_chips(x, y):
    return [(1 - x, y), (x, 1 - y), (1 - x, 1 - y)]


def _remote(src, dst, send_sem, recv_sem, dev):
    return pltpu.make_async_remote_copy(src_ref=src, dst_ref=dst, send_sem=send_sem, recv_sem=recv_sem,
                                        device_id=dev, device_id_type=MESH)


PAIR_FORWARD_ID = 1
PAIR_ALLGATHER_ID = 2


def _sibling_handshake():
    x, y, c = _coords()
    barrier = pltpu.get_barrier_semaphore()
    pl.semaphore_signal(barrier, inc=1, device_id=(x, y, 1 - c), device_id_type=MESH)
    pl.semaphore_wait(barrier, 1)


def _halves(a):
    return a.reshape(a.shape[:-2] + (2, a.shape[-2] // 2, a.shape[-1]))


def _place_shards_call(shards, chip_idx, nch):
    nw = len(shards)

    def body(chip_ref, *refs):
        for w in range(nw):
            refs[nw + w][...] = refs[w][...].astype(WIRE_DTYPE)

    in_specs = [pl.BlockSpec((s.shape[0] // nch, s.shape[1]), lambda i, chip_ref: (i, 0)) for s in shards]
    out_specs = [pl.BlockSpec((None, s.shape[0] // nch, s.shape[1]), lambda i, chip_ref: (chip_ref[0], i, 0))
                 for s in shards]
    return pl.pallas_call(
        body, name="weights_place",
        grid_spec=pltpu.PrefetchScalarGridSpec(num_scalar_prefetch=1, grid=(nch,),
                                               in_specs=in_specs, out_specs=out_specs),
        out_shape=[jax.ShapeDtypeStruct((N_CHIPS,) + s.shape, WIRE_DTYPE) for s in shards],
        compiler_params=_params(("arbitrary",)),
    )(chip_idx, *shards)


def _allgather_call(placed, from_chips, name, meanwhile=None):
    nw = len(placed)
    ncp = 3 * nw
    extra, extra_specs, extra_shapes, extra_out_specs, extra_scratch, extra_body = meanwhile or ([], [], [], [], [], None)
    ne, no = len(extra), len(extra_shapes)

    def body(*refs):
        outs = refs[nw + ne:2 * nw + ne]
        send1, recv1, send2, recv2 = refs[2 * nw + ne + no:2 * nw + ne + no + 4]
        x, y, c = _coords()
        chip = 2 * x + y
        others = _other_chips(x, y)
        first, passed = [], []
        if not from_chips:
            _sibling_handshake()
        if from_chips:
            for w in range(nw):
                for k, (ox, oy) in enumerate(others):
                    mine = outs[w].at[chip, c]
                    cp = _remote(mine, mine, send1.at[3 * w + k], recv1.at[3 * w + k], (ox, oy, c))
                    cp.start()
                    first.append(cp)
        if meanwhile:
            extra_body(refs[nw:nw + ne], refs[2 * nw + ne:2 * nw + ne + no], refs[2 * nw + ne + no + 4:])
        for w in range(nw):
            for k, (ox, oy) in enumerate(others):
                piece = outs[w].at[2 * ox + oy, c]
                if from_chips:
                    _remote(piece, piece, send1.at[3 * w + k], recv1.at[3 * w + k], (ox, oy, c)).wait_recv()
                cp = _remote(piece, piece, send2.at[3 * w + k], recv2.at[3 * w + k], (x, y, 1 - c))
                cp.start()
                passed.append(cp)
        for w in range(nw):
            for k, (ox, oy) in enumerate(others):
                piece = outs[w].at[2 * ox + oy, 1 - c]
                _remote(piece, piece, send2.at[3 * w + k], recv2.at[3 * w + k], (x, y, 1 - c)).wait_recv()
        for cp in first + passed:
            cp.wait_send()

    return pl.pallas_call(
        body, name=name,
        in_specs=[ANY] * nw + list(extra_specs),
        out_specs=[ANY] * nw + list(extra_out_specs),
        out_shape=[jax.ShapeDtypeStruct(s.shape, s.dtype) for s in placed] + list(extra_shapes),
        input_output_aliases={w: w for w in range(nw)},
        scratch_shapes=[pltpu.SemaphoreType.DMA((ncp,))] * 4 + list(extra_scratch),
        compiler_params=_params() if from_chips else _params(collective_id=PAIR_FORWARD_ID),
    )(*placed, *extra)


HBM_SPEC = pl.BlockSpec(memory_space=pltpu.HBM)
SEM_SPEC = pl.BlockSpec(memory_space=pltpu.SEMAPHORE)
SPLIT_EFFECT = pltpu.SideEffectType.DATAFLOW_SIDE_EFFECTING


def _in_hbm(a):
    return pltpu.with_memory_space_constraint(a, pltpu.HBM)


def _gather_copies(bufs, send, recv):
    x, y, c = _coords()
    chip = 2 * x + y
    cps = []
    for w, buf in enumerate(bufs):
        for k, (ox, oy) in enumerate(_other_chips(x, y)):
            mine, theirs = buf.at[chip, c], buf.at[2 * ox + oy, c]
            sems = (send.at[3 * w + k], recv.at[3 * w + k], (ox, oy, c))
            cps.append((_remote(mine, mine, *sems), _remote(theirs, theirs, *sems)))
    return cps


def _gather_start_call(bufs, after):
    nw = len(bufs)

    def body(*refs):
        ins, send, recv, token = refs[:nw], refs[nw + 1], refs[nw + 2], refs[2 * nw + 3]
        for out, _ in _gather_copies(ins, send, recv):
            out.start()
        token[...] = jnp.zeros(token.shape, F32)

    res = pl.pallas_call(
        body, name="weights_gather_start",
        in_specs=[HBM_SPEC] * nw + [ANY],
        out_specs=[SEM_SPEC, SEM_SPEC] + [HBM_SPEC] * nw + [pl.BlockSpec(memory_space=pltpu.VMEM)],
        out_shape=[pltpu.SemaphoreType.DMA((3 * nw,)), pltpu.SemaphoreType.DMA((3 * nw,))]
        + [pltpu.HBM(b.shape, b.dtype) for b in bufs] + [jax.ShapeDtypeStruct((8, LANES), F32)],
        input_output_aliases={w: 2 + w for w in range(nw)},
        compiler_params=pltpu.CompilerParams(has_side_effects=SPLIT_EFFECT),
    )(*[_in_hbm(b) for b in bufs], after)
    return res[0], res[1], list(res[2:2 + nw]), res[2 + nw]


def _gather_wait_call(bufs, send, recv, after):
    nw = len(bufs)

    def body(*refs):
        ins, send, recv = refs[:nw], refs[nw], refs[nw + 1]
        for out, back in _gather_copies(ins, send, recv):
            out.wait_send()
            back.wait_recv()

    return pl.pallas_call(
        body, name="weights_gather_wait",
        in_specs=[HBM_SPEC] * nw + [SEM_SPEC, SEM_SPEC, ANY],
        out_specs=[HBM_SPEC] * nw,
        out_shape=[pltpu.HBM(b.shape, b.dtype) for b in bufs],
        input_output_aliases={w: w for w in range(nw)},
        compiler_params=pltpu.CompilerParams(has_side_effects=SPLIT_EFFECT),
    )(*bufs, send, recv, after)


def _scatter_copies(srcs, lands, send, recv, wholes):
    x, y, c = _coords()
    me = 4 * x + 2 * y + c
    cps = []
    for w, (src, land) in enumerate(zip(srcs, lands)):
        for r in range(1, N_DEV):
            px, py, pc = ((1 - x) if r & 4 else x, (1 - y) if r & 2 else y, (1 - c) if r & 1 else c)
            sems = (send.at[(N_DEV - 1) * w + r - 1], recv.at[(N_DEV - 1) * w + r - 1], (px, py, pc))
            piece = src if wholes[w] else src.at[2 * px + py, pc]
            cps.append((_remote(piece, land.at[me], *sems), _remote(piece, land.at[4 * px + 2 * py + pc], *sems)))
    return cps


def _scatter_start_call(srcs, lands, wholes, name):
    nw = len(srcs)
    ncp = (N_DEV - 1) * nw

    def body(*refs):
        ins, lnd, send, recv, token = refs[:nw], refs[nw:2 * nw], refs[2 * nw], refs[2 * nw + 1], refs[4 * nw + 2]
        for out, _ in _scatter_copies(ins, lnd, send, recv, wholes):
            out.start()
        token[...] = jnp.zeros(token.shape, F32)

    res = pl.pallas_call(
        body, name=name,
        in_specs=[HBM_SPEC] * (2 * nw),
        out_specs=[SEM_SPEC, SEM_SPEC] + [HBM_SPEC] * (2 * nw) + [pl.BlockSpec(memory_space=pltpu.VMEM)],
        out_shape=[pltpu.SemaphoreType.DMA((ncp,)), pltpu.SemaphoreType.DMA((ncp,))]
        + [pltpu.HBM(b.shape, b.dtype) for b in list(srcs) + list(lands)] + [jax.ShapeDtypeStruct((8, LANES), F32)],
        input_output_aliases={i: 2 + i for i in range(2 * nw)},
        compiler_params=pltpu.CompilerParams(has_side_effects=SPLIT_EFFECT),
    )(*[_in_hbm(b) for b in list(srcs) + list(lands)])
    return res[0], res[1], list(res[2:2 + nw]), list(res[2 + nw:2 + 2 * nw]), res[2 + 2 * nw]


def _scatter_wait_call(srcs, lands, send, recv, after, wholes, name):
    nw = len(srcs)

    def body(*refs):
        ins, lnd, send, recv = refs[:nw], refs[nw:2 * nw], refs[2 * nw], refs[2 * nw + 1]
        for out, back in _scatter_copies(ins, lnd, send, recv, wholes):
            out.wait_send()
            back.wait_recv()

    res = pl.pallas_call(
        body, name=name,
        in_specs=[HBM_SPEC] * (2 * nw) + [SEM_SPEC, SEM_SPEC, ANY],
        out_specs=[HBM_SPEC] * (2 * nw),
        out_shape=[pltpu.HBM(b.shape, b.dtype) for b in list(srcs) + list(lands)],
        input_output_aliases={i: i for i in range(2 * nw)},
        compiler_params=pltpu.CompilerParams(has_side_effects=SPLIT_EFFECT),
    )(*srcs, *lands, send, recv, after)
    return list(res[nw:])


def _reduce_call(own, lands, idx, nch, name, dep=None):
    nw = len(own)
    deps = [] if dep is None else [dep]

    def body(idx_ref, *refs):
        refs = refs[:2 * nw] + refs[2 * nw + len(deps):]
        for w in range(nw):
            tot = refs[w][...]
            for r in range(1, N_DEV):
                tot = tot + refs[nw + w][idx_ref[1 + r]].astype(F32)
            refs[2 * nw + w][...] = tot

    in_specs, out_specs, out_shape = [], [], []
    for s in own:
        in_specs.append(pl.BlockSpec((None, None, s.shape[2] // nch, s.shape[3]),
                                     lambda i, idx_ref: (idx_ref[0], idx_ref[1], i, 0)))
    for s in own:
        in_specs.append(pl.BlockSpec((N_DEV, s.shape[2] // nch, s.shape[3]), lambda i, idx_ref: (0, i, 0)))
    for s in own:
        out_specs.append(pl.BlockSpec((None, s.shape[2] // nch, s.shape[3]), lambda i, idx_ref: (idx_ref[1], i, 0)))
        out_shape.append(jax.ShapeDtypeStruct((2,) + s.shape[2:], F32))
    return pl.pallas_call(
        body, name=name,
        grid_spec=pltpu.PrefetchScalarGridSpec(num_scalar_prefetch=1, grid=(nch,),
                                               in_specs=in_specs + [ANY] * len(deps), out_specs=out_specs),
        out_shape=out_shape,
        compiler_params=_params(("arbitrary",)),
    )(idx, *own, *lands, *deps)


def _pair_allgather_call(halves, name):
    nw = len(halves)

    def body(*refs):
        outs = refs[nw:2 * nw]
        send, recv = refs[2 * nw:]
        x, y, c = _coords()
        _sibling_handshake()
        cps = []
        for w in range(nw):
            cp = _remote(outs[w].at[c], outs[w].at[c], send.at[w], recv.at[w], (x, y, 1 - c))
            cp.start()
            cps.append(cp)
        for w in range(nw):
            theirs = outs[w].at[1 - c]
            _remote(theirs, theirs, send.at[w], recv.at[w], (x, y, 1 - c)).wait_recv()
        for cp in cps:
            cp.wait_send()

    outs = pl.pallas_call(
        body, name=name,
        in_specs=[ANY] * nw, out_specs=[ANY] * nw,
        out_shape=[jax.ShapeDtypeStruct(h.shape, h.dtype) for h in halves],
        input_output_aliases={w: w for w in range(nw)},
        scratch_shapes=[pltpu.SemaphoreType.DMA((nw,))] * 2,
        compiler_params=pltpu.CompilerParams(collective_id=PAIR_ALLGATHER_ID),
    )(*halves)
    return [o.reshape(2 * h.shape[1], h.shape[2]) for o, h in zip(outs, halves)]


def _adamw(w, g, m, v):
    m = ADAM_B1 * m + (1.0 - ADAM_B1) * g
    v = ADAM_B2 * v + (1.0 - ADAM_B2) * (g * g)
    m_hat = m / (1.0 - ADAM_B1 ** ADAM_STEP)
    v_hat = v / (1.0 - ADAM_B2 ** ADAM_STEP)
    delta = -ADAM_LR * (m_hat / (jnp.sqrt(v_hat) + ADAM_EPS) + ADAM_WD * w)
    return delta, m, v


def _adamw_call(ws, gs, ms, vs, nch, name):
    nw = len(ws)

    def body(*refs):
        for w in range(nw):
            g = refs[nw + w][...]
            delta, m, v = _adamw(refs[w][...], g, refs[2 * nw + w][...], refs[3 * nw + w][...])
            refs[4 * nw + w][...] = g
            refs[5 * nw + w][...] = delta
            refs[6 * nw + w][...] = m
            refs[7 * nw + w][...] = v

    specs = [pl.BlockSpec((a.shape[0] // nch, a.shape[1]), lambda i: (i, 0)) for a in ws]
    res = pl.pallas_call(
        body, name=name,
        grid=(nch,),
        in_specs=specs * 4, out_specs=specs * 4,
        out_shape=[jax.ShapeDtypeStruct(a.shape, F32) for a in ws] * 4,
        compiler_params=_params(("arbitrary",)),
    )(*ws, *gs, *ms, *vs)
    return res[:nw], res[nw:2 * nw], res[2 * nw:3 * nw], res[3 * nw:]


def _small_call(gathered, own, me_idx, w, m, v):
    def fold(row):
        tot = row[:, 0:LANES] + row[:, LANES:2 * LANES] + row[:, 2 * LANES:3 * LANES] + row[:, 3 * LANES:4 * LANES]
        return tot + pltpu.roll(tot, HEAD_DIM, axis=1)

    def body(me_ref, gh_ref, gp_ref, oh_ref, op_ref, wh, wp, mh, mp, vh, vp, *outs):
        me = me_ref[0]

        def total(ga_ref, own_ref):
            term = lambda i: jnp.where(me == i, own_ref[...], ga_ref[i]).astype(F32)
            tot = term(0)
            for i in range(1, N_DEV):
                tot = tot + term(i)
            return tot

        g_head, g_pool = total(gh_ref, oh_ref), total(gp_ref, op_ref)
        unfolded = g_head[4:5, :]
        folded = jnp.concatenate([fold(unfolded[:, :ATTN_WIDTH]), fold(unfolded[:, ATTN_WIDTH:]),
                                  jnp.zeros((1, 1024 - 2 * LANES), F32)], axis=-1)
        row = lax.broadcasted_iota(jnp.int32, g_head.shape, 0)
        g_head = jnp.where(row == 3, folded, g_head)
        for k, (g, w_ref, m_ref, v_ref) in enumerate(((g_head, wh, mh, vh), (g_pool, wp, mp, vp))):
            delta, mm, vv = _adamw(w_ref[...], g, m_ref[...], v_ref[...])
            for out, val in zip(outs[k::2], (g, delta, mm, vv)):
                out[...] = val

    vmem = pl.BlockSpec(memory_space=pltpu.VMEM)
    res = pl.pallas_call(
        body, name="adamw_small",
        in_specs=[pl.BlockSpec(memory_space=pltpu.SMEM)] + [vmem] * 10,
        out_shape=[jax.ShapeDtypeStruct(a.shape, F32) for a in w] * 4,
        compiler_params=_params(),
    )(me_idx, *gathered, *own, *w, *m, *v)
    return [(res[2 * k], res[2 * k + 1]) for k in range(4)]


def _pack_small(p, folded=True, loss=None):
    z = lambda n: jnp.zeros((n,), F32)
    rows = [p["mix_norm_g"], p["mlp_norm_g"],
            jnp.concatenate([p["pool_scale"], p["rel_bias"].reshape(-1), z(1024 - POOL_WIDTH - N_BUCKETS * N_HEADS)])]
    if folded:
        rows += [jnp.concatenate([p["q_norm_g"], z(LANES - HEAD_DIM), p["k_norm_g"], z(1024 - LANES - HEAD_DIM)]), z(1024)]
    else:
        rows += [z(1024), jnp.concatenate([p["q_norm_g"], p["k_norm_g"]])]
    rows += [z(1024) if loss is None else jnp.concatenate([loss.reshape(1), z(1023)])]
    return jnp.stack(rows + [z(1024)] * 2), p["pool_w"].reshape(-1, 1024)


def _unpack_small(head, pool):
    return dict(
        mix_norm_g=head[0], mlp_norm_g=head[1], pool_scale=head[2, :POOL_WIDTH],
        rel_bias=head[2, POOL_WIDTH:POOL_WIDTH + N_BUCKETS * N_HEADS].reshape(N_BUCKETS, N_HEADS),
        q_norm_g=head[3, :HEAD_DIM], k_norm_g=head[3, LANES:LANES + HEAD_DIM],
        pool_w=pool.reshape(len(POOL_WINDOWS), LANES, LANES))


_WEIGHT_ORDER = ("mix_norm_g", "w_in", "pool_w", "pool_scale", "q_norm_g", "k_norm_g", "rel_bias", "w_out",
                 "mlp_norm_g", "w_up", "w_down")
_BIG = ("w_in", "w_out", "w_up", "w_down")


def kernel(x, mix_norm_g, w_in, pool_w, pool_scale, q_norm_g, k_norm_g, rel_bias, w_out, mlp_norm_g, w_up, w_down, loss_target, m_mix_norm_g, m_w_in, m_pool_w, m_pool_scale, m_q_norm_g, m_k_norm_g, m_rel_bias, m_w_out, m_mlp_norm_g, m_w_up, m_w_down, v_mix_norm_g, v_w_in, v_pool_w, v_pool_scale, v_q_norm_g, v_k_norm_g, v_rel_bias, v_w_out, v_mlp_norm_g, v_w_up, v_w_down):
    w = dict(mix_norm_g=mix_norm_g, w_in=w_in, pool_w=pool_w, pool_scale=pool_scale, q_norm_g=q_norm_g,
             k_norm_g=k_norm_g, rel_bias=rel_bias, w_out=w_out, mlp_norm_g=mlp_norm_g, w_up=w_up, w_down=w_down)
    m = dict(mix_norm_g=m_mix_norm_g, w_in=m_w_in, pool_w=m_pool_w, pool_scale=m_pool_scale, q_norm_g=m_q_norm_g,
             k_norm_g=m_k_norm_g, rel_bias=m_rel_bias, w_out=m_w_out, mlp_norm_g=m_mlp_norm_g, w_up=m_w_up, w_down=m_w_down)
    v = dict(mix_norm_g=v_mix_norm_g, w_in=v_w_in, pool_w=v_pool_w, pool_scale=v_pool_scale, q_norm_g=v_q_norm_g,
             k_norm_g=v_k_norm_g, rel_bias=v_rel_bias, w_out=v_w_out, mlp_norm_g=v_mlp_norm_g, w_up=v_w_up, w_down=v_w_down)
    xc, yc, cc = _coords()

    c_idx = jnp.reshape(cc, (1,)).astype(jnp.int32)
    chip_idx = jnp.reshape(2 * xc + yc, (1,)).astype(jnp.int32)
    me = 4 * xc + 2 * yc + cc
    whole = lambda t: t.reshape(t.shape[0], t.shape[1] * t.shape[2], t.shape[3])

    placed_in = [_halves(p) for p in _place_shards_call([w[_BIG[0]]], chip_idx, nch=4)]
    win_f, bias, *placed_mlp = _allgather_call(placed_in, from_chips=True, name="weights_allgather_in",
                                               meanwhile=_startup_work(rel_bias, [w[n] for n in _BIG[1:]]))
    wsend, wrecv, in_flight, started = _gather_start_call(placed_mlp, win_f)

    def mlp_weights(after):
        landed = _gather_wait_call(in_flight, wsend, wrecv, after)
        wout_f, wup_f, wdown_f = _allgather_call(landed, from_chips=False, name="weights_pair_forward")
        return whole(wout_f).reshape(-1, wout_f.shape[-1]), whole(wup_f), whole(wdown_f)

    split = []

    def on_mlp_grads(*wire_grads):
        srcs = [_halves(g) for g in wire_grads]
        lands = [lax.empty((N_DEV,) + s.shape[2:], s.dtype) for s in srcs]
        split.extend(_scatter_start_call(srcs, lands, [False] * len(srcs), "grads_scatter_start"))
        return split[4]

    loss_part, dx, big_grads, small_grads = _local_grads(
        x[0], loss_target[0], mix_norm_g + started[0, 0], whole(win_f), pool_w, pool_scale, q_norm_g, k_norm_g, bias,
        mlp_norm_g, mlp_weights, on_mlp_grads)
    g_in, g_out, g_up, g_down = big_grads
    gsend, grecv, srcs_thru, lands_thru, _ = split
    lands_mlp = _scatter_wait_call(srcs_thru, lands_thru, gsend, grecv, g_in[1], [False] * 3, "grads_scatter_wait")

    head_own, pool_own = _pack_small(small_grads, folded=False, loss=loss_part)
    small_own = (head_own, pool_own.astype(WIRE_DTYPE))
    last_srcs = [_halves(g_in[1]), *small_own]
    last_lands = [lax.empty((N_DEV,) + last_srcs[0].shape[2:], WIRE_DTYPE)]
    last_lands += [lax.empty((N_DEV,) + a.shape, a.dtype) for a in small_own]
    lsend, lrecv, last_srcs, last_lands, last_started = _scatter_start_call(
        last_srcs, last_lands, [False, True, True], "grads_scatter_start_last")
    idx = jnp.concatenate([chip_idx, c_idx] + [jnp.reshape(jnp.bitwise_xor(me, r), (1,)) for r in range(1, N_DEV)])
    idx = idx.astype(jnp.int32)
    mlp = _BIG[1:]

    def update(names, own32, lands, tag, dep=None):
        halves = _reduce_call([_halves(g) for g in own32], lands, idx, 4, "grads_reduce_" + tag, dep)
        reduced = _pair_allgather_call(list(halves), "grads_pair_allgather_" + tag)
        return _adamw_call([w[n] for n in names], reduced, [m[n] for n in names], [v[n] for n in names], 8, "adamw_" + tag)

    out_mlp = update(mlp, [g_out[0], g_up[0], g_down[0]], lands_mlp, "mlp", last_started)
    land_in, *small_all = _scatter_wait_call(last_srcs, last_lands, lsend, lrecv, out_mlp[3][-1], [False, True, True],
                                             "grads_scatter_wait_last")
    out_in = update(_BIG[:1], [g_in[0]], [land_in], "in")
    g_pack, d_pack, m_pack, v_pack = _small_call(
        small_all, small_own, jnp.reshape(me, (1,)).astype(jnp.int32), _pack_small(w), _pack_small(m), _pack_small(v))

    grads, deltas, new_m, new_v = (_unpack_small(*a) for a in (g_pack, d_pack, m_pack, v_pack))
    for k, res in enumerate((grads, deltas, new_m, new_v)):
        res[_BIG[0]] = out_in[k][0]
        for i, n in enumerate(mlp):
            res[n] = out_mlp[k][i]
    loss = g_pack[0][LOSS_ROW, 0]
    return (loss, dx[None], *[grads[n] for n in _WEIGHT_ORDER], *[deltas[n] for n in _WEIGHT_ORDER],
            *[new_m[n] for n in _WEIGHT_ORDER], *[new_v[n] for n in _WEIGHT_ORDER])
```

```python
import math

import jax
import jax.numpy as jnp
import numpy as np
from jax import lax
from jax.experimental import pallas as pl
from jax.experimental.pallas import tpu as pltpu

F32 = jnp.float32
MXU_DTYPE = jnp.bfloat16
WIRE_DTYPE = jnp.bfloat16

NORM_EPS = 1e-6
NEG_INF = -1e30
LANES = 128
HEAD_DIM = 64
N_HEADS = 8
POOL_WIDTH = 512
ATTN_WIDTH = 512
POOL_WINDOWS = (2, 4, 8, 16)
POOL_HALO = 16
DILATED_PATTERNS = ((128, 1), (512, 4), (2048, 16))
ATT_BLOCK = 128
ATT_SUPER = ATT_BLOCK * max(dl for _, dl in DILATED_PATTERNS)
ATT_UNITS = ATT_SUPER // ATT_BLOCK
N_BUCKETS = 32
NO_BUCKET = -1
MAX_DISTANCE = 2048
N_CHIPS = 4
N_DEV = 8
ADAM_LR, ADAM_B1, ADAM_B2, ADAM_EPS, ADAM_WD, ADAM_STEP = 0.001, 0.9, 0.999, 1e-08, 0.01, 10
VMEM_LIMIT = 56 * 1024 * 1024
MESH = pl.DeviceIdType.MESH
ANY = pl.BlockSpec(memory_space=pl.ANY)

LOSS_ROW = 5


def _mm(a, b):
    return jnp.dot(a, b, preferred_element_type=F32)


def _mm_nt(a, b):
    return lax.dot_general(a, b, (((1,), (1,)), ((), ())), preferred_element_type=F32)


def _mm_tn(a, b):
    return lax.dot_general(a, b, (((0,), (0,)), ((), ())), preferred_element_type=F32)


def _params(sem=None, **kw):
    if sem is not None:
        kw["dimension_semantics"] = sem
    return pltpu.CompilerParams(vmem_limit_bytes=VMEM_LIMIT, **kw)


def _low_half():
    return lax.broadcasted_iota(jnp.int32, (1, LANES), 1) < HEAD_DIM


def _head_sum_bcast(y):
    lo = _low_half()
    outs = []
    for j in range(y.shape[1] // LANES):
        c = y[:, j * LANES:(j + 1) * LANES]
        s_lo = jnp.sum(jnp.where(lo, c, 0.0), axis=-1, keepdims=True)
        s_hi = jnp.sum(jnp.where(lo, 0.0, c), axis=-1, keepdims=True)
        outs.append(jnp.where(lo, s_lo, s_hi))
    return jnp.concatenate(outs, axis=-1)


def _rms_bwd(dn, hn, r):
    return r * (dn - hn * jnp.mean(dn * hn, axis=-1, keepdims=True))


def _t5_bucket_np(dist):
    max_exact = N_BUCKETS // 2
    d_f = np.maximum(dist, 1).astype(np.float32)
    ratio = (np.log(d_f / np.float32(max_exact)) / np.float32(math.log(MAX_DISTANCE / max_exact))).astype(np.float32)
    large = max_exact + (ratio * np.float32(N_BUCKETS - max_exact)).astype(np.int32)
    large = np.minimum(large, N_BUCKETS - 1)
    return np.where(dist < max_exact, dist, large).astype(np.int32)


def _window_offsets(dl):
    if dl == 1:
        return _by4_positions(ATT_BLOCK), _by4_positions(2 * ATT_BLOCK)
    return np.arange(ATT_BLOCK), np.arange(2 * ATT_BLOCK)


def _bucket_tables():
    tables = []
    for _, dl in DILATED_PATTERNS:
        qq, kk = _window_offsets(dl)
        dist = qq[:, None] + ATT_BLOCK - kk[None, :]
        bucket = _t5_bucket_np(np.clip(dist, 0, ATT_BLOCK) * dl)
        tables.append(np.where((dist >= 0) & (dist <= ATT_BLOCK), bucket, NO_BUCKET))
    return np.stack(tables).astype(np.int32)


def _previous_block_keys():
    return np.stack([np.broadcast_to(_window_offsets(dl)[1][None, :] < ATT_BLOCK, (ATT_BLOCK, 2 * ATT_BLOCK))
                     for _, dl in DILATED_PATTERNS])


def _f1_call(x, g1, win, poolw, pscale, qg, kg, tm):
    s, d = x.shape
    nblk = s // tm

    def body(x_ref, g1_ref, win_ref, pw_ref, ps_ref, qg_ref, kg_ref,
             a_ref, pooled_ref, ypool_ref, q32_ref, k32_ref, qn_ref, kn_ref, v_ref, ubuf):
        i = pl.program_id(0)
        xv = x_ref[...]
        r = lax.rsqrt(jnp.mean(xv * xv, axis=-1, keepdims=True) + NORM_EPS)
        a = ((xv * r) * g1_ref[...]).astype(MXU_DTYPE)
        a_ref[...] = a
        u = _mm(a, win_ref[0])
        q = _mm(a, win_ref[1])
        k = _mm(a, win_ref[2])
        v_ref[...] = _mm(a, win_ref[3])
        q32_ref[...] = q
        k32_ref[...] = k
        rq = lax.rsqrt(_head_sum_bcast(q * q) * (1.0 / HEAD_DIM) + NORM_EPS)
        qn_ref[...] = ((q * rq) * qg_ref[...]) * (HEAD_DIM ** -0.5)
        rk = lax.rsqrt(_head_sum_bcast(k * k) * (1.0 / HEAD_DIM) + NORM_EPS)
        kn_ref[...] = (k * rk) * kg_ref[...]

        ubuf[0:POOL_HALO, :] = jnp.where(i > 0, ubuf[tm:tm + POOL_HALO, :], 0.0)
        ubuf[POOL_HALO:POOL_HALO + tm, :] = u
        t = i * tm + lax.broadcasted_iota(jnp.int32, (tm, 1), 0)
        for g, w in enumerate(POOL_WINDOWS):
            ls = slice(g * LANES, (g + 1) * LANES)
            ug = u[:, ls]
            acc = ug
            for sh in range(1, w):
                acc = acc + ubuf[POOL_HALO - sh:POOL_HALO - sh + tm, ls]
            cnt = jnp.minimum(t + 1, w).astype(F32)
            pooled = (acc / cnt - ug).astype(MXU_DTYPE)
            pooled_ref[:, ls] = pooled
            ypool_ref[:, ls] = (_mm(pooled, pw_ref[g]) * ps_ref[:, ls]).astype(MXU_DTYPE)

    tok = lambda w: pl.BlockSpec((tm, w), lambda i: (i, 0))
    full = lambda shp: pl.BlockSpec(shp, lambda i: (0,) * len(shp))
    return pl.pallas_call(
        body, name="fwd_inproj",
        grid=(nblk,),
        in_specs=[tok(d), full((1, d)), full(win.shape), full(poolw.shape), full((1, POOL_WIDTH)),
                  full((1, ATTN_WIDTH)), full((1, ATTN_WIDTH))],
        out_specs=[tok(d), tok(POOL_WIDTH), tok(POOL_WIDTH), tok(ATTN_WIDTH), tok(ATTN_WIDTH),
                   tok(ATTN_WIDTH), tok(ATTN_WIDTH), tok(ATTN_WIDTH)],
        out_shape=[jax.ShapeDtypeStruct((s, d), MXU_DTYPE),
                   jax.ShapeDtypeStruct((s, POOL_WIDTH), MXU_DTYPE),
                   jax.ShapeDtypeStruct((s, POOL_WIDTH), MXU_DTYPE),
                   jax.ShapeDtypeStruct((s, ATTN_WIDTH), F32),
                   jax.ShapeDtypeStruct((s, ATTN_WIDTH), F32),
                   jax.ShapeDtypeStruct((s, ATTN_WIDTH), F32),
                   jax.ShapeDtypeStruct((s, ATTN_WIDTH), F32),
                   jax.ShapeDtypeStruct((s, ATTN_WIDTH), F32)],
        scratch_shapes=[pltpu.VMEM((tm + POOL_HALO, POOL_WIDTH), F32)],
        compiler_params=_params(("arbitrary",)),
    )(x, g1, win, poolw, pscale, qg, kg)


DEINT = 4
assert [dl for _, dl in DILATED_PATTERNS] == [1, DEINT, DEINT * DEINT]


def _by4_positions(n):
    pos = np.arange(n)
    return DEINT * (pos % (n // DEINT)) + pos // (n // DEINT)


def _masked_bias(b_ref, p, n):
    return b_ref[p, jnp.minimum(n, 1)].reshape(2 * ATT_BLOCK, 2 * ATT_BLOCK)


def _unit_rows(u, dl):
    assert isinstance(u, int)
    sq, sk = ATT_SUPER // DEINT, 2 * ATT_SUPER // DEINT
    if dl == 1:
        n = ATT_BLOCK // DEINT
        return (u, [pl.ds(r * sq + n * u, n) for r in range(DEINT)],
                [pl.ds(r * sk + sk // 2 + n * (u - 1), 2 * n) for r in range(DEINT)])
    if dl == DEINT:
        r, b = u % DEINT, u // DEINT
        return (b, [pl.ds(r * sq + ATT_BLOCK * b, ATT_BLOCK)],
                [pl.ds(r * sk + sk // 2 + ATT_BLOCK * (b - 1), 2 * ATT_BLOCK)])
    r, a = u % DEINT, u // DEINT
    return 0, [pl.ds(r * sq + a, ATT_BLOCK, stride=DEINT)], [pl.ds(r * sk + a, 2 * ATT_BLOCK, stride=DEINT)]


def _take(ref, runs):
    parts = [ref[run, :] for run in runs]
    return parts[0] if len(parts) == 1 else jnp.concatenate(parts, axis=0)


def _put(ref, runs, value, add=False):
    n = value.shape[0] // len(runs)
    for i, run in enumerate(runs):
        part = value[i * n:(i + 1) * n]
        ref[run, :] = ref[run, :] + part if add else part


def _deinterleave(dst, src, n):
    seg = n // DEINT
    for r in range(DEINT):
        dst[r * seg:(r + 1) * seg, :] = src[pl.ds(r, seg, stride=DEINT), :]


def _deinterleave_pair(dst, prev, cur):
    seg = prev.shape[0] // DEINT
    for r in range(DEINT):
        dst[2 * r * seg:(2 * r + 1) * seg, :] = prev[pl.ds(r, seg, stride=DEINT), :]
        dst[(2 * r + 1) * seg:(2 * r + 2) * seg, :] = cur[pl.ds(r, seg, stride=DEINT), :]


def _interleave(dst, src, n, offset=0):
    seg = n // DEINT
    stride = src.shape[0] // DEINT
    for r in range(DEINT):
        dst[pl.ds(r, seg, stride=DEINT), :] = src[r * stride + offset:r * stride + offset + seg, :]


def _attn_fwd_call(qn, kn, v, bias):
    s, w = qn.shape
    nsb = s // ATT_SUPER
    npair = w // LANES

    def body(q_ref, kc_ref, kp_ref, vc_ref, vp_ref, b_ref, o_ref, lse_ref, qf, kf, vf, acc_s, m_s, l_s):
        sb = pl.program_id(1)
        _deinterleave(qf, q_ref, ATT_SUPER)
        _deinterleave_pair(kf, kp_ref, kc_ref)
        _deinterleave_pair(vf, vp_ref, vc_ref)
        lo = _low_half()
        for p, (_, dl) in enumerate(DILATED_PATTERNS):
            def unit(u, carry, p=p, dl=dl):
                b, rows_q, rows_k = _unit_rows(u, dl)
                qp = _take(qf, rows_q).astype(MXU_DTYPE)
                kcat = _take(kf, rows_k).astype(MXU_DTYPE)
                vcat = _take(vf, rows_k).astype(MXU_DTYPE)
                zero = jnp.zeros_like(qp)
                q2 = jnp.concatenate([jnp.where(lo, qp, zero), jnp.where(lo, zero, qp)], axis=0)
                sc = _mm_nt(q2, kcat) + _masked_bias(b_ref, p, sb * (ATT_UNITS // dl) + b)
                m2 = jnp.max(sc, axis=-1, keepdims=True)
                pr = jnp.exp(sc - m2)
                l2 = jnp.sum(pr, axis=-1, keepdims=True)
                acc2 = _mm(pr.astype(MXU_DTYPE), vcat)
                acc = jnp.where(lo, acc2[:ATT_BLOCK], acc2[ATT_BLOCK:])
                m = jnp.where(lo, m2[:ATT_BLOCK], m2[ATT_BLOCK:])
                l = jnp.where(lo, l2[:ATT_BLOCK], l2[ATT_BLOCK:])
                if p == 0:
                    _put(acc_s, rows_q, acc)
                    _put(m_s, rows_q, m)
                    _put(l_s, rows_q, l)
                else:
                    m_old = _take(m_s, rows_q)
                    m_new = jnp.maximum(m_old, m)
                    a_old = jnp.exp(m_old - m_new)
                    a_new = jnp.exp(m - m_new)
                    _put(acc_s, rows_q, a_old * _take(acc_s, rows_q) + a_new * acc)
                    _put(l_s, rows_q, a_old * _take(l_s, rows_q) + a_new * l)
                    _put(m_s, rows_q, m_new)
                return carry

            for u in range(ATT_UNITS):
                unit(u, None)
        l = l_s[...]
        acc_s[...] = acc_s[...] / l
        m_s[...] = m_s[...] + jnp.log(l)
        _interleave(o_ref, acc_s, ATT_SUPER)
        _interleave(lse_ref, m_s, ATT_SUPER)

    cur = pl.BlockSpec((ATT_SUPER, LANES), lambda j, t: (t, j))
    prev = pl.BlockSpec((ATT_SUPER, LANES), lambda j, t: (jnp.maximum(t - 1, 0), j))
    bspec = pl.BlockSpec((len(DILATED_PATTERNS), 2, 2, ATT_BLOCK, 2 * ATT_BLOCK), lambda j, t: (0, 0, j, 0, 0))
    return pl.pallas_call(
        body, name="attn_fwd",
        grid=(npair, nsb),
        in_specs=[cur, cur, prev, cur, prev, bspec],
        out_specs=[cur, cur],
        out_shape=[jax.ShapeDtypeStruct((s, w), F32), jax.ShapeDtypeStruct((s, w), F32)],
        scratch_shapes=[pltpu.VMEM((ATT_SUPER, LANES), F32), pltpu.VMEM((2 * ATT_SUPER, LANES), F32),
                        pltpu.VMEM((2 * ATT_SUPER, LANES), F32), pltpu.VMEM((ATT_SUPER, LANES), F32),
                        pltpu.VMEM((ATT_SUPER, LANES), F32), pltpu.VMEM((ATT_SUPER, LANES), F32)],
        compiler_params=_params(("arbitrary", "arbitrary")),
    )(qn, kn, kn, v, v, bias)


def _attn_bwd_call(qn, kn, v, do, lse, delta, bias, dep=None):
    s, w = qn.shape
    nsb = s // ATT_SUPER
    npair = w // LANES
    deps = [] if dep is None else [dep]

    def body(q_ref, kc_ref, kp_ref, vc_ref, vp_ref, do_ref, lse_ref, dlt_ref, b_ref, *rest):
        dq_ref, dk_ref, dv_ref, db_ref, qf, kf, vf, dof, lsef, dltf, dqf, dkf, dvf = rest[len(deps):]
        step = pl.program_id(1)
        sb = nsb - 1 - step
        seg = ATT_SUPER // DEINT
        _deinterleave(qf, q_ref, ATT_SUPER)
        _deinterleave(dof, do_ref, ATT_SUPER)
        _deinterleave_pair(kf, kp_ref, kc_ref)
        _deinterleave_pair(vf, vp_ref, vc_ref)
        _deinterleave(lsef, lse_ref, ATT_SUPER)
        _deinterleave(dltf, dlt_ref, ATT_SUPER)

        db_ref[...] = jnp.where(step > 0, db_ref[...], 0.0)
        for acc in (dkf, dvf):
            for r in range(DEINT):
                this, before = pl.ds((2 * r + 1) * seg, seg), pl.ds(2 * r * seg, seg)
                acc[this, :] = jnp.where(step > 0, acc[before, :], 0.0)
                acc[before, :] = jnp.zeros((seg, LANES), F32)
        lo = _low_half()
        for p, (_, dl) in enumerate(DILATED_PATTERNS):
            def unit(u, carry, p=p, dl=dl):
                b, rows_q, rows_k = _unit_rows(u, dl)
                qp = _take(qf, rows_q).astype(MXU_DTYPE)
                dop = _take(dof, rows_q).astype(MXU_DTYPE)
                kcat = _take(kf, rows_k).astype(MXU_DTYPE)
                vcat = _take(vf, rows_k).astype(MXU_DTYPE)
                lse2 = _take(lsef, rows_q)
                dlt2 = _take(dltf, rows_q)
                zero = jnp.zeros_like(qp)
                q2 = jnp.concatenate([jnp.where(lo, qp, zero), jnp.where(lo, zero, qp)], axis=0)
                do2 = jnp.concatenate([jnp.where(lo, dop, zero), jnp.where(lo, zero, dop)], axis=0)
                lse_c = jnp.concatenate([lse2[:, 0:1], lse2[:, HEAD_DIM:HEAD_DIM + 1]], axis=0)
                dlt_c = jnp.concatenate([dlt2[:, 0:1], dlt2[:, HEAD_DIM:HEAD_DIM + 1]], axis=0)
                sc = _mm_nt(q2, kcat) + _masked_bias(b_ref, p, sb * (ATT_UNITS // dl) + b)
                pr = jnp.exp(sc - lse_c)
                ds = pr * (_mm_nt(do2, vcat) - dlt_c)
                db_ref[p] += ds.reshape(2, ATT_BLOCK, 2 * ATT_BLOCK)
                ds_c = ds.astype(MXU_DTYPE)
                dq2 = _mm(ds_c, kcat)
                dk = _mm_tn(ds_c, q2)
                dv = _mm_tn(pr.astype(MXU_DTYPE), do2)
                dq = jnp.where(lo, dq2[:ATT_BLOCK], dq2[ATT_BLOCK:])
                _put(dqf, rows_q, dq, add=p > 0)
                _put(dkf, rows_k, dk, add=True)
                _put(dvf, rows_k, dv, add=True)
                return carry

            for u in range(ATT_UNITS):
                unit(u, None)
        _interleave(dq_ref, dqf, ATT_SUPER)
        _interleave(dk_ref, dkf, ATT_SUPER, offset=seg)
        _interleave(dv_ref, dvf, ATT_SUPER, offset=seg)

    cur = pl.BlockSpec((ATT_SUPER, LANES), lambda j, t: (nsb - 1 - t, j))
    prev = pl.BlockSpec((ATT_SUPER, LANES), lambda j, t: (jnp.maximum(nsb - 2 - t, 0), j))
    npat = len(DILATED_PATTERNS)
    bspec = pl.BlockSpec((npat, 2, 2, ATT_BLOCK, 2 * ATT_BLOCK), lambda j, t: (0, 0, j, 0, 0))
    dbspec = pl.BlockSpec((npat, 2, ATT_BLOCK, 2 * ATT_BLOCK), lambda j, t: (0, j, 0, 0))
    sup = lambda: pltpu.VMEM((ATT_SUPER, LANES), F32)
    sup2 = lambda: pltpu.VMEM((2 * ATT_SUPER, LANES), F32)
    return pl.pallas_call(
        body, name="attn_bwd",
        grid=(npair, nsb),
        in_specs=[cur, cur, prev, cur, prev, cur, cur, cur, bspec] + [ANY] * len(deps),
        out_specs=[cur, cur, cur, dbspec],
        out_shape=[jax.ShapeDtypeStruct((s, w), F32)] * 3
        + [jax.ShapeDtypeStruct((npat, N_HEADS, ATT_BLOCK, 2 * ATT_BLOCK), F32)],
        scratch_shapes=[sup(), sup2(), sup2(), sup(), sup(), sup(), sup(), sup2(), sup2()],
        compiler_params=_params(("arbitrary", "arbitrary")),
    )(qn, kn, kn, v, v, do, lse, delta, bias, *deps)


def _bias_table_work(rel_bias):
    buckets = jnp.asarray(_bucket_tables())
    prev_keys = jnp.asarray(_previous_block_keys().astype(np.int32))
    npat = buckets.shape[0]

    def body(rb_ref, bk_ref, pk_ref, out_ref):
        for p in range(npat):
            for half in range(2):
                ks = slice(half * ATT_BLOCK, (half + 1) * ATT_BLOCK)
                bk = bk_ref[p, :, ks]
                absent = pk_ref[p, :, ks] != 0
                for h in range(N_HEADS):
                    def pick(b, acc, h=h, bk=bk):
                        return jnp.where(bk == b, rb_ref[b, h], acc)

                    tab = lax.fori_loop(0, N_BUCKETS, pick, jnp.full((ATT_BLOCK, ATT_BLOCK), NEG_INF, F32))
                    out_ref[p, 1, h, :, ks] = tab
                    out_ref[p, 0, h, :, ks] = jnp.where(absent, NEG_INF, tab)

    vmem = pl.BlockSpec(memory_space=pltpu.VMEM)
    return ([rel_bias, buckets, prev_keys], [pl.BlockSpec(memory_space=pltpu.SMEM), vmem, vmem],
            jax.ShapeDtypeStruct((npat, 2, N_HEADS, ATT_BLOCK, 2 * ATT_BLOCK), F32), body)


def _rel_bias_grad_call(dbias, buckets):
    npat, nh = dbias.shape[0], dbias.shape[1]

    def body(db_ref, bk_ref, out_ref):
        lane = lax.broadcasted_iota(jnp.int32, (nh, LANES), 1)
        out = jnp.zeros((nh, LANES), F32)
        for b in range(N_BUCKETS):
            tot = jnp.zeros((nh, 1), F32)
            for p in range(npat):
                hit = jnp.where(bk_ref[p][None] == b, db_ref[p], 0.0)
                tot = tot + jnp.sum(jnp.sum(hit, axis=1), axis=-1, keepdims=True)
            out = jnp.where(lane == b, tot, out)
        out_ref[...] = out

    return pl.pallas_call(
        body, name="rel_bias_grad",
        out_shape=jax.ShapeDtypeStruct((nh, LANES), F32),
        compiler_params=_params(),
    )(dbias, buckets)


def _f2_call(x, tgt, ypool, o, wout, wup, wdown, g2, tm):
    s, d = x.shape
    nblk = s // tm
    nch, _, fch = wup.shape
    dff = nch * fch
    mixw = POOL_WIDTH + ATTN_WIDTH

    def body(x_ref, t_ref, yp_ref, o_ref, g2_ref, wout_hbm, wup_hbm, wdown_hbm,
             mixed_ref, c_ref, ff_ref, dz_ref, dy_ref, dh1_ref, dyp_ref, do_ref, dlt_ref, dg2_ref, loss_ref,
             wout_v, wup_v, wdown_v, rz, wsem):
        i = pl.program_id(0)

        @pl.when(i == 0)
        def _():
            copies = [pltpu.make_async_copy(wout_hbm, wout_v, wsem.at[0])]
            for j in range(nch):
                copies.append(pltpu.make_async_copy(wup_hbm.at[j], wup_v.at[j], wsem.at[1 + 2 * j]))
                copies.append(pltpu.make_async_copy(wdown_hbm.at[j], wdown_v.at[j], wsem.at[2 + 2 * j]))
            for cp in copies:
                cp.start()
            dg2_ref[...] = jnp.zeros(dg2_ref.shape, F32)
            loss_ref[...] = jnp.zeros(loss_ref.shape, F32)
            for cp in copies:
                cp.wait()

        o = o_ref[...]
        mixed = jnp.concatenate([yp_ref[...], o.astype(MXU_DTYPE)], axis=-1)
        mixed_ref[...] = mixed
        h1 = x_ref[...] + _mm(mixed, wout_v[...])
        r2 = lax.rsqrt(jnp.mean(h1 * h1, axis=-1, keepdims=True) + NORM_EPS)
        hn = h1 * r2
        c = (hn * g2_ref[...]).astype(MXU_DTYPE)
        c_ref[...] = c
        y = h1
        for j in range(nch):
            cs = slice(j * fch, (j + 1) * fch)
            z = jnp.maximum(_mm(c, wup_v[j]), 0.0)
            rz[:, cs] = z
            ff = (z * z).astype(MXU_DTYPE)
            ff_ref[:, cs] = ff
            y = y + _mm(ff, wdown_v[j])
        err = y - t_ref[...]
        loss_ref[...] += jnp.sum(err * err) * (0.5 / d)
        dy = err * (1.0 / d)
        dy_c = dy.astype(MXU_DTYPE)
        dy_ref[...] = dy_c
        dc = jnp.zeros((tm, d), F32)
        for j in range(nch):
            cs = slice(j * fch, (j + 1) * fch)
            dz = (_mm_nt(dy_c, wdown_v[j]) * (2.0 * rz[:, cs])).astype(MXU_DTYPE)
            dz_ref[:, cs] = dz
            dc = dc + _mm_nt(dz, wup_v[j])
        dg2_ref[...] += jnp.sum(dc * hn, axis=0, keepdims=True)
        dh1 = dy + _rms_bwd(dc * g2_ref[...], hn, r2)
        dh1_ref[...] = dh1
        dmix = _mm_nt(dh1.astype(MXU_DTYPE), wout_v[...])
        dyp_ref[...] = dmix[:, :POOL_WIDTH]
        do = dmix[:, POOL_WIDTH:]
        do_ref[...] = do
        dlt_ref[...] = _head_sum_bcast(do * o)

    tok = lambda w: pl.BlockSpec((tm, w), lambda i: (i, 0))
    const = lambda shp: pl.BlockSpec(shp, lambda i: (0,) * len(shp))
    return pl.pallas_call(
        body, name="fwd_mlp_bwd_mlp",
        grid=(nblk,),
        in_specs=[tok(d), tok(d), tok(POOL_WIDTH), tok(ATTN_WIDTH), const((1, d)), ANY, ANY, ANY],
        out_specs=[tok(mixw), tok(d), tok(dff), tok(dff), tok(d), tok(d), tok(POOL_WIDTH), tok(ATTN_WIDTH),
                   tok(ATTN_WIDTH), const((1, d)), const((1, LANES))],
        out_shape=[jax.ShapeDtypeStruct((s, mixw), MXU_DTYPE),
                   jax.ShapeDtypeStruct((s, d), MXU_DTYPE),
                   jax.ShapeDtypeStruct((s, dff), MXU_DTYPE),
                   jax.ShapeDtypeStruct((s, dff), MXU_DTYPE),
                   jax.ShapeDtypeStruct((s, d), MXU_DTYPE),
                   jax.ShapeDtypeStruct((s, d), F32),
                   jax.ShapeDtypeStruct((s, POOL_WIDTH), F32),
                   jax.ShapeDtypeStruct((s, ATTN_WIDTH), F32),
                   jax.ShapeDtypeStruct((s, ATTN_WIDTH), F32),
                   jax.ShapeDtypeStruct((1, d), F32),
                   jax.ShapeDtypeStruct((1, LANES), F32)],
        scratch_shapes=[pltpu.VMEM(wout.shape, MXU_DTYPE), pltpu.VMEM(wup.shape, MXU_DTYPE),
                        pltpu.VMEM(wdown.shape, MXU_DTYPE), pltpu.VMEM((tm, dff), F32),
                        pltpu.SemaphoreType.DMA((1 + 2 * nch,))],
        compiler_params=_params(("arbitrary",)),
    )(x, tgt, ypool, o, g2, wout, wup, wdown)


RING_SLOTS = 3


def _bproj_call(dqn, dkn, dv, q32, k32, dypool, pooled, x, dh1, win, poolw, pscale, qg, kg, g1, tm):
    s, d = x.shape
    nblk = s // tm
    ngrp = len(POOL_WINDOWS)
    streams = [dqn, dkn, dv, q32, k32, dypool, pooled, x, dh1]
    ns = len(streams)
    assert nblk >= 2

    def body(*refs):
        hbm = refs[:ns]
        win_hbm, pw_ref, ps_ref, qg_ref, kg_ref, g1_ref = refs[ns:ns + 6]
        dx_ref, dproj_ref, dg1_ref, dqg_ref, dkg_ref, dpw_ref, dps_ref, win_v, ebuf = refs[ns + 6:ns + 15]
        rings, sems = refs[ns + 15:2 * ns + 15], refs[2 * ns + 15]
        step = pl.program_id(0)
        i = nblk - 1 - step

        def fetch(t):
            rows = pl.ds(pl.multiple_of((nblk - 1 - t) * tm, tm), tm)
            return [pltpu.make_async_copy(h.at[rows], ring.at[t % RING_SLOTS], sems.at[k, t % RING_SLOTS])
                    for k, (h, ring) in enumerate(zip(hbm, rings))]

        @pl.when(step == 0)
        def _():
            for t in range(2):
                for cp in fetch(t):
                    cp.start()
            pltpu.sync_copy(win_hbm, win_v)
            dg1_ref[...] = jnp.zeros(dg1_ref.shape, F32)
            dqg_ref[...] = jnp.zeros(dqg_ref.shape, F32)
            dkg_ref[...] = jnp.zeros(dkg_ref.shape, F32)
            dpw_ref[...] = jnp.zeros(dpw_ref.shape, F32)
            dps_ref[...] = jnp.zeros(dps_ref.shape, F32)
            ebuf[tm:tm + POOL_HALO, :] = jnp.zeros((POOL_HALO, POOL_WIDTH), F32)

        @pl.when(step > 0)
        def _():
            ebuf[tm:tm + POOL_HALO, :] = ebuf[0:POOL_HALO, :]

        @pl.when(step + 2 < nblk)
        def _():
            for cp in fetch(step + 2):
                cp.start()

        for cp in fetch(step):
            cp.wait()
        dqn_ref, dkn_ref, dv_ref, q_ref, k_ref, dyp_ref, pooled_ref, x_ref, dh1_ref = (
            ring.at[step % RING_SLOTS] for ring in rings)

        def qk_bwd(dn_sum, raw, gain, scale, dgain_ref):
            rr = lax.rsqrt(_head_sum_bcast(raw * raw) * (1.0 / HEAD_DIM) + NORM_EPS)
            hn = raw * rr
            dgain_ref[...] += jnp.sum(dn_sum * hn, axis=0, keepdims=True) * scale
            dn = dn_sum * (gain * scale)
            return rr * (dn - hn * (_head_sum_bcast(dn * hn) * (1.0 / HEAD_DIM)))

        dq = qk_bwd(dqn_ref[...], q_ref[...], qg_ref[...], HEAD_DIM ** -0.5, dqg_ref)
        dk = qk_bwd(dkn_ref[...], k_ref[...], kg_ref[...], 1.0, dkg_ref)

        t = i * tm + lax.broadcasted_iota(jnp.int32, (tm, 1), 0)
        dpooled = []
        for g, w in enumerate(POOL_WINDOWS):
            ls = slice(g * LANES, (g + 1) * LANES)
            dm = dyp_ref[:, ls]
            pg = pooled_ref[:, ls]
            dps_ref[:, ls] += jnp.sum(dm * _mm(pg, pw_ref[g]), axis=0, keepdims=True)
            dms = (dm * ps_ref[:, ls]).astype(MXU_DTYPE)
            dpw_ref[g] += _mm_tn(pg, dms)
            dpg = _mm_nt(dms, pw_ref[g])
            dpooled.append(dpg)
            ebuf[0:tm, ls] = dpg / jnp.minimum(t + 1, w).astype(F32)
        du = []
        for g, w in enumerate(POOL_WINDOWS):
            ls = slice(g * LANES, (g + 1) * LANES)
            acc = ebuf[0:tm, ls]
            for sh in range(1, w):
                acc = acc + ebuf[sh:sh + tm, ls]
            du.append(acc - dpooled[g])
        parts = [jnp.concatenate(du, axis=-1), dq, dk, dv_ref[...]]
        da = jnp.zeros((tm, d), F32)
        for p, part in enumerate(parts):
            pc = part.astype(MXU_DTYPE)
            dproj_ref[:, p * POOL_WIDTH:(p + 1) * POOL_WIDTH] = pc
            da = da + _mm_nt(pc, win_v[p])
        xv = x_ref[...]
        r = lax.rsqrt(jnp.mean(xv * xv, axis=-1, keepdims=True) + NORM_EPS)
        xn = xv * r
        dg1_ref[...] += jnp.sum(da * xn, axis=0, keepdims=True)
        dx_ref[...] = dh1_ref[...] + _rms_bwd(da * g1_ref[...], xn, r)

    tok = lambda w: pl.BlockSpec((tm, w), lambda t: (nblk - 1 - t, 0))
    const = lambda shp: pl.BlockSpec(shp, lambda t: (0,) * len(shp))
    return pl.pallas_call(
        body, name="bwd_inproj",
        grid=(nblk,),
        in_specs=[ANY] * (ns + 1) + [const(poolw.shape), const((1, POOL_WIDTH)), const((1, ATTN_WIDTH)),
                                     const((1, ATTN_WIDTH)), const((1, d))],
        out_specs=[tok(d), tok(4 * POOL_WIDTH), const((1, d)), const((1, ATTN_WIDTH)), const((1, ATTN_WIDTH)),
                   const((ngrp, LANES, LANES)), const((1, POOL_WIDTH))],
        out_shape=[jax.ShapeDtypeStruct((s, d), F32),
                   jax.ShapeDtypeStruct((s, 4 * POOL_WIDTH), MXU_DTYPE),
                   jax.ShapeDtypeStruct((1, d), F32),
                   jax.ShapeDtypeStruct((1, ATTN_WIDTH), F32),
                   jax.ShapeDtypeStruct((1, ATTN_WIDTH), F32),
                   jax.ShapeDtypeStruct((ngrp, LANES, LANES), F32),
                   jax.ShapeDtypeStruct((1, POOL_WIDTH), F32)],
        scratch_shapes=[pltpu.VMEM(win.shape, MXU_DTYPE), pltpu.VMEM((tm + POOL_HALO, POOL_WIDTH), F32)]
        + [pltpu.VMEM((RING_SLOTS, tm, a.shape[1]), a.dtype) for a in streams]
        + [pltpu.SemaphoreType.DMA((ns, RING_SLOTS))],
        compiler_params=_params(("arbitrary",)),
    )(*streams, win, poolw, pscale, qg, kg, g1)


def _wgrad_call(a, b, bm, bn, bk, out_shape, out_block, out_index, name):
    s, m = a.shape
    _, n = b.shape
    nk = s // bk

    def body(a_ref, b_ref, o_ref, wire_ref):
        k = pl.program_id(2)
        acc = jnp.where(k > 0, o_ref[...], 0.0) + _mm_tn(a_ref[...].astype(MXU_DTYPE), b_ref[...].astype(MXU_DTYPE))
        o_ref[...] = acc
        wire_ref[...] = acc.astype(WIRE_DTYPE)

    return pl.pallas_call(
        body, name=name,
        grid=(m // bm, n // bn, nk),
        in_specs=[pl.BlockSpec((bk, bm), lambda i, j, k: (k, i)), pl.BlockSpec((bk, bn), lambda i, j, k: (k, j))],
        out_specs=[pl.BlockSpec(out_block, out_index)] * 2,
        out_shape=[jax.ShapeDtypeStruct(out_shape, F32), jax.ShapeDtypeStruct(out_shape, WIRE_DTYPE)],
        compiler_params=_params(("arbitrary", "arbitrary", "arbitrary")),
    )(a, b)


def _local_grads(x, tgt, g1, win, poolw, pscale, qg, kg, bias, g2, mlp_weights, on_mlp_grads=None):
    s, d = x.shape
    g1r, g2r = g1.reshape(1, d), g2.reshape(1, d)
    psr = pscale.reshape(1, POOL_WIDTH)
    qgr = jnp.tile(qg, N_HEADS).reshape(1, ATTN_WIDTH)
    kgr = jnp.tile(kg, N_HEADS).reshape(1, ATTN_WIDTH)
    pw_c = poolw.astype(MXU_DTYPE)
    buckets = jnp.asarray(_bucket_tables())
    bk = min(s, 4096)

    a, pooled, ypool, q32, k32, qn, kn, v = _f1_call(x, g1r, win, pw_c, psr, qgr, kgr, tm=1024)
    o, lse = _attn_fwd_call(qn, kn, v, bias)
    wout, wup, wdown = mlp_weights(o)
    mixed, c, ff, dz, dy, dh1, dypool, do, delta, dg2, loss = _f2_call(x, tgt, ypool, o, wout, wup, wdown, g2r, tm=256)
    dff = ff.shape[1]
    g_out = [g.reshape(N_CHIPS, d // N_CHIPS, d)
             for g in _wgrad_call(mixed, dh1, d, d, bk // 4, (d, d), (d, d), lambda i, j, k: (0, 0), "wgrad_out")]
    g_up = _wgrad_call(c, dz, d, dff // N_CHIPS, bk, (N_CHIPS, d, dff // N_CHIPS), (None, d, dff // N_CHIPS),
                       lambda i, j, k: (j, 0, 0), "wgrad_up")
    g_down = _wgrad_call(ff, dy, dff // N_CHIPS, d, bk, (N_CHIPS, dff // N_CHIPS, d), (None, dff // N_CHIPS, d),
                         lambda i, j, k: (i, 0, 0), "wgrad_down")
    dep = None if on_mlp_grads is None else on_mlp_grads(g_out[1], g_up[1], g_down[1])
    dqn, dkn, dv, dbias = _attn_bwd_call(qn, kn, v, do, lse, delta, bias, dep)
    dx, dproj, dg1, dqg, dkg, dpw, dps = _bproj_call(
        dqn, dkn, dv, q32, k32, dypool, pooled, x, dh1, win, pw_c, psr, qgr, kgr, g1r, tm=512)
    nin = dproj.shape[1] // N_CHIPS
    g_in = _wgrad_call(a, dproj, d, nin, bk, (N_CHIPS, d, nin), (None, d, nin), lambda i, j, k: (j, 0, 0), "wgrad_in")
    drb = _rel_bias_grad_call(dbias, buckets)
    small = dict(
        mix_norm_g=dg1.reshape(d), mlp_norm_g=dg2.reshape(d), pool_scale=dps.reshape(POOL_WIDTH),
        q_norm_g=dqg.reshape(ATTN_WIDTH), k_norm_g=dkg.reshape(ATTN_WIDTH),
        rel_bias=drb[:, :N_BUCKETS].T, pool_w=dpw)
    return loss[0, 0], dx, (g_in, g_out, g_up, g_down), small


def _coords():
    return lax.axis_index("x"), lax.axis_index("y"), lax.axis_index("c")


def _other_chips(x, y):
    return [(1 - x, y), (x, 1 - y), (1 - x, 1 - y)]


def _remote(src, dst, send_sem, recv_sem, dev):
    return pltpu.make_async_remote_copy(src_ref=src, dst_ref=dst, send_sem=send_sem, recv_sem=recv_sem,
                                        device_id=dev, device_id_type=MESH)


PAIR_FORWARD_ID = 1
PAIR_ALLGATHER_ID = 2


def _sibling_handshake():
    x, y, c = _coords()
    barrier = pltpu.get_barrier_semaphore()
    pl.semaphore_signal(barrier, inc=1, device_id=(x, y, 1 - c), device_id_type=MESH)
    pl.semaphore_wait(barrier, 1)


def _halves(a):
    return a.reshape(a.shape[:-2] + (2, a.shape[-2] // 2, a.shape[-1]))


def _place_shards_call(shards, chip_idx, nch):
    nw = len(shards)

    def body(chip_ref, *refs):
        for w in range(nw):
            refs[nw + w][...] = refs[w][...].astype(WIRE_DTYPE)

    in_specs = [pl.BlockSpec((s.shape[0] // nch, s.shape[1]), lambda i, chip_ref: (i, 0)) for s in shards]
    out_specs = [pl.BlockSpec((None, s.shape[0] // nch, s.shape[1]), lambda i, chip_ref: (chip_ref[0], i, 0))
                 for s in shards]
    return pl.pallas_call(
        body, name="weights_place",
        grid_spec=pltpu.PrefetchScalarGridSpec(num_scalar_prefetch=1, grid=(nch,),
                                               in_specs=in_specs, out_specs=out_specs),
        out_shape=[jax.ShapeDtypeStruct((N_CHIPS,) + s.shape, WIRE_DTYPE) for s in shards],
        compiler_params=_params(("arbitrary",)),
    )(chip_idx, *shards)


def _allgather_call(placed, from_chips, name, meanwhile=None):
    nw = len(placed)
    ncp = 3 * nw
    extra, extra_specs, extra_shape, extra_body = meanwhile if meanwhile else ([], [], None, None)
    ne = len(extra)

    def body(*refs):
        outs = refs[nw + ne:2 * nw + ne]
        send1, recv1, send2, recv2 = refs[-4:]
        x, y, c = _coords()
        chip = 2 * x + y
        others = _other_chips(x, y)
        first, passed = [], []
        if not from_chips:
            _sibling_handshake()
        if from_chips:
            for w in range(nw):
                for k, (ox, oy) in enumerate(others):
                    mine = outs[w].at[chip, c]
                    cp = _remote(mine, mine, send1.at[3 * w + k], recv1.at[3 * w + k], (ox, oy, c))
                    cp.start()
                    first.append(cp)
        if meanwhile:
            extra_body(*refs[nw:nw + ne], refs[2 * nw + ne])
        for w in range(nw):
            for k, (ox, oy) in enumerate(others):
                piece = outs[w].at[2 * ox + oy, c]
                if from_chips:
                    _remote(piece, piece, send1.at[3 * w + k], recv1.at[3 * w + k], (ox, oy, c)).wait_recv()
                cp = _remote(piece, piece, send2.at[3 * w + k], recv2.at[3 * w + k], (x, y, 1 - c))
                cp.start()
                passed.append(cp)
        for w in range(nw):
            for k, (ox, oy) in enumerate(others):
                piece = outs[w].at[2 * ox + oy, 1 - c]
                _remote(piece, piece, send2.at[3 * w + k], recv2.at[3 * w + k], (x, y, 1 - c)).wait_recv()
        for cp in first + passed:
            cp.wait_send()

    return pl.pallas_call(
        body, name=name,
        in_specs=[ANY] * nw + list(extra_specs),
        out_specs=[ANY] * nw + ([pl.BlockSpec(memory_space=pltpu.VMEM)] if meanwhile else []),
        out_shape=[jax.ShapeDtypeStruct(s.shape, s.dtype) for s in placed] + ([extra_shape] if meanwhile else []),
        input_output_aliases={w: w for w in range(nw)},
        scratch_shapes=[pltpu.SemaphoreType.DMA((ncp,))] * 4,
        compiler_params=_params() if from_chips else _params(collective_id=PAIR_FORWARD_ID),
    )(*placed, *extra)


HBM_SPEC = pl.BlockSpec(memory_space=pltpu.HBM)
SEM_SPEC = pl.BlockSpec(memory_space=pltpu.SEMAPHORE)
SPLIT_EFFECT = pltpu.SideEffectType.DATAFLOW_SIDE_EFFECTING


def _in_hbm(a):
    return pltpu.with_memory_space_constraint(a, pltpu.HBM)


def _gather_copies(bufs, send, recv):
    x, y, c = _coords()
    chip = 2 * x + y
    cps = []
    for w, buf in enumerate(bufs):
        for k, (ox, oy) in enumerate(_other_chips(x, y)):
            mine, theirs = buf.at[chip, c], buf.at[2 * ox + oy, c]
            sems = (send.at[3 * w + k], recv.at[3 * w + k], (ox, oy, c))
            cps.append((_remote(mine, mine, *sems), _remote(theirs, theirs, *sems)))
    return cps


def _gather_start_call(bufs, after):
    nw = len(bufs)

    def body(*refs):
        ins, send, recv, token = refs[:nw], refs[nw + 1], refs[nw + 2], refs[2 * nw + 3]
        for out, _ in _gather_copies(ins, send, recv):
            out.start()
        token[...] = jnp.zeros(token.shape, F32)

    res = pl.pallas_call(
        body, name="weights_gather_start",
        in_specs=[HBM_SPEC] * nw + [ANY],
        out_specs=[SEM_SPEC, SEM_SPEC] + [HBM_SPEC] * nw + [pl.BlockSpec(memory_space=pltpu.VMEM)],
        out_shape=[pltpu.SemaphoreType.DMA((3 * nw,)), pltpu.SemaphoreType.DMA((3 * nw,))]
        + [pltpu.HBM(b.shape, b.dtype) for b in bufs] + [jax.ShapeDtypeStruct((8, LANES), F32)],
        input_output_aliases={w: 2 + w for w in range(nw)},
        compiler_params=pltpu.CompilerParams(has_side_effects=SPLIT_EFFECT),
    )(*[_in_hbm(b) for b in bufs], after)
    return res[0], res[1], list(res[2:2 + nw]), res[2 + nw]


def _gather_wait_call(bufs, send, recv, after):
    nw = len(bufs)

    def body(*refs):
        ins, send, recv = refs[:nw], refs[nw], refs[nw + 1]
        for out, back in _gather_copies(ins, send, recv):
            out.wait_send()
            back.wait_recv()

    return pl.pallas_call(
        body, name="weights_gather_wait",
        in_specs=[HBM_SPEC] * nw + [SEM_SPEC, SEM_SPEC, ANY],
        out_specs=[HBM_SPEC] * nw,
        out_shape=[pltpu.HBM(b.shape, b.dtype) for b in bufs],
        input_output_aliases={w: w for w in range(nw)},
        compiler_params=pltpu.CompilerParams(has_side_effects=SPLIT_EFFECT),
    )(*bufs, send, recv, after)


def _scatter_copies(srcs, lands, send, recv, wholes):
    x, y, c = _coords()
    me = 4 * x + 2 * y + c
    cps = []
    for w, (src, land) in enumerate(zip(srcs, lands)):
        for r in range(1, N_DEV):
            px, py, pc = ((1 - x) if r & 4 else x, (1 - y) if r & 2 else y, (1 - c) if r & 1 else c)
            sems = (send.at[(N_DEV - 1) * w + r - 1], recv.at[(N_DEV - 1) * w + r - 1], (px, py, pc))
            piece = src if wholes[w] else src.at[2 * px + py, pc]
            cps.append((_remote(piece, land.at[me], *sems), _remote(piece, land.at[4 * px + 2 * py + pc], *sems)))
    return cps


def _scatter_start_call(srcs, lands, wholes, name):
    nw = len(srcs)
    ncp = (N_DEV - 1) * nw

    def body(*refs):
        ins, lnd, send, recv, token = refs[:nw], refs[nw:2 * nw], refs[2 * nw], refs[2 * nw + 1], refs[4 * nw + 2]
        for out, _ in _scatter_copies(ins, lnd, send, recv, wholes):
            out.start()
        token[...] = jnp.zeros(token.shape, F32)

    res = pl.pallas_call(
        body, name=name,
        in_specs=[HBM_SPEC] * (2 * nw),
        out_specs=[SEM_SPEC, SEM_SPEC] + [HBM_SPEC] * (2 * nw) + [pl.BlockSpec(memory_space=pltpu.VMEM)],
        out_shape=[pltpu.SemaphoreType.DMA((ncp,)), pltpu.SemaphoreType.DMA((ncp,))]
        + [pltpu.HBM(b.shape, b.dtype) for b in list(srcs) + list(lands)] + [jax.ShapeDtypeStruct((8, LANES), F32)],
        input_output_aliases={i: 2 + i for i in range(2 * nw)},
        compiler_params=pltpu.CompilerParams(has_side_effects=SPLIT_EFFECT),
    )(*[_in_hbm(b) for b in list(srcs) + list(lands)])
    return res[0], res[1], list(res[2:2 + nw]), list(res[2 + nw:2 + 2 * nw]), res[2 + 2 * nw]


def _scatter_wait_call(srcs, lands, send, recv, after, wholes, name):
    nw = len(srcs)

    def body(*refs):
        ins, lnd, send, recv = refs[:nw], refs[nw:2 * nw], refs[2 * nw], refs[2 * nw + 1]
        for out, back in _scatter_copies(ins, lnd, send, recv, wholes):
            out.wait_send()
            back.wait_recv()

    res = pl.pallas_call(
        body, name=name,
        in_specs=[HBM_SPEC] * (2 * nw) + [SEM_SPEC, SEM_SPEC, ANY],
        out_specs=[HBM_SPEC] * (2 * nw),
        out_shape=[pltpu.HBM(b.shape, b.dtype) for b in list(srcs) + list(lands)],
        input_output_aliases={i: i for i in range(2 * nw)},
        compiler_params=pltpu.CompilerParams(has_side_effects=SPLIT_EFFECT),
    )(*srcs, *lands, send, recv, after)
    return list(res[nw:])


def _reduce_call(own, lands, idx, nch, name, dep=None):
    nw = len(own)
    deps = [] if dep is None else [dep]

    def body(idx_ref, *refs):
        refs = refs[:2 * nw] + refs[2 * nw + len(deps):]
        for w in range(nw):
            tot = refs[w][...]
            for r in range(1, N_DEV):
                tot = tot + refs[nw + w][idx_ref[1 + r]].astype(F32)
            refs[2 * nw + w][...] = tot

    in_specs, out_specs, out_shape = [], [], []
    for s in own:
        in_specs.append(pl.BlockSpec((None, None, s.shape[2] // nch, s.shape[3]),
                                     lambda i, idx_ref: (idx_ref[0], idx_ref[1], i, 0)))
    for s in own:
        in_specs.append(pl.BlockSpec((N_DEV, s.shape[2] // nch, s.shape[3]), lambda i, idx_ref: (0, i, 0)))
    for s in own:
        out_specs.append(pl.BlockSpec((None, s.shape[2] // nch, s.shape[3]), lambda i, idx_ref: (idx_ref[1], i, 0)))
        out_shape.append(jax.ShapeDtypeStruct((2,) + s.shape[2:], F32))
    return pl.pallas_call(
        body, name=name,
        grid_spec=pltpu.PrefetchScalarGridSpec(num_scalar_prefetch=1, grid=(nch,),
                                               in_specs=in_specs + [ANY] * len(deps), out_specs=out_specs),
        out_shape=out_shape,
        compiler_params=_params(("arbitrary",)),
    )(idx, *own, *lands, *deps)


def _pair_allgather_call(halves, name):
    nw = len(halves)

    def body(*refs):
        outs = refs[nw:2 * nw]
        send, recv = refs[2 * nw:]
        x, y, c = _coords()
        _sibling_handshake()
        cps = []
        for w in range(nw):
            cp = _remote(outs[w].at[c], outs[w].at[c], send.at[w], recv.at[w], (x, y, 1 - c))
            cp.start()
            cps.append(cp)
        for w in range(nw):
            theirs = outs[w].at[1 - c]
            _remote(theirs, theirs, send.at[w], recv.at[w], (x, y, 1 - c)).wait_recv()
        for cp in cps:
            cp.wait_send()

    outs = pl.pallas_call(
        body, name=name,
        in_specs=[ANY] * nw, out_specs=[ANY] * nw,
        out_shape=[jax.ShapeDtypeStruct(h.shape, h.dtype) for h in halves],
        input_output_aliases={w: w for w in range(nw)},
        scratch_shapes=[pltpu.SemaphoreType.DMA((nw,))] * 2,
        compiler_params=pltpu.CompilerParams(collective_id=PAIR_ALLGATHER_ID),
    )(*halves)
    return [o.reshape(2 * h.shape[1], h.shape[2]) for o, h in zip(outs, halves)]


def _adamw(w, g, m, v):
    m = ADAM_B1 * m + (1.0 - ADAM_B1) * g
    v = ADAM_B2 * v + (1.0 - ADAM_B2) * (g * g)
    m_hat = m / (1.0 - ADAM_B1 ** ADAM_STEP)
    v_hat = v / (1.0 - ADAM_B2 ** ADAM_STEP)
    delta = -ADAM_LR * (m_hat / (jnp.sqrt(v_hat) + ADAM_EPS) + ADAM_WD * w)
    return delta, m, v


def _adamw_call(ws, gs, ms, vs, nch, name):
    nw = len(ws)

    def body(*refs):
        for w in range(nw):
            g = refs[nw + w][...]
            delta, m, v = _adamw(refs[w][...], g, refs[2 * nw + w][...], refs[3 * nw + w][...])
            refs[4 * nw + w][...] = g
            refs[5 * nw + w][...] = delta
            refs[6 * nw + w][...] = m
            refs[7 * nw + w][...] = v

    specs = [pl.BlockSpec((a.shape[0] // nch, a.shape[1]), lambda i: (i, 0)) for a in ws]
    res = pl.pallas_call(
        body, name=name,
        grid=(nch,),
        in_specs=specs * 4, out_specs=specs * 4,
        out_shape=[jax.ShapeDtypeStruct(a.shape, F32) for a in ws] * 4,
        compiler_params=_params(("arbitrary",)),
    )(*ws, *gs, *ms, *vs)
    return res[:nw], res[nw:2 * nw], res[2 * nw:3 * nw], res[3 * nw:]


def _small_call(gathered, own, me_idx, w, m, v):
    def fold(row):
        tot = row[:, 0:LANES] + row[:, LANES:2 * LANES] + row[:, 2 * LANES:3 * LANES] + row[:, 3 * LANES:4 * LANES]
        return tot + pltpu.roll(tot, HEAD_DIM, axis=1)

    def body(me_ref, gh_ref, gp_ref, oh_ref, op_ref, wh, wp, mh, mp, vh, vp, *outs):
        me = me_ref[0]

        def total(ga_ref, own_ref):
            term = lambda i: jnp.where(me == i, own_ref[...], ga_ref[i]).astype(F32)
            tot = term(0)
            for i in range(1, N_DEV):
                tot = tot + term(i)
            return tot

        g_head, g_pool = total(gh_ref, oh_ref), total(gp_ref, op_ref)
        unfolded = g_head[4:5, :]
        folded = jnp.concatenate([fold(unfolded[:, :ATTN_WIDTH]), fold(unfolded[:, ATTN_WIDTH:]),
                                  jnp.zeros((1, 1024 - 2 * LANES), F32)], axis=-1)
        row = lax.broadcasted_iota(jnp.int32, g_head.shape, 0)
        g_head = jnp.where(row == 3, folded, g_head)
        for k, (g, w_ref, m_ref, v_ref) in enumerate(((g_head, wh, mh, vh), (g_pool, wp, mp, vp))):
            delta, mm, vv = _adamw(w_ref[...], g, m_ref[...], v_ref[...])
            for out, val in zip(outs[k::2], (g, delta, mm, vv)):
                out[...] = val

    vmem = pl.BlockSpec(memory_space=pltpu.VMEM)
    res = pl.pallas_call(
        body, name="adamw_small",
        in_specs=[pl.BlockSpec(memory_space=pltpu.SMEM)] + [vmem] * 10,
        out_shape=[jax.ShapeDtypeStruct(a.shape, F32) for a in w] * 4,
        compiler_params=_params(),
    )(me_idx, *gathered, *own, *w, *m, *v)
    return [(res[2 * k], res[2 * k + 1]) for k in range(4)]


def _pack_small(p, folded=True, loss=None):
    z = lambda n: jnp.zeros((n,), F32)
    rows = [p["mix_norm_g"], p["mlp_norm_g"],
            jnp.concatenate([p["pool_scale"], p["rel_bias"].reshape(-1), z(1024 - POOL_WIDTH - N_BUCKETS * N_HEADS)])]
    if folded:
        rows += [jnp.concatenate([p["q_norm_g"], z(LANES - HEAD_DIM), p["k_norm_g"], z(1024 - LANES - HEAD_DIM)]), z(1024)]
    else:
        rows += [z(1024), jnp.concatenate([p["q_norm_g"], p["k_norm_g"]])]
    rows += [z(1024) if loss is None else jnp.concatenate([loss.reshape(1), z(1023)])]
    return jnp.stack(rows + [z(1024)] * 2), p["pool_w"].reshape(-1, 1024)


def _unpack_small(head, pool):
    return dict(
        mix_norm_g=head[0], mlp_norm_g=head[1], pool_scale=head[2, :POOL_WIDTH],
        rel_bias=head[2, POOL_WIDTH:POOL_WIDTH + N_BUCKETS * N_HEADS].reshape(N_BUCKETS, N_HEADS),
        q_norm_g=head[3, :HEAD_DIM], k_norm_g=head[3, LANES:LANES + HEAD_DIM],
        pool_w=pool.reshape(len(POOL_WINDOWS), LANES, LANES))


_WEIGHT_ORDER = ("mix_norm_g", "w_in", "pool_w", "pool_scale", "q_norm_g", "k_norm_g", "rel_bias", "w_out",
                 "mlp_norm_g", "w_up", "w_down")
_BIG = ("w_in", "w_out", "w_up", "w_down")


def kernel(x, mix_norm_g, w_in, pool_w, pool_scale, q_norm_g, k_norm_g, rel_bias, w_out, mlp_norm_g, w_up, w_down, loss_target, m_mix_norm_g, m_w_in, m_pool_w, m_pool_scale, m_q_norm_g, m_k_norm_g, m_rel_bias, m_w_out, m_mlp_norm_g, m_w_up, m_w_down, v_mix_norm_g, v_w_in, v_pool_w, v_pool_scale, v_q_norm_g, v_k_norm_g, v_rel_bias, v_w_out, v_mlp_norm_g, v_w_up, v_w_down):
    w = dict(mix_norm_g=mix_norm_g, w_in=w_in, pool_w=pool_w, pool_scale=pool_scale, q_norm_g=q_norm_g,
             k_norm_g=k_norm_g, rel_bias=rel_bias, w_out=w_out, mlp_norm_g=mlp_norm_g, w_up=w_up, w_down=w_down)
    m = dict(mix_norm_g=m_mix_norm_g, w_in=m_w_in, pool_w=m_pool_w, pool_scale=m_pool_scale, q_norm_g=m_q_norm_g,
             k_norm_g=m_k_norm_g, rel_bias=m_rel_bias, w_out=m_w_out, mlp_norm_g=m_mlp_norm_g, w_up=m_w_up, w_down=m_w_down)
    v = dict(mix_norm_g=v_mix_norm_g, w_in=v_w_in, pool_w=v_pool_w, pool_scale=v_pool_scale, q_norm_g=v_q_norm_g,
             k_norm_g=v_k_norm_g, rel_bias=v_rel_bias, w_out=v_w_out, mlp_norm_g=v_mlp_norm_g, w_up=v_w_up, w_down=v_w_down)
    xc, yc, cc = _coords()

    c_idx = jnp.reshape(cc, (1,)).astype(jnp.int32)
    chip_idx = jnp.reshape(2 * xc + yc, (1,)).astype(jnp.int32)
    me = 4 * xc + 2 * yc + cc
    whole = lambda t: t.reshape(t.shape[0], t.shape[1] * t.shape[2], t.shape[3])

    placed = [_halves(p) for p in _place_shards_call([w[n] for n in _BIG], chip_idx, nch=4)]
    win_f, bias = _allgather_call(placed[:1], from_chips=True, name="weights_allgather_in",
                                  meanwhile=_bias_table_work(rel_bias))
    wsend, wrecv, in_flight, started = _gather_start_call(placed[1:], win_f)

    def mlp_weights(after):
        landed = _gather_wait_call(in_flight, wsend, wrecv, after)
        wout_f, wup_f, wdown_f = _allgather_call(landed, from_chips=False, name="weights_pair_forward")
        return whole(wout_f).reshape(-1, wout_f.shape[-1]), whole(wup_f), whole(wdown_f)

    split = []

    def on_mlp_grads(*wire_grads):
        srcs = [_halves(g) for g in wire_grads]
        lands = [lax.empty((N_DEV,) + s.shape[2:], s.dtype) for s in srcs]
        split.extend(_scatter_start_call(srcs, lands, [False] * len(srcs), "grads_scatter_start"))
        return split[4]

    loss_part, dx, big_grads, small_grads = _local_grads(
        x[0], loss_target[0], mix_norm_g + started[0, 0], whole(win_f), pool_w, pool_scale, q_norm_g, k_norm_g, bias,
        mlp_norm_g, mlp_weights, on_mlp_grads)
    g_in, g_out, g_up, g_down = big_grads
    gsend, grecv, srcs_thru, lands_thru, _ = split
    lands_mlp = _scatter_wait_call(srcs_thru, lands_thru, gsend, grecv, g_in[1], [False] * 3, "grads_scatter_wait")

    head_own, pool_own = _pack_small(small_grads, folded=False, loss=loss_part)
    small_own = (head_own, pool_own.astype(WIRE_DTYPE))
    last_srcs = [_halves(g_in[1]), *small_own]
    last_lands = [lax.empty((N_DEV,) + last_srcs[0].shape[2:], WIRE_DTYPE)]
    last_lands += [lax.empty((N_DEV,) + a.shape, a.dtype) for a in small_own]
    lsend, lrecv, last_srcs, last_lands, last_started = _scatter_start_call(
        last_srcs, last_lands, [False, True, True], "grads_scatter_start_last")
    idx = jnp.concatenate([chip_idx, c_idx] + [jnp.reshape(jnp.bitwise_xor(me, r), (1,)) for r in range(1, N_DEV)])
    idx = idx.astype(jnp.int32)
    mlp = _BIG[1:]

    def update(names, own32, lands, tag, dep=None):
        halves = _reduce_call([_halves(g) for g in own32], lands, idx, 4, "grads_reduce_" + tag, dep)
        reduced = _pair_allgather_call(list(halves), "grads_pair_allgather_" + tag)
        return _adamw_call([w[n] for n in names], reduced, [m[n] for n in names], [v[n] for n in names], 8, "adamw_" + tag)

    out_mlp = update(mlp, [g_out[0], g_up[0], g_down[0]], lands_mlp, "mlp", last_started)
    land_in, *small_all = _scatter_wait_call(last_srcs, last_lands, lsend, lrecv, out_mlp[3][-1], [False, True, True],
                                             "grads_scatter_wait_last")
    out_in = update(_BIG[:1], [g_in[0]], [land_in], "in")
    g_pack, d_pack, m_pack, v_pack = _small_call(
        small_all, small_own, jnp.reshape(me, (1,)).astype(jnp.int32), _pack_small(w), _pack_small(m), _pack_small(v))

    grads, deltas, new_m, new_v = (_unpack_small(*a) for a in (g_pack, d_pack, m_pack, v_pack))
    for k, res in enumerate((grads, deltas, new_m, new_v)):
        res[_BIG[0]] = out_in[k][0]
        for i, n in enumerate(mlp):
            res[n] = out_mlp[k][i]
    loss = g_pack[0][LOSS_ROW, 0]
    return (loss, dx[None], *[grads[n] for n in _WEIGHT_ORDER], *[deltas[n] for n in _WEIGHT_ORDER],
            *[new_m[n] for n in _WEIGHT_ORDER], *[new_v[n] for n in _WEIGHT_ORDER])
```

```python
import math

import jax
import jax.numpy as jnp
import numpy as np
from jax import lax
from jax.experimental import pallas as pl
from jax.experimental.pallas import tpu as pltpu

F32 = jnp.float32
MXU_DTYPE = jnp.bfloat16
WIRE_DTYPE = jnp.bfloat16

NORM_EPS = 1e-6
NEG_INF = -1e30
LANES = 128
HEAD_DIM = 64
N_HEADS = 8
POOL_WIDTH = 512
ATTN_WIDTH = 512
POOL_WINDOWS = (2, 4, 8, 16)
POOL_HALO = 16
DILATED_PATTERNS = ((128, 1), (512, 4), (2048, 16))
ATT_BLOCK = 128
ATT_SUPER = ATT_BLOCK * max(dl for _, dl in DILATED_PATTERNS)
ATT_UNITS = ATT_SUPER // ATT_BLOCK
N_BUCKETS = 32
NO_BUCKET = -1
MAX_DISTANCE = 2048
N_CHIPS = 4
N_DEV = 8
ADAM_LR, ADAM_B1, ADAM_B2, ADAM_EPS, ADAM_WD, ADAM_STEP = 0.001, 0.9, 0.999, 1e-08, 0.01, 10
VMEM_LIMIT = 56 * 1024 * 1024
MESH = pl.DeviceIdType.MESH
ANY = pl.BlockSpec(memory_space=pl.ANY)

LOSS_ROW = 5


def _mm(a, b):
    return jnp.dot(a, b, preferred_element_type=F32)


def _mm_nt(a, b):
    return lax.dot_general(a, b, (((1,), (1,)), ((), ())), preferred_element_type=F32)


def _mm_tn(a, b):
    return lax.dot_general(a, b, (((0,), (0,)), ((), ())), preferred_element_type=F32)


def _params(sem=None, **kw):
    if sem is not None:
        kw["dimension_semantics"] = sem
    return pltpu.CompilerParams(vmem_limit_bytes=VMEM_LIMIT, **kw)


def _low_half():
    return lax.broadcasted_iota(jnp.int32, (1, LANES), 1) < HEAD_DIM


def _head_sum_bcast(y):
    lo = _low_half()
    outs = []
    for j in range(y.shape[1] // LANES):
        c = y[:, j * LANES:(j + 1) * LANES]
        s_lo = jnp.sum(jnp.where(lo, c, 0.0), axis=-1, keepdims=True)
        s_hi = jnp.sum(jnp.where(lo, 0.0, c), axis=-1, keepdims=True)
        outs.append(jnp.where(lo, s_lo, s_hi))
    return jnp.concatenate(outs, axis=-1)


def _rms_bwd(dn, hn, r):
    return r * (dn - hn * jnp.mean(dn * hn, axis=-1, keepdims=True))


def _t5_bucket_np(dist):
    max_exact = N_BUCKETS // 2
    d_f = np.maximum(dist, 1).astype(np.float32)
    ratio = (np.log(d_f / np.float32(max_exact)) / np.float32(math.log(MAX_DISTANCE / max_exact))).astype(np.float32)
    large = max_exact + (ratio * np.float32(N_BUCKETS - max_exact)).astype(np.int32)
    large = np.minimum(large, N_BUCKETS - 1)
    return np.where(dist < max_exact, dist, large).astype(np.int32)


def _window_offsets(dl):
    if dl == 1:
        return _by4_positions(ATT_BLOCK), _by4_positions(2 * ATT_BLOCK)
    return np.arange(ATT_BLOCK), np.arange(2 * ATT_BLOCK)


def _bucket_tables():
    tables = []
    for _, dl in DILATED_PATTERNS:
        qq, kk = _window_offsets(dl)
        dist = qq[:, None] + ATT_BLOCK - kk[None, :]
        bucket = _t5_bucket_np(np.clip(dist, 0, ATT_BLOCK) * dl)
        tables.append(np.where((dist >= 0) & (dist <= ATT_BLOCK), bucket, NO_BUCKET))
    return np.stack(tables).astype(np.int32)


def _previous_block_keys():
    return np.stack([np.broadcast_to(_window_offsets(dl)[1][None, :] < ATT_BLOCK, (ATT_BLOCK, 2 * ATT_BLOCK))
                     for _, dl in DILATED_PATTERNS])


def _f1_call(x, g1, win, poolw, pscale, qg, kg, tm):
    s, d = x.shape
    nblk = s // tm

    def body(x_hbm, g1_ref, win_hbm, pw_ref, ps_ref, qg_ref, kg_ref,
             a_ref, pooled_ref, ypool_ref, q32_ref, k32_ref, qn_ref, kn_ref, v_ref, ubuf, win_ref, xring, xsem):
        i = pl.program_id(0)

        def fetch(t):
            rows = pl.ds(pl.multiple_of(t * tm, tm), tm)
            return pltpu.make_async_copy(x_hbm.at[rows], xring.at[t % RING_SLOTS], xsem.at[t % RING_SLOTS])

        @pl.when(i == 0)
        def _():
            for t in range(min(2, nblk)):
                fetch(t).start()
            pltpu.sync_copy(win_hbm, win_ref)

        @pl.when(i + 2 < nblk)
        def _():
            fetch(i + 2).start()

        fetch(i).wait()
        xv = xring[i % RING_SLOTS]
        r = lax.rsqrt(jnp.mean(xv * xv, axis=-1, keepdims=True) + NORM_EPS)
        a = ((xv * r) * g1_ref[...]).astype(MXU_DTYPE)
        a_ref[...] = a
        u = _mm(a, win_ref[0])
        q = _mm(a, win_ref[1])
        k = _mm(a, win_ref[2])
        v_ref[...] = _mm(a, win_ref[3])
        q32_ref[...] = q
        k32_ref[...] = k
        rq = lax.rsqrt(_head_sum_bcast(q * q) * (1.0 / HEAD_DIM) + NORM_EPS)
        qn_ref[...] = ((q * rq) * qg_ref[...]) * (HEAD_DIM ** -0.5)
        rk = lax.rsqrt(_head_sum_bcast(k * k) * (1.0 / HEAD_DIM) + NORM_EPS)
        kn_ref[...] = (k * rk) * kg_ref[...]

        ubuf[0:POOL_HALO, :] = jnp.where(i > 0, ubuf[tm:tm + POOL_HALO, :], 0.0)
        ubuf[POOL_HALO:POOL_HALO + tm, :] = u
        t = i * tm + lax.broadcasted_iota(jnp.int32, (tm, 1), 0)
        for g, w in enumerate(POOL_WINDOWS):
            ls = slice(g * LANES, (g + 1) * LANES)
            ug = u[:, ls]
            acc = ug
            for sh in range(1, w):
                acc = acc + ubuf[POOL_HALO - sh:POOL_HALO - sh + tm, ls]
            cnt = jnp.minimum(t + 1, w).astype(F32)
            pooled = (acc / cnt - ug).astype(MXU_DTYPE)
            pooled_ref[:, ls] = pooled
            ypool_ref[:, ls] = (_mm(pooled, pw_ref[g]) * ps_ref[:, ls]).astype(MXU_DTYPE)

    tok = lambda w: pl.BlockSpec((tm, w), lambda i: (i, 0))
    full = lambda shp: pl.BlockSpec(shp, lambda i: (0,) * len(shp))
    return pl.pallas_call(
        body, name="fwd_inproj",
        grid=(nblk,),
        in_specs=[ANY, full((1, d)), ANY, full(poolw.shape), full((1, POOL_WIDTH)),
                  full((1, ATTN_WIDTH)), full((1, ATTN_WIDTH))],
        out_specs=[tok(d), tok(POOL_WIDTH), tok(POOL_WIDTH), tok(ATTN_WIDTH), tok(ATTN_WIDTH),
                   tok(ATTN_WIDTH), tok(ATTN_WIDTH), tok(ATTN_WIDTH)],
        out_shape=[jax.ShapeDtypeStruct((s, d), MXU_DTYPE),
                   jax.ShapeDtypeStruct((s, POOL_WIDTH), MXU_DTYPE),
                   jax.ShapeDtypeStruct((s, POOL_WIDTH), MXU_DTYPE),
                   jax.ShapeDtypeStruct((s, ATTN_WIDTH), F32),
                   jax.ShapeDtypeStruct((s, ATTN_WIDTH), F32),
                   jax.ShapeDtypeStruct((s, ATTN_WIDTH), F32),
                   jax.ShapeDtypeStruct((s, ATTN_WIDTH), F32),
                   jax.ShapeDtypeStruct((s, ATTN_WIDTH), F32)],
        scratch_shapes=[pltpu.VMEM((tm + POOL_HALO, POOL_WIDTH), F32), pltpu.VMEM(win.shape, win.dtype),
                        pltpu.VMEM((RING_SLOTS, tm, d), F32), pltpu.SemaphoreType.DMA((RING_SLOTS,))],
        compiler_params=_params(("arbitrary",)),
    )(x, g1, win, poolw, pscale, qg, kg)


DEINT = 4
assert [dl for _, dl in DILATED_PATTERNS] == [1, DEINT, DEINT * DEINT]


def _by4_positions(n):
    pos = np.arange(n)
    return DEINT * (pos % (n // DEINT)) + pos // (n // DEINT)


def _masked_bias(b_ref, p, n):
    return b_ref[p, jnp.minimum(n, 1)].reshape(2 * ATT_BLOCK, 2 * ATT_BLOCK)


def _unit_rows(u, dl):
    assert isinstance(u, int)
    sq, sk = ATT_SUPER // DEINT, 2 * ATT_SUPER // DEINT
    if dl == 1:
        n = ATT_BLOCK // DEINT
        return (u, [pl.ds(r * sq + n * u, n) for r in range(DEINT)],
                [pl.ds(r * sk + sk // 2 + n * (u - 1), 2 * n) for r in range(DEINT)])
    if dl == DEINT:
        r, b = u % DEINT, u // DEINT
        return (b, [pl.ds(r * sq + ATT_BLOCK * b, ATT_BLOCK)],
                [pl.ds(r * sk + sk // 2 + ATT_BLOCK * (b - 1), 2 * ATT_BLOCK)])
    r, a = u % DEINT, u // DEINT
    return 0, [pl.ds(r * sq + a, ATT_BLOCK, stride=DEINT)], [pl.ds(r * sk + a, 2 * ATT_BLOCK, stride=DEINT)]


def _take(ref, runs):
    parts = [ref[run, :] for run in runs]
    return parts[0] if len(parts) == 1 else jnp.concatenate(parts, axis=0)


def _put(ref, runs, value, add=False):
    n = value.shape[0] // len(runs)
    for i, run in enumerate(runs):
        part = value[i * n:(i + 1) * n]
        ref[run, :] = ref[run, :] + part if add else part


def _deinterleave(dst, src, n):
    seg = n // DEINT
    for r in range(DEINT):
        dst[r * seg:(r + 1) * seg, :] = src[pl.ds(r, seg, stride=DEINT), :]


def _deinterleave_pair(dst, prev, cur):
    seg = prev.shape[0] // DEINT
    for r in range(DEINT):
        dst[2 * r * seg:(2 * r + 1) * seg, :] = prev[pl.ds(r, seg, stride=DEINT), :]
        dst[(2 * r + 1) * seg:(2 * r + 2) * seg, :] = cur[pl.ds(r, seg, stride=DEINT), :]


def _interleave(dst, src, n, offset=0):
    seg = n // DEINT
    stride = src.shape[0] // DEINT
    for r in range(DEINT):
        dst[pl.ds(r, seg, stride=DEINT), :] = src[r * stride + offset:r * stride + offset + seg, :]


def _attn_fwd_call(qn, kn, v, bias):
    s, w = qn.shape
    nsb = s // ATT_SUPER
    npair = w // LANES

    def body(q_ref, kc_ref, kp_ref, vc_ref, vp_ref, b_ref, o_ref, lse_ref, qf, kf, vf, acc_s, m_s, l_s):
        sb = pl.program_id(1)
        _deinterleave(qf, q_ref, ATT_SUPER)
        _deinterleave_pair(kf, kp_ref, kc_ref)
        _deinterleave_pair(vf, vp_ref, vc_ref)
        lo = _low_half()
        for p, (_, dl) in enumerate(DILATED_PATTERNS):
            def unit(u, carry, p=p, dl=dl):
                b, rows_q, rows_k = _unit_rows(u, dl)
                qp = _take(qf, rows_q).astype(MXU_DTYPE)
                kcat = _take(kf, rows_k).astype(MXU_DTYPE)
                vcat = _take(vf, rows_k).astype(MXU_DTYPE)
                zero = jnp.zeros_like(qp)
                q2 = jnp.concatenate([jnp.where(lo, qp, zero), jnp.where(lo, zero, qp)], axis=0)
                sc = _mm_nt(q2, kcat) + _masked_bias(b_ref, p, sb * (ATT_UNITS // dl) + b)
                m2 = jnp.max(sc, axis=-1, keepdims=True)
                pr = jnp.exp(sc - m2)
                l2 = jnp.sum(pr, axis=-1, keepdims=True)
                acc2 = _mm(pr.astype(MXU_DTYPE), vcat)
                acc = jnp.where(lo, acc2[:ATT_BLOCK], acc2[ATT_BLOCK:])
                m = jnp.where(lo, m2[:ATT_BLOCK], m2[ATT_BLOCK:])
                l = jnp.where(lo, l2[:ATT_BLOCK], l2[ATT_BLOCK:])
                if p == 0:
                    _put(acc_s, rows_q, acc)
                    _put(m_s, rows_q, m)
                    _put(l_s, rows_q, l)
                else:
                    m_old = _take(m_s, rows_q)
                    m_new = jnp.maximum(m_old, m)
                    a_old = jnp.exp(m_old - m_new)
                    a_new = jnp.exp(m - m_new)
                    _put(acc_s, rows_q, a_old * _take(acc_s, rows_q) + a_new * acc)
                    _put(l_s, rows_q, a_old * _take(l_s, rows_q) + a_new * l)
                    _put(m_s, rows_q, m_new)
                return carry

            for u in range(ATT_UNITS):
                unit(u, None)
        l = l_s[...]
        acc_s[...] = acc_s[...] / l
        m_s[...] = m_s[...] + jnp.log(l)
        _interleave(o_ref, acc_s, ATT_SUPER)
        _interleave(lse_ref, m_s, ATT_SUPER)

    cur = pl.BlockSpec((ATT_SUPER, LANES), lambda j, t: (t, j))
    prev = pl.BlockSpec((ATT_SUPER, LANES), lambda j, t: (jnp.maximum(t - 1, 0), j))
    bspec = pl.BlockSpec((len(DILATED_PATTERNS), 2, 2, ATT_BLOCK, 2 * ATT_BLOCK), lambda j, t: (0, 0, j, 0, 0))
    return pl.pallas_call(
        body, name="attn_fwd",
        grid=(npair, nsb),
        in_specs=[cur, cur, prev, cur, prev, bspec],
        out_specs=[cur, cur],
        out_shape=[jax.ShapeDtypeStruct((s, w), F32), jax.ShapeDtypeStruct((s, w), F32)],
        scratch_shapes=[pltpu.VMEM((ATT_SUPER, LANES), F32), pltpu.VMEM((2 * ATT_SUPER, LANES), F32),
                        pltpu.VMEM((2 * ATT_SUPER, LANES), F32), pltpu.VMEM((ATT_SUPER, LANES), F32),
                        pltpu.VMEM((ATT_SUPER, LANES), F32), pltpu.VMEM((ATT_SUPER, LANES), F32)],
        compiler_params=_params(("arbitrary", "arbitrary")),
    )(qn, kn, kn, v, v, bias)


def _attn_bwd_call(qn, kn, v, do, lse, delta, bias, dep=None):
    s, w = qn.shape
    nsb = s // ATT_SUPER
    npair = w // LANES
    deps = [] if dep is None else [dep]

    def body(q_ref, kc_ref, kp_ref, vc_ref, vp_ref, do_ref, lse_ref, dlt_ref, b_ref, *rest):
        dq_ref, dk_ref, dv_ref, db_ref, qf, kf, vf, dof, lsef, dltf, dqf, dkf, dvf = rest[len(deps):]
        step = pl.program_id(1)
        sb = nsb - 1 - step
        seg = ATT_SUPER // DEINT
        _deinterleave(qf, q_ref, ATT_SUPER)
        _deinterleave(dof, do_ref, ATT_SUPER)
        _deinterleave_pair(kf, kp_ref, kc_ref)
        _deinterleave_pair(vf, vp_ref, vc_ref)
        _deinterleave(lsef, lse_ref, ATT_SUPER)
        _deinterleave(dltf, dlt_ref, ATT_SUPER)

        db_ref[...] = jnp.where(step > 0, db_ref[...], 0.0)
        for acc in (dkf, dvf):
            for r in range(DEINT):
                this, before = pl.ds((2 * r + 1) * seg, seg), pl.ds(2 * r * seg, seg)
                acc[this, :] = jnp.where(step > 0, acc[before, :], 0.0)
                acc[before, :] = jnp.zeros((seg, LANES), F32)
        lo = _low_half()
        for p, (_, dl) in enumerate(DILATED_PATTERNS):
            def unit(u, carry, p=p, dl=dl):
                b, rows_q, rows_k = _unit_rows(u, dl)
                qp = _take(qf, rows_q).astype(MXU_DTYPE)
                dop = _take(dof, rows_q).astype(MXU_DTYPE)
                kcat = _take(kf, rows_k).astype(MXU_DTYPE)
                vcat = _take(vf, rows_k).astype(MXU_DTYPE)
                lse2 = _take(lsef, rows_q)
                dlt2 = _take(dltf, rows_q)
                zero = jnp.zeros_like(qp)
                q2 = jnp.concatenate([jnp.where(lo, qp, zero), jnp.where(lo, zero, qp)], axis=0)
                do2 = jnp.concatenate([jnp.where(lo, dop, zero), jnp.where(lo, zero, dop)], axis=0)
                lse_c = jnp.concatenate([lse2[:, 0:1], lse2[:, HEAD_DIM:HEAD_DIM + 1]], axis=0)
                dlt_c = jnp.concatenate([dlt2[:, 0:1], dlt2[:, HEAD_DIM:HEAD_DIM + 1]], axis=0)
                sc = _mm_nt(q2, kcat) + _masked_bias(b_ref, p, sb * (ATT_UNITS // dl) + b)
                pr = jnp.exp(sc - lse_c)
                ds = pr * (_mm_nt(do2, vcat) - dlt_c)
                db_ref[p] += ds.reshape(2, ATT_BLOCK, 2 * ATT_BLOCK)
                ds_c = ds.astype(MXU_DTYPE)
                dq2 = _mm(ds_c, kcat)
                dk = _mm_tn(ds_c, q2)
                dv = _mm_tn(pr.astype(MXU_DTYPE), do2)
                dq = jnp.where(lo, dq2[:ATT_BLOCK], dq2[ATT_BLOCK:])
                _put(dqf, rows_q, dq, add=p > 0)
                _put(dkf, rows_k, dk, add=True)
                _put(dvf, rows_k, dv, add=True)
                return carry

            for u in range(ATT_UNITS):
                unit(u, None)
        _interleave(dq_ref, dqf, ATT_SUPER)
        _interleave(dk_ref, dkf, ATT_SUPER, offset=seg)
        _interleave(dv_ref, dvf, ATT_SUPER, offset=seg)

    cur = pl.BlockSpec((ATT_SUPER, LANES), lambda j, t: (nsb - 1 - t, j))
    prev = pl.BlockSpec((ATT_SUPER, LANES), lambda j, t: (jnp.maximum(nsb - 2 - t, 0), j))
    npat = len(DILATED_PATTERNS)
    bspec = pl.BlockSpec((npat, 2, 2, ATT_BLOCK, 2 * ATT_BLOCK), lambda j, t: (0, 0, j, 0, 0))
    dbspec = pl.BlockSpec((npat, 2, ATT_BLOCK, 2 * ATT_BLOCK), lambda j, t: (0, j, 0, 0))
    sup = lambda: pltpu.VMEM((ATT_SUPER, LANES), F32)
    sup2 = lambda: pltpu.VMEM((2 * ATT_SUPER, LANES), F32)
    return pl.pallas_call(
        body, name="attn_bwd",
        grid=(npair, nsb),
        in_specs=[cur, cur, prev, cur, prev, cur, cur, cur, bspec] + [ANY] * len(deps),
        out_specs=[cur, cur, cur, dbspec],
        out_shape=[jax.ShapeDtypeStruct((s, w), F32)] * 3
        + [jax.ShapeDtypeStruct((npat, N_HEADS, ATT_BLOCK, 2 * ATT_BLOCK), F32)],
        scratch_shapes=[sup(), sup2(), sup2(), sup(), sup(), sup(), sup(), sup2(), sup2()],
        compiler_params=_params(("arbitrary", "arbitrary")),
    )(qn, kn, kn, v, v, do, lse, delta, bias, *deps)


def _bias_table_work(rel_bias):
    buckets = jnp.asarray(_bucket_tables())
    prev_keys = jnp.asarray(_previous_block_keys().astype(np.int32))
    npat = buckets.shape[0]

    def body(rb_ref, bk_ref, pk_ref, out_ref):
        for p in range(npat):
            for half in range(2):
                ks = slice(half * ATT_BLOCK, (half + 1) * ATT_BLOCK)
                bk = bk_ref[p, :, ks]
                absent = pk_ref[p, :, ks] != 0
                for h in range(N_HEADS):
                    def pick(b, acc, h=h, bk=bk):
                        return jnp.where(bk == b, rb_ref[b, h], acc)

                    tab = lax.fori_loop(0, N_BUCKETS, pick, jnp.full((ATT_BLOCK, ATT_BLOCK), NEG_INF, F32))
                    out_ref[p, 1, h, :, ks] = tab
                    out_ref[p, 0, h, :, ks] = jnp.where(absent, NEG_INF, tab)

    vmem = pl.BlockSpec(memory_space=pltpu.VMEM)
    return ([rel_bias, buckets, prev_keys], [pl.BlockSpec(memory_space=pltpu.SMEM), vmem, vmem],
            jax.ShapeDtypeStruct((npat, 2, N_HEADS, ATT_BLOCK, 2 * ATT_BLOCK), F32), body)


def _rel_bias_grad_call(dbias, buckets):
    npat, nh = dbias.shape[0], dbias.shape[1]

    def body(db_ref, bk_ref, out_ref):
        lane = lax.broadcasted_iota(jnp.int32, (nh, LANES), 1)
        out = jnp.zeros((nh, LANES), F32)
        for b in range(N_BUCKETS):
            tot = jnp.zeros((nh, 1), F32)
            for p in range(npat):
                hit = jnp.where(bk_ref[p][None] == b, db_ref[p], 0.0)
                tot = tot + jnp.sum(jnp.sum(hit, axis=1), axis=-1, keepdims=True)
            out = jnp.where(lane == b, tot, out)
        out_ref[...] = out

    return pl.pallas_call(
        body, name="rel_bias_grad",
        out_shape=jax.ShapeDtypeStruct((nh, LANES), F32),
        compiler_params=_params(),
    )(dbias, buckets)


def _f2_call(x, tgt, ypool, o, wout, wup, wdown, g2, tm):
    s, d = x.shape
    nblk = s // tm
    nch, _, fch = wup.shape
    dff = nch * fch
    mixw = POOL_WIDTH + ATTN_WIDTH

    def body(x_ref, t_ref, yp_ref, o_ref, g2_ref, wout_hbm, wup_hbm, wdown_hbm,
             mixed_ref, c_ref, ff_ref, dz_ref, dy_ref, dh1_ref, dyp_ref, do_ref, dlt_ref, dg2_ref, loss_ref,
             wout_v, wup_v, wdown_v, rz, wsem):
        i = pl.program_id(0)

        @pl.when(i == 0)
        def _():
            copies = [pltpu.make_async_copy(wout_hbm, wout_v, wsem.at[0])]
            for j in range(nch):
                copies.append(pltpu.make_async_copy(wup_hbm.at[j], wup_v.at[j], wsem.at[1 + 2 * j]))
                copies.append(pltpu.make_async_copy(wdown_hbm.at[j], wdown_v.at[j], wsem.at[2 + 2 * j]))
            for cp in copies:
                cp.start()
            dg2_ref[...] = jnp.zeros(dg2_ref.shape, F32)
            loss_ref[...] = jnp.zeros(loss_ref.shape, F32)
            for cp in copies:
                cp.wait()

        o = o_ref[...]
        mixed = jnp.concatenate([yp_ref[...], o.astype(MXU_DTYPE)], axis=-1)
        mixed_ref[...] = mixed
        h1 = x_ref[...] + _mm(mixed, wout_v[...])
        r2 = lax.rsqrt(jnp.mean(h1 * h1, axis=-1, keepdims=True) + NORM_EPS)
        hn = h1 * r2
        c = (hn * g2_ref[...]).astype(MXU_DTYPE)
        c_ref[...] = c
        y = h1
        for j in range(nch):
            cs = slice(j * fch, (j + 1) * fch)
            z = jnp.maximum(_mm(c, wup_v[j]), 0.0)
            rz[:, cs] = z
            ff = (z * z).astype(MXU_DTYPE)
            ff_ref[:, cs] = ff
            y = y + _mm(ff, wdown_v[j])
        err = y - t_ref[...]
        loss_ref[...] += jnp.sum(err * err) * (0.5 / d)
        dy = err * (1.0 / d)
        dy_c = dy.astype(MXU_DTYPE)
        dy_ref[...] = dy_c
        dc = jnp.zeros((tm, d), F32)
        for j in range(nch):
            cs = slice(j * fch, (j + 1) * fch)
            dz = (_mm_nt(dy_c, wdown_v[j]) * (2.0 * rz[:, cs])).astype(MXU_DTYPE)
            dz_ref[:, cs] = dz
            dc = dc + _mm_nt(dz, wup_v[j])
        dg2_ref[...] += jnp.sum(dc * hn, axis=0, keepdims=True)
        dh1 = dy + _rms_bwd(dc * g2_ref[...], hn, r2)
        dh1_ref[...] = dh1
        dmix = _mm_nt(dh1.astype(MXU_DTYPE), wout_v[...])
        dyp_ref[...] = dmix[:, :POOL_WIDTH]
        do = dmix[:, POOL_WIDTH:]
        do_ref[...] = do
        dlt_ref[...] = _head_sum_bcast(do * o)

    tok = lambda w: pl.BlockSpec((tm, w), lambda i: (i, 0))
    const = lambda shp: pl.BlockSpec(shp, lambda i: (0,) * len(shp))
    return pl.pallas_call(
        body, name="fwd_mlp_bwd_mlp",
        grid=(nblk,),
        in_specs=[tok(d), tok(d), tok(POOL_WIDTH), tok(ATTN_WIDTH), const((1, d)), ANY, ANY, ANY],
        out_specs=[tok(mixw), tok(d), tok(dff), tok(dff), tok(d), tok(d), tok(POOL_WIDTH), tok(ATTN_WIDTH),
                   tok(ATTN_WIDTH), const((1, d)), const((1, LANES))],
        out_shape=[jax.ShapeDtypeStruct((s, mixw), MXU_DTYPE),
                   jax.ShapeDtypeStruct((s, d), MXU_DTYPE),
                   jax.ShapeDtypeStruct((s, dff), MXU_DTYPE),
                   jax.ShapeDtypeStruct((s, dff), MXU_DTYPE),
                   jax.ShapeDtypeStruct((s, d), MXU_DTYPE),
                   jax.ShapeDtypeStruct((s, d), F32),
                   jax.ShapeDtypeStruct((s, POOL_WIDTH), F32),
                   jax.ShapeDtypeStruct((s, ATTN_WIDTH), F32),
                   jax.ShapeDtypeStruct((s, ATTN_WIDTH), F32),
                   jax.ShapeDtypeStruct((1, d), F32),
                   jax.ShapeDtypeStruct((1, LANES), F32)],
        scratch_shapes=[pltpu.VMEM(wout.shape, MXU_DTYPE), pltpu.VMEM(wup.shape, MXU_DTYPE),
                        pltpu.VMEM(wdown.shape, MXU_DTYPE), pltpu.VMEM((tm, dff), F32),
                        pltpu.SemaphoreType.DMA((1 + 2 * nch,))],
        compiler_params=_params(("arbitrary",)),
    )(x, tgt, ypool, o, g2, wout, wup, wdown)


RING_SLOTS = 3


def _bproj_call(dqn, dkn, dv, q32, k32, dypool, pooled, x, dh1, win, poolw, pscale, qg, kg, g1, tm):
    s, d = x.shape
    nblk = s // tm
    ngrp = len(POOL_WINDOWS)
    streams = [dqn, dkn, dv, q32, k32, dypool, pooled, x, dh1]
    ns = len(streams)
    assert nblk >= 2

    def body(*refs):
        hbm = refs[:ns]
        win_hbm, pw_ref, ps_ref, qg_ref, kg_ref, g1_ref = refs[ns:ns + 6]
        dx_ref, dproj_ref, dg1_ref, dqg_ref, dkg_ref, dpw_ref, dps_ref, win_v, ebuf = refs[ns + 6:ns + 15]
        rings, sems = refs[ns + 15:2 * ns + 15], refs[2 * ns + 15]
        step = pl.program_id(0)
        i = nblk - 1 - step

        def fetch(t):
            rows = pl.ds(pl.multiple_of((nblk - 1 - t) * tm, tm), tm)
            return [pltpu.make_async_copy(h.at[rows], ring.at[t % RING_SLOTS], sems.at[k, t % RING_SLOTS])
                    for k, (h, ring) in enumerate(zip(hbm, rings))]

        @pl.when(step == 0)
        def _():
            for t in range(2):
                for cp in fetch(t):
                    cp.start()
            pltpu.sync_copy(win_hbm, win_v)
            dg1_ref[...] = jnp.zeros(dg1_ref.shape, F32)
            dqg_ref[...] = jnp.zeros(dqg_ref.shape, F32)
            dkg_ref[...] = jnp.zeros(dkg_ref.shape, F32)
            dpw_ref[...] = jnp.zeros(dpw_ref.shape, F32)
            dps_ref[...] = jnp.zeros(dps_ref.shape, F32)
            ebuf[tm:tm + POOL_HALO, :] = jnp.zeros((POOL_HALO, POOL_WIDTH), F32)

        @pl.when(step > 0)
        def _():
            ebuf[tm:tm + POOL_HALO, :] = ebuf[0:POOL_HALO, :]

        @pl.when(step + 2 < nblk)
        def _():
            for cp in fetch(step + 2):
                cp.start()

        for cp in fetch(step):
            cp.wait()
        dqn_ref, dkn_ref, dv_ref, q_ref, k_ref, dyp_ref, pooled_ref, x_ref, dh1_ref = (
            ring.at[step % RING_SLOTS] for ring in rings)

        def qk_bwd(dn_sum, raw, gain, scale, dgain_ref):
            rr = lax.rsqrt(_head_sum_bcast(raw * raw) * (1.0 / HEAD_DIM) + NORM_EPS)
            hn = raw * rr
            dgain_ref[...] += jnp.sum(dn_sum * hn, axis=0, keepdims=True) * scale
            dn = dn_sum * (gain * scale)
            return rr * (dn - hn * (_head_sum_bcast(dn * hn) * (1.0 / HEAD_DIM)))

        dq = qk_bwd(dqn_ref[...], q_ref[...], qg_ref[...], HEAD_DIM ** -0.5, dqg_ref)
        dk = qk_bwd(dkn_ref[...], k_ref[...], kg_ref[...], 1.0, dkg_ref)

        t = i * tm + lax.broadcasted_iota(jnp.int32, (tm, 1), 0)
        dpooled = []
        for g, w in enumerate(POOL_WINDOWS):
            ls = slice(g * LANES, (g + 1) * LANES)
            dm = dyp_ref[:, ls]
            pg = pooled_ref[:, ls]
            dps_ref[:, ls] += jnp.sum(dm * _mm(pg, pw_ref[g]), axis=0, keepdims=True)
            dms = (dm * ps_ref[:, ls]).astype(MXU_DTYPE)
            dpw_ref[g] += _mm_tn(pg, dms)
            dpg = _mm_nt(dms, pw_ref[g])
            dpooled.append(dpg)
            ebuf[0:tm, ls] = dpg / jnp.minimum(t + 1, w).astype(F32)
        du = []
        for g, w in enumerate(POOL_WINDOWS):
            ls = slice(g * LANES, (g + 1) * LANES)
            acc = ebuf[0:tm, ls]
            for sh in range(1, w):
                acc = acc + ebuf[sh:sh + tm, ls]
            du.append(acc - dpooled[g])
        parts = [jnp.concatenate(du, axis=-1), dq, dk, dv_ref[...]]
        da = jnp.zeros((tm, d), F32)
        for p, part in enumerate(parts):
            pc = part.astype(MXU_DTYPE)
            dproj_ref[:, p * POOL_WIDTH:(p + 1) * POOL_WIDTH] = pc
            da = da + _mm_nt(pc, win_v[p])
        xv = x_ref[...]
        r = lax.rsqrt(jnp.mean(xv * xv, axis=-1, keepdims=True) + NORM_EPS)
        xn = xv * r
        dg1_ref[...] += jnp.sum(da * xn, axis=0, keepdims=True)
        dx_ref[...] = dh1_ref[...] + _rms_bwd(da * g1_ref[...], xn, r)

    tok = lambda w: pl.BlockSpec((tm, w), lambda t: (nblk - 1 - t, 0))
    const = lambda shp: pl.BlockSpec(shp, lambda t: (0,) * len(shp))
    return pl.pallas_call(
        body, name="bwd_inproj",
        grid=(nblk,),
        in_specs=[ANY] * (ns + 1) + [const(poolw.shape), const((1, POOL_WIDTH)), const((1, ATTN_WIDTH)),
                                     const((1, ATTN_WIDTH)), const((1, d))],
        out_specs=[tok(d), tok(4 * POOL_WIDTH), const((1, d)), const((1, ATTN_WIDTH)), const((1, ATTN_WIDTH)),
                   const((ngrp, LANES, LANES)), const((1, POOL_WIDTH))],
        out_shape=[jax.ShapeDtypeStruct((s, d), F32),
                   jax.ShapeDtypeStruct((s, 4 * POOL_WIDTH), MXU_DTYPE),
                   jax.ShapeDtypeStruct((1, d), F32),
                   jax.ShapeDtypeStruct((1, ATTN_WIDTH), F32),
                   jax.ShapeDtypeStruct((1, ATTN_WIDTH), F32),
                   jax.ShapeDtypeStruct((ngrp, LANES, LANES), F32),
                   jax.ShapeDtypeStruct((1, POOL_WIDTH), F32)],
        scratch_shapes=[pltpu.VMEM(win.shape, MXU_DTYPE), pltpu.VMEM((tm + POOL_HALO, POOL_WIDTH), F32)]
        + [pltpu.VMEM((RING_SLOTS, tm, a.shape[1]), a.dtype) for a in streams]
        + [pltpu.SemaphoreType.DMA((ns, RING_SLOTS))],
        compiler_params=_params(("arbitrary",)),
    )(*streams, win, poolw, pscale, qg, kg, g1)


def _wgrad_call(a, b, bm, bn, bk, out_shape, out_block, out_index, name):
    s, m = a.shape
    _, n = b.shape
    nk = s // bk

    def body(a_ref, b_ref, o_ref, wire_ref):
        k = pl.program_id(2)
        acc = jnp.where(k > 0, o_ref[...], 0.0) + _mm_tn(a_ref[...].astype(MXU_DTYPE), b_ref[...].astype(MXU_DTYPE))
        o_ref[...] = acc
        wire_ref[...] = acc.astype(WIRE_DTYPE)

    return pl.pallas_call(
        body, name=name,
        grid=(m // bm, n // bn, nk),
        in_specs=[pl.BlockSpec((bk, bm), lambda i, j, k: (k, i)), pl.BlockSpec((bk, bn), lambda i, j, k: (k, j))],
        out_specs=[pl.BlockSpec(out_block, out_index)] * 2,
        out_shape=[jax.ShapeDtypeStruct(out_shape, F32), jax.ShapeDtypeStruct(out_shape, WIRE_DTYPE)],
        compiler_params=_params(("arbitrary", "arbitrary", "arbitrary")),
    )(a, b)


def _local_grads(x, tgt, g1, win, poolw, pscale, qg, kg, bias, g2, mlp_weights, on_mlp_grads=None):
    s, d = x.shape
    g1r, g2r = g1.reshape(1, d), g2.reshape(1, d)
    psr = pscale.reshape(1, POOL_WIDTH)
    qgr = jnp.tile(qg, N_HEADS).reshape(1, ATTN_WIDTH)
    kgr = jnp.tile(kg, N_HEADS).reshape(1, ATTN_WIDTH)
    pw_c = poolw.astype(MXU_DTYPE)
    buckets = jnp.asarray(_bucket_tables())
    bk = min(s, 4096)

    a, pooled, ypool, q32, k32, qn, kn, v = _f1_call(x, g1r, win, pw_c, psr, qgr, kgr, tm=1024)
    o, lse = _attn_fwd_call(qn, kn, v, bias)
    wout, wup, wdown = mlp_weights(o)
    mixed, c, ff, dz, dy, dh1, dypool, do, delta, dg2, loss = _f2_call(x, tgt, ypool, o, wout, wup, wdown, g2r, tm=256)
    dff = ff.shape[1]
    g_out = [g.reshape(N_CHIPS, d // N_CHIPS, d)
             for g in _wgrad_call(mixed, dh1, d, d, bk // 4, (d, d), (d, d), lambda i, j, k: (0, 0), "wgrad_out")]
    g_up = _wgrad_call(c, dz, d, dff // N_CHIPS, bk, (N_CHIPS, d, dff // N_CHIPS), (None, d, dff // N_CHIPS),
                       lambda i, j, k: (j, 0, 0), "wgrad_up")
    g_down = _wgrad_call(ff, dy, dff // N_CHIPS, d, bk, (N_CHIPS, dff // N_CHIPS, d), (None, dff // N_CHIPS, d),
                         lambda i, j, k: (i, 0, 0), "wgrad_down")
    dep = None if on_mlp_grads is None else on_mlp_grads(g_out[1], g_up[1], g_down[1])
    dqn, dkn, dv, dbias = _attn_bwd_call(qn, kn, v, do, lse, delta, bias, dep)
    dx, dproj, dg1, dqg, dkg, dpw, dps = _bproj_call(
        dqn, dkn, dv, q32, k32, dypool, pooled, x, dh1, win, pw_c, psr, qgr, kgr, g1r, tm=512)
    nin = dproj.shape[1] // N_CHIPS
    g_in = _wgrad_call(a, dproj, d, nin, bk, (N_CHIPS, d, nin), (None, d, nin), lambda i, j, k: (j, 0, 0), "wgrad_in")
    drb = _rel_bias_grad_call(dbias, buckets)
    small = dict(
        mix_norm_g=dg1.reshape(d), mlp_norm_g=dg2.reshape(d), pool_scale=dps.reshape(POOL_WIDTH),
        q_norm_g=dqg.reshape(ATTN_WIDTH), k_norm_g=dkg.reshape(ATTN_WIDTH),
        rel_bias=drb[:, :N_BUCKETS].T, pool_w=dpw)
    return loss[0, 0], dx, (g_in, g_out, g_up, g_down), small


def _coords():
    return lax.axis_index("x"), lax.axis_index("y"), lax.axis_index("c")


def _other_chips(x, y):
    return [(1 - x, y), (x, 1 - y), (1 - x, 1 - y)]


def _remote(src, dst, send_sem, recv_sem, dev):
    return pltpu.make_async_remote_copy(src_ref=src, dst_ref=dst, send_sem=send_sem, recv_sem=recv_sem,
                                        device_id=dev, device_id_type=MESH)


PAIR_FORWARD_ID = 1
PAIR_ALLGATHER_ID = 2


def _sibling_handshake():
    x, y, c = _coords()
    barrier = pltpu.get_barrier_semaphore()
    pl.semaphore_signal(barrier, inc=1, device_id=(x, y, 1 - c), device_id_type=MESH)
    pl.semaphore_wait(barrier, 1)


def _halves(a):
    return a.reshape(a.shape[:-2] + (2, a.shape[-2] // 2, a.shape[-1]))


def _place_shards_call(shards, chip_idx, nch):
    nw = len(shards)

    def body(chip_ref, *refs):
        for w in range(nw):
            refs[nw + w][...] = refs[w][...].astype(WIRE_DTYPE)

    in_specs = [pl.BlockSpec((s.shape[0] // nch, s.shape[1]), lambda i, chip_ref: (i, 0)) for s in shards]
    out_specs = [pl.BlockSpec((None, s.shape[0] // nch, s.shape[1]), lambda i, chip_ref: (chip_ref[0], i, 0))
                 for s in shards]
    return pl.pallas_call(
        body, name="weights_place",
        grid_spec=pltpu.PrefetchScalarGridSpec(num_scalar_prefetch=1, grid=(nch,),
                                               in_specs=in_specs, out_specs=out_specs),
        out_shape=[jax.ShapeDtypeStruct((N_CHIPS,) + s.shape, WIRE_DTYPE) for s in shards],
        compiler_params=_params(("arbitrary",)),
    )(chip_idx, *shards)


def _allgather_call(placed, from_chips, name, meanwhile=None):
    nw = len(placed)
    ncp = 3 * nw
    extra, extra_specs, extra_shape, extra_body = meanwhile if meanwhile else ([], [], None, None)
    ne = len(extra)

    def body(*refs):
        outs = refs[nw + ne:2 * nw + ne]
        send1, recv1, send2, recv2 = refs[-4:]
        x, y, c = _coords()
        chip = 2 * x + y
        others = _other_chips(x, y)
        first, passed = [], []
        if not from_chips:
            _sibling_handshake()
        if from_chips:
            for w in range(nw):
                for k, (ox, oy) in enumerate(others):
                    mine = outs[w].at[chip, c]
                    cp = _remote(mine, mine, send1.at[3 * w + k], recv1.at[3 * w + k], (ox, oy, c))
                    cp.start()
                    first.append(cp)
        if meanwhile:
            extra_body(*refs[nw:nw + ne], refs[2 * nw + ne])
        for w in range(nw):
            for k, (ox, oy) in enumerate(others):
                piece = outs[w].at[2 * ox + oy, c]
                if from_chips:
                    _remote(piece, piece, send1.at[3 * w + k], recv1.at[3 * w + k], (ox, oy, c)).wait_recv()
                cp = _remote(piece, piece, send2.at[3 * w + k], recv2.at[3 * w + k], (x, y, 1 - c))
                cp.start()
                passed.append(cp)
        for w in range(nw):
            for k, (ox, oy) in enumerate(others):
                piece = outs[w].at[2 * ox + oy, 1 - c]
                _remote(piece, piece, send2.at[3 * w + k], recv2.at[3 * w + k], (x, y, 1 - c)).wait_recv()
        for cp in first + passed:
            cp.wait_send()

    return pl.pallas_call(
        body, name=name,
        in_specs=[ANY] * nw + list(extra_specs),
        out_specs=[ANY] * nw + ([pl.BlockSpec(memory_space=pltpu.VMEM)] if meanwhile else []),
        out_shape=[jax.ShapeDtypeStruct(s.shape, s.dtype) for s in placed] + ([extra_shape] if meanwhile else []),
        input_output_aliases={w: w for w in range(nw)},
        scratch_shapes=[pltpu.SemaphoreType.DMA((ncp,))] * 4,
        compiler_params=_params() if from_chips else _params(collective_id=PAIR_FORWARD_ID),
    )(*placed, *extra)


HBM_SPEC = pl.BlockSpec(memory_space=pltpu.HBM)
SEM_SPEC = pl.BlockSpec(memory_space=pltpu.SEMAPHORE)
SPLIT_EFFECT = pltpu.SideEffectType.DATAFLOW_SIDE_EFFECTING


def _in_hbm(a):
    return pltpu.with_memory_space_constraint(a, pltpu.HBM)


def _gather_copies(bufs, send, recv):
    x, y, c = _coords()
    chip = 2 * x + y
    cps = []
    for w, buf in enumerate(bufs):
        for k, (ox, oy) in enumerate(_other_chips(x, y)):
            mine, theirs = buf.at[chip, c], buf.at[2 * ox + oy, c]
            sems = (send.at[3 * w + k], recv.at[3 * w + k], (ox, oy, c))
            cps.append((_remote(mine, mine, *sems), _remote(theirs, theirs, *sems)))
    return cps


def _gather_start_call(bufs, after):
    nw = len(bufs)

    def body(*refs):
        ins, send, recv, token = refs[:nw], refs[nw + 1], refs[nw + 2], refs[2 * nw + 3]
        for out, _ in _gather_copies(ins, send, recv):
            out.start()
        token[...] = jnp.zeros(token.shape, F32)

    res = pl.pallas_call(
        body, name="weights_gather_start",
        in_specs=[HBM_SPEC] * nw + [ANY],
        out_specs=[SEM_SPEC, SEM_SPEC] + [HBM_SPEC] * nw + [pl.BlockSpec(memory_space=pltpu.VMEM)],
        out_shape=[pltpu.SemaphoreType.DMA((3 * nw,)), pltpu.SemaphoreType.DMA((3 * nw,))]
        + [pltpu.HBM(b.shape, b.dtype) for b in bufs] + [jax.ShapeDtypeStruct((8, LANES), F32)],
        input_output_aliases={w: 2 + w for w in range(nw)},
        compiler_params=pltpu.CompilerParams(has_side_effects=SPLIT_EFFECT),
    )(*[_in_hbm(b) for b in bufs], after)
    return res[0], res[1], list(res[2:2 + nw]), res[2 + nw]


def _gather_wait_call(bufs, send, recv, after):
    nw = len(bufs)

    def body(*refs):
        ins, send, recv = refs[:nw], refs[nw], refs[nw + 1]
        for out, back in _gather_copies(ins, send, recv):
            out.wait_send()
            back.wait_recv()

    return pl.pallas_call(
        body, name="weights_gather_wait",
        in_specs=[HBM_SPEC] * nw + [SEM_SPEC, SEM_SPEC, ANY],
        out_specs=[HBM_SPEC] * nw,
        out_shape=[pltpu.HBM(b.shape, b.dtype) for b in bufs],
        input_output_aliases={w: w for w in range(nw)},
        compiler_params=pltpu.CompilerParams(has_side_effects=SPLIT_EFFECT),
    )(*bufs, send, recv, after)


def _scatter_copies(srcs, lands, send, recv, wholes):
    x, y, c = _coords()
    me = 4 * x + 2 * y + c
    cps = []
    for w, (src, land) in enumerate(zip(srcs, lands)):
        for r in range(1, N_DEV):
            px, py, pc = ((1 - x) if r & 4 else x, (1 - y) if r & 2 else y, (1 - c) if r & 1 else c)
            sems = (send.at[(N_DEV - 1) * w + r - 1], recv.at[(N_DEV - 1) * w + r - 1], (px, py, pc))
            piece = src if wholes[w] else src.at[2 * px + py, pc]
            cps.append((_remote(piece, land.at[me], *sems), _remote(piece, land.at[4 * px + 2 * py + pc], *sems)))
    return cps


def _scatter_start_call(srcs, lands, wholes, name):
    nw = len(srcs)
    ncp = (N_DEV - 1) * nw

    def body(*refs):
        ins, lnd, send, recv, token = refs[:nw], refs[nw:2 * nw], refs[2 * nw], refs[2 * nw + 1], refs[4 * nw + 2]
        for out, _ in _scatter_copies(ins, lnd, send, recv, wholes):
            out.start()
        token[...] = jnp.zeros(token.shape, F32)

    res = pl.pallas_call(
        body, name=name,
        in_specs=[HBM_SPEC] * (2 * nw),
        out_specs=[SEM_SPEC, SEM_SPEC] + [HBM_SPEC] * (2 * nw) + [pl.BlockSpec(memory_space=pltpu.VMEM)],
        out_shape=[pltpu.SemaphoreType.DMA((ncp,)), pltpu.SemaphoreType.DMA((ncp,))]
        + [pltpu.HBM(b.shape, b.dtype) for b in list(srcs) + list(lands)] + [jax.ShapeDtypeStruct((8, LANES), F32)],
        input_output_aliases={i: 2 + i for i in range(2 * nw)},
        compiler_params=pltpu.CompilerParams(has_side_effects=SPLIT_EFFECT),
    )(*[_in_hbm(b) for b in list(srcs) + list(lands)])
    return res[0], res[1], list(res[2:2 + nw]), list(res[2 + nw:2 + 2 * nw]), res[2 + 2 * nw]


def _scatter_wait_call(srcs, lands, send, recv, after, wholes, name):
    nw = len(srcs)

    def body(*refs):
        ins, lnd, send, recv = refs[:nw], refs[nw:2 * nw], refs[2 * nw], refs[2 * nw + 1]
        for out, back in _scatter_copies(ins, lnd, send, recv, wholes):
            out.wait_send()
            back.wait_recv()

    res = pl.pallas_call(
        body, name=name,
        in_specs=[HBM_SPEC] * (2 * nw) + [SEM_SPEC, SEM_SPEC, ANY],
        out_specs=[HBM_SPEC] * (2 * nw),
        out_shape=[pltpu.HBM(b.shape, b.dtype) for b in list(srcs) + list(lands)],
        input_output_aliases={i: i for i in range(2 * nw)},
        compiler_params=pltpu.CompilerParams(has_side_effects=SPLIT_EFFECT),
    )(*srcs, *lands, send, recv, after)
    return list(res[nw:])


def _reduce_call(own, lands, idx, nch, name, dep=None):
    nw = len(own)
    deps = [] if dep is None else [dep]

    def body(idx_ref, *refs):
        refs = refs[:2 * nw] + refs[2 * nw + len(deps):]
        for w in range(nw):
            tot = refs[w][...]
            for r in range(1, N_DEV):
                tot = tot + refs[nw + w][idx_ref[1 + r]].astype(F32)
            refs[2 * nw + w][...] = tot

    in_specs, out_specs, out_shape = [], [], []
    for s in own:
        in_specs.append(pl.BlockSpec((None, None, s.shape[2] // nch, s.shape[3]),
                                     lambda i, idx_ref: (idx_ref[0], idx_ref[1], i, 0)))
    for s in own:
        in_specs.append(pl.BlockSpec((N_DEV, s.shape[2] // nch, s.shape[3]), lambda i, idx_ref: (0, i, 0)))
    for s in own:
        out_specs.append(pl.BlockSpec((None, s.shape[2] // nch, s.shape[3]), lambda i, idx_ref: (idx_ref[1], i, 0)))
        out_shape.append(jax.ShapeDtypeStruct((2,) + s.shape[2:], F32))
    return pl.pallas_call(
        body, name=name,
        grid_spec=pltpu.PrefetchScalarGridSpec(num_scalar_prefetch=1, grid=(nch,),
                                               in_specs=in_specs + [ANY] * len(deps), out_specs=out_specs),
        out_shape=out_shape,
        compiler_params=_params(("arbitrary",)),
    )(idx, *own, *lands, *deps)


def _pair_allgather_call(halves, name):
    nw = len(halves)

    def body(*refs):
        outs = refs[nw:2 * nw]
        send, recv = refs[2 * nw:]
        x, y, c = _coords()
        _sibling_handshake()
        cps = []
        for w in range(nw):
            cp = _remote(outs[w].at[c], outs[w].at[c], send.at[w], recv.at[w], (x, y, 1 - c))
            cp.start()
            cps.append(cp)
        for w in range(nw):
            theirs = outs[w].at[1 - c]
            _remote(theirs, theirs, send.at[w], recv.at[w], (x, y, 1 - c)).wait_recv()
        for cp in cps:
            cp.wait_send()

    outs = pl.pallas_call(
        body, name=name,
        in_specs=[ANY] * nw, out_specs=[ANY] * nw,
        out_shape=[jax.ShapeDtypeStruct(h.shape, h.dtype) for h in halves],
        input_output_aliases={w: w for w in range(nw)},
        scratch_shapes=[pltpu.SemaphoreType.DMA((nw,))] * 2,
        compiler_params=pltpu.CompilerParams(collective_id=PAIR_ALLGATHER_ID),
    )(*halves)
    return [o.reshape(2 * h.shape[1], h.shape[2]) for o, h in zip(outs, halves)]


def _adamw(w, g, m, v):
    m = ADAM_B1 * m + (1.0 - ADAM_B1) * g
    v = ADAM_B2 * v + (1.0 - ADAM_B2) * (g * g)
    m_hat = m / (1.0 - ADAM_B1 ** ADAM_STEP)
    v_hat = v / (1.0 - ADAM_B2 ** ADAM_STEP)
    delta = -ADAM_LR * (m_hat / (jnp.sqrt(v_hat) + ADAM_EPS) + ADAM_WD * w)
    return delta, m, v


def _adamw_call(ws, gs, ms, vs, nch, name):
    nw = len(ws)

    def body(*refs):
        for w in range(nw):
            g = refs[nw + w][...]
            delta, m, v = _adamw(refs[w][...], g, refs[2 * nw + w][...], refs[3 * nw + w][...])
            refs[4 * nw + w][...] = g
            refs[5 * nw + w][...] = delta
            refs[6 * nw + w][...] = m
            refs[7 * nw + w][...] = v

    specs = [pl.BlockSpec((a.shape[0] // nch, a.shape[1]), lambda i: (i, 0)) for a in ws]
    res = pl.pallas_call(
        body, name=name,
        grid=(nch,),
        in_specs=specs * 4, out_specs=specs * 4,
        out_shape=[jax.ShapeDtypeStruct(a.shape, F32) for a in ws] * 4,
        compiler_params=_params(("arbitrary",)),
    )(*ws, *gs, *ms, *vs)
    return res[:nw], res[nw:2 * nw], res[2 * nw:3 * nw], res[3 * nw:]


def _small_call(gathered, own, me_idx, w, m, v):
    def fold(row):
        tot = row[:, 0:LANES] + row[:, LANES:2 * LANES] + row[:, 2 * LANES:3 * LANES] + row[:, 3 * LANES:4 * LANES]
        return tot + pltpu.roll(tot, HEAD_DIM, axis=1)

    def body(me_ref, gh_ref, gp_ref, oh_ref, op_ref, wh, wp, mh, mp, vh, vp, *outs):
        me = me_ref[0]

        def total(ga_ref, own_ref):
            term = lambda i: jnp.where(me == i, own_ref[...], ga_ref[i]).astype(F32)
            tot = term(0)
            for i in range(1, N_DEV):
                tot = tot + term(i)
            return tot

        g_head, g_pool = total(gh_ref, oh_ref), total(gp_ref, op_ref)
        unfolded = g_head[4:5, :]
        folded = jnp.concatenate([fold(unfolded[:, :ATTN_WIDTH]), fold(unfolded[:, ATTN_WIDTH:]),
                                  jnp.zeros((1, 1024 - 2 * LANES), F32)], axis=-1)
        row = lax.broadcasted_iota(jnp.int32, g_head.shape, 0)
        g_head = jnp.where(row == 3, folded, g_head)
        for k, (g, w_ref, m_ref, v_ref) in enumerate(((g_head, wh, mh, vh), (g_pool, wp, mp, vp))):
            delta, mm, vv = _adamw(w_ref[...], g, m_ref[...], v_ref[...])
            for out, val in zip(outs[k::2], (g, delta, mm, vv)):
                out[...] = val

    vmem = pl.BlockSpec(memory_space=pltpu.VMEM)
    res = pl.pallas_call(
        body, name="adamw_small",
        in_specs=[pl.BlockSpec(memory_space=pltpu.SMEM)] + [vmem] * 10,
        out_shape=[jax.ShapeDtypeStruct(a.shape, F32) for a in w] * 4,
        compiler_params=_params(),
    )(me_idx, *gathered, *own, *w, *m, *v)
    return [(res[2 * k], res[2 * k + 1]) for k in range(4)]


def _pack_small(p, folded=True, loss=None):
    z = lambda n: jnp.zeros((n,), F32)
    rows = [p["mix_norm_g"], p["mlp_norm_g"],
            jnp.concatenate([p["pool_scale"], p["rel_bias"].reshape(-1), z(1024 - POOL_WIDTH - N_BUCKETS * N_HEADS)])]
    if folded:
        rows += [jnp.concatenate([p["q_norm_g"], z(LANES - HEAD_DIM), p["k_norm_g"], z(1024 - LANES - HEAD_DIM)]), z(1024)]
    else:
        rows += [z(1024), jnp.concatenate([p["q_norm_g"], p["k_norm_g"]])]
    rows += [z(1024) if loss is None else jnp.concatenate([loss.reshape(1), z(1023)])]
    return jnp.stack(rows + [z(1024)] * 2), p["pool_w"].reshape(-1, 1024)


def _unpack_small(head, pool):
    return dict(
        mix_norm_g=head[0], mlp_norm_g=head[1], pool_scale=head[2, :POOL_WIDTH],
        rel_bias=head[2, POOL_WIDTH:POOL_WIDTH + N_BUCKETS * N_HEADS].reshape(N_BUCKETS, N_HEADS),
        q_norm_g=head[3, :HEAD_DIM], k_norm_g=head[3, LANES:LANES + HEAD_DIM],
        pool_w=pool.reshape(len(POOL_WINDOWS), LANES, LANES))


_WEIGHT_ORDER = ("mix_norm_g", "w_in", "pool_w", "pool_scale", "q_norm_g", "k_norm_g", "rel_bias", "w_out",
                 "mlp_norm_g", "w_up", "w_down")
_BIG = ("w_in", "w_out", "w_up", "w_down")


def kernel(x, mix_norm_g, w_in, pool_w, pool_scale, q_norm_g, k_norm_g, rel_bias, w_out, mlp_norm_g, w_up, w_down, loss_target, m_mix_norm_g, m_w_in, m_pool_w, m_pool_scale, m_q_norm_g, m_k_norm_g, m_rel_bias, m_w_out, m_mlp_norm_g, m_w_up, m_w_down, v_mix_norm_g, v_w_in, v_pool_w, v_pool_scale, v_q_norm_g, v_k_norm_g, v_rel_bias, v_w_out, v_mlp_norm_g, v_w_up, v_w_down):
    w = dict(mix_norm_g=mix_norm_g, w_in=w_in, pool_w=pool_w, pool_scale=pool_scale, q_norm_g=q_norm_g,
             k_norm_g=k_norm_g, rel_bias=rel_bias, w_out=w_out, mlp_norm_g=mlp_norm_g, w_up=w_up, w_down=w_down)
    m = dict(mix_norm_g=m_mix_norm_g, w_in=m_w_in, pool_w=m_pool_w, pool_scale=m_pool_scale, q_norm_g=m_q_norm_g,
             k_norm_g=m_k_norm_g, rel_bias=m_rel_bias, w_out=m_w_out, mlp_norm_g=m_mlp_norm_g, w_up=m_w_up, w_down=m_w_down)
    v = dict(mix_norm_g=v_mix_norm_g, w_in=v_w_in, pool_w=v_pool_w, pool_scale=v_pool_scale, q_norm_g=v_q_norm_g,
             k_norm_g=v_k_norm_g, rel_bias=v_rel_bias, w_out=v_w_out, mlp_norm_g=v_mlp_norm_g, w_up=v_w_up, w_down=v_w_down)
    xc, yc, cc = _coords()

    c_idx = jnp.reshape(cc, (1,)).astype(jnp.int32)
    chip_idx = jnp.reshape(2 * xc + yc, (1,)).astype(jnp.int32)
    me = 4 * xc + 2 * yc + cc
    whole = lambda t: t.reshape(t.shape[0], t.shape[1] * t.shape[2], t.shape[3])

    placed = [_halves(p) for p in _place_shards_call([w[n] for n in _BIG], chip_idx, nch=4)]
    win_f, bias = _allgather_call(placed[:1], from_chips=True, name="weights_allgather_in",
                                  meanwhile=_bias_table_work(rel_bias))
    wsend, wrecv, in_flight, started = _gather_start_call(placed[1:], win_f)

    def mlp_weights(after):
        landed = _gather_wait_call(in_flight, wsend, wrecv, after)
        wout_f, wup_f, wdown_f = _allgather_call(landed, from_chips=False, name="weights_pair_forward")
        return whole(wout_f).reshape(-1, wout_f.shape[-1]), whole(wup_f), whole(wdown_f)

    split = []

    def on_mlp_grads(*wire_grads):
        srcs = [_halves(g) for g in wire_grads]
        lands = [lax.empty((N_DEV,) + s.shape[2:], s.dtype) for s in srcs]
        split.extend(_scatter_start_call(srcs, lands, [False] * len(srcs), "grads_scatter_start"))
        return split[4]

    loss_part, dx, big_grads, small_grads = _local_grads(
        x[0], loss_target[0], mix_norm_g + started[0, 0], whole(win_f), pool_w, pool_scale, q_norm_g, k_norm_g, bias,
        mlp_norm_g, mlp_weights, on_mlp_grads)
    g_in, g_out, g_up, g_down = big_grads
    gsend, grecv, srcs_thru, lands_thru, _ = split
    lands_mlp = _scatter_wait_call(srcs_thru, lands_thru, gsend, grecv, g_in[1], [False] * 3, "grads_scatter_wait")

    head_own, pool_own = _pack_small(small_grads, folded=False, loss=loss_part)
    small_own = (head_own, pool_own.astype(WIRE_DTYPE))
    last_srcs = [_halves(g_in[1]), *small_own]
    last_lands = [lax.empty((N_DEV,) + last_srcs[0].shape[2:], WIRE_DTYPE)]
    last_lands += [lax.empty((N_DEV,) + a.shape, a.dtype) for a in small_own]
    lsend, lrecv, last_srcs, last_lands, last_started = _scatter_start_call(
        last_srcs, last_lands, [False, True, True], "grads_scatter_start_last")
    idx = jnp.concatenate([chip_idx, c_idx] + [jnp.reshape(jnp.bitwise_xor(me, r), (1,)) for r in range(1, N_DEV)])
    idx = idx.astype(jnp.int32)
    mlp = _BIG[1:]

    def update(names, own32, lands, tag, dep=None):
        halves = _reduce_call([_halves(g) for g in own32], lands, idx, 4, "grads_reduce_" + tag, dep)
        reduced = _pair_allgather_call(list(halves), "grads_pair_allgather_" + tag)
        return _adamw_call([w[n] for n in names], reduced, [m[n] for n in names], [v[n] for n in names], 8, "adamw_" + tag)

    out_mlp = update(mlp, [g_out[0], g_up[0], g_down[0]], lands_mlp, "mlp", last_started)
    land_in, *small_all = _scatter_wait_call(last_srcs, last_lands, lsend, lrecv, out_mlp[3][-1], [False, True, True],
                                             "grads_scatter_wait_last")
    out_in = update(_BIG[:1], [g_in[0]], [land_in], "in")
    g_pack, d_pack, m_pack, v_pack = _small_call(
        small_all, small_own, jnp.reshape(me, (1,)).astype(jnp.int32), _pack_small(w), _pack_small(m), _pack_small(v))

    grads, deltas, new_m, new_v = (_unpack_small(*a) for a in (g_pack, d_pack, m_pack, v_pack))
    for k, res in enumerate((grads, deltas, new_m, new_v)):
        res[_BIG[0]] = out_in[k][0]
        for i, n in enumerate(mlp):
            res[n] = out_mlp[k][i]
    loss = g_pack[0][LOSS_ROW, 0]
    return (loss, dx[None], *[grads[n] for n in _WEIGHT_ORDER], *[deltas[n] for n in _WEIGHT_ORDER],
            *[new_m[n] for n in _WEIGHT_ORDER], *[new_v[n] for n in _WEIGHT_ORDER])
```

```python
import math

import jax
import jax.numpy as jnp
import numpy as np
from jax import lax
from jax.experimental import pallas as pl
from jax.experimental.pallas import tpu as pltpu

F32 = jnp.float32
MXU_DTYPE = jnp.bfloat16
WIRE_DTYPE = jnp.bfloat16

NORM_EPS = 1e-6
NEG_INF = -1e30
LANES = 128
HEAD_DIM = 64
N_HEADS = 8
POOL_WIDTH = 512
ATTN_WIDTH = 512
POOL_WINDOWS = (2, 4, 8, 16)
POOL_HALO = 16
DILATED_PATTERNS = ((128, 1), (512, 4), (2048, 16))
ATT_BLOCK = 128
ATT_SUPER = ATT_BLOCK * max(dl for _, dl in DILATED_PATTERNS)
ATT_UNITS = ATT_SUPER // ATT_BLOCK
N_BUCKETS = 32
NO_BUCKET = -1
MAX_DISTANCE = 2048
N_CHIPS = 4
N_DEV = 8
ADAM_LR, ADAM_B1, ADAM_B2, ADAM_EPS, ADAM_WD, ADAM_STEP = 0.001, 0.9, 0.999, 1e-08, 0.01, 10
VMEM_LIMIT = 56 * 1024 * 1024
MESH = pl.DeviceIdType.MESH
ANY = pl.BlockSpec(memory_space=pl.ANY)

LOSS_ROW = 5


def _mm(a, b):
    return jnp.dot(a, b, preferred_element_type=F32)


def _mm_nt(a, b):
    return lax.dot_general(a, b, (((1,), (1,)), ((), ())), preferred_element_type=F32)


def _mm_tn(a, b):
    return lax.dot_general(a, b, (((0,), (0,)), ((), ())), preferred_element_type=F32)


def _params(sem=None, **kw):
    if sem is not None:
        kw["dimension_semantics"] = sem
    return pltpu.CompilerParams(vmem_limit_bytes=VMEM_LIMIT, **kw)


def _low_half():
    return lax.broadcasted_iota(jnp.int32, (1, LANES), 1) < HEAD_DIM


def _head_sum_bcast(y):
    lo = _low_half()
    outs = []
    for j in range(y.shape[1] // LANES):
        c = y[:, j * LANES:(j + 1) * LANES]
        s_lo = jnp.sum(jnp.where(lo, c, 0.0), axis=-1, keepdims=True)
        s_hi = jnp.sum(jnp.where(lo, 0.0, c), axis=-1, keepdims=True)
        outs.append(jnp.where(lo, s_lo, s_hi))
    return jnp.concatenate(outs, axis=-1)


def _rms_bwd(dn, hn, r):
    return r * (dn - hn * jnp.mean(dn * hn, axis=-1, keepdims=True))


def _t5_bucket_np(dist):
    max_exact = N_BUCKETS // 2
    d_f = np.maximum(dist, 1).astype(np.float32)
    ratio = (np.log(d_f / np.float32(max_exact)) / np.float32(math.log(MAX_DISTANCE / max_exact))).astype(np.float32)
    large = max_exact + (ratio * np.float32(N_BUCKETS - max_exact)).astype(np.int32)
    large = np.minimum(large, N_BUCKETS - 1)
    return np.where(dist < max_exact, dist, large).astype(np.int32)


def _window_offsets(dl):
    if dl == 1:
        return _by4_positions(ATT_BLOCK), _by4_positions(2 * ATT_BLOCK)
    return np.arange(ATT_BLOCK), np.arange(2 * ATT_BLOCK)


def _bucket_tables():
    tables = []
    for _, dl in DILATED_PATTERNS:
        qq, kk = _window_offsets(dl)
        dist = qq[:, None] + ATT_BLOCK - kk[None, :]
        bucket = _t5_bucket_np(np.clip(dist, 0, ATT_BLOCK) * dl)
        tables.append(np.where((dist >= 0) & (dist <= ATT_BLOCK), bucket, NO_BUCKET))
    return np.stack(tables).astype(np.int32)


def _previous_block_keys():
    return np.stack([np.broadcast_to(_window_offsets(dl)[1][None, :] < ATT_BLOCK, (ATT_BLOCK, 2 * ATT_BLOCK))
                     for _, dl in DILATED_PATTERNS])


def _f1_call(x, g1, win, poolw, pscale, qg, kg, tm):
    s, d = x.shape
    nblk = s // tm

    def body(x_ref, g1_ref, win_ref, pw_ref, ps_ref, qg_ref, kg_ref,
             a_ref, pooled_ref, ypool_ref, q32_ref, k32_ref, qn_ref, kn_ref, v_ref, ubuf):
        i = pl.program_id(0)
        xv = x_ref[...]
        r = lax.rsqrt(jnp.mean(xv * xv, axis=-1, keepdims=True) + NORM_EPS)
        a = ((xv * r) * g1_ref[...]).astype(MXU_DTYPE)
        a_ref[...] = a
        u = _mm(a, win_ref[0])
        q = _mm(a, win_ref[1])
        k = _mm(a, win_ref[2])
        v_ref[...] = _mm(a, win_ref[3])
        q32_ref[...] = q
        k32_ref[...] = k
        rq = lax.rsqrt(_head_sum_bcast(q * q) * (1.0 / HEAD_DIM) + NORM_EPS)
        qn_ref[...] = ((q * rq) * qg_ref[...]) * (HEAD_DIM ** -0.5)
        rk = lax.rsqrt(_head_sum_bcast(k * k) * (1.0 / HEAD_DIM) + NORM_EPS)
        kn_ref[...] = (k * rk) * kg_ref[...]

        ubuf[0:POOL_HALO, :] = jnp.where(i > 0, ubuf[tm:tm + POOL_HALO, :], 0.0)
        ubuf[POOL_HALO:POOL_HALO + tm, :] = u
        t = i * tm + lax.broadcasted_iota(jnp.int32, (tm, 1), 0)
        for g, w in enumerate(POOL_WINDOWS):
            ls = slice(g * LANES, (g + 1) * LANES)
            ug = u[:, ls]
            acc = ug
            for sh in range(1, w):
                acc = acc + ubuf[POOL_HALO - sh:POOL_HALO - sh + tm, ls]
            cnt = jnp.minimum(t + 1, w).astype(F32)
            pooled = (acc / cnt - ug).astype(MXU_DTYPE)
            pooled_ref[:, ls] = pooled
            ypool_ref[:, ls] = (_mm(pooled, pw_ref[g]) * ps_ref[:, ls]).astype(MXU_DTYPE)

    tok = lambda w: pl.BlockSpec((tm, w), lambda i: (i, 0))
    full = lambda shp: pl.BlockSpec(shp, lambda i: (0,) * len(shp))
    return pl.pallas_call(
        body, name="fwd_inproj",
        grid=(nblk,),
        in_specs=[tok(d), full((1, d)), full(win.shape), full(poolw.shape), full((1, POOL_WIDTH)),
                  full((1, ATTN_WIDTH)), full((1, ATTN_WIDTH))],
        out_specs=[tok(d), tok(POOL_WIDTH), tok(POOL_WIDTH), tok(ATTN_WIDTH), tok(ATTN_WIDTH),
                   tok(ATTN_WIDTH), tok(ATTN_WIDTH), tok(ATTN_WIDTH)],
        out_shape=[jax.ShapeDtypeStruct((s, d), MXU_DTYPE),
                   jax.ShapeDtypeStruct((s, POOL_WIDTH), MXU_DTYPE),
                   jax.ShapeDtypeStruct((s, POOL_WIDTH), MXU_DTYPE),
                   jax.ShapeDtypeStruct((s, ATTN_WIDTH), F32),
                   jax.ShapeDtypeStruct((s, ATTN_WIDTH), F32),
                   jax.ShapeDtypeStruct((s, ATTN_WIDTH), F32),
                   jax.ShapeDtypeStruct((s, ATTN_WIDTH), F32),
                   jax.ShapeDtypeStruct((s, ATTN_WIDTH), F32)],
        scratch_shapes=[pltpu.VMEM((tm + POOL_HALO, POOL_WIDTH), F32)],
        compiler_params=_params(("arbitrary",)),
    )(x, g1, win, poolw, pscale, qg, kg)


DEINT = 4
assert [dl for _, dl in DILATED_PATTERNS] == [1, DEINT, DEINT * DEINT]


def _by4_positions(n):
    pos = np.arange(n)
    return DEINT * (pos % (n // DEINT)) + pos // (n // DEINT)


def _masked_bias(b_ref, p, n):
    return b_ref[p, jnp.minimum(n, 1)].reshape(2 * ATT_BLOCK, 2 * ATT_BLOCK)


def _unit_rows(u, dl):
    assert isinstance(u, int)
    sq, sk = ATT_SUPER // DEINT, 2 * ATT_SUPER // DEINT
    if dl == 1:
        n = ATT_BLOCK // DEINT
        return (u, [pl.ds(r * sq + n * u, n) for r in range(DEINT)],
                [pl.ds(r * sk + sk // 2 + n * (u - 1), 2 * n) for r in range(DEINT)])
    if dl == DEINT:
        r, b = u % DEINT, u // DEINT
        return (b, [pl.ds(r * sq + ATT_BLOCK * b, ATT_BLOCK)],
                [pl.ds(r * sk + sk // 2 + ATT_BLOCK * (b - 1), 2 * ATT_BLOCK)])
    r, a = u % DEINT, u // DEINT
    return 0, [pl.ds(r * sq + a, ATT_BLOCK, stride=DEINT)], [pl.ds(r * sk + a, 2 * ATT_BLOCK, stride=DEINT)]


def _take(ref, runs):
    parts = [ref[run, :] for run in runs]
    return parts[0] if len(parts) == 1 else jnp.concatenate(parts, axis=0)


def _put(ref, runs, value, add=False):
    n = value.shape[0] // len(runs)
    for i, run in enumerate(runs):
        part = value[i * n:(i + 1) * n]
        ref[run, :] = ref[run, :] + part if add else part


def _deinterleave(dst, src, n):
    seg = n // DEINT
    for r in range(DEINT):
        dst[r * seg:(r + 1) * seg, :] = src[pl.ds(r, seg, stride=DEINT), :]


def _deinterleave_pair(dst, prev, cur):
    seg = prev.shape[0] // DEINT
    for r in range(DEINT):
        dst[2 * r * seg:(2 * r + 1) * seg, :] = prev[pl.ds(r, seg, stride=DEINT), :]
        dst[(2 * r + 1) * seg:(2 * r + 2) * seg, :] = cur[pl.ds(r, seg, stride=DEINT), :]


def _interleave(dst, src, n, offset=0):
    seg = n // DEINT
    stride = src.shape[0] // DEINT
    for r in range(DEINT):
        dst[pl.ds(r, seg, stride=DEINT), :] = src[r * stride + offset:r * stride + offset + seg, :]


def _attn_fwd_call(qn, kn, v, bias):
    s, w = qn.shape
    nsb = s // ATT_SUPER
    npair = w // LANES

    def body(q_ref, kc_ref, kp_ref, vc_ref, vp_ref, b_ref, o_ref, lse_ref, qf, kf, vf, acc_s, m_s, l_s):
        sb = pl.program_id(1)
        _deinterleave(qf, q_ref, ATT_SUPER)
        _deinterleave_pair(kf, kp_ref, kc_ref)
        _deinterleave_pair(vf, vp_ref, vc_ref)
        lo = _low_half()
        for p, (_, dl) in enumerate(DILATED_PATTERNS):
            def unit(u, carry, p=p, dl=dl):
                b, rows_q, rows_k = _unit_rows(u, dl)
                qp = _take(qf, rows_q).astype(MXU_DTYPE)
                kcat = _take(kf, rows_k).astype(MXU_DTYPE)
                vcat = _take(vf, rows_k).astype(MXU_DTYPE)
                zero = jnp.zeros_like(qp)
                q2 = jnp.concatenate([jnp.where(lo, qp, zero), jnp.where(lo, zero, qp)], axis=0)
                sc = _mm_nt(q2, kcat) + _masked_bias(b_ref, p, sb * (ATT_UNITS // dl) + b)
                m2 = jnp.max(sc, axis=-1, keepdims=True)
                pr = jnp.exp(sc - m2)
                l2 = jnp.sum(pr, axis=-1, keepdims=True)
                acc2 = _mm(pr.astype(MXU_DTYPE), vcat)
                acc = jnp.where(lo, acc2[:ATT_BLOCK], acc2[ATT_BLOCK:])
                m = jnp.where(lo, m2[:ATT_BLOCK], m2[ATT_BLOCK:])
                l = jnp.where(lo, l2[:ATT_BLOCK], l2[ATT_BLOCK:])
                if p == 0:
                    _put(acc_s, rows_q, acc)
                    _put(m_s, rows_q, m)
                    _put(l_s, rows_q, l)
                else:
                    m_old = _take(m_s, rows_q)
                    m_new = jnp.maximum(m_old, m)
                    a_old = jnp.exp(m_old - m_new)
                    a_new = jnp.exp(m - m_new)
                    _put(acc_s, rows_q, a_old * _take(acc_s, rows_q) + a_new * acc)
                    _put(l_s, rows_q, a_old * _take(l_s, rows_q) + a_new * l)
                    _put(m_s, rows_q, m_new)
                return carry

            for u in range(ATT_UNITS):
                unit(u, None)
        l = l_s[...]
        acc_s[...] = acc_s[...] / l
        m_s[...] = m_s[...] + jnp.log(l)
        _interleave(o_ref, acc_s, ATT_SUPER)
        _interleave(lse_ref, m_s, ATT_SUPER)

    cur = pl.BlockSpec((ATT_SUPER, LANES), lambda j, t: (t, j))
    prev = pl.BlockSpec((ATT_SUPER, LANES), lambda j, t: (jnp.maximum(t - 1, 0), j))
    bspec = pl.BlockSpec((len(DILATED_PATTERNS), 2, 2, ATT_BLOCK, 2 * ATT_BLOCK), lambda j, t: (0, 0, j, 0, 0))
    return pl.pallas_call(
        body, name="attn_fwd",
        grid=(npair, nsb),
        in_specs=[cur, cur, prev, cur, prev, bspec],
        out_specs=[cur, cur],
        out_shape=[jax.ShapeDtypeStruct((s, w), F32), jax.ShapeDtypeStruct((s, w), F32)],
        scratch_shapes=[pltpu.VMEM((ATT_SUPER, LANES), F32), pltpu.VMEM((2 * ATT_SUPER, LANES), F32),
                        pltpu.VMEM((2 * ATT_SUPER, LANES), F32), pltpu.VMEM((ATT_SUPER, LANES), F32),
                        pltpu.VMEM((ATT_SUPER, LANES), F32), pltpu.VMEM((ATT_SUPER, LANES), F32)],
        compiler_params=_params(("arbitrary", "arbitrary")),
    )(qn, kn, kn, v, v, bias)


def _attn_bwd_call(qn, kn, v, do, lse, delta, bias, dep=None):
    s, w = qn.shape
    nsb = s // ATT_SUPER
    npair = w // LANES
    deps = [] if dep is None else [dep]

    def body(q_ref, kc_ref, kp_ref, vc_ref, vp_ref, do_ref, lse_ref, dlt_ref, b_ref, *rest):
        dq_ref, dk_ref, dv_ref, db_ref, qf, kf, vf, dof, lsef, dltf, dqf, dkf, dvf = rest[len(deps):]
        step = pl.program_id(1)
        sb = nsb - 1 - step
        seg = ATT_SUPER // DEINT
        _deinterleave(qf, q_ref, ATT_SUPER)
        _deinterleave(dof, do_ref, ATT_SUPER)
        _deinterleave_pair(kf, kp_ref, kc_ref)
        _deinterleave_pair(vf, vp_ref, vc_ref)
        _deinterleave(lsef, lse_ref, ATT_SUPER)
        _deinterleave(dltf, dlt_ref, ATT_SUPER)

        db_ref[...] = jnp.where(step > 0, db_ref[...], 0.0)
        for acc in (dkf, dvf):
            for r in range(DEINT):
                this, before = pl.ds((2 * r + 1) * seg, seg), pl.ds(2 * r * seg, seg)
                acc[this, :] = jnp.where(step > 0, acc[before, :], 0.0)
                acc[before, :] = jnp.zeros((seg, LANES), F32)
        lo = _low_half()
        for p, (_, dl) in enumerate(DILATED_PATTERNS):
            def unit(u, carry, p=p, dl=dl):
                b, rows_q, rows_k = _unit_rows(u, dl)
                qp = _take(qf, rows_q).astype(MXU_DTYPE)
                dop = _take(dof, rows_q).astype(MXU_DTYPE)
                kcat = _take(kf, rows_k).astype(MXU_DTYPE)
                vcat = _take(vf, rows_k).astype(MXU_DTYPE)
                lse2 = _take(lsef, rows_q)
                dlt2 = _take(dltf, rows_q)
                zero = jnp.zeros_like(qp)
                q2 = jnp.concatenate([jnp.where(lo, qp, zero), jnp.where(lo, zero, qp)], axis=0)
                do2 = jnp.concatenate([jnp.where(lo, dop, zero), jnp.where(lo, zero, dop)], axis=0)
                lse_c = jnp.concatenate([lse2[:, 0:1], lse2[:, HEAD_DIM:HEAD_DIM + 1]], axis=0)
                dlt_c = jnp.concatenate([dlt2[:, 0:1], dlt2[:, HEAD_DIM:HEAD_DIM + 1]], axis=0)
                sc = _mm_nt(q2, kcat) + _masked_bias(b_ref, p, sb * (ATT_UNITS // dl) + b)
                pr = jnp.exp(sc - lse_c)
                ds = pr * (_mm_nt(do2, vcat) - dlt_c)
                db_ref[p] += ds.reshape(2, ATT_BLOCK, 2 * ATT_BLOCK)
                ds_c = ds.astype(MXU_DTYPE)
                dq2 = _mm(ds_c, kcat)
                dk = _mm_tn(ds_c, q2)
                dv = _mm_tn(pr.astype(MXU_DTYPE), do2)
                dq = jnp.where(lo, dq2[:ATT_BLOCK], dq2[ATT_BLOCK:])
                _put(dqf, rows_q, dq, add=p > 0)
                _put(dkf, rows_k, dk, add=True)
                _put(dvf, rows_k, dv, add=True)
                return carry

            for u in range(ATT_UNITS):
                unit(u, None)
        _interleave(dq_ref, dqf, ATT_SUPER)
        _interleave(dk_ref, dkf, ATT_SUPER, offset=seg)
        _interleave(dv_ref, dvf, ATT_SUPER, offset=seg)

    cur = pl.BlockSpec((ATT_SUPER, LANES), lambda j, t: (nsb - 1 - t, j))
    prev = pl.BlockSpec((ATT_SUPER, LANES), lambda j, t: (jnp.maximum(nsb - 2 - t, 0), j))
    npat = len(DILATED_PATTERNS)
    bspec = pl.BlockSpec((npat, 2, 2, ATT_BLOCK, 2 * ATT_BLOCK), lambda j, t: (0, 0, j, 0, 0))
    dbspec = pl.BlockSpec((npat, 2, ATT_BLOCK, 2 * ATT_BLOCK), lambda j, t: (0, j, 0, 0))
    sup = lambda: pltpu.VMEM((ATT_SUPER, LANES), F32)
    sup2 = lambda: pltpu.VMEM((2 * ATT_SUPER, LANES), F32)
    return pl.pallas_call(
        body, name="attn_bwd",
        grid=(npair, nsb),
        in_specs=[cur, cur, prev, cur, prev, cur, cur, cur, bspec] + [ANY] * len(deps),
        out_specs=[cur, cur, cur, dbspec],
        out_shape=[jax.ShapeDtypeStruct((s, w), F32)] * 3
        + [jax.ShapeDtypeStruct((npat, N_HEADS, ATT_BLOCK, 2 * ATT_BLOCK), F32)],
        scratch_shapes=[sup(), sup2(), sup2(), sup(), sup(), sup(), sup(), sup2(), sup2()],
        compiler_params=_params(("arbitrary", "arbitrary")),
    )(qn, kn, kn, v, v, do, lse, delta, bias, *deps)


def _bias_table_work(rel_bias):
    buckets = jnp.asarray(_bucket_tables())
    prev_keys = jnp.asarray(_previous_block_keys().astype(np.int32))
    npat = buckets.shape[0]

    def body(rb_ref, bk_ref, pk_ref, out_ref):
        for p in range(npat):
            for half in range(2):
                ks = slice(half * ATT_BLOCK, (half + 1) * ATT_BLOCK)
                bk = bk_ref[p, :, ks]
                absent = pk_ref[p, :, ks] != 0
                for h in range(N_HEADS):
                    def pick(b, acc, h=h, bk=bk):
                        return jnp.where(bk == b, rb_ref[b, h], acc)

                    tab = lax.fori_loop(0, N_BUCKETS, pick, jnp.full((ATT_BLOCK, ATT_BLOCK), NEG_INF, F32))
                    out_ref[p, 1, h, :, ks] = tab
                    out_ref[p, 0, h, :, ks] = jnp.where(absent, NEG_INF, tab)

    vmem = pl.BlockSpec(memory_space=pltpu.VMEM)
    return ([rel_bias, buckets, prev_keys], [pl.BlockSpec(memory_space=pltpu.SMEM), vmem, vmem],
            jax.ShapeDtypeStruct((npat, 2, N_HEADS, ATT_BLOCK, 2 * ATT_BLOCK), F32), body)


def _rel_bias_grad_call(dbias, buckets):
    npat, nh = dbias.shape[0], dbias.shape[1]

    def body(db_ref, bk_ref, out_ref):
        lane = lax.broadcasted_iota(jnp.int32, (nh, LANES), 1)
        out = jnp.zeros((nh, LANES), F32)
        for b in range(N_BUCKETS):
            tot = jnp.zeros((nh, 1), F32)
            for p in range(npat):
                hit = jnp.where(bk_ref[p][None] == b, db_ref[p], 0.0)
                tot = tot + jnp.sum(jnp.sum(hit, axis=1), axis=-1, keepdims=True)
            out = jnp.where(lane == b, tot, out)
        out_ref[...] = out

    return pl.pallas_call(
        body, name="rel_bias_grad",
        out_shape=jax.ShapeDtypeStruct((nh, LANES), F32),
        compiler_params=_params(),
    )(dbias, buckets)


def _f2_call(x, tgt, ypool, o, wout, wup, wdown, g2, tm):
    s, d = x.shape
    nblk = s // tm
    nch, _, fch = wup.shape
    dff = nch * fch
    mixw = POOL_WIDTH + ATTN_WIDTH

    def body(x_ref, t_ref, yp_ref, o_ref, g2_ref, wout_hbm, wup_hbm, wdown_hbm,
             mixed_ref, c_ref, ff_ref, dz_ref, dy_ref, dh1_ref, dyp_ref, do_ref, dlt_ref, dg2_ref, loss_ref,
             wout_v, wup_v, wdown_v, rz, wsem):
        i = pl.program_id(0)

        @pl.when(i == 0)
        def _():
            copies = [pltpu.make_async_copy(wout_hbm, wout_v, wsem.at[0])]
            for j in range(nch):
                copies.append(pltpu.make_async_copy(wup_hbm.at[j], wup_v.at[j], wsem.at[1 + 2 * j]))
                copies.append(pltpu.make_async_copy(wdown_hbm.at[j], wdown_v.at[j], wsem.at[2 + 2 * j]))
            for cp in copies:
                cp.start()
            dg2_ref[...] = jnp.zeros(dg2_ref.shape, F32)
            loss_ref[...] = jnp.zeros(loss_ref.shape, F32)
            for cp in copies:
                cp.wait()

        o = o_ref[...]
        mixed = jnp.concatenate([yp_ref[...], o.astype(MXU_DTYPE)], axis=-1)
        mixed_ref[...] = mixed
        h1 = x_ref[...] + _mm(mixed, wout_v[...])
        r2 = lax.rsqrt(jnp.mean(h1 * h1, axis=-1, keepdims=True) + NORM_EPS)
        hn = h1 * r2
        c = (hn * g2_ref[...]).astype(MXU_DTYPE)
        c_ref[...] = c
        y = h1
        for j in range(nch):
            cs = slice(j * fch, (j + 1) * fch)
            z = jnp.maximum(_mm(c, wup_v[j]), 0.0)
            rz[:, cs] = z
            ff = (z * z).astype(MXU_DTYPE)
            ff_ref[:, cs] = ff
            y = y + _mm(ff, wdown_v[j])
        err = y - t_ref[...]
        loss_ref[...] += jnp.sum(err * err) * (0.5 / d)
        dy = err * (1.0 / d)
        dy_c = dy.astype(MXU_DTYPE)
        dy_ref[...] = dy_c
        dc = jnp.zeros((tm, d), F32)
        for j in range(nch):
            cs = slice(j * fch, (j + 1) * fch)
            dz = (_mm_nt(dy_c, wdown_v[j]) * (2.0 * rz[:, cs])).astype(MXU_DTYPE)
            dz_ref[:, cs] = dz
            dc = dc + _mm_nt(dz, wup_v[j])
        dg2_ref[...] += jnp.sum(dc * hn, axis=0, keepdims=True)
        dh1 = dy + _rms_bwd(dc * g2_ref[...], hn, r2)
        dh1_ref[...] = dh1
        dmix = _mm_nt(dh1.astype(MXU_DTYPE), wout_v[...])
        dyp_ref[...] = dmix[:, :POOL_WIDTH]
        do = dmix[:, POOL_WIDTH:]
        do_ref[...] = do
        dlt_ref[...] = _head_sum_bcast(do * o)

    tok = lambda w: pl.BlockSpec((tm, w), lambda i: (i, 0))
    const = lambda shp: pl.BlockSpec(shp, lambda i: (0,) * len(shp))
    return pl.pallas_call(
        body, name="fwd_mlp_bwd_mlp",
        grid=(nblk,),
        in_specs=[tok(d), tok(d), tok(POOL_WIDTH), tok(ATTN_WIDTH), const((1, d)), ANY, ANY, ANY],
        out_specs=[tok(mixw), tok(d), tok(dff), tok(dff), tok(d), tok(d), tok(POOL_WIDTH), tok(ATTN_WIDTH),
                   tok(ATTN_WIDTH), const((1, d)), const((1, LANES))],
        out_shape=[jax.ShapeDtypeStruct((s, mixw), MXU_DTYPE),
                   jax.ShapeDtypeStruct((s, d), MXU_DTYPE),
                   jax.ShapeDtypeStruct((s, dff), MXU_DTYPE),
                   jax.ShapeDtypeStruct((s, dff), MXU_DTYPE),
                   jax.ShapeDtypeStruct((s, d), MXU_DTYPE),
                   jax.ShapeDtypeStruct((s, d), F32),
                   jax.ShapeDtypeStruct((s, POOL_WIDTH), F32),
                   jax.ShapeDtypeStruct((s, ATTN_WIDTH), F32),
                   jax.ShapeDtypeStruct((s, ATTN_WIDTH), F32),
                   jax.ShapeDtypeStruct((1, d), F32),
                   jax.ShapeDtypeStruct((1, LANES), F32)],
        scratch_shapes=[pltpu.VMEM(wout.shape, MXU_DTYPE), pltpu.VMEM(wup.shape, MXU_DTYPE),
                        pltpu.VMEM(wdown.shape, MXU_DTYPE), pltpu.VMEM((tm, dff), F32),
                        pltpu.SemaphoreType.DMA((1 + 2 * nch,))],
        compiler_params=_params(("arbitrary",)),
    )(x, tgt, ypool, o, g2, wout, wup, wdown)


RING_SLOTS = 3


def _bproj_call(dqn, dkn, dv, q32, k32, dypool, pooled, x, dh1, win, poolw, pscale, qg, kg, g1, tm):
    s, d = x.shape
    nblk = s // tm
    ngrp = len(POOL_WINDOWS)
    streams = [dqn, dkn, dv, q32, k32, dypool, pooled, x, dh1]
    ns = len(streams)
    assert nblk >= 2

    def body(*refs):
        hbm = refs[:ns]
        win_hbm, pw_ref, ps_ref, qg_ref, kg_ref, g1_ref = refs[ns:ns + 6]
        dx_ref, dproj_ref, dg1_ref, dqg_ref, dkg_ref, dpw_ref, dps_ref, win_v, ebuf = refs[ns + 6:ns + 15]
        rings, sems = refs[ns + 15:2 * ns + 15], refs[2 * ns + 15]
        step = pl.program_id(0)
        i = nblk - 1 - step

        def fetch(t):
            rows = pl.ds(pl.multiple_of((nblk - 1 - t) * tm, tm), tm)
            return [pltpu.make_async_copy(h.at[rows], ring.at[t % RING_SLOTS], sems.at[k, t % RING_SLOTS])
                    for k, (h, ring) in enumerate(zip(hbm, rings))]

        @pl.when(step == 0)
        def _():
            for t in range(2):
                for cp in fetch(t):
                    cp.start()
            pltpu.sync_copy(win_hbm, win_v)
            dg1_ref[...] = jnp.zeros(dg1_ref.shape, F32)
            dqg_ref[...] = jnp.zeros(dqg_ref.shape, F32)
            dkg_ref[...] = jnp.zeros(dkg_ref.shape, F32)
            dpw_ref[...] = jnp.zeros(dpw_ref.shape, F32)
            dps_ref[...] = jnp.zeros(dps_ref.shape, F32)
            ebuf[tm:tm + POOL_HALO, :] = jnp.zeros((POOL_HALO, POOL_WIDTH), F32)

        @pl.when(step > 0)
        def _():
            ebuf[tm:tm + POOL_HALO, :] = ebuf[0:POOL_HALO, :]

        @pl.when(step + 2 < nblk)
        def _():
            for cp in fetch(step + 2):
                cp.start()

        for cp in fetch(step):
            cp.wait()
        dqn_ref, dkn_ref, dv_ref, q_ref, k_ref, dyp_ref, pooled_ref, x_ref, dh1_ref = (
            ring.at[step % RING_SLOTS] for ring in rings)

        def qk_bwd(dn_sum, raw, gain, scale, dgain_ref):
            rr = lax.rsqrt(_head_sum_bcast(raw * raw) * (1.0 / HEAD_DIM) + NORM_EPS)
            hn = raw * rr
            dgain_ref[...] += jnp.sum(dn_sum * hn, axis=0, keepdims=True) * scale
            dn = dn_sum * (gain * scale)
            return rr * (dn - hn * (_head_sum_bcast(dn * hn) * (1.0 / HEAD_DIM)))

        dq = qk_bwd(dqn_ref[...], q_ref[...], qg_ref[...], HEAD_DIM ** -0.5, dqg_ref)
        dk = qk_bwd(dkn_ref[...], k_ref[...], kg_ref[...], 1.0, dkg_ref)

        t = i * tm + lax.broadcasted_iota(jnp.int32, (tm, 1), 0)
        dpooled = []
        for g, w in enumerate(POOL_WINDOWS):
            ls = slice(g * LANES, (g + 1) * LANES)
            dm = dyp_ref[:, ls]
            pg = pooled_ref[:, ls]
            dps_ref[:, ls] += jnp.sum(dm * _mm(pg, pw_ref[g]), axis=0, keepdims=True)
            dms = (dm * ps_ref[:, ls]).astype(MXU_DTYPE)
            dpw_ref[g] += _mm_tn(pg, dms)
            dpg = _mm_nt(dms, pw_ref[g])
            dpooled.append(dpg)
            ebuf[0:tm, ls] = dpg / jnp.minimum(t + 1, w).astype(F32)
        du = []
        for g, w in enumerate(POOL_WINDOWS):
            ls = slice(g * LANES, (g + 1) * LANES)
            acc = ebuf[0:tm, ls]
            for sh in range(1, w):
                acc = acc + ebuf[sh:sh + tm, ls]
            du.append(acc - dpooled[g])
        parts = [jnp.concatenate(du, axis=-1), dq, dk, dv_ref[...]]
        da = jnp.zeros((tm, d), F32)
        for p, part in enumerate(parts):
            pc = part.astype(MXU_DTYPE)
            dproj_ref[:, p * POOL_WIDTH:(p + 1) * POOL_WIDTH] = pc
            da = da + _mm_nt(pc, win_v[p])
        xv = x_ref[...]
        r = lax.rsqrt(jnp.mean(xv * xv, axis=-1, keepdims=True) + NORM_EPS)
        xn = xv * r
        dg1_ref[...] += jnp.sum(da * xn, axis=0, keepdims=True)
        dx_ref[...] = dh1_ref[...] + _rms_bwd(da * g1_ref[...], xn, r)

    tok = lambda w: pl.BlockSpec((tm, w), lambda t: (nblk - 1 - t, 0))
    const = lambda shp: pl.BlockSpec(shp, lambda t: (0,) * len(shp))
    return pl.pallas_call(
        body, name="bwd_inproj",
        grid=(nblk,),
        in_specs=[ANY] * (ns + 1) + [const(poolw.shape), const((1, POOL_WIDTH)), const((1, ATTN_WIDTH)),
                                     const((1, ATTN_WIDTH)), const((1, d))],
        out_specs=[tok(d), tok(4 * POOL_WIDTH), const((1, d)), const((1, ATTN_WIDTH)), const((1, ATTN_WIDTH)),
                   const((ngrp, LANES, LANES)), const((1, POOL_WIDTH))],
        out_shape=[jax.ShapeDtypeStruct((s, d), F32),
                   jax.ShapeDtypeStruct((s, 4 * POOL_WIDTH), MXU_DTYPE),
                   jax.ShapeDtypeStruct((1, d), F32),
                   jax.ShapeDtypeStruct((1, ATTN_WIDTH), F32),
                   jax.ShapeDtypeStruct((1, ATTN_WIDTH), F32),
                   jax.ShapeDtypeStruct((ngrp, LANES, LANES), F32),
                   jax.ShapeDtypeStruct((1, POOL_WIDTH), F32)],
        scratch_shapes=[pltpu.VMEM(win.shape, MXU_DTYPE), pltpu.VMEM((tm + POOL_HALO, POOL_WIDTH), F32)]
        + [pltpu.VMEM((RING_SLOTS, tm, a.shape[1]), a.dtype) for a in streams]
        + [pltpu.SemaphoreType.DMA((ns, RING_SLOTS))],
        compiler_params=_params(("arbitrary",)),
    )(*streams, win, poolw, pscale, qg, kg, g1)


def _wgrad_call(a, b, bm, bn, bk, out_shape, out_block, out_index, name):
    s, m = a.shape
    _, n = b.shape
    nk = s // bk

    def body(a_ref, b_ref, o_ref, wire_ref):
        k = pl.program_id(2)
        acc = jnp.where(k > 0, o_ref[...], 0.0) + _mm_tn(a_ref[...].astype(MXU_DTYPE), b_ref[...].astype(MXU_DTYPE))
        o_ref[...] = acc
        wire_ref[...] = acc.astype(WIRE_DTYPE)

    return pl.pallas_call(
        body, name=name,
        grid=(m // bm, n // bn, nk),
        in_specs=[pl.BlockSpec((bk, bm), lambda i, j, k: (k, i)), pl.BlockSpec((bk, bn), lambda i, j, k: (k, j))],
        out_specs=[pl.BlockSpec(out_block, out_index)] * 2,
        out_shape=[jax.ShapeDtypeStruct(out_shape, F32), jax.ShapeDtypeStruct(out_shape, WIRE_DTYPE)],
        compiler_params=_params(("arbitrary", "arbitrary", "arbitrary")),
    )(a, b)


def _local_grads(x, tgt, g1, win, poolw, pscale, qg, kg, bias, g2, mlp_weights, on_mlp_grads=None):
    s, d = x.shape
    g1r, g2r = g1.reshape(1, d), g2.reshape(1, d)
    psr = pscale.reshape(1, POOL_WIDTH)
    qgr = jnp.tile(qg, N_HEADS).reshape(1, ATTN_WIDTH)
    kgr = jnp.tile(kg, N_HEADS).reshape(1, ATTN_WIDTH)
    pw_c = poolw.astype(MXU_DTYPE)
    buckets = jnp.asarray(_bucket_tables())
    bk = min(s, 4096)

    a, pooled, ypool, q32, k32, qn, kn, v = _f1_call(x, g1r, win, pw_c, psr, qgr, kgr, tm=1024)
    o, lse = _attn_fwd_call(qn, kn, v, bias)
    wout, wup, wdown = mlp_weights(o)
    mixed, c, ff, dz, dy, dh1, dypool, do, delta, dg2, loss = _f2_call(x, tgt, ypool, o, wout, wup, wdown, g2r, tm=256)
    dff = ff.shape[1]
    g_out = [g.reshape(N_CHIPS, d // N_CHIPS, d)
             for g in _wgrad_call(mixed, dh1, d, d, bk // 4, (d, d), (d, d), lambda i, j, k: (0, 0), "wgrad_out")]
    g_up = _wgrad_call(c, dz, d, dff // N_CHIPS, bk, (N_CHIPS, d, dff // N_CHIPS), (None, d, dff // N_CHIPS),
                       lambda i, j, k: (j, 0, 0), "wgrad_up")
    g_down = _wgrad_call(ff, dy, dff // N_CHIPS, d, bk, (N_CHIPS, dff // N_CHIPS, d), (None, dff // N_CHIPS, d),
                         lambda i, j, k: (i, 0, 0), "wgrad_down")
    dep = None if on_mlp_grads is None else on_mlp_grads(g_out[1], g_up[1], g_down[1])
    dqn, dkn, dv, dbias = _attn_bwd_call(qn, kn, v, do, lse, delta, bias, dep)
    dx, dproj, dg1, dqg, dkg, dpw, dps = _bproj_call(
        dqn, dkn, dv, q32, k32, dypool, pooled, x, dh1, win, pw_c, psr, qgr, kgr, g1r, tm=512)
    nin = dproj.shape[1] // N_CHIPS
    g_in = _wgrad_call(a, dproj, d, nin, bk, (N_CHIPS, d, nin), (None, d, nin), lambda i, j, k: (j, 0, 0), "wgrad_in")
    drb = _rel_bias_grad_call(dbias, buckets)
    small = dict(
        mix_norm_g=dg1.reshape(d), mlp_norm_g=dg2.reshape(d), pool_scale=dps.reshape(POOL_WIDTH),
        q_norm_g=dqg.reshape(ATTN_WIDTH), k_norm_g=dkg.reshape(ATTN_WIDTH),
        rel_bias=drb[:, :N_BUCKETS].T, pool_w=dpw)
    return loss[0, 0], dx, (g_in, g_out, g_up, g_down), small


def _coords():
    return lax.axis_index("x"), lax.axis_index("y"), lax.axis_index("c")


def _other_chips(x, y):
    return [(1 - x, y), (x, 1 - y), (1 - x, 1 - y)]


def _remote(src, dst, send_sem, recv_sem, dev):
    return pltpu.make_async_remote_copy(src_ref=src, dst_ref=dst, send_sem=send_sem, recv_sem=recv_sem,
                                        device_id=dev, device_id_type=MESH)


PAIR_FORWARD_ID = 1
PAIR_ALLGATHER_ID = 2


def _sibling_handshake():
    x, y, c = _coords()
    barrier = pltpu.get_barrier_semaphore()
    pl.semaphore_signal(barrier, inc=1, device_id=(x, y, 1 - c), device_id_type=MESH)
    pl.semaphore_wait(barrier, 1)


def _halves(a):
    return a.reshape(a.shape[:-2] + (2, a.shape[-2] // 2, a.shape[-1]))


def _place_shards_call(shards, chip_idx, nch):
    nw = len(shards)

    def body(chip_ref, *refs):
        for w in range(nw):
            refs[nw + w][...] = refs[w][...].astype(WIRE_DTYPE)

    in_specs = [pl.BlockSpec((s.shape[0] // nch, s.shape[1]), lambda i, chip_ref: (i, 0)) for s in shards]
    out_specs = [pl.BlockSpec((None, s.shape[0] // nch, s.shape[1]), lambda i, chip_ref: (chip_ref[0], i, 0))
                 for s in shards]
    return pl.pallas_call(
        body, name="weights_place",
        grid_spec=pltpu.PrefetchScalarGridSpec(num_scalar_prefetch=1, grid=(nch,),
                                               in_specs=in_specs, out_specs=out_specs),
        out_shape=[jax.ShapeDtypeStruct((N_CHIPS,) + s.shape, WIRE_DTYPE) for s in shards],
        compiler_params=_params(("arbitrary",)),
    )(chip_idx, *shards)


def _allgather_call(placed, from_chips, name, meanwhile=None):
    nw = len(placed)
    ncp = 3 * nw
    extra, extra_specs, extra_shape, extra_body = meanwhile if meanwhile else ([], [], None, None)
    ne = len(extra)

    def body(*refs):
        outs = refs[nw + ne:2 * nw + ne]
        send1, recv1, send2, recv2 = refs[-4:]
        x, y, c = _coords()
        chip = 2 * x + y
        others = _other_chips(x, y)
        first, passed = [], []
        if not from_chips:
            _sibling_handshake()
        if from_chips:
            for w in range(nw):
                for k, (ox, oy) in enumerate(others):
                    mine = outs[w].at[chip, c]
                    cp = _remote(mine, mine, send1.at[3 * w + k], recv1.at[3 * w + k], (ox, oy, c))
                    cp.start()
                    first.append(cp)
        if meanwhile:
            extra_body(*refs[nw:nw + ne], refs[2 * nw + ne])
        for w in range(nw):
            for k, (ox, oy) in enumerate(others):
                piece = outs[w].at[2 * ox + oy, c]
                if from_chips:
                    _remote(piece, piece, send1.at[3 * w + k], recv1.at[3 * w + k], (ox, oy, c)).wait_recv()
                cp = _remote(piece, piece, send2.at[3 * w + k], recv2.at[3 * w + k], (x, y, 1 - c))
                cp.start()
                passed.append(cp)
        for w in range(nw):
            for k, (ox, oy) in enumerate(others):
                piece = outs[w].at[2 * ox + oy, 1 - c]
                _remote(piece, piece, send2.at[3 * w + k], recv2.at[3 * w + k], (x, y, 1 - c)).wait_recv()
        for cp in first + passed:
            cp.wait_send()

    return pl.pallas_call(
        body, name=name,
        in_specs=[ANY] * nw + list(extra_specs),
        out_specs=[ANY] * nw + ([pl.BlockSpec(memory_space=pltpu.VMEM)] if meanwhile else []),
        out_shape=[jax.ShapeDtypeStruct(s.shape, s.dtype) for s in placed] + ([extra_shape] if meanwhile else []),
        input_output_aliases={w: w for w in range(nw)},
        scratch_shapes=[pltpu.SemaphoreType.DMA((ncp,))] * 4,
        compiler_params=_params() if from_chips else _params(collective_id=PAIR_FORWARD_ID),
    )(*placed, *extra)


HBM_SPEC = pl.BlockSpec(memory_space=pltpu.HBM)
SEM_SPEC = pl.BlockSpec(memory_space=pltpu.SEMAPHORE)
SPLIT_EFFECT = pltpu.SideEffectType.DATAFLOW_SIDE_EFFECTING


def _in_hbm(a):
    return pltpu.with_memory_space_constraint(a, pltpu.HBM)


def _gather_copies(bufs, send, recv):
    x, y, c = _coords()
    chip = 2 * x + y
    cps = []
    for w, buf in enumerate(bufs):
        for k, (ox, oy) in enumerate(_other_chips(x, y)):
            mine, theirs = buf.at[chip, c], buf.at[2 * ox + oy, c]
            sems = (send.at[3 * w + k], recv.at[3 * w + k], (ox, oy, c))
            cps.append((_remote(mine, mine, *sems), _remote(theirs, theirs, *sems)))
    return cps


def _gather_start_call(bufs, after):
    nw = len(bufs)

    def body(*refs):
        ins, send, recv, token = refs[:nw], refs[nw + 1], refs[nw + 2], refs[2 * nw + 3]
        for out, _ in _gather_copies(ins, send, recv):
            out.start()
        token[...] = jnp.zeros(token.shape, F32)

    res = pl.pallas_call(
        body, name="weights_gather_start",
        in_specs=[HBM_SPEC] * nw + [ANY],
        out_specs=[SEM_SPEC, SEM_SPEC] + [HBM_SPEC] * nw + [pl.BlockSpec(memory_space=pltpu.VMEM)],
        out_shape=[pltpu.SemaphoreType.DMA((3 * nw,)), pltpu.SemaphoreType.DMA((3 * nw,))]
        + [pltpu.HBM(b.shape, b.dtype) for b in bufs] + [jax.ShapeDtypeStruct((8, LANES), F32)],
        input_output_aliases={w: 2 + w for w in range(nw)},
        compiler_params=pltpu.CompilerParams(has_side_effects=SPLIT_EFFECT),
    )(*[_in_hbm(b) for b in bufs], after)
    return res[0], res[1], list(res[2:2 + nw]), res[2 + nw]


def _gather_wait_call(bufs, send, recv, after):
    nw = len(bufs)

    def body(*refs):
        ins, send, recv = refs[:nw], refs[nw], refs[nw + 1]
        for out, back in _gather_copies(ins, send, recv):
            out.wait_send()
            back.wait_recv()

    return pl.pallas_call(
        body, name="weights_gather_wait",
        in_specs=[HBM_SPEC] * nw + [SEM_SPEC, SEM_SPEC, ANY],
        out_specs=[HBM_SPEC] * nw,
        out_shape=[pltpu.HBM(b.shape, b.dtype) for b in bufs],
        input_output_aliases={w: w for w in range(nw)},
        compiler_params=pltpu.CompilerParams(has_side_effects=SPLIT_EFFECT),
    )(*bufs, send, recv, after)


def _scatter_copies(srcs, lands, send, recv, wholes):
    x, y, c = _coords()
    me = 4 * x + 2 * y + c
    cps = []
    for w, (src, land) in enumerate(zip(srcs, lands)):
        for r in range(1, N_DEV):
            px, py, pc = ((1 - x) if r & 4 else x, (1 - y) if r & 2 else y, (1 - c) if r & 1 else c)
            sems = (send.at[(N_DEV - 1) * w + r - 1], recv.at[(N_DEV - 1) * w + r - 1], (px, py, pc))
            piece = src if wholes[w] else src.at[2 * px + py, pc]
            cps.append((_remote(piece, land.at[me], *sems), _remote(piece, land.at[4 * px + 2 * py + pc], *sems)))
    return cps


def _scatter_start_call(srcs, lands, wholes, name):
    nw = len(srcs)
    ncp = (N_DEV - 1) * nw

    def body(*refs):
        ins, lnd, send, recv, token = refs[:nw], refs[nw:2 * nw], refs[2 * nw], refs[2 * nw + 1], refs[4 * nw + 2]
        for out, _ in _scatter_copies(ins, lnd, send, recv, wholes):
            out.start()
        token[...] = jnp.zeros(token.shape, F32)

    res = pl.pallas_call(
        body, name=name,
        in_specs=[HBM_SPEC] * (2 * nw),
        out_specs=[SEM_SPEC, SEM_SPEC] + [HBM_SPEC] * (2 * nw) + [pl.BlockSpec(memory_space=pltpu.VMEM)],
        out_shape=[pltpu.SemaphoreType.DMA((ncp,)), pltpu.SemaphoreType.DMA((ncp,))]
        + [pltpu.HBM(b.shape, b.dtype) for b in list(srcs) + list(lands)] + [jax.ShapeDtypeStruct((8, LANES), F32)],
        input_output_aliases={i: 2 + i for i in range(2 * nw)},
        compiler_params=pltpu.CompilerParams(has_side_effects=SPLIT_EFFECT),
    )(*[_in_hbm(b) for b in list(srcs) + list(lands)])
    return res[0], res[1], list(res[2:2 + nw]), list(res[2 + nw:2 + 2 * nw]), res[2 + 2 * nw]


def _scatter_wait_call(srcs, lands, send, recv, after, wholes, name):
    nw = len(srcs)

    def body(*refs):
        ins, lnd, send, recv = refs[:nw], refs[nw:2 * nw], refs[2 * nw], refs[2 * nw + 1]
        for out, back in _scatter_copies(ins, lnd, send, recv, wholes):
            out.wait_send()
            back.wait_recv()

    res = pl.pallas_call(
        body, name=name,
        in_specs=[HBM_SPEC] * (2 * nw) + [SEM_SPEC, SEM_SPEC, ANY],
        out_specs=[HBM_SPEC] * (2 * nw),
        out_shape=[pltpu.HBM(b.shape, b.dtype) for b in list(srcs) + list(lands)],
        input_output_aliases={i: i for i in range(2 * nw)},
        compiler_params=pltpu.CompilerParams(has_side_effects=SPLIT_EFFECT),
    )(*srcs, *lands, send, recv, after)
    return list(res[nw:])


def _reduce_call(own, lands, idx, nch, name, dep=None):
    nw = len(own)
    deps = [] if dep is None else [dep]

    def body(idx_ref, *refs):
        refs = refs[:2 * nw] + refs[2 * nw + len(deps):]
        for w in range(nw):
            tot = refs[w][...]
            for r in range(1, N_DEV):
                tot = tot + refs[nw + w][idx_ref[1 + r]].astype(F32)
            refs[2 * nw + w][...] = tot

    in_specs, out_specs, out_shape = [], [], []
    for s in own:
        in_specs.append(pl.BlockSpec((None, None, s.shape[2] // nch, s.shape[3]),
                                     lambda i, idx_ref: (idx_ref[0], idx_ref[1], i, 0)))
    for s in own:
        in_specs.append(pl.BlockSpec((N_DEV, s.shape[2] // nch, s.shape[3]), lambda i, idx_ref: (0, i, 0)))
    for s in own:
        out_specs.append(pl.BlockSpec((None, s.shape[2] // nch, s.shape[3]), lambda i, idx_ref: (idx_ref[1], i, 0)))
        out_shape.append(jax.ShapeDtypeStruct((2,) + s.shape[2:], F32))
    return pl.pallas_call(
        body, name=name,
        grid_spec=pltpu.PrefetchScalarGridSpec(num_scalar_prefetch=1, grid=(nch,),
                                               in_specs=in_specs + [ANY] * len(deps), out_specs=out_specs),
        out_shape=out_shape,
        compiler_params=_params(("arbitrary",)),
    )(idx, *own, *lands, *deps)


def _pair_allgather_call(halves, name):
    nw = len(halves)

    def body(*refs):
        outs = refs[nw:2 * nw]
        send, recv = refs[2 * nw:]
        x, y, c = _coords()
        _sibling_handshake()
        cps = []
        for w in range(nw):
            cp = _remote(outs[w].at[c], outs[w].at[c], send.at[w], recv.at[w], (x, y, 1 - c))
            cp.start()
            cps.append(cp)
        for w in range(nw):
            theirs = outs[w].at[1 - c]
            _remote(theirs, theirs, send.at[w], recv.at[w], (x, y, 1 - c)).wait_recv()
        for cp in cps:
            cp.wait_send()

    outs = pl.pallas_call(
        body, name=name,
        in_specs=[ANY] * nw, out_specs=[ANY] * nw,
        out_shape=[jax.ShapeDtypeStruct(h.shape, h.dtype) for h in halves],
        input_output_aliases={w: w for w in range(nw)},
        scratch_shapes=[pltpu.SemaphoreType.DMA((nw,))] * 2,
        compiler_params=pltpu.CompilerParams(collective_id=PAIR_ALLGATHER_ID),
    )(*halves)
    return [o.reshape(2 * h.shape[1], h.shape[2]) for o, h in zip(outs, halves)]


def _adamw(w, g, m, v):
    m = ADAM_B1 * m + (1.0 - ADAM_B1) * g
    v = ADAM_B2 * v + (1.0 - ADAM_B2) * (g * g)
    m_hat = m / (1.0 - ADAM_B1 ** ADAM_STEP)
    v_hat = v / (1.0 - ADAM_B2 ** ADAM_STEP)
    delta = -ADAM_LR * (m_hat / (jnp.sqrt(v_hat) + ADAM_EPS) + ADAM_WD * w)
    return delta, m, v


def _adamw_call(ws, gs, ms, vs, nch, name):
    nw = len(ws)

    def body(*refs):
        for w in range(nw):
            g = refs[nw + w][...]
            delta, m, v = _adamw(refs[w][...], g, refs[2 * nw + w][...], refs[3 * nw + w][...])
            refs[4 * nw + w][...] = g
            refs[5 * nw + w][...] = delta
            refs[6 * nw + w][...] = m
            refs[7 * nw + w][...] = v

    specs = [pl.BlockSpec((a.shape[0] // nch, a.shape[1]), lambda i: (i, 0)) for a in ws]
    res = pl.pallas_call(
        body, name=name,
        grid=(nch,),
        in_specs=specs * 4, out_specs=specs * 4,
        out_shape=[jax.ShapeDtypeStruct(a.shape, F32) for a in ws] * 4,
        compiler_params=_params(("arbitrary",)),
    )(*ws, *gs, *ms, *vs)
    return res[:nw], res[nw:2 * nw], res[2 * nw:3 * nw], res[3 * nw:]


def _small_call(gathered, own, me_idx, w, m, v):
    cuts = dict(mix_norm_g=(0, 0, 1024), mlp_norm_g=(1, 0, 1024), pool_scale=(2, 0, POOL_WIDTH),
                rel_bias=(2, POOL_WIDTH, N_BUCKETS * N_HEADS), q_norm_g=(3, 0, HEAD_DIM), k_norm_g=(3, LANES, HEAD_DIM))
    n_out = len(cuts) + 1

    def fold(row):
        tot = row[:, 0:LANES] + row[:, LANES:2 * LANES] + row[:, 2 * LANES:3 * LANES] + row[:, 3 * LANES:4 * LANES]
        return tot + pltpu.roll(tot, HEAD_DIM, axis=1)

    def body(me_ref, gh_ref, gp_ref, oh_ref, op_ref, wh, wp, mh, mp, vh, vp, loss_ref, *outs):
        me = me_ref[0]

        def total(ga_ref, own_ref):
            term = lambda i: jnp.where(me == i, own_ref[...], ga_ref[i]).astype(F32)
            tot = term(0)
            for i in range(1, N_DEV):
                tot = tot + term(i)
            return tot

        g_head, g_pool = total(gh_ref, oh_ref), total(gp_ref, op_ref)
        unfolded = g_head[4:5, :]
        folded = jnp.concatenate([fold(unfolded[:, :ATTN_WIDTH]), fold(unfolded[:, ATTN_WIDTH:]),
                                  jnp.zeros((1, 1024 - 2 * LANES), F32)], axis=-1)
        row = lax.broadcasted_iota(jnp.int32, g_head.shape, 0)
        g_head = jnp.where(row == 3, folded, g_head)
        loss_ref[...] = g_head[LOSS_ROW:LOSS_ROW + 1, 0:LANES]
        heads = (g_head,) + _adamw(wh[...], g_head, mh[...], vh[...])
        pools = (g_pool,) + _adamw(wp[...], g_pool, mp[...], vp[...])
        for kind in range(4):
            mine = outs[kind * n_out:(kind + 1) * n_out]
            for out, (row, at, n) in zip(mine, cuts.values()):
                out[...] = heads[kind][row:row + 1, at:at + n]
            mine[-1][...] = pools[kind]

    vmem = pl.BlockSpec(memory_space=pltpu.VMEM)
    shapes = [jax.ShapeDtypeStruct((1, n), F32) for _, _, n in cuts.values()] + [jax.ShapeDtypeStruct(w[1].shape, F32)]
    res = pl.pallas_call(
        body, name="adamw_small",
        in_specs=[pl.BlockSpec(memory_space=pltpu.SMEM)] + [vmem] * 10,
        out_shape=[jax.ShapeDtypeStruct((1, LANES), F32)] + shapes * 4,
        compiler_params=_params(),
    )(me_idx, *gathered, *own, *w, *m, *v)

    def unpack(mine):
        p = {n: a.reshape(-1) for n, a in zip(cuts, mine)}
        p["rel_bias"] = p["rel_bias"].reshape(N_BUCKETS, N_HEADS)
        p["pool_w"] = mine[-1].reshape(len(POOL_WINDOWS), LANES, LANES)
        return p

    return [res[0]] + [unpack(res[1 + kind * n_out:1 + (kind + 1) * n_out]) for kind in range(4)]


def _pack_small(p, folded=True, loss=None):
    z = lambda n: jnp.zeros((n,), F32)
    rows = [p["mix_norm_g"], p["mlp_norm_g"],
            jnp.concatenate([p["pool_scale"], p["rel_bias"].reshape(-1), z(1024 - POOL_WIDTH - N_BUCKETS * N_HEADS)])]
    if folded:
        rows += [jnp.concatenate([p["q_norm_g"], z(LANES - HEAD_DIM), p["k_norm_g"], z(1024 - LANES - HEAD_DIM)]), z(1024)]
    else:
        rows += [z(1024), jnp.concatenate([p["q_norm_g"], p["k_norm_g"]])]
    rows += [z(1024) if loss is None else jnp.concatenate([loss.reshape(1), z(1023)])]
    return jnp.stack(rows + [z(1024)] * 2), p["pool_w"].reshape(-1, LANES)


_WEIGHT_ORDER = ("mix_norm_g", "w_in", "pool_w", "pool_scale", "q_norm_g", "k_norm_g", "rel_bias", "w_out",
                 "mlp_norm_g", "w_up", "w_down")
_BIG = ("w_in", "w_out", "w_up", "w_down")


def kernel(x, mix_norm_g, w_in, pool_w, pool_scale, q_norm_g, k_norm_g, rel_bias, w_out, mlp_norm_g, w_up, w_down, loss_target, m_mix_norm_g, m_w_in, m_pool_w, m_pool_scale, m_q_norm_g, m_k_norm_g, m_rel_bias, m_w_out, m_mlp_norm_g, m_w_up, m_w_down, v_mix_norm_g, v_w_in, v_pool_w, v_pool_scale, v_q_norm_g, v_k_norm_g, v_rel_bias, v_w_out, v_mlp_norm_g, v_w_up, v_w_down):
    w = dict(mix_norm_g=mix_norm_g, w_in=w_in, pool_w=pool_w, pool_scale=pool_scale, q_norm_g=q_norm_g,
             k_norm_g=k_norm_g, rel_bias=rel_bias, w_out=w_out, mlp_norm_g=mlp_norm_g, w_up=w_up, w_down=w_down)
    m = dict(mix_norm_g=m_mix_norm_g, w_in=m_w_in, pool_w=m_pool_w, pool_scale=m_pool_scale, q_norm_g=m_q_norm_g,
             k_norm_g=m_k_norm_g, rel_bias=m_rel_bias, w_out=m_w_out, mlp_norm_g=m_mlp_norm_g, w_up=m_w_up, w_down=m_w_down)
    v = dict(mix_norm_g=v_mix_norm_g, w_in=v_w_in, pool_w=v_pool_w, pool_scale=v_pool_scale, q_norm_g=v_q_norm_g,
             k_norm_g=v_k_norm_g, rel_bias=v_rel_bias, w_out=v_w_out, mlp_norm_g=v_mlp_norm_g, w_up=v_w_up, w_down=v_w_down)
    xc, yc, cc = _coords()

    c_idx = jnp.reshape(cc, (1,)).astype(jnp.int32)
    chip_idx = jnp.reshape(2 * xc + yc, (1,)).astype(jnp.int32)
    me = 4 * xc + 2 * yc + cc
    whole = lambda t: t.reshape(t.shape[0], t.shape[1] * t.shape[2], t.shape[3])

    placed = [_halves(p) for p in _place_shards_call([w[n] for n in _BIG], chip_idx, nch=4)]
    win_f, bias = _allgather_call(placed[:1], from_chips=True, name="weights_allgather_in",
                                  meanwhile=_bias_table_work(rel_bias))
    wsend, wrecv, in_flight, started = _gather_start_call(placed[1:], win_f)

    def mlp_weights(after):
        landed = _gather_wait_call(in_flight, wsend, wrecv, after)
        wout_f, wup_f, wdown_f = _allgather_call(landed, from_chips=False, name="weights_pair_forward")
        return whole(wout_f).reshape(-1, wout_f.shape[-1]), whole(wup_f), whole(wdown_f)

    split = []

    def on_mlp_grads(*wire_grads):
        srcs = [_halves(g) for g in wire_grads]
        lands = [lax.empty((N_DEV,) + s.shape[2:], s.dtype) for s in srcs]
        split.extend(_scatter_start_call(srcs, lands, [False] * len(srcs), "grads_scatter_start"))
        return split[4]

    loss_part, dx, big_grads, small_grads = _local_grads(
        x[0], loss_target[0], mix_norm_g + started[0, 0], whole(win_f), pool_w, pool_scale, q_norm_g, k_norm_g, bias,
        mlp_norm_g, mlp_weights, on_mlp_grads)
    g_in, g_out, g_up, g_down = big_grads
    gsend, grecv, srcs_thru, lands_thru, _ = split
    lands_mlp = _scatter_wait_call(srcs_thru, lands_thru, gsend, grecv, g_in[1], [False] * 3, "grads_scatter_wait")

    head_own, pool_own = _pack_small(small_grads, folded=False, loss=loss_part)
    small_own = (head_own, pool_own.astype(WIRE_DTYPE))
    last_srcs = [_halves(g_in[1]), *small_own]
    last_lands = [lax.empty((N_DEV,) + last_srcs[0].shape[2:], WIRE_DTYPE)]
    last_lands += [lax.empty((N_DEV,) + a.shape, a.dtype) for a in small_own]
    lsend, lrecv, last_srcs, last_lands, last_started = _scatter_start_call(
        last_srcs, last_lands, [False, True, True], "grads_scatter_start_last")
    idx = jnp.concatenate([chip_idx, c_idx] + [jnp.reshape(jnp.bitwise_xor(me, r), (1,)) for r in range(1, N_DEV)])
    idx = idx.astype(jnp.int32)
    mlp = _BIG[1:]

    def update(names, own32, lands, tag, dep=None):
        halves = _reduce_call([_halves(g) for g in own32], lands, idx, 4, "grads_reduce_" + tag, dep)
        reduced = _pair_allgather_call(list(halves), "grads_pair_allgather_" + tag)
        return _adamw_call([w[n] for n in names], reduced, [m[n] for n in names], [v[n] for n in names], 8, "adamw_" + tag)

    out_mlp = update(mlp, [g_out[0], g_up[0], g_down[0]], lands_mlp, "mlp", last_started)
    land_in, *small_all = _scatter_wait_call(last_srcs, last_lands, lsend, lrecv, out_mlp[3][-1], [False, True, True],
                                             "grads_scatter_wait_last")
    out_in = update(_BIG[:1], [g_in[0]], [land_in], "in")
    loss_row, grads, deltas, new_m, new_v = _small_call(
        small_all, small_own, jnp.reshape(me, (1,)).astype(jnp.int32), _pack_small(w), _pack_small(m), _pack_small(v))

    for k, res in enumerate((grads, deltas, new_m, new_v)):
        res[_BIG[0]] = out_in[k][0]
        for i, n in enumerate(mlp):
            res[n] = out_mlp[k][i]
    loss = loss_row[0, 0]
    return (loss, dx[None], *[grads[n] for n in _WEIGHT_ORDER], *[deltas[n] for n in _WEIGHT_ORDER],
            *[new_m[n] for n in _WEIGHT_ORDER], *[new_v[n] for n in _WEIGHT_ORDER])
```

```python
import math

import jax
import jax.numpy as jnp
import numpy as np
from jax import lax
from jax.experimental import pallas as pl
from jax.experimental.pallas import tpu as pltpu

F32 = jnp.float32
MXU_DTYPE = jnp.bfloat16
WIRE_DTYPE = jnp.bfloat16

NORM_EPS = 1e-6
NEG_INF = -1e30
LANES = 128
HEAD_DIM = 64
N_HEADS = 8
POOL_WIDTH = 512
ATTN_WIDTH = 512
POOL_WINDOWS = (2, 4, 8, 16)
POOL_HALO = 16
DILATED_PATTERNS = ((128, 1), (512, 4), (2048, 16))
ATT_BLOCK = 128
ATT_SUPER = ATT_BLOCK * max(dl for _, dl in DILATED_PATTERNS)
ATT_UNITS = ATT_SUPER // ATT_BLOCK
N_BUCKETS = 32
NO_BUCKET = -1
MAX_DISTANCE = 2048
N_CHIPS = 4
N_DEV = 8
ADAM_LR, ADAM_B1, ADAM_B2, ADAM_EPS, ADAM_WD, ADAM_STEP = 0.001, 0.9, 0.999, 1e-08, 0.01, 10
VMEM_LIMIT = 56 * 1024 * 1024
MESH = pl.DeviceIdType.MESH
ANY = pl.BlockSpec(memory_space=pl.ANY)

LOSS_ROW = 5


def _mm(a, b):
    return jnp.dot(a, b, preferred_element_type=F32)


def _mm_nt(a, b):
    return lax.dot_general(a, b, (((1,), (1,)), ((), ())), preferred_element_type=F32)


def _mm_tn(a, b):
    return lax.dot_general(a, b, (((0,), (0,)), ((), ())), preferred_element_type=F32)


def _params(sem=None, **kw):
    if sem is not None:
        kw["dimension_semantics"] = sem
    return pltpu.CompilerParams(vmem_limit_bytes=VMEM_LIMIT, **kw)


def _low_half():
    return lax.broadcasted_iota(jnp.int32, (1, LANES), 1) < HEAD_DIM


def _head_sum_bcast(y):
    lo = _low_half()
    outs = []
    for j in range(y.shape[1] // LANES):
        c = y[:, j * LANES:(j + 1) * LANES]
        s_lo = jnp.sum(jnp.where(lo, c, 0.0), axis=-1, keepdims=True)
        s_hi = jnp.sum(jnp.where(lo, 0.0, c), axis=-1, keepdims=True)
        outs.append(jnp.where(lo, s_lo, s_hi))
    return jnp.concatenate(outs, axis=-1)


def _rms_bwd(dn, hn, r):
    return r * (dn - hn * jnp.mean(dn * hn, axis=-1, keepdims=True))


def _t5_bucket_np(dist):
    max_exact = N_BUCKETS // 2
    d_f = np.maximum(dist, 1).astype(np.float32)
    ratio = (np.log(d_f / np.float32(max_exact)) / np.float32(math.log(MAX_DISTANCE / max_exact))).astype(np.float32)
    large = max_exact + (ratio * np.float32(N_BUCKETS - max_exact)).astype(np.int32)
    large = np.minimum(large, N_BUCKETS - 1)
    return np.where(dist < max_exact, dist, large).astype(np.int32)


def _window_offsets(dl):
    if dl == 1:
        return _by4_positions(ATT_BLOCK), _by4_positions(2 * ATT_BLOCK)
    return np.arange(ATT_BLOCK), np.arange(2 * ATT_BLOCK)


def _bucket_tables():
    tables = []
    for _, dl in DILATED_PATTERNS:
        qq, kk = _window_offsets(dl)
        dist = qq[:, None] + ATT_BLOCK - kk[None, :]
        bucket = _t5_bucket_np(np.clip(dist, 0, ATT_BLOCK) * dl)
        tables.append(np.where((dist >= 0) & (dist <= ATT_BLOCK), bucket, NO_BUCKET))
    return np.stack(tables).astype(np.int32)


def _previous_block_keys():
    return np.stack([np.broadcast_to(_window_offsets(dl)[1][None, :] < ATT_BLOCK, (ATT_BLOCK, 2 * ATT_BLOCK))
                     for _, dl in DILATED_PATTERNS])


def _f1_call(x, g1, win, poolw, pscale, qg, kg, tm):
    s, d = x.shape
    nblk = s // tm

    def body(x_ref, g1_ref, win_ref, pw_ref, ps_ref, qg_ref, kg_ref,
             a_ref, pooled_ref, ypool_ref, q32_ref, k32_ref, qn_ref, kn_ref, v_ref, ubuf):
        i = pl.program_id(0)
        xv = x_ref[...]
        r = lax.rsqrt(jnp.mean(xv * xv, axis=-1, keepdims=True) + NORM_EPS)
        a = ((xv * r) * g1_ref[...]).astype(MXU_DTYPE)
        a_ref[...] = a
        u = _mm(a, win_ref[0])
        q = _mm(a, win_ref[1])
        k = _mm(a, win_ref[2])
        v_ref[...] = _mm(a, win_ref[3])
        q32_ref[...] = q
        k32_ref[...] = k
        rq = lax.rsqrt(_head_sum_bcast(q * q) * (1.0 / HEAD_DIM) + NORM_EPS)
        qn_ref[...] = ((q * rq) * qg_ref[...]) * (HEAD_DIM ** -0.5)
        rk = lax.rsqrt(_head_sum_bcast(k * k) * (1.0 / HEAD_DIM) + NORM_EPS)
        kn_ref[...] = (k * rk) * kg_ref[...]

        ubuf[0:POOL_HALO, :] = jnp.where(i > 0, ubuf[tm:tm + POOL_HALO, :], 0.0)
        ubuf[POOL_HALO:POOL_HALO + tm, :] = u
        t = i * tm + lax.broadcasted_iota(jnp.int32, (tm, 1), 0)
        for g, w in enumerate(POOL_WINDOWS):
            ls = slice(g * LANES, (g + 1) * LANES)
            ug = u[:, ls]
            acc = ug
            for sh in range(1, w):
                acc = acc + ubuf[POOL_HALO - sh:POOL_HALO - sh + tm, ls]
            cnt = jnp.minimum(t + 1, w).astype(F32)
            pooled = (acc / cnt - ug).astype(MXU_DTYPE)
            pooled_ref[:, ls] = pooled
            ypool_ref[:, ls] = (_mm(pooled, pw_ref[g]) * ps_ref[:, ls]).astype(MXU_DTYPE)

    tok = lambda w: pl.BlockSpec((tm, w), lambda i: (i, 0))
    full = lambda shp: pl.BlockSpec(shp, lambda i: (0,) * len(shp))
    return pl.pallas_call(
        body, name="fwd_inproj",
        grid=(nblk,),
        in_specs=[tok(d), full((1, d)), full(win.shape), full(poolw.shape), full((1, POOL_WIDTH)),
                  full((1, ATTN_WIDTH)), full((1, ATTN_WIDTH))],
        out_specs=[tok(d), tok(POOL_WIDTH), tok(POOL_WIDTH), tok(ATTN_WIDTH), tok(ATTN_WIDTH),
                   tok(ATTN_WIDTH), tok(ATTN_WIDTH), tok(ATTN_WIDTH)],
        out_shape=[jax.ShapeDtypeStruct((s, d), MXU_DTYPE),
                   jax.ShapeDtypeStruct((s, POOL_WIDTH), MXU_DTYPE),
                   jax.ShapeDtypeStruct((s, POOL_WIDTH), MXU_DTYPE),
                   jax.ShapeDtypeStruct((s, ATTN_WIDTH), F32),
                   jax.ShapeDtypeStruct((s, ATTN_WIDTH), F32),
                   jax.ShapeDtypeStruct((s, ATTN_WIDTH), F32),
                   jax.ShapeDtypeStruct((s, ATTN_WIDTH), F32),
                   jax.ShapeDtypeStruct((s, ATTN_WIDTH), F32)],
        scratch_shapes=[pltpu.VMEM((tm + POOL_HALO, POOL_WIDTH), F32)],
        compiler_params=_params(("arbitrary",)),
    )(x, g1, win, poolw, pscale, qg, kg)


DEINT = 4
assert [dl for _, dl in DILATED_PATTERNS] == [1, DEINT, DEINT * DEINT]


def _by4_positions(n):
    pos = np.arange(n)
    return DEINT * (pos % (n // DEINT)) + pos // (n // DEINT)


def _masked_bias(b_ref, p, n):
    return b_ref[p, jnp.minimum(n, 1)].reshape(2 * ATT_BLOCK, 2 * ATT_BLOCK)


def _unit_rows(u, dl):
    assert isinstance(u, int)
    sq, sk = ATT_SUPER // DEINT, 2 * ATT_SUPER // DEINT
    if dl == 1:
        n = ATT_BLOCK // DEINT
        return (u, [pl.ds(r * sq + n * u, n) for r in range(DEINT)],
                [pl.ds(r * sk + sk // 2 + n * (u - 1), 2 * n) for r in range(DEINT)])
    if dl == DEINT:
        r, b = u % DEINT, u // DEINT
        return (b, [pl.ds(r * sq + ATT_BLOCK * b, ATT_BLOCK)],
                [pl.ds(r * sk + sk // 2 + ATT_BLOCK * (b - 1), 2 * ATT_BLOCK)])
    r, a = u % DEINT, u // DEINT
    return 0, [pl.ds(r * sq + a, ATT_BLOCK, stride=DEINT)], [pl.ds(r * sk + a, 2 * ATT_BLOCK, stride=DEINT)]


def _take(ref, runs):
    parts = [ref[run, :] for run in runs]
    return parts[0] if len(parts) == 1 else jnp.concatenate(parts, axis=0)


def _put(ref, runs, value, add=False):
    n = value.shape[0] // len(runs)
    for i, run in enumerate(runs):
        part = value[i * n:(i + 1) * n]
        ref[run, :] = ref[run, :] + part if add else part


def _deinterleave(dst, src, n):
    seg = n // DEINT
    for r in range(DEINT):
        dst[r * seg:(r + 1) * seg, :] = src[pl.ds(r, seg, stride=DEINT), :]


def _deinterleave_pair(dst, prev, cur):
    seg = prev.shape[0] // DEINT
    for r in range(DEINT):
        dst[2 * r * seg:(2 * r + 1) * seg, :] = prev[pl.ds(r, seg, stride=DEINT), :]
        dst[(2 * r + 1) * seg:(2 * r + 2) * seg, :] = cur[pl.ds(r, seg, stride=DEINT), :]


def _interleave(dst, src, n, offset=0):
    seg = n // DEINT
    stride = src.shape[0] // DEINT
    for r in range(DEINT):
        dst[pl.ds(r, seg, stride=DEINT), :] = src[r * stride + offset:r * stride + offset + seg, :]


def _attn_fwd_call(qn, kn, v, bias):
    s, w = qn.shape
    nsb = s // ATT_SUPER
    npair = w // LANES

    def body(q_ref, kc_ref, vc_ref, b_ref, o_ref, lse_ref, qf, kf, vf, acc_s, m_s, l_s):
        sb = pl.program_id(1)

        @pl.when((pl.program_id(0) == 0) & (sb == 0))
        def _():
            kf[...] = jnp.zeros_like(kf)
            vf[...] = jnp.zeros_like(vf)

        seg = ATT_SUPER // DEINT
        _deinterleave(qf, q_ref, ATT_SUPER)
        for r in range(DEINT):
            for dst, src in ((kf, kc_ref), (vf, vc_ref)):
                dst[2 * r * seg:(2 * r + 1) * seg, :] = dst[(2 * r + 1) * seg:(2 * r + 2) * seg, :]
                dst[(2 * r + 1) * seg:(2 * r + 2) * seg, :] = src[pl.ds(r, seg, stride=DEINT), :]
        lo = _low_half()
        for p, (_, dl) in enumerate(DILATED_PATTERNS):
            def unit(u, carry, p=p, dl=dl):
                b, rows_q, rows_k = _unit_rows(u, dl)
                qp = _take(qf, rows_q).astype(MXU_DTYPE)
                kcat = _take(kf, rows_k).astype(MXU_DTYPE)
                vcat = _take(vf, rows_k).astype(MXU_DTYPE)
                zero = jnp.zeros_like(qp)
                q2 = jnp.concatenate([jnp.where(lo, qp, zero), jnp.where(lo, zero, qp)], axis=0)
                sc = _mm_nt(q2, kcat) + _masked_bias(b_ref, p, sb * (ATT_UNITS // dl) + b)
                m2 = jnp.max(sc, axis=-1, keepdims=True)
                pr = jnp.exp(sc - m2)
                l2 = jnp.sum(pr, axis=-1, keepdims=True)
                acc2 = _mm(pr.astype(MXU_DTYPE), vcat)
                acc = jnp.where(lo, acc2[:ATT_BLOCK], acc2[ATT_BLOCK:])
                m = jnp.where(lo, m2[:ATT_BLOCK], m2[ATT_BLOCK:])
                l = jnp.where(lo, l2[:ATT_BLOCK], l2[ATT_BLOCK:])
                if p == 0:
                    _put(acc_s, rows_q, acc)
                    _put(m_s, rows_q, m)
                    _put(l_s, rows_q, l)
                else:
                    m_old = _take(m_s, rows_q)
                    m_new = jnp.maximum(m_old, m)
                    a_old = jnp.exp(m_old - m_new)
                    a_new = jnp.exp(m - m_new)
                    _put(acc_s, rows_q, a_old * _take(acc_s, rows_q) + a_new * acc)
                    _put(l_s, rows_q, a_old * _take(l_s, rows_q) + a_new * l)
                    _put(m_s, rows_q, m_new)
                return carry

            for u in range(ATT_UNITS):
                unit(u, None)
        l = l_s[...]
        acc_s[...] = acc_s[...] / l
        m_s[...] = m_s[...] + jnp.log(l)
        _interleave(o_ref, acc_s, ATT_SUPER)
        _interleave(lse_ref, m_s, ATT_SUPER)

    cur = pl.BlockSpec((ATT_SUPER, LANES), lambda j, t: (t, j))
    bspec = pl.BlockSpec((len(DILATED_PATTERNS), 2, 2, ATT_BLOCK, 2 * ATT_BLOCK), lambda j, t: (0, 0, j, 0, 0))
    return pl.pallas_call(
        body, name="attn_fwd",
        grid=(npair, nsb),
        in_specs=[cur, cur, cur, bspec],
        out_specs=[cur, cur],
        out_shape=[jax.ShapeDtypeStruct((s, w), F32), jax.ShapeDtypeStruct((s, w), F32)],
        scratch_shapes=[pltpu.VMEM((ATT_SUPER, LANES), F32), pltpu.VMEM((2 * ATT_SUPER, LANES), F32),
                        pltpu.VMEM((2 * ATT_SUPER, LANES), F32), pltpu.VMEM((ATT_SUPER, LANES), F32),
                        pltpu.VMEM((ATT_SUPER, LANES), F32), pltpu.VMEM((ATT_SUPER, LANES), F32)],
        compiler_params=_params(("arbitrary", "arbitrary")),
    )(qn, kn, v, bias)


def _attn_bwd_call(qn, kn, v, do, lse, delta, bias, dep=None):
    s, w = qn.shape
    nsb = s // ATT_SUPER
    npair = w // LANES
    deps = [] if dep is None else [dep]

    def body(q_ref, kc_ref, kp_ref, vc_ref, vp_ref, do_ref, lse_ref, dlt_ref, b_ref, *rest):
        dq_ref, dk_ref, dv_ref, db_ref, qf, kf, vf, dof, lsef, dltf, dqf, dkf, dvf = rest[len(deps):]
        step = pl.program_id(1)
        sb = nsb - 1 - step
        seg = ATT_SUPER // DEINT
        _deinterleave(qf, q_ref, ATT_SUPER)
        _deinterleave(dof, do_ref, ATT_SUPER)
        _deinterleave_pair(kf, kp_ref, kc_ref)
        _deinterleave_pair(vf, vp_ref, vc_ref)
        _deinterleave(lsef, lse_ref, ATT_SUPER)
        _deinterleave(dltf, dlt_ref, ATT_SUPER)

        db_ref[...] = jnp.where(step > 0, db_ref[...], 0.0)
        for acc in (dkf, dvf):
            for r in range(DEINT):
                this, before = pl.ds((2 * r + 1) * seg, seg), pl.ds(2 * r * seg, seg)
                acc[this, :] = jnp.where(step > 0, acc[before, :], 0.0)
                acc[before, :] = jnp.zeros((seg, LANES), F32)
        lo = _low_half()
        for p, (_, dl) in enumerate(DILATED_PATTERNS):
            def unit(u, carry, p=p, dl=dl):
                b, rows_q, rows_k = _unit_rows(u, dl)
                qp = _take(qf, rows_q).astype(MXU_DTYPE)
                dop = _take(dof, rows_q).astype(MXU_DTYPE)
                kcat = _take(kf, rows_k).astype(MXU_DTYPE)
                vcat = _take(vf, rows_k).astype(MXU_DTYPE)
                lse2 = _take(lsef, rows_q)
                dlt2 = _take(dltf, rows_q)
                zero = jnp.zeros_like(qp)
                q2 = jnp.concatenate([jnp.where(lo, qp, zero), jnp.where(lo, zero, qp)], axis=0)
                do2 = jnp.concatenate([jnp.where(lo, dop, zero), jnp.where(lo, zero, dop)], axis=0)
                lse_c = jnp.concatenate([lse2[:, 0:1], lse2[:, HEAD_DIM:HEAD_DIM + 1]], axis=0)
                dlt_c = jnp.concatenate([dlt2[:, 0:1], dlt2[:, HEAD_DIM:HEAD_DIM + 1]], axis=0)
                sc = _mm_nt(q2, kcat) + _masked_bias(b_ref, p, sb * (ATT_UNITS // dl) + b)
                pr = jnp.exp(sc - lse_c)
                ds = pr * (_mm_nt(do2, vcat) - dlt_c)
                db_ref[p] += ds.reshape(2, ATT_BLOCK, 2 * ATT_BLOCK)
                ds_c = ds.astype(MXU_DTYPE)
                dq2 = _mm(ds_c, kcat)
                dk = _mm_tn(ds_c, q2)
                dv = _mm_tn(pr.astype(MXU_DTYPE), do2)
                dq = jnp.where(lo, dq2[:ATT_BLOCK], dq2[ATT_BLOCK:])
                _put(dqf, rows_q, dq, add=p > 0)
                _put(dkf, rows_k, dk, add=True)
                _put(dvf, rows_k, dv, add=True)
                return carry

            for u in range(ATT_UNITS):
                unit(u, None)
        _interleave(dq_ref, dqf, ATT_SUPER)
        _interleave(dk_ref, dkf, ATT_SUPER, offset=seg)
        _interleave(dv_ref, dvf, ATT_SUPER, offset=seg)

    cur = pl.BlockSpec((ATT_SUPER, LANES), lambda j, t: (nsb - 1 - t, j))
    prev = pl.BlockSpec((ATT_SUPER, LANES), lambda j, t: (jnp.maximum(nsb - 2 - t, 0), j))
    npat = len(DILATED_PATTERNS)
    bspec = pl.BlockSpec((npat, 2, 2, ATT_BLOCK, 2 * ATT_BLOCK), lambda j, t: (0, 0, j, 0, 0))
    dbspec = pl.BlockSpec((npat, 2, ATT_BLOCK, 2 * ATT_BLOCK), lambda j, t: (0, j, 0, 0))
    sup = lambda: pltpu.VMEM((ATT_SUPER, LANES), F32)
    sup2 = lambda: pltpu.VMEM((2 * ATT_SUPER, LANES), F32)
    return pl.pallas_call(
        body, name="attn_bwd",
        grid=(npair, nsb),
        in_specs=[cur, cur, prev, cur, prev, cur, cur, cur, bspec] + [ANY] * len(deps),
        out_specs=[cur, cur, cur, dbspec],
        out_shape=[jax.ShapeDtypeStruct((s, w), F32)] * 3
        + [jax.ShapeDtypeStruct((npat, N_HEADS, ATT_BLOCK, 2 * ATT_BLOCK), F32)],
        scratch_shapes=[sup(), sup2(), sup2(), sup(), sup(), sup(), sup(), sup2(), sup2()],
        compiler_params=_params(("arbitrary", "arbitrary")),
    )(qn, kn, kn, v, v, do, lse, delta, bias, *deps)


def _bias_table_work(rel_bias):
    buckets = jnp.asarray(_bucket_tables())
    prev_keys = jnp.asarray(_previous_block_keys().astype(np.int32))
    npat = buckets.shape[0]

    def body(rb_ref, bk_ref, pk_ref, out_ref):
        for p in range(npat):
            for half in range(2):
                ks = slice(half * ATT_BLOCK, (half + 1) * ATT_BLOCK)
                bk = bk_ref[p, :, ks]
                absent = pk_ref[p, :, ks] != 0
                for h in range(N_HEADS):
                    def pick(b, acc, h=h, bk=bk):
                        return jnp.where(bk == b, rb_ref[b, h], acc)

                    tab = lax.fori_loop(0, N_BUCKETS, pick, jnp.full((ATT_BLOCK, ATT_BLOCK), NEG_INF, F32))
                    out_ref[p, 1, h, :, ks] = tab
                    out_ref[p, 0, h, :, ks] = jnp.where(absent, NEG_INF, tab)

    vmem = pl.BlockSpec(memory_space=pltpu.VMEM)
    return ([rel_bias, buckets, prev_keys], [pl.BlockSpec(memory_space=pltpu.SMEM), vmem, vmem],
            jax.ShapeDtypeStruct((npat, 2, N_HEADS, ATT_BLOCK, 2 * ATT_BLOCK), F32), body)


def _rel_bias_grad_call(dbias, buckets):
    npat, nh = dbias.shape[0], dbias.shape[1]

    def body(db_ref, bk_ref, out_ref):
        lane = lax.broadcasted_iota(jnp.int32, (nh, LANES), 1)
        out = jnp.zeros((nh, LANES), F32)
        for b in range(N_BUCKETS):
            tot = jnp.zeros((nh, 1), F32)
            for p in range(npat):
                hit = jnp.where(bk_ref[p][None] == b, db_ref[p], 0.0)
                tot = tot + jnp.sum(jnp.sum(hit, axis=1), axis=-1, keepdims=True)
            out = jnp.where(lane == b, tot, out)
        out_ref[...] = out

    return pl.pallas_call(
        body, name="rel_bias_grad",
        out_shape=jax.ShapeDtypeStruct((nh, LANES), F32),
        compiler_params=_params(),
    )(dbias, buckets)


def _f2_call(x, tgt, ypool, o, wout, wup, wdown, g2, tm):
    s, d = x.shape
    nblk = s // tm
    nch, _, fch = wup.shape
    dff = nch * fch
    mixw = POOL_WIDTH + ATTN_WIDTH

    def body(x_ref, t_ref, yp_ref, o_ref, g2_ref, wout_hbm, wup_hbm, wdown_hbm,
             mixed_ref, c_ref, ff_ref, dz_ref, dy_ref, dh1_ref, dyp_ref, do_ref, dlt_ref, dg2_ref, loss_ref,
             wout_v, wup_v, wdown_v, rz, wsem):
        i = pl.program_id(0)

        @pl.when(i == 0)
        def _():
            copies = [pltpu.make_async_copy(wout_hbm, wout_v, wsem.at[0])]
            for j in range(nch):
                copies.append(pltpu.make_async_copy(wup_hbm.at[j], wup_v.at[j], wsem.at[1 + 2 * j]))
                copies.append(pltpu.make_async_copy(wdown_hbm.at[j], wdown_v.at[j], wsem.at[2 + 2 * j]))
            for cp in copies:
                cp.start()
            dg2_ref[...] = jnp.zeros(dg2_ref.shape, F32)
            loss_ref[...] = jnp.zeros(loss_ref.shape, F32)
            for cp in copies:
                cp.wait()

        o = o_ref[...]
        mixed = jnp.concatenate([yp_ref[...], o.astype(MXU_DTYPE)], axis=-1)
        mixed_ref[...] = mixed
        h1 = x_ref[...] + _mm(mixed, wout_v[...])
        r2 = lax.rsqrt(jnp.mean(h1 * h1, axis=-1, keepdims=True) + NORM_EPS)
        hn = h1 * r2
        c = (hn * g2_ref[...]).astype(MXU_DTYPE)
        c_ref[...] = c
        y = h1
        for j in range(nch):
            cs = slice(j * fch, (j + 1) * fch)
            z = jnp.maximum(_mm(c, wup_v[j]), 0.0)
            rz[:, cs] = z
            ff = (z * z).astype(MXU_DTYPE)
            ff_ref[:, cs] = ff
            y = y + _mm(ff, wdown_v[j])
        err = y - t_ref[...]
        loss_ref[...] += jnp.sum(err * err) * (0.5 / d)
        dy = err * (1.0 / d)
        dy_c = dy.astype(MXU_DTYPE)
        dy_ref[...] = dy_c
        dc = jnp.zeros((tm, d), F32)
        for j in range(nch):
            cs = slice(j * fch, (j + 1) * fch)
            dz = (_mm_nt(dy_c, wdown_v[j]) * (2.0 * rz[:, cs])).astype(MXU_DTYPE)
            dz_ref[:, cs] = dz
            dc = dc + _mm_nt(dz, wup_v[j])
        dg2_ref[...] += jnp.sum(dc * hn, axis=0, keepdims=True)
        dh1 = dy + _rms_bwd(dc * g2_ref[...], hn, r2)
        dh1_ref[...] = dh1
        dmix = _mm_nt(dh1.astype(MXU_DTYPE), wout_v[...])
        dyp_ref[...] = dmix[:, :POOL_WIDTH]
        do = dmix[:, POOL_WIDTH:]
        do_ref[...] = do
        dlt_ref[...] = _head_sum_bcast(do * o)

    tok = lambda w: pl.BlockSpec((tm, w), lambda i: (i, 0))
    const = lambda shp: pl.BlockSpec(shp, lambda i: (0,) * len(shp))
    return pl.pallas_call(
        body, name="fwd_mlp_bwd_mlp",
        grid=(nblk,),
        in_specs=[tok(d), tok(d), tok(POOL_WIDTH), tok(ATTN_WIDTH), const((1, d)), ANY, ANY, ANY],
        out_specs=[tok(mixw), tok(d), tok(dff), tok(dff), tok(d), tok(d), tok(POOL_WIDTH), tok(ATTN_WIDTH),
                   tok(ATTN_WIDTH), const((1, d)), const((1, LANES))],
        out_shape=[jax.ShapeDtypeStruct((s, mixw), MXU_DTYPE),
                   jax.ShapeDtypeStruct((s, d), MXU_DTYPE),
                   jax.ShapeDtypeStruct((s, dff), MXU_DTYPE),
                   jax.ShapeDtypeStruct((s, dff), MXU_DTYPE),
                   jax.ShapeDtypeStruct((s, d), MXU_DTYPE),
                   jax.ShapeDtypeStruct((s, d), F32),
                   jax.ShapeDtypeStruct((s, POOL_WIDTH), F32),
                   jax.ShapeDtypeStruct((s, ATTN_WIDTH), F32),
                   jax.ShapeDtypeStruct((s, ATTN_WIDTH), F32),
                   jax.ShapeDtypeStruct((1, d), F32),
                   jax.ShapeDtypeStruct((1, LANES), F32)],
        scratch_shapes=[pltpu.VMEM(wout.shape, MXU_DTYPE), pltpu.VMEM(wup.shape, MXU_DTYPE),
                        pltpu.VMEM(wdown.shape, MXU_DTYPE), pltpu.VMEM((tm, dff), F32),
                        pltpu.SemaphoreType.DMA((1 + 2 * nch,))],
        compiler_params=_params(("arbitrary",)),
    )(x, tgt, ypool, o, g2, wout, wup, wdown)


RING_SLOTS = 3


def _bproj_call(dqn, dkn, dv, q32, k32, dypool, pooled, x, dh1, win, poolw, pscale, qg, kg, g1, tm):
    s, d = x.shape
    nblk = s // tm
    ngrp = len(POOL_WINDOWS)
    streams = [dqn, dkn, dv, q32, k32, dypool, pooled, x, dh1]
    ns = len(streams)
    assert nblk >= 2

    def body(*refs):
        hbm = refs[:ns]
        win_hbm, pw_ref, ps_ref, qg_ref, kg_ref, g1_ref = refs[ns:ns + 6]
        dx_ref, dproj_ref, dg1_ref, dqg_ref, dkg_ref, dpw_ref, dps_ref, win_v, ebuf = refs[ns + 6:ns + 15]
        rings, sems = refs[ns + 15:2 * ns + 15], refs[2 * ns + 15]
        step = pl.program_id(0)
        i = nblk - 1 - step

        def fetch(t):
            rows = pl.ds(pl.multiple_of((nblk - 1 - t) * tm, tm), tm)
            return [pltpu.make_async_copy(h.at[rows], ring.at[t % RING_SLOTS], sems.at[k, t % RING_SLOTS])
                    for k, (h, ring) in enumerate(zip(hbm, rings))]

        @pl.when(step == 0)
        def _():
            for t in range(2):
                for cp in fetch(t):
                    cp.start()
            pltpu.sync_copy(win_hbm, win_v)
            dg1_ref[...] = jnp.zeros(dg1_ref.shape, F32)
            dqg_ref[...] = jnp.zeros(dqg_ref.shape, F32)
            dkg_ref[...] = jnp.zeros(dkg_ref.shape, F32)
            dpw_ref[...] = jnp.zeros(dpw_ref.shape, F32)
            dps_ref[...] = jnp.zeros(dps_ref.shape, F32)
            ebuf[tm:tm + POOL_HALO, :] = jnp.zeros((POOL_HALO, POOL_WIDTH), F32)

        @pl.when(step > 0)
        def _():
            ebuf[tm:tm + POOL_HALO, :] = ebuf[0:POOL_HALO, :]

        @pl.when(step + 2 < nblk)
        def _():
            for cp in fetch(step + 2):
                cp.start()

        for cp in fetch(step):
            cp.wait()
        dqn_ref, dkn_ref, dv_ref, q_ref, k_ref, dyp_ref, pooled_ref, x_ref, dh1_ref = (
            ring.at[step % RING_SLOTS] for ring in rings)

        def qk_bwd(dn_sum, raw, gain, scale, dgain_ref):
            rr = lax.rsqrt(_head_sum_bcast(raw * raw) * (1.0 / HEAD_DIM) + NORM_EPS)
            hn = raw * rr
            dgain_ref[...] += jnp.sum(dn_sum * hn, axis=0, keepdims=True) * scale
            dn = dn_sum * (gain * scale)
            return rr * (dn - hn * (_head_sum_bcast(dn * hn) * (1.0 / HEAD_DIM)))

        dq = qk_bwd(dqn_ref[...], q_ref[...], qg_ref[...], HEAD_DIM ** -0.5, dqg_ref)
        dk = qk_bwd(dkn_ref[...], k_ref[...], kg_ref[...], 1.0, dkg_ref)

        t = i * tm + lax.broadcasted_iota(jnp.int32, (tm, 1), 0)
        dpooled = []
        for g, w in enumerate(POOL_WINDOWS):
            ls = slice(g * LANES, (g + 1) * LANES)
            dm = dyp_ref[:, ls]
            pg = pooled_ref[:, ls]
            dps_ref[:, ls] += jnp.sum(dm * _mm(pg, pw_ref[g]), axis=0, keepdims=True)
            dms = (dm * ps_ref[:, ls]).astype(MXU_DTYPE)
            dpw_ref[g] += _mm_tn(pg, dms)
            dpg = _mm_nt(dms, pw_ref[g])
            dpooled.append(dpg)
            ebuf[0:tm, ls] = dpg / jnp.minimum(t + 1, w).astype(F32)
        du = []
        for g, w in enumerate(POOL_WINDOWS):
            ls = slice(g * LANES, (g + 1) * LANES)
            acc = ebuf[0:tm, ls]
            for sh in range(1, w):
                acc = acc + ebuf[sh:sh + tm, ls]
            du.append(acc - dpooled[g])
        parts = [jnp.concatenate(du, axis=-1), dq, dk, dv_ref[...]]
        da = jnp.zeros((tm, d), F32)
        for p, part in enumerate(parts):
            pc = part.astype(MXU_DTYPE)
            dproj_ref[:, p * POOL_WIDTH:(p + 1) * POOL_WIDTH] = pc
            da = da + _mm_nt(pc, win_v[p])
        xv = x_ref[...]
        r = lax.rsqrt(jnp.mean(xv * xv, axis=-1, keepdims=True) + NORM_EPS)
        xn = xv * r
        dg1_ref[...] += jnp.sum(da * xn, axis=0, keepdims=True)
        dx_ref[...] = dh1_ref[...] + _rms_bwd(da * g1_ref[...], xn, r)

    tok = lambda w: pl.BlockSpec((tm, w), lambda t: (nblk - 1 - t, 0))
    const = lambda shp: pl.BlockSpec(shp, lambda t: (0,) * len(shp))
    return pl.pallas_call(
        body, name="bwd_inproj",
        grid=(nblk,),
        in_specs=[ANY] * (ns + 1) + [const(poolw.shape), const((1, POOL_WIDTH)), const((1, ATTN_WIDTH)),
                                     const((1, ATTN_WIDTH)), const((1, d))],
        out_specs=[tok(d), tok(4 * POOL_WIDTH), const((1, d)), const((1, ATTN_WIDTH)), const((1, ATTN_WIDTH)),
                   const((ngrp, LANES, LANES)), const((1, POOL_WIDTH))],
        out_shape=[jax.ShapeDtypeStruct((s, d), F32),
                   jax.ShapeDtypeStruct((s, 4 * POOL_WIDTH), MXU_DTYPE),
                   jax.ShapeDtypeStruct((1, d), F32),
                   jax.ShapeDtypeStruct((1, ATTN_WIDTH), F32),
                   jax.ShapeDtypeStruct((1, ATTN_WIDTH), F32),
                   jax.ShapeDtypeStruct((ngrp, LANES, LANES), F32),
                   jax.ShapeDtypeStruct((1, POOL_WIDTH), F32)],
        scratch_shapes=[pltpu.VMEM(win.shape, MXU_DTYPE), pltpu.VMEM((tm + POOL_HALO, POOL_WIDTH), F32)]
        + [pltpu.VMEM((RING_SLOTS, tm, a.shape[1]), a.dtype) for a in streams]
        + [pltpu.SemaphoreType.DMA((ns, RING_SLOTS))],
        compiler_params=_params(("arbitrary",)),
    )(*streams, win, poolw, pscale, qg, kg, g1)


def _wgrad_call(a, b, bm, bn, bk, out_shape, out_block, out_index, name):
    s, m = a.shape
    _, n = b.shape
    nk = s // bk

    def body(a_ref, b_ref, o_ref, wire_ref):
        k = pl.program_id(2)
        acc = jnp.where(k > 0, o_ref[...], 0.0) + _mm_tn(a_ref[...].astype(MXU_DTYPE), b_ref[...].astype(MXU_DTYPE))
        o_ref[...] = acc
        wire_ref[...] = acc.astype(WIRE_DTYPE)

    return pl.pallas_call(
        body, name=name,
        grid=(m // bm, n // bn, nk),
        in_specs=[pl.BlockSpec((bk, bm), lambda i, j, k: (k, i)), pl.BlockSpec((bk, bn), lambda i, j, k: (k, j))],
        out_specs=[pl.BlockSpec(out_block, out_index)] * 2,
        out_shape=[jax.ShapeDtypeStruct(out_shape, F32), jax.ShapeDtypeStruct(out_shape, WIRE_DTYPE)],
        compiler_params=_params(("arbitrary", "arbitrary", "arbitrary")),
    )(a, b)


def _local_grads(x, tgt, g1, win, poolw, pscale, qg, kg, bias, g2, mlp_weights, on_mlp_grads=None):
    s, d = x.shape
    g1r, g2r = g1.reshape(1, d), g2.reshape(1, d)
    psr = pscale.reshape(1, POOL_WIDTH)
    qgr = jnp.tile(qg, N_HEADS).reshape(1, ATTN_WIDTH)
    kgr = jnp.tile(kg, N_HEADS).reshape(1, ATTN_WIDTH)
    pw_c = poolw.astype(MXU_DTYPE)
    buckets = jnp.asarray(_bucket_tables())
    bk = min(s, 4096)

    a, pooled, ypool, q32, k32, qn, kn, v = _f1_call(x, g1r, win, pw_c, psr, qgr, kgr, tm=1024)
    o, lse = _attn_fwd_call(qn, kn, v, bias)
    wout, wup, wdown = mlp_weights(o)
    mixed, c, ff, dz, dy, dh1, dypool, do, delta, dg2, loss = _f2_call(x, tgt, ypool, o, wout, wup, wdown, g2r, tm=256)
    dff = ff.shape[1]
    g_out = [g.reshape(N_CHIPS, d // N_CHIPS, d)
             for g in _wgrad_call(mixed, dh1, d, d, bk // 4, (d, d), (d, d), lambda i, j, k: (0, 0), "wgrad_out")]
    g_up = _wgrad_call(c, dz, d, dff // N_CHIPS, bk, (N_CHIPS, d, dff // N_CHIPS), (None, d, dff // N_CHIPS),
                       lambda i, j, k: (j, 0, 0), "wgrad_up")
    g_down = _wgrad_call(ff, dy, dff // N_CHIPS, d, bk, (N_CHIPS, dff // N_CHIPS, d), (None, dff // N_CHIPS, d),
                         lambda i, j, k: (i, 0, 0), "wgrad_down")
    dep = None if on_mlp_grads is None else on_mlp_grads(g_out[1], g_up[1], g_down[1])
    dqn, dkn, dv, dbias = _attn_bwd_call(qn, kn, v, do, lse, delta, bias, dep)
    dx, dproj, dg1, dqg, dkg, dpw, dps = _bproj_call(
        dqn, dkn, dv, q32, k32, dypool, pooled, x, dh1, win, pw_c, psr, qgr, kgr, g1r, tm=512)
    nin = dproj.shape[1] // N_CHIPS
    g_in = _wgrad_call(a, dproj, d, nin, bk, (N_CHIPS, d, nin), (None, d, nin), lambda i, j, k: (j, 0, 0), "wgrad_in")
    drb = _rel_bias_grad_call(dbias, buckets)
    small = dict(
        mix_norm_g=dg1.reshape(d), mlp_norm_g=dg2.reshape(d), pool_scale=dps.reshape(POOL_WIDTH),
        q_norm_g=dqg.reshape(ATTN_WIDTH), k_norm_g=dkg.reshape(ATTN_WIDTH),
        rel_bias=drb[:, :N_BUCKETS].T, pool_w=dpw)
    return loss[0, 0], dx, (g_in, g_out, g_up, g_down), small


def _coords():
    return lax.axis_index("x"), lax.axis_index("y"), lax.axis_index("c")


def _other_chips(x, y):
    return [(1 - x, y), (x, 1 - y), (1 - x, 1 - y)]


def _remote(src, dst, send_sem, recv_sem, dev):
    return pltpu.make_async_remote_copy(src_ref=src, dst_ref=dst, send_sem=send_sem, recv_sem=recv_sem,
                                        device_id=dev, device_id_type=MESH)


PAIR_FORWARD_ID = 1
PAIR_ALLGATHER_ID = 2


def _sibling_handshake():
    x, y, c = _coords()
    barrier = pltpu.get_barrier_semaphore()
    pl.semaphore_signal(barrier, inc=1, device_id=(x, y, 1 - c), device_id_type=MESH)
    pl.semaphore_wait(barrier, 1)


def _halves(a):
    return a.reshape(a.shape[:-2] + (2, a.shape[-2] // 2, a.shape[-1]))


def _place_shards_call(shards, chip_idx, nch):
    nw = len(shards)

    def body(chip_ref, *refs):
        for w in range(nw):
            refs[nw + w][...] = refs[w][...].astype(WIRE_DTYPE)

    in_specs = [pl.BlockSpec((s.shape[0] // nch, s.shape[1]), lambda i, chip_ref: (i, 0)) for s in shards]
    out_specs = [pl.BlockSpec((None, s.shape[0] // nch, s.shape[1]), lambda i, chip_ref: (chip_ref[0], i, 0))
                 for s in shards]
    return pl.pallas_call(
        body, name="weights_place",
        grid_spec=pltpu.PrefetchScalarGridSpec(num_scalar_prefetch=1, grid=(nch,),
                                               in_specs=in_specs, out_specs=out_specs),
        out_shape=[jax.ShapeDtypeStruct((N_CHIPS,) + s.shape, WIRE_DTYPE) for s in shards],
        compiler_params=_params(("arbitrary",)),
    )(chip_idx, *shards)


def _allgather_call(placed, from_chips, name, meanwhile=None):
    nw = len(placed)
    ncp = 3 * nw
    extra, extra_specs, extra_shape, extra_body = meanwhile if meanwhile else ([], [], None, None)
    ne = len(extra)

    def body(*refs):
        outs = refs[nw + ne:2 * nw + ne]
        send1, recv1, send2, recv2 = refs[-4:]
        x, y, c = _coords()
        chip = 2 * x + y
        others = _other_chips(x, y)
        first, passed = [], []
        if not from_chips:
            _sibling_handshake()
        if from_chips:
            for w in range(nw):
                for k, (ox, oy) in enumerate(others):
                    mine = outs[w].at[chip, c]
                    cp = _remote(mine, mine, send1.at[3 * w + k], recv1.at[3 * w + k], (ox, oy, c))
                    cp.start()
                    first.append(cp)
        if meanwhile:
            extra_body(*refs[nw:nw + ne], refs[2 * nw + ne])
        for w in range(nw):
            for k, (ox, oy) in enumerate(others):
                piece = outs[w].at[2 * ox + oy, c]
                if from_chips:
                    _remote(piece, piece, send1.at[3 * w + k], recv1.at[3 * w + k], (ox, oy, c)).wait_recv()
                cp = _remote(piece, piece, send2.at[3 * w + k], recv2.at[3 * w + k], (x, y, 1 - c))
                cp.start()
                passed.append(cp)
        for w in range(nw):
            for k, (ox, oy) in enumerate(others):
                piece = outs[w].at[2 * ox + oy, 1 - c]
                _remote(piece, piece, send2.at[3 * w + k], recv2.at[3 * w + k], (x, y, 1 - c)).wait_recv()
        for cp in first + passed:
            cp.wait_send()

    return pl.pallas_call(
        body, name=name,
        in_specs=[ANY] * nw + list(extra_specs),
        out_specs=[ANY] * nw + ([pl.BlockSpec(memory_space=pltpu.VMEM)] if meanwhile else []),
        out_shape=[jax.ShapeDtypeStruct(s.shape, s.dtype) for s in placed] + ([extra_shape] if meanwhile else []),
        input_output_aliases={w: w for w in range(nw)},
        scratch_shapes=[pltpu.SemaphoreType.DMA((ncp,))] * 4,
        compiler_params=_params() if from_chips else _params(collective_id=PAIR_FORWARD_ID),
    )(*placed, *extra)


HBM_SPEC = pl.BlockSpec(memory_space=pltpu.HBM)
SEM_SPEC = pl.BlockSpec(memory_space=pltpu.SEMAPHORE)
SPLIT_EFFECT = pltpu.SideEffectType.DATAFLOW_SIDE_EFFECTING


def _in_hbm(a):
    return pltpu.with_memory_space_constraint(a, pltpu.HBM)


def _gather_copies(bufs, send, recv):
    x, y, c = _coords()
    chip = 2 * x + y
    cps = []
    for w, buf in enumerate(bufs):
        for k, (ox, oy) in enumerate(_other_chips(x, y)):
            mine, theirs = buf.at[chip, c], buf.at[2 * ox + oy, c]
            sems = (send.at[3 * w + k], recv.at[3 * w + k], (ox, oy, c))
            cps.append((_remote(mine, mine, *sems), _remote(theirs, theirs, *sems)))
    return cps


def _gather_start_call(bufs, after):
    nw = len(bufs)

    def body(*refs):
        ins, send, recv, token = refs[:nw], refs[nw + 1], refs[nw + 2], refs[2 * nw + 3]
        for out, _ in _gather_copies(ins, send, recv):
            out.start()
        token[...] = jnp.zeros(token.shape, F32)

    res = pl.pallas_call(
        body, name="weights_gather_start",
        in_specs=[HBM_SPEC] * nw + [ANY],
        out_specs=[SEM_SPEC, SEM_SPEC] + [HBM_SPEC] * nw + [pl.BlockSpec(memory_space=pltpu.VMEM)],
        out_shape=[pltpu.SemaphoreType.DMA((3 * nw,)), pltpu.SemaphoreType.DMA((3 * nw,))]
        + [pltpu.HBM(b.shape, b.dtype) for b in bufs] + [jax.ShapeDtypeStruct((8, LANES), F32)],
        input_output_aliases={w: 2 + w for w in range(nw)},
        compiler_params=pltpu.CompilerParams(has_side_effects=SPLIT_EFFECT),
    )(*[_in_hbm(b) for b in bufs], after)
    return res[0], res[1], list(res[2:2 + nw]), res[2 + nw]


def _gather_wait_call(bufs, send, recv, after):
    nw = len(bufs)

    def body(*refs):
        ins, send, recv = refs[:nw], refs[nw], refs[nw + 1]
        for out, back in _gather_copies(ins, send, recv):
            out.wait_send()
            back.wait_recv()

    return pl.pallas_call(
        body, name="weights_gather_wait",
        in_specs=[HBM_SPEC] * nw + [SEM_SPEC, SEM_SPEC, ANY],
        out_specs=[HBM_SPEC] * nw,
        out_shape=[pltpu.HBM(b.shape, b.dtype) for b in bufs],
        input_output_aliases={w: w for w in range(nw)},
        compiler_params=pltpu.CompilerParams(has_side_effects=SPLIT_EFFECT),
    )(*bufs, send, recv, after)


def _scatter_copies(srcs, lands, send, recv, wholes):
    x, y, c = _coords()
    me = 4 * x + 2 * y + c
    cps = []
    for w, (src, land) in enumerate(zip(srcs, lands)):
        for r in range(1, N_DEV):
            px, py, pc = ((1 - x) if r & 4 else x, (1 - y) if r & 2 else y, (1 - c) if r & 1 else c)
            sems = (send.at[(N_DEV - 1) * w + r - 1], recv.at[(N_DEV - 1) * w + r - 1], (px, py, pc))
            piece = src if wholes[w] else src.at[2 * px + py, pc]
            cps.append((_remote(piece, land.at[me], *sems), _remote(piece, land.at[4 * px + 2 * py + pc], *sems)))
    return cps


def _scatter_start_call(srcs, lands, wholes, name):
    nw = len(srcs)
    ncp = (N_DEV - 1) * nw

    def body(*refs):
        ins, lnd, send, recv, token = refs[:nw], refs[nw:2 * nw], refs[2 * nw], refs[2 * nw + 1], refs[4 * nw + 2]
        for out, _ in _scatter_copies(ins, lnd, send, recv, wholes):
            out.start()
        token[...] = jnp.zeros(token.shape, F32)

    res = pl.pallas_call(
        body, name=name,
        in_specs=[HBM_SPEC] * (2 * nw),
        out_specs=[SEM_SPEC, SEM_SPEC] + [HBM_SPEC] * (2 * nw) + [pl.BlockSpec(memory_space=pltpu.VMEM)],
        out_shape=[pltpu.SemaphoreType.DMA((ncp,)), pltpu.SemaphoreType.DMA((ncp,))]
        + [pltpu.HBM(b.shape, b.dtype) for b in list(srcs) + list(lands)] + [jax.ShapeDtypeStruct((8, LANES), F32)],
        input_output_aliases={i: 2 + i for i in range(2 * nw)},
        compiler_params=pltpu.CompilerParams(has_side_effects=SPLIT_EFFECT),
    )(*[_in_hbm(b) for b in list(srcs) + list(lands)])
    return res[0], res[1], list(res[2:2 + nw]), list(res[2 + nw:2 + 2 * nw]), res[2 + 2 * nw]


def _scatter_wait_call(srcs, lands, send, recv, after, wholes, name):
    nw = len(srcs)

    def body(*refs):
        ins, lnd, send, recv = refs[:nw], refs[nw:2 * nw], refs[2 * nw], refs[2 * nw + 1]
        for out, back in _scatter_copies(ins, lnd, send, recv, wholes):
            out.wait_send()
            back.wait_recv()

    res = pl.pallas_call(
        body, name=name,
        in_specs=[HBM_SPEC] * (2 * nw) + [SEM_SPEC, SEM_SPEC, ANY],
        out_specs=[HBM_SPEC] * (2 * nw),
        out_shape=[pltpu.HBM(b.shape, b.dtype) for b in list(srcs) + list(lands)],
        input_output_aliases={i: i for i in range(2 * nw)},
        compiler_params=pltpu.CompilerParams(has_side_effects=SPLIT_EFFECT),
    )(*srcs, *lands, send, recv, after)
    return list(res[nw:])


def _reduce_call(own, lands, idx, nch, name, dep=None):
    nw = len(own)
    deps = [] if dep is None else [dep]

    def body(idx_ref, *refs):
        refs = refs[:2 * nw] + refs[2 * nw + len(deps):]
        for w in range(nw):
            tot = refs[w][...]
            for r in range(1, N_DEV):
                tot = tot + refs[nw + w][idx_ref[1 + r]].astype(F32)
            refs[2 * nw + w][...] = tot

    in_specs, out_specs, out_shape = [], [], []
    for s in own:
        in_specs.append(pl.BlockSpec((None, None, s.shape[2] // nch, s.shape[3]),
                                     lambda i, idx_ref: (idx_ref[0], idx_ref[1], i, 0)))
    for s in own:
        in_specs.append(pl.BlockSpec((N_DEV, s.shape[2] // nch, s.shape[3]), lambda i, idx_ref: (0, i, 0)))
    for s in own:
        out_specs.append(pl.BlockSpec((None, s.shape[2] // nch, s.shape[3]), lambda i, idx_ref: (idx_ref[1], i, 0)))
        out_shape.append(jax.ShapeDtypeStruct((2,) + s.shape[2:], F32))
    return pl.pallas_call(
        body, name=name,
        grid_spec=pltpu.PrefetchScalarGridSpec(num_scalar_prefetch=1, grid=(nch,),
                                               in_specs=in_specs + [ANY] * len(deps), out_specs=out_specs),
        out_shape=out_shape,
        compiler_params=_params(("arbitrary",)),
    )(idx, *own, *lands, *deps)


def _pair_allgather_call(halves, name):
    nw = len(halves)

    def body(*refs):
        outs = refs[nw:2 * nw]
        send, recv = refs[2 * nw:]
        x, y, c = _coords()
        _sibling_handshake()
        cps = []
        for w in range(nw):
            cp = _remote(outs[w].at[c], outs[w].at[c], send.at[w], recv.at[w], (x, y, 1 - c))
            cp.start()
            cps.append(cp)
        for w in range(nw):
            theirs = outs[w].at[1 - c]
            _remote(theirs, theirs, send.at[w], recv.at[w], (x, y, 1 - c)).wait_recv()
        for cp in cps:
            cp.wait_send()

    outs = pl.pallas_call(
        body, name=name,
        in_specs=[ANY] * nw, out_specs=[ANY] * nw,
        out_shape=[jax.ShapeDtypeStruct(h.shape, h.dtype) for h in halves],
        input_output_aliases={w: w for w in range(nw)},
        scratch_shapes=[pltpu.SemaphoreType.DMA((nw,))] * 2,
        compiler_params=pltpu.CompilerParams(collective_id=PAIR_ALLGATHER_ID),
    )(*halves)
    return [o.reshape(2 * h.shape[1], h.shape[2]) for o, h in zip(outs, halves)]


def _adamw(w, g, m, v):
    m = ADAM_B1 * m + (1.0 - ADAM_B1) * g
    v = ADAM_B2 * v + (1.0 - ADAM_B2) * (g * g)
    m_hat = m / (1.0 - ADAM_B1 ** ADAM_STEP)
    v_hat = v / (1.0 - ADAM_B2 ** ADAM_STEP)
    delta = -ADAM_LR * (m_hat / (jnp.sqrt(v_hat) + ADAM_EPS) + ADAM_WD * w)
    return delta, m, v


def _adamw_call(ws, gs, ms, vs, nch, name):
    nw = len(ws)

    def body(*refs):
        for w in range(nw):
            g = refs[nw + w][...]
            delta, m, v = _adamw(refs[w][...], g, refs[2 * nw + w][...], refs[3 * nw + w][...])
            refs[4 * nw + w][...] = g
            refs[5 * nw + w][...] = delta
            refs[6 * nw + w][...] = m
            refs[7 * nw + w][...] = v

    specs = [pl.BlockSpec((a.shape[0] // nch, a.shape[1]), lambda i: (i, 0)) for a in ws]
    res = pl.pallas_call(
        body, name=name,
        grid=(nch,),
        in_specs=specs * 4, out_specs=specs * 4,
        out_shape=[jax.ShapeDtypeStruct(a.shape, F32) for a in ws] * 4,
        compiler_params=_params(("arbitrary",)),
    )(*ws, *gs, *ms, *vs)
    return res[:nw], res[nw:2 * nw], res[2 * nw:3 * nw], res[3 * nw:]


def _small_call(gathered, own, me_idx, w, m, v):
    cuts = dict(mix_norm_g=(0, 0, 1024), mlp_norm_g=(1, 0, 1024), pool_scale=(2, 0, POOL_WIDTH),
                rel_bias=(2, POOL_WIDTH, N_BUCKETS * N_HEADS), q_norm_g=(3, 0, HEAD_DIM), k_norm_g=(3, LANES, HEAD_DIM))
    n_out = len(cuts) + 1

    def fold(row):
        tot = row[:, 0:LANES] + row[:, LANES:2 * LANES] + row[:, 2 * LANES:3 * LANES] + row[:, 3 * LANES:4 * LANES]
        return tot + pltpu.roll(tot, HEAD_DIM, axis=1)

    def body(me_ref, gh_ref, gp_ref, oh_ref, op_ref, wh, wp, mh, mp, vh, vp, loss_ref, *outs):
        me = me_ref[0]

        def total(ga_ref, own_ref):
            term = lambda i: jnp.where(me == i, own_ref[...], ga_ref[i]).astype(F32)
            tot = term(0)
            for i in range(1, N_DEV):
                tot = tot + term(i)
            return tot

        g_head, g_pool = total(gh_ref, oh_ref), total(gp_ref, op_ref)
        unfolded = g_head[4:5, :]
        folded = jnp.concatenate([fold(unfolded[:, :ATTN_WIDTH]), fold(unfolded[:, ATTN_WIDTH:]),
                                  jnp.zeros((1, 1024 - 2 * LANES), F32)], axis=-1)
        row = lax.broadcasted_iota(jnp.int32, g_head.shape, 0)
        g_head = jnp.where(row == 3, folded, g_head)
        loss_ref[...] = g_head[LOSS_ROW:LOSS_ROW + 1, 0:LANES]
        heads = (g_head,) + _adamw(wh[...], g_head, mh[...], vh[...])
        pools = (g_pool,) + _adamw(wp[...], g_pool, mp[...], vp[...])
        for kind in range(4):
            mine = outs[kind * n_out:(kind + 1) * n_out]
            for out, (row, at, n) in zip(mine, cuts.values()):
                out[...] = heads[kind][row:row + 1, at:at + n]
            mine[-1][...] = pools[kind]

    vmem = pl.BlockSpec(memory_space=pltpu.VMEM)
    shapes = [jax.ShapeDtypeStruct((1, n), F32) for _, _, n in cuts.values()] + [jax.ShapeDtypeStruct(w[1].shape, F32)]
    res = pl.pallas_call(
        body, name="adamw_small",
        in_specs=[pl.BlockSpec(memory_space=pltpu.SMEM)] + [vmem] * 10,
        out_shape=[jax.ShapeDtypeStruct((1, LANES), F32)] + shapes * 4,
        compiler_params=_params(),
    )(me_idx, *gathered, *own, *w, *m, *v)

    def unpack(mine):
        p = {n: a.reshape(-1) for n, a in zip(cuts, mine)}
        p["rel_bias"] = p["rel_bias"].reshape(N_BUCKETS, N_HEADS)
        p["pool_w"] = mine[-1].reshape(len(POOL_WINDOWS), LANES, LANES)
        return p

    return [res[0]] + [unpack(res[1 + kind * n_out:1 + (kind + 1) * n_out]) for kind in range(4)]


def _pack_small(p, folded=True, loss=None):
    z = lambda n: jnp.zeros((n,), F32)
    rows = [p["mix_norm_g"], p["mlp_norm_g"],
            jnp.concatenate([p["pool_scale"], p["rel_bias"].reshape(-1), z(1024 - POOL_WIDTH - N_BUCKETS * N_HEADS)])]
    if folded:
        rows += [jnp.concatenate([p["q_norm_g"], z(LANES - HEAD_DIM), p["k_norm_g"], z(1024 - LANES - HEAD_DIM)]), z(1024)]
    else:
        rows += [z(1024), jnp.concatenate([p["q_norm_g"], p["k_norm_g"]])]
    rows += [z(1024) if loss is None else jnp.concatenate([loss.reshape(1), z(1023)])]
    return jnp.stack(rows + [z(1024)] * 2), p["pool_w"].reshape(-1, LANES)


_WEIGHT_ORDER = ("mix_norm_g", "w_in", "pool_w", "pool_scale", "q_norm_g", "k_norm_g", "rel_bias", "w_out",
                 "mlp_norm_g", "w_up", "w_down")
_BIG = ("w_in", "w_out", "w_up", "w_down")


def kernel(x, mix_norm_g, w_in, pool_w, pool_scale, q_norm_g, k_norm_g, rel_bias, w_out, mlp_norm_g, w_up, w_down, loss_target, m_mix_norm_g, m_w_in, m_pool_w, m_pool_scale, m_q_norm_g, m_k_norm_g, m_rel_bias, m_w_out, m_mlp_norm_g, m_w_up, m_w_down, v_mix_norm_g, v_w_in, v_pool_w, v_pool_scale, v_q_norm_g, v_k_norm_g, v_rel_bias, v_w_out, v_mlp_norm_g, v_w_up, v_w_down):
    w = dict(mix_norm_g=mix_norm_g, w_in=w_in, pool_w=pool_w, pool_scale=pool_scale, q_norm_g=q_norm_g,
             k_norm_g=k_norm_g, rel_bias=rel_bias, w_out=w_out, mlp_norm_g=mlp_norm_g, w_up=w_up, w_down=w_down)
    m = dict(mix_norm_g=m_mix_norm_g, w_in=m_w_in, pool_w=m_pool_w, pool_scale=m_pool_scale, q_norm_g=m_q_norm_g,
             k_norm_g=m_k_norm_g, rel_bias=m_rel_bias, w_out=m_w_out, mlp_norm_g=m_mlp_norm_g, w_up=m_w_up, w_down=m_w_down)
    v = dict(mix_norm_g=v_mix_norm_g, w_in=v_w_in, pool_w=v_pool_w, pool_scale=v_pool_scale, q_norm_g=v_q_norm_g,
             k_norm_g=v_k_norm_g, rel_bias=v_rel_bias, w_out=v_w_out, mlp_norm_g=v_mlp_norm_g, w_up=v_w_up, w_down=v_w_down)
    xc, yc, cc = _coords()

    c_idx = jnp.reshape(cc, (1,)).astype(jnp.int32)
    chip_idx = jnp.reshape(2 * xc + yc, (1,)).astype(jnp.int32)
    me = 4 * xc + 2 * yc + cc
    whole = lambda t: t.reshape(t.shape[0], t.shape[1] * t.shape[2], t.shape[3])

    placed = [_halves(p) for p in _place_shards_call([w[n] for n in _BIG], chip_idx, nch=4)]
    win_f, bias = _allgather_call(placed[:1], from_chips=True, name="weights_allgather_in",
                                  meanwhile=_bias_table_work(rel_bias))
    wsend, wrecv, in_flight, started = _gather_start_call(placed[1:], win_f)

    def mlp_weights(after):
        landed = _gather_wait_call(in_flight, wsend, wrecv, after)
        wout_f, wup_f, wdown_f = _allgather_call(landed, from_chips=False, name="weights_pair_forward")
        return whole(wout_f).reshape(-1, wout_f.shape[-1]), whole(wup_f), whole(wdown_f)

    split = []

    def on_mlp_grads(*wire_grads):
        srcs = [_halves(g) for g in wire_grads]
        lands = [lax.empty((N_DEV,) + s.shape[2:], s.dtype) for s in srcs]
        split.extend(_scatter_start_call(srcs, lands, [False] * len(srcs), "grads_scatter_start"))
        return split[4]

    loss_part, dx, big_grads, small_grads = _local_grads(
        x[0], loss_target[0], mix_norm_g + started[0, 0], whole(win_f), pool_w, pool_scale, q_norm_g, k_norm_g, bias,
        mlp_norm_g, mlp_weights, on_mlp_grads)
    g_in, g_out, g_up, g_down = big_grads
    gsend, grecv, srcs_thru, lands_thru, _ = split
    lands_mlp = _scatter_wait_call(srcs_thru, lands_thru, gsend, grecv, g_in[1], [False] * 3, "grads_scatter_wait")

    head_own, pool_own = _pack_small(small_grads, folded=False, loss=loss_part)
    small_own = (head_own, pool_own.astype(WIRE_DTYPE))
    last_srcs = [_halves(g_in[1]), *small_own]
    last_lands = [lax.empty((N_DEV,) + last_srcs[0].shape[2:], WIRE_DTYPE)]
    last_lands += [lax.empty((N_DEV,) + a.shape, a.dtype) for a in small_own]
    lsend, lrecv, last_srcs, last_lands, last_started = _scatter_start_call(
        last_srcs, last_lands, [False, True, True], "grads_scatter_start_last")
    idx = jnp.concatenate([chip_idx, c_idx] + [jnp.reshape(jnp.bitwise_xor(me, r), (1,)) for r in range(1, N_DEV)])
    idx = idx.astype(jnp.int32)
    mlp = _BIG[1:]

    def update(names, own32, lands, tag, dep=None):
        halves = _reduce_call([_halves(g) for g in own32], lands, idx, 4, "grads_reduce_" + tag, dep)
        reduced = _pair_allgather_call(list(halves), "grads_pair_allgather_" + tag)
        return _adamw_call([w[n] for n in names], reduced, [m[n] for n in names], [v[n] for n in names], 8, "adamw_" + tag)

    out_mlp = update(mlp, [g_out[0], g_up[0], g_down[0]], lands_mlp, "mlp", last_started)
    land_in, *small_all = _scatter_wait_call(last_srcs, last_lands, lsend, lrecv, out_mlp[3][-1], [False, True, True],
                                             "grads_scatter_wait_last")
    out_in = update(_BIG[:1], [g_in[0]], [land_in], "in")
    loss_row, grads, deltas, new_m, new_v = _small_call(
        small_all, small_own, jnp.reshape(me, (1,)).astype(jnp.int32), _pack_small(w), _pack_small(m), _pack_small(v))

    for k, res in enumerate((grads, deltas, new_m, new_v)):
        res[_BIG[0]] = out_in[k][0]
        for i, n in enumerate(mlp):
            res[n] = out_mlp[k][i]
    loss = loss_row[0, 0]
    return (loss, dx[None], *[grads[n] for n in _WEIGHT_ORDER], *[deltas[n] for n in _WEIGHT_ORDER],
            *[new_m[n] for n in _WEIGHT_ORDER], *[new_v[n] for n in _WEIGHT_ORDER])
```

```python
import math

import jax
import jax.numpy as jnp
import numpy as np
from jax import lax
from jax.experimental import pallas as pl
from jax.experimental.pallas import tpu as pltpu

F32 = jnp.float32
MXU_DTYPE = jnp.bfloat16
WIRE_DTYPE = jnp.bfloat16

NORM_EPS = 1e-6
NEG_INF = -1e30
LANES = 128
HEAD_DIM = 64
N_HEADS = 8
POOL_WIDTH = 512
ATTN_WIDTH = 512
POOL_WINDOWS = (2, 4, 8, 16)
POOL_HALO = 16
DILATED_PATTERNS = ((128, 1), (512, 4), (2048, 16))
ATT_BLOCK = 128
ATT_SUPER = ATT_BLOCK * max(dl for _, dl in DILATED_PATTERNS)
ATT_UNITS = ATT_SUPER // ATT_BLOCK
N_BUCKETS = 32
NO_BUCKET = -1
MAX_DISTANCE = 2048
N_CHIPS = 4
N_DEV = 8
ADAM_LR, ADAM_B1, ADAM_B2, ADAM_EPS, ADAM_WD, ADAM_STEP = 0.001, 0.9, 0.999, 1e-08, 0.01, 10
VMEM_LIMIT = 56 * 1024 * 1024
MESH = pl.DeviceIdType.MESH
ANY = pl.BlockSpec(memory_space=pl.ANY)

LOSS_ROW = 5


def _mm(a, b):
    return jnp.dot(a, b, preferred_element_type=F32)


def _mm_nt(a, b):
    return lax.dot_general(a, b, (((1,), (1,)), ((), ())), preferred_element_type=F32)


def _mm_tn(a, b):
    return lax.dot_general(a, b, (((0,), (0,)), ((), ())), preferred_element_type=F32)


def _params(sem=None, **kw):
    if sem is not None:
        kw["dimension_semantics"] = sem
    return pltpu.CompilerParams(vmem_limit_bytes=VMEM_LIMIT, **kw)


def _low_half():
    return lax.broadcasted_iota(jnp.int32, (1, LANES), 1) < HEAD_DIM


def _head_sum_bcast(y):
    lo = _low_half()
    outs = []
    for j in range(y.shape[1] // LANES):
        c = y[:, j * LANES:(j + 1) * LANES]
        s_lo = jnp.sum(jnp.where(lo, c, 0.0), axis=-1, keepdims=True)
        s_hi = jnp.sum(jnp.where(lo, 0.0, c), axis=-1, keepdims=True)
        outs.append(jnp.where(lo, s_lo, s_hi))
    return jnp.concatenate(outs, axis=-1)


def _rms_bwd(dn, hn, r):
    return r * (dn - hn * jnp.mean(dn * hn, axis=-1, keepdims=True))


def _t5_bucket_np(dist):
    max_exact = N_BUCKETS // 2
    d_f = np.maximum(dist, 1).astype(np.float32)
    ratio = (np.log(d_f / np.float32(max_exact)) / np.float32(math.log(MAX_DISTANCE / max_exact))).astype(np.float32)
    large = max_exact + (ratio * np.float32(N_BUCKETS - max_exact)).astype(np.int32)
    large = np.minimum(large, N_BUCKETS - 1)
    return np.where(dist < max_exact, dist, large).astype(np.int32)


def _window_offsets(dl):
    if dl == 1:
        return _by4_positions(ATT_BLOCK), _by4_positions(2 * ATT_BLOCK)
    return np.arange(ATT_BLOCK), np.arange(2 * ATT_BLOCK)


def _bucket_tables():
    tables = []
    for _, dl in DILATED_PATTERNS:
        qq, kk = _window_offsets(dl)
        dist = qq[:, None] + ATT_BLOCK - kk[None, :]
        bucket = _t5_bucket_np(np.clip(dist, 0, ATT_BLOCK) * dl)
        tables.append(np.where((dist >= 0) & (dist <= ATT_BLOCK), bucket, NO_BUCKET))
    return np.stack(tables).astype(np.int32)


def _previous_block_keys():
    return np.stack([np.broadcast_to(_window_offsets(dl)[1][None, :] < ATT_BLOCK, (ATT_BLOCK, 2 * ATT_BLOCK))
                     for _, dl in DILATED_PATTERNS])


def _f1_call(x, g1, win, poolw, pscale, qg, kg, tm, dep=None):
    s, d = x.shape
    nblk = s // tm
    deps = [] if dep is None else [dep]

    def body(x_ref, g1_ref, win_ref, pw_ref, ps_ref, qg_ref, kg_ref, *rest):
        a_ref, pooled_ref, ypool_ref, q32_ref, k32_ref, qn_ref, kn_ref, v_ref, ubuf = rest[len(deps):]
        i = pl.program_id(0)
        xv = x_ref[...]
        r = lax.rsqrt(jnp.mean(xv * xv, axis=-1, keepdims=True) + NORM_EPS)
        a = ((xv * r) * g1_ref[...]).astype(MXU_DTYPE)
        a_ref[...] = a
        u = _mm(a, win_ref[0])
        q = _mm(a, win_ref[1])
        k = _mm(a, win_ref[2])
        v_ref[...] = _mm(a, win_ref[3])
        q32_ref[...] = q
        k32_ref[...] = k
        rq = lax.rsqrt(_head_sum_bcast(q * q) * (1.0 / HEAD_DIM) + NORM_EPS)
        qn_ref[...] = ((q * rq) * qg_ref[...]) * (HEAD_DIM ** -0.5)
        rk = lax.rsqrt(_head_sum_bcast(k * k) * (1.0 / HEAD_DIM) + NORM_EPS)
        kn_ref[...] = (k * rk) * kg_ref[...]

        ubuf[0:POOL_HALO, :] = jnp.where(i > 0, ubuf[tm:tm + POOL_HALO, :], 0.0)
        ubuf[POOL_HALO:POOL_HALO + tm, :] = u
        t = i * tm + lax.broadcasted_iota(jnp.int32, (tm, 1), 0)
        for g, w in enumerate(POOL_WINDOWS):
            ls = slice(g * LANES, (g + 1) * LANES)
            ug = u[:, ls]
            acc = ug
            for sh in range(1, w):
                acc = acc + ubuf[POOL_HALO - sh:POOL_HALO - sh + tm, ls]
            cnt = jnp.minimum(t + 1, w).astype(F32)
            pooled = (acc / cnt - ug).astype(MXU_DTYPE)
            pooled_ref[:, ls] = pooled
            ypool_ref[:, ls] = (_mm(pooled, pw_ref[g]) * ps_ref[:, ls]).astype(MXU_DTYPE)

    tok = lambda w: pl.BlockSpec((tm, w), lambda i: (i, 0))
    full = lambda shp: pl.BlockSpec(shp, lambda i: (0,) * len(shp))
    return pl.pallas_call(
        body, name="fwd_inproj",
        grid=(nblk,),
        in_specs=[tok(d), full((1, d)), full(win.shape), full(poolw.shape), full((1, POOL_WIDTH)),
                  full((1, ATTN_WIDTH)), full((1, ATTN_WIDTH))] + [ANY] * len(deps),
        out_specs=[tok(d), tok(POOL_WIDTH), tok(POOL_WIDTH), tok(ATTN_WIDTH), tok(ATTN_WIDTH),
                   tok(ATTN_WIDTH), tok(ATTN_WIDTH), tok(ATTN_WIDTH)],
        out_shape=[jax.ShapeDtypeStruct((s, d), MXU_DTYPE),
                   jax.ShapeDtypeStruct((s, POOL_WIDTH), MXU_DTYPE),
                   jax.ShapeDtypeStruct((s, POOL_WIDTH), MXU_DTYPE),
                   jax.ShapeDtypeStruct((s, ATTN_WIDTH), F32),
                   jax.ShapeDtypeStruct((s, ATTN_WIDTH), F32),
                   jax.ShapeDtypeStruct((s, ATTN_WIDTH), F32),
                   jax.ShapeDtypeStruct((s, ATTN_WIDTH), F32),
                   jax.ShapeDtypeStruct((s, ATTN_WIDTH), F32)],
        scratch_shapes=[pltpu.VMEM((tm + POOL_HALO, POOL_WIDTH), F32)],
        compiler_params=_params(("arbitrary",)),
    )(x, g1, win, poolw, pscale, qg, kg, *deps)


DEINT = 4
assert [dl for _, dl in DILATED_PATTERNS] == [1, DEINT, DEINT * DEINT]


def _by4_positions(n):
    pos = np.arange(n)
    return DEINT * (pos % (n // DEINT)) + pos // (n // DEINT)


def _masked_bias(b_ref, p, n):
    return b_ref[p, jnp.minimum(n, 1)].reshape(2 * ATT_BLOCK, 2 * ATT_BLOCK)


def _unit_rows(u, dl):
    assert isinstance(u, int)
    sq, sk = ATT_SUPER // DEINT, 2 * ATT_SUPER // DEINT
    if dl == 1:
        n = ATT_BLOCK // DEINT
        return (u, [pl.ds(r * sq + n * u, n) for r in range(DEINT)],
                [pl.ds(r * sk + sk // 2 + n * (u - 1), 2 * n) for r in range(DEINT)])
    if dl == DEINT:
        r, b = u % DEINT, u // DEINT
        return (b, [pl.ds(r * sq + ATT_BLOCK * b, ATT_BLOCK)],
                [pl.ds(r * sk + sk // 2 + ATT_BLOCK * (b - 1), 2 * ATT_BLOCK)])
    r, a = u % DEINT, u // DEINT
    return 0, [pl.ds(r * sq + a, ATT_BLOCK, stride=DEINT)], [pl.ds(r * sk + a, 2 * ATT_BLOCK, stride=DEINT)]


def _take(ref, runs):
    parts = [ref[run, :] for run in runs]
    return parts[0] if len(parts) == 1 else jnp.concatenate(parts, axis=0)


def _put(ref, runs, value, add=False):
    n = value.shape[0] // len(runs)
    for i, run in enumerate(runs):
        part = value[i * n:(i + 1) * n]
        ref[run, :] = ref[run, :] + part if add else part


def _deinterleave(dst, src, n):
    seg = n // DEINT
    for r in range(DEINT):
        dst[r * seg:(r + 1) * seg, :] = src[pl.ds(r, seg, stride=DEINT), :]


def _deinterleave_pair(dst, prev, cur):
    seg = prev.shape[0] // DEINT
    for r in range(DEINT):
        dst[2 * r * seg:(2 * r + 1) * seg, :] = prev[pl.ds(r, seg, stride=DEINT), :]
        dst[(2 * r + 1) * seg:(2 * r + 2) * seg, :] = cur[pl.ds(r, seg, stride=DEINT), :]


def _interleave(dst, src, n, offset=0):
    seg = n // DEINT
    stride = src.shape[0] // DEINT
    for r in range(DEINT):
        dst[pl.ds(r, seg, stride=DEINT), :] = src[r * stride + offset:r * stride + offset + seg, :]


def _attn_fwd_call(qn, kn, v, bias):
    s, w = qn.shape
    nsb = s // ATT_SUPER
    npair = w // LANES

    def body(q_ref, kc_ref, vc_ref, b_ref, o_ref, lse_ref, qf, kf, vf, acc_s, m_s, l_s):
        sb = pl.program_id(1)

        @pl.when((pl.program_id(0) == 0) & (sb == 0))
        def _():
            kf[...] = jnp.zeros_like(kf)
            vf[...] = jnp.zeros_like(vf)

        seg = ATT_SUPER // DEINT
        _deinterleave(qf, q_ref, ATT_SUPER)
        for r in range(DEINT):
            for dst, src in ((kf, kc_ref), (vf, vc_ref)):
                dst[2 * r * seg:(2 * r + 1) * seg, :] = dst[(2 * r + 1) * seg:(2 * r + 2) * seg, :]
                dst[(2 * r + 1) * seg:(2 * r + 2) * seg, :] = src[pl.ds(r, seg, stride=DEINT), :]
        lo = _low_half()
        for p, (_, dl) in enumerate(DILATED_PATTERNS):
            def unit(u, carry, p=p, dl=dl):
                b, rows_q, rows_k = _unit_rows(u, dl)
                qp = _take(qf, rows_q).astype(MXU_DTYPE)
                kcat = _take(kf, rows_k).astype(MXU_DTYPE)
                vcat = _take(vf, rows_k).astype(MXU_DTYPE)
                zero = jnp.zeros_like(qp)
                q2 = jnp.concatenate([jnp.where(lo, qp, zero), jnp.where(lo, zero, qp)], axis=0)
                sc = _mm_nt(q2, kcat) + _masked_bias(b_ref, p, sb * (ATT_UNITS // dl) + b)
                m2 = jnp.max(sc, axis=-1, keepdims=True)
                pr = jnp.exp(sc - m2)
                l2 = jnp.sum(pr, axis=-1, keepdims=True)
                acc2 = _mm(pr.astype(MXU_DTYPE), vcat)
                acc = jnp.where(lo, acc2[:ATT_BLOCK], acc2[ATT_BLOCK:])
                m = jnp.where(lo, m2[:ATT_BLOCK], m2[ATT_BLOCK:])
                l = jnp.where(lo, l2[:ATT_BLOCK], l2[ATT_BLOCK:])
                if p == 0:
                    _put(acc_s, rows_q, acc)
                    _put(m_s, rows_q, m)
                    _put(l_s, rows_q, l)
                else:
                    m_old = _take(m_s, rows_q)
                    m_new = jnp.maximum(m_old, m)
                    a_old = jnp.exp(m_old - m_new)
                    a_new = jnp.exp(m - m_new)
                    _put(acc_s, rows_q, a_old * _take(acc_s, rows_q) + a_new * acc)
                    _put(l_s, rows_q, a_old * _take(l_s, rows_q) + a_new * l)
                    _put(m_s, rows_q, m_new)
                return carry

            for u in range(ATT_UNITS):
                unit(u, None)
        l = l_s[...]
        acc_s[...] = acc_s[...] / l
        m_s[...] = m_s[...] + jnp.log(l)
        _interleave(o_ref, acc_s, ATT_SUPER)
        _interleave(lse_ref, m_s, ATT_SUPER)

    cur = pl.BlockSpec((ATT_SUPER, LANES), lambda j, t: (t, j))
    bspec = pl.BlockSpec((len(DILATED_PATTERNS), 2, 2, ATT_BLOCK, 2 * ATT_BLOCK), lambda j, t: (0, 0, j, 0, 0))
    return pl.pallas_call(
        body, name="attn_fwd",
        grid=(npair, nsb),
        in_specs=[cur, cur, cur, bspec],
        out_specs=[cur, cur],
        out_shape=[jax.ShapeDtypeStruct((s, w), F32), jax.ShapeDtypeStruct((s, w), F32)],
        scratch_shapes=[pltpu.VMEM((ATT_SUPER, LANES), F32), pltpu.VMEM((2 * ATT_SUPER, LANES), F32),
                        pltpu.VMEM((2 * ATT_SUPER, LANES), F32), pltpu.VMEM((ATT_SUPER, LANES), F32),
                        pltpu.VMEM((ATT_SUPER, LANES), F32), pltpu.VMEM((ATT_SUPER, LANES), F32)],
        compiler_params=_params(("arbitrary", "arbitrary")),
    )(qn, kn, v, bias)


def _attn_bwd_call(qn, kn, v, do, lse, delta, bias, dep=None):
    s, w = qn.shape
    nsb = s // ATT_SUPER
    npair = w // LANES
    deps = [] if dep is None else [dep]

    def body(q_ref, kc_ref, kp_ref, vc_ref, vp_ref, do_ref, lse_ref, dlt_ref, b_ref, *rest):
        dq_ref, dk_ref, dv_ref, db_ref, qf, kf, vf, dof, lsef, dltf, dqf, dkf, dvf = rest[len(deps):]
        step = pl.program_id(1)
        sb = nsb - 1 - step
        seg = ATT_SUPER // DEINT
        _deinterleave(qf, q_ref, ATT_SUPER)
        _deinterleave(dof, do_ref, ATT_SUPER)
        _deinterleave_pair(kf, kp_ref, kc_ref)
        _deinterleave_pair(vf, vp_ref, vc_ref)
        _deinterleave(lsef, lse_ref, ATT_SUPER)
        _deinterleave(dltf, dlt_ref, ATT_SUPER)

        db_ref[...] = jnp.where(step > 0, db_ref[...], 0.0)
        for acc in (dkf, dvf):
            for r in range(DEINT):
                this, before = pl.ds((2 * r + 1) * seg, seg), pl.ds(2 * r * seg, seg)
                acc[this, :] = jnp.where(step > 0, acc[before, :], 0.0)
                acc[before, :] = jnp.zeros((seg, LANES), F32)
        lo = _low_half()
        for p, (_, dl) in enumerate(DILATED_PATTERNS):
            def unit(u, carry, p=p, dl=dl):
                b, rows_q, rows_k = _unit_rows(u, dl)
                qp = _take(qf, rows_q).astype(MXU_DTYPE)
                dop = _take(dof, rows_q).astype(MXU_DTYPE)
                kcat = _take(kf, rows_k).astype(MXU_DTYPE)
                vcat = _take(vf, rows_k).astype(MXU_DTYPE)
                lse2 = _take(lsef, rows_q)
                dlt2 = _take(dltf, rows_q)
                zero = jnp.zeros_like(qp)
                q2 = jnp.concatenate([jnp.where(lo, qp, zero), jnp.where(lo, zero, qp)], axis=0)
                do2 = jnp.concatenate([jnp.where(lo, dop, zero), jnp.where(lo, zero, dop)], axis=0)
                lse_c = jnp.concatenate([lse2[:, 0:1], lse2[:, HEAD_DIM:HEAD_DIM + 1]], axis=0)
                dlt_c = jnp.concatenate([dlt2[:, 0:1], dlt2[:, HEAD_DIM:HEAD_DIM + 1]], axis=0)
                sc = _mm_nt(q2, kcat) + _masked_bias(b_ref, p, sb * (ATT_UNITS // dl) + b)
                pr = jnp.exp(sc - lse_c)
                ds = pr * (_mm_nt(do2, vcat) - dlt_c)
                db_ref[p] += ds.reshape(2, ATT_BLOCK, 2 * ATT_BLOCK)
                ds_c = ds.astype(MXU_DTYPE)
                dq2 = _mm(ds_c, kcat)
                dk = _mm_tn(ds_c, q2)
                dv = _mm_tn(pr.astype(MXU_DTYPE), do2)
                dq = jnp.where(lo, dq2[:ATT_BLOCK], dq2[ATT_BLOCK:])
                _put(dqf, rows_q, dq, add=p > 0)
                _put(dkf, rows_k, dk, add=True)
                _put(dvf, rows_k, dv, add=True)
                return carry

            for u in range(ATT_UNITS):
                unit(u, None)
        _interleave(dq_ref, dqf, ATT_SUPER)
        _interleave(dk_ref, dkf, ATT_SUPER, offset=seg)
        _interleave(dv_ref, dvf, ATT_SUPER, offset=seg)

    cur = pl.BlockSpec((ATT_SUPER, LANES), lambda j, t: (nsb - 1 - t, j))
    prev = pl.BlockSpec((ATT_SUPER, LANES), lambda j, t: (jnp.maximum(nsb - 2 - t, 0), j))
    npat = len(DILATED_PATTERNS)
    bspec = pl.BlockSpec((npat, 2, 2, ATT_BLOCK, 2 * ATT_BLOCK), lambda j, t: (0, 0, j, 0, 0))
    dbspec = pl.BlockSpec((npat, 2, ATT_BLOCK, 2 * ATT_BLOCK), lambda j, t: (0, j, 0, 0))
    sup = lambda: pltpu.VMEM((ATT_SUPER, LANES), F32)
    sup2 = lambda: pltpu.VMEM((2 * ATT_SUPER, LANES), F32)
    return pl.pallas_call(
        body, name="attn_bwd",
        grid=(npair, nsb),
        in_specs=[cur, cur, prev, cur, prev, cur, cur, cur, bspec] + [ANY] * len(deps),
        out_specs=[cur, cur, cur, dbspec],
        out_shape=[jax.ShapeDtypeStruct((s, w), F32)] * 3
        + [jax.ShapeDtypeStruct((npat, N_HEADS, ATT_BLOCK, 2 * ATT_BLOCK), F32)],
        scratch_shapes=[sup(), sup2(), sup2(), sup(), sup(), sup(), sup(), sup2(), sup2()],
        compiler_params=_params(("arbitrary", "arbitrary")),
    )(qn, kn, kn, v, v, do, lse, delta, bias, *deps)


def _bias_table_work(rel_bias):
    buckets = jnp.asarray(_bucket_tables())
    prev_keys = jnp.asarray(_previous_block_keys().astype(np.int32))
    npat = buckets.shape[0]

    def body(rb_ref, bk_ref, pk_ref, out_ref):
        for p in range(npat):
            for half in range(2):
                ks = slice(half * ATT_BLOCK, (half + 1) * ATT_BLOCK)
                bk = bk_ref[p, :, ks]
                absent = pk_ref[p, :, ks] != 0
                for h in range(N_HEADS):
                    def pick(b, acc, h=h, bk=bk):
                        return jnp.where(bk == b, rb_ref[b, h], acc)

                    tab = lax.fori_loop(0, N_BUCKETS, pick, jnp.full((ATT_BLOCK, ATT_BLOCK), NEG_INF, F32))
                    out_ref[p, 1, h, :, ks] = tab
                    out_ref[p, 0, h, :, ks] = jnp.where(absent, NEG_INF, tab)

    vmem = pl.BlockSpec(memory_space=pltpu.VMEM)
    return ([rel_bias, buckets, prev_keys], [pl.BlockSpec(memory_space=pltpu.SMEM), vmem, vmem],
            jax.ShapeDtypeStruct((npat, 2, N_HEADS, ATT_BLOCK, 2 * ATT_BLOCK), F32), body)


def _rel_bias_grad_call(dbias, buckets):
    npat, nh = dbias.shape[0], dbias.shape[1]

    def body(db_ref, bk_ref, out_ref):
        lane = lax.broadcasted_iota(jnp.int32, (nh, LANES), 1)
        out = jnp.zeros((nh, LANES), F32)
        for b in range(N_BUCKETS):
            tot = jnp.zeros((nh, 1), F32)
            for p in range(npat):
                hit = jnp.where(bk_ref[p][None] == b, db_ref[p], 0.0)
                tot = tot + jnp.sum(jnp.sum(hit, axis=1), axis=-1, keepdims=True)
            out = jnp.where(lane == b, tot, out)
        out_ref[...] = out

    return pl.pallas_call(
        body, name="rel_bias_grad",
        out_shape=jax.ShapeDtypeStruct((nh, LANES), F32),
        compiler_params=_params(),
    )(dbias, buckets)


def _f2_call(x, tgt, ypool, o, wout, wup, wdown, g2, tm):
    s, d = x.shape
    nblk = s // tm
    nch, _, fch = wup.shape
    dff = nch * fch
    mixw = POOL_WIDTH + ATTN_WIDTH

    def body(x_ref, t_ref, yp_ref, o_ref, g2_ref, wout_hbm, wup_hbm, wdown_hbm,
             mixed_ref, c_ref, ff_ref, dz_ref, dy_ref, dh1_ref, dyp_ref, do_ref, dlt_ref, dg2_ref, loss_ref,
             wout_v, wup_v, wdown_v, rz, wsem):
        i = pl.program_id(0)

        @pl.when(i == 0)
        def _():
            copies = [pltpu.make_async_copy(wout_hbm, wout_v, wsem.at[0])]
            for j in range(nch):
                copies.append(pltpu.make_async_copy(wup_hbm.at[j], wup_v.at[j], wsem.at[1 + 2 * j]))
                copies.append(pltpu.make_async_copy(wdown_hbm.at[j], wdown_v.at[j], wsem.at[2 + 2 * j]))
            for cp in copies:
                cp.start()
            dg2_ref[...] = jnp.zeros(dg2_ref.shape, F32)
            loss_ref[...] = jnp.zeros(loss_ref.shape, F32)
            for cp in copies:
                cp.wait()

        o = o_ref[...]
        mixed = jnp.concatenate([yp_ref[...], o.astype(MXU_DTYPE)], axis=-1)
        mixed_ref[...] = mixed
        h1 = x_ref[...] + _mm(mixed, wout_v[...])
        r2 = lax.rsqrt(jnp.mean(h1 * h1, axis=-1, keepdims=True) + NORM_EPS)
        hn = h1 * r2
        c = (hn * g2_ref[...]).astype(MXU_DTYPE)
        c_ref[...] = c
        y = h1
        for j in range(nch):
            cs = slice(j * fch, (j + 1) * fch)
            z = jnp.maximum(_mm(c, wup_v[j]), 0.0)
            rz[:, cs] = z
            ff = (z * z).astype(MXU_DTYPE)
            ff_ref[:, cs] = ff
            y = y + _mm(ff, wdown_v[j])
        err = y - t_ref[...]
        loss_ref[...] += jnp.sum(err * err) * (0.5 / d)
        dy = err * (1.0 / d)
        dy_c = dy.astype(MXU_DTYPE)
        dy_ref[...] = dy_c
        dc = jnp.zeros((tm, d), F32)
        for j in range(nch):
            cs = slice(j * fch, (j + 1) * fch)
            dz = (_mm_nt(dy_c, wdown_v[j]) * (2.0 * rz[:, cs])).astype(MXU_DTYPE)
            dz_ref[:, cs] = dz
            dc = dc + _mm_nt(dz, wup_v[j])
        dg2_ref[...] += jnp.sum(dc * hn, axis=0, keepdims=True)
        dh1 = dy + _rms_bwd(dc * g2_ref[...], hn, r2)
        dh1_ref[...] = dh1
        dmix = _mm_nt(dh1.astype(MXU_DTYPE), wout_v[...])
        dyp_ref[...] = dmix[:, :POOL_WIDTH]
        do = dmix[:, POOL_WIDTH:]
        do_ref[...] = do
        dlt_ref[...] = _head_sum_bcast(do * o)

    tok = lambda w: pl.BlockSpec((tm, w), lambda i: (i, 0))
    const = lambda shp: pl.BlockSpec(shp, lambda i: (0,) * len(shp))
    return pl.pallas_call(
        body, name="fwd_mlp_bwd_mlp",
        grid=(nblk,),
        in_specs=[tok(d), tok(d), tok(POOL_WIDTH), tok(ATTN_WIDTH), const((1, d)), ANY, ANY, ANY],
        out_specs=[tok(mixw), tok(d), tok(dff), tok(dff), tok(d), tok(d), tok(POOL_WIDTH), tok(ATTN_WIDTH),
                   tok(ATTN_WIDTH), const((1, d)), const((1, LANES))],
        out_shape=[jax.ShapeDtypeStruct((s, mixw), MXU_DTYPE),
                   jax.ShapeDtypeStruct((s, d), MXU_DTYPE),
                   jax.ShapeDtypeStruct((s, dff), MXU_DTYPE),
                   jax.ShapeDtypeStruct((s, dff), MXU_DTYPE),
                   jax.ShapeDtypeStruct((s, d), MXU_DTYPE),
                   jax.ShapeDtypeStruct((s, d), F32),
                   jax.ShapeDtypeStruct((s, POOL_WIDTH), F32),
                   jax.ShapeDtypeStruct((s, ATTN_WIDTH), F32),
                   jax.ShapeDtypeStruct((s, ATTN_WIDTH), F32),
                   jax.ShapeDtypeStruct((1, d), F32),
                   jax.ShapeDtypeStruct((1, LANES), F32)],
        scratch_shapes=[pltpu.VMEM(wout.shape, MXU_DTYPE), pltpu.VMEM(wup.shape, MXU_DTYPE),
                        pltpu.VMEM(wdown.shape, MXU_DTYPE), pltpu.VMEM((tm, dff), F32),
                        pltpu.SemaphoreType.DMA((1 + 2 * nch,))],
        compiler_params=_params(("arbitrary",)),
    )(x, tgt, ypool, o, g2, wout, wup, wdown)


RING_SLOTS = 3


def _bproj_call(dqn, dkn, dv, q32, k32, dypool, pooled, x, dh1, win, poolw, pscale, qg, kg, g1, tm):
    s, d = x.shape
    nblk = s // tm
    ngrp = len(POOL_WINDOWS)
    streams = [dqn, dkn, dv, q32, k32, dypool, pooled, x, dh1]
    ns = len(streams)
    assert nblk >= 2

    def body(*refs):
        hbm = refs[:ns]
        win_hbm, pw_ref, ps_ref, qg_ref, kg_ref, g1_ref = refs[ns:ns + 6]
        dx_ref, dproj_ref, dg1_ref, dqg_ref, dkg_ref, dpw_ref, dps_ref, win_v, ebuf = refs[ns + 6:ns + 15]
        rings, sems = refs[ns + 15:2 * ns + 15], refs[2 * ns + 15]
        step = pl.program_id(0)
        i = nblk - 1 - step

        def fetch(t):
            rows = pl.ds(pl.multiple_of((nblk - 1 - t) * tm, tm), tm)
            return [pltpu.make_async_copy(h.at[rows], ring.at[t % RING_SLOTS], sems.at[k, t % RING_SLOTS])
                    for k, (h, ring) in enumerate(zip(hbm, rings))]

        @pl.when(step == 0)
        def _():
            for t in range(2):
                for cp in fetch(t):
                    cp.start()
            pltpu.sync_copy(win_hbm, win_v)
            dg1_ref[...] = jnp.zeros(dg1_ref.shape, F32)
            dqg_ref[...] = jnp.zeros(dqg_ref.shape, F32)
            dkg_ref[...] = jnp.zeros(dkg_ref.shape, F32)
            dpw_ref[...] = jnp.zeros(dpw_ref.shape, F32)
            dps_ref[...] = jnp.zeros(dps_ref.shape, F32)
            ebuf[tm:tm + POOL_HALO, :] = jnp.zeros((POOL_HALO, POOL_WIDTH), F32)

        @pl.when(step > 0)
        def _():
            ebuf[tm:tm + POOL_HALO, :] = ebuf[0:POOL_HALO, :]

        @pl.when(step + 2 < nblk)
        def _():
            for cp in fetch(step + 2):
                cp.start()

        for cp in fetch(step):
            cp.wait()
        dqn_ref, dkn_ref, dv_ref, q_ref, k_ref, dyp_ref, pooled_ref, x_ref, dh1_ref = (
            ring.at[step % RING_SLOTS] for ring in rings)

        def qk_bwd(dn_sum, raw, gain, scale, dgain_ref):
            rr = lax.rsqrt(_head_sum_bcast(raw * raw) * (1.0 / HEAD_DIM) + NORM_EPS)
            hn = raw * rr
            dgain_ref[...] += jnp.sum(dn_sum * hn, axis=0, keepdims=True) * scale
            dn = dn_sum * (gain * scale)
            return rr * (dn - hn * (_head_sum_bcast(dn * hn) * (1.0 / HEAD_DIM)))

        dq = qk_bwd(dqn_ref[...], q_ref[...], qg_ref[...], HEAD_DIM ** -0.5, dqg_ref)
        dk = qk_bwd(dkn_ref[...], k_ref[...], kg_ref[...], 1.0, dkg_ref)

        t = i * tm + lax.broadcasted_iota(jnp.int32, (tm, 1), 0)
        dpooled = []
        for g, w in enumerate(POOL_WINDOWS):
            ls = slice(g * LANES, (g + 1) * LANES)
            dm = dyp_ref[:, ls]
            pg = pooled_ref[:, ls]
            dps_ref[:, ls] += jnp.sum(dm * _mm(pg, pw_ref[g]), axis=0, keepdims=True)
            dms = (dm * ps_ref[:, ls]).astype(MXU_DTYPE)
            dpw_ref[g] += _mm_tn(pg, dms)
            dpg = _mm_nt(dms, pw_ref[g])
            dpooled.append(dpg)
            ebuf[0:tm, ls] = dpg / jnp.minimum(t + 1, w).astype(F32)
        du = []
        for g, w in enumerate(POOL_WINDOWS):
            ls = slice(g * LANES, (g + 1) * LANES)
            acc = ebuf[0:tm, ls]
            for sh in range(1, w):
                acc = acc + ebuf[sh:sh + tm, ls]
            du.append(acc - dpooled[g])
        parts = [jnp.concatenate(du, axis=-1), dq, dk, dv_ref[...]]
        da = jnp.zeros((tm, d), F32)
        for p, part in enumerate(parts):
            pc = part.astype(MXU_DTYPE)
            dproj_ref[:, p * POOL_WIDTH:(p + 1) * POOL_WIDTH] = pc
            da = da + _mm_nt(pc, win_v[p])
        xv = x_ref[...]
        r = lax.rsqrt(jnp.mean(xv * xv, axis=-1, keepdims=True) + NORM_EPS)
        xn = xv * r
        dg1_ref[...] += jnp.sum(da * xn, axis=0, keepdims=True)
        dx_ref[...] = dh1_ref[...] + _rms_bwd(da * g1_ref[...], xn, r)

    tok = lambda w: pl.BlockSpec((tm, w), lambda t: (nblk - 1 - t, 0))
    const = lambda shp: pl.BlockSpec(shp, lambda t: (0,) * len(shp))
    return pl.pallas_call(
        body, name="bwd_inproj",
        grid=(nblk,),
        in_specs=[ANY] * (ns + 1) + [const(poolw.shape), const((1, POOL_WIDTH)), const((1, ATTN_WIDTH)),
                                     const((1, ATTN_WIDTH)), const((1, d))],
        out_specs=[tok(d), tok(4 * POOL_WIDTH), const((1, d)), const((1, ATTN_WIDTH)), const((1, ATTN_WIDTH)),
                   const((ngrp, LANES, LANES)), const((1, POOL_WIDTH))],
        out_shape=[jax.ShapeDtypeStruct((s, d), F32),
                   jax.ShapeDtypeStruct((s, 4 * POOL_WIDTH), MXU_DTYPE),
                   jax.ShapeDtypeStruct((1, d), F32),
                   jax.ShapeDtypeStruct((1, ATTN_WIDTH), F32),
                   jax.ShapeDtypeStruct((1, ATTN_WIDTH), F32),
                   jax.ShapeDtypeStruct((ngrp, LANES, LANES), F32),
                   jax.ShapeDtypeStruct((1, POOL_WIDTH), F32)],
        scratch_shapes=[pltpu.VMEM(win.shape, MXU_DTYPE), pltpu.VMEM((tm + POOL_HALO, POOL_WIDTH), F32)]
        + [pltpu.VMEM((RING_SLOTS, tm, a.shape[1]), a.dtype) for a in streams]
        + [pltpu.SemaphoreType.DMA((ns, RING_SLOTS))],
        compiler_params=_params(("arbitrary",)),
    )(*streams, win, poolw, pscale, qg, kg, g1)


def _wgrad_call(a, b, bm, bn, bk, out_shape, out_block, out_index, name):
    s, m = a.shape
    _, n = b.shape
    nk = s // bk

    def body(a_ref, b_ref, o_ref, wire_ref):
        k = pl.program_id(2)
        acc = jnp.where(k > 0, o_ref[...], 0.0) + _mm_tn(a_ref[...].astype(MXU_DTYPE), b_ref[...].astype(MXU_DTYPE))
        o_ref[...] = acc
        wire_ref[...] = acc.astype(WIRE_DTYPE)

    return pl.pallas_call(
        body, name=name,
        grid=(m // bm, n // bn, nk),
        in_specs=[pl.BlockSpec((bk, bm), lambda i, j, k: (k, i)), pl.BlockSpec((bk, bn), lambda i, j, k: (k, j))],
        out_specs=[pl.BlockSpec(out_block, out_index)] * 2,
        out_shape=[jax.ShapeDtypeStruct(out_shape, F32), jax.ShapeDtypeStruct(out_shape, WIRE_DTYPE)],
        compiler_params=_params(("arbitrary", "arbitrary", "arbitrary")),
    )(a, b)


def _local_grads(x, tgt, g1, win, poolw, pscale, qg, kg, bias, g2, mlp_weights, on_mlp_grads=None, first_dep=None):
    s, d = x.shape
    g1r, g2r = g1.reshape(1, d), g2.reshape(1, d)
    psr = pscale.reshape(1, POOL_WIDTH)
    qgr = jnp.tile(qg, N_HEADS).reshape(1, ATTN_WIDTH)
    kgr = jnp.tile(kg, N_HEADS).reshape(1, ATTN_WIDTH)
    pw_c = poolw.astype(MXU_DTYPE)
    buckets = jnp.asarray(_bucket_tables())
    bk = min(s, 4096)

    a, pooled, ypool, q32, k32, qn, kn, v = _f1_call(x, g1r, win, pw_c, psr, qgr, kgr, tm=1024, dep=first_dep)
    o, lse = _attn_fwd_call(qn, kn, v, bias)
    wout, wup, wdown = mlp_weights(o)
    mixed, c, ff, dz, dy, dh1, dypool, do, delta, dg2, loss = _f2_call(x, tgt, ypool, o, wout, wup, wdown, g2r, tm=256)
    dff = ff.shape[1]
    g_out = [g.reshape(N_CHIPS, d // N_CHIPS, d)
             for g in _wgrad_call(mixed, dh1, d, d, bk // 4, (d, d), (d, d), lambda i, j, k: (0, 0), "wgrad_out")]
    g_up = _wgrad_call(c, dz, d, dff // N_CHIPS, bk, (N_CHIPS, d, dff // N_CHIPS), (None, d, dff // N_CHIPS),
                       lambda i, j, k: (j, 0, 0), "wgrad_up")
    g_down = _wgrad_call(ff, dy, dff // N_CHIPS, d, bk, (N_CHIPS, dff // N_CHIPS, d), (None, dff // N_CHIPS, d),
                         lambda i, j, k: (i, 0, 0), "wgrad_down")
    dep = None if on_mlp_grads is None else on_mlp_grads(g_out[1], g_up[1], g_down[1])
    dqn, dkn, dv, dbias = _attn_bwd_call(qn, kn, v, do, lse, delta, bias, dep)
    dx, dproj, dg1, dqg, dkg, dpw, dps = _bproj_call(
        dqn, dkn, dv, q32, k32, dypool, pooled, x, dh1, win, pw_c, psr, qgr, kgr, g1r, tm=512)
    nin = dproj.shape[1] // N_CHIPS
    g_in = _wgrad_call(a, dproj, d, nin, bk, (N_CHIPS, d, nin), (None, d, nin), lambda i, j, k: (j, 0, 0), "wgrad_in")
    drb = _rel_bias_grad_call(dbias, buckets)
    small = dict(
        mix_norm_g=dg1.reshape(d), mlp_norm_g=dg2.reshape(d), pool_scale=dps.reshape(POOL_WIDTH),
        q_norm_g=dqg.reshape(ATTN_WIDTH), k_norm_g=dkg.reshape(ATTN_WIDTH),
        rel_bias=drb[:, :N_BUCKETS].T, pool_w=dpw)
    return loss[0, 0], dx, (g_in, g_out, g_up, g_down), small


def _coords():
    return lax.axis_index("x"), lax.axis_index("y"), lax.axis_index("c")


def _other_chips(x, y):
    return [(1 - x, y), (x, 1 - y), (1 - x, 1 - y)]


def _remote(src, dst, send_sem, recv_sem, dev):
    return pltpu.make_async_remote_copy(src_ref=src, dst_ref=dst, send_sem=send_sem, recv_sem=recv_sem,
                                        device_id=dev, device_id_type=MESH)


PAIR_FORWARD_ID = 1
PAIR_ALLGATHER_ID = 2


def _sibling_handshake():
    x, y, c = _coords()
    barrier = pltpu.get_barrier_semaphore()
    pl.semaphore_signal(barrier, inc=1, device_id=(x, y, 1 - c), device_id_type=MESH)
    pl.semaphore_wait(barrier, 1)


def _halves(a):
    return a.reshape(a.shape[:-2] + (2, a.shape[-2] // 2, a.shape[-1]))


def _place_shards_call(shards, chip_idx, nch):
    nw = len(shards)

    def body(chip_ref, *refs):
        for w in range(nw):
            refs[nw + w][...] = refs[w][...].astype(WIRE_DTYPE)

    in_specs = [pl.BlockSpec((s.shape[0] // nch, s.shape[1]), lambda i, chip_ref: (i, 0)) for s in shards]
    out_specs = [pl.BlockSpec((None, s.shape[0] // nch, s.shape[1]), lambda i, chip_ref: (chip_ref[0], i, 0))
                 for s in shards]
    return pl.pallas_call(
        body, name="weights_place",
        grid_spec=pltpu.PrefetchScalarGridSpec(num_scalar_prefetch=1, grid=(nch,),
                                               in_specs=in_specs, out_specs=out_specs),
        out_shape=[jax.ShapeDtypeStruct((N_CHIPS,) + s.shape, WIRE_DTYPE) for s in shards],
        compiler_params=_params(("arbitrary",)),
    )(chip_idx, *shards)


def _allgather_call(placed, from_chips, name, meanwhile=None):
    nw = len(placed)
    ncp = 3 * nw
    extra, extra_specs, extra_shape, extra_body = meanwhile if meanwhile else ([], [], None, None)
    ne = len(extra)

    def body(*refs):
        outs = refs[nw + ne:2 * nw + ne]
        send1, recv1, send2, recv2 = refs[-4:]
        x, y, c = _coords()
        chip = 2 * x + y
        others = _other_chips(x, y)
        first, passed = [], []
        if not from_chips:
            _sibling_handshake()
        if from_chips:
            for w in range(nw):
                for k, (ox, oy) in enumerate(others):
                    mine = outs[w].at[chip, c]
                    cp = _remote(mine, mine, send1.at[3 * w + k], recv1.at[3 * w + k], (ox, oy, c))
                    cp.start()
                    first.append(cp)
        if meanwhile:
            extra_body(*refs[nw:nw + ne], refs[2 * nw + ne])
        for w in range(nw):
            for k, (ox, oy) in enumerate(others):
                piece = outs[w].at[2 * ox + oy, c]
                if from_chips:
                    _remote(piece, piece, send1.at[3 * w + k], recv1.at[3 * w + k], (ox, oy, c)).wait_recv()
                cp = _remote(piece, piece, send2.at[3 * w + k], recv2.at[3 * w + k], (x, y, 1 - c))
                cp.start()
                passed.append(cp)
        for w in range(nw):
            for k, (ox, oy) in enumerate(others):
                piece = outs[w].at[2 * ox + oy, 1 - c]
                _remote(piece, piece, send2.at[3 * w + k], recv2.at[3 * w + k], (x, y, 1 - c)).wait_recv()
        for cp in first + passed:
            cp.wait_send()

    return pl.pallas_call(
        body, name=name,
        in_specs=[ANY] * nw + list(extra_specs),
        out_specs=[ANY] * nw + ([pl.BlockSpec(memory_space=pltpu.VMEM)] if meanwhile else []),
        out_shape=[jax.ShapeDtypeStruct(s.shape, s.dtype) for s in placed] + ([extra_shape] if meanwhile else []),
        input_output_aliases={w: w for w in range(nw)},
        scratch_shapes=[pltpu.SemaphoreType.DMA((ncp,))] * 4,
        compiler_params=_params() if from_chips else _params(collective_id=PAIR_FORWARD_ID),
    )(*placed, *extra)


HBM_SPEC = pl.BlockSpec(memory_space=pltpu.HBM)
SEM_SPEC = pl.BlockSpec(memory_space=pltpu.SEMAPHORE)
SPLIT_EFFECT = pltpu.SideEffectType.DATAFLOW_SIDE_EFFECTING


def _in_hbm(a):
    return pltpu.with_memory_space_constraint(a, pltpu.HBM)


def _gather_copies(bufs, send, recv):
    x, y, c = _coords()
    chip = 2 * x + y
    cps = []
    for w, buf in enumerate(bufs):
        for k, (ox, oy) in enumerate(_other_chips(x, y)):
            mine, theirs = buf.at[chip, c], buf.at[2 * ox + oy, c]
            sems = (send.at[3 * w + k], recv.at[3 * w + k], (ox, oy, c))
            cps.append((_remote(mine, mine, *sems), _remote(theirs, theirs, *sems)))
    return cps


def _gather_start_call(bufs, after):
    nw = len(bufs)

    def body(*refs):
        ins, send, recv, token = refs[:nw], refs[nw + 1], refs[nw + 2], refs[2 * nw + 3]
        for out, _ in _gather_copies(ins, send, recv):
            out.start()
        token[...] = jnp.zeros(token.shape, F32)

    res = pl.pallas_call(
        body, name="weights_gather_start",
        in_specs=[HBM_SPEC] * nw + [ANY],
        out_specs=[SEM_SPEC, SEM_SPEC] + [HBM_SPEC] * nw + [pl.BlockSpec(memory_space=pltpu.VMEM)],
        out_shape=[pltpu.SemaphoreType.DMA((3 * nw,)), pltpu.SemaphoreType.DMA((3 * nw,))]
        + [pltpu.HBM(b.shape, b.dtype) for b in bufs] + [jax.ShapeDtypeStruct((8, LANES), F32)],
        input_output_aliases={w: 2 + w for w in range(nw)},
        compiler_params=pltpu.CompilerParams(has_side_effects=SPLIT_EFFECT),
    )(*[_in_hbm(b) for b in bufs], after)
    return res[0], res[1], list(res[2:2 + nw]), res[2 + nw]


def _gather_wait_call(bufs, send, recv, after):
    nw = len(bufs)

    def body(*refs):
        ins, send, recv = refs[:nw], refs[nw], refs[nw + 1]
        for out, back in _gather_copies(ins, send, recv):
            out.wait_send()
            back.wait_recv()

    return pl.pallas_call(
        body, name="weights_gather_wait",
        in_specs=[HBM_SPEC] * nw + [SEM_SPEC, SEM_SPEC, ANY],
        out_specs=[HBM_SPEC] * nw,
        out_shape=[pltpu.HBM(b.shape, b.dtype) for b in bufs],
        input_output_aliases={w: w for w in range(nw)},
        compiler_params=pltpu.CompilerParams(has_side_effects=SPLIT_EFFECT),
    )(*bufs, send, recv, after)


def _scatter_copies(srcs, lands, send, recv, wholes):
    x, y, c = _coords()
    me = 4 * x + 2 * y + c
    cps = []
    for w, (src, land) in enumerate(zip(srcs, lands)):
        for r in range(1, N_DEV):
            px, py, pc = ((1 - x) if r & 4 else x, (1 - y) if r & 2 else y, (1 - c) if r & 1 else c)
            sems = (send.at[(N_DEV - 1) * w + r - 1], recv.at[(N_DEV - 1) * w + r - 1], (px, py, pc))
            piece = src if wholes[w] else src.at[2 * px + py, pc]
            cps.append((_remote(piece, land.at[me], *sems), _remote(piece, land.at[4 * px + 2 * py + pc], *sems)))
    return cps


def _scatter_start_call(srcs, lands, wholes, name):
    nw = len(srcs)
    ncp = (N_DEV - 1) * nw

    def body(*refs):
        ins, lnd, send, recv, token = refs[:nw], refs[nw:2 * nw], refs[2 * nw], refs[2 * nw + 1], refs[4 * nw + 2]
        for out, _ in _scatter_copies(ins, lnd, send, recv, wholes):
            out.start()
        token[...] = jnp.zeros(token.shape, F32)

    res = pl.pallas_call(
        body, name=name,
        in_specs=[HBM_SPEC] * (2 * nw),
        out_specs=[SEM_SPEC, SEM_SPEC] + [HBM_SPEC] * (2 * nw) + [pl.BlockSpec(memory_space=pltpu.VMEM)],
        out_shape=[pltpu.SemaphoreType.DMA((ncp,)), pltpu.SemaphoreType.DMA((ncp,))]
        + [pltpu.HBM(b.shape, b.dtype) for b in list(srcs) + list(lands)] + [jax.ShapeDtypeStruct((8, LANES), F32)],
        input_output_aliases={i: 2 + i for i in range(2 * nw)},
        compiler_params=pltpu.CompilerParams(has_side_effects=SPLIT_EFFECT),
    )(*[_in_hbm(b) for b in list(srcs) + list(lands)])
    return res[0], res[1], list(res[2:2 + nw]), list(res[2 + nw:2 + 2 * nw]), res[2 + 2 * nw]


def _scatter_wait_call(srcs, lands, send, recv, after, wholes, name):
    nw = len(srcs)

    def body(*refs):
        ins, lnd, send, recv = refs[:nw], refs[nw:2 * nw], refs[2 * nw], refs[2 * nw + 1]
        for out, back in _scatter_copies(ins, lnd, send, recv, wholes):
            out.wait_send()
            back.wait_recv()

    res = pl.pallas_call(
        body, name=name,
        in_specs=[HBM_SPEC] * (2 * nw) + [SEM_SPEC, SEM_SPEC, ANY],
        out_specs=[HBM_SPEC] * (2 * nw),
        out_shape=[pltpu.HBM(b.shape, b.dtype) for b in list(srcs) + list(lands)],
        input_output_aliases={i: i for i in range(2 * nw)},
        compiler_params=pltpu.CompilerParams(has_side_effects=SPLIT_EFFECT),
    )(*srcs, *lands, send, recv, after)
    return list(res[nw:])


def _reduce_call(own, lands, idx, nch, name, dep=None):
    nw = len(own)
    deps = [] if dep is None else [dep]

    def body(idx_ref, *refs):
        refs = refs[:2 * nw] + refs[2 * nw + len(deps):]
        for w in range(nw):
            tot = refs[w][...]
            for r in range(1, N_DEV):
                tot = tot + refs[nw + w][idx_ref[1 + r]].astype(F32)
            refs[2 * nw + w][...] = tot

    in_specs, out_specs, out_shape = [], [], []
    for s in own:
        in_specs.append(pl.BlockSpec((None, None, s.shape[2] // nch, s.shape[3]),
                                     lambda i, idx_ref: (idx_ref[0], idx_ref[1], i, 0)))
    for s in own:
        in_specs.append(pl.BlockSpec((N_DEV, s.shape[2] // nch, s.shape[3]), lambda i, idx_ref: (0, i, 0)))
    for s in own:
        out_specs.append(pl.BlockSpec((None, s.shape[2] // nch, s.shape[3]), lambda i, idx_ref: (idx_ref[1], i, 0)))
        out_shape.append(jax.ShapeDtypeStruct((2,) + s.shape[2:], F32))
    return pl.pallas_call(
        body, name=name,
        grid_spec=pltpu.PrefetchScalarGridSpec(num_scalar_prefetch=1, grid=(nch,),
                                               in_specs=in_specs + [ANY] * len(deps), out_specs=out_specs),
        out_shape=out_shape,
        compiler_params=_params(("arbitrary",)),
    )(idx, *own, *lands, *deps)


def _pair_allgather_call(halves, name):
    nw = len(halves)

    def body(*refs):
        outs = refs[nw:2 * nw]
        send, recv = refs[2 * nw:]
        x, y, c = _coords()
        _sibling_handshake()
        cps = []
        for w in range(nw):
            cp = _remote(outs[w].at[c], outs[w].at[c], send.at[w], recv.at[w], (x, y, 1 - c))
            cp.start()
            cps.append(cp)
        for w in range(nw):
            theirs = outs[w].at[1 - c]
            _remote(theirs, theirs, send.at[w], recv.at[w], (x, y, 1 - c)).wait_recv()
        for cp in cps:
            cp.wait_send()

    outs = pl.pallas_call(
        body, name=name,
        in_specs=[ANY] * nw, out_specs=[ANY] * nw,
        out_shape=[jax.ShapeDtypeStruct(h.shape, h.dtype) for h in halves],
        input_output_aliases={w: w for w in range(nw)},
        scratch_shapes=[pltpu.SemaphoreType.DMA((nw,))] * 2,
        compiler_params=pltpu.CompilerParams(collective_id=PAIR_ALLGATHER_ID),
    )(*halves)
    return [o.reshape(2 * h.shape[1], h.shape[2]) for o, h in zip(outs, halves)]


def _adamw(w, g, m, v):
    m = ADAM_B1 * m + (1.0 - ADAM_B1) * g
    v = ADAM_B2 * v + (1.0 - ADAM_B2) * (g * g)
    m_hat = m / (1.0 - ADAM_B1 ** ADAM_STEP)
    v_hat = v / (1.0 - ADAM_B2 ** ADAM_STEP)
    delta = -ADAM_LR * (m_hat / (jnp.sqrt(v_hat) + ADAM_EPS) + ADAM_WD * w)
    return delta, m, v


def _adamw_call(ws, gs, ms, vs, nch, name):
    nw = len(ws)

    def body(*refs):
        for w in range(nw):
            g = refs[nw + w][...]
            delta, m, v = _adamw(refs[w][...], g, refs[2 * nw + w][...], refs[3 * nw + w][...])
            refs[4 * nw + w][...] = g
            refs[5 * nw + w][...] = delta
            refs[6 * nw + w][...] = m
            refs[7 * nw + w][...] = v

    specs = [pl.BlockSpec((a.shape[0] // nch, a.shape[1]), lambda i: (i, 0)) for a in ws]
    res = pl.pallas_call(
        body, name=name,
        grid=(nch,),
        in_specs=specs * 4, out_specs=specs * 4,
        out_shape=[jax.ShapeDtypeStruct(a.shape, F32) for a in ws] * 4,
        compiler_params=_params(("arbitrary",)),
    )(*ws, *gs, *ms, *vs)
    return res[:nw], res[nw:2 * nw], res[2 * nw:3 * nw], res[3 * nw:]


def _small_call(gathered, own, me_idx, w, m, v):
    cuts = dict(mix_norm_g=(0, 0, 1024), mlp_norm_g=(1, 0, 1024), pool_scale=(2, 0, POOL_WIDTH),
                rel_bias=(2, POOL_WIDTH, N_BUCKETS * N_HEADS), q_norm_g=(3, 0, HEAD_DIM), k_norm_g=(3, LANES, HEAD_DIM))
    n_out = len(cuts) + 1

    def fold(row):
        tot = row[:, 0:LANES] + row[:, LANES:2 * LANES] + row[:, 2 * LANES:3 * LANES] + row[:, 3 * LANES:4 * LANES]
        return tot + pltpu.roll(tot, HEAD_DIM, axis=1)

    def body(me_ref, gh_ref, gp_ref, oh_ref, op_ref, wh, wp, mh, mp, vh, vp, loss_ref, *outs):
        me = me_ref[0]

        def total(ga_ref, own_ref):
            term = lambda i: jnp.where(me == i, own_ref[...], ga_ref[i]).astype(F32)
            tot = term(0)
            for i in range(1, N_DEV):
                tot = tot + term(i)
            return tot

        g_head, g_pool = total(gh_ref, oh_ref), total(gp_ref, op_ref)
        unfolded = g_head[4:5, :]
        folded = jnp.concatenate([fold(unfolded[:, :ATTN_WIDTH]), fold(unfolded[:, ATTN_WIDTH:]),
                                  jnp.zeros((1, 1024 - 2 * LANES), F32)], axis=-1)
        row = lax.broadcasted_iota(jnp.int32, g_head.shape, 0)
        g_head = jnp.where(row == 3, folded, g_head)
        loss_ref[...] = g_head[LOSS_ROW:LOSS_ROW + 1, 0:LANES]
        heads = (g_head,) + _adamw(wh[...], g_head, mh[...], vh[...])
        pools = (g_pool,) + _adamw(wp[...], g_pool, mp[...], vp[...])
        for kind in range(4):
            mine = outs[kind * n_out:(kind + 1) * n_out]
            for out, (row, at, n) in zip(mine, cuts.values()):
                out[...] = heads[kind][row:row + 1, at:at + n]
            mine[-1][...] = pools[kind]

    vmem = pl.BlockSpec(memory_space=pltpu.VMEM)
    shapes = [jax.ShapeDtypeStruct((1, n), F32) for _, _, n in cuts.values()] + [jax.ShapeDtypeStruct(w[1].shape, F32)]
    res = pl.pallas_call(
        body, name="adamw_small",
        in_specs=[pl.BlockSpec(memory_space=pltpu.SMEM)] + [vmem] * 10,
        out_shape=[jax.ShapeDtypeStruct((1, LANES), F32)] + shapes * 4,
        compiler_params=_params(),
    )(me_idx, *gathered, *own, *w, *m, *v)

    def unpack(mine):
        p = {n: a.reshape(-1) for n, a in zip(cuts, mine)}
        p["rel_bias"] = p["rel_bias"].reshape(N_BUCKETS, N_HEADS)
        p["pool_w"] = mine[-1].reshape(len(POOL_WINDOWS), LANES, LANES)
        return p

    return [res[0]] + [unpack(res[1 + kind * n_out:1 + (kind + 1) * n_out]) for kind in range(4)]


def _pack_small(p, folded=True, loss=None):
    z = lambda n: jnp.zeros((n,), F32)
    rows = [p["mix_norm_g"], p["mlp_norm_g"],
            jnp.concatenate([p["pool_scale"], p["rel_bias"].reshape(-1), z(1024 - POOL_WIDTH - N_BUCKETS * N_HEADS)])]
    if folded:
        rows += [jnp.concatenate([p["q_norm_g"], z(LANES - HEAD_DIM), p["k_norm_g"], z(1024 - LANES - HEAD_DIM)]), z(1024)]
    else:
        rows += [z(1024), jnp.concatenate([p["q_norm_g"], p["k_norm_g"]])]
    rows += [z(1024) if loss is None else jnp.concatenate([loss.reshape(1), z(1023)])]
    return jnp.stack(rows + [z(1024)] * 2), p["pool_w"].reshape(-1, LANES)


_WEIGHT_ORDER = ("mix_norm_g", "w_in", "pool_w", "pool_scale", "q_norm_g", "k_norm_g", "rel_bias", "w_out",
                 "mlp_norm_g", "w_up", "w_down")
_BIG = ("w_in", "w_out", "w_up", "w_down")


def kernel(x, mix_norm_g, w_in, pool_w, pool_scale, q_norm_g, k_norm_g, rel_bias, w_out, mlp_norm_g, w_up, w_down, loss_target, m_mix_norm_g, m_w_in, m_pool_w, m_pool_scale, m_q_norm_g, m_k_norm_g, m_rel_bias, m_w_out, m_mlp_norm_g, m_w_up, m_w_down, v_mix_norm_g, v_w_in, v_pool_w, v_pool_scale, v_q_norm_g, v_k_norm_g, v_rel_bias, v_w_out, v_mlp_norm_g, v_w_up, v_w_down):
    w = dict(mix_norm_g=mix_norm_g, w_in=w_in, pool_w=pool_w, pool_scale=pool_scale, q_norm_g=q_norm_g,
             k_norm_g=k_norm_g, rel_bias=rel_bias, w_out=w_out, mlp_norm_g=mlp_norm_g, w_up=w_up, w_down=w_down)
    m = dict(mix_norm_g=m_mix_norm_g, w_in=m_w_in, pool_w=m_pool_w, pool_scale=m_pool_scale, q_norm_g=m_q_norm_g,
             k_norm_g=m_k_norm_g, rel_bias=m_rel_bias, w_out=m_w_out, mlp_norm_g=m_mlp_norm_g, w_up=m_w_up, w_down=m_w_down)
    v = dict(mix_norm_g=v_mix_norm_g, w_in=v_w_in, pool_w=v_pool_w, pool_scale=v_pool_scale, q_norm_g=v_q_norm_g,
             k_norm_g=v_k_norm_g, rel_bias=v_rel_bias, w_out=v_w_out, mlp_norm_g=v_mlp_norm_g, w_up=v_w_up, w_down=v_w_down)
    xc, yc, cc = _coords()

    c_idx = jnp.reshape(cc, (1,)).astype(jnp.int32)
    chip_idx = jnp.reshape(2 * xc + yc, (1,)).astype(jnp.int32)
    me = 4 * xc + 2 * yc + cc
    whole = lambda t: t.reshape(t.shape[0], t.shape[1] * t.shape[2], t.shape[3])

    placed = [_halves(p) for p in _place_shards_call([w[n] for n in _BIG], chip_idx, nch=4)]
    win_f, bias = _allgather_call(placed[:1], from_chips=True, name="weights_allgather_in",
                                  meanwhile=_bias_table_work(rel_bias))
    wsend, wrecv, in_flight, started = _gather_start_call(placed[1:], win_f)

    def mlp_weights(after):
        landed = _gather_wait_call(in_flight, wsend, wrecv, after)
        wout_f, wup_f, wdown_f = _allgather_call(landed, from_chips=False, name="weights_pair_forward")
        return whole(wout_f).reshape(-1, wout_f.shape[-1]), whole(wup_f), whole(wdown_f)

    split = []

    def on_mlp_grads(*wire_grads):
        srcs = [_halves(g) for g in wire_grads]
        lands = [lax.empty((N_DEV,) + s.shape[2:], s.dtype) for s in srcs]
        split.extend(_scatter_start_call(srcs, lands, [False] * len(srcs), "grads_scatter_start"))
        return split[4]

    loss_part, dx, big_grads, small_grads = _local_grads(
        x[0], loss_target[0], mix_norm_g, whole(win_f), pool_w, pool_scale, q_norm_g, k_norm_g, bias,
        mlp_norm_g, mlp_weights, on_mlp_grads, first_dep=started)
    g_in, g_out, g_up, g_down = big_grads
    gsend, grecv, srcs_thru, lands_thru, _ = split
    lands_mlp = _scatter_wait_call(srcs_thru, lands_thru, gsend, grecv, g_in[1], [False] * 3, "grads_scatter_wait")

    head_own, pool_own = _pack_small(small_grads, folded=False, loss=loss_part)
    small_own = (head_own, pool_own.astype(WIRE_DTYPE))
    last_srcs = [_halves(g_in[1]), *small_own]
    last_lands = [lax.empty((N_DEV,) + last_srcs[0].shape[2:], WIRE_DTYPE)]
    last_lands += [lax.empty((N_DEV,) + a.shape, a.dtype) for a in small_own]
    lsend, lrecv, last_srcs, last_lands, last_started = _scatter_start_call(
        last_srcs, last_lands, [False, True, True], "grads_scatter_start_last")
    idx = jnp.concatenate([chip_idx, c_idx] + [jnp.reshape(jnp.bitwise_xor(me, r), (1,)) for r in range(1, N_DEV)])
    idx = idx.astype(jnp.int32)
    mlp = _BIG[1:]

    def update(names, own32, lands, tag, dep=None):
        halves = _reduce_call([_halves(g) for g in own32], lands, idx, 4, "grads_reduce_" + tag, dep)
        reduced = _pair_allgather_call(list(halves), "grads_pair_allgather_" + tag)
        return _adamw_call([w[n] for n in names], reduced, [m[n] for n in names], [v[n] for n in names], 8, "adamw_" + tag)

    out_mlp = update(mlp, [g_out[0], g_up[0], g_down[0]], lands_mlp, "mlp", last_started)
    land_in, *small_all = _scatter_wait_call(last_srcs, last_lands, lsend, lrecv, out_mlp[3][-1], [False, True, True],
                                             "grads_scatter_wait_last")
    out_in = update(_BIG[:1], [g_in[0]], [land_in], "in")
    loss_row, grads, deltas, new_m, new_v = _small_call(
        small_all, small_own, jnp.reshape(me, (1,)).astype(jnp.int32), _pack_small(w), _pack_small(m), _pack_small(v))

    for k, res in enumerate((grads, deltas, new_m, new_v)):
        res[_BIG[0]] = out_in[k][0]
        for i, n in enumerate(mlp):
            res[n] = out_mlp[k][i]
    loss = loss_row[0, 0]
    return (loss, dx[None], *[grads[n] for n in _WEIGHT_ORDER], *[deltas[n] for n in _WEIGHT_ORDER],
            *[new_m[n] for n in _WEIGHT_ORDER], *[new_v[n] for n in _WEIGHT_ORDER])
```

```python
import math

import jax
import jax.numpy as jnp
import numpy as np
from jax import lax
from jax.experimental import pallas as pl
from jax.experimental.pallas import tpu as pltpu

F32 = jnp.float32
MXU_DTYPE = jnp.bfloat16
WIRE_DTYPE = jnp.bfloat16

NORM_EPS = 1e-6
NEG_INF = -1e30
LANES = 128
HEAD_DIM = 64
N_HEADS = 8
POOL_WIDTH = 512
ATTN_WIDTH = 512
POOL_WINDOWS = (2, 4, 8, 16)
POOL_HALO = 16
DILATED_PATTERNS = ((128, 1), (512, 4), (2048, 16))
ATT_BLOCK = 128
ATT_SUPER = ATT_BLOCK * max(dl for _, dl in DILATED_PATTERNS)
ATT_UNITS = ATT_SUPER // ATT_BLOCK
N_BUCKETS = 32
NO_BUCKET = -1
MAX_DISTANCE = 2048
N_CHIPS = 4
N_DEV = 8
ADAM_LR, ADAM_B1, ADAM_B2, ADAM_EPS, ADAM_WD, ADAM_STEP = 0.001, 0.9, 0.999, 1e-08, 0.01, 10
VMEM_LIMIT = 56 * 1024 * 1024
MESH = pl.DeviceIdType.MESH
ANY = pl.BlockSpec(memory_space=pl.ANY)

LOSS_ROW = 5


def _mm(a, b):
    return jnp.dot(a, b, preferred_element_type=F32)


def _mm_nt(a, b):
    return lax.dot_general(a, b, (((1,), (1,)), ((), ())), preferred_element_type=F32)


def _mm_tn(a, b):
    return lax.dot_general(a, b, (((0,), (0,)), ((), ())), preferred_element_type=F32)


def _params(sem=None, **kw):
    if sem is not None:
        kw["dimension_semantics"] = sem
    return pltpu.CompilerParams(vmem_limit_bytes=VMEM_LIMIT, **kw)


def _low_half():
    return lax.broadcasted_iota(jnp.int32, (1, LANES), 1) < HEAD_DIM


def _head_sum_bcast(y):
    lo = _low_half()
    outs = []
    for j in range(y.shape[1] // LANES):
        c = y[:, j * LANES:(j + 1) * LANES]
        s_lo = jnp.sum(jnp.where(lo, c, 0.0), axis=-1, keepdims=True)
        s_hi = jnp.sum(jnp.where(lo, 0.0, c), axis=-1, keepdims=True)
        outs.append(jnp.where(lo, s_lo, s_hi))
    return jnp.concatenate(outs, axis=-1)


def _rms_bwd(dn, hn, r):
    return r * (dn - hn * jnp.mean(dn * hn, axis=-1, keepdims=True))


def _t5_bucket_np(dist):
    max_exact = N_BUCKETS // 2
    d_f = np.maximum(dist, 1).astype(np.float32)
    ratio = (np.log(d_f / np.float32(max_exact)) / np.float32(math.log(MAX_DISTANCE / max_exact))).astype(np.float32)
    large = max_exact + (ratio * np.float32(N_BUCKETS - max_exact)).astype(np.int32)
    large = np.minimum(large, N_BUCKETS - 1)
    return np.where(dist < max_exact, dist, large).astype(np.int32)


def _window_offsets(dl):
    if dl == 1:
        return _by4_positions(ATT_BLOCK), _by4_positions(2 * ATT_BLOCK)
    return np.arange(ATT_BLOCK), np.arange(2 * ATT_BLOCK)


def _bucket_tables():
    tables = []
    for _, dl in DILATED_PATTERNS:
        qq, kk = _window_offsets(dl)
        dist = qq[:, None] + ATT_BLOCK - kk[None, :]
        bucket = _t5_bucket_np(np.clip(dist, 0, ATT_BLOCK) * dl)
        tables.append(np.where((dist >= 0) & (dist <= ATT_BLOCK), bucket, NO_BUCKET))
    return np.stack(tables).astype(np.int32)


def _previous_block_keys():
    return np.stack([np.broadcast_to(_window_offsets(dl)[1][None, :] < ATT_BLOCK, (ATT_BLOCK, 2 * ATT_BLOCK))
                     for _, dl in DILATED_PATTERNS])


def _f1_call(x, g1, win, poolw, pscale, qg, kg, tm, dep=None):
    s, d = x.shape
    nblk = s // tm
    deps = [] if dep is None else [dep]

    def body(x_ref, g1_ref, win_ref, pw_ref, ps_ref, qg_ref, kg_ref, *rest):
        a_ref, pooled_ref, ypool_ref, q32_ref, k32_ref, qn_ref, kn_ref, v_ref, ubuf = rest[len(deps):]
        i = pl.program_id(0)
        xv = x_ref[...]
        r = lax.rsqrt(jnp.mean(xv * xv, axis=-1, keepdims=True) + NORM_EPS)
        a = ((xv * r) * g1_ref[...]).astype(MXU_DTYPE)
        a_ref[...] = a
        u = _mm(a, win_ref[0])
        q = _mm(a, win_ref[1])
        k = _mm(a, win_ref[2])
        v_ref[...] = _mm(a, win_ref[3])
        q32_ref[...] = q
        k32_ref[...] = k
        rq = lax.rsqrt(_head_sum_bcast(q * q) * (1.0 / HEAD_DIM) + NORM_EPS)
        qn_ref[...] = ((q * rq) * qg_ref[...]) * (HEAD_DIM ** -0.5)
        rk = lax.rsqrt(_head_sum_bcast(k * k) * (1.0 / HEAD_DIM) + NORM_EPS)
        kn_ref[...] = (k * rk) * kg_ref[...]

        ubuf[0:POOL_HALO, :] = jnp.where(i > 0, ubuf[tm:tm + POOL_HALO, :], 0.0)
        ubuf[POOL_HALO:POOL_HALO + tm, :] = u
        t = i * tm + lax.broadcasted_iota(jnp.int32, (tm, 1), 0)
        for g, w in enumerate(POOL_WINDOWS):
            ls = slice(g * LANES, (g + 1) * LANES)
            ug = u[:, ls]
            acc = ug
            for sh in range(1, w):
                acc = acc + ubuf[POOL_HALO - sh:POOL_HALO - sh + tm, ls]
            cnt = jnp.minimum(t + 1, w).astype(F32)
            pooled = (acc / cnt - ug).astype(MXU_DTYPE)
            pooled_ref[:, ls] = pooled
            ypool_ref[:, ls] = (_mm(pooled, pw_ref[g]) * ps_ref[:, ls]).astype(MXU_DTYPE)

    tok = lambda w: pl.BlockSpec((tm, w), lambda i: (i, 0))
    full = lambda shp: pl.BlockSpec(shp, lambda i: (0,) * len(shp))
    return pl.pallas_call(
        body, name="fwd_inproj",
        grid=(nblk,),
        in_specs=[tok(d), full((1, d)), full(win.shape), full(poolw.shape), full((1, POOL_WIDTH)),
                  full((1, ATTN_WIDTH)), full((1, ATTN_WIDTH))] + [ANY] * len(deps),
        out_specs=[tok(d), tok(POOL_WIDTH), tok(POOL_WIDTH), tok(ATTN_WIDTH), tok(ATTN_WIDTH),
                   tok(ATTN_WIDTH), tok(ATTN_WIDTH), tok(ATTN_WIDTH)],
        out_shape=[jax.ShapeDtypeStruct((s, d), MXU_DTYPE),
                   jax.ShapeDtypeStruct((s, POOL_WIDTH), MXU_DTYPE),
                   jax.ShapeDtypeStruct((s, POOL_WIDTH), MXU_DTYPE),
                   jax.ShapeDtypeStruct((s, ATTN_WIDTH), F32),
                   jax.ShapeDtypeStruct((s, ATTN_WIDTH), F32),
                   jax.ShapeDtypeStruct((s, ATTN_WIDTH), F32),
                   jax.ShapeDtypeStruct((s, ATTN_WIDTH), F32),
                   jax.ShapeDtypeStruct((s, ATTN_WIDTH), F32)],
        scratch_shapes=[pltpu.VMEM((tm + POOL_HALO, POOL_WIDTH), F32)],
        compiler_params=_params(("arbitrary",)),
    )(x, g1, win, poolw, pscale, qg, kg, *deps)


DEINT = 4
assert [dl for _, dl in DILATED_PATTERNS] == [1, DEINT, DEINT * DEINT]


def _by4_positions(n):
    pos = np.arange(n)
    return DEINT * (pos % (n // DEINT)) + pos // (n // DEINT)


def _masked_bias(b_ref, p, n):
    return b_ref[p, jnp.minimum(n, 1)].reshape(2 * ATT_BLOCK, 2 * ATT_BLOCK)


def _unit_rows(u, dl):
    assert isinstance(u, int)
    sq, sk = ATT_SUPER // DEINT, 2 * ATT_SUPER // DEINT
    if dl == 1:
        n = ATT_BLOCK // DEINT
        return (u, [pl.ds(r * sq + n * u, n) for r in range(DEINT)],
                [pl.ds(r * sk + sk // 2 + n * (u - 1), 2 * n) for r in range(DEINT)])
    if dl == DEINT:
        r, b = u % DEINT, u // DEINT
        return (b, [pl.ds(r * sq + ATT_BLOCK * b, ATT_BLOCK)],
                [pl.ds(r * sk + sk // 2 + ATT_BLOCK * (b - 1), 2 * ATT_BLOCK)])
    r, a = u % DEINT, u // DEINT
    return 0, [pl.ds(r * sq + a, ATT_BLOCK, stride=DEINT)], [pl.ds(r * sk + a, 2 * ATT_BLOCK, stride=DEINT)]


def _take(ref, runs):
    parts = [ref[run, :] for run in runs]
    return parts[0] if len(parts) == 1 else jnp.concatenate(parts, axis=0)


def _put(ref, runs, value, add=False):
    n = value.shape[0] // len(runs)
    for i, run in enumerate(runs):
        part = value[i * n:(i + 1) * n]
        ref[run, :] = ref[run, :] + part if add else part


def _deinterleave(dst, src, n):
    seg = n // DEINT
    for r in range(DEINT):
        dst[r * seg:(r + 1) * seg, :] = src[pl.ds(r, seg, stride=DEINT), :]


def _deinterleave_pair(dst, prev, cur):
    seg = prev.shape[0] // DEINT
    for r in range(DEINT):
        dst[2 * r * seg:(2 * r + 1) * seg, :] = prev[pl.ds(r, seg, stride=DEINT), :]
        dst[(2 * r + 1) * seg:(2 * r + 2) * seg, :] = cur[pl.ds(r, seg, stride=DEINT), :]


def _interleave(dst, src, n, offset=0):
    seg = n // DEINT
    stride = src.shape[0] // DEINT
    for r in range(DEINT):
        dst[pl.ds(r, seg, stride=DEINT), :] = src[r * stride + offset:r * stride + offset + seg, :]


def _attn_fwd_call(qn, kn, v, bias):
    s, w = qn.shape
    nsb = s // ATT_SUPER
    npair = w // LANES

    def body(q_ref, kc_ref, vc_ref, b_ref, o_ref, lse_ref, qf, kf, vf, acc_s, m_s, l_s):
        sb = pl.program_id(1)

        @pl.when((pl.program_id(0) == 0) & (sb == 0))
        def _():
            kf[...] = jnp.zeros_like(kf)
            vf[...] = jnp.zeros_like(vf)

        seg = ATT_SUPER // DEINT
        _deinterleave(qf, q_ref, ATT_SUPER)
        for r in range(DEINT):
            for dst, src in ((kf, kc_ref), (vf, vc_ref)):
                dst[2 * r * seg:(2 * r + 1) * seg, :] = dst[(2 * r + 1) * seg:(2 * r + 2) * seg, :]
                dst[(2 * r + 1) * seg:(2 * r + 2) * seg, :] = src[pl.ds(r, seg, stride=DEINT), :]
        lo = _low_half()
        for p, (_, dl) in enumerate(DILATED_PATTERNS):
            def unit(u, carry, p=p, dl=dl):
                b, rows_q, rows_k = _unit_rows(u, dl)
                qp = _take(qf, rows_q).astype(MXU_DTYPE)
                kcat = _take(kf, rows_k).astype(MXU_DTYPE)
                vcat = _take(vf, rows_k).astype(MXU_DTYPE)
                zero = jnp.zeros_like(qp)
                q2 = jnp.concatenate([jnp.where(lo, qp, zero), jnp.where(lo, zero, qp)], axis=0)
                sc = _mm_nt(q2, kcat) + _masked_bias(b_ref, p, sb * (ATT_UNITS // dl) + b)
                m2 = jnp.max(sc, axis=-1, keepdims=True)
                pr = jnp.exp(sc - m2)
                l2 = jnp.sum(pr, axis=-1, keepdims=True)
                acc2 = _mm(pr.astype(MXU_DTYPE), vcat)
                acc = jnp.where(lo, acc2[:ATT_BLOCK], acc2[ATT_BLOCK:])
                m = jnp.where(lo, m2[:ATT_BLOCK], m2[ATT_BLOCK:])
                l = jnp.where(lo, l2[:ATT_BLOCK], l2[ATT_BLOCK:])
                if p == 0:
                    _put(acc_s, rows_q, acc)
                    _put(m_s, rows_q, m)
                    _put(l_s, rows_q, l)
                else:
                    m_old = _take(m_s, rows_q)
                    m_new = jnp.maximum(m_old, m)
                    a_old = jnp.exp(m_old - m_new)
                    a_new = jnp.exp(m - m_new)
                    _put(acc_s, rows_q, a_old * _take(acc_s, rows_q) + a_new * acc)
                    _put(l_s, rows_q, a_old * _take(l_s, rows_q) + a_new * l)
                    _put(m_s, rows_q, m_new)
                return carry

            for u in range(ATT_UNITS):
                unit(u, None)
        l = l_s[...]
        acc_s[...] = acc_s[...] / l
        m_s[...] = m_s[...] + jnp.log(l)
        _interleave(o_ref, acc_s, ATT_SUPER)
        _interleave(lse_ref, m_s, ATT_SUPER)

    cur = pl.BlockSpec((ATT_SUPER, LANES), lambda j, t: (t, j))
    bspec = pl.BlockSpec((len(DILATED_PATTERNS), 2, 2, ATT_BLOCK, 2 * ATT_BLOCK), lambda j, t: (0, 0, j, 0, 0))
    return pl.pallas_call(
        body, name="attn_fwd",
        grid=(npair, nsb),
        in_specs=[cur, cur, cur, bspec],
        out_specs=[cur, cur],
        out_shape=[jax.ShapeDtypeStruct((s, w), F32), jax.ShapeDtypeStruct((s, w), F32)],
        scratch_shapes=[pltpu.VMEM((ATT_SUPER, LANES), F32), pltpu.VMEM((2 * ATT_SUPER, LANES), F32),
                        pltpu.VMEM((2 * ATT_SUPER, LANES), F32), pltpu.VMEM((ATT_SUPER, LANES), F32),
                        pltpu.VMEM((ATT_SUPER, LANES), F32), pltpu.VMEM((ATT_SUPER, LANES), F32)],
        compiler_params=_params(("arbitrary", "arbitrary")),
    )(qn, kn, v, bias)


def _attn_bwd_call(qn, kn, v, do, lse, delta, bias, dep=None):
    s, w = qn.shape
    nsb = s // ATT_SUPER
    npair = w // LANES
    deps = [] if dep is None else [dep]

    def body(q_ref, kc_ref, kp_ref, vc_ref, vp_ref, do_ref, lse_ref, dlt_ref, b_ref, *rest):
        dq_ref, dk_ref, dv_ref, db_ref, qf, kf, vf, dof, lsef, dltf, dqf, dkf, dvf = rest[len(deps):]
        step = pl.program_id(1)
        sb = nsb - 1 - step
        seg = ATT_SUPER // DEINT
        _deinterleave(qf, q_ref, ATT_SUPER)
        _deinterleave(dof, do_ref, ATT_SUPER)
        _deinterleave_pair(kf, kp_ref, kc_ref)
        _deinterleave_pair(vf, vp_ref, vc_ref)
        _deinterleave(lsef, lse_ref, ATT_SUPER)
        _deinterleave(dltf, dlt_ref, ATT_SUPER)

        db_ref[...] = jnp.where(step > 0, db_ref[...], 0.0)
        for acc in (dkf, dvf):
            for r in range(DEINT):
                this, before = pl.ds((2 * r + 1) * seg, seg), pl.ds(2 * r * seg, seg)
                acc[this, :] = jnp.where(step > 0, acc[before, :], 0.0)
                acc[before, :] = jnp.zeros((seg, LANES), F32)
        lo = _low_half()
        for p, (_, dl) in enumerate(DILATED_PATTERNS):
            def unit(u, carry, p=p, dl=dl):
                b, rows_q, rows_k = _unit_rows(u, dl)
                qp = _take(qf, rows_q).astype(MXU_DTYPE)
                dop = _take(dof, rows_q).astype(MXU_DTYPE)
                kcat = _take(kf, rows_k).astype(MXU_DTYPE)
                vcat = _take(vf, rows_k).astype(MXU_DTYPE)
                lse2 = _take(lsef, rows_q)
                dlt2 = _take(dltf, rows_q)
                zero = jnp.zeros_like(qp)
                q2 = jnp.concatenate([jnp.where(lo, qp, zero), jnp.where(lo, zero, qp)], axis=0)
                do2 = jnp.concatenate([jnp.where(lo, dop, zero), jnp.where(lo, zero, dop)], axis=0)
                lse_c = jnp.concatenate([lse2[:, 0:1], lse2[:, HEAD_DIM:HEAD_DIM + 1]], axis=0)
                dlt_c = jnp.concatenate([dlt2[:, 0:1], dlt2[:, HEAD_DIM:HEAD_DIM + 1]], axis=0)
                sc = _mm_nt(q2, kcat) + _masked_bias(b_ref, p, sb * (ATT_UNITS // dl) + b)
                pr = jnp.exp(sc - lse_c)
                ds = pr * (_mm_nt(do2, vcat) - dlt_c)
                db_ref[p] += ds.reshape(2, ATT_BLOCK, 2 * ATT_BLOCK)
                ds_c = ds.astype(MXU_DTYPE)
                dq2 = _mm(ds_c, kcat)
                dk = _mm_tn(ds_c, q2)
                dv = _mm_tn(pr.astype(MXU_DTYPE), do2)
                dq = jnp.where(lo, dq2[:ATT_BLOCK], dq2[ATT_BLOCK:])
                _put(dqf, rows_q, dq, add=p > 0)
                _put(dkf, rows_k, dk, add=True)
                _put(dvf, rows_k, dv, add=True)
                return carry

            for u in range(ATT_UNITS):
                unit(u, None)
        _interleave(dq_ref, dqf, ATT_SUPER)
        _interleave(dk_ref, dkf, ATT_SUPER, offset=seg)
        _interleave(dv_ref, dvf, ATT_SUPER, offset=seg)

    cur = pl.BlockSpec((ATT_SUPER, LANES), lambda j, t: (nsb - 1 - t, j))
    prev = pl.BlockSpec((ATT_SUPER, LANES), lambda j, t: (jnp.maximum(nsb - 2 - t, 0), j))
    npat = len(DILATED_PATTERNS)
    bspec = pl.BlockSpec((npat, 2, 2, ATT_BLOCK, 2 * ATT_BLOCK), lambda j, t: (0, 0, j, 0, 0))
    dbspec = pl.BlockSpec((npat, 2, ATT_BLOCK, 2 * ATT_BLOCK), lambda j, t: (0, j, 0, 0))
    sup = lambda: pltpu.VMEM((ATT_SUPER, LANES), F32)
    sup2 = lambda: pltpu.VMEM((2 * ATT_SUPER, LANES), F32)
    return pl.pallas_call(
        body, name="attn_bwd",
        grid=(npair, nsb),
        in_specs=[cur, cur, prev, cur, prev, cur, cur, cur, bspec] + [ANY] * len(deps),
        out_specs=[cur, cur, cur, dbspec],
        out_shape=[jax.ShapeDtypeStruct((s, w), F32)] * 3
        + [jax.ShapeDtypeStruct((npat, N_HEADS, ATT_BLOCK, 2 * ATT_BLOCK), F32)],
        scratch_shapes=[sup(), sup2(), sup2(), sup(), sup(), sup(), sup(), sup2(), sup2()],
        compiler_params=_params(("arbitrary", "arbitrary")),
    )(qn, kn, kn, v, v, do, lse, delta, bias, *deps)


def _bias_table_work(rel_bias):
    buckets = jnp.asarray(_bucket_tables())
    prev_keys = jnp.asarray(_previous_block_keys().astype(np.int32))
    npat = buckets.shape[0]

    def body(rb_ref, bk_ref, pk_ref, out_ref):
        for p in range(npat):
            for half in range(2):
                ks = slice(half * ATT_BLOCK, (half + 1) * ATT_BLOCK)
                bk = bk_ref[p, :, ks]
                absent = pk_ref[p, :, ks] != 0
                for h in range(N_HEADS):
                    def pick(b, acc, h=h, bk=bk):
                        return jnp.where(bk == b, rb_ref[b, h], acc)

                    tab = lax.fori_loop(0, N_BUCKETS, pick, jnp.full((ATT_BLOCK, ATT_BLOCK), NEG_INF, F32))
                    out_ref[p, 1, h, :, ks] = tab
                    out_ref[p, 0, h, :, ks] = jnp.where(absent, NEG_INF, tab)

    vmem = pl.BlockSpec(memory_space=pltpu.VMEM)
    return ([rel_bias, buckets, prev_keys], [pl.BlockSpec(memory_space=pltpu.SMEM), vmem, vmem],
            jax.ShapeDtypeStruct((npat, 2, N_HEADS, ATT_BLOCK, 2 * ATT_BLOCK), F32), body)


def _rel_bias_grad_call(dbias, buckets):
    npat, nh = dbias.shape[0], dbias.shape[1]

    def body(db_ref, bk_ref, out_ref):
        lane = lax.broadcasted_iota(jnp.int32, (nh, LANES), 1)
        out = jnp.zeros((nh, LANES), F32)
        for b in range(N_BUCKETS):
            tot = jnp.zeros((nh, 1), F32)
            for p in range(npat):
                hit = jnp.where(bk_ref[p][None] == b, db_ref[p], 0.0)
                tot = tot + jnp.sum(jnp.sum(hit, axis=1), axis=-1, keepdims=True)
            out = jnp.where(lane == b, tot, out)
        out_ref[...] = out

    return pl.pallas_call(
        body, name="rel_bias_grad",
        out_shape=jax.ShapeDtypeStruct((nh, LANES), F32),
        compiler_params=_params(),
    )(dbias, buckets)


def _f2_call(x, tgt, ypool, o, wout, wup, wdown, g2, tm):
    s, d = x.shape
    nblk = s // tm
    nch, _, fch = wup.shape
    dff = nch * fch
    mixw = POOL_WIDTH + ATTN_WIDTH

    def body(x_ref, t_ref, yp_ref, o_ref, g2_ref, wout_hbm, wup_hbm, wdown_hbm,
             mixed_ref, c_ref, ff_ref, dz_ref, dy_ref, dh1_ref, dyp_ref, do_ref, dlt_ref, dg2_ref, loss_ref,
             wout_v, wup_v, wdown_v, rz, wsem):
        i = pl.program_id(0)

        @pl.when(i == 0)
        def _():
            copies = [pltpu.make_async_copy(wout_hbm, wout_v, wsem.at[0])]
            for j in range(nch):
                copies.append(pltpu.make_async_copy(wup_hbm.at[j], wup_v.at[j], wsem.at[1 + 2 * j]))
                copies.append(pltpu.make_async_copy(wdown_hbm.at[j], wdown_v.at[j], wsem.at[2 + 2 * j]))
            for cp in copies:
                cp.start()
            dg2_ref[...] = jnp.zeros(dg2_ref.shape, F32)
            loss_ref[...] = jnp.zeros(loss_ref.shape, F32)
            for cp in copies:
                cp.wait()

        o = o_ref[...]
        mixed = jnp.concatenate([yp_ref[...], o.astype(MXU_DTYPE)], axis=-1)
        mixed_ref[...] = mixed
        h1 = x_ref[...] + _mm(mixed, wout_v[...])
        r2 = lax.rsqrt(jnp.mean(h1 * h1, axis=-1, keepdims=True) + NORM_EPS)
        hn = h1 * r2
        c = (hn * g2_ref[...]).astype(MXU_DTYPE)
        c_ref[...] = c
        y = h1
        for j in range(nch):
            cs = slice(j * fch, (j + 1) * fch)
            z = jnp.maximum(_mm(c, wup_v[j]), 0.0)
            rz[:, cs] = z
            ff = (z * z).astype(MXU_DTYPE)
            ff_ref[:, cs] = ff
            y = y + _mm(ff, wdown_v[j])
        err = y - t_ref[...]
        loss_ref[...] += jnp.sum(err * err) * (0.5 / d)
        dy = err * (1.0 / d)
        dy_c = dy.astype(MXU_DTYPE)
        dy_ref[...] = dy_c
        dc = jnp.zeros((tm, d), F32)
        for j in range(nch):
            cs = slice(j * fch, (j + 1) * fch)
            dz = (_mm_nt(dy_c, wdown_v[j]) * (2.0 * rz[:, cs])).astype(MXU_DTYPE)
            dz_ref[:, cs] = dz
            dc = dc + _mm_nt(dz, wup_v[j])
        dg2_ref[...] += jnp.sum(dc * hn, axis=0, keepdims=True)
        dh1 = dy + _rms_bwd(dc * g2_ref[...], hn, r2)
        dh1_ref[...] = dh1
        dmix = _mm_nt(dh1.astype(MXU_DTYPE), wout_v[...])
        dyp_ref[...] = dmix[:, :POOL_WIDTH]
        do = dmix[:, POOL_WIDTH:]
        do_ref[...] = do
        dlt_ref[...] = _head_sum_bcast(do * o)

    tok = lambda w: pl.BlockSpec((tm, w), lambda i: (i, 0))
    const = lambda shp: pl.BlockSpec(shp, lambda i: (0,) * len(shp))
    return pl.pallas_call(
        body, name="fwd_mlp_bwd_mlp",
        grid=(nblk,),
        in_specs=[tok(d), tok(d), tok(POOL_WIDTH), tok(ATTN_WIDTH), const((1, d)), ANY, ANY, ANY],
        out_specs=[tok(mixw), tok(d), tok(dff), tok(dff), tok(d), tok(d), tok(POOL_WIDTH), tok(ATTN_WIDTH),
                   tok(ATTN_WIDTH), const((1, d)), const((1, LANES))],
        out_shape=[jax.ShapeDtypeStruct((s, mixw), MXU_DTYPE),
                   jax.ShapeDtypeStruct((s, d), MXU_DTYPE),
                   jax.ShapeDtypeStruct((s, dff), MXU_DTYPE),
                   jax.ShapeDtypeStruct((s, dff), MXU_DTYPE),
                   jax.ShapeDtypeStruct((s, d), MXU_DTYPE),
                   jax.ShapeDtypeStruct((s, d), F32),
                   jax.ShapeDtypeStruct((s, POOL_WIDTH), F32),
                   jax.ShapeDtypeStruct((s, ATTN_WIDTH), F32),
                   jax.ShapeDtypeStruct((s, ATTN_WIDTH), F32),
                   jax.ShapeDtypeStruct((1, d), F32),
                   jax.ShapeDtypeStruct((1, LANES), F32)],
        scratch_shapes=[pltpu.VMEM(wout.shape, MXU_DTYPE), pltpu.VMEM(wup.shape, MXU_DTYPE),
                        pltpu.VMEM(wdown.shape, MXU_DTYPE), pltpu.VMEM((tm, dff), F32),
                        pltpu.SemaphoreType.DMA((1 + 2 * nch,))],
        compiler_params=_params(("arbitrary",)),
    )(x, tgt, ypool, o, g2, wout, wup, wdown)


RING_SLOTS = 3


def _bproj_call(dqn, dkn, dv, q32, k32, dypool, pooled, x, dh1, win, poolw, pscale, qg, kg, g1, tm):
    s, d = x.shape
    nblk = s // tm
    ngrp = len(POOL_WINDOWS)
    streams = [dqn, dkn, dv, q32, k32, dypool, pooled, x, dh1]
    ns = len(streams)
    assert nblk >= 2

    def body(*refs):
        hbm = refs[:ns]
        win_hbm, pw_ref, ps_ref, qg_ref, kg_ref, g1_ref = refs[ns:ns + 6]
        dx_ref, dproj_ref, dg1_ref, dqg_ref, dkg_ref, dpw_ref, dps_ref, win_v, ebuf = refs[ns + 6:ns + 15]
        rings, sems = refs[ns + 15:2 * ns + 15], refs[2 * ns + 15]
        step = pl.program_id(0)
        i = nblk - 1 - step

        def fetch(t):
            rows = pl.ds(pl.multiple_of((nblk - 1 - t) * tm, tm), tm)
            return [pltpu.make_async_copy(h.at[rows], ring.at[t % RING_SLOTS], sems.at[k, t % RING_SLOTS])
                    for k, (h, ring) in enumerate(zip(hbm, rings))]

        @pl.when(step == 0)
        def _():
            for t in range(2):
                for cp in fetch(t):
                    cp.start()
            pltpu.sync_copy(win_hbm, win_v)
            dg1_ref[...] = jnp.zeros(dg1_ref.shape, F32)
            dqg_ref[...] = jnp.zeros(dqg_ref.shape, F32)
            dkg_ref[...] = jnp.zeros(dkg_ref.shape, F32)
            dpw_ref[...] = jnp.zeros(dpw_ref.shape, F32)
            dps_ref[...] = jnp.zeros(dps_ref.shape, F32)
            ebuf[tm:tm + POOL_HALO, :] = jnp.zeros((POOL_HALO, POOL_WIDTH), F32)

        @pl.when(step > 0)
        def _():
            ebuf[tm:tm + POOL_HALO, :] = ebuf[0:POOL_HALO, :]

        @pl.when(step + 2 < nblk)
        def _():
            for cp in fetch(step + 2):
                cp.start()

        for cp in fetch(step):
            cp.wait()
        dqn_ref, dkn_ref, dv_ref, q_ref, k_ref, dyp_ref, pooled_ref, x_ref, dh1_ref = (
            ring.at[step % RING_SLOTS] for ring in rings)

        def qk_bwd(dn_sum, raw, gain, scale, dgain_ref):
            rr = lax.rsqrt(_head_sum_bcast(raw * raw) * (1.0 / HEAD_DIM) + NORM_EPS)
            hn = raw * rr
            dgain_ref[...] += jnp.sum(dn_sum * hn, axis=0, keepdims=True) * scale
            dn = dn_sum * (gain * scale)
            return rr * (dn - hn * (_head_sum_bcast(dn * hn) * (1.0 / HEAD_DIM)))

        dq = qk_bwd(dqn_ref[...], q_ref[...], qg_ref[...], HEAD_DIM ** -0.5, dqg_ref)
        dk = qk_bwd(dkn_ref[...], k_ref[...], kg_ref[...], 1.0, dkg_ref)

        t = i * tm + lax.broadcasted_iota(jnp.int32, (tm, 1), 0)
        dpooled = []
        for g, w in enumerate(POOL_WINDOWS):
            ls = slice(g * LANES, (g + 1) * LANES)
            dm = dyp_ref[:, ls]
            pg = pooled_ref[:, ls]
            dps_ref[:, ls] += jnp.sum(dm * _mm(pg, pw_ref[g]), axis=0, keepdims=True)
            dms = (dm * ps_ref[:, ls]).astype(MXU_DTYPE)
            dpw_ref[g] += _mm_tn(pg, dms)
            dpg = _mm_nt(dms, pw_ref[g])
            dpooled.append(dpg)
            ebuf[0:tm, ls] = dpg / jnp.minimum(t + 1, w).astype(F32)
        du = []
        for g, w in enumerate(POOL_WINDOWS):
            ls = slice(g * LANES, (g + 1) * LANES)
            acc = ebuf[0:tm, ls]
            for sh in range(1, w):
                acc = acc + ebuf[sh:sh + tm, ls]
            du.append(acc - dpooled[g])
        parts = [jnp.concatenate(du, axis=-1), dq, dk, dv_ref[...]]
        da = jnp.zeros((tm, d), F32)
        for p, part in enumerate(parts):
            pc = part.astype(MXU_DTYPE)
            dproj_ref[:, p * POOL_WIDTH:(p + 1) * POOL_WIDTH] = pc
            da = da + _mm_nt(pc, win_v[p])
        xv = x_ref[...]
        r = lax.rsqrt(jnp.mean(xv * xv, axis=-1, keepdims=True) + NORM_EPS)
        xn = xv * r
        dg1_ref[...] += jnp.sum(da * xn, axis=0, keepdims=True)
        dx_ref[...] = dh1_ref[...] + _rms_bwd(da * g1_ref[...], xn, r)

    tok = lambda w: pl.BlockSpec((tm, w), lambda t: (nblk - 1 - t, 0))
    const = lambda shp: pl.BlockSpec(shp, lambda t: (0,) * len(shp))
    return pl.pallas_call(
        body, name="bwd_inproj",
        grid=(nblk,),
        in_specs=[ANY] * (ns + 1) + [const(poolw.shape), const((1, POOL_WIDTH)), const((1, ATTN_WIDTH)),
                                     const((1, ATTN_WIDTH)), const((1, d))],
        out_specs=[tok(d), tok(4 * POOL_WIDTH), const((1, d)), const((1, ATTN_WIDTH)), const((1, ATTN_WIDTH)),
                   const((ngrp, LANES, LANES)), const((1, POOL_WIDTH))],
        out_shape=[jax.ShapeDtypeStruct((s, d), F32),
                   jax.ShapeDtypeStruct((s, 4 * POOL_WIDTH), MXU_DTYPE),
                   jax.ShapeDtypeStruct((1, d), F32),
                   jax.ShapeDtypeStruct((1, ATTN_WIDTH), F32),
                   jax.ShapeDtypeStruct((1, ATTN_WIDTH), F32),
                   jax.ShapeDtypeStruct((ngrp, LANES, LANES), F32),
                   jax.ShapeDtypeStruct((1, POOL_WIDTH), F32)],
        scratch_shapes=[pltpu.VMEM(win.shape, MXU_DTYPE), pltpu.VMEM((tm + POOL_HALO, POOL_WIDTH), F32)]
        + [pltpu.VMEM((RING_SLOTS, tm, a.shape[1]), a.dtype) for a in streams]
        + [pltpu.SemaphoreType.DMA((ns, RING_SLOTS))],
        compiler_params=_params(("arbitrary",)),
    )(*streams, win, poolw, pscale, qg, kg, g1)


def _wgrad_call(a, b, bm, bn, bk, out_shape, out_block, out_index, name, dep=None):
    s, m = a.shape
    _, n = b.shape
    nk = s // bk
    deps = [] if dep is None else [dep]

    def body(a_ref, b_ref, *rest):
        o_ref, wire_ref = rest[len(deps):]
        k = pl.program_id(2)
        acc = jnp.where(k > 0, o_ref[...], 0.0) + _mm_tn(a_ref[...].astype(MXU_DTYPE), b_ref[...].astype(MXU_DTYPE))
        o_ref[...] = acc
        wire_ref[...] = acc.astype(WIRE_DTYPE)

    return pl.pallas_call(
        body, name=name,
        grid=(m // bm, n // bn, nk),
        in_specs=[pl.BlockSpec((bk, bm), lambda i, j, k: (k, i)), pl.BlockSpec((bk, bn), lambda i, j, k: (k, j))]
        + [ANY] * len(deps),
        out_specs=[pl.BlockSpec(out_block, out_index)] * 2,
        out_shape=[jax.ShapeDtypeStruct(out_shape, F32), jax.ShapeDtypeStruct(out_shape, WIRE_DTYPE)],
        compiler_params=_params(("arbitrary", "arbitrary", "arbitrary")),
    )(a, b, *deps)


def _local_grads(x, tgt, g1, win, poolw, pscale, qg, kg, bias, g2, mlp_weights, on_mlp_grads=None, first_dep=None):
    s, d = x.shape
    g1r, g2r = g1.reshape(1, d), g2.reshape(1, d)
    psr = pscale.reshape(1, POOL_WIDTH)
    qgr = jnp.tile(qg, N_HEADS).reshape(1, ATTN_WIDTH)
    kgr = jnp.tile(kg, N_HEADS).reshape(1, ATTN_WIDTH)
    pw_c = poolw.astype(MXU_DTYPE)
    buckets = jnp.asarray(_bucket_tables())
    bk = min(s, 4096)

    a, pooled, ypool, q32, k32, qn, kn, v = _f1_call(x, g1r, win, pw_c, psr, qgr, kgr, tm=1024, dep=first_dep)
    o, lse = _attn_fwd_call(qn, kn, v, bias)
    wout, wup, wdown = mlp_weights(o)
    mixed, c, ff, dz, dy, dh1, dypool, do, delta, dg2, loss = _f2_call(x, tgt, ypool, o, wout, wup, wdown, g2r, tm=256)
    dff = ff.shape[1]
    g_out = [g.reshape(N_CHIPS, d // N_CHIPS, d)
             for g in _wgrad_call(mixed, dh1, d, d, bk // 4, (d, d), (d, d), lambda i, j, k: (0, 0), "wgrad_out")]
    g_up = _wgrad_call(c, dz, d, dff // N_CHIPS, bk, (N_CHIPS, d, dff // N_CHIPS), (None, d, dff // N_CHIPS),
                       lambda i, j, k: (j, 0, 0), "wgrad_up")
    g_down = _wgrad_call(ff, dy, dff // N_CHIPS, d, bk, (N_CHIPS, dff // N_CHIPS, d), (None, dff // N_CHIPS, d),
                         lambda i, j, k: (i, 0, 0), "wgrad_down")
    dep = None if on_mlp_grads is None else on_mlp_grads(g_out[1], g_up[1], g_down[1])
    dqn, dkn, dv, dbias = _attn_bwd_call(qn, kn, v, do, lse, delta, bias, dep)
    dx, dproj, dg1, dqg, dkg, dpw, dps = _bproj_call(
        dqn, dkn, dv, q32, k32, dypool, pooled, x, dh1, win, pw_c, psr, qgr, kgr, g1r, tm=512)
    nin = dproj.shape[1] // N_CHIPS
    drb = _rel_bias_grad_call(dbias, buckets)
    g_in = _wgrad_call(a, dproj, d, nin, bk, (N_CHIPS, d, nin), (None, d, nin), lambda i, j, k: (j, 0, 0), "wgrad_in",
                       dep=drb)
    small = dict(
        mix_norm_g=dg1.reshape(d), mlp_norm_g=dg2.reshape(d), pool_scale=dps.reshape(POOL_WIDTH),
        q_norm_g=dqg.reshape(ATTN_WIDTH), k_norm_g=dkg.reshape(ATTN_WIDTH),
        rel_bias=drb[:, :N_BUCKETS].T, pool_w=dpw)
    return loss[0, 0], dx, (g_in, g_out, g_up, g_down), small


def _coords():
    return lax.axis_index("x"), lax.axis_index("y"), lax.axis_index("c")


def _other_chips(x, y):
    return [(1 - x, y), (x, 1 - y), (1 - x, 1 - y)]


def _remote(src, dst, send_sem, recv_sem, dev):
    return pltpu.make_async_remote_copy(src_ref=src, dst_ref=dst, send_sem=send_sem, recv_sem=recv_sem,
                                        device_id=dev, device_id_type=MESH)


PAIR_FORWARD_ID = 1
PAIR_ALLGATHER_ID = 2


def _sibling_handshake():
    x, y, c = _coords()
    barrier = pltpu.get_barrier_semaphore()
    pl.semaphore_signal(barrier, inc=1, device_id=(x, y, 1 - c), device_id_type=MESH)
    pl.semaphore_wait(barrier, 1)


def _halves(a):
    return a.reshape(a.shape[:-2] + (2, a.shape[-2] // 2, a.shape[-1]))


def _place_shards_call(shards, chip_idx, nch):
    nw = len(shards)

    def body(chip_ref, *refs):
        for w in range(nw):
            refs[nw + w][...] = refs[w][...].astype(WIRE_DTYPE)

    in_specs = [pl.BlockSpec((s.shape[0] // nch, s.shape[1]), lambda i, chip_ref: (i, 0)) for s in shards]
    out_specs = [pl.BlockSpec((None, s.shape[0] // nch, s.shape[1]), lambda i, chip_ref: (chip_ref[0], i, 0))
                 for s in shards]
    return pl.pallas_call(
        body, name="weights_place",
        grid_spec=pltpu.PrefetchScalarGridSpec(num_scalar_prefetch=1, grid=(nch,),
                                               in_specs=in_specs, out_specs=out_specs),
        out_shape=[jax.ShapeDtypeStruct((N_CHIPS,) + s.shape, WIRE_DTYPE) for s in shards],
        compiler_params=_params(("arbitrary",)),
    )(chip_idx, *shards)


def _allgather_call(placed, from_chips, name, meanwhile=None):
    nw = len(placed)
    ncp = 3 * nw
    extra, extra_specs, extra_shape, extra_body = meanwhile if meanwhile else ([], [], None, None)
    ne = len(extra)

    def body(*refs):
        outs = refs[nw + ne:2 * nw + ne]
        send1, recv1, send2, recv2 = refs[-4:]
        x, y, c = _coords()
        chip = 2 * x + y
        others = _other_chips(x, y)
        first, passed = [], []
        if not from_chips:
            _sibling_handshake()
        if from_chips:
            for w in range(nw):
                for k, (ox, oy) in enumerate(others):
                    mine = outs[w].at[chip, c]
                    cp = _remote(mine, mine, send1.at[3 * w + k], recv1.at[3 * w + k], (ox, oy, c))
                    cp.start()
                    first.append(cp)
        if meanwhile:
            extra_body(*refs[nw:nw + ne], refs[2 * nw + ne])
        for w in range(nw):
            for k, (ox, oy) in enumerate(others):
                piece = outs[w].at[2 * ox + oy, c]
                if from_chips:
                    _remote(piece, piece, send1.at[3 * w + k], recv1.at[3 * w + k], (ox, oy, c)).wait_recv()
                cp = _remote(piece, piece, send2.at[3 * w + k], recv2.at[3 * w + k], (x, y, 1 - c))
                cp.start()
                passed.append(cp)
        for w in range(nw):
            for k, (ox, oy) in enumerate(others):
                piece = outs[w].at[2 * ox + oy, 1 - c]
                _remote(piece, piece, send2.at[3 * w + k], recv2.at[3 * w + k], (x, y, 1 - c)).wait_recv()
        for cp in first + passed:
            cp.wait_send()

    return pl.pallas_call(
        body, name=name,
        in_specs=[ANY] * nw + list(extra_specs),
        out_specs=[ANY] * nw + ([pl.BlockSpec(memory_space=pltpu.VMEM)] if meanwhile else []),
        out_shape=[jax.ShapeDtypeStruct(s.shape, s.dtype) for s in placed] + ([extra_shape] if meanwhile else []),
        input_output_aliases={w: w for w in range(nw)},
        scratch_shapes=[pltpu.SemaphoreType.DMA((ncp,))] * 4,
        compiler_params=_params() if from_chips else _params(collective_id=PAIR_FORWARD_ID),
    )(*placed, *extra)


HBM_SPEC = pl.BlockSpec(memory_space=pltpu.HBM)
SEM_SPEC = pl.BlockSpec(memory_space=pltpu.SEMAPHORE)
SPLIT_EFFECT = pltpu.SideEffectType.DATAFLOW_SIDE_EFFECTING


def _in_hbm(a):
    return pltpu.with_memory_space_constraint(a, pltpu.HBM)


def _gather_copies(bufs, send, recv):
    x, y, c = _coords()
    chip = 2 * x + y
    cps = []
    for w, buf in enumerate(bufs):
        for k, (ox, oy) in enumerate(_other_chips(x, y)):
            mine, theirs = buf.at[chip, c], buf.at[2 * ox + oy, c]
            sems = (send.at[3 * w + k], recv.at[3 * w + k], (ox, oy, c))
            cps.append((_remote(mine, mine, *sems), _remote(theirs, theirs, *sems)))
    return cps


def _gather_start_call(bufs, after):
    nw = len(bufs)

    def body(*refs):
        ins, send, recv, token = refs[:nw], refs[nw + 1], refs[nw + 2], refs[2 * nw + 3]
        for out, _ in _gather_copies(ins, send, recv):
            out.start()
        token[...] = jnp.zeros(token.shape, F32)

    res = pl.pallas_call(
        body, name="weights_gather_start",
        in_specs=[HBM_SPEC] * nw + [ANY],
        out_specs=[SEM_SPEC, SEM_SPEC] + [HBM_SPEC] * nw + [pl.BlockSpec(memory_space=pltpu.VMEM)],
        out_shape=[pltpu.SemaphoreType.DMA((3 * nw,)), pltpu.SemaphoreType.DMA((3 * nw,))]
        + [pltpu.HBM(b.shape, b.dtype) for b in bufs] + [jax.ShapeDtypeStruct((8, LANES), F32)],
        input_output_aliases={w: 2 + w for w in range(nw)},
        compiler_params=pltpu.CompilerParams(has_side_effects=SPLIT_EFFECT),
    )(*[_in_hbm(b) for b in bufs], after)
    return res[0], res[1], list(res[2:2 + nw]), res[2 + nw]


def _gather_wait_call(bufs, send, recv, after):
    nw = len(bufs)

    def body(*refs):
        ins, send, recv = refs[:nw], refs[nw], refs[nw + 1]
        for out, back in _gather_copies(ins, send, recv):
            out.wait_send()
            back.wait_recv()

    return pl.pallas_call(
        body, name="weights_gather_wait",
        in_specs=[HBM_SPEC] * nw + [SEM_SPEC, SEM_SPEC, ANY],
        out_specs=[HBM_SPEC] * nw,
        out_shape=[pltpu.HBM(b.shape, b.dtype) for b in bufs],
        input_output_aliases={w: w for w in range(nw)},
        compiler_params=pltpu.CompilerParams(has_side_effects=SPLIT_EFFECT),
    )(*bufs, send, recv, after)


def _scatter_copies(srcs, lands, send, recv, wholes):
    x, y, c = _coords()
    me = 4 * x + 2 * y + c
    cps = []
    for w, (src, land) in enumerate(zip(srcs, lands)):
        for r in range(1, N_DEV):
            px, py, pc = ((1 - x) if r & 4 else x, (1 - y) if r & 2 else y, (1 - c) if r & 1 else c)
            sems = (send.at[(N_DEV - 1) * w + r - 1], recv.at[(N_DEV - 1) * w + r - 1], (px, py, pc))
            piece = src if wholes[w] else src.at[2 * px + py, pc]
            cps.append((_remote(piece, land.at[me], *sems), _remote(piece, land.at[4 * px + 2 * py + pc], *sems)))
    return cps


def _scatter_start_call(srcs, lands, wholes, name):
    nw = len(srcs)
    ncp = (N_DEV - 1) * nw

    def body(*refs):
        ins, lnd, send, recv, token = refs[:nw], refs[nw:2 * nw], refs[2 * nw], refs[2 * nw + 1], refs[4 * nw + 2]
        for out, _ in _scatter_copies(ins, lnd, send, recv, wholes):
            out.start()
        token[...] = jnp.zeros(token.shape, F32)

    res = pl.pallas_call(
        body, name=name,
        in_specs=[HBM_SPEC] * (2 * nw),
        out_specs=[SEM_SPEC, SEM_SPEC] + [HBM_SPEC] * (2 * nw) + [pl.BlockSpec(memory_space=pltpu.VMEM)],
        out_shape=[pltpu.SemaphoreType.DMA((ncp,)), pltpu.SemaphoreType.DMA((ncp,))]
        + [pltpu.HBM(b.shape, b.dtype) for b in list(srcs) + list(lands)] + [jax.ShapeDtypeStruct((8, LANES), F32)],
        input_output_aliases={i: 2 + i for i in range(2 * nw)},
        compiler_params=pltpu.CompilerParams(has_side_effects=SPLIT_EFFECT),
    )(*[_in_hbm(b) for b in list(srcs) + list(lands)])
    return res[0], res[1], list(res[2:2 + nw]), list(res[2 + nw:2 + 2 * nw]), res[2 + 2 * nw]


def _scatter_wait_call(srcs, lands, send, recv, after, wholes, name):
    nw = len(srcs)

    def body(*refs):
        ins, lnd, send, recv = refs[:nw], refs[nw:2 * nw], refs[2 * nw], refs[2 * nw + 1]
        for out, back in _scatter_copies(ins, lnd, send, recv, wholes):
            out.wait_send()
            back.wait_recv()

    res = pl.pallas_call(
        body, name=name,
        in_specs=[HBM_SPEC] * (2 * nw) + [SEM_SPEC, SEM_SPEC, ANY],
        out_specs=[HBM_SPEC] * (2 * nw),
        out_shape=[pltpu.HBM(b.shape, b.dtype) for b in list(srcs) + list(lands)],
        input_output_aliases={i: i for i in range(2 * nw)},
        compiler_params=pltpu.CompilerParams(has_side_effects=SPLIT_EFFECT),
    )(*srcs, *lands, send, recv, after)
    return list(res[nw:])


def _reduce_call(own, lands, idx, nch, name, dep=None):
    nw = len(own)
    deps = [] if dep is None else [dep]

    def body(idx_ref, *refs):
        refs = refs[:2 * nw] + refs[2 * nw + len(deps):]
        for w in range(nw):
            tot = refs[w][...]
            for r in range(1, N_DEV):
                tot = tot + refs[nw + w][idx_ref[1 + r]].astype(F32)
            refs[2 * nw + w][...] = tot

    in_specs, out_specs, out_shape = [], [], []
    for s in own:
        in_specs.append(pl.BlockSpec((None, None, s.shape[2] // nch, s.shape[3]),
                                     lambda i, idx_ref: (idx_ref[0], idx_ref[1], i, 0)))
    for s in own:
        in_specs.append(pl.BlockSpec((N_DEV, s.shape[2] // nch, s.shape[3]), lambda i, idx_ref: (0, i, 0)))
    for s in own:
        out_specs.append(pl.BlockSpec((None, s.shape[2] // nch, s.shape[3]), lambda i, idx_ref: (idx_ref[1], i, 0)))
        out_shape.append(jax.ShapeDtypeStruct((2,) + s.shape[2:], F32))
    return pl.pallas_call(
        body, name=name,
        grid_spec=pltpu.PrefetchScalarGridSpec(num_scalar_prefetch=1, grid=(nch,),
                                               in_specs=in_specs + [ANY] * len(deps), out_specs=out_specs),
        out_shape=out_shape,
        compiler_params=_params(("arbitrary",)),
    )(idx, *own, *lands, *deps)


def _pair_allgather_call(halves, name):
    nw = len(halves)

    def body(*refs):
        outs = refs[nw:2 * nw]
        send, recv = refs[2 * nw:]
        x, y, c = _coords()
        _sibling_handshake()
        cps = []
        for w in range(nw):
            cp = _remote(outs[w].at[c], outs[w].at[c], send.at[w], recv.at[w], (x, y, 1 - c))
            cp.start()
            cps.append(cp)
        for w in range(nw):
            theirs = outs[w].at[1 - c]
            _remote(theirs, theirs, send.at[w], recv.at[w], (x, y, 1 - c)).wait_recv()
        for cp in cps:
            cp.wait_send()

    outs = pl.pallas_call(
        body, name=name,
        in_specs=[ANY] * nw, out_specs=[ANY] * nw,
        out_shape=[jax.ShapeDtypeStruct(h.shape, h.dtype) for h in halves],
        input_output_aliases={w: w for w in range(nw)},
        scratch_shapes=[pltpu.SemaphoreType.DMA((nw,))] * 2,
        compiler_params=pltpu.CompilerParams(collective_id=PAIR_ALLGATHER_ID),
    )(*halves)
    return [o.reshape(2 * h.shape[1], h.shape[2]) for o, h in zip(outs, halves)]


def _adamw(w, g, m, v):
    m = ADAM_B1 * m + (1.0 - ADAM_B1) * g
    v = ADAM_B2 * v + (1.0 - ADAM_B2) * (g * g)
    m_hat = m / (1.0 - ADAM_B1 ** ADAM_STEP)
    v_hat = v / (1.0 - ADAM_B2 ** ADAM_STEP)
    delta = -ADAM_LR * (m_hat / (jnp.sqrt(v_hat) + ADAM_EPS) + ADAM_WD * w)
    return delta, m, v


def _adamw_call(ws, gs, ms, vs, nch, name):
    nw = len(ws)

    def body(*refs):
        for w in range(nw):
            g = refs[nw + w][...]
            delta, m, v = _adamw(refs[w][...], g, refs[2 * nw + w][...], refs[3 * nw + w][...])
            refs[4 * nw + w][...] = g
            refs[5 * nw + w][...] = delta
            refs[6 * nw + w][...] = m
            refs[7 * nw + w][...] = v

    specs = [pl.BlockSpec((a.shape[0] // nch, a.shape[1]), lambda i: (i, 0)) for a in ws]
    res = pl.pallas_call(
        body, name=name,
        grid=(nch,),
        in_specs=specs * 4, out_specs=specs * 4,
        out_shape=[jax.ShapeDtypeStruct(a.shape, F32) for a in ws] * 4,
        compiler_params=_params(("arbitrary",)),
    )(*ws, *gs, *ms, *vs)
    return res[:nw], res[nw:2 * nw], res[2 * nw:3 * nw], res[3 * nw:]


def _small_call(gathered, own, me_idx, w, m, v):
    cuts = dict(mix_norm_g=(0, 0, 1024), mlp_norm_g=(1, 0, 1024), pool_scale=(2, 0, POOL_WIDTH),
                rel_bias=(2, POOL_WIDTH, N_BUCKETS * N_HEADS), q_norm_g=(3, 0, HEAD_DIM), k_norm_g=(3, LANES, HEAD_DIM))
    n_out = len(cuts) + 1

    def fold(row):
        tot = row[:, 0:LANES] + row[:, LANES:2 * LANES] + row[:, 2 * LANES:3 * LANES] + row[:, 3 * LANES:4 * LANES]
        return tot + pltpu.roll(tot, HEAD_DIM, axis=1)

    def body(me_ref, gh_ref, gp_ref, oh_ref, op_ref, wh, wp, mh, mp, vh, vp, loss_ref, *outs):
        me = me_ref[0]

        def total(ga_ref, own_ref):
            term = lambda i: jnp.where(me == i, own_ref[...], ga_ref[i]).astype(F32)
            tot = term(0)
            for i in range(1, N_DEV):
                tot = tot + term(i)
            return tot

        g_head, g_pool = total(gh_ref, oh_ref), total(gp_ref, op_ref)
        unfolded = g_head[4:5, :]
        folded = jnp.concatenate([fold(unfolded[:, :ATTN_WIDTH]), fold(unfolded[:, ATTN_WIDTH:]),
                                  jnp.zeros((1, 1024 - 2 * LANES), F32)], axis=-1)
        row = lax.broadcasted_iota(jnp.int32, g_head.shape, 0)
        g_head = jnp.where(row == 3, folded, g_head)
        loss_ref[...] = g_head[LOSS_ROW:LOSS_ROW + 1, 0:LANES]
        heads = (g_head,) + _adamw(wh[...], g_head, mh[...], vh[...])
        pools = (g_pool,) + _adamw(wp[...], g_pool, mp[...], vp[...])
        for kind in range(4):
            mine = outs[kind * n_out:(kind + 1) * n_out]
            for out, (row, at, n) in zip(mine, cuts.values()):
                out[...] = heads[kind][row:row + 1, at:at + n]
            mine[-1][...] = pools[kind]

    vmem = pl.BlockSpec(memory_space=pltpu.VMEM)
    shapes = [jax.ShapeDtypeStruct((1, n), F32) for _, _, n in cuts.values()] + [jax.ShapeDtypeStruct(w[1].shape, F32)]
    res = pl.pallas_call(
        body, name="adamw_small",
        in_specs=[pl.BlockSpec(memory_space=pltpu.SMEM)] + [vmem] * 10,
        out_shape=[jax.ShapeDtypeStruct((1, LANES), F32)] + shapes * 4,
        compiler_params=_params(),
    )(me_idx, *gathered, *own, *w, *m, *v)

    def unpack(mine):
        p = {n: a.reshape(-1) for n, a in zip(cuts, mine)}
        p["rel_bias"] = p["rel_bias"].reshape(N_BUCKETS, N_HEADS)
        p["pool_w"] = mine[-1].reshape(len(POOL_WINDOWS), LANES, LANES)
        return p

    return [res[0]] + [unpack(res[1 + kind * n_out:1 + (kind + 1) * n_out]) for kind in range(4)]


def _pack_small(p, folded=True, loss=None):
    z = lambda n: jnp.zeros((n,), F32)
    rows = [p["mix_norm_g"], p["mlp_norm_g"],
            jnp.concatenate([p["pool_scale"], p["rel_bias"].reshape(-1), z(1024 - POOL_WIDTH - N_BUCKETS * N_HEADS)])]
    if folded:
        rows += [jnp.concatenate([p["q_norm_g"], z(LANES - HEAD_DIM), p["k_norm_g"], z(1024 - LANES - HEAD_DIM)]), z(1024)]
    else:
        rows += [z(1024), jnp.concatenate([p["q_norm_g"], p["k_norm_g"]])]
    rows += [z(1024) if loss is None else jnp.concatenate([loss.reshape(1), z(1023)])]
    return jnp.stack(rows + [z(1024)] * 2), p["pool_w"].reshape(-1, LANES)


_WEIGHT_ORDER = ("mix_norm_g", "w_in", "pool_w", "pool_scale", "q_norm_g", "k_norm_g", "rel_bias", "w_out",
                 "mlp_norm_g", "w_up", "w_down")
_BIG = ("w_in", "w_out", "w_up", "w_down")


def kernel(x, mix_norm_g, w_in, pool_w, pool_scale, q_norm_g, k_norm_g, rel_bias, w_out, mlp_norm_g, w_up, w_down, loss_target, m_mix_norm_g, m_w_in, m_pool_w, m_pool_scale, m_q_norm_g, m_k_norm_g, m_rel_bias, m_w_out, m_mlp_norm_g, m_w_up, m_w_down, v_mix_norm_g, v_w_in, v_pool_w, v_pool_scale, v_q_norm_g, v_k_norm_g, v_rel_bias, v_w_out, v_mlp_norm_g, v_w_up, v_w_down):
    w = dict(mix_norm_g=mix_norm_g, w_in=w_in, pool_w=pool_w, pool_scale=pool_scale, q_norm_g=q_norm_g,
             k_norm_g=k_norm_g, rel_bias=rel_bias, w_out=w_out, mlp_norm_g=mlp_norm_g, w_up=w_up, w_down=w_down)
    m = dict(mix_norm_g=m_mix_norm_g, w_in=m_w_in, pool_w=m_pool_w, pool_scale=m_pool_scale, q_norm_g=m_q_norm_g,
             k_norm_g=m_k_norm_g, rel_bias=m_rel_bias, w_out=m_w_out, mlp_norm_g=m_mlp_norm_g, w_up=m_w_up, w_down=m_w_down)
    v = dict(mix_norm_g=v_mix_norm_g, w_in=v_w_in, pool_w=v_pool_w, pool_scale=v_pool_scale, q_norm_g=v_q_norm_g,
             k_norm_g=v_k_norm_g, rel_bias=v_rel_bias, w_out=v_w_out, mlp_norm_g=v_mlp_norm_g, w_up=v_w_up, w_down=v_w_down)
    xc, yc, cc = _coords()

    c_idx = jnp.reshape(cc, (1,)).astype(jnp.int32)
    chip_idx = jnp.reshape(2 * xc + yc, (1,)).astype(jnp.int32)
    me = 4 * xc + 2 * yc + cc
    whole = lambda t: t.reshape(t.shape[0], t.shape[1] * t.shape[2], t.shape[3])

    placed = [_halves(p) for p in _place_shards_call([w[n] for n in _BIG], chip_idx, nch=4)]
    win_f, bias = _allgather_call(placed[:1], from_chips=True, name="weights_allgather_in",
                                  meanwhile=_bias_table_work(rel_bias))
    wsend, wrecv, in_flight, started = _gather_start_call(placed[1:], win_f)

    def mlp_weights(after):
        landed = _gather_wait_call(in_flight, wsend, wrecv, after)
        wout_f, wup_f, wdown_f = _allgather_call(landed, from_chips=False, name="weights_pair_forward")
        return whole(wout_f).reshape(-1, wout_f.shape[-1]), whole(wup_f), whole(wdown_f)

    split = []

    def on_mlp_grads(*wire_grads):
        srcs = [_halves(g) for g in wire_grads]
        lands = [lax.empty((N_DEV,) + s.shape[2:], s.dtype) for s in srcs]
        split.extend(_scatter_start_call(srcs, lands, [False] * len(srcs), "grads_scatter_start"))
        return split[4]

    loss_part, dx, big_grads, small_grads = _local_grads(
        x[0], loss_target[0], mix_norm_g, whole(win_f), pool_w, pool_scale, q_norm_g, k_norm_g, bias,
        mlp_norm_g, mlp_weights, on_mlp_grads, first_dep=started)
    g_in, g_out, g_up, g_down = big_grads
    gsend, grecv, srcs_thru, lands_thru, _ = split
    lands_mlp = _scatter_wait_call(srcs_thru, lands_thru, gsend, grecv, g_in[1], [False] * 3, "grads_scatter_wait")

    head_own, pool_own = _pack_small(small_grads, folded=False, loss=loss_part)
    small_own = (head_own, pool_own.astype(WIRE_DTYPE))
    last_srcs = [_halves(g_in[1]), *small_own]
    last_lands = [lax.empty((N_DEV,) + last_srcs[0].shape[2:], WIRE_DTYPE)]
    last_lands += [lax.empty((N_DEV,) + a.shape, a.dtype) for a in small_own]
    lsend, lrecv, last_srcs, last_lands, last_started = _scatter_start_call(
        last_srcs, last_lands, [False, True, True], "grads_scatter_start_last")
    idx = jnp.concatenate([chip_idx, c_idx] + [jnp.reshape(jnp.bitwise_xor(me, r), (1,)) for r in range(1, N_DEV)])
    idx = idx.astype(jnp.int32)
    mlp = _BIG[1:]

    def update(names, own32, lands, tag, dep=None):
        halves = _reduce_call([_halves(g) for g in own32], lands, idx, 4, "grads_reduce_" + tag, dep)
        reduced = _pair_allgather_call(list(halves), "grads_pair_allgather_" + tag)
        return _adamw_call([w[n] for n in names], reduced, [m[n] for n in names], [v[n] for n in names], 8, "adamw_" + tag)

    out_mlp = update(mlp, [g_out[0], g_up[0], g_down[0]], lands_mlp, "mlp", last_started)
    land_in, *small_all = _scatter_wait_call(last_srcs, last_lands, lsend, lrecv, out_mlp[3][-1], [False, True, True],
                                             "grads_scatter_wait_last")
    out_in = update(_BIG[:1], [g_in[0]], [land_in], "in")
    loss_row, grads, deltas, new_m, new_v = _small_call(
        small_all, small_own, jnp.reshape(me, (1,)).astype(jnp.int32), _pack_small(w), _pack_small(m), _pack_small(v))

    for k, res in enumerate((grads, deltas, new_m, new_v)):
        res[_BIG[0]] = out_in[k][0]
        for i, n in enumerate(mlp):
            res[n] = out_mlp[k][i]
    loss = loss_row[0, 0]
    return (loss, dx[None], *[grads[n] for n in _WEIGHT_ORDER], *[deltas[n] for n in _WEIGHT_ORDER],
            *[new_m[n] for n in _WEIGHT_ORDER], *[new_v[n] for n in _WEIGHT_ORDER])
```

```python
import math

import jax
import jax.numpy as jnp
import numpy as np
from jax import lax
from jax.experimental import pallas as pl
from jax.experimental.pallas import tpu as pltpu

F32 = jnp.float32
MXU_DTYPE = jnp.bfloat16
WIRE_DTYPE = jnp.bfloat16

NORM_EPS = 1e-6
NEG_INF = -1e30
LANES = 128
HEAD_DIM = 64
N_HEADS = 8
POOL_WIDTH = 512
ATTN_WIDTH = 512
POOL_WINDOWS = (2, 4, 8, 16)
POOL_HALO = 16
DILATED_PATTERNS = ((128, 1), (512, 4), (2048, 16))
ATT_BLOCK = 128
ATT_SUPER = ATT_BLOCK * max(dl for _, dl in DILATED_PATTERNS)
ATT_UNITS = ATT_SUPER // ATT_BLOCK
N_BUCKETS = 32
NO_BUCKET = -1
MAX_DISTANCE = 2048
N_CHIPS = 4
N_DEV = 8
ADAM_LR, ADAM_B1, ADAM_B2, ADAM_EPS, ADAM_WD, ADAM_STEP = 0.001, 0.9, 0.999, 1e-08, 0.01, 10
VMEM_LIMIT = 56 * 1024 * 1024
MESH = pl.DeviceIdType.MESH
ANY = pl.BlockSpec(memory_space=pl.ANY)

LOSS_ROW = 5


def _mm(a, b):
    return jnp.dot(a, b, preferred_element_type=F32)


def _mm_nt(a, b):
    return lax.dot_general(a, b, (((1,), (1,)), ((), ())), preferred_element_type=F32)


def _mm_tn(a, b):
    return lax.dot_general(a, b, (((0,), (0,)), ((), ())), preferred_element_type=F32)


def _params(sem=None, **kw):
    if sem is not None:
        kw["dimension_semantics"] = sem
    return pltpu.CompilerParams(vmem_limit_bytes=VMEM_LIMIT, **kw)


def _low_half():
    return lax.broadcasted_iota(jnp.int32, (1, LANES), 1) < HEAD_DIM


def _head_sum_bcast(y):
    lo = _low_half()
    outs = []
    for j in range(y.shape[1] // LANES):
        c = y[:, j * LANES:(j + 1) * LANES]
        s_lo = jnp.sum(jnp.where(lo, c, 0.0), axis=-1, keepdims=True)
        s_hi = jnp.sum(jnp.where(lo, 0.0, c), axis=-1, keepdims=True)
        outs.append(jnp.where(lo, s_lo, s_hi))
    return jnp.concatenate(outs, axis=-1)


def _rms_bwd(dn, hn, r):
    return r * (dn - hn * jnp.mean(dn * hn, axis=-1, keepdims=True))


def _t5_bucket_np(dist):
    max_exact = N_BUCKETS // 2
    d_f = np.maximum(dist, 1).astype(np.float32)
    ratio = (np.log(d_f / np.float32(max_exact)) / np.float32(math.log(MAX_DISTANCE / max_exact))).astype(np.float32)
    large = max_exact + (ratio * np.float32(N_BUCKETS - max_exact)).astype(np.int32)
    large = np.minimum(large, N_BUCKETS - 1)
    return np.where(dist < max_exact, dist, large).astype(np.int32)


def _window_offsets(dl):
    if dl == 1:
        return _by4_positions(ATT_BLOCK), _by4_positions(2 * ATT_BLOCK)
    return np.arange(ATT_BLOCK), np.arange(2 * ATT_BLOCK)


def _bucket_tables():
    tables = []
    for _, dl in DILATED_PATTERNS:
        qq, kk = _window_offsets(dl)
        dist = qq[:, None] + ATT_BLOCK - kk[None, :]
        bucket = _t5_bucket_np(np.clip(dist, 0, ATT_BLOCK) * dl)
        tables.append(np.where((dist >= 0) & (dist <= ATT_BLOCK), bucket, NO_BUCKET))
    return np.stack(tables).astype(np.int32)


def _previous_block_keys():
    return np.stack([np.broadcast_to(_window_offsets(dl)[1][None, :] < ATT_BLOCK, (ATT_BLOCK, 2 * ATT_BLOCK))
                     for _, dl in DILATED_PATTERNS])


def _f1_call(x, g1, win, poolw, pscale, qg, kg, tm, dep=None):
    s, d = x.shape
    nblk = s // tm
    deps = [] if dep is None else [dep]

    def body(x_ref, g1_ref, win_ref, pw_ref, ps_ref, qg_ref, kg_ref, *rest):
        a_ref, pooled_ref, ypool_ref, q32_ref, k32_ref, qn_ref, kn_ref, v_ref, ubuf = rest[len(deps):]
        i = pl.program_id(0)
        xv = x_ref[...]
        r = lax.rsqrt(jnp.mean(xv * xv, axis=-1, keepdims=True) + NORM_EPS)
        a = ((xv * r) * g1_ref[...]).astype(MXU_DTYPE)
        a_ref[...] = a
        u = _mm(a, win_ref[0])
        q = _mm(a, win_ref[1])
        k = _mm(a, win_ref[2])
        v_ref[...] = _mm(a, win_ref[3])
        q32_ref[...] = q
        k32_ref[...] = k
        rq = lax.rsqrt(_head_sum_bcast(q * q) * (1.0 / HEAD_DIM) + NORM_EPS)
        qn_ref[...] = ((q * rq) * qg_ref[...]) * (HEAD_DIM ** -0.5)
        rk = lax.rsqrt(_head_sum_bcast(k * k) * (1.0 / HEAD_DIM) + NORM_EPS)
        kn_ref[...] = (k * rk) * kg_ref[...]

        ubuf[0:POOL_HALO, :] = jnp.where(i > 0, ubuf[tm:tm + POOL_HALO, :], 0.0)
        ubuf[POOL_HALO:POOL_HALO + tm, :] = u
        t = i * tm + lax.broadcasted_iota(jnp.int32, (tm, 1), 0)
        for g, w in enumerate(POOL_WINDOWS):
            ls = slice(g * LANES, (g + 1) * LANES)
            ug = u[:, ls]
            acc = ug
            for sh in range(1, w):
                acc = acc + ubuf[POOL_HALO - sh:POOL_HALO - sh + tm, ls]
            cnt = jnp.minimum(t + 1, w).astype(F32)
            pooled = (acc / cnt - ug).astype(MXU_DTYPE)
            pooled_ref[:, ls] = pooled
            ypool_ref[:, ls] = (_mm(pooled, pw_ref[g].astype(MXU_DTYPE)) * ps_ref[:, ls]).astype(MXU_DTYPE)

    tok = lambda w: pl.BlockSpec((tm, w), lambda i: (i, 0))
    full = lambda shp: pl.BlockSpec(shp, lambda i: (0,) * len(shp))
    return pl.pallas_call(
        body, name="fwd_inproj",
        grid=(nblk,),
        in_specs=[tok(d), full((1, d)), full(win.shape), full(poolw.shape), full((1, POOL_WIDTH)),
                  full((1, ATTN_WIDTH)), full((1, ATTN_WIDTH))] + [ANY] * len(deps),
        out_specs=[tok(d), tok(POOL_WIDTH), tok(POOL_WIDTH), tok(ATTN_WIDTH), tok(ATTN_WIDTH),
                   tok(ATTN_WIDTH), tok(ATTN_WIDTH), tok(ATTN_WIDTH)],
        out_shape=[jax.ShapeDtypeStruct((s, d), MXU_DTYPE),
                   jax.ShapeDtypeStruct((s, POOL_WIDTH), MXU_DTYPE),
                   jax.ShapeDtypeStruct((s, POOL_WIDTH), MXU_DTYPE),
                   jax.ShapeDtypeStruct((s, ATTN_WIDTH), F32),
                   jax.ShapeDtypeStruct((s, ATTN_WIDTH), F32),
                   jax.ShapeDtypeStruct((s, ATTN_WIDTH), F32),
                   jax.ShapeDtypeStruct((s, ATTN_WIDTH), F32),
                   jax.ShapeDtypeStruct((s, ATTN_WIDTH), F32)],
        scratch_shapes=[pltpu.VMEM((tm + POOL_HALO, POOL_WIDTH), F32)],
        compiler_params=_params(("arbitrary",)),
    )(x, g1, win, poolw, pscale, qg, kg, *deps)


DEINT = 4
assert [dl for _, dl in DILATED_PATTERNS] == [1, DEINT, DEINT * DEINT]


def _by4_positions(n):
    pos = np.arange(n)
    return DEINT * (pos % (n // DEINT)) + pos // (n // DEINT)


def _masked_bias(b_ref, p, n):
    return b_ref[p, jnp.minimum(n, 1)].reshape(2 * ATT_BLOCK, 2 * ATT_BLOCK)


def _unit_rows(u, dl):
    assert isinstance(u, int)
    sq, sk = ATT_SUPER // DEINT, 2 * ATT_SUPER // DEINT
    if dl == 1:
        n = ATT_BLOCK // DEINT
        return (u, [pl.ds(r * sq + n * u, n) for r in range(DEINT)],
                [pl.ds(r * sk + sk // 2 + n * (u - 1), 2 * n) for r in range(DEINT)])
    if dl == DEINT:
        r, b = u % DEINT, u // DEINT
        return (b, [pl.ds(r * sq + ATT_BLOCK * b, ATT_BLOCK)],
                [pl.ds(r * sk + sk // 2 + ATT_BLOCK * (b - 1), 2 * ATT_BLOCK)])
    r, a = u % DEINT, u // DEINT
    return 0, [pl.ds(r * sq + a, ATT_BLOCK, stride=DEINT)], [pl.ds(r * sk + a, 2 * ATT_BLOCK, stride=DEINT)]


def _take(ref, runs):
    parts = [ref[run, :] for run in runs]
    return parts[0] if len(parts) == 1 else jnp.concatenate(parts, axis=0)


def _put(ref, runs, value, add=False):
    n = value.shape[0] // len(runs)
    for i, run in enumerate(runs):
        part = value[i * n:(i + 1) * n]
        ref[run, :] = ref[run, :] + part if add else part


def _deinterleave(dst, src, n):
    seg = n // DEINT
    for r in range(DEINT):
        dst[r * seg:(r + 1) * seg, :] = src[pl.ds(r, seg, stride=DEINT), :]


def _deinterleave_pair(dst, prev, cur):
    seg = prev.shape[0] // DEINT
    for r in range(DEINT):
        dst[2 * r * seg:(2 * r + 1) * seg, :] = prev[pl.ds(r, seg, stride=DEINT), :]
        dst[(2 * r + 1) * seg:(2 * r + 2) * seg, :] = cur[pl.ds(r, seg, stride=DEINT), :]


def _interleave(dst, src, n, offset=0):
    seg = n // DEINT
    stride = src.shape[0] // DEINT
    for r in range(DEINT):
        dst[pl.ds(r, seg, stride=DEINT), :] = src[r * stride + offset:r * stride + offset + seg, :]


def _attn_fwd_call(qn, kn, v, bias):
    s, w = qn.shape
    nsb = s // ATT_SUPER
    npair = w // LANES

    def body(q_ref, kc_ref, vc_ref, b_ref, o_ref, lse_ref, qf, kf, vf, acc_s, m_s, l_s):
        sb = pl.program_id(1)

        @pl.when((pl.program_id(0) == 0) & (sb == 0))
        def _():
            kf[...] = jnp.zeros_like(kf)
            vf[...] = jnp.zeros_like(vf)

        seg = ATT_SUPER // DEINT
        _deinterleave(qf, q_ref, ATT_SUPER)
        for r in range(DEINT):
            for dst, src in ((kf, kc_ref), (vf, vc_ref)):
                dst[2 * r * seg:(2 * r + 1) * seg, :] = dst[(2 * r + 1) * seg:(2 * r + 2) * seg, :]
                dst[(2 * r + 1) * seg:(2 * r + 2) * seg, :] = src[pl.ds(r, seg, stride=DEINT), :]
        lo = _low_half()
        for p, (_, dl) in enumerate(DILATED_PATTERNS):
            def unit(u, carry, p=p, dl=dl):
                b, rows_q, rows_k = _unit_rows(u, dl)
                qp = _take(qf, rows_q).astype(MXU_DTYPE)
                kcat = _take(kf, rows_k).astype(MXU_DTYPE)
                vcat = _take(vf, rows_k).astype(MXU_DTYPE)
                zero = jnp.zeros_like(qp)
                q2 = jnp.concatenate([jnp.where(lo, qp, zero), jnp.where(lo, zero, qp)], axis=0)
                sc = _mm_nt(q2, kcat) + _masked_bias(b_ref, p, sb * (ATT_UNITS // dl) + b)
                m2 = jnp.max(sc, axis=-1, keepdims=True)
                pr = jnp.exp(sc - m2)
                l2 = jnp.sum(pr, axis=-1, keepdims=True)
                acc2 = _mm(pr.astype(MXU_DTYPE), vcat)
                acc = jnp.where(lo, acc2[:ATT_BLOCK], acc2[ATT_BLOCK:])
                m = jnp.where(lo, m2[:ATT_BLOCK], m2[ATT_BLOCK:])
                l = jnp.where(lo, l2[:ATT_BLOCK], l2[ATT_BLOCK:])
                if p == 0:
                    _put(acc_s, rows_q, acc)
                    _put(m_s, rows_q, m)
                    _put(l_s, rows_q, l)
                else:
                    m_old = _take(m_s, rows_q)
                    m_new = jnp.maximum(m_old, m)
                    a_old = jnp.exp(m_old - m_new)
                    a_new = jnp.exp(m - m_new)
                    _put(acc_s, rows_q, a_old * _take(acc_s, rows_q) + a_new * acc)
                    _put(l_s, rows_q, a_old * _take(l_s, rows_q) + a_new * l)
                    _put(m_s, rows_q, m_new)
                return carry

            for u in range(ATT_UNITS):
                unit(u, None)
        l = l_s[...]
        acc_s[...] = acc_s[...] / l
        m_s[...] = m_s[...] + jnp.log(l)
        _interleave(o_ref, acc_s, ATT_SUPER)
        _interleave(lse_ref, m_s, ATT_SUPER)

    cur = pl.BlockSpec((ATT_SUPER, LANES), lambda j, t: (t, j))
    bspec = pl.BlockSpec((len(DILATED_PATTERNS), 2, 2, ATT_BLOCK, 2 * ATT_BLOCK), lambda j, t: (0, 0, j, 0, 0))
    return pl.pallas_call(
        body, name="attn_fwd",
        grid=(npair, nsb),
        in_specs=[cur, cur, cur, bspec],
        out_specs=[cur, cur],
        out_shape=[jax.ShapeDtypeStruct((s, w), F32), jax.ShapeDtypeStruct((s, w), F32)],
        scratch_shapes=[pltpu.VMEM((ATT_SUPER, LANES), F32), pltpu.VMEM((2 * ATT_SUPER, LANES), F32),
                        pltpu.VMEM((2 * ATT_SUPER, LANES), F32), pltpu.VMEM((ATT_SUPER, LANES), F32),
                        pltpu.VMEM((ATT_SUPER, LANES), F32), pltpu.VMEM((ATT_SUPER, LANES), F32)],
        compiler_params=_params(("arbitrary", "arbitrary")),
    )(qn, kn, v, bias)


def _attn_bwd_call(qn, kn, v, do, lse, delta, bias, dep=None):
    s, w = qn.shape
    nsb = s // ATT_SUPER
    npair = w // LANES
    deps = [] if dep is None else [dep]

    def body(q_ref, kc_ref, kp_ref, vc_ref, vp_ref, do_ref, lse_ref, dlt_ref, b_ref, *rest):
        dq_ref, dk_ref, dv_ref, db_ref, qf, kf, vf, dof, lsef, dltf, dqf, dkf, dvf = rest[len(deps):]
        step = pl.program_id(1)
        sb = nsb - 1 - step
        seg = ATT_SUPER // DEINT
        _deinterleave(qf, q_ref, ATT_SUPER)
        _deinterleave(dof, do_ref, ATT_SUPER)
        _deinterleave_pair(kf, kp_ref, kc_ref)
        _deinterleave_pair(vf, vp_ref, vc_ref)
        _deinterleave(lsef, lse_ref, ATT_SUPER)
        _deinterleave(dltf, dlt_ref, ATT_SUPER)

        db_ref[...] = jnp.where(step > 0, db_ref[...], 0.0)
        for acc in (dkf, dvf):
            for r in range(DEINT):
                this, before = pl.ds((2 * r + 1) * seg, seg), pl.ds(2 * r * seg, seg)
                acc[this, :] = jnp.where(step > 0, acc[before, :], 0.0)
                acc[before, :] = jnp.zeros((seg, LANES), F32)
        lo = _low_half()
        for p, (_, dl) in enumerate(DILATED_PATTERNS):
            def unit(u, carry, p=p, dl=dl):
                b, rows_q, rows_k = _unit_rows(u, dl)
                qp = _take(qf, rows_q).astype(MXU_DTYPE)
                dop = _take(dof, rows_q).astype(MXU_DTYPE)
                kcat = _take(kf, rows_k).astype(MXU_DTYPE)
                vcat = _take(vf, rows_k).astype(MXU_DTYPE)
                lse2 = _take(lsef, rows_q)
                dlt2 = _take(dltf, rows_q)
                zero = jnp.zeros_like(qp)
                q2 = jnp.concatenate([jnp.where(lo, qp, zero), jnp.where(lo, zero, qp)], axis=0)
                do2 = jnp.concatenate([jnp.where(lo, dop, zero), jnp.where(lo, zero, dop)], axis=0)
                lse_c = jnp.concatenate([lse2[:, 0:1], lse2[:, HEAD_DIM:HEAD_DIM + 1]], axis=0)
                dlt_c = jnp.concatenate([dlt2[:, 0:1], dlt2[:, HEAD_DIM:HEAD_DIM + 1]], axis=0)
                sc = _mm_nt(q2, kcat) + _masked_bias(b_ref, p, sb * (ATT_UNITS // dl) + b)
                pr = jnp.exp(sc - lse_c)
                ds = pr * (_mm_nt(do2, vcat) - dlt_c)
                db_ref[p] += ds.reshape(2, ATT_BLOCK, 2 * ATT_BLOCK)
                ds_c = ds.astype(MXU_DTYPE)
                dq2 = _mm(ds_c, kcat)
                dk = _mm_tn(ds_c, q2)
                dv = _mm_tn(pr.astype(MXU_DTYPE), do2)
                dq = jnp.where(lo, dq2[:ATT_BLOCK], dq2[ATT_BLOCK:])
                _put(dqf, rows_q, dq, add=p > 0)
                _put(dkf, rows_k, dk, add=True)
                _put(dvf, rows_k, dv, add=True)
                return carry

            for u in range(ATT_UNITS):
                unit(u, None)
        _interleave(dq_ref, dqf, ATT_SUPER)
        _interleave(dk_ref, dkf, ATT_SUPER, offset=seg)
        _interleave(dv_ref, dvf, ATT_SUPER, offset=seg)

    cur = pl.BlockSpec((ATT_SUPER, LANES), lambda j, t: (nsb - 1 - t, j))
    prev = pl.BlockSpec((ATT_SUPER, LANES), lambda j, t: (jnp.maximum(nsb - 2 - t, 0), j))
    npat = len(DILATED_PATTERNS)
    bspec = pl.BlockSpec((npat, 2, 2, ATT_BLOCK, 2 * ATT_BLOCK), lambda j, t: (0, 0, j, 0, 0))
    dbspec = pl.BlockSpec((npat, 2, ATT_BLOCK, 2 * ATT_BLOCK), lambda j, t: (0, j, 0, 0))
    sup = lambda: pltpu.VMEM((ATT_SUPER, LANES), F32)
    sup2 = lambda: pltpu.VMEM((2 * ATT_SUPER, LANES), F32)
    return pl.pallas_call(
        body, name="attn_bwd",
        grid=(npair, nsb),
        in_specs=[cur, cur, prev, cur, prev, cur, cur, cur, bspec] + [ANY] * len(deps),
        out_specs=[cur, cur, cur, dbspec],
        out_shape=[jax.ShapeDtypeStruct((s, w), F32)] * 3
        + [jax.ShapeDtypeStruct((npat, N_HEADS, ATT_BLOCK, 2 * ATT_BLOCK), F32)],
        scratch_shapes=[sup(), sup2(), sup2(), sup(), sup(), sup(), sup(), sup2(), sup2()],
        compiler_params=_params(("arbitrary", "arbitrary")),
    )(qn, kn, kn, v, v, do, lse, delta, bias, *deps)


def _bias_table_work(rel_bias):
    buckets = jnp.asarray(_bucket_tables())
    prev_keys = jnp.asarray(_previous_block_keys().astype(np.int32))
    npat = buckets.shape[0]

    def body(rb_ref, bk_ref, pk_ref, out_ref):
        for p in range(npat):
            for half in range(2):
                ks = slice(half * ATT_BLOCK, (half + 1) * ATT_BLOCK)
                bk = bk_ref[p, :, ks]
                absent = pk_ref[p, :, ks] != 0
                for h in range(N_HEADS):
                    def pick(b, acc, h=h, bk=bk):
                        return jnp.where(bk == b, rb_ref[b, h], acc)

                    tab = lax.fori_loop(0, N_BUCKETS, pick, jnp.full((ATT_BLOCK, ATT_BLOCK), NEG_INF, F32))
                    out_ref[p, 1, h, :, ks] = tab
                    out_ref[p, 0, h, :, ks] = jnp.where(absent, NEG_INF, tab)

    vmem = pl.BlockSpec(memory_space=pltpu.VMEM)
    return ([rel_bias, buckets, prev_keys], [pl.BlockSpec(memory_space=pltpu.SMEM), vmem, vmem],
            jax.ShapeDtypeStruct((npat, 2, N_HEADS, ATT_BLOCK, 2 * ATT_BLOCK), F32), body)


def _rel_bias_grad_call(dbias, buckets):
    npat, nh = dbias.shape[0], dbias.shape[1]

    def body(db_ref, bk_ref, out_ref):
        lane = lax.broadcasted_iota(jnp.int32, (nh, LANES), 1)
        out = jnp.zeros((nh, LANES), F32)
        for b in range(N_BUCKETS):
            tot = jnp.zeros((nh, 1), F32)
            for p in range(npat):
                hit = jnp.where(bk_ref[p][None] == b, db_ref[p], 0.0)
                tot = tot + jnp.sum(jnp.sum(hit, axis=1), axis=-1, keepdims=True)
            out = jnp.where(lane == b, tot, out)
        out_ref[...] = out

    return pl.pallas_call(
        body, name="rel_bias_grad",
        out_shape=jax.ShapeDtypeStruct((nh, LANES), F32),
        compiler_params=_params(),
    )(dbias, buckets)


def _f2_call(x, tgt, ypool, o, wout, wup, wdown, g2, tm):
    s, d = x.shape
    nblk = s // tm
    nch, _, fch = wup.shape
    dff = nch * fch
    mixw = POOL_WIDTH + ATTN_WIDTH

    def body(x_ref, t_ref, yp_ref, o_ref, g2_ref, wout_hbm, wup_hbm, wdown_hbm,
             mixed_ref, c_ref, ff_ref, dz_ref, dy_ref, dh1_ref, dyp_ref, do_ref, dlt_ref, dg2_ref, loss_ref,
             wout_v, wup_v, wdown_v, rz, wsem):
        i = pl.program_id(0)

        @pl.when(i == 0)
        def _():
            copies = [pltpu.make_async_copy(wout_hbm, wout_v, wsem.at[0])]
            for j in range(nch):
                copies.append(pltpu.make_async_copy(wup_hbm.at[j], wup_v.at[j], wsem.at[1 + 2 * j]))
                copies.append(pltpu.make_async_copy(wdown_hbm.at[j], wdown_v.at[j], wsem.at[2 + 2 * j]))
            for cp in copies:
                cp.start()
            dg2_ref[...] = jnp.zeros(dg2_ref.shape, F32)
            loss_ref[...] = jnp.zeros(loss_ref.shape, F32)
            for cp in copies:
                cp.wait()

        o = o_ref[...]
        mixed = jnp.concatenate([yp_ref[...], o.astype(MXU_DTYPE)], axis=-1)
        mixed_ref[...] = mixed
        h1 = x_ref[...] + _mm(mixed, wout_v[...])
        r2 = lax.rsqrt(jnp.mean(h1 * h1, axis=-1, keepdims=True) + NORM_EPS)
        hn = h1 * r2
        c = (hn * g2_ref[...]).astype(MXU_DTYPE)
        c_ref[...] = c
        y = h1
        for j in range(nch):
            cs = slice(j * fch, (j + 1) * fch)
            z = jnp.maximum(_mm(c, wup_v[j]), 0.0)
            rz[:, cs] = z
            ff = (z * z).astype(MXU_DTYPE)
            ff_ref[:, cs] = ff
            y = y + _mm(ff, wdown_v[j])
        err = y - t_ref[...]
        loss_ref[...] += jnp.sum(err * err) * (0.5 / d)
        dy = err * (1.0 / d)
        dy_c = dy.astype(MXU_DTYPE)
        dy_ref[...] = dy_c
        dc = jnp.zeros((tm, d), F32)
        for j in range(nch):
            cs = slice(j * fch, (j + 1) * fch)
            dz = (_mm_nt(dy_c, wdown_v[j]) * (2.0 * rz[:, cs])).astype(MXU_DTYPE)
            dz_ref[:, cs] = dz
            dc = dc + _mm_nt(dz, wup_v[j])
        dg2_ref[...] += jnp.sum(dc * hn, axis=0, keepdims=True)
        dh1 = dy + _rms_bwd(dc * g2_ref[...], hn, r2)
        dh1_ref[...] = dh1
        dmix = _mm_nt(dh1.astype(MXU_DTYPE), wout_v[...])
        dyp_ref[...] = dmix[:, :POOL_WIDTH]
        do = dmix[:, POOL_WIDTH:]
        do_ref[...] = do
        dlt_ref[...] = _head_sum_bcast(do * o)

    tok = lambda w: pl.BlockSpec((tm, w), lambda i: (i, 0))
    const = lambda shp: pl.BlockSpec(shp, lambda i: (0,) * len(shp))
    return pl.pallas_call(
        body, name="fwd_mlp_bwd_mlp",
        grid=(nblk,),
        in_specs=[tok(d), tok(d), tok(POOL_WIDTH), tok(ATTN_WIDTH), const((1, d)), ANY, ANY, ANY],
        out_specs=[tok(mixw), tok(d), tok(dff), tok(dff), tok(d), tok(d), tok(POOL_WIDTH), tok(ATTN_WIDTH),
                   tok(ATTN_WIDTH), const((1, d)), const((1, LANES))],
        out_shape=[jax.ShapeDtypeStruct((s, mixw), MXU_DTYPE),
                   jax.ShapeDtypeStruct((s, d), MXU_DTYPE),
                   jax.ShapeDtypeStruct((s, dff), MXU_DTYPE),
                   jax.ShapeDtypeStruct((s, dff), MXU_DTYPE),
                   jax.ShapeDtypeStruct((s, d), MXU_DTYPE),
                   jax.ShapeDtypeStruct((s, d), F32),
                   jax.ShapeDtypeStruct((s, POOL_WIDTH), F32),
                   jax.ShapeDtypeStruct((s, ATTN_WIDTH), F32),
                   jax.ShapeDtypeStruct((s, ATTN_WIDTH), F32),
                   jax.ShapeDtypeStruct((1, d), F32),
                   jax.ShapeDtypeStruct((1, LANES), F32)],
        scratch_shapes=[pltpu.VMEM(wout.shape, MXU_DTYPE), pltpu.VMEM(wup.shape, MXU_DTYPE),
                        pltpu.VMEM(wdown.shape, MXU_DTYPE), pltpu.VMEM((tm, dff), F32),
                        pltpu.SemaphoreType.DMA((1 + 2 * nch,))],
        compiler_params=_params(("arbitrary",)),
    )(x, tgt, ypool, o, g2, wout, wup, wdown)


RING_SLOTS = 3


def _bproj_call(dqn, dkn, dv, q32, k32, dypool, pooled, x, dh1, win, poolw, pscale, qg, kg, g1, tm):
    s, d = x.shape
    nblk = s // tm
    ngrp = len(POOL_WINDOWS)
    streams = [dqn, dkn, dv, q32, k32, dypool, pooled, x, dh1]
    ns = len(streams)
    assert nblk >= 2

    def body(*refs):
        hbm = refs[:ns]
        win_hbm, pw_ref, ps_ref, qg_ref, kg_ref, g1_ref = refs[ns:ns + 6]
        dx_ref, dproj_ref, dg1_ref, dqg_ref, dkg_ref, dpw_ref, dps_ref, win_v, ebuf = refs[ns + 6:ns + 15]
        rings, sems = refs[ns + 15:2 * ns + 15], refs[2 * ns + 15]
        step = pl.program_id(0)
        i = nblk - 1 - step

        def fetch(t):
            rows = pl.ds(pl.multiple_of((nblk - 1 - t) * tm, tm), tm)
            return [pltpu.make_async_copy(h.at[rows], ring.at[t % RING_SLOTS], sems.at[k, t % RING_SLOTS])
                    for k, (h, ring) in enumerate(zip(hbm, rings))]

        @pl.when(step == 0)
        def _():
            for t in range(2):
                for cp in fetch(t):
                    cp.start()
            pltpu.sync_copy(win_hbm, win_v)
            dg1_ref[...] = jnp.zeros(dg1_ref.shape, F32)
            dqg_ref[...] = jnp.zeros(dqg_ref.shape, F32)
            dkg_ref[...] = jnp.zeros(dkg_ref.shape, F32)
            dpw_ref[...] = jnp.zeros(dpw_ref.shape, F32)
            dps_ref[...] = jnp.zeros(dps_ref.shape, F32)
            ebuf[tm:tm + POOL_HALO, :] = jnp.zeros((POOL_HALO, POOL_WIDTH), F32)

        @pl.when(step > 0)
        def _():
            ebuf[tm:tm + POOL_HALO, :] = ebuf[0:POOL_HALO, :]

        @pl.when(step + 2 < nblk)
        def _():
            for cp in fetch(step + 2):
                cp.start()

        for cp in fetch(step):
            cp.wait()
        dqn_ref, dkn_ref, dv_ref, q_ref, k_ref, dyp_ref, pooled_ref, x_ref, dh1_ref = (
            ring.at[step % RING_SLOTS] for ring in rings)

        def qk_bwd(dn_sum, raw, gain, scale, dgain_ref):
            rr = lax.rsqrt(_head_sum_bcast(raw * raw) * (1.0 / HEAD_DIM) + NORM_EPS)
            hn = raw * rr
            dgain_ref[...] += jnp.sum(dn_sum * hn, axis=0, keepdims=True) * scale
            dn = dn_sum * (gain * scale)
            return rr * (dn - hn * (_head_sum_bcast(dn * hn) * (1.0 / HEAD_DIM)))

        dq = qk_bwd(dqn_ref[...], q_ref[...], qg_ref[...], HEAD_DIM ** -0.5, dqg_ref)
        dk = qk_bwd(dkn_ref[...], k_ref[...], kg_ref[...], 1.0, dkg_ref)

        t = i * tm + lax.broadcasted_iota(jnp.int32, (tm, 1), 0)
        dpooled = []
        for g, w in enumerate(POOL_WINDOWS):
            ls = slice(g * LANES, (g + 1) * LANES)
            dm = dyp_ref[:, ls]
            pg = pooled_ref[:, ls]
            pw_g = pw_ref[g].astype(MXU_DTYPE)
            dps_ref[:, ls] += jnp.sum(dm * _mm(pg, pw_g), axis=0, keepdims=True)
            dms = (dm * ps_ref[:, ls]).astype(MXU_DTYPE)
            dpw_ref[g] += _mm_tn(pg, dms)
            dpg = _mm_nt(dms, pw_g)
            dpooled.append(dpg)
            ebuf[0:tm, ls] = dpg / jnp.minimum(t + 1, w).astype(F32)
        du = []
        for g, w in enumerate(POOL_WINDOWS):
            ls = slice(g * LANES, (g + 1) * LANES)
            acc = ebuf[0:tm, ls]
            for sh in range(1, w):
                acc = acc + ebuf[sh:sh + tm, ls]
            du.append(acc - dpooled[g])
        parts = [jnp.concatenate(du, axis=-1), dq, dk, dv_ref[...]]
        da = jnp.zeros((tm, d), F32)
        for p, part in enumerate(parts):
            pc = part.astype(MXU_DTYPE)
            dproj_ref[:, p * POOL_WIDTH:(p + 1) * POOL_WIDTH] = pc
            da = da + _mm_nt(pc, win_v[p])
        xv = x_ref[...]
        r = lax.rsqrt(jnp.mean(xv * xv, axis=-1, keepdims=True) + NORM_EPS)
        xn = xv * r
        dg1_ref[...] += jnp.sum(da * xn, axis=0, keepdims=True)
        dx_ref[...] = dh1_ref[...] + _rms_bwd(da * g1_ref[...], xn, r)

    tok = lambda w: pl.BlockSpec((tm, w), lambda t: (nblk - 1 - t, 0))
    const = lambda shp: pl.BlockSpec(shp, lambda t: (0,) * len(shp))
    return pl.pallas_call(
        body, name="bwd_inproj",
        grid=(nblk,),
        in_specs=[ANY] * (ns + 1) + [const(poolw.shape), const((1, POOL_WIDTH)), const((1, ATTN_WIDTH)),
                                     const((1, ATTN_WIDTH)), const((1, d))],
        out_specs=[tok(d), tok(4 * POOL_WIDTH), const((1, d)), const((1, ATTN_WIDTH)), const((1, ATTN_WIDTH)),
                   const((ngrp, LANES, LANES)), const((1, POOL_WIDTH))],
        out_shape=[jax.ShapeDtypeStruct((s, d), F32),
                   jax.ShapeDtypeStruct((s, 4 * POOL_WIDTH), MXU_DTYPE),
                   jax.ShapeDtypeStruct((1, d), F32),
                   jax.ShapeDtypeStruct((1, ATTN_WIDTH), F32),
                   jax.ShapeDtypeStruct((1, ATTN_WIDTH), F32),
                   jax.ShapeDtypeStruct((ngrp, LANES, LANES), F32),
                   jax.ShapeDtypeStruct((1, POOL_WIDTH), F32)],
        scratch_shapes=[pltpu.VMEM(win.shape, MXU_DTYPE), pltpu.VMEM((tm + POOL_HALO, POOL_WIDTH), F32)]
        + [pltpu.VMEM((RING_SLOTS, tm, a.shape[1]), a.dtype) for a in streams]
        + [pltpu.SemaphoreType.DMA((ns, RING_SLOTS))],
        compiler_params=_params(("arbitrary",)),
    )(*streams, win, poolw, pscale, qg, kg, g1)


def _wgrad_call(a, b, bm, bn, bk, out_shape, out_block, out_index, name, dep=None):
    s, m = a.shape
    _, n = b.shape
    nk = s // bk
    deps = [] if dep is None else [dep]

    def body(a_ref, b_ref, *rest):
        o_ref, wire_ref = rest[len(deps):]
        k = pl.program_id(2)
        acc = jnp.where(k > 0, o_ref[...], 0.0) + _mm_tn(a_ref[...].astype(MXU_DTYPE), b_ref[...].astype(MXU_DTYPE))
        o_ref[...] = acc
        wire_ref[...] = acc.astype(WIRE_DTYPE)

    return pl.pallas_call(
        body, name=name,
        grid=(m // bm, n // bn, nk),
        in_specs=[pl.BlockSpec((bk, bm), lambda i, j, k: (k, i)), pl.BlockSpec((bk, bn), lambda i, j, k: (k, j))]
        + [ANY] * len(deps),
        out_specs=[pl.BlockSpec(out_block, out_index)] * 2,
        out_shape=[jax.ShapeDtypeStruct(out_shape, F32), jax.ShapeDtypeStruct(out_shape, WIRE_DTYPE)],
        compiler_params=_params(("arbitrary", "arbitrary", "arbitrary")),
    )(a, b, *deps)


def _local_grads(x, tgt, g1, win, poolw, pscale, qg, kg, bias, g2, mlp_weights, on_mlp_grads=None, first_dep=None):
    s, d = x.shape
    g1r, g2r = g1.reshape(1, d), g2.reshape(1, d)
    psr = pscale.reshape(1, POOL_WIDTH)
    qgr = jnp.tile(qg, N_HEADS).reshape(1, ATTN_WIDTH)
    kgr = jnp.tile(kg, N_HEADS).reshape(1, ATTN_WIDTH)
    pw_c = poolw
    buckets = jnp.asarray(_bucket_tables())
    bk = min(s, 4096)

    a, pooled, ypool, q32, k32, qn, kn, v = _f1_call(x, g1r, win, pw_c, psr, qgr, kgr, tm=1024, dep=first_dep)
    o, lse = _attn_fwd_call(qn, kn, v, bias)
    wout, wup, wdown = mlp_weights(o)
    mixed, c, ff, dz, dy, dh1, dypool, do, delta, dg2, loss = _f2_call(x, tgt, ypool, o, wout, wup, wdown, g2r, tm=256)
    dff = ff.shape[1]
    g_out = [g.reshape(N_CHIPS, d // N_CHIPS, d)
             for g in _wgrad_call(mixed, dh1, d, d, bk // 4, (d, d), (d, d), lambda i, j, k: (0, 0), "wgrad_out")]
    g_up = _wgrad_call(c, dz, d, dff // N_CHIPS, bk, (N_CHIPS, d, dff // N_CHIPS), (None, d, dff // N_CHIPS),
                       lambda i, j, k: (j, 0, 0), "wgrad_up")
    g_down = _wgrad_call(ff, dy, dff // N_CHIPS, d, bk, (N_CHIPS, dff // N_CHIPS, d), (None, dff // N_CHIPS, d),
                         lambda i, j, k: (i, 0, 0), "wgrad_down")
    dep = None if on_mlp_grads is None else on_mlp_grads(g_out[1], g_up[1], g_down[1])
    dqn, dkn, dv, dbias = _attn_bwd_call(qn, kn, v, do, lse, delta, bias, dep)
    dx, dproj, dg1, dqg, dkg, dpw, dps = _bproj_call(
        dqn, dkn, dv, q32, k32, dypool, pooled, x, dh1, win, pw_c, psr, qgr, kgr, g1r, tm=512)
    nin = dproj.shape[1] // N_CHIPS
    drb = _rel_bias_grad_call(dbias, buckets)
    g_in = _wgrad_call(a, dproj, d, nin, bk, (N_CHIPS, d, nin), (None, d, nin), lambda i, j, k: (j, 0, 0), "wgrad_in",
                       dep=drb)
    small = dict(
        mix_norm_g=dg1.reshape(d), mlp_norm_g=dg2.reshape(d), pool_scale=dps.reshape(POOL_WIDTH),
        q_norm_g=dqg.reshape(ATTN_WIDTH), k_norm_g=dkg.reshape(ATTN_WIDTH),
        rel_bias=drb[:, :N_BUCKETS].T, pool_w=dpw)
    return loss[0, 0], dx, (g_in, g_out, g_up, g_down), small


def _coords():
    return lax.axis_index("x"), lax.axis_index("y"), lax.axis_index("c")


def _other_chips(x, y):
    return [(1 - x, y), (x, 1 - y), (1 - x, 1 - y)]


def _remote(src, dst, send_sem, recv_sem, dev):
    return pltpu.make_async_remote_copy(src_ref=src, dst_ref=dst, send_sem=send_sem, recv_sem=recv_sem,
                                        device_id=dev, device_id_type=MESH)


PAIR_FORWARD_ID = 1
PAIR_ALLGATHER_ID = 2


def _sibling_handshake():
    x, y, c = _coords()
    barrier = pltpu.get_barrier_semaphore()
    pl.semaphore_signal(barrier, inc=1, device_id=(x, y, 1 - c), device_id_type=MESH)
    pl.semaphore_wait(barrier, 1)


def _halves(a):
    return a.reshape(a.shape[:-2] + (2, a.shape[-2] // 2, a.shape[-1]))


def _place_shards_call(shards, chip_idx, nch):
    nw = len(shards)

    def body(chip_ref, *refs):
        for w in range(nw):
            refs[nw + w][...] = refs[w][...].astype(WIRE_DTYPE)

    in_specs = [pl.BlockSpec((s.shape[0] // nch, s.shape[1]), lambda i, chip_ref: (i, 0)) for s in shards]
    out_specs = [pl.BlockSpec((None, s.shape[0] // nch, s.shape[1]), lambda i, chip_ref: (chip_ref[0], i, 0))
                 for s in shards]
    return pl.pallas_call(
        body, name="weights_place",
        grid_spec=pltpu.PrefetchScalarGridSpec(num_scalar_prefetch=1, grid=(nch,),
                                               in_specs=in_specs, out_specs=out_specs),
        out_shape=[jax.ShapeDtypeStruct((N_CHIPS,) + s.shape, WIRE_DTYPE) for s in shards],
        compiler_params=_params(("arbitrary",)),
    )(chip_idx, *shards)


def _allgather_call(placed, from_chips, name, meanwhile=None):
    nw = len(placed)
    ncp = 3 * nw
    extra, extra_specs, extra_shape, extra_body = meanwhile if meanwhile else ([], [], None, None)
    ne = len(extra)

    def body(*refs):
        outs = refs[nw + ne:2 * nw + ne]
        send1, recv1, send2, recv2 = refs[-4:]
        x, y, c = _coords()
        chip = 2 * x + y
        others = _other_chips(x, y)
        first, passed = [], []
        if not from_chips:
            _sibling_handshake()
        if from_chips:
            for w in range(nw):
                for k, (ox, oy) in enumerate(others):
                    mine = outs[w].at[chip, c]
                    cp = _remote(mine, mine, send1.at[3 * w + k], recv1.at[3 * w + k], (ox, oy, c))
                    cp.start()
                    first.append(cp)
        if meanwhile:
            extra_body(*refs[nw:nw + ne], refs[2 * nw + ne])
        for w in range(nw):
            for k, (ox, oy) in enumerate(others):
                piece = outs[w].at[2 * ox + oy, c]
                if from_chips:
                    _remote(piece, piece, send1.at[3 * w + k], recv1.at[3 * w + k], (ox, oy, c)).wait_recv()
                cp = _remote(piece, piece, send2.at[3 * w + k], recv2.at[3 * w + k], (x, y, 1 - c))
                cp.start()
                passed.append(cp)
        for w in range(nw):
            for k, (ox, oy) in enumerate(others):
                piece = outs[w].at[2 * ox + oy, 1 - c]
                _remote(piece, piece, send2.at[3 * w + k], recv2.at[3 * w + k], (x, y, 1 - c)).wait_recv()
        for cp in first + passed:
            cp.wait_send()

    return pl.pallas_call(
        body, name=name,
        in_specs=[ANY] * nw + list(extra_specs),
        out_specs=[ANY] * nw + ([pl.BlockSpec(memory_space=pltpu.VMEM)] if meanwhile else []),
        out_shape=[jax.ShapeDtypeStruct(s.shape, s.dtype) for s in placed] + ([extra_shape] if meanwhile else []),
        input_output_aliases={w: w for w in range(nw)},
        scratch_shapes=[pltpu.SemaphoreType.DMA((ncp,))] * 4,
        compiler_params=_params() if from_chips else _params(collective_id=PAIR_FORWARD_ID),
    )(*placed, *extra)


HBM_SPEC = pl.BlockSpec(memory_space=pltpu.HBM)
SEM_SPEC = pl.BlockSpec(memory_space=pltpu.SEMAPHORE)
SPLIT_EFFECT = pltpu.SideEffectType.DATAFLOW_SIDE_EFFECTING


def _in_hbm(a):
    return pltpu.with_memory_space_constraint(a, pltpu.HBM)


def _gather_copies(bufs, send, recv):
    x, y, c = _coords()
    chip = 2 * x + y
    cps = []
    for w, buf in enumerate(bufs):
        for k, (ox, oy) in enumerate(_other_chips(x, y)):
            mine, theirs = buf.at[chip, c], buf.at[2 * ox + oy, c]
            sems = (send.at[3 * w + k], recv.at[3 * w + k], (ox, oy, c))
            cps.append((_remote(mine, mine, *sems), _remote(theirs, theirs, *sems)))
    return cps


def _gather_start_call(bufs, after):
    nw = len(bufs)

    def body(*refs):
        ins, send, recv, token = refs[:nw], refs[nw + 1], refs[nw + 2], refs[2 * nw + 3]
        for out, _ in _gather_copies(ins, send, recv):
            out.start()
        token[...] = jnp.zeros(token.shape, F32)

    res = pl.pallas_call(
        body, name="weights_gather_start",
        in_specs=[HBM_SPEC] * nw + [ANY],
        out_specs=[SEM_SPEC, SEM_SPEC] + [HBM_SPEC] * nw + [pl.BlockSpec(memory_space=pltpu.VMEM)],
        out_shape=[pltpu.SemaphoreType.DMA((3 * nw,)), pltpu.SemaphoreType.DMA((3 * nw,))]
        + [pltpu.HBM(b.shape, b.dtype) for b in bufs] + [jax.ShapeDtypeStruct((8, LANES), F32)],
        input_output_aliases={w: 2 + w for w in range(nw)},
        compiler_params=pltpu.CompilerParams(has_side_effects=SPLIT_EFFECT),
    )(*[_in_hbm(b) for b in bufs], after)
    return res[0], res[1], list(res[2:2 + nw]), res[2 + nw]


def _gather_wait_call(bufs, send, recv, after):
    nw = len(bufs)

    def body(*refs):
        ins, send, recv = refs[:nw], refs[nw], refs[nw + 1]
        for out, back in _gather_copies(ins, send, recv):
            out.wait_send()
            back.wait_recv()

    return pl.pallas_call(
        body, name="weights_gather_wait",
        in_specs=[HBM_SPEC] * nw + [SEM_SPEC, SEM_SPEC, ANY],
        out_specs=[HBM_SPEC] * nw,
        out_shape=[pltpu.HBM(b.shape, b.dtype) for b in bufs],
        input_output_aliases={w: w for w in range(nw)},
        compiler_params=pltpu.CompilerParams(has_side_effects=SPLIT_EFFECT),
    )(*bufs, send, recv, after)


def _scatter_copies(srcs, lands, send, recv, wholes):
    x, y, c = _coords()
    me = 4 * x + 2 * y + c
    cps = []
    for w, (src, land) in enumerate(zip(srcs, lands)):
        for r in range(1, N_DEV):
            px, py, pc = ((1 - x) if r & 4 else x, (1 - y) if r & 2 else y, (1 - c) if r & 1 else c)
            sems = (send.at[(N_DEV - 1) * w + r - 1], recv.at[(N_DEV - 1) * w + r - 1], (px, py, pc))
            piece = src if wholes[w] else src.at[2 * px + py, pc]
            cps.append((_remote(piece, land.at[me], *sems), _remote(piece, land.at[4 * px + 2 * py + pc], *sems)))
    return cps


def _scatter_start_call(srcs, lands, wholes, name):
    nw = len(srcs)
    ncp = (N_DEV - 1) * nw

    def body(*refs):
        ins, lnd, send, recv, token = refs[:nw], refs[nw:2 * nw], refs[2 * nw], refs[2 * nw + 1], refs[4 * nw + 2]
        for out, _ in _scatter_copies(ins, lnd, send, recv, wholes):
            out.start()
        token[...] = jnp.zeros(token.shape, F32)

    res = pl.pallas_call(
        body, name=name,
        in_specs=[HBM_SPEC] * (2 * nw),
        out_specs=[SEM_SPEC, SEM_SPEC] + [HBM_SPEC] * (2 * nw) + [pl.BlockSpec(memory_space=pltpu.VMEM)],
        out_shape=[pltpu.SemaphoreType.DMA((ncp,)), pltpu.SemaphoreType.DMA((ncp,))]
        + [pltpu.HBM(b.shape, b.dtype) for b in list(srcs) + list(lands)] + [jax.ShapeDtypeStruct((8, LANES), F32)],
        input_output_aliases={i: 2 + i for i in range(2 * nw)},
        compiler_params=pltpu.CompilerParams(has_side_effects=SPLIT_EFFECT),
    )(*[_in_hbm(b) for b in list(srcs) + list(lands)])
    return res[0], res[1], list(res[2:2 + nw]), list(res[2 + nw:2 + 2 * nw]), res[2 + 2 * nw]


def _scatter_wait_call(srcs, lands, send, recv, after, wholes, name):
    nw = len(srcs)

    def body(*refs):
        ins, lnd, send, recv = refs[:nw], refs[nw:2 * nw], refs[2 * nw], refs[2 * nw + 1]
        for out, back in _scatter_copies(ins, lnd, send, recv, wholes):
            out.wait_send()
            back.wait_recv()

    res = pl.pallas_call(
        body, name=name,
        in_specs=[HBM_SPEC] * (2 * nw) + [SEM_SPEC, SEM_SPEC, ANY],
        out_specs=[HBM_SPEC] * (2 * nw),
        out_shape=[pltpu.HBM(b.shape, b.dtype) for b in list(srcs) + list(lands)],
        input_output_aliases={i: i for i in range(2 * nw)},
        compiler_params=pltpu.CompilerParams(has_side_effects=SPLIT_EFFECT),
    )(*srcs, *lands, send, recv, after)
    return list(res[nw:])


def _reduce_call(own, lands, idx, nch, name, dep=None):
    nw = len(own)
    deps = [] if dep is None else [dep]

    def body(idx_ref, *refs):
        refs = refs[:2 * nw] + refs[2 * nw + len(deps):]
        for w in range(nw):
            tot = refs[w][...]
            for r in range(1, N_DEV):
                tot = tot + refs[nw + w][idx_ref[1 + r]].astype(F32)
            refs[2 * nw + w][...] = tot

    in_specs, out_specs, out_shape = [], [], []
    for s in own:
        in_specs.append(pl.BlockSpec((None, None, s.shape[2] // nch, s.shape[3]),
                                     lambda i, idx_ref: (idx_ref[0], idx_ref[1], i, 0)))
    for s in own:
        in_specs.append(pl.BlockSpec((N_DEV, s.shape[2] // nch, s.shape[3]), lambda i, idx_ref: (0, i, 0)))
    for s in own:
        out_specs.append(pl.BlockSpec((None, s.shape[2] // nch, s.shape[3]), lambda i, idx_ref: (idx_ref[1], i, 0)))
        out_shape.append(jax.ShapeDtypeStruct((2,) + s.shape[2:], F32))
    return pl.pallas_call(
        body, name=name,
        grid_spec=pltpu.PrefetchScalarGridSpec(num_scalar_prefetch=1, grid=(nch,),
                                               in_specs=in_specs + [ANY] * len(deps), out_specs=out_specs),
        out_shape=out_shape,
        compiler_params=_params(("arbitrary",)),
    )(idx, *own, *lands, *deps)


def _pair_allgather_call(halves, name):
    nw = len(halves)

    def body(*refs):
        outs = refs[nw:2 * nw]
        send, recv = refs[2 * nw:]
        x, y, c = _coords()
        _sibling_handshake()
        cps = []
        for w in range(nw):
            cp = _remote(outs[w].at[c], outs[w].at[c], send.at[w], recv.at[w], (x, y, 1 - c))
            cp.start()
            cps.append(cp)
        for w in range(nw):
            theirs = outs[w].at[1 - c]
            _remote(theirs, theirs, send.at[w], recv.at[w], (x, y, 1 - c)).wait_recv()
        for cp in cps:
            cp.wait_send()

    outs = pl.pallas_call(
        body, name=name,
        in_specs=[ANY] * nw, out_specs=[ANY] * nw,
        out_shape=[jax.ShapeDtypeStruct(h.shape, h.dtype) for h in halves],
        input_output_aliases={w: w for w in range(nw)},
        scratch_shapes=[pltpu.SemaphoreType.DMA((nw,))] * 2,
        compiler_params=pltpu.CompilerParams(collective_id=PAIR_ALLGATHER_ID),
    )(*halves)
    return [o.reshape(2 * h.shape[1], h.shape[2]) for o, h in zip(outs, halves)]


def _adamw(w, g, m, v):
    m = ADAM_B1 * m + (1.0 - ADAM_B1) * g
    v = ADAM_B2 * v + (1.0 - ADAM_B2) * (g * g)
    m_hat = m / (1.0 - ADAM_B1 ** ADAM_STEP)
    v_hat = v / (1.0 - ADAM_B2 ** ADAM_STEP)
    delta = -ADAM_LR * (m_hat / (jnp.sqrt(v_hat) + ADAM_EPS) + ADAM_WD * w)
    return delta, m, v


def _adamw_call(ws, gs, ms, vs, nch, name):
    nw = len(ws)

    def body(*refs):
        for w in range(nw):
            g = refs[nw + w][...]
            delta, m, v = _adamw(refs[w][...], g, refs[2 * nw + w][...], refs[3 * nw + w][...])
            refs[4 * nw + w][...] = g
            refs[5 * nw + w][...] = delta
            refs[6 * nw + w][...] = m
            refs[7 * nw + w][...] = v

    specs = [pl.BlockSpec((a.shape[0] // nch, a.shape[1]), lambda i: (i, 0)) for a in ws]
    res = pl.pallas_call(
        body, name=name,
        grid=(nch,),
        in_specs=specs * 4, out_specs=specs * 4,
        out_shape=[jax.ShapeDtypeStruct(a.shape, F32) for a in ws] * 4,
        compiler_params=_params(("arbitrary",)),
    )(*ws, *gs, *ms, *vs)
    return res[:nw], res[nw:2 * nw], res[2 * nw:3 * nw], res[3 * nw:]


def _small_call(gathered, own, me_idx, w, m, v):
    cuts = dict(mix_norm_g=(0, 0, 1024), mlp_norm_g=(1, 0, 1024), pool_scale=(2, 0, POOL_WIDTH),
                rel_bias=(2, POOL_WIDTH, N_BUCKETS * N_HEADS), q_norm_g=(3, 0, HEAD_DIM), k_norm_g=(3, LANES, HEAD_DIM))
    n_out = len(cuts) + 1

    def fold(row):
        tot = row[:, 0:LANES] + row[:, LANES:2 * LANES] + row[:, 2 * LANES:3 * LANES] + row[:, 3 * LANES:4 * LANES]
        return tot + pltpu.roll(tot, HEAD_DIM, axis=1)

    def body(me_ref, gh_ref, gp_ref, oh_ref, op_ref, wh, wp, mh, mp, vh, vp, loss_ref, *outs):
        me = me_ref[0]

        def total(ga_ref, own_ref):
            term = lambda i: jnp.where(me == i, own_ref[...], ga_ref[i]).astype(F32)
            tot = term(0)
            for i in range(1, N_DEV):
                tot = tot + term(i)
            return tot

        g_head, g_pool = total(gh_ref, oh_ref), total(gp_ref, op_ref)
        unfolded = g_head[4:5, :]
        folded = jnp.concatenate([fold(unfolded[:, :ATTN_WIDTH]), fold(unfolded[:, ATTN_WIDTH:]),
                                  jnp.zeros((1, 1024 - 2 * LANES), F32)], axis=-1)
        row = lax.broadcasted_iota(jnp.int32, g_head.shape, 0)
        g_head = jnp.where(row == 3, folded, g_head)
        loss_ref[...] = g_head[LOSS_ROW:LOSS_ROW + 1, 0:LANES]
        heads = (g_head,) + _adamw(wh[...], g_head, mh[...], vh[...])
        pools = (g_pool,) + _adamw(wp[...], g_pool, mp[...], vp[...])
        for kind in range(4):
            mine = outs[kind * n_out:(kind + 1) * n_out]
            for out, (row, at, n) in zip(mine, cuts.values()):
                out[...] = heads[kind][row:row + 1, at:at + n]
            mine[-1][...] = pools[kind]

    vmem = pl.BlockSpec(memory_space=pltpu.VMEM)
    shapes = [jax.ShapeDtypeStruct((1, n), F32) for _, _, n in cuts.values()] + [jax.ShapeDtypeStruct(w[1].shape, F32)]
    res = pl.pallas_call(
        body, name="adamw_small",
        in_specs=[pl.BlockSpec(memory_space=pltpu.SMEM)] + [vmem] * 10,
        out_shape=[jax.ShapeDtypeStruct((1, LANES), F32)] + shapes * 4,
        compiler_params=_params(),
    )(me_idx, *gathered, *own, *w, *m, *v)

    def unpack(mine):
        p = {n: a.reshape(-1) for n, a in zip(cuts, mine)}
        p["rel_bias"] = p["rel_bias"].reshape(N_BUCKETS, N_HEADS)
        p["pool_w"] = mine[-1].reshape(len(POOL_WINDOWS), LANES, LANES)
        return p

    return [res[0]] + [unpack(res[1 + kind * n_out:1 + (kind + 1) * n_out]) for kind in range(4)]


def _pack_small(p, folded=True, loss=None):
    z = lambda n: jnp.zeros((n,), F32)
    rows = [p["mix_norm_g"], p["mlp_norm_g"],
            jnp.concatenate([p["pool_scale"], p["rel_bias"].reshape(-1), z(1024 - POOL_WIDTH - N_BUCKETS * N_HEADS)])]
    if folded:
        rows += [jnp.concatenate([p["q_norm_g"], z(LANES - HEAD_DIM), p["k_norm_g"], z(1024 - LANES - HEAD_DIM)]), z(1024)]
    else:
        rows += [z(1024), jnp.concatenate([p["q_norm_g"], p["k_norm_g"]])]
    rows += [z(1024) if loss is None else jnp.concatenate([loss.reshape(1), z(1023)])]
    return jnp.stack(rows + [z(1024)] * 2), p["pool_w"].reshape(-1, LANES)


_WEIGHT_ORDER = ("mix_norm_g", "w_in", "pool_w", "pool_scale", "q_norm_g", "k_norm_g", "rel_bias", "w_out",
                 "mlp_norm_g", "w_up", "w_down")
_BIG = ("w_in", "w_out", "w_up", "w_down")


def kernel(x, mix_norm_g, w_in, pool_w, pool_scale, q_norm_g, k_norm_g, rel_bias, w_out, mlp_norm_g, w_up, w_down, loss_target, m_mix_norm_g, m_w_in, m_pool_w, m_pool_scale, m_q_norm_g, m_k_norm_g, m_rel_bias, m_w_out, m_mlp_norm_g, m_w_up, m_w_down, v_mix_norm_g, v_w_in, v_pool_w, v_pool_scale, v_q_norm_g, v_k_norm_g, v_rel_bias, v_w_out, v_mlp_norm_g, v_w_up, v_w_down):
    w = dict(mix_norm_g=mix_norm_g, w_in=w_in, pool_w=pool_w, pool_scale=pool_scale, q_norm_g=q_norm_g,
             k_norm_g=k_norm_g, rel_bias=rel_bias, w_out=w_out, mlp_norm_g=mlp_norm_g, w_up=w_up, w_down=w_down)
    m = dict(mix_norm_g=m_mix_norm_g, w_in=m_w_in, pool_w=m_pool_w, pool_scale=m_pool_scale, q_norm_g=m_q_norm_g,
             k_norm_g=m_k_norm_g, rel_bias=m_rel_bias, w_out=m_w_out, mlp_norm_g=m_mlp_norm_g, w_up=m_w_up, w_down=m_w_down)
    v = dict(mix_norm_g=v_mix_norm_g, w_in=v_w_in, pool_w=v_pool_w, pool_scale=v_pool_scale, q_norm_g=v_q_norm_g,
             k_norm_g=v_k_norm_g, rel_bias=v_rel_bias, w_out=v_w_out, mlp_norm_g=v_mlp_norm_g, w_up=v_w_up, w_down=v_w_down)
    xc, yc, cc = _coords()

    c_idx = jnp.reshape(cc, (1,)).astype(jnp.int32)
    chip_idx = jnp.reshape(2 * xc + yc, (1,)).astype(jnp.int32)
    me = 4 * xc + 2 * yc + cc
    whole = lambda t: t.reshape(t.shape[0], t.shape[1] * t.shape[2], t.shape[3])

    placed = [_halves(p) for p in _place_shards_call([w[n] for n in _BIG], chip_idx, nch=4)]
    win_f, bias = _allgather_call(placed[:1], from_chips=True, name="weights_allgather_in",
                                  meanwhile=_bias_table_work(rel_bias))
    wsend, wrecv, in_flight, started = _gather_start_call(placed[1:], win_f)

    def mlp_weights(after):
        landed = _gather_wait_call(in_flight, wsend, wrecv, after)
        wout_f, wup_f, wdown_f = _allgather_call(landed, from_chips=False, name="weights_pair_forward")
        return whole(wout_f).reshape(-1, wout_f.shape[-1]), whole(wup_f), whole(wdown_f)

    split = []

    def on_mlp_grads(*wire_grads):
        srcs = [_halves(g) for g in wire_grads]
        lands = [lax.empty((N_DEV,) + s.shape[2:], s.dtype) for s in srcs]
        split.extend(_scatter_start_call(srcs, lands, [False] * len(srcs), "grads_scatter_start"))
        return split[4]

    loss_part, dx, big_grads, small_grads = _local_grads(
        x[0], loss_target[0], mix_norm_g, whole(win_f), pool_w, pool_scale, q_norm_g, k_norm_g, bias,
        mlp_norm_g, mlp_weights, on_mlp_grads, first_dep=started)
    g_in, g_out, g_up, g_down = big_grads
    gsend, grecv, srcs_thru, lands_thru, _ = split
    lands_mlp = _scatter_wait_call(srcs_thru, lands_thru, gsend, grecv, g_in[1], [False] * 3, "grads_scatter_wait")

    head_own, pool_own = _pack_small(small_grads, folded=False, loss=loss_part)
    small_own = (head_own, pool_own.astype(WIRE_DTYPE))
    last_srcs = [_halves(g_in[1]), *small_own]
    last_lands = [lax.empty((N_DEV,) + last_srcs[0].shape[2:], WIRE_DTYPE)]
    last_lands += [lax.empty((N_DEV,) + a.shape, a.dtype) for a in small_own]
    lsend, lrecv, last_srcs, last_lands, last_started = _scatter_start_call(
        last_srcs, last_lands, [False, True, True], "grads_scatter_start_last")
    idx = jnp.concatenate([chip_idx, c_idx] + [jnp.reshape(jnp.bitwise_xor(me, r), (1,)) for r in range(1, N_DEV)])
    idx = idx.astype(jnp.int32)
    mlp = _BIG[1:]

    def update(names, own32, lands, tag, dep=None):
        halves = _reduce_call([_halves(g) for g in own32], lands, idx, 4, "grads_reduce_" + tag, dep)
        reduced = _pair_allgather_call(list(halves), "grads_pair_allgather_" + tag)
        return _adamw_call([w[n] for n in names], reduced, [m[n] for n in names], [v[n] for n in names], 8, "adamw_" + tag)

    out_mlp = update(mlp, [g_out[0], g_up[0], g_down[0]], lands_mlp, "mlp", last_started)
    land_in, *small_all = _scatter_wait_call(last_srcs, last_lands, lsend, lrecv, out_mlp[3][-1], [False, True, True],
                                             "grads_scatter_wait_last")
    out_in = update(_BIG[:1], [g_in[0]], [land_in], "in")
    loss_row, grads, deltas, new_m, new_v = _small_call(
        small_all, small_own, jnp.reshape(me, (1,)).astype(jnp.int32), _pack_small(w), _pack_small(m), _pack_small(v))

    for k, res in enumerate((grads, deltas, new_m, new_v)):
        res[_BIG[0]] = out_in[k][0]
        for i, n in enumerate(mlp):
            res[n] = out_mlp[k][i]
    loss = loss_row[0, 0]
    return (loss, dx[None], *[grads[n] for n in _WEIGHT_ORDER], *[deltas[n] for n in _WEIGHT_ORDER],
            *[new_m[n] for n in _WEIGHT_ORDER], *[new_v[n] for n in _WEIGHT_ORDER])
```

```python
import math

import jax
import jax.numpy as jnp
import numpy as np
from jax import lax
from jax.experimental import pallas as pl
from jax.experimental.pallas import tpu as pltpu

F32 = jnp.float32
MXU_DTYPE = jnp.bfloat16
WIRE_DTYPE = jnp.bfloat16

NORM_EPS = 1e-6
NEG_INF = -1e30
LANES = 128
HEAD_DIM = 64
N_HEADS = 8
POOL_WIDTH = 512
ATTN_WIDTH = 512
POOL_WINDOWS = (2, 4, 8, 16)
POOL_HALO = 16
DILATED_PATTERNS = ((128, 1), (512, 4), (2048, 16))
ATT_BLOCK = 128
ATT_SUPER = ATT_BLOCK * max(dl for _, dl in DILATED_PATTERNS)
ATT_UNITS = ATT_SUPER // ATT_BLOCK
N_BUCKETS = 32
NO_BUCKET = -1
MAX_DISTANCE = 2048
N_CHIPS = 4
N_DEV = 8
ADAM_LR, ADAM_B1, ADAM_B2, ADAM_EPS, ADAM_WD, ADAM_STEP = 0.001, 0.9, 0.999, 1e-08, 0.01, 10
VMEM_LIMIT = 56 * 1024 * 1024
MESH = pl.DeviceIdType.MESH
ANY = pl.BlockSpec(memory_space=pl.ANY)

LOSS_ROW = 5


def _mm(a, b):
    return jnp.dot(a, b, preferred_element_type=F32)


def _mm_nt(a, b):
    return lax.dot_general(a, b, (((1,), (1,)), ((), ())), preferred_element_type=F32)


def _mm_tn(a, b):
    return lax.dot_general(a, b, (((0,), (0,)), ((), ())), preferred_element_type=F32)


def _params(sem=None, **kw):
    if sem is not None:
        kw["dimension_semantics"] = sem
    return pltpu.CompilerParams(vmem_limit_bytes=VMEM_LIMIT, **kw)


def _low_half():
    return lax.broadcasted_iota(jnp.int32, (1, LANES), 1) < HEAD_DIM


def _head_sum_bcast(y):
    lo = _low_half()
    outs = []
    for j in range(y.shape[1] // LANES):
        c = y[:, j * LANES:(j + 1) * LANES]
        s_lo = jnp.sum(jnp.where(lo, c, 0.0), axis=-1, keepdims=True)
        s_hi = jnp.sum(jnp.where(lo, 0.0, c), axis=-1, keepdims=True)
        outs.append(jnp.where(lo, s_lo, s_hi))
    return jnp.concatenate(outs, axis=-1)


def _rms_bwd(dn, hn, r):
    return r * (dn - hn * jnp.mean(dn * hn, axis=-1, keepdims=True))


def _t5_bucket_np(dist):
    max_exact = N_BUCKETS // 2
    d_f = np.maximum(dist, 1).astype(np.float32)
    ratio = (np.log(d_f / np.float32(max_exact)) / np.float32(math.log(MAX_DISTANCE / max_exact))).astype(np.float32)
    large = max_exact + (ratio * np.float32(N_BUCKETS - max_exact)).astype(np.int32)
    large = np.minimum(large, N_BUCKETS - 1)
    return np.where(dist < max_exact, dist, large).astype(np.int32)


def _window_offsets(dl):
    if dl == 1:
        return _by4_positions(ATT_BLOCK), _by4_positions(2 * ATT_BLOCK)
    return np.arange(ATT_BLOCK), np.arange(2 * ATT_BLOCK)


def _bucket_tables():
    tables = []
    for _, dl in DILATED_PATTERNS:
        qq, kk = _window_offsets(dl)
        dist = qq[:, None] + ATT_BLOCK - kk[None, :]
        bucket = _t5_bucket_np(np.clip(dist, 0, ATT_BLOCK) * dl)
        tables.append(np.where((dist >= 0) & (dist <= ATT_BLOCK), bucket, NO_BUCKET))
    return np.stack(tables).astype(np.int32)


def _previous_block_keys():
    return np.stack([np.broadcast_to(_window_offsets(dl)[1][None, :] < ATT_BLOCK, (ATT_BLOCK, 2 * ATT_BLOCK))
                     for _, dl in DILATED_PATTERNS])


def _f1_call(x, g1, win, poolw, pscale, qg, kg, tm, dep=None):
    s, d = x.shape
    nblk = s // tm
    deps = [] if dep is None else [dep]

    def body(x_ref, g1_ref, win_ref, pw_ref, ps_ref, qg_ref, kg_ref, *rest):
        a_ref, pooled_ref, ypool_ref, q32_ref, k32_ref, qn_ref, kn_ref, v_ref, ubuf = rest[len(deps):]
        i = pl.program_id(0)
        xv = x_ref[...]
        r = lax.rsqrt(jnp.mean(xv * xv, axis=-1, keepdims=True) + NORM_EPS)
        a = ((xv * r) * g1_ref[...]).astype(MXU_DTYPE)
        a_ref[...] = a
        u = _mm(a, win_ref[0])
        q = _mm(a, win_ref[1])
        k = _mm(a, win_ref[2])
        v_ref[...] = _mm(a, win_ref[3])
        q32_ref[...] = q
        k32_ref[...] = k
        rq = lax.rsqrt(_head_sum_bcast(q * q) * (1.0 / HEAD_DIM) + NORM_EPS)
        qn_ref[...] = ((q * rq) * qg_ref[...]) * (HEAD_DIM ** -0.5)
        rk = lax.rsqrt(_head_sum_bcast(k * k) * (1.0 / HEAD_DIM) + NORM_EPS)
        kn_ref[...] = (k * rk) * kg_ref[...]

        ubuf[0:POOL_HALO, :] = jnp.where(i > 0, ubuf[tm:tm + POOL_HALO, :], 0.0)
        ubuf[POOL_HALO:POOL_HALO + tm, :] = u
        t = i * tm + lax.broadcasted_iota(jnp.int32, (tm, 1), 0)
        for g, w in enumerate(POOL_WINDOWS):
            ls = slice(g * LANES, (g + 1) * LANES)
            ug = u[:, ls]
            acc = ug
            for sh in range(1, w):
                acc = acc + ubuf[POOL_HALO - sh:POOL_HALO - sh + tm, ls]
            cnt = jnp.minimum(t + 1, w).astype(F32)
            pooled = (acc / cnt - ug).astype(MXU_DTYPE)
            pooled_ref[:, ls] = pooled
            ypool_ref[:, ls] = (_mm(pooled, pw_ref[g]) * ps_ref[:, ls]).astype(MXU_DTYPE)

    tok = lambda w: pl.BlockSpec((tm, w), lambda i: (i, 0))
    full = lambda shp: pl.BlockSpec(shp, lambda i: (0,) * len(shp))
    return pl.pallas_call(
        body, name="fwd_inproj",
        grid=(nblk,),
        in_specs=[tok(d), full((1, d)), full(win.shape), full(poolw.shape), full((1, POOL_WIDTH)),
                  full((1, ATTN_WIDTH)), full((1, ATTN_WIDTH))] + [ANY] * len(deps),
        out_specs=[tok(d), tok(POOL_WIDTH), tok(POOL_WIDTH), tok(ATTN_WIDTH), tok(ATTN_WIDTH),
                   tok(ATTN_WIDTH), tok(ATTN_WIDTH), tok(ATTN_WIDTH)],
        out_shape=[jax.ShapeDtypeStruct((s, d), MXU_DTYPE),
                   jax.ShapeDtypeStruct((s, POOL_WIDTH), MXU_DTYPE),
                   jax.ShapeDtypeStruct((s, POOL_WIDTH), MXU_DTYPE),
                   jax.ShapeDtypeStruct((s, ATTN_WIDTH), F32),
                   jax.ShapeDtypeStruct((s, ATTN_WIDTH), F32),
                   jax.ShapeDtypeStruct((s, ATTN_WIDTH), F32),
                   jax.ShapeDtypeStruct((s, ATTN_WIDTH), F32),
                   jax.ShapeDtypeStruct((s, ATTN_WIDTH), F32)],
        scratch_shapes=[pltpu.VMEM((tm + POOL_HALO, POOL_WIDTH), F32)],
        compiler_params=_params(("arbitrary",)),
    )(x, g1, win, poolw, pscale, qg, kg, *deps)


DEINT = 4
assert [dl for _, dl in DILATED_PATTERNS] == [1, DEINT, DEINT * DEINT]


def _by4_positions(n):
    pos = np.arange(n)
    return DEINT * (pos % (n // DEINT)) + pos // (n // DEINT)


def _masked_bias(b_ref, p, n):
    return b_ref[p, jnp.minimum(n, 1)].reshape(2 * ATT_BLOCK, 2 * ATT_BLOCK)


def _unit_rows(u, dl):
    assert isinstance(u, int)
    sq, sk = ATT_SUPER // DEINT, 2 * ATT_SUPER // DEINT
    if dl == 1:
        n = ATT_BLOCK // DEINT
        return (u, [pl.ds(r * sq + n * u, n) for r in range(DEINT)],
                [pl.ds(r * sk + sk // 2 + n * (u - 1), 2 * n) for r in range(DEINT)])
    if dl == DEINT:
        r, b = u % DEINT, u // DEINT
        return (b, [pl.ds(r * sq + ATT_BLOCK * b, ATT_BLOCK)],
                [pl.ds(r * sk + sk // 2 + ATT_BLOCK * (b - 1), 2 * ATT_BLOCK)])
    r, a = u % DEINT, u // DEINT
    return 0, [pl.ds(r * sq + a, ATT_BLOCK, stride=DEINT)], [pl.ds(r * sk + a, 2 * ATT_BLOCK, stride=DEINT)]


def _take(ref, runs):
    parts = [ref[run, :] for run in runs]
    return parts[0] if len(parts) == 1 else jnp.concatenate(parts, axis=0)


def _put(ref, runs, value, add=False):
    n = value.shape[0] // len(runs)
    for i, run in enumerate(runs):
        part = value[i * n:(i + 1) * n]
        ref[run, :] = ref[run, :] + part if add else part


def _deinterleave(dst, src, n):
    seg = n // DEINT
    for r in range(DEINT):
        dst[r * seg:(r + 1) * seg, :] = src[pl.ds(r, seg, stride=DEINT), :]


def _deinterleave_pair(dst, prev, cur):
    seg = prev.shape[0] // DEINT
    for r in range(DEINT):
        dst[2 * r * seg:(2 * r + 1) * seg, :] = prev[pl.ds(r, seg, stride=DEINT), :]
        dst[(2 * r + 1) * seg:(2 * r + 2) * seg, :] = cur[pl.ds(r, seg, stride=DEINT), :]


def _interleave(dst, src, n, offset=0):
    seg = n // DEINT
    stride = src.shape[0] // DEINT
    for r in range(DEINT):
        dst[pl.ds(r, seg, stride=DEINT), :] = src[r * stride + offset:r * stride + offset + seg, :]


def _attn_fwd_call(qn, kn, v, bias):
    s, w = qn.shape
    nsb = s // ATT_SUPER
    npair = w // LANES

    def body(q_ref, kc_ref, vc_ref, b_ref, o_ref, lse_ref, qf, kf, vf, acc_s, m_s, l_s):
        sb = pl.program_id(1)

        @pl.when((pl.program_id(0) == 0) & (sb == 0))
        def _():
            kf[...] = jnp.zeros_like(kf)
            vf[...] = jnp.zeros_like(vf)

        seg = ATT_SUPER // DEINT
        _deinterleave(qf, q_ref, ATT_SUPER)
        for r in range(DEINT):
            for dst, src in ((kf, kc_ref), (vf, vc_ref)):
                dst[2 * r * seg:(2 * r + 1) * seg, :] = dst[(2 * r + 1) * seg:(2 * r + 2) * seg, :]
                dst[(2 * r + 1) * seg:(2 * r + 2) * seg, :] = src[pl.ds(r, seg, stride=DEINT), :]
        lo = _low_half()
        for p, (_, dl) in enumerate(DILATED_PATTERNS):
            def unit(u, carry, p=p, dl=dl):
                b, rows_q, rows_k = _unit_rows(u, dl)
                qp = _take(qf, rows_q).astype(MXU_DTYPE)
                kcat = _take(kf, rows_k).astype(MXU_DTYPE)
                vcat = _take(vf, rows_k).astype(MXU_DTYPE)
                zero = jnp.zeros_like(qp)
                q2 = jnp.concatenate([jnp.where(lo, qp, zero), jnp.where(lo, zero, qp)], axis=0)
                sc = _mm_nt(q2, kcat) + _masked_bias(b_ref, p, sb * (ATT_UNITS // dl) + b)
                m2 = jnp.max(sc, axis=-1, keepdims=True)
                pr = jnp.exp(sc - m2)
                l2 = jnp.sum(pr, axis=-1, keepdims=True)
                acc2 = _mm(pr.astype(MXU_DTYPE), vcat)
                acc = jnp.where(lo, acc2[:ATT_BLOCK], acc2[ATT_BLOCK:])
                m = jnp.where(lo, m2[:ATT_BLOCK], m2[ATT_BLOCK:])
                l = jnp.where(lo, l2[:ATT_BLOCK], l2[ATT_BLOCK:])
                if p == 0:
                    _put(acc_s, rows_q, acc)
                    _put(m_s, rows_q, m)
                    _put(l_s, rows_q, l)
                else:
                    m_old = _take(m_s, rows_q)
                    m_new = jnp.maximum(m_old, m)
                    a_old = jnp.exp(m_old - m_new)
                    a_new = jnp.exp(m - m_new)
                    _put(acc_s, rows_q, a_old * _take(acc_s, rows_q) + a_new * acc)
                    _put(l_s, rows_q, a_old * _take(l_s, rows_q) + a_new * l)
                    _put(m_s, rows_q, m_new)
                return carry

            for u in range(ATT_UNITS):
                unit(u, None)
        l = l_s[...]
        acc_s[...] = acc_s[...] / l
        m_s[...] = m_s[...] + jnp.log(l)
        _interleave(o_ref, acc_s, ATT_SUPER)
        _interleave(lse_ref, m_s, ATT_SUPER)

    cur = pl.BlockSpec((ATT_SUPER, LANES), lambda j, t: (t, j))
    bspec = pl.BlockSpec((len(DILATED_PATTERNS), 2, 2, ATT_BLOCK, 2 * ATT_BLOCK), lambda j, t: (0, 0, j, 0, 0))
    return pl.pallas_call(
        body, name="attn_fwd",
        grid=(npair, nsb),
        in_specs=[cur, cur, cur, bspec],
        out_specs=[cur, cur],
        out_shape=[jax.ShapeDtypeStruct((s, w), F32), jax.ShapeDtypeStruct((s, w), F32)],
        scratch_shapes=[pltpu.VMEM((ATT_SUPER, LANES), F32), pltpu.VMEM((2 * ATT_SUPER, LANES), F32),
                        pltpu.VMEM((2 * ATT_SUPER, LANES), F32), pltpu.VMEM((ATT_SUPER, LANES), F32),
                        pltpu.VMEM((ATT_SUPER, LANES), F32), pltpu.VMEM((ATT_SUPER, LANES), F32)],
        compiler_params=_params(("arbitrary", "arbitrary")),
    )(qn, kn, v, bias)


def _attn_bwd_call(qn, kn, v, do, lse, delta, bias, dep=None):
    s, w = qn.shape
    nsb = s // ATT_SUPER
    npair = w // LANES
    deps = [] if dep is None else [dep]

    def body(q_ref, kc_ref, kp_ref, vc_ref, vp_ref, do_ref, lse_ref, dlt_ref, b_ref, *rest):
        dq_ref, dk_ref, dv_ref, db_ref, qf, kf, vf, dof, lsef, dltf, dqf, dkf, dvf = rest[len(deps):]
        step = pl.program_id(1)
        sb = nsb - 1 - step
        seg = ATT_SUPER // DEINT
        _deinterleave(qf, q_ref, ATT_SUPER)
        _deinterleave(dof, do_ref, ATT_SUPER)
        _deinterleave_pair(kf, kp_ref, kc_ref)
        _deinterleave_pair(vf, vp_ref, vc_ref)
        _deinterleave(lsef, lse_ref, ATT_SUPER)
        _deinterleave(dltf, dlt_ref, ATT_SUPER)

        db_ref[...] = jnp.where(step > 0, db_ref[...], 0.0)
        for acc in (dkf, dvf):
            for r in range(DEINT):
                this, before = pl.ds((2 * r + 1) * seg, seg), pl.ds(2 * r * seg, seg)
                acc[this, :] = jnp.where(step > 0, acc[before, :], 0.0)
                acc[before, :] = jnp.zeros((seg, LANES), F32)
        lo = _low_half()
        for p, (_, dl) in enumerate(DILATED_PATTERNS):
            def unit(u, carry, p=p, dl=dl):
                b, rows_q, rows_k = _unit_rows(u, dl)
                qp = _take(qf, rows_q).astype(MXU_DTYPE)
                dop = _take(dof, rows_q).astype(MXU_DTYPE)
                kcat = _take(kf, rows_k).astype(MXU_DTYPE)
                vcat = _take(vf, rows_k).astype(MXU_DTYPE)
                lse2 = _take(lsef, rows_q)
                dlt2 = _take(dltf, rows_q)
                zero = jnp.zeros_like(qp)
                q2 = jnp.concatenate([jnp.where(lo, qp, zero), jnp.where(lo, zero, qp)], axis=0)
                do2 = jnp.concatenate([jnp.where(lo, dop, zero), jnp.where(lo, zero, dop)], axis=0)
                lse_c = jnp.concatenate([lse2[:, 0:1], lse2[:, HEAD_DIM:HEAD_DIM + 1]], axis=0)
                dlt_c = jnp.concatenate([dlt2[:, 0:1], dlt2[:, HEAD_DIM:HEAD_DIM + 1]], axis=0)
                sc = _mm_nt(q2, kcat) + _masked_bias(b_ref, p, sb * (ATT_UNITS // dl) + b)
                pr = jnp.exp(sc - lse_c)
                ds = pr * (_mm_nt(do2, vcat) - dlt_c)
                db_ref[p] += ds.reshape(2, ATT_BLOCK, 2 * ATT_BLOCK)
                ds_c = ds.astype(MXU_DTYPE)
                dq2 = _mm(ds_c, kcat)
                dk = _mm_tn(ds_c, q2)
                dv = _mm_tn(pr.astype(MXU_DTYPE), do2)
                dq = jnp.where(lo, dq2[:ATT_BLOCK], dq2[ATT_BLOCK:])
                _put(dqf, rows_q, dq, add=p > 0)
                _put(dkf, rows_k, dk, add=True)
                _put(dvf, rows_k, dv, add=True)
                return carry

            for u in range(ATT_UNITS):
                unit(u, None)
        _interleave(dq_ref, dqf, ATT_SUPER)
        _interleave(dk_ref, dkf, ATT_SUPER, offset=seg)
        _interleave(dv_ref, dvf, ATT_SUPER, offset=seg)

    cur = pl.BlockSpec((ATT_SUPER, LANES), lambda j, t: (nsb - 1 - t, j))
    prev = pl.BlockSpec((ATT_SUPER, LANES), lambda j, t: (jnp.maximum(nsb - 2 - t, 0), j))
    npat = len(DILATED_PATTERNS)
    bspec = pl.BlockSpec((npat, 2, 2, ATT_BLOCK, 2 * ATT_BLOCK), lambda j, t: (0, 0, j, 0, 0))
    dbspec = pl.BlockSpec((npat, 2, ATT_BLOCK, 2 * ATT_BLOCK), lambda j, t: (0, j, 0, 0))
    sup = lambda: pltpu.VMEM((ATT_SUPER, LANES), F32)
    sup2 = lambda: pltpu.VMEM((2 * ATT_SUPER, LANES), F32)
    return pl.pallas_call(
        body, name="attn_bwd",
        grid=(npair, nsb),
        in_specs=[cur, cur, prev, cur, prev, cur, cur, cur, bspec] + [ANY] * len(deps),
        out_specs=[cur, cur, cur, dbspec],
        out_shape=[jax.ShapeDtypeStruct((s, w), F32)] * 3
        + [jax.ShapeDtypeStruct((npat, N_HEADS, ATT_BLOCK, 2 * ATT_BLOCK), F32)],
        scratch_shapes=[sup(), sup2(), sup2(), sup(), sup(), sup(), sup(), sup2(), sup2()],
        compiler_params=_params(("arbitrary", "arbitrary")),
    )(qn, kn, kn, v, v, do, lse, delta, bias, *deps)


def _bias_table_work(rel_bias):
    buckets = jnp.asarray(_bucket_tables())
    prev_keys = jnp.asarray(_previous_block_keys().astype(np.int32))
    npat = buckets.shape[0]

    def body(rb_ref, bk_ref, pk_ref, out_ref):
        for p in range(npat):
            for half in range(2):
                ks = slice(half * ATT_BLOCK, (half + 1) * ATT_BLOCK)
                bk = bk_ref[p, :, ks]
                absent = pk_ref[p, :, ks] != 0
                for h in range(N_HEADS):
                    def pick(b, acc, h=h, bk=bk):
                        return jnp.where(bk == b, rb_ref[b, h], acc)

                    tab = lax.fori_loop(0, N_BUCKETS, pick, jnp.full((ATT_BLOCK, ATT_BLOCK), NEG_INF, F32))
                    out_ref[p, 1, h, :, ks] = tab
                    out_ref[p, 0, h, :, ks] = jnp.where(absent, NEG_INF, tab)

    vmem = pl.BlockSpec(memory_space=pltpu.VMEM)
    return ([rel_bias, buckets, prev_keys], [pl.BlockSpec(memory_space=pltpu.SMEM), vmem, vmem],
            jax.ShapeDtypeStruct((npat, 2, N_HEADS, ATT_BLOCK, 2 * ATT_BLOCK), F32), body)


def _rel_bias_grad_call(dbias, buckets):
    npat, nh = dbias.shape[0], dbias.shape[1]

    def body(db_ref, bk_ref, out_ref):
        lane = lax.broadcasted_iota(jnp.int32, (nh, LANES), 1)
        out = jnp.zeros((nh, LANES), F32)
        for b in range(N_BUCKETS):
            tot = jnp.zeros((nh, 1), F32)
            for p in range(npat):
                hit = jnp.where(bk_ref[p][None] == b, db_ref[p], 0.0)
                tot = tot + jnp.sum(jnp.sum(hit, axis=1), axis=-1, keepdims=True)
            out = jnp.where(lane == b, tot, out)
        out_ref[...] = out

    return pl.pallas_call(
        body, name="rel_bias_grad",
        out_shape=jax.ShapeDtypeStruct((nh, LANES), F32),
        compiler_params=_params(),
    )(dbias, buckets)


def _f2_call(x, tgt, ypool, o, wout, wup, wdown, g2, tm):
    s, d = x.shape
    nblk = s // tm
    nch, _, fch = wup.shape
    dff = nch * fch
    mixw = POOL_WIDTH + ATTN_WIDTH

    def body(x_ref, t_ref, yp_ref, o_ref, g2_ref, wout_hbm, wup_hbm, wdown_hbm,
             mixed_ref, c_ref, ff_ref, dz_ref, dy_ref, dh1_ref, dyp_ref, do_ref, dlt_ref, dg2_ref, loss_ref,
             wout_v, wup_v, wdown_v, rz, wsem):
        i = pl.program_id(0)

        @pl.when(i == 0)
        def _():
            copies = [pltpu.make_async_copy(wout_hbm, wout_v, wsem.at[0])]
            for j in range(nch):
                copies.append(pltpu.make_async_copy(wup_hbm.at[j], wup_v.at[j], wsem.at[1 + 2 * j]))
                copies.append(pltpu.make_async_copy(wdown_hbm.at[j], wdown_v.at[j], wsem.at[2 + 2 * j]))
            for cp in copies:
                cp.start()
            dg2_ref[...] = jnp.zeros(dg2_ref.shape, F32)
            loss_ref[...] = jnp.zeros(loss_ref.shape, F32)
            for cp in copies:
                cp.wait()

        o = o_ref[...]
        mixed = jnp.concatenate([yp_ref[...], o.astype(MXU_DTYPE)], axis=-1)
        mixed_ref[...] = mixed
        h1 = x_ref[...] + _mm(mixed, wout_v[...])
        r2 = lax.rsqrt(jnp.mean(h1 * h1, axis=-1, keepdims=True) + NORM_EPS)
        hn = h1 * r2
        c = (hn * g2_ref[...]).astype(MXU_DTYPE)
        c_ref[...] = c
        y = h1
        for j in range(nch):
            cs = slice(j * fch, (j + 1) * fch)
            z = jnp.maximum(_mm(c, wup_v[j]), 0.0)
            rz[:, cs] = z
            ff = (z * z).astype(MXU_DTYPE)
            ff_ref[:, cs] = ff
            y = y + _mm(ff, wdown_v[j])
        err = y - t_ref[...]
        loss_ref[...] += jnp.sum(err * err) * (0.5 / d)
        dy = err * (1.0 / d)
        dy_c = dy.astype(MXU_DTYPE)
        dy_ref[...] = dy_c
        dc = jnp.zeros((tm, d), F32)
        for j in range(nch):
            cs = slice(j * fch, (j + 1) * fch)
            dz = (_mm_nt(dy_c, wdown_v[j]) * (2.0 * rz[:, cs])).astype(MXU_DTYPE)
            dz_ref[:, cs] = dz
            dc = dc + _mm_nt(dz, wup_v[j])
        dg2_ref[...] += jnp.sum(dc * hn, axis=0, keepdims=True)
        dh1 = dy + _rms_bwd(dc * g2_ref[...], hn, r2)
        dh1_ref[...] = dh1
        dmix = _mm_nt(dh1.astype(MXU_DTYPE), wout_v[...])
        dyp_ref[...] = dmix[:, :POOL_WIDTH]
        do = dmix[:, POOL_WIDTH:]
        do_ref[...] = do
        dlt_ref[...] = _head_sum_bcast(do * o)

    tok = lambda w: pl.BlockSpec((tm, w), lambda i: (i, 0))
    const = lambda shp: pl.BlockSpec(shp, lambda i: (0,) * len(shp))
    return pl.pallas_call(
        body, name="fwd_mlp_bwd_mlp",
        grid=(nblk,),
        in_specs=[tok(d), tok(d), tok(POOL_WIDTH), tok(ATTN_WIDTH), const((1, d)), ANY, ANY, ANY],
        out_specs=[tok(mixw), tok(d), tok(dff), tok(dff), tok(d), tok(d), tok(POOL_WIDTH), tok(ATTN_WIDTH),
                   tok(ATTN_WIDTH), const((1, d)), const((1, LANES))],
        out_shape=[jax.ShapeDtypeStruct((s, mixw), MXU_DTYPE),
                   jax.ShapeDtypeStruct((s, d), MXU_DTYPE),
                   jax.ShapeDtypeStruct((s, dff), MXU_DTYPE),
                   jax.ShapeDtypeStruct((s, dff), MXU_DTYPE),
                   jax.ShapeDtypeStruct((s, d), MXU_DTYPE),
                   jax.ShapeDtypeStruct((s, d), F32),
                   jax.ShapeDtypeStruct((s, POOL_WIDTH), F32),
                   jax.ShapeDtypeStruct((s, ATTN_WIDTH), F32),
                   jax.ShapeDtypeStruct((s, ATTN_WIDTH), F32),
                   jax.ShapeDtypeStruct((1, d), F32),
                   jax.ShapeDtypeStruct((1, LANES), F32)],
        scratch_shapes=[pltpu.VMEM(wout.shape, MXU_DTYPE), pltpu.VMEM(wup.shape, MXU_DTYPE),
                        pltpu.VMEM(wdown.shape, MXU_DTYPE), pltpu.VMEM((tm, dff), F32),
                        pltpu.SemaphoreType.DMA((1 + 2 * nch,))],
        compiler_params=_params(("arbitrary",)),
    )(x, tgt, ypool, o, g2, wout, wup, wdown)


RING_SLOTS = 3


def _bproj_call(dqn, dkn, dv, q32, k32, dypool, pooled, x, dh1, win, poolw, pscale, qg, kg, g1, tm):
    s, d = x.shape
    nblk = s // tm
    ngrp = len(POOL_WINDOWS)
    streams = [dqn, dkn, dv, q32, k32, dypool, pooled, x, dh1]
    ns = len(streams)
    assert nblk >= 2

    def body(*refs):
        hbm = refs[:ns]
        win_hbm, pw_ref, ps_ref, qg_ref, kg_ref, g1_ref = refs[ns:ns + 6]
        dx_ref, dproj_ref, dg1_ref, dqg_ref, dkg_ref, dpw_ref, dps_ref, win_v, ebuf = refs[ns + 6:ns + 15]
        rings, sems = refs[ns + 15:2 * ns + 15], refs[2 * ns + 15]
        step = pl.program_id(0)
        i = nblk - 1 - step

        def fetch(t):
            rows = pl.ds(pl.multiple_of((nblk - 1 - t) * tm, tm), tm)
            return [pltpu.make_async_copy(h.at[rows], ring.at[t % RING_SLOTS], sems.at[k, t % RING_SLOTS])
                    for k, (h, ring) in enumerate(zip(hbm, rings))]

        @pl.when(step == 0)
        def _():
            for t in range(2):
                for k, cp in enumerate(fetch(t)):
                    cp.start(priority=k % 2)
            pltpu.sync_copy(win_hbm, win_v)
            dg1_ref[...] = jnp.zeros(dg1_ref.shape, F32)
            dqg_ref[...] = jnp.zeros(dqg_ref.shape, F32)
            dkg_ref[...] = jnp.zeros(dkg_ref.shape, F32)
            dpw_ref[...] = jnp.zeros(dpw_ref.shape, F32)
            dps_ref[...] = jnp.zeros(dps_ref.shape, F32)
            ebuf[tm:tm + POOL_HALO, :] = jnp.zeros((POOL_HALO, POOL_WIDTH), F32)

        @pl.when(step > 0)
        def _():
            ebuf[tm:tm + POOL_HALO, :] = ebuf[0:POOL_HALO, :]

        @pl.when(step + 2 < nblk)
        def _():
            for k, cp in enumerate(fetch(step + 2)):
                cp.start(priority=k % 2)

        for cp in fetch(step):
            cp.wait()
        dqn_ref, dkn_ref, dv_ref, q_ref, k_ref, dyp_ref, pooled_ref, x_ref, dh1_ref = (
            ring.at[step % RING_SLOTS] for ring in rings)

        def qk_bwd(dn_sum, raw, gain, scale, dgain_ref):
            rr = lax.rsqrt(_head_sum_bcast(raw * raw) * (1.0 / HEAD_DIM) + NORM_EPS)
            hn = raw * rr
            dgain_ref[...] += jnp.sum(dn_sum * hn, axis=0, keepdims=True) * scale
            dn = dn_sum * (gain * scale)
            return rr * (dn - hn * (_head_sum_bcast(dn * hn) * (1.0 / HEAD_DIM)))

        dq = qk_bwd(dqn_ref[...], q_ref[...], qg_ref[...], HEAD_DIM ** -0.5, dqg_ref)
        dk = qk_bwd(dkn_ref[...], k_ref[...], kg_ref[...], 1.0, dkg_ref)

        t = i * tm + lax.broadcasted_iota(jnp.int32, (tm, 1), 0)
        dpooled = []
        for g, w in enumerate(POOL_WINDOWS):
            ls = slice(g * LANES, (g + 1) * LANES)
            dm = dyp_ref[:, ls]
            pg = pooled_ref[:, ls]
            dps_ref[:, ls] += jnp.sum(dm * _mm(pg, pw_ref[g]), axis=0, keepdims=True)
            dms = (dm * ps_ref[:, ls]).astype(MXU_DTYPE)
            dpw_ref[g] += _mm_tn(pg, dms)
            dpg = _mm_nt(dms, pw_ref[g])
            dpooled.append(dpg)
            ebuf[0:tm, ls] = dpg / jnp.minimum(t + 1, w).astype(F32)
        du = []
        for g, w in enumerate(POOL_WINDOWS):
            ls = slice(g * LANES, (g + 1) * LANES)
            acc = ebuf[0:tm, ls]
            for sh in range(1, w):
                acc = acc + ebuf[sh:sh + tm, ls]
            du.append(acc - dpooled[g])
        parts = [jnp.concatenate(du, axis=-1), dq, dk, dv_ref[...]]
        da = jnp.zeros((tm, d), F32)
        for p, part in enumerate(parts):
            pc = part.astype(MXU_DTYPE)
            dproj_ref[:, p * POOL_WIDTH:(p + 1) * POOL_WIDTH] = pc
            da = da + _mm_nt(pc, win_v[p])
        xv = x_ref[...]
        r = lax.rsqrt(jnp.mean(xv * xv, axis=-1, keepdims=True) + NORM_EPS)
        xn = xv * r
        dg1_ref[...] += jnp.sum(da * xn, axis=0, keepdims=True)
        dx_ref[...] = dh1_ref[...] + _rms_bwd(da * g1_ref[...], xn, r)

    tok = lambda w: pl.BlockSpec((tm, w), lambda t: (nblk - 1 - t, 0))
    const = lambda shp: pl.BlockSpec(shp, lambda t: (0,) * len(shp))
    return pl.pallas_call(
        body, name="bwd_inproj",
        grid=(nblk,),
        in_specs=[ANY] * (ns + 1) + [const(poolw.shape), const((1, POOL_WIDTH)), const((1, ATTN_WIDTH)),
                                     const((1, ATTN_WIDTH)), const((1, d))],
        out_specs=[tok(d), tok(4 * POOL_WIDTH), const((1, d)), const((1, ATTN_WIDTH)), const((1, ATTN_WIDTH)),
                   const((ngrp, LANES, LANES)), const((1, POOL_WIDTH))],
        out_shape=[jax.ShapeDtypeStruct((s, d), F32),
                   jax.ShapeDtypeStruct((s, 4 * POOL_WIDTH), MXU_DTYPE),
                   jax.ShapeDtypeStruct((1, d), F32),
                   jax.ShapeDtypeStruct((1, ATTN_WIDTH), F32),
                   jax.ShapeDtypeStruct((1, ATTN_WIDTH), F32),
                   jax.ShapeDtypeStruct((ngrp, LANES, LANES), F32),
                   jax.ShapeDtypeStruct((1, POOL_WIDTH), F32)],
        scratch_shapes=[pltpu.VMEM(win.shape, MXU_DTYPE), pltpu.VMEM((tm + POOL_HALO, POOL_WIDTH), F32)]
        + [pltpu.VMEM((RING_SLOTS, tm, a.shape[1]), a.dtype) for a in streams]
        + [pltpu.SemaphoreType.DMA((ns, RING_SLOTS))],
        compiler_params=_params(("arbitrary",)),
    )(*streams, win, poolw, pscale, qg, kg, g1)


def _wgrad_call(a, b, bm, bn, bk, out_shape, out_block, out_index, name, dep=None):
    s, m = a.shape
    _, n = b.shape
    nk = s // bk
    deps = [] if dep is None else [dep]

    def body(a_ref, b_ref, *rest):
        o_ref, wire_ref = rest[len(deps):]
        k = pl.program_id(2)
        acc = jnp.where(k > 0, o_ref[...], 0.0) + _mm_tn(a_ref[...].astype(MXU_DTYPE), b_ref[...].astype(MXU_DTYPE))
        o_ref[...] = acc
        wire_ref[...] = acc.astype(WIRE_DTYPE)

    return pl.pallas_call(
        body, name=name,
        grid=(m // bm, n // bn, nk),
        in_specs=[pl.BlockSpec((bk, bm), lambda i, j, k: (k, i)), pl.BlockSpec((bk, bn), lambda i, j, k: (k, j))]
        + [ANY] * len(deps),
        out_specs=[pl.BlockSpec(out_block, out_index)] * 2,
        out_shape=[jax.ShapeDtypeStruct(out_shape, F32), jax.ShapeDtypeStruct(out_shape, WIRE_DTYPE)],
        compiler_params=_params(("arbitrary", "arbitrary", "arbitrary")),
    )(a, b, *deps)


def _local_grads(x, tgt, g1, win, poolw, pscale, qg, kg, bias, g2, mlp_weights, on_mlp_grads=None, first_dep=None):
    s, d = x.shape
    g1r, g2r = g1.reshape(1, d), g2.reshape(1, d)
    psr = pscale.reshape(1, POOL_WIDTH)
    qgr = jnp.tile(qg, N_HEADS).reshape(1, ATTN_WIDTH)
    kgr = jnp.tile(kg, N_HEADS).reshape(1, ATTN_WIDTH)
    pw_c = poolw.astype(MXU_DTYPE)
    buckets = jnp.asarray(_bucket_tables())
    bk = min(s, 4096)

    a, pooled, ypool, q32, k32, qn, kn, v = _f1_call(x, g1r, win, pw_c, psr, qgr, kgr, tm=1024, dep=first_dep)
    o, lse = _attn_fwd_call(qn, kn, v, bias)
    wout, wup, wdown = mlp_weights(o)
    mixed, c, ff, dz, dy, dh1, dypool, do, delta, dg2, loss = _f2_call(x, tgt, ypool, o, wout, wup, wdown, g2r, tm=256)
    dff = ff.shape[1]
    g_out = [g.reshape(N_CHIPS, d // N_CHIPS, d)
             for g in _wgrad_call(mixed, dh1, d, d, bk // 4, (d, d), (d, d), lambda i, j, k: (0, 0), "wgrad_out")]
    g_up = _wgrad_call(c, dz, d, dff // N_CHIPS, bk, (N_CHIPS, d, dff // N_CHIPS), (None, d, dff // N_CHIPS),
                       lambda i, j, k: (j, 0, 0), "wgrad_up")
    g_down = _wgrad_call(ff, dy, dff // N_CHIPS, d, bk, (N_CHIPS, dff // N_CHIPS, d), (None, dff // N_CHIPS, d),
                         lambda i, j, k: (i, 0, 0), "wgrad_down")
    dep = None if on_mlp_grads is None else on_mlp_grads(g_out[1], g_up[1], g_down[1])
    dqn, dkn, dv, dbias = _attn_bwd_call(qn, kn, v, do, lse, delta, bias, dep)
    dx, dproj, dg1, dqg, dkg, dpw, dps = _bproj_call(
        dqn, dkn, dv, q32, k32, dypool, pooled, x, dh1, win, pw_c, psr, qgr, kgr, g1r, tm=512)
    nin = dproj.shape[1] // N_CHIPS
    drb = _rel_bias_grad_call(dbias, buckets)
    g_in = _wgrad_call(a, dproj, d, nin, bk, (N_CHIPS, d, nin), (None, d, nin), lambda i, j, k: (j, 0, 0), "wgrad_in",
                       dep=drb)
    small = dict(
        mix_norm_g=dg1.reshape(d), mlp_norm_g=dg2.reshape(d), pool_scale=dps.reshape(POOL_WIDTH),
        q_norm_g=dqg.reshape(ATTN_WIDTH), k_norm_g=dkg.reshape(ATTN_WIDTH),
        rel_bias=drb[:, :N_BUCKETS].T, pool_w=dpw)
    return loss[0, 0], dx, (g_in, g_out, g_up, g_down), small


def _coords():
    return lax.axis_index("x"), lax.axis_index("y"), lax.axis_index("c")


def _other_chips(x, y):
    return [(1 - x, y), (x, 1 - y), (1 - x, 1 - y)]


def _remote(src, dst, send_sem, recv_sem, dev):
    return pltpu.make_async_remote_copy(src_ref=src, dst_ref=dst, send_sem=send_sem, recv_sem=recv_sem,
                                        device_id=dev, device_id_type=MESH)


PAIR_FORWARD_ID = 1
PAIR_ALLGATHER_ID = 2


def _sibling_handshake():
    x, y, c = _coords()
    barrier = pltpu.get_barrier_semaphore()
    pl.semaphore_signal(barrier, inc=1, device_id=(x, y, 1 - c), device_id_type=MESH)
    pl.semaphore_wait(barrier, 1)


def _halves(a):
    return a.reshape(a.shape[:-2] + (2, a.shape[-2] // 2, a.shape[-1]))


def _place_shards_call(shards, chip_idx, nch):
    nw = len(shards)

    def body(chip_ref, *refs):
        for w in range(nw):
            refs[nw + w][...] = refs[w][...].astype(WIRE_DTYPE)

    in_specs = [pl.BlockSpec((s.shape[0] // nch, s.shape[1]), lambda i, chip_ref: (i, 0)) for s in shards]
    out_specs = [pl.BlockSpec((None, s.shape[0] // nch, s.shape[1]), lambda i, chip_ref: (chip_ref[0], i, 0))
                 for s in shards]
    return pl.pallas_call(
        body, name="weights_place",
        grid_spec=pltpu.PrefetchScalarGridSpec(num_scalar_prefetch=1, grid=(nch,),
                                               in_specs=in_specs, out_specs=out_specs),
        out_shape=[jax.ShapeDtypeStruct((N_CHIPS,) + s.shape, WIRE_DTYPE) for s in shards],
        compiler_params=_params(("arbitrary",)),
    )(chip_idx, *shards)


def _allgather_call(placed, from_chips, name, meanwhile=None):
    nw = len(placed)
    ncp = 3 * nw
    extra, extra_specs, extra_shape, extra_body = meanwhile if meanwhile else ([], [], None, None)
    ne = len(extra)

    def body(*refs):
        outs = refs[nw + ne:2 * nw + ne]
        send1, recv1, send2, recv2 = refs[-4:]
        x, y, c = _coords()
        chip = 2 * x + y
        others = _other_chips(x, y)
        first, passed = [], []
        if not from_chips:
            _sibling_handshake()
        if from_chips:
            for w in range(nw):
                for k, (ox, oy) in enumerate(others):
                    mine = outs[w].at[chip, c]
                    cp = _remote(mine, mine, send1.at[3 * w + k], recv1.at[3 * w + k], (ox, oy, c))
                    cp.start()
                    first.append(cp)
        if meanwhile:
            extra_body(*refs[nw:nw + ne], refs[2 * nw + ne])
        for w in range(nw):
            for k, (ox, oy) in enumerate(others):
                piece = outs[w].at[2 * ox + oy, c]
                if from_chips:
                    _remote(piece, piece, send1.at[3 * w + k], recv1.at[3 * w + k], (ox, oy, c)).wait_recv()
                cp = _remote(piece, piece, send2.at[3 * w + k], recv2.at[3 * w + k], (x, y, 1 - c))
                cp.start()
                passed.append(cp)
        for w in range(nw):
            for k, (ox, oy) in enumerate(others):
                piece = outs[w].at[2 * ox + oy, 1 - c]
                _remote(piece, piece, send2.at[3 * w + k], recv2.at[3 * w + k], (x, y, 1 - c)).wait_recv()
        for cp in first + passed:
            cp.wait_send()

    return pl.pallas_call(
        body, name=name,
        in_specs=[ANY] * nw + list(extra_specs),
        out_specs=[ANY] * nw + ([pl.BlockSpec(memory_space=pltpu.VMEM)] if meanwhile else []),
        out_shape=[jax.ShapeDtypeStruct(s.shape, s.dtype) for s in placed] + ([extra_shape] if meanwhile else []),
        input_output_aliases={w: w for w in range(nw)},
        scratch_shapes=[pltpu.SemaphoreType.DMA((ncp,))] * 4,
        compiler_params=_params() if from_chips else _params(collective_id=PAIR_FORWARD_ID),
    )(*placed, *extra)


HBM_SPEC = pl.BlockSpec(memory_space=pltpu.HBM)
SEM_SPEC = pl.BlockSpec(memory_space=pltpu.SEMAPHORE)
SPLIT_EFFECT = pltpu.SideEffectType.DATAFLOW_SIDE_EFFECTING


def _in_hbm(a):
    return pltpu.with_memory_space_constraint(a, pltpu.HBM)


def _gather_copies(bufs, send, recv):
    x, y, c = _coords()
    chip = 2 * x + y
    cps = []
    for w, buf in enumerate(bufs):
        for k, (ox, oy) in enumerate(_other_chips(x, y)):
            mine, theirs = buf.at[chip, c], buf.at[2 * ox + oy, c]
            sems = (send.at[3 * w + k], recv.at[3 * w + k], (ox, oy, c))
            cps.append((_remote(mine, mine, *sems), _remote(theirs, theirs, *sems)))
    return cps


def _gather_start_call(bufs, after):
    nw = len(bufs)

    def body(*refs):
        ins, send, recv, token = refs[:nw], refs[nw + 1], refs[nw + 2], refs[2 * nw + 3]
        for out, _ in _gather_copies(ins, send, recv):
            out.start()
        token[...] = jnp.zeros(token.shape, F32)

    res = pl.pallas_call(
        body, name="weights_gather_start",
        in_specs=[HBM_SPEC] * nw + [ANY],
        out_specs=[SEM_SPEC, SEM_SPEC] + [HBM_SPEC] * nw + [pl.BlockSpec(memory_space=pltpu.VMEM)],
        out_shape=[pltpu.SemaphoreType.DMA((3 * nw,)), pltpu.SemaphoreType.DMA((3 * nw,))]
        + [pltpu.HBM(b.shape, b.dtype) for b in bufs] + [jax.ShapeDtypeStruct((8, LANES), F32)],
        input_output_aliases={w: 2 + w for w in range(nw)},
        compiler_params=pltpu.CompilerParams(has_side_effects=SPLIT_EFFECT),
    )(*[_in_hbm(b) for b in bufs], after)
    return res[0], res[1], list(res[2:2 + nw]), res[2 + nw]


def _gather_wait_call(bufs, send, recv, after):
    nw = len(bufs)

    def body(*refs):
        ins, send, recv = refs[:nw], refs[nw], refs[nw + 1]
        for out, back in _gather_copies(ins, send, recv):
            out.wait_send()
            back.wait_recv()

    return pl.pallas_call(
        body, name="weights_gather_wait",
        in_specs=[HBM_SPEC] * nw + [SEM_SPEC, SEM_SPEC, ANY],
        out_specs=[HBM_SPEC] * nw,
        out_shape=[pltpu.HBM(b.shape, b.dtype) for b in bufs],
        input_output_aliases={w: w for w in range(nw)},
        compiler_params=pltpu.CompilerParams(has_side_effects=SPLIT_EFFECT),
    )(*bufs, send, recv, after)


def _scatter_copies(srcs, lands, send, recv, wholes):
    x, y, c = _coords()
    me = 4 * x + 2 * y + c
    cps = []
    for w, (src, land) in enumerate(zip(srcs, lands)):
        for r in range(1, N_DEV):
            px, py, pc = ((1 - x) if r & 4 else x, (1 - y) if r & 2 else y, (1 - c) if r & 1 else c)
            sems = (send.at[(N_DEV - 1) * w + r - 1], recv.at[(N_DEV - 1) * w + r - 1], (px, py, pc))
            piece = src if wholes[w] else src.at[2 * px + py, pc]
            cps.append((_remote(piece, land.at[me], *sems), _remote(piece, land.at[4 * px + 2 * py + pc], *sems)))
    return cps


def _scatter_start_call(srcs, lands, wholes, name):
    nw = len(srcs)
    ncp = (N_DEV - 1) * nw

    def body(*refs):
        ins, lnd, send, recv, token = refs[:nw], refs[nw:2 * nw], refs[2 * nw], refs[2 * nw + 1], refs[4 * nw + 2]
        for out, _ in _scatter_copies(ins, lnd, send, recv, wholes):
            out.start()
        token[...] = jnp.zeros(token.shape, F32)

    res = pl.pallas_call(
        body, name=name,
        in_specs=[HBM_SPEC] * (2 * nw),
        out_specs=[SEM_SPEC, SEM_SPEC] + [HBM_SPEC] * (2 * nw) + [pl.BlockSpec(memory_space=pltpu.VMEM)],
        out_shape=[pltpu.SemaphoreType.DMA((ncp,)), pltpu.SemaphoreType.DMA((ncp,))]
        + [pltpu.HBM(b.shape, b.dtype) for b in list(srcs) + list(lands)] + [jax.ShapeDtypeStruct((8, LANES), F32)],
        input_output_aliases={i: 2 + i for i in range(2 * nw)},
        compiler_params=pltpu.CompilerParams(has_side_effects=SPLIT_EFFECT),
    )(*[_in_hbm(b) for b in list(srcs) + list(lands)])
    return res[0], res[1], list(res[2:2 + nw]), list(res[2 + nw:2 + 2 * nw]), res[2 + 2 * nw]


def _scatter_wait_call(srcs, lands, send, recv, after, wholes, name):
    nw = len(srcs)

    def body(*refs):
        ins, lnd, send, recv = refs[:nw], refs[nw:2 * nw], refs[2 * nw], refs[2 * nw + 1]
        for out, back in _scatter_copies(ins, lnd, send, recv, wholes):
            out.wait_send()
            back.wait_recv()

    res = pl.pallas_call(
        body, name=name,
        in_specs=[HBM_SPEC] * (2 * nw) + [SEM_SPEC, SEM_SPEC, ANY],
        out_specs=[HBM_SPEC] * (2 * nw),
        out_shape=[pltpu.HBM(b.shape, b.dtype) for b in list(srcs) + list(lands)],
        input_output_aliases={i: i for i in range(2 * nw)},
        compiler_params=pltpu.CompilerParams(has_side_effects=SPLIT_EFFECT),
    )(*srcs, *lands, send, recv, after)
    return list(res[nw:])


def _reduce_call(own, lands, idx, nch, name, dep=None):
    nw = len(own)
    deps = [] if dep is None else [dep]

    def body(idx_ref, *refs):
        refs = refs[:2 * nw] + refs[2 * nw + len(deps):]
        for w in range(nw):
            tot = refs[w][...]
            for r in range(1, N_DEV):
                tot = tot + refs[nw + w][idx_ref[1 + r]].astype(F32)
            refs[2 * nw + w][...] = tot

    in_specs, out_specs, out_shape = [], [], []
    for s in own:
        in_specs.append(pl.BlockSpec((None, None, s.shape[2] // nch, s.shape[3]),
                                     lambda i, idx_ref: (idx_ref[0], idx_ref[1], i, 0)))
    for s in own:
        in_specs.append(pl.BlockSpec((N_DEV, s.shape[2] // nch, s.shape[3]), lambda i, idx_ref: (0, i, 0)))
    for s in own:
        out_specs.append(pl.BlockSpec((None, s.shape[2] // nch, s.shape[3]), lambda i, idx_ref: (idx_ref[1], i, 0)))
        out_shape.append(jax.ShapeDtypeStruct((2,) + s.shape[2:], F32))
    return pl.pallas_call(
        body, name=name,
        grid_spec=pltpu.PrefetchScalarGridSpec(num_scalar_prefetch=1, grid=(nch,),
                                               in_specs=in_specs + [ANY] * len(deps), out_specs=out_specs),
        out_shape=out_shape,
        compiler_params=_params(("arbitrary",)),
    )(idx, *own, *lands, *deps)


def _pair_allgather_call(halves, name):
    nw = len(halves)

    def body(*refs):
        outs = refs[nw:2 * nw]
        send, recv = refs[2 * nw:]
        x, y, c = _coords()
        _sibling_handshake()
        cps = []
        for w in range(nw):
            cp = _remote(outs[w].at[c], outs[w].at[c], send.at[w], recv.at[w], (x, y, 1 - c))
            cp.start()
            cps.append(cp)
        for w in range(nw):
            theirs = outs[w].at[1 - c]
            _remote(theirs, theirs, send.at[w], recv.at[w], (x, y, 1 - c)).wait_recv()
        for cp in cps:
            cp.wait_send()

    outs = pl.pallas_call(
        body, name=name,
        in_specs=[ANY] * nw, out_specs=[ANY] * nw,
        out_shape=[jax.ShapeDtypeStruct(h.shape, h.dtype) for h in halves],
        input_output_aliases={w: w for w in range(nw)},
        scratch_shapes=[pltpu.SemaphoreType.DMA((nw,))] * 2,
        compiler_params=pltpu.CompilerParams(collective_id=PAIR_ALLGATHER_ID),
    )(*halves)
    return [o.reshape(2 * h.shape[1], h.shape[2]) for o, h in zip(outs, halves)]


def _adamw(w, g, m, v):
    m = ADAM_B1 * m + (1.0 - ADAM_B1) * g
    v = ADAM_B2 * v + (1.0 - ADAM_B2) * (g * g)
    m_hat = m / (1.0 - ADAM_B1 ** ADAM_STEP)
    v_hat = v / (1.0 - ADAM_B2 ** ADAM_STEP)
    delta = -ADAM_LR * (m_hat / (jnp.sqrt(v_hat) + ADAM_EPS) + ADAM_WD * w)
    return delta, m, v


def _adamw_call(ws, gs, ms, vs, nch, name):
    nw = len(ws)

    def body(*refs):
        for w in range(nw):
            g = refs[nw + w][...]
            delta, m, v = _adamw(refs[w][...], g, refs[2 * nw + w][...], refs[3 * nw + w][...])
            refs[4 * nw + w][...] = g
            refs[5 * nw + w][...] = delta
            refs[6 * nw + w][...] = m
            refs[7 * nw + w][...] = v

    specs = [pl.BlockSpec((a.shape[0] // nch, a.shape[1]), lambda i: (i, 0)) for a in ws]
    res = pl.pallas_call(
        body, name=name,
        grid=(nch,),
        in_specs=specs * 4, out_specs=specs * 4,
        out_shape=[jax.ShapeDtypeStruct(a.shape, F32) for a in ws] * 4,
        compiler_params=_params(("arbitrary",)),
    )(*ws, *gs, *ms, *vs)
    return res[:nw], res[nw:2 * nw], res[2 * nw:3 * nw], res[3 * nw:]


def _small_call(gathered, own, me_idx, w, m, v):
    cuts = dict(mix_norm_g=(0, 0, 1024), mlp_norm_g=(1, 0, 1024), pool_scale=(2, 0, POOL_WIDTH),
                rel_bias=(2, POOL_WIDTH, N_BUCKETS * N_HEADS), q_norm_g=(3, 0, HEAD_DIM), k_norm_g=(3, LANES, HEAD_DIM))
    n_out = len(cuts) + 1

    def fold(row):
        tot = row[:, 0:LANES] + row[:, LANES:2 * LANES] + row[:, 2 * LANES:3 * LANES] + row[:, 3 * LANES:4 * LANES]
        return tot + pltpu.roll(tot, HEAD_DIM, axis=1)

    def body(me_ref, gh_ref, gp_ref, oh_ref, op_ref, wh, wp, mh, mp, vh, vp, loss_ref, *outs):
        me = me_ref[0]

        def total(ga_ref, own_ref):
            term = lambda i: jnp.where(me == i, own_ref[...], ga_ref[i]).astype(F32)
            tot = term(0)
            for i in range(1, N_DEV):
                tot = tot + term(i)
            return tot

        g_head, g_pool = total(gh_ref, oh_ref), total(gp_ref, op_ref)
        unfolded = g_head[4:5, :]
        folded = jnp.concatenate([fold(unfolded[:, :ATTN_WIDTH]), fold(unfolded[:, ATTN_WIDTH:]),
                                  jnp.zeros((1, 1024 - 2 * LANES), F32)], axis=-1)
        row = lax.broadcasted_iota(jnp.int32, g_head.shape, 0)
        g_head = jnp.where(row == 3, folded, g_head)
        loss_ref[...] = g_head[LOSS_ROW:LOSS_ROW + 1, 0:LANES]
        heads = (g_head,) + _adamw(wh[...], g_head, mh[...], vh[...])
        pools = (g_pool,) + _adamw(wp[...], g_pool, mp[...], vp[...])
        for kind in range(4):
            mine = outs[kind * n_out:(kind + 1) * n_out]
            for out, (row, at, n) in zip(mine, cuts.values()):
                out[...] = heads[kind][row:row + 1, at:at + n]
            mine[-1][...] = pools[kind]

    vmem = pl.BlockSpec(memory_space=pltpu.VMEM)
    shapes = [jax.ShapeDtypeStruct((1, n), F32) for _, _, n in cuts.values()] + [jax.ShapeDtypeStruct(w[1].shape, F32)]
    res = pl.pallas_call(
        body, name="adamw_small",
        in_specs=[pl.BlockSpec(memory_space=pltpu.SMEM)] + [vmem] * 10,
        out_shape=[jax.ShapeDtypeStruct((1, LANES), F32)] + shapes * 4,
        compiler_params=_params(),
    )(me_idx, *gathered, *own, *w, *m, *v)

    def unpack(mine):
        p = {n: a.reshape(-1) for n, a in zip(cuts, mine)}
        p["rel_bias"] = p["rel_bias"].reshape(N_BUCKETS, N_HEADS)
        p["pool_w"] = mine[-1].reshape(len(POOL_WINDOWS), LANES, LANES)
        return p

    return [res[0]] + [unpack(res[1 + kind * n_out:1 + (kind + 1) * n_out]) for kind in range(4)]


def _pack_small(p, folded=True, loss=None):
    z = lambda n: jnp.zeros((n,), F32)
    rows = [p["mix_norm_g"], p["mlp_norm_g"],
            jnp.concatenate([p["pool_scale"], p["rel_bias"].reshape(-1), z(1024 - POOL_WIDTH - N_BUCKETS * N_HEADS)])]
    if folded:
        rows += [jnp.concatenate([p["q_norm_g"], z(LANES - HEAD_DIM), p["k_norm_g"], z(1024 - LANES - HEAD_DIM)]), z(1024)]
    else:
        rows += [z(1024), jnp.concatenate([p["q_norm_g"], p["k_norm_g"]])]
    rows += [z(1024) if loss is None else jnp.concatenate([loss.reshape(1), z(1023)])]
    return jnp.stack(rows + [z(1024)] * 2), p["pool_w"].reshape(-1, LANES)


_WEIGHT_ORDER = ("mix_norm_g", "w_in", "pool_w", "pool_scale", "q_norm_g", "k_norm_g", "rel_bias", "w_out",
                 "mlp_norm_g", "w_up", "w_down")
_BIG = ("w_in", "w_out", "w_up", "w_down")


def kernel(x, mix_norm_g, w_in, pool_w, pool_scale, q_norm_g, k_norm_g, rel_bias, w_out, mlp_norm_g, w_up, w_down, loss_target, m_mix_norm_g, m_w_in, m_pool_w, m_pool_scale, m_q_norm_g, m_k_norm_g, m_rel_bias, m_w_out, m_mlp_norm_g, m_w_up, m_w_down, v_mix_norm_g, v_w_in, v_pool_w, v_pool_scale, v_q_norm_g, v_k_norm_g, v_rel_bias, v_w_out, v_mlp_norm_g, v_w_up, v_w_down):
    w = dict(mix_norm_g=mix_norm_g, w_in=w_in, pool_w=pool_w, pool_scale=pool_scale, q_norm_g=q_norm_g,
             k_norm_g=k_norm_g, rel_bias=rel_bias, w_out=w_out, mlp_norm_g=mlp_norm_g, w_up=w_up, w_down=w_down)
    m = dict(mix_norm_g=m_mix_norm_g, w_in=m_w_in, pool_w=m_pool_w, pool_scale=m_pool_scale, q_norm_g=m_q_norm_g,
             k_norm_g=m_k_norm_g, rel_bias=m_rel_bias, w_out=m_w_out, mlp_norm_g=m_mlp_norm_g, w_up=m_w_up, w_down=m_w_down)
    v = dict(mix_norm_g=v_mix_norm_g, w_in=v_w_in, pool_w=v_pool_w, pool_scale=v_pool_scale, q_norm_g=v_q_norm_g,
             k_norm_g=v_k_norm_g, rel_bias=v_rel_bias, w_out=v_w_out, mlp_norm_g=v_mlp_norm_g, w_up=v_w_up, w_down=v_w_down)
    xc, yc, cc = _coords()

    c_idx = jnp.reshape(cc, (1,)).astype(jnp.int32)
    chip_idx = jnp.reshape(2 * xc + yc, (1,)).astype(jnp.int32)
    me = 4 * xc + 2 * yc + cc
    whole = lambda t: t.reshape(t.shape[0], t.shape[1] * t.shape[2], t.shape[3])

    placed = [_halves(p) for p in _place_shards_call([w[n] for n in _BIG], chip_idx, nch=4)]
    win_f, bias = _allgather_call(placed[:1], from_chips=True, name="weights_allgather_in",
                                  meanwhile=_bias_table_work(rel_bias))
    wsend, wrecv, in_flight, started = _gather_start_call(placed[1:], win_f)

    def mlp_weights(after):
        landed = _gather_wait_call(in_flight, wsend, wrecv, after)
        wout_f, wup_f, wdown_f = _allgather_call(landed, from_chips=False, name="weights_pair_forward")
        return whole(wout_f).reshape(-1, wout_f.shape[-1]), whole(wup_f), whole(wdown_f)

    split = []

    def on_mlp_grads(*wire_grads):
        srcs = [_halves(g) for g in wire_grads]
        lands = [lax.empty((N_DEV,) + s.shape[2:], s.dtype) for s in srcs]
        split.extend(_scatter_start_call(srcs, lands, [False] * len(srcs), "grads_scatter_start"))
        return split[4]

    loss_part, dx, big_grads, small_grads = _local_grads(
        x[0], loss_target[0], mix_norm_g, whole(win_f), pool_w, pool_scale, q_norm_g, k_norm_g, bias,
        mlp_norm_g, mlp_weights, on_mlp_grads, first_dep=started)
    g_in, g_out, g_up, g_down = big_grads
    gsend, grecv, srcs_thru, lands_thru, _ = split
    lands_mlp = _scatter_wait_call(srcs_thru, lands_thru, gsend, grecv, g_in[1], [False] * 3, "grads_scatter_wait")

    head_own, pool_own = _pack_small(small_grads, folded=False, loss=loss_part)
    small_own = (head_own, pool_own.astype(WIRE_DTYPE))
    last_srcs = [_halves(g_in[1]), *small_own]
    last_lands = [lax.empty((N_DEV,) + last_srcs[0].shape[2:], WIRE_DTYPE)]
    last_lands += [lax.empty((N_DEV,) + a.shape, a.dtype) for a in small_own]
    lsend, lrecv, last_srcs, last_lands, last_started = _scatter_start_call(
        last_srcs, last_lands, [False, True, True], "grads_scatter_start_last")
    idx = jnp.concatenate([chip_idx, c_idx] + [jnp.reshape(jnp.bitwise_xor(me, r), (1,)) for r in range(1, N_DEV)])
    idx = idx.astype(jnp.int32)
    mlp = _BIG[1:]

    def update(names, own32, lands, tag, dep=None):
        halves = _reduce_call([_halves(g) for g in own32], lands, idx, 4, "grads_reduce_" + tag, dep)
        reduced = _pair_allgather_call(list(halves), "grads_pair_allgather_" + tag)
        return _adamw_call([w[n] for n in names], reduced, [m[n] for n in names], [v[n] for n in names], 8, "adamw_" + tag)

    out_mlp = update(mlp, [g_out[0], g_up[0], g_down[0]], lands_mlp, "mlp", last_started)
    land_in, *small_all = _scatter_wait_call(last_srcs, last_lands, lsend, lrecv, out_mlp[3][-1], [False, True, True],
                                             "grads_scatter_wait_last")
    out_in = update(_BIG[:1], [g_in[0]], [land_in], "in")
    loss_row, grads, deltas, new_m, new_v = _small_call(
        small_all, small_own, jnp.reshape(me, (1,)).astype(jnp.int32), _pack_small(w), _pack_small(m), _pack_small(v))

    for k, res in enumerate((grads, deltas, new_m, new_v)):
        res[_BIG[0]] = out_in[k][0]
        for i, n in enumerate(mlp):
            res[n] = out_mlp[k][i]
    loss = loss_row[0, 0]
    return (loss, dx[None], *[grads[n] for n in _WEIGHT_ORDER], *[deltas[n] for n in _WEIGHT_ORDER],
            *[new_m[n] for n in _WEIGHT_ORDER], *[new_v[n] for n in _WEIGHT_ORDER])
```

```python
import math

import jax
import jax.numpy as jnp
import numpy as np
from jax import lax
from jax.experimental import pallas as pl
from jax.experimental.pallas import tpu as pltpu

F32 = jnp.float32
MXU_DTYPE = jnp.bfloat16
WIRE_DTYPE = jnp.bfloat16

NORM_EPS = 1e-6
NEG_INF = -1e30
LANES = 128
HEAD_DIM = 64
N_HEADS = 8
POOL_WIDTH = 512
ATTN_WIDTH = 512
POOL_WINDOWS = (2, 4, 8, 16)
POOL_HALO = 16
DILATED_PATTERNS = ((128, 1), (512, 4), (2048, 16))
ATT_BLOCK = 128
ATT_SUPER = ATT_BLOCK * max(dl for _, dl in DILATED_PATTERNS)
ATT_UNITS = ATT_SUPER // ATT_BLOCK
N_BUCKETS = 32
NO_BUCKET = -1
MAX_DISTANCE = 2048
N_CHIPS = 4
N_DEV = 8
ADAM_LR, ADAM_B1, ADAM_B2, ADAM_EPS, ADAM_WD, ADAM_STEP = 0.001, 0.9, 0.999, 1e-08, 0.01, 10
VMEM_LIMIT = 56 * 1024 * 1024
MESH = pl.DeviceIdType.MESH
ANY = pl.BlockSpec(memory_space=pl.ANY)

LOSS_ROW = 5


def _mm(a, b):
    return jnp.dot(a, b, preferred_element_type=F32)


def _mm_nt(a, b):
    return lax.dot_general(a, b, (((1,), (1,)), ((), ())), preferred_element_type=F32)


def _mm_tn(a, b):
    return lax.dot_general(a, b, (((0,), (0,)), ((), ())), preferred_element_type=F32)


def _params(sem=None, **kw):
    if sem is not None:
        kw["dimension_semantics"] = sem
    return pltpu.CompilerParams(vmem_limit_bytes=VMEM_LIMIT, **kw)


def _low_half():
    return lax.broadcasted_iota(jnp.int32, (1, LANES), 1) < HEAD_DIM


def _head_sum_bcast(y):
    lo = _low_half()
    outs = []
    for j in range(y.shape[1] // LANES):
        c = y[:, j * LANES:(j + 1) * LANES]
        s_lo = jnp.sum(jnp.where(lo, c, 0.0), axis=-1, keepdims=True)
        s_hi = jnp.sum(jnp.where(lo, 0.0, c), axis=-1, keepdims=True)
        outs.append(jnp.where(lo, s_lo, s_hi))
    return jnp.concatenate(outs, axis=-1)


def _rms_bwd(dn, hn, r):
    return r * (dn - hn * jnp.mean(dn * hn, axis=-1, keepdims=True))


def _t5_bucket_np(dist):
    max_exact = N_BUCKETS // 2
    d_f = np.maximum(dist, 1).astype(np.float32)
    ratio = (np.log(d_f / np.float32(max_exact)) / np.float32(math.log(MAX_DISTANCE / max_exact))).astype(np.float32)
    large = max_exact + (ratio * np.float32(N_BUCKETS - max_exact)).astype(np.int32)
    large = np.minimum(large, N_BUCKETS - 1)
    return np.where(dist < max_exact, dist, large).astype(np.int32)


def _window_offsets(dl):
    if dl == 1:
        return _by4_positions(ATT_BLOCK), _by4_positions(2 * ATT_BLOCK)
    return np.arange(ATT_BLOCK), np.arange(2 * ATT_BLOCK)


def _bucket_tables():
    tables = []
    for _, dl in DILATED_PATTERNS:
        qq, kk = _window_offsets(dl)
        dist = qq[:, None] + ATT_BLOCK - kk[None, :]
        bucket = _t5_bucket_np(np.clip(dist, 0, ATT_BLOCK) * dl)
        tables.append(np.where((dist >= 0) & (dist <= ATT_BLOCK), bucket, NO_BUCKET))
    return np.stack(tables).astype(np.int32)


def _previous_block_keys():
    return np.stack([np.broadcast_to(_window_offsets(dl)[1][None, :] < ATT_BLOCK, (ATT_BLOCK, 2 * ATT_BLOCK))
                     for _, dl in DILATED_PATTERNS])


def _f1_call(x, g1, win, poolw, pscale, qg, kg, tm, dep=None):
    s, d = x.shape
    nblk = s // tm
    deps = [] if dep is None else [dep]

    def body(x_ref, g1_ref, win_ref, pw_ref, ps_ref, qg_ref, kg_ref, *rest):
        a_ref, pooled_ref, ypool_ref, q32_ref, k32_ref, qn_ref, kn_ref, v_ref, ubuf = rest[len(deps):]
        i = pl.program_id(0)
        xv = x_ref[...]
        r = lax.rsqrt(jnp.mean(xv * xv, axis=-1, keepdims=True) + NORM_EPS)
        a = ((xv * r) * g1_ref[...]).astype(MXU_DTYPE)
        a_ref[...] = a
        u = _mm(a, win_ref[0])
        q = _mm(a, win_ref[1])
        k = _mm(a, win_ref[2])
        v_ref[...] = _mm(a, win_ref[3])
        q32_ref[...] = q
        k32_ref[...] = k
        rq = lax.rsqrt(_head_sum_bcast(q * q) * (1.0 / HEAD_DIM) + NORM_EPS)
        qn_ref[...] = ((q * rq) * qg_ref[...]) * (HEAD_DIM ** -0.5)
        rk = lax.rsqrt(_head_sum_bcast(k * k) * (1.0 / HEAD_DIM) + NORM_EPS)
        kn_ref[...] = (k * rk) * kg_ref[...]

        ubuf[0:POOL_HALO, :] = jnp.where(i > 0, ubuf[tm:tm + POOL_HALO, :], 0.0)
        ubuf[POOL_HALO:POOL_HALO + tm, :] = u
        t = i * tm + lax.broadcasted_iota(jnp.int32, (tm, 1), 0)
        for g, w in enumerate(POOL_WINDOWS):
            ls = slice(g * LANES, (g + 1) * LANES)
            ug = u[:, ls]
            acc = ug
            for sh in range(1, w):
                acc = acc + ubuf[POOL_HALO - sh:POOL_HALO - sh + tm, ls]
            cnt = jnp.minimum(t + 1, w).astype(F32)
            pooled = (acc / cnt - ug).astype(MXU_DTYPE)
            pooled_ref[:, ls] = pooled
            ypool_ref[:, ls] = (_mm(pooled, pw_ref[g]) * ps_ref[:, ls]).astype(MXU_DTYPE)

    tok = lambda w: pl.BlockSpec((tm, w), lambda i: (i, 0))
    full = lambda shp: pl.BlockSpec(shp, lambda i: (0,) * len(shp))
    return pl.pallas_call(
        body, name="fwd_inproj",
        grid=(nblk,),
        in_specs=[tok(d), full((1, d)), full(win.shape), full(poolw.shape), full((1, POOL_WIDTH)),
                  full((1, ATTN_WIDTH)), full((1, ATTN_WIDTH))] + [ANY] * len(deps),
        out_specs=[tok(d), tok(POOL_WIDTH), tok(POOL_WIDTH), tok(ATTN_WIDTH), tok(ATTN_WIDTH),
                   tok(ATTN_WIDTH), tok(ATTN_WIDTH), tok(ATTN_WIDTH)],
        out_shape=[jax.ShapeDtypeStruct((s, d), MXU_DTYPE),
                   jax.ShapeDtypeStruct((s, POOL_WIDTH), MXU_DTYPE),
                   jax.ShapeDtypeStruct((s, POOL_WIDTH), MXU_DTYPE),
                   jax.ShapeDtypeStruct((s, ATTN_WIDTH), F32),
                   jax.ShapeDtypeStruct((s, ATTN_WIDTH), F32),
                   jax.ShapeDtypeStruct((s, ATTN_WIDTH), F32),
                   jax.ShapeDtypeStruct((s, ATTN_WIDTH), F32),
                   jax.ShapeDtypeStruct((s, ATTN_WIDTH), F32)],
        scratch_shapes=[pltpu.VMEM((tm + POOL_HALO, POOL_WIDTH), F32)],
        compiler_params=_params(("arbitrary",)),
    )(x, g1, win, poolw, pscale, qg, kg, *deps)


DEINT = 4
assert [dl for _, dl in DILATED_PATTERNS] == [1, DEINT, DEINT * DEINT]


def _by4_positions(n):
    pos = np.arange(n)
    return DEINT * (pos % (n // DEINT)) + pos // (n // DEINT)


def _masked_bias(b_ref, p, n):
    return b_ref[p, jnp.minimum(n, 1)].reshape(2 * ATT_BLOCK, 2 * ATT_BLOCK)


def _unit_rows(u, dl):
    assert isinstance(u, int)
    sq, sk = ATT_SUPER // DEINT, 2 * ATT_SUPER // DEINT
    if dl == 1:
        n = ATT_BLOCK // DEINT
        return (u, [pl.ds(r * sq + n * u, n) for r in range(DEINT)],
                [pl.ds(r * sk + sk // 2 + n * (u - 1), 2 * n) for r in range(DEINT)])
    if dl == DEINT:
        r, b = u % DEINT, u // DEINT
        return (b, [pl.ds(r * sq + ATT_BLOCK * b, ATT_BLOCK)],
                [pl.ds(r * sk + sk // 2 + ATT_BLOCK * (b - 1), 2 * ATT_BLOCK)])
    r, a = u % DEINT, u // DEINT
    return 0, [pl.ds(r * sq + a, ATT_BLOCK, stride=DEINT)], [pl.ds(r * sk + a, 2 * ATT_BLOCK, stride=DEINT)]


def _take(ref, runs):
    parts = [ref[run, :] for run in runs]
    return parts[0] if len(parts) == 1 else jnp.concatenate(parts, axis=0)


def _put(ref, runs, value, add=False):
    n = value.shape[0] // len(runs)
    for i, run in enumerate(runs):
        part = value[i * n:(i + 1) * n]
        ref[run, :] = ref[run, :] + part if add else part


def _deinterleave(dst, src, n):
    seg = n // DEINT
    for r in range(DEINT):
        dst[r * seg:(r + 1) * seg, :] = src[pl.ds(r, seg, stride=DEINT), :]


def _deinterleave_pair(dst, prev, cur):
    seg = prev.shape[0] // DEINT
    for r in range(DEINT):
        dst[2 * r * seg:(2 * r + 1) * seg, :] = prev[pl.ds(r, seg, stride=DEINT), :]
        dst[(2 * r + 1) * seg:(2 * r + 2) * seg, :] = cur[pl.ds(r, seg, stride=DEINT), :]


def _interleave(dst, src, n, offset=0):
    seg = n // DEINT
    stride = src.shape[0] // DEINT
    for r in range(DEINT):
        dst[pl.ds(r, seg, stride=DEINT), :] = src[r * stride + offset:r * stride + offset + seg, :]


def _attn_fwd_call(qn, kn, v, bias):
    s, w = qn.shape
    nsb = s // ATT_SUPER
    npair = w // LANES

    def body(q_ref, kc_ref, vc_ref, b_ref, o_ref, lse_ref, qf, kf, vf, acc_s, m_s, l_s):
        sb = pl.program_id(1)

        @pl.when((pl.program_id(0) == 0) & (sb == 0))
        def _():
            kf[...] = jnp.zeros_like(kf)
            vf[...] = jnp.zeros_like(vf)

        seg = ATT_SUPER // DEINT
        _deinterleave(qf, q_ref, ATT_SUPER)
        for r in range(DEINT):
            for dst, src in ((kf, kc_ref), (vf, vc_ref)):
                dst[2 * r * seg:(2 * r + 1) * seg, :] = dst[(2 * r + 1) * seg:(2 * r + 2) * seg, :]
                dst[(2 * r + 1) * seg:(2 * r + 2) * seg, :] = src[pl.ds(r, seg, stride=DEINT), :]
        lo = _low_half()
        for p, (_, dl) in enumerate(DILATED_PATTERNS):
            def unit(u, carry, p=p, dl=dl):
                b, rows_q, rows_k = _unit_rows(u, dl)
                qp = _take(qf, rows_q).astype(MXU_DTYPE)
                kcat = _take(kf, rows_k).astype(MXU_DTYPE)
                vcat = _take(vf, rows_k).astype(MXU_DTYPE)
                zero = jnp.zeros_like(qp)
                q2 = jnp.concatenate([jnp.where(lo, qp, zero), jnp.where(lo, zero, qp)], axis=0)
                sc = _mm_nt(q2, kcat) + _masked_bias(b_ref, p, sb * (ATT_UNITS // dl) + b)
                m2 = jnp.max(sc, axis=-1, keepdims=True)
                pr = jnp.exp(sc - m2)
                l2 = jnp.sum(pr, axis=-1, keepdims=True)
                acc2 = _mm(pr.astype(MXU_DTYPE), vcat)
                acc = jnp.where(lo, acc2[:ATT_BLOCK], acc2[ATT_BLOCK:])
                m = jnp.where(lo, m2[:ATT_BLOCK], m2[ATT_BLOCK:])
                l = jnp.where(lo, l2[:ATT_BLOCK], l2[ATT_BLOCK:])
                if p == 0:
                    _put(acc_s, rows_q, acc)
                    _put(m_s, rows_q, m)
                    _put(l_s, rows_q, l)
                else:
                    m_old = _take(m_s, rows_q)
                    m_new = jnp.maximum(m_old, m)
                    a_old = jnp.exp(m_old - m_new)
                    a_new = jnp.exp(m - m_new)
                    _put(acc_s, rows_q, a_old * _take(acc_s, rows_q) + a_new * acc)
                    _put(l_s, rows_q, a_old * _take(l_s, rows_q) + a_new * l)
                    _put(m_s, rows_q, m_new)
                return carry

            for u in range(ATT_UNITS):
                unit(u, None)
        l = l_s[...]
        acc_s[...] = acc_s[...] / l
        m_s[...] = m_s[...] + jnp.log(l)
        _interleave(o_ref, acc_s, ATT_SUPER)
        _interleave(lse_ref, m_s, ATT_SUPER)

    cur = pl.BlockSpec((ATT_SUPER, LANES), lambda j, t: (t, j))
    bspec = pl.BlockSpec((len(DILATED_PATTERNS), 2, 2, ATT_BLOCK, 2 * ATT_BLOCK), lambda j, t: (0, 0, j, 0, 0))
    return pl.pallas_call(
        body, name="attn_fwd",
        grid=(npair, nsb),
        in_specs=[cur, cur, cur, bspec],
        out_specs=[cur, cur],
        out_shape=[jax.ShapeDtypeStruct((s, w), F32), jax.ShapeDtypeStruct((s, w), F32)],
        scratch_shapes=[pltpu.VMEM((ATT_SUPER, LANES), F32), pltpu.VMEM((2 * ATT_SUPER, LANES), F32),
                        pltpu.VMEM((2 * ATT_SUPER, LANES), F32), pltpu.VMEM((ATT_SUPER, LANES), F32),
                        pltpu.VMEM((ATT_SUPER, LANES), F32), pltpu.VMEM((ATT_SUPER, LANES), F32)],
        compiler_params=_params(("arbitrary", "arbitrary")),
    )(qn, kn, v, bias)


def _attn_bwd_call(qn, kn, v, do, lse, delta, bias, dep=None):
    s, w = qn.shape
    nsb = s // ATT_SUPER
    npair = w // LANES
    deps = [] if dep is None else [dep]

    def body(q_ref, kc_ref, kp_ref, vc_ref, vp_ref, do_ref, lse_ref, dlt_ref, b_ref, *rest):
        dq_ref, dk_ref, dv_ref, db_ref, qf, kf, vf, dof, lsef, dltf, dqf, dkf, dvf = rest[len(deps):]
        step = pl.program_id(1)
        sb = nsb - 1 - step
        seg = ATT_SUPER // DEINT
        _deinterleave(qf, q_ref, ATT_SUPER)
        _deinterleave(dof, do_ref, ATT_SUPER)
        _deinterleave_pair(kf, kp_ref, kc_ref)
        _deinterleave_pair(vf, vp_ref, vc_ref)
        _deinterleave(lsef, lse_ref, ATT_SUPER)
        _deinterleave(dltf, dlt_ref, ATT_SUPER)

        db_ref[...] = jnp.where(step > 0, db_ref[...], 0.0)
        for acc in (dkf, dvf):
            for r in range(DEINT):
                this, before = pl.ds((2 * r + 1) * seg, seg), pl.ds(2 * r * seg, seg)
                acc[this, :] = jnp.where(step > 0, acc[before, :], 0.0)
                acc[before, :] = jnp.zeros((seg, LANES), F32)
        lo = _low_half()
        for p, (_, dl) in enumerate(DILATED_PATTERNS):
            def unit(u, carry, p=p, dl=dl):
                b, rows_q, rows_k = _unit_rows(u, dl)
                qp = _take(qf, rows_q).astype(MXU_DTYPE)
                dop = _take(dof, rows_q).astype(MXU_DTYPE)
                kcat = _take(kf, rows_k).astype(MXU_DTYPE)
                vcat = _take(vf, rows_k).astype(MXU_DTYPE)
                lse2 = _take(lsef, rows_q)
                dlt2 = _take(dltf, rows_q)
                zero = jnp.zeros_like(qp)
                q2 = jnp.concatenate([jnp.where(lo, qp, zero), jnp.where(lo, zero, qp)], axis=0)
                do2 = jnp.concatenate([jnp.where(lo, dop, zero), jnp.where(lo, zero, dop)], axis=0)
                lse_c = jnp.concatenate([lse2[:, 0:1], lse2[:, HEAD_DIM:HEAD_DIM + 1]], axis=0)
                dlt_c = jnp.concatenate([dlt2[:, 0:1], dlt2[:, HEAD_DIM:HEAD_DIM + 1]], axis=0)
                sc = _mm_nt(q2, kcat) + _masked_bias(b_ref, p, sb * (ATT_UNITS // dl) + b)
                pr = jnp.exp(sc - lse_c)
                ds = pr * (_mm_nt(do2, vcat) - dlt_c)
                db_ref[p] += ds.reshape(2, ATT_BLOCK, 2 * ATT_BLOCK)
                ds_c = ds.astype(MXU_DTYPE)
                dq2 = _mm(ds_c, kcat)
                dk = _mm_tn(ds_c, q2)
                dv = _mm_tn(pr.astype(MXU_DTYPE), do2)
                dq = jnp.where(lo, dq2[:ATT_BLOCK], dq2[ATT_BLOCK:])
                _put(dqf, rows_q, dq, add=p > 0)
                _put(dkf, rows_k, dk, add=True)
                _put(dvf, rows_k, dv, add=True)
                return carry

            for u in range(ATT_UNITS):
                unit(u, None)
        _interleave(dq_ref, dqf, ATT_SUPER)
        _interleave(dk_ref, dkf, ATT_SUPER, offset=seg)
        _interleave(dv_ref, dvf, ATT_SUPER, offset=seg)

    cur = pl.BlockSpec((ATT_SUPER, LANES), lambda j, t: (nsb - 1 - t, j))
    prev = pl.BlockSpec((ATT_SUPER, LANES), lambda j, t: (jnp.maximum(nsb - 2 - t, 0), j))
    npat = len(DILATED_PATTERNS)
    bspec = pl.BlockSpec((npat, 2, 2, ATT_BLOCK, 2 * ATT_BLOCK), lambda j, t: (0, 0, j, 0, 0))
    dbspec = pl.BlockSpec((npat, 2, ATT_BLOCK, 2 * ATT_BLOCK), lambda j, t: (0, j, 0, 0))
    sup = lambda: pltpu.VMEM((ATT_SUPER, LANES), F32)
    sup2 = lambda: pltpu.VMEM((2 * ATT_SUPER, LANES), F32)
    return pl.pallas_call(
        body, name="attn_bwd",
        grid=(npair, nsb),
        in_specs=[cur, cur, prev, cur, prev, cur, cur, cur, bspec] + [ANY] * len(deps),
        out_specs=[cur, cur, cur, dbspec],
        out_shape=[jax.ShapeDtypeStruct((s, w), F32)] * 3
        + [jax.ShapeDtypeStruct((npat, N_HEADS, ATT_BLOCK, 2 * ATT_BLOCK), F32)],
        scratch_shapes=[sup(), sup2(), sup2(), sup(), sup(), sup(), sup(), sup2(), sup2()],
        compiler_params=_params(("arbitrary", "arbitrary")),
    )(qn, kn, kn, v, v, do, lse, delta, bias, *deps)


def _bias_table_work(rel_bias):
    buckets = jnp.asarray(_bucket_tables())
    prev_keys = jnp.asarray(_previous_block_keys().astype(np.int32))
    npat = buckets.shape[0]

    def body(rb_ref, bk_ref, pk_ref, out_ref):
        for p in range(npat):
            for half in range(2):
                ks = slice(half * ATT_BLOCK, (half + 1) * ATT_BLOCK)
                bk = bk_ref[p, :, ks]
                absent = pk_ref[p, :, ks] != 0
                for h in range(N_HEADS):
                    def pick(b, acc, h=h, bk=bk):
                        return jnp.where(bk == b, rb_ref[b, h], acc)

                    tab = lax.fori_loop(0, N_BUCKETS, pick, jnp.full((ATT_BLOCK, ATT_BLOCK), NEG_INF, F32))
                    out_ref[p, 1, h, :, ks] = tab
                    out_ref[p, 0, h, :, ks] = jnp.where(absent, NEG_INF, tab)

    vmem = pl.BlockSpec(memory_space=pltpu.VMEM)
    return ([rel_bias, buckets, prev_keys], [pl.BlockSpec(memory_space=pltpu.SMEM), vmem, vmem],
            jax.ShapeDtypeStruct((npat, 2, N_HEADS, ATT_BLOCK, 2 * ATT_BLOCK), F32), body)


def _rel_bias_grad_call(dbias, buckets):
    npat, nh = dbias.shape[0], dbias.shape[1]

    def body(db_ref, bk_ref, out_ref):
        lane = lax.broadcasted_iota(jnp.int32, (nh, LANES), 1)
        out = jnp.zeros((nh, LANES), F32)
        for b in range(N_BUCKETS):
            tot = jnp.zeros((nh, 1), F32)
            for p in range(npat):
                hit = jnp.where(bk_ref[p][None] == b, db_ref[p], 0.0)
                tot = tot + jnp.sum(jnp.sum(hit, axis=1), axis=-1, keepdims=True)
            out = jnp.where(lane == b, tot, out)
        out_ref[...] = out

    return pl.pallas_call(
        body, name="rel_bias_grad",
        out_shape=jax.ShapeDtypeStruct((nh, LANES), F32),
        compiler_params=_params(),
    )(dbias, buckets)


def _f2_call(x, tgt, ypool, o, wout, wup, wdown, g2, tm):
    s, d = x.shape
    nblk = s // tm
    nch, _, fch = wup.shape
    dff = nch * fch
    mixw = POOL_WIDTH + ATTN_WIDTH

    def body(x_ref, t_ref, yp_ref, o_ref, g2_ref, wout_hbm, wup_hbm, wdown_hbm,
             mixed_ref, c_ref, ff_ref, dz_ref, dy_ref, dh1_ref, dyp_ref, do_ref, dlt_ref, dg2_ref, loss_ref,
             wout_v, wup_v, wdown_v, rz, wsem):
        i = pl.program_id(0)

        @pl.when(i == 0)
        def _():
            copies = [pltpu.make_async_copy(wout_hbm, wout_v, wsem.at[0])]
            for j in range(nch):
                copies.append(pltpu.make_async_copy(wup_hbm.at[j], wup_v.at[j], wsem.at[1 + 2 * j]))
                copies.append(pltpu.make_async_copy(wdown_hbm.at[j], wdown_v.at[j], wsem.at[2 + 2 * j]))
            for k, cp in enumerate(copies):
                cp.start(priority=k % 2)
            dg2_ref[...] = jnp.zeros(dg2_ref.shape, F32)
            loss_ref[...] = jnp.zeros(loss_ref.shape, F32)
            for cp in copies:
                cp.wait()

        o = o_ref[...]
        mixed = jnp.concatenate([yp_ref[...], o.astype(MXU_DTYPE)], axis=-1)
        mixed_ref[...] = mixed
        h1 = x_ref[...] + _mm(mixed, wout_v[...])
        r2 = lax.rsqrt(jnp.mean(h1 * h1, axis=-1, keepdims=True) + NORM_EPS)
        hn = h1 * r2
        c = (hn * g2_ref[...]).astype(MXU_DTYPE)
        c_ref[...] = c
        y = h1
        for j in range(nch):
            cs = slice(j * fch, (j + 1) * fch)
            z = jnp.maximum(_mm(c, wup_v[j]), 0.0)
            rz[:, cs] = z
            ff = (z * z).astype(MXU_DTYPE)
            ff_ref[:, cs] = ff
            y = y + _mm(ff, wdown_v[j])
        err = y - t_ref[...]
        loss_ref[...] += jnp.sum(err * err) * (0.5 / d)
        dy = err * (1.0 / d)
        dy_c = dy.astype(MXU_DTYPE)
        dy_ref[...] = dy_c
        dc = jnp.zeros((tm, d), F32)
        for j in range(nch):
            cs = slice(j * fch, (j + 1) * fch)
            dz = (_mm_nt(dy_c, wdown_v[j]) * (2.0 * rz[:, cs])).astype(MXU_DTYPE)
            dz_ref[:, cs] = dz
            dc = dc + _mm_nt(dz, wup_v[j])
        dg2_ref[...] += jnp.sum(dc * hn, axis=0, keepdims=True)
        dh1 = dy + _rms_bwd(dc * g2_ref[...], hn, r2)
        dh1_ref[...] = dh1
        dmix = _mm_nt(dh1.astype(MXU_DTYPE), wout_v[...])
        dyp_ref[...] = dmix[:, :POOL_WIDTH]
        do = dmix[:, POOL_WIDTH:]
        do_ref[...] = do
        dlt_ref[...] = _head_sum_bcast(do * o)

    tok = lambda w: pl.BlockSpec((tm, w), lambda i: (i, 0))
    const = lambda shp: pl.BlockSpec(shp, lambda i: (0,) * len(shp))
    return pl.pallas_call(
        body, name="fwd_mlp_bwd_mlp",
        grid=(nblk,),
        in_specs=[tok(d), tok(d), tok(POOL_WIDTH), tok(ATTN_WIDTH), const((1, d)), ANY, ANY, ANY],
        out_specs=[tok(mixw), tok(d), tok(dff), tok(dff), tok(d), tok(d), tok(POOL_WIDTH), tok(ATTN_WIDTH),
                   tok(ATTN_WIDTH), const((1, d)), const((1, LANES))],
        out_shape=[jax.ShapeDtypeStruct((s, mixw), MXU_DTYPE),
                   jax.ShapeDtypeStruct((s, d), MXU_DTYPE),
                   jax.ShapeDtypeStruct((s, dff), MXU_DTYPE),
                   jax.ShapeDtypeStruct((s, dff), MXU_DTYPE),
                   jax.ShapeDtypeStruct((s, d), MXU_DTYPE),
                   jax.ShapeDtypeStruct((s, d), F32),
                   jax.ShapeDtypeStruct((s, POOL_WIDTH), F32),
                   jax.ShapeDtypeStruct((s, ATTN_WIDTH), F32),
                   jax.ShapeDtypeStruct((s, ATTN_WIDTH), F32),
                   jax.ShapeDtypeStruct((1, d), F32),
                   jax.ShapeDtypeStruct((1, LANES), F32)],
        scratch_shapes=[pltpu.VMEM(wout.shape, MXU_DTYPE), pltpu.VMEM(wup.shape, MXU_DTYPE),
                        pltpu.VMEM(wdown.shape, MXU_DTYPE), pltpu.VMEM((tm, dff), F32),
                        pltpu.SemaphoreType.DMA((1 + 2 * nch,))],
        compiler_params=_params(("arbitrary",)),
    )(x, tgt, ypool, o, g2, wout, wup, wdown)


RING_SLOTS = 3


def _bproj_call(dqn, dkn, dv, q32, k32, dypool, pooled, x, dh1, win, poolw, pscale, qg, kg, g1, tm):
    s, d = x.shape
    nblk = s // tm
    ngrp = len(POOL_WINDOWS)
    streams = [dqn, dkn, dv, q32, k32, dypool, pooled, x, dh1]
    ns = len(streams)
    assert nblk >= 2

    def body(*refs):
        hbm = refs[:ns]
        win_hbm, pw_ref, ps_ref, qg_ref, kg_ref, g1_ref = refs[ns:ns + 6]
        dx_ref, dproj_ref, dg1_ref, dqg_ref, dkg_ref, dpw_ref, dps_ref, win_v, ebuf = refs[ns + 6:ns + 15]
        rings, sems = refs[ns + 15:2 * ns + 15], refs[2 * ns + 15]
        step = pl.program_id(0)
        i = nblk - 1 - step

        def fetch(t):
            rows = pl.ds(pl.multiple_of((nblk - 1 - t) * tm, tm), tm)
            return [pltpu.make_async_copy(h.at[rows], ring.at[t % RING_SLOTS], sems.at[k, t % RING_SLOTS])
                    for k, (h, ring) in enumerate(zip(hbm, rings))]

        @pl.when(step == 0)
        def _():
            for t in range(2):
                for k, cp in enumerate(fetch(t)):
                    cp.start(priority=k % 2)
            pltpu.sync_copy(win_hbm, win_v)
            dg1_ref[...] = jnp.zeros(dg1_ref.shape, F32)
            dqg_ref[...] = jnp.zeros(dqg_ref.shape, F32)
            dkg_ref[...] = jnp.zeros(dkg_ref.shape, F32)
            dpw_ref[...] = jnp.zeros(dpw_ref.shape, F32)
            dps_ref[...] = jnp.zeros(dps_ref.shape, F32)
            ebuf[tm:tm + POOL_HALO, :] = jnp.zeros((POOL_HALO, POOL_WIDTH), F32)

        @pl.when(step > 0)
        def _():
            ebuf[tm:tm + POOL_HALO, :] = ebuf[0:POOL_HALO, :]

        @pl.when(step + 2 < nblk)
        def _():
            for k, cp in enumerate(fetch(step + 2)):
                cp.start(priority=k % 2)

        for cp in fetch(step):
            cp.wait()
        dqn_ref, dkn_ref, dv_ref, q_ref, k_ref, dyp_ref, pooled_ref, x_ref, dh1_ref = (
            ring.at[step % RING_SLOTS] for ring in rings)

        def qk_bwd(dn_sum, raw, gain, scale, dgain_ref):
            rr = lax.rsqrt(_head_sum_bcast(raw * raw) * (1.0 / HEAD_DIM) + NORM_EPS)
            hn = raw * rr
            dgain_ref[...] += jnp.sum(dn_sum * hn, axis=0, keepdims=True) * scale
            dn = dn_sum * (gain * scale)
            return rr * (dn - hn * (_head_sum_bcast(dn * hn) * (1.0 / HEAD_DIM)))

        dq = qk_bwd(dqn_ref[...], q_ref[...], qg_ref[...], HEAD_DIM ** -0.5, dqg_ref)
        dk = qk_bwd(dkn_ref[...], k_ref[...], kg_ref[...], 1.0, dkg_ref)

        t = i * tm + lax.broadcasted_iota(jnp.int32, (tm, 1), 0)
        dpooled = []
        for g, w in enumerate(POOL_WINDOWS):
            ls = slice(g * LANES, (g + 1) * LANES)
            dm = dyp_ref[:, ls]
            pg = pooled_ref[:, ls]
            dps_ref[:, ls] += jnp.sum(dm * _mm(pg, pw_ref[g]), axis=0, keepdims=True)
            dms = (dm * ps_ref[:, ls]).astype(MXU_DTYPE)
            dpw_ref[g] += _mm_tn(pg, dms)
            dpg = _mm_nt(dms, pw_ref[g])
            dpooled.append(dpg)
            ebuf[0:tm, ls] = dpg / jnp.minimum(t + 1, w).astype(F32)
        du = []
        for g, w in enumerate(POOL_WINDOWS):
            ls = slice(g * LANES, (g + 1) * LANES)
            acc = ebuf[0:tm, ls]
            for sh in range(1, w):
                acc = acc + ebuf[sh:sh + tm, ls]
            du.append(acc - dpooled[g])
        parts = [jnp.concatenate(du, axis=-1), dq, dk, dv_ref[...]]
        da = jnp.zeros((tm, d), F32)
        for p, part in enumerate(parts):
            pc = part.astype(MXU_DTYPE)
            dproj_ref[:, p * POOL_WIDTH:(p + 1) * POOL_WIDTH] = pc
            da = da + _mm_nt(pc, win_v[p])
        xv = x_ref[...]
        r = lax.rsqrt(jnp.mean(xv * xv, axis=-1, keepdims=True) + NORM_EPS)
        xn = xv * r
        dg1_ref[...] += jnp.sum(da * xn, axis=0, keepdims=True)
        dx_ref[...] = dh1_ref[...] + _rms_bwd(da * g1_ref[...], xn, r)

    tok = lambda w: pl.BlockSpec((tm, w), lambda t: (nblk - 1 - t, 0))
    const = lambda shp: pl.BlockSpec(shp, lambda t: (0,) * len(shp))
    return pl.pallas_call(
        body, name="bwd_inproj",
        grid=(nblk,),
        in_specs=[ANY] * (ns + 1) + [const(poolw.shape), const((1, POOL_WIDTH)), const((1, ATTN_WIDTH)),
                                     const((1, ATTN_WIDTH)), const((1, d))],
        out_specs=[tok(d), tok(4 * POOL_WIDTH), const((1, d)), const((1, ATTN_WIDTH)), const((1, ATTN_WIDTH)),
                   const((ngrp, LANES, LANES)), const((1, POOL_WIDTH))],
        out_shape=[jax.ShapeDtypeStruct((s, d), F32),
                   jax.ShapeDtypeStruct((s, 4 * POOL_WIDTH), MXU_DTYPE),
                   jax.ShapeDtypeStruct((1, d), F32),
                   jax.ShapeDtypeStruct((1, ATTN_WIDTH), F32),
                   jax.ShapeDtypeStruct((1, ATTN_WIDTH), F32),
                   jax.ShapeDtypeStruct((ngrp, LANES, LANES), F32),
                   jax.ShapeDtypeStruct((1, POOL_WIDTH), F32)],
        scratch_shapes=[pltpu.VMEM(win.shape, MXU_DTYPE), pltpu.VMEM((tm + POOL_HALO, POOL_WIDTH), F32)]
        + [pltpu.VMEM((RING_SLOTS, tm, a.shape[1]), a.dtype) for a in streams]
        + [pltpu.SemaphoreType.DMA((ns, RING_SLOTS))],
        compiler_params=_params(("arbitrary",)),
    )(*streams, win, poolw, pscale, qg, kg, g1)


def _wgrad_call(a, b, bm, bn, bk, out_shape, out_block, out_index, name, dep=None):
    s, m = a.shape
    _, n = b.shape
    nk = s // bk
    deps = [] if dep is None else [dep]

    def body(a_ref, b_ref, *rest):
        o_ref, wire_ref = rest[len(deps):]
        k = pl.program_id(2)
        acc = jnp.where(k > 0, o_ref[...], 0.0) + _mm_tn(a_ref[...].astype(MXU_DTYPE), b_ref[...].astype(MXU_DTYPE))
        o_ref[...] = acc
        wire_ref[...] = acc.astype(WIRE_DTYPE)

    return pl.pallas_call(
        body, name=name,
        grid=(m // bm, n // bn, nk),
        in_specs=[pl.BlockSpec((bk, bm), lambda i, j, k: (k, i)), pl.BlockSpec((bk, bn), lambda i, j, k: (k, j))]
        + [ANY] * len(deps),
        out_specs=[pl.BlockSpec(out_block, out_index)] * 2,
        out_shape=[jax.ShapeDtypeStruct(out_shape, F32), jax.ShapeDtypeStruct(out_shape, WIRE_DTYPE)],
        compiler_params=_params(("arbitrary", "arbitrary", "arbitrary")),
    )(a, b, *deps)


def _local_grads(x, tgt, g1, win, poolw, pscale, qg, kg, bias, g2, mlp_weights, on_mlp_grads=None, first_dep=None):
    s, d = x.shape
    g1r, g2r = g1.reshape(1, d), g2.reshape(1, d)
    psr = pscale.reshape(1, POOL_WIDTH)
    qgr = jnp.tile(qg, N_HEADS).reshape(1, ATTN_WIDTH)
    kgr = jnp.tile(kg, N_HEADS).reshape(1, ATTN_WIDTH)
    pw_c = poolw.astype(MXU_DTYPE)
    buckets = jnp.asarray(_bucket_tables())
    bk = min(s, 4096)

    a, pooled, ypool, q32, k32, qn, kn, v = _f1_call(x, g1r, win, pw_c, psr, qgr, kgr, tm=1024, dep=first_dep)
    o, lse = _attn_fwd_call(qn, kn, v, bias)
    wout, wup, wdown = mlp_weights(o)
    mixed, c, ff, dz, dy, dh1, dypool, do, delta, dg2, loss = _f2_call(x, tgt, ypool, o, wout, wup, wdown, g2r, tm=256)
    dff = ff.shape[1]
    g_out = [g.reshape(N_CHIPS, d // N_CHIPS, d)
             for g in _wgrad_call(mixed, dh1, d, d, bk // 4, (d, d), (d, d), lambda i, j, k: (0, 0), "wgrad_out")]
    g_up = _wgrad_call(c, dz, d, dff // N_CHIPS, bk, (N_CHIPS, d, dff // N_CHIPS), (None, d, dff // N_CHIPS),
                       lambda i, j, k: (j, 0, 0), "wgrad_up")
    g_down = _wgrad_call(ff, dy, dff // N_CHIPS, d, bk, (N_CHIPS, dff // N_CHIPS, d), (None, dff // N_CHIPS, d),
                         lambda i, j, k: (i, 0, 0), "wgrad_down")
    dep = None if on_mlp_grads is None else on_mlp_grads(g_out[1], g_up[1], g_down[1])
    dqn, dkn, dv, dbias = _attn_bwd_call(qn, kn, v, do, lse, delta, bias, dep)
    dx, dproj, dg1, dqg, dkg, dpw, dps = _bproj_call(
        dqn, dkn, dv, q32, k32, dypool, pooled, x, dh1, win, pw_c, psr, qgr, kgr, g1r, tm=512)
    nin = dproj.shape[1] // N_CHIPS
    drb = _rel_bias_grad_call(dbias, buckets)
    g_in = _wgrad_call(a, dproj, d, nin, bk, (N_CHIPS, d, nin), (None, d, nin), lambda i, j, k: (j, 0, 0), "wgrad_in",
                       dep=drb)
    small = dict(
        mix_norm_g=dg1.reshape(d), mlp_norm_g=dg2.reshape(d), pool_scale=dps.reshape(POOL_WIDTH),
        q_norm_g=dqg.reshape(ATTN_WIDTH), k_norm_g=dkg.reshape(ATTN_WIDTH),
        rel_bias=drb[:, :N_BUCKETS].T, pool_w=dpw)
    return loss[0, 0], dx, (g_in, g_out, g_up, g_down), small


def _coords():
    return lax.axis_index("x"), lax.axis_index("y"), lax.axis_index("c")


def _other_chips(x, y):
    return [(1 - x, y), (x, 1 - y), (1 - x, 1 - y)]


def _remote(src, dst, send_sem, recv_sem, dev):
    return pltpu.make_async_remote_copy(src_ref=src, dst_ref=dst, send_sem=send_sem, recv_sem=recv_sem,
                                        device_id=dev, device_id_type=MESH)


PAIR_FORWARD_ID = 1
PAIR_ALLGATHER_ID = 2


def _sibling_handshake():
    x, y, c = _coords()
    barrier = pltpu.get_barrier_semaphore()
    pl.semaphore_signal(barrier, inc=1, device_id=(x, y, 1 - c), device_id_type=MESH)
    pl.semaphore_wait(barrier, 1)


def _halves(a):
    return a.reshape(a.shape[:-2] + (2, a.shape[-2] // 2, a.shape[-1]))


def _place_shards_call(shards, chip_idx, nch):
    nw = len(shards)

    def body(chip_ref, *refs):
        for w in range(nw):
            refs[nw + w][...] = refs[w][...].astype(WIRE_DTYPE)

    in_specs = [pl.BlockSpec((s.shape[0] // nch, s.shape[1]), lambda i, chip_ref: (i, 0)) for s in shards]
    out_specs = [pl.BlockSpec((None, s.shape[0] // nch, s.shape[1]), lambda i, chip_ref: (chip_ref[0], i, 0))
                 for s in shards]
    return pl.pallas_call(
        body, name="weights_place",
        grid_spec=pltpu.PrefetchScalarGridSpec(num_scalar_prefetch=1, grid=(nch,),
                                               in_specs=in_specs, out_specs=out_specs),
        out_shape=[jax.ShapeDtypeStruct((N_CHIPS,) + s.shape, WIRE_DTYPE) for s in shards],
        compiler_params=_params(("arbitrary",)),
    )(chip_idx, *shards)


def _allgather_call(placed, from_chips, name, meanwhile=None):
    nw = len(placed)
    ncp = 3 * nw
    extra, extra_specs, extra_shape, extra_body = meanwhile if meanwhile else ([], [], None, None)
    ne = len(extra)

    def body(*refs):
        outs = refs[nw + ne:2 * nw + ne]
        send1, recv1, send2, recv2 = refs[-4:]
        x, y, c = _coords()
        chip = 2 * x + y
        others = _other_chips(x, y)
        first, passed = [], []
        if not from_chips:
            _sibling_handshake()
        if from_chips:
            for w in range(nw):
                for k, (ox, oy) in enumerate(others):
                    mine = outs[w].at[chip, c]
                    cp = _remote(mine, mine, send1.at[3 * w + k], recv1.at[3 * w + k], (ox, oy, c))
                    cp.start()
                    first.append(cp)
        if meanwhile:
            extra_body(*refs[nw:nw + ne], refs[2 * nw + ne])
        for w in range(nw):
            for k, (ox, oy) in enumerate(others):
                piece = outs[w].at[2 * ox + oy, c]
                if from_chips:
                    _remote(piece, piece, send1.at[3 * w + k], recv1.at[3 * w + k], (ox, oy, c)).wait_recv()
                cp = _remote(piece, piece, send2.at[3 * w + k], recv2.at[3 * w + k], (x, y, 1 - c))
                cp.start()
                passed.append(cp)
        for w in range(nw):
            for k, (ox, oy) in enumerate(others):
                piece = outs[w].at[2 * ox + oy, 1 - c]
                _remote(piece, piece, send2.at[3 * w + k], recv2.at[3 * w + k], (x, y, 1 - c)).wait_recv()
        for cp in first + passed:
            cp.wait_send()

    return pl.pallas_call(
        body, name=name,
        in_specs=[ANY] * nw + list(extra_specs),
        out_specs=[ANY] * nw + ([pl.BlockSpec(memory_space=pltpu.VMEM)] if meanwhile else []),
        out_shape=[jax.ShapeDtypeStruct(s.shape, s.dtype) for s in placed] + ([extra_shape] if meanwhile else []),
        input_output_aliases={w: w for w in range(nw)},
        scratch_shapes=[pltpu.SemaphoreType.DMA((ncp,))] * 4,
        compiler_params=_params() if from_chips else _params(collective_id=PAIR_FORWARD_ID),
    )(*placed, *extra)


HBM_SPEC = pl.BlockSpec(memory_space=pltpu.HBM)
SEM_SPEC = pl.BlockSpec(memory_space=pltpu.SEMAPHORE)
SPLIT_EFFECT = pltpu.SideEffectType.DATAFLOW_SIDE_EFFECTING


def _in_hbm(a):
    return pltpu.with_memory_space_constraint(a, pltpu.HBM)


def _gather_copies(bufs, send, recv):
    x, y, c = _coords()
    chip = 2 * x + y
    cps = []
    for w, buf in enumerate(bufs):
        for k, (ox, oy) in enumerate(_other_chips(x, y)):
            mine, theirs = buf.at[chip, c], buf.at[2 * ox + oy, c]
            sems = (send.at[3 * w + k], recv.at[3 * w + k], (ox, oy, c))
            cps.append((_remote(mine, mine, *sems), _remote(theirs, theirs, *sems)))
    return cps


def _gather_start_call(bufs, after):
    nw = len(bufs)

    def body(*refs):
        ins, send, recv, token = refs[:nw], refs[nw + 1], refs[nw + 2], refs[2 * nw + 3]
        for out, _ in _gather_copies(ins, send, recv):
            out.start()
        token[...] = jnp.zeros(token.shape, F32)

    res = pl.pallas_call(
        body, name="weights_gather_start",
        in_specs=[HBM_SPEC] * nw + [ANY],
        out_specs=[SEM_SPEC, SEM_SPEC] + [HBM_SPEC] * nw + [pl.BlockSpec(memory_space=pltpu.VMEM)],
        out_shape=[pltpu.SemaphoreType.DMA((3 * nw,)), pltpu.SemaphoreType.DMA((3 * nw,))]
        + [pltpu.HBM(b.shape, b.dtype) for b in bufs] + [jax.ShapeDtypeStruct((8, LANES), F32)],
        input_output_aliases={w: 2 + w for w in range(nw)},
        compiler_params=pltpu.CompilerParams(has_side_effects=SPLIT_EFFECT),
    )(*[_in_hbm(b) for b in bufs], after)
    return res[0], res[1], list(res[2:2 + nw]), res[2 + nw]


def _gather_wait_call(bufs, send, recv, after):
    nw = len(bufs)

    def body(*refs):
        ins, send, recv = refs[:nw], refs[nw], refs[nw + 1]
        for out, back in _gather_copies(ins, send, recv):
            out.wait_send()
            back.wait_recv()

    return pl.pallas_call(
        body, name="weights_gather_wait",
        in_specs=[HBM_SPEC] * nw + [SEM_SPEC, SEM_SPEC, ANY],
        out_specs=[HBM_SPEC] * nw,
        out_shape=[pltpu.HBM(b.shape, b.dtype) for b in bufs],
        input_output_aliases={w: w for w in range(nw)},
        compiler_params=pltpu.CompilerParams(has_side_effects=SPLIT_EFFECT),
    )(*bufs, send, recv, after)


def _scatter_copies(srcs, lands, send, recv, wholes):
    x, y, c = _coords()
    me = 4 * x + 2 * y + c
    cps = []
    for w, (src, land) in enumerate(zip(srcs, lands)):
        for r in range(1, N_DEV):
            px, py, pc = ((1 - x) if r & 4 else x, (1 - y) if r & 2 else y, (1 - c) if r & 1 else c)
            sems = (send.at[(N_DEV - 1) * w + r - 1], recv.at[(N_DEV - 1) * w + r - 1], (px, py, pc))
            piece = src if wholes[w] else src.at[2 * px + py, pc]
            cps.append((_remote(piece, land.at[me], *sems), _remote(piece, land.at[4 * px + 2 * py + pc], *sems)))
    return cps


def _scatter_start_call(srcs, lands, wholes, name):
    nw = len(srcs)
    ncp = (N_DEV - 1) * nw

    def body(*refs):
        ins, lnd, send, recv, token = refs[:nw], refs[nw:2 * nw], refs[2 * nw], refs[2 * nw + 1], refs[4 * nw + 2]
        for out, _ in _scatter_copies(ins, lnd, send, recv, wholes):
            out.start()
        token[...] = jnp.zeros(token.shape, F32)

    res = pl.pallas_call(
        body, name=name,
        in_specs=[HBM_SPEC] * (2 * nw),
        out_specs=[SEM_SPEC, SEM_SPEC] + [HBM_SPEC] * (2 * nw) + [pl.BlockSpec(memory_space=pltpu.VMEM)],
        out_shape=[pltpu.SemaphoreType.DMA((ncp,)), pltpu.SemaphoreType.DMA((ncp,))]
        + [pltpu.HBM(b.shape, b.dtype) for b in list(srcs) + list(lands)] + [jax.ShapeDtypeStruct((8, LANES), F32)],
        input_output_aliases={i: 2 + i for i in range(2 * nw)},
        compiler_params=pltpu.CompilerParams(has_side_effects=SPLIT_EFFECT),
    )(*[_in_hbm(b) for b in list(srcs) + list(lands)])
    return res[0], res[1], list(res[2:2 + nw]), list(res[2 + nw:2 + 2 * nw]), res[2 + 2 * nw]


def _scatter_wait_call(srcs, lands, send, recv, after, wholes, name):
    nw = len(srcs)

    def body(*refs):
        ins, lnd, send, recv = refs[:nw], refs[nw:2 * nw], refs[2 * nw], refs[2 * nw + 1]
        for out, back in _scatter_copies(ins, lnd, send, recv, wholes):
            out.wait_send()
            back.wait_recv()

    res = pl.pallas_call(
        body, name=name,
        in_specs=[HBM_SPEC] * (2 * nw) + [SEM_SPEC, SEM_SPEC, ANY],
        out_specs=[HBM_SPEC] * (2 * nw),
        out_shape=[pltpu.HBM(b.shape, b.dtype) for b in list(srcs) + list(lands)],
        input_output_aliases={i: i for i in range(2 * nw)},
        compiler_params=pltpu.CompilerParams(has_side_effects=SPLIT_EFFECT),
    )(*srcs, *lands, send, recv, after)
    return list(res[nw:])


def _reduce_call(own, lands, idx, nch, name, dep=None):
    nw = len(own)
    deps = [] if dep is None else [dep]

    def body(idx_ref, *refs):
        refs = refs[:2 * nw] + refs[2 * nw + len(deps):]
        for w in range(nw):
            tot = refs[w][...]
            for r in range(1, N_DEV):
                tot = tot + refs[nw + w][idx_ref[1 + r]].astype(F32)
            refs[2 * nw + w][...] = tot

    in_specs, out_specs, out_shape = [], [], []
    for s in own:
        in_specs.append(pl.BlockSpec((None, None, s.shape[2] // nch, s.shape[3]),
                                     lambda i, idx_ref: (idx_ref[0], idx_ref[1], i, 0)))
    for s in own:
        in_specs.append(pl.BlockSpec((N_DEV, s.shape[2] // nch, s.shape[3]), lambda i, idx_ref: (0, i, 0)))
    for s in own:
        out_specs.append(pl.BlockSpec((None, s.shape[2] // nch, s.shape[3]), lambda i, idx_ref: (idx_ref[1], i, 0)))
        out_shape.append(jax.ShapeDtypeStruct((2,) + s.shape[2:], F32))
    return pl.pallas_call(
        body, name=name,
        grid_spec=pltpu.PrefetchScalarGridSpec(num_scalar_prefetch=1, grid=(nch,),
                                               in_specs=in_specs + [ANY] * len(deps), out_specs=out_specs),
        out_shape=out_shape,
        compiler_params=_params(("arbitrary",)),
    )(idx, *own, *lands, *deps)


def _pair_allgather_call(halves, name):
    nw = len(halves)

    def body(*refs):
        outs = refs[nw:2 * nw]
        send, recv = refs[2 * nw:]
        x, y, c = _coords()
        _sibling_handshake()
        cps = []
        for w in range(nw):
            cp = _remote(outs[w].at[c], outs[w].at[c], send.at[w], recv.at[w], (x, y, 1 - c))
            cp.start()
            cps.append(cp)
        for w in range(nw):
            theirs = outs[w].at[1 - c]
            _remote(theirs, theirs, send.at[w], recv.at[w], (x, y, 1 - c)).wait_recv()
        for cp in cps:
            cp.wait_send()

    outs = pl.pallas_call(
        body, name=name,
        in_specs=[ANY] * nw, out_specs=[ANY] * nw,
        out_shape=[jax.ShapeDtypeStruct(h.shape, h.dtype) for h in halves],
        input_output_aliases={w: w for w in range(nw)},
        scratch_shapes=[pltpu.SemaphoreType.DMA((nw,))] * 2,
        compiler_params=pltpu.CompilerParams(collective_id=PAIR_ALLGATHER_ID),
    )(*halves)
    return [o.reshape(2 * h.shape[1], h.shape[2]) for o, h in zip(outs, halves)]


def _adamw(w, g, m, v):
    m = ADAM_B1 * m + (1.0 - ADAM_B1) * g
    v = ADAM_B2 * v + (1.0 - ADAM_B2) * (g * g)
    m_hat = m / (1.0 - ADAM_B1 ** ADAM_STEP)
    v_hat = v / (1.0 - ADAM_B2 ** ADAM_STEP)
    delta = -ADAM_LR * (m_hat / (jnp.sqrt(v_hat) + ADAM_EPS) + ADAM_WD * w)
    return delta, m, v


def _adamw_call(ws, gs, ms, vs, nch, name):
    nw = len(ws)

    def body(*refs):
        for w in range(nw):
            g = refs[nw + w][...]
            delta, m, v = _adamw(refs[w][...], g, refs[2 * nw + w][...], refs[3 * nw + w][...])
            refs[4 * nw + w][...] = g
            refs[5 * nw + w][...] = delta
            refs[6 * nw + w][...] = m
            refs[7 * nw + w][...] = v

    specs = [pl.BlockSpec((a.shape[0] // nch, a.shape[1]), lambda i: (i, 0)) for a in ws]
    res = pl.pallas_call(
        body, name=name,
        grid=(nch,),
        in_specs=specs * 4, out_specs=specs * 4,
        out_shape=[jax.ShapeDtypeStruct(a.shape, F32) for a in ws] * 4,
        compiler_params=_params(("arbitrary",)),
    )(*ws, *gs, *ms, *vs)
    return res[:nw], res[nw:2 * nw], res[2 * nw:3 * nw], res[3 * nw:]


def _small_call(gathered, own, me_idx, w, m, v):
    cuts = dict(mix_norm_g=(0, 0, 1024), mlp_norm_g=(1, 0, 1024), pool_scale=(2, 0, POOL_WIDTH),
                rel_bias=(2, POOL_WIDTH, N_BUCKETS * N_HEADS), q_norm_g=(3, 0, HEAD_DIM), k_norm_g=(3, LANES, HEAD_DIM))
    n_out = len(cuts) + 1

    def fold(row):
        tot = row[:, 0:LANES] + row[:, LANES:2 * LANES] + row[:, 2 * LANES:3 * LANES] + row[:, 3 * LANES:4 * LANES]
        return tot + pltpu.roll(tot, HEAD_DIM, axis=1)

    def body(me_ref, gh_ref, gp_ref, oh_ref, op_ref, wh, wp, mh, mp, vh, vp, loss_ref, *outs):
        me = me_ref[0]

        def total(ga_ref, own_ref):
            term = lambda i: jnp.where(me == i, own_ref[...], ga_ref[i]).astype(F32)
            tot = term(0)
            for i in range(1, N_DEV):
                tot = tot + term(i)
            return tot

        g_head, g_pool = total(gh_ref, oh_ref), total(gp_ref, op_ref)
        unfolded = g_head[4:5, :]
        folded = jnp.concatenate([fold(unfolded[:, :ATTN_WIDTH]), fold(unfolded[:, ATTN_WIDTH:]),
                                  jnp.zeros((1, 1024 - 2 * LANES), F32)], axis=-1)
        row = lax.broadcasted_iota(jnp.int32, g_head.shape, 0)
        g_head = jnp.where(row == 3, folded, g_head)
        loss_ref[...] = g_head[LOSS_ROW:LOSS_ROW + 1, 0:LANES]
        heads = (g_head,) + _adamw(wh[...], g_head, mh[...], vh[...])
        pools = (g_pool,) + _adamw(wp[...], g_pool, mp[...], vp[...])
        for kind in range(4):
            mine = outs[kind * n_out:(kind + 1) * n_out]
            for out, (row, at, n) in zip(mine, cuts.values()):
                out[...] = heads[kind][row:row + 1, at:at + n]
            mine[-1][...] = pools[kind]

    vmem = pl.BlockSpec(memory_space=pltpu.VMEM)
    shapes = [jax.ShapeDtypeStruct((1, n), F32) for _, _, n in cuts.values()] + [jax.ShapeDtypeStruct(w[1].shape, F32)]
    res = pl.pallas_call(
        body, name="adamw_small",
        in_specs=[pl.BlockSpec(memory_space=pltpu.SMEM)] + [vmem] * 10,
        out_shape=[jax.ShapeDtypeStruct((1, LANES), F32)] + shapes * 4,
        compiler_params=_params(),
    )(me_idx, *gathered, *own, *w, *m, *v)

    def unpack(mine):
        p = {n: a.reshape(-1) for n, a in zip(cuts, mine)}
        p["rel_bias"] = p["rel_bias"].reshape(N_BUCKETS, N_HEADS)
        p["pool_w"] = mine[-1].reshape(len(POOL_WINDOWS), LANES, LANES)
        return p

    return [res[0]] + [unpack(res[1 + kind * n_out:1 + (kind + 1) * n_out]) for kind in range(4)]


def _pack_small(p, folded=True, loss=None):
    z = lambda n: jnp.zeros((n,), F32)
    rows = [p["mix_norm_g"], p["mlp_norm_g"],
            jnp.concatenate([p["pool_scale"], p["rel_bias"].reshape(-1), z(1024 - POOL_WIDTH - N_BUCKETS * N_HEADS)])]
    if folded:
        rows += [jnp.concatenate([p["q_norm_g"], z(LANES - HEAD_DIM), p["k_norm_g"], z(1024 - LANES - HEAD_DIM)]), z(1024)]
    else:
        rows += [z(1024), jnp.concatenate([p["q_norm_g"], p["k_norm_g"]])]
    rows += [z(1024) if loss is None else jnp.concatenate([loss.reshape(1), z(1023)])]
    return jnp.stack(rows + [z(1024)] * 2), p["pool_w"].reshape(-1, LANES)


_WEIGHT_ORDER = ("mix_norm_g", "w_in", "pool_w", "pool_scale", "q_norm_g", "k_norm_g", "rel_bias", "w_out",
                 "mlp_norm_g", "w_up", "w_down")
_BIG = ("w_in", "w_out", "w_up", "w_down")


def kernel(x, mix_norm_g, w_in, pool_w, pool_scale, q_norm_g, k_norm_g, rel_bias, w_out, mlp_norm_g, w_up, w_down, loss_target, m_mix_norm_g, m_w_in, m_pool_w, m_pool_scale, m_q_norm_g, m_k_norm_g, m_rel_bias, m_w_out, m_mlp_norm_g, m_w_up, m_w_down, v_mix_norm_g, v_w_in, v_pool_w, v_pool_scale, v_q_norm_g, v_k_norm_g, v_rel_bias, v_w_out, v_mlp_norm_g, v_w_up, v_w_down):
    w = dict(mix_norm_g=mix_norm_g, w_in=w_in, pool_w=pool_w, pool_scale=pool_scale, q_norm_g=q_norm_g,
             k_norm_g=k_norm_g, rel_bias=rel_bias, w_out=w_out, mlp_norm_g=mlp_norm_g, w_up=w_up, w_down=w_down)
    m = dict(mix_norm_g=m_mix_norm_g, w_in=m_w_in, pool_w=m_pool_w, pool_scale=m_pool_scale, q_norm_g=m_q_norm_g,
             k_norm_g=m_k_norm_g, rel_bias=m_rel_bias, w_out=m_w_out, mlp_norm_g=m_mlp_norm_g, w_up=m_w_up, w_down=m_w_down)
    v = dict(mix_norm_g=v_mix_norm_g, w_in=v_w_in, pool_w=v_pool_w, pool_scale=v_pool_scale, q_norm_g=v_q_norm_g,
             k_norm_g=v_k_norm_g, rel_bias=v_rel_bias, w_out=v_w_out, mlp_norm_g=v_mlp_norm_g, w_up=v_w_up, w_down=v_w_down)
    xc, yc, cc = _coords()

    c_idx = jnp.reshape(cc, (1,)).astype(jnp.int32)
    chip_idx = jnp.reshape(2 * xc + yc, (1,)).astype(jnp.int32)
    me = 4 * xc + 2 * yc + cc
    whole = lambda t: t.reshape(t.shape[0], t.shape[1] * t.shape[2], t.shape[3])

    placed = [_halves(p) for p in _place_shards_call([w[n] for n in _BIG], chip_idx, nch=4)]
    win_f, bias = _allgather_call(placed[:1], from_chips=True, name="weights_allgather_in",
                                  meanwhile=_bias_table_work(rel_bias))
    wsend, wrecv, in_flight, started = _gather_start_call(placed[1:], win_f)

    def mlp_weights(after):
        landed = _gather_wait_call(in_flight, wsend, wrecv, after)
        wout_f, wup_f, wdown_f = _allgather_call(landed, from_chips=False, name="weights_pair_forward")
        return whole(wout_f).reshape(-1, wout_f.shape[-1]), whole(wup_f), whole(wdown_f)

    split = []

    def on_mlp_grads(*wire_grads):
        srcs = [_halves(g) for g in wire_grads]
        lands = [lax.empty((N_DEV,) + s.shape[2:], s.dtype) for s in srcs]
        split.extend(_scatter_start_call(srcs, lands, [False] * len(srcs), "grads_scatter_start"))
        return split[4]

    loss_part, dx, big_grads, small_grads = _local_grads(
        x[0], loss_target[0], mix_norm_g, whole(win_f), pool_w, pool_scale, q_norm_g, k_norm_g, bias,
        mlp_norm_g, mlp_weights, on_mlp_grads, first_dep=started)
    g_in, g_out, g_up, g_down = big_grads
    gsend, grecv, srcs_thru, lands_thru, _ = split
    lands_mlp = _scatter_wait_call(srcs_thru, lands_thru, gsend, grecv, g_in[1], [False] * 3, "grads_scatter_wait")

    head_own, pool_own = _pack_small(small_grads, folded=False, loss=loss_part)
    small_own = (head_own, pool_own.astype(WIRE_DTYPE))
    last_srcs = [_halves(g_in[1]), *small_own]
    last_lands = [lax.empty((N_DEV,) + last_srcs[0].shape[2:], WIRE_DTYPE)]
    last_lands += [lax.empty((N_DEV,) + a.shape, a.dtype) for a in small_own]
    lsend, lrecv, last_srcs, last_lands, last_started = _scatter_start_call(
        last_srcs, last_lands, [False, True, True], "grads_scatter_start_last")
    idx = jnp.concatenate([chip_idx, c_idx] + [jnp.reshape(jnp.bitwise_xor(me, r), (1,)) for r in range(1, N_DEV)])
    idx = idx.astype(jnp.int32)
    mlp = _BIG[1:]

    def update(names, own32, lands, tag, dep=None):
        halves = _reduce_call([_halves(g) for g in own32], lands, idx, 4, "grads_reduce_" + tag, dep)
        reduced = _pair_allgather_call(list(halves), "grads_pair_allgather_" + tag)
        return _adamw_call([w[n] for n in names], reduced, [m[n] for n in names], [v[n] for n in names], 8, "adamw_" + tag)

    out_mlp = update(mlp, [g_out[0], g_up[0], g_down[0]], lands_mlp, "mlp", last_started)
    land_in, *small_all = _scatter_wait_call(last_srcs, last_lands, lsend, lrecv, out_mlp[3][-1], [False, True, True],
                                             "grads_scatter_wait_last")
    out_in = update(_BIG[:1], [g_in[0]], [land_in], "in")
    loss_row, grads, deltas, new_m, new_v = _small_call(
        small_all, small_own, jnp.reshape(me, (1,)).astype(jnp.int32), _pack_small(w), _pack_small(m), _pack_small(v))

    for k, res in enumerate((grads, deltas, new_m, new_v)):
        res[_BIG[0]] = out_in[k][0]
        for i, n in enumerate(mlp):
            res[n] = out_mlp[k][i]
    loss = loss_row[0, 0]
    return (loss, dx[None], *[grads[n] for n in _WEIGHT_ORDER], *[deltas[n] for n in _WEIGHT_ORDER],
            *[new_m[n] for n in _WEIGHT_ORDER], *[new_v[n] for n in _WEIGHT_ORDER])
```
